```python
import jax, jax.numpy as jnp
from jax import lax
import numpy as np

D_MODEL = 1024
BATCH = 8
SEQ = 4096
DEPTH = 1

MIX_WIDTH = D_MODEL
SB_WIDTH = MIX_WIDTH // 2
SB_HEAD_DIM = 64
SB_HEADS = SB_WIDTH // SB_HEAD_DIM
HG_WIDTH = MIX_WIDTH - SB_WIDTH
HG_HEAD_DIM = 128
HG_HEADS = HG_WIDTH // HG_HEAD_DIM
D_FF = 2816
Q_BLOCK = 128
HG_CHUNK = 64
EPS = 1e-6
IN_WIDTHS = (SB_WIDTH, SB_WIDTH, SB_WIDTH, HG_WIDTH, HG_WIDTH, HG_WIDTH, HG_WIDTH)
IN_COLS = sum(IN_WIDTHS)
IN_SPLITS = tuple(int(s) for s in np.cumsum(IN_WIDTHS)[:-1])

kernel_name = "hybrid_stickbreak_hgrn2_macaron"


def rmsnorm(x, gain):
    xf = x.astype(jnp.float32)
    y = xf * lax.rsqrt(jnp.mean(xf * xf, axis=-1, keepdims=True) + EPS)
    return (y * gain.astype(jnp.float32)).astype(x.dtype)


def head_rmsnorm(o, gain, n_heads, head_dim):
    B, T, _ = o.shape
    of = o.astype(jnp.float32).reshape(B, T, n_heads, head_dim)
    of = of * lax.rsqrt(jnp.mean(of * of, axis=-1, keepdims=True) + EPS)
    return of.reshape(B, T, n_heads * head_dim) * gain.astype(jnp.float32)


def swiglu(h, w_gate, w_up, w_down):
    return (jax.nn.silu(h @ w_gate) * (h @ w_up)) @ w_down


def split_heads(a, n_heads, head_dim):
    B, T, _ = a.shape
    return a.reshape(B, T, n_heads, head_dim).transpose(0, 2, 1, 3)


def merge_heads(a):
    B, H, T, Dh = a.shape
    return a.transpose(0, 2, 1, 3).reshape(B, T, H * Dh)


def stick_breaking_attention(q, k, v):
    B, H, T, Dh = q.shape
    n_blocks = T // Q_BLOCK
    scale = Dh ** -0.5
    q_blocks = q.reshape(B, H, n_blocks, Q_BLOCK, Dh).transpose(2, 0, 1, 3, 4)
    key_pos = jnp.arange(T)

    def block(args):
        q_blk, blk_idx = args
        z = jnp.einsum('bhqd,bhkd->bhqk', q_blk, k) * scale
        q_pos = blk_idx * Q_BLOCK + jnp.arange(Q_BLOCK)
        causal = key_pos[None, :] < q_pos[:, None]
        log_beta = jax.nn.log_sigmoid(z)
        log_rest = jnp.where(causal, jax.nn.log_sigmoid(-z), 0.0)
        tail = lax.cumsum(log_rest, axis=3, reverse=True) - log_rest
        w = jnp.where(causal, jnp.exp(log_beta + tail), 0.0)
        return jnp.einsum('bhqk,bhkd->bhqd', w, v)

    out = lax.map(block, (q_blocks, jnp.arange(n_blocks)))
    return out.transpose(1, 2, 0, 3, 4).reshape(B, H, T, Dh)


def hgrn2_chunkwise(q, k, v, log_f):
    B, H, T, Dk = q.shape
    Dv = v.shape[-1]
    n = T // HG_CHUNK

    def to_chunks(a):
        return a.reshape(B, H, n, HG_CHUNK, a.shape[-1]).transpose(2, 0, 1, 3, 4)

    qc, kc, vc = to_chunks(q), to_chunks(k), to_chunks(v)
    bc = lax.cumsum(to_chunks(log_f), axis=3)
    idx = jnp.arange(HG_CHUNK)
    incl = (idx[:, None] >= idx[None, :])[:, :, None]

    def step(S, xs):
        q_c, k_c, v_c, b_c = xs
        diff = b_c[:, :, :, None, :] - b_c[:, :, None, :, :]
        decay = jnp.where(incl, jnp.exp(jnp.minimum(diff, 0.0)), 0.0)
        scores = jnp.einsum('bhtd,bhsd,bhtsd->bhts', q_c, k_c, decay)
        o_intra = jnp.einsum('bhts,bhsv->bhtv', scores, v_c)
        o_inter = jnp.einsum('bhtd,bhdv->bhtv', q_c * jnp.exp(b_c), S)
        b_last = b_c[:, :, -1:, :]
        S_new = jnp.exp(b_last[:, :, 0, :])[..., None] * S + jnp.einsum(
            'bhsd,bhsv->bhdv', k_c * jnp.exp(b_last - b_c), v_c)
        return S_new, o_intra + o_inter

    S0 = jnp.zeros((B, H, Dk, Dv), jnp.float32)
    _, o = lax.scan(step, S0, (qc, kc, vc, bc))
    return o.transpose(1, 2, 0, 3, 4).reshape(B, H, T, Dv)


def _fwd_setup_inputs(seed: int = 0) -> dict:
    key = jax.random.key(seed)
    ks = jax.random.split(key, 20)
    f32 = jnp.float32

    def normal(k, shape, scale):
        return jax.random.normal(k, shape, f32) * scale

    def gain(k, shape):
        return 1.0 + 0.02 * jax.random.normal(k, shape, f32)

    return {
        "x": jax.random.normal(ks[0], (BATCH, SEQ, D_MODEL), f32),
        "ffn1_norm": gain(ks[1], (DEPTH, D_MODEL)),
        "ffn1_w_gate": normal(ks[2], (DEPTH, D_MODEL, D_FF), D_MODEL ** -0.5),
        "ffn1_w_up": normal(ks[3], (DEPTH, D_MODEL, D_FF), D_MODEL ** -0.5),
        "ffn1_w_down": normal(ks[4], (DEPTH, D_FF, D_MODEL), D_FF ** -0.5),
        "mix_norm": gain(ks[5], (DEPTH, D_MODEL)),
        "w_in": normal(ks[6], (DEPTH, D_MODEL, IN_COLS), D_MODEL ** -0.5),
        "sb_out_norm": gain(ks[7], (DEPTH, SB_WIDTH)),
        "hg_lower_bound_logits": normal(ks[8], (DEPTH + 1, HG_WIDTH), 0.1),
        "hg_out_norm": gain(ks[9], (DEPTH, HG_WIDTH)),
        "w_out": normal(ks[10], (DEPTH, MIX_WIDTH, D_MODEL), MIX_WIDTH ** -0.5),
        "ffn2_norm": gain(ks[11], (DEPTH, D_MODEL)),
        "ffn2_w_gate": normal(ks[12], (DEPTH, D_MODEL, D_FF), D_MODEL ** -0.5),
        "ffn2_w_up": normal(ks[13], (DEPTH, D_MODEL, D_FF), D_MODEL ** -0.5),
        "ffn2_w_down": normal(ks[14], (DEPTH, D_FF, D_MODEL), D_FF ** -0.5),
        "final_norm": gain(ks[15], (D_MODEL,)),
    }


def _fwd_reference(x, ffn1_norm, ffn1_w_gate, ffn1_w_up, ffn1_w_down, mix_norm, w_in, sb_out_norm,
              hg_lower_bound_logits, hg_out_norm, w_out, ffn2_norm, ffn2_w_gate, ffn2_w_up,
              ffn2_w_down, final_norm):
    f32 = jnp.float32
    lower_bounds = lax.cumsum(jax.nn.softmax(hg_lower_bound_logits.astype(f32), axis=0), axis=0)

    for l in range(DEPTH):
        h = rmsnorm(x, ffn1_norm[l])
        x = x + 0.5 * swiglu(h, ffn1_w_gate[l], ffn1_w_up[l], ffn1_w_down[l])

        h = rmsnorm(x, mix_norm[l])
        proj = h @ w_in[l]
        q_sb, k_sb, v_sb, q_hg, f_hg, i_hg, g_hg = jnp.split(proj, IN_SPLITS, axis=-1)

        o_sb = stick_breaking_attention(split_heads(q_sb.astype(f32), SB_HEADS, SB_HEAD_DIM),
                                        split_heads(k_sb.astype(f32), SB_HEADS, SB_HEAD_DIM),
                                        split_heads(v_sb.astype(f32), SB_HEADS, SB_HEAD_DIM))
        o_sb = head_rmsnorm(merge_heads(o_sb), sb_out_norm[l], SB_HEADS, SB_HEAD_DIM)

        lb = lower_bounds[l]
        forget = lb + (1.0 - lb) * jax.nn.sigmoid(f_hg.astype(f32))
        q_h = jax.nn.silu(q_hg.astype(f32))
        o_hg = hgrn2_chunkwise(split_heads(q_h, HG_HEADS, HG_HEAD_DIM),
                               split_heads(1.0 - forget, HG_HEADS, HG_HEAD_DIM),
                               split_heads(i_hg.astype(f32), HG_HEADS, HG_HEAD_DIM),
                               split_heads(jnp.log(forget), HG_HEADS, HG_HEAD_DIM))
        o_hg = head_rmsnorm(merge_heads(o_hg), hg_out_norm[l], HG_HEADS, HG_HEAD_DIM) * jax.nn.silu(g_hg.astype(f32))

        mixed = jnp.concatenate([o_sb, o_hg], axis=-1).astype(x.dtype)
        x = x + mixed @ w_out[l]

        h = rmsnorm(x, ffn2_norm[l])
        x = x + 0.5 * swiglu(h, ffn2_w_gate[l], ffn2_w_up[l], ffn2_w_down[l])

    return rmsnorm(x, final_norm)


import jax as _jax
import jax.numpy as _jnp

TWIN_FORMAT = 'train_step'
FWD_PARAMS = ['x', 'ffn1_norm', 'ffn1_w_gate', 'ffn1_w_up', 'ffn1_w_down', 'mix_norm', 'w_in', 'sb_out_norm', 'hg_lower_bound_logits', 'hg_out_norm', 'w_out', 'ffn2_norm', 'ffn2_w_gate', 'ffn2_w_up', 'ffn2_w_down', 'final_norm']
TWIN_WEIGHTS = ['ffn1_norm', 'ffn1_w_gate', 'ffn1_w_up', 'ffn1_w_down', 'mix_norm', 'w_in', 'sb_out_norm', 'hg_lower_bound_logits', 'hg_out_norm', 'w_out', 'ffn2_norm', 'ffn2_w_gate', 'ffn2_w_up', 'ffn2_w_down', 'final_norm']
TWIN_DIFF_INPUT = 'x'
TWIN_INPUTS = ['x', 'ffn1_norm', 'ffn1_w_gate', 'ffn1_w_up', 'ffn1_w_down', 'mix_norm', 'w_in', 'sb_out_norm', 'hg_lower_bound_logits', 'hg_out_norm', 'w_out', 'ffn2_norm', 'ffn2_w_gate', 'ffn2_w_up', 'ffn2_w_down', 'final_norm', 'loss_target', 'm_ffn1_norm', 'm_ffn1_w_gate', 'm_ffn1_w_up', 'm_ffn1_w_down', 'm_mix_norm', 'm_w_in', 'm_sb_out_norm', 'm_hg_lower_bound_logits', 'm_hg_out_norm', 'm_w_out', 'm_ffn2_norm', 'm_ffn2_w_gate', 'm_ffn2_w_up', 'm_ffn2_w_down', 'm_final_norm', 'v_ffn1_norm', 'v_ffn1_w_gate', 'v_ffn1_w_up', 'v_ffn1_w_down', 'v_mix_norm', 'v_w_in', 'v_sb_out_norm', 'v_hg_lower_bound_logits', 'v_hg_out_norm', 'v_w_out', 'v_ffn2_norm', 'v_ffn2_w_gate', 'v_ffn2_w_up', 'v_ffn2_w_down', 'v_final_norm']
TWIN_OUTPUTS = ['loss', 'grad_x', 'grad_ffn1_norm', 'grad_ffn1_w_gate', 'grad_ffn1_w_up', 'grad_ffn1_w_down', 'grad_mix_norm', 'grad_w_in', 'grad_sb_out_norm', 'grad_hg_lower_bound_logits', 'grad_hg_out_norm', 'grad_w_out', 'grad_ffn2_norm', 'grad_ffn2_w_gate', 'grad_ffn2_w_up', 'grad_ffn2_w_down', 'grad_final_norm', 'delta_ffn1_norm', 'delta_ffn1_w_gate', 'delta_ffn1_w_up', 'delta_ffn1_w_down', 'delta_mix_norm', 'delta_w_in', 'delta_sb_out_norm', 'delta_hg_lower_bound_logits', 'delta_hg_out_norm', 'delta_w_out', 'delta_ffn2_norm', 'delta_ffn2_w_gate', 'delta_ffn2_w_up', 'delta_ffn2_w_down', 'delta_final_norm', 'new_m_ffn1_norm', 'new_m_ffn1_w_gate', 'new_m_ffn1_w_up', 'new_m_ffn1_w_down', 'new_m_mix_norm', 'new_m_w_in', 'new_m_sb_out_norm', 'new_m_hg_lower_bound_logits', 'new_m_hg_out_norm', 'new_m_w_out', 'new_m_ffn2_norm', 'new_m_ffn2_w_gate', 'new_m_ffn2_w_up', 'new_m_ffn2_w_down', 'new_m_final_norm', 'new_v_ffn1_norm', 'new_v_ffn1_w_gate', 'new_v_ffn1_w_up', 'new_v_ffn1_w_down', 'new_v_mix_norm', 'new_v_w_in', 'new_v_sb_out_norm', 'new_v_hg_lower_bound_logits', 'new_v_hg_out_norm', 'new_v_w_out', 'new_v_ffn2_norm', 'new_v_ffn2_w_gate', 'new_v_ffn2_w_up', 'new_v_ffn2_w_down', 'new_v_final_norm']
TWIN_LEAF_KINDS = {'loss': 'loss', 'grad_x': 'grad_x', 'grad_ffn1_norm': 'grad_w', 'grad_ffn1_w_gate': 'grad_w', 'grad_ffn1_w_up': 'grad_w', 'grad_ffn1_w_down': 'grad_w', 'grad_mix_norm': 'grad_w', 'grad_w_in': 'grad_w', 'grad_sb_out_norm': 'grad_w', 'grad_hg_lower_bound_logits': 'grad_w', 'grad_hg_out_norm': 'grad_w', 'grad_w_out': 'grad_w', 'grad_ffn2_norm': 'grad_w', 'grad_ffn2_w_gate': 'grad_w', 'grad_ffn2_w_up': 'grad_w', 'grad_ffn2_w_down': 'grad_w', 'grad_final_norm': 'grad_w', 'delta_ffn1_norm': 'delta_w', 'delta_ffn1_w_gate': 'delta_w', 'delta_ffn1_w_up': 'delta_w', 'delta_ffn1_w_down': 'delta_w', 'delta_mix_norm': 'delta_w', 'delta_w_in': 'delta_w', 'delta_sb_out_norm': 'delta_w', 'delta_hg_lower_bound_logits': 'delta_w', 'delta_hg_out_norm': 'delta_w', 'delta_w_out': 'delta_w', 'delta_ffn2_norm': 'delta_w', 'delta_ffn2_w_gate': 'delta_w', 'delta_ffn2_w_up': 'delta_w', 'delta_ffn2_w_down': 'delta_w', 'delta_final_norm': 'delta_w', 'new_m_ffn1_norm': 'new_m', 'new_m_ffn1_w_gate': 'new_m', 'new_m_ffn1_w_up': 'new_m', 'new_m_ffn1_w_down': 'new_m', 'new_m_mix_norm': 'new_m', 'new_m_w_in': 'new_m', 'new_m_sb_out_norm': 'new_m', 'new_m_hg_lower_bound_logits': 'new_m', 'new_m_hg_out_norm': 'new_m', 'new_m_w_out': 'new_m', 'new_m_ffn2_norm': 'new_m', 'new_m_ffn2_w_gate': 'new_m', 'new_m_ffn2_w_up': 'new_m', 'new_m_ffn2_w_down': 'new_m', 'new_m_final_norm': 'new_m', 'new_v_ffn1_norm': 'new_v', 'new_v_ffn1_w_gate': 'new_v', 'new_v_ffn1_w_up': 'new_v', 'new_v_ffn1_w_down': 'new_v', 'new_v_mix_norm': 'new_v', 'new_v_w_in': 'new_v', 'new_v_sb_out_norm': 'new_v', 'new_v_hg_lower_bound_logits': 'new_v', 'new_v_hg_out_norm': 'new_v', 'new_v_w_out': 'new_v', 'new_v_ffn2_norm': 'new_v', 'new_v_ffn2_w_gate': 'new_v', 'new_v_ffn2_w_up': 'new_v', 'new_v_ffn2_w_down': 'new_v', 'new_v_final_norm': 'new_v'}


def _forward(args):
    return _fwd_reference(*[args[k] for k in FWD_PARAMS])


def _output_shape():
    def fwd():
        inp = _fwd_setup_inputs(0)
        return _fwd_reference(*[inp[k] for k in FWD_PARAMS])
    out = _jax.eval_shape(fwd)
    return out.shape, out.dtype

N_MICROBATCH = 1
ADAM_LR = 0.001
ADAM_B1 = 0.9
ADAM_B2 = 0.999
ADAM_EPS = 1e-08
ADAM_WD = 0.01
ADAM_STEP = 10
PER_EXAMPLE_BATCH_AXIS = {'x': 0, 'loss_target': 0}
SHARED_INPUTS = []
_WEIGHT_DTYPES = {'ffn1_norm': _jnp.float32, 'ffn1_w_gate': _jnp.float32, 'ffn1_w_up': _jnp.float32, 'ffn1_w_down': _jnp.float32, 'mix_norm': _jnp.float32, 'w_in': _jnp.float32, 'sb_out_norm': _jnp.float32, 'hg_lower_bound_logits': _jnp.float32, 'hg_out_norm': _jnp.float32, 'w_out': _jnp.float32, 'ffn2_norm': _jnp.float32, 'ffn2_w_gate': _jnp.float32, 'ffn2_w_up': _jnp.float32, 'ffn2_w_down': _jnp.float32, 'final_norm': _jnp.float32}
MOMENT_SCALE = {'ffn1_norm': 8.346186e-02, 'ffn1_w_gate': 3.600683e-02, 'ffn1_w_up': 3.485833e-02, 'ffn1_w_down': 5.776808e-02, 'mix_norm': 1.430988e-01, 'w_in': 7.670475e-02, 'sb_out_norm': 1.515088e-01, 'hg_lower_bound_logits': 7.487133e-03, 'hg_out_norm': 7.923007e-02, 'w_out': 1.134144e-01, 'ffn2_norm': 6.021271e-02, 'ffn2_w_gate': 2.415345e-02, 'ffn2_w_up': 2.343251e-02, 'ffn2_w_down': 3.884961e-02, 'final_norm': 3.199796e+01}


def _to_microbatches(a, axis):
    t = _jnp.moveaxis(a, axis, 0)
    t = t.reshape((N_MICROBATCH, t.shape[0] // N_MICROBATCH) + t.shape[1:])
    return _jnp.moveaxis(t, 1, axis + 1)


def setup_inputs(seed: int = 0) -> dict:
    inp = _fwd_setup_inputs(seed)
    key = _jax.random.fold_in(_jax.random.key(seed), 7919)
    shape, _ = _output_shape()
    out = dict(inp)
    out["loss_target"] = _jax.random.normal(_jax.random.fold_in(key, 0), shape, _jnp.float32)
    for i, name in enumerate(TWIN_WEIGHTS):
        w = inp[name].astype(_jnp.float32)
        if MOMENT_SCALE is None:
            s = _jnp.sqrt(_jnp.mean(_jnp.square(w)) + 1e-30)
        else:
            s = MOMENT_SCALE[name]
        km, kv = _jax.random.split(_jax.random.fold_in(key, i + 1))
        out[name] = w
        out["m_" + name] = s * _jax.random.normal(km, w.shape, _jnp.float32)
        out["v_" + name] = (s * s) * _jax.random.uniform(kv, w.shape, _jnp.float32, 0.5, 1.5)
    if N_MICROBATCH > 1:
        for name, axis in PER_EXAMPLE_BATCH_AXIS.items():
            out[name] = _to_microbatches(out[name], axis)
    return {'x': out['x'], 'ffn1_norm': out['ffn1_norm'], 'ffn1_w_gate': out['ffn1_w_gate'], 'ffn1_w_up': out['ffn1_w_up'], 'ffn1_w_down': out['ffn1_w_down'], 'mix_norm': out['mix_norm'], 'w_in': out['w_in'], 'sb_out_norm': out['sb_out_norm'], 'hg_lower_bound_logits': out['hg_lower_bound_logits'], 'hg_out_norm': out['hg_out_norm'], 'w_out': out['w_out'], 'ffn2_norm': out['ffn2_norm'], 'ffn2_w_gate': out['ffn2_w_gate'], 'ffn2_w_up': out['ffn2_w_up'], 'ffn2_w_down': out['ffn2_w_down'], 'final_norm': out['final_norm'], 'loss_target': out['loss_target'], 'm_ffn1_norm': out['m_ffn1_norm'], 'm_ffn1_w_gate': out['m_ffn1_w_gate'], 'm_ffn1_w_up': out['m_ffn1_w_up'], 'm_ffn1_w_down': out['m_ffn1_w_down'], 'm_mix_norm': out['m_mix_norm'], 'm_w_in': out['m_w_in'], 'm_sb_out_norm': out['m_sb_out_norm'], 'm_hg_lower_bound_logits': out['m_hg_lower_bound_logits'], 'm_hg_out_norm': out['m_hg_out_norm'], 'm_w_out': out['m_w_out'], 'm_ffn2_norm': out['m_ffn2_norm'], 'm_ffn2_w_gate': out['m_ffn2_w_gate'], 'm_ffn2_w_up': out['m_ffn2_w_up'], 'm_ffn2_w_down': out['m_ffn2_w_down'], 'm_final_norm': out['m_final_norm'], 'v_ffn1_norm': out['v_ffn1_norm'], 'v_ffn1_w_gate': out['v_ffn1_w_gate'], 'v_ffn1_w_up': out['v_ffn1_w_up'], 'v_ffn1_w_down': out['v_ffn1_w_down'], 'v_mix_norm': out['v_mix_norm'], 'v_w_in': out['v_w_in'], 'v_sb_out_norm': out['v_sb_out_norm'], 'v_hg_lower_bound_logits': out['v_hg_lower_bound_logits'], 'v_hg_out_norm': out['v_hg_out_norm'], 'v_w_out': out['v_w_out'], 'v_ffn2_norm': out['v_ffn2_norm'], 'v_ffn2_w_gate': out['v_ffn2_w_gate'], 'v_ffn2_w_up': out['v_ffn2_w_up'], 'v_ffn2_w_down': out['v_ffn2_w_down'], 'v_final_norm': out['v_final_norm']}


def _loss(weights, diff, rest, loss_target):
    with _jax.named_scope("forward"):
        args = {**rest, TWIN_DIFF_INPUT: diff, **{k: w.astype(_WEIGHT_DTYPES[k]) for k, w in weights.items()}}
        y = _forward(args)
    with _jax.named_scope("loss_head"):
        err = _jnp.square(y.astype(_jnp.float32) - loss_target)
        return 0.5 * _jnp.sum(_jnp.mean(err, axis=-1)) if err.ndim else 0.5 * err


def _adamw(w, g, m, v):
    m = ADAM_B1 * m + (1.0 - ADAM_B1) * g
    v = ADAM_B2 * v + (1.0 - ADAM_B2) * _jnp.square(g)
    m_hat = m / (1.0 - ADAM_B1 ** ADAM_STEP)
    v_hat = v / (1.0 - ADAM_B2 ** ADAM_STEP)
    delta = -ADAM_LR * (m_hat / (_jnp.sqrt(v_hat) + ADAM_EPS) + ADAM_WD * w)
    return delta, m, v


def reference(x, ffn1_norm, ffn1_w_gate, ffn1_w_up, ffn1_w_down, mix_norm, w_in, sb_out_norm, hg_lower_bound_logits, hg_out_norm, w_out, ffn2_norm, ffn2_w_gate, ffn2_w_up, ffn2_w_down, final_norm, loss_target, m_ffn1_norm, m_ffn1_w_gate, m_ffn1_w_up, m_ffn1_w_down, m_mix_norm, m_w_in, m_sb_out_norm, m_hg_lower_bound_logits, m_hg_out_norm, m_w_out, m_ffn2_norm, m_ffn2_w_gate, m_ffn2_w_up, m_ffn2_w_down, m_final_norm, v_ffn1_norm, v_ffn1_w_gate, v_ffn1_w_up, v_ffn1_w_down, v_mix_norm, v_w_in, v_sb_out_norm, v_hg_lower_bound_logits, v_hg_out_norm, v_w_out, v_ffn2_norm, v_ffn2_w_gate, v_ffn2_w_up, v_ffn2_w_down, v_final_norm):
    given = dict(x=x, ffn1_norm=ffn1_norm, ffn1_w_gate=ffn1_w_gate, ffn1_w_up=ffn1_w_up, ffn1_w_down=ffn1_w_down, mix_norm=mix_norm, w_in=w_in, sb_out_norm=sb_out_norm, hg_lower_bound_logits=hg_lower_bound_logits, hg_out_norm=hg_out_norm, w_out=w_out, ffn2_norm=ffn2_norm, ffn2_w_gate=ffn2_w_gate, ffn2_w_up=ffn2_w_up, ffn2_w_down=ffn2_w_down, final_norm=final_norm, loss_target=loss_target, m_ffn1_norm=m_ffn1_norm, m_ffn1_w_gate=m_ffn1_w_gate, m_ffn1_w_up=m_ffn1_w_up, m_ffn1_w_down=m_ffn1_w_down, m_mix_norm=m_mix_norm, m_w_in=m_w_in, m_sb_out_norm=m_sb_out_norm, m_hg_lower_bound_logits=m_hg_lower_bound_logits, m_hg_out_norm=m_hg_out_norm, m_w_out=m_w_out, m_ffn2_norm=m_ffn2_norm, m_ffn2_w_gate=m_ffn2_w_gate, m_ffn2_w_up=m_ffn2_w_up, m_ffn2_w_down=m_ffn2_w_down, m_final_norm=m_final_norm, v_ffn1_norm=v_ffn1_norm, v_ffn1_w_gate=v_ffn1_w_gate, v_ffn1_w_up=v_ffn1_w_up, v_ffn1_w_down=v_ffn1_w_down, v_mix_norm=v_mix_norm, v_w_in=v_w_in, v_sb_out_norm=v_sb_out_norm, v_hg_lower_bound_logits=v_hg_lower_bound_logits, v_hg_out_norm=v_hg_out_norm, v_w_out=v_w_out, v_ffn2_norm=v_ffn2_norm, v_ffn2_w_gate=v_ffn2_w_gate, v_ffn2_w_up=v_ffn2_w_up, v_ffn2_w_down=v_ffn2_w_down, v_final_norm=v_final_norm)
    weights = {n: given[n] for n in TWIN_WEIGHTS}
    shared = {n: given[n] for n in SHARED_INPUTS}
    per_example = {n: given[n] for n in ['x']}
    grad_fn = _jax.value_and_grad(_loss, argnums=(0, 1))

    def one_microbatch(ex, loss_target):
        ex = dict(ex)
        diff = ex.pop(TWIN_DIFF_INPUT)
        return grad_fn(weights, diff, {**shared, **ex}, loss_target)

    if N_MICROBATCH == 1:
        loss, (grad_w, grad_x) = one_microbatch(per_example, given["loss_target"])
    else:
        def body(carry, xs):
            loss_sum, grad_sum = carry
            l_k, (gw_k, gx_k) = one_microbatch(xs[0], xs[1])
            with _jax.named_scope("update"):
                return (loss_sum + l_k, _jax.tree.map(_jnp.add, grad_sum, gw_k)), gx_k

        init = (_jnp.zeros((), _jnp.float32), _jax.tree.map(_jnp.zeros_like, weights))
        (loss, grad_w), grad_x = _jax.lax.scan(body, init, (per_example, given["loss_target"]))
    with _jax.named_scope("update"):
        delta_w, new_m, new_v = {}, {}, {}
        for n in TWIN_WEIGHTS:
            delta_w[n], new_m[n], new_v[n] = _adamw(weights[n], grad_w[n], given["m_" + n], given["v_" + n])
    return (loss, grad_x, *[grad_w[n] for n in TWIN_WEIGHTS], *[delta_w[n] for n in TWIN_WEIGHTS],
            *[new_m[n] for n in TWIN_WEIGHTS], *[new_v[n] for n in TWIN_WEIGHTS])
```

```python
import jax
import jax.numpy as jnp
from jax import lax
from jax.experimental import pallas as pl
from jax.experimental.pallas import tpu as pltpu

F32, BF16 = jnp.float32, jnp.bfloat16
D_MODEL = 1024
D_FF = 2816
SB_WIDTH = 512
HG_WIDTH = 512
SB_HEAD_DIM = 64
HG_HEAD_DIM = 128
IN_COLS = 3584
EPS = 1e-6
N_DEV = 8
FF_SHARD = D_FF // N_DEV
IN_SHARD = IN_COLS // N_DEV
OUT_SHARD = D_MODEL // N_DEV
LANES = 128
HG_CHUNK = 16
VMEM_LIMIT_BYTES = 48 * 1024 * 1024
ADAM_LR, ADAM_B1, ADAM_B2, ADAM_EPS, ADAM_WD, ADAM_STEP = 0.001, 0.9, 0.999, 1e-08, 0.01, 10
MESH = pl.DeviceIdType.MESH


def _params(*semantics):
    return pltpu.CompilerParams(dimension_semantics=semantics, vmem_limit_bytes=VMEM_LIMIT_BYTES)


def _dot(a, b):
    return jnp.dot(a, b, preferred_element_type=F32)


def _dot_nt(a, b):
    return lax.dot_general(a, b, (((1,), (1,)), ((), ())), preferred_element_type=F32)


def _dot_tn(a, b):
    return lax.dot_general(a, b, (((0,), (0,)), ((), ())), preferred_element_type=F32)


def _split3(x):
    hi = x.astype(BF16)
    r1 = x - hi.astype(F32)
    mid = r1.astype(BF16)
    lo = (r1 - mid.astype(F32)).astype(BF16)
    return hi, mid, lo


def _rms(xv):
    rstd = lax.rsqrt(jnp.mean(xv * xv, axis=-1, keepdims=True) + EPS)
    return xv * rstd, rstd


def _sigmoid(x):
    return 1.0 / (1.0 + jnp.exp(-x))


def _mm(a, b, *, name, tm, tn, nt=False, scale=None, add=None, out_dtype=F32):
    m, k = a.shape
    n = b.shape[0] if nt else b.shape[1]
    assert m % tm == 0 and n % tn == 0, (name, a.shape, b.shape, tm, tn)

    def body(*refs):
        a_ref, b_ref = refs[0], refs[1]
        o_ref = refs[-1]
        av = a_ref[...].astype(BF16)
        bv = b_ref[...].astype(BF16)
        r = _dot_nt(av, bv) if nt else _dot(av, bv)
        if scale is not None:
            r = r * scale
        if add is not None:
            r = r + refs[2][...]
        o_ref[...] = r.astype(out_dtype)

    in_specs = [
        pl.BlockSpec((tm, k), lambda i, j: (i, 0)),
        pl.BlockSpec((tn, k), lambda i, j: (j, 0)) if nt else pl.BlockSpec((k, tn), lambda i, j: (0, j)),
    ]
    operands = [a, b]
    if add is not None:
        in_specs.append(pl.BlockSpec((tm, tn), lambda i, j: (i, j)))
        operands.append(add)
    return pl.pallas_call(
        body,
        name=name,
        grid=(m // tm, n // tn),
        in_specs=in_specs,
        out_specs=pl.BlockSpec((tm, tn), lambda i, j: (i, j)),
        out_shape=jax.ShapeDtypeStruct((m, n), out_dtype),
        compiler_params=_params("parallel", "parallel"),
    )(*operands)


def _ffn_fwd(x, gain, wg, wu, wd, *, name, tm=512, tf=256):
    t = x.shape[0]
    nj = D_FF // tf

    def body(x_ref, g_ref, wg_ref, wu_ref, wd_ref, xo_ref, a_ref, b_ref, ht_ref, st_ref, h_scr, acc):
        j = pl.program_id(1)

        @pl.when(j == 0)
        def _():
            xhat, _ = _rms(x_ref[...])
            h = xhat * g_ref[...]
            h_scr[...] = h.astype(BF16)
            ht_ref[...] = h.T.astype(BF16)
            acc[...] = jnp.zeros_like(acc)

        h = h_scr[...]
        a = _dot(h, wg_ref[...])
        b = _dot(h, wu_ref[...])
        a_ref[...] = a.astype(BF16)
        b_ref[...] = b.astype(BF16)
        s = a * _sigmoid(a) * b
        st_ref[...] = s.T.astype(BF16)
        acc[...] += _dot(s.astype(BF16), wd_ref[...])

        @pl.when(j == nj - 1)
        def _():
            xo_ref[...] = x_ref[...] + 0.5 * acc[...]

    return pl.pallas_call(
        body,
        name=name,
        grid=(t // tm, nj),
        in_specs=[
            pl.BlockSpec((tm, D_MODEL), lambda i, j: (i, 0)),
            pl.BlockSpec((1, D_MODEL), lambda i, j: (0, 0)),
            pl.BlockSpec((D_MODEL, tf), lambda i, j: (0, j)),
            pl.BlockSpec((D_MODEL, tf), lambda i, j: (0, j)),
            pl.BlockSpec((tf, D_MODEL), lambda i, j: (j, 0)),
        ],
        out_specs=[
            pl.BlockSpec((tm, D_MODEL), lambda i, j: (i, 0)),
            pl.BlockSpec((tm, tf), lambda i, j: (i, j)),
            pl.BlockSpec((tm, tf), lambda i, j: (i, j)),
            pl.BlockSpec((D_MODEL, tm), lambda i, j: (0, i)),
            pl.BlockSpec((tf, tm), lambda i, j: (j, i)),
        ],
        out_shape=[
            jax.ShapeDtypeStruct((t, D_MODEL), F32),
            jax.ShapeDtypeStruct((t, D_FF), BF16),
            jax.ShapeDtypeStruct((t, D_FF), BF16),
            jax.ShapeDtypeStruct((D_MODEL, t), BF16),
            jax.ShapeDtypeStruct((D_FF, t), BF16),
        ],
        scratch_shapes=[pltpu.VMEM((tm, D_MODEL), BF16), pltpu.VMEM((tm, D_MODEL), F32)],
        compiler_params=_params("parallel", "arbitrary"),
    )(x, gain, wg, wu, wd)


def _ffn_bwd(dout, x, gain, a, b, wg, wu, wd, *, name, tm=512, tf=256):
    t = x.shape[0]
    nj = D_FF // tf

    def body(do_ref, x_ref, g_ref, a_ref, b_ref, wg_ref, wu_ref, wd_ref, dx_ref, dg_ref, da_ref, db_ref, dob_ref,
             dob_scr, dh):
        i = pl.program_id(0)
        j = pl.program_id(1)

        @pl.when(j == 0)
        def _():
            d = (0.5 * do_ref[...]).astype(BF16)
            dob_scr[...] = d
            dob_ref[...] = d
            dh[...] = jnp.zeros_like(dh)

        ds = _dot_nt(dob_scr[...], wd_ref[...])
        av = a_ref[...].astype(F32)
        bv = b_ref[...].astype(F32)
        sig = _sigmoid(av)
        dbv = (ds * (av * sig)).astype(BF16)
        dav = (ds * bv * (sig * (1.0 + av * (1.0 - sig)))).astype(BF16)
        da_ref[...] = dav
        db_ref[...] = dbv
        dh[...] += _dot_nt(dav, wg_ref[...]) + _dot_nt(dbv, wu_ref[...])

        @pl.when(j == nj - 1)
        def _():
            xhat, rstd = _rms(x_ref[...])
            dhv = dh[...]
            part = jnp.sum(dhv * xhat, axis=0, keepdims=True)

            @pl.when(i == 0)
            def _():
                dg_ref[...] = part

            @pl.when(i > 0)
            def _():
                dg_ref[...] += part

            dxh = dhv * g_ref[...]
            dx_ref[...] = do_ref[...] + rstd * (dxh - xhat * jnp.mean(dxh * xhat, axis=-1, keepdims=True))

    return pl.pallas_call(
        body,
        name=name,
        grid=(t // tm, nj),
        in_specs=[
            pl.BlockSpec((tm, D_MODEL), lambda i, j: (i, 0)),
            pl.BlockSpec((tm, D_MODEL), lambda i, j: (i, 0)),
            pl.BlockSpec((1, D_MODEL), lambda i, j: (0, 0)),
            pl.BlockSpec((tm, tf), lambda i, j: (i, j)),
            pl.BlockSpec((tm, tf), lambda i, j: (i, j)),
            pl.BlockSpec((D_MODEL, tf), lambda i, j: (0, j)),
            pl.BlockSpec((D_MODEL, tf), lambda i, j: (0, j)),
            pl.BlockSpec((tf, D_MODEL), lambda i, j: (j, 0)),
        ],
        out_specs=[
            pl.BlockSpec((tm, D_MODEL), lambda i, j: (i, 0)),
            pl.BlockSpec((1, D_MODEL), lambda i, j: (0, 0)),
            pl.BlockSpec((tm, tf), lambda i, j: (i, j)),
            pl.BlockSpec((tm, tf), lambda i, j: (i, j)),
            pl.BlockSpec((tm, D_MODEL), lambda i, j: (i, 0)),
        ],
        out_shape=[
            jax.ShapeDtypeStruct((t, D_MODEL), F32),
            jax.ShapeDtypeStruct((1, D_MODEL), F32),
            jax.ShapeDtypeStruct((t, D_FF), BF16),
            jax.ShapeDtypeStruct((t, D_FF), BF16),
            jax.ShapeDtypeStruct((t, D_MODEL), BF16),
        ],
        scratch_shapes=[pltpu.VMEM((tm, D_MODEL), BF16), pltpu.VMEM((tm, D_MODEL), F32)],
        compiler_params=_params("arbitrary", "arbitrary"),
    )(dout, x, gain, a, b, wg, wu, wd)


def _norm_fwd(x, gain, *, name, tm=512):
    t = x.shape[0]

    def body(x_ref, g_ref, h_ref, ht_ref):
        xhat, _ = _rms(x_ref[...])
        h = xhat * g_ref[...]
        h_ref[...] = h.astype(BF16)
        ht_ref[...] = h.T.astype(BF16)

    return pl.pallas_call(
        body,
        name=name,
        grid=(t // tm,),
        in_specs=[pl.BlockSpec((tm, D_MODEL), lambda i: (i, 0)), pl.BlockSpec((1, D_MODEL), lambda i: (0, 0))],
        out_specs=[pl.BlockSpec((tm, D_MODEL), lambda i: (i, 0)), pl.BlockSpec((D_MODEL, tm), lambda i: (0, i))],
        out_shape=[jax.ShapeDtypeStruct((t, D_MODEL), BF16), jax.ShapeDtypeStruct((D_MODEL, t), BF16)],
        compiler_params=_params("parallel"),
    )(x, gain)


def _norm_bwd(dh, x, gain, dres, *, name, tm=512):
    t = x.shape[0]

    def body(dh_ref, x_ref, g_ref, dr_ref, dx_ref, dg_ref):
        i = pl.program_id(0)
        xhat, rstd = _rms(x_ref[...])
        dhv = dh_ref[...]
        part = jnp.sum(dhv * xhat, axis=0, keepdims=True)

        @pl.when(i == 0)
        def _():
            dg_ref[...] = part

        @pl.when(i > 0)
        def _():
            dg_ref[...] += part

        dxh = dhv * g_ref[...]
        dx_ref[...] = dr_ref[...] + rstd * (dxh - xhat * jnp.mean(dxh * xhat, axis=-1, keepdims=True))

    row = pl.BlockSpec((tm, D_MODEL), lambda i: (i, 0))
    vec = pl.BlockSpec((1, D_MODEL), lambda i: (0, 0))
    return pl.pallas_call(
        body,
        name=name,
        grid=(t // tm,),
        in_specs=[row, row, vec, row],
        out_specs=[row, vec],
        out_shape=[jax.ShapeDtypeStruct((t, D_MODEL), F32), jax.ShapeDtypeStruct((1, D_MODEL), F32)],
        compiler_params=_params("arbitrary"),
    )(dh, x, gain, dres)


ATT_TILE = 128


def _head_masks():
    lane = lax.broadcasted_iota(jnp.int32, (1, LANES), 1)
    first = (lane < SB_HEAD_DIM).astype(F32)
    return first, 1.0 - first


def _tri(n, relation):
    r = lax.broadcasted_iota(jnp.int32, (n, n), 0)
    c = lax.broadcasted_iota(jnp.int32, (n, n), 1)
    return relation(r, c).astype(BF16)


def _scan_dot(x, tri3):
    return _dot(jnp.concatenate(_split3(x), axis=1), tri3)


def _log_terms(z):
    sp = jnp.log(1.0 + jnp.exp(-jnp.abs(z)))
    return jnp.minimum(z, 0.0) - sp, -jnp.maximum(z, 0.0) - sp


def _attn_fwd(proj, *, name):
    t = proj.shape[0]
    tile = ATT_TILE
    n_pairs = SB_WIDTH // LANES

    def body(q_ref, k_ref, v_ref, o_ref, l_ref):
        qi = pl.program_id(1)
        masks = _head_masks()
        q = q_ref[...] * (SB_HEAD_DIM ** -0.5)
        qh = [(q * m).astype(BF16) for m in masks]
        tri = _tri(tile, lambda j, s: j > s)
        tri3 = jnp.concatenate([tri, tri, tri], axis=0)
        trow = qi * tile + lax.broadcasted_iota(jnp.int32, (tile, tile), 0)
        scol = lax.broadcasted_iota(jnp.int32, (tile, tile), 1)

        def step(it, carry):
            acc, c0, c1 = carry
            kb = qi - it
            off = pl.multiple_of(kb * tile, tile)
            kblk = k_ref[pl.ds(off, tile), :].astype(BF16)
            vblk = v_ref[pl.ds(off, tile), :]
            causal = (scol + off) < trow
            cs = [c0, c1]
            for h in range(2):
                z = _dot_nt(qh[h], kblk)
                lbeta, lrest = _log_terms(z)
                lrest = jnp.where(causal, lrest, 0.0)
                tail = _scan_dot(lrest, tri3) + cs[h]
                w = jnp.where(causal, jnp.exp(lbeta + tail), 0.0)
                acc = acc + _dot(w.astype(BF16), (vblk * masks[h]).astype(BF16))
                cs[h] = cs[h] + jnp.sum(lrest, axis=1, keepdims=True)
            return acc, cs[0], cs[1]

        zero = jnp.zeros((tile, 1), F32)
        acc, c0, c1 = lax.fori_loop(0, qi + 1, step, (jnp.zeros((tile, LANES), F32), zero, zero))
        o_ref[...] = acc
        l_ref[...] = c0 * masks[0] + c1 * masks[1]

    return pl.pallas_call(
        body,
        name=name,
        grid=(n_pairs, t // tile),
        in_specs=[
            pl.BlockSpec((tile, LANES), lambda p, i: (i, p)),
            pl.BlockSpec((t, LANES), lambda p, i: (0, n_pairs + p)),
            pl.BlockSpec((t, LANES), lambda p, i: (0, 2 * n_pairs + p)),
        ],
        out_specs=[pl.BlockSpec((tile, LANES), lambda p, i: (i, p))] * 2,
        out_shape=[jax.ShapeDtypeStruct((t, SB_WIDTH), F32)] * 2,
        compiler_params=_params("parallel", "parallel"),
    )(proj, proj, proj)


def _attn_bwd(proj, ltot, do, *, name):
    t = proj.shape[0]
    tile = ATT_TILE
    n_pairs = SB_WIDTH // LANES
    scale = SB_HEAD_DIM ** -0.5

    def body(q_ref, k_ref, v_ref, l_ref, do_ref, dq_ref, dk_ref, dv_ref):
        qi = pl.program_id(1)

        @pl.when(qi == 0)
        def _():
            dk_ref[...] = jnp.zeros_like(dk_ref)
            dv_ref[...] = jnp.zeros_like(dv_ref)

        masks = _head_masks()
        q = q_ref[...] * scale
        dov = do_ref[...]
        lt = l_ref[...]
        qh = [(q * m).astype(BF16) for m in masks]
        doh = [(dov * m).astype(BF16) for m in masks]
        total = [jnp.max(jnp.where(m > 0.0, lt, -jnp.inf), axis=1, keepdims=True) for m in masks]
        upto = _tri(tile, lambda j, s: j <= s)
        upto3 = jnp.concatenate([upto, upto, upto], axis=0)
        before = _tri(tile, lambda s, j: s < j)
        before3 = jnp.concatenate([before, before, before], axis=0)
        trow = qi * tile + lax.broadcasted_iota(jnp.int32, (tile, tile), 0)
        scol = lax.broadcasted_iota(jnp.int32, (tile, tile), 1)

        def step(kb, carry):
            dq, cl0, cl1, cg0, cg1 = carry
            off = pl.multiple_of(kb * tile, tile)
            kblk = k_ref[pl.ds(off, tile), :].astype(BF16)
            vblk = v_ref[pl.ds(off, tile), :].astype(BF16)
            causal = (scol + off) < trow
            cl = [cl0, cl1]
            cg = [cg0, cg1]
            dk_acc = jnp.zeros((tile, LANES), F32)
            dv_acc = jnp.zeros((tile, LANES), F32)
            for h in range(2):
                z = _dot_nt(qh[h], kblk)
                lbeta, lrest = _log_terms(z)
                lrest = jnp.where(causal, lrest, 0.0)
                tail = total[h] - (_scan_dot(lrest, upto3) + cl[h])
                w = jnp.where(causal, jnp.exp(lbeta + tail), 0.0)
                g = w * _dot_nt(doh[h], vblk)
                prior = _scan_dot(g, before3) + cg[h]
                sig = jnp.exp(lbeta)
                dz = jnp.where(causal, g * (1.0 - sig) - prior * sig, 0.0)
                dzb = dz.astype(BF16)
                dq = dq + _dot(dzb, (kblk * masks[h].astype(BF16)))
                dk_acc = dk_acc + _dot(dz.T.astype(BF16), qh[h])
                dv_acc = dv_acc + _dot(w.T.astype(BF16), doh[h])
                cl[h] = cl[h] + jnp.sum(lrest, axis=1, keepdims=True)
                cg[h] = cg[h] + jnp.sum(g, axis=1, keepdims=True)
            dk_ref[pl.ds(off, tile), :] += dk_acc
            dv_ref[pl.ds(off, tile), :] += dv_acc
            return dq, cl[0], cl[1], cg[0], cg[1]

        zero = jnp.zeros((tile, 1), F32)
        dq = lax.fori_loop(0, qi + 1, step, (jnp.zeros((tile, LANES), F32), zero, zero, zero, zero))[0]
        dq_ref[...] = dq * scale

    tile_spec = pl.BlockSpec((tile, LANES), lambda p, i: (i, p))
    full_spec = pl.BlockSpec((t, LANES), lambda p, i: (0, p))
    return pl.pallas_call(
        body,
        name=name,
        grid=(n_pairs, t // tile),
        in_specs=[
            tile_spec,
            pl.BlockSpec((t, LANES), lambda p, i: (0, n_pairs + p)),
            pl.BlockSpec((t, LANES), lambda p, i: (0, 2 * n_pairs + p)),
            tile_spec,
            tile_spec,
        ],
        out_specs=[tile_spec, full_spec, full_spec],
        out_shape=[jax.ShapeDtypeStruct((t, SB_WIDTH), F32)] * 3,
        compiler_params=_params("arbitrary", "arbitrary"),
    )(proj, proj, proj, ltot, do)


HG_BLOCK = 256
HG_HEADS = HG_WIDTH // HG_HEAD_DIM


def _chunk_mats(n):
    r = lax.broadcasted_iota(jnp.int32, (n, n), 0)
    c = lax.broadcasted_iota(jnp.int32, (n, n), 1)
    same = (r // HG_CHUNK) == (c // HG_CHUNK)
    upto = (same & (c <= r)).astype(BF16)
    whole = same.astype(BF16)
    onward = (same & (c >= r)).astype(BF16)
    return upto, whole, onward


def _rows_dot(mat, x):
    return _dot(jnp.concatenate([mat, mat, mat], axis=1), jnp.concatenate(_split3(x), axis=0))


def _lower_bound(lg_ref):
    lg = lg_ref[...]
    return _sigmoid(lg[0:1, :] - lg[1:2, :])


def _hgrn_prepare(q_ref, f_ref, lb, h, upto, whole):
    cols = slice(h * HG_HEAD_DIM, (h + 1) * HG_HEAD_DIM)
    lbh = lb[:, cols]
    sg = _sigmoid(f_ref[:, cols])
    forget = lbh + (1.0 - lbh) * sg
    logf = jnp.log(forget)
    kk = (1.0 - lbh) * (1.0 - sg)
    qv = q_ref[:, cols]
    qsig = _sigmoid(qv)
    qh = qv * qsig
    b = _rows_dot(upto, logf)
    blast = _rows_dot(whole, logf)
    return dict(lbh=lbh, sg=sg, forget=forget, kk=kk, qv=qv, qsig=qsig, qh=qh, b=b, eb=jnp.exp(b),
                ekb=jnp.exp(blast - b), dl=jnp.exp(blast))


def _hgrn_fwd(proj, logits, *, name):
    t = proj.shape[0]
    tb = HG_BLOCK
    nc = tb // HG_CHUNK
    hd = HG_HEAD_DIM

    def body(q_ref, f_ref, i_ref, lg_ref, o_ref, st_ref, state, qh_s, kk_s, b_s, qe_s, ke_s, dl_s):
        @pl.when(pl.program_id(0) == 0)
        def _():
            state[...] = jnp.zeros_like(state)

        lb = _lower_bound(lg_ref)
        upto, whole, _ = _chunk_mats(tb)
        for h in range(HG_HEADS):
            p = _hgrn_prepare(q_ref, f_ref, lb, h, upto, whole)
            qh_s[h] = p["qh"]
            kk_s[h] = p["kk"]
            b_s[h] = p["b"]
            qe_s[h] = (p["qh"] * p["eb"]).astype(BF16)
            ke_s[h] = (p["kk"] * p["ekb"]).astype(BF16)
            dl_s[h] = p["dl"]
        rowi = lax.broadcasted_iota(jnp.int32, (HG_CHUNK, hd), 0)

        def chunk(c, _):
            r0 = pl.multiple_of(c * HG_CHUNK, HG_CHUNK)
            rows = pl.ds(r0, HG_CHUNK)
            for h in range(HG_HEADS):
                cols = slice(h * hd, (h + 1) * hd)
                bc = b_s[h, rows, :]
                qc = qh_s[h, rows, :]
                kc = kk_s[h, rows, :]
                vc = i_ref[rows, cols]
                s_in = state[h]
                st_ref[c, h] = s_in
                o = _dot_nt(qe_s[h, rows, :], s_in.astype(BF16))
                for s in range(HG_CHUNK):
                    pair = jnp.where(rowi >= s, qc * jnp.exp(bc - bc[s:s + 1, :]) * kc[s:s + 1, :], 0.0)
                    o = o + jnp.sum(pair, axis=1, keepdims=True) * vc[s:s + 1, :]
                o_ref[rows, cols] = o
                state[h] = s_in * dl_s[h, pl.ds(r0, 1), :] + _dot_tn(vc.astype(BF16), ke_s[h, rows, :])
            return 0

        lax.fori_loop(0, nc, chunk, 0)

    blk = lambda col: pl.BlockSpec((tb, HG_WIDTH), lambda i: (i, col))
    head_f32 = pltpu.VMEM((HG_HEADS, tb, hd), F32)
    head_bf16 = pltpu.VMEM((HG_HEADS, tb, hd), BF16)
    return pl.pallas_call(
        body,
        name=name,
        grid=(t // tb,),
        in_specs=[blk(3), blk(4), blk(5), pl.BlockSpec((2, HG_WIDTH), lambda i: (0, 0))],
        out_specs=[
            pl.BlockSpec((tb, HG_WIDTH), lambda i: (i, 0)),
            pl.BlockSpec((nc, HG_HEADS, hd, hd), lambda i: (i, 0, 0, 0)),
        ],
        out_shape=[
            jax.ShapeDtypeStruct((t, HG_WIDTH), F32),
            jax.ShapeDtypeStruct((t // HG_CHUNK, HG_HEADS, hd, hd), F32),
        ],
        scratch_shapes=[pltpu.VMEM((HG_HEADS, hd, hd), F32), head_f32, head_f32, head_f32, head_bf16, head_bf16,
                        head_f32],
        compiler_params=_params("arbitrary"),
    )(proj, proj, proj, logits)


def _hgrn_bwd(proj, logits, states, do, *, name):
    t = proj.shape[0]
    tb = HG_BLOCK
    nb = t // tb
    nc = tb // HG_CHUNK
    hd = HG_HEAD_DIM

    def body(q_ref, f_ref, i_ref, lg_ref, st_ref, do_ref, dq_ref, df_ref, di_ref, dlb_ref,
             dstate, qh_s, kk_s, b_s, eb_s, ekb_s, qe_s, ke_s, dl_s, dqh_s, dkk_s, dlf_s):
        step = pl.program_id(0)

        @pl.when(step == 0)
        def _():
            dstate[...] = jnp.zeros_like(dstate)
            dlb_ref[...] = jnp.zeros_like(dlb_ref)

        lb = _lower_bound(lg_ref)
        upto, whole, _ = _chunk_mats(tb)
        prepared = []
        for h in range(HG_HEADS):
            p = _hgrn_prepare(q_ref, f_ref, lb, h, upto, whole)
            prepared.append(p)
            qh_s[h] = p["qh"]
            kk_s[h] = p["kk"]
            b_s[h] = p["b"]
            eb_s[h] = p["eb"]
            ekb_s[h] = p["ekb"]
            qe_s[h] = (p["qh"] * p["eb"]).astype(BF16)
            ke_s[h] = (p["kk"] * p["ekb"]).astype(BF16)
            dl_s[h] = p["dl"]
        rowi = lax.broadcasted_iota(jnp.int32, (HG_CHUNK, hd), 0)
        r16 = lax.broadcasted_iota(jnp.int32, (HG_CHUNK, HG_CHUNK), 0)
        c16 = lax.broadcasted_iota(jnp.int32, (HG_CHUNK, HG_CHUNK), 1)
        onward = (c16 >= r16).astype(BF16)

        def chunk(it, _):
            c = nc - 1 - it
            r0 = pl.multiple_of(c * HG_CHUNK, HG_CHUNK)
            rows = pl.ds(r0, HG_CHUNK)
            for h in range(HG_HEADS):
                cols = slice(h * hd, (h + 1) * hd)
                bc = b_s[h, rows, :]
                qc = qh_s[h, rows, :]
                kc = kk_s[h, rows, :]
                vc = i_ref[rows, cols]
                doc = do_ref[rows, cols]
                s_in = st_ref[c, h]
                ds_out = dstate[h]
                ds_out_b = ds_out.astype(BF16)
                docb = doc.astype(BF16)
                dl_row = dl_s[h, pl.ds(r0, 1), :]
                dqh = _dot(docb, s_in.astype(BF16)) * eb_s[h, rows, :]
                dkk = _dot(vc.astype(BF16), ds_out_b) * ekb_s[h, rows, :]
                dv = _dot_nt(ke_s[h, rows, :], ds_out_b)
                db = dqh * qc - dkk * kc
                dwhole = jnp.sum(dkk * kc, axis=0, keepdims=True) + jnp.sum(ds_out * s_in, axis=0, keepdims=True) * dl_row
                for s in range(HG_CHUNK):
                    keep = rowi >= s
                    at_s = rowi == s
                    e = jnp.exp(bc - bc[s:s + 1, :])
                    k_row = kc[s:s + 1, :]
                    pcol = jnp.sum(jnp.where(keep, qc * e * k_row, 0.0), axis=1, keepdims=True)
                    dpcol = jnp.sum(doc * vc[s:s + 1, :], axis=1, keepdims=True)
                    m = jnp.where(keep, e * dpcol, 0.0)
                    mk = m * k_row
                    dk_row = jnp.sum(m * qc, axis=0, keepdims=True)
                    dqh = dqh + mk
                    dkk = dkk + jnp.where(at_s, dk_row, 0.0)
                    db = db + mk * qc - jnp.where(at_s, dk_row * k_row, 0.0)
                    dv = dv + jnp.where(at_s, jnp.sum(pcol * doc, axis=0, keepdims=True), 0.0)
                dqh_s[h, rows, :] = dqh
                dkk_s[h, rows, :] = dkk
                dlf_s[h, rows, :] = _rows_dot(onward, db) + dwhole
                di_ref[rows, cols] = dv
                dstate[h] = ds_out * dl_row + _dot_tn(docb, qe_s[h, rows, :])
            return 0

        lax.fori_loop(0, nc, chunk, 0)
        for h in range(HG_HEADS):
            cols = slice(h * hd, (h + 1) * hd)
            p = prepared[h]
            dq_ref[:, cols] = dqh_s[h] * (p["qsig"] * (1.0 + p["qv"] * (1.0 - p["qsig"])))
            dforget = dlf_s[h] / p["forget"] - dkk_s[h]
            df_ref[:, cols] = dforget * (1.0 - p["lbh"]) * p["sg"] * (1.0 - p["sg"])
            dlb_ref[:, cols] += jnp.sum(dforget * (1.0 - p["sg"]), axis=0, keepdims=True)

    blk = lambda col: pl.BlockSpec((tb, HG_WIDTH), lambda i: (nb - 1 - i, col))
    vec = pl.BlockSpec((1, HG_WIDTH), lambda i: (0, 0))
    head_f32 = pltpu.VMEM((HG_HEADS, tb, hd), F32)
    head_bf16 = pltpu.VMEM((HG_HEADS, tb, hd), BF16)
    return pl.pallas_call(
        body,
        name=name,
        grid=(nb,),
        in_specs=[
            blk(3), blk(4), blk(5),
            pl.BlockSpec((2, HG_WIDTH), lambda i: (0, 0)),
            pl.BlockSpec((nc, HG_HEADS, hd, hd), lambda i: (nb - 1 - i, 0, 0, 0)),
            blk(0),
        ],
        out_specs=[blk(0), blk(0), blk(0), vec],
        out_shape=[jax.ShapeDtypeStruct((t, HG_WIDTH), F32)] * 3 + [jax.ShapeDtypeStruct((1, HG_WIDTH), F32)],
        scratch_shapes=[
            pltpu.VMEM((HG_HEADS, hd, hd), F32),
            head_f32, head_f32, head_f32, head_f32, head_f32, head_bf16, head_bf16, head_f32,
            head_f32, head_f32, head_f32,
        ],
        compiler_params=_params("arbitrary"),
    )(proj, proj, proj, logits, states, do)


def _group_mat(width, head_dim):
    r = lax.broadcasted_iota(jnp.int32, (width, width), 0)
    c = lax.broadcasted_iota(jnp.int32, (width, width), 1)
    return ((r // head_dim) == (c // head_dim)).astype(BF16)


def _head_mean(x, mat, head_dim):
    hi = x.astype(BF16)
    lo = (x - hi.astype(F32)).astype(BF16)
    return (_dot(hi, mat) + _dot(lo, mat)) * (1.0 / head_dim)


def _mix_out_fwd(o_sb, o_hg, proj, g_sb, g_hg, w_out, x1, *, name, tm=256):
    t = x1.shape[0]

    def body(osb_ref, ohg_ref, gate_ref, gsb_ref, ghg_ref, w_ref, x_ref, xo_ref, mt_ref):
        msb = _group_mat(SB_WIDTH, SB_HEAD_DIM)
        mhg = _group_mat(HG_WIDTH, HG_HEAD_DIM)
        osb = osb_ref[...]
        ohg = ohg_ref[...]
        nsb = osb * lax.rsqrt(_head_mean(osb * osb, msb, SB_HEAD_DIM) + EPS) * gsb_ref[...]
        gate = gate_ref[...]
        nhg = ohg * lax.rsqrt(_head_mean(ohg * ohg, mhg, HG_HEAD_DIM) + EPS) * ghg_ref[...] * (gate * _sigmoid(gate))
        mixed = jnp.concatenate([nsb, nhg], axis=1)
        mt_ref[...] = mixed.T.astype(BF16)
        xo_ref[...] = x_ref[...] + _dot(mixed.astype(BF16), w_ref[...])

    half = pl.BlockSpec((tm, SB_WIDTH), lambda i: (i, 0))
    vec = pl.BlockSpec((1, SB_WIDTH), lambda i: (0, 0))
    row = pl.BlockSpec((tm, D_MODEL), lambda i: (i, 0))
    return pl.pallas_call(
        body,
        name=name,
        grid=(t // tm,),
        in_specs=[half, half, pl.BlockSpec((tm, HG_WIDTH), lambda i: (i, 6)), vec, vec,
                  pl.BlockSpec((D_MODEL, D_MODEL), lambda i: (0, 0)), row],
        out_specs=[row, pl.BlockSpec((D_MODEL, tm), lambda i: (0, i))],
        out_shape=[jax.ShapeDtypeStruct((t, D_MODEL), F32), jax.ShapeDtypeStruct((D_MODEL, t), BF16)],
        compiler_params=_params("parallel"),
    )(o_sb, o_hg, proj, g_sb, g_hg, w_out, x1)


def _mix_out_bwd(dx2, o_sb, o_hg, proj, g_sb, g_hg, w_out, *, name, tm=256):
    t = dx2.shape[0]

    def body(dx_ref, osb_ref, ohg_ref, gate_ref, gsb_ref, ghg_ref, w_ref, dosb_ref, dohg_ref, dgate_ref, dgsb_ref,
             dghg_ref, dxb_ref):
        i = pl.program_id(0)
        msb = _group_mat(SB_WIDTH, SB_HEAD_DIM)
        mhg = _group_mat(HG_WIDTH, HG_HEAD_DIM)
        dxb = dx_ref[...].astype(BF16)
        dxb_ref[...] = dxb
        dmixed = _dot_nt(dxb, w_ref[...])
        dnsb = dmixed[:, :SB_WIDTH]
        dy = dmixed[:, SB_WIDTH:]

        osb = osb_ref[...]
        rstd = lax.rsqrt(_head_mean(osb * osb, msb, SB_HEAD_DIM) + EPS)
        ohat = osb * rstd
        part_sb = jnp.sum(dnsb * ohat, axis=0, keepdims=True)
        dohat = dnsb * gsb_ref[...]
        dosb_ref[...] = rstd * (dohat - ohat * _head_mean(dohat * ohat, msb, SB_HEAD_DIM))

        ohg = ohg_ref[...]
        rstd = lax.rsqrt(_head_mean(ohg * ohg, mhg, HG_HEAD_DIM) + EPS)
        ohat = ohg * rstd
        gate = gate_ref[...]
        sig = _sigmoid(gate)
        dn = dy * (gate * sig)
        dgate_ref[...] = dy * (ohat * ghg_ref[...]) * (sig * (1.0 + gate * (1.0 - sig)))
        part_hg = jnp.sum(dn * ohat, axis=0, keepdims=True)
        dohat = dn * ghg_ref[...]
        dohg_ref[...] = rstd * (dohat - ohat * _head_mean(dohat * ohat, mhg, HG_HEAD_DIM))

        @pl.when(i == 0)
        def _():
            dgsb_ref[...] = part_sb
            dghg_ref[...] = part_hg

        @pl.when(i > 0)
        def _():
            dgsb_ref[...] += part_sb
            dghg_ref[...] += part_hg

    half = pl.BlockSpec((tm, SB_WIDTH), lambda i: (i, 0))
    vec = pl.BlockSpec((1, SB_WIDTH), lambda i: (0, 0))
    row = pl.BlockSpec((tm, D_MODEL), lambda i: (i, 0))
    return pl.pallas_call(
        body,
        name=name,
        grid=(t // tm,),
        in_specs=[row, half, half, pl.BlockSpec((tm, HG_WIDTH), lambda i: (i, 6)), vec, vec,
                  pl.BlockSpec((D_MODEL, D_MODEL), lambda i: (0, 0))],
        out_specs=[half, half, half, vec, vec, row],
        out_shape=[jax.ShapeDtypeStruct((t, SB_WIDTH), F32)] * 3 + [jax.ShapeDtypeStruct((1, SB_WIDTH), F32)] * 2
        + [jax.ShapeDtypeStruct((t, D_MODEL), BF16)],
        compiler_params=_params("arbitrary"),
    )(dx2, o_sb, o_hg, proj, g_sb, g_hg, w_out)


def _loss_head(x3, gain, target, *, name, tm=512):
    t = x3.shape[0]

    def body(x_ref, g_ref, y_ref, dx_ref, dg_ref, loss_ref):
        i = pl.program_id(0)
        xhat, rstd = _rms(x_ref[...])
        err = xhat * g_ref[...] - y_ref[...]
        part_loss = 0.5 * jnp.sum(jnp.mean(err * err, axis=-1, keepdims=True), axis=0, keepdims=True)
        dy = err * (1.0 / D_MODEL)
        part_g = jnp.sum(dy * xhat, axis=0, keepdims=True)

        @pl.when(i == 0)
        def _():
            dg_ref[...] = part_g
            loss_ref[...] = jnp.broadcast_to(part_loss, loss_ref.shape)

        @pl.when(i > 0)
        def _():
            dg_ref[...] += part_g
            loss_ref[...] += jnp.broadcast_to(part_loss, loss_ref.shape)

        dxh = dy * g_ref[...]
        dx_ref[...] = rstd * (dxh - xhat * jnp.mean(dxh * xhat, axis=-1, keepdims=True))

    row = pl.BlockSpec((tm, D_MODEL), lambda i: (i, 0))
    vec = pl.BlockSpec((1, D_MODEL), lambda i: (0, 0))
    return pl.pallas_call(
        body,
        name=name,
        grid=(t // tm,),
        in_specs=[row, vec, row],
        out_specs=[row, vec, vec],
        out_shape=[jax.ShapeDtypeStruct((t, D_MODEL), F32), jax.ShapeDtypeStruct((1, D_MODEL), F32),
                   jax.ShapeDtypeStruct((1, D_MODEL), F32)],
        compiler_params=_params("arbitrary"),
    )(x3, gain, target)


def _local_step(x, target, norms, logits, w):
    x1, a1, b1, h1t, s1t = _ffn_fwd(x, norms["ffn1"], w["g1"], w["u1"], w["d1"], name="ffn1_fwd")
    hm, hmt = _norm_fwd(x1, norms["mix"], name="mix_norm_fwd")
    proj = _mm(hm, w["in"], name="in_proj", tm=512, tn=512)
    o_sb, ltot = _attn_fwd(proj, name="sb_attn_fwd")
    o_hg, states = _hgrn_fwd(proj, logits, name="hgrn2_fwd")
    x2, mixed_t = _mix_out_fwd(o_sb, o_hg, proj, norms["sb"], norms["hg"], w["out"], x1, name="mix_out_fwd")
    x3, a2, b2, h2t, s2t = _ffn_fwd(x2, norms["ffn2"], w["g2"], w["u2"], w["d2"], name="ffn2_fwd")
    dx3, d_final, loss_row = _loss_head(x3, norms["final"], target, name="loss_head")

    gw, gv = {}, {"final": d_final}
    dx2, gv["ffn2"], da2, db2, dob2 = _ffn_bwd(dx3, x2, norms["ffn2"], a2, b2, w["g2"], w["u2"], w["d2"], name="ffn2_bwd")
    gw["g2"] = _mm(h2t, da2, name="ffn2_dgate", tm=D_MODEL, tn=256)
    gw["u2"] = _mm(h2t, db2, name="ffn2_dup", tm=D_MODEL, tn=256)
    gw["d2"] = _mm(s2t, dob2, name="ffn2_ddown", tm=256, tn=D_MODEL)

    do_sb, do_hg, d_gate, gv["sb"], gv["hg"], dx2b = _mix_out_bwd(
        dx2, o_sb, o_hg, proj, norms["sb"], norms["hg"], w["out"], name="mix_out_bwd")
    gw["out"] = _mm(mixed_t, dx2b, name="out_dw", tm=D_MODEL, tn=256)
    dq_sb, dk_sb, dv_sb = _attn_bwd(proj, ltot, do_sb, name="sb_attn_bwd")
    dq_hg, df_hg, di_hg, d_lb = _hgrn_bwd(proj, logits, states, do_hg, name="hgrn2_bwd")
    dproj = jnp.concatenate([dq_sb, dk_sb, dv_sb, dq_hg, df_hg, di_hg, d_gate], axis=1).astype(BF16)
    gw["in"] = _mm(hmt, dproj, name="in_dw", tm=D_MODEL, tn=256)
    dhm = _mm(dproj, w["in"], name="in_dx", tm=512, tn=D_MODEL, nt=True)
    dx1, gv["mix"] = _norm_bwd(dhm, x1, norms["mix"], dx2, name="mix_norm_bwd")

    dx, gv["ffn1"], da1, db1, dob1 = _ffn_bwd(dx1, x, norms["ffn1"], a1, b1, w["g1"], w["u1"], w["d1"], name="ffn1_bwd")
    gw["g1"] = _mm(h1t, da1, name="ffn1_dgate", tm=D_MODEL, tn=256)
    gw["u1"] = _mm(h1t, db1, name="ffn1_dup", tm=D_MODEL, tn=256)
    gw["d1"] = _mm(s1t, dob1, name="ffn1_ddown", tm=256, tn=D_MODEL)
    gv["lb"] = d_lb
    return loss_row, dx, gw, gv


HBM = pl.BlockSpec(memory_space=pl.ANY)


def _place():
    return lax.axis_index("x"), lax.axis_index("y"), lax.axis_index("c")


def _slot(px, py, pc):
    return 4 * px + 2 * py + pc


def _all_gather(blocks, *, name):
    n = len(blocks)

    def body(*refs):
        ins, outs = refs[:n], refs[n:2 * n]
        send_sems, recv_sems, local_sems = refs[2 * n:]
        x, y, c = _place()
        me, sibling = (x, y, c), (x, y, 1 - c)
        chips = [(1 - x, y), (x, 1 - y), (1 - x, 1 - y)]

        def copy(a, k, block, to, src=None):
            dst = outs[a].at[_slot(*block)]
            return pltpu.make_async_remote_copy(
                src_ref=dst if src is None else src, dst_ref=dst, send_sem=send_sems.at[7 * a + k],
                recv_sem=recv_sems.at[7 * a + k], device_id=to, device_id_type=MESH)

        mine = [pltpu.make_async_copy(ins[a], outs[a].at[_slot(*me)], local_sems.at[a]) for a in range(n)]
        for cp in mine:
            cp.start()
        first = []
        for a in range(n):
            first.append(copy(a, 0, me, sibling, src=ins[a]))
            first += [copy(a, 1 + j, me, (*chip, c), src=ins[a]) for j, chip in enumerate(chips)]
        for cp in first:
            cp.start()
        passed = []
        for j, chip in enumerate(chips):
            for a in range(n):
                copy(a, 1 + j, (*chip, c), me).wait_recv()
                fwd = copy(a, 4 + j, (*chip, c), sibling)
                fwd.start()
                passed.append(fwd)
        for a in range(n):
            copy(a, 0, sibling, me).wait_recv()
            for j, chip in enumerate(chips):
                copy(a, 4 + j, (*chip, 1 - c), me).wait_recv()
        for cp in first + passed:
            cp.wait_send()
        for cp in mine:
            cp.wait()

    return pl.pallas_call(
        body,
        name=name,
        in_specs=[HBM] * n,
        out_specs=[HBM] * n,
        out_shape=[jax.ShapeDtypeStruct((N_DEV,) + b.shape, b.dtype) for b in blocks],
        scratch_shapes=[pltpu.SemaphoreType.DMA((7 * n,)), pltpu.SemaphoreType.DMA((7 * n,)),
                        pltpu.SemaphoreType.DMA((n,))],
    )(*blocks)


def _flipped(place, d):
    return tuple(1 - p if (d >> (2 - axis)) & 1 else p for axis, p in enumerate(place))


def _scatter_by_owner(stacks, *, name):
    n = len(stacks)

    def body(*refs):
        ins, outs = refs[:n], refs[n:2 * n]
        send_sems, recv_sems, local_sems = refs[2 * n:]
        me = _place()
        mine = [pltpu.make_async_copy(ins[a].at[_slot(*me)], outs[a].at[_slot(*me)], local_sems.at[a]) for a in range(n)]
        for cp in mine:
            cp.start()
        copies = []
        for d in range(1, N_DEV):
            peer = _flipped(me, d)
            for a in range(n):
                copies.append(pltpu.make_async_remote_copy(
                    src_ref=ins[a].at[_slot(*peer)], dst_ref=outs[a].at[_slot(*me)], send_sem=send_sems.at[7 * a + d - 1],
                    recv_sem=recv_sems.at[7 * a + d - 1], device_id=peer, device_id_type=MESH))
        for cp in copies:
            cp.start()
        for cp in copies:
            cp.wait()
        for cp in mine:
            cp.wait()

    return pl.pallas_call(
        body,
        name=name,
        in_specs=[HBM] * n,
        out_specs=[HBM] * n,
        out_shape=[jax.ShapeDtypeStruct(s.shape, s.dtype) for s in stacks],
        scratch_shapes=[pltpu.SemaphoreType.DMA((7 * n,)), pltpu.SemaphoreType.DMA((7 * n,)),
                        pltpu.SemaphoreType.DMA((n,))],
    )(*stacks)


def _adamw(w, g, m, v):
    m = ADAM_B1 * m + (1.0 - ADAM_B1) * g
    v = ADAM_B2 * v + (1.0 - ADAM_B2) * (g * g)
    m_hat = m / (1.0 - ADAM_B1 ** ADAM_STEP)
    v_hat = v / (1.0 - ADAM_B2 ** ADAM_STEP)
    delta = -ADAM_LR * (m_hat / (jnp.sqrt(v_hat) + ADAM_EPS) + ADAM_WD * w)
    return delta, m, v


def _sum_and_update(parts, w, m, v, *, name, tr):
    rows, cols = w.shape

    def body(p_ref, w_ref, m_ref, v_ref, g_ref, d_ref, mo_ref, vo_ref):
        g = p_ref[0].astype(F32)
        for s in range(1, N_DEV):
            g = g + p_ref[s].astype(F32)
        g_ref[...] = g
        d_ref[...], mo_ref[...], vo_ref[...] = _adamw(w_ref[...], g, m_ref[...], v_ref[...])

    flat = pl.BlockSpec((tr, cols), lambda i: (i, 0))
    return pl.pallas_call(
        body,
        name=name,
        grid=(rows // tr,),
        in_specs=[pl.BlockSpec((N_DEV, tr, cols), lambda i: (0, i, 0)), flat, flat, flat],
        out_specs=[flat] * 4,
        out_shape=[jax.ShapeDtypeStruct((rows, cols), F32)] * 4,
        compiler_params=_params("parallel"),
    )(parts, w, m, v)


VEC_ROWS = 8
ROW_LOGITS, ROW_LOSS = 5, 7


def _vectors_update(part, w, m, v, *, name):
    def body(p_ref, w_ref, m_ref, v_ref, g_ref, d_ref, mo_ref, vo_ref, loss_ref, all_ref, send_sems, recv_sems):
        me = _place()
        all_ref[_slot(*me)] = p_ref[...]
        copies = []
        for d in range(1, N_DEV):
            peer = _flipped(me, d)
            copies.append(pltpu.make_async_remote_copy(
                src_ref=p_ref, dst_ref=all_ref.at[_slot(*me)], send_sem=send_sems.at[d - 1], recv_sem=recv_sems.at[d - 1],
                device_id=peer, device_id_type=MESH))
        for cp in copies:
            cp.start()
        for cp in copies:
            cp.wait()
        total = all_ref[0]
        for s in range(1, N_DEV):
            total = total + all_ref[s]
        wv = w_ref[...]
        half = D_MODEL // 2
        lb = _sigmoid(wv[ROW_LOGITS:ROW_LOGITS + 1, :half] - wv[ROW_LOGITS:ROW_LOGITS + 1, half:])
        d_first = total[ROW_LOGITS:ROW_LOGITS + 1, :half] * lb * (1.0 - lb)
        d_logits = jnp.concatenate([d_first, -d_first], axis=1)
        rowi = lax.broadcasted_iota(jnp.int32, (VEC_ROWS, D_MODEL), 0)
        g = jnp.where(rowi == ROW_LOGITS, d_logits, jnp.where(rowi < ROW_LOGITS, total, 0.0))
        g_ref[...] = g
        d_ref[...], mo_ref[...], vo_ref[...] = _adamw(wv, g, m_ref[...], v_ref[...])
        loss_ref[...] = total[ROW_LOSS:ROW_LOSS + 1, :]

    vmem = pl.BlockSpec(memory_space=pltpu.VMEM)
    return pl.pallas_call(
        body,
        name=name,
        in_specs=[vmem] * 4,
        out_specs=[vmem] * 5,
        out_shape=[jax.ShapeDtypeStruct((VEC_ROWS, D_MODEL), F32)] * 4 + [jax.ShapeDtypeStruct((1, D_MODEL), F32)],
        scratch_shapes=[pltpu.VMEM((N_DEV, VEC_ROWS, D_MODEL), F32), pltpu.SemaphoreType.DMA((7,)),
                        pltpu.SemaphoreType.DMA((7,))],
    )(part, w, m, v)


COL_KEYS = ("g1", "u1", "in", "g2", "u2")
COL_WIDTHS = (FF_SHARD, FF_SHARD, IN_SHARD, FF_SHARD, FF_SHARD)
ROW_KEYS = ("d1", "out", "d2")
ROW_HEIGHTS = (FF_SHARD, OUT_SHARD, FF_SHARD)


def _offsets(sizes):
    out, at = [], 0
    for s in sizes:
        out.append((at, at + s))
        at += s
    return out


def _pack_shards(shards):
    cols = jnp.concatenate([shards[k] for k in COL_KEYS], axis=1)
    rows = jnp.concatenate([shards[k] for k in ROW_KEYS], axis=0)
    return cols, rows


def _unpack_shards(cols, rows):
    out = {k: cols[:, lo:hi] for k, (lo, hi) in zip(COL_KEYS, _offsets(COL_WIDTHS))}
    out.update({k: rows[lo:hi, :] for k, (lo, hi) in zip(ROW_KEYS, _offsets(ROW_HEIGHTS))})
    return out


def _full_from_stacks(cols, rows):
    out = {}
    for k, (lo, hi) in zip(COL_KEYS, _offsets(COL_WIDTHS)):
        out[k] = cols[:, :, lo:hi].transpose(1, 0, 2).reshape(D_MODEL, N_DEV * (hi - lo))
    for k, (lo, hi) in zip(ROW_KEYS, _offsets(ROW_HEIGHTS)):
        out[k] = rows[:, lo:hi, :].reshape(N_DEV * (hi - lo), D_MODEL)
    return out


def _stacks_from_full(full):
    cols = jnp.concatenate(
        [full[k].reshape(D_MODEL, N_DEV, wd).transpose(1, 0, 2) for k, wd in zip(COL_KEYS, COL_WIDTHS)], axis=2)
    rows = jnp.concatenate([full[k].reshape(N_DEV, ht, D_MODEL) for k, ht in zip(ROW_KEYS, ROW_HEIGHTS)], axis=1)
    return cols.astype(BF16), rows.astype(BF16)


def _vector_rows(rows):
    rowi = lax.broadcasted_iota(jnp.int32, (VEC_ROWS, D_MODEL), 0)
    out = jnp.zeros((VEC_ROWS, D_MODEL), F32)
    for i, r in enumerate(rows):
        if r is not None:
            out = jnp.where(rowi == i, r, out)
    return out


def kernel(x, ffn1_norm, ffn1_w_gate, ffn1_w_up, ffn1_w_down, mix_norm, w_in, sb_out_norm, hg_lower_bound_logits, hg_out_norm, w_out, ffn2_norm, ffn2_w_gate, ffn2_w_up, ffn2_w_down, final_norm, loss_target, m_ffn1_norm, m_ffn1_w_gate, m_ffn1_w_up, m_ffn1_w_down, m_mix_norm, m_w_in, m_sb_out_norm, m_hg_lower_bound_logits, m_hg_out_norm, m_w_out, m_ffn2_norm, m_ffn2_w_gate, m_ffn2_w_up, m_ffn2_w_down, m_final_norm, v_ffn1_norm, v_ffn1_w_gate, v_ffn1_w_up, v_ffn1_w_down, v_mix_norm, v_w_in, v_sb_out_norm, v_hg_lower_bound_logits, v_hg_out_norm, v_w_out, v_ffn2_norm, v_ffn2_w_gate, v_ffn2_w_up, v_ffn2_w_down, v_final_norm):
    def matrices(g1, u1, d1, win, wout, g2, u2, d2):
        return {"g1": g1[0], "u1": u1[0], "d1": d1[0], "in": win[0], "out": wout[0], "g2": g2[0], "u2": u2[0], "d2": d2[0]}

    def vectors(n1, nm, nsb, lg, nhg, n2, nf):
        return [n1, nm, n2, nf.reshape(1, D_MODEL), jnp.concatenate([nsb, nhg], axis=1), lg.reshape(1, D_MODEL), None, None]

    w_sh = matrices(ffn1_w_gate, ffn1_w_up, ffn1_w_down, w_in, w_out, ffn2_w_gate, ffn2_w_up, ffn2_w_down)
    m_sh = matrices(m_ffn1_w_gate, m_ffn1_w_up, m_ffn1_w_down, m_w_in, m_w_out, m_ffn2_w_gate, m_ffn2_w_up, m_ffn2_w_down)
    v_sh = matrices(v_ffn1_w_gate, v_ffn1_w_up, v_ffn1_w_down, v_w_in, v_w_out, v_ffn2_w_gate, v_ffn2_w_up, v_ffn2_w_down)

    w_cols, w_rows = _pack_shards(w_sh)
    all_cols, all_rows = _all_gather([w_cols.astype(BF16), w_rows.astype(BF16)], name="gather_weights")
    w_full = _full_from_stacks(all_cols, all_rows)

    norms = {"ffn1": ffn1_norm, "mix": mix_norm, "sb": sb_out_norm, "hg": hg_out_norm, "ffn2": ffn2_norm,
             "final": final_norm.reshape(1, D_MODEL)}
    loss_row, grad_x, gw, gv = _local_step(x[0], loss_target[0], norms, hg_lower_bound_logits, w_full)

    g_cols, g_rows = _scatter_by_owner(list(_stacks_from_full(gw)), name="exchange_grads")
    m_cols, m_rows = _pack_shards(m_sh)
    v_cols, v_rows = _pack_shards(v_sh)
    col_out = _sum_and_update(g_cols, w_cols, m_cols, v_cols, name="adamw_cols", tr=128)
    row_out = _sum_and_update(g_rows, w_rows, m_rows, v_rows, name="adamw_rows", tr=208)
    mats = [_unpack_shards(c, r) for c, r in zip(col_out, row_out)]

    lb_row = jnp.concatenate([gv["lb"], jnp.zeros_like(gv["lb"])], axis=1)
    part = _vector_rows([gv["ffn1"], gv["mix"], gv["ffn2"], gv["final"], jnp.concatenate([gv["sb"], gv["hg"]], axis=1),
                         lb_row, None, loss_row])
    vec_w = _vector_rows(vectors(ffn1_norm, mix_norm, sb_out_norm, hg_lower_bound_logits, hg_out_norm, ffn2_norm, final_norm))
    vec_m = _vector_rows(vectors(m_ffn1_norm, m_mix_norm, m_sb_out_norm, m_hg_lower_bound_logits, m_hg_out_norm,
                                 m_ffn2_norm, m_final_norm))
    vec_v = _vector_rows(vectors(v_ffn1_norm, v_mix_norm, v_sb_out_norm, v_hg_lower_bound_logits, v_hg_out_norm,
                                 v_ffn2_norm, v_final_norm))
    *vecs, loss_out = _vectors_update(part, vec_w, vec_m, vec_v, name="vectors_update")

    def leaves(mat, vec):
        half = D_MODEL // 2
        return (
            vec[0:1], mat["g1"][None], mat["u1"][None], mat["d1"][None], vec[1:2], mat["in"][None], vec[4:5, :half],
            vec[ROW_LOGITS].reshape(2, half), vec[4:5, half:], mat["out"][None], vec[2:3], mat["g2"][None], mat["u2"][None],
            mat["d2"][None], vec[3],
        )

    out = [loss_out[0, 0], grad_x[None]]
    for mat, vec in zip(mats, vecs):
        out.extend(leaves(mat, vec))
    return tuple(out)
```

```python
import jax
import jax.numpy as jnp
from jax import lax
from jax.experimental import pallas as pl
from jax.experimental.pallas import tpu as pltpu

F32, BF16 = jnp.float32, jnp.bfloat16
D_MODEL = 1024
D_FF = 2816
SB_WIDTH = 512
HG_WIDTH = 512
SB_HEAD_DIM = 64
HG_HEAD_DIM = 128
IN_COLS = 3584
EPS = 1e-6
N_DEV = 8
FF_SHARD = D_FF // N_DEV
IN_SHARD = IN_COLS // N_DEV
OUT_SHARD = D_MODEL // N_DEV
LANES = 128
HG_CHUNK = 16
VMEM_LIMIT_BYTES = 48 * 1024 * 1024
ADAM_LR, ADAM_B1, ADAM_B2, ADAM_EPS, ADAM_WD, ADAM_STEP = 0.001, 0.9, 0.999, 1e-08, 0.01, 10
MESH = pl.DeviceIdType.MESH


def _params(*semantics):
    return pltpu.CompilerParams(dimension_semantics=semantics, vmem_limit_bytes=VMEM_LIMIT_BYTES)


def _dot(a, b):
    return jnp.dot(a, b, preferred_element_type=F32)


def _dot_nt(a, b):
    return lax.dot_general(a, b, (((1,), (1,)), ((), ())), preferred_element_type=F32)


def _dot_tn(a, b):
    return lax.dot_general(a, b, (((0,), (0,)), ((), ())), preferred_element_type=F32)


def _split3(x):
    hi = x.astype(BF16)
    r1 = x - hi.astype(F32)
    mid = r1.astype(BF16)
    lo = (r1 - mid.astype(F32)).astype(BF16)
    return hi, mid, lo


def _rms(xv):
    rstd = lax.rsqrt(jnp.mean(xv * xv, axis=-1, keepdims=True) + EPS)
    return xv * rstd, rstd


def _sigmoid(x):
    return 1.0 / (1.0 + jnp.exp(-x))


def _mm(a, b, *, name, tm, tn, nt=False, scale=None, add=None, out_dtype=F32):
    m, k = a.shape
    n = b.shape[0] if nt else b.shape[1]
    assert m % tm == 0 and n % tn == 0, (name, a.shape, b.shape, tm, tn)

    def body(*refs):
        a_ref, b_ref = refs[0], refs[1]
        o_ref = refs[-1]
        av = a_ref[...].astype(BF16)
        bv = b_ref[...].astype(BF16)
        r = _dot_nt(av, bv) if nt else _dot(av, bv)
        if scale is not None:
            r = r * scale
        if add is not None:
            r = r + refs[2][...]
        o_ref[...] = r.astype(out_dtype)

    in_specs = [
        pl.BlockSpec((tm, k), lambda i, j: (i, 0)),
        pl.BlockSpec((tn, k), lambda i, j: (j, 0)) if nt else pl.BlockSpec((k, tn), lambda i, j: (0, j)),
    ]
    operands = [a, b]
    if add is not None:
        in_specs.append(pl.BlockSpec((tm, tn), lambda i, j: (i, j)))
        operands.append(add)
    return pl.pallas_call(
        body,
        name=name,
        grid=(m // tm, n // tn),
        in_specs=in_specs,
        out_specs=pl.BlockSpec((tm, tn), lambda i, j: (i, j)),
        out_shape=jax.ShapeDtypeStruct((m, n), out_dtype),
        compiler_params=_params("parallel", "parallel"),
    )(*operands)


def _ffn_fwd(x, gain, wg, wu, wd, *, name, tm=512, tf=256):
    t = x.shape[0]
    nj = D_FF // tf

    def body(x_ref, g_ref, wg_ref, wu_ref, wd_ref, xo_ref, a_ref, b_ref, ht_ref, st_ref, h_scr, acc):
        j = pl.program_id(1)

        @pl.when(j == 0)
        def _():
            xhat, _ = _rms(x_ref[...])
            h = xhat * g_ref[...]
            h_scr[...] = h.astype(BF16)
            ht_ref[...] = h.T.astype(BF16)
            acc[...] = jnp.zeros_like(acc)

        h = h_scr[...]
        a = _dot(h, wg_ref[...])
        b = _dot(h, wu_ref[...])
        a_ref[...] = a.astype(BF16)
        b_ref[...] = b.astype(BF16)
        s = a * _sigmoid(a) * b
        st_ref[...] = s.T.astype(BF16)
        acc[...] += _dot(s.astype(BF16), wd_ref[...])

        @pl.when(j == nj - 1)
        def _():
            xo_ref[...] = x_ref[...] + 0.5 * acc[...]

    return pl.pallas_call(
        body,
        name=name,
        grid=(t // tm, nj),
        in_specs=[
            pl.BlockSpec((tm, D_MODEL), lambda i, j: (i, 0)),
            pl.BlockSpec((1, D_MODEL), lambda i, j: (0, 0)),
            pl.BlockSpec((D_MODEL, tf), lambda i, j: (0, j)),
            pl.BlockSpec((D_MODEL, tf), lambda i, j: (0, j)),
            pl.BlockSpec((tf, D_MODEL), lambda i, j: (j, 0)),
        ],
        out_specs=[
            pl.BlockSpec((tm, D_MODEL), lambda i, j: (i, 0)),
            pl.BlockSpec((tm, tf), lambda i, j: (i, j)),
            pl.BlockSpec((tm, tf), lambda i, j: (i, j)),
            pl.BlockSpec((D_MODEL, tm), lambda i, j: (0, i)),
            pl.BlockSpec((tf, tm), lambda i, j: (j, i)),
        ],
        out_shape=[
            jax.ShapeDtypeStruct((t, D_MODEL), F32),
            jax.ShapeDtypeStruct((t, D_FF), BF16),
            jax.ShapeDtypeStruct((t, D_FF), BF16),
            jax.ShapeDtypeStruct((D_MODEL, t), BF16),
            jax.ShapeDtypeStruct((D_FF, t), BF16),
        ],
        scratch_shapes=[pltpu.VMEM((tm, D_MODEL), BF16), pltpu.VMEM((tm, D_MODEL), F32)],
        compiler_params=_params("parallel", "arbitrary"),
    )(x, gain, wg, wu, wd)


def _ffn_bwd(dout, x, gain, a, b, wg, wu, wd, *, name, tm=512, tf=256):
    t = x.shape[0]
    nj = D_FF // tf

    def body(do_ref, x_ref, g_ref, a_ref, b_ref, wg_ref, wu_ref, wd_ref, dx_ref, dg_ref, da_ref, db_ref, dob_ref,
             dob_scr, dh):
        i = pl.program_id(0)
        j = pl.program_id(1)

        @pl.when(j == 0)
        def _():
            d = (0.5 * do_ref[...]).astype(BF16)
            dob_scr[...] = d
            dob_ref[...] = d
            dh[...] = jnp.zeros_like(dh)

        ds = _dot_nt(dob_scr[...], wd_ref[...])
        av = a_ref[...].astype(F32)
        bv = b_ref[...].astype(F32)
        sig = _sigmoid(av)
        dbv = (ds * (av * sig)).astype(BF16)
        dav = (ds * bv * (sig * (1.0 + av * (1.0 - sig)))).astype(BF16)
        da_ref[...] = dav
        db_ref[...] = dbv
        dh[...] += _dot_nt(dav, wg_ref[...]) + _dot_nt(dbv, wu_ref[...])

        @pl.when(j == nj - 1)
        def _():
            xhat, rstd = _rms(x_ref[...])
            dhv = dh[...]
            part = jnp.sum(dhv * xhat, axis=0, keepdims=True)

            @pl.when(i == 0)
            def _():
                dg_ref[...] = part

            @pl.when(i > 0)
            def _():
                dg_ref[...] += part

            dxh = dhv * g_ref[...]
            dx_ref[...] = do_ref[...] + rstd * (dxh - xhat * jnp.mean(dxh * xhat, axis=-1, keepdims=True))

    return pl.pallas_call(
        body,
        name=name,
        grid=(t // tm, nj),
        in_specs=[
            pl.BlockSpec((tm, D_MODEL), lambda i, j: (i, 0)),
            pl.BlockSpec((tm, D_MODEL), lambda i, j: (i, 0)),
            pl.BlockSpec((1, D_MODEL), lambda i, j: (0, 0)),
            pl.BlockSpec((tm, tf), lambda i, j: (i, j)),
            pl.BlockSpec((tm, tf), lambda i, j: (i, j)),
            pl.BlockSpec((D_MODEL, tf), lambda i, j: (0, j)),
            pl.BlockSpec((D_MODEL, tf), lambda i, j: (0, j)),
            pl.BlockSpec((tf, D_MODEL), lambda i, j: (j, 0)),
        ],
        out_specs=[
            pl.BlockSpec((tm, D_MODEL), lambda i, j: (i, 0)),
            pl.BlockSpec((1, D_MODEL), lambda i, j: (0, 0)),
            pl.BlockSpec((tm, tf), lambda i, j: (i, j)),
            pl.BlockSpec((tm, tf), lambda i, j: (i, j)),
            pl.BlockSpec((tm, D_MODEL), lambda i, j: (i, 0)),
        ],
        out_shape=[
            jax.ShapeDtypeStruct((t, D_MODEL), F32),
            jax.ShapeDtypeStruct((1, D_MODEL), F32),
            jax.ShapeDtypeStruct((t, D_FF), BF16),
            jax.ShapeDtypeStruct((t, D_FF), BF16),
            jax.ShapeDtypeStruct((t, D_MODEL), BF16),
        ],
        scratch_shapes=[pltpu.VMEM((tm, D_MODEL), BF16), pltpu.VMEM((tm, D_MODEL), F32)],
        compiler_params=_params("arbitrary", "arbitrary"),
    )(dout, x, gain, a, b, wg, wu, wd)


def _norm_fwd(x, gain, *, name, tm=512):
    t = x.shape[0]

    def body(x_ref, g_ref, h_ref, ht_ref):
        xhat, _ = _rms(x_ref[...])
        h = xhat * g_ref[...]
        h_ref[...] = h.astype(BF16)
        ht_ref[...] = h.T.astype(BF16)

    return pl.pallas_call(
        body,
        name=name,
        grid=(t // tm,),
        in_specs=[pl.BlockSpec((tm, D_MODEL), lambda i: (i, 0)), pl.BlockSpec((1, D_MODEL), lambda i: (0, 0))],
        out_specs=[pl.BlockSpec((tm, D_MODEL), lambda i: (i, 0)), pl.BlockSpec((D_MODEL, tm), lambda i: (0, i))],
        out_shape=[jax.ShapeDtypeStruct((t, D_MODEL), BF16), jax.ShapeDtypeStruct((D_MODEL, t), BF16)],
        compiler_params=_params("parallel"),
    )(x, gain)


def _norm_bwd(dh, x, gain, dres, *, name, tm=512):
    t = x.shape[0]

    def body(dh_ref, x_ref, g_ref, dr_ref, dx_ref, dg_ref):
        i = pl.program_id(0)
        xhat, rstd = _rms(x_ref[...])
        dhv = dh_ref[...]
        part = jnp.sum(dhv * xhat, axis=0, keepdims=True)

        @pl.when(i == 0)
        def _():
            dg_ref[...] = part

        @pl.when(i > 0)
        def _():
            dg_ref[...] += part

        dxh = dhv * g_ref[...]
        dx_ref[...] = dr_ref[...] + rstd * (dxh - xhat * jnp.mean(dxh * xhat, axis=-1, keepdims=True))

    row = pl.BlockSpec((tm, D_MODEL), lambda i: (i, 0))
    vec = pl.BlockSpec((1, D_MODEL), lambda i: (0, 0))
    return pl.pallas_call(
        body,
        name=name,
        grid=(t // tm,),
        in_specs=[row, row, vec, row],
        out_specs=[row, vec],
        out_shape=[jax.ShapeDtypeStruct((t, D_MODEL), F32), jax.ShapeDtypeStruct((1, D_MODEL), F32)],
        compiler_params=_params("arbitrary"),
    )(dh, x, gain, dres)


ATT_Q_TILE = 512
ATT_K_BLOCK = 256


def _first_head_lanes():
    return lax.broadcasted_iota(jnp.int32, (1, LANES), 1) < SB_HEAD_DIM


def _stack_heads(x):
    first = _first_head_lanes()
    return jnp.concatenate([jnp.where(first, x, 0.0), jnp.where(first, 0.0, x)], axis=0)


def _unstack_heads(x, rows):
    return jnp.where(_first_head_lanes(), x[:rows], x[rows:])


def _tri(n, relation):
    r = lax.broadcasted_iota(jnp.int32, (n, n), 0)
    c = lax.broadcasted_iota(jnp.int32, (n, n), 1)
    return relation(r, c).astype(BF16)


def _scan_dot(x, tri3):
    return _dot(jnp.concatenate(_split3(x), axis=1), tri3)


def _log_terms(z):
    sp = jnp.log(1.0 + jnp.exp(-jnp.abs(z)))
    return jnp.minimum(z, 0.0) - sp, -jnp.maximum(z, 0.0) - sp


def _attn_fwd(proj, *, name):
    t = proj.shape[0]
    tq, tk = ATT_Q_TILE, ATT_K_BLOCK
    diag = tq // tk
    n_pairs = SB_WIDTH // LANES

    def body(q_ref, k_ref, v_ref, o_ref, l_ref):
        qi = pl.program_id(1)
        q = q_ref[...] * (SB_HEAD_DIM ** -0.5)
        qs = _stack_heads(q).astype(BF16)
        tri = _tri(tk, lambda j, s: j > s)
        tri3 = jnp.concatenate([tri, tri, tri], axis=0)
        trow = lax.broadcasted_iota(jnp.int32, (tq, tk), 0)
        scol = lax.broadcasted_iota(jnp.int32, (tq, tk), 1)

        def block(off, carry, causal):
            acc, c = carry
            z = _dot_nt(qs, k_ref[pl.ds(off, tk), :].astype(BF16))
            lbeta, lrest = _log_terms(z)
            if causal is not None:
                lrest = jnp.where(causal, lrest, 0.0)
            w = jnp.exp(lbeta + (_scan_dot(lrest, tri3) + c))
            if causal is not None:
                w = jnp.where(causal, w, 0.0)
            acc = acc + _dot(w.astype(BF16), v_ref[pl.ds(off, tk), :].astype(BF16))
            return acc, c + jnp.sum(lrest, axis=1, keepdims=True)

        carry = (jnp.zeros((2 * tq, LANES), F32), jnp.zeros((2 * tq, 1), F32))
        for j in reversed(range(diag)):
            off = pl.multiple_of(qi * tq + j * tk, tk)
            mask = (scol + j * tk) < trow
            carry = block(off, carry, jnp.concatenate([mask, mask], axis=0))
        n_full = qi * diag

        def step(it, carry):
            return block(pl.multiple_of((n_full - 1 - it) * tk, tk), carry, None)

        acc, c = lax.fori_loop(0, n_full, step, carry)
        o_ref[...] = _unstack_heads(acc, tq)
        l_ref[...] = _unstack_heads(jnp.broadcast_to(c, (2 * tq, LANES)), tq)

    return pl.pallas_call(
        body,
        name=name,
        grid=(n_pairs, t // tq),
        in_specs=[
            pl.BlockSpec((tq, LANES), lambda p, i: (i, p)),
            pl.BlockSpec((t, LANES), lambda p, i: (0, n_pairs + p)),
            pl.BlockSpec((t, LANES), lambda p, i: (0, 2 * n_pairs + p)),
        ],
        out_specs=[pl.BlockSpec((tq, LANES), lambda p, i: (i, p))] * 2,
        out_shape=[jax.ShapeDtypeStruct((t, SB_WIDTH), F32)] * 2,
        compiler_params=_params("parallel", "parallel"),
    )(proj, proj, proj)


def _attn_bwd(proj, ltot, do, *, name):
    t = proj.shape[0]
    tq, tk = ATT_Q_TILE, ATT_K_BLOCK
    diag = tq // tk
    n_pairs = SB_WIDTH // LANES
    scale = SB_HEAD_DIM ** -0.5

    def body(q_ref, k_ref, v_ref, l_ref, do_ref, dq_ref, dk_ref, dv_ref):
        qi = pl.program_id(1)

        @pl.when(qi == 0)
        def _():
            dk_ref[...] = jnp.zeros_like(dk_ref)
            dv_ref[...] = jnp.zeros_like(dv_ref)

        q = q_ref[...] * scale
        lt = l_ref[...]
        qs = _stack_heads(q).astype(BF16)
        dos = _stack_heads(do_ref[...]).astype(BF16)
        first = _first_head_lanes()
        total = jnp.concatenate([jnp.max(jnp.where(first, lt, -jnp.inf), axis=1, keepdims=True),
                                 jnp.max(jnp.where(first, -jnp.inf, lt), axis=1, keepdims=True)], axis=0)
        upto = _tri(tk, lambda j, s: j <= s)
        upto3 = jnp.concatenate([upto, upto, upto], axis=0)
        before = _tri(tk, lambda s, j: s < j)
        before3 = jnp.concatenate([before, before, before], axis=0)
        trow = lax.broadcasted_iota(jnp.int32, (tq, tk), 0)
        scol = lax.broadcasted_iota(jnp.int32, (tq, tk), 1)

        def block(off, carry, causal):
            dq, cl, cg = carry
            kblk = k_ref[pl.ds(off, tk), :].astype(BF16)
            vblk = v_ref[pl.ds(off, tk), :].astype(BF16)
            z = _dot_nt(qs, kblk)
            lbeta, lrest = _log_terms(z)
            if causal is not None:
                lrest = jnp.where(causal, lrest, 0.0)
            w = jnp.exp(lbeta + (total - (_scan_dot(lrest, upto3) + cl)))
            if causal is not None:
                w = jnp.where(causal, w, 0.0)
            g = w * _dot_nt(dos, vblk)
            prior = _scan_dot(g, before3) + cg
            sig = jnp.exp(lbeta)
            dz = g * (1.0 - sig) - prior * sig
            if causal is not None:
                dz = jnp.where(causal, dz, 0.0)
            dq = dq + _dot(dz.astype(BF16), kblk)
            dk_ref[pl.ds(off, tk), :] += _dot(dz.T.astype(BF16), qs)
            dv_ref[pl.ds(off, tk), :] += _dot(w.T.astype(BF16), dos)
            return dq, cl + jnp.sum(lrest, axis=1, keepdims=True), cg + jnp.sum(g, axis=1, keepdims=True)

        def step(kb, carry):
            return block(pl.multiple_of(kb * tk, tk), carry, None)

        zero = jnp.zeros((2 * tq, 1), F32)
        carry = lax.fori_loop(0, qi * diag, step, (jnp.zeros((2 * tq, LANES), F32), zero, zero))
        for j in range(diag):
            off = pl.multiple_of(qi * tq + j * tk, tk)
            mask = (scol + j * tk) < trow
            carry = block(off, carry, jnp.concatenate([mask, mask], axis=0))
        dq_ref[...] = _unstack_heads(carry[0], tq) * scale

    tile_spec = pl.BlockSpec((tq, LANES), lambda p, i: (i, p))
    full_spec = pl.BlockSpec((t, LANES), lambda p, i: (0, p))
    return pl.pallas_call(
        body,
        name=name,
        grid=(n_pairs, t // tq),
        in_specs=[
            tile_spec,
            pl.BlockSpec((t, LANES), lambda p, i: (0, n_pairs + p)),
            pl.BlockSpec((t, LANES), lambda p, i: (0, 2 * n_pairs + p)),
            tile_spec,
            tile_spec,
        ],
        out_specs=[tile_spec, full_spec, full_spec],
        out_shape=[jax.ShapeDtypeStruct((t, SB_WIDTH), F32)] * 3,
        compiler_params=_params("arbitrary", "arbitrary"),
    )(proj, proj, proj, ltot, do)


HG_BLOCK = 256
HG_HEADS = HG_WIDTH // HG_HEAD_DIM


def _chunk_mats(n):
    r = lax.broadcasted_iota(jnp.int32, (n, n), 0)
    c = lax.broadcasted_iota(jnp.int32, (n, n), 1)
    same = (r // HG_CHUNK) == (c // HG_CHUNK)
    upto = (same & (c <= r)).astype(BF16)
    whole = same.astype(BF16)
    onward = (same & (c >= r)).astype(BF16)
    return upto, whole, onward


def _rows_dot(mat, x):
    return _dot(jnp.concatenate([mat, mat, mat], axis=1), jnp.concatenate(_split3(x), axis=0))


def _lower_bound(lg_ref):
    lg = lg_ref[...]
    return _sigmoid(lg[0:1, :] - lg[1:2, :])


def _hgrn_prepare(q_ref, f_ref, lb, h, upto, whole):
    cols = slice(h * HG_HEAD_DIM, (h + 1) * HG_HEAD_DIM)
    lbh = lb[:, cols]
    sg = _sigmoid(f_ref[:, cols])
    forget = lbh + (1.0 - lbh) * sg
    logf = jnp.log(forget)
    kk = (1.0 - lbh) * (1.0 - sg)
    qv = q_ref[:, cols]
    qsig = _sigmoid(qv)
    qh = qv * qsig
    b = _rows_dot(upto, logf)
    blast = _rows_dot(whole, logf)
    return dict(lbh=lbh, sg=sg, forget=forget, kk=kk, qv=qv, qsig=qsig, qh=qh, b=b, eb=jnp.exp(b),
                ekb=jnp.exp(blast - b), dl=jnp.exp(blast))


def _hgrn_fwd(proj, logits, *, name):
    t = proj.shape[0]
    tb = HG_BLOCK
    nc = tb // HG_CHUNK
    hd = HG_HEAD_DIM

    def body(q_ref, f_ref, i_ref, lg_ref, o_ref, st_ref, state, qh_s, kk_s, b_s, qe_s, ke_s, dl_s):
        @pl.when(pl.program_id(0) == 0)
        def _():
            state[...] = jnp.zeros_like(state)

        lb = _lower_bound(lg_ref)
        upto, whole, _ = _chunk_mats(tb)
        for h in range(HG_HEADS):
            p = _hgrn_prepare(q_ref, f_ref, lb, h, upto, whole)
            qh_s[h] = p["qh"]
            kk_s[h] = p["kk"]
            b_s[h] = p["b"]
            qe_s[h] = (p["qh"] * p["eb"]).astype(BF16)
            ke_s[h] = (p["kk"] * p["ekb"]).astype(BF16)
            dl_s[h] = p["dl"]
        rowi = lax.broadcasted_iota(jnp.int32, (HG_CHUNK, hd), 0)

        def chunk(c, _):
            r0 = pl.multiple_of(c * HG_CHUNK, HG_CHUNK)
            rows = pl.ds(r0, HG_CHUNK)
            for h in range(HG_HEADS):
                cols = slice(h * hd, (h + 1) * hd)
                bc = b_s[h, rows, :]
                qc = qh_s[h, rows, :]
                kc = kk_s[h, rows, :]
                vc = i_ref[rows, cols]
                s_in = state[h]
                st_ref[c, h] = s_in
                o = _dot_nt(qe_s[h, rows, :], s_in.astype(BF16))
                for s in range(HG_CHUNK):
                    pair = jnp.where(rowi >= s, qc * jnp.exp(bc - bc[s:s + 1, :]) * kc[s:s + 1, :], 0.0)
                    o = o + jnp.sum(pair, axis=1, keepdims=True) * vc[s:s + 1, :]
                o_ref[rows, cols] = o
                state[h] = s_in * dl_s[h, pl.ds(r0, 1), :] + _dot_tn(vc.astype(BF16), ke_s[h, rows, :])
            return 0

        lax.fori_loop(0, nc, chunk, 0)

    blk = lambda col: pl.BlockSpec((tb, HG_WIDTH), lambda i: (i, col))
    head_f32 = pltpu.VMEM((HG_HEADS, tb, hd), F32)
    head_bf16 = pltpu.VMEM((HG_HEADS, tb, hd), BF16)
    return pl.pallas_call(
        body,
        name=name,
        grid=(t // tb,),
        in_specs=[blk(3), blk(4), blk(5), pl.BlockSpec((2, HG_WIDTH), lambda i: (0, 0))],
        out_specs=[
            pl.BlockSpec((tb, HG_WIDTH), lambda i: (i, 0)),
            pl.BlockSpec((nc, HG_HEADS, hd, hd), lambda i: (i, 0, 0, 0)),
        ],
        out_shape=[
            jax.ShapeDtypeStruct((t, HG_WIDTH), F32),
            jax.ShapeDtypeStruct((t // HG_CHUNK, HG_HEADS, hd, hd), F32),
        ],
        scratch_shapes=[pltpu.VMEM((HG_HEADS, hd, hd), F32), head_f32, head_f32, head_f32, head_bf16, head_bf16,
                        head_f32],
        compiler_params=_params("arbitrary"),
    )(proj, proj, proj, logits)


def _hgrn_bwd(proj, logits, states, do, *, name):
    t = proj.shape[0]
    tb = HG_BLOCK
    nb = t // tb
    nc = tb // HG_CHUNK
    hd = HG_HEAD_DIM

    def body(q_ref, f_ref, i_ref, lg_ref, st_ref, do_ref, dq_ref, df_ref, di_ref, dlb_ref,
             dstate, qh_s, kk_s, b_s, eb_s, ekb_s, qe_s, ke_s, dl_s, dqh_s, dkk_s, dlf_s):
        step = pl.program_id(0)

        @pl.when(step == 0)
        def _():
            dstate[...] = jnp.zeros_like(dstate)
            dlb_ref[...] = jnp.zeros_like(dlb_ref)

        lb = _lower_bound(lg_ref)
        upto, whole, _ = _chunk_mats(tb)
        prepared = []
        for h in range(HG_HEADS):
            p = _hgrn_prepare(q_ref, f_ref, lb, h, upto, whole)
            prepared.append(p)
            qh_s[h] = p["qh"]
            kk_s[h] = p["kk"]
            b_s[h] = p["b"]
            eb_s[h] = p["eb"]
            ekb_s[h] = p["ekb"]
            qe_s[h] = (p["qh"] * p["eb"]).astype(BF16)
            ke_s[h] = (p["kk"] * p["ekb"]).astype(BF16)
            dl_s[h] = p["dl"]
        rowi = lax.broadcasted_iota(jnp.int32, (HG_CHUNK, hd), 0)
        r16 = lax.broadcasted_iota(jnp.int32, (HG_CHUNK, HG_CHUNK), 0)
        c16 = lax.broadcasted_iota(jnp.int32, (HG_CHUNK, HG_CHUNK), 1)
        onward = (c16 >= r16).astype(BF16)

        def chunk(it, _):
            c = nc - 1 - it
            r0 = pl.multiple_of(c * HG_CHUNK, HG_CHUNK)
            rows = pl.ds(r0, HG_CHUNK)
            for h in range(HG_HEADS):
                cols = slice(h * hd, (h + 1) * hd)
                bc = b_s[h, rows, :]
                qc = qh_s[h, rows, :]
                kc = kk_s[h, rows, :]
                vc = i_ref[rows, cols]
                doc = do_ref[rows, cols]
                s_in = st_ref[c, h]
                ds_out = dstate[h]
                ds_out_b = ds_out.astype(BF16)
                docb = doc.astype(BF16)
                dl_row = dl_s[h, pl.ds(r0, 1), :]
                dqh = _dot(docb, s_in.astype(BF16)) * eb_s[h, rows, :]
                dkk = _dot(vc.astype(BF16), ds_out_b) * ekb_s[h, rows, :]
                dv = _dot_nt(ke_s[h, rows, :], ds_out_b)
                db = dqh * qc - dkk * kc
                dwhole = jnp.sum(dkk * kc, axis=0, keepdims=True) + jnp.sum(ds_out * s_in, axis=0, keepdims=True) * dl_row
                for s in range(HG_CHUNK):
                    keep = rowi >= s
                    at_s = rowi == s
                    e = jnp.exp(bc - bc[s:s + 1, :])
                    k_row = kc[s:s + 1, :]
                    pcol = jnp.sum(jnp.where(keep, qc * e * k_row, 0.0), axis=1, keepdims=True)
                    dpcol = jnp.sum(doc * vc[s:s + 1, :], axis=1, keepdims=True)
                    m = jnp.where(keep, e * dpcol, 0.0)
                    mk = m * k_row
                    dk_row = jnp.sum(m * qc, axis=0, keepdims=True)
                    dqh = dqh + mk
                    dkk = dkk + jnp.where(at_s, dk_row, 0.0)
                    db = db + mk * qc - jnp.where(at_s, dk_row * k_row, 0.0)
                    dv = dv + jnp.where(at_s, jnp.sum(pcol * doc, axis=0, keepdims=True), 0.0)
                dqh_s[h, rows, :] = dqh
                dkk_s[h, rows, :] = dkk
                dlf_s[h, rows, :] = _rows_dot(onward, db) + dwhole
                di_ref[rows, cols] = dv
                dstate[h] = ds_out * dl_row + _dot_tn(docb, qe_s[h, rows, :])
            return 0

        lax.fori_loop(0, nc, chunk, 0)
        for h in range(HG_HEADS):
            cols = slice(h * hd, (h + 1) * hd)
            p = prepared[h]
            dq_ref[:, cols] = dqh_s[h] * (p["qsig"] * (1.0 + p["qv"] * (1.0 - p["qsig"])))
            dforget = dlf_s[h] / p["forget"] - dkk_s[h]
            df_ref[:, cols] = dforget * (1.0 - p["lbh"]) * p["sg"] * (1.0 - p["sg"])
            dlb_ref[:, cols] += jnp.sum(dforget * (1.0 - p["sg"]), axis=0, keepdims=True)

    blk = lambda col: pl.BlockSpec((tb, HG_WIDTH), lambda i: (nb - 1 - i, col))
    vec = pl.BlockSpec((1, HG_WIDTH), lambda i: (0, 0))
    head_f32 = pltpu.VMEM((HG_HEADS, tb, hd), F32)
    head_bf16 = pltpu.VMEM((HG_HEADS, tb, hd), BF16)
    return pl.pallas_call(
        body,
        name=name,
        grid=(nb,),
        in_specs=[
            blk(3), blk(4), blk(5),
            pl.BlockSpec((2, HG_WIDTH), lambda i: (0, 0)),
            pl.BlockSpec((nc, HG_HEADS, hd, hd), lambda i: (nb - 1 - i, 0, 0, 0)),
            blk(0),
        ],
        out_specs=[blk(0), blk(0), blk(0), vec],
        out_shape=[jax.ShapeDtypeStruct((t, HG_WIDTH), F32)] * 3 + [jax.ShapeDtypeStruct((1, HG_WIDTH), F32)],
        scratch_shapes=[
            pltpu.VMEM((HG_HEADS, hd, hd), F32),
            head_f32, head_f32, head_f32, head_f32, head_f32, head_bf16, head_bf16, head_f32,
            head_f32, head_f32, head_f32,
        ],
        compiler_params=_params("arbitrary"),
    )(proj, proj, proj, logits, states, do)


def _group_mat(width, head_dim):
    r = lax.broadcasted_iota(jnp.int32, (width, width), 0)
    c = lax.broadcasted_iota(jnp.int32, (width, width), 1)
    return ((r // head_dim) == (c // head_dim)).astype(BF16)


def _head_mean(x, mat, head_dim):
    hi = x.astype(BF16)
    lo = (x - hi.astype(F32)).astype(BF16)
    return (_dot(hi, mat) + _dot(lo, mat)) * (1.0 / head_dim)


def _mix_out_fwd(o_sb, o_hg, proj, g_sb, g_hg, w_out, x1, *, name, tm=256):
    t = x1.shape[0]

    def body(osb_ref, ohg_ref, gate_ref, gsb_ref, ghg_ref, w_ref, x_ref, xo_ref, mt_ref):
        msb = _group_mat(SB_WIDTH, SB_HEAD_DIM)
        mhg = _group_mat(HG_WIDTH, HG_HEAD_DIM)
        osb = osb_ref[...]
        ohg = ohg_ref[...]
        nsb = osb * lax.rsqrt(_head_mean(osb * osb, msb, SB_HEAD_DIM) + EPS) * gsb_ref[...]
        gate = gate_ref[...]
        nhg = ohg * lax.rsqrt(_head_mean(ohg * ohg, mhg, HG_HEAD_DIM) + EPS) * ghg_ref[...] * (gate * _sigmoid(gate))
        mixed = jnp.concatenate([nsb, nhg], axis=1)
        mt_ref[...] = mixed.T.astype(BF16)
        xo_ref[...] = x_ref[...] + _dot(mixed.astype(BF16), w_ref[...])

    half = pl.BlockSpec((tm, SB_WIDTH), lambda i: (i, 0))
    vec = pl.BlockSpec((1, SB_WIDTH), lambda i: (0, 0))
    row = pl.BlockSpec((tm, D_MODEL), lambda i: (i, 0))
    return pl.pallas_call(
        body,
        name=name,
        grid=(t // tm,),
        in_specs=[half, half, pl.BlockSpec((tm, HG_WIDTH), lambda i: (i, 6)), vec, vec,
                  pl.BlockSpec((D_MODEL, D_MODEL), lambda i: (0, 0)), row],
        out_specs=[row, pl.BlockSpec((D_MODEL, tm), lambda i: (0, i))],
        out_shape=[jax.ShapeDtypeStruct((t, D_MODEL), F32), jax.ShapeDtypeStruct((D_MODEL, t), BF16)],
        compiler_params=_params("parallel"),
    )(o_sb, o_hg, proj, g_sb, g_hg, w_out, x1)


def _mix_out_bwd(dx2, o_sb, o_hg, proj, g_sb, g_hg, w_out, *, name, tm=256):
    t = dx2.shape[0]

    def body(dx_ref, osb_ref, ohg_ref, gate_ref, gsb_ref, ghg_ref, w_ref, dosb_ref, dohg_ref, dgate_ref, dgsb_ref,
             dghg_ref, dxb_ref):
        i = pl.program_id(0)
        msb = _group_mat(SB_WIDTH, SB_HEAD_DIM)
        mhg = _group_mat(HG_WIDTH, HG_HEAD_DIM)
        dxb = dx_ref[...].astype(BF16)
        dxb_ref[...] = dxb
        dmixed = _dot_nt(dxb, w_ref[...])
        dnsb = dmixed[:, :SB_WIDTH]
        dy = dmixed[:, SB_WIDTH:]

        osb = osb_ref[...]
        rstd = lax.rsqrt(_head_mean(osb * osb, msb, SB_HEAD_DIM) + EPS)
        ohat = osb * rstd
        part_sb = jnp.sum(dnsb * ohat, axis=0, keepdims=True)
        dohat = dnsb * gsb_ref[...]
        dosb_ref[...] = rstd * (dohat - ohat * _head_mean(dohat * ohat, msb, SB_HEAD_DIM))

        ohg = ohg_ref[...]
        rstd = lax.rsqrt(_head_mean(ohg * ohg, mhg, HG_HEAD_DIM) + EPS)
        ohat = ohg * rstd
        gate = gate_ref[...]
        sig = _sigmoid(gate)
        dn = dy * (gate * sig)
        dgate_ref[...] = dy * (ohat * ghg_ref[...]) * (sig * (1.0 + gate * (1.0 - sig)))
        part_hg = jnp.sum(dn * ohat, axis=0, keepdims=True)
        dohat = dn * ghg_ref[...]
        dohg_ref[...] = rstd * (dohat - ohat * _head_mean(dohat * ohat, mhg, HG_HEAD_DIM))

        @pl.when(i == 0)
        def _():
            dgsb_ref[...] = part_sb
            dghg_ref[...] = part_hg

        @pl.when(i > 0)
        def _():
            dgsb_ref[...] += part_sb
            dghg_ref[...] += part_hg

    half = pl.BlockSpec((tm, SB_WIDTH), lambda i: (i, 0))
    vec = pl.BlockSpec((1, SB_WIDTH), lambda i: (0, 0))
    row = pl.BlockSpec((tm, D_MODEL), lambda i: (i, 0))
    return pl.pallas_call(
        body,
        name=name,
        grid=(t // tm,),
        in_specs=[row, half, half, pl.BlockSpec((tm, HG_WIDTH), lambda i: (i, 6)), vec, vec,
                  pl.BlockSpec((D_MODEL, D_MODEL), lambda i: (0, 0))],
        out_specs=[half, half, half, vec, vec, row],
        out_shape=[jax.ShapeDtypeStruct((t, SB_WIDTH), F32)] * 3 + [jax.ShapeDtypeStruct((1, SB_WIDTH), F32)] * 2
        + [jax.ShapeDtypeStruct((t, D_MODEL), BF16)],
        compiler_params=_params("arbitrary"),
    )(dx2, o_sb, o_hg, proj, g_sb, g_hg, w_out)


def _loss_head(x3, gain, target, *, name, tm=512):
    t = x3.shape[0]

    def body(x_ref, g_ref, y_ref, dx_ref, dg_ref, loss_ref):
        i = pl.program_id(0)
        xhat, rstd = _rms(x_ref[...])
        err = xhat * g_ref[...] - y_ref[...]
        part_loss = 0.5 * jnp.sum(jnp.mean(err * err, axis=-1, keepdims=True), axis=0, keepdims=True)
        dy = err * (1.0 / D_MODEL)
        part_g = jnp.sum(dy * xhat, axis=0, keepdims=True)

        @pl.when(i == 0)
        def _():
            dg_ref[...] = part_g
            loss_ref[...] = jnp.broadcast_to(part_loss, loss_ref.shape)

        @pl.when(i > 0)
        def _():
            dg_ref[...] += part_g
            loss_ref[...] += jnp.broadcast_to(part_loss, loss_ref.shape)

        dxh = dy * g_ref[...]
        dx_ref[...] = rstd * (dxh - xhat * jnp.mean(dxh * xhat, axis=-1, keepdims=True))

    row = pl.BlockSpec((tm, D_MODEL), lambda i: (i, 0))
    vec = pl.BlockSpec((1, D_MODEL), lambda i: (0, 0))
    return pl.pallas_call(
        body,
        name=name,
        grid=(t // tm,),
        in_specs=[row, vec, row],
        out_specs=[row, vec, vec],
        out_shape=[jax.ShapeDtypeStruct((t, D_MODEL), F32), jax.ShapeDtypeStruct((1, D_MODEL), F32),
                   jax.ShapeDtypeStruct((1, D_MODEL), F32)],
        compiler_params=_params("arbitrary"),
    )(x3, gain, target)


def _local_step(x, target, norms, logits, w):
    x1, a1, b1, h1t, s1t = _ffn_fwd(x, norms["ffn1"], w["g1"], w["u1"], w["d1"], name="ffn1_fwd")
    hm, hmt = _norm_fwd(x1, norms["mix"], name="mix_norm_fwd")
    proj = _mm(hm, w["in"], name="in_proj", tm=512, tn=512)
    o_sb, ltot = _attn_fwd(proj, name="sb_attn_fwd")
    o_hg, states = _hgrn_fwd(proj, logits, name="hgrn2_fwd")
    x2, mixed_t = _mix_out_fwd(o_sb, o_hg, proj, norms["sb"], norms["hg"], w["out"], x1, name="mix_out_fwd")
    x3, a2, b2, h2t, s2t = _ffn_fwd(x2, norms["ffn2"], w["g2"], w["u2"], w["d2"], name="ffn2_fwd")
    dx3, d_final, loss_row = _loss_head(x3, norms["final"], target, name="loss_head")

    gw, gv = {}, {"final": d_final}
    dx2, gv["ffn2"], da2, db2, dob2 = _ffn_bwd(dx3, x2, norms["ffn2"], a2, b2, w["g2"], w["u2"], w["d2"], name="ffn2_bwd")
    gw["g2"] = _mm(h2t, da2, name="ffn2_dgate", tm=D_MODEL, tn=256)
    gw["u2"] = _mm(h2t, db2, name="ffn2_dup", tm=D_MODEL, tn=256)
    gw["d2"] = _mm(s2t, dob2, name="ffn2_ddown", tm=256, tn=D_MODEL)

    do_sb, do_hg, d_gate, gv["sb"], gv["hg"], dx2b = _mix_out_bwd(
        dx2, o_sb, o_hg, proj, norms["sb"], norms["hg"], w["out"], name="mix_out_bwd")
    gw["out"] = _mm(mixed_t, dx2b, name="out_dw", tm=D_MODEL, tn=256)
    dq_sb, dk_sb, dv_sb = _attn_bwd(proj, ltot, do_sb, name="sb_attn_bwd")
    dq_hg, df_hg, di_hg, d_lb = _hgrn_bwd(proj, logits, states, do_hg, name="hgrn2_bwd")
    dproj = jnp.concatenate([dq_sb, dk_sb, dv_sb, dq_hg, df_hg, di_hg, d_gate], axis=1).astype(BF16)
    gw["in"] = _mm(hmt, dproj, name="in_dw", tm=D_MODEL, tn=256)
    dhm = _mm(dproj, w["in"], name="in_dx", tm=512, tn=D_MODEL, nt=True)
    dx1, gv["mix"] = _norm_bwd(dhm, x1, norms["mix"], dx2, name="mix_norm_bwd")

    dx, gv["ffn1"], da1, db1, dob1 = _ffn_bwd(dx1, x, norms["ffn1"], a1, b1, w["g1"], w["u1"], w["d1"], name="ffn1_bwd")
    gw["g1"] = _mm(h1t, da1, name="ffn1_dgate", tm=D_MODEL, tn=256)
    gw["u1"] = _mm(h1t, db1, name="ffn1_dup", tm=D_MODEL, tn=256)
    gw["d1"] = _mm(s1t, dob1, name="ffn1_ddown", tm=256, tn=D_MODEL)
    gv["lb"] = d_lb
    return loss_row, dx, gw, gv


HBM = pl.BlockSpec(memory_space=pl.ANY)


def _place():
    return lax.axis_index("x"), lax.axis_index("y"), lax.axis_index("c")


def _slot(px, py, pc):
    return 4 * px + 2 * py + pc


def _all_gather(blocks, *, name):
    n = len(blocks)

    def body(*refs):
        ins, outs = refs[:n], refs[n:2 * n]
        send_sems, recv_sems, local_sems = refs[2 * n:]
        x, y, c = _place()
        me, sibling = (x, y, c), (x, y, 1 - c)
        chips = [(1 - x, y), (x, 1 - y), (1 - x, 1 - y)]

        def copy(a, k, block, to, src=None):
            dst = outs[a].at[_slot(*block)]
            return pltpu.make_async_remote_copy(
                src_ref=dst if src is None else src, dst_ref=dst, send_sem=send_sems.at[7 * a + k],
                recv_sem=recv_sems.at[7 * a + k], device_id=to, device_id_type=MESH)

        mine = [pltpu.make_async_copy(ins[a], outs[a].at[_slot(*me)], local_sems.at[a]) for a in range(n)]
        for cp in mine:
            cp.start()
        first = []
        for a in range(n):
            first.append(copy(a, 0, me, sibling, src=ins[a]))
            first += [copy(a, 1 + j, me, (*chip, c), src=ins[a]) for j, chip in enumerate(chips)]
        for cp in first:
            cp.start()
        passed = []
        for j, chip in enumerate(chips):
            for a in range(n):
                copy(a, 1 + j, (*chip, c), me).wait_recv()
                fwd = copy(a, 4 + j, (*chip, c), sibling)
                fwd.start()
                passed.append(fwd)
        for a in range(n):
            copy(a, 0, sibling, me).wait_recv()
            for j, chip in enumerate(chips):
                copy(a, 4 + j, (*chip, 1 - c), me).wait_recv()
        for cp in first + passed:
            cp.wait_send()
        for cp in mine:
            cp.wait()

    return pl.pallas_call(
        body,
        name=name,
        in_specs=[HBM] * n,
        out_specs=[HBM] * n,
        out_shape=[jax.ShapeDtypeStruct((N_DEV,) + b.shape, b.dtype) for b in blocks],
        scratch_shapes=[pltpu.SemaphoreType.DMA((7 * n,)), pltpu.SemaphoreType.DMA((7 * n,)),
                        pltpu.SemaphoreType.DMA((n,))],
    )(*blocks)


def _flipped(place, d):
    return tuple(1 - p if (d >> (2 - axis)) & 1 else p for axis, p in enumerate(place))


def _scatter_by_owner(stacks, *, name):
    n = len(stacks)

    def body(*refs):
        ins, outs = refs[:n], refs[n:2 * n]
        send_sems, recv_sems, local_sems = refs[2 * n:]
        me = _place()
        mine = [pltpu.make_async_copy(ins[a].at[_slot(*me)], outs[a].at[_slot(*me)], local_sems.at[a]) for a in range(n)]
        for cp in mine:
            cp.start()
        copies = []
        for d in range(1, N_DEV):
            peer = _flipped(me, d)
            for a in range(n):
                copies.append(pltpu.make_async_remote_copy(
                    src_ref=ins[a].at[_slot(*peer)], dst_ref=outs[a].at[_slot(*me)], send_sem=send_sems.at[7 * a + d - 1],
                    recv_sem=recv_sems.at[7 * a + d - 1], device_id=peer, device_id_type=MESH))
        for cp in copies:
            cp.start()
        for cp in copies:
            cp.wait()
        for cp in mine:
            cp.wait()

    return pl.pallas_call(
        body,
        name=name,
        in_specs=[HBM] * n,
        out_specs=[HBM] * n,
        out_shape=[jax.ShapeDtypeStruct(s.shape, s.dtype) for s in stacks],
        scratch_shapes=[pltpu.SemaphoreType.DMA((7 * n,)), pltpu.SemaphoreType.DMA((7 * n,)),
                        pltpu.SemaphoreType.DMA((n,))],
    )(*stacks)


def _adamw(w, g, m, v):
    m = ADAM_B1 * m + (1.0 - ADAM_B1) * g
    v = ADAM_B2 * v + (1.0 - ADAM_B2) * (g * g)
    m_hat = m / (1.0 - ADAM_B1 ** ADAM_STEP)
    v_hat = v / (1.0 - ADAM_B2 ** ADAM_STEP)
    delta = -ADAM_LR * (m_hat / (jnp.sqrt(v_hat) + ADAM_EPS) + ADAM_WD * w)
    return delta, m, v


def _sum_and_update(parts, w, m, v, *, name, tr):
    rows, cols = w.shape

    def body(p_ref, w_ref, m_ref, v_ref, g_ref, d_ref, mo_ref, vo_ref):
        g = p_ref[0].astype(F32)
        for s in range(1, N_DEV):
            g = g + p_ref[s].astype(F32)
        g_ref[...] = g
        d_ref[...], mo_ref[...], vo_ref[...] = _adamw(w_ref[...], g, m_ref[...], v_ref[...])

    flat = pl.BlockSpec((tr, cols), lambda i: (i, 0))
    return pl.pallas_call(
        body,
        name=name,
        grid=(rows // tr,),
        in_specs=[pl.BlockSpec((N_DEV, tr, cols), lambda i: (0, i, 0)), flat, flat, flat],
        out_specs=[flat] * 4,
        out_shape=[jax.ShapeDtypeStruct((rows, cols), F32)] * 4,
        compiler_params=_params("parallel"),
    )(parts, w, m, v)


VEC_ROWS = 8
ROW_LOGITS, ROW_LOSS = 5, 7


def _vectors_update(part, w, m, v, *, name):
    def body(p_ref, w_ref, m_ref, v_ref, g_ref, d_ref, mo_ref, vo_ref, loss_ref, all_ref, send_sems, recv_sems):
        me = _place()
        all_ref[_slot(*me)] = p_ref[...]
        copies = []
        for d in range(1, N_DEV):
            peer = _flipped(me, d)
            copies.append(pltpu.make_async_remote_copy(
                src_ref=p_ref, dst_ref=all_ref.at[_slot(*me)], send_sem=send_sems.at[d - 1], recv_sem=recv_sems.at[d - 1],
                device_id=peer, device_id_type=MESH))
        for cp in copies:
            cp.start()
        for cp in copies:
            cp.wait()
        total = all_ref[0]
        for s in range(1, N_DEV):
            total = total + all_ref[s]
        wv = w_ref[...]
        half = D_MODEL // 2
        lb = _sigmoid(wv[ROW_LOGITS:ROW_LOGITS + 1, :half] - wv[ROW_LOGITS:ROW_LOGITS + 1, half:])
        d_first = total[ROW_LOGITS:ROW_LOGITS + 1, :half] * lb * (1.0 - lb)
        d_logits = jnp.concatenate([d_first, -d_first], axis=1)
        rowi = lax.broadcasted_iota(jnp.int32, (VEC_ROWS, D_MODEL), 0)
        g = jnp.where(rowi == ROW_LOGITS, d_logits, jnp.where(rowi < ROW_LOGITS, total, 0.0))
        g_ref[...] = g
        d_ref[...], mo_ref[...], vo_ref[...] = _adamw(wv, g, m_ref[...], v_ref[...])
        loss_ref[...] = total[ROW_LOSS:ROW_LOSS + 1, :]

    vmem = pl.BlockSpec(memory_space=pltpu.VMEM)
    return pl.pallas_call(
        body,
        name=name,
        in_specs=[vmem] * 4,
        out_specs=[vmem] * 5,
        out_shape=[jax.ShapeDtypeStruct((VEC_ROWS, D_MODEL), F32)] * 4 + [jax.ShapeDtypeStruct((1, D_MODEL), F32)],
        scratch_shapes=[pltpu.VMEM((N_DEV, VEC_ROWS, D_MODEL), F32), pltpu.SemaphoreType.DMA((7,)),
                        pltpu.SemaphoreType.DMA((7,))],
    )(part, w, m, v)


COL_KEYS = ("g1", "u1", "in", "g2", "u2")
COL_WIDTHS = (FF_SHARD, FF_SHARD, IN_SHARD, FF_SHARD, FF_SHARD)
ROW_KEYS = ("d1", "out", "d2")
ROW_HEIGHTS = (FF_SHARD, OUT_SHARD, FF_SHARD)


def _offsets(sizes):
    out, at = [], 0
    for s in sizes:
        out.append((at, at + s))
        at += s
    return out


def _pack_shards(shards):
    cols = jnp.concatenate([shards[k] for k in COL_KEYS], axis=1)
    rows = jnp.concatenate([shards[k] for k in ROW_KEYS], axis=0)
    return cols, rows


def _unpack_shards(cols, rows):
    out = {k: cols[:, lo:hi] for k, (lo, hi) in zip(COL_KEYS, _offsets(COL_WIDTHS))}
    out.update({k: rows[lo:hi, :] for k, (lo, hi) in zip(ROW_KEYS, _offsets(ROW_HEIGHTS))})
    return out


def _full_from_stacks(cols, rows):
    out = {}
    for k, (lo, hi) in zip(COL_KEYS, _offsets(COL_WIDTHS)):
        out[k] = cols[:, :, lo:hi].transpose(1, 0, 2).reshape(D_MODEL, N_DEV * (hi - lo))
    for k, (lo, hi) in zip(ROW_KEYS, _offsets(ROW_HEIGHTS)):
        out[k] = rows[:, lo:hi, :].reshape(N_DEV * (hi - lo), D_MODEL)
    return out


def _stacks_from_full(full):
    cols = jnp.concatenate(
        [full[k].reshape(D_MODEL, N_DEV, wd).transpose(1, 0, 2) for k, wd in zip(COL_KEYS, COL_WIDTHS)], axis=2)
    rows = jnp.concatenate([full[k].reshape(N_DEV, ht, D_MODEL) for k, ht in zip(ROW_KEYS, ROW_HEIGHTS)], axis=1)
    return cols.astype(BF16), rows.astype(BF16)


def _vector_rows(rows):
    rowi = lax.broadcasted_iota(jnp.int32, (VEC_ROWS, D_MODEL), 0)
    out = jnp.zeros((VEC_ROWS, D_MODEL), F32)
    for i, r in enumerate(rows):
        if r is not None:
            out = jnp.where(rowi == i, r, out)
    return out


def kernel(x, ffn1_norm, ffn1_w_gate, ffn1_w_up, ffn1_w_down, mix_norm, w_in, sb_out_norm, hg_lower_bound_logits, hg_out_norm, w_out, ffn2_norm, ffn2_w_gate, ffn2_w_up, ffn2_w_down, final_norm, loss_target, m_ffn1_norm, m_ffn1_w_gate, m_ffn1_w_up, m_ffn1_w_down, m_mix_norm, m_w_in, m_sb_out_norm, m_hg_lower_bound_logits, m_hg_out_norm, m_w_out, m_ffn2_norm, m_ffn2_w_gate, m_ffn2_w_up, m_ffn2_w_down, m_final_norm, v_ffn1_norm, v_ffn1_w_gate, v_ffn1_w_up, v_ffn1_w_down, v_mix_norm, v_w_in, v_sb_out_norm, v_hg_lower_bound_logits, v_hg_out_norm, v_w_out, v_ffn2_norm, v_ffn2_w_gate, v_ffn2_w_up, v_ffn2_w_down, v_final_norm):
    def matrices(g1, u1, d1, win, wout, g2, u2, d2):
        return {"g1": g1[0], "u1": u1[0], "d1": d1[0], "in": win[0], "out": wout[0], "g2": g2[0], "u2": u2[0], "d2": d2[0]}

    def vectors(n1, nm, nsb, lg, nhg, n2, nf):
        return [n1, nm, n2, nf.reshape(1, D_MODEL), jnp.concatenate([nsb, nhg], axis=1), lg.reshape(1, D_MODEL), None, None]

    w_sh = matrices(ffn1_w_gate, ffn1_w_up, ffn1_w_down, w_in, w_out, ffn2_w_gate, ffn2_w_up, ffn2_w_down)
    m_sh = matrices(m_ffn1_w_gate, m_ffn1_w_up, m_ffn1_w_down, m_w_in, m_w_out, m_ffn2_w_gate, m_ffn2_w_up, m_ffn2_w_down)
    v_sh = matrices(v_ffn1_w_gate, v_ffn1_w_up, v_ffn1_w_down, v_w_in, v_w_out, v_ffn2_w_gate, v_ffn2_w_up, v_ffn2_w_down)

    w_cols, w_rows = _pack_shards(w_sh)
    all_cols, all_rows = _all_gather([w_cols.astype(BF16), w_rows.astype(BF16)], name="gather_weights")
    w_full = _full_from_stacks(all_cols, all_rows)

    norms = {"ffn1": ffn1_norm, "mix": mix_norm, "sb": sb_out_norm, "hg": hg_out_norm, "ffn2": ffn2_norm,
             "final": final_norm.reshape(1, D_MODEL)}
    loss_row, grad_x, gw, gv = _local_step(x[0], loss_target[0], norms, hg_lower_bound_logits, w_full)

    g_cols, g_rows = _scatter_by_owner(list(_stacks_from_full(gw)), name="exchange_grads")
    m_cols, m_rows = _pack_shards(m_sh)
    v_cols, v_rows = _pack_shards(v_sh)
    col_out = _sum_and_update(g_cols, w_cols, m_cols, v_cols, name="adamw_cols", tr=128)
    row_out = _sum_and_update(g_rows, w_rows, m_rows, v_rows, name="adamw_rows", tr=208)
    mats = [_unpack_shards(c, r) for c, r in zip(col_out, row_out)]

    lb_row = jnp.concatenate([gv["lb"], jnp.zeros_like(gv["lb"])], axis=1)
    part = _vector_rows([gv["ffn1"], gv["mix"], gv["ffn2"], gv["final"], jnp.concatenate([gv["sb"], gv["hg"]], axis=1),
                         lb_row, None, loss_row])
    vec_w = _vector_rows(vectors(ffn1_norm, mix_norm, sb_out_norm, hg_lower_bound_logits, hg_out_norm, ffn2_norm, final_norm))
    vec_m = _vector_rows(vectors(m_ffn1_norm, m_mix_norm, m_sb_out_norm, m_hg_lower_bound_logits, m_hg_out_norm,
                                 m_ffn2_norm, m_final_norm))
    vec_v = _vector_rows(vectors(v_ffn1_norm, v_mix_norm, v_sb_out_norm, v_hg_lower_bound_logits, v_hg_out_norm,
                                 v_ffn2_norm, v_final_norm))
    *vecs, loss_out = _vectors_update(part, vec_w, vec_m, vec_v, name="vectors_update")

    def leaves(mat, vec):
        half = D_MODEL // 2
        return (
            vec[0:1], mat["g1"][None], mat["u1"][None], mat["d1"][None], vec[1:2], mat["in"][None], vec[4:5, :half],
            vec[ROW_LOGITS].reshape(2, half), vec[4:5, half:], mat["out"][None], vec[2:3], mat["g2"][None], mat["u2"][None],
            mat["d2"][None], vec[3],
        )

    out = [loss_out[0, 0], grad_x[None]]
    for mat, vec in zip(mats, vecs):
        out.extend(leaves(mat, vec))
    return tuple(out)
```

```python
import jax
import jax.numpy as jnp
from jax import lax
from jax.experimental import pallas as pl
from jax.experimental.pallas import tpu as pltpu

F32, BF16 = jnp.float32, jnp.bfloat16
D_MODEL = 1024
D_FF = 2816
SB_WIDTH = 512
HG_WIDTH = 512
SB_HEAD_DIM = 64
HG_HEAD_DIM = 128
IN_COLS = 3584
EPS = 1e-6
N_DEV = 8
FF_SHARD = D_FF // N_DEV
IN_SHARD = IN_COLS // N_DEV
OUT_SHARD = D_MODEL // N_DEV
LANES = 128
HG_CHUNK = 16
VMEM_LIMIT_BYTES = 48 * 1024 * 1024
ADAM_LR, ADAM_B1, ADAM_B2, ADAM_EPS, ADAM_WD, ADAM_STEP = 0.001, 0.9, 0.999, 1e-08, 0.01, 10
MESH = pl.DeviceIdType.MESH


def _params(*semantics):
    return pltpu.CompilerParams(dimension_semantics=semantics, vmem_limit_bytes=VMEM_LIMIT_BYTES)


def _dot(a, b):
    return jnp.dot(a, b, preferred_element_type=F32)


def _dot_nt(a, b):
    return lax.dot_general(a, b, (((1,), (1,)), ((), ())), preferred_element_type=F32)


def _dot_tn(a, b):
    return lax.dot_general(a, b, (((0,), (0,)), ((), ())), preferred_element_type=F32)


def _split3(x):
    hi = x.astype(BF16)
    r1 = x - hi.astype(F32)
    mid = r1.astype(BF16)
    lo = (r1 - mid.astype(F32)).astype(BF16)
    return hi, mid, lo


def _rms(xv):
    rstd = lax.rsqrt(jnp.mean(xv * xv, axis=-1, keepdims=True) + EPS)
    return xv * rstd, rstd


def _sigmoid(x):
    return 1.0 / (1.0 + jnp.exp(-x))


def _mm(a, b, *, name, tm, tn, nt=False, scale=None, add=None, out_dtype=F32):
    m, k = a.shape
    n = b.shape[0] if nt else b.shape[1]
    assert m % tm == 0 and n % tn == 0, (name, a.shape, b.shape, tm, tn)

    def body(*refs):
        a_ref, b_ref = refs[0], refs[1]
        o_ref = refs[-1]
        av = a_ref[...].astype(BF16)
        bv = b_ref[...].astype(BF16)
        r = _dot_nt(av, bv) if nt else _dot(av, bv)
        if scale is not None:
            r = r * scale
        if add is not None:
            r = r + refs[2][...]
        o_ref[...] = r.astype(out_dtype)

    in_specs = [
        pl.BlockSpec((tm, k), lambda i, j: (i, 0)),
        pl.BlockSpec((tn, k), lambda i, j: (j, 0)) if nt else pl.BlockSpec((k, tn), lambda i, j: (0, j)),
    ]
    operands = [a, b]
    if add is not None:
        in_specs.append(pl.BlockSpec((tm, tn), lambda i, j: (i, j)))
        operands.append(add)
    return pl.pallas_call(
        body,
        name=name,
        grid=(m // tm, n // tn),
        in_specs=in_specs,
        out_specs=pl.BlockSpec((tm, tn), lambda i, j: (i, j)),
        out_shape=jax.ShapeDtypeStruct((m, n), out_dtype),
        compiler_params=_params("parallel", "parallel"),
    )(*operands)


def _ffn_fwd(x, gain, wgt, wut, wd, *, name, tm=512, tf=256):
    t = x.shape[0]
    nj = D_FF // tf

    def body(x_ref, g_ref, wg_ref, wu_ref, wd_ref, xo_ref, a_ref, b_ref, h_ref, st_ref, acc):
        j = pl.program_id(1)

        @pl.when(j == 0)
        def _():
            xhat, _ = _rms(x_ref[...])
            h_ref[...] = (xhat * g_ref[...]).astype(BF16)
            acc[...] = jnp.zeros_like(acc)

        h = h_ref[...]
        a = _dot_nt(h, wg_ref[...])
        b = _dot_nt(h, wu_ref[...])
        a_ref[...] = a.astype(BF16)
        b_ref[...] = b.astype(BF16)
        s = a * _sigmoid(a) * b
        st_ref[...] = s.T.astype(BF16)
        acc[...] += _dot(s.astype(BF16), wd_ref[...])

        @pl.when(j == nj - 1)
        def _():
            xo_ref[...] = x_ref[...] + 0.5 * acc[...]

    return pl.pallas_call(
        body,
        name=name,
        grid=(t // tm, nj),
        in_specs=[
            pl.BlockSpec((tm, D_MODEL), lambda i, j: (i, 0)),
            pl.BlockSpec((1, D_MODEL), lambda i, j: (0, 0)),
            pl.BlockSpec((tf, D_MODEL), lambda i, j: (j, 0)),
            pl.BlockSpec((tf, D_MODEL), lambda i, j: (j, 0)),
            pl.BlockSpec((tf, D_MODEL), lambda i, j: (j, 0)),
        ],
        out_specs=[
            pl.BlockSpec((tm, D_MODEL), lambda i, j: (i, 0)),
            pl.BlockSpec((tm, tf), lambda i, j: (i, j)),
            pl.BlockSpec((tm, tf), lambda i, j: (i, j)),
            pl.BlockSpec((tm, D_MODEL), lambda i, j: (i, 0)),
            pl.BlockSpec((tf, tm), lambda i, j: (j, i)),
        ],
        out_shape=[
            jax.ShapeDtypeStruct((t, D_MODEL), F32),
            jax.ShapeDtypeStruct((t, D_FF), BF16),
            jax.ShapeDtypeStruct((t, D_FF), BF16),
            jax.ShapeDtypeStruct((t, D_MODEL), BF16),
            jax.ShapeDtypeStruct((D_FF, t), BF16),
        ],
        scratch_shapes=[pltpu.VMEM((tm, D_MODEL), F32)],
        compiler_params=_params("parallel", "arbitrary"),
    )(x, gain, wgt, wut, wd)


def _ffn_bwd(dout, x, gain, a, b, wgt, wut, wd, *, name, tm=512, tf=256):
    t = x.shape[0]
    nj = D_FF // tf

    def body(do_ref, x_ref, g_ref, a_ref, b_ref, wg_ref, wu_ref, wd_ref, dx_ref, dg_ref, da_ref, db_ref, dob_ref,
             dob_scr, dh):
        i = pl.program_id(0)
        j = pl.program_id(1)

        @pl.when(j == 0)
        def _():
            d = (0.5 * do_ref[...]).astype(BF16)
            dob_scr[...] = d
            dob_ref[...] = d
            dh[...] = jnp.zeros_like(dh)

        ds = _dot_nt(dob_scr[...], wd_ref[...])
        av = a_ref[...].astype(F32)
        bv = b_ref[...].astype(F32)
        sig = _sigmoid(av)
        dbv = ds * (av * sig)
        dav = ds * bv * (sig * (1.0 + av * (1.0 - sig)))
        da_ref[...] = dav.T.astype(BF16)
        db_ref[...] = dbv.T.astype(BF16)
        dh[...] += _dot(dav.astype(BF16), wg_ref[...]) + _dot(dbv.astype(BF16), wu_ref[...])

        @pl.when(j == nj - 1)
        def _():
            xhat, rstd = _rms(x_ref[...])
            dhv = dh[...]
            part = jnp.sum(dhv * xhat, axis=0, keepdims=True)

            @pl.when(i == 0)
            def _():
                dg_ref[...] = part

            @pl.when(i > 0)
            def _():
                dg_ref[...] += part

            dxh = dhv * g_ref[...]
            dx_ref[...] = do_ref[...] + rstd * (dxh - xhat * jnp.mean(dxh * xhat, axis=-1, keepdims=True))

    return pl.pallas_call(
        body,
        name=name,
        grid=(t // tm, nj),
        in_specs=[
            pl.BlockSpec((tm, D_MODEL), lambda i, j: (i, 0)),
            pl.BlockSpec((tm, D_MODEL), lambda i, j: (i, 0)),
            pl.BlockSpec((1, D_MODEL), lambda i, j: (0, 0)),
            pl.BlockSpec((tm, tf), lambda i, j: (i, j)),
            pl.BlockSpec((tm, tf), lambda i, j: (i, j)),
            pl.BlockSpec((tf, D_MODEL), lambda i, j: (j, 0)),
            pl.BlockSpec((tf, D_MODEL), lambda i, j: (j, 0)),
            pl.BlockSpec((tf, D_MODEL), lambda i, j: (j, 0)),
        ],
        out_specs=[
            pl.BlockSpec((tm, D_MODEL), lambda i, j: (i, 0)),
            pl.BlockSpec((1, D_MODEL), lambda i, j: (0, 0)),
            pl.BlockSpec((tf, tm), lambda i, j: (j, i)),
            pl.BlockSpec((tf, tm), lambda i, j: (j, i)),
            pl.BlockSpec((tm, D_MODEL), lambda i, j: (i, 0)),
        ],
        out_shape=[
            jax.ShapeDtypeStruct((t, D_MODEL), F32),
            jax.ShapeDtypeStruct((1, D_MODEL), F32),
            jax.ShapeDtypeStruct((D_FF, t), BF16),
            jax.ShapeDtypeStruct((D_FF, t), BF16),
            jax.ShapeDtypeStruct((t, D_MODEL), BF16),
        ],
        scratch_shapes=[pltpu.VMEM((tm, D_MODEL), BF16), pltpu.VMEM((tm, D_MODEL), F32)],
        compiler_params=_params("arbitrary", "arbitrary"),
    )(dout, x, gain, a, b, wgt, wut, wd)


def _norm_fwd(x, gain, *, name, tm=512):
    t = x.shape[0]

    def body(x_ref, g_ref, h_ref, ht_ref):
        xhat, _ = _rms(x_ref[...])
        h = xhat * g_ref[...]
        h_ref[...] = h.astype(BF16)
        ht_ref[...] = h.T.astype(BF16)

    return pl.pallas_call(
        body,
        name=name,
        grid=(t // tm,),
        in_specs=[pl.BlockSpec((tm, D_MODEL), lambda i: (i, 0)), pl.BlockSpec((1, D_MODEL), lambda i: (0, 0))],
        out_specs=[pl.BlockSpec((tm, D_MODEL), lambda i: (i, 0)), pl.BlockSpec((D_MODEL, tm), lambda i: (0, i))],
        out_shape=[jax.ShapeDtypeStruct((t, D_MODEL), BF16), jax.ShapeDtypeStruct((D_MODEL, t), BF16)],
        compiler_params=_params("parallel"),
    )(x, gain)


def _norm_bwd(dh, x, gain, dres, *, name, tm=512):
    t = x.shape[0]

    def body(dh_ref, x_ref, g_ref, dr_ref, dx_ref, dg_ref):
        i = pl.program_id(0)
        xhat, rstd = _rms(x_ref[...])
        dhv = dh_ref[...]
        part = jnp.sum(dhv * xhat, axis=0, keepdims=True)

        @pl.when(i == 0)
        def _():
            dg_ref[...] = part

        @pl.when(i > 0)
        def _():
            dg_ref[...] += part

        dxh = dhv * g_ref[...]
        dx_ref[...] = dr_ref[...] + rstd * (dxh - xhat * jnp.mean(dxh * xhat, axis=-1, keepdims=True))

    row = pl.BlockSpec((tm, D_MODEL), lambda i: (i, 0))
    vec = pl.BlockSpec((1, D_MODEL), lambda i: (0, 0))
    return pl.pallas_call(
        body,
        name=name,
        grid=(t // tm,),
        in_specs=[row, row, vec, row],
        out_specs=[row, vec],
        out_shape=[jax.ShapeDtypeStruct((t, D_MODEL), F32), jax.ShapeDtypeStruct((1, D_MODEL), F32)],
        compiler_params=_params("arbitrary"),
    )(dh, x, gain, dres)


ATT_Q_TILE = 512
ATT_K_BLOCK = 256


def _first_head_lanes():
    return lax.broadcasted_iota(jnp.int32, (1, LANES), 1) < SB_HEAD_DIM


def _stack_heads(x):
    first = _first_head_lanes()
    return jnp.concatenate([jnp.where(first, x, 0.0), jnp.where(first, 0.0, x)], axis=0)


def _unstack_heads(x, rows):
    return jnp.where(_first_head_lanes(), x[:rows], x[rows:])


def _tri(n, relation):
    r = lax.broadcasted_iota(jnp.int32, (n, n), 0)
    c = lax.broadcasted_iota(jnp.int32, (n, n), 1)
    return relation(r, c).astype(BF16)


def _scan_dot(x, tri3):
    return _dot(jnp.concatenate(_split3(x), axis=1), tri3)


def _log_terms(z):
    sp = jnp.log(1.0 + jnp.exp(-jnp.abs(z)))
    return jnp.minimum(z, 0.0) - sp, -jnp.maximum(z, 0.0) - sp


def _attn_fwd(proj, *, name):
    t = proj.shape[0]
    tq, tk = ATT_Q_TILE, ATT_K_BLOCK
    diag = tq // tk
    n_pairs = SB_WIDTH // LANES

    def body(q_ref, k_ref, v_ref, o_ref, l_ref):
        qi = pl.program_id(1)
        q = q_ref[...] * (SB_HEAD_DIM ** -0.5)
        qs = _stack_heads(q).astype(BF16)
        tri = _tri(tk, lambda j, s: j > s)
        tri3 = jnp.concatenate([tri, tri, tri], axis=0)
        trow = lax.broadcasted_iota(jnp.int32, (tq, tk), 0)
        scol = lax.broadcasted_iota(jnp.int32, (tq, tk), 1)

        def block(off, carry, causal):
            acc, c = carry
            z = _dot_nt(qs, k_ref[pl.ds(off, tk), :].astype(BF16))
            lbeta, lrest = _log_terms(z)
            if causal is not None:
                lrest = jnp.where(causal, lrest, 0.0)
            w = jnp.exp(lbeta + (_scan_dot(lrest, tri3) + c))
            if causal is not None:
                w = jnp.where(causal, w, 0.0)
            acc = acc + _dot(w.astype(BF16), v_ref[pl.ds(off, tk), :].astype(BF16))
            return acc, c + jnp.sum(lrest, axis=1, keepdims=True)

        carry = (jnp.zeros((2 * tq, LANES), F32), jnp.zeros((2 * tq, 1), F32))
        for j in reversed(range(diag)):
            off = pl.multiple_of(qi * tq + j * tk, tk)
            mask = (scol + j * tk) < trow
            carry = block(off, carry, jnp.concatenate([mask, mask], axis=0))
        n_full = qi * diag

        def step(it, carry):
            return block(pl.multiple_of((n_full - 1 - it) * tk, tk), carry, None)

        acc, c = lax.fori_loop(0, n_full, step, carry)
        o_ref[...] = _unstack_heads(acc, tq)
        l_ref[...] = _unstack_heads(jnp.broadcast_to(c, (2 * tq, LANES)), tq)

    return pl.pallas_call(
        body,
        name=name,
        grid=(n_pairs, t // tq),
        in_specs=[
            pl.BlockSpec((tq, LANES), lambda p, i: (i, p)),
            pl.BlockSpec((t, LANES), lambda p, i: (0, n_pairs + p)),
            pl.BlockSpec((t, LANES), lambda p, i: (0, 2 * n_pairs + p)),
        ],
        out_specs=[pl.BlockSpec((tq, LANES), lambda p, i: (i, p))] * 2,
        out_shape=[jax.ShapeDtypeStruct((t, SB_WIDTH), F32)] * 2,
        compiler_params=_params("parallel", "parallel"),
    )(proj, proj, proj)


def _attn_bwd(proj, ltot, do, *, name):
    t = proj.shape[0]
    tq, tk = ATT_Q_TILE, ATT_K_BLOCK
    diag = tq // tk
    n_pairs = SB_WIDTH // LANES
    scale = SB_HEAD_DIM ** -0.5

    def body(q_ref, k_ref, v_ref, l_ref, do_ref, dq_ref, dk_ref, dv_ref):
        qi = pl.program_id(1)

        @pl.when(qi == 0)
        def _():
            dk_ref[...] = jnp.zeros_like(dk_ref)
            dv_ref[...] = jnp.zeros_like(dv_ref)

        q = q_ref[...] * scale
        lt = l_ref[...]
        qs = _stack_heads(q).astype(BF16)
        dos = _stack_heads(do_ref[...]).astype(BF16)
        first = _first_head_lanes()
        total = jnp.concatenate([jnp.max(jnp.where(first, lt, -jnp.inf), axis=1, keepdims=True),
                                 jnp.max(jnp.where(first, -jnp.inf, lt), axis=1, keepdims=True)], axis=0)
        upto = _tri(tk, lambda j, s: j <= s)
        upto3 = jnp.concatenate([upto, upto, upto], axis=0)
        before = _tri(tk, lambda s, j: s < j)
        before3 = jnp.concatenate([before, before, before], axis=0)
        trow = lax.broadcasted_iota(jnp.int32, (tq, tk), 0)
        scol = lax.broadcasted_iota(jnp.int32, (tq, tk), 1)

        def block(off, carry, causal):
            dq, cl, cg = carry
            kblk = k_ref[pl.ds(off, tk), :].astype(BF16)
            vblk = v_ref[pl.ds(off, tk), :].astype(BF16)
            z = _dot_nt(qs, kblk)
            lbeta, lrest = _log_terms(z)
            if causal is not None:
                lrest = jnp.where(causal, lrest, 0.0)
            w = jnp.exp(lbeta + (total - (_scan_dot(lrest, upto3) + cl)))
            if causal is not None:
                w = jnp.where(causal, w, 0.0)
            g = w * _dot_nt(dos, vblk)
            prior = _scan_dot(g, before3) + cg
            sig = jnp.exp(lbeta)
            dz = g * (1.0 - sig) - prior * sig
            if causal is not None:
                dz = jnp.where(causal, dz, 0.0)
            dq = dq + _dot(dz.astype(BF16), kblk)
            dk_ref[pl.ds(off, tk), :] += _dot(dz.T.astype(BF16), qs)
            dv_ref[pl.ds(off, tk), :] += _dot(w.T.astype(BF16), dos)
            return dq, cl + jnp.sum(lrest, axis=1, keepdims=True), cg + jnp.sum(g, axis=1, keepdims=True)

        def step(kb, carry):
            return block(pl.multiple_of(kb * tk, tk), carry, None)

        zero = jnp.zeros((2 * tq, 1), F32)
        carry = lax.fori_loop(0, qi * diag, step, (jnp.zeros((2 * tq, LANES), F32), zero, zero))
        for j in range(diag):
            off = pl.multiple_of(qi * tq + j * tk, tk)
            mask = (scol + j * tk) < trow
            carry = block(off, carry, jnp.concatenate([mask, mask], axis=0))
        dq_ref[...] = _unstack_heads(carry[0], tq) * scale

    tile_spec = pl.BlockSpec((tq, LANES), lambda p, i: (i, p))
    full_spec = pl.BlockSpec((t, LANES), lambda p, i: (0, p))
    return pl.pallas_call(
        body,
        name=name,
        grid=(n_pairs, t // tq),
        in_specs=[
            tile_spec,
            pl.BlockSpec((t, LANES), lambda p, i: (0, n_pairs + p)),
            pl.BlockSpec((t, LANES), lambda p, i: (0, 2 * n_pairs + p)),
            tile_spec,
            tile_spec,
        ],
        out_specs=[tile_spec, full_spec, full_spec],
        out_shape=[jax.ShapeDtypeStruct((t, SB_WIDTH), F32)] * 3,
        compiler_params=_params("arbitrary", "arbitrary"),
    )(proj, proj, proj, ltot, do)


HG_BLOCK = 256
HG_HEADS = HG_WIDTH // HG_HEAD_DIM


def _chunk_mats(n):
    r = lax.broadcasted_iota(jnp.int32, (n, n), 0)
    c = lax.broadcasted_iota(jnp.int32, (n, n), 1)
    same = (r // HG_CHUNK) == (c // HG_CHUNK)
    upto = (same & (c <= r)).astype(BF16)
    whole = same.astype(BF16)
    onward = (same & (c >= r)).astype(BF16)
    return upto, whole, onward


def _rows_dot(mat, x):
    return _dot(jnp.concatenate([mat, mat, mat], axis=1), jnp.concatenate(_split3(x), axis=0))


def _lower_bound(lg_ref):
    lg = lg_ref[...]
    return _sigmoid(lg[0:1, :] - lg[1:2, :])


def _hgrn_prepare(q_ref, f_ref, lb, h, upto, whole):
    cols = slice(h * HG_HEAD_DIM, (h + 1) * HG_HEAD_DIM)
    lbh = lb[:, cols]
    sg = _sigmoid(f_ref[:, cols])
    forget = lbh + (1.0 - lbh) * sg
    logf = jnp.log(forget)
    kk = (1.0 - lbh) * (1.0 - sg)
    qv = q_ref[:, cols]
    qsig = _sigmoid(qv)
    qh = qv * qsig
    b = _rows_dot(upto, logf)
    blast = _rows_dot(whole, logf)
    return dict(lbh=lbh, sg=sg, forget=forget, kk=kk, qv=qv, qsig=qsig, qh=qh, b=b, eb=jnp.exp(b),
                ekb=jnp.exp(blast - b), dl=jnp.exp(blast))


def _hgrn_fwd(proj, logits, *, name):
    t = proj.shape[0]
    tb = HG_BLOCK
    nc = tb // HG_CHUNK
    hd = HG_HEAD_DIM

    def body(q_ref, f_ref, i_ref, lg_ref, o_ref, st_ref, state, qh_s, kk_s, b_s, qe_s, ke_s, dl_s):
        @pl.when(pl.program_id(0) == 0)
        def _():
            state[...] = jnp.zeros_like(state)

        lb = _lower_bound(lg_ref)
        upto, whole, _ = _chunk_mats(tb)
        for h in range(HG_HEADS):
            p = _hgrn_prepare(q_ref, f_ref, lb, h, upto, whole)
            qh_s[h] = p["qh"]
            kk_s[h] = p["kk"]
            b_s[h] = p["b"]
            qe_s[h] = (p["qh"] * p["eb"]).astype(BF16)
            ke_s[h] = (p["kk"] * p["ekb"]).astype(BF16)
            dl_s[h] = p["dl"]
        rowi = lax.broadcasted_iota(jnp.int32, (HG_CHUNK, hd), 0)

        def chunk(c, _):
            r0 = pl.multiple_of(c * HG_CHUNK, HG_CHUNK)
            rows = pl.ds(r0, HG_CHUNK)
            for h in range(HG_HEADS):
                cols = slice(h * hd, (h + 1) * hd)
                bc = b_s[h, rows, :]
                qc = qh_s[h, rows, :]
                kc = kk_s[h, rows, :]
                vc = i_ref[rows, cols]
                s_in = state[h]
                st_ref[c, h] = s_in
                o = _dot_nt(qe_s[h, rows, :], s_in.astype(BF16))
                for s in range(HG_CHUNK):
                    pair = jnp.where(rowi >= s, qc * jnp.exp(bc - bc[s:s + 1, :]) * kc[s:s + 1, :], 0.0)
                    o = o + jnp.sum(pair, axis=1, keepdims=True) * vc[s:s + 1, :]
                o_ref[rows, cols] = o
                state[h] = s_in * dl_s[h, pl.ds(r0, 1), :] + _dot_tn(vc.astype(BF16), ke_s[h, rows, :])
            return 0

        lax.fori_loop(0, nc, chunk, 0)

    blk = lambda col: pl.BlockSpec((tb, HG_WIDTH), lambda i: (i, col))
    head_f32 = pltpu.VMEM((HG_HEADS, tb, hd), F32)
    head_bf16 = pltpu.VMEM((HG_HEADS, tb, hd), BF16)
    return pl.pallas_call(
        body,
        name=name,
        grid=(t // tb,),
        in_specs=[blk(3), blk(4), blk(5), pl.BlockSpec((2, HG_WIDTH), lambda i: (0, 0))],
        out_specs=[
            pl.BlockSpec((tb, HG_WIDTH), lambda i: (i, 0)),
            pl.BlockSpec((nc, HG_HEADS, hd, hd), lambda i: (i, 0, 0, 0)),
        ],
        out_shape=[
            jax.ShapeDtypeStruct((t, HG_WIDTH), F32),
            jax.ShapeDtypeStruct((t // HG_CHUNK, HG_HEADS, hd, hd), F32),
        ],
        scratch_shapes=[pltpu.VMEM((HG_HEADS, hd, hd), F32), head_f32, head_f32, head_f32, head_bf16, head_bf16,
                        head_f32],
        compiler_params=_params("arbitrary"),
    )(proj, proj, proj, logits)


def _hgrn_bwd(proj, logits, states, do, *, name):
    t = proj.shape[0]
    tb = HG_BLOCK
    nb = t // tb
    nc = tb // HG_CHUNK
    hd = HG_HEAD_DIM

    def body(q_ref, f_ref, i_ref, lg_ref, st_ref, do_ref, dq_ref, df_ref, di_ref, dlb_ref,
             dstate, qh_s, kk_s, b_s, eb_s, ekb_s, qe_s, ke_s, dl_s, dqh_s, dkk_s, dlf_s):
        step = pl.program_id(0)

        @pl.when(step == 0)
        def _():
            dstate[...] = jnp.zeros_like(dstate)
            dlb_ref[...] = jnp.zeros_like(dlb_ref)

        lb = _lower_bound(lg_ref)
        upto, whole, _ = _chunk_mats(tb)
        prepared = []
        for h in range(HG_HEADS):
            p = _hgrn_prepare(q_ref, f_ref, lb, h, upto, whole)
            prepared.append(p)
            qh_s[h] = p["qh"]
            kk_s[h] = p["kk"]
            b_s[h] = p["b"]
            eb_s[h] = p["eb"]
            ekb_s[h] = p["ekb"]
            qe_s[h] = (p["qh"] * p["eb"]).astype(BF16)
            ke_s[h] = (p["kk"] * p["ekb"]).astype(BF16)
            dl_s[h] = p["dl"]
        rowi = lax.broadcasted_iota(jnp.int32, (HG_CHUNK, hd), 0)
        r16 = lax.broadcasted_iota(jnp.int32, (HG_CHUNK, HG_CHUNK), 0)
        c16 = lax.broadcasted_iota(jnp.int32, (HG_CHUNK, HG_CHUNK), 1)
        onward = (c16 >= r16).astype(BF16)

        def chunk(it, _):
            c = nc - 1 - it
            r0 = pl.multiple_of(c * HG_CHUNK, HG_CHUNK)
            rows = pl.ds(r0, HG_CHUNK)
            for h in range(HG_HEADS):
                cols = slice(h * hd, (h + 1) * hd)
                bc = b_s[h, rows, :]
                qc = qh_s[h, rows, :]
                kc = kk_s[h, rows, :]
                vc = i_ref[rows, cols]
                doc = do_ref[rows, cols]
                s_in = st_ref[c, h]
                ds_out = dstate[h]
                ds_out_b = ds_out.astype(BF16)
                docb = doc.astype(BF16)
                dl_row = dl_s[h, pl.ds(r0, 1), :]
                dqh = _dot(docb, s_in.astype(BF16)) * eb_s[h, rows, :]
                dkk = _dot(vc.astype(BF16), ds_out_b) * ekb_s[h, rows, :]
                dv = _dot_nt(ke_s[h, rows, :], ds_out_b)
                db = dqh * qc - dkk * kc
                dwhole = jnp.sum(dkk * kc, axis=0, keepdims=True) + jnp.sum(ds_out * s_in, axis=0, keepdims=True) * dl_row
                for s in range(HG_CHUNK):
                    keep = rowi >= s
                    at_s = rowi == s
                    e = jnp.exp(bc - bc[s:s + 1, :])
                    k_row = kc[s:s + 1, :]
                    pcol = jnp.sum(jnp.where(keep, qc * e * k_row, 0.0), axis=1, keepdims=True)
                    dpcol = jnp.sum(doc * vc[s:s + 1, :], axis=1, keepdims=True)
                    m = jnp.where(keep, e * dpcol, 0.0)
                    mk = m * k_row
                    dk_row = jnp.sum(m * qc, axis=0, keepdims=True)
                    dqh = dqh + mk
                    dkk = dkk + jnp.where(at_s, dk_row, 0.0)
                    db = db + mk * qc - jnp.where(at_s, dk_row * k_row, 0.0)
                    dv = dv + jnp.where(at_s, jnp.sum(pcol * doc, axis=0, keepdims=True), 0.0)
                dqh_s[h, rows, :] = dqh
                dkk_s[h, rows, :] = dkk
                dlf_s[h, rows, :] = _rows_dot(onward, db) + dwhole
                di_ref[rows, cols] = dv
                dstate[h] = ds_out * dl_row + _dot_tn(docb, qe_s[h, rows, :])
            return 0

        lax.fori_loop(0, nc, chunk, 0)
        for h in range(HG_HEADS):
            cols = slice(h * hd, (h + 1) * hd)
            p = prepared[h]
            dq_ref[:, cols] = dqh_s[h] * (p["qsig"] * (1.0 + p["qv"] * (1.0 - p["qsig"])))
            dforget = dlf_s[h] / p["forget"] - dkk_s[h]
            df_ref[:, cols] = dforget * (1.0 - p["lbh"]) * p["sg"] * (1.0 - p["sg"])
            dlb_ref[:, cols] += jnp.sum(dforget * (1.0 - p["sg"]), axis=0, keepdims=True)

    blk = lambda col: pl.BlockSpec((tb, HG_WIDTH), lambda i: (nb - 1 - i, col))
    vec = pl.BlockSpec((1, HG_WIDTH), lambda i: (0, 0))
    head_f32 = pltpu.VMEM((HG_HEADS, tb, hd), F32)
    head_bf16 = pltpu.VMEM((HG_HEADS, tb, hd), BF16)
    return pl.pallas_call(
        body,
        name=name,
        grid=(nb,),
        in_specs=[
            blk(3), blk(4), blk(5),
            pl.BlockSpec((2, HG_WIDTH), lambda i: (0, 0)),
            pl.BlockSpec((nc, HG_HEADS, hd, hd), lambda i: (nb - 1 - i, 0, 0, 0)),
            blk(0),
        ],
        out_specs=[blk(0), blk(0), blk(0), vec],
        out_shape=[jax.ShapeDtypeStruct((t, HG_WIDTH), F32)] * 3 + [jax.ShapeDtypeStruct((1, HG_WIDTH), F32)],
        scratch_shapes=[
            pltpu.VMEM((HG_HEADS, hd, hd), F32),
            head_f32, head_f32, head_f32, head_f32, head_f32, head_bf16, head_bf16, head_f32,
            head_f32, head_f32, head_f32,
        ],
        compiler_params=_params("arbitrary"),
    )(proj, proj, proj, logits, states, do)


def _group_mat(width, head_dim):
    r = lax.broadcasted_iota(jnp.int32, (width, width), 0)
    c = lax.broadcasted_iota(jnp.int32, (width, width), 1)
    return ((r // head_dim) == (c // head_dim)).astype(BF16)


def _head_mean(x, mat, head_dim):
    hi = x.astype(BF16)
    lo = (x - hi.astype(F32)).astype(BF16)
    return (_dot(hi, mat) + _dot(lo, mat)) * (1.0 / head_dim)


def _mix_out_fwd(o_sb, o_hg, proj, g_sb, g_hg, w_out, x1, *, name, tm=256):
    t = x1.shape[0]

    def body(osb_ref, ohg_ref, gate_ref, gsb_ref, ghg_ref, w_ref, x_ref, xo_ref, mt_ref):
        msb = _group_mat(SB_WIDTH, SB_HEAD_DIM)
        mhg = _group_mat(HG_WIDTH, HG_HEAD_DIM)
        osb = osb_ref[...]
        ohg = ohg_ref[...]
        nsb = osb * lax.rsqrt(_head_mean(osb * osb, msb, SB_HEAD_DIM) + EPS) * gsb_ref[...]
        gate = gate_ref[...]
        nhg = ohg * lax.rsqrt(_head_mean(ohg * ohg, mhg, HG_HEAD_DIM) + EPS) * ghg_ref[...] * (gate * _sigmoid(gate))
        mixed = jnp.concatenate([nsb, nhg], axis=1)
        mt_ref[...] = mixed.T.astype(BF16)
        xo_ref[...] = x_ref[...] + _dot(mixed.astype(BF16), w_ref[...])

    half = pl.BlockSpec((tm, SB_WIDTH), lambda i: (i, 0))
    vec = pl.BlockSpec((1, SB_WIDTH), lambda i: (0, 0))
    row = pl.BlockSpec((tm, D_MODEL), lambda i: (i, 0))
    return pl.pallas_call(
        body,
        name=name,
        grid=(t // tm,),
        in_specs=[half, half, pl.BlockSpec((tm, HG_WIDTH), lambda i: (i, 6)), vec, vec,
                  pl.BlockSpec((D_MODEL, D_MODEL), lambda i: (0, 0)), row],
        out_specs=[row, pl.BlockSpec((D_MODEL, tm), lambda i: (0, i))],
        out_shape=[jax.ShapeDtypeStruct((t, D_MODEL), F32), jax.ShapeDtypeStruct((D_MODEL, t), BF16)],
        compiler_params=_params("parallel"),
    )(o_sb, o_hg, proj, g_sb, g_hg, w_out, x1)


def _mix_out_bwd(dx2, o_sb, o_hg, proj, g_sb, g_hg, w_out, *, name, tm=256):
    t = dx2.shape[0]

    def body(dx_ref, osb_ref, ohg_ref, gate_ref, gsb_ref, ghg_ref, w_ref, dosb_ref, dohg_ref, dgate_ref, dgsb_ref,
             dghg_ref, dxb_ref):
        i = pl.program_id(0)
        msb = _group_mat(SB_WIDTH, SB_HEAD_DIM)
        mhg = _group_mat(HG_WIDTH, HG_HEAD_DIM)
        dxb = dx_ref[...].astype(BF16)
        dxb_ref[...] = dxb
        dmixed = _dot_nt(dxb, w_ref[...])
        dnsb = dmixed[:, :SB_WIDTH]
        dy = dmixed[:, SB_WIDTH:]

        osb = osb_ref[...]
        rstd = lax.rsqrt(_head_mean(osb * osb, msb, SB_HEAD_DIM) + EPS)
        ohat = osb * rstd
        part_sb = jnp.sum(dnsb * ohat, axis=0, keepdims=True)
        dohat = dnsb * gsb_ref[...]
        dosb_ref[...] = rstd * (dohat - ohat * _head_mean(dohat * ohat, msb, SB_HEAD_DIM))

        ohg = ohg_ref[...]
        rstd = lax.rsqrt(_head_mean(ohg * ohg, mhg, HG_HEAD_DIM) + EPS)
        ohat = ohg * rstd
        gate = gate_ref[...]
        sig = _sigmoid(gate)
        dn = dy * (gate * sig)
        dgate_ref[...] = dy * (ohat * ghg_ref[...]) * (sig * (1.0 + gate * (1.0 - sig)))
        part_hg = jnp.sum(dn * ohat, axis=0, keepdims=True)
        dohat = dn * ghg_ref[...]
        dohg_ref[...] = rstd * (dohat - ohat * _head_mean(dohat * ohat, mhg, HG_HEAD_DIM))

        @pl.when(i == 0)
        def _():
            dgsb_ref[...] = part_sb
            dghg_ref[...] = part_hg

        @pl.when(i > 0)
        def _():
            dgsb_ref[...] += part_sb
            dghg_ref[...] += part_hg

    half = pl.BlockSpec((tm, SB_WIDTH), lambda i: (i, 0))
    vec = pl.BlockSpec((1, SB_WIDTH), lambda i: (0, 0))
    row = pl.BlockSpec((tm, D_MODEL), lambda i: (i, 0))
    return pl.pallas_call(
        body,
        name=name,
        grid=(t // tm,),
        in_specs=[row, half, half, pl.BlockSpec((tm, HG_WIDTH), lambda i: (i, 6)), vec, vec,
                  pl.BlockSpec((D_MODEL, D_MODEL), lambda i: (0, 0))],
        out_specs=[half, half, half, vec, vec, row],
        out_shape=[jax.ShapeDtypeStruct((t, SB_WIDTH), F32)] * 3 + [jax.ShapeDtypeStruct((1, SB_WIDTH), F32)] * 2
        + [jax.ShapeDtypeStruct((t, D_MODEL), BF16)],
        compiler_params=_params("arbitrary"),
    )(dx2, o_sb, o_hg, proj, g_sb, g_hg, w_out)


def _loss_head(x3, gain, target, *, name, tm=512):
    t = x3.shape[0]

    def body(x_ref, g_ref, y_ref, dx_ref, dg_ref, loss_ref):
        i = pl.program_id(0)
        xhat, rstd = _rms(x_ref[...])
        err = xhat * g_ref[...] - y_ref[...]
        part_loss = 0.5 * jnp.sum(jnp.mean(err * err, axis=-1, keepdims=True), axis=0, keepdims=True)
        dy = err * (1.0 / D_MODEL)
        part_g = jnp.sum(dy * xhat, axis=0, keepdims=True)

        @pl.when(i == 0)
        def _():
            dg_ref[...] = part_g
            loss_ref[...] = jnp.broadcast_to(part_loss, loss_ref.shape)

        @pl.when(i > 0)
        def _():
            dg_ref[...] += part_g
            loss_ref[...] += jnp.broadcast_to(part_loss, loss_ref.shape)

        dxh = dy * g_ref[...]
        dx_ref[...] = rstd * (dxh - xhat * jnp.mean(dxh * xhat, axis=-1, keepdims=True))

    row = pl.BlockSpec((tm, D_MODEL), lambda i: (i, 0))
    vec = pl.BlockSpec((1, D_MODEL), lambda i: (0, 0))
    return pl.pallas_call(
        body,
        name=name,
        grid=(t // tm,),
        in_specs=[row, vec, row],
        out_specs=[row, vec, vec],
        out_shape=[jax.ShapeDtypeStruct((t, D_MODEL), F32), jax.ShapeDtypeStruct((1, D_MODEL), F32),
                   jax.ShapeDtypeStruct((1, D_MODEL), F32)],
        compiler_params=_params("arbitrary"),
    )(x3, gain, target)


def _local_step(x, target, norms, logits, w):
    x1, a1, b1, h1, s1t = _ffn_fwd(x, norms["ffn1"], w["g1t"], w["u1t"], w["d1"], name="ffn1_fwd")
    hm, hmt = _norm_fwd(x1, norms["mix"], name="mix_norm_fwd")
    proj = _mm(hm, w["in"], name="in_proj", tm=512, tn=512)
    o_sb, ltot = _attn_fwd(proj, name="sb_attn_fwd")
    o_hg, states = _hgrn_fwd(proj, logits, name="hgrn2_fwd")
    x2, mixed_t = _mix_out_fwd(o_sb, o_hg, proj, norms["sb"], norms["hg"], w["out"], x1, name="mix_out_fwd")
    x3, a2, b2, h2, s2t = _ffn_fwd(x2, norms["ffn2"], w["g2t"], w["u2t"], w["d2"], name="ffn2_fwd")
    dx3, d_final, loss_row = _loss_head(x3, norms["final"], target, name="loss_head")

    def weight_grad(lhs_t, rhs, name):
        return _mm(lhs_t, rhs, name=name, tm=256 if lhs_t.shape[0] % 256 == 0 else lhs_t.shape[0], tn=D_MODEL, out_dtype=BF16)

    gw, gv = {}, {"final": d_final}
    dx2, gv["ffn2"], da2t, db2t, dob2 = _ffn_bwd(dx3, x2, norms["ffn2"], a2, b2, w["g2t"], w["u2t"], w["d2"],
                                                 name="ffn2_bwd")
    gw["g2t"] = weight_grad(da2t, h2, "ffn2_dgate")
    gw["u2t"] = weight_grad(db2t, h2, "ffn2_dup")
    gw["d2"] = weight_grad(s2t, dob2, "ffn2_ddown")

    do_sb, do_hg, d_gate, gv["sb"], gv["hg"], dx2b = _mix_out_bwd(
        dx2, o_sb, o_hg, proj, norms["sb"], norms["hg"], w["out"], name="mix_out_bwd")
    gw["out"] = weight_grad(mixed_t, dx2b, "out_dw")
    dq_sb, dk_sb, dv_sb = _attn_bwd(proj, ltot, do_sb, name="sb_attn_bwd")
    dq_hg, df_hg, di_hg, d_lb = _hgrn_bwd(proj, logits, states, do_hg, name="hgrn2_bwd")
    dproj = jnp.concatenate([dq_sb, dk_sb, dv_sb, dq_hg, df_hg, di_hg, d_gate], axis=1).astype(BF16)
    gw["in"] = _mm(hmt, dproj, name="in_dw", tm=D_MODEL, tn=256)
    dhm = _mm(dproj, w["in"], name="in_dx", tm=512, tn=D_MODEL, nt=True)
    dx1, gv["mix"] = _norm_bwd(dhm, x1, norms["mix"], dx2, name="mix_norm_bwd")

    dx, gv["ffn1"], da1t, db1t, dob1 = _ffn_bwd(dx1, x, norms["ffn1"], a1, b1, w["g1t"], w["u1t"], w["d1"],
                                                name="ffn1_bwd")
    gw["g1t"] = weight_grad(da1t, h1, "ffn1_dgate")
    gw["u1t"] = weight_grad(db1t, h1, "ffn1_dup")
    gw["d1"] = weight_grad(s1t, dob1, "ffn1_ddown")
    gv["lb"] = d_lb
    return loss_row, dx, gw, gv


HBM = pl.BlockSpec(memory_space=pl.ANY)


def _place():
    return lax.axis_index("x"), lax.axis_index("y"), lax.axis_index("c")


def _slot(px, py, pc):
    return 4 * px + 2 * py + pc


def _all_gather(blocks, *, name):
    n = len(blocks)

    def body(*refs):
        ins, outs = refs[:n], refs[n:2 * n]
        send_sems, recv_sems, local_sems = refs[2 * n:]
        x, y, c = _place()
        me, sibling = (x, y, c), (x, y, 1 - c)
        chips = [(1 - x, y), (x, 1 - y), (1 - x, 1 - y)]

        def copy(a, k, block, to, src=None):
            dst = outs[a].at[_slot(*block)]
            return pltpu.make_async_remote_copy(
                src_ref=dst if src is None else src, dst_ref=dst, send_sem=send_sems.at[7 * a + k],
                recv_sem=recv_sems.at[7 * a + k], device_id=to, device_id_type=MESH)

        mine = [pltpu.make_async_copy(ins[a], outs[a].at[_slot(*me)], local_sems.at[a]) for a in range(n)]
        for cp in mine:
            cp.start()
        first = []
        for a in range(n):
            first.append(copy(a, 0, me, sibling, src=ins[a]))
            first += [copy(a, 1 + j, me, (*chip, c), src=ins[a]) for j, chip in enumerate(chips)]
        for cp in first:
            cp.start()
        passed = []
        for j, chip in enumerate(chips):
            for a in range(n):
                copy(a, 1 + j, (*chip, c), me).wait_recv()
                fwd = copy(a, 4 + j, (*chip, c), sibling)
                fwd.start()
                passed.append(fwd)
        for a in range(n):
            copy(a, 0, sibling, me).wait_recv()
            for j, chip in enumerate(chips):
                copy(a, 4 + j, (*chip, 1 - c), me).wait_recv()
        for cp in first + passed:
            cp.wait_send()
        for cp in mine:
            cp.wait()

    return pl.pallas_call(
        body,
        name=name,
        in_specs=[HBM] * n,
        out_specs=[HBM] * n,
        out_shape=[jax.ShapeDtypeStruct((N_DEV,) + b.shape, b.dtype) for b in blocks],
        scratch_shapes=[pltpu.SemaphoreType.DMA((7 * n,)), pltpu.SemaphoreType.DMA((7 * n,)),
                        pltpu.SemaphoreType.DMA((n,))],
    )(*blocks)


def _flipped(place, d):
    return tuple(1 - p if (d >> (2 - axis)) & 1 else p for axis, p in enumerate(place))


def _scatter_by_owner(stacks, *, name):
    n = len(stacks)

    def body(*refs):
        ins, outs = refs[:n], refs[n:2 * n]
        send_sems, recv_sems, local_sems = refs[2 * n:]
        me = _place()
        mine = [pltpu.make_async_copy(ins[a].at[_slot(*me)], outs[a].at[_slot(*me)], local_sems.at[a]) for a in range(n)]
        for cp in mine:
            cp.start()
        copies = []
        for d in range(1, N_DEV):
            peer = _flipped(me, d)
            for a in range(n):
                copies.append(pltpu.make_async_remote_copy(
                    src_ref=ins[a].at[_slot(*peer)], dst_ref=outs[a].at[_slot(*me)], send_sem=send_sems.at[7 * a + d - 1],
                    recv_sem=recv_sems.at[7 * a + d - 1], device_id=peer, device_id_type=MESH))
        for cp in copies:
            cp.start()
        for cp in copies:
            cp.wait()
        for cp in mine:
            cp.wait()

    return pl.pallas_call(
        body,
        name=name,
        in_specs=[HBM] * n,
        out_specs=[HBM] * n,
        out_shape=[jax.ShapeDtypeStruct(s.shape, s.dtype) for s in stacks],
        scratch_shapes=[pltpu.SemaphoreType.DMA((7 * n,)), pltpu.SemaphoreType.DMA((7 * n,)),
                        pltpu.SemaphoreType.DMA((n,))],
    )(*stacks)


def _adamw(w, g, m, v):
    m = ADAM_B1 * m + (1.0 - ADAM_B1) * g
    v = ADAM_B2 * v + (1.0 - ADAM_B2) * (g * g)
    m_hat = m / (1.0 - ADAM_B1 ** ADAM_STEP)
    v_hat = v / (1.0 - ADAM_B2 ** ADAM_STEP)
    delta = -ADAM_LR * (m_hat / (jnp.sqrt(v_hat) + ADAM_EPS) + ADAM_WD * w)
    return delta, m, v


def _sum_and_update(parts, w, m, v, *, name, tr, transposed=False):
    _, rows, cols = w.shape
    pad = -cols % LANES

    def body(p_ref, w_ref, m_ref, v_ref, g_ref, d_ref, mo_ref, vo_ref):
        g = p_ref[0].astype(F32)
        for s in range(1, N_DEV):
            g = g + p_ref[s].astype(F32)
        if transposed:
            if pad:
                g = jnp.concatenate([g, jnp.zeros((pad, tr), F32)], axis=0)
            g = g.T[:, :cols]
        g_ref[0] = g
        d_ref[0], mo_ref[0], vo_ref[0] = _adamw(w_ref[0], g, m_ref[0], v_ref[0])

    flat = pl.BlockSpec((1, tr, cols), lambda i: (0, i, 0))
    if transposed:
        part_spec = pl.BlockSpec((N_DEV, cols, tr), lambda i: (0, 0, i))
    else:
        part_spec = pl.BlockSpec((N_DEV, tr, cols), lambda i: (0, i, 0))
    return pl.pallas_call(
        body,
        name=name,
        grid=(rows // tr,),
        in_specs=[part_spec, flat, flat, flat],
        out_specs=[flat] * 4,
        out_shape=[jax.ShapeDtypeStruct((1, rows, cols), F32)] * 4,
        compiler_params=_params("parallel"),
    )(parts, w, m, v)


VEC_ROWS = 8
ROW_LOGITS, ROW_LOSS = 5, 7


def _vectors_update(part, w, m, v, *, name):
    def body(p_ref, w_ref, m_ref, v_ref, g_ref, d_ref, mo_ref, vo_ref, loss_ref, all_ref, send_sems, recv_sems):
        me = _place()
        all_ref[_slot(*me)] = p_ref[...]
        copies = []
        for d in range(1, N_DEV):
            peer = _flipped(me, d)
            copies.append(pltpu.make_async_remote_copy(
                src_ref=p_ref, dst_ref=all_ref.at[_slot(*me)], send_sem=send_sems.at[d - 1], recv_sem=recv_sems.at[d - 1],
                device_id=peer, device_id_type=MESH))
        for cp in copies:
            cp.start()
        for cp in copies:
            cp.wait()
        total = all_ref[0]
        for s in range(1, N_DEV):
            total = total + all_ref[s]
        wv = w_ref[...]
        half = D_MODEL // 2
        lb = _sigmoid(wv[ROW_LOGITS:ROW_LOGITS + 1, :half] - wv[ROW_LOGITS:ROW_LOGITS + 1, half:])
        d_first = total[ROW_LOGITS:ROW_LOGITS + 1, :half] * lb * (1.0 - lb)
        d_logits = jnp.concatenate([d_first, -d_first], axis=1)
        rowi = lax.broadcasted_iota(jnp.int32, (VEC_ROWS, D_MODEL), 0)
        g = jnp.where(rowi == ROW_LOGITS, d_logits, jnp.where(rowi < ROW_LOGITS, total, 0.0))
        g_ref[...] = g
        d_ref[...], mo_ref[...], vo_ref[...] = _adamw(wv, g, m_ref[...], v_ref[...])
        loss_ref[...] = total[ROW_LOSS:ROW_LOSS + 1, :]

    vmem = pl.BlockSpec(memory_space=pltpu.VMEM)
    return pl.pallas_call(
        body,
        name=name,
        in_specs=[vmem] * 4,
        out_specs=[vmem] * 5,
        out_shape=[jax.ShapeDtypeStruct((VEC_ROWS, D_MODEL), F32)] * 4 + [jax.ShapeDtypeStruct((1, D_MODEL), F32)],
        scratch_shapes=[pltpu.VMEM((N_DEV, VEC_ROWS, D_MODEL), F32), pltpu.SemaphoreType.DMA((7,)),
                        pltpu.SemaphoreType.DMA((7,))],
    )(part, w, m, v)


ROW_SHARDED = ("d1", "d2", "out")
TRANSPOSED = ("g1t", "u1t", "g2t", "u2t")


def _vector_rows(rows):
    rowi = lax.broadcasted_iota(jnp.int32, (VEC_ROWS, D_MODEL), 0)
    out = jnp.zeros((VEC_ROWS, D_MODEL), F32)
    for i, r in enumerate(rows):
        if r is not None:
            out = jnp.where(rowi == i, r, out)
    return out


def kernel(x, ffn1_norm, ffn1_w_gate, ffn1_w_up, ffn1_w_down, mix_norm, w_in, sb_out_norm, hg_lower_bound_logits, hg_out_norm, w_out, ffn2_norm, ffn2_w_gate, ffn2_w_up, ffn2_w_down, final_norm, loss_target, m_ffn1_norm, m_ffn1_w_gate, m_ffn1_w_up, m_ffn1_w_down, m_mix_norm, m_w_in, m_sb_out_norm, m_hg_lower_bound_logits, m_hg_out_norm, m_w_out, m_ffn2_norm, m_ffn2_w_gate, m_ffn2_w_up, m_ffn2_w_down, m_final_norm, v_ffn1_norm, v_ffn1_w_gate, v_ffn1_w_up, v_ffn1_w_down, v_mix_norm, v_w_in, v_sb_out_norm, v_hg_lower_bound_logits, v_hg_out_norm, v_w_out, v_ffn2_norm, v_ffn2_w_gate, v_ffn2_w_up, v_ffn2_w_down, v_final_norm):
    def matrices(g1, u1, d1, win, wout, g2, u2, d2):
        return {"g1t": g1, "u1t": u1, "d1": d1, "in": win, "out": wout, "g2t": g2, "u2t": u2, "d2": d2}

    def vectors(n1, nm, nsb, lg, nhg, n2, nf):
        return [n1, nm, n2, nf.reshape(1, D_MODEL), jnp.concatenate([nsb, nhg], axis=1), lg.reshape(1, D_MODEL), None, None]

    w_sh = matrices(ffn1_w_gate, ffn1_w_up, ffn1_w_down, w_in, w_out, ffn2_w_gate, ffn2_w_up, ffn2_w_down)
    m_sh = matrices(m_ffn1_w_gate, m_ffn1_w_up, m_ffn1_w_down, m_w_in, m_w_out, m_ffn2_w_gate, m_ffn2_w_up, m_ffn2_w_down)
    v_sh = matrices(v_ffn1_w_gate, v_ffn1_w_up, v_ffn1_w_down, v_w_in, v_w_out, v_ffn2_w_gate, v_ffn2_w_up, v_ffn2_w_down)
    keys = list(w_sh)

    blocks = [(w_sh[k][0].T if k in TRANSPOSED else w_sh[k][0]).astype(BF16) for k in keys]
    stacks = dict(zip(keys, _all_gather(blocks, name="gather_weights")))
    w_full = {k: stacks[k].reshape(-1, D_MODEL) for k in ROW_SHARDED + TRANSPOSED}
    w_full["in"] = stacks["in"].transpose(1, 0, 2).reshape(D_MODEL, IN_COLS)

    norms = {"ffn1": ffn1_norm, "mix": mix_norm, "sb": sb_out_norm, "hg": hg_out_norm, "ffn2": ffn2_norm,
             "final": final_norm.reshape(1, D_MODEL)}
    loss_row, grad_x, gw, gv = _local_step(x[0], loss_target[0], norms, hg_lower_bound_logits, w_full)

    by_owner = [gw[k].reshape(N_DEV, -1, D_MODEL) for k in keys if k != "in"]
    by_owner.append(gw["in"].reshape(D_MODEL, N_DEV, IN_SHARD).transpose(1, 0, 2).astype(BF16))
    parts = dict(zip([k for k in keys if k != "in"] + ["in"], _scatter_by_owner(by_owner, name="exchange_grads")))
    tiles = {"g1t": 256, "u1t": 256, "g2t": 256, "u2t": 256, "d1": 176, "d2": 176, "out": 128, "in": 256}
    updated = {k: _sum_and_update(parts[k], w_sh[k], m_sh[k], v_sh[k], name="adamw_" + k, tr=tiles[k],
                                  transposed=k in TRANSPOSED) for k in keys}
    mats = [{k: updated[k][i] for k in keys} for i in range(4)]

    lb_row = jnp.concatenate([gv["lb"], jnp.zeros_like(gv["lb"])], axis=1)
    part = _vector_rows([gv["ffn1"], gv["mix"], gv["ffn2"], gv["final"], jnp.concatenate([gv["sb"], gv["hg"]], axis=1),
                         lb_row, None, loss_row])
    vec_w = _vector_rows(vectors(ffn1_norm, mix_norm, sb_out_norm, hg_lower_bound_logits, hg_out_norm, ffn2_norm, final_norm))
    vec_m = _vector_rows(vectors(m_ffn1_norm, m_mix_norm, m_sb_out_norm, m_hg_lower_bound_logits, m_hg_out_norm,
                                 m_ffn2_norm, m_final_norm))
    vec_v = _vector_rows(vectors(v_ffn1_norm, v_mix_norm, v_sb_out_norm, v_hg_lower_bound_logits, v_hg_out_norm,
                                 v_ffn2_norm, v_final_norm))
    *vecs, loss_out = _vectors_update(part, vec_w, vec_m, vec_v, name="vectors_update")

    def leaves(mat, vec):
        half = D_MODEL // 2
        return (
            vec[0:1], mat["g1t"], mat["u1t"], mat["d1"], vec[1:2], mat["in"], vec[4:5, :half],
            vec[ROW_LOGITS].reshape(2, half), vec[4:5, half:], mat["out"], vec[2:3], mat["g2t"], mat["u2t"],
            mat["d2"], vec[3],
        )

    out = [loss_out[0, 0], grad_x[None]]
    for mat, vec in zip(mats, vecs):
        out.extend(leaves(mat, vec))
    return tuple(out)
```

```python
import jax
import jax.numpy as jnp
from jax import lax
from jax.experimental import pallas as pl
from jax.experimental.pallas import tpu as pltpu

F32, BF16 = jnp.float32, jnp.bfloat16
D_MODEL = 1024
D_FF = 2816
SB_WIDTH = 512
HG_WIDTH = 512
SB_HEAD_DIM = 64
HG_HEAD_DIM = 128
IN_COLS = 3584
EPS = 1e-6
N_DEV = 8
FF_SHARD = D_FF // N_DEV
IN_SHARD = IN_COLS // N_DEV
OUT_SHARD = D_MODEL // N_DEV
LANES = 128
HG_CHUNK = 16
VMEM_LIMIT_BYTES = 48 * 1024 * 1024
ADAM_LR, ADAM_B1, ADAM_B2, ADAM_EPS, ADAM_WD, ADAM_STEP = 0.001, 0.9, 0.999, 1e-08, 0.01, 10
MESH = pl.DeviceIdType.MESH


def _params(*semantics):
    return pltpu.CompilerParams(dimension_semantics=semantics, vmem_limit_bytes=VMEM_LIMIT_BYTES)


def _dot(a, b):
    return jnp.dot(a, b, preferred_element_type=F32)


def _dot_nt(a, b):
    return lax.dot_general(a, b, (((1,), (1,)), ((), ())), preferred_element_type=F32)


def _dot_tn(a, b):
    return lax.dot_general(a, b, (((0,), (0,)), ((), ())), preferred_element_type=F32)


def _split3(x):
    hi = x.astype(BF16)
    r1 = x - hi.astype(F32)
    mid = r1.astype(BF16)
    lo = (r1 - mid.astype(F32)).astype(BF16)
    return hi, mid, lo


def _rms(xv):
    rstd = lax.rsqrt(jnp.mean(xv * xv, axis=-1, keepdims=True) + EPS)
    return xv * rstd, rstd


def _sigmoid(x):
    return 1.0 / (1.0 + jnp.exp(-x))


def _mm(a, b, *, name, tm, tn, nt=False, scale=None, add=None, out_dtype=F32):
    m, k = a.shape
    n = b.shape[0] if nt else b.shape[1]
    assert m % tm == 0 and n % tn == 0, (name, a.shape, b.shape, tm, tn)

    def body(*refs):
        a_ref, b_ref = refs[0], refs[1]
        o_ref = refs[-1]
        av = a_ref[...].astype(BF16)
        bv = b_ref[...].astype(BF16)
        r = _dot_nt(av, bv) if nt else _dot(av, bv)
        if scale is not None:
            r = r * scale
        if add is not None:
            r = r + refs[2][...]
        o_ref[...] = r.astype(out_dtype)

    in_specs = [
        pl.BlockSpec((tm, k), lambda i, j: (i, 0)),
        pl.BlockSpec((tn, k), lambda i, j: (j, 0)) if nt else pl.BlockSpec((k, tn), lambda i, j: (0, j)),
    ]
    operands = [a, b]
    if add is not None:
        in_specs.append(pl.BlockSpec((tm, tn), lambda i, j: (i, j)))
        operands.append(add)
    return pl.pallas_call(
        body,
        name=name,
        grid=(m // tm, n // tn),
        in_specs=in_specs,
        out_specs=pl.BlockSpec((tm, tn), lambda i, j: (i, j)),
        out_shape=jax.ShapeDtypeStruct((m, n), out_dtype),
        compiler_params=_params("parallel", "parallel"),
    )(*operands)


def _ffn_fwd(x, gain, wgt, wut, wd, *, name, tm=512, tf=256):
    t = x.shape[0]
    nj = D_FF // tf

    def body(x_ref, g_ref, wg_ref, wu_ref, wd_ref, xo_ref, a_ref, b_ref, h_ref, st_ref, acc):
        j = pl.program_id(1)

        @pl.when(j == 0)
        def _():
            xhat, _ = _rms(x_ref[...])
            h_ref[...] = (xhat * g_ref[...]).astype(BF16)
            acc[...] = jnp.zeros_like(acc)

        h = h_ref[...]
        a = _dot_nt(h, wg_ref[...])
        b = _dot_nt(h, wu_ref[...])
        a_ref[...] = a.astype(BF16)
        b_ref[...] = b.astype(BF16)
        s = a * _sigmoid(a) * b
        st_ref[...] = s.T.astype(BF16)
        acc[...] += _dot(s.astype(BF16), wd_ref[...])

        @pl.when(j == nj - 1)
        def _():
            xo_ref[...] = x_ref[...] + 0.5 * acc[...]

    return pl.pallas_call(
        body,
        name=name,
        grid=(t // tm, nj),
        in_specs=[
            pl.BlockSpec((tm, D_MODEL), lambda i, j: (i, 0)),
            pl.BlockSpec((1, D_MODEL), lambda i, j: (0, 0)),
            pl.BlockSpec((tf, D_MODEL), lambda i, j: (j, 0)),
            pl.BlockSpec((tf, D_MODEL), lambda i, j: (j, 0)),
            pl.BlockSpec((tf, D_MODEL), lambda i, j: (j, 0)),
        ],
        out_specs=[
            pl.BlockSpec((tm, D_MODEL), lambda i, j: (i, 0)),
            pl.BlockSpec((tm, tf), lambda i, j: (i, j)),
            pl.BlockSpec((tm, tf), lambda i, j: (i, j)),
            pl.BlockSpec((tm, D_MODEL), lambda i, j: (i, 0)),
            pl.BlockSpec((tf, tm), lambda i, j: (j, i)),
        ],
        out_shape=[
            jax.ShapeDtypeStruct((t, D_MODEL), F32),
            jax.ShapeDtypeStruct((t, D_FF), BF16),
            jax.ShapeDtypeStruct((t, D_FF), BF16),
            jax.ShapeDtypeStruct((t, D_MODEL), BF16),
            jax.ShapeDtypeStruct((D_FF, t), BF16),
        ],
        scratch_shapes=[pltpu.VMEM((tm, D_MODEL), F32)],
        compiler_params=_params("parallel", "arbitrary"),
    )(x, gain, wgt, wut, wd)


def _ffn_bwd(dout, x, gain, a, b, wgt, wut, wd, *, name, tm=512, tf=256):
    t = x.shape[0]
    nj = D_FF // tf

    def body(do_ref, x_ref, g_ref, a_ref, b_ref, wg_ref, wu_ref, wd_ref, dx_ref, dg_ref, da_ref, db_ref, dob_ref,
             dob_scr, dh):
        i = pl.program_id(0)
        j = pl.program_id(1)

        @pl.when(j == 0)
        def _():
            d = (0.5 * do_ref[...]).astype(BF16)
            dob_scr[...] = d
            dob_ref[...] = d
            dh[...] = jnp.zeros_like(dh)

        ds = _dot_nt(dob_scr[...], wd_ref[...])
        av = a_ref[...].astype(F32)
        bv = b_ref[...].astype(F32)
        sig = _sigmoid(av)
        dbv = ds * (av * sig)
        dav = ds * bv * (sig * (1.0 + av * (1.0 - sig)))
        da_ref[...] = dav.T.astype(BF16)
        db_ref[...] = dbv.T.astype(BF16)
        dh[...] += _dot(dav.astype(BF16), wg_ref[...]) + _dot(dbv.astype(BF16), wu_ref[...])

        @pl.when(j == nj - 1)
        def _():
            xhat, rstd = _rms(x_ref[...])
            dhv = dh[...]
            part = jnp.sum(dhv * xhat, axis=0, keepdims=True)

            @pl.when(i == 0)
            def _():
                dg_ref[...] = part

            @pl.when(i > 0)
            def _():
                dg_ref[...] += part

            dxh = dhv * g_ref[...]
            dx_ref[...] = do_ref[...] + rstd * (dxh - xhat * jnp.mean(dxh * xhat, axis=-1, keepdims=True))

    return pl.pallas_call(
        body,
        name=name,
        grid=(t // tm, nj),
        in_specs=[
            pl.BlockSpec((tm, D_MODEL), lambda i, j: (i, 0)),
            pl.BlockSpec((tm, D_MODEL), lambda i, j: (i, 0)),
            pl.BlockSpec((1, D_MODEL), lambda i, j: (0, 0)),
            pl.BlockSpec((tm, tf), lambda i, j: (i, j)),
            pl.BlockSpec((tm, tf), lambda i, j: (i, j)),
            pl.BlockSpec((tf, D_MODEL), lambda i, j: (j, 0)),
            pl.BlockSpec((tf, D_MODEL), lambda i, j: (j, 0)),
            pl.BlockSpec((tf, D_MODEL), lambda i, j: (j, 0)),
        ],
        out_specs=[
            pl.BlockSpec((tm, D_MODEL), lambda i, j: (i, 0)),
            pl.BlockSpec((1, D_MODEL), lambda i, j: (0, 0)),
            pl.BlockSpec((tf, tm), lambda i, j: (j, i)),
            pl.BlockSpec((tf, tm), lambda i, j: (j, i)),
            pl.BlockSpec((tm, D_MODEL), lambda i, j: (i, 0)),
        ],
        out_shape=[
            jax.ShapeDtypeStruct((t, D_MODEL), F32),
            jax.ShapeDtypeStruct((1, D_MODEL), F32),
            jax.ShapeDtypeStruct((D_FF, t), BF16),
            jax.ShapeDtypeStruct((D_FF, t), BF16),
            jax.ShapeDtypeStruct((t, D_MODEL), BF16),
        ],
        scratch_shapes=[pltpu.VMEM((tm, D_MODEL), BF16), pltpu.VMEM((tm, D_MODEL), F32)],
        compiler_params=_params("arbitrary", "arbitrary"),
    )(dout, x, gain, a, b, wgt, wut, wd)


def _norm_fwd(x, gain, *, name, tm=512):
    t = x.shape[0]

    def body(x_ref, g_ref, h_ref, ht_ref):
        xhat, _ = _rms(x_ref[...])
        h = xhat * g_ref[...]
        h_ref[...] = h.astype(BF16)
        ht_ref[...] = h.T.astype(BF16)

    return pl.pallas_call(
        body,
        name=name,
        grid=(t // tm,),
        in_specs=[pl.BlockSpec((tm, D_MODEL), lambda i: (i, 0)), pl.BlockSpec((1, D_MODEL), lambda i: (0, 0))],
        out_specs=[pl.BlockSpec((tm, D_MODEL), lambda i: (i, 0)), pl.BlockSpec((D_MODEL, tm), lambda i: (0, i))],
        out_shape=[jax.ShapeDtypeStruct((t, D_MODEL), BF16), jax.ShapeDtypeStruct((D_MODEL, t), BF16)],
        compiler_params=_params("parallel"),
    )(x, gain)


def _norm_bwd(dh, x, gain, dres, *, name, tm=512):
    t = x.shape[0]

    def body(dh_ref, x_ref, g_ref, dr_ref, dx_ref, dg_ref):
        i = pl.program_id(0)
        xhat, rstd = _rms(x_ref[...])
        dhv = dh_ref[...]
        part = jnp.sum(dhv * xhat, axis=0, keepdims=True)

        @pl.when(i == 0)
        def _():
            dg_ref[...] = part

        @pl.when(i > 0)
        def _():
            dg_ref[...] += part

        dxh = dhv * g_ref[...]
        dx_ref[...] = dr_ref[...] + rstd * (dxh - xhat * jnp.mean(dxh * xhat, axis=-1, keepdims=True))

    row = pl.BlockSpec((tm, D_MODEL), lambda i: (i, 0))
    vec = pl.BlockSpec((1, D_MODEL), lambda i: (0, 0))
    return pl.pallas_call(
        body,
        name=name,
        grid=(t // tm,),
        in_specs=[row, row, vec, row],
        out_specs=[row, vec],
        out_shape=[jax.ShapeDtypeStruct((t, D_MODEL), F32), jax.ShapeDtypeStruct((1, D_MODEL), F32)],
        compiler_params=_params("arbitrary"),
    )(dh, x, gain, dres)


ATT_Q_TILE = 512
ATT_K_BLOCK = 256


def _first_head_lanes():
    return lax.broadcasted_iota(jnp.int32, (1, LANES), 1) < SB_HEAD_DIM


def _stack_heads(x):
    first = _first_head_lanes()
    return jnp.concatenate([jnp.where(first, x, 0.0), jnp.where(first, 0.0, x)], axis=0)


def _unstack_heads(x, rows):
    return jnp.where(_first_head_lanes(), x[:rows], x[rows:])


def _tri(n, relation):
    r = lax.broadcasted_iota(jnp.int32, (n, n), 0)
    c = lax.broadcasted_iota(jnp.int32, (n, n), 1)
    return relation(r, c).astype(BF16)


def _scan_dot(x, tri3):
    return _dot(jnp.concatenate(_split3(x), axis=1), tri3)


def _log_terms(z):
    sp = jnp.log(1.0 + jnp.exp(-jnp.abs(z)))
    return jnp.minimum(z, 0.0) - sp, -jnp.maximum(z, 0.0) - sp


def _attn_fwd(proj, *, name):
    t = proj.shape[0]
    tq, tk = ATT_Q_TILE, ATT_K_BLOCK
    diag = tq // tk
    n_pairs = SB_WIDTH // LANES

    def body(q_ref, k_ref, v_ref, o_ref, l_ref):
        qi = pl.program_id(1)
        q = q_ref[...] * (SB_HEAD_DIM ** -0.5)
        qs = _stack_heads(q).astype(BF16)
        tri = _tri(tk, lambda j, s: j > s)
        tri3 = jnp.concatenate([tri, tri, tri], axis=0)
        trow = lax.broadcasted_iota(jnp.int32, (tq, tk), 0)
        scol = lax.broadcasted_iota(jnp.int32, (tq, tk), 1)

        def block(off, carry, causal):
            acc, c = carry
            z = _dot_nt(qs, k_ref[pl.ds(off, tk), :].astype(BF16))
            lbeta, lrest = _log_terms(z)
            if causal is not None:
                lrest = jnp.where(causal, lrest, 0.0)
            w = jnp.exp(lbeta + (_scan_dot(lrest, tri3) + c))
            if causal is not None:
                w = jnp.where(causal, w, 0.0)
            acc = acc + _dot(w.astype(BF16), v_ref[pl.ds(off, tk), :].astype(BF16))
            return acc, c + jnp.sum(lrest, axis=1, keepdims=True)

        carry = (jnp.zeros((2 * tq, LANES), F32), jnp.zeros((2 * tq, 1), F32))
        for j in reversed(range(diag)):
            off = pl.multiple_of(qi * tq + j * tk, tk)
            mask = (scol + j * tk) < trow
            carry = block(off, carry, jnp.concatenate([mask, mask], axis=0))
        n_full = qi * diag

        def step(it, carry):
            return block(pl.multiple_of((n_full - 1 - it) * tk, tk), carry, None)

        acc, c = lax.fori_loop(0, n_full, step, carry)
        o_ref[...] = _unstack_heads(acc, tq)
        l_ref[...] = _unstack_heads(jnp.broadcast_to(c, (2 * tq, LANES)), tq)

    return pl.pallas_call(
        body,
        name=name,
        grid=(n_pairs, t // tq),
        in_specs=[
            pl.BlockSpec((tq, LANES), lambda p, i: (i, p)),
            pl.BlockSpec((t, LANES), lambda p, i: (0, n_pairs + p)),
            pl.BlockSpec((t, LANES), lambda p, i: (0, 2 * n_pairs + p)),
        ],
        out_specs=[pl.BlockSpec((tq, LANES), lambda p, i: (i, p))] * 2,
        out_shape=[jax.ShapeDtypeStruct((t, SB_WIDTH), F32)] * 2,
        compiler_params=_params("parallel", "parallel"),
    )(proj, proj, proj)


def _attn_bwd(proj, ltot, do, *, name, tie=None):
    t = proj.shape[0]
    tq, tk = ATT_Q_TILE, ATT_K_BLOCK
    diag = tq // tk
    n_pairs = SB_WIDTH // LANES
    scale = SB_HEAD_DIM ** -0.5

    def body(q_ref, k_ref, v_ref, l_ref, do_ref, *rest):
        dq_ref, dk_ref, dv_ref = rest[-3:]
        qi = pl.program_id(1)

        @pl.when(qi == 0)
        def _():
            dk_ref[...] = jnp.zeros_like(dk_ref)
            dv_ref[...] = jnp.zeros_like(dv_ref)

        q = q_ref[...] * scale
        lt = l_ref[...]
        qs = _stack_heads(q).astype(BF16)
        dos = _stack_heads(do_ref[...]).astype(BF16)
        first = _first_head_lanes()
        total = jnp.concatenate([jnp.max(jnp.where(first, lt, -jnp.inf), axis=1, keepdims=True),
                                 jnp.max(jnp.where(first, -jnp.inf, lt), axis=1, keepdims=True)], axis=0)
        upto = _tri(tk, lambda j, s: j <= s)
        upto3 = jnp.concatenate([upto, upto, upto], axis=0)
        before = _tri(tk, lambda s, j: s < j)
        before3 = jnp.concatenate([before, before, before], axis=0)
        trow = lax.broadcasted_iota(jnp.int32, (tq, tk), 0)
        scol = lax.broadcasted_iota(jnp.int32, (tq, tk), 1)

        def block(off, carry, causal):
            dq, cl, cg = carry
            kblk = k_ref[pl.ds(off, tk), :].astype(BF16)
            vblk = v_ref[pl.ds(off, tk), :].astype(BF16)
            z = _dot_nt(qs, kblk)
            lbeta, lrest = _log_terms(z)
            if causal is not None:
                lrest = jnp.where(causal, lrest, 0.0)
            w = jnp.exp(lbeta + (total - (_scan_dot(lrest, upto3) + cl)))
            if causal is not None:
                w = jnp.where(causal, w, 0.0)
            g = w * _dot_nt(dos, vblk)
            prior = _scan_dot(g, before3) + cg
            sig = jnp.exp(lbeta)
            dz = g * (1.0 - sig) - prior * sig
            if causal is not None:
                dz = jnp.where(causal, dz, 0.0)
            dq = dq + _dot(dz.astype(BF16), kblk)
            dk_ref[pl.ds(off, tk), :] += _dot(dz.T.astype(BF16), qs)
            dv_ref[pl.ds(off, tk), :] += _dot(w.T.astype(BF16), dos)
            return dq, cl + jnp.sum(lrest, axis=1, keepdims=True), cg + jnp.sum(g, axis=1, keepdims=True)

        def step(kb, carry):
            return block(pl.multiple_of(kb * tk, tk), carry, None)

        zero = jnp.zeros((2 * tq, 1), F32)
        carry = lax.fori_loop(0, qi * diag, step, (jnp.zeros((2 * tq, LANES), F32), zero, zero))
        for j in range(diag):
            off = pl.multiple_of(qi * tq + j * tk, tk)
            mask = (scol + j * tk) < trow
            carry = block(off, carry, jnp.concatenate([mask, mask], axis=0))
        dq_ref[...] = _unstack_heads(carry[0], tq) * scale

    tile_spec = pl.BlockSpec((tq, LANES), lambda p, i: (i, p))
    full_spec = pl.BlockSpec((t, LANES), lambda p, i: (0, p))
    return pl.pallas_call(
        body,
        name=name,
        grid=(n_pairs, t // tq),
        in_specs=[
            tile_spec,
            pl.BlockSpec((t, LANES), lambda p, i: (0, n_pairs + p)),
            pl.BlockSpec((t, LANES), lambda p, i: (0, 2 * n_pairs + p)),
            tile_spec,
            tile_spec,
        ] + ([] if tie is None else [pl.BlockSpec(memory_space=pl.ANY)]),
        out_specs=[tile_spec, full_spec, full_spec],
        out_shape=[jax.ShapeDtypeStruct((t, SB_WIDTH), F32)] * 3,
        compiler_params=_params("arbitrary", "arbitrary"),
    )(proj, proj, proj, ltot, do, *([] if tie is None else [tie]))


HG_BLOCK = 256
HG_HEADS = HG_WIDTH // HG_HEAD_DIM


def _chunk_mats(n):
    r = lax.broadcasted_iota(jnp.int32, (n, n), 0)
    c = lax.broadcasted_iota(jnp.int32, (n, n), 1)
    same = (r // HG_CHUNK) == (c // HG_CHUNK)
    upto = (same & (c <= r)).astype(BF16)
    whole = same.astype(BF16)
    onward = (same & (c >= r)).astype(BF16)
    return upto, whole, onward


def _rows_dot(mat, x):
    return _dot(jnp.concatenate([mat, mat, mat], axis=1), jnp.concatenate(_split3(x), axis=0))


def _lower_bound(lg_ref):
    lg = lg_ref[...]
    return _sigmoid(lg[0:1, :] - lg[1:2, :])


def _hgrn_prepare(q_ref, f_ref, lb, h, upto, whole):
    cols = slice(h * HG_HEAD_DIM, (h + 1) * HG_HEAD_DIM)
    lbh = lb[:, cols]
    sg = _sigmoid(f_ref[:, cols])
    forget = lbh + (1.0 - lbh) * sg
    logf = jnp.log(forget)
    kk = (1.0 - lbh) * (1.0 - sg)
    qv = q_ref[:, cols]
    qsig = _sigmoid(qv)
    qh = qv * qsig
    b = _rows_dot(upto, logf)
    blast = _rows_dot(whole, logf)
    return dict(lbh=lbh, sg=sg, forget=forget, kk=kk, qv=qv, qsig=qsig, qh=qh, b=b, eb=jnp.exp(b),
                ekb=jnp.exp(blast - b), dl=jnp.exp(blast))


def _hgrn_fwd(proj, logits, *, name):
    t = proj.shape[0]
    tb = HG_BLOCK
    nc = tb // HG_CHUNK
    hd = HG_HEAD_DIM

    def body(q_ref, f_ref, i_ref, lg_ref, o_ref, st_ref, state, qh_s, kk_s, b_s, qe_s, ke_s, dl_s):
        @pl.when(pl.program_id(0) == 0)
        def _():
            state[...] = jnp.zeros_like(state)

        lb = _lower_bound(lg_ref)
        upto, whole, _ = _chunk_mats(tb)
        for h in range(HG_HEADS):
            p = _hgrn_prepare(q_ref, f_ref, lb, h, upto, whole)
            qh_s[h] = p["qh"]
            kk_s[h] = p["kk"]
            b_s[h] = p["b"]
            qe_s[h] = (p["qh"] * p["eb"]).astype(BF16)
            ke_s[h] = (p["kk"] * p["ekb"]).astype(BF16)
            dl_s[h] = p["dl"]
        rowi = lax.broadcasted_iota(jnp.int32, (HG_CHUNK, hd), 0)

        def chunk(c, _):
            r0 = pl.multiple_of(c * HG_CHUNK, HG_CHUNK)
            rows = pl.ds(r0, HG_CHUNK)
            for h in range(HG_HEADS):
                cols = slice(h * hd, (h + 1) * hd)
                bc = b_s[h, rows, :]
                qc = qh_s[h, rows, :]
                kc = kk_s[h, rows, :]
                vc = i_ref[rows, cols]
                s_in = state[h]
                st_ref[c, h] = s_in
                o = _dot_nt(qe_s[h, rows, :], s_in.astype(BF16))
                for s in range(HG_CHUNK):
                    pair = jnp.where(rowi >= s, qc * jnp.exp(bc - bc[s:s + 1, :]) * kc[s:s + 1, :], 0.0)
                    o = o + jnp.sum(pair, axis=1, keepdims=True) * vc[s:s + 1, :]
                o_ref[rows, cols] = o
                state[h] = s_in * dl_s[h, pl.ds(r0, 1), :] + _dot_tn(vc.astype(BF16), ke_s[h, rows, :])
            return 0

        lax.fori_loop(0, nc, chunk, 0)

    blk = lambda col: pl.BlockSpec((tb, HG_WIDTH), lambda i: (i, col))
    head_f32 = pltpu.VMEM((HG_HEADS, tb, hd), F32)
    head_bf16 = pltpu.VMEM((HG_HEADS, tb, hd), BF16)
    return pl.pallas_call(
        body,
        name=name,
        grid=(t // tb,),
        in_specs=[blk(3), blk(4), blk(5), pl.BlockSpec((2, HG_WIDTH), lambda i: (0, 0))],
        out_specs=[
            pl.BlockSpec((tb, HG_WIDTH), lambda i: (i, 0)),
            pl.BlockSpec((nc, HG_HEADS, hd, hd), lambda i: (i, 0, 0, 0)),
        ],
        out_shape=[
            jax.ShapeDtypeStruct((t, HG_WIDTH), F32),
            jax.ShapeDtypeStruct((t // HG_CHUNK, HG_HEADS, hd, hd), F32),
        ],
        scratch_shapes=[pltpu.VMEM((HG_HEADS, hd, hd), F32), head_f32, head_f32, head_f32, head_bf16, head_bf16,
                        head_f32],
        compiler_params=_params("arbitrary"),
    )(proj, proj, proj, logits)


def _hgrn_bwd(proj, logits, states, do, *, name):
    t = proj.shape[0]
    tb = HG_BLOCK
    nb = t // tb
    nc = tb // HG_CHUNK
    hd = HG_HEAD_DIM

    def body(q_ref, f_ref, i_ref, lg_ref, st_ref, do_ref, dq_ref, df_ref, di_ref, dlb_ref,
             dstate, qh_s, kk_s, b_s, eb_s, ekb_s, qe_s, ke_s, dl_s, dqh_s, dkk_s, dlf_s):
        step = pl.program_id(0)

        @pl.when(step == 0)
        def _():
            dstate[...] = jnp.zeros_like(dstate)
            dlb_ref[...] = jnp.zeros_like(dlb_ref)

        lb = _lower_bound(lg_ref)
        upto, whole, _ = _chunk_mats(tb)
        prepared = []
        for h in range(HG_HEADS):
            p = _hgrn_prepare(q_ref, f_ref, lb, h, upto, whole)
            prepared.append(p)
            qh_s[h] = p["qh"]
            kk_s[h] = p["kk"]
            b_s[h] = p["b"]
            eb_s[h] = p["eb"]
            ekb_s[h] = p["ekb"]
            qe_s[h] = (p["qh"] * p["eb"]).astype(BF16)
            ke_s[h] = (p["kk"] * p["ekb"]).astype(BF16)
            dl_s[h] = p["dl"]
        rowi = lax.broadcasted_iota(jnp.int32, (HG_CHUNK, hd), 0)
        r16 = lax.broadcasted_iota(jnp.int32, (HG_CHUNK, HG_CHUNK), 0)
        c16 = lax.broadcasted_iota(jnp.int32, (HG_CHUNK, HG_CHUNK), 1)
        onward = (c16 >= r16).astype(BF16)

        def chunk(it, _):
            c = nc - 1 - it
            r0 = pl.multiple_of(c * HG_CHUNK, HG_CHUNK)
            rows = pl.ds(r0, HG_CHUNK)
            for h in range(HG_HEADS):
                cols = slice(h * hd, (h + 1) * hd)
                bc = b_s[h, rows, :]
                qc = qh_s[h, rows, :]
                kc = kk_s[h, rows, :]
                vc = i_ref[rows, cols]
                doc = do_ref[rows, cols]
                s_in = st_ref[c, h]
                ds_out = dstate[h]
                ds_out_b = ds_out.astype(BF16)
                docb = doc.astype(BF16)
                dl_row = dl_s[h, pl.ds(r0, 1), :]
                dqh = _dot(docb, s_in.astype(BF16)) * eb_s[h, rows, :]
                dkk = _dot(vc.astype(BF16), ds_out_b) * ekb_s[h, rows, :]
                dv = _dot_nt(ke_s[h, rows, :], ds_out_b)
                db = dqh * qc - dkk * kc
                dwhole = jnp.sum(dkk * kc, axis=0, keepdims=True) + jnp.sum(ds_out * s_in, axis=0, keepdims=True) * dl_row
                for s in range(HG_CHUNK):
                    keep = rowi >= s
                    at_s = rowi == s
                    e = jnp.exp(bc - bc[s:s + 1, :])
                    k_row = kc[s:s + 1, :]
                    pcol = jnp.sum(jnp.where(keep, qc * e * k_row, 0.0), axis=1, keepdims=True)
                    dpcol = jnp.sum(doc * vc[s:s + 1, :], axis=1, keepdims=True)
                    m = jnp.where(keep, e * dpcol, 0.0)
                    mk = m * k_row
                    dk_row = jnp.sum(m * qc, axis=0, keepdims=True)
                    dqh = dqh + mk
                    dkk = dkk + jnp.where(at_s, dk_row, 0.0)
                    db = db + mk * qc - jnp.where(at_s, dk_row * k_row, 0.0)
                    dv = dv + jnp.where(at_s, jnp.sum(pcol * doc, axis=0, keepdims=True), 0.0)
                dqh_s[h, rows, :] = dqh
                dkk_s[h, rows, :] = dkk
                dlf_s[h, rows, :] = _rows_dot(onward, db) + dwhole
                di_ref[rows, cols] = dv
                dstate[h] = ds_out * dl_row + _dot_tn(docb, qe_s[h, rows, :])
            return 0

        lax.fori_loop(0, nc, chunk, 0)
        for h in range(HG_HEADS):
            cols = slice(h * hd, (h + 1) * hd)
            p = prepared[h]
            dq_ref[:, cols] = dqh_s[h] * (p["qsig"] * (1.0 + p["qv"] * (1.0 - p["qsig"])))
            dforget = dlf_s[h] / p["forget"] - dkk_s[h]
            df_ref[:, cols] = dforget * (1.0 - p["lbh"]) * p["sg"] * (1.0 - p["sg"])
            dlb_ref[:, cols] += jnp.sum(dforget * (1.0 - p["sg"]), axis=0, keepdims=True)

    blk = lambda col: pl.BlockSpec((tb, HG_WIDTH), lambda i: (nb - 1 - i, col))
    vec = pl.BlockSpec((1, HG_WIDTH), lambda i: (0, 0))
    head_f32 = pltpu.VMEM((HG_HEADS, tb, hd), F32)
    head_bf16 = pltpu.VMEM((HG_HEADS, tb, hd), BF16)
    return pl.pallas_call(
        body,
        name=name,
        grid=(nb,),
        in_specs=[
            blk(3), blk(4), blk(5),
            pl.BlockSpec((2, HG_WIDTH), lambda i: (0, 0)),
            pl.BlockSpec((nc, HG_HEADS, hd, hd), lambda i: (nb - 1 - i, 0, 0, 0)),
            blk(0),
        ],
        out_specs=[blk(0), blk(0), blk(0), vec],
        out_shape=[jax.ShapeDtypeStruct((t, HG_WIDTH), F32)] * 3 + [jax.ShapeDtypeStruct((1, HG_WIDTH), F32)],
        scratch_shapes=[
            pltpu.VMEM((HG_HEADS, hd, hd), F32),
            head_f32, head_f32, head_f32, head_f32, head_f32, head_bf16, head_bf16, head_f32,
            head_f32, head_f32, head_f32,
        ],
        compiler_params=_params("arbitrary"),
    )(proj, proj, proj, logits, states, do)


def _group_mat(width, head_dim):
    r = lax.broadcasted_iota(jnp.int32, (width, width), 0)
    c = lax.broadcasted_iota(jnp.int32, (width, width), 1)
    return ((r // head_dim) == (c // head_dim)).astype(BF16)


def _head_mean(x, mat, head_dim):
    hi = x.astype(BF16)
    lo = (x - hi.astype(F32)).astype(BF16)
    return (_dot(hi, mat) + _dot(lo, mat)) * (1.0 / head_dim)


def _mix_out_fwd(o_sb, o_hg, proj, g_sb, g_hg, w_out, x1, *, name, tm=256):
    t = x1.shape[0]

    def body(osb_ref, ohg_ref, gate_ref, gsb_ref, ghg_ref, w_ref, x_ref, xo_ref, mt_ref):
        msb = _group_mat(SB_WIDTH, SB_HEAD_DIM)
        mhg = _group_mat(HG_WIDTH, HG_HEAD_DIM)
        osb = osb_ref[...]
        ohg = ohg_ref[...]
        nsb = osb * lax.rsqrt(_head_mean(osb * osb, msb, SB_HEAD_DIM) + EPS) * gsb_ref[...]
        gate = gate_ref[...]
        nhg = ohg * lax.rsqrt(_head_mean(ohg * ohg, mhg, HG_HEAD_DIM) + EPS) * ghg_ref[...] * (gate * _sigmoid(gate))
        mixed = jnp.concatenate([nsb, nhg], axis=1)
        mt_ref[...] = mixed.T.astype(BF16)
        xo_ref[...] = x_ref[...] + _dot(mixed.astype(BF16), w_ref[...])

    half = pl.BlockSpec((tm, SB_WIDTH), lambda i: (i, 0))
    vec = pl.BlockSpec((1, SB_WIDTH), lambda i: (0, 0))
    row = pl.BlockSpec((tm, D_MODEL), lambda i: (i, 0))
    return pl.pallas_call(
        body,
        name=name,
        grid=(t // tm,),
        in_specs=[half, half, pl.BlockSpec((tm, HG_WIDTH), lambda i: (i, 6)), vec, vec,
                  pl.BlockSpec((D_MODEL, D_MODEL), lambda i: (0, 0)), row],
        out_specs=[row, pl.BlockSpec((D_MODEL, tm), lambda i: (0, i))],
        out_shape=[jax.ShapeDtypeStruct((t, D_MODEL), F32), jax.ShapeDtypeStruct((D_MODEL, t), BF16)],
        compiler_params=_params("parallel"),
    )(o_sb, o_hg, proj, g_sb, g_hg, w_out, x1)


def _mix_out_bwd(dx2, o_sb, o_hg, proj, g_sb, g_hg, w_out, *, name, tm=256):
    t = dx2.shape[0]

    def body(dx_ref, osb_ref, ohg_ref, gate_ref, gsb_ref, ghg_ref, w_ref, dosb_ref, dohg_ref, dgate_ref, dgsb_ref,
             dghg_ref, dxb_ref):
        i = pl.program_id(0)
        msb = _group_mat(SB_WIDTH, SB_HEAD_DIM)
        mhg = _group_mat(HG_WIDTH, HG_HEAD_DIM)
        dxb = dx_ref[...].astype(BF16)
        dxb_ref[...] = dxb
        dmixed = _dot_nt(dxb, w_ref[...])
        dnsb = dmixed[:, :SB_WIDTH]
        dy = dmixed[:, SB_WIDTH:]

        osb = osb_ref[...]
        rstd = lax.rsqrt(_head_mean(osb * osb, msb, SB_HEAD_DIM) + EPS)
        ohat = osb * rstd
        part_sb = jnp.sum(dnsb * ohat, axis=0, keepdims=True)
        dohat = dnsb * gsb_ref[...]
        dosb_ref[...] = rstd * (dohat - ohat * _head_mean(dohat * ohat, msb, SB_HEAD_DIM))

        ohg = ohg_ref[...]
        rstd = lax.rsqrt(_head_mean(ohg * ohg, mhg, HG_HEAD_DIM) + EPS)
        ohat = ohg * rstd
        gate = gate_ref[...]
        sig = _sigmoid(gate)
        dn = dy * (gate * sig)
        dgate_ref[...] = dy * (ohat * ghg_ref[...]) * (sig * (1.0 + gate * (1.0 - sig)))
        part_hg = jnp.sum(dn * ohat, axis=0, keepdims=True)
        dohat = dn * ghg_ref[...]
        dohg_ref[...] = rstd * (dohat - ohat * _head_mean(dohat * ohat, mhg, HG_HEAD_DIM))

        @pl.when(i == 0)
        def _():
            dgsb_ref[...] = part_sb
            dghg_ref[...] = part_hg

        @pl.when(i > 0)
        def _():
            dgsb_ref[...] += part_sb
            dghg_ref[...] += part_hg

    half = pl.BlockSpec((tm, SB_WIDTH), lambda i: (i, 0))
    vec = pl.BlockSpec((1, SB_WIDTH), lambda i: (0, 0))
    row = pl.BlockSpec((tm, D_MODEL), lambda i: (i, 0))
    return pl.pallas_call(
        body,
        name=name,
        grid=(t // tm,),
        in_specs=[row, half, half, pl.BlockSpec((tm, HG_WIDTH), lambda i: (i, 6)), vec, vec,
                  pl.BlockSpec((D_MODEL, D_MODEL), lambda i: (0, 0))],
        out_specs=[half, half, half, vec, vec, row],
        out_shape=[jax.ShapeDtypeStruct((t, SB_WIDTH), F32)] * 3 + [jax.ShapeDtypeStruct((1, SB_WIDTH), F32)] * 2
        + [jax.ShapeDtypeStruct((t, D_MODEL), BF16)],
        compiler_params=_params("arbitrary"),
    )(dx2, o_sb, o_hg, proj, g_sb, g_hg, w_out)


def _loss_head(x3, gain, target, *, name, tm=512):
    t = x3.shape[0]

    def body(x_ref, g_ref, y_ref, dx_ref, dg_ref, loss_ref):
        i = pl.program_id(0)
        xhat, rstd = _rms(x_ref[...])
        err = xhat * g_ref[...] - y_ref[...]
        part_loss = 0.5 * jnp.sum(jnp.mean(err * err, axis=-1, keepdims=True), axis=0, keepdims=True)
        dy = err * (1.0 / D_MODEL)
        part_g = jnp.sum(dy * xhat, axis=0, keepdims=True)

        @pl.when(i == 0)
        def _():
            dg_ref[...] = part_g
            loss_ref[...] = jnp.broadcast_to(part_loss, loss_ref.shape)

        @pl.when(i > 0)
        def _():
            dg_ref[...] += part_g
            loss_ref[...] += jnp.broadcast_to(part_loss, loss_ref.shape)

        dxh = dy * g_ref[...]
        dx_ref[...] = rstd * (dxh - xhat * jnp.mean(dxh * xhat, axis=-1, keepdims=True))

    row = pl.BlockSpec((tm, D_MODEL), lambda i: (i, 0))
    vec = pl.BlockSpec((1, D_MODEL), lambda i: (0, 0))
    return pl.pallas_call(
        body,
        name=name,
        grid=(t // tm,),
        in_specs=[row, vec, row],
        out_specs=[row, vec, vec],
        out_shape=[jax.ShapeDtypeStruct((t, D_MODEL), F32), jax.ShapeDtypeStruct((1, D_MODEL), F32),
                   jax.ShapeDtypeStruct((1, D_MODEL), F32)],
        compiler_params=_params("arbitrary"),
    )(x3, gain, target)


def _local_step(x, target, norms, logits, w, weights_after=None, grads_ready=None):
    w = dict(w)
    x1, a1, b1, h1, s1t = _ffn_fwd(x, norms["ffn1"], w["g1t"], w["u1t"], w["d1"], name="ffn1_fwd")
    if weights_after is not None:
        w.update(weights_after("ffn1", x1))
    hm, hmt = _norm_fwd(x1, norms["mix"], name="mix_norm_fwd")
    proj = _mm(hm, w["in"], name="in_proj", tm=512, tn=512)
    o_sb, ltot = _attn_fwd(proj, name="sb_attn_fwd")
    o_hg, states = _hgrn_fwd(proj, logits, name="hgrn2_fwd")
    x2, mixed_t = _mix_out_fwd(o_sb, o_hg, proj, norms["sb"], norms["hg"], w["out"], x1, name="mix_out_fwd")
    if weights_after is not None:
        w.update(weights_after("mix", x2))
    x3, a2, b2, h2, s2t = _ffn_fwd(x2, norms["ffn2"], w["g2t"], w["u2t"], w["d2"], name="ffn2_fwd")
    dx3, d_final, loss_row = _loss_head(x3, norms["final"], target, name="loss_head")

    def weight_grad(lhs_t, rhs, name):
        return _mm(lhs_t, rhs, name=name, tm=256 if lhs_t.shape[0] % 256 == 0 else lhs_t.shape[0], tn=D_MODEL, out_dtype=BF16)

    gw, gv = {}, {"final": d_final}
    dx2, gv["ffn2"], da2t, db2t, dob2 = _ffn_bwd(dx3, x2, norms["ffn2"], a2, b2, w["g2t"], w["u2t"], w["d2"],
                                                 name="ffn2_bwd")
    gw["g2t"] = weight_grad(da2t, h2, "ffn2_dgate")
    gw["u2t"] = weight_grad(db2t, h2, "ffn2_dup")
    gw["d2"] = weight_grad(s2t, dob2, "ffn2_ddown")

    do_sb, do_hg, d_gate, gv["sb"], gv["hg"], dx2b = _mix_out_bwd(
        dx2, o_sb, o_hg, proj, norms["sb"], norms["hg"], w["out"], name="mix_out_bwd")
    gw["out"] = weight_grad(mixed_t, dx2b, "out_dw")
    tie = grads_ready("mix", gw) if grads_ready is not None else None
    dq_sb, dk_sb, dv_sb = _attn_bwd(proj, ltot, do_sb, name="sb_attn_bwd", tie=tie)
    dq_hg, df_hg, di_hg, d_lb = _hgrn_bwd(proj, logits if tie is None else logits + tie[0, 0], states, do_hg,
                                          name="hgrn2_bwd")
    dproj = jnp.concatenate([dq_sb, dk_sb, dv_sb, dq_hg, df_hg, di_hg, d_gate], axis=1).astype(BF16)
    gw["in"] = _mm(hmt, dproj, name="in_dw", tm=D_MODEL, tn=256)
    tie = grads_ready("in", gw) if grads_ready is not None else None
    dhm = _mm(dproj, w["in"], name="in_dx", tm=512, tn=D_MODEL, nt=True)
    dx1, gv["mix"] = _norm_bwd(dhm, x1, norms["mix"] if tie is None else norms["mix"] + tie[0, 0], dx2,
                               name="mix_norm_bwd")

    dx, gv["ffn1"], da1t, db1t, dob1 = _ffn_bwd(dx1, x, norms["ffn1"], a1, b1, w["g1t"], w["u1t"], w["d1"],
                                                name="ffn1_bwd")
    gw["g1t"] = weight_grad(da1t, h1, "ffn1_dgate")
    gw["u1t"] = weight_grad(db1t, h1, "ffn1_dup")
    gw["d1"] = weight_grad(s1t, dob1, "ffn1_ddown")
    gv["lb"] = d_lb
    return loss_row, dx, gw, gv


HBM = pl.BlockSpec(memory_space=pl.ANY)


def _place():
    return lax.axis_index("x"), lax.axis_index("y"), lax.axis_index("c")


def _slot(px, py, pc):
    return 4 * px + 2 * py + pc


def _all_gather(blocks, *, name):
    n = len(blocks)

    def body(*refs):
        ins, outs = refs[:n], refs[n:2 * n]
        send_sems, recv_sems, local_sems = refs[2 * n:]
        x, y, c = _place()
        me, sibling = (x, y, c), (x, y, 1 - c)
        chips = [(1 - x, y), (x, 1 - y), (1 - x, 1 - y)]

        def copy(a, k, block, to, src=None):
            dst = outs[a].at[_slot(*block)]
            return pltpu.make_async_remote_copy(
                src_ref=dst if src is None else src, dst_ref=dst, send_sem=send_sems.at[7 * a + k],
                recv_sem=recv_sems.at[7 * a + k], device_id=to, device_id_type=MESH)

        mine = [pltpu.make_async_copy(ins[a], outs[a].at[_slot(*me)], local_sems.at[a]) for a in range(n)]
        for cp in mine:
            cp.start()
        first = []
        for a in range(n):
            first.append(copy(a, 0, me, sibling, src=ins[a]))
            first += [copy(a, 1 + j, me, (*chip, c), src=ins[a]) for j, chip in enumerate(chips)]
        for cp in first:
            cp.start()
        passed = []
        for j, chip in enumerate(chips):
            for a in range(n):
                copy(a, 1 + j, (*chip, c), me).wait_recv()
                fwd = copy(a, 4 + j, (*chip, c), sibling)
                fwd.start()
                passed.append(fwd)
        for a in range(n):
            copy(a, 0, sibling, me).wait_recv()
            for j, chip in enumerate(chips):
                copy(a, 4 + j, (*chip, 1 - c), me).wait_recv()
        for cp in first + passed:
            cp.wait_send()
        for cp in mine:
            cp.wait()

    return pl.pallas_call(
        body,
        name=name,
        in_specs=[HBM] * n,
        out_specs=[HBM] * n,
        out_shape=[jax.ShapeDtypeStruct((N_DEV,) + b.shape, b.dtype) for b in blocks],
        scratch_shapes=[pltpu.SemaphoreType.DMA((7 * n,)), pltpu.SemaphoreType.DMA((7 * n,)),
                        pltpu.SemaphoreType.DMA((n,))],
    )(*blocks)


def _flipped(place, d):
    return tuple(1 - p if (d >> (2 - axis)) & 1 else p for axis, p in enumerate(place))


def _scatter_by_owner(stacks, *, name):
    n = len(stacks)

    def body(*refs):
        ins, outs = refs[:n], refs[n:2 * n]
        send_sems, recv_sems, local_sems = refs[2 * n:]
        me = _place()
        mine = [pltpu.make_async_copy(ins[a].at[_slot(*me)], outs[a].at[_slot(*me)], local_sems.at[a]) for a in range(n)]
        for cp in mine:
            cp.start()
        copies = []
        for d in range(1, N_DEV):
            peer = _flipped(me, d)
            for a in range(n):
                copies.append(pltpu.make_async_remote_copy(
                    src_ref=ins[a].at[_slot(*peer)], dst_ref=outs[a].at[_slot(*me)], send_sem=send_sems.at[7 * a + d - 1],
                    recv_sem=recv_sems.at[7 * a + d - 1], device_id=peer, device_id_type=MESH))
        for cp in copies:
            cp.start()
        for cp in copies:
            cp.wait()
        for cp in mine:
            cp.wait()

    return pl.pallas_call(
        body,
        name=name,
        in_specs=[HBM] * n,
        out_specs=[HBM] * n,
        out_shape=[jax.ShapeDtypeStruct(s.shape, s.dtype) for s in stacks],
        scratch_shapes=[pltpu.SemaphoreType.DMA((7 * n,)), pltpu.SemaphoreType.DMA((7 * n,)),
                        pltpu.SemaphoreType.DMA((n,))],
    )(*stacks)


SEM = pl.BlockSpec(memory_space=pltpu.SEMAPHORE)
EFFECT = pltpu.SideEffectType.DATAFLOW_SIDE_EFFECTING


def _split_copies(me, srcs, lands, send_sems, recv_sems, by_owner):
    copies = []
    for d in range(1, N_DEV):
        peer = _flipped(me, d)
        for a, (src, land) in enumerate(zip(srcs, lands)):
            copies.append(pltpu.make_async_remote_copy(
                src_ref=src.at[_slot(*peer)] if by_owner else src, dst_ref=land.at[_slot(*me)],
                send_sem=send_sems.at[7 * a + d - 1], recv_sem=recv_sems.at[7 * a + d - 1], device_id=peer,
                device_id_type=MESH))
    return copies


def _copies_start(srcs, *, name, by_owner):
    n = len(srcs)
    land_shapes = [s.shape if by_owner else (N_DEV,) + s.shape for s in srcs]
    lands = [pltpu.with_memory_space_constraint(lax.empty(shape, s.dtype), pltpu.HBM) for shape, s in zip(land_shapes, srcs)]
    srcs = [pltpu.with_memory_space_constraint(s, pltpu.HBM) for s in srcs]

    def body(*refs):
        src_refs, land_refs = refs[:n], refs[n:2 * n]
        send_sems, recv_sems = refs[2 * n], refs[2 * n + 1]
        token = refs[-1]
        for cp in _split_copies(_place(), src_refs, land_refs, send_sems, recv_sems, by_owner):
            cp.start()
        token[...] = jnp.zeros_like(token)

    out = pl.pallas_call(
        body,
        name=name,
        in_specs=[HBM] * (2 * n),
        out_specs=[SEM, SEM] + [HBM] * (2 * n) + [pl.BlockSpec(memory_space=pltpu.VMEM)],
        out_shape=[pltpu.SemaphoreType.DMA((7 * n,)), pltpu.SemaphoreType.DMA((7 * n,))]
        + [pltpu.HBM(s.shape, s.dtype) for s in srcs] + [pltpu.HBM(shape, s.dtype) for shape, s in zip(land_shapes, srcs)]
        + [jax.ShapeDtypeStruct((8, LANES), F32)],
        input_output_aliases={i: 2 + i for i in range(2 * n)},
        compiler_params=pltpu.CompilerParams(has_side_effects=EFFECT),
    )(*srcs, *lands)
    return (out[0], out[1], out[2:2 + n], out[2 + n:2 + 2 * n]), out[-1]


def _copies_wait(started, after, *, name, by_owner):
    send_sems, recv_sems, srcs, lands = started
    n = len(srcs)

    def body(*refs):
        src_refs, land_refs = refs[:n], refs[n:2 * n]
        for cp in _split_copies(_place(), src_refs, land_refs, refs[2 * n], refs[2 * n + 1], by_owner):
            cp.wait_send()
            cp.wait_recv()

    out = pl.pallas_call(
        body,
        name=name,
        in_specs=[HBM] * (2 * n) + [SEM, SEM, HBM],
        out_specs=[HBM] * (2 * n),
        out_shape=[pltpu.HBM(s.shape, s.dtype) for s in srcs] + [pltpu.HBM(s.shape, s.dtype) for s in lands],
        input_output_aliases={i: i for i in range(2 * n)},
        compiler_params=pltpu.CompilerParams(has_side_effects=EFFECT),
    )(*srcs, *lands, send_sems, recv_sems, after)
    return out[n:]


def _with_own(lands, own, slot):
    zero = jnp.zeros((), jnp.int32)
    return [lax.dynamic_update_slice(land, o[None], (slot.astype(jnp.int32),) + (zero,) * o.ndim)
            for land, o in zip(lands, own)]


def _adamw(w, g, m, v):
    m = ADAM_B1 * m + (1.0 - ADAM_B1) * g
    v = ADAM_B2 * v + (1.0 - ADAM_B2) * (g * g)
    m_hat = m / (1.0 - ADAM_B1 ** ADAM_STEP)
    v_hat = v / (1.0 - ADAM_B2 ** ADAM_STEP)
    delta = -ADAM_LR * (m_hat / (jnp.sqrt(v_hat) + ADAM_EPS) + ADAM_WD * w)
    return delta, m, v


def _sum_and_update(parts, w, m, v, *, name, tr, transposed=False):
    _, rows, cols = w.shape
    pad = -cols % LANES

    def body(p_ref, w_ref, m_ref, v_ref, g_ref, d_ref, mo_ref, vo_ref):
        g = p_ref[0].astype(F32)
        for s in range(1, N_DEV):
            g = g + p_ref[s].astype(F32)
        if transposed:
            if pad:
                g = jnp.concatenate([g, jnp.zeros((pad, tr), F32)], axis=0)
            g = g.T[:, :cols]
        g_ref[0] = g
        d_ref[0], mo_ref[0], vo_ref[0] = _adamw(w_ref[0], g, m_ref[0], v_ref[0])

    flat = pl.BlockSpec((1, tr, cols), lambda i: (0, i, 0))
    if transposed:
        part_spec = pl.BlockSpec((N_DEV, cols, tr), lambda i: (0, 0, i))
    else:
        part_spec = pl.BlockSpec((N_DEV, tr, cols), lambda i: (0, i, 0))
    return pl.pallas_call(
        body,
        name=name,
        grid=(rows // tr,),
        in_specs=[part_spec, flat, flat, flat],
        out_specs=[flat] * 4,
        out_shape=[jax.ShapeDtypeStruct((1, rows, cols), F32)] * 4,
        compiler_params=_params("parallel"),
    )(parts, w, m, v)


VEC_ROWS = 8
ROW_LOGITS, ROW_LOSS = 5, 7


def _vectors_update(part, w, m, v, *, name):
    def body(p_ref, w_ref, m_ref, v_ref, g_ref, d_ref, mo_ref, vo_ref, loss_ref, all_ref, send_sems, recv_sems):
        me = _place()
        all_ref[_slot(*me)] = p_ref[...]
        copies = []
        for d in range(1, N_DEV):
            peer = _flipped(me, d)
            copies.append(pltpu.make_async_remote_copy(
                src_ref=p_ref, dst_ref=all_ref.at[_slot(*me)], send_sem=send_sems.at[d - 1], recv_sem=recv_sems.at[d - 1],
                device_id=peer, device_id_type=MESH))
        for cp in copies:
            cp.start()
        for cp in copies:
            cp.wait()
        total = all_ref[0]
        for s in range(1, N_DEV):
            total = total + all_ref[s]
        wv = w_ref[...]
        half = D_MODEL // 2
        lb = _sigmoid(wv[ROW_LOGITS:ROW_LOGITS + 1, :half] - wv[ROW_LOGITS:ROW_LOGITS + 1, half:])
        d_first = total[ROW_LOGITS:ROW_LOGITS + 1, :half] * lb * (1.0 - lb)
        d_logits = jnp.concatenate([d_first, -d_first], axis=1)
        rowi = lax.broadcasted_iota(jnp.int32, (VEC_ROWS, D_MODEL), 0)
        g = jnp.where(rowi == ROW_LOGITS, d_logits, jnp.where(rowi < ROW_LOGITS, total, 0.0))
        g_ref[...] = g
        d_ref[...], mo_ref[...], vo_ref[...] = _adamw(wv, g, m_ref[...], v_ref[...])
        loss_ref[...] = total[ROW_LOSS:ROW_LOSS + 1, :]

    vmem = pl.BlockSpec(memory_space=pltpu.VMEM)
    return pl.pallas_call(
        body,
        name=name,
        in_specs=[vmem] * 4,
        out_specs=[vmem] * 5,
        out_shape=[jax.ShapeDtypeStruct((VEC_ROWS, D_MODEL), F32)] * 4 + [jax.ShapeDtypeStruct((1, D_MODEL), F32)],
        scratch_shapes=[pltpu.VMEM((N_DEV, VEC_ROWS, D_MODEL), F32), pltpu.SemaphoreType.DMA((7,)),
                        pltpu.SemaphoreType.DMA((7,))],
    )(part, w, m, v)


ROW_SHARDED = ("d1", "d2", "out")
TRANSPOSED = ("g1t", "u1t", "g2t", "u2t")


def _vector_rows(rows):
    rowi = lax.broadcasted_iota(jnp.int32, (VEC_ROWS, D_MODEL), 0)
    out = jnp.zeros((VEC_ROWS, D_MODEL), F32)
    for i, r in enumerate(rows):
        if r is not None:
            out = jnp.where(rowi == i, r, out)
    return out


def kernel(x, ffn1_norm, ffn1_w_gate, ffn1_w_up, ffn1_w_down, mix_norm, w_in, sb_out_norm, hg_lower_bound_logits, hg_out_norm, w_out, ffn2_norm, ffn2_w_gate, ffn2_w_up, ffn2_w_down, final_norm, loss_target, m_ffn1_norm, m_ffn1_w_gate, m_ffn1_w_up, m_ffn1_w_down, m_mix_norm, m_w_in, m_sb_out_norm, m_hg_lower_bound_logits, m_hg_out_norm, m_w_out, m_ffn2_norm, m_ffn2_w_gate, m_ffn2_w_up, m_ffn2_w_down, m_final_norm, v_ffn1_norm, v_ffn1_w_gate, v_ffn1_w_up, v_ffn1_w_down, v_mix_norm, v_w_in, v_sb_out_norm, v_hg_lower_bound_logits, v_hg_out_norm, v_w_out, v_ffn2_norm, v_ffn2_w_gate, v_ffn2_w_up, v_ffn2_w_down, v_final_norm):
    def matrices(g1, u1, d1, win, wout, g2, u2, d2):
        return {"g1t": g1, "u1t": u1, "d1": d1, "in": win, "out": wout, "g2t": g2, "u2t": u2, "d2": d2}

    def vectors(n1, nm, nsb, lg, nhg, n2, nf):
        return [n1, nm, n2, nf.reshape(1, D_MODEL), jnp.concatenate([nsb, nhg], axis=1), lg.reshape(1, D_MODEL), None, None]

    w_sh = matrices(ffn1_w_gate, ffn1_w_up, ffn1_w_down, w_in, w_out, ffn2_w_gate, ffn2_w_up, ffn2_w_down)
    m_sh = matrices(m_ffn1_w_gate, m_ffn1_w_up, m_ffn1_w_down, m_w_in, m_w_out, m_ffn2_w_gate, m_ffn2_w_up, m_ffn2_w_down)
    v_sh = matrices(v_ffn1_w_gate, v_ffn1_w_up, v_ffn1_w_down, v_w_in, v_w_out, v_ffn2_w_gate, v_ffn2_w_up, v_ffn2_w_down)
    keys = list(w_sh)

    slot = _slot(*_place())

    def full(key, stack):
        if key == "in":
            return stack.transpose(1, 0, 2).reshape(D_MODEL, IN_COLS)
        return stack.reshape(-1, D_MODEL)

    def by_owner(key, grad):
        if key == "in":
            return grad.reshape(D_MODEL, N_DEV, IN_SHARD).transpose(1, 0, 2).astype(BF16)
        return grad.reshape(N_DEV, -1, D_MODEL)

    blocks = {k: (w_sh[k][0].T if k in TRANSPOSED else w_sh[k][0]).astype(BF16) for k in keys}
    first, mid, last = ("g1t", "u1t", "d1"), ("in", "out"), ("g2t", "u2t", "d2")
    w_first = {k: full(k, s) for k, s in zip(first, _all_gather([blocks[k] for k in first], name="gather_ffn1"))}
    flights = {}
    flights["ffn1"], token_mid = _copies_start([blocks[k] for k in mid], name="gather_mid_start", by_owner=False)
    flights["mix"], token_last = _copies_start([blocks[k] for k in last], name="gather_ffn2_start", by_owner=False)

    def weights_after(stage, result):
        group = mid if stage == "ffn1" else last
        lands = _copies_wait(flights[stage], result, name="gather_" + stage + "_wait", by_owner=False)
        return {k: full(k, s) for k, s in zip(group, _with_own(lands, [blocks[k] for k in group], slot))}

    sent = {}

    def grads_ready(stage, gw):
        group = ("g2t", "u2t", "d2", "out") if stage == "mix" else ("in",)
        stacks = [by_owner(k, gw[k]) for k in group]
        flight, token = _copies_start(stacks, name="grads_" + stage + "_start", by_owner=True)
        sent[stage] = (group, stacks, flight)
        return token

    norms = {"ffn1": ffn1_norm + (token_mid[0, 0] + token_last[0, 0]), "mix": mix_norm, "sb": sb_out_norm,
             "hg": hg_out_norm, "ffn2": ffn2_norm, "final": final_norm.reshape(1, D_MODEL)}
    loss_row, grad_x, gw, gv = _local_step(x[0], loss_target[0], norms, hg_lower_bound_logits, w_first, weights_after,
                                           grads_ready)

    parts = dict(zip(first, _scatter_by_owner([by_owner(k, gw[k]) for k in first], name="exchange_ffn1")))
    for stage, (group, stacks, flight) in sent.items():
        lands = _copies_wait(flight, parts["d1"], name="grads_" + stage + "_wait", by_owner=True)
        own = [lax.dynamic_index_in_dim(s, slot, keepdims=False) for s in stacks]
        parts.update(zip(group, _with_own(lands, own, slot)))
    tiles = {"g1t": 256, "u1t": 256, "g2t": 256, "u2t": 256, "d1": 176, "d2": 176, "out": 128, "in": 256}
    updated = {k: _sum_and_update(parts[k], w_sh[k], m_sh[k], v_sh[k], name="adamw_" + k, tr=tiles[k],
                                  transposed=k in TRANSPOSED) for k in keys}
    mats = [{k: updated[k][i] for k in keys} for i in range(4)]

    lb_row = jnp.concatenate([gv["lb"], jnp.zeros_like(gv["lb"])], axis=1)
    part = _vector_rows([gv["ffn1"], gv["mix"], gv["ffn2"], gv["final"], jnp.concatenate([gv["sb"], gv["hg"]], axis=1),
                         lb_row, None, loss_row])
    vec_w = _vector_rows(vectors(ffn1_norm, mix_norm, sb_out_norm, hg_lower_bound_logits, hg_out_norm, ffn2_norm, final_norm))
    vec_m = _vector_rows(vectors(m_ffn1_norm, m_mix_norm, m_sb_out_norm, m_hg_lower_bound_logits, m_hg_out_norm,
                                 m_ffn2_norm, m_final_norm))
    vec_v = _vector_rows(vectors(v_ffn1_norm, v_mix_norm, v_sb_out_norm, v_hg_lower_bound_logits, v_hg_out_norm,
                                 v_ffn2_norm, v_final_norm))
    *vecs, loss_out = _vectors_update(part, vec_w, vec_m, vec_v, name="vectors_update")

    def leaves(mat, vec):
        half = D_MODEL // 2
        return (
            vec[0:1], mat["g1t"], mat["u1t"], mat["d1"], vec[1:2], mat["in"], vec[4:5, :half],
            vec[ROW_LOGITS].reshape(2, half), vec[4:5, half:], mat["out"], vec[2:3], mat["g2t"], mat["u2t"],
            mat["d2"], vec[3],
        )

    out = [loss_out[0, 0], grad_x[None]]
    for mat, vec in zip(mats, vecs):
        out.extend(leaves(mat, vec))
    return tuple(out)
```

```python
import jax
import jax.numpy as jnp
from jax import lax
from jax.experimental import pallas as pl
from jax.experimental.pallas import tpu as pltpu

F32, BF16 = jnp.float32, jnp.bfloat16
D_MODEL = 1024
D_FF = 2816
SB_WIDTH = 512
HG_WIDTH = 512
SB_HEAD_DIM = 64
HG_HEAD_DIM = 128
IN_COLS = 3584
EPS = 1e-6
N_DEV = 8
FF_SHARD = D_FF // N_DEV
IN_SHARD = IN_COLS // N_DEV
OUT_SHARD = D_MODEL // N_DEV
LANES = 128
HG_CHUNK = 16
VMEM_LIMIT_BYTES = 48 * 1024 * 1024
ADAM_LR, ADAM_B1, ADAM_B2, ADAM_EPS, ADAM_WD, ADAM_STEP = 0.001, 0.9, 0.999, 1e-08, 0.01, 10
MESH = pl.DeviceIdType.MESH


def _params(*semantics):
    return pltpu.CompilerParams(dimension_semantics=semantics, vmem_limit_bytes=VMEM_LIMIT_BYTES)


def _dot(a, b):
    return jnp.dot(a, b, preferred_element_type=F32)


def _dot_nt(a, b):
    return lax.dot_general(a, b, (((1,), (1,)), ((), ())), preferred_element_type=F32)


def _dot_tn(a, b):
    return lax.dot_general(a, b, (((0,), (0,)), ((), ())), preferred_element_type=F32)


def _split3(x):
    hi = x.astype(BF16)
    r1 = x - hi.astype(F32)
    mid = r1.astype(BF16)
    lo = (r1 - mid.astype(F32)).astype(BF16)
    return hi, mid, lo


def _rms(xv):
    rstd = lax.rsqrt(jnp.mean(xv * xv, axis=-1, keepdims=True) + EPS)
    return xv * rstd, rstd


def _sigmoid(x):
    return 1.0 / (1.0 + jnp.exp(-x))


def _mm(a, b, *, name, tm, tn, nt=False, out_dtype=F32, tie=None):
    m, k = a.shape
    n = b.shape[0] if nt else b.shape[1]
    assert m % tm == 0 and n % tn == 0, (name, a.shape, b.shape, tm, tn)

    def body(a_ref, b_ref, *rest):
        av = a_ref[...].astype(BF16)
        bv = b_ref[...].astype(BF16)
        rest[-1][...] = (_dot_nt(av, bv) if nt else _dot(av, bv)).astype(out_dtype)

    in_specs = [
        pl.BlockSpec((tm, k), lambda i, j: (i, 0)),
        pl.BlockSpec((tn, k), lambda i, j: (j, 0)) if nt else pl.BlockSpec((k, tn), lambda i, j: (0, j)),
    ]
    operands = [a, b]
    if tie is not None:
        in_specs.append(pl.BlockSpec(memory_space=pl.ANY))
        operands.append(tie)
    return pl.pallas_call(
        body,
        name=name,
        grid=(m // tm, n // tn),
        in_specs=in_specs,
        out_specs=pl.BlockSpec((tm, tn), lambda i, j: (i, j)),
        out_shape=jax.ShapeDtypeStruct((m, n), out_dtype),
        compiler_params=_params("parallel", "parallel"),
    )(*operands)


def _ffn_fwd(x, gain, wgt, wut, wd, *, name, tm=512, tf=256):
    t = x.shape[0]
    nj = D_FF // tf

    def body(x_ref, g_ref, wg_ref, wu_ref, wd_ref, xo_ref, a_ref, b_ref, h_ref, st_ref, acc):
        j = pl.program_id(1)

        @pl.when(j == 0)
        def _():
            xhat, _ = _rms(x_ref[...])
            h_ref[...] = (xhat * g_ref[...]).astype(BF16)
            acc[...] = jnp.zeros_like(acc)

        h = h_ref[...]
        a = _dot_nt(h, wg_ref[...])
        b = _dot_nt(h, wu_ref[...])
        a_ref[...] = a.astype(BF16)
        b_ref[...] = b.astype(BF16)
        s = a * _sigmoid(a) * b
        st_ref[...] = s.T.astype(BF16)
        acc[...] += _dot(s.astype(BF16), wd_ref[...])

        @pl.when(j == nj - 1)
        def _():
            xo_ref[...] = x_ref[...] + 0.5 * acc[...]

    return pl.pallas_call(
        body,
        name=name,
        grid=(t // tm, nj),
        in_specs=[
            pl.BlockSpec((tm, D_MODEL), lambda i, j: (i, 0)),
            pl.BlockSpec((1, D_MODEL), lambda i, j: (0, 0)),
            pl.BlockSpec((tf, D_MODEL), lambda i, j: (j, 0)),
            pl.BlockSpec((tf, D_MODEL), lambda i, j: (j, 0)),
            pl.BlockSpec((tf, D_MODEL), lambda i, j: (j, 0)),
        ],
        out_specs=[
            pl.BlockSpec((tm, D_MODEL), lambda i, j: (i, 0)),
            pl.BlockSpec((tm, tf), lambda i, j: (i, j)),
            pl.BlockSpec((tm, tf), lambda i, j: (i, j)),
            pl.BlockSpec((tm, D_MODEL), lambda i, j: (i, 0)),
            pl.BlockSpec((tf, tm), lambda i, j: (j, i)),
        ],
        out_shape=[
            jax.ShapeDtypeStruct((t, D_MODEL), F32),
            jax.ShapeDtypeStruct((t, D_FF), BF16),
            jax.ShapeDtypeStruct((t, D_FF), BF16),
            jax.ShapeDtypeStruct((t, D_MODEL), BF16),
            jax.ShapeDtypeStruct((D_FF, t), BF16),
        ],
        scratch_shapes=[pltpu.VMEM((tm, D_MODEL), F32)],
        compiler_params=_params("parallel", "arbitrary"),
    )(x, gain, wgt, wut, wd)


def _ffn_bwd(dout, x, gain, a, b, wgt, wut, wd, *, name, tm=512, tf=256):
    t = x.shape[0]
    nj = D_FF // tf

    def body(do_ref, x_ref, g_ref, a_ref, b_ref, wg_ref, wu_ref, wd_ref, dx_ref, dg_ref, da_ref, db_ref, dob_ref,
             dob_scr, dh):
        i = pl.program_id(0)
        j = pl.program_id(1)

        @pl.when(j == 0)
        def _():
            d = (0.5 * do_ref[...]).astype(BF16)
            dob_scr[...] = d
            dob_ref[...] = d
            dh[...] = jnp.zeros_like(dh)

        ds = _dot_nt(dob_scr[...], wd_ref[...])
        av = a_ref[...].astype(F32)
        bv = b_ref[...].astype(F32)
        sig = _sigmoid(av)
        dbv = ds * (av * sig)
        dav = ds * bv * (sig * (1.0 + av * (1.0 - sig)))
        da_ref[...] = dav.T.astype(BF16)
        db_ref[...] = dbv.T.astype(BF16)
        dh[...] += _dot(dav.astype(BF16), wg_ref[...]) + _dot(dbv.astype(BF16), wu_ref[...])

        @pl.when(j == nj - 1)
        def _():
            xhat, rstd = _rms(x_ref[...])
            dhv = dh[...]
            part = jnp.sum(dhv * xhat, axis=0, keepdims=True)

            @pl.when(i == 0)
            def _():
                dg_ref[...] = part

            @pl.when(i > 0)
            def _():
                dg_ref[...] += part

            dxh = dhv * g_ref[...]
            dx_ref[...] = do_ref[...] + rstd * (dxh - xhat * jnp.mean(dxh * xhat, axis=-1, keepdims=True))

    return pl.pallas_call(
        body,
        name=name,
        grid=(t // tm, nj),
        in_specs=[
            pl.BlockSpec((tm, D_MODEL), lambda i, j: (i, 0)),
            pl.BlockSpec((tm, D_MODEL), lambda i, j: (i, 0)),
            pl.BlockSpec((1, D_MODEL), lambda i, j: (0, 0)),
            pl.BlockSpec((tm, tf), lambda i, j: (i, j)),
            pl.BlockSpec((tm, tf), lambda i, j: (i, j)),
            pl.BlockSpec((tf, D_MODEL), lambda i, j: (j, 0)),
            pl.BlockSpec((tf, D_MODEL), lambda i, j: (j, 0)),
            pl.BlockSpec((tf, D_MODEL), lambda i, j: (j, 0)),
        ],
        out_specs=[
            pl.BlockSpec((tm, D_MODEL), lambda i, j: (i, 0)),
            pl.BlockSpec((1, D_MODEL), lambda i, j: (0, 0)),
            pl.BlockSpec((tf, tm), lambda i, j: (j, i)),
            pl.BlockSpec((tf, tm), lambda i, j: (j, i)),
            pl.BlockSpec((tm, D_MODEL), lambda i, j: (i, 0)),
        ],
        out_shape=[
            jax.ShapeDtypeStruct((t, D_MODEL), F32),
            jax.ShapeDtypeStruct((1, D_MODEL), F32),
            jax.ShapeDtypeStruct((D_FF, t), BF16),
            jax.ShapeDtypeStruct((D_FF, t), BF16),
            jax.ShapeDtypeStruct((t, D_MODEL), BF16),
        ],
        scratch_shapes=[pltpu.VMEM((tm, D_MODEL), BF16), pltpu.VMEM((tm, D_MODEL), F32)],
        compiler_params=_params("arbitrary", "arbitrary"),
    )(dout, x, gain, a, b, wgt, wut, wd)


def _norm_fwd(x, gain, *, name, tm=512):
    t = x.shape[0]

    def body(x_ref, g_ref, h_ref, ht_ref):
        xhat, _ = _rms(x_ref[...])
        h = xhat * g_ref[...]
        h_ref[...] = h.astype(BF16)
        ht_ref[...] = h.T.astype(BF16)

    return pl.pallas_call(
        body,
        name=name,
        grid=(t // tm,),
        in_specs=[pl.BlockSpec((tm, D_MODEL), lambda i: (i, 0)), pl.BlockSpec((1, D_MODEL), lambda i: (0, 0))],
        out_specs=[pl.BlockSpec((tm, D_MODEL), lambda i: (i, 0)), pl.BlockSpec((D_MODEL, tm), lambda i: (0, i))],
        out_shape=[jax.ShapeDtypeStruct((t, D_MODEL), BF16), jax.ShapeDtypeStruct((D_MODEL, t), BF16)],
        compiler_params=_params("parallel"),
    )(x, gain)


def _norm_bwd(dh, x, gain, dres, *, name, tm=512):
    t = x.shape[0]

    def body(dh_ref, x_ref, g_ref, dr_ref, dx_ref, dg_ref):
        i = pl.program_id(0)
        xhat, rstd = _rms(x_ref[...])
        dhv = dh_ref[...]
        part = jnp.sum(dhv * xhat, axis=0, keepdims=True)

        @pl.when(i == 0)
        def _():
            dg_ref[...] = part

        @pl.when(i > 0)
        def _():
            dg_ref[...] += part

        dxh = dhv * g_ref[...]
        dx_ref[...] = dr_ref[...] + rstd * (dxh - xhat * jnp.mean(dxh * xhat, axis=-1, keepdims=True))

    row = pl.BlockSpec((tm, D_MODEL), lambda i: (i, 0))
    vec = pl.BlockSpec((1, D_MODEL), lambda i: (0, 0))
    return pl.pallas_call(
        body,
        name=name,
        grid=(t // tm,),
        in_specs=[row, row, vec, row],
        out_specs=[row, vec],
        out_shape=[jax.ShapeDtypeStruct((t, D_MODEL), F32), jax.ShapeDtypeStruct((1, D_MODEL), F32)],
        compiler_params=_params("arbitrary"),
    )(dh, x, gain, dres)


ATT_Q_TILE = 512
ATT_K_BLOCK = 256


def _first_head_lanes():
    return lax.broadcasted_iota(jnp.int32, (1, LANES), 1) < SB_HEAD_DIM


def _stack_heads(x):
    first = _first_head_lanes()
    return jnp.concatenate([jnp.where(first, x, 0.0), jnp.where(first, 0.0, x)], axis=0)


def _unstack_heads(x, rows):
    return jnp.where(_first_head_lanes(), x[:rows], x[rows:])


def _tri(n, relation):
    r = lax.broadcasted_iota(jnp.int32, (n, n), 0)
    c = lax.broadcasted_iota(jnp.int32, (n, n), 1)
    return relation(r, c).astype(BF16)


def _scan_dot(x, tri3):
    return _dot(jnp.concatenate(_split3(x), axis=1), tri3)


def _log_terms(z):
    sp = jnp.log(1.0 + jnp.exp(-jnp.abs(z)))
    return jnp.minimum(z, 0.0) - sp, -jnp.maximum(z, 0.0) - sp


def _attn_fwd(proj, *, name):
    t = proj.shape[0]
    tq, tk = ATT_Q_TILE, ATT_K_BLOCK
    diag = tq // tk
    n_pairs = SB_WIDTH // LANES

    def body(q_ref, k_ref, v_ref, o_ref, l_ref):
        qi = pl.program_id(1)
        q = q_ref[...] * (SB_HEAD_DIM ** -0.5)
        qs = _stack_heads(q).astype(BF16)
        tri = _tri(tk, lambda j, s: j > s)
        tri3 = jnp.concatenate([tri, tri, tri], axis=0)
        trow = lax.broadcasted_iota(jnp.int32, (tq, tk), 0)
        scol = lax.broadcasted_iota(jnp.int32, (tq, tk), 1)

        def block(off, carry, causal):
            acc, c = carry
            z = _dot_nt(qs, k_ref[pl.ds(off, tk), :].astype(BF16))
            lbeta, lrest = _log_terms(z)
            if causal is not None:
                lrest = jnp.where(causal, lrest, 0.0)
            w = jnp.exp(lbeta + (_scan_dot(lrest, tri3) + c))
            if causal is not None:
                w = jnp.where(causal, w, 0.0)
            acc = acc + _dot(w.astype(BF16), v_ref[pl.ds(off, tk), :].astype(BF16))
            return acc, c + jnp.sum(lrest, axis=1, keepdims=True)

        carry = (jnp.zeros((2 * tq, LANES), F32), jnp.zeros((2 * tq, 1), F32))
        for j in reversed(range(diag)):
            off = pl.multiple_of(qi * tq + j * tk, tk)
            mask = (scol + j * tk) < trow
            carry = block(off, carry, jnp.concatenate([mask, mask], axis=0))
        n_full = qi * diag

        def step(it, carry):
            return block(pl.multiple_of((n_full - 1 - it) * tk, tk), carry, None)

        acc, c = lax.fori_loop(0, n_full, step, carry)
        o_ref[...] = _unstack_heads(acc, tq)
        l_ref[...] = _unstack_heads(jnp.broadcast_to(c, (2 * tq, LANES)), tq)

    return pl.pallas_call(
        body,
        name=name,
        grid=(n_pairs, t // tq),
        in_specs=[
            pl.BlockSpec((tq, LANES), lambda p, i: (i, p)),
            pl.BlockSpec((t, LANES), lambda p, i: (0, n_pairs + p)),
            pl.BlockSpec((t, LANES), lambda p, i: (0, 2 * n_pairs + p)),
        ],
        out_specs=[pl.BlockSpec((tq, LANES), lambda p, i: (i, p))] * 2,
        out_shape=[jax.ShapeDtypeStruct((t, SB_WIDTH), F32)] * 2,
        compiler_params=_params("parallel", "parallel"),
    )(proj, proj, proj)


def _attn_bwd(proj, ltot, do, *, name, tie=None):
    t = proj.shape[0]
    tq, tk = ATT_Q_TILE, ATT_K_BLOCK
    diag = tq // tk
    n_pairs = SB_WIDTH // LANES
    scale = SB_HEAD_DIM ** -0.5

    def body(q_ref, k_ref, v_ref, l_ref, do_ref, *rest):
        dq_ref, dk_ref, dv_ref = rest[-3:]
        qi = pl.program_id(1)

        @pl.when(qi == 0)
        def _():
            dk_ref[...] = jnp.zeros_like(dk_ref)
            dv_ref[...] = jnp.zeros_like(dv_ref)

        q = q_ref[...] * scale
        lt = l_ref[...]
        qs = _stack_heads(q).astype(BF16)
        dos = _stack_heads(do_ref[...]).astype(BF16)
        first = _first_head_lanes()
        total = jnp.concatenate([jnp.max(jnp.where(first, lt, -jnp.inf), axis=1, keepdims=True),
                                 jnp.max(jnp.where(first, -jnp.inf, lt), axis=1, keepdims=True)], axis=0)
        upto = _tri(tk, lambda j, s: j <= s)
        upto3 = jnp.concatenate([upto, upto, upto], axis=0)
        before = _tri(tk, lambda s, j: s < j)
        before3 = jnp.concatenate([before, before, before], axis=0)
        trow = lax.broadcasted_iota(jnp.int32, (tq, tk), 0)
        scol = lax.broadcasted_iota(jnp.int32, (tq, tk), 1)

        def block(off, carry, causal):
            dq, cl, cg = carry
            kblk = k_ref[pl.ds(off, tk), :].astype(BF16)
            vblk = v_ref[pl.ds(off, tk), :].astype(BF16)
            z = _dot_nt(qs, kblk)
            lbeta, lrest = _log_terms(z)
            if causal is not None:
                lrest = jnp.where(causal, lrest, 0.0)
            w = jnp.exp(lbeta + (total - (_scan_dot(lrest, upto3) + cl)))
            if causal is not None:
                w = jnp.where(causal, w, 0.0)
            g = w * _dot_nt(dos, vblk)
            prior = _scan_dot(g, before3) + cg
            sig = jnp.exp(lbeta)
            dz = g * (1.0 - sig) - prior * sig
            if causal is not None:
                dz = jnp.where(causal, dz, 0.0)
            dq = dq + _dot(dz.astype(BF16), kblk)
            dk_ref[pl.ds(off, tk), :] += _dot(dz.T.astype(BF16), qs)
            dv_ref[pl.ds(off, tk), :] += _dot(w.T.astype(BF16), dos)
            return dq, cl + jnp.sum(lrest, axis=1, keepdims=True), cg + jnp.sum(g, axis=1, keepdims=True)

        def step(kb, carry):
            return block(pl.multiple_of(kb * tk, tk), carry, None)

        zero = jnp.zeros((2 * tq, 1), F32)
        carry = lax.fori_loop(0, qi * diag, step, (jnp.zeros((2 * tq, LANES), F32), zero, zero))
        for j in range(diag):
            off = pl.multiple_of(qi * tq + j * tk, tk)
            mask = (scol + j * tk) < trow
            carry = block(off, carry, jnp.concatenate([mask, mask], axis=0))
        dq_ref[...] = _unstack_heads(carry[0], tq) * scale

    tile_spec = pl.BlockSpec((tq, LANES), lambda p, i: (i, p))
    full_spec = pl.BlockSpec((t, LANES), lambda p, i: (0, p))
    return pl.pallas_call(
        body,
        name=name,
        grid=(n_pairs, t // tq),
        in_specs=[
            tile_spec,
            pl.BlockSpec((t, LANES), lambda p, i: (0, n_pairs + p)),
            pl.BlockSpec((t, LANES), lambda p, i: (0, 2 * n_pairs + p)),
            tile_spec,
            tile_spec,
        ] + ([] if tie is None else [pl.BlockSpec(memory_space=pl.ANY)]),
        out_specs=[tile_spec, full_spec, full_spec],
        out_shape=[jax.ShapeDtypeStruct((t, SB_WIDTH), F32)] * 3,
        compiler_params=_params("arbitrary", "arbitrary"),
    )(proj, proj, proj, ltot, do, *([] if tie is None else [tie]))


HG_BLOCK = 256
HG_HEADS = HG_WIDTH // HG_HEAD_DIM


def _chunk_mats(n):
    r = lax.broadcasted_iota(jnp.int32, (n, n), 0)
    c = lax.broadcasted_iota(jnp.int32, (n, n), 1)
    same = (r // HG_CHUNK) == (c // HG_CHUNK)
    upto = (same & (c <= r)).astype(BF16)
    whole = same.astype(BF16)
    onward = (same & (c >= r)).astype(BF16)
    return upto, whole, onward


def _rows_dot(mat, x):
    return _dot(jnp.concatenate([mat, mat, mat], axis=1), jnp.concatenate(_split3(x), axis=0))


def _lower_bound(lg_ref):
    lg = lg_ref[...]
    return _sigmoid(lg[0:1, :] - lg[1:2, :])


def _hgrn_prepare(q_ref, f_ref, lb, h, upto, whole):
    cols = slice(h * HG_HEAD_DIM, (h + 1) * HG_HEAD_DIM)
    lbh = lb[:, cols]
    sg = _sigmoid(f_ref[:, cols])
    forget = lbh + (1.0 - lbh) * sg
    logf = jnp.log(forget)
    kk = (1.0 - lbh) * (1.0 - sg)
    qv = q_ref[:, cols]
    qsig = _sigmoid(qv)
    qh = qv * qsig
    b = _rows_dot(upto, logf)
    blast = _rows_dot(whole, logf)
    return dict(lbh=lbh, sg=sg, forget=forget, kk=kk, qv=qv, qsig=qsig, qh=qh, b=b, eb=jnp.exp(b),
                ekb=jnp.exp(blast - b), dl=jnp.exp(blast))


def _hgrn_fwd(proj, logits, *, name):
    t = proj.shape[0]
    tb = HG_BLOCK
    nc = tb // HG_CHUNK
    hd = HG_HEAD_DIM

    def body(q_ref, f_ref, i_ref, lg_ref, o_ref, st_ref, state, qh_s, kk_s, b_s, qe_s, ke_s, dl_s):
        @pl.when(pl.program_id(0) == 0)
        def _():
            state[...] = jnp.zeros_like(state)

        lb = _lower_bound(lg_ref)
        upto, whole, _ = _chunk_mats(tb)
        for h in range(HG_HEADS):
            p = _hgrn_prepare(q_ref, f_ref, lb, h, upto, whole)
            qh_s[h] = p["qh"]
            kk_s[h] = p["kk"]
            b_s[h] = p["b"]
            qe_s[h] = (p["qh"] * p["eb"]).astype(BF16)
            ke_s[h] = (p["kk"] * p["ekb"]).astype(BF16)
            dl_s[h] = p["dl"]
        rowi = lax.broadcasted_iota(jnp.int32, (HG_CHUNK, hd), 0)

        def chunk(c, _):
            r0 = pl.multiple_of(c * HG_CHUNK, HG_CHUNK)
            rows = pl.ds(r0, HG_CHUNK)
            for h in range(HG_HEADS):
                cols = slice(h * hd, (h + 1) * hd)
                bc = b_s[h, rows, :]
                qc = qh_s[h, rows, :]
                kc = kk_s[h, rows, :]
                vc = i_ref[rows, cols]
                s_in = state[h]
                st_ref[c, h] = s_in
                o = _dot_nt(qe_s[h, rows, :], s_in.astype(BF16))
                for s in range(HG_CHUNK):
                    pair = jnp.where(rowi >= s, qc * jnp.exp(bc - bc[s:s + 1, :]) * kc[s:s + 1, :], 0.0)
                    o = o + jnp.sum(pair, axis=1, keepdims=True) * vc[s:s + 1, :]
                o_ref[rows, cols] = o
                state[h] = s_in * dl_s[h, pl.ds(r0, 1), :] + _dot_tn(vc.astype(BF16), ke_s[h, rows, :])
            return 0

        lax.fori_loop(0, nc, chunk, 0)

    blk = lambda col: pl.BlockSpec((tb, HG_WIDTH), lambda i: (i, col))
    head_f32 = pltpu.VMEM((HG_HEADS, tb, hd), F32)
    head_bf16 = pltpu.VMEM((HG_HEADS, tb, hd), BF16)
    return pl.pallas_call(
        body,
        name=name,
        grid=(t // tb,),
        in_specs=[blk(3), blk(4), blk(5), pl.BlockSpec((2, HG_WIDTH), lambda i: (0, 0))],
        out_specs=[
            pl.BlockSpec((tb, HG_WIDTH), lambda i: (i, 0)),
            pl.BlockSpec((nc, HG_HEADS, hd, hd), lambda i: (i, 0, 0, 0)),
        ],
        out_shape=[
            jax.ShapeDtypeStruct((t, HG_WIDTH), F32),
            jax.ShapeDtypeStruct((t // HG_CHUNK, HG_HEADS, hd, hd), F32),
        ],
        scratch_shapes=[pltpu.VMEM((HG_HEADS, hd, hd), F32), head_f32, head_f32, head_f32, head_bf16, head_bf16,
                        head_f32],
        compiler_params=_params("arbitrary"),
    )(proj, proj, proj, logits)


def _hgrn_bwd(proj, logits, states, do, *, name):
    t = proj.shape[0]
    tb = HG_BLOCK
    nb = t // tb
    nc = tb // HG_CHUNK
    hd = HG_HEAD_DIM

    def body(q_ref, f_ref, i_ref, lg_ref, st_ref, do_ref, dq_ref, df_ref, di_ref, dlb_ref,
             dstate, qh_s, kk_s, b_s, eb_s, ekb_s, qe_s, ke_s, dl_s, dqh_s, dkk_s, dlf_s):
        step = pl.program_id(0)

        @pl.when(step == 0)
        def _():
            dstate[...] = jnp.zeros_like(dstate)
            dlb_ref[...] = jnp.zeros_like(dlb_ref)

        lb = _lower_bound(lg_ref)
        upto, whole, _ = _chunk_mats(tb)
        prepared = []
        for h in range(HG_HEADS):
            p = _hgrn_prepare(q_ref, f_ref, lb, h, upto, whole)
            prepared.append(p)
            qh_s[h] = p["qh"]
            kk_s[h] = p["kk"]
            b_s[h] = p["b"]
            eb_s[h] = p["eb"]
            ekb_s[h] = p["ekb"]
            qe_s[h] = (p["qh"] * p["eb"]).astype(BF16)
            ke_s[h] = (p["kk"] * p["ekb"]).astype(BF16)
            dl_s[h] = p["dl"]
        rowi = lax.broadcasted_iota(jnp.int32, (HG_CHUNK, hd), 0)
        r16 = lax.broadcasted_iota(jnp.int32, (HG_CHUNK, HG_CHUNK), 0)
        c16 = lax.broadcasted_iota(jnp.int32, (HG_CHUNK, HG_CHUNK), 1)
        onward = (c16 >= r16).astype(BF16)

        def chunk(it, _):
            c = nc - 1 - it
            r0 = pl.multiple_of(c * HG_CHUNK, HG_CHUNK)
            rows = pl.ds(r0, HG_CHUNK)
            for h in range(HG_HEADS):
                cols = slice(h * hd, (h + 1) * hd)
                bc = b_s[h, rows, :]
                qc = qh_s[h, rows, :]
                kc = kk_s[h, rows, :]
                vc = i_ref[rows, cols]
                doc = do_ref[rows, cols]
                s_in = st_ref[c, h]
                ds_out = dstate[h]
                ds_out_b = ds_out.astype(BF16)
                docb = doc.astype(BF16)
                dl_row = dl_s[h, pl.ds(r0, 1), :]
                dqh = _dot(docb, s_in.astype(BF16)) * eb_s[h, rows, :]
                dkk = _dot(vc.astype(BF16), ds_out_b) * ekb_s[h, rows, :]
                dv = _dot_nt(ke_s[h, rows, :], ds_out_b)
                db = dqh * qc - dkk * kc
                dwhole = jnp.sum(dkk * kc, axis=0, keepdims=True) + jnp.sum(ds_out * s_in, axis=0, keepdims=True) * dl_row
                for s in range(HG_CHUNK):
                    keep = rowi >= s
                    at_s = rowi == s
                    e = jnp.exp(bc - bc[s:s + 1, :])
                    k_row = kc[s:s + 1, :]
                    pcol = jnp.sum(jnp.where(keep, qc * e * k_row, 0.0), axis=1, keepdims=True)
                    dpcol = jnp.sum(doc * vc[s:s + 1, :], axis=1, keepdims=True)
                    m = jnp.where(keep, e * dpcol, 0.0)
                    mk = m * k_row
                    dk_row = jnp.sum(m * qc, axis=0, keepdims=True)
                    dqh = dqh + mk
                    dkk = dkk + jnp.where(at_s, dk_row, 0.0)
                    db = db + mk * qc - jnp.where(at_s, dk_row * k_row, 0.0)
                    dv = dv + jnp.where(at_s, jnp.sum(pcol * doc, axis=0, keepdims=True), 0.0)
                dqh_s[h, rows, :] = dqh
                dkk_s[h, rows, :] = dkk
                dlf_s[h, rows, :] = _rows_dot(onward, db) + dwhole
                di_ref[rows, cols] = dv
                dstate[h] = ds_out * dl_row + _dot_tn(docb, qe_s[h, rows, :])
            return 0

        lax.fori_loop(0, nc, chunk, 0)
        for h in range(HG_HEADS):
            cols = slice(h * hd, (h + 1) * hd)
            p = prepared[h]
            dq_ref[:, cols] = dqh_s[h] * (p["qsig"] * (1.0 + p["qv"] * (1.0 - p["qsig"])))
            dforget = dlf_s[h] / p["forget"] - dkk_s[h]
            df_ref[:, cols] = dforget * (1.0 - p["lbh"]) * p["sg"] * (1.0 - p["sg"])
            dlb_ref[:, cols] += jnp.sum(dforget * (1.0 - p["sg"]), axis=0, keepdims=True)

    blk = lambda col: pl.BlockSpec((tb, HG_WIDTH), lambda i: (nb - 1 - i, col))
    vec = pl.BlockSpec((1, HG_WIDTH), lambda i: (0, 0))
    head_f32 = pltpu.VMEM((HG_HEADS, tb, hd), F32)
    head_bf16 = pltpu.VMEM((HG_HEADS, tb, hd), BF16)
    return pl.pallas_call(
        body,
        name=name,
        grid=(nb,),
        in_specs=[
            blk(3), blk(4), blk(5),
            pl.BlockSpec((2, HG_WIDTH), lambda i: (0, 0)),
            pl.BlockSpec((nc, HG_HEADS, hd, hd), lambda i: (nb - 1 - i, 0, 0, 0)),
            blk(0),
        ],
        out_specs=[blk(0), blk(0), blk(0), vec],
        out_shape=[jax.ShapeDtypeStruct((t, HG_WIDTH), F32)] * 3 + [jax.ShapeDtypeStruct((1, HG_WIDTH), F32)],
        scratch_shapes=[
            pltpu.VMEM((HG_HEADS, hd, hd), F32),
            head_f32, head_f32, head_f32, head_f32, head_f32, head_bf16, head_bf16, head_f32,
            head_f32, head_f32, head_f32,
        ],
        compiler_params=_params("arbitrary"),
    )(proj, proj, proj, logits, states, do)


def _group_mat(width, head_dim):
    r = lax.broadcasted_iota(jnp.int32, (width, width), 0)
    c = lax.broadcasted_iota(jnp.int32, (width, width), 1)
    return ((r // head_dim) == (c // head_dim)).astype(BF16)


def _head_mean(x, mat, head_dim):
    hi = x.astype(BF16)
    lo = (x - hi.astype(F32)).astype(BF16)
    return (_dot(hi, mat) + _dot(lo, mat)) * (1.0 / head_dim)


def _mix_out_fwd(o_sb, o_hg, proj, g_sb, g_hg, w_out, x1, *, name, tm=256):
    t = x1.shape[0]

    def body(osb_ref, ohg_ref, gate_ref, gsb_ref, ghg_ref, w_ref, x_ref, xo_ref, mt_ref):
        msb = _group_mat(SB_WIDTH, SB_HEAD_DIM)
        mhg = _group_mat(HG_WIDTH, HG_HEAD_DIM)
        osb = osb_ref[...]
        ohg = ohg_ref[...]
        nsb = osb * lax.rsqrt(_head_mean(osb * osb, msb, SB_HEAD_DIM) + EPS) * gsb_ref[...]
        gate = gate_ref[...]
        nhg = ohg * lax.rsqrt(_head_mean(ohg * ohg, mhg, HG_HEAD_DIM) + EPS) * ghg_ref[...] * (gate * _sigmoid(gate))
        mixed = jnp.concatenate([nsb, nhg], axis=1)
        mt_ref[...] = mixed.T.astype(BF16)
        xo_ref[...] = x_ref[...] + _dot(mixed.astype(BF16), w_ref[...])

    half = pl.BlockSpec((tm, SB_WIDTH), lambda i: (i, 0))
    vec = pl.BlockSpec((1, SB_WIDTH), lambda i: (0, 0))
    row = pl.BlockSpec((tm, D_MODEL), lambda i: (i, 0))
    return pl.pallas_call(
        body,
        name=name,
        grid=(t // tm,),
        in_specs=[half, half, pl.BlockSpec((tm, HG_WIDTH), lambda i: (i, 6)), vec, vec,
                  pl.BlockSpec((D_MODEL, D_MODEL), lambda i: (0, 0)), row],
        out_specs=[row, pl.BlockSpec((D_MODEL, tm), lambda i: (0, i))],
        out_shape=[jax.ShapeDtypeStruct((t, D_MODEL), F32), jax.ShapeDtypeStruct((D_MODEL, t), BF16)],
        compiler_params=_params("parallel"),
    )(o_sb, o_hg, proj, g_sb, g_hg, w_out, x1)


def _mix_out_bwd(dx2, o_sb, o_hg, proj, g_sb, g_hg, w_out, *, name, tm=256):
    t = dx2.shape[0]

    def body(dx_ref, osb_ref, ohg_ref, gate_ref, gsb_ref, ghg_ref, w_ref, dosb_ref, dohg_ref, dgate_ref, dgsb_ref,
             dghg_ref, dxb_ref):
        i = pl.program_id(0)
        msb = _group_mat(SB_WIDTH, SB_HEAD_DIM)
        mhg = _group_mat(HG_WIDTH, HG_HEAD_DIM)
        dxb = dx_ref[...].astype(BF16)
        dxb_ref[...] = dxb
        dmixed = _dot_nt(dxb, w_ref[...])
        dnsb = dmixed[:, :SB_WIDTH]
        dy = dmixed[:, SB_WIDTH:]

        osb = osb_ref[...]
        rstd = lax.rsqrt(_head_mean(osb * osb, msb, SB_HEAD_DIM) + EPS)
        ohat = osb * rstd
        part_sb = jnp.sum(dnsb * ohat, axis=0, keepdims=True)
        dohat = dnsb * gsb_ref[...]
        dosb_ref[...] = rstd * (dohat - ohat * _head_mean(dohat * ohat, msb, SB_HEAD_DIM))

        ohg = ohg_ref[...]
        rstd = lax.rsqrt(_head_mean(ohg * ohg, mhg, HG_HEAD_DIM) + EPS)
        ohat = ohg * rstd
        gate = gate_ref[...]
        sig = _sigmoid(gate)
        dn = dy * (gate * sig)
        dgate_ref[...] = dy * (ohat * ghg_ref[...]) * (sig * (1.0 + gate * (1.0 - sig)))
        part_hg = jnp.sum(dn * ohat, axis=0, keepdims=True)
        dohat = dn * ghg_ref[...]
        dohg_ref[...] = rstd * (dohat - ohat * _head_mean(dohat * ohat, mhg, HG_HEAD_DIM))

        @pl.when(i == 0)
        def _():
            dgsb_ref[...] = part_sb
            dghg_ref[...] = part_hg

        @pl.when(i > 0)
        def _():
            dgsb_ref[...] += part_sb
            dghg_ref[...] += part_hg

    half = pl.BlockSpec((tm, SB_WIDTH), lambda i: (i, 0))
    vec = pl.BlockSpec((1, SB_WIDTH), lambda i: (0, 0))
    row = pl.BlockSpec((tm, D_MODEL), lambda i: (i, 0))
    return pl.pallas_call(
        body,
        name=name,
        grid=(t // tm,),
        in_specs=[row, half, half, pl.BlockSpec((tm, HG_WIDTH), lambda i: (i, 6)), vec, vec,
                  pl.BlockSpec((D_MODEL, D_MODEL), lambda i: (0, 0))],
        out_specs=[half, half, half, vec, vec, row],
        out_shape=[jax.ShapeDtypeStruct((t, SB_WIDTH), F32)] * 3 + [jax.ShapeDtypeStruct((1, SB_WIDTH), F32)] * 2
        + [jax.ShapeDtypeStruct((t, D_MODEL), BF16)],
        compiler_params=_params("arbitrary"),
    )(dx2, o_sb, o_hg, proj, g_sb, g_hg, w_out)


def _loss_head(x3, gain, target, *, name, tm=512):
    t = x3.shape[0]

    def body(x_ref, g_ref, y_ref, dx_ref, dg_ref, loss_ref):
        i = pl.program_id(0)
        xhat, rstd = _rms(x_ref[...])
        err = xhat * g_ref[...] - y_ref[...]
        part_loss = 0.5 * jnp.sum(jnp.mean(err * err, axis=-1, keepdims=True), axis=0, keepdims=True)
        dy = err * (1.0 / D_MODEL)
        part_g = jnp.sum(dy * xhat, axis=0, keepdims=True)

        @pl.when(i == 0)
        def _():
            dg_ref[...] = part_g
            loss_ref[...] = jnp.broadcast_to(part_loss, loss_ref.shape)

        @pl.when(i > 0)
        def _():
            dg_ref[...] += part_g
            loss_ref[...] += jnp.broadcast_to(part_loss, loss_ref.shape)

        dxh = dy * g_ref[...]
        dx_ref[...] = rstd * (dxh - xhat * jnp.mean(dxh * xhat, axis=-1, keepdims=True))

    row = pl.BlockSpec((tm, D_MODEL), lambda i: (i, 0))
    vec = pl.BlockSpec((1, D_MODEL), lambda i: (0, 0))
    return pl.pallas_call(
        body,
        name=name,
        grid=(t // tm,),
        in_specs=[row, vec, row],
        out_specs=[row, vec, vec],
        out_shape=[jax.ShapeDtypeStruct((t, D_MODEL), F32), jax.ShapeDtypeStruct((1, D_MODEL), F32),
                   jax.ShapeDtypeStruct((1, D_MODEL), F32)],
        compiler_params=_params("arbitrary"),
    )(x3, gain, target)


def _local_step(x, target, norms, logits, w, weights_after=None, grads_ready=None):
    w = dict(w)
    x1, a1, b1, h1, s1t = _ffn_fwd(x, norms["ffn1"], w["g1t"], w["u1t"], w["d1"], name="ffn1_fwd")
    if weights_after is not None:
        w.update(weights_after("ffn1", x1))
    hm, hmt = _norm_fwd(x1, norms["mix"], name="mix_norm_fwd")
    proj = _mm(hm, w["in"], name="in_proj", tm=512, tn=512)
    o_sb, ltot = _attn_fwd(proj, name="sb_attn_fwd")
    o_hg, states = _hgrn_fwd(proj, logits, name="hgrn2_fwd")
    x2, mixed_t = _mix_out_fwd(o_sb, o_hg, proj, norms["sb"], norms["hg"], w["out"], x1, name="mix_out_fwd")
    if weights_after is not None:
        w.update(weights_after("mix", x2))
    x3, a2, b2, h2, s2t = _ffn_fwd(x2, norms["ffn2"], w["g2t"], w["u2t"], w["d2"], name="ffn2_fwd")
    dx3, d_final, loss_row = _loss_head(x3, norms["final"], target, name="loss_head")

    def weight_grad(lhs_t, rhs, name, tie=None):
        return _mm(lhs_t, rhs, name=name, tm=256, tn=D_MODEL, out_dtype=BF16, tie=tie)

    def sent(stage):
        return grads_ready(stage, gw) if grads_ready is not None else None

    gw, gv = {}, {"final": d_final}
    dx2, gv["ffn2"], da2t, db2t, dob2 = _ffn_bwd(dx3, x2, norms["ffn2"], a2, b2, w["g2t"], w["u2t"], w["d2"],
                                                 name="ffn2_bwd")
    gw["g2t"] = weight_grad(da2t, h2, "ffn2_dgate")
    gw["u2t"] = weight_grad(db2t, h2, "ffn2_dup")
    gw["d2"] = weight_grad(s2t, dob2, "ffn2_ddown")

    do_sb, do_hg, d_gate, gv["sb"], gv["hg"], dx2b = _mix_out_bwd(
        dx2, o_sb, o_hg, proj, norms["sb"], norms["hg"], w["out"], name="mix_out_bwd")
    gw["out"] = weight_grad(mixed_t, dx2b, "out_dw")
    tie = sent("mix")
    dq_sb, dk_sb, dv_sb = _attn_bwd(proj, ltot, do_sb, name="sb_attn_bwd", tie=tie)
    dq_hg, df_hg, di_hg, d_lb = _hgrn_bwd(proj, logits if tie is None else logits + tie[0, 0], states, do_hg,
                                          name="hgrn2_bwd")
    dproj = jnp.concatenate([dq_sb, dk_sb, dv_sb, dq_hg, df_hg, di_hg, d_gate], axis=1).astype(BF16)
    gw["in"] = _mm(hmt, dproj, name="in_dw", tm=D_MODEL, tn=256)
    tie = sent("in")
    dhm = _mm(dproj, w["in"], name="in_dx", tm=512, tn=D_MODEL, nt=True)
    dx1, gv["mix"] = _norm_bwd(dhm, x1, norms["mix"] if tie is None else norms["mix"] + tie[0, 0], dx2,
                               name="mix_norm_bwd")

    dx, gv["ffn1"], da1t, db1t, dob1 = _ffn_bwd(dx1, x, norms["ffn1"], a1, b1, w["g1t"], w["u1t"], w["d1"],
                                                name="ffn1_bwd")
    gw["g1t"] = weight_grad(da1t, h1, "ffn1_dgate")
    gw["u1t"] = weight_grad(db1t, h1, "ffn1_dup", tie=sent("g1t"))
    gw["d1"] = weight_grad(s1t, dob1, "ffn1_ddown", tie=sent("u1t"))
    sent("d1")
    gv["lb"] = d_lb
    return loss_row, dx, gw, gv


HBM = pl.BlockSpec(memory_space=pl.ANY)


def _place():
    return lax.axis_index("x"), lax.axis_index("y"), lax.axis_index("c")


def _slot(px, py, pc):
    return 4 * px + 2 * py + pc


def _all_gather(blocks, *, name):
    n = len(blocks)

    def body(*refs):
        ins, outs = refs[:n], refs[n:2 * n]
        send_sems, recv_sems, local_sems = refs[2 * n:]
        x, y, c = _place()
        me, sibling = (x, y, c), (x, y, 1 - c)
        chips = [(1 - x, y), (x, 1 - y), (1 - x, 1 - y)]

        def copy(a, k, block, to, src=None):
            dst = outs[a].at[_slot(*block)]
            return pltpu.make_async_remote_copy(
                src_ref=dst if src is None else src, dst_ref=dst, send_sem=send_sems.at[7 * a + k],
                recv_sem=recv_sems.at[7 * a + k], device_id=to, device_id_type=MESH)

        mine = [pltpu.make_async_copy(ins[a], outs[a].at[_slot(*me)], local_sems.at[a]) for a in range(n)]
        for cp in mine:
            cp.start()
        first = []
        for a in range(n):
            first.append(copy(a, 0, me, sibling, src=ins[a]))
            first += [copy(a, 1 + j, me, (*chip, c), src=ins[a]) for j, chip in enumerate(chips)]
        for cp in first:
            cp.start()
        passed = []
        for j, chip in enumerate(chips):
            for a in range(n):
                copy(a, 1 + j, (*chip, c), me).wait_recv()
                fwd = copy(a, 4 + j, (*chip, c), sibling)
                fwd.start()
                passed.append(fwd)
        for a in range(n):
            copy(a, 0, sibling, me).wait_recv()
            for j, chip in enumerate(chips):
                copy(a, 4 + j, (*chip, 1 - c), me).wait_recv()
        for cp in first + passed:
            cp.wait_send()
        for cp in mine:
            cp.wait()

    return pl.pallas_call(
        body,
        name=name,
        in_specs=[HBM] * n,
        out_specs=[HBM] * n,
        out_shape=[jax.ShapeDtypeStruct((N_DEV,) + b.shape, b.dtype) for b in blocks],
        scratch_shapes=[pltpu.SemaphoreType.DMA((7 * n,)), pltpu.SemaphoreType.DMA((7 * n,)),
                        pltpu.SemaphoreType.DMA((n,))],
    )(*blocks)


def _flipped(place, d):
    return tuple(1 - p if (d >> (2 - axis)) & 1 else p for axis, p in enumerate(place))


def _scatter_by_owner(stacks, *, name):
    n = len(stacks)

    def body(*refs):
        ins, outs = refs[:n], refs[n:2 * n]
        send_sems, recv_sems, local_sems = refs[2 * n:]
        me = _place()
        mine = [pltpu.make_async_copy(ins[a].at[_slot(*me)], outs[a].at[_slot(*me)], local_sems.at[a]) for a in range(n)]
        for cp in mine:
            cp.start()
        copies = []
        for d in range(1, N_DEV):
            peer = _flipped(me, d)
            for a in range(n):
                copies.append(pltpu.make_async_remote_copy(
                    src_ref=ins[a].at[_slot(*peer)], dst_ref=outs[a].at[_slot(*me)], send_sem=send_sems.at[7 * a + d - 1],
                    recv_sem=recv_sems.at[7 * a + d - 1], device_id=peer, device_id_type=MESH))
        for cp in copies:
            cp.start()
        for cp in copies:
            cp.wait()
        for cp in mine:
            cp.wait()

    return pl.pallas_call(
        body,
        name=name,
        in_specs=[HBM] * n,
        out_specs=[HBM] * n,
        out_shape=[jax.ShapeDtypeStruct(s.shape, s.dtype) for s in stacks],
        scratch_shapes=[pltpu.SemaphoreType.DMA((7 * n,)), pltpu.SemaphoreType.DMA((7 * n,)),
                        pltpu.SemaphoreType.DMA((n,))],
    )(*stacks)


SEM = pl.BlockSpec(memory_space=pltpu.SEMAPHORE)
EFFECT = pltpu.SideEffectType.DATAFLOW_SIDE_EFFECTING


def _split_copies(me, srcs, lands, send_sems, recv_sems, by_owner):
    copies = []
    for d in range(1, N_DEV):
        peer = _flipped(me, d)
        for a, (src, land) in enumerate(zip(srcs, lands)):
            copies.append(pltpu.make_async_remote_copy(
                src_ref=src.at[_slot(*peer)] if by_owner else src, dst_ref=land.at[_slot(*me)],
                send_sem=send_sems.at[7 * a + d - 1], recv_sem=recv_sems.at[7 * a + d - 1], device_id=peer,
                device_id_type=MESH))
    return copies


def _copies_start(srcs, *, name, by_owner, after=None):
    n = len(srcs)
    extra = [] if after is None else [after]
    land_shapes = [s.shape if by_owner else (N_DEV,) + s.shape for s in srcs]
    lands = [pltpu.with_memory_space_constraint(lax.empty(shape, s.dtype), pltpu.HBM) for shape, s in zip(land_shapes, srcs)]
    srcs = [pltpu.with_memory_space_constraint(s, pltpu.HBM) for s in srcs]

    def body(*refs):
        src_refs, land_refs = refs[:n], refs[n:2 * n]
        send_sems, recv_sems = refs[2 * n + len(extra)], refs[2 * n + len(extra) + 1]
        token = refs[-1]
        for cp in _split_copies(_place(), src_refs, land_refs, send_sems, recv_sems, by_owner):
            cp.start()
        token[...] = jnp.zeros_like(token)

    out = pl.pallas_call(
        body,
        name=name,
        in_specs=[HBM] * (2 * n + len(extra)),
        out_specs=[SEM, SEM] + [HBM] * (2 * n) + [pl.BlockSpec(memory_space=pltpu.VMEM)],
        out_shape=[pltpu.SemaphoreType.DMA((7 * n,)), pltpu.SemaphoreType.DMA((7 * n,))]
        + [pltpu.HBM(s.shape, s.dtype) for s in srcs] + [pltpu.HBM(shape, s.dtype) for shape, s in zip(land_shapes, srcs)]
        + [jax.ShapeDtypeStruct((8, LANES), F32)],
        input_output_aliases={i: 2 + i for i in range(2 * n)},
        compiler_params=pltpu.CompilerParams(has_side_effects=EFFECT),
    )(*srcs, *lands, *extra)
    return (out[0], out[1], out[2:2 + n], out[2 + n:2 + 2 * n]), out[-1]


def _copies_wait(started, after, *, name, by_owner):
    send_sems, recv_sems, srcs, lands = started
    n = len(srcs)

    def body(*refs):
        src_refs, land_refs = refs[:n], refs[n:2 * n]
        for cp in _split_copies(_place(), src_refs, land_refs, refs[2 * n], refs[2 * n + 1], by_owner):
            cp.wait_send()
            cp.wait_recv()

    out = pl.pallas_call(
        body,
        name=name,
        in_specs=[HBM] * (2 * n) + [SEM, SEM, HBM],
        out_specs=[HBM] * (2 * n),
        out_shape=[pltpu.HBM(s.shape, s.dtype) for s in srcs] + [pltpu.HBM(s.shape, s.dtype) for s in lands],
        input_output_aliases={i: i for i in range(2 * n)},
        compiler_params=pltpu.CompilerParams(has_side_effects=EFFECT),
    )(*srcs, *lands, send_sems, recv_sems, after)
    return out[n:]


def _with_own(lands, own, slot):
    zero = jnp.zeros((), jnp.int32)
    return [lax.dynamic_update_slice(land, o[None], (slot.astype(jnp.int32),) + (zero,) * o.ndim)
            for land, o in zip(lands, own)]


def _adamw(w, g, m, v):
    m = ADAM_B1 * m + (1.0 - ADAM_B1) * g
    v = ADAM_B2 * v + (1.0 - ADAM_B2) * (g * g)
    m_hat = m / (1.0 - ADAM_B1 ** ADAM_STEP)
    v_hat = v / (1.0 - ADAM_B2 ** ADAM_STEP)
    delta = -ADAM_LR * (m_hat / (jnp.sqrt(v_hat) + ADAM_EPS) + ADAM_WD * w)
    return delta, m, v


def _sum_and_update(parts, w, m, v, *, name, tr, transposed=False):
    _, rows, cols = w.shape
    pad = -cols % LANES

    def body(p_ref, w_ref, m_ref, v_ref, g_ref, d_ref, mo_ref, vo_ref):
        g = p_ref[0].astype(F32)
        for s in range(1, N_DEV):
            g = g + p_ref[s].astype(F32)
        if transposed:
            if pad:
                g = jnp.concatenate([g, jnp.zeros((pad, tr), F32)], axis=0)
            g = g.T[:, :cols]
        g_ref[0] = g
        d_ref[0], mo_ref[0], vo_ref[0] = _adamw(w_ref[0], g, m_ref[0], v_ref[0])

    flat = pl.BlockSpec((1, tr, cols), lambda i: (0, i, 0))
    if transposed:
        part_spec = pl.BlockSpec((N_DEV, cols, tr), lambda i: (0, 0, i))
    else:
        part_spec = pl.BlockSpec((N_DEV, tr, cols), lambda i: (0, i, 0))
    return pl.pallas_call(
        body,
        name=name,
        grid=(rows // tr,),
        in_specs=[part_spec, flat, flat, flat],
        out_specs=[flat] * 4,
        out_shape=[jax.ShapeDtypeStruct((1, rows, cols), F32)] * 4,
        compiler_params=_params("parallel"),
    )(parts, w, m, v)


VEC_ROWS = 8
ROW_LOGITS, ROW_LOSS = 5, 7


def _vectors_update(part, w, m, v, *, name):
    def body(p_ref, w_ref, m_ref, v_ref, g_ref, d_ref, mo_ref, vo_ref, loss_ref, all_ref, send_sems, recv_sems):
        me = _place()
        all_ref[_slot(*me)] = p_ref[...]
        copies = []
        for d in range(1, N_DEV):
            peer = _flipped(me, d)
            copies.append(pltpu.make_async_remote_copy(
                src_ref=p_ref, dst_ref=all_ref.at[_slot(*me)], send_sem=send_sems.at[d - 1], recv_sem=recv_sems.at[d - 1],
                device_id=peer, device_id_type=MESH))
        for cp in copies:
            cp.start()
        for cp in copies:
            cp.wait()
        total = all_ref[0]
        for s in range(1, N_DEV):
            total = total + all_ref[s]
        wv = w_ref[...]
        half = D_MODEL // 2
        lb = _sigmoid(wv[ROW_LOGITS:ROW_LOGITS + 1, :half] - wv[ROW_LOGITS:ROW_LOGITS + 1, half:])
        d_first = total[ROW_LOGITS:ROW_LOGITS + 1, :half] * lb * (1.0 - lb)
        d_logits = jnp.concatenate([d_first, -d_first], axis=1)
        rowi = lax.broadcasted_iota(jnp.int32, (VEC_ROWS, D_MODEL), 0)
        g = jnp.where(rowi == ROW_LOGITS, d_logits, jnp.where(rowi < ROW_LOGITS, total, 0.0))
        g_ref[...] = g
        d_ref[...], mo_ref[...], vo_ref[...] = _adamw(wv, g, m_ref[...], v_ref[...])
        loss_ref[...] = total[ROW_LOSS:ROW_LOSS + 1, :]

    vmem = pl.BlockSpec(memory_space=pltpu.VMEM)
    return pl.pallas_call(
        body,
        name=name,
        in_specs=[vmem] * 4,
        out_specs=[vmem] * 5,
        out_shape=[jax.ShapeDtypeStruct((VEC_ROWS, D_MODEL), F32)] * 4 + [jax.ShapeDtypeStruct((1, D_MODEL), F32)],
        scratch_shapes=[pltpu.VMEM((N_DEV, VEC_ROWS, D_MODEL), F32), pltpu.SemaphoreType.DMA((7,)),
                        pltpu.SemaphoreType.DMA((7,))],
    )(part, w, m, v)


ROW_SHARDED = ("d1", "d2", "out")
TRANSPOSED = ("g1t", "u1t", "g2t", "u2t")


def _vector_rows(rows):
    rowi = lax.broadcasted_iota(jnp.int32, (VEC_ROWS, D_MODEL), 0)
    out = jnp.zeros((VEC_ROWS, D_MODEL), F32)
    for i, r in enumerate(rows):
        if r is not None:
            out = jnp.where(rowi == i, r, out)
    return out


def kernel(x, ffn1_norm, ffn1_w_gate, ffn1_w_up, ffn1_w_down, mix_norm, w_in, sb_out_norm, hg_lower_bound_logits, hg_out_norm, w_out, ffn2_norm, ffn2_w_gate, ffn2_w_up, ffn2_w_down, final_norm, loss_target, m_ffn1_norm, m_ffn1_w_gate, m_ffn1_w_up, m_ffn1_w_down, m_mix_norm, m_w_in, m_sb_out_norm, m_hg_lower_bound_logits, m_hg_out_norm, m_w_out, m_ffn2_norm, m_ffn2_w_gate, m_ffn2_w_up, m_ffn2_w_down, m_final_norm, v_ffn1_norm, v_ffn1_w_gate, v_ffn1_w_up, v_ffn1_w_down, v_mix_norm, v_w_in, v_sb_out_norm, v_hg_lower_bound_logits, v_hg_out_norm, v_w_out, v_ffn2_norm, v_ffn2_w_gate, v_ffn2_w_up, v_ffn2_w_down, v_final_norm):
    def matrices(g1, u1, d1, win, wout, g2, u2, d2):
        return {"g1t": g1, "u1t": u1, "d1": d1, "in": win, "out": wout, "g2t": g2, "u2t": u2, "d2": d2}

    def vectors(n1, nm, nsb, lg, nhg, n2, nf):
        return [n1, nm, n2, nf.reshape(1, D_MODEL), jnp.concatenate([nsb, nhg], axis=1), lg.reshape(1, D_MODEL), None, None]

    w_sh = matrices(ffn1_w_gate, ffn1_w_up, ffn1_w_down, w_in, w_out, ffn2_w_gate, ffn2_w_up, ffn2_w_down)
    m_sh = matrices(m_ffn1_w_gate, m_ffn1_w_up, m_ffn1_w_down, m_w_in, m_w_out, m_ffn2_w_gate, m_ffn2_w_up, m_ffn2_w_down)
    v_sh = matrices(v_ffn1_w_gate, v_ffn1_w_up, v_ffn1_w_down, v_w_in, v_w_out, v_ffn2_w_gate, v_ffn2_w_up, v_ffn2_w_down)
    keys = list(w_sh)

    slot = _slot(*_place())

    def full(key, stack):
        if key == "in":
            return stack.transpose(1, 0, 2).reshape(D_MODEL, IN_COLS)
        return stack.reshape(-1, D_MODEL)

    def by_owner(key, grad):
        if key == "in":
            return grad.reshape(D_MODEL, N_DEV, IN_SHARD).transpose(1, 0, 2).astype(BF16)
        return grad.reshape(N_DEV, -1, D_MODEL)

    blocks = {k: (w_sh[k][0].T if k in TRANSPOSED else w_sh[k][0]).astype(BF16) for k in keys}
    first, mid, last = ("g1t", "u1t", "d1"), ("in", "out"), ("g2t", "u2t", "d2")
    w_first = {k: full(k, s) for k, s in zip(first, _all_gather([blocks[k] for k in first], name="gather_ffn1"))}
    flights = {}
    flights["ffn1"], token_mid = _copies_start([blocks[k] for k in mid], name="gather_mid_start", by_owner=False,
                                               after=w_first["d1"])
    flights["mix"], token_last = _copies_start([blocks[k] for k in last], name="gather_ffn2_start", by_owner=False,
                                               after=token_mid)

    def weights_after(stage, result):
        group = mid if stage == "ffn1" else last
        lands = _copies_wait(flights[stage], result, name="gather_" + stage + "_wait", by_owner=False)
        return {k: full(k, s) for k, s in zip(group, _with_own(lands, [blocks[k] for k in group], slot))}

    groups = {"mix": ("g2t", "u2t", "d2", "out"), "in": ("in",), "g1t": ("g1t",), "u1t": ("u1t",), "d1": ("d1",)}
    sent = {}

    def grads_ready(stage, gw):
        stacks = [by_owner(k, gw[k]) for k in groups[stage]]
        flight, token = _copies_start(stacks, name="grads_" + stage + "_start", by_owner=True)
        sent[stage] = (stacks, flight)
        return token

    norms = {"ffn1": ffn1_norm + token_last[0, 0], "mix": mix_norm, "sb": sb_out_norm, "hg": hg_out_norm,
             "ffn2": ffn2_norm, "final": final_norm.reshape(1, D_MODEL)}
    loss_row, grad_x, gw, gv = _local_step(x[0], loss_target[0], norms, hg_lower_bound_logits, w_first, weights_after,
                                           grads_ready)

    tiles = {"g1t": 256, "u1t": 256, "g2t": 256, "u2t": 256, "d1": 176, "d2": 176, "out": 128, "in": 256}
    updated, after = {}, grad_x
    for stage, (stacks, flight) in sent.items():
        lands = _copies_wait(flight, after, name="grads_" + stage + "_wait", by_owner=True)
        own = [lax.dynamic_index_in_dim(s, slot, keepdims=False) for s in stacks]
        for k, part in zip(groups[stage], _with_own(lands, own, slot)):
            updated[k] = _sum_and_update(part, w_sh[k], m_sh[k], v_sh[k], name="adamw_" + k, tr=tiles[k],
                                         transposed=k in TRANSPOSED)
            after = updated[k][0]
    mats = [{k: updated[k][i] for k in keys} for i in range(4)]

    lb_row = jnp.concatenate([gv["lb"], jnp.zeros_like(gv["lb"])], axis=1)
    part = _vector_rows([gv["ffn1"], gv["mix"], gv["ffn2"], gv["final"], jnp.concatenate([gv["sb"], gv["hg"]], axis=1),
                         lb_row, None, loss_row])
    vec_w = _vector_rows(vectors(ffn1_norm, mix_norm, sb_out_norm, hg_lower_bound_logits, hg_out_norm, ffn2_norm, final_norm))
    vec_m = _vector_rows(vectors(m_ffn1_norm, m_mix_norm, m_sb_out_norm, m_hg_lower_bound_logits, m_hg_out_norm,
                                 m_ffn2_norm, m_final_norm))
    vec_v = _vector_rows(vectors(v_ffn1_norm, v_mix_norm, v_sb_out_norm, v_hg_lower_bound_logits, v_hg_out_norm,
                                 v_ffn2_norm, v_final_norm))
    *vecs, loss_out = _vectors_update(part, vec_w, vec_m, vec_v, name="vectors_update")

    def leaves(mat, vec):
        half = D_MODEL // 2
        return (
            vec[0:1], mat["g1t"], mat["u1t"], mat["d1"], vec[1:2], mat["in"], vec[4:5, :half],
            vec[ROW_LOGITS].reshape(2, half), vec[4:5, half:], mat["out"], vec[2:3], mat["g2t"], mat["u2t"],
            mat["d2"], vec[3],
        )

    out = [loss_out[0, 0], grad_x[None]]
    for mat, vec in zip(mats, vecs):
        out.extend(leaves(mat, vec))
    return tuple(out)
```

```python
import jax
import jax.numpy as jnp
from jax import lax
from jax.experimental import pallas as pl
from jax.experimental.pallas import tpu as pltpu

F32, BF16 = jnp.float32, jnp.bfloat16
D_MODEL = 1024
D_FF = 2816
SB_WIDTH = 512
HG_WIDTH = 512
SB_HEAD_DIM = 64
HG_HEAD_DIM = 128
IN_COLS = 3584
EPS = 1e-6
N_DEV = 8
FF_SHARD = D_FF // N_DEV
IN_SHARD = IN_COLS // N_DEV
OUT_SHARD = D_MODEL // N_DEV
LANES = 128
HG_CHUNK = 16
VMEM_LIMIT_BYTES = 48 * 1024 * 1024
ADAM_LR, ADAM_B1, ADAM_B2, ADAM_EPS, ADAM_WD, ADAM_STEP = 0.001, 0.9, 0.999, 1e-08, 0.01, 10
MESH = pl.DeviceIdType.MESH


def _params(*semantics):
    return pltpu.CompilerParams(dimension_semantics=semantics, vmem_limit_bytes=VMEM_LIMIT_BYTES)


def _dot(a, b):
    return jnp.dot(a, b, preferred_element_type=F32)


def _dot_nt(a, b):
    return lax.dot_general(a, b, (((1,), (1,)), ((), ())), preferred_element_type=F32)


def _dot_tn(a, b):
    return lax.dot_general(a, b, (((0,), (0,)), ((), ())), preferred_element_type=F32)


def _split3(x):
    hi = x.astype(BF16)
    r1 = x - hi.astype(F32)
    mid = r1.astype(BF16)
    lo = (r1 - mid.astype(F32)).astype(BF16)
    return hi, mid, lo


def _rms(xv):
    rstd = lax.rsqrt(jnp.mean(xv * xv, axis=-1, keepdims=True) + EPS)
    return xv * rstd, rstd


def _sigmoid(x):
    return 1.0 / (1.0 + jnp.exp(-x))


def _mm(a, b, *, name, tm, tn, nt=False, ta=False, out_dtype=F32, tie=None):
    k, m = a.shape if ta else a.shape[::-1]
    n = b.shape[0] if nt else b.shape[1]
    assert m % tm == 0 and n % tn == 0 and not (nt and ta), (name, a.shape, b.shape, tm, tn)

    def body(a_ref, b_ref, *rest):
        av = a_ref[...].astype(BF16)
        bv = b_ref[...].astype(BF16)
        rest[-1][...] = (_dot_nt(av, bv) if nt else _dot_tn(av, bv) if ta else _dot(av, bv)).astype(out_dtype)

    in_specs = [
        pl.BlockSpec((k, tm), lambda i, j: (0, i)) if ta else pl.BlockSpec((tm, k), lambda i, j: (i, 0)),
        pl.BlockSpec((tn, k), lambda i, j: (j, 0)) if nt else pl.BlockSpec((k, tn), lambda i, j: (0, j)),
    ]
    operands = [a, b]
    if tie is not None:
        in_specs.append(pl.BlockSpec(memory_space=pl.ANY))
        operands.append(tie)
    return pl.pallas_call(
        body,
        name=name,
        grid=(m // tm, n // tn),
        in_specs=in_specs,
        out_specs=pl.BlockSpec((tm, tn), lambda i, j: (i, j)),
        out_shape=jax.ShapeDtypeStruct((m, n), out_dtype),
        compiler_params=_params("parallel", "parallel"),
    )(*operands)


def _ffn_fwd(x, gain, wgt, wut, wd, *, name, tm=512, tf=256):
    t = x.shape[0]
    nj = D_FF // tf

    def body(x_ref, g_ref, wg_ref, wu_ref, wd_ref, xo_ref, a_ref, b_ref, h_ref, st_ref, acc):
        j = pl.program_id(1)

        @pl.when(j == 0)
        def _():
            xhat, _ = _rms(x_ref[...])
            h_ref[...] = (xhat * g_ref[...]).astype(BF16)
            acc[...] = jnp.zeros_like(acc)

        h = h_ref[...]
        a = _dot_nt(h, wg_ref[...])
        b = _dot_nt(h, wu_ref[...])
        a_ref[...] = a.astype(BF16)
        b_ref[...] = b.astype(BF16)
        s = (a * _sigmoid(a) * b).astype(BF16)
        st_ref[...] = s
        acc[...] += _dot(s, wd_ref[...])

        @pl.when(j == nj - 1)
        def _():
            xo_ref[...] = x_ref[...] + 0.5 * acc[...]

    return pl.pallas_call(
        body,
        name=name,
        grid=(t // tm, nj),
        in_specs=[
            pl.BlockSpec((tm, D_MODEL), lambda i, j: (i, 0)),
            pl.BlockSpec((1, D_MODEL), lambda i, j: (0, 0)),
            pl.BlockSpec((tf, D_MODEL), lambda i, j: (j, 0)),
            pl.BlockSpec((tf, D_MODEL), lambda i, j: (j, 0)),
            pl.BlockSpec((tf, D_MODEL), lambda i, j: (j, 0)),
        ],
        out_specs=[
            pl.BlockSpec((tm, D_MODEL), lambda i, j: (i, 0)),
            pl.BlockSpec((tm, tf), lambda i, j: (i, j)),
            pl.BlockSpec((tm, tf), lambda i, j: (i, j)),
            pl.BlockSpec((tm, D_MODEL), lambda i, j: (i, 0)),
            pl.BlockSpec((tm, tf), lambda i, j: (i, j)),
        ],
        out_shape=[
            jax.ShapeDtypeStruct((t, D_MODEL), F32),
            jax.ShapeDtypeStruct((t, D_FF), BF16),
            jax.ShapeDtypeStruct((t, D_FF), BF16),
            jax.ShapeDtypeStruct((t, D_MODEL), BF16),
            jax.ShapeDtypeStruct((t, D_FF), BF16),
        ],
        scratch_shapes=[pltpu.VMEM((tm, D_MODEL), F32)],
        compiler_params=_params("parallel", "arbitrary"),
    )(x, gain, wgt, wut, wd)


def _ffn_bwd(dout, x, gain, a, b, wgt, wut, wd, *, name, tm=512, tf=256):
    t = x.shape[0]
    nj = D_FF // tf

    def body(do_ref, x_ref, g_ref, a_ref, b_ref, wg_ref, wu_ref, wd_ref, dx_ref, dg_ref, da_ref, db_ref, dob_ref,
             dob_scr, dh):
        i = pl.program_id(0)
        j = pl.program_id(1)

        @pl.when(j == 0)
        def _():
            d = (0.5 * do_ref[...]).astype(BF16)
            dob_scr[...] = d
            dob_ref[...] = d
            dh[...] = jnp.zeros_like(dh)

        ds = _dot_nt(dob_scr[...], wd_ref[...])
        av = a_ref[...].astype(F32)
        bv = b_ref[...].astype(F32)
        sig = _sigmoid(av)
        dbv = (ds * (av * sig)).astype(BF16)
        dav = (ds * bv * (sig * (1.0 + av * (1.0 - sig)))).astype(BF16)
        da_ref[...] = dav
        db_ref[...] = dbv
        dh[...] += _dot(dav, wg_ref[...]) + _dot(dbv, wu_ref[...])

        @pl.when(j == nj - 1)
        def _():
            xhat, rstd = _rms(x_ref[...])
            dhv = dh[...]
            part = jnp.sum(dhv * xhat, axis=0, keepdims=True)

            @pl.when(i == 0)
            def _():
                dg_ref[...] = part

            @pl.when(i > 0)
            def _():
                dg_ref[...] += part

            dxh = dhv * g_ref[...]
            dx_ref[...] = do_ref[...] + rstd * (dxh - xhat * jnp.mean(dxh * xhat, axis=-1, keepdims=True))

    return pl.pallas_call(
        body,
        name=name,
        grid=(t // tm, nj),
        in_specs=[
            pl.BlockSpec((tm, D_MODEL), lambda i, j: (i, 0)),
            pl.BlockSpec((tm, D_MODEL), lambda i, j: (i, 0)),
            pl.BlockSpec((1, D_MODEL), lambda i, j: (0, 0)),
            pl.BlockSpec((tm, tf), lambda i, j: (i, j)),
            pl.BlockSpec((tm, tf), lambda i, j: (i, j)),
            pl.BlockSpec((tf, D_MODEL), lambda i, j: (j, 0)),
            pl.BlockSpec((tf, D_MODEL), lambda i, j: (j, 0)),
            pl.BlockSpec((tf, D_MODEL), lambda i, j: (j, 0)),
        ],
        out_specs=[
            pl.BlockSpec((tm, D_MODEL), lambda i, j: (i, 0)),
            pl.BlockSpec((1, D_MODEL), lambda i, j: (0, 0)),
            pl.BlockSpec((tm, tf), lambda i, j: (i, j)),
            pl.BlockSpec((tm, tf), lambda i, j: (i, j)),
            pl.BlockSpec((tm, D_MODEL), lambda i, j: (i, 0)),
        ],
        out_shape=[
            jax.ShapeDtypeStruct((t, D_MODEL), F32),
            jax.ShapeDtypeStruct((1, D_MODEL), F32),
            jax.ShapeDtypeStruct((t, D_FF), BF16),
            jax.ShapeDtypeStruct((t, D_FF), BF16),
            jax.ShapeDtypeStruct((t, D_MODEL), BF16),
        ],
        scratch_shapes=[pltpu.VMEM((tm, D_MODEL), BF16), pltpu.VMEM((tm, D_MODEL), F32)],
        compiler_params=_params("arbitrary", "arbitrary"),
    )(dout, x, gain, a, b, wgt, wut, wd)


def _norm_fwd(x, gain, *, name, tm=512):
    t = x.shape[0]

    def body(x_ref, g_ref, h_ref):
        xhat, _ = _rms(x_ref[...])
        h_ref[...] = (xhat * g_ref[...]).astype(BF16)

    return pl.pallas_call(
        body,
        name=name,
        grid=(t // tm,),
        in_specs=[pl.BlockSpec((tm, D_MODEL), lambda i: (i, 0)), pl.BlockSpec((1, D_MODEL), lambda i: (0, 0))],
        out_specs=pl.BlockSpec((tm, D_MODEL), lambda i: (i, 0)),
        out_shape=jax.ShapeDtypeStruct((t, D_MODEL), BF16),
        compiler_params=_params("parallel"),
    )(x, gain)


def _norm_bwd(dh, x, gain, dres, *, name, tm=512):
    t = x.shape[0]

    def body(dh_ref, x_ref, g_ref, dr_ref, dx_ref, dg_ref):
        i = pl.program_id(0)
        xhat, rstd = _rms(x_ref[...])
        dhv = dh_ref[...]
        part = jnp.sum(dhv * xhat, axis=0, keepdims=True)

        @pl.when(i == 0)
        def _():
            dg_ref[...] = part

        @pl.when(i > 0)
        def _():
            dg_ref[...] += part

        dxh = dhv * g_ref[...]
        dx_ref[...] = dr_ref[...] + rstd * (dxh - xhat * jnp.mean(dxh * xhat, axis=-1, keepdims=True))

    row = pl.BlockSpec((tm, D_MODEL), lambda i: (i, 0))
    vec = pl.BlockSpec((1, D_MODEL), lambda i: (0, 0))
    return pl.pallas_call(
        body,
        name=name,
        grid=(t // tm,),
        in_specs=[row, row, vec, row],
        out_specs=[row, vec],
        out_shape=[jax.ShapeDtypeStruct((t, D_MODEL), F32), jax.ShapeDtypeStruct((1, D_MODEL), F32)],
        compiler_params=_params("arbitrary"),
    )(dh, x, gain, dres)


ATT_Q_TILE = 512
ATT_K_BLOCK = 256


def _first_head_lanes():
    return lax.broadcasted_iota(jnp.int32, (1, LANES), 1) < SB_HEAD_DIM


def _stack_heads(x):
    first = _first_head_lanes()
    return jnp.concatenate([jnp.where(first, x, 0.0), jnp.where(first, 0.0, x)], axis=0)


def _unstack_heads(x, rows):
    return jnp.where(_first_head_lanes(), x[:rows], x[rows:])


def _tri(n, relation):
    r = lax.broadcasted_iota(jnp.int32, (n, n), 0)
    c = lax.broadcasted_iota(jnp.int32, (n, n), 1)
    return relation(r, c).astype(BF16)


def _scan_dot(x, tri):
    hi = x.astype(BF16)
    lo = (x - hi.astype(F32)).astype(BF16)
    return _dot(jnp.concatenate([hi, lo], axis=1), jnp.concatenate([tri, tri], axis=0))


def _log_terms(z):
    sp = jnp.log(1.0 + jnp.exp(-jnp.abs(z)))
    return jnp.minimum(z, 0.0) - sp, -jnp.maximum(z, 0.0) - sp


def _attn_fwd(proj, *, name):
    t = proj.shape[0]
    tq, tk = ATT_Q_TILE, ATT_K_BLOCK
    diag = tq // tk
    n_pairs = SB_WIDTH // LANES

    def body(q_ref, k_ref, v_ref, o_ref, l_ref):
        qi = pl.program_id(1)
        q = q_ref[...] * (SB_HEAD_DIM ** -0.5)
        qs = _stack_heads(q).astype(BF16)
        tri = _tri(tk, lambda j, s: j > s)
        trow = lax.broadcasted_iota(jnp.int32, (tq, tk), 0)
        scol = lax.broadcasted_iota(jnp.int32, (tq, tk), 1)

        def block(off, carry, causal):
            acc, c = carry
            z = _dot_nt(qs, k_ref[pl.ds(off, tk), :].astype(BF16))
            lbeta, lrest = _log_terms(z)
            if causal is not None:
                lrest = jnp.where(causal, lrest, 0.0)
            w = jnp.exp(lbeta + (_scan_dot(lrest, tri) + c))
            if causal is not None:
                w = jnp.where(causal, w, 0.0)
            acc = acc + _dot(w.astype(BF16), v_ref[pl.ds(off, tk), :].astype(BF16))
            return acc, c + jnp.sum(lrest, axis=1, keepdims=True)

        carry = (jnp.zeros((2 * tq, LANES), F32), jnp.zeros((2 * tq, 1), F32))
        for j in reversed(range(diag)):
            off = pl.multiple_of(qi * tq + j * tk, tk)
            mask = (scol + j * tk) < trow
            carry = block(off, carry, jnp.concatenate([mask, mask], axis=0))
        n_full = qi * diag

        def step(it, carry):
            return block(pl.multiple_of((n_full - 1 - it) * tk, tk), carry, None)

        acc, c = lax.fori_loop(0, n_full, step, carry)
        o_ref[...] = _unstack_heads(acc, tq)
        l_ref[...] = _unstack_heads(jnp.broadcast_to(c, (2 * tq, LANES)), tq)

    return pl.pallas_call(
        body,
        name=name,
        grid=(n_pairs, t // tq),
        in_specs=[
            pl.BlockSpec((tq, LANES), lambda p, i: (i, p)),
            pl.BlockSpec((t, LANES), lambda p, i: (0, n_pairs + p)),
            pl.BlockSpec((t, LANES), lambda p, i: (0, 2 * n_pairs + p)),
        ],
        out_specs=[pl.BlockSpec((tq, LANES), lambda p, i: (i, p))] * 2,
        out_shape=[jax.ShapeDtypeStruct((t, SB_WIDTH), F32)] * 2,
        compiler_params=_params("parallel", "parallel"),
    )(proj, proj, proj)


def _attn_bwd(proj, ltot, do, *, name, tie=None):
    t = proj.shape[0]
    tq, tk = ATT_Q_TILE, ATT_K_BLOCK
    diag = tq // tk
    n_pairs = SB_WIDTH // LANES
    scale = SB_HEAD_DIM ** -0.5

    def body(q_ref, k_ref, v_ref, l_ref, do_ref, *rest):
        dq_ref, dk_ref, dv_ref = rest[-3:]
        qi = pl.program_id(1)

        @pl.when(qi == 0)
        def _():
            dk_ref[...] = jnp.zeros_like(dk_ref)
            dv_ref[...] = jnp.zeros_like(dv_ref)

        q = q_ref[...] * scale
        lt = l_ref[...]
        qs = _stack_heads(q).astype(BF16)
        dos = _stack_heads(do_ref[...]).astype(BF16)
        first = _first_head_lanes()
        total = jnp.concatenate([jnp.max(jnp.where(first, lt, -jnp.inf), axis=1, keepdims=True),
                                 jnp.max(jnp.where(first, -jnp.inf, lt), axis=1, keepdims=True)], axis=0)
        upto = _tri(tk, lambda j, s: j <= s)
        before = _tri(tk, lambda s, j: s < j)
        trow = lax.broadcasted_iota(jnp.int32, (tq, tk), 0)
        scol = lax.broadcasted_iota(jnp.int32, (tq, tk), 1)

        def block(off, carry, causal):
            dq, cl, cg = carry
            kblk = k_ref[pl.ds(off, tk), :].astype(BF16)
            vblk = v_ref[pl.ds(off, tk), :].astype(BF16)
            z = _dot_nt(qs, kblk)
            lbeta, lrest = _log_terms(z)
            if causal is not None:
                lrest = jnp.where(causal, lrest, 0.0)
            w = jnp.exp(lbeta + (total - (_scan_dot(lrest, upto) + cl)))
            if causal is not None:
                w = jnp.where(causal, w, 0.0)
            g = w * _dot_nt(dos, vblk)
            prior = _scan_dot(g, before) + cg
            sig = jnp.exp(lbeta)
            dz = g * (1.0 - sig) - prior * sig
            if causal is not None:
                dz = jnp.where(causal, dz, 0.0)
            dzb = dz.astype(BF16)
            dq = dq + _dot(dzb, kblk)
            dk_ref[pl.ds(off, tk), :] += _dot_tn(dzb, qs)
            dv_ref[pl.ds(off, tk), :] += _dot_tn(w.astype(BF16), dos)
            return dq, cl + jnp.sum(lrest, axis=1, keepdims=True), cg + jnp.sum(g, axis=1, keepdims=True)

        def step(kb, carry):
            return block(pl.multiple_of(kb * tk, tk), carry, None)

        zero = jnp.zeros((2 * tq, 1), F32)
        carry = lax.fori_loop(0, qi * diag, step, (jnp.zeros((2 * tq, LANES), F32), zero, zero))
        for j in range(diag):
            off = pl.multiple_of(qi * tq + j * tk, tk)
            mask = (scol + j * tk) < trow
            carry = block(off, carry, jnp.concatenate([mask, mask], axis=0))
        dq_ref[...] = _unstack_heads(carry[0], tq) * scale

    tile_spec = pl.BlockSpec((tq, LANES), lambda p, i: (i, p))
    full_spec = pl.BlockSpec((t, LANES), lambda p, i: (0, p))
    return pl.pallas_call(
        body,
        name=name,
        grid=(n_pairs, t // tq),
        in_specs=[
            tile_spec,
            pl.BlockSpec((t, LANES), lambda p, i: (0, n_pairs + p)),
            pl.BlockSpec((t, LANES), lambda p, i: (0, 2 * n_pairs + p)),
            tile_spec,
            tile_spec,
        ] + ([] if tie is None else [pl.BlockSpec(memory_space=pl.ANY)]),
        out_specs=[tile_spec, full_spec, full_spec],
        out_shape=[jax.ShapeDtypeStruct((t, SB_WIDTH), F32)] * 3,
        compiler_params=_params("arbitrary", "arbitrary"),
    )(proj, proj, proj, ltot, do, *([] if tie is None else [tie]))


HG_BLOCK = 256
HG_HEADS = HG_WIDTH // HG_HEAD_DIM


def _chunk_mats(n):
    r = lax.broadcasted_iota(jnp.int32, (n, n), 0)
    c = lax.broadcasted_iota(jnp.int32, (n, n), 1)
    same = (r // HG_CHUNK) == (c // HG_CHUNK)
    upto = (same & (c <= r)).astype(BF16)
    whole = same.astype(BF16)
    onward = (same & (c >= r)).astype(BF16)
    return upto, whole, onward


def _rows_dot(mat, x):
    return _dot(jnp.concatenate([mat, mat, mat], axis=1), jnp.concatenate(_split3(x), axis=0))


def _lower_bound(lg_ref):
    lg = lg_ref[...]
    return _sigmoid(lg[0:1, :] - lg[1:2, :])


def _hgrn_prepare(q_ref, f_ref, lb, h, upto, whole):
    cols = slice(h * HG_HEAD_DIM, (h + 1) * HG_HEAD_DIM)
    lbh = lb[:, cols]
    sg = _sigmoid(f_ref[:, cols])
    forget = lbh + (1.0 - lbh) * sg
    logf = jnp.log(forget)
    kk = (1.0 - lbh) * (1.0 - sg)
    qv = q_ref[:, cols]
    qsig = _sigmoid(qv)
    qh = qv * qsig
    b = _rows_dot(upto, logf)
    blast = _rows_dot(whole, logf)
    return dict(lbh=lbh, sg=sg, forget=forget, kk=kk, qv=qv, qsig=qsig, qh=qh, b=b, eb=jnp.exp(b),
                ekb=jnp.exp(blast - b), dl=jnp.exp(blast))


def _hgrn_fwd(proj, logits, *, name):
    t = proj.shape[0]
    tb = HG_BLOCK
    nc = tb // HG_CHUNK
    hd = HG_HEAD_DIM

    def body(q_ref, f_ref, i_ref, lg_ref, o_ref, st_ref, state, qh_s, kk_s, b_s, qe_s, ke_s, dl_s):
        @pl.when(pl.program_id(0) == 0)
        def _():
            state[...] = jnp.zeros_like(state)

        lb = _lower_bound(lg_ref)
        upto, whole, _ = _chunk_mats(tb)
        for h in range(HG_HEADS):
            p = _hgrn_prepare(q_ref, f_ref, lb, h, upto, whole)
            qh_s[h] = p["qh"]
            kk_s[h] = p["kk"]
            b_s[h] = p["b"]
            qe_s[h] = (p["qh"] * p["eb"]).astype(BF16)
            ke_s[h] = (p["kk"] * p["ekb"]).astype(BF16)
            dl_s[h] = p["dl"]
        rowi = lax.broadcasted_iota(jnp.int32, (HG_CHUNK, hd), 0)

        def chunk(c, _):
            r0 = pl.multiple_of(c * HG_CHUNK, HG_CHUNK)
            rows = pl.ds(r0, HG_CHUNK)
            for h in range(HG_HEADS):
                cols = slice(h * hd, (h + 1) * hd)
                bc = b_s[h, rows, :]
                qc = qh_s[h, rows, :]
                kc = kk_s[h, rows, :]
                vc = i_ref[rows, cols]
                s_in = state[h]
                st_ref[c, h] = s_in
                o = _dot_nt(qe_s[h, rows, :], s_in.astype(BF16))
                for s in range(HG_CHUNK):
                    pair = jnp.where(rowi >= s, qc * jnp.exp(bc - bc[s:s + 1, :]) * kc[s:s + 1, :], 0.0)
                    o = o + jnp.sum(pair, axis=1, keepdims=True) * vc[s:s + 1, :]
                o_ref[rows, cols] = o
                state[h] = s_in * dl_s[h, pl.ds(r0, 1), :] + _dot_tn(vc.astype(BF16), ke_s[h, rows, :])
            return 0

        lax.fori_loop(0, nc, chunk, 0)

    blk = lambda col: pl.BlockSpec((tb, HG_WIDTH), lambda i: (i, col))
    head_f32 = pltpu.VMEM((HG_HEADS, tb, hd), F32)
    head_bf16 = pltpu.VMEM((HG_HEADS, tb, hd), BF16)
    return pl.pallas_call(
        body,
        name=name,
        grid=(t // tb,),
        in_specs=[blk(3), blk(4), blk(5), pl.BlockSpec((2, HG_WIDTH), lambda i: (0, 0))],
        out_specs=[
            pl.BlockSpec((tb, HG_WIDTH), lambda i: (i, 0)),
            pl.BlockSpec((nc, HG_HEADS, hd, hd), lambda i: (i, 0, 0, 0)),
        ],
        out_shape=[
            jax.ShapeDtypeStruct((t, HG_WIDTH), F32),
            jax.ShapeDtypeStruct((t // HG_CHUNK, HG_HEADS, hd, hd), F32),
        ],
        scratch_shapes=[pltpu.VMEM((HG_HEADS, hd, hd), F32), head_f32, head_f32, head_f32, head_bf16, head_bf16,
                        head_f32],
        compiler_params=_params("arbitrary"),
    )(proj, proj, proj, logits)


def _hgrn_bwd(proj, logits, states, do, *, name):
    t = proj.shape[0]
    tb = HG_BLOCK
    nb = t // tb
    nc = tb // HG_CHUNK
    hd = HG_HEAD_DIM

    def body(q_ref, f_ref, i_ref, lg_ref, st_ref, do_ref, dq_ref, df_ref, di_ref, dlb_ref,
             dstate, qh_s, kk_s, b_s, eb_s, ekb_s, qe_s, ke_s, dl_s, dqh_s, dkk_s, dlf_s):
        step = pl.program_id(0)

        @pl.when(step == 0)
        def _():
            dstate[...] = jnp.zeros_like(dstate)
            dlb_ref[...] = jnp.zeros_like(dlb_ref)

        lb = _lower_bound(lg_ref)
        upto, whole, _ = _chunk_mats(tb)
        prepared = []
        for h in range(HG_HEADS):
            p = _hgrn_prepare(q_ref, f_ref, lb, h, upto, whole)
            prepared.append(p)
            qh_s[h] = p["qh"]
            kk_s[h] = p["kk"]
            b_s[h] = p["b"]
            eb_s[h] = p["eb"]
            ekb_s[h] = p["ekb"]
            qe_s[h] = (p["qh"] * p["eb"]).astype(BF16)
            ke_s[h] = (p["kk"] * p["ekb"]).astype(BF16)
            dl_s[h] = p["dl"]
        rowi = lax.broadcasted_iota(jnp.int32, (HG_CHUNK, hd), 0)
        r16 = lax.broadcasted_iota(jnp.int32, (HG_CHUNK, HG_CHUNK), 0)
        c16 = lax.broadcasted_iota(jnp.int32, (HG_CHUNK, HG_CHUNK), 1)
        onward = (c16 >= r16).astype(BF16)

        def chunk(it, _):
            c = nc - 1 - it
            r0 = pl.multiple_of(c * HG_CHUNK, HG_CHUNK)
            rows = pl.ds(r0, HG_CHUNK)
            for h in range(HG_HEADS):
                cols = slice(h * hd, (h + 1) * hd)
                bc = b_s[h, rows, :]
                qc = qh_s[h, rows, :]
                kc = kk_s[h, rows, :]
                vc = i_ref[rows, cols]
                doc = do_ref[rows, cols]
                s_in = st_ref[c, h]
                ds_out = dstate[h]
                ds_out_b = ds_out.astype(BF16)
                docb = doc.astype(BF16)
                dl_row = dl_s[h, pl.ds(r0, 1), :]
                dqh = _dot(docb, s_in.astype(BF16)) * eb_s[h, rows, :]
                dkk = _dot(vc.astype(BF16), ds_out_b) * ekb_s[h, rows, :]
                dv = _dot_nt(ke_s[h, rows, :], ds_out_b)
                db = dqh * qc - dkk * kc
                dwhole = jnp.sum(dkk * kc, axis=0, keepdims=True) + jnp.sum(ds_out * s_in, axis=0, keepdims=True) * dl_row
                for s in range(HG_CHUNK):
                    keep = rowi >= s
                    at_s = rowi == s
                    e = jnp.exp(bc - bc[s:s + 1, :])
                    k_row = kc[s:s + 1, :]
                    pcol = jnp.sum(jnp.where(keep, qc * e * k_row, 0.0), axis=1, keepdims=True)
                    dpcol = jnp.sum(doc * vc[s:s + 1, :], axis=1, keepdims=True)
                    m = jnp.where(keep, e * dpcol, 0.0)
                    mk = m * k_row
                    dk_row = jnp.sum(m * qc, axis=0, keepdims=True)
                    dqh = dqh + mk
                    dkk = dkk + jnp.where(at_s, dk_row, 0.0)
                    db = db + mk * qc - jnp.where(at_s, dk_row * k_row, 0.0)
                    dv = dv + jnp.where(at_s, jnp.sum(pcol * doc, axis=0, keepdims=True), 0.0)
                dqh_s[h, rows, :] = dqh
                dkk_s[h, rows, :] = dkk
                dlf_s[h, rows, :] = _rows_dot(onward, db) + dwhole
                di_ref[rows, cols] = dv
                dstate[h] = ds_out * dl_row + _dot_tn(docb, qe_s[h, rows, :])
            return 0

        lax.fori_loop(0, nc, chunk, 0)
        for h in range(HG_HEADS):
            cols = slice(h * hd, (h + 1) * hd)
            p = prepared[h]
            dq_ref[:, cols] = dqh_s[h] * (p["qsig"] * (1.0 + p["qv"] * (1.0 - p["qsig"])))
            dforget = dlf_s[h] / p["forget"] - dkk_s[h]
            df_ref[:, cols] = dforget * (1.0 - p["lbh"]) * p["sg"] * (1.0 - p["sg"])
            dlb_ref[:, cols] += jnp.sum(dforget * (1.0 - p["sg"]), axis=0, keepdims=True)

    blk = lambda col: pl.BlockSpec((tb, HG_WIDTH), lambda i: (nb - 1 - i, col))
    vec = pl.BlockSpec((1, HG_WIDTH), lambda i: (0, 0))
    head_f32 = pltpu.VMEM((HG_HEADS, tb, hd), F32)
    head_bf16 = pltpu.VMEM((HG_HEADS, tb, hd), BF16)
    return pl.pallas_call(
        body,
        name=name,
        grid=(nb,),
        in_specs=[
            blk(3), blk(4), blk(5),
            pl.BlockSpec((2, HG_WIDTH), lambda i: (0, 0)),
            pl.BlockSpec((nc, HG_HEADS, hd, hd), lambda i: (nb - 1 - i, 0, 0, 0)),
            blk(0),
        ],
        out_specs=[blk(0), blk(0), blk(0), vec],
        out_shape=[jax.ShapeDtypeStruct((t, HG_WIDTH), F32)] * 3 + [jax.ShapeDtypeStruct((1, HG_WIDTH), F32)],
        scratch_shapes=[
            pltpu.VMEM((HG_HEADS, hd, hd), F32),
            head_f32, head_f32, head_f32, head_f32, head_f32, head_bf16, head_bf16, head_f32,
            head_f32, head_f32, head_f32,
        ],
        compiler_params=_params("arbitrary"),
    )(proj, proj, proj, logits, states, do)


def _group_mat(width, head_dim):
    r = lax.broadcasted_iota(jnp.int32, (width, width), 0)
    c = lax.broadcasted_iota(jnp.int32, (width, width), 1)
    return ((r // head_dim) == (c // head_dim)).astype(BF16)


def _head_mean(x, mat, head_dim):
    hi = x.astype(BF16)
    lo = (x - hi.astype(F32)).astype(BF16)
    return (_dot(hi, mat) + _dot(lo, mat)) * (1.0 / head_dim)


def _mix_out_fwd(o_sb, o_hg, proj, g_sb, g_hg, w_out, x1, *, name, tm=256):
    t = x1.shape[0]

    def body(osb_ref, ohg_ref, gate_ref, gsb_ref, ghg_ref, w_ref, x_ref, xo_ref, mt_ref):
        msb = _group_mat(SB_WIDTH, SB_HEAD_DIM)
        mhg = _group_mat(HG_WIDTH, HG_HEAD_DIM)
        osb = osb_ref[...]
        ohg = ohg_ref[...]
        nsb = osb * lax.rsqrt(_head_mean(osb * osb, msb, SB_HEAD_DIM) + EPS) * gsb_ref[...]
        gate = gate_ref[...]
        nhg = ohg * lax.rsqrt(_head_mean(ohg * ohg, mhg, HG_HEAD_DIM) + EPS) * ghg_ref[...] * (gate * _sigmoid(gate))
        mixed = jnp.concatenate([nsb, nhg], axis=1).astype(BF16)
        mt_ref[...] = mixed
        xo_ref[...] = x_ref[...] + _dot(mixed, w_ref[...])

    half = pl.BlockSpec((tm, SB_WIDTH), lambda i: (i, 0))
    vec = pl.BlockSpec((1, SB_WIDTH), lambda i: (0, 0))
    row = pl.BlockSpec((tm, D_MODEL), lambda i: (i, 0))
    return pl.pallas_call(
        body,
        name=name,
        grid=(t // tm,),
        in_specs=[half, half, pl.BlockSpec((tm, HG_WIDTH), lambda i: (i, 6)), vec, vec,
                  pl.BlockSpec((D_MODEL, D_MODEL), lambda i: (0, 0)), row],
        out_specs=[row, row],
        out_shape=[jax.ShapeDtypeStruct((t, D_MODEL), F32), jax.ShapeDtypeStruct((t, D_MODEL), BF16)],
        compiler_params=_params("parallel"),
    )(o_sb, o_hg, proj, g_sb, g_hg, w_out, x1)


def _mix_out_bwd(dx2, o_sb, o_hg, proj, g_sb, g_hg, w_out, *, name, tm=256):
    t = dx2.shape[0]

    def body(dx_ref, osb_ref, ohg_ref, gate_ref, gsb_ref, ghg_ref, w_ref, dosb_ref, dohg_ref, dgate_ref, dgsb_ref,
             dghg_ref, dxb_ref):
        i = pl.program_id(0)
        msb = _group_mat(SB_WIDTH, SB_HEAD_DIM)
        mhg = _group_mat(HG_WIDTH, HG_HEAD_DIM)
        dxb = dx_ref[...].astype(BF16)
        dxb_ref[...] = dxb
        dmixed = _dot_nt(dxb, w_ref[...])
        dnsb = dmixed[:, :SB_WIDTH]
        dy = dmixed[:, SB_WIDTH:]

        osb = osb_ref[...]
        rstd = lax.rsqrt(_head_mean(osb * osb, msb, SB_HEAD_DIM) + EPS)
        ohat = osb * rstd
        part_sb = jnp.sum(dnsb * ohat, axis=0, keepdims=True)
        dohat = dnsb * gsb_ref[...]
        dosb_ref[...] = rstd * (dohat - ohat * _head_mean(dohat * ohat, msb, SB_HEAD_DIM))

        ohg = ohg_ref[...]
        rstd = lax.rsqrt(_head_mean(ohg * ohg, mhg, HG_HEAD_DIM) + EPS)
        ohat = ohg * rstd
        gate = gate_ref[...]
        sig = _sigmoid(gate)
        dn = dy * (gate * sig)
        dgate_ref[...] = dy * (ohat * ghg_ref[...]) * (sig * (1.0 + gate * (1.0 - sig)))
        part_hg = jnp.sum(dn * ohat, axis=0, keepdims=True)
        dohat = dn * ghg_ref[...]
        dohg_ref[...] = rstd * (dohat - ohat * _head_mean(dohat * ohat, mhg, HG_HEAD_DIM))

        @pl.when(i == 0)
        def _():
            dgsb_ref[...] = part_sb
            dghg_ref[...] = part_hg

        @pl.when(i > 0)
        def _():
            dgsb_ref[...] += part_sb
            dghg_ref[...] += part_hg

    half = pl.BlockSpec((tm, SB_WIDTH), lambda i: (i, 0))
    vec = pl.BlockSpec((1, SB_WIDTH), lambda i: (0, 0))
    row = pl.BlockSpec((tm, D_MODEL), lambda i: (i, 0))
    return pl.pallas_call(
        body,
        name=name,
        grid=(t // tm,),
        in_specs=[row, half, half, pl.BlockSpec((tm, HG_WIDTH), lambda i: (i, 6)), vec, vec,
                  pl.BlockSpec((D_MODEL, D_MODEL), lambda i: (0, 0))],
        out_specs=[half, half, half, vec, vec, row],
        out_shape=[jax.ShapeDtypeStruct((t, SB_WIDTH), F32)] * 3 + [jax.ShapeDtypeStruct((1, SB_WIDTH), F32)] * 2
        + [jax.ShapeDtypeStruct((t, D_MODEL), BF16)],
        compiler_params=_params("arbitrary"),
    )(dx2, o_sb, o_hg, proj, g_sb, g_hg, w_out)


def _loss_head(x3, gain, target, *, name, tm=512):
    t = x3.shape[0]

    def body(x_ref, g_ref, y_ref, dx_ref, dg_ref, loss_ref):
        i = pl.program_id(0)
        xhat, rstd = _rms(x_ref[...])
        err = xhat * g_ref[...] - y_ref[...]
        part_loss = 0.5 * jnp.sum(jnp.mean(err * err, axis=-1, keepdims=True), axis=0, keepdims=True)
        dy = err * (1.0 / D_MODEL)
        part_g = jnp.sum(dy * xhat, axis=0, keepdims=True)

        @pl.when(i == 0)
        def _():
            dg_ref[...] = part_g
            loss_ref[...] = jnp.broadcast_to(part_loss, loss_ref.shape)

        @pl.when(i > 0)
        def _():
            dg_ref[...] += part_g
            loss_ref[...] += jnp.broadcast_to(part_loss, loss_ref.shape)

        dxh = dy * g_ref[...]
        dx_ref[...] = rstd * (dxh - xhat * jnp.mean(dxh * xhat, axis=-1, keepdims=True))

    row = pl.BlockSpec((tm, D_MODEL), lambda i: (i, 0))
    vec = pl.BlockSpec((1, D_MODEL), lambda i: (0, 0))
    return pl.pallas_call(
        body,
        name=name,
        grid=(t // tm,),
        in_specs=[row, vec, row],
        out_specs=[row, vec, vec],
        out_shape=[jax.ShapeDtypeStruct((t, D_MODEL), F32), jax.ShapeDtypeStruct((1, D_MODEL), F32),
                   jax.ShapeDtypeStruct((1, D_MODEL), F32)],
        compiler_params=_params("arbitrary"),
    )(x3, gain, target)


def _local_step(x, target, norms, logits, w, weights_after=None, grads_ready=None):
    w = dict(w)
    x1, a1, b1, h1, s1 = _ffn_fwd(x, norms["ffn1"], w["g1t"], w["u1t"], w["d1"], name="ffn1_fwd")
    if weights_after is not None:
        w.update(weights_after("ffn1", x1))
    hm = _norm_fwd(x1, norms["mix"], name="mix_norm_fwd")
    proj = _mm(hm, w["in"], name="in_proj", tm=512, tn=512)
    o_sb, ltot = _attn_fwd(proj, name="sb_attn_fwd")
    o_hg, states = _hgrn_fwd(proj, logits, name="hgrn2_fwd")
    x2, mixed = _mix_out_fwd(o_sb, o_hg, proj, norms["sb"], norms["hg"], w["out"], x1, name="mix_out_fwd")
    if weights_after is not None:
        w.update(weights_after("mix", x2))
    x3, a2, b2, h2, s2 = _ffn_fwd(x2, norms["ffn2"], w["g2t"], w["u2t"], w["d2"], name="ffn2_fwd")
    dx3, d_final, loss_row = _loss_head(x3, norms["final"], target, name="loss_head")

    def weight_grad(lhs, rhs, name, tie=None):
        return _mm(lhs, rhs, name=name, tm=256, tn=D_MODEL, ta=True, out_dtype=BF16, tie=tie)

    def sent(stage):
        return grads_ready(stage, gw) if grads_ready is not None else None

    gw, gv = {}, {"final": d_final}
    dx2, gv["ffn2"], da2, db2, dob2 = _ffn_bwd(dx3, x2, norms["ffn2"], a2, b2, w["g2t"], w["u2t"], w["d2"],
                                               name="ffn2_bwd")
    gw["g2t"] = weight_grad(da2, h2, "ffn2_dgate")
    gw["u2t"] = weight_grad(db2, h2, "ffn2_dup")
    gw["d2"] = weight_grad(s2, dob2, "ffn2_ddown")

    do_sb, do_hg, d_gate, gv["sb"], gv["hg"], dx2b = _mix_out_bwd(
        dx2, o_sb, o_hg, proj, norms["sb"], norms["hg"], w["out"], name="mix_out_bwd")
    gw["out"] = weight_grad(mixed, dx2b, "out_dw")
    tie = sent("mix")
    dq_sb, dk_sb, dv_sb = _attn_bwd(proj, ltot, do_sb, name="sb_attn_bwd", tie=tie)
    dq_hg, df_hg, di_hg, d_lb = _hgrn_bwd(proj, logits if tie is None else logits + tie[0, 0], states, do_hg,
                                          name="hgrn2_bwd")
    dproj = jnp.concatenate([dq_sb, dk_sb, dv_sb, dq_hg, df_hg, di_hg, d_gate], axis=1).astype(BF16)
    gw["in"] = _mm(hm, dproj, name="in_dw", tm=D_MODEL, tn=256, ta=True)
    tie = sent("in")
    dhm = _mm(dproj, w["in"], name="in_dx", tm=512, tn=D_MODEL, nt=True)
    dx1, gv["mix"] = _norm_bwd(dhm, x1, norms["mix"] if tie is None else norms["mix"] + tie[0, 0], dx2,
                               name="mix_norm_bwd")

    dx, gv["ffn1"], da1, db1, dob1 = _ffn_bwd(dx1, x, norms["ffn1"], a1, b1, w["g1t"], w["u1t"], w["d1"],
                                              name="ffn1_bwd")
    gw["g1t"] = weight_grad(da1, h1, "ffn1_dgate")
    gw["u1t"] = weight_grad(db1, h1, "ffn1_dup", tie=sent("g1t"))
    gw["d1"] = weight_grad(s1, dob1, "ffn1_ddown", tie=sent("u1t"))
    sent("d1")
    gv["lb"] = d_lb
    return loss_row, dx, gw, gv


HBM = pl.BlockSpec(memory_space=pl.ANY)


def _place():
    return lax.axis_index("x"), lax.axis_index("y"), lax.axis_index("c")


def _slot(px, py, pc):
    return 4 * px + 2 * py + pc


def _all_gather(blocks, *, name):
    n = len(blocks)

    def body(*refs):
        ins, outs = refs[:n], refs[n:2 * n]
        send_sems, recv_sems, local_sems = refs[2 * n:]
        x, y, c = _place()
        me, sibling = (x, y, c), (x, y, 1 - c)
        chips = [(1 - x, y), (x, 1 - y), (1 - x, 1 - y)]

        def copy(a, k, block, to, src=None):
            dst = outs[a].at[_slot(*block)]
            return pltpu.make_async_remote_copy(
                src_ref=dst if src is None else src, dst_ref=dst, send_sem=send_sems.at[7 * a + k],
                recv_sem=recv_sems.at[7 * a + k], device_id=to, device_id_type=MESH)

        mine = [pltpu.make_async_copy(ins[a], outs[a].at[_slot(*me)], local_sems.at[a]) for a in range(n)]
        for cp in mine:
            cp.start()
        first = []
        for a in range(n):
            first.append(copy(a, 0, me, sibling, src=ins[a]))
            first += [copy(a, 1 + j, me, (*chip, c), src=ins[a]) for j, chip in enumerate(chips)]
        for cp in first:
            cp.start()
        passed = []
        for j, chip in enumerate(chips):
            for a in range(n):
                copy(a, 1 + j, (*chip, c), me).wait_recv()
                fwd = copy(a, 4 + j, (*chip, c), sibling)
                fwd.start()
                passed.append(fwd)
        for a in range(n):
            copy(a, 0, sibling, me).wait_recv()
            for j, chip in enumerate(chips):
                copy(a, 4 + j, (*chip, 1 - c), me).wait_recv()
        for cp in first + passed:
            cp.wait_send()
        for cp in mine:
            cp.wait()

    return pl.pallas_call(
        body,
        name=name,
        in_specs=[HBM] * n,
        out_specs=[HBM] * n,
        out_shape=[jax.ShapeDtypeStruct((N_DEV,) + b.shape, b.dtype) for b in blocks],
        scratch_shapes=[pltpu.SemaphoreType.DMA((7 * n,)), pltpu.SemaphoreType.DMA((7 * n,)),
                        pltpu.SemaphoreType.DMA((n,))],
    )(*blocks)


def _flipped(place, d):
    return tuple(1 - p if (d >> (2 - axis)) & 1 else p for axis, p in enumerate(place))


def _scatter_by_owner(stacks, *, name):
    n = len(stacks)

    def body(*refs):
        ins, outs = refs[:n], refs[n:2 * n]
        send_sems, recv_sems, local_sems = refs[2 * n:]
        me = _place()
        mine = [pltpu.make_async_copy(ins[a].at[_slot(*me)], outs[a].at[_slot(*me)], local_sems.at[a]) for a in range(n)]
        for cp in mine:
            cp.start()
        copies = []
        for d in range(1, N_DEV):
            peer = _flipped(me, d)
            for a in range(n):
                copies.append(pltpu.make_async_remote_copy(
                    src_ref=ins[a].at[_slot(*peer)], dst_ref=outs[a].at[_slot(*me)], send_sem=send_sems.at[7 * a + d - 1],
                    recv_sem=recv_sems.at[7 * a + d - 1], device_id=peer, device_id_type=MESH))
        for cp in copies:
            cp.start()
        for cp in copies:
            cp.wait()
        for cp in mine:
            cp.wait()

    return pl.pallas_call(
        body,
        name=name,
        in_specs=[HBM] * n,
        out_specs=[HBM] * n,
        out_shape=[jax.ShapeDtypeStruct(s.shape, s.dtype) for s in stacks],
        scratch_shapes=[pltpu.SemaphoreType.DMA((7 * n,)), pltpu.SemaphoreType.DMA((7 * n,)),
                        pltpu.SemaphoreType.DMA((n,))],
    )(*stacks)


SEM = pl.BlockSpec(memory_space=pltpu.SEMAPHORE)
EFFECT = pltpu.SideEffectType.DATAFLOW_SIDE_EFFECTING


def _split_copies(me, srcs, lands, send_sems, recv_sems, by_owner):
    copies = []
    for d in range(1, N_DEV):
        peer = _flipped(me, d)
        for a, (src, land) in enumerate(zip(srcs, lands)):
            copies.append(pltpu.make_async_remote_copy(
                src_ref=src.at[_slot(*peer)] if by_owner else src, dst_ref=land.at[_slot(*me)],
                send_sem=send_sems.at[7 * a + d - 1], recv_sem=recv_sems.at[7 * a + d - 1], device_id=peer,
                device_id_type=MESH))
    return copies


def _copies_start(srcs, *, name, by_owner, after=None):
    n = len(srcs)
    extra = [] if after is None else [after]
    land_shapes = [s.shape if by_owner else (N_DEV,) + s.shape for s in srcs]
    lands = [pltpu.with_memory_space_constraint(lax.empty(shape, s.dtype), pltpu.HBM) for shape, s in zip(land_shapes, srcs)]
    srcs = [pltpu.with_memory_space_constraint(s, pltpu.HBM) for s in srcs]

    def body(*refs):
        src_refs, land_refs = refs[:n], refs[n:2 * n]
        send_sems, recv_sems = refs[2 * n + len(extra)], refs[2 * n + len(extra) + 1]
        token = refs[-1]
        for cp in _split_copies(_place(), src_refs, land_refs, send_sems, recv_sems, by_owner):
            cp.start()
        token[...] = jnp.zeros_like(token)

    out = pl.pallas_call(
        body,
        name=name,
        in_specs=[HBM] * (2 * n + len(extra)),
        out_specs=[SEM, SEM] + [HBM] * (2 * n) + [pl.BlockSpec(memory_space=pltpu.VMEM)],
        out_shape=[pltpu.SemaphoreType.DMA((7 * n,)), pltpu.SemaphoreType.DMA((7 * n,))]
        + [pltpu.HBM(s.shape, s.dtype) for s in srcs] + [pltpu.HBM(shape, s.dtype) for shape, s in zip(land_shapes, srcs)]
        + [jax.ShapeDtypeStruct((8, LANES), F32)],
        input_output_aliases={i: 2 + i for i in range(2 * n)},
        compiler_params=pltpu.CompilerParams(has_side_effects=EFFECT),
    )(*srcs, *lands, *extra)
    return (out[0], out[1], out[2:2 + n], out[2 + n:2 + 2 * n]), out[-1]


def _copies_wait(started, after, *, name, by_owner):
    send_sems, recv_sems, srcs, lands = started
    n = len(srcs)

    def body(*refs):
        src_refs, land_refs = refs[:n], refs[n:2 * n]
        for cp in _split_copies(_place(), src_refs, land_refs, refs[2 * n], refs[2 * n + 1], by_owner):
            cp.wait_send()
            cp.wait_recv()

    out = pl.pallas_call(
        body,
        name=name,
        in_specs=[HBM] * (2 * n) + [SEM, SEM, HBM],
        out_specs=[HBM] * (2 * n),
        out_shape=[pltpu.HBM(s.shape, s.dtype) for s in srcs] + [pltpu.HBM(s.shape, s.dtype) for s in lands],
        input_output_aliases={i: i for i in range(2 * n)},
        compiler_params=pltpu.CompilerParams(has_side_effects=EFFECT),
    )(*srcs, *lands, send_sems, recv_sems, after)
    return out[n:]


def _with_own(lands, own, slot):
    zero = jnp.zeros((), jnp.int32)
    return [lax.dynamic_update_slice(land, o[None], (slot.astype(jnp.int32),) + (zero,) * o.ndim)
            for land, o in zip(lands, own)]


def _adamw(w, g, m, v):
    m = ADAM_B1 * m + (1.0 - ADAM_B1) * g
    v = ADAM_B2 * v + (1.0 - ADAM_B2) * (g * g)
    m_hat = m / (1.0 - ADAM_B1 ** ADAM_STEP)
    v_hat = v / (1.0 - ADAM_B2 ** ADAM_STEP)
    delta = -ADAM_LR * (m_hat / (jnp.sqrt(v_hat) + ADAM_EPS) + ADAM_WD * w)
    return delta, m, v


def _sum_and_update(parts, w, m, v, *, name, tr, transposed=False):
    _, rows, cols = w.shape
    pad = -cols % LANES

    def body(p_ref, w_ref, m_ref, v_ref, g_ref, d_ref, mo_ref, vo_ref):
        g = p_ref[0].astype(F32)
        for s in range(1, N_DEV):
            g = g + p_ref[s].astype(F32)
        if transposed:
            if pad:
                g = jnp.concatenate([g, jnp.zeros((pad, tr), F32)], axis=0)
            g = g.T[:, :cols]
        g_ref[0] = g
        d_ref[0], mo_ref[0], vo_ref[0] = _adamw(w_ref[0], g, m_ref[0], v_ref[0])

    flat = pl.BlockSpec((1, tr, cols), lambda i: (0, i, 0))
    if transposed:
        part_spec = pl.BlockSpec((N_DEV, cols, tr), lambda i: (0, 0, i))
    else:
        part_spec = pl.BlockSpec((N_DEV, tr, cols), lambda i: (0, i, 0))
    return pl.pallas_call(
        body,
        name=name,
        grid=(rows // tr,),
        in_specs=[part_spec, flat, flat, flat],
        out_specs=[flat] * 4,
        out_shape=[jax.ShapeDtypeStruct((1, rows, cols), F32)] * 4,
        compiler_params=_params("parallel"),
    )(parts, w, m, v)


VEC_ROWS = 8
ROW_LOGITS, ROW_LOSS = 5, 7


def _vectors_update(part, w, m, v, *, name):
    def body(p_ref, w_ref, m_ref, v_ref, g_ref, d_ref, mo_ref, vo_ref, loss_ref, all_ref, send_sems, recv_sems):
        me = _place()
        all_ref[_slot(*me)] = p_ref[...]
        copies = []
        for d in range(1, N_DEV):
            peer = _flipped(me, d)
            copies.append(pltpu.make_async_remote_copy(
                src_ref=p_ref, dst_ref=all_ref.at[_slot(*me)], send_sem=send_sems.at[d - 1], recv_sem=recv_sems.at[d - 1],
                device_id=peer, device_id_type=MESH))
        for cp in copies:
            cp.start()
        for cp in copies:
            cp.wait()
        total = all_ref[0]
        for s in range(1, N_DEV):
            total = total + all_ref[s]
        wv = w_ref[...]
        half = D_MODEL // 2
        lb = _sigmoid(wv[ROW_LOGITS:ROW_LOGITS + 1, :half] - wv[ROW_LOGITS:ROW_LOGITS + 1, half:])
        d_first = total[ROW_LOGITS:ROW_LOGITS + 1, :half] * lb * (1.0 - lb)
        d_logits = jnp.concatenate([d_first, -d_first], axis=1)
        rowi = lax.broadcasted_iota(jnp.int32, (VEC_ROWS, D_MODEL), 0)
        g = jnp.where(rowi == ROW_LOGITS, d_logits, jnp.where(rowi < ROW_LOGITS, total, 0.0))
        g_ref[...] = g
        d_ref[...], mo_ref[...], vo_ref[...] = _adamw(wv, g, m_ref[...], v_ref[...])
        loss_ref[...] = total[ROW_LOSS:ROW_LOSS + 1, :]

    vmem = pl.BlockSpec(memory_space=pltpu.VMEM)
    return pl.pallas_call(
        body,
        name=name,
        in_specs=[vmem] * 4,
        out_specs=[vmem] * 5,
        out_shape=[jax.ShapeDtypeStruct((VEC_ROWS, D_MODEL), F32)] * 4 + [jax.ShapeDtypeStruct((1, D_MODEL), F32)],
        scratch_shapes=[pltpu.VMEM((N_DEV, VEC_ROWS, D_MODEL), F32), pltpu.SemaphoreType.DMA((7,)),
                        pltpu.SemaphoreType.DMA((7,))],
    )(part, w, m, v)


ROW_SHARDED = ("d1", "d2", "out")
TRANSPOSED = ("g1t", "u1t", "g2t", "u2t")


def _vector_rows(rows):
    rowi = lax.broadcasted_iota(jnp.int32, (VEC_ROWS, D_MODEL), 0)
    out = jnp.zeros((VEC_ROWS, D_MODEL), F32)
    for i, r in enumerate(rows):
        if r is not None:
            out = jnp.where(rowi == i, r, out)
    return out


def kernel(x, ffn1_norm, ffn1_w_gate, ffn1_w_up, ffn1_w_down, mix_norm, w_in, sb_out_norm, hg_lower_bound_logits, hg_out_norm, w_out, ffn2_norm, ffn2_w_gate, ffn2_w_up, ffn2_w_down, final_norm, loss_target, m_ffn1_norm, m_ffn1_w_gate, m_ffn1_w_up, m_ffn1_w_down, m_mix_norm, m_w_in, m_sb_out_norm, m_hg_lower_bound_logits, m_hg_out_norm, m_w_out, m_ffn2_norm, m_ffn2_w_gate, m_ffn2_w_up, m_ffn2_w_down, m_final_norm, v_ffn1_norm, v_ffn1_w_gate, v_ffn1_w_up, v_ffn1_w_down, v_mix_norm, v_w_in, v_sb_out_norm, v_hg_lower_bound_logits, v_hg_out_norm, v_w_out, v_ffn2_norm, v_ffn2_w_gate, v_ffn2_w_up, v_ffn2_w_down, v_final_norm):
    def matrices(g1, u1, d1, win, wout, g2, u2, d2):
        return {"g1t": g1, "u1t": u1, "d1": d1, "in": win, "out": wout, "g2t": g2, "u2t": u2, "d2": d2}

    def vectors(n1, nm, nsb, lg, nhg, n2, nf):
        return [n1, nm, n2, nf.reshape(1, D_MODEL), jnp.concatenate([nsb, nhg], axis=1), lg.reshape(1, D_MODEL), None, None]

    w_sh = matrices(ffn1_w_gate, ffn1_w_up, ffn1_w_down, w_in, w_out, ffn2_w_gate, ffn2_w_up, ffn2_w_down)
    m_sh = matrices(m_ffn1_w_gate, m_ffn1_w_up, m_ffn1_w_down, m_w_in, m_w_out, m_ffn2_w_gate, m_ffn2_w_up, m_ffn2_w_down)
    v_sh = matrices(v_ffn1_w_gate, v_ffn1_w_up, v_ffn1_w_down, v_w_in, v_w_out, v_ffn2_w_gate, v_ffn2_w_up, v_ffn2_w_down)
    keys = list(w_sh)

    slot = _slot(*_place())

    def full(key, stack):
        if key == "in":
            return stack.transpose(1, 0, 2).reshape(D_MODEL, IN_COLS)
        return stack.reshape(-1, D_MODEL)

    def by_owner(key, grad):
        if key == "in":
            return grad.reshape(D_MODEL, N_DEV, IN_SHARD).transpose(1, 0, 2).astype(BF16)
        return grad.reshape(N_DEV, -1, D_MODEL)

    blocks = {k: (w_sh[k][0].T if k in TRANSPOSED else w_sh[k][0]).astype(BF16) for k in keys}
    first, mid, last = ("g1t", "u1t", "d1"), ("in", "out"), ("g2t", "u2t", "d2")
    w_first = {k: full(k, s) for k, s in zip(first, _all_gather([blocks[k] for k in first], name="gather_ffn1"))}
    flights = {}
    flights["ffn1"], token_mid = _copies_start([blocks[k] for k in mid], name="gather_mid_start", by_owner=False,
                                               after=w_first["d1"])
    flights["mix"], token_last = _copies_start([blocks[k] for k in last], name="gather_ffn2_start", by_owner=False,
                                               after=token_mid)

    def weights_after(stage, result):
        group = mid if stage == "ffn1" else last
        lands = _copies_wait(flights[stage], result, name="gather_" + stage + "_wait", by_owner=False)
        return {k: full(k, s) for k, s in zip(group, _with_own(lands, [blocks[k] for k in group], slot))}

    groups = {"mix": ("g2t", "u2t", "d2", "out"), "in": ("in",), "g1t": ("g1t",), "u1t": ("u1t",), "d1": ("d1",)}
    sent, sent_tokens = {}, []

    def grads_ready(stage, gw):
        stacks = [by_owner(k, gw[k]) for k in groups[stage]]
        flight, token = _copies_start(stacks, name="grads_" + stage + "_start", by_owner=True)
        sent[stage] = (stacks, flight)
        sent_tokens.append(token)
        return token

    norms = {"ffn1": ffn1_norm + token_last[0, 0], "mix": mix_norm, "sb": sb_out_norm, "hg": hg_out_norm,
             "ffn2": ffn2_norm, "final": final_norm.reshape(1, D_MODEL)}
    loss_row, grad_x, gw, gv = _local_step(x[0], loss_target[0], norms, hg_lower_bound_logits, w_first, weights_after,
                                           grads_ready)

    tiles = {"g1t": 256, "u1t": 256, "g2t": 256, "u2t": 256, "d1": 176, "d2": 176, "out": 128, "in": 256}
    updated, after = {}, sent_tokens[-1]
    for stage, (stacks, flight) in sent.items():
        lands = _copies_wait(flight, after, name="grads_" + stage + "_wait", by_owner=True)
        own = [lax.dynamic_index_in_dim(s, slot, keepdims=False) for s in stacks]
        for k, part in zip(groups[stage], _with_own(lands, own, slot)):
            updated[k] = _sum_and_update(part, w_sh[k], m_sh[k], v_sh[k], name="adamw_" + k, tr=tiles[k],
                                         transposed=k in TRANSPOSED)
            after = updated[k][0]
    mats = [{k: updated[k][i] for k in keys} for i in range(4)]

    lb_row = jnp.concatenate([gv["lb"], jnp.zeros_like(gv["lb"])], axis=1)
    part = _vector_rows([gv["ffn1"], gv["mix"], gv["ffn2"], gv["final"], jnp.concatenate([gv["sb"], gv["hg"]], axis=1),
                         lb_row, None, loss_row])
    vec_w = _vector_rows(vectors(ffn1_norm, mix_norm, sb_out_norm, hg_lower_bound_logits, hg_out_norm, ffn2_norm, final_norm))
    vec_m = _vector_rows(vectors(m_ffn1_norm, m_mix_norm, m_sb_out_norm, m_hg_lower_bound_logits, m_hg_out_norm,
                                 m_ffn2_norm, m_final_norm))
    vec_v = _vector_rows(vectors(v_ffn1_norm, v_mix_norm, v_sb_out_norm, v_hg_lower_bound_logits, v_hg_out_norm,
                                 v_ffn2_norm, v_final_norm))
    *vecs, loss_out = _vectors_update(part, vec_w, vec_m, vec_v, name="vectors_update")

    def leaves(mat, vec):
        half = D_MODEL // 2
        return (
            vec[0:1], mat["g1t"], mat["u1t"], mat["d1"], vec[1:2], mat["in"], vec[4:5, :half],
            vec[ROW_LOGITS].reshape(2, half), vec[4:5, half:], mat["out"], vec[2:3], mat["g2t"], mat["u2t"],
            mat["d2"], vec[3],
        )

    out = [loss_out[0, 0], grad_x[None]]
    for mat, vec in zip(mats, vecs):
        out.extend(leaves(mat, vec))
    return tuple(out)
```

```python
import jax
import jax.numpy as jnp
from jax import lax
from jax.experimental import pallas as pl
from jax.experimental.pallas import tpu as pltpu

F32, BF16 = jnp.float32, jnp.bfloat16
D_MODEL = 1024
D_FF = 2816
SB_WIDTH = 512
HG_WIDTH = 512
SB_HEAD_DIM = 64
HG_HEAD_DIM = 128
IN_COLS = 3584
EPS = 1e-6
N_DEV = 8
FF_SHARD = D_FF // N_DEV
IN_SHARD = IN_COLS // N_DEV
OUT_SHARD = D_MODEL // N_DEV
LANES = 128
HG_CHUNK = 16
VMEM_LIMIT_BYTES = 48 * 1024 * 1024
ADAM_LR, ADAM_B1, ADAM_B2, ADAM_EPS, ADAM_WD, ADAM_STEP = 0.001, 0.9, 0.999, 1e-08, 0.01, 10
MESH = pl.DeviceIdType.MESH


def _params(*semantics):
    return pltpu.CompilerParams(dimension_semantics=semantics, vmem_limit_bytes=VMEM_LIMIT_BYTES)


def _dot(a, b):
    return jnp.dot(a, b, preferred_element_type=F32)


def _dot_nt(a, b):
    return lax.dot_general(a, b, (((1,), (1,)), ((), ())), preferred_element_type=F32)


def _dot_tn(a, b):
    return lax.dot_general(a, b, (((0,), (0,)), ((), ())), preferred_element_type=F32)


def _split3(x):
    hi = x.astype(BF16)
    r1 = x - hi.astype(F32)
    mid = r1.astype(BF16)
    lo = (r1 - mid.astype(F32)).astype(BF16)
    return hi, mid, lo


def _rms(xv):
    rstd = lax.rsqrt(jnp.mean(xv * xv, axis=-1, keepdims=True) + EPS)
    return xv * rstd, rstd


def _sigmoid(x):
    return 1.0 / (1.0 + jnp.exp(-x))


def _mm(a, b, *, name, tm, tn, nt=False, ta=False, out_dtype=F32, tie=None):
    k, m = a.shape if ta else a.shape[::-1]
    n = b.shape[0] if nt else b.shape[1]
    assert m % tm == 0 and n % tn == 0 and not (nt and ta), (name, a.shape, b.shape, tm, tn)

    def body(a_ref, b_ref, *rest):
        av = a_ref[...].astype(BF16)
        bv = b_ref[...].astype(BF16)
        rest[-1][...] = (_dot_nt(av, bv) if nt else _dot_tn(av, bv) if ta else _dot(av, bv)).astype(out_dtype)

    in_specs = [
        pl.BlockSpec((k, tm), lambda i, j: (0, i)) if ta else pl.BlockSpec((tm, k), lambda i, j: (i, 0)),
        pl.BlockSpec((tn, k), lambda i, j: (j, 0)) if nt else pl.BlockSpec((k, tn), lambda i, j: (0, j)),
    ]
    operands = [a, b]
    if tie is not None:
        in_specs.append(pl.BlockSpec(memory_space=pl.ANY))
        operands.append(tie)
    return pl.pallas_call(
        body,
        name=name,
        grid=(m // tm, n // tn),
        in_specs=in_specs,
        out_specs=pl.BlockSpec((tm, tn), lambda i, j: (i, j)),
        out_shape=jax.ShapeDtypeStruct((m, n), out_dtype),
        compiler_params=_params("parallel", "parallel"),
    )(*operands)


def _ffn_fwd(x, gain, wgt, wut, wd, *, name, tm=512, tf=256):
    t = x.shape[0]
    nj = D_FF // tf

    def body(x_ref, g_ref, wg_ref, wu_ref, wd_prev_ref, wd_last_ref, xo_ref, a_ref, b_ref, h_ref, st_ref, acc, s_prev):
        j = pl.program_id(1)

        @pl.when(j == 0)
        def _():
            xhat, _ = _rms(x_ref[...])
            h_ref[...] = (xhat * g_ref[...]).astype(BF16)
            acc[...] = jnp.zeros_like(acc)
            s_prev[...] = jnp.zeros_like(s_prev)

        acc[...] += _dot(s_prev[...], wd_prev_ref[...])
        h = h_ref[...]
        a = _dot_nt(h, wg_ref[...])
        b = _dot_nt(h, wu_ref[...])
        a_ref[...] = a.astype(BF16)
        b_ref[...] = b.astype(BF16)
        s = (a * _sigmoid(a) * b).astype(BF16)
        st_ref[...] = s
        s_prev[...] = s

        @pl.when(j == nj - 1)
        def _():
            xo_ref[...] = x_ref[...] + 0.5 * (acc[...] + _dot(s, wd_last_ref[...]))

    return pl.pallas_call(
        body,
        name=name,
        grid=(t // tm, nj),
        in_specs=[
            pl.BlockSpec((tm, D_MODEL), lambda i, j: (i, 0)),
            pl.BlockSpec((1, D_MODEL), lambda i, j: (0, 0)),
            pl.BlockSpec((tf, D_MODEL), lambda i, j: (j, 0)),
            pl.BlockSpec((tf, D_MODEL), lambda i, j: (j, 0)),
            pl.BlockSpec((tf, D_MODEL), lambda i, j: (jnp.maximum(j - 1, 0), 0)),
            pl.BlockSpec((tf, D_MODEL), lambda i, j: (nj - 1, 0)),
        ],
        out_specs=[
            pl.BlockSpec((tm, D_MODEL), lambda i, j: (i, 0)),
            pl.BlockSpec((tm, tf), lambda i, j: (i, j)),
            pl.BlockSpec((tm, tf), lambda i, j: (i, j)),
            pl.BlockSpec((tm, D_MODEL), lambda i, j: (i, 0)),
            pl.BlockSpec((tm, tf), lambda i, j: (i, j)),
        ],
        out_shape=[
            jax.ShapeDtypeStruct((t, D_MODEL), F32),
            jax.ShapeDtypeStruct((t, D_FF), BF16),
            jax.ShapeDtypeStruct((t, D_FF), BF16),
            jax.ShapeDtypeStruct((t, D_MODEL), BF16),
            jax.ShapeDtypeStruct((t, D_FF), BF16),
        ],
        scratch_shapes=[pltpu.VMEM((tm, D_MODEL), F32), pltpu.VMEM((tm, tf), BF16)],
        compiler_params=_params("parallel", "arbitrary"),
    )(x, gain, wgt, wut, wd, wd)


def _ffn_bwd(dout, x, gain, a, b, wgt, wut, wd, *, name, tm=512, tf=256):
    t = x.shape[0]
    nj = D_FF // tf

    def body(do_ref, x_ref, g_ref, a_ref, b_ref, wg_prev_ref, wu_prev_ref, wg_last_ref, wu_last_ref, wd_ref,
             dx_ref, dg_ref, da_ref, db_ref, dob_ref, dob_scr, dh, da_prev, db_prev):
        i = pl.program_id(0)
        j = pl.program_id(1)

        @pl.when(j == 0)
        def _():
            d = (0.5 * do_ref[...]).astype(BF16)
            dob_scr[...] = d
            dob_ref[...] = d
            dh[...] = jnp.zeros_like(dh)
            da_prev[...] = jnp.zeros_like(da_prev)
            db_prev[...] = jnp.zeros_like(db_prev)

        dh[...] += _dot(da_prev[...], wg_prev_ref[...]) + _dot(db_prev[...], wu_prev_ref[...])
        ds = _dot_nt(dob_scr[...], wd_ref[...])
        av = a_ref[...].astype(F32)
        bv = b_ref[...].astype(F32)
        sig = _sigmoid(av)
        dbv = (ds * (av * sig)).astype(BF16)
        dav = (ds * bv * (sig * (1.0 + av * (1.0 - sig)))).astype(BF16)
        da_ref[...] = dav
        db_ref[...] = dbv
        da_prev[...] = dav
        db_prev[...] = dbv

        @pl.when(j == nj - 1)
        def _():
            xhat, rstd = _rms(x_ref[...])
            dhv = dh[...] + _dot(dav, wg_last_ref[...]) + _dot(dbv, wu_last_ref[...])
            part = jnp.sum(dhv * xhat, axis=0, keepdims=True)

            @pl.when(i == 0)
            def _():
                dg_ref[...] = part

            @pl.when(i > 0)
            def _():
                dg_ref[...] += part

            dxh = dhv * g_ref[...]
            dx_ref[...] = do_ref[...] + rstd * (dxh - xhat * jnp.mean(dxh * xhat, axis=-1, keepdims=True))

    return pl.pallas_call(
        body,
        name=name,
        grid=(t // tm, nj),
        in_specs=[
            pl.BlockSpec((tm, D_MODEL), lambda i, j: (i, 0)),
            pl.BlockSpec((tm, D_MODEL), lambda i, j: (i, 0)),
            pl.BlockSpec((1, D_MODEL), lambda i, j: (0, 0)),
            pl.BlockSpec((tm, tf), lambda i, j: (i, j)),
            pl.BlockSpec((tm, tf), lambda i, j: (i, j)),
            pl.BlockSpec((tf, D_MODEL), lambda i, j: (jnp.maximum(j - 1, 0), 0)),
            pl.BlockSpec((tf, D_MODEL), lambda i, j: (jnp.maximum(j - 1, 0), 0)),
            pl.BlockSpec((tf, D_MODEL), lambda i, j: (nj - 1, 0)),
            pl.BlockSpec((tf, D_MODEL), lambda i, j: (nj - 1, 0)),
            pl.BlockSpec((tf, D_MODEL), lambda i, j: (j, 0)),
        ],
        out_specs=[
            pl.BlockSpec((tm, D_MODEL), lambda i, j: (i, 0)),
            pl.BlockSpec((1, D_MODEL), lambda i, j: (0, 0)),
            pl.BlockSpec((tm, tf), lambda i, j: (i, j)),
            pl.BlockSpec((tm, tf), lambda i, j: (i, j)),
            pl.BlockSpec((tm, D_MODEL), lambda i, j: (i, 0)),
        ],
        out_shape=[
            jax.ShapeDtypeStruct((t, D_MODEL), F32),
            jax.ShapeDtypeStruct((1, D_MODEL), F32),
            jax.ShapeDtypeStruct((t, D_FF), BF16),
            jax.ShapeDtypeStruct((t, D_FF), BF16),
            jax.ShapeDtypeStruct((t, D_MODEL), BF16),
        ],
        scratch_shapes=[pltpu.VMEM((tm, D_MODEL), BF16), pltpu.VMEM((tm, D_MODEL), F32), pltpu.VMEM((tm, tf), BF16),
                        pltpu.VMEM((tm, tf), BF16)],
        compiler_params=_params("arbitrary", "arbitrary"),
    )(dout, x, gain, a, b, wgt, wut, wgt, wut, wd)


def _norm_fwd(x, gain, *, name, tm=512):
    t = x.shape[0]

    def body(x_ref, g_ref, h_ref):
        xhat, _ = _rms(x_ref[...])
        h_ref[...] = (xhat * g_ref[...]).astype(BF16)

    return pl.pallas_call(
        body,
        name=name,
        grid=(t // tm,),
        in_specs=[pl.BlockSpec((tm, D_MODEL), lambda i: (i, 0)), pl.BlockSpec((1, D_MODEL), lambda i: (0, 0))],
        out_specs=pl.BlockSpec((tm, D_MODEL), lambda i: (i, 0)),
        out_shape=jax.ShapeDtypeStruct((t, D_MODEL), BF16),
        compiler_params=_params("parallel"),
    )(x, gain)


def _norm_bwd(dh, x, gain, dres, *, name, tm=512):
    t = x.shape[0]

    def body(dh_ref, x_ref, g_ref, dr_ref, dx_ref, dg_ref):
        i = pl.program_id(0)
        xhat, rstd = _rms(x_ref[...])
        dhv = dh_ref[...]
        part = jnp.sum(dhv * xhat, axis=0, keepdims=True)

        @pl.when(i == 0)
        def _():
            dg_ref[...] = part

        @pl.when(i > 0)
        def _():
            dg_ref[...] += part

        dxh = dhv * g_ref[...]
        dx_ref[...] = dr_ref[...] + rstd * (dxh - xhat * jnp.mean(dxh * xhat, axis=-1, keepdims=True))

    row = pl.BlockSpec((tm, D_MODEL), lambda i: (i, 0))
    vec = pl.BlockSpec((1, D_MODEL), lambda i: (0, 0))
    return pl.pallas_call(
        body,
        name=name,
        grid=(t // tm,),
        in_specs=[row, row, vec, row],
        out_specs=[row, vec],
        out_shape=[jax.ShapeDtypeStruct((t, D_MODEL), F32), jax.ShapeDtypeStruct((1, D_MODEL), F32)],
        compiler_params=_params("arbitrary"),
    )(dh, x, gain, dres)


ATT_Q_TILE = 512
ATT_K_BLOCK = 256


def _first_head_lanes():
    return lax.broadcasted_iota(jnp.int32, (1, LANES), 1) < SB_HEAD_DIM


def _stack_heads(x):
    first = _first_head_lanes()
    return jnp.concatenate([jnp.where(first, x, 0.0), jnp.where(first, 0.0, x)], axis=0)


def _unstack_heads(x, rows):
    return jnp.where(_first_head_lanes(), x[:rows], x[rows:])


def _tri(n, relation):
    r = lax.broadcasted_iota(jnp.int32, (n, n), 0)
    c = lax.broadcasted_iota(jnp.int32, (n, n), 1)
    return relation(r, c).astype(BF16)


def _scan_dot(x, tri):
    hi = x.astype(BF16)
    lo = (x - hi.astype(F32)).astype(BF16)
    return _dot(jnp.concatenate([hi, lo], axis=1), jnp.concatenate([tri, tri], axis=0))


def _log_terms(z):
    lbeta = jnp.minimum(z, 0.0) - jnp.log(1.0 + jnp.exp(-jnp.abs(z)))
    return lbeta, lbeta - z


def _attn_fwd(proj, *, name):
    t = proj.shape[0]
    tq, tk = ATT_Q_TILE, ATT_K_BLOCK
    diag = tq // tk
    n_pairs = SB_WIDTH // LANES

    def body(q_ref, k_ref, v_ref, o_ref, l_ref):
        qi = pl.program_id(1)
        q = q_ref[...] * (SB_HEAD_DIM ** -0.5)
        qs = _stack_heads(q).astype(BF16)
        tri = _tri(tk, lambda j, s: j > s)
        trow = lax.broadcasted_iota(jnp.int32, (tq, tk), 0)
        scol = lax.broadcasted_iota(jnp.int32, (tq, tk), 1)

        def block(off, carry, causal):
            acc, c = carry
            lbeta, lrest = _log_terms(_dot_nt(qs, k_ref[pl.ds(off, tk), :].astype(BF16)))
            if causal is not None:
                lrest = jnp.where(causal, lrest, 0.0)
            w = jnp.exp(lbeta + (_scan_dot(lrest, tri) + c))
            if causal is not None:
                w = jnp.where(causal, w, 0.0)
            acc = acc + _dot(w.astype(BF16), v_ref[pl.ds(off, tk), :].astype(BF16))
            return acc, c + jnp.sum(lrest, axis=1, keepdims=True)

        carry = (jnp.zeros((2 * tq, LANES), F32), jnp.zeros((2 * tq, 1), F32))
        for j in reversed(range(diag)):
            off = pl.multiple_of(qi * tq + j * tk, tk)
            mask = (scol + j * tk) < trow
            carry = block(off, carry, jnp.concatenate([mask, mask], axis=0))
        n_full = qi * diag

        def step(it, carry):
            return block(pl.multiple_of((n_full - 1 - it) * tk, tk), carry, None)

        acc, c = lax.fori_loop(0, n_full, step, carry)
        o_ref[...] = _unstack_heads(acc, tq)
        l_ref[...] = _unstack_heads(jnp.broadcast_to(c, (2 * tq, LANES)), tq)

    return pl.pallas_call(
        body,
        name=name,
        grid=(n_pairs, t // tq),
        in_specs=[
            pl.BlockSpec((tq, LANES), lambda p, i: (i, p)),
            pl.BlockSpec((t, LANES), lambda p, i: (0, n_pairs + p)),
            pl.BlockSpec((t, LANES), lambda p, i: (0, 2 * n_pairs + p)),
        ],
        out_specs=[pl.BlockSpec((tq, LANES), lambda p, i: (i, p))] * 2,
        out_shape=[jax.ShapeDtypeStruct((t, SB_WIDTH), F32)] * 2,
        compiler_params=_params("parallel", "parallel"),
    )(proj, proj, proj)


def _attn_bwd(proj, ltot, do, *, name, tie=None):
    t = proj.shape[0]
    tq, tk = ATT_Q_TILE, ATT_K_BLOCK
    diag = tq // tk
    n_pairs = SB_WIDTH // LANES
    scale = SB_HEAD_DIM ** -0.5

    def body(q_ref, k_ref, v_ref, l_ref, do_ref, *rest):
        dq_ref, dk_ref, dv_ref = rest[-3:]
        qi = pl.program_id(1)

        @pl.when(qi == 0)
        def _():
            dk_ref[...] = jnp.zeros_like(dk_ref)
            dv_ref[...] = jnp.zeros_like(dv_ref)

        q = q_ref[...] * scale
        lt = l_ref[...]
        qs = _stack_heads(q).astype(BF16)
        dos = _stack_heads(do_ref[...]).astype(BF16)
        first = _first_head_lanes()
        total = jnp.concatenate([jnp.max(jnp.where(first, lt, -jnp.inf), axis=1, keepdims=True),
                                 jnp.max(jnp.where(first, -jnp.inf, lt), axis=1, keepdims=True)], axis=0)
        upto = _tri(tk, lambda j, s: j <= s)
        before = _tri(tk, lambda s, j: s < j)
        trow = lax.broadcasted_iota(jnp.int32, (tq, tk), 0)
        scol = lax.broadcasted_iota(jnp.int32, (tq, tk), 1)

        def block(off, carry, causal):
            dq, cl, cg = carry
            kblk = k_ref[pl.ds(off, tk), :].astype(BF16)
            lbeta, lrest = _log_terms(_dot_nt(qs, kblk))
            if causal is not None:
                lrest = jnp.where(causal, lrest, 0.0)
            w = jnp.exp(lbeta + (total - (_scan_dot(lrest, upto) + cl)))
            if causal is not None:
                w = jnp.where(causal, w, 0.0)
            g = w * _dot_nt(dos, v_ref[pl.ds(off, tk), :].astype(BF16))
            prior = _scan_dot(g, before) + cg
            sig = jnp.exp(lbeta)
            dz = g * (1.0 - sig) - prior * sig
            if causal is not None:
                dz = jnp.where(causal, dz, 0.0)
            dzb = dz.astype(BF16)
            dq = dq + _dot(dzb, kblk)
            dk_ref[pl.ds(off, tk), :] += _dot_tn(dzb, qs)
            dv_ref[pl.ds(off, tk), :] += _dot_tn(w.astype(BF16), dos)
            return dq, cl + jnp.sum(lrest, axis=1, keepdims=True), cg + jnp.sum(g, axis=1, keepdims=True)

        def step(kb, carry):
            return block(pl.multiple_of(kb * tk, tk), carry, None)

        zero = jnp.zeros((2 * tq, 1), F32)
        carry = lax.fori_loop(0, qi * diag, step, (jnp.zeros((2 * tq, LANES), F32), zero, zero))
        for j in range(diag):
            off = pl.multiple_of(qi * tq + j * tk, tk)
            mask = (scol + j * tk) < trow
            carry = block(off, carry, jnp.concatenate([mask, mask], axis=0))
        dq_ref[...] = _unstack_heads(carry[0], tq) * scale

    tile_spec = pl.BlockSpec((tq, LANES), lambda p, i: (i, p))
    full_spec = pl.BlockSpec((t, LANES), lambda p, i: (0, p))
    return pl.pallas_call(
        body,
        name=name,
        grid=(n_pairs, t // tq),
        in_specs=[
            tile_spec,
            pl.BlockSpec((t, LANES), lambda p, i: (0, n_pairs + p)),
            pl.BlockSpec((t, LANES), lambda p, i: (0, 2 * n_pairs + p)),
            tile_spec,
            tile_spec,
        ] + ([] if tie is None else [pl.BlockSpec(memory_space=pl.ANY)]),
        out_specs=[tile_spec, full_spec, full_spec],
        out_shape=[jax.ShapeDtypeStruct((t, SB_WIDTH), F32)] * 3,
        compiler_params=_params("arbitrary", "arbitrary"),
    )(proj, proj, proj, ltot, do, *([] if tie is None else [tie]))


HG_BLOCK = 256
HG_HEADS = HG_WIDTH // HG_HEAD_DIM


def _chunk_mats(n):
    r = lax.broadcasted_iota(jnp.int32, (n, n), 0)
    c = lax.broadcasted_iota(jnp.int32, (n, n), 1)
    same = (r // HG_CHUNK) == (c // HG_CHUNK)
    upto = (same & (c <= r)).astype(BF16)
    whole = same.astype(BF16)
    onward = (same & (c >= r)).astype(BF16)
    return upto, whole, onward


def _rows_dot(mat, x):
    return _dot(jnp.concatenate([mat, mat, mat], axis=1), jnp.concatenate(_split3(x), axis=0))


def _lower_bound(lg_ref):
    lg = lg_ref[...]
    return _sigmoid(lg[0:1, :] - lg[1:2, :])


def _hgrn_prepare(q_ref, f_ref, lb, h, upto, whole):
    cols = slice(h * HG_HEAD_DIM, (h + 1) * HG_HEAD_DIM)
    lbh = lb[:, cols]
    sg = _sigmoid(f_ref[:, cols])
    forget = lbh + (1.0 - lbh) * sg
    logf = jnp.log(forget)
    kk = (1.0 - lbh) * (1.0 - sg)
    qv = q_ref[:, cols]
    qsig = _sigmoid(qv)
    qh = qv * qsig
    b = _rows_dot(upto, logf)
    blast = _rows_dot(whole, logf)
    return dict(lbh=lbh, sg=sg, forget=forget, kk=kk, qv=qv, qsig=qsig, qh=qh, b=b, eb=jnp.exp(b),
                ekb=jnp.exp(blast - b), dl=jnp.exp(blast))


def _hgrn_fwd(proj, logits, *, name):
    t = proj.shape[0]
    tb = HG_BLOCK
    nc = tb // HG_CHUNK
    hd = HG_HEAD_DIM

    def body(q_ref, f_ref, i_ref, lg_ref, o_ref, st_ref, state, qh_s, kk_s, b_s, qe_s, ke_s, dl_s):
        @pl.when(pl.program_id(0) == 0)
        def _():
            state[...] = jnp.zeros_like(state)

        lb = _lower_bound(lg_ref)
        upto, whole, _ = _chunk_mats(tb)
        for h in range(HG_HEADS):
            p = _hgrn_prepare(q_ref, f_ref, lb, h, upto, whole)
            qh_s[h] = p["qh"]
            kk_s[h] = p["kk"]
            b_s[h] = p["b"]
            qe_s[h] = (p["qh"] * p["eb"]).astype(BF16)
            ke_s[h] = (p["kk"] * p["ekb"]).astype(BF16)
            dl_s[h] = p["dl"]
        rowi = lax.broadcasted_iota(jnp.int32, (HG_CHUNK, hd), 0)

        def chunk(c, _):
            r0 = pl.multiple_of(c * HG_CHUNK, HG_CHUNK)
            rows = pl.ds(r0, HG_CHUNK)
            for h in range(HG_HEADS):
                cols = slice(h * hd, (h + 1) * hd)
                bc = b_s[h, rows, :]
                qc = qh_s[h, rows, :]
                kc = kk_s[h, rows, :]
                vc = i_ref[rows, cols]
                s_in = state[h]
                st_ref[c, h] = s_in
                o = _dot_nt(qe_s[h, rows, :], s_in.astype(BF16))
                for s in range(HG_CHUNK):
                    pair = jnp.where(rowi >= s, qc * jnp.exp(bc - bc[s:s + 1, :]) * kc[s:s + 1, :], 0.0)
                    o = o + jnp.sum(pair, axis=1, keepdims=True) * vc[s:s + 1, :]
                o_ref[rows, cols] = o
                state[h] = s_in * dl_s[h, pl.ds(r0, 1), :] + _dot_tn(vc.astype(BF16), ke_s[h, rows, :])
            return 0

        lax.fori_loop(0, nc, chunk, 0)

    blk = lambda col: pl.BlockSpec((tb, HG_WIDTH), lambda i: (i, col))
    head_f32 = pltpu.VMEM((HG_HEADS, tb, hd), F32)
    head_bf16 = pltpu.VMEM((HG_HEADS, tb, hd), BF16)
    return pl.pallas_call(
        body,
        name=name,
        grid=(t // tb,),
        in_specs=[blk(3), blk(4), blk(5), pl.BlockSpec((2, HG_WIDTH), lambda i: (0, 0))],
        out_specs=[
            pl.BlockSpec((tb, HG_WIDTH), lambda i: (i, 0)),
            pl.BlockSpec((nc, HG_HEADS, hd, hd), lambda i: (i, 0, 0, 0)),
        ],
        out_shape=[
            jax.ShapeDtypeStruct((t, HG_WIDTH), F32),
            jax.ShapeDtypeStruct((t // HG_CHUNK, HG_HEADS, hd, hd), F32),
        ],
        scratch_shapes=[pltpu.VMEM((HG_HEADS, hd, hd), F32), head_f32, head_f32, head_f32, head_bf16, head_bf16,
                        head_f32],
        compiler_params=_params("arbitrary"),
    )(proj, proj, proj, logits)


def _hgrn_bwd(proj, logits, states, do, *, name):
    t = proj.shape[0]
    tb = HG_BLOCK
    nb = t // tb
    nc = tb // HG_CHUNK
    hd = HG_HEAD_DIM

    def body(q_ref, f_ref, i_ref, lg_ref, st_ref, do_ref, dq_ref, df_ref, di_ref, dlb_ref,
             dstate, qh_s, kk_s, b_s, eb_s, ekb_s, qe_s, ke_s, dl_s, dqh_s, dkk_s, dlf_s):
        step = pl.program_id(0)

        @pl.when(step == 0)
        def _():
            dstate[...] = jnp.zeros_like(dstate)
            dlb_ref[...] = jnp.zeros_like(dlb_ref)

        lb = _lower_bound(lg_ref)
        upto, whole, _ = _chunk_mats(tb)
        prepared = []
        for h in range(HG_HEADS):
            p = _hgrn_prepare(q_ref, f_ref, lb, h, upto, whole)
            prepared.append(p)
            qh_s[h] = p["qh"]
            kk_s[h] = p["kk"]
            b_s[h] = p["b"]
            eb_s[h] = p["eb"]
            ekb_s[h] = p["ekb"]
            qe_s[h] = (p["qh"] * p["eb"]).astype(BF16)
            ke_s[h] = (p["kk"] * p["ekb"]).astype(BF16)
            dl_s[h] = p["dl"]
        rowi = lax.broadcasted_iota(jnp.int32, (HG_CHUNK, hd), 0)
        r16 = lax.broadcasted_iota(jnp.int32, (HG_CHUNK, HG_CHUNK), 0)
        c16 = lax.broadcasted_iota(jnp.int32, (HG_CHUNK, HG_CHUNK), 1)
        onward = (c16 >= r16).astype(BF16)

        def chunk(it, _):
            c = nc - 1 - it
            r0 = pl.multiple_of(c * HG_CHUNK, HG_CHUNK)
            rows = pl.ds(r0, HG_CHUNK)
            for h in range(HG_HEADS):
                cols = slice(h * hd, (h + 1) * hd)
                bc = b_s[h, rows, :]
                qc = qh_s[h, rows, :]
                kc = kk_s[h, rows, :]
                vc = i_ref[rows, cols]
                doc = do_ref[rows, cols]
                s_in = st_ref[c, h]
                ds_out = dstate[h]
                ds_out_b = ds_out.astype(BF16)
                docb = doc.astype(BF16)
                dl_row = dl_s[h, pl.ds(r0, 1), :]
                dqh = _dot(docb, s_in.astype(BF16)) * eb_s[h, rows, :]
                dkk = _dot(vc.astype(BF16), ds_out_b) * ekb_s[h, rows, :]
                dv = _dot_nt(ke_s[h, rows, :], ds_out_b)
                db = dqh * qc - dkk * kc
                dwhole = jnp.sum(dkk * kc, axis=0, keepdims=True) + jnp.sum(ds_out * s_in, axis=0, keepdims=True) * dl_row
                for s in range(HG_CHUNK):
                    keep = rowi >= s
                    at_s = rowi == s
                    e = jnp.exp(bc - bc[s:s + 1, :])
                    k_row = kc[s:s + 1, :]
                    pcol = jnp.sum(jnp.where(keep, qc * e * k_row, 0.0), axis=1, keepdims=True)
                    dpcol = jnp.sum(doc * vc[s:s + 1, :], axis=1, keepdims=True)
                    m = jnp.where(keep, e * dpcol, 0.0)
                    mk = m * k_row
                    dk_row = jnp.sum(m * qc, axis=0, keepdims=True)
                    dqh = dqh + mk
                    dkk = dkk + jnp.where(at_s, dk_row, 0.0)
                    db = db + mk * qc - jnp.where(at_s, dk_row * k_row, 0.0)
                    dv = dv + jnp.where(at_s, jnp.sum(pcol * doc, axis=0, keepdims=True), 0.0)
                dqh_s[h, rows, :] = dqh
                dkk_s[h, rows, :] = dkk
                dlf_s[h, rows, :] = _rows_dot(onward, db) + dwhole
                di_ref[rows, cols] = dv
                dstate[h] = ds_out * dl_row + _dot_tn(docb, qe_s[h, rows, :])
            return 0

        lax.fori_loop(0, nc, chunk, 0)
        for h in range(HG_HEADS):
            cols = slice(h * hd, (h + 1) * hd)
            p = prepared[h]
            dq_ref[:, cols] = dqh_s[h] * (p["qsig"] * (1.0 + p["qv"] * (1.0 - p["qsig"])))
            dforget = dlf_s[h] / p["forget"] - dkk_s[h]
            df_ref[:, cols] = dforget * (1.0 - p["lbh"]) * p["sg"] * (1.0 - p["sg"])
            dlb_ref[:, cols] += jnp.sum(dforget * (1.0 - p["sg"]), axis=0, keepdims=True)

    blk = lambda col: pl.BlockSpec((tb, HG_WIDTH), lambda i: (nb - 1 - i, col))
    vec = pl.BlockSpec((1, HG_WIDTH), lambda i: (0, 0))
    head_f32 = pltpu.VMEM((HG_HEADS, tb, hd), F32)
    head_bf16 = pltpu.VMEM((HG_HEADS, tb, hd), BF16)
    return pl.pallas_call(
        body,
        name=name,
        grid=(nb,),
        in_specs=[
            blk(3), blk(4), blk(5),
            pl.BlockSpec((2, HG_WIDTH), lambda i: (0, 0)),
            pl.BlockSpec((nc, HG_HEADS, hd, hd), lambda i: (nb - 1 - i, 0, 0, 0)),
            blk(0),
        ],
        out_specs=[blk(0), blk(0), blk(0), vec],
        out_shape=[jax.ShapeDtypeStruct((t, HG_WIDTH), F32)] * 3 + [jax.ShapeDtypeStruct((1, HG_WIDTH), F32)],
        scratch_shapes=[
            pltpu.VMEM((HG_HEADS, hd, hd), F32),
            head_f32, head_f32, head_f32, head_f32, head_f32, head_bf16, head_bf16, head_f32,
            head_f32, head_f32, head_f32,
        ],
        compiler_params=_params("arbitrary"),
    )(proj, proj, proj, logits, states, do)


def _group_mat(width, head_dim):
    r = lax.broadcasted_iota(jnp.int32, (width, width), 0)
    c = lax.broadcasted_iota(jnp.int32, (width, width), 1)
    return ((r // head_dim) == (c // head_dim)).astype(BF16)


def _head_mean(x, mat, head_dim):
    hi = x.astype(BF16)
    lo = (x - hi.astype(F32)).astype(BF16)
    return (_dot(hi, mat) + _dot(lo, mat)) * (1.0 / head_dim)


def _mix_out_fwd(o_sb, o_hg, proj, g_sb, g_hg, w_out, x1, *, name, tm=256):
    t = x1.shape[0]

    def body(osb_ref, ohg_ref, gate_ref, gsb_ref, ghg_ref, w_ref, x_ref, xo_ref, mt_ref):
        msb = _group_mat(SB_WIDTH, SB_HEAD_DIM)
        mhg = _group_mat(HG_WIDTH, HG_HEAD_DIM)
        osb = osb_ref[...]
        ohg = ohg_ref[...]
        nsb = osb * lax.rsqrt(_head_mean(osb * osb, msb, SB_HEAD_DIM) + EPS) * gsb_ref[...]
        gate = gate_ref[...]
        nhg = ohg * lax.rsqrt(_head_mean(ohg * ohg, mhg, HG_HEAD_DIM) + EPS) * ghg_ref[...] * (gate * _sigmoid(gate))
        mixed = jnp.concatenate([nsb, nhg], axis=1).astype(BF16)
        mt_ref[...] = mixed
        xo_ref[...] = x_ref[...] + _dot(mixed, w_ref[...])

    half = pl.BlockSpec((tm, SB_WIDTH), lambda i: (i, 0))
    vec = pl.BlockSpec((1, SB_WIDTH), lambda i: (0, 0))
    row = pl.BlockSpec((tm, D_MODEL), lambda i: (i, 0))
    return pl.pallas_call(
        body,
        name=name,
        grid=(t // tm,),
        in_specs=[half, half, pl.BlockSpec((tm, HG_WIDTH), lambda i: (i, 6)), vec, vec,
                  pl.BlockSpec((D_MODEL, D_MODEL), lambda i: (0, 0)), row],
        out_specs=[row, row],
        out_shape=[jax.ShapeDtypeStruct((t, D_MODEL), F32), jax.ShapeDtypeStruct((t, D_MODEL), BF16)],
        compiler_params=_params("parallel"),
    )(o_sb, o_hg, proj, g_sb, g_hg, w_out, x1)


def _mix_out_bwd(dx2, o_sb, o_hg, proj, g_sb, g_hg, w_out, *, name, tm=256):
    t = dx2.shape[0]

    def body(dx_ref, osb_ref, ohg_ref, gate_ref, gsb_ref, ghg_ref, w_ref, dosb_ref, dohg_ref, dgate_ref, dgsb_ref,
             dghg_ref, dxb_ref):
        i = pl.program_id(0)
        msb = _group_mat(SB_WIDTH, SB_HEAD_DIM)
        mhg = _group_mat(HG_WIDTH, HG_HEAD_DIM)
        dxb = dx_ref[...].astype(BF16)
        dxb_ref[...] = dxb
        dmixed = _dot_nt(dxb, w_ref[...])
        dnsb = dmixed[:, :SB_WIDTH]
        dy = dmixed[:, SB_WIDTH:]

        osb = osb_ref[...]
        rstd = lax.rsqrt(_head_mean(osb * osb, msb, SB_HEAD_DIM) + EPS)
        ohat = osb * rstd
        part_sb = jnp.sum(dnsb * ohat, axis=0, keepdims=True)
        dohat = dnsb * gsb_ref[...]
        dosb_ref[...] = rstd * (dohat - ohat * _head_mean(dohat * ohat, msb, SB_HEAD_DIM))

        ohg = ohg_ref[...]
        rstd = lax.rsqrt(_head_mean(ohg * ohg, mhg, HG_HEAD_DIM) + EPS)
        ohat = ohg * rstd
        gate = gate_ref[...]
        sig = _sigmoid(gate)
        dn = dy * (gate * sig)
        dgate_ref[...] = dy * (ohat * ghg_ref[...]) * (sig * (1.0 + gate * (1.0 - sig)))
        part_hg = jnp.sum(dn * ohat, axis=0, keepdims=True)
        dohat = dn * ghg_ref[...]
        dohg_ref[...] = rstd * (dohat - ohat * _head_mean(dohat * ohat, mhg, HG_HEAD_DIM))

        @pl.when(i == 0)
        def _():
            dgsb_ref[...] = part_sb
            dghg_ref[...] = part_hg

        @pl.when(i > 0)
        def _():
            dgsb_ref[...] += part_sb
            dghg_ref[...] += part_hg

    half = pl.BlockSpec((tm, SB_WIDTH), lambda i: (i, 0))
    vec = pl.BlockSpec((1, SB_WIDTH), lambda i: (0, 0))
    row = pl.BlockSpec((tm, D_MODEL), lambda i: (i, 0))
    return pl.pallas_call(
        body,
        name=name,
        grid=(t // tm,),
        in_specs=[row, half, half, pl.BlockSpec((tm, HG_WIDTH), lambda i: (i, 6)), vec, vec,
                  pl.BlockSpec((D_MODEL, D_MODEL), lambda i: (0, 0))],
        out_specs=[half, half, half, vec, vec, row],
        out_shape=[jax.ShapeDtypeStruct((t, SB_WIDTH), F32)] * 3 + [jax.ShapeDtypeStruct((1, SB_WIDTH), F32)] * 2
        + [jax.ShapeDtypeStruct((t, D_MODEL), BF16)],
        compiler_params=_params("arbitrary"),
    )(dx2, o_sb, o_hg, proj, g_sb, g_hg, w_out)


def _loss_head(x3, gain, target, *, name, tm=512):
    t = x3.shape[0]

    def body(x_ref, g_ref, y_ref, dx_ref, dg_ref, loss_ref):
        i = pl.program_id(0)
        xhat, rstd = _rms(x_ref[...])
        err = xhat * g_ref[...] - y_ref[...]
        part_loss = 0.5 * jnp.sum(jnp.mean(err * err, axis=-1, keepdims=True), axis=0, keepdims=True)
        dy = err * (1.0 / D_MODEL)
        part_g = jnp.sum(dy * xhat, axis=0, keepdims=True)

        @pl.when(i == 0)
        def _():
            dg_ref[...] = part_g
            loss_ref[...] = jnp.broadcast_to(part_loss, loss_ref.shape)

        @pl.when(i > 0)
        def _():
            dg_ref[...] += part_g
            loss_ref[...] += jnp.broadcast_to(part_loss, loss_ref.shape)

        dxh = dy * g_ref[...]
        dx_ref[...] = rstd * (dxh - xhat * jnp.mean(dxh * xhat, axis=-1, keepdims=True))

    row = pl.BlockSpec((tm, D_MODEL), lambda i: (i, 0))
    vec = pl.BlockSpec((1, D_MODEL), lambda i: (0, 0))
    return pl.pallas_call(
        body,
        name=name,
        grid=(t // tm,),
        in_specs=[row, vec, row],
        out_specs=[row, vec, vec],
        out_shape=[jax.ShapeDtypeStruct((t, D_MODEL), F32), jax.ShapeDtypeStruct((1, D_MODEL), F32),
                   jax.ShapeDtypeStruct((1, D_MODEL), F32)],
        compiler_params=_params("arbitrary"),
    )(x3, gain, target)


def _local_step(x, target, norms, logits, w, weights_after=None, grads_ready=None):
    w = dict(w)
    x1, a1, b1, h1, s1 = _ffn_fwd(x, norms["ffn1"], w["g1t"], w["u1t"], w["d1"], name="ffn1_fwd")
    if weights_after is not None:
        w.update(weights_after("ffn1", x1))
    hm = _norm_fwd(x1, norms["mix"], name="mix_norm_fwd")
    proj = _mm(hm, w["in"], name="in_proj", tm=512, tn=IN_COLS)
    o_sb, ltot = _attn_fwd(proj, name="sb_attn_fwd")
    o_hg, states = _hgrn_fwd(proj, logits, name="hgrn2_fwd")
    x2, mixed = _mix_out_fwd(o_sb, o_hg, proj, norms["sb"], norms["hg"], w["out"], x1, name="mix_out_fwd")
    if weights_after is not None:
        w.update(weights_after("mix", x2))
    x3, a2, b2, h2, s2 = _ffn_fwd(x2, norms["ffn2"], w["g2t"], w["u2t"], w["d2"], name="ffn2_fwd")
    dx3, d_final, loss_row = _loss_head(x3, norms["final"], target, name="loss_head")

    def weight_grad(lhs, rhs, name, tie=None):
        return _mm(lhs, rhs, name=name, tm=256, tn=D_MODEL, ta=True, out_dtype=BF16, tie=tie)

    def sent(stage):
        return grads_ready(stage, gw) if grads_ready is not None else None

    gw, gv = {}, {"final": d_final}
    dx2, gv["ffn2"], da2, db2, dob2 = _ffn_bwd(dx3, x2, norms["ffn2"], a2, b2, w["g2t"], w["u2t"], w["d2"],
                                               name="ffn2_bwd")
    gw["g2t"] = weight_grad(da2, h2, "ffn2_dgate")
    gw["u2t"] = weight_grad(db2, h2, "ffn2_dup")
    gw["d2"] = weight_grad(s2, dob2, "ffn2_ddown")

    do_sb, do_hg, d_gate, gv["sb"], gv["hg"], dx2b = _mix_out_bwd(
        dx2, o_sb, o_hg, proj, norms["sb"], norms["hg"], w["out"], name="mix_out_bwd")
    gw["out"] = weight_grad(mixed, dx2b, "out_dw")
    tie = sent("mix")
    dq_sb, dk_sb, dv_sb = _attn_bwd(proj, ltot, do_sb, name="sb_attn_bwd", tie=tie)
    dq_hg, df_hg, di_hg, d_lb = _hgrn_bwd(proj, logits if tie is None else logits + tie[0, 0], states, do_hg,
                                          name="hgrn2_bwd")
    dproj = jnp.concatenate([dq_sb, dk_sb, dv_sb, dq_hg, df_hg, di_hg, d_gate], axis=1).astype(BF16)
    gw["in"] = _mm(hm, dproj, name="in_dw", tm=D_MODEL, tn=256, ta=True)
    tie = sent("in")
    dhm = _mm(dproj, w["in"], name="in_dx", tm=512, tn=D_MODEL, nt=True)
    dx1, gv["mix"] = _norm_bwd(dhm, x1, norms["mix"] if tie is None else norms["mix"] + tie[0, 0], dx2,
                               name="mix_norm_bwd")

    dx, gv["ffn1"], da1, db1, dob1 = _ffn_bwd(dx1, x, norms["ffn1"], a1, b1, w["g1t"], w["u1t"], w["d1"],
                                              name="ffn1_bwd")
    gw["g1t"] = weight_grad(da1, h1, "ffn1_dgate")
    gw["u1t"] = weight_grad(db1, h1, "ffn1_dup", tie=sent("g1t"))
    gw["d1"] = weight_grad(s1, dob1, "ffn1_ddown", tie=sent("u1t"))
    sent("d1")
    gv["lb"] = d_lb
    return loss_row, dx, gw, gv


HBM = pl.BlockSpec(memory_space=pl.ANY)


def _place():
    return lax.axis_index("x"), lax.axis_index("y"), lax.axis_index("c")


def _slot(px, py, pc):
    return 4 * px + 2 * py + pc


def _all_gather(blocks, *, name):
    n = len(blocks)

    def body(*refs):
        ins, outs = refs[:n], refs[n:2 * n]
        send_sems, recv_sems, local_sems = refs[2 * n:]
        x, y, c = _place()
        me, sibling = (x, y, c), (x, y, 1 - c)
        chips = [(1 - x, y), (x, 1 - y), (1 - x, 1 - y)]

        def copy(a, k, block, to, src=None):
            dst = outs[a].at[_slot(*block)]
            return pltpu.make_async_remote_copy(
                src_ref=dst if src is None else src, dst_ref=dst, send_sem=send_sems.at[7 * a + k],
                recv_sem=recv_sems.at[7 * a + k], device_id=to, device_id_type=MESH)

        mine = [pltpu.make_async_copy(ins[a], outs[a].at[_slot(*me)], local_sems.at[a]) for a in range(n)]
        for cp in mine:
            cp.start()
        first = []
        for a in range(n):
            first.append(copy(a, 0, me, sibling, src=ins[a]))
            first += [copy(a, 1 + j, me, (*chip, c), src=ins[a]) for j, chip in enumerate(chips)]
        for cp in first:
            cp.start()
        passed = []
        for j, chip in enumerate(chips):
            for a in range(n):
                copy(a, 1 + j, (*chip, c), me).wait_recv()
                fwd = copy(a, 4 + j, (*chip, c), sibling)
                fwd.start()
                passed.append(fwd)
        for a in range(n):
            copy(a, 0, sibling, me).wait_recv()
            for j, chip in enumerate(chips):
                copy(a, 4 + j, (*chip, 1 - c), me).wait_recv()
        for cp in first + passed:
            cp.wait_send()
        for cp in mine:
            cp.wait()

    return pl.pallas_call(
        body,
        name=name,
        in_specs=[HBM] * n,
        out_specs=[HBM] * n,
        out_shape=[jax.ShapeDtypeStruct((N_DEV,) + b.shape, b.dtype) for b in blocks],
        scratch_shapes=[pltpu.SemaphoreType.DMA((7 * n,)), pltpu.SemaphoreType.DMA((7 * n,)),
                        pltpu.SemaphoreType.DMA((n,))],
    )(*blocks)


def _flipped(place, d):
    return tuple(1 - p if (d >> (2 - axis)) & 1 else p for axis, p in enumerate(place))


def _scatter_by_owner(stacks, *, name):
    n = len(stacks)

    def body(*refs):
        ins, outs = refs[:n], refs[n:2 * n]
        send_sems, recv_sems, local_sems = refs[2 * n:]
        me = _place()
        mine = [pltpu.make_async_copy(ins[a].at[_slot(*me)], outs[a].at[_slot(*me)], local_sems.at[a]) for a in range(n)]
        for cp in mine:
            cp.start()
        copies = []
        for d in range(1, N_DEV):
            peer = _flipped(me, d)
            for a in range(n):
                copies.append(pltpu.make_async_remote_copy(
                    src_ref=ins[a].at[_slot(*peer)], dst_ref=outs[a].at[_slot(*me)], send_sem=send_sems.at[7 * a + d - 1],
                    recv_sem=recv_sems.at[7 * a + d - 1], device_id=peer, device_id_type=MESH))
        for cp in copies:
            cp.start()
        for cp in copies:
            cp.wait()
        for cp in mine:
            cp.wait()

    return pl.pallas_call(
        body,
        name=name,
        in_specs=[HBM] * n,
        out_specs=[HBM] * n,
        out_shape=[jax.ShapeDtypeStruct(s.shape, s.dtype) for s in stacks],
        scratch_shapes=[pltpu.SemaphoreType.DMA((7 * n,)), pltpu.SemaphoreType.DMA((7 * n,)),
                        pltpu.SemaphoreType.DMA((n,))],
    )(*stacks)


SEM = pl.BlockSpec(memory_space=pltpu.SEMAPHORE)
EFFECT = pltpu.SideEffectType.DATAFLOW_SIDE_EFFECTING


def _split_copies(me, srcs, lands, send_sems, recv_sems, by_owner):
    copies = []
    for d in range(1, N_DEV):
        peer = _flipped(me, d)
        for a, (src, land) in enumerate(zip(srcs, lands)):
            copies.append(pltpu.make_async_remote_copy(
                src_ref=src.at[_slot(*peer)] if by_owner else src, dst_ref=land.at[_slot(*me)],
                send_sem=send_sems.at[7 * a + d - 1], recv_sem=recv_sems.at[7 * a + d - 1], device_id=peer,
                device_id_type=MESH))
    return copies


def _copies_start(srcs, *, name, by_owner, after=None):
    n = len(srcs)
    extra = [] if after is None else [after]
    land_shapes = [s.shape if by_owner else (N_DEV,) + s.shape for s in srcs]
    lands = [pltpu.with_memory_space_constraint(lax.empty(shape, s.dtype), pltpu.HBM) for shape, s in zip(land_shapes, srcs)]
    srcs = [pltpu.with_memory_space_constraint(s, pltpu.HBM) for s in srcs]

    def body(*refs):
        src_refs, land_refs = refs[:n], refs[n:2 * n]
        send_sems, recv_sems = refs[2 * n + len(extra)], refs[2 * n + len(extra) + 1]
        token = refs[-1]
        for cp in _split_copies(_place(), src_refs, land_refs, send_sems, recv_sems, by_owner):
            cp.start()
        token[...] = jnp.zeros_like(token)

    out = pl.pallas_call(
        body,
        name=name,
        in_specs=[HBM] * (2 * n + len(extra)),
        out_specs=[SEM, SEM] + [HBM] * (2 * n) + [pl.BlockSpec(memory_space=pltpu.VMEM)],
        out_shape=[pltpu.SemaphoreType.DMA((7 * n,)), pltpu.SemaphoreType.DMA((7 * n,))]
        + [pltpu.HBM(s.shape, s.dtype) for s in srcs] + [pltpu.HBM(shape, s.dtype) for shape, s in zip(land_shapes, srcs)]
        + [jax.ShapeDtypeStruct((8, LANES), F32)],
        input_output_aliases={i: 2 + i for i in range(2 * n)},
        compiler_params=pltpu.CompilerParams(has_side_effects=EFFECT),
    )(*srcs, *lands, *extra)
    return (out[0], out[1], out[2:2 + n], out[2 + n:2 + 2 * n]), out[-1]


def _copies_wait(started, after, *, name, by_owner):
    send_sems, recv_sems, srcs, lands = started
    n = len(srcs)

    def body(*refs):
        src_refs, land_refs = refs[:n], refs[n:2 * n]
        for cp in _split_copies(_place(), src_refs, land_refs, refs[2 * n], refs[2 * n + 1], by_owner):
            cp.wait_send()
            cp.wait_recv()

    out = pl.pallas_call(
        body,
        name=name,
        in_specs=[HBM] * (2 * n) + [SEM, SEM, HBM],
        out_specs=[HBM] * (2 * n),
        out_shape=[pltpu.HBM(s.shape, s.dtype) for s in srcs] + [pltpu.HBM(s.shape, s.dtype) for s in lands],
        input_output_aliases={i: i for i in range(2 * n)},
        compiler_params=pltpu.CompilerParams(has_side_effects=EFFECT),
    )(*srcs, *lands, send_sems, recv_sems, after)
    return out[n:]


def _with_own(lands, own, slot):
    zero = jnp.zeros((), jnp.int32)
    return [lax.dynamic_update_slice(land, o[None], (slot.astype(jnp.int32),) + (zero,) * o.ndim)
            for land, o in zip(lands, own)]


def _adamw(w, g, m, v):
    m = ADAM_B1 * m + (1.0 - ADAM_B1) * g
    v = ADAM_B2 * v + (1.0 - ADAM_B2) * (g * g)
    m_hat = m / (1.0 - ADAM_B1 ** ADAM_STEP)
    v_hat = v / (1.0 - ADAM_B2 ** ADAM_STEP)
    delta = -ADAM_LR * (m_hat / (jnp.sqrt(v_hat) + ADAM_EPS) + ADAM_WD * w)
    return delta, m, v


def _sum_and_update(parts, w, m, v, *, name, tr, transposed=False):
    _, rows, cols = w.shape
    pad = -cols % LANES

    def body(p_ref, w_ref, m_ref, v_ref, g_ref, d_ref, mo_ref, vo_ref):
        g = p_ref[0].astype(F32)
        for s in range(1, N_DEV):
            g = g + p_ref[s].astype(F32)
        if transposed:
            if pad:
                g = jnp.concatenate([g, jnp.zeros((pad, tr), F32)], axis=0)
            g = g.T[:, :cols]
        g_ref[0] = g
        d_ref[0], mo_ref[0], vo_ref[0] = _adamw(w_ref[0], g, m_ref[0], v_ref[0])

    flat = pl.BlockSpec((1, tr, cols), lambda i: (0, i, 0))
    if transposed:
        part_spec = pl.BlockSpec((N_DEV, cols, tr), lambda i: (0, 0, i))
    else:
        part_spec = pl.BlockSpec((N_DEV, tr, cols), lambda i: (0, i, 0))
    return pl.pallas_call(
        body,
        name=name,
        grid=(rows // tr,),
        in_specs=[part_spec, flat, flat, flat],
        out_specs=[flat] * 4,
        out_shape=[jax.ShapeDtypeStruct((1, rows, cols), F32)] * 4,
        compiler_params=_params("parallel"),
    )(parts, w, m, v)


VEC_ROWS = 8
ROW_LOGITS, ROW_LOSS = 5, 7


def _vectors_update(part, w, m, v, *, name):
    def body(p_ref, w_ref, m_ref, v_ref, g_ref, d_ref, mo_ref, vo_ref, loss_ref, all_ref, send_sems, recv_sems):
        me = _place()
        all_ref[_slot(*me)] = p_ref[...]
        copies = []
        for d in range(1, N_DEV):
            peer = _flipped(me, d)
            copies.append(pltpu.make_async_remote_copy(
                src_ref=p_ref, dst_ref=all_ref.at[_slot(*me)], send_sem=send_sems.at[d - 1], recv_sem=recv_sems.at[d - 1],
                device_id=peer, device_id_type=MESH))
        for cp in copies:
            cp.start()
        for cp in copies:
            cp.wait()
        total = all_ref[0]
        for s in range(1, N_DEV):
            total = total + all_ref[s]
        wv = w_ref[...]
        half = D_MODEL // 2
        lb = _sigmoid(wv[ROW_LOGITS:ROW_LOGITS + 1, :half] - wv[ROW_LOGITS:ROW_LOGITS + 1, half:])
        d_first = total[ROW_LOGITS:ROW_LOGITS + 1, :half] * lb * (1.0 - lb)
        d_logits = jnp.concatenate([d_first, -d_first], axis=1)
        rowi = lax.broadcasted_iota(jnp.int32, (VEC_ROWS, D_MODEL), 0)
        g = jnp.where(rowi == ROW_LOGITS, d_logits, jnp.where(rowi < ROW_LOGITS, total, 0.0))
        g_ref[...] = g
        d_ref[...], mo_ref[...], vo_ref[...] = _adamw(wv, g, m_ref[...], v_ref[...])
        loss_ref[...] = total[ROW_LOSS:ROW_LOSS + 1, :]

    vmem = pl.BlockSpec(memory_space=pltpu.VMEM)
    return pl.pallas_call(
        body,
        name=name,
        in_specs=[vmem] * 4,
        out_specs=[vmem] * 5,
        out_shape=[jax.ShapeDtypeStruct((VEC_ROWS, D_MODEL), F32)] * 4 + [jax.ShapeDtypeStruct((1, D_MODEL), F32)],
        scratch_shapes=[pltpu.VMEM((N_DEV, VEC_ROWS, D_MODEL), F32), pltpu.SemaphoreType.DMA((7,)),
                        pltpu.SemaphoreType.DMA((7,))],
    )(part, w, m, v)


ROW_SHARDED = ("d1", "d2", "out")
TRANSPOSED = ("g1t", "u1t", "g2t", "u2t")


def _vector_rows(rows):
    rowi = lax.broadcasted_iota(jnp.int32, (VEC_ROWS, D_MODEL), 0)
    out = jnp.zeros((VEC_ROWS, D_MODEL), F32)
    for i, r in enumerate(rows):
        if r is not None:
            out = jnp.where(rowi == i, r, out)
    return out


def kernel(x, ffn1_norm, ffn1_w_gate, ffn1_w_up, ffn1_w_down, mix_norm, w_in, sb_out_norm, hg_lower_bound_logits, hg_out_norm, w_out, ffn2_norm, ffn2_w_gate, ffn2_w_up, ffn2_w_down, final_norm, loss_target, m_ffn1_norm, m_ffn1_w_gate, m_ffn1_w_up, m_ffn1_w_down, m_mix_norm, m_w_in, m_sb_out_norm, m_hg_lower_bound_logits, m_hg_out_norm, m_w_out, m_ffn2_norm, m_ffn2_w_gate, m_ffn2_w_up, m_ffn2_w_down, m_final_norm, v_ffn1_norm, v_ffn1_w_gate, v_ffn1_w_up, v_ffn1_w_down, v_mix_norm, v_w_in, v_sb_out_norm, v_hg_lower_bound_logits, v_hg_out_norm, v_w_out, v_ffn2_norm, v_ffn2_w_gate, v_ffn2_w_up, v_ffn2_w_down, v_final_norm):
    def matrices(g1, u1, d1, win, wout, g2, u2, d2):
        return {"g1t": g1, "u1t": u1, "d1": d1, "in": win, "out": wout, "g2t": g2, "u2t": u2, "d2": d2}

    def vectors(n1, nm, nsb, lg, nhg, n2, nf):
        return [n1, nm, n2, nf.reshape(1, D_MODEL), jnp.concatenate([nsb, nhg], axis=1), lg.reshape(1, D_MODEL), None, None]

    w_sh = matrices(ffn1_w_gate, ffn1_w_up, ffn1_w_down, w_in, w_out, ffn2_w_gate, ffn2_w_up, ffn2_w_down)
    m_sh = matrices(m_ffn1_w_gate, m_ffn1_w_up, m_ffn1_w_down, m_w_in, m_w_out, m_ffn2_w_gate, m_ffn2_w_up, m_ffn2_w_down)
    v_sh = matrices(v_ffn1_w_gate, v_ffn1_w_up, v_ffn1_w_down, v_w_in, v_w_out, v_ffn2_w_gate, v_ffn2_w_up, v_ffn2_w_down)
    keys = list(w_sh)

    slot = _slot(*_place())

    def full(key, stack):
        if key == "in":
            return stack.transpose(1, 0, 2).reshape(D_MODEL, IN_COLS)
        return stack.reshape(-1, D_MODEL)

    def by_owner(key, grad):
        if key == "in":
            return grad.reshape(D_MODEL, N_DEV, IN_SHARD).transpose(1, 0, 2).astype(BF16)
        return grad.reshape(N_DEV, -1, D_MODEL)

    blocks = {k: (w_sh[k][0].T if k in TRANSPOSED else w_sh[k][0]).astype(BF16) for k in keys}
    first, mid, last = ("g1t", "u1t", "d1"), ("in", "out"), ("g2t", "u2t", "d2")
    w_first = {k: full(k, s) for k, s in zip(first, _all_gather([blocks[k] for k in first], name="gather_ffn1"))}
    flights = {}
    flights["ffn1"], token_mid = _copies_start([blocks[k] for k in mid], name="gather_mid_start", by_owner=False,
                                               after=w_first["d1"])
    flights["mix"], token_last = _copies_start([blocks[k] for k in last], name="gather_ffn2_start", by_owner=False,
                                               after=token_mid)

    def weights_after(stage, result):
        group = mid if stage == "ffn1" else last
        lands = _copies_wait(flights[stage], result, name="gather_" + stage + "_wait", by_owner=False)
        return {k: full(k, s) for k, s in zip(group, _with_own(lands, [blocks[k] for k in group], slot))}

    groups = {"mix": ("g2t", "u2t", "d2", "out"), "in": ("in",), "g1t": ("g1t",), "u1t": ("u1t",), "d1": ("d1",)}
    sent, sent_tokens = {}, []

    def grads_ready(stage, gw):
        stacks = [by_owner(k, gw[k]) for k in groups[stage]]
        flight, token = _copies_start(stacks, name="grads_" + stage + "_start", by_owner=True)
        sent[stage] = (stacks, flight)
        sent_tokens.append(token)
        return token

    norms = {"ffn1": ffn1_norm + token_last[0, 0], "mix": mix_norm, "sb": sb_out_norm, "hg": hg_out_norm,
             "ffn2": ffn2_norm, "final": final_norm.reshape(1, D_MODEL)}
    loss_row, grad_x, gw, gv = _local_step(x[0], loss_target[0], norms, hg_lower_bound_logits, w_first, weights_after,
                                           grads_ready)

    tiles = {"g1t": 256, "u1t": 256, "g2t": 256, "u2t": 256, "d1": 176, "d2": 176, "out": 128, "in": 256}
    updated, after = {}, sent_tokens[-1]
    for stage, (stacks, flight) in sent.items():
        lands = _copies_wait(flight, after, name="grads_" + stage + "_wait", by_owner=True)
        own = [lax.dynamic_index_in_dim(s, slot, keepdims=False) for s in stacks]
        for k, part in zip(groups[stage], _with_own(lands, own, slot)):
            updated[k] = _sum_and_update(part, w_sh[k], m_sh[k], v_sh[k], name="adamw_" + k, tr=tiles[k],
                                         transposed=k in TRANSPOSED)
            after = updated[k][0]
    mats = [{k: updated[k][i] for k in keys} for i in range(4)]

    lb_row = jnp.concatenate([gv["lb"], jnp.zeros_like(gv["lb"])], axis=1)
    part = _vector_rows([gv["ffn1"], gv["mix"], gv["ffn2"], gv["final"], jnp.concatenate([gv["sb"], gv["hg"]], axis=1),
                         lb_row, None, loss_row])
    vec_w = _vector_rows(vectors(ffn1_norm, mix_norm, sb_out_norm, hg_lower_bound_logits, hg_out_norm, ffn2_norm, final_norm))
    vec_m = _vector_rows(vectors(m_ffn1_norm, m_mix_norm, m_sb_out_norm, m_hg_lower_bound_logits, m_hg_out_norm,
                                 m_ffn2_norm, m_final_norm))
    vec_v = _vector_rows(vectors(v_ffn1_norm, v_mix_norm, v_sb_out_norm, v_hg_lower_bound_logits, v_hg_out_norm,
                                 v_ffn2_norm, v_final_norm))
    *vecs, loss_out = _vectors_update(part, vec_w, vec_m, vec_v, name="vectors_update")

    def leaves(mat, vec):
        half = D_MODEL // 2
        return (
            vec[0:1], mat["g1t"], mat["u1t"], mat["d1"], vec[1:2], mat["in"], vec[4:5, :half],
            vec[ROW_LOGITS].reshape(2, half), vec[4:5, half:], mat["out"], vec[2:3], mat["g2t"], mat["u2t"],
            mat["d2"], vec[3],
        )

    out = [loss_out[0, 0], grad_x[None]]
    for mat, vec in zip(mats, vecs):
        out.extend(leaves(mat, vec))
    return tuple(out)
```

```python
import jax
import jax.numpy as jnp
from jax import lax
from jax.experimental import pallas as pl
from jax.experimental.pallas import tpu as pltpu

F32, BF16 = jnp.float32, jnp.bfloat16
D_MODEL = 1024
D_FF = 2816
SB_WIDTH = 512
HG_WIDTH = 512
SB_HEAD_DIM = 64
HG_HEAD_DIM = 128
IN_COLS = 3584
EPS = 1e-6
N_DEV = 8
FF_SHARD = D_FF // N_DEV
IN_SHARD = IN_COLS // N_DEV
OUT_SHARD = D_MODEL // N_DEV
LANES = 128
HG_CHUNK = 16
VMEM_LIMIT_BYTES = 48 * 1024 * 1024
FFN_BWD_VMEM_LIMIT_BYTES = 56 * 1024 * 1024
ADAM_LR, ADAM_B1, ADAM_B2, ADAM_EPS, ADAM_WD, ADAM_STEP = 0.001, 0.9, 0.999, 1e-08, 0.01, 10
MESH = pl.DeviceIdType.MESH


def _params(*semantics, vmem_limit_bytes=VMEM_LIMIT_BYTES):
    return pltpu.CompilerParams(dimension_semantics=semantics, vmem_limit_bytes=vmem_limit_bytes)


def _dot(a, b):
    return jnp.dot(a, b, preferred_element_type=F32)


def _dot_nt(a, b):
    return lax.dot_general(a, b, (((1,), (1,)), ((), ())), preferred_element_type=F32)


def _dot_tn(a, b):
    return lax.dot_general(a, b, (((0,), (0,)), ((), ())), preferred_element_type=F32)


def _split3(x):
    hi = x.astype(BF16)
    r1 = x - hi.astype(F32)
    mid = r1.astype(BF16)
    lo = (r1 - mid.astype(F32)).astype(BF16)
    return hi, mid, lo


def _rms(xv):
    rstd = lax.rsqrt(jnp.mean(xv * xv, axis=-1, keepdims=True) + EPS)
    return xv * rstd, rstd


def _sigmoid(x):
    return 1.0 / (1.0 + jnp.exp(-x))


def _mm(a, b, *, name, tm, tn, nt=False, ta=False, out_dtype=F32, tie=None):
    k, m = a.shape if ta else a.shape[::-1]
    n = b.shape[0] if nt else b.shape[1]
    assert m % tm == 0 and n % tn == 0 and not (nt and ta), (name, a.shape, b.shape, tm, tn)

    def body(a_ref, b_ref, *rest):
        av = a_ref[...].astype(BF16)
        bv = b_ref[...].astype(BF16)
        rest[-1][...] = (_dot_nt(av, bv) if nt else _dot_tn(av, bv) if ta else _dot(av, bv)).astype(out_dtype)

    in_specs = [
        pl.BlockSpec((k, tm), lambda i, j: (0, i)) if ta else pl.BlockSpec((tm, k), lambda i, j: (i, 0)),
        pl.BlockSpec((tn, k), lambda i, j: (j, 0)) if nt else pl.BlockSpec((k, tn), lambda i, j: (0, j)),
    ]
    operands = [a, b]
    if tie is not None:
        in_specs.append(pl.BlockSpec(memory_space=pl.ANY))
        operands.append(tie)
    return pl.pallas_call(
        body,
        name=name,
        grid=(m // tm, n // tn),
        in_specs=in_specs,
        out_specs=pl.BlockSpec((tm, tn), lambda i, j: (i, j)),
        out_shape=jax.ShapeDtypeStruct((m, n), out_dtype),
        compiler_params=_params("parallel", "parallel"),
    )(*operands)


def _ffn_fwd(x, gain, wgt, wut, wd, *, name, tm=1024, tf=256):
    t = x.shape[0]
    nj = D_FF // tf

    def body(x_ref, g_ref, wg_ref, wu_ref, wd_prev_ref, wd_last_ref, xo_ref, a_ref, b_ref, h_ref, st_ref, acc, s_prev):
        j = pl.program_id(1)

        @pl.when(j == 0)
        def _():
            xhat, _ = _rms(x_ref[...])
            h_ref[...] = (xhat * g_ref[...]).astype(BF16)
            acc[...] = jnp.zeros_like(acc)
            s_prev[...] = jnp.zeros_like(s_prev)

        acc[...] += _dot(s_prev[...], wd_prev_ref[...])
        h = h_ref[...]
        a = _dot_nt(h, wg_ref[...])
        b = _dot_nt(h, wu_ref[...])
        a_ref[...] = a.astype(BF16)
        b_ref[...] = b.astype(BF16)
        s = (a * _sigmoid(a) * b).astype(BF16)
        st_ref[...] = s
        s_prev[...] = s

        @pl.when(j == nj - 1)
        def _():
            xo_ref[...] = x_ref[...] + 0.5 * (acc[...] + _dot(s, wd_last_ref[...]))

    return pl.pallas_call(
        body,
        name=name,
        grid=(t // tm, nj),
        in_specs=[
            pl.BlockSpec((tm, D_MODEL), lambda i, j: (i, 0)),
            pl.BlockSpec((1, D_MODEL), lambda i, j: (0, 0)),
            pl.BlockSpec((tf, D_MODEL), lambda i, j: (j, 0)),
            pl.BlockSpec((tf, D_MODEL), lambda i, j: (j, 0)),
            pl.BlockSpec((tf, D_MODEL), lambda i, j: (jnp.maximum(j - 1, 0), 0)),
            pl.BlockSpec((tf, D_MODEL), lambda i, j: (nj - 1, 0)),
        ],
        out_specs=[
            pl.BlockSpec((tm, D_MODEL), lambda i, j: (i, 0)),
            pl.BlockSpec((tm, tf), lambda i, j: (i, j)),
            pl.BlockSpec((tm, tf), lambda i, j: (i, j)),
            pl.BlockSpec((tm, D_MODEL), lambda i, j: (i, 0)),
            pl.BlockSpec((tm, tf), lambda i, j: (i, j)),
        ],
        out_shape=[
            jax.ShapeDtypeStruct((t, D_MODEL), F32),
            jax.ShapeDtypeStruct((t, D_FF), BF16),
            jax.ShapeDtypeStruct((t, D_FF), BF16),
            jax.ShapeDtypeStruct((t, D_MODEL), BF16),
            jax.ShapeDtypeStruct((t, D_FF), BF16),
        ],
        scratch_shapes=[pltpu.VMEM((tm, D_MODEL), F32), pltpu.VMEM((tm, tf), BF16)],
        compiler_params=_params("parallel", "arbitrary"),
    )(x, gain, wgt, wut, wd, wd)


def _ffn_bwd(dout, x, gain, a, b, wgt, wut, wd, *, name, tm=1024, tf=256):
    t = x.shape[0]
    nj = D_FF // tf

    def body(do_ref, x_ref, g_ref, a_ref, b_ref, wg_prev_ref, wu_prev_ref, wg_last_ref, wu_last_ref, wd_ref,
             dx_ref, dg_ref, da_ref, db_ref, dob_ref, dob_scr, dh, da_prev, db_prev):
        i = pl.program_id(0)
        j = pl.program_id(1)

        @pl.when(j == 0)
        def _():
            d = (0.5 * do_ref[...]).astype(BF16)
            dob_scr[...] = d
            dob_ref[...] = d
            dh[...] = jnp.zeros_like(dh)
            da_prev[...] = jnp.zeros_like(da_prev)
            db_prev[...] = jnp.zeros_like(db_prev)

        dh[...] += _dot(da_prev[...], wg_prev_ref[...]) + _dot(db_prev[...], wu_prev_ref[...])
        ds = _dot_nt(dob_scr[...], wd_ref[...])
        av = a_ref[...].astype(F32)
        bv = b_ref[...].astype(F32)
        sig = _sigmoid(av)
        dbv = (ds * (av * sig)).astype(BF16)
        dav = (ds * bv * (sig * (1.0 + av * (1.0 - sig)))).astype(BF16)
        da_ref[...] = dav
        db_ref[...] = dbv
        da_prev[...] = dav
        db_prev[...] = dbv

        @pl.when(j == nj - 1)
        def _():
            xhat, rstd = _rms(x_ref[...])
            dhv = dh[...] + _dot(dav, wg_last_ref[...]) + _dot(dbv, wu_last_ref[...])
            part = jnp.sum(dhv * xhat, axis=0, keepdims=True)

            @pl.when(i == 0)
            def _():
                dg_ref[...] = part

            @pl.when(i > 0)
            def _():
                dg_ref[...] += part

            dxh = dhv * g_ref[...]
            dx_ref[...] = do_ref[...] + rstd * (dxh - xhat * jnp.mean(dxh * xhat, axis=-1, keepdims=True))

    return pl.pallas_call(
        body,
        name=name,
        grid=(t // tm, nj),
        in_specs=[
            pl.BlockSpec((tm, D_MODEL), lambda i, j: (i, 0)),
            pl.BlockSpec((tm, D_MODEL), lambda i, j: (i, 0)),
            pl.BlockSpec((1, D_MODEL), lambda i, j: (0, 0)),
            pl.BlockSpec((tm, tf), lambda i, j: (i, j)),
            pl.BlockSpec((tm, tf), lambda i, j: (i, j)),
            pl.BlockSpec((tf, D_MODEL), lambda i, j: (jnp.maximum(j - 1, 0), 0)),
            pl.BlockSpec((tf, D_MODEL), lambda i, j: (jnp.maximum(j - 1, 0), 0)),
            pl.BlockSpec((tf, D_MODEL), lambda i, j: (nj - 1, 0)),
            pl.BlockSpec((tf, D_MODEL), lambda i, j: (nj - 1, 0)),
            pl.BlockSpec((tf, D_MODEL), lambda i, j: (j, 0)),
        ],
        out_specs=[
            pl.BlockSpec((tm, D_MODEL), lambda i, j: (i, 0)),
            pl.BlockSpec((1, D_MODEL), lambda i, j: (0, 0)),
            pl.BlockSpec((tm, tf), lambda i, j: (i, j)),
            pl.BlockSpec((tm, tf), lambda i, j: (i, j)),
            pl.BlockSpec((tm, D_MODEL), lambda i, j: (i, 0)),
        ],
        out_shape=[
            jax.ShapeDtypeStruct((t, D_MODEL), F32),
            jax.ShapeDtypeStruct((1, D_MODEL), F32),
            jax.ShapeDtypeStruct((t, D_FF), BF16),
            jax.ShapeDtypeStruct((t, D_FF), BF16),
            jax.ShapeDtypeStruct((t, D_MODEL), BF16),
        ],
        scratch_shapes=[pltpu.VMEM((tm, D_MODEL), BF16), pltpu.VMEM((tm, D_MODEL), F32), pltpu.VMEM((tm, tf), BF16),
                        pltpu.VMEM((tm, tf), BF16)],
        compiler_params=_params("arbitrary", "arbitrary", vmem_limit_bytes=FFN_BWD_VMEM_LIMIT_BYTES),
    )(dout, x, gain, a, b, wgt, wut, wgt, wut, wd)


def _norm_fwd(x, gain, *, name, tm=512):
    t = x.shape[0]

    def body(x_ref, g_ref, h_ref):
        xhat, _ = _rms(x_ref[...])
        h_ref[...] = (xhat * g_ref[...]).astype(BF16)

    return pl.pallas_call(
        body,
        name=name,
        grid=(t // tm,),
        in_specs=[pl.BlockSpec((tm, D_MODEL), lambda i: (i, 0)), pl.BlockSpec((1, D_MODEL), lambda i: (0, 0))],
        out_specs=pl.BlockSpec((tm, D_MODEL), lambda i: (i, 0)),
        out_shape=jax.ShapeDtypeStruct((t, D_MODEL), BF16),
        compiler_params=_params("parallel"),
    )(x, gain)


def _norm_bwd(dh, x, gain, dres, *, name, tm=512):
    t = x.shape[0]

    def body(dh_ref, x_ref, g_ref, dr_ref, dx_ref, dg_ref):
        i = pl.program_id(0)
        xhat, rstd = _rms(x_ref[...])
        dhv = dh_ref[...]
        part = jnp.sum(dhv * xhat, axis=0, keepdims=True)

        @pl.when(i == 0)
        def _():
            dg_ref[...] = part

        @pl.when(i > 0)
        def _():
            dg_ref[...] += part

        dxh = dhv * g_ref[...]
        dx_ref[...] = dr_ref[...] + rstd * (dxh - xhat * jnp.mean(dxh * xhat, axis=-1, keepdims=True))

    row = pl.BlockSpec((tm, D_MODEL), lambda i: (i, 0))
    vec = pl.BlockSpec((1, D_MODEL), lambda i: (0, 0))
    return pl.pallas_call(
        body,
        name=name,
        grid=(t // tm,),
        in_specs=[row, row, vec, row],
        out_specs=[row, vec],
        out_shape=[jax.ShapeDtypeStruct((t, D_MODEL), F32), jax.ShapeDtypeStruct((1, D_MODEL), F32)],
        compiler_params=_params("arbitrary"),
    )(dh, x, gain, dres)


ATT_Q_TILE = 512
ATT_K_BLOCK = 256


def _first_head_lanes():
    return lax.broadcasted_iota(jnp.int32, (1, LANES), 1) < SB_HEAD_DIM


def _stack_heads(x):
    first = _first_head_lanes()
    return jnp.concatenate([jnp.where(first, x, 0.0), jnp.where(first, 0.0, x)], axis=0)


def _unstack_heads(x, rows):
    return jnp.where(_first_head_lanes(), x[:rows], x[rows:])


def _tri(n, relation):
    r = lax.broadcasted_iota(jnp.int32, (n, n), 0)
    c = lax.broadcasted_iota(jnp.int32, (n, n), 1)
    return relation(r, c).astype(BF16)


def _scan_dot(x, tri):
    hi = x.astype(BF16)
    lo = (x - hi.astype(F32)).astype(BF16)
    return _dot(jnp.concatenate([hi, lo], axis=1), jnp.concatenate([tri, tri], axis=0))


def _log_terms(z):
    lbeta = jnp.minimum(z, 0.0) - jnp.log(1.0 + jnp.exp(-jnp.abs(z)))
    return lbeta, lbeta - z


def _attn_fwd(proj, *, name):
    t = proj.shape[0]
    tq, tk = ATT_Q_TILE, ATT_K_BLOCK
    diag = tq // tk
    n_pairs = SB_WIDTH // LANES

    def body(q_ref, k_ref, v_ref, o_ref, kept_ref):
        qi = pl.program_id(1)
        q = q_ref[...] * (SB_HEAD_DIM ** -0.5)
        qs = _stack_heads(q).astype(BF16)
        tri = _tri(tk, lambda j, s: j > s)
        trow = lax.broadcasted_iota(jnp.int32, (tq, tk), 0)
        scol = lax.broadcasted_iota(jnp.int32, (tq, tk), 1)

        def block(kb, carry, causal):
            acc, c = carry
            off = pl.multiple_of(kb * tk, tk)
            lbeta, lrest = _log_terms(_dot_nt(qs, k_ref[pl.ds(off, tk), :].astype(BF16)))
            if causal is not None:
                lrest = jnp.where(causal, lrest, 0.0)
            w = jnp.exp(lbeta + (_scan_dot(lrest, tri) + c))
            if causal is not None:
                w = jnp.where(causal, w, 0.0)
            wb = w.astype(BF16)
            kept_ref[0, 0, kb] = wb
            acc = acc + _dot(wb, v_ref[pl.ds(off, tk), :].astype(BF16))
            return acc, c + jnp.sum(lrest, axis=1, keepdims=True)

        carry = (jnp.zeros((2 * tq, LANES), F32), jnp.zeros((2 * tq, 1), F32))
        n_full = qi * diag
        for j in reversed(range(diag)):
            mask = (scol + j * tk) < trow
            carry = block(n_full + j, carry, jnp.concatenate([mask, mask], axis=0))

        def step(it, carry):
            return block(n_full - 1 - it, carry, None)

        acc, _ = lax.fori_loop(0, n_full, step, carry)
        o_ref[...] = _unstack_heads(acc, tq)

    return pl.pallas_call(
        body,
        name=name,
        grid=(n_pairs, t // tq),
        in_specs=[
            pl.BlockSpec((tq, LANES), lambda p, i: (i, p)),
            pl.BlockSpec((t, LANES), lambda p, i: (0, n_pairs + p)),
            pl.BlockSpec((t, LANES), lambda p, i: (0, 2 * n_pairs + p)),
        ],
        out_specs=[pl.BlockSpec((tq, LANES), lambda p, i: (i, p)),
                   pl.BlockSpec((1, 1, t // tk, 2 * tq, tk), lambda p, i: (p, i, 0, 0, 0))],
        out_shape=[jax.ShapeDtypeStruct((t, SB_WIDTH), F32),
                   jax.ShapeDtypeStruct((n_pairs, t // tq, t // tk, 2 * tq, tk), BF16)],
        compiler_params=_params("parallel", "parallel"),
    )(proj, proj, proj)


def _attn_bwd(proj, kept, do, *, name, tie=None):
    t = proj.shape[0]
    tq, tk = ATT_Q_TILE, ATT_K_BLOCK
    diag = tq // tk
    n_pairs = SB_WIDTH // LANES
    scale = SB_HEAD_DIM ** -0.5

    def body(q_ref, k_ref, v_ref, kept_ref, do_ref, *rest):
        dq_ref, dk_ref, dv_ref = rest[-3:]
        qi = pl.program_id(1)

        @pl.when(qi == 0)
        def _():
            dk_ref[...] = jnp.zeros_like(dk_ref)
            dv_ref[...] = jnp.zeros_like(dv_ref)

        qs = _stack_heads(q_ref[...] * scale).astype(BF16)
        dos = _stack_heads(do_ref[...]).astype(BF16)
        before = _tri(tk, lambda s, j: s < j)
        trow = lax.broadcasted_iota(jnp.int32, (tq, tk), 0)
        scol = lax.broadcasted_iota(jnp.int32, (tq, tk), 1)

        def block(kb, carry, causal):
            dq, cg = carry
            off = pl.multiple_of(kb * tk, tk)
            wb = kept_ref[0, 0, kb]
            kblk = k_ref[pl.ds(off, tk), :].astype(BF16)
            sig = 0.5 + 0.5 * jnp.tanh(0.5 * _dot_nt(qs, kblk))
            g = wb.astype(F32) * _dot_nt(dos, v_ref[pl.ds(off, tk), :].astype(BF16))
            prior = _scan_dot(g, before) + cg
            dz = g - sig * (g + prior)
            if causal is not None:
                dz = jnp.where(causal, dz, 0.0)
            dzb = dz.astype(BF16)
            dq = dq + _dot(dzb, kblk)
            dk_ref[pl.ds(off, tk), :] += _dot_tn(dzb, qs)
            dv_ref[pl.ds(off, tk), :] += _dot_tn(wb, dos)
            return dq, cg + jnp.sum(g, axis=1, keepdims=True)

        n_full = qi * diag
        carry = lax.fori_loop(0, n_full, lambda kb, carry: block(kb, carry, None),
                              (jnp.zeros((2 * tq, LANES), F32), jnp.zeros((2 * tq, 1), F32)))
        for j in range(diag):
            mask = (scol + j * tk) < trow
            carry = block(n_full + j, carry, jnp.concatenate([mask, mask], axis=0))
        dq_ref[...] = _unstack_heads(carry[0], tq) * scale

    tile_spec = pl.BlockSpec((tq, LANES), lambda p, i: (i, p))
    full_spec = pl.BlockSpec((t, LANES), lambda p, i: (0, p))
    return pl.pallas_call(
        body,
        name=name,
        grid=(n_pairs, t // tq),
        in_specs=[
            tile_spec,
            pl.BlockSpec((t, LANES), lambda p, i: (0, n_pairs + p)),
            pl.BlockSpec((t, LANES), lambda p, i: (0, 2 * n_pairs + p)),
            pl.BlockSpec((1, 1, t // tk, 2 * tq, tk), lambda p, i: (p, i, 0, 0, 0)),
            tile_spec,
        ] + ([] if tie is None else [pl.BlockSpec(memory_space=pl.ANY)]),
        out_specs=[tile_spec, full_spec, full_spec],
        out_shape=[jax.ShapeDtypeStruct((t, SB_WIDTH), F32)] * 3,
        compiler_params=_params("arbitrary", "arbitrary"),
    )(proj, proj, proj, kept, do, *([] if tie is None else [tie]))


HG_BLOCK = 256
HG_HEADS = HG_WIDTH // HG_HEAD_DIM


def _chunk_mats(n):
    r = lax.broadcasted_iota(jnp.int32, (n, n), 0)
    c = lax.broadcasted_iota(jnp.int32, (n, n), 1)
    same = (r // HG_CHUNK) == (c // HG_CHUNK)
    upto = (same & (c <= r)).astype(BF16)
    whole = same.astype(BF16)
    onward = (same & (c >= r)).astype(BF16)
    return upto, whole, onward


def _rows_dot(mat, x):
    return _dot(jnp.concatenate([mat, mat, mat], axis=1), jnp.concatenate(_split3(x), axis=0))


def _lower_bound(lg_ref):
    lg = lg_ref[...]
    return _sigmoid(lg[0:1, :] - lg[1:2, :])


def _hgrn_prepare(q_ref, f_ref, lb, h, upto, whole):
    cols = slice(h * HG_HEAD_DIM, (h + 1) * HG_HEAD_DIM)
    lbh = lb[:, cols]
    sg = _sigmoid(f_ref[:, cols])
    forget = lbh + (1.0 - lbh) * sg
    logf = jnp.log(forget)
    kk = (1.0 - lbh) * (1.0 - sg)
    qv = q_ref[:, cols]
    qsig = _sigmoid(qv)
    qh = qv * qsig
    b = _rows_dot(upto, logf)
    blast = _rows_dot(whole, logf)
    return dict(lbh=lbh, sg=sg, forget=forget, kk=kk, qv=qv, qsig=qsig, qh=qh, b=b, eb=jnp.exp(b),
                ekb=jnp.exp(blast - b), dl=jnp.exp(blast))


def _hgrn_fwd(proj, logits, *, name):
    t = proj.shape[0]
    tb = HG_BLOCK
    nc = tb // HG_CHUNK
    hd = HG_HEAD_DIM

    def body(q_ref, f_ref, i_ref, lg_ref, o_ref, st_ref, state, qh_s, kk_s, b_s, qe_s, ke_s, dl_s):
        @pl.when(pl.program_id(0) == 0)
        def _():
            state[...] = jnp.zeros_like(state)

        lb = _lower_bound(lg_ref)
        upto, whole, _ = _chunk_mats(tb)
        for h in range(HG_HEADS):
            p = _hgrn_prepare(q_ref, f_ref, lb, h, upto, whole)
            qh_s[h] = p["qh"]
            kk_s[h] = p["kk"]
            b_s[h] = p["b"]
            qe_s[h] = (p["qh"] * p["eb"]).astype(BF16)
            ke_s[h] = (p["kk"] * p["ekb"]).astype(BF16)
            dl_s[h] = p["dl"]
        rowi = lax.broadcasted_iota(jnp.int32, (HG_CHUNK, hd), 0)

        def chunk(c, _):
            r0 = pl.multiple_of(c * HG_CHUNK, HG_CHUNK)
            rows = pl.ds(r0, HG_CHUNK)
            for h in range(HG_HEADS):
                cols = slice(h * hd, (h + 1) * hd)
                bc = b_s[h, rows, :]
                qc = qh_s[h, rows, :]
                kc = kk_s[h, rows, :]
                vc = i_ref[rows, cols]
                s_in = state[h]
                st_ref[c, h] = s_in
                o = _dot_nt(qe_s[h, rows, :], s_in.astype(BF16))
                for s in range(HG_CHUNK):
                    pair = jnp.where(rowi >= s, qc * jnp.exp(bc - bc[s:s + 1, :]) * kc[s:s + 1, :], 0.0)
                    o = o + jnp.sum(pair, axis=1, keepdims=True) * vc[s:s + 1, :]
                o_ref[rows, cols] = o
                state[h] = s_in * dl_s[h, pl.ds(r0, 1), :] + _dot_tn(vc.astype(BF16), ke_s[h, rows, :])
            return 0

        lax.fori_loop(0, nc, chunk, 0)

    blk = lambda col: pl.BlockSpec((tb, HG_WIDTH), lambda i: (i, col))
    head_f32 = pltpu.VMEM((HG_HEADS, tb, hd), F32)
    head_bf16 = pltpu.VMEM((HG_HEADS, tb, hd), BF16)
    return pl.pallas_call(
        body,
        name=name,
        grid=(t // tb,),
        in_specs=[blk(3), blk(4), blk(5), pl.BlockSpec((2, HG_WIDTH), lambda i: (0, 0))],
        out_specs=[
            pl.BlockSpec((tb, HG_WIDTH), lambda i: (i, 0)),
            pl.BlockSpec((nc, HG_HEADS, hd, hd), lambda i: (i, 0, 0, 0)),
        ],
        out_shape=[
            jax.ShapeDtypeStruct((t, HG_WIDTH), F32),
            jax.ShapeDtypeStruct((t // HG_CHUNK, HG_HEADS, hd, hd), F32),
        ],
        scratch_shapes=[pltpu.VMEM((HG_HEADS, hd, hd), F32), head_f32, head_f32, head_f32, head_bf16, head_bf16,
                        head_f32],
        compiler_params=_params("arbitrary"),
    )(proj, proj, proj, logits)


def _hgrn_bwd(proj, logits, states, do, *, name):
    t = proj.shape[0]
    tb = HG_BLOCK
    nb = t // tb
    nc = tb // HG_CHUNK
    hd = HG_HEAD_DIM

    def body(q_ref, f_ref, i_ref, lg_ref, st_ref, do_ref, dq_ref, df_ref, di_ref, dlb_ref,
             dstate, qh_s, kk_s, b_s, eb_s, ekb_s, qe_s, ke_s, dl_s, dqh_s, dkk_s, dlf_s):
        step = pl.program_id(0)

        @pl.when(step == 0)
        def _():
            dstate[...] = jnp.zeros_like(dstate)
            dlb_ref[...] = jnp.zeros_like(dlb_ref)

        lb = _lower_bound(lg_ref)
        upto, whole, _ = _chunk_mats(tb)
        prepared = []
        for h in range(HG_HEADS):
            p = _hgrn_prepare(q_ref, f_ref, lb, h, upto, whole)
            prepared.append(p)
            qh_s[h] = p["qh"]
            kk_s[h] = p["kk"]
            b_s[h] = p["b"]
            eb_s[h] = p["eb"]
            ekb_s[h] = p["ekb"]
            qe_s[h] = (p["qh"] * p["eb"]).astype(BF16)
            ke_s[h] = (p["kk"] * p["ekb"]).astype(BF16)
            dl_s[h] = p["dl"]
        rowi = lax.broadcasted_iota(jnp.int32, (HG_CHUNK, hd), 0)
        r16 = lax.broadcasted_iota(jnp.int32, (HG_CHUNK, HG_CHUNK), 0)
        c16 = lax.broadcasted_iota(jnp.int32, (HG_CHUNK, HG_CHUNK), 1)
        onward = (c16 >= r16).astype(BF16)

        def chunk(it, _):
            c = nc - 1 - it
            r0 = pl.multiple_of(c * HG_CHUNK, HG_CHUNK)
            rows = pl.ds(r0, HG_CHUNK)
            for h in range(HG_HEADS):
                cols = slice(h * hd, (h + 1) * hd)
                bc = b_s[h, rows, :]
                qc = qh_s[h, rows, :]
                kc = kk_s[h, rows, :]
                vc = i_ref[rows, cols]
                doc = do_ref[rows, cols]
                s_in = st_ref[c, h]
                ds_out = dstate[h]
                ds_out_b = ds_out.astype(BF16)
                docb = doc.astype(BF16)
                dl_row = dl_s[h, pl.ds(r0, 1), :]
                dqh = _dot(docb, s_in.astype(BF16)) * eb_s[h, rows, :]
                dkk = _dot(vc.astype(BF16), ds_out_b) * ekb_s[h, rows, :]
                dv = _dot_nt(ke_s[h, rows, :], ds_out_b)
                db = dqh * qc - dkk * kc
                dwhole = jnp.sum(dkk * kc, axis=0, keepdims=True) + jnp.sum(ds_out * s_in, axis=0, keepdims=True) * dl_row
                for s in range(HG_CHUNK):
                    keep = rowi >= s
                    at_s = rowi == s
                    e = jnp.exp(bc - bc[s:s + 1, :])
                    k_row = kc[s:s + 1, :]
                    pcol = jnp.sum(jnp.where(keep, qc * e * k_row, 0.0), axis=1, keepdims=True)
                    dpcol = jnp.sum(doc * vc[s:s + 1, :], axis=1, keepdims=True)
                    m = jnp.where(keep, e * dpcol, 0.0)
                    mk = m * k_row
                    dk_row = jnp.sum(m * qc, axis=0, keepdims=True)
                    dqh = dqh + mk
                    dkk = dkk + jnp.where(at_s, dk_row, 0.0)
                    db = db + mk * qc - jnp.where(at_s, dk_row * k_row, 0.0)
                    dv = dv + jnp.where(at_s, jnp.sum(pcol * doc, axis=0, keepdims=True), 0.0)
                dqh_s[h, rows, :] = dqh
                dkk_s[h, rows, :] = dkk
                dlf_s[h, rows, :] = _rows_dot(onward, db) + dwhole
                di_ref[rows, cols] = dv
                dstate[h] = ds_out * dl_row + _dot_tn(docb, qe_s[h, rows, :])
            return 0

        lax.fori_loop(0, nc, chunk, 0)
        for h in range(HG_HEADS):
            cols = slice(h * hd, (h + 1) * hd)
            p = prepared[h]
            dq_ref[:, cols] = dqh_s[h] * (p["qsig"] * (1.0 + p["qv"] * (1.0 - p["qsig"])))
            dforget = dlf_s[h] / p["forget"] - dkk_s[h]
            df_ref[:, cols] = dforget * (1.0 - p["lbh"]) * p["sg"] * (1.0 - p["sg"])
            dlb_ref[:, cols] += jnp.sum(dforget * (1.0 - p["sg"]), axis=0, keepdims=True)

    blk = lambda col: pl.BlockSpec((tb, HG_WIDTH), lambda i: (nb - 1 - i, col))
    vec = pl.BlockSpec((1, HG_WIDTH), lambda i: (0, 0))
    head_f32 = pltpu.VMEM((HG_HEADS, tb, hd), F32)
    head_bf16 = pltpu.VMEM((HG_HEADS, tb, hd), BF16)
    return pl.pallas_call(
        body,
        name=name,
        grid=(nb,),
        in_specs=[
            blk(3), blk(4), blk(5),
            pl.BlockSpec((2, HG_WIDTH), lambda i: (0, 0)),
            pl.BlockSpec((nc, HG_HEADS, hd, hd), lambda i: (nb - 1 - i, 0, 0, 0)),
            blk(0),
        ],
        out_specs=[blk(0), blk(0), blk(0), vec],
        out_shape=[jax.ShapeDtypeStruct((t, HG_WIDTH), F32)] * 3 + [jax.ShapeDtypeStruct((1, HG_WIDTH), F32)],
        scratch_shapes=[
            pltpu.VMEM((HG_HEADS, hd, hd), F32),
            head_f32, head_f32, head_f32, head_f32, head_f32, head_bf16, head_bf16, head_f32,
            head_f32, head_f32, head_f32,
        ],
        compiler_params=_params("arbitrary"),
    )(proj, proj, proj, logits, states, do)


def _group_mat(width, head_dim):
    r = lax.broadcasted_iota(jnp.int32, (width, width), 0)
    c = lax.broadcasted_iota(jnp.int32, (width, width), 1)
    return ((r // head_dim) == (c // head_dim)).astype(BF16)


def _head_mean(x, mat, head_dim):
    hi = x.astype(BF16)
    lo = (x - hi.astype(F32)).astype(BF16)
    return (_dot(hi, mat) + _dot(lo, mat)) * (1.0 / head_dim)


def _mix_out_fwd(o_sb, o_hg, proj, g_sb, g_hg, w_out, x1, *, name, tm=256):
    t = x1.shape[0]

    def body(osb_ref, ohg_ref, gate_ref, gsb_ref, ghg_ref, w_ref, x_ref, xo_ref, mt_ref):
        msb = _group_mat(SB_WIDTH, SB_HEAD_DIM)
        mhg = _group_mat(HG_WIDTH, HG_HEAD_DIM)
        osb = osb_ref[...]
        ohg = ohg_ref[...]
        nsb = osb * lax.rsqrt(_head_mean(osb * osb, msb, SB_HEAD_DIM) + EPS) * gsb_ref[...]
        gate = gate_ref[...]
        nhg = ohg * lax.rsqrt(_head_mean(ohg * ohg, mhg, HG_HEAD_DIM) + EPS) * ghg_ref[...] * (gate * _sigmoid(gate))
        mixed = jnp.concatenate([nsb, nhg], axis=1).astype(BF16)
        mt_ref[...] = mixed
        xo_ref[...] = x_ref[...] + _dot(mixed, w_ref[...])

    half = pl.BlockSpec((tm, SB_WIDTH), lambda i: (i, 0))
    vec = pl.BlockSpec((1, SB_WIDTH), lambda i: (0, 0))
    row = pl.BlockSpec((tm, D_MODEL), lambda i: (i, 0))
    return pl.pallas_call(
        body,
        name=name,
        grid=(t // tm,),
        in_specs=[half, half, pl.BlockSpec((tm, HG_WIDTH), lambda i: (i, 6)), vec, vec,
                  pl.BlockSpec((D_MODEL, D_MODEL), lambda i: (0, 0)), row],
        out_specs=[row, row],
        out_shape=[jax.ShapeDtypeStruct((t, D_MODEL), F32), jax.ShapeDtypeStruct((t, D_MODEL), BF16)],
        compiler_params=_params("parallel"),
    )(o_sb, o_hg, proj, g_sb, g_hg, w_out, x1)


def _mix_out_bwd(dx2, o_sb, o_hg, proj, g_sb, g_hg, w_out, *, name, tm=256):
    t = dx2.shape[0]

    def body(dx_ref, osb_ref, ohg_ref, gate_ref, gsb_ref, ghg_ref, w_ref, dosb_ref, dohg_ref, dgate_ref, dgsb_ref,
             dghg_ref, dxb_ref):
        i = pl.program_id(0)
        msb = _group_mat(SB_WIDTH, SB_HEAD_DIM)
        mhg = _group_mat(HG_WIDTH, HG_HEAD_DIM)
        dxb = dx_ref[...].astype(BF16)
        dxb_ref[...] = dxb
        dmixed = _dot_nt(dxb, w_ref[...])
        dnsb = dmixed[:, :SB_WIDTH]
        dy = dmixed[:, SB_WIDTH:]

        osb = osb_ref[...]
        rstd = lax.rsqrt(_head_mean(osb * osb, msb, SB_HEAD_DIM) + EPS)
        ohat = osb * rstd
        part_sb = jnp.sum(dnsb * ohat, axis=0, keepdims=True)
        dohat = dnsb * gsb_ref[...]
        dosb_ref[...] = rstd * (dohat - ohat * _head_mean(dohat * ohat, msb, SB_HEAD_DIM))

        ohg = ohg_ref[...]
        rstd = lax.rsqrt(_head_mean(ohg * ohg, mhg, HG_HEAD_DIM) + EPS)
        ohat = ohg * rstd
        gate = gate_ref[...]
        sig = _sigmoid(gate)
        dn = dy * (gate * sig)
        dgate_ref[...] = dy * (ohat * ghg_ref[...]) * (sig * (1.0 + gate * (1.0 - sig)))
        part_hg = jnp.sum(dn * ohat, axis=0, keepdims=True)
        dohat = dn * ghg_ref[...]
        dohg_ref[...] = rstd * (dohat - ohat * _head_mean(dohat * ohat, mhg, HG_HEAD_DIM))

        @pl.when(i == 0)
        def _():
            dgsb_ref[...] = part_sb
            dghg_ref[...] = part_hg

        @pl.when(i > 0)
        def _():
            dgsb_ref[...] += part_sb
            dghg_ref[...] += part_hg

    half = pl.BlockSpec((tm, SB_WIDTH), lambda i: (i, 0))
    vec = pl.BlockSpec((1, SB_WIDTH), lambda i: (0, 0))
    row = pl.BlockSpec((tm, D_MODEL), lambda i: (i, 0))
    return pl.pallas_call(
        body,
        name=name,
        grid=(t // tm,),
        in_specs=[row, half, half, pl.BlockSpec((tm, HG_WIDTH), lambda i: (i, 6)), vec, vec,
                  pl.BlockSpec((D_MODEL, D_MODEL), lambda i: (0, 0))],
        out_specs=[half, half, half, vec, vec, row],
        out_shape=[jax.ShapeDtypeStruct((t, SB_WIDTH), F32)] * 3 + [jax.ShapeDtypeStruct((1, SB_WIDTH), F32)] * 2
        + [jax.ShapeDtypeStruct((t, D_MODEL), BF16)],
        compiler_params=_params("arbitrary"),
    )(dx2, o_sb, o_hg, proj, g_sb, g_hg, w_out)


def _loss_head(x3, gain, target, *, name, tm=512):
    t = x3.shape[0]

    def body(x_ref, g_ref, y_ref, dx_ref, dg_ref, loss_ref):
        i = pl.program_id(0)
        xhat, rstd = _rms(x_ref[...])
        err = xhat * g_ref[...] - y_ref[...]
        part_loss = 0.5 * jnp.sum(jnp.mean(err * err, axis=-1, keepdims=True), axis=0, keepdims=True)
        dy = err * (1.0 / D_MODEL)
        part_g = jnp.sum(dy * xhat, axis=0, keepdims=True)

        @pl.when(i == 0)
        def _():
            dg_ref[...] = part_g
            loss_ref[...] = jnp.broadcast_to(part_loss, loss_ref.shape)

        @pl.when(i > 0)
        def _():
            dg_ref[...] += part_g
            loss_ref[...] += jnp.broadcast_to(part_loss, loss_ref.shape)

        dxh = dy * g_ref[...]
        dx_ref[...] = rstd * (dxh - xhat * jnp.mean(dxh * xhat, axis=-1, keepdims=True))

    row = pl.BlockSpec((tm, D_MODEL), lambda i: (i, 0))
    vec = pl.BlockSpec((1, D_MODEL), lambda i: (0, 0))
    return pl.pallas_call(
        body,
        name=name,
        grid=(t // tm,),
        in_specs=[row, vec, row],
        out_specs=[row, vec, vec],
        out_shape=[jax.ShapeDtypeStruct((t, D_MODEL), F32), jax.ShapeDtypeStruct((1, D_MODEL), F32),
                   jax.ShapeDtypeStruct((1, D_MODEL), F32)],
        compiler_params=_params("arbitrary"),
    )(x3, gain, target)


def _local_step(x, target, norms, logits, w, weights_after=None, grads_ready=None):
    w = dict(w)
    x1, a1, b1, h1, s1 = _ffn_fwd(x, norms["ffn1"], w["g1t"], w["u1t"], w["d1"], name="ffn1_fwd")
    if weights_after is not None:
        w.update(weights_after("ffn1", x1))
    hm = _norm_fwd(x1, norms["mix"], name="mix_norm_fwd")
    proj = _mm(hm, w["in"], name="in_proj", tm=512, tn=IN_COLS)
    o_sb, sb_kept = _attn_fwd(proj, name="sb_attn_fwd")
    o_hg, states = _hgrn_fwd(proj, logits, name="hgrn2_fwd")
    x2, mixed = _mix_out_fwd(o_sb, o_hg, proj, norms["sb"], norms["hg"], w["out"], x1, name="mix_out_fwd")
    if weights_after is not None:
        w.update(weights_after("mix", x2))
    x3, a2, b2, h2, s2 = _ffn_fwd(x2, norms["ffn2"], w["g2t"], w["u2t"], w["d2"], name="ffn2_fwd")
    dx3, d_final, loss_row = _loss_head(x3, norms["final"], target, name="loss_head")

    def weight_grad(lhs, rhs, name, tie=None):
        return _mm(lhs, rhs, name=name, tm=256, tn=D_MODEL, ta=True, out_dtype=BF16, tie=tie)

    def sent(stage):
        return grads_ready(stage, gw) if grads_ready is not None else None

    gw, gv = {}, {"final": d_final}
    dx2, gv["ffn2"], da2, db2, dob2 = _ffn_bwd(dx3, x2, norms["ffn2"], a2, b2, w["g2t"], w["u2t"], w["d2"],
                                               name="ffn2_bwd")
    gw["g2t"] = weight_grad(da2, h2, "ffn2_dgate")
    gw["u2t"] = weight_grad(db2, h2, "ffn2_dup")
    gw["d2"] = weight_grad(s2, dob2, "ffn2_ddown")

    do_sb, do_hg, d_gate, gv["sb"], gv["hg"], dx2b = _mix_out_bwd(
        dx2, o_sb, o_hg, proj, norms["sb"], norms["hg"], w["out"], name="mix_out_bwd")
    gw["out"] = weight_grad(mixed, dx2b, "out_dw")
    tie = sent("mix")
    dq_sb, dk_sb, dv_sb = _attn_bwd(proj, sb_kept, do_sb, name="sb_attn_bwd", tie=tie)
    dq_hg, df_hg, di_hg, d_lb = _hgrn_bwd(proj, logits if tie is None else logits + tie[0, 0], states, do_hg,
                                          name="hgrn2_bwd")
    dproj = jnp.concatenate([dq_sb, dk_sb, dv_sb, dq_hg, df_hg, di_hg, d_gate], axis=1).astype(BF16)
    gw["in"] = _mm(hm, dproj, name="in_dw", tm=D_MODEL, tn=256, ta=True)
    tie = sent("in")
    dhm = _mm(dproj, w["in"], name="in_dx", tm=512, tn=D_MODEL, nt=True)
    dx1, gv["mix"] = _norm_bwd(dhm, x1, norms["mix"] if tie is None else norms["mix"] + tie[0, 0], dx2,
                               name="mix_norm_bwd")

    dx, gv["ffn1"], da1, db1, dob1 = _ffn_bwd(dx1, x, norms["ffn1"], a1, b1, w["g1t"], w["u1t"], w["d1"],
                                              name="ffn1_bwd")
    gw["g1t"] = weight_grad(da1, h1, "ffn1_dgate")
    gw["u1t"] = weight_grad(db1, h1, "ffn1_dup", tie=sent("g1t"))
    gw["d1"] = weight_grad(s1, dob1, "ffn1_ddown", tie=sent("u1t"))
    sent("d1")
    gv["lb"] = d_lb
    return loss_row, dx, gw, gv


HBM = pl.BlockSpec(memory_space=pl.ANY)


def _place():
    return lax.axis_index("x"), lax.axis_index("y"), lax.axis_index("c")


def _slot(px, py, pc):
    return 4 * px + 2 * py + pc


def _all_gather(blocks, *, name):
    n = len(blocks)

    def body(*refs):
        ins, outs = refs[:n], refs[n:2 * n]
        send_sems, recv_sems, local_sems = refs[2 * n:]
        x, y, c = _place()
        me, sibling = (x, y, c), (x, y, 1 - c)
        chips = [(1 - x, y), (x, 1 - y), (1 - x, 1 - y)]

        def copy(a, k, block, to, src=None):
            dst = outs[a].at[_slot(*block)]
            return pltpu.make_async_remote_copy(
                src_ref=dst if src is None else src, dst_ref=dst, send_sem=send_sems.at[7 * a + k],
                recv_sem=recv_sems.at[7 * a + k], device_id=to, device_id_type=MESH)

        mine = [pltpu.make_async_copy(ins[a], outs[a].at[_slot(*me)], local_sems.at[a]) for a in range(n)]
        for cp in mine:
            cp.start()
        first = []
        for a in range(n):
            first.append(copy(a, 0, me, sibling, src=ins[a]))
            first += [copy(a, 1 + j, me, (*chip, c), src=ins[a]) for j, chip in enumerate(chips)]
        for cp in first:
            cp.start()
        passed = []
        for j, chip in enumerate(chips):
            for a in range(n):
                copy(a, 1 + j, (*chip, c), me).wait_recv()
                fwd = copy(a, 4 + j, (*chip, c), sibling)
                fwd.start()
                passed.append(fwd)
        for a in range(n):
            copy(a, 0, sibling, me).wait_recv()
            for j, chip in enumerate(chips):
                copy(a, 4 + j, (*chip, 1 - c), me).wait_recv()
        for cp in first + passed:
            cp.wait_send()
        for cp in mine:
            cp.wait()

    return pl.pallas_call(
        body,
        name=name,
        in_specs=[HBM] * n,
        out_specs=[HBM] * n,
        out_shape=[jax.ShapeDtypeStruct((N_DEV,) + b.shape, b.dtype) for b in blocks],
        scratch_shapes=[pltpu.SemaphoreType.DMA((7 * n,)), pltpu.SemaphoreType.DMA((7 * n,)),
                        pltpu.SemaphoreType.DMA((n,))],
    )(*blocks)


def _flipped(place, d):
    return tuple(1 - p if (d >> (2 - axis)) & 1 else p for axis, p in enumerate(place))


def _scatter_by_owner(stacks, *, name):
    n = len(stacks)

    def body(*refs):
        ins, outs = refs[:n], refs[n:2 * n]
        send_sems, recv_sems, local_sems = refs[2 * n:]
        me = _place()
        mine = [pltpu.make_async_copy(ins[a].at[_slot(*me)], outs[a].at[_slot(*me)], local_sems.at[a]) for a in range(n)]
        for cp in mine:
            cp.start()
        copies = []
        for d in range(1, N_DEV):
            peer = _flipped(me, d)
            for a in range(n):
                copies.append(pltpu.make_async_remote_copy(
                    src_ref=ins[a].at[_slot(*peer)], dst_ref=outs[a].at[_slot(*me)], send_sem=send_sems.at[7 * a + d - 1],
                    recv_sem=recv_sems.at[7 * a + d - 1], device_id=peer, device_id_type=MESH))
        for cp in copies:
            cp.start()
        for cp in copies:
            cp.wait()
        for cp in mine:
            cp.wait()

    return pl.pallas_call(
        body,
        name=name,
        in_specs=[HBM] * n,
        out_specs=[HBM] * n,
        out_shape=[jax.ShapeDtypeStruct(s.shape, s.dtype) for s in stacks],
        scratch_shapes=[pltpu.SemaphoreType.DMA((7 * n,)), pltpu.SemaphoreType.DMA((7 * n,)),
                        pltpu.SemaphoreType.DMA((n,))],
    )(*stacks)


SEM = pl.BlockSpec(memory_space=pltpu.SEMAPHORE)
EFFECT = pltpu.SideEffectType.DATAFLOW_SIDE_EFFECTING


def _split_copies(me, srcs, lands, send_sems, recv_sems, by_owner):
    copies = []
    for d in range(1, N_DEV):
        peer = _flipped(me, d)
        for a, (src, land) in enumerate(zip(srcs, lands)):
            copies.append(pltpu.make_async_remote_copy(
                src_ref=src.at[_slot(*peer)] if by_owner else src, dst_ref=land.at[_slot(*me)],
                send_sem=send_sems.at[7 * a + d - 1], recv_sem=recv_sems.at[7 * a + d - 1], device_id=peer,
                device_id_type=MESH))
    return copies


def _copies_start(srcs, *, name, by_owner, after=None):
    n = len(srcs)
    extra = [] if after is None else [after]
    land_shapes = [s.shape if by_owner else (N_DEV,) + s.shape for s in srcs]
    lands = [pltpu.with_memory_space_constraint(lax.empty(shape, s.dtype), pltpu.HBM) for shape, s in zip(land_shapes, srcs)]
    srcs = [pltpu.with_memory_space_constraint(s, pltpu.HBM) for s in srcs]

    def body(*refs):
        src_refs, land_refs = refs[:n], refs[n:2 * n]
        send_sems, recv_sems = refs[2 * n + len(extra)], refs[2 * n + len(extra) + 1]
        token = refs[-1]
        for cp in _split_copies(_place(), src_refs, land_refs, send_sems, recv_sems, by_owner):
            cp.start()
        token[...] = jnp.zeros_like(token)

    out = pl.pallas_call(
        body,
        name=name,
        in_specs=[HBM] * (2 * n + len(extra)),
        out_specs=[SEM, SEM] + [HBM] * (2 * n) + [pl.BlockSpec(memory_space=pltpu.VMEM)],
        out_shape=[pltpu.SemaphoreType.DMA((7 * n,)), pltpu.SemaphoreType.DMA((7 * n,))]
        + [pltpu.HBM(s.shape, s.dtype) for s in srcs] + [pltpu.HBM(shape, s.dtype) for shape, s in zip(land_shapes, srcs)]
        + [jax.ShapeDtypeStruct((8, LANES), F32)],
        input_output_aliases={i: 2 + i for i in range(2 * n)},
        compiler_params=pltpu.CompilerParams(has_side_effects=EFFECT),
    )(*srcs, *lands, *extra)
    return (out[0], out[1], out[2:2 + n], out[2 + n:2 + 2 * n]), out[-1]


def _copies_wait(started, after, *, name, by_owner):
    send_sems, recv_sems, srcs, lands = started
    n = len(srcs)

    def body(*refs):
        src_refs, land_refs = refs[:n], refs[n:2 * n]
        for cp in _split_copies(_place(), src_refs, land_refs, refs[2 * n], refs[2 * n + 1], by_owner):
            cp.wait_send()
            cp.wait_recv()

    out = pl.pallas_call(
        body,
        name=name,
        in_specs=[HBM] * (2 * n) + [SEM, SEM, HBM],
        out_specs=[HBM] * (2 * n),
        out_shape=[pltpu.HBM(s.shape, s.dtype) for s in srcs] + [pltpu.HBM(s.shape, s.dtype) for s in lands],
        input_output_aliases={i: i for i in range(2 * n)},
        compiler_params=pltpu.CompilerParams(has_side_effects=EFFECT),
    )(*srcs, *lands, send_sems, recv_sems, after)
    return out[n:]


def _with_own(lands, own, slot):
    zero = jnp.zeros((), jnp.int32)
    return [lax.dynamic_update_slice(land, o[None], (slot.astype(jnp.int32),) + (zero,) * o.ndim)
            for land, o in zip(lands, own)]


def _adamw(w, g, m, v):
    m = ADAM_B1 * m + (1.0 - ADAM_B1) * g
    v = ADAM_B2 * v + (1.0 - ADAM_B2) * (g * g)
    m_hat = m / (1.0 - ADAM_B1 ** ADAM_STEP)
    v_hat = v / (1.0 - ADAM_B2 ** ADAM_STEP)
    delta = -ADAM_LR * (m_hat / (jnp.sqrt(v_hat) + ADAM_EPS) + ADAM_WD * w)
    return delta, m, v


def _sum_and_update(parts, w, m, v, *, name, tr, transposed=False, tie=None):
    _, rows, cols = w.shape
    pad = -cols % LANES

    def body(p_ref, w_ref, m_ref, v_ref, *rest):
        g_ref, d_ref, mo_ref, vo_ref = rest[-4:]
        g = p_ref[0].astype(F32)
        for s in range(1, N_DEV):
            g = g + p_ref[s].astype(F32)
        if transposed:
            if pad:
                g = jnp.concatenate([g, jnp.zeros((pad, tr), F32)], axis=0)
            g = g.T[:, :cols]
        g_ref[0] = g
        d_ref[0], mo_ref[0], vo_ref[0] = _adamw(w_ref[0], g, m_ref[0], v_ref[0])

    flat = pl.BlockSpec((1, tr, cols), lambda i: (0, i, 0))
    if transposed:
        part_spec = pl.BlockSpec((N_DEV, cols, tr), lambda i: (0, 0, i))
    else:
        part_spec = pl.BlockSpec((N_DEV, tr, cols), lambda i: (0, i, 0))
    return pl.pallas_call(
        body,
        name=name,
        grid=(rows // tr,),
        in_specs=[part_spec, flat, flat, flat] + ([] if tie is None else [pl.BlockSpec(memory_space=pl.ANY)]),
        out_specs=[flat] * 4,
        out_shape=[jax.ShapeDtypeStruct((1, rows, cols), F32)] * 4,
        compiler_params=_params("parallel"),
    )(parts, w, m, v, *([] if tie is None else [tie]))


VEC_ROWS = 8
ROW_LOGITS, ROW_LOSS = 5, 7


def _vectors_update(part, w, m, v, *, name):
    def body(p_ref, w_ref, m_ref, v_ref, g_ref, d_ref, mo_ref, vo_ref, loss_ref, all_ref, send_sems, recv_sems):
        me = _place()
        all_ref[_slot(*me)] = p_ref[...]
        copies = []
        for d in range(1, N_DEV):
            peer = _flipped(me, d)
            copies.append(pltpu.make_async_remote_copy(
                src_ref=p_ref, dst_ref=all_ref.at[_slot(*me)], send_sem=send_sems.at[d - 1], recv_sem=recv_sems.at[d - 1],
                device_id=peer, device_id_type=MESH))
        for cp in copies:
            cp.start()
        for cp in copies:
            cp.wait()
        total = all_ref[0]
        for s in range(1, N_DEV):
            total = total + all_ref[s]
        wv = w_ref[...]
        half = D_MODEL // 2
        lb = _sigmoid(wv[ROW_LOGITS:ROW_LOGITS + 1, :half] - wv[ROW_LOGITS:ROW_LOGITS + 1, half:])
        d_first = total[ROW_LOGITS:ROW_LOGITS + 1, :half] * lb * (1.0 - lb)
        d_logits = jnp.concatenate([d_first, -d_first], axis=1)
        rowi = lax.broadcasted_iota(jnp.int32, (VEC_ROWS, D_MODEL), 0)
        g = jnp.where(rowi == ROW_LOGITS, d_logits, jnp.where(rowi < ROW_LOGITS, total, 0.0))
        g_ref[...] = g
        d_ref[...], mo_ref[...], vo_ref[...] = _adamw(wv, g, m_ref[...], v_ref[...])
        loss_ref[...] = total[ROW_LOSS:ROW_LOSS + 1, :]

    vmem = pl.BlockSpec(memory_space=pltpu.VMEM)
    return pl.pallas_call(
        body,
        name=name,
        in_specs=[vmem] * 4,
        out_specs=[vmem] * 5,
        out_shape=[jax.ShapeDtypeStruct((VEC_ROWS, D_MODEL), F32)] * 4 + [jax.ShapeDtypeStruct((1, D_MODEL), F32)],
        scratch_shapes=[pltpu.VMEM((N_DEV, VEC_ROWS, D_MODEL), F32), pltpu.SemaphoreType.DMA((7,)),
                        pltpu.SemaphoreType.DMA((7,))],
    )(part, w, m, v)


ROW_SHARDED = ("d1", "d2", "out")
TRANSPOSED = ("g1t", "u1t", "g2t", "u2t")


def _vector_rows(rows):
    rowi = lax.broadcasted_iota(jnp.int32, (VEC_ROWS, D_MODEL), 0)
    out = jnp.zeros((VEC_ROWS, D_MODEL), F32)
    for i, r in enumerate(rows):
        if r is not None:
            out = jnp.where(rowi == i, r, out)
    return out


def kernel(x, ffn1_norm, ffn1_w_gate, ffn1_w_up, ffn1_w_down, mix_norm, w_in, sb_out_norm, hg_lower_bound_logits, hg_out_norm, w_out, ffn2_norm, ffn2_w_gate, ffn2_w_up, ffn2_w_down, final_norm, loss_target, m_ffn1_norm, m_ffn1_w_gate, m_ffn1_w_up, m_ffn1_w_down, m_mix_norm, m_w_in, m_sb_out_norm, m_hg_lower_bound_logits, m_hg_out_norm, m_w_out, m_ffn2_norm, m_ffn2_w_gate, m_ffn2_w_up, m_ffn2_w_down, m_final_norm, v_ffn1_norm, v_ffn1_w_gate, v_ffn1_w_up, v_ffn1_w_down, v_mix_norm, v_w_in, v_sb_out_norm, v_hg_lower_bound_logits, v_hg_out_norm, v_w_out, v_ffn2_norm, v_ffn2_w_gate, v_ffn2_w_up, v_ffn2_w_down, v_final_norm):
    def matrices(g1, u1, d1, win, wout, g2, u2, d2):
        return {"g1t": g1, "u1t": u1, "d1": d1, "in": win, "out": wout, "g2t": g2, "u2t": u2, "d2": d2}

    def vectors(n1, nm, nsb, lg, nhg, n2, nf):
        return [n1, nm, n2, nf.reshape(1, D_MODEL), jnp.concatenate([nsb, nhg], axis=1), lg.reshape(1, D_MODEL), None, None]

    w_sh = matrices(ffn1_w_gate, ffn1_w_up, ffn1_w_down, w_in, w_out, ffn2_w_gate, ffn2_w_up, ffn2_w_down)
    m_sh = matrices(m_ffn1_w_gate, m_ffn1_w_up, m_ffn1_w_down, m_w_in, m_w_out, m_ffn2_w_gate, m_ffn2_w_up, m_ffn2_w_down)
    v_sh = matrices(v_ffn1_w_gate, v_ffn1_w_up, v_ffn1_w_down, v_w_in, v_w_out, v_ffn2_w_gate, v_ffn2_w_up, v_ffn2_w_down)
    keys = list(w_sh)

    slot = _slot(*_place())

    def full(key, stack):
        if key == "in":
            return stack.transpose(1, 0, 2).reshape(D_MODEL, IN_COLS)
        return stack.reshape(-1, D_MODEL)

    def by_owner(key, grad):
        if key == "in":
            return grad.reshape(D_MODEL, N_DEV, IN_SHARD).transpose(1, 0, 2).astype(BF16)
        return grad.reshape(N_DEV, -1, D_MODEL)

    blocks = {k: (w_sh[k][0].T if k in TRANSPOSED else w_sh[k][0]).astype(BF16) for k in keys}
    first, mid, last = ("g1t", "u1t", "d1"), ("in", "out"), ("g2t", "u2t", "d2")
    w_first = {k: full(k, s) for k, s in zip(first, _all_gather([blocks[k] for k in first], name="gather_ffn1"))}
    flights = {}
    flights["ffn1"], token_mid = _copies_start([blocks[k] for k in mid], name="gather_mid_start", by_owner=False,
                                               after=w_first["d1"])
    flights["mix"], token_last = _copies_start([blocks[k] for k in last], name="gather_ffn2_start", by_owner=False,
                                               after=token_mid)

    def weights_after(stage, result):
        group = mid if stage == "ffn1" else last
        lands = _copies_wait(flights[stage], result, name="gather_" + stage + "_wait", by_owner=False)
        return {k: full(k, s) for k, s in zip(group, _with_own(lands, [blocks[k] for k in group], slot))}

    groups = {"mix": ("g2t", "u2t", "d2", "out"), "in": ("in",), "g1t": ("g1t",), "u1t": ("u1t",), "d1": ("d1",)}
    sent, sent_tokens = {}, []

    def grads_ready(stage, gw):
        stacks = [by_owner(k, gw[k]) for k in groups[stage]]
        flight, token = _copies_start(stacks, name="grads_" + stage + "_start", by_owner=True)
        sent[stage] = (stacks, flight)
        sent_tokens.append(token)
        return token

    norms = {"ffn1": ffn1_norm + token_last[0, 0], "mix": mix_norm, "sb": sb_out_norm, "hg": hg_out_norm,
             "ffn2": ffn2_norm, "final": final_norm.reshape(1, D_MODEL)}
    loss_row, grad_x, gw, gv = _local_step(x[0], loss_target[0], norms, hg_lower_bound_logits, w_first, weights_after,
                                           grads_ready)

    tiles = {"g1t": 256, "u1t": 256, "g2t": 256, "u2t": 256, "d1": 176, "d2": 176, "out": 128, "in": 256}
    updated, after = {}, sent_tokens[-1]
    for stage, (stacks, flight) in sent.items():
        lands = _copies_wait(flight, after, name="grads_" + stage + "_wait", by_owner=True)
        own = [lax.dynamic_index_in_dim(s, slot, keepdims=False) for s in stacks]
        for k, part in zip(groups[stage], _with_own(lands, own, slot)):
            updated[k] = _sum_and_update(part, w_sh[k], m_sh[k], v_sh[k], name="adamw_" + k, tr=tiles[k],
                                         transposed=k in TRANSPOSED, tie=after)
            after = updated[k][0]
    mats = [{k: updated[k][i] for k in keys} for i in range(4)]

    lb_row = jnp.concatenate([gv["lb"], jnp.zeros_like(gv["lb"])], axis=1)
    part = _vector_rows([gv["ffn1"], gv["mix"], gv["ffn2"], gv["final"], jnp.concatenate([gv["sb"], gv["hg"]], axis=1),
                         lb_row, None, loss_row])
    vec_w = _vector_rows(vectors(ffn1_norm, mix_norm, sb_out_norm, hg_lower_bound_logits, hg_out_norm, ffn2_norm, final_norm))
    vec_m = _vector_rows(vectors(m_ffn1_norm, m_mix_norm, m_sb_out_norm, m_hg_lower_bound_logits, m_hg_out_norm,
                                 m_ffn2_norm, m_final_norm))
    vec_v = _vector_rows(vectors(v_ffn1_norm, v_mix_norm, v_sb_out_norm, v_hg_lower_bound_logits, v_hg_out_norm,
                                 v_ffn2_norm, v_final_norm))
    *vecs, loss_out = _vectors_update(part, vec_w, vec_m, vec_v, name="vectors_update")

    def leaves(mat, vec):
        half = D_MODEL // 2
        return (
            vec[0:1], mat["g1t"], mat["u1t"], mat["d1"], vec[1:2], mat["in"], vec[4:5, :half],
            vec[ROW_LOGITS].reshape(2, half), vec[4:5, half:], mat["out"], vec[2:3], mat["g2t"], mat["u2t"],
            mat["d2"], vec[3],
        )

    out = [loss_out[0, 0], grad_x[None]]
    for mat, vec in zip(mats, vecs):
        out.extend(leaves(mat, vec))
    return tuple(out)
```

```python
import jax
import jax.numpy as jnp
from jax import lax
from jax.experimental import pallas as pl
from jax.experimental.pallas import tpu as pltpu

F32, BF16 = jnp.float32, jnp.bfloat16
D_MODEL = 1024
D_FF = 2816
SB_WIDTH = 512
HG_WIDTH = 512
SB_HEAD_DIM = 64
HG_HEAD_DIM = 128
IN_COLS = 3584
EPS = 1e-6
N_DEV = 8
LANES = 128
HG_CHUNK = 16
VMEM_LIMIT_BYTES = 48 * 1024 * 1024
FFN_BWD_VMEM_LIMIT_BYTES = 56 * 1024 * 1024
ADAM_LR, ADAM_B1, ADAM_B2, ADAM_EPS, ADAM_WD, ADAM_STEP = 0.001, 0.9, 0.999, 1e-08, 0.01, 10
MESH = pl.DeviceIdType.MESH


def _params(*semantics, vmem_limit_bytes=VMEM_LIMIT_BYTES):
    return pltpu.CompilerParams(dimension_semantics=semantics, vmem_limit_bytes=vmem_limit_bytes)


def _dot(a, b):
    return jnp.dot(a, b, preferred_element_type=F32)


def _dot_nt(a, b):
    return lax.dot_general(a, b, (((1,), (1,)), ((), ())), preferred_element_type=F32)


def _dot_tn(a, b):
    return lax.dot_general(a, b, (((0,), (0,)), ((), ())), preferred_element_type=F32)


def _split3(x):
    hi = x.astype(BF16)
    r1 = x - hi.astype(F32)
    mid = r1.astype(BF16)
    lo = (r1 - mid.astype(F32)).astype(BF16)
    return hi, mid, lo


def _rms(xv):
    rstd = lax.rsqrt(jnp.mean(xv * xv, axis=-1, keepdims=True) + EPS)
    return xv * rstd, rstd


def _sigmoid(x):
    return 1.0 / (1.0 + jnp.exp(-x))


def _mm(a, b, *, name, tm, tn, nt=False, ta=False, out_dtype=F32, tie=None):
    k, m = a.shape if ta else a.shape[::-1]
    n = b.shape[0] if nt else b.shape[1]
    assert m % tm == 0 and n % tn == 0 and not (nt and ta), (name, a.shape, b.shape, tm, tn)

    def body(a_ref, b_ref, *rest):
        av = a_ref[...].astype(BF16)
        bv = b_ref[...].astype(BF16)
        rest[-1][...] = (_dot_nt(av, bv) if nt else _dot_tn(av, bv) if ta else _dot(av, bv)).astype(out_dtype)

    in_specs = [
        pl.BlockSpec((k, tm), lambda i, j: (0, i)) if ta else pl.BlockSpec((tm, k), lambda i, j: (i, 0)),
        pl.BlockSpec((tn, k), lambda i, j: (j, 0)) if nt else pl.BlockSpec((k, tn), lambda i, j: (0, j)),
    ]
    operands = [a, b]
    if tie is not None:
        in_specs.append(pl.BlockSpec(memory_space=pl.ANY))
        operands.append(tie)
    return pl.pallas_call(
        body,
        name=name,
        grid=(m // tm, n // tn),
        in_specs=in_specs,
        out_specs=pl.BlockSpec((tm, tn), lambda i, j: (i, j)),
        out_shape=jax.ShapeDtypeStruct((m, n), out_dtype),
        compiler_params=_params("parallel", "parallel"),
    )(*operands)


def _ffn_fwd(x, gain, wgt, wut, wd, *, name, tm=1024, tf=256):
    t = x.shape[0]
    nj = D_FF // tf

    def body(x_ref, g_ref, wg_ref, wu_ref, wd_prev_ref, wd_last_ref, xo_ref, a_ref, b_ref, h_ref, st_ref, acc, s_prev):
        j = pl.program_id(1)

        @pl.when(j == 0)
        def _():
            xhat, _ = _rms(x_ref[...])
            h_ref[...] = (xhat * g_ref[...]).astype(BF16)
            acc[...] = jnp.zeros_like(acc)
            s_prev[...] = jnp.zeros_like(s_prev)

        acc[...] += _dot(s_prev[...], wd_prev_ref[...])
        h = h_ref[...]
        a = _dot_nt(h, wg_ref[...])
        b = _dot_nt(h, wu_ref[...])
        a_ref[...] = a.astype(BF16)
        b_ref[...] = b.astype(BF16)
        s = (a * _sigmoid(a) * b).astype(BF16)
        st_ref[...] = s
        s_prev[...] = s

        @pl.when(j == nj - 1)
        def _():
            xo_ref[...] = x_ref[...] + 0.5 * (acc[...] + _dot(s, wd_last_ref[...]))

    return pl.pallas_call(
        body,
        name=name,
        grid=(t // tm, nj),
        in_specs=[
            pl.BlockSpec((tm, D_MODEL), lambda i, j: (i, 0)),
            pl.BlockSpec((1, D_MODEL), lambda i, j: (0, 0)),
            pl.BlockSpec((tf, D_MODEL), lambda i, j: (j, 0)),
            pl.BlockSpec((tf, D_MODEL), lambda i, j: (j, 0)),
            pl.BlockSpec((tf, D_MODEL), lambda i, j: (jnp.maximum(j - 1, 0), 0)),
            pl.BlockSpec((tf, D_MODEL), lambda i, j: (nj - 1, 0)),
        ],
        out_specs=[
            pl.BlockSpec((tm, D_MODEL), lambda i, j: (i, 0)),
            pl.BlockSpec((tm, tf), lambda i, j: (i, j)),
            pl.BlockSpec((tm, tf), lambda i, j: (i, j)),
            pl.BlockSpec((tm, D_MODEL), lambda i, j: (i, 0)),
            pl.BlockSpec((tm, tf), lambda i, j: (i, j)),
        ],
        out_shape=[
            jax.ShapeDtypeStruct((t, D_MODEL), F32),
            jax.ShapeDtypeStruct((t, D_FF), BF16),
            jax.ShapeDtypeStruct((t, D_FF), BF16),
            jax.ShapeDtypeStruct((t, D_MODEL), BF16),
            jax.ShapeDtypeStruct((t, D_FF), BF16),
        ],
        scratch_shapes=[pltpu.VMEM((tm, D_MODEL), F32), pltpu.VMEM((tm, tf), BF16)],
        compiler_params=_params("parallel", "arbitrary"),
    )(x, gain, wgt, wut, wd, wd)


def _ffn_bwd(dout, x, gain, a, b, wgt, wut, wd, *, name, tm=1024, tf=256):
    t = x.shape[0]
    nj = D_FF // tf

    def body(do_ref, x_ref, g_ref, a_ref, b_ref, wg_prev_ref, wu_prev_ref, wg_last_ref, wu_last_ref, wd_ref,
             dx_ref, dg_ref, da_ref, db_ref, dob_ref, dob_scr, dh, da_prev, db_prev):
        i = pl.program_id(0)
        j = pl.program_id(1)

        @pl.when(j == 0)
        def _():
            d = (0.5 * do_ref[...]).astype(BF16)
            dob_scr[...] = d
            dob_ref[...] = d
            dh[...] = jnp.zeros_like(dh)
            da_prev[...] = jnp.zeros_like(da_prev)
            db_prev[...] = jnp.zeros_like(db_prev)

        dh[...] += _dot(da_prev[...], wg_prev_ref[...]) + _dot(db_prev[...], wu_prev_ref[...])
        ds = _dot_nt(dob_scr[...], wd_ref[...])
        av = a_ref[...].astype(F32)
        bv = b_ref[...].astype(F32)
        sig = _sigmoid(av)
        dbv = (ds * (av * sig)).astype(BF16)
        dav = (ds * bv * (sig * (1.0 + av * (1.0 - sig)))).astype(BF16)
        da_ref[...] = dav
        db_ref[...] = dbv
        da_prev[...] = dav
        db_prev[...] = dbv

        @pl.when(j == nj - 1)
        def _():
            xhat, rstd = _rms(x_ref[...])
            dhv = dh[...] + _dot(dav, wg_last_ref[...]) + _dot(dbv, wu_last_ref[...])
            part = jnp.sum(dhv * xhat, axis=0, keepdims=True)

            @pl.when(i == 0)
            def _():
                dg_ref[...] = part

            @pl.when(i > 0)
            def _():
                dg_ref[...] += part

            dxh = dhv * g_ref[...]
            dx_ref[...] = do_ref[...] + rstd * (dxh - xhat * jnp.mean(dxh * xhat, axis=-1, keepdims=True))

    return pl.pallas_call(
        body,
        name=name,
        grid=(t // tm, nj),
        in_specs=[
            pl.BlockSpec((tm, D_MODEL), lambda i, j: (i, 0)),
            pl.BlockSpec((tm, D_MODEL), lambda i, j: (i, 0)),
            pl.BlockSpec((1, D_MODEL), lambda i, j: (0, 0)),
            pl.BlockSpec((tm, tf), lambda i, j: (i, j)),
            pl.BlockSpec((tm, tf), lambda i, j: (i, j)),
            pl.BlockSpec((tf, D_MODEL), lambda i, j: (jnp.maximum(j - 1, 0), 0)),
            pl.BlockSpec((tf, D_MODEL), lambda i, j: (jnp.maximum(j - 1, 0), 0)),
            pl.BlockSpec((tf, D_MODEL), lambda i, j: (nj - 1, 0)),
            pl.BlockSpec((tf, D_MODEL), lambda i, j: (nj - 1, 0)),
            pl.BlockSpec((tf, D_MODEL), lambda i, j: (j, 0)),
        ],
        out_specs=[
            pl.BlockSpec((tm, D_MODEL), lambda i, j: (i, 0)),
            pl.BlockSpec((1, D_MODEL), lambda i, j: (0, 0)),
            pl.BlockSpec((tm, tf), lambda i, j: (i, j)),
            pl.BlockSpec((tm, tf), lambda i, j: (i, j)),
            pl.BlockSpec((tm, D_MODEL), lambda i, j: (i, 0)),
        ],
        out_shape=[
            jax.ShapeDtypeStruct((t, D_MODEL), F32),
            jax.ShapeDtypeStruct((1, D_MODEL), F32),
            jax.ShapeDtypeStruct((t, D_FF), BF16),
            jax.ShapeDtypeStruct((t, D_FF), BF16),
            jax.ShapeDtypeStruct((t, D_MODEL), BF16),
        ],
        scratch_shapes=[pltpu.VMEM((tm, D_MODEL), BF16), pltpu.VMEM((tm, D_MODEL), F32), pltpu.VMEM((tm, tf), BF16),
                        pltpu.VMEM((tm, tf), BF16)],
        compiler_params=_params("arbitrary", "arbitrary", vmem_limit_bytes=FFN_BWD_VMEM_LIMIT_BYTES),
    )(dout, x, gain, a, b, wgt, wut, wgt, wut, wd)


def _norm_fwd(x, gain, *, name, tm=512):
    t = x.shape[0]

    def body(x_ref, g_ref, h_ref):
        xhat, _ = _rms(x_ref[...])
        h_ref[...] = (xhat * g_ref[...]).astype(BF16)

    return pl.pallas_call(
        body,
        name=name,
        grid=(t // tm,),
        in_specs=[pl.BlockSpec((tm, D_MODEL), lambda i: (i, 0)), pl.BlockSpec((1, D_MODEL), lambda i: (0, 0))],
        out_specs=pl.BlockSpec((tm, D_MODEL), lambda i: (i, 0)),
        out_shape=jax.ShapeDtypeStruct((t, D_MODEL), BF16),
        compiler_params=_params("parallel"),
    )(x, gain)


def _norm_bwd(dh, x, gain, dres, *, name, tm=512):
    t = x.shape[0]

    def body(dh_ref, x_ref, g_ref, dr_ref, dx_ref, dg_ref):
        i = pl.program_id(0)
        xhat, rstd = _rms(x_ref[...])
        dhv = dh_ref[...]
        part = jnp.sum(dhv * xhat, axis=0, keepdims=True)

        @pl.when(i == 0)
        def _():
            dg_ref[...] = part

        @pl.when(i > 0)
        def _():
            dg_ref[...] += part

        dxh = dhv * g_ref[...]
        dx_ref[...] = dr_ref[...] + rstd * (dxh - xhat * jnp.mean(dxh * xhat, axis=-1, keepdims=True))

    row = pl.BlockSpec((tm, D_MODEL), lambda i: (i, 0))
    vec = pl.BlockSpec((1, D_MODEL), lambda i: (0, 0))
    return pl.pallas_call(
        body,
        name=name,
        grid=(t // tm,),
        in_specs=[row, row, vec, row],
        out_specs=[row, vec],
        out_shape=[jax.ShapeDtypeStruct((t, D_MODEL), F32), jax.ShapeDtypeStruct((1, D_MODEL), F32)],
        compiler_params=_params("arbitrary"),
    )(dh, x, gain, dres)


ATT_Q_TILE = 512
ATT_K_BLOCK = 256


def _first_head_lanes():
    return lax.broadcasted_iota(jnp.int32, (1, LANES), 1) < SB_HEAD_DIM


def _stack_heads(x):
    first = _first_head_lanes()
    return jnp.concatenate([jnp.where(first, x, 0.0), jnp.where(first, 0.0, x)], axis=0)


def _unstack_heads(x, rows):
    return jnp.where(_first_head_lanes(), x[:rows], x[rows:])


def _tri(n, relation):
    r = lax.broadcasted_iota(jnp.int32, (n, n), 0)
    c = lax.broadcasted_iota(jnp.int32, (n, n), 1)
    return relation(r, c).astype(BF16)


def _scan_dot(x, tri):
    hi = x.astype(BF16)
    lo = (x - hi.astype(F32)).astype(BF16)
    return _dot(jnp.concatenate([hi, lo], axis=1), jnp.concatenate([tri, tri], axis=0))


def _log_terms(z):
    lbeta = jnp.minimum(z, 0.0) - jnp.log(1.0 + jnp.exp(-jnp.abs(z)))
    return lbeta, lbeta - z


def _attn_fwd(proj, *, name):
    t = proj.shape[0]
    tq, tk = ATT_Q_TILE, ATT_K_BLOCK
    diag = tq // tk
    n_pairs = SB_WIDTH // LANES

    def body(q_ref, k_ref, v_ref, o_ref, kept_ref):
        qi = pl.program_id(1)
        q = q_ref[...] * (SB_HEAD_DIM ** -0.5)
        qs = _stack_heads(q).astype(BF16)
        tri = _tri(tk, lambda j, s: j > s)
        trow = lax.broadcasted_iota(jnp.int32, (tq, tk), 0)
        scol = lax.broadcasted_iota(jnp.int32, (tq, tk), 1)

        def block(kb, carry, causal):
            acc, c = carry
            off = pl.multiple_of(kb * tk, tk)
            lbeta, lrest = _log_terms(_dot_nt(qs, k_ref[pl.ds(off, tk), :].astype(BF16)))
            if causal is not None:
                lrest = jnp.where(causal, lrest, 0.0)
            w = jnp.exp(lbeta + (_scan_dot(lrest, tri) + c))
            if causal is not None:
                w = jnp.where(causal, w, 0.0)
            wb = w.astype(BF16)
            kept_ref[0, 0, kb] = wb
            acc = acc + _dot(wb, v_ref[pl.ds(off, tk), :].astype(BF16))
            return acc, c + jnp.sum(lrest, axis=1, keepdims=True)

        carry = (jnp.zeros((2 * tq, LANES), F32), jnp.zeros((2 * tq, 1), F32))
        n_full = qi * diag
        for j in reversed(range(diag)):
            mask = (scol + j * tk) < trow
            carry = block(n_full + j, carry, jnp.concatenate([mask, mask], axis=0))

        def step(it, carry):
            return block(n_full - 1 - it, carry, None)

        acc, _ = lax.fori_loop(0, n_full, step, carry)
        o_ref[...] = _unstack_heads(acc, tq)

    return pl.pallas_call(
        body,
        name=name,
        grid=(n_pairs, t // tq),
        in_specs=[
            pl.BlockSpec((tq, LANES), lambda p, i: (i, p)),
            pl.BlockSpec((t, LANES), lambda p, i: (0, n_pairs + p)),
            pl.BlockSpec((t, LANES), lambda p, i: (0, 2 * n_pairs + p)),
        ],
        out_specs=[pl.BlockSpec((tq, LANES), lambda p, i: (i, p)),
                   pl.BlockSpec((1, 1, t // tk, 2 * tq, tk), lambda p, i: (p, i, 0, 0, 0))],
        out_shape=[jax.ShapeDtypeStruct((t, SB_WIDTH), F32),
                   jax.ShapeDtypeStruct((n_pairs, t // tq, t // tk, 2 * tq, tk), BF16)],
        compiler_params=_params("parallel", "parallel"),
    )(proj, proj, proj)


def _attn_bwd(proj, kept, do, *, name, tie=None):
    t = proj.shape[0]
    tq, tk = ATT_Q_TILE, ATT_K_BLOCK
    diag = tq // tk
    n_pairs = SB_WIDTH // LANES
    scale = SB_HEAD_DIM ** -0.5

    def body(q_ref, k_ref, v_ref, kept_ref, do_ref, *rest):
        dq_ref, dk_ref, dv_ref = rest[-3:]
        qi = pl.program_id(1)

        @pl.when(qi == 0)
        def _():
            dk_ref[...] = jnp.zeros_like(dk_ref)
            dv_ref[...] = jnp.zeros_like(dv_ref)

        qs = _stack_heads(q_ref[...] * scale).astype(BF16)
        dos = _stack_heads(do_ref[...]).astype(BF16)
        before = _tri(tk, lambda s, j: s < j)
        trow = lax.broadcasted_iota(jnp.int32, (tq, tk), 0)
        scol = lax.broadcasted_iota(jnp.int32, (tq, tk), 1)

        def block(kb, carry, causal):
            dq, cg = carry
            off = pl.multiple_of(kb * tk, tk)
            wb = kept_ref[0, 0, kb]
            kblk = k_ref[pl.ds(off, tk), :].astype(BF16)
            sig = 0.5 + 0.5 * jnp.tanh(0.5 * _dot_nt(qs, kblk))
            g = wb.astype(F32) * _dot_nt(dos, v_ref[pl.ds(off, tk), :].astype(BF16))
            prior = _scan_dot(g, before) + cg
            dz = g - sig * (g + prior)
            if causal is not None:
                dz = jnp.where(causal, dz, 0.0)
            dzb = dz.astype(BF16)
            dq = dq + _dot(dzb, kblk)
            dk_ref[pl.ds(off, tk), :] += _dot_tn(dzb, qs)
            dv_ref[pl.ds(off, tk), :] += _dot_tn(wb, dos)
            return dq, cg + jnp.sum(g, axis=1, keepdims=True)

        n_full = qi * diag
        carry = lax.fori_loop(0, n_full, lambda kb, carry: block(kb, carry, None),
                              (jnp.zeros((2 * tq, LANES), F32), jnp.zeros((2 * tq, 1), F32)))
        for j in range(diag):
            mask = (scol + j * tk) < trow
            carry = block(n_full + j, carry, jnp.concatenate([mask, mask], axis=0))
        dq_ref[...] = _unstack_heads(carry[0], tq) * scale

    tile_spec = pl.BlockSpec((tq, LANES), lambda p, i: (i, p))
    full_spec = pl.BlockSpec((t, LANES), lambda p, i: (0, p))
    return pl.pallas_call(
        body,
        name=name,
        grid=(n_pairs, t // tq),
        in_specs=[
            tile_spec,
            pl.BlockSpec((t, LANES), lambda p, i: (0, n_pairs + p)),
            pl.BlockSpec((t, LANES), lambda p, i: (0, 2 * n_pairs + p)),
            pl.BlockSpec((1, 1, t // tk, 2 * tq, tk), lambda p, i: (p, i, 0, 0, 0)),
            tile_spec,
        ] + ([] if tie is None else [pl.BlockSpec(memory_space=pl.ANY)]),
        out_specs=[tile_spec, full_spec, full_spec],
        out_shape=[jax.ShapeDtypeStruct((t, SB_WIDTH), F32)] * 3,
        compiler_params=_params("arbitrary", "arbitrary"),
    )(proj, proj, proj, kept, do, *([] if tie is None else [tie]))


HG_BLOCK = 256
HG_HEADS = HG_WIDTH // HG_HEAD_DIM


def _chunk_mats(n):
    r = lax.broadcasted_iota(jnp.int32, (n, n), 0)
    c = lax.broadcasted_iota(jnp.int32, (n, n), 1)
    same = (r // HG_CHUNK) == (c // HG_CHUNK)
    upto = (same & (c <= r)).astype(BF16)
    whole = same.astype(BF16)
    onward = (same & (c >= r)).astype(BF16)
    return upto, whole, onward


def _rows_dot(mat, x):
    return _dot(jnp.concatenate([mat, mat, mat], axis=1), jnp.concatenate(_split3(x), axis=0))


def _split_heads(x):
    return jnp.stack([x[:, h * HG_HEAD_DIM:(h + 1) * HG_HEAD_DIM] for h in range(HG_HEADS)], axis=0)


def _merge_heads(x):
    return jnp.concatenate([x[h] for h in range(HG_HEADS)], axis=1)


def _lower_bound(lg_ref):
    lg = lg_ref[...]
    return _sigmoid(lg[0:1, :] - lg[1:2, :])


def _hgrn_prepare(q_ref, f_ref, lb, h, upto, whole):
    cols = slice(h * HG_HEAD_DIM, (h + 1) * HG_HEAD_DIM)
    lbh = lb[:, cols]
    sg = _sigmoid(f_ref[:, cols])
    forget = lbh + (1.0 - lbh) * sg
    logf = jnp.log(forget)
    kk = (1.0 - lbh) * (1.0 - sg)
    qv = q_ref[:, cols]
    qsig = _sigmoid(qv)
    qh = qv * qsig
    b = _rows_dot(upto, logf)
    blast = _rows_dot(whole, logf)
    return dict(lbh=lbh, sg=sg, forget=forget, kk=kk, qv=qv, qsig=qsig, qh=qh, b=b, eb=jnp.exp(b),
                ekb=jnp.exp(blast - b), dl=jnp.exp(blast))


def _hgrn_fwd(proj, logits, *, name):
    t = proj.shape[0]
    tb = HG_BLOCK
    nc = tb // HG_CHUNK
    hd = HG_HEAD_DIM

    def body(q_ref, f_ref, i_ref, lg_ref, o_ref, st_ref, state, qh_s, kk_s, b_s, qe_s, ke_s, dl_s):
        @pl.when(pl.program_id(0) == 0)
        def _():
            state[...] = jnp.zeros_like(state)

        lb = _lower_bound(lg_ref)
        upto, whole, _ = _chunk_mats(tb)
        for h in range(HG_HEADS):
            p = _hgrn_prepare(q_ref, f_ref, lb, h, upto, whole)
            qh_s[h] = p["qh"]
            kk_s[h] = p["kk"]
            b_s[h] = p["b"]
            qe_s[h] = (p["qh"] * p["eb"]).astype(BF16)
            ke_s[h] = (p["kk"] * p["ekb"]).astype(BF16)
            dl_s[h] = p["dl"]
        rowi = lax.broadcasted_iota(jnp.int32, (HG_HEADS, HG_CHUNK, hd), 1)

        def chunk(c, _):
            r0 = pl.multiple_of(c * HG_CHUNK, HG_CHUNK)
            rows = pl.ds(r0, HG_CHUNK)
            bc = b_s[:, rows, :]
            qc = qh_s[:, rows, :]
            kc = kk_s[:, rows, :]
            vc = _split_heads(i_ref[rows, :])
            s_in = state[...]
            st_ref[c] = s_in
            s_in_b = s_in.astype(BF16)
            qe = qe_s[:, rows, :]
            o = jnp.stack([_dot_nt(qe[h], s_in_b[h]) for h in range(HG_HEADS)], axis=0)
            for s in range(HG_CHUNK):
                pair = jnp.where(rowi >= s, qc * jnp.exp(bc - bc[:, s:s + 1, :]) * kc[:, s:s + 1, :], 0.0)
                o = o + jnp.sum(pair, axis=2, keepdims=True) * vc[:, s:s + 1, :]
            o_ref[rows, :] = _merge_heads(o)
            vcb = vc.astype(BF16)
            ke = ke_s[:, rows, :]
            update = jnp.stack([_dot_tn(vcb[h], ke[h]) for h in range(HG_HEADS)], axis=0)
            state[...] = s_in * dl_s[:, pl.ds(r0, 1), :] + update
            return 0

        lax.fori_loop(0, nc, chunk, 0)

    blk = lambda col: pl.BlockSpec((tb, HG_WIDTH), lambda i: (i, col))
    head_f32 = pltpu.VMEM((HG_HEADS, tb, hd), F32)
    head_bf16 = pltpu.VMEM((HG_HEADS, tb, hd), BF16)
    return pl.pallas_call(
        body,
        name=name,
        grid=(t // tb,),
        in_specs=[blk(3), blk(4), blk(5), pl.BlockSpec((2, HG_WIDTH), lambda i: (0, 0))],
        out_specs=[
            pl.BlockSpec((tb, HG_WIDTH), lambda i: (i, 0)),
            pl.BlockSpec((nc, HG_HEADS, hd, hd), lambda i: (i, 0, 0, 0)),
        ],
        out_shape=[
            jax.ShapeDtypeStruct((t, HG_WIDTH), F32),
            jax.ShapeDtypeStruct((t // HG_CHUNK, HG_HEADS, hd, hd), F32),
        ],
        scratch_shapes=[pltpu.VMEM((HG_HEADS, hd, hd), F32), head_f32, head_f32, head_f32, head_bf16, head_bf16,
                        head_f32],
        compiler_params=_params("arbitrary"),
    )(proj, proj, proj, logits)


def _hgrn_bwd(proj, logits, states, do, *, name):
    t = proj.shape[0]
    tb = HG_BLOCK
    nb = t // tb
    nc = tb // HG_CHUNK
    hd = HG_HEAD_DIM

    def body(q_ref, f_ref, i_ref, lg_ref, st_ref, do_ref, dq_ref, df_ref, di_ref, dlb_ref,
             dstate, qh_s, kk_s, b_s, eb_s, ekb_s, qe_s, ke_s, dl_s, dqh_s, dkk_s, dlf_s):
        step = pl.program_id(0)

        @pl.when(step == 0)
        def _():
            dstate[...] = jnp.zeros_like(dstate)
            dlb_ref[...] = jnp.zeros_like(dlb_ref)

        lb = _lower_bound(lg_ref)
        upto, whole, _ = _chunk_mats(tb)
        prepared = []
        for h in range(HG_HEADS):
            p = _hgrn_prepare(q_ref, f_ref, lb, h, upto, whole)
            prepared.append(p)
            qh_s[h] = p["qh"]
            kk_s[h] = p["kk"]
            b_s[h] = p["b"]
            eb_s[h] = p["eb"]
            ekb_s[h] = p["ekb"]
            qe_s[h] = (p["qh"] * p["eb"]).astype(BF16)
            ke_s[h] = (p["kk"] * p["ekb"]).astype(BF16)
            dl_s[h] = p["dl"]
        rowi = lax.broadcasted_iota(jnp.int32, (HG_CHUNK, hd), 0)
        r16 = lax.broadcasted_iota(jnp.int32, (HG_CHUNK, HG_CHUNK), 0)
        c16 = lax.broadcasted_iota(jnp.int32, (HG_CHUNK, HG_CHUNK), 1)
        onward = (c16 >= r16).astype(BF16)

        def chunk(it, _):
            c = nc - 1 - it
            r0 = pl.multiple_of(c * HG_CHUNK, HG_CHUNK)
            rows = pl.ds(r0, HG_CHUNK)
            for h in range(HG_HEADS):
                cols = slice(h * hd, (h + 1) * hd)
                bc = b_s[h, rows, :]
                qc = qh_s[h, rows, :]
                kc = kk_s[h, rows, :]
                vc = i_ref[rows, cols]
                doc = do_ref[rows, cols]
                s_in = st_ref[c, h]
                ds_out = dstate[h]
                ds_out_b = ds_out.astype(BF16)
                docb = doc.astype(BF16)
                dl_row = dl_s[h, pl.ds(r0, 1), :]
                dqh = _dot(docb, s_in.astype(BF16)) * eb_s[h, rows, :]
                dkk = _dot(vc.astype(BF16), ds_out_b) * ekb_s[h, rows, :]
                dv = _dot_nt(ke_s[h, rows, :], ds_out_b)
                db = dqh * qc - dkk * kc
                dwhole = jnp.sum(dkk * kc, axis=0, keepdims=True) + jnp.sum(ds_out * s_in, axis=0, keepdims=True) * dl_row
                dk_rows, dv_rows = [], []
                for s in range(HG_CHUNK):
                    keep = rowi >= s
                    e = jnp.exp(bc - bc[s:s + 1, :])
                    k_row = kc[s:s + 1, :]
                    pcol = jnp.sum(jnp.where(keep, qc * e * k_row, 0.0), axis=1, keepdims=True)
                    dpcol = jnp.sum(doc * vc[s:s + 1, :], axis=1, keepdims=True)
                    m = jnp.where(keep, e * dpcol, 0.0)
                    y = m * qc
                    dqh = dqh + m * k_row
                    db = db + y * k_row
                    dk_rows.append(jnp.sum(y, axis=0, keepdims=True))
                    dv_rows.append(jnp.sum(pcol * doc, axis=0, keepdims=True))
                dkk_pairs = jnp.concatenate(dk_rows, axis=0)
                dkk = dkk + dkk_pairs
                db = db - dkk_pairs * kc
                dv = dv + jnp.concatenate(dv_rows, axis=0)
                dqh_s[h, rows, :] = dqh
                dkk_s[h, rows, :] = dkk
                dlf_s[h, rows, :] = _rows_dot(onward, db) + dwhole
                di_ref[rows, cols] = dv
                dstate[h] = ds_out * dl_row + _dot_tn(docb, qe_s[h, rows, :])
            return 0

        lax.fori_loop(0, nc, chunk, 0)
        for h in range(HG_HEADS):
            cols = slice(h * hd, (h + 1) * hd)
            p = prepared[h]
            dq_ref[:, cols] = dqh_s[h] * (p["qsig"] * (1.0 + p["qv"] * (1.0 - p["qsig"])))
            dforget = dlf_s[h] / p["forget"] - dkk_s[h]
            df_ref[:, cols] = dforget * (1.0 - p["lbh"]) * p["sg"] * (1.0 - p["sg"])
            dlb_ref[:, cols] += jnp.sum(dforget * (1.0 - p["sg"]), axis=0, keepdims=True)

    blk = lambda col: pl.BlockSpec((tb, HG_WIDTH), lambda i: (nb - 1 - i, col))
    vec = pl.BlockSpec((1, HG_WIDTH), lambda i: (0, 0))
    head_f32 = pltpu.VMEM((HG_HEADS, tb, hd), F32)
    head_bf16 = pltpu.VMEM((HG_HEADS, tb, hd), BF16)
    return pl.pallas_call(
        body,
        name=name,
        grid=(nb,),
        in_specs=[
            blk(3), blk(4), blk(5),
            pl.BlockSpec((2, HG_WIDTH), lambda i: (0, 0)),
            pl.BlockSpec((nc, HG_HEADS, hd, hd), lambda i: (nb - 1 - i, 0, 0, 0)),
            blk(0),
        ],
        out_specs=[blk(0), blk(0), blk(0), vec],
        out_shape=[jax.ShapeDtypeStruct((t, HG_WIDTH), F32)] * 3 + [jax.ShapeDtypeStruct((1, HG_WIDTH), F32)],
        scratch_shapes=[
            pltpu.VMEM((HG_HEADS, hd, hd), F32),
            head_f32, head_f32, head_f32, head_f32, head_f32, head_bf16, head_bf16, head_f32,
            head_f32, head_f32, head_f32,
        ],
        compiler_params=_params("arbitrary"),
    )(proj, proj, proj, logits, states, do)


def _group_mat(width, head_dim):
    r = lax.broadcasted_iota(jnp.int32, (width, width), 0)
    c = lax.broadcasted_iota(jnp.int32, (width, width), 1)
    return ((r // head_dim) == (c // head_dim)).astype(BF16)


def _head_mean(x, mat, head_dim):
    hi = x.astype(BF16)
    lo = (x - hi.astype(F32)).astype(BF16)
    return (_dot(hi, mat) + _dot(lo, mat)) * (1.0 / head_dim)


def _mix_out_fwd(o_sb, o_hg, proj, g_sb, g_hg, w_out, x1, *, name, tm=256):
    t = x1.shape[0]

    def body(osb_ref, ohg_ref, gate_ref, gsb_ref, ghg_ref, w_ref, x_ref, xo_ref, mt_ref):
        msb = _group_mat(SB_WIDTH, SB_HEAD_DIM)
        mhg = _group_mat(HG_WIDTH, HG_HEAD_DIM)
        osb = osb_ref[...]
        ohg = ohg_ref[...]
        nsb = osb * lax.rsqrt(_head_mean(osb * osb, msb, SB_HEAD_DIM) + EPS) * gsb_ref[...]
        gate = gate_ref[...]
        nhg = ohg * lax.rsqrt(_head_mean(ohg * ohg, mhg, HG_HEAD_DIM) + EPS) * ghg_ref[...] * (gate * _sigmoid(gate))
        mixed = jnp.concatenate([nsb, nhg], axis=1).astype(BF16)
        mt_ref[...] = mixed
        xo_ref[...] = x_ref[...] + _dot(mixed, w_ref[...])

    half = pl.BlockSpec((tm, SB_WIDTH), lambda i: (i, 0))
    vec = pl.BlockSpec((1, SB_WIDTH), lambda i: (0, 0))
    row = pl.BlockSpec((tm, D_MODEL), lambda i: (i, 0))
    return pl.pallas_call(
        body,
        name=name,
        grid=(t // tm,),
        in_specs=[half, half, pl.BlockSpec((tm, HG_WIDTH), lambda i: (i, 6)), vec, vec,
                  pl.BlockSpec((D_MODEL, D_MODEL), lambda i: (0, 0)), row],
        out_specs=[row, row],
        out_shape=[jax.ShapeDtypeStruct((t, D_MODEL), F32), jax.ShapeDtypeStruct((t, D_MODEL), BF16)],
        compiler_params=_params("parallel"),
    )(o_sb, o_hg, proj, g_sb, g_hg, w_out, x1)


def _mix_out_bwd(dx2, o_sb, o_hg, proj, g_sb, g_hg, w_out, *, name, tm=256):
    t = dx2.shape[0]

    def body(dx_ref, osb_ref, ohg_ref, gate_ref, gsb_ref, ghg_ref, w_ref, dosb_ref, dohg_ref, dgate_ref, dgsb_ref,
             dghg_ref, dxb_ref):
        i = pl.program_id(0)
        msb = _group_mat(SB_WIDTH, SB_HEAD_DIM)
        mhg = _group_mat(HG_WIDTH, HG_HEAD_DIM)
        dxb = dx_ref[...].astype(BF16)
        dxb_ref[...] = dxb
        dmixed = _dot_nt(dxb, w_ref[...])
        dnsb = dmixed[:, :SB_WIDTH]
        dy = dmixed[:, SB_WIDTH:]

        osb = osb_ref[...]
        rstd = lax.rsqrt(_head_mean(osb * osb, msb, SB_HEAD_DIM) + EPS)
        ohat = osb * rstd
        part_sb = jnp.sum(dnsb * ohat, axis=0, keepdims=True)
        dohat = dnsb * gsb_ref[...]
        dosb_ref[...] = rstd * (dohat - ohat * _head_mean(dohat * ohat, msb, SB_HEAD_DIM))

        ohg = ohg_ref[...]
        rstd = lax.rsqrt(_head_mean(ohg * ohg, mhg, HG_HEAD_DIM) + EPS)
        ohat = ohg * rstd
        gate = gate_ref[...]
        sig = _sigmoid(gate)
        dn = dy * (gate * sig)
        dgate_ref[...] = dy * (ohat * ghg_ref[...]) * (sig * (1.0 + gate * (1.0 - sig)))
        part_hg = jnp.sum(dn * ohat, axis=0, keepdims=True)
        dohat = dn * ghg_ref[...]
        dohg_ref[...] = rstd * (dohat - ohat * _head_mean(dohat * ohat, mhg, HG_HEAD_DIM))

        @pl.when(i == 0)
        def _():
            dgsb_ref[...] = part_sb
            dghg_ref[...] = part_hg

        @pl.when(i > 0)
        def _():
            dgsb_ref[...] += part_sb
            dghg_ref[...] += part_hg

    half = pl.BlockSpec((tm, SB_WIDTH), lambda i: (i, 0))
    vec = pl.BlockSpec((1, SB_WIDTH), lambda i: (0, 0))
    row = pl.BlockSpec((tm, D_MODEL), lambda i: (i, 0))
    return pl.pallas_call(
        body,
        name=name,
        grid=(t // tm,),
        in_specs=[row, half, half, pl.BlockSpec((tm, HG_WIDTH), lambda i: (i, 6)), vec, vec,
                  pl.BlockSpec((D_MODEL, D_MODEL), lambda i: (0, 0))],
        out_specs=[half, half, half, vec, vec, row],
        out_shape=[jax.ShapeDtypeStruct((t, SB_WIDTH), F32)] * 3 + [jax.ShapeDtypeStruct((1, SB_WIDTH), F32)] * 2
        + [jax.ShapeDtypeStruct((t, D_MODEL), BF16)],
        compiler_params=_params("arbitrary"),
    )(dx2, o_sb, o_hg, proj, g_sb, g_hg, w_out)


def _loss_head(x3, gain, target, *, name, tm=512):
    t = x3.shape[0]

    def body(x_ref, g_ref, y_ref, dx_ref, dg_ref, loss_ref):
        i = pl.program_id(0)
        xhat, rstd = _rms(x_ref[...])
        err = xhat * g_ref[...] - y_ref[...]
        part_loss = 0.5 * jnp.sum(jnp.mean(err * err, axis=-1, keepdims=True), axis=0, keepdims=True)
        dy = err * (1.0 / D_MODEL)
        part_g = jnp.sum(dy * xhat, axis=0, keepdims=True)

        @pl.when(i == 0)
        def _():
            dg_ref[...] = part_g
            loss_ref[...] = jnp.broadcast_to(part_loss, loss_ref.shape)

        @pl.when(i > 0)
        def _():
            dg_ref[...] += part_g
            loss_ref[...] += jnp.broadcast_to(part_loss, loss_ref.shape)

        dxh = dy * g_ref[...]
        dx_ref[...] = rstd * (dxh - xhat * jnp.mean(dxh * xhat, axis=-1, keepdims=True))

    row = pl.BlockSpec((tm, D_MODEL), lambda i: (i, 0))
    vec = pl.BlockSpec((1, D_MODEL), lambda i: (0, 0))
    return pl.pallas_call(
        body,
        name=name,
        grid=(t // tm,),
        in_specs=[row, vec, row],
        out_specs=[row, vec, vec],
        out_shape=[jax.ShapeDtypeStruct((t, D_MODEL), F32), jax.ShapeDtypeStruct((1, D_MODEL), F32),
                   jax.ShapeDtypeStruct((1, D_MODEL), F32)],
        compiler_params=_params("arbitrary"),
    )(x3, gain, target)


def _local_step(x, target, norms, logits, w, weights_after=None, grads_ready=None):
    w = dict(w)
    x1, a1, b1, h1, s1 = _ffn_fwd(x, norms["ffn1"], w["g1t"], w["u1t"], w["d1"], name="ffn1_fwd")
    if weights_after is not None:
        w.update(weights_after("ffn1", x1))
    hm = _norm_fwd(x1, norms["mix"], name="mix_norm_fwd")
    proj = _mm(hm, w["int"], name="in_proj", tm=512, tn=IN_COLS, nt=True)
    o_sb, sb_kept = _attn_fwd(proj, name="sb_attn_fwd")
    o_hg, states = _hgrn_fwd(proj, logits, name="hgrn2_fwd")
    x2, mixed = _mix_out_fwd(o_sb, o_hg, proj, norms["sb"], norms["hg"], w["out"], x1, name="mix_out_fwd")
    if weights_after is not None:
        w.update(weights_after("mix", x2))
    x3, a2, b2, h2, s2 = _ffn_fwd(x2, norms["ffn2"], w["g2t"], w["u2t"], w["d2"], name="ffn2_fwd")
    dx3, d_final, loss_row = _loss_head(x3, norms["final"], target, name="loss_head")

    def weight_grad(lhs, rhs, name, tie=None):
        return _mm(lhs, rhs, name=name, tm=256, tn=D_MODEL, ta=True, out_dtype=BF16, tie=tie)

    def sent(stage):
        return grads_ready(stage, gw) if grads_ready is not None else None

    gw, gv = {}, {"final": d_final}
    dx2, gv["ffn2"], da2, db2, dob2 = _ffn_bwd(dx3, x2, norms["ffn2"], a2, b2, w["g2t"], w["u2t"], w["d2"],
                                               name="ffn2_bwd")
    gw["g2t"] = weight_grad(da2, h2, "ffn2_dgate")
    gw["u2t"] = weight_grad(db2, h2, "ffn2_dup")
    gw["d2"] = weight_grad(s2, dob2, "ffn2_ddown")

    do_sb, do_hg, d_gate, gv["sb"], gv["hg"], dx2b = _mix_out_bwd(
        dx2, o_sb, o_hg, proj, norms["sb"], norms["hg"], w["out"], name="mix_out_bwd")
    gw["out"] = weight_grad(mixed, dx2b, "out_dw")
    tie = sent("mix")
    dq_sb, dk_sb, dv_sb = _attn_bwd(proj, sb_kept, do_sb, name="sb_attn_bwd", tie=tie)
    dq_hg, df_hg, di_hg, d_lb = _hgrn_bwd(proj, logits if tie is None else logits + tie[0, 0], states, do_hg,
                                          name="hgrn2_bwd")
    dproj = jnp.concatenate([dq_sb, dk_sb, dv_sb, dq_hg, df_hg, di_hg, d_gate], axis=1).astype(BF16)
    gw["int"] = weight_grad(dproj, hm, "in_dw")
    tie = sent("in")
    dhm = _mm(dproj, w["int"], name="in_dx", tm=512, tn=D_MODEL)
    dx1, gv["mix"] = _norm_bwd(dhm, x1, norms["mix"] if tie is None else norms["mix"] + tie[0, 0], dx2,
                               name="mix_norm_bwd")

    dx, gv["ffn1"], da1, db1, dob1 = _ffn_bwd(dx1, x, norms["ffn1"], a1, b1, w["g1t"], w["u1t"], w["d1"],
                                              name="ffn1_bwd")
    gw["g1t"] = weight_grad(da1, h1, "ffn1_dgate")
    gw["u1t"] = weight_grad(db1, h1, "ffn1_dup", tie=sent("g1t"))
    gw["d1"] = weight_grad(s1, dob1, "ffn1_ddown", tie=sent("u1t"))
    sent("d1")
    gv["lb"] = d_lb
    return loss_row, dx, gw, gv


HBM = pl.BlockSpec(memory_space=pl.ANY)


def _place():
    return lax.axis_index("x"), lax.axis_index("y"), lax.axis_index("c")


def _slot(px, py, pc):
    return 4 * px + 2 * py + pc


def _all_gather(blocks, *, name):
    n = len(blocks)

    def body(*refs):
        ins, outs = refs[:n], refs[n:2 * n]
        send_sems, recv_sems, local_sems = refs[2 * n:]
        x, y, c = _place()
        me, sibling = (x, y, c), (x, y, 1 - c)
        chips = [(1 - x, y), (x, 1 - y), (1 - x, 1 - y)]

        def copy(a, k, block, to, src=None):
            dst = outs[a].at[_slot(*block)]
            return pltpu.make_async_remote_copy(
                src_ref=dst if src is None else src, dst_ref=dst, send_sem=send_sems.at[7 * a + k],
                recv_sem=recv_sems.at[7 * a + k], device_id=to, device_id_type=MESH)

        mine = [pltpu.make_async_copy(ins[a], outs[a].at[_slot(*me)], local_sems.at[a]) for a in range(n)]
        for cp in mine:
            cp.start()
        first = []
        for a in range(n):
            first.append(copy(a, 0, me, sibling, src=ins[a]))
            first += [copy(a, 1 + j, me, (*chip, c), src=ins[a]) for j, chip in enumerate(chips)]
        for cp in first:
            cp.start()
        passed = []
        for j, chip in enumerate(chips):
            for a in range(n):
                copy(a, 1 + j, (*chip, c), me).wait_recv()
                fwd = copy(a, 4 + j, (*chip, c), sibling)
                fwd.start()
                passed.append(fwd)
        for a in range(n):
            copy(a, 0, sibling, me).wait_recv()
            for j, chip in enumerate(chips):
                copy(a, 4 + j, (*chip, 1 - c), me).wait_recv()
        for cp in first + passed:
            cp.wait_send()
        for cp in mine:
            cp.wait()

    return pl.pallas_call(
        body,
        name=name,
        in_specs=[HBM] * n,
        out_specs=[HBM] * n,
        out_shape=[jax.ShapeDtypeStruct((N_DEV,) + b.shape, b.dtype) for b in blocks],
        scratch_shapes=[pltpu.SemaphoreType.DMA((7 * n,)), pltpu.SemaphoreType.DMA((7 * n,)),
                        pltpu.SemaphoreType.DMA((n,))],
    )(*blocks)


def _flipped(place, d):
    return tuple(1 - p if (d >> (2 - axis)) & 1 else p for axis, p in enumerate(place))


SEM = pl.BlockSpec(memory_space=pltpu.SEMAPHORE)
EFFECT = pltpu.SideEffectType.DATAFLOW_SIDE_EFFECTING


def _split_copies(me, srcs, lands, send_sems, recv_sems, by_owner):
    copies = []
    for d in range(1, N_DEV):
        peer = _flipped(me, d)
        for a, (src, land) in enumerate(zip(srcs, lands)):
            copies.append(pltpu.make_async_remote_copy(
                src_ref=src.at[_slot(*peer)] if by_owner else src, dst_ref=land.at[_slot(*me)],
                send_sem=send_sems.at[7 * a + d - 1], recv_sem=recv_sems.at[7 * a + d - 1], device_id=peer,
                device_id_type=MESH))
    return copies


def _copies_start(srcs, *, name, by_owner, after=None):
    n = len(srcs)
    extra = [] if after is None else [after]
    land_shapes = [s.shape if by_owner else (N_DEV,) + s.shape for s in srcs]
    lands = [pltpu.with_memory_space_constraint(lax.empty(shape, s.dtype), pltpu.HBM) for shape, s in zip(land_shapes, srcs)]
    srcs = [pltpu.with_memory_space_constraint(s, pltpu.HBM) for s in srcs]

    def body(*refs):
        src_refs, land_refs = refs[:n], refs[n:2 * n]
        send_sems, recv_sems = refs[2 * n + len(extra)], refs[2 * n + len(extra) + 1]
        token = refs[-1]
        for cp in _split_copies(_place(), src_refs, land_refs, send_sems, recv_sems, by_owner):
            cp.start()
        token[...] = jnp.zeros_like(token)

    out = pl.pallas_call(
        body,
        name=name,
        in_specs=[HBM] * (2 * n + len(extra)),
        out_specs=[SEM, SEM] + [HBM] * (2 * n) + [pl.BlockSpec(memory_space=pltpu.VMEM)],
        out_shape=[pltpu.SemaphoreType.DMA((7 * n,)), pltpu.SemaphoreType.DMA((7 * n,))]
        + [pltpu.HBM(s.shape, s.dtype) for s in srcs] + [pltpu.HBM(shape, s.dtype) for shape, s in zip(land_shapes, srcs)]
        + [jax.ShapeDtypeStruct((8, LANES), F32)],
        input_output_aliases={i: 2 + i for i in range(2 * n)},
        compiler_params=pltpu.CompilerParams(has_side_effects=EFFECT),
    )(*srcs, *lands, *extra)
    return (out[0], out[1], out[2:2 + n], out[2 + n:2 + 2 * n]), out[-1]


def _copies_wait(started, after, *, name, by_owner):
    send_sems, recv_sems, srcs, lands = started
    n = len(srcs)

    def body(*refs):
        src_refs, land_refs = refs[:n], refs[n:2 * n]
        for cp in _split_copies(_place(), src_refs, land_refs, refs[2 * n], refs[2 * n + 1], by_owner):
            cp.wait_send()
            cp.wait_recv()

    out = pl.pallas_call(
        body,
        name=name,
        in_specs=[HBM] * (2 * n) + [SEM, SEM, HBM],
        out_specs=[HBM] * (2 * n),
        out_shape=[pltpu.HBM(s.shape, s.dtype) for s in srcs] + [pltpu.HBM(s.shape, s.dtype) for s in lands],
        input_output_aliases={i: i for i in range(2 * n)},
        compiler_params=pltpu.CompilerParams(has_side_effects=EFFECT),
    )(*srcs, *lands, send_sems, recv_sems, after)
    return out[n:]


def _with_own(lands, own, slot):
    zero = jnp.zeros((), jnp.int32)
    return [lax.dynamic_update_slice(land, o[None], (slot.astype(jnp.int32),) + (zero,) * o.ndim)
            for land, o in zip(lands, own)]


def _adamw(w, g, m, v):
    m = ADAM_B1 * m + (1.0 - ADAM_B1) * g
    v = ADAM_B2 * v + (1.0 - ADAM_B2) * (g * g)
    m_hat = m / (1.0 - ADAM_B1 ** ADAM_STEP)
    v_hat = v / (1.0 - ADAM_B2 ** ADAM_STEP)
    delta = -ADAM_LR * (m_hat / (jnp.sqrt(v_hat) + ADAM_EPS) + ADAM_WD * w)
    return delta, m, v


def _sum_and_update(parts, w, m, v, *, name, tr, transposed=False, tie=None):
    _, rows, cols = w.shape
    pad = -cols % LANES

    def body(p_ref, w_ref, m_ref, v_ref, *rest):
        g_ref, d_ref, mo_ref, vo_ref = rest[-4:]
        g = p_ref[0].astype(F32)
        for s in range(1, N_DEV):
            g = g + p_ref[s].astype(F32)
        if transposed:
            if pad:
                g = jnp.concatenate([g, jnp.zeros((pad, tr), F32)], axis=0)
            g = g.T[:, :cols]
        g_ref[0] = g
        d_ref[0], mo_ref[0], vo_ref[0] = _adamw(w_ref[0], g, m_ref[0], v_ref[0])

    flat = pl.BlockSpec((1, tr, cols), lambda i: (0, i, 0))
    if transposed:
        part_spec = pl.BlockSpec((N_DEV, cols, tr), lambda i: (0, 0, i))
    else:
        part_spec = pl.BlockSpec((N_DEV, tr, cols), lambda i: (0, i, 0))
    return pl.pallas_call(
        body,
        name=name,
        grid=(rows // tr,),
        in_specs=[part_spec, flat, flat, flat] + ([] if tie is None else [pl.BlockSpec(memory_space=pl.ANY)]),
        out_specs=[flat] * 4,
        out_shape=[jax.ShapeDtypeStruct((1, rows, cols), F32)] * 4,
        compiler_params=_params("parallel"),
    )(parts, w, m, v, *([] if tie is None else [tie]))


VEC_ROWS = 8
ROW_LOGITS, ROW_LOSS = 5, 7


def _vectors_update(part, w, m, v, *, name):
    def body(p_ref, w_ref, m_ref, v_ref, g_ref, d_ref, mo_ref, vo_ref, loss_ref, all_ref, send_sems, recv_sems):
        me = _place()
        all_ref[_slot(*me)] = p_ref[...]
        copies = []
        for d in range(1, N_DEV):
            peer = _flipped(me, d)
            copies.append(pltpu.make_async_remote_copy(
                src_ref=p_ref, dst_ref=all_ref.at[_slot(*me)], send_sem=send_sems.at[d - 1], recv_sem=recv_sems.at[d - 1],
                device_id=peer, device_id_type=MESH))
        for cp in copies:
            cp.start()
        for cp in copies:
            cp.wait()
        total = all_ref[0]
        for s in range(1, N_DEV):
            total = total + all_ref[s]
        wv = w_ref[...]
        half = D_MODEL // 2
        lb = _sigmoid(wv[ROW_LOGITS:ROW_LOGITS + 1, :half] - wv[ROW_LOGITS:ROW_LOGITS + 1, half:])
        d_first = total[ROW_LOGITS:ROW_LOGITS + 1, :half] * lb * (1.0 - lb)
        d_logits = jnp.concatenate([d_first, -d_first], axis=1)
        rowi = lax.broadcasted_iota(jnp.int32, (VEC_ROWS, D_MODEL), 0)
        g = jnp.where(rowi == ROW_LOGITS, d_logits, jnp.where(rowi < ROW_LOGITS, total, 0.0))
        g_ref[...] = g
        d_ref[...], mo_ref[...], vo_ref[...] = _adamw(wv, g, m_ref[...], v_ref[...])
        loss_ref[...] = total[ROW_LOSS:ROW_LOSS + 1, :]

    vmem = pl.BlockSpec(memory_space=pltpu.VMEM)
    return pl.pallas_call(
        body,
        name=name,
        in_specs=[vmem] * 4,
        out_specs=[vmem] * 5,
        out_shape=[jax.ShapeDtypeStruct((VEC_ROWS, D_MODEL), F32)] * 4 + [jax.ShapeDtypeStruct((1, D_MODEL), F32)],
        scratch_shapes=[pltpu.VMEM((N_DEV, VEC_ROWS, D_MODEL), F32), pltpu.SemaphoreType.DMA((7,)),
                        pltpu.SemaphoreType.DMA((7,))],
    )(part, w, m, v)


TRANSPOSED = ("g1t", "u1t", "g2t", "u2t", "int")


def _vector_rows(rows):
    rowi = lax.broadcasted_iota(jnp.int32, (VEC_ROWS, D_MODEL), 0)
    out = jnp.zeros((VEC_ROWS, D_MODEL), F32)
    for i, r in enumerate(rows):
        if r is not None:
            out = jnp.where(rowi == i, r, out)
    return out


def kernel(x, ffn1_norm, ffn1_w_gate, ffn1_w_up, ffn1_w_down, mix_norm, w_in, sb_out_norm, hg_lower_bound_logits, hg_out_norm, w_out, ffn2_norm, ffn2_w_gate, ffn2_w_up, ffn2_w_down, final_norm, loss_target, m_ffn1_norm, m_ffn1_w_gate, m_ffn1_w_up, m_ffn1_w_down, m_mix_norm, m_w_in, m_sb_out_norm, m_hg_lower_bound_logits, m_hg_out_norm, m_w_out, m_ffn2_norm, m_ffn2_w_gate, m_ffn2_w_up, m_ffn2_w_down, m_final_norm, v_ffn1_norm, v_ffn1_w_gate, v_ffn1_w_up, v_ffn1_w_down, v_mix_norm, v_w_in, v_sb_out_norm, v_hg_lower_bound_logits, v_hg_out_norm, v_w_out, v_ffn2_norm, v_ffn2_w_gate, v_ffn2_w_up, v_ffn2_w_down, v_final_norm):
    def matrices(g1, u1, d1, win, wout, g2, u2, d2):
        return {"g1t": g1, "u1t": u1, "d1": d1, "int": win, "out": wout, "g2t": g2, "u2t": u2, "d2": d2}

    def vectors(n1, nm, nsb, lg, nhg, n2, nf):
        return [n1, nm, n2, nf.reshape(1, D_MODEL), jnp.concatenate([nsb, nhg], axis=1), lg.reshape(1, D_MODEL), None, None]

    w_sh = matrices(ffn1_w_gate, ffn1_w_up, ffn1_w_down, w_in, w_out, ffn2_w_gate, ffn2_w_up, ffn2_w_down)
    m_sh = matrices(m_ffn1_w_gate, m_ffn1_w_up, m_ffn1_w_down, m_w_in, m_w_out, m_ffn2_w_gate, m_ffn2_w_up, m_ffn2_w_down)
    v_sh = matrices(v_ffn1_w_gate, v_ffn1_w_up, v_ffn1_w_down, v_w_in, v_w_out, v_ffn2_w_gate, v_ffn2_w_up, v_ffn2_w_down)
    keys = list(w_sh)

    slot = _slot(*_place())

    def full(key, stack):
        return stack.reshape(-1, D_MODEL)

    def by_owner(key, grad):
        return grad.reshape(N_DEV, -1, D_MODEL)

    blocks = {k: (w_sh[k][0].T if k in TRANSPOSED else w_sh[k][0]).astype(BF16) for k in keys}
    first, mid, last = ("g1t", "u1t", "d1"), ("int", "out"), ("g2t", "u2t", "d2")
    w_first = {k: full(k, s) for k, s in zip(first, _all_gather([blocks[k] for k in first], name="gather_ffn1"))}
    flights = {}
    flights["ffn1"], token_mid = _copies_start([blocks[k] for k in mid], name="gather_mid_start", by_owner=False,
                                               after=w_first["d1"])
    flights["mix"], token_last = _copies_start([blocks[k] for k in last], name="gather_ffn2_start", by_owner=False,
                                               after=token_mid)

    def weights_after(stage, result):
        group = mid if stage == "ffn1" else last
        lands = _copies_wait(flights[stage], result, name="gather_" + stage + "_wait", by_owner=False)
        return {k: full(k, s) for k, s in zip(group, _with_own(lands, [blocks[k] for k in group], slot))}

    groups = {"mix": ("g2t", "u2t", "d2", "out"), "in": ("int",), "g1t": ("g1t",), "u1t": ("u1t",), "d1": ("d1",)}
    sent, sent_tokens = {}, []

    def grads_ready(stage, gw):
        stacks = [by_owner(k, gw[k]) for k in groups[stage]]
        flight, token = _copies_start(stacks, name="grads_" + stage + "_start", by_owner=True)
        sent[stage] = (stacks, flight)
        sent_tokens.append(token)
        return token

    norms = {"ffn1": ffn1_norm + token_last[0, 0], "mix": mix_norm, "sb": sb_out_norm, "hg": hg_out_norm,
             "ffn2": ffn2_norm, "final": final_norm.reshape(1, D_MODEL)}
    loss_row, grad_x, gw, gv = _local_step(x[0], loss_target[0], norms, hg_lower_bound_logits, w_first, weights_after,
                                           grads_ready)

    tiles = {"g1t": 256, "u1t": 256, "g2t": 256, "u2t": 256, "d1": 176, "d2": 176, "out": 128, "int": 256}
    updated, after = {}, sent_tokens[-1]
    for stage, (stacks, flight) in sent.items():
        lands = _copies_wait(flight, after, name="grads_" + stage + "_wait", by_owner=True)
        own = [lax.dynamic_index_in_dim(s, slot, keepdims=False) for s in stacks]
        for k, part in zip(groups[stage], _with_own(lands, own, slot)):
            updated[k] = _sum_and_update(part, w_sh[k], m_sh[k], v_sh[k], name="adamw_" + k, tr=tiles[k],
                                         transposed=k in TRANSPOSED, tie=after)
            after = updated[k][0]
    mats = [{k: updated[k][i] for k in keys} for i in range(4)]

    lb_row = jnp.concatenate([gv["lb"], jnp.zeros_like(gv["lb"])], axis=1)
    part = _vector_rows([gv["ffn1"], gv["mix"], gv["ffn2"], gv["final"], jnp.concatenate([gv["sb"], gv["hg"]], axis=1),
                         lb_row, None, loss_row])
    vec_w = _vector_rows(vectors(ffn1_norm, mix_norm, sb_out_norm, hg_lower_bound_logits, hg_out_norm, ffn2_norm, final_norm))
    vec_m = _vector_rows(vectors(m_ffn1_norm, m_mix_norm, m_sb_out_norm, m_hg_lower_bound_logits, m_hg_out_norm,
                                 m_ffn2_norm, m_final_norm))
    vec_v = _vector_rows(vectors(v_ffn1_norm, v_mix_norm, v_sb_out_norm, v_hg_lower_bound_logits, v_hg_out_norm,
                                 v_ffn2_norm, v_final_norm))
    *vecs, loss_out = _vectors_update(part, vec_w, vec_m, vec_v, name="vectors_update")

    def leaves(mat, vec):
        half = D_MODEL // 2
        return (
            vec[0:1], mat["g1t"], mat["u1t"], mat["d1"], vec[1:2], mat["int"], vec[4:5, :half],
            vec[ROW_LOGITS].reshape(2, half), vec[4:5, half:], mat["out"], vec[2:3], mat["g2t"], mat["u2t"],
            mat["d2"], vec[3],
        )

    out = [loss_out[0, 0], grad_x[None]]
    for mat, vec in zip(mats, vecs):
        out.extend(leaves(mat, vec))
    return tuple(out)
```

```python
import jax
import jax.numpy as jnp
from jax import lax
from jax.experimental import pallas as pl
from jax.experimental.pallas import tpu as pltpu

F32, BF16 = jnp.float32, jnp.bfloat16
D_MODEL = 1024
D_FF = 2816
SB_WIDTH = 512
HG_WIDTH = 512
SB_HEAD_DIM = 64
HG_HEAD_DIM = 128
IN_COLS = 3584
EPS = 1e-6
N_DEV = 8
LANES = 128
HG_CHUNK = 16
VMEM_LIMIT_BYTES = 48 * 1024 * 1024
FFN_BWD_VMEM_LIMIT_BYTES = 56 * 1024 * 1024
ADAM_LR, ADAM_B1, ADAM_B2, ADAM_EPS, ADAM_WD, ADAM_STEP = 0.001, 0.9, 0.999, 1e-08, 0.01, 10
MESH = pl.DeviceIdType.MESH


def _params(*semantics, vmem_limit_bytes=VMEM_LIMIT_BYTES):
    return pltpu.CompilerParams(dimension_semantics=semantics, vmem_limit_bytes=vmem_limit_bytes)


def _dot(a, b):
    return jnp.dot(a, b, preferred_element_type=F32)


def _dot_nt(a, b):
    return lax.dot_general(a, b, (((1,), (1,)), ((), ())), preferred_element_type=F32)


def _dot_tn(a, b):
    return lax.dot_general(a, b, (((0,), (0,)), ((), ())), preferred_element_type=F32)


def _split3(x):
    hi = x.astype(BF16)
    r1 = x - hi.astype(F32)
    mid = r1.astype(BF16)
    lo = (r1 - mid.astype(F32)).astype(BF16)
    return hi, mid, lo


def _rms(xv):
    rstd = lax.rsqrt(jnp.mean(xv * xv, axis=-1, keepdims=True) + EPS)
    return xv * rstd, rstd


def _sigmoid(x):
    return 1.0 / (1.0 + jnp.exp(-x))


def _mm(a, b, *, name, tm, tn, nt=False, ta=False, out_dtype=F32, tie=None):
    k, m = a.shape if ta else a.shape[::-1]
    n = b.shape[0] if nt else b.shape[1]
    assert m % tm == 0 and n % tn == 0 and not (nt and ta), (name, a.shape, b.shape, tm, tn)

    def body(a_ref, b_ref, *rest):
        av = a_ref[...].astype(BF16)
        bv = b_ref[...].astype(BF16)
        rest[-1][...] = (_dot_nt(av, bv) if nt else _dot_tn(av, bv) if ta else _dot(av, bv)).astype(out_dtype)

    in_specs = [
        pl.BlockSpec((k, tm), lambda i, j: (0, i)) if ta else pl.BlockSpec((tm, k), lambda i, j: (i, 0)),
        pl.BlockSpec((tn, k), lambda i, j: (j, 0)) if nt else pl.BlockSpec((k, tn), lambda i, j: (0, j)),
    ]
    operands = [a, b]
    if tie is not None:
        in_specs.append(pl.BlockSpec(memory_space=pl.ANY))
        operands.append(tie)
    return pl.pallas_call(
        body,
        name=name,
        grid=(m // tm, n // tn),
        in_specs=in_specs,
        out_specs=pl.BlockSpec((tm, tn), lambda i, j: (i, j)),
        out_shape=jax.ShapeDtypeStruct((m, n), out_dtype),
        compiler_params=_params("parallel", "parallel"),
    )(*operands)


def _ffn_fwd(x, gain, wgt, wut, wd, *, name, tm=1024, tf=256):
    t = x.shape[0]
    nj = D_FF // tf

    def body(x_ref, g_ref, wg_ref, wu_ref, wd_prev_ref, wd_last_ref, xo_ref, a_ref, b_ref, h_ref, st_ref, acc, s_prev):
        j = pl.program_id(1)

        @pl.when(j == 0)
        def _():
            xhat, _ = _rms(x_ref[...])
            h_ref[...] = (xhat * g_ref[...]).astype(BF16)
            acc[...] = jnp.zeros_like(acc)
            s_prev[...] = jnp.zeros_like(s_prev)

        acc[...] += _dot(s_prev[...], wd_prev_ref[...])
        h = h_ref[...]
        a = _dot_nt(h, wg_ref[...])
        b = _dot_nt(h, wu_ref[...])
        a_ref[...] = a.astype(BF16)
        b_ref[...] = b.astype(BF16)
        s = (a * _sigmoid(a) * b).astype(BF16)
        st_ref[...] = s
        s_prev[...] = s

        @pl.when(j == nj - 1)
        def _():
            xo_ref[...] = x_ref[...] + 0.5 * (acc[...] + _dot(s, wd_last_ref[...]))

    return pl.pallas_call(
        body,
        name=name,
        grid=(t // tm, nj),
        in_specs=[
            pl.BlockSpec((tm, D_MODEL), lambda i, j: (i, 0)),
            pl.BlockSpec((1, D_MODEL), lambda i, j: (0, 0)),
            pl.BlockSpec((tf, D_MODEL), lambda i, j: (j, 0)),
            pl.BlockSpec((tf, D_MODEL), lambda i, j: (j, 0)),
            pl.BlockSpec((tf, D_MODEL), lambda i, j: (jnp.maximum(j - 1, 0), 0)),
            pl.BlockSpec((tf, D_MODEL), lambda i, j: (nj - 1, 0)),
        ],
        out_specs=[
            pl.BlockSpec((tm, D_MODEL), lambda i, j: (i, 0)),
            pl.BlockSpec((tm, tf), lambda i, j: (i, j)),
            pl.BlockSpec((tm, tf), lambda i, j: (i, j)),
            pl.BlockSpec((tm, D_MODEL), lambda i, j: (i, 0)),
            pl.BlockSpec((tm, tf), lambda i, j: (i, j)),
        ],
        out_shape=[
            jax.ShapeDtypeStruct((t, D_MODEL), F32),
            jax.ShapeDtypeStruct((t, D_FF), BF16),
            jax.ShapeDtypeStruct((t, D_FF), BF16),
            jax.ShapeDtypeStruct((t, D_MODEL), BF16),
            jax.ShapeDtypeStruct((t, D_FF), BF16),
        ],
        scratch_shapes=[pltpu.VMEM((tm, D_MODEL), F32), pltpu.VMEM((tm, tf), BF16)],
        compiler_params=_params("parallel", "arbitrary"),
    )(x, gain, wgt, wut, wd, wd)


def _ffn_bwd(dout, x, gain, a, b, wgt, wut, wd, *, name, tm=1024, tf=256):
    t = x.shape[0]
    nj = D_FF // tf

    def body(do_ref, x_ref, g_ref, a_ref, b_ref, wg_prev_ref, wu_prev_ref, wg_last_ref, wu_last_ref, wd_ref,
             dx_ref, dg_ref, da_ref, db_ref, dob_ref, dob_scr, dh, da_prev, db_prev):
        i = pl.program_id(0)
        j = pl.program_id(1)

        @pl.when(j == 0)
        def _():
            d = (0.5 * do_ref[...]).astype(BF16)
            dob_scr[...] = d
            dob_ref[...] = d
            dh[...] = jnp.zeros_like(dh)
            da_prev[...] = jnp.zeros_like(da_prev)
            db_prev[...] = jnp.zeros_like(db_prev)

        dh[...] += _dot(da_prev[...], wg_prev_ref[...]) + _dot(db_prev[...], wu_prev_ref[...])
        ds = _dot_nt(dob_scr[...], wd_ref[...])
        av = a_ref[...].astype(F32)
        bv = b_ref[...].astype(F32)
        sig = _sigmoid(av)
        dbv = (ds * (av * sig)).astype(BF16)
        dav = (ds * bv * (sig * (1.0 + av * (1.0 - sig)))).astype(BF16)
        da_ref[...] = dav
        db_ref[...] = dbv
        da_prev[...] = dav
        db_prev[...] = dbv

        @pl.when(j == nj - 1)
        def _():
            xhat, rstd = _rms(x_ref[...])
            dhv = dh[...] + _dot(dav, wg_last_ref[...]) + _dot(dbv, wu_last_ref[...])
            part = jnp.sum(dhv * xhat, axis=0, keepdims=True)

            @pl.when(i == 0)
            def _():
                dg_ref[...] = part

            @pl.when(i > 0)
            def _():
                dg_ref[...] += part

            dxh = dhv * g_ref[...]
            dx_ref[...] = do_ref[...] + rstd * (dxh - xhat * jnp.mean(dxh * xhat, axis=-1, keepdims=True))

    return pl.pallas_call(
        body,
        name=name,
        grid=(t // tm, nj),
        in_specs=[
            pl.BlockSpec((tm, D_MODEL), lambda i, j: (i, 0)),
            pl.BlockSpec((tm, D_MODEL), lambda i, j: (i, 0)),
            pl.BlockSpec((1, D_MODEL), lambda i, j: (0, 0)),
            pl.BlockSpec((tm, tf), lambda i, j: (i, j)),
            pl.BlockSpec((tm, tf), lambda i, j: (i, j)),
            pl.BlockSpec((tf, D_MODEL), lambda i, j: (jnp.maximum(j - 1, 0), 0)),
            pl.BlockSpec((tf, D_MODEL), lambda i, j: (jnp.maximum(j - 1, 0), 0)),
            pl.BlockSpec((tf, D_MODEL), lambda i, j: (nj - 1, 0)),
            pl.BlockSpec((tf, D_MODEL), lambda i, j: (nj - 1, 0)),
            pl.BlockSpec((tf, D_MODEL), lambda i, j: (j, 0)),
        ],
        out_specs=[
            pl.BlockSpec((tm, D_MODEL), lambda i, j: (i, 0)),
            pl.BlockSpec((1, D_MODEL), lambda i, j: (0, 0)),
            pl.BlockSpec((tm, tf), lambda i, j: (i, j)),
            pl.BlockSpec((tm, tf), lambda i, j: (i, j)),
            pl.BlockSpec((tm, D_MODEL), lambda i, j: (i, 0)),
        ],
        out_shape=[
            jax.ShapeDtypeStruct((t, D_MODEL), F32),
            jax.ShapeDtypeStruct((1, D_MODEL), F32),
            jax.ShapeDtypeStruct((t, D_FF), BF16),
            jax.ShapeDtypeStruct((t, D_FF), BF16),
            jax.ShapeDtypeStruct((t, D_MODEL), BF16),
        ],
        scratch_shapes=[pltpu.VMEM((tm, D_MODEL), BF16), pltpu.VMEM((tm, D_MODEL), F32), pltpu.VMEM((tm, tf), BF16),
                        pltpu.VMEM((tm, tf), BF16)],
        compiler_params=_params("arbitrary", "arbitrary", vmem_limit_bytes=FFN_BWD_VMEM_LIMIT_BYTES),
    )(dout, x, gain, a, b, wgt, wut, wgt, wut, wd)


def _norm_fwd(x, gain, *, name, tm=512):
    t = x.shape[0]

    def body(x_ref, g_ref, h_ref):
        xhat, _ = _rms(x_ref[...])
        h_ref[...] = (xhat * g_ref[...]).astype(BF16)

    return pl.pallas_call(
        body,
        name=name,
        grid=(t // tm,),
        in_specs=[pl.BlockSpec((tm, D_MODEL), lambda i: (i, 0)), pl.BlockSpec((1, D_MODEL), lambda i: (0, 0))],
        out_specs=pl.BlockSpec((tm, D_MODEL), lambda i: (i, 0)),
        out_shape=jax.ShapeDtypeStruct((t, D_MODEL), BF16),
        compiler_params=_params("parallel"),
    )(x, gain)


def _norm_bwd(dh, x, gain, dres, *, name, tm=512):
    t = x.shape[0]

    def body(dh_ref, x_ref, g_ref, dr_ref, dx_ref, dg_ref):
        i = pl.program_id(0)
        xhat, rstd = _rms(x_ref[...])
        dhv = dh_ref[...]
        part = jnp.sum(dhv * xhat, axis=0, keepdims=True)

        @pl.when(i == 0)
        def _():
            dg_ref[...] = part

        @pl.when(i > 0)
        def _():
            dg_ref[...] += part

        dxh = dhv * g_ref[...]
        dx_ref[...] = dr_ref[...] + rstd * (dxh - xhat * jnp.mean(dxh * xhat, axis=-1, keepdims=True))

    row = pl.BlockSpec((tm, D_MODEL), lambda i: (i, 0))
    vec = pl.BlockSpec((1, D_MODEL), lambda i: (0, 0))
    return pl.pallas_call(
        body,
        name=name,
        grid=(t // tm,),
        in_specs=[row, row, vec, row],
        out_specs=[row, vec],
        out_shape=[jax.ShapeDtypeStruct((t, D_MODEL), F32), jax.ShapeDtypeStruct((1, D_MODEL), F32)],
        compiler_params=_params("arbitrary"),
    )(dh, x, gain, dres)


ATT_Q_TILE = 512
ATT_K_BLOCK = 256


def _first_head_lanes():
    return lax.broadcasted_iota(jnp.int32, (1, LANES), 1) < SB_HEAD_DIM


def _stack_heads(x):
    first = _first_head_lanes()
    return jnp.concatenate([jnp.where(first, x, 0.0), jnp.where(first, 0.0, x)], axis=0)


def _unstack_heads(x, rows):
    return jnp.where(_first_head_lanes(), x[:rows], x[rows:])


def _tri(n, relation):
    r = lax.broadcasted_iota(jnp.int32, (n, n), 0)
    c = lax.broadcasted_iota(jnp.int32, (n, n), 1)
    return relation(r, c).astype(BF16)


def _scan_dot(x, tri):
    hi = x.astype(BF16)
    lo = (x - hi.astype(F32)).astype(BF16)
    return _dot(jnp.concatenate([hi, lo], axis=1), jnp.concatenate([tri, tri], axis=0))


def _log_terms(z):
    lbeta = jnp.minimum(z, 0.0) - jnp.log(1.0 + jnp.exp(-jnp.abs(z)))
    return lbeta, lbeta - z


def _attn_fwd(proj, *, name):
    t = proj.shape[0]
    tq, tk = ATT_Q_TILE, ATT_K_BLOCK
    diag = tq // tk
    n_pairs = SB_WIDTH // LANES

    def body(q_ref, k_ref, v_ref, o_ref, kept_ref):
        qi = pl.program_id(1)
        q = q_ref[...] * (SB_HEAD_DIM ** -0.5)
        qs = _stack_heads(q).astype(BF16)
        tri = _tri(tk, lambda j, s: j > s)
        trow = lax.broadcasted_iota(jnp.int32, (tq, tk), 0)
        scol = lax.broadcasted_iota(jnp.int32, (tq, tk), 1)

        def block(kb, carry, causal):
            acc, c = carry
            off = pl.multiple_of(kb * tk, tk)
            lbeta, lrest = _log_terms(_dot_nt(qs, k_ref[pl.ds(off, tk), :].astype(BF16)))
            if causal is not None:
                lrest = jnp.where(causal, lrest, 0.0)
            w = jnp.exp(lbeta + (_scan_dot(lrest, tri) + c))
            if causal is not None:
                w = jnp.where(causal, w, 0.0)
            wb = w.astype(BF16)
            kept_ref[0, 0, kb] = wb
            acc = acc + _dot(wb, v_ref[pl.ds(off, tk), :].astype(BF16))
            return acc, c + jnp.sum(lrest, axis=1, keepdims=True)

        carry = (jnp.zeros((2 * tq, LANES), F32), jnp.zeros((2 * tq, 1), F32))
        n_full = qi * diag
        for j in reversed(range(diag)):
            mask = (scol + j * tk) < trow
            carry = block(n_full + j, carry, jnp.concatenate([mask, mask], axis=0))

        def step(it, carry):
            return block(n_full - 1 - it, carry, None)

        acc, _ = lax.fori_loop(0, n_full, step, carry)
        o_ref[...] = _unstack_heads(acc, tq)

    return pl.pallas_call(
        body,
        name=name,
        grid=(n_pairs, t // tq),
        in_specs=[
            pl.BlockSpec((tq, LANES), lambda p, i: (i, p)),
            pl.BlockSpec((t, LANES), lambda p, i: (0, n_pairs + p)),
            pl.BlockSpec((t, LANES), lambda p, i: (0, 2 * n_pairs + p)),
        ],
        out_specs=[pl.BlockSpec((tq, LANES), lambda p, i: (i, p)),
                   pl.BlockSpec((1, 1, t // tk, 2 * tq, tk), lambda p, i: (p, i, 0, 0, 0))],
        out_shape=[jax.ShapeDtypeStruct((t, SB_WIDTH), F32),
                   jax.ShapeDtypeStruct((n_pairs, t // tq, t // tk, 2 * tq, tk), BF16)],
        compiler_params=_params("parallel", "parallel"),
    )(proj, proj, proj)


def _attn_bwd(proj, kept, do, *, name, tie=None):
    t = proj.shape[0]
    tq, tk = ATT_Q_TILE, ATT_K_BLOCK
    diag = tq // tk
    n_pairs = SB_WIDTH // LANES
    scale = SB_HEAD_DIM ** -0.5

    def body(q_ref, k_ref, v_ref, kept_ref, do_ref, *rest):
        dq_ref, dk_ref, dv_ref = rest[-3:]
        qi = pl.program_id(1)

        @pl.when(qi == 0)
        def _():
            dk_ref[...] = jnp.zeros_like(dk_ref)
            dv_ref[...] = jnp.zeros_like(dv_ref)

        qs = _stack_heads(q_ref[...] * scale).astype(BF16)
        dos = _stack_heads(do_ref[...]).astype(BF16)
        before = _tri(tk, lambda s, j: s < j)
        trow = lax.broadcasted_iota(jnp.int32, (tq, tk), 0)
        scol = lax.broadcasted_iota(jnp.int32, (tq, tk), 1)

        def block(kb, carry, causal):
            dq, cg = carry
            off = pl.multiple_of(kb * tk, tk)
            wb = kept_ref[0, 0, kb]
            kblk = k_ref[pl.ds(off, tk), :].astype(BF16)
            sig = 0.5 + 0.5 * jnp.tanh(0.5 * _dot_nt(qs, kblk))
            g = wb.astype(F32) * _dot_nt(dos, v_ref[pl.ds(off, tk), :].astype(BF16))
            prior = _scan_dot(g, before) + cg
            dz = g - sig * (g + prior)
            if causal is not None:
                dz = jnp.where(causal, dz, 0.0)
            dzb = dz.astype(BF16)
            dq = dq + _dot(dzb, kblk)
            dk_ref[pl.ds(off, tk), :] += _dot_tn(dzb, qs)
            dv_ref[pl.ds(off, tk), :] += _dot_tn(wb, dos)
            return dq, cg + jnp.sum(g, axis=1, keepdims=True)

        n_full = qi * diag
        carry = lax.fori_loop(0, n_full, lambda kb, carry: block(kb, carry, None),
                              (jnp.zeros((2 * tq, LANES), F32), jnp.zeros((2 * tq, 1), F32)))
        for j in range(diag):
            mask = (scol + j * tk) < trow
            carry = block(n_full + j, carry, jnp.concatenate([mask, mask], axis=0))
        dq_ref[...] = _unstack_heads(carry[0], tq) * scale

    tile_spec = pl.BlockSpec((tq, LANES), lambda p, i: (i, p))
    full_spec = pl.BlockSpec((t, LANES), lambda p, i: (0, p))
    return pl.pallas_call(
        body,
        name=name,
        grid=(n_pairs, t // tq),
        in_specs=[
            tile_spec,
            pl.BlockSpec((t, LANES), lambda p, i: (0, n_pairs + p)),
            pl.BlockSpec((t, LANES), lambda p, i: (0, 2 * n_pairs + p)),
            pl.BlockSpec((1, 1, t // tk, 2 * tq, tk), lambda p, i: (p, i, 0, 0, 0)),
            tile_spec,
        ] + ([] if tie is None else [pl.BlockSpec(memory_space=pl.ANY)]),
        out_specs=[tile_spec, full_spec, full_spec],
        out_shape=[jax.ShapeDtypeStruct((t, SB_WIDTH), F32)] * 3,
        compiler_params=_params("arbitrary", "arbitrary"),
    )(proj, proj, proj, kept, do, *([] if tie is None else [tie]))


HG_BLOCK = 256
HG_HEADS = HG_WIDTH // HG_HEAD_DIM


def _chunk_mats(n):
    r = lax.broadcasted_iota(jnp.int32, (n, n), 0)
    c = lax.broadcasted_iota(jnp.int32, (n, n), 1)
    same = (r // HG_CHUNK) == (c // HG_CHUNK)
    upto = (same & (c <= r)).astype(BF16)
    whole = same.astype(BF16)
    onward = (same & (c >= r)).astype(BF16)
    return upto, whole, onward


def _rows_dot(mat, x):
    return _dot(jnp.concatenate([mat, mat, mat], axis=1), jnp.concatenate(_split3(x), axis=0))


def _split_heads(x):
    return jnp.stack([x[:, h * HG_HEAD_DIM:(h + 1) * HG_HEAD_DIM] for h in range(HG_HEADS)], axis=0)


def _merge_heads(x):
    return jnp.concatenate([x[h] for h in range(HG_HEADS)], axis=1)


def _lower_bound(lg_ref):
    lg = lg_ref[...]
    return _sigmoid(lg[0:1, :] - lg[1:2, :])


def _hgrn_prepare(q_ref, f_ref, lb, h, upto, whole):
    cols = slice(h * HG_HEAD_DIM, (h + 1) * HG_HEAD_DIM)
    lbh = lb[:, cols]
    sg = _sigmoid(f_ref[:, cols])
    forget = lbh + (1.0 - lbh) * sg
    logf = jnp.log(forget)
    kk = (1.0 - lbh) * (1.0 - sg)
    qv = q_ref[:, cols]
    qsig = _sigmoid(qv)
    qh = qv * qsig
    b = _rows_dot(upto, logf)
    blast = _rows_dot(whole, logf)
    return dict(lbh=lbh, sg=sg, forget=forget, kk=kk, qv=qv, qsig=qsig, qh=qh, b=b, eb=jnp.exp(b),
                ekb=jnp.exp(blast - b), dl=jnp.exp(blast))


def _hgrn_fwd(proj, logits, *, name):
    t = proj.shape[0]
    tb = HG_BLOCK
    nc = tb // HG_CHUNK
    hd = HG_HEAD_DIM

    def body(q_ref, f_ref, i_ref, lg_ref, o_ref, st_ref, state, qh_s, kk_s, b_s, qe_s, ke_s, dl_s):
        @pl.when(pl.program_id(0) == 0)
        def _():
            state[...] = jnp.zeros_like(state)

        lb = _lower_bound(lg_ref)
        upto, whole, _ = _chunk_mats(tb)
        for h in range(HG_HEADS):
            p = _hgrn_prepare(q_ref, f_ref, lb, h, upto, whole)
            qh_s[h] = p["qh"]
            kk_s[h] = p["kk"]
            b_s[h] = p["b"]
            qe_s[h] = (p["qh"] * p["eb"]).astype(BF16)
            ke_s[h] = (p["kk"] * p["ekb"]).astype(BF16)
            dl_s[h] = p["dl"]
        rowi = lax.broadcasted_iota(jnp.int32, (HG_HEADS, HG_CHUNK, hd), 1)

        def chunk(c, _):
            r0 = pl.multiple_of(c * HG_CHUNK, HG_CHUNK)
            rows = pl.ds(r0, HG_CHUNK)
            bc = b_s[:, rows, :]
            qc = qh_s[:, rows, :]
            kc = kk_s[:, rows, :]
            vc = _split_heads(i_ref[rows, :])
            s_in = state[...]
            st_ref[c] = s_in
            s_in_b = s_in.astype(BF16)
            qe = qe_s[:, rows, :]
            o = jnp.stack([_dot_nt(qe[h], s_in_b[h]) for h in range(HG_HEADS)], axis=0)
            for s in range(HG_CHUNK):
                pair = jnp.where(rowi >= s, qc * jnp.exp(bc - bc[:, s:s + 1, :]) * kc[:, s:s + 1, :], 0.0)
                o = o + jnp.sum(pair, axis=2, keepdims=True) * vc[:, s:s + 1, :]
            o_ref[rows, :] = _merge_heads(o)
            vcb = vc.astype(BF16)
            ke = ke_s[:, rows, :]
            update = jnp.stack([_dot_tn(vcb[h], ke[h]) for h in range(HG_HEADS)], axis=0)
            state[...] = s_in * dl_s[:, pl.ds(r0, 1), :] + update
            return 0

        lax.fori_loop(0, nc, chunk, 0)

    blk = lambda col: pl.BlockSpec((tb, HG_WIDTH), lambda i: (i, col))
    head_f32 = pltpu.VMEM((HG_HEADS, tb, hd), F32)
    head_bf16 = pltpu.VMEM((HG_HEADS, tb, hd), BF16)
    return pl.pallas_call(
        body,
        name=name,
        grid=(t // tb,),
        in_specs=[blk(3), blk(4), blk(5), pl.BlockSpec((2, HG_WIDTH), lambda i: (0, 0))],
        out_specs=[
            pl.BlockSpec((tb, HG_WIDTH), lambda i: (i, 0)),
            pl.BlockSpec((nc, HG_HEADS, hd, hd), lambda i: (i, 0, 0, 0)),
        ],
        out_shape=[
            jax.ShapeDtypeStruct((t, HG_WIDTH), F32),
            jax.ShapeDtypeStruct((t // HG_CHUNK, HG_HEADS, hd, hd), F32),
        ],
        scratch_shapes=[pltpu.VMEM((HG_HEADS, hd, hd), F32), head_f32, head_f32, head_f32, head_bf16, head_bf16,
                        head_f32],
        compiler_params=_params("arbitrary"),
    )(proj, proj, proj, logits)


def _hgrn_bwd(proj, logits, states, do, *, name):
    t = proj.shape[0]
    tb = HG_BLOCK
    nb = t // tb
    nc = tb // HG_CHUNK
    hd = HG_HEAD_DIM

    def body(q_ref, f_ref, i_ref, lg_ref, st_ref, do_ref, dq_ref, df_ref, di_ref, dlb_ref,
             dstate, qh_s, kk_s, b_s, eb_s, ekb_s, qe_s, ke_s, dl_s, dqh_s, dkk_s, dlf_s):
        step = pl.program_id(0)

        @pl.when(step == 0)
        def _():
            dstate[...] = jnp.zeros_like(dstate)
            dlb_ref[...] = jnp.zeros_like(dlb_ref)

        lb = _lower_bound(lg_ref)
        upto, whole, _ = _chunk_mats(tb)
        prepared = []
        for h in range(HG_HEADS):
            p = _hgrn_prepare(q_ref, f_ref, lb, h, upto, whole)
            prepared.append(p)
            qh_s[h] = p["qh"]
            kk_s[h] = p["kk"]
            b_s[h] = p["b"]
            eb_s[h] = p["eb"]
            ekb_s[h] = p["ekb"]
            qe_s[h] = (p["qh"] * p["eb"]).astype(BF16)
            ke_s[h] = (p["kk"] * p["ekb"]).astype(BF16)
            dl_s[h] = p["dl"]
        rowi = lax.broadcasted_iota(jnp.int32, (HG_CHUNK, hd), 0)
        r16 = lax.broadcasted_iota(jnp.int32, (HG_CHUNK, HG_CHUNK), 0)
        c16 = lax.broadcasted_iota(jnp.int32, (HG_CHUNK, HG_CHUNK), 1)
        onward = (c16 >= r16).astype(BF16)

        def chunk(it, _):
            c = nc - 1 - it
            r0 = pl.multiple_of(c * HG_CHUNK, HG_CHUNK)
            rows = pl.ds(r0, HG_CHUNK)
            for h in range(HG_HEADS):
                cols = slice(h * hd, (h + 1) * hd)
                bc = b_s[h, rows, :]
                qc = qh_s[h, rows, :]
                kc = kk_s[h, rows, :]
                vc = i_ref[rows, cols]
                doc = do_ref[rows, cols]
                s_in = st_ref[c, h]
                ds_out = dstate[h]
                ds_out_b = ds_out.astype(BF16)
                docb = doc.astype(BF16)
                dl_row = dl_s[h, pl.ds(r0, 1), :]
                dqh = _dot(docb, s_in.astype(BF16)) * eb_s[h, rows, :]
                dkk = _dot(vc.astype(BF16), ds_out_b) * ekb_s[h, rows, :]
                dv = _dot_nt(ke_s[h, rows, :], ds_out_b)
                db = dqh * qc - dkk * kc
                dwhole = jnp.sum(dkk * kc, axis=0, keepdims=True) + jnp.sum(ds_out * s_in, axis=0, keepdims=True) * dl_row
                dk_rows, dv_rows = [], []
                for s in range(HG_CHUNK):
                    keep = rowi >= s
                    e = jnp.exp(bc - bc[s:s + 1, :])
                    k_row = kc[s:s + 1, :]
                    pcol = jnp.sum(jnp.where(keep, qc * e * k_row, 0.0), axis=1, keepdims=True)
                    dpcol = jnp.sum(doc * vc[s:s + 1, :], axis=1, keepdims=True)
                    m = jnp.where(keep, e * dpcol, 0.0)
                    y = m * qc
                    dqh = dqh + m * k_row
                    db = db + y * k_row
                    dk_rows.append(jnp.sum(y, axis=0, keepdims=True))
                    dv_rows.append(jnp.sum(pcol * doc, axis=0, keepdims=True))
                dkk_pairs = jnp.concatenate(dk_rows, axis=0)
                dkk = dkk + dkk_pairs
                db = db - dkk_pairs * kc
                dv = dv + jnp.concatenate(dv_rows, axis=0)
                dqh_s[h, rows, :] = dqh
                dkk_s[h, rows, :] = dkk
                dlf_s[h, rows, :] = _rows_dot(onward, db) + dwhole
                di_ref[rows, cols] = dv
                dstate[h] = ds_out * dl_row + _dot_tn(docb, qe_s[h, rows, :])
            return 0

        lax.fori_loop(0, nc, chunk, 0)
        for h in range(HG_HEADS):
            cols = slice(h * hd, (h + 1) * hd)
            p = prepared[h]
            dq_ref[:, cols] = dqh_s[h] * (p["qsig"] * (1.0 + p["qv"] * (1.0 - p["qsig"])))
            dforget = dlf_s[h] / p["forget"] - dkk_s[h]
            df_ref[:, cols] = dforget * (1.0 - p["lbh"]) * p["sg"] * (1.0 - p["sg"])
            dlb_ref[:, cols] += jnp.sum(dforget * (1.0 - p["sg"]), axis=0, keepdims=True)

    blk = lambda col: pl.BlockSpec((tb, HG_WIDTH), lambda i: (nb - 1 - i, col))
    vec = pl.BlockSpec((1, HG_WIDTH), lambda i: (0, 0))
    head_f32 = pltpu.VMEM((HG_HEADS, tb, hd), F32)
    head_bf16 = pltpu.VMEM((HG_HEADS, tb, hd), BF16)
    return pl.pallas_call(
        body,
        name=name,
        grid=(nb,),
        in_specs=[
            blk(3), blk(4), blk(5),
            pl.BlockSpec((2, HG_WIDTH), lambda i: (0, 0)),
            pl.BlockSpec((nc, HG_HEADS, hd, hd), lambda i: (nb - 1 - i, 0, 0, 0)),
            blk(0),
        ],
        out_specs=[blk(0), blk(0), blk(0), vec],
        out_shape=[jax.ShapeDtypeStruct((t, HG_WIDTH), F32)] * 3 + [jax.ShapeDtypeStruct((1, HG_WIDTH), F32)],
        scratch_shapes=[
            pltpu.VMEM((HG_HEADS, hd, hd), F32),
            head_f32, head_f32, head_f32, head_f32, head_f32, head_bf16, head_bf16, head_f32,
            head_f32, head_f32, head_f32,
        ],
        compiler_params=_params("arbitrary"),
    )(proj, proj, proj, logits, states, do)


def _group_mat(width, head_dim):
    r = lax.broadcasted_iota(jnp.int32, (width, width), 0)
    c = lax.broadcasted_iota(jnp.int32, (width, width), 1)
    return ((r // head_dim) == (c // head_dim)).astype(BF16)


def _head_mean(x, mat, head_dim):
    hi = x.astype(BF16)
    lo = (x - hi.astype(F32)).astype(BF16)
    return (_dot(hi, mat) + _dot(lo, mat)) * (1.0 / head_dim)


def _mix_out_fwd(o_sb, o_hg, proj, g_sb, g_hg, w_out, x1, *, name, tm=256):
    t = x1.shape[0]

    def body(osb_ref, ohg_ref, gate_ref, gsb_ref, ghg_ref, w_ref, x_ref, xo_ref, mt_ref):
        msb = _group_mat(SB_WIDTH, SB_HEAD_DIM)
        mhg = _group_mat(HG_WIDTH, HG_HEAD_DIM)
        osb = osb_ref[...]
        ohg = ohg_ref[...]
        nsb = osb * lax.rsqrt(_head_mean(osb * osb, msb, SB_HEAD_DIM) + EPS) * gsb_ref[...]
        gate = gate_ref[...]
        nhg = ohg * lax.rsqrt(_head_mean(ohg * ohg, mhg, HG_HEAD_DIM) + EPS) * ghg_ref[...] * (gate * _sigmoid(gate))
        mixed = jnp.concatenate([nsb, nhg], axis=1).astype(BF16)
        mt_ref[...] = mixed
        xo_ref[...] = x_ref[...] + _dot(mixed, w_ref[...])

    half = pl.BlockSpec((tm, SB_WIDTH), lambda i: (i, 0))
    vec = pl.BlockSpec((1, SB_WIDTH), lambda i: (0, 0))
    row = pl.BlockSpec((tm, D_MODEL), lambda i: (i, 0))
    return pl.pallas_call(
        body,
        name=name,
        grid=(t // tm,),
        in_specs=[half, half, pl.BlockSpec((tm, HG_WIDTH), lambda i: (i, 6)), vec, vec,
                  pl.BlockSpec((D_MODEL, D_MODEL), lambda i: (0, 0)), row],
        out_specs=[row, row],
        out_shape=[jax.ShapeDtypeStruct((t, D_MODEL), F32), jax.ShapeDtypeStruct((t, D_MODEL), BF16)],
        compiler_params=_params("parallel"),
    )(o_sb, o_hg, proj, g_sb, g_hg, w_out, x1)


def _mix_out_bwd(dx2, o_sb, o_hg, proj, g_sb, g_hg, w_out, *, name, tm=256):
    t = dx2.shape[0]

    def body(dx_ref, osb_ref, ohg_ref, gate_ref, gsb_ref, ghg_ref, w_ref, dosb_ref, dohg_ref, dgate_ref, dgsb_ref,
             dghg_ref, dxb_ref):
        i = pl.program_id(0)
        msb = _group_mat(SB_WIDTH, SB_HEAD_DIM)
        mhg = _group_mat(HG_WIDTH, HG_HEAD_DIM)
        dxb = dx_ref[...].astype(BF16)
        dxb_ref[...] = dxb
        dmixed = _dot_nt(dxb, w_ref[...])
        dnsb = dmixed[:, :SB_WIDTH]
        dy = dmixed[:, SB_WIDTH:]

        osb = osb_ref[...]
        rstd = lax.rsqrt(_head_mean(osb * osb, msb, SB_HEAD_DIM) + EPS)
        ohat = osb * rstd
        part_sb = jnp.sum(dnsb * ohat, axis=0, keepdims=True)
        dohat = dnsb * gsb_ref[...]
        dosb_ref[...] = rstd * (dohat - ohat * _head_mean(dohat * ohat, msb, SB_HEAD_DIM))

        ohg = ohg_ref[...]
        rstd = lax.rsqrt(_head_mean(ohg * ohg, mhg, HG_HEAD_DIM) + EPS)
        ohat = ohg * rstd
        gate = gate_ref[...]
        sig = _sigmoid(gate)
        dn = dy * (gate * sig)
        dgate_ref[...] = dy * (ohat * ghg_ref[...]) * (sig * (1.0 + gate * (1.0 - sig)))
        part_hg = jnp.sum(dn * ohat, axis=0, keepdims=True)
        dohat = dn * ghg_ref[...]
        dohg_ref[...] = rstd * (dohat - ohat * _head_mean(dohat * ohat, mhg, HG_HEAD_DIM))

        @pl.when(i == 0)
        def _():
            dgsb_ref[...] = part_sb
            dghg_ref[...] = part_hg

        @pl.when(i > 0)
        def _():
            dgsb_ref[...] += part_sb
            dghg_ref[...] += part_hg

    half = pl.BlockSpec((tm, SB_WIDTH), lambda i: (i, 0))
    vec = pl.BlockSpec((1, SB_WIDTH), lambda i: (0, 0))
    row = pl.BlockSpec((tm, D_MODEL), lambda i: (i, 0))
    return pl.pallas_call(
        body,
        name=name,
        grid=(t // tm,),
        in_specs=[row, half, half, pl.BlockSpec((tm, HG_WIDTH), lambda i: (i, 6)), vec, vec,
                  pl.BlockSpec((D_MODEL, D_MODEL), lambda i: (0, 0))],
        out_specs=[half, half, half, vec, vec, row],
        out_shape=[jax.ShapeDtypeStruct((t, SB_WIDTH), F32)] * 3 + [jax.ShapeDtypeStruct((1, SB_WIDTH), F32)] * 2
        + [jax.ShapeDtypeStruct((t, D_MODEL), BF16)],
        compiler_params=_params("arbitrary"),
    )(dx2, o_sb, o_hg, proj, g_sb, g_hg, w_out)


def _loss_head(x3, gain, target, *, name, tm=512):
    t = x3.shape[0]

    def body(x_ref, g_ref, y_ref, dx_ref, dg_ref, loss_ref):
        i = pl.program_id(0)
        xhat, rstd = _rms(x_ref[...])
        err = xhat * g_ref[...] - y_ref[...]
        part_loss = 0.5 * jnp.sum(jnp.mean(err * err, axis=-1, keepdims=True), axis=0, keepdims=True)
        dy = err * (1.0 / D_MODEL)
        part_g = jnp.sum(dy * xhat, axis=0, keepdims=True)

        @pl.when(i == 0)
        def _():
            dg_ref[...] = part_g
            loss_ref[...] = jnp.broadcast_to(part_loss, loss_ref.shape)

        @pl.when(i > 0)
        def _():
            dg_ref[...] += part_g
            loss_ref[...] += jnp.broadcast_to(part_loss, loss_ref.shape)

        dxh = dy * g_ref[...]
        dx_ref[...] = rstd * (dxh - xhat * jnp.mean(dxh * xhat, axis=-1, keepdims=True))

    row = pl.BlockSpec((tm, D_MODEL), lambda i: (i, 0))
    vec = pl.BlockSpec((1, D_MODEL), lambda i: (0, 0))
    return pl.pallas_call(
        body,
        name=name,
        grid=(t // tm,),
        in_specs=[row, vec, row],
        out_specs=[row, vec, vec],
        out_shape=[jax.ShapeDtypeStruct((t, D_MODEL), F32), jax.ShapeDtypeStruct((1, D_MODEL), F32),
                   jax.ShapeDtypeStruct((1, D_MODEL), F32)],
        compiler_params=_params("arbitrary"),
    )(x3, gain, target)


def _local_step(x, target, norms, logits, w, weights_after=None, grads_ready=None):
    w = dict(w)
    x1, a1, b1, h1, s1 = _ffn_fwd(x, norms["ffn1"], w["g1t"], w["u1t"], w["d1"], name="ffn1_fwd")
    if weights_after is not None:
        w.update(weights_after("ffn1", x1))
    hm = _norm_fwd(x1, norms["mix"], name="mix_norm_fwd")
    proj = _mm(hm, w["int"], name="in_proj", tm=512, tn=IN_COLS, nt=True)
    o_sb, sb_kept = _attn_fwd(proj, name="sb_attn_fwd")
    o_hg, states = _hgrn_fwd(proj, logits, name="hgrn2_fwd")
    x2, mixed = _mix_out_fwd(o_sb, o_hg, proj, norms["sb"], norms["hg"], w["out"], x1, name="mix_out_fwd")
    if weights_after is not None:
        w.update(weights_after("mix", x2))
    x3, a2, b2, h2, s2 = _ffn_fwd(x2, norms["ffn2"], w["g2t"], w["u2t"], w["d2"], name="ffn2_fwd")
    dx3, d_final, loss_row = _loss_head(x3, norms["final"], target, name="loss_head")

    def weight_grad(lhs, rhs, name, tie=None):
        return _mm(lhs, rhs, name=name, tm=256, tn=D_MODEL, ta=True, out_dtype=BF16, tie=tie)

    def sent(stage):
        return grads_ready(stage, gw) if grads_ready is not None else None

    gw, gv = {}, {"final": d_final}
    dx2, gv["ffn2"], da2, db2, dob2 = _ffn_bwd(dx3, x2, norms["ffn2"], a2, b2, w["g2t"], w["u2t"], w["d2"],
                                               name="ffn2_bwd")
    gw["g2t"] = weight_grad(da2, h2, "ffn2_dgate")
    gw["u2t"] = weight_grad(db2, h2, "ffn2_dup")
    gw["d2"] = weight_grad(s2, dob2, "ffn2_ddown")

    do_sb, do_hg, d_gate, gv["sb"], gv["hg"], dx2b = _mix_out_bwd(
        dx2, o_sb, o_hg, proj, norms["sb"], norms["hg"], w["out"], name="mix_out_bwd")
    gw["out"] = weight_grad(mixed, dx2b, "out_dw")
    tie = sent("mix")
    dq_sb, dk_sb, dv_sb = _attn_bwd(proj, sb_kept, do_sb, name="sb_attn_bwd", tie=tie)
    dq_hg, df_hg, di_hg, d_lb = _hgrn_bwd(proj, logits if tie is None else logits + tie[0, 0], states, do_hg,
                                          name="hgrn2_bwd")
    dproj = jnp.concatenate([dq_sb, dk_sb, dv_sb, dq_hg, df_hg, di_hg, d_gate], axis=1).astype(BF16)
    gw["int"] = weight_grad(dproj, hm, "in_dw")
    tie = sent("in")
    dhm = _mm(dproj, w["int"], name="in_dx", tm=512, tn=D_MODEL)
    dx1, gv["mix"] = _norm_bwd(dhm, x1, norms["mix"] if tie is None else norms["mix"] + tie[0, 0], dx2,
                               name="mix_norm_bwd")

    dx, gv["ffn1"], da1, db1, dob1 = _ffn_bwd(dx1, x, norms["ffn1"], a1, b1, w["g1t"], w["u1t"], w["d1"],
                                              name="ffn1_bwd")
    gw["g1t"] = weight_grad(da1, h1, "ffn1_dgate")
    gw["u1t"] = weight_grad(db1, h1, "ffn1_dup", tie=sent("g1t"))
    gw["d1"] = weight_grad(s1, dob1, "ffn1_ddown", tie=sent("u1t"))
    sent("d1")
    gv["lb"] = d_lb
    return loss_row, dx, gw, gv


HBM = pl.BlockSpec(memory_space=pl.ANY)


def _place():
    return lax.axis_index("x"), lax.axis_index("y"), lax.axis_index("c")


def _slot(px, py, pc):
    return 4 * px + 2 * py + pc


def _all_gather(blocks, *, name):
    n = len(blocks)

    def body(*refs):
        ins, outs = refs[:n], refs[n:2 * n]
        send_sems, recv_sems, local_sems = refs[2 * n:]
        x, y, c = _place()
        me, sibling = (x, y, c), (x, y, 1 - c)
        chips = [(1 - x, y), (x, 1 - y), (1 - x, 1 - y)]

        def copy(a, k, block, to, src=None):
            dst = outs[a].at[_slot(*block)]
            return pltpu.make_async_remote_copy(
                src_ref=dst if src is None else src, dst_ref=dst, send_sem=send_sems.at[7 * a + k],
                recv_sem=recv_sems.at[7 * a + k], device_id=to, device_id_type=MESH)

        mine = [pltpu.make_async_copy(ins[a], outs[a].at[_slot(*me)], local_sems.at[a]) for a in range(n)]
        for cp in mine:
            cp.start()
        first = []
        for a in range(n):
            first.append(copy(a, 0, me, sibling, src=ins[a]))
            first += [copy(a, 1 + j, me, (*chip, c), src=ins[a]) for j, chip in enumerate(chips)]
        for cp in first:
            cp.start()
        passed = []
        for j, chip in enumerate(chips):
            for a in range(n):
                copy(a, 1 + j, (*chip, c), me).wait_recv()
                fwd = copy(a, 4 + j, (*chip, c), sibling)
                fwd.start()
                passed.append(fwd)
        for a in range(n):
            copy(a, 0, sibling, me).wait_recv()
            for j, chip in enumerate(chips):
                copy(a, 4 + j, (*chip, 1 - c), me).wait_recv()
        for cp in first + passed:
            cp.wait_send()
        for cp in mine:
            cp.wait()

    return pl.pallas_call(
        body,
        name=name,
        in_specs=[HBM] * n,
        out_specs=[HBM] * n,
        out_shape=[jax.ShapeDtypeStruct((N_DEV,) + b.shape, b.dtype) for b in blocks],
        scratch_shapes=[pltpu.SemaphoreType.DMA((7 * n,)), pltpu.SemaphoreType.DMA((7 * n,)),
                        pltpu.SemaphoreType.DMA((n,))],
    )(*blocks)


def _flipped(place, d):
    return tuple(1 - p if (d >> (2 - axis)) & 1 else p for axis, p in enumerate(place))


SEM = pl.BlockSpec(memory_space=pltpu.SEMAPHORE)
EFFECT = pltpu.SideEffectType.DATAFLOW_SIDE_EFFECTING


def _split_copies(me, srcs, lands, send_sems, recv_sems, by_owner):
    copies = []
    for d in range(1, N_DEV):
        peer = _flipped(me, d)
        for a, (src, land) in enumerate(zip(srcs, lands)):
            copies.append(pltpu.make_async_remote_copy(
                src_ref=src.at[_slot(*peer)] if by_owner else src, dst_ref=land.at[_slot(*me)],
                send_sem=send_sems.at[7 * a + d - 1], recv_sem=recv_sems.at[7 * a + d - 1], device_id=peer,
                device_id_type=MESH))
    return copies


def _copies_start(srcs, *, name, by_owner, after=None):
    n = len(srcs)
    extra = [] if after is None else [after]
    land_shapes = [s.shape if by_owner else (N_DEV,) + s.shape for s in srcs]
    lands = [pltpu.with_memory_space_constraint(lax.empty(shape, s.dtype), pltpu.HBM) for shape, s in zip(land_shapes, srcs)]
    srcs = [pltpu.with_memory_space_constraint(s, pltpu.HBM) for s in srcs]

    def body(*refs):
        src_refs, land_refs = refs[:n], refs[n:2 * n]
        send_sems, recv_sems = refs[2 * n + len(extra)], refs[2 * n + len(extra) + 1]
        token = refs[-1]
        for cp in _split_copies(_place(), src_refs, land_refs, send_sems, recv_sems, by_owner):
            cp.start()
        token[...] = jnp.zeros_like(token)

    out = pl.pallas_call(
        body,
        name=name,
        in_specs=[HBM] * (2 * n + len(extra)),
        out_specs=[SEM, SEM] + [HBM] * (2 * n) + [pl.BlockSpec(memory_space=pltpu.VMEM)],
        out_shape=[pltpu.SemaphoreType.DMA((7 * n,)), pltpu.SemaphoreType.DMA((7 * n,))]
        + [pltpu.HBM(s.shape, s.dtype) for s in srcs] + [pltpu.HBM(shape, s.dtype) for shape, s in zip(land_shapes, srcs)]
        + [jax.ShapeDtypeStruct((8, LANES), F32)],
        input_output_aliases={i: 2 + i for i in range(2 * n)},
        compiler_params=pltpu.CompilerParams(has_side_effects=EFFECT),
    )(*srcs, *lands, *extra)
    return (out[0], out[1], out[2:2 + n], out[2 + n:2 + 2 * n]), out[-1]


def _copies_wait(started, after, *, name, by_owner):
    send_sems, recv_sems, srcs, lands = started
    n = len(srcs)

    def body(*refs):
        src_refs, land_refs = refs[:n], refs[n:2 * n]
        for cp in _split_copies(_place(), src_refs, land_refs, refs[2 * n], refs[2 * n + 1], by_owner):
            cp.wait_send()
            cp.wait_recv()

    out = pl.pallas_call(
        body,
        name=name,
        in_specs=[HBM] * (2 * n) + [SEM, SEM, HBM],
        out_specs=[HBM] * (2 * n),
        out_shape=[pltpu.HBM(s.shape, s.dtype) for s in srcs] + [pltpu.HBM(s.shape, s.dtype) for s in lands],
        input_output_aliases={i: i for i in range(2 * n)},
        compiler_params=pltpu.CompilerParams(has_side_effects=EFFECT),
    )(*srcs, *lands, send_sems, recv_sems, after)
    return out[n:]


def _with_own(lands, own, slot):
    zero = jnp.zeros((), jnp.int32)
    return [lax.dynamic_update_slice(land, o[None], (slot.astype(jnp.int32),) + (zero,) * o.ndim)
            for land, o in zip(lands, own)]


def _adamw(w, g, m, v):
    m = ADAM_B1 * m + (1.0 - ADAM_B1) * g
    v = ADAM_B2 * v + (1.0 - ADAM_B2) * (g * g)
    m_hat = m / (1.0 - ADAM_B1 ** ADAM_STEP)
    v_hat = v / (1.0 - ADAM_B2 ** ADAM_STEP)
    delta = -ADAM_LR * (m_hat / (jnp.sqrt(v_hat) + ADAM_EPS) + ADAM_WD * w)
    return delta, m, v


def _sum_and_update(parts, w, m, v, *, name, tie=None):
    _, rows, cols = w.shape
    tr = rows // 2

    def body(p_ref, w_ref, m_ref, v_ref, *rest):
        g_ref, d_ref, mo_ref, vo_ref = rest[-4:]
        g = p_ref[0].astype(F32)
        for s in range(1, N_DEV):
            g = g + p_ref[s].astype(F32)
        g_ref[0] = g
        d_ref[0], mo_ref[0], vo_ref[0] = _adamw(w_ref[0], g, m_ref[0], v_ref[0])

    flat = pl.BlockSpec((1, tr, cols), lambda i: (0, i, 0))
    return pl.pallas_call(
        body,
        name=name,
        grid=(rows // tr,),
        in_specs=[pl.BlockSpec((N_DEV, tr, cols), lambda i: (0, i, 0)), flat, flat, flat]
        + ([] if tie is None else [pl.BlockSpec(memory_space=pl.ANY)]),
        out_specs=[flat] * 4,
        out_shape=[jax.ShapeDtypeStruct((1, rows, cols), F32)] * 4,
        compiler_params=_params("parallel"),
    )(parts, w, m, v, *([] if tie is None else [tie]))


VEC_ROWS = 8
ROW_LOGITS, ROW_LOSS = 5, 7


def _vectors_update(part, w, m, v, *, name):
    def body(p_ref, w_ref, m_ref, v_ref, g_ref, d_ref, mo_ref, vo_ref, loss_ref, all_ref, send_sems, recv_sems):
        me = _place()
        all_ref[_slot(*me)] = p_ref[...]
        copies = []
        for d in range(1, N_DEV):
            peer = _flipped(me, d)
            copies.append(pltpu.make_async_remote_copy(
                src_ref=p_ref, dst_ref=all_ref.at[_slot(*me)], send_sem=send_sems.at[d - 1], recv_sem=recv_sems.at[d - 1],
                device_id=peer, device_id_type=MESH))
        for cp in copies:
            cp.start()
        for cp in copies:
            cp.wait()
        total = all_ref[0]
        for s in range(1, N_DEV):
            total = total + all_ref[s]
        wv = w_ref[...]
        half = D_MODEL // 2
        lb = _sigmoid(wv[ROW_LOGITS:ROW_LOGITS + 1, :half] - wv[ROW_LOGITS:ROW_LOGITS + 1, half:])
        d_first = total[ROW_LOGITS:ROW_LOGITS + 1, :half] * lb * (1.0 - lb)
        d_logits = jnp.concatenate([d_first, -d_first], axis=1)
        rowi = lax.broadcasted_iota(jnp.int32, (VEC_ROWS, D_MODEL), 0)
        g = jnp.where(rowi == ROW_LOGITS, d_logits, jnp.where(rowi < ROW_LOGITS, total, 0.0))
        g_ref[...] = g
        d_ref[...], mo_ref[...], vo_ref[...] = _adamw(wv, g, m_ref[...], v_ref[...])
        loss_ref[...] = total[ROW_LOSS:ROW_LOSS + 1, :]

    vmem = pl.BlockSpec(memory_space=pltpu.VMEM)
    return pl.pallas_call(
        body,
        name=name,
        in_specs=[vmem] * 4,
        out_specs=[vmem] * 5,
        out_shape=[jax.ShapeDtypeStruct((VEC_ROWS, D_MODEL), F32)] * 4 + [jax.ShapeDtypeStruct((1, D_MODEL), F32)],
        scratch_shapes=[pltpu.VMEM((N_DEV, VEC_ROWS, D_MODEL), F32), pltpu.SemaphoreType.DMA((7,)),
                        pltpu.SemaphoreType.DMA((7,))],
    )(part, w, m, v)


TRANSPOSED = ("g1t", "u1t", "g2t", "u2t", "int")


def _vector_rows(rows):
    rowi = lax.broadcasted_iota(jnp.int32, (VEC_ROWS, D_MODEL), 0)
    out = jnp.zeros((VEC_ROWS, D_MODEL), F32)
    for i, r in enumerate(rows):
        if r is not None:
            out = jnp.where(rowi == i, r, out)
    return out


def kernel(x, ffn1_norm, ffn1_w_gate, ffn1_w_up, ffn1_w_down, mix_norm, w_in, sb_out_norm, hg_lower_bound_logits, hg_out_norm, w_out, ffn2_norm, ffn2_w_gate, ffn2_w_up, ffn2_w_down, final_norm, loss_target, m_ffn1_norm, m_ffn1_w_gate, m_ffn1_w_up, m_ffn1_w_down, m_mix_norm, m_w_in, m_sb_out_norm, m_hg_lower_bound_logits, m_hg_out_norm, m_w_out, m_ffn2_norm, m_ffn2_w_gate, m_ffn2_w_up, m_ffn2_w_down, m_final_norm, v_ffn1_norm, v_ffn1_w_gate, v_ffn1_w_up, v_ffn1_w_down, v_mix_norm, v_w_in, v_sb_out_norm, v_hg_lower_bound_logits, v_hg_out_norm, v_w_out, v_ffn2_norm, v_ffn2_w_gate, v_ffn2_w_up, v_ffn2_w_down, v_final_norm):
    def matrices(g1, u1, d1, win, wout, g2, u2, d2):
        return {"g1t": g1, "u1t": u1, "d1": d1, "int": win, "out": wout, "g2t": g2, "u2t": u2, "d2": d2}

    def vectors(n1, nm, nsb, lg, nhg, n2, nf):
        return [n1, nm, n2, nf.reshape(1, D_MODEL), jnp.concatenate([nsb, nhg], axis=1), lg.reshape(1, D_MODEL), None, None]

    w_sh = matrices(ffn1_w_gate, ffn1_w_up, ffn1_w_down, w_in, w_out, ffn2_w_gate, ffn2_w_up, ffn2_w_down)
    m_sh = matrices(m_ffn1_w_gate, m_ffn1_w_up, m_ffn1_w_down, m_w_in, m_w_out, m_ffn2_w_gate, m_ffn2_w_up, m_ffn2_w_down)
    v_sh = matrices(v_ffn1_w_gate, v_ffn1_w_up, v_ffn1_w_down, v_w_in, v_w_out, v_ffn2_w_gate, v_ffn2_w_up, v_ffn2_w_down)
    keys = list(w_sh)

    slot = _slot(*_place())

    def full(key, stack):
        return stack.reshape(-1, D_MODEL)

    def by_owner(key, grad):
        return grad.reshape(N_DEV, -1, D_MODEL)

    def view(key, a):
        return jnp.swapaxes(a, 1, 2) if key in TRANSPOSED else a

    blocks = {k: view(k, w_sh[k])[0].astype(BF16) for k in keys}
    first, mid, last = ("g1t", "u1t", "d1"), ("int", "out"), ("g2t", "u2t", "d2")
    w_first = {k: full(k, s) for k, s in zip(first, _all_gather([blocks[k] for k in first], name="gather_ffn1"))}
    flights = {}
    flights["ffn1"], token_mid = _copies_start([blocks[k] for k in mid], name="gather_mid_start", by_owner=False,
                                               after=w_first["d1"])
    flights["mix"], token_last = _copies_start([blocks[k] for k in last], name="gather_ffn2_start", by_owner=False,
                                               after=token_mid)

    def weights_after(stage, result):
        group = mid if stage == "ffn1" else last
        lands = _copies_wait(flights[stage], result, name="gather_" + stage + "_wait", by_owner=False)
        return {k: full(k, s) for k, s in zip(group, _with_own(lands, [blocks[k] for k in group], slot))}

    groups = {"mix": ("g2t", "u2t", "d2", "out"), "in": ("int",), "g1t": ("g1t",), "u1t": ("u1t",), "d1": ("d1",)}
    sent, sent_tokens = {}, []

    def grads_ready(stage, gw):
        stacks = [by_owner(k, gw[k]) for k in groups[stage]]
        flight, token = _copies_start(stacks, name="grads_" + stage + "_start", by_owner=True)
        sent[stage] = (stacks, flight)
        sent_tokens.append(token)
        return token

    norms = {"ffn1": ffn1_norm + token_last[0, 0], "mix": mix_norm, "sb": sb_out_norm, "hg": hg_out_norm,
             "ffn2": ffn2_norm, "final": final_norm.reshape(1, D_MODEL)}
    loss_row, grad_x, gw, gv = _local_step(x[0], loss_target[0], norms, hg_lower_bound_logits, w_first, weights_after,
                                           grads_ready)

    updated, after = {}, sent_tokens[-1]
    for stage, (stacks, flight) in sent.items():
        lands = _copies_wait(flight, after, name="grads_" + stage + "_wait", by_owner=True)
        own = [lax.dynamic_index_in_dim(s, slot, keepdims=False) for s in stacks]
        for k, part in zip(groups[stage], _with_own(lands, own, slot)):
            updated[k] = _sum_and_update(part, view(k, w_sh[k]), view(k, m_sh[k]), view(k, v_sh[k]), name="adamw_" + k,
                                         tie=after)
            after = updated[k][0]
    mats = [{k: view(k, updated[k][i]) for k in keys} for i in range(4)]

    lb_row = jnp.concatenate([gv["lb"], jnp.zeros_like(gv["lb"])], axis=1)
    part = _vector_rows([gv["ffn1"], gv["mix"], gv["ffn2"], gv["final"], jnp.concatenate([gv["sb"], gv["hg"]], axis=1),
                         lb_row, None, loss_row])
    vec_w = _vector_rows(vectors(ffn1_norm, mix_norm, sb_out_norm, hg_lower_bound_logits, hg_out_norm, ffn2_norm, final_norm))
    vec_m = _vector_rows(vectors(m_ffn1_norm, m_mix_norm, m_sb_out_norm, m_hg_lower_bound_logits, m_hg_out_norm,
                                 m_ffn2_norm, m_final_norm))
    vec_v = _vector_rows(vectors(v_ffn1_norm, v_mix_norm, v_sb_out_norm, v_hg_lower_bound_logits, v_hg_out_norm,
                                 v_ffn2_norm, v_final_norm))
    *vecs, loss_out = _vectors_update(part, vec_w, vec_m, vec_v, name="vectors_update")

    def leaves(mat, vec):
        half = D_MODEL // 2
        return (
            vec[0:1], mat["g1t"], mat["u1t"], mat["d1"], vec[1:2], mat["int"], vec[4:5, :half],
            vec[ROW_LOGITS].reshape(2, half), vec[4:5, half:], mat["out"], vec[2:3], mat["g2t"], mat["u2t"],
            mat["d2"], vec[3],
        )

    out = [loss_out[0, 0], grad_x[None]]
    for mat, vec in zip(mats, vecs):
        out.extend(leaves(mat, vec))
    return tuple(out)
```

```python
import jax
import jax.numpy as jnp
from jax import lax
from jax.experimental import pallas as pl
from jax.experimental.pallas import tpu as pltpu

F32, BF16 = jnp.float32, jnp.bfloat16
D_MODEL = 1024
D_FF = 2816
SB_WIDTH = 512
HG_WIDTH = 512
SB_HEAD_DIM = 64
HG_HEAD_DIM = 128
IN_COLS = 3584
EPS = 1e-6
N_DEV = 8
LANES = 128
HG_CHUNK = 16
VMEM_LIMIT_BYTES = 48 * 1024 * 1024
FFN_BWD_VMEM_LIMIT_BYTES = 56 * 1024 * 1024
ADAM_LR, ADAM_B1, ADAM_B2, ADAM_EPS, ADAM_WD, ADAM_STEP = 0.001, 0.9, 0.999, 1e-08, 0.01, 10
MESH = pl.DeviceIdType.MESH


def _params(*semantics, vmem_limit_bytes=VMEM_LIMIT_BYTES):
    return pltpu.CompilerParams(dimension_semantics=semantics, vmem_limit_bytes=vmem_limit_bytes)


def _dot(a, b):
    return jnp.dot(a, b, preferred_element_type=F32)


def _dot_nt(a, b):
    return lax.dot_general(a, b, (((1,), (1,)), ((), ())), preferred_element_type=F32)


def _dot_tn(a, b):
    return lax.dot_general(a, b, (((0,), (0,)), ((), ())), preferred_element_type=F32)


def _split3(x):
    hi = x.astype(BF16)
    r1 = x - hi.astype(F32)
    mid = r1.astype(BF16)
    lo = (r1 - mid.astype(F32)).astype(BF16)
    return hi, mid, lo


def _rms(xv):
    rstd = lax.rsqrt(jnp.mean(xv * xv, axis=-1, keepdims=True) + EPS)
    return xv * rstd, rstd


def _sigmoid(x):
    return 1.0 / (1.0 + jnp.exp(-x))


def _mm(a, b, *, name, tm, tn, nt=False, ta=False, out_dtype=F32, tie=None):
    k, m = a.shape if ta else a.shape[::-1]
    n = b.shape[0] if nt else b.shape[1]
    assert m % tm == 0 and n % tn == 0 and not (nt and ta), (name, a.shape, b.shape, tm, tn)

    def body(a_ref, b_ref, *rest):
        av = a_ref[...].astype(BF16)
        bv = b_ref[...].astype(BF16)
        rest[-1][...] = (_dot_nt(av, bv) if nt else _dot_tn(av, bv) if ta else _dot(av, bv)).astype(out_dtype)

    in_specs = [
        pl.BlockSpec((k, tm), lambda i, j: (0, i)) if ta else pl.BlockSpec((tm, k), lambda i, j: (i, 0)),
        pl.BlockSpec((tn, k), lambda i, j: (j, 0)) if nt else pl.BlockSpec((k, tn), lambda i, j: (0, j)),
    ]
    operands = [a, b]
    if tie is not None:
        in_specs.append(pl.BlockSpec(memory_space=pl.ANY))
        operands.append(tie)
    return pl.pallas_call(
        body,
        name=name,
        grid=(m // tm, n // tn),
        in_specs=in_specs,
        out_specs=pl.BlockSpec((tm, tn), lambda i, j: (i, j)),
        out_shape=jax.ShapeDtypeStruct((m, n), out_dtype),
        compiler_params=_params("parallel", "parallel"),
    )(*operands)


def _ffn_fwd(x, gain, wgt, wut, wd, *, name, tm=1024, tf=256):
    t = x.shape[0]
    nj = D_FF // tf

    def body(x_ref, g_ref, wg_ref, wu_ref, wd_prev_ref, wd_last_ref, xo_ref, a_ref, b_ref, h_ref, st_ref, acc, s_prev):
        j = pl.program_id(1)

        @pl.when(j == 0)
        def _():
            xhat, _ = _rms(x_ref[...])
            h_ref[...] = (xhat * g_ref[...]).astype(BF16)
            acc[...] = jnp.zeros_like(acc)
            s_prev[...] = jnp.zeros_like(s_prev)

        acc[...] += _dot(s_prev[...], wd_prev_ref[...])
        h = h_ref[...]
        a = _dot_nt(h, wg_ref[...])
        b = _dot_nt(h, wu_ref[...])
        a_ref[...] = a.astype(BF16)
        b_ref[...] = b.astype(BF16)
        s = (a * _sigmoid(a) * b).astype(BF16)
        st_ref[...] = s
        s_prev[...] = s

        @pl.when(j == nj - 1)
        def _():
            xo_ref[...] = x_ref[...] + 0.5 * (acc[...] + _dot(s, wd_last_ref[...]))

    return pl.pallas_call(
        body,
        name=name,
        grid=(t // tm, nj),
        in_specs=[
            pl.BlockSpec((tm, D_MODEL), lambda i, j: (i, 0)),
            pl.BlockSpec((1, D_MODEL), lambda i, j: (0, 0)),
            pl.BlockSpec((tf, D_MODEL), lambda i, j: (j, 0)),
            pl.BlockSpec((tf, D_MODEL), lambda i, j: (j, 0)),
            pl.BlockSpec((tf, D_MODEL), lambda i, j: (jnp.maximum(j - 1, 0), 0)),
            pl.BlockSpec((tf, D_MODEL), lambda i, j: (nj - 1, 0)),
        ],
        out_specs=[
            pl.BlockSpec((tm, D_MODEL), lambda i, j: (i, 0)),
            pl.BlockSpec((tm, tf), lambda i, j: (i, j)),
            pl.BlockSpec((tm, tf), lambda i, j: (i, j)),
            pl.BlockSpec((tm, D_MODEL), lambda i, j: (i, 0)),
            pl.BlockSpec((tm, tf), lambda i, j: (i, j)),
        ],
        out_shape=[
            jax.ShapeDtypeStruct((t, D_MODEL), F32),
            jax.ShapeDtypeStruct((t, D_FF), BF16),
            jax.ShapeDtypeStruct((t, D_FF), BF16),
            jax.ShapeDtypeStruct((t, D_MODEL), BF16),
            jax.ShapeDtypeStruct((t, D_FF), BF16),
        ],
        scratch_shapes=[pltpu.VMEM((tm, D_MODEL), F32), pltpu.VMEM((tm, tf), BF16)],
        compiler_params=_params("parallel", "arbitrary"),
    )(x, gain, wgt, wut, wd, wd)


def _ffn_bwd(dout, x, gain, a, b, wgt, wut, wd, *, name, tm=1024, tf=256):
    t = x.shape[0]
    nj = D_FF // tf

    def body(do_ref, x_ref, g_ref, a_ref, b_ref, wg_prev_ref, wu_prev_ref, wg_last_ref, wu_last_ref, wd_ref,
             dx_ref, dg_ref, da_ref, db_ref, dob_ref, dob_scr, dh, da_prev, db_prev):
        i = pl.program_id(0)
        j = pl.program_id(1)

        @pl.when(j == 0)
        def _():
            d = (0.5 * do_ref[...]).astype(BF16)
            dob_scr[...] = d
            dob_ref[...] = d
            dh[...] = jnp.zeros_like(dh)
            da_prev[...] = jnp.zeros_like(da_prev)
            db_prev[...] = jnp.zeros_like(db_prev)

        dh[...] += _dot(da_prev[...], wg_prev_ref[...]) + _dot(db_prev[...], wu_prev_ref[...])
        ds = _dot_nt(dob_scr[...], wd_ref[...])
        av = a_ref[...].astype(F32)
        bv = b_ref[...].astype(F32)
        sig = _sigmoid(av)
        dbv = (ds * (av * sig)).astype(BF16)
        dav = (ds * bv * (sig * (1.0 + av * (1.0 - sig)))).astype(BF16)
        da_ref[...] = dav
        db_ref[...] = dbv
        da_prev[...] = dav
        db_prev[...] = dbv

        @pl.when(j == nj - 1)
        def _():
            xhat, rstd = _rms(x_ref[...])
            dhv = dh[...] + _dot(dav, wg_last_ref[...]) + _dot(dbv, wu_last_ref[...])
            part = jnp.sum(dhv * xhat, axis=0, keepdims=True)

            @pl.when(i == 0)
            def _():
                dg_ref[...] = part

            @pl.when(i > 0)
            def _():
                dg_ref[...] += part

            dxh = dhv * g_ref[...]
            dx_ref[...] = do_ref[...] + rstd * (dxh - xhat * jnp.mean(dxh * xhat, axis=-1, keepdims=True))

    return pl.pallas_call(
        body,
        name=name,
        grid=(t // tm, nj),
        in_specs=[
            pl.BlockSpec((tm, D_MODEL), lambda i, j: (i, 0)),
            pl.BlockSpec((tm, D_MODEL), lambda i, j: (i, 0)),
            pl.BlockSpec((1, D_MODEL), lambda i, j: (0, 0)),
            pl.BlockSpec((tm, tf), lambda i, j: (i, j)),
            pl.BlockSpec((tm, tf), lambda i, j: (i, j)),
            pl.BlockSpec((tf, D_MODEL), lambda i, j: (jnp.maximum(j - 1, 0), 0)),
            pl.BlockSpec((tf, D_MODEL), lambda i, j: (jnp.maximum(j - 1, 0), 0)),
            pl.BlockSpec((tf, D_MODEL), lambda i, j: (nj - 1, 0)),
            pl.BlockSpec((tf, D_MODEL), lambda i, j: (nj - 1, 0)),
            pl.BlockSpec((tf, D_MODEL), lambda i, j: (j, 0)),
        ],
        out_specs=[
            pl.BlockSpec((tm, D_MODEL), lambda i, j: (i, 0)),
            pl.BlockSpec((1, D_MODEL), lambda i, j: (0, 0)),
            pl.BlockSpec((tm, tf), lambda i, j: (i, j)),
            pl.BlockSpec((tm, tf), lambda i, j: (i, j)),
            pl.BlockSpec((tm, D_MODEL), lambda i, j: (i, 0)),
        ],
        out_shape=[
            jax.ShapeDtypeStruct((t, D_MODEL), F32),
            jax.ShapeDtypeStruct((1, D_MODEL), F32),
            jax.ShapeDtypeStruct((t, D_FF), BF16),
            jax.ShapeDtypeStruct((t, D_FF), BF16),
            jax.ShapeDtypeStruct((t, D_MODEL), BF16),
        ],
        scratch_shapes=[pltpu.VMEM((tm, D_MODEL), BF16), pltpu.VMEM((tm, D_MODEL), F32), pltpu.VMEM((tm, tf), BF16),
                        pltpu.VMEM((tm, tf), BF16)],
        compiler_params=_params("arbitrary", "arbitrary", vmem_limit_bytes=FFN_BWD_VMEM_LIMIT_BYTES),
    )(dout, x, gain, a, b, wgt, wut, wgt, wut, wd)


def _norm_fwd(x, gain, *, name, tm=512):
    t = x.shape[0]

    def body(x_ref, g_ref, h_ref):
        xhat, _ = _rms(x_ref[...])
        h_ref[...] = (xhat * g_ref[...]).astype(BF16)

    return pl.pallas_call(
        body,
        name=name,
        grid=(t // tm,),
        in_specs=[pl.BlockSpec((tm, D_MODEL), lambda i: (i, 0)), pl.BlockSpec((1, D_MODEL), lambda i: (0, 0))],
        out_specs=pl.BlockSpec((tm, D_MODEL), lambda i: (i, 0)),
        out_shape=jax.ShapeDtypeStruct((t, D_MODEL), BF16),
        compiler_params=_params("parallel"),
    )(x, gain)


def _norm_bwd(dh, x, gain, dres, *, name, tm=512):
    t = x.shape[0]

    def body(dh_ref, x_ref, g_ref, dr_ref, dx_ref, dg_ref):
        i = pl.program_id(0)
        xhat, rstd = _rms(x_ref[...])
        dhv = dh_ref[...]
        part = jnp.sum(dhv * xhat, axis=0, keepdims=True)

        @pl.when(i == 0)
        def _():
            dg_ref[...] = part

        @pl.when(i > 0)
        def _():
            dg_ref[...] += part

        dxh = dhv * g_ref[...]
        dx_ref[...] = dr_ref[...] + rstd * (dxh - xhat * jnp.mean(dxh * xhat, axis=-1, keepdims=True))

    row = pl.BlockSpec((tm, D_MODEL), lambda i: (i, 0))
    vec = pl.BlockSpec((1, D_MODEL), lambda i: (0, 0))
    return pl.pallas_call(
        body,
        name=name,
        grid=(t // tm,),
        in_specs=[row, row, vec, row],
        out_specs=[row, vec],
        out_shape=[jax.ShapeDtypeStruct((t, D_MODEL), F32), jax.ShapeDtypeStruct((1, D_MODEL), F32)],
        compiler_params=_params("arbitrary"),
    )(dh, x, gain, dres)


ATT_Q_TILE = 512
ATT_K_BLOCK = 256


def _first_head_lanes():
    return lax.broadcasted_iota(jnp.int32, (1, LANES), 1) < SB_HEAD_DIM


def _stack_heads(x):
    first = _first_head_lanes()
    return jnp.concatenate([jnp.where(first, x, 0.0), jnp.where(first, 0.0, x)], axis=0)


def _unstack_heads(x, rows):
    return jnp.where(_first_head_lanes(), x[:rows], x[rows:])


def _rows_from(x, first, rows):
    return x if first == 0 else jnp.concatenate([x[first:rows], x[rows + first:]], axis=0)


def _rows_into(full, part, first, rows):
    if first == 0:
        return part
    n = rows - first
    return jnp.concatenate([full[:first], part[:n], full[rows:rows + first], part[n:]], axis=0)


def _tri(n, relation):
    r = lax.broadcasted_iota(jnp.int32, (n, n), 0)
    c = lax.broadcasted_iota(jnp.int32, (n, n), 1)
    return relation(r, c).astype(BF16)


def _scan_dot(x, tri):
    hi = x.astype(BF16)
    lo = (x - hi.astype(F32)).astype(BF16)
    return _dot(jnp.concatenate([hi, lo], axis=1), jnp.concatenate([tri, tri], axis=0))


def _log_terms(z):
    lbeta = jnp.minimum(z, 0.0) - jnp.log(1.0 + jnp.exp(-jnp.abs(z)))
    return lbeta, lbeta - z


def _attn_fwd(proj, *, name):
    t = proj.shape[0]
    tq, tk = ATT_Q_TILE, ATT_K_BLOCK
    diag = tq // tk
    n_pairs = SB_WIDTH // LANES

    def body(q_ref, k_ref, v_ref, o_ref, kept_ref):
        qi = pl.program_id(1)
        q = q_ref[...] * (SB_HEAD_DIM ** -0.5)
        qs = _stack_heads(q).astype(BF16)
        tri = _tri(tk, lambda j, s: j > s)
        trow = lax.broadcasted_iota(jnp.int32, (tq, tk), 0)
        scol = lax.broadcasted_iota(jnp.int32, (tq, tk), 1)

        def block(kb, carry, causal, first=0):
            acc, c = carry
            off = pl.multiple_of(kb * tk, tk)
            lbeta, lrest = _log_terms(_dot_nt(_rows_from(qs, first, tq), k_ref[pl.ds(off, tk), :].astype(BF16)))
            if causal is not None:
                lrest = jnp.where(causal, lrest, 0.0)
            w = jnp.exp(lbeta + (_scan_dot(lrest, tri) + _rows_from(c, first, tq)))
            if causal is not None:
                w = jnp.where(causal, w, 0.0)
            wb = w.astype(BF16)
            kept_ref[0, 0, kb] = _rows_into(jnp.zeros((2 * tq, tk), BF16), wb, first, tq)
            acc = _rows_into(acc, _rows_from(acc, first, tq) + _dot(wb, v_ref[pl.ds(off, tk), :].astype(BF16)), first, tq)
            return acc, _rows_into(c, _rows_from(c, first, tq) + jnp.sum(lrest, axis=1, keepdims=True), first, tq)

        carry = (jnp.zeros((2 * tq, LANES), F32), jnp.zeros((2 * tq, 1), F32))
        n_full = qi * diag
        for j in reversed(range(diag)):
            mask = ((scol + j * tk) < trow)[j * tk:]
            carry = block(n_full + j, carry, jnp.concatenate([mask, mask], axis=0), first=j * tk)

        def step(it, carry):
            return block(n_full - 1 - it, carry, None)

        acc, _ = lax.fori_loop(0, n_full, step, carry)
        o_ref[...] = _unstack_heads(acc, tq)

    return pl.pallas_call(
        body,
        name=name,
        grid=(n_pairs, t // tq),
        in_specs=[
            pl.BlockSpec((tq, LANES), lambda p, i: (i, p)),
            pl.BlockSpec((t, LANES), lambda p, i: (0, n_pairs + p)),
            pl.BlockSpec((t, LANES), lambda p, i: (0, 2 * n_pairs + p)),
        ],
        out_specs=[pl.BlockSpec((tq, LANES), lambda p, i: (i, p)),
                   pl.BlockSpec((1, 1, t // tk, 2 * tq, tk), lambda p, i: (p, i, 0, 0, 0))],
        out_shape=[jax.ShapeDtypeStruct((t, SB_WIDTH), F32),
                   jax.ShapeDtypeStruct((n_pairs, t // tq, t // tk, 2 * tq, tk), BF16)],
        compiler_params=_params("parallel", "parallel"),
    )(proj, proj, proj)


def _attn_bwd(proj, kept, do, *, name, tie=None):
    t = proj.shape[0]
    tq, tk = ATT_Q_TILE, ATT_K_BLOCK
    diag = tq // tk
    n_pairs = SB_WIDTH // LANES
    scale = SB_HEAD_DIM ** -0.5

    def body(q_ref, k_ref, v_ref, kept_ref, do_ref, *rest):
        dq_ref, dk_ref, dv_ref = rest[-3:]
        qi = pl.program_id(1)

        @pl.when(qi == 0)
        def _():
            dk_ref[...] = jnp.zeros_like(dk_ref)
            dv_ref[...] = jnp.zeros_like(dv_ref)

        qs = _stack_heads(q_ref[...] * scale).astype(BF16)
        dos = _stack_heads(do_ref[...]).astype(BF16)
        before = _tri(tk, lambda s, j: s < j)
        trow = lax.broadcasted_iota(jnp.int32, (tq, tk), 0)
        scol = lax.broadcasted_iota(jnp.int32, (tq, tk), 1)

        def block(kb, carry, causal, first=0):
            dq, cg = carry
            off = pl.multiple_of(kb * tk, tk)
            q_rows, do_rows = _rows_from(qs, first, tq), _rows_from(dos, first, tq)
            wb = _rows_from(kept_ref[0, 0, kb], first, tq)
            kblk = k_ref[pl.ds(off, tk), :].astype(BF16)
            sig = 0.5 + 0.5 * jnp.tanh(0.5 * _dot_nt(q_rows, kblk))
            g = wb.astype(F32) * _dot_nt(do_rows, v_ref[pl.ds(off, tk), :].astype(BF16))
            prior = _scan_dot(g, before) + _rows_from(cg, first, tq)
            dz = g - sig * (g + prior)
            if causal is not None:
                dz = jnp.where(causal, dz, 0.0)
            dzb = dz.astype(BF16)
            dq = _rows_into(dq, _rows_from(dq, first, tq) + _dot(dzb, kblk), first, tq)
            dk_ref[pl.ds(off, tk), :] += _dot_tn(dzb, q_rows)
            dv_ref[pl.ds(off, tk), :] += _dot_tn(wb, do_rows)
            return dq, _rows_into(cg, _rows_from(cg, first, tq) + jnp.sum(g, axis=1, keepdims=True), first, tq)

        n_full = qi * diag
        carry = lax.fori_loop(0, n_full, lambda kb, carry: block(kb, carry, None),
                              (jnp.zeros((2 * tq, LANES), F32), jnp.zeros((2 * tq, 1), F32)))
        for j in range(diag):
            mask = ((scol + j * tk) < trow)[j * tk:]
            carry = block(n_full + j, carry, jnp.concatenate([mask, mask], axis=0), first=j * tk)
        dq_ref[...] = _unstack_heads(carry[0], tq) * scale

    tile_spec = pl.BlockSpec((tq, LANES), lambda p, i: (i, p))
    full_spec = pl.BlockSpec((t, LANES), lambda p, i: (0, p))
    return pl.pallas_call(
        body,
        name=name,
        grid=(n_pairs, t // tq),
        in_specs=[
            tile_spec,
            pl.BlockSpec((t, LANES), lambda p, i: (0, n_pairs + p)),
            pl.BlockSpec((t, LANES), lambda p, i: (0, 2 * n_pairs + p)),
            pl.BlockSpec((1, 1, t // tk, 2 * tq, tk), lambda p, i: (p, i, 0, 0, 0)),
            tile_spec,
        ] + ([] if tie is None else [pl.BlockSpec(memory_space=pl.ANY)]),
        out_specs=[tile_spec, full_spec, full_spec],
        out_shape=[jax.ShapeDtypeStruct((t, SB_WIDTH), F32)] * 3,
        compiler_params=_params("arbitrary", "arbitrary"),
    )(proj, proj, proj, kept, do, *([] if tie is None else [tie]))


HG_BLOCK = 128
HG_HEADS = HG_WIDTH // HG_HEAD_DIM


def _chunk_mats(n):
    r = lax.broadcasted_iota(jnp.int32, (n, n), 0)
    c = lax.broadcasted_iota(jnp.int32, (n, n), 1)
    same = (r // HG_CHUNK) == (c // HG_CHUNK)
    upto = (same & (c <= r)).astype(BF16)
    whole = same.astype(BF16)
    onward = (same & (c >= r)).astype(BF16)
    return upto, whole, onward


def _rows_dot(mat, x):
    return _dot(jnp.concatenate([mat, mat, mat], axis=1), jnp.concatenate(_split3(x), axis=0))


def _split_heads(x):
    return jnp.stack([x[:, h * HG_HEAD_DIM:(h + 1) * HG_HEAD_DIM] for h in range(HG_HEADS)], axis=0)


def _merge_heads(x):
    return jnp.concatenate([x[h] for h in range(HG_HEADS)], axis=1)


def _lower_bound(lg_ref):
    lg = lg_ref[...]
    return _sigmoid(lg[0:1, :] - lg[1:2, :])


def _hgrn_prepare(q_ref, f_ref, lb, h, upto, whole):
    cols = slice(h * HG_HEAD_DIM, (h + 1) * HG_HEAD_DIM)
    lbh = lb[:, cols]
    sg = _sigmoid(f_ref[:, cols])
    forget = lbh + (1.0 - lbh) * sg
    logf = jnp.log(forget)
    kk = (1.0 - lbh) * (1.0 - sg)
    qv = q_ref[:, cols]
    qsig = _sigmoid(qv)
    qh = qv * qsig
    b = _rows_dot(upto, logf)
    blast = _rows_dot(whole, logf)
    return dict(lbh=lbh, sg=sg, forget=forget, kk=kk, qv=qv, qsig=qsig, qh=qh, b=b, eb=jnp.exp(b),
                ekb=jnp.exp(blast - b), dl=jnp.exp(blast))


def _hgrn_fwd(proj, logits, *, name):
    t = proj.shape[0]
    tb = HG_BLOCK
    nc = tb // HG_CHUNK
    hd = HG_HEAD_DIM

    def body(q_ref, f_ref, i_ref, lg_ref, o_ref, st_ref, state, qh_s, kk_s, b_s, qe_s, ke_s, dl_s):
        @pl.when(pl.program_id(0) == 0)
        def _():
            state[...] = jnp.zeros_like(state)

        lb = _lower_bound(lg_ref)
        upto, whole, _ = _chunk_mats(tb)
        for h in range(HG_HEADS):
            p = _hgrn_prepare(q_ref, f_ref, lb, h, upto, whole)
            qh_s[h] = p["qh"]
            kk_s[h] = p["kk"]
            b_s[h] = p["b"]
            qe_s[h] = (p["qh"] * p["eb"]).astype(BF16)
            ke_s[h] = (p["kk"] * p["ekb"]).astype(BF16)
            dl_s[h] = p["dl"]
        rowi = lax.broadcasted_iota(jnp.int32, (HG_HEADS, HG_CHUNK, hd), 1)

        def chunk(c, _):
            r0 = pl.multiple_of(c * HG_CHUNK, HG_CHUNK)
            rows = pl.ds(r0, HG_CHUNK)
            bc = b_s[:, rows, :]
            qc = qh_s[:, rows, :]
            kc = kk_s[:, rows, :]
            vc = _split_heads(i_ref[rows, :])
            s_in = state[...]
            st_ref[c] = s_in
            s_in_b = s_in.astype(BF16)
            qe = qe_s[:, rows, :]
            o = jnp.stack([_dot_nt(qe[h], s_in_b[h]) for h in range(HG_HEADS)], axis=0)
            for s in range(HG_CHUNK):
                pair = jnp.where(rowi >= s, qc * jnp.exp(bc - bc[:, s:s + 1, :]) * kc[:, s:s + 1, :], 0.0)
                o = o + jnp.sum(pair, axis=2, keepdims=True) * vc[:, s:s + 1, :]
            o_ref[rows, :] = _merge_heads(o)
            vcb = vc.astype(BF16)
            ke = ke_s[:, rows, :]
            update = jnp.stack([_dot_tn(vcb[h], ke[h]) for h in range(HG_HEADS)], axis=0)
            state[...] = s_in * dl_s[:, pl.ds(r0, 1), :] + update
            return 0

        lax.fori_loop(0, nc, chunk, 0)

    blk = lambda col: pl.BlockSpec((tb, HG_WIDTH), lambda i: (i, col))
    head_f32 = pltpu.VMEM((HG_HEADS, tb, hd), F32)
    head_bf16 = pltpu.VMEM((HG_HEADS, tb, hd), BF16)
    return pl.pallas_call(
        body,
        name=name,
        grid=(t // tb,),
        in_specs=[blk(3), blk(4), blk(5), pl.BlockSpec((2, HG_WIDTH), lambda i: (0, 0))],
        out_specs=[
            pl.BlockSpec((tb, HG_WIDTH), lambda i: (i, 0)),
            pl.BlockSpec((nc, HG_HEADS, hd, hd), lambda i: (i, 0, 0, 0)),
        ],
        out_shape=[
            jax.ShapeDtypeStruct((t, HG_WIDTH), F32),
            jax.ShapeDtypeStruct((t // HG_CHUNK, HG_HEADS, hd, hd), F32),
        ],
        scratch_shapes=[pltpu.VMEM((HG_HEADS, hd, hd), F32), head_f32, head_f32, head_f32, head_bf16, head_bf16,
                        head_f32],
        compiler_params=_params("arbitrary"),
    )(proj, proj, proj, logits)


def _hgrn_bwd(proj, logits, states, do, *, name):
    t = proj.shape[0]
    tb = HG_BLOCK
    nb = t // tb
    nc = tb // HG_CHUNK
    hd = HG_HEAD_DIM

    def body(q_ref, f_ref, i_ref, lg_ref, st_ref, do_ref, dq_ref, df_ref, di_ref, dlb_ref,
             dstate, qh_s, kk_s, b_s, eb_s, ekb_s, qe_s, ke_s, dl_s, dqh_s, dkk_s, dlf_s):
        step = pl.program_id(0)

        @pl.when(step == 0)
        def _():
            dstate[...] = jnp.zeros_like(dstate)
            dlb_ref[...] = jnp.zeros_like(dlb_ref)

        lb = _lower_bound(lg_ref)
        upto, whole, _ = _chunk_mats(tb)
        prepared = []
        for h in range(HG_HEADS):
            p = _hgrn_prepare(q_ref, f_ref, lb, h, upto, whole)
            prepared.append(p)
            qh_s[h] = p["qh"]
            kk_s[h] = p["kk"]
            b_s[h] = p["b"]
            eb_s[h] = p["eb"]
            ekb_s[h] = p["ekb"]
            qe_s[h] = (p["qh"] * p["eb"]).astype(BF16)
            ke_s[h] = (p["kk"] * p["ekb"]).astype(BF16)
            dl_s[h] = p["dl"]
        rowi = lax.broadcasted_iota(jnp.int32, (HG_CHUNK, hd), 0)
        r16 = lax.broadcasted_iota(jnp.int32, (HG_CHUNK, HG_CHUNK), 0)
        c16 = lax.broadcasted_iota(jnp.int32, (HG_CHUNK, HG_CHUNK), 1)
        onward = (c16 >= r16).astype(BF16)

        def chunk(it, _):
            c = nc - 1 - it
            r0 = pl.multiple_of(c * HG_CHUNK, HG_CHUNK)
            rows = pl.ds(r0, HG_CHUNK)
            for h in range(HG_HEADS):
                cols = slice(h * hd, (h + 1) * hd)
                bc = b_s[h, rows, :]
                qc = qh_s[h, rows, :]
                kc = kk_s[h, rows, :]
                vc = i_ref[rows, cols]
                doc = do_ref[rows, cols]
                s_in = st_ref[c, h]
                ds_out = dstate[h]
                ds_out_b = ds_out.astype(BF16)
                docb = doc.astype(BF16)
                dl_row = dl_s[h, pl.ds(r0, 1), :]
                dqh = _dot(docb, s_in.astype(BF16)) * eb_s[h, rows, :]
                dkk = _dot(vc.astype(BF16), ds_out_b) * ekb_s[h, rows, :]
                dv = _dot_nt(ke_s[h, rows, :], ds_out_b)
                db = dqh * qc - dkk * kc
                dwhole = jnp.sum(dkk * kc, axis=0, keepdims=True) + jnp.sum(ds_out * s_in, axis=0, keepdims=True) * dl_row
                dk_rows, dv_rows = [], []
                for s in range(HG_CHUNK):
                    keep = rowi >= s
                    e = jnp.exp(bc - bc[s:s + 1, :])
                    k_row = kc[s:s + 1, :]
                    pcol = jnp.sum(jnp.where(keep, qc * e * k_row, 0.0), axis=1, keepdims=True)
                    dpcol = jnp.sum(doc * vc[s:s + 1, :], axis=1, keepdims=True)
                    m = jnp.where(keep, e * dpcol, 0.0)
                    y = m * qc
                    dqh = dqh + m * k_row
                    db = db + y * k_row
                    dk_rows.append(jnp.sum(y, axis=0, keepdims=True))
                    dv_rows.append(jnp.sum(pcol * doc, axis=0, keepdims=True))
                dkk_pairs = jnp.concatenate(dk_rows, axis=0)
                dkk = dkk + dkk_pairs
                db = db - dkk_pairs * kc
                dv = dv + jnp.concatenate(dv_rows, axis=0)
                dqh_s[h, rows, :] = dqh
                dkk_s[h, rows, :] = dkk
                dlf_s[h, rows, :] = _rows_dot(onward, db) + dwhole
                di_ref[rows, cols] = dv
                dstate[h] = ds_out * dl_row + _dot_tn(docb, qe_s[h, rows, :])
            return 0

        lax.fori_loop(0, nc, chunk, 0)
        for h in range(HG_HEADS):
            cols = slice(h * hd, (h + 1) * hd)
            p = prepared[h]
            dq_ref[:, cols] = dqh_s[h] * (p["qsig"] * (1.0 + p["qv"] * (1.0 - p["qsig"])))
            dforget = dlf_s[h] / p["forget"] - dkk_s[h]
            df_ref[:, cols] = dforget * (1.0 - p["lbh"]) * p["sg"] * (1.0 - p["sg"])
            dlb_ref[:, cols] += jnp.sum(dforget * (1.0 - p["sg"]), axis=0, keepdims=True)

    blk = lambda col: pl.BlockSpec((tb, HG_WIDTH), lambda i: (nb - 1 - i, col))
    vec = pl.BlockSpec((1, HG_WIDTH), lambda i: (0, 0))
    head_f32 = pltpu.VMEM((HG_HEADS, tb, hd), F32)
    head_bf16 = pltpu.VMEM((HG_HEADS, tb, hd), BF16)
    return pl.pallas_call(
        body,
        name=name,
        grid=(nb,),
        in_specs=[
            blk(3), blk(4), blk(5),
            pl.BlockSpec((2, HG_WIDTH), lambda i: (0, 0)),
            pl.BlockSpec((nc, HG_HEADS, hd, hd), lambda i: (nb - 1 - i, 0, 0, 0)),
            blk(0),
        ],
        out_specs=[blk(0), blk(0), blk(0), vec],
        out_shape=[jax.ShapeDtypeStruct((t, HG_WIDTH), F32)] * 3 + [jax.ShapeDtypeStruct((1, HG_WIDTH), F32)],
        scratch_shapes=[
            pltpu.VMEM((HG_HEADS, hd, hd), F32),
            head_f32, head_f32, head_f32, head_f32, head_f32, head_bf16, head_bf16, head_f32,
            head_f32, head_f32, head_f32,
        ],
        compiler_params=_params("arbitrary"),
    )(proj, proj, proj, logits, states, do)


def _group_mat(width, head_dim):
    r = lax.broadcasted_iota(jnp.int32, (width, width), 0)
    c = lax.broadcasted_iota(jnp.int32, (width, width), 1)
    return ((r // head_dim) == (c // head_dim)).astype(BF16)


def _head_mean(x, mat, head_dim):
    hi = x.astype(BF16)
    lo = (x - hi.astype(F32)).astype(BF16)
    return (_dot(hi, mat) + _dot(lo, mat)) * (1.0 / head_dim)


def _mix_out_fwd(o_sb, o_hg, proj, g_sb, g_hg, w_out, x1, *, name, tm=512):
    t = x1.shape[0]

    def body(osb_ref, ohg_ref, gate_ref, gsb_ref, ghg_ref, w_ref, x_ref, xo_ref, mt_ref):
        msb = _group_mat(SB_WIDTH, SB_HEAD_DIM)
        mhg = _group_mat(HG_WIDTH, HG_HEAD_DIM)
        osb = osb_ref[...]
        ohg = ohg_ref[...]
        nsb = osb * lax.rsqrt(_head_mean(osb * osb, msb, SB_HEAD_DIM) + EPS) * gsb_ref[...]
        gate = gate_ref[...]
        nhg = ohg * lax.rsqrt(_head_mean(ohg * ohg, mhg, HG_HEAD_DIM) + EPS) * ghg_ref[...] * (gate * _sigmoid(gate))
        mixed = jnp.concatenate([nsb, nhg], axis=1).astype(BF16)
        mt_ref[...] = mixed
        xo_ref[...] = x_ref[...] + _dot(mixed, w_ref[...])

    half = pl.BlockSpec((tm, SB_WIDTH), lambda i: (i, 0))
    vec = pl.BlockSpec((1, SB_WIDTH), lambda i: (0, 0))
    row = pl.BlockSpec((tm, D_MODEL), lambda i: (i, 0))
    return pl.pallas_call(
        body,
        name=name,
        grid=(t // tm,),
        in_specs=[half, half, pl.BlockSpec((tm, HG_WIDTH), lambda i: (i, 6)), vec, vec,
                  pl.BlockSpec((D_MODEL, D_MODEL), lambda i: (0, 0)), row],
        out_specs=[row, row],
        out_shape=[jax.ShapeDtypeStruct((t, D_MODEL), F32), jax.ShapeDtypeStruct((t, D_MODEL), BF16)],
        compiler_params=_params("parallel"),
    )(o_sb, o_hg, proj, g_sb, g_hg, w_out, x1)


def _mix_out_bwd(dx2, o_sb, o_hg, proj, g_sb, g_hg, w_out, *, name, tm=512):
    t = dx2.shape[0]

    def body(dx_ref, osb_ref, ohg_ref, gate_ref, gsb_ref, ghg_ref, w_ref, dosb_ref, dohg_ref, dgate_ref, dgsb_ref,
             dghg_ref, dxb_ref):
        i = pl.program_id(0)
        msb = _group_mat(SB_WIDTH, SB_HEAD_DIM)
        mhg = _group_mat(HG_WIDTH, HG_HEAD_DIM)
        dxb = dx_ref[...].astype(BF16)
        dxb_ref[...] = dxb
        dmixed = _dot_nt(dxb, w_ref[...])
        dnsb = dmixed[:, :SB_WIDTH]
        dy = dmixed[:, SB_WIDTH:]

        osb = osb_ref[...]
        rstd = lax.rsqrt(_head_mean(osb * osb, msb, SB_HEAD_DIM) + EPS)
        ohat = osb * rstd
        part_sb = jnp.sum(dnsb * ohat, axis=0, keepdims=True)
        dohat = dnsb * gsb_ref[...]
        dosb_ref[...] = rstd * (dohat - ohat * _head_mean(dohat * ohat, msb, SB_HEAD_DIM))

        ohg = ohg_ref[...]
        rstd = lax.rsqrt(_head_mean(ohg * ohg, mhg, HG_HEAD_DIM) + EPS)
        ohat = ohg * rstd
        gate = gate_ref[...]
        sig = _sigmoid(gate)
        dn = dy * (gate * sig)
        dgate_ref[...] = dy * (ohat * ghg_ref[...]) * (sig * (1.0 + gate * (1.0 - sig)))
        part_hg = jnp.sum(dn * ohat, axis=0, keepdims=True)
        dohat = dn * ghg_ref[...]
        dohg_ref[...] = rstd * (dohat - ohat * _head_mean(dohat * ohat, mhg, HG_HEAD_DIM))

        @pl.when(i == 0)
        def _():
            dgsb_ref[...] = part_sb
            dghg_ref[...] = part_hg

        @pl.when(i > 0)
        def _():
            dgsb_ref[...] += part_sb
            dghg_ref[...] += part_hg

    half = pl.BlockSpec((tm, SB_WIDTH), lambda i: (i, 0))
    vec = pl.BlockSpec((1, SB_WIDTH), lambda i: (0, 0))
    row = pl.BlockSpec((tm, D_MODEL), lambda i: (i, 0))
    return pl.pallas_call(
        body,
        name=name,
        grid=(t // tm,),
        in_specs=[row, half, half, pl.BlockSpec((tm, HG_WIDTH), lambda i: (i, 6)), vec, vec,
                  pl.BlockSpec((D_MODEL, D_MODEL), lambda i: (0, 0))],
        out_specs=[half, half, half, vec, vec, row],
        out_shape=[jax.ShapeDtypeStruct((t, SB_WIDTH), F32)] * 3 + [jax.ShapeDtypeStruct((1, SB_WIDTH), F32)] * 2
        + [jax.ShapeDtypeStruct((t, D_MODEL), BF16)],
        compiler_params=_params("arbitrary"),
    )(dx2, o_sb, o_hg, proj, g_sb, g_hg, w_out)


def _loss_head(x3, gain, target, *, name, tm=512):
    t = x3.shape[0]

    def body(x_ref, g_ref, y_ref, dx_ref, dg_ref, loss_ref):
        i = pl.program_id(0)
        xhat, rstd = _rms(x_ref[...])
        err = xhat * g_ref[...] - y_ref[...]
        part_loss = 0.5 * jnp.sum(jnp.mean(err * err, axis=-1, keepdims=True), axis=0, keepdims=True)
        dy = err * (1.0 / D_MODEL)
        part_g = jnp.sum(dy * xhat, axis=0, keepdims=True)

        @pl.when(i == 0)
        def _():
            dg_ref[...] = part_g
            loss_ref[...] = jnp.broadcast_to(part_loss, loss_ref.shape)

        @pl.when(i > 0)
        def _():
            dg_ref[...] += part_g
            loss_ref[...] += jnp.broadcast_to(part_loss, loss_ref.shape)

        dxh = dy * g_ref[...]
        dx_ref[...] = rstd * (dxh - xhat * jnp.mean(dxh * xhat, axis=-1, keepdims=True))

    row = pl.BlockSpec((tm, D_MODEL), lambda i: (i, 0))
    vec = pl.BlockSpec((1, D_MODEL), lambda i: (0, 0))
    return pl.pallas_call(
        body,
        name=name,
        grid=(t // tm,),
        in_specs=[row, vec, row],
        out_specs=[row, vec, vec],
        out_shape=[jax.ShapeDtypeStruct((t, D_MODEL), F32), jax.ShapeDtypeStruct((1, D_MODEL), F32),
                   jax.ShapeDtypeStruct((1, D_MODEL), F32)],
        compiler_params=_params("arbitrary"),
    )(x3, gain, target)


def _local_step(x, target, norms, logits, w, weights_after=None, grads_ready=None):
    w = dict(w)
    x1, a1, b1, h1, s1 = _ffn_fwd(x, norms["ffn1"], w["g1t"], w["u1t"], w["d1"], name="ffn1_fwd")
    if weights_after is not None:
        w.update(weights_after("ffn1", x1))
    hm = _norm_fwd(x1, norms["mix"], name="mix_norm_fwd")
    proj = _mm(hm, w["int"], name="in_proj", tm=512, tn=IN_COLS, nt=True)
    o_sb, sb_kept = _attn_fwd(proj, name="sb_attn_fwd")
    o_hg, states = _hgrn_fwd(proj, logits, name="hgrn2_fwd")
    x2, mixed = _mix_out_fwd(o_sb, o_hg, proj, norms["sb"], norms["hg"], w["out"], x1, name="mix_out_fwd")
    if weights_after is not None:
        w.update(weights_after("mix", x2))
    x3, a2, b2, h2, s2 = _ffn_fwd(x2, norms["ffn2"], w["g2t"], w["u2t"], w["d2"], name="ffn2_fwd")
    dx3, d_final, loss_row = _loss_head(x3, norms["final"], target, name="loss_head")

    def weight_grad(lhs, rhs, name, tie=None):
        return _mm(lhs, rhs, name=name, tm=256, tn=D_MODEL, ta=True, out_dtype=BF16, tie=tie)

    def sent(stage):
        return grads_ready(stage, gw) if grads_ready is not None else None

    gw, gv = {}, {"final": d_final}
    dx2, gv["ffn2"], da2, db2, dob2 = _ffn_bwd(dx3, x2, norms["ffn2"], a2, b2, w["g2t"], w["u2t"], w["d2"],
                                               name="ffn2_bwd")
    gw["g2t"] = weight_grad(da2, h2, "ffn2_dgate")
    gw["u2t"] = weight_grad(db2, h2, "ffn2_dup")
    gw["d2"] = weight_grad(s2, dob2, "ffn2_ddown")

    do_sb, do_hg, d_gate, gv["sb"], gv["hg"], dx2b = _mix_out_bwd(
        dx2, o_sb, o_hg, proj, norms["sb"], norms["hg"], w["out"], name="mix_out_bwd")
    gw["out"] = weight_grad(mixed, dx2b, "out_dw")
    tie = sent("mix")
    dq_sb, dk_sb, dv_sb = _attn_bwd(proj, sb_kept, do_sb, name="sb_attn_bwd", tie=tie)
    dq_hg, df_hg, di_hg, d_lb = _hgrn_bwd(proj, logits if tie is None else logits + tie[0, 0], states, do_hg,
                                          name="hgrn2_bwd")
    dproj = jnp.concatenate([dq_sb, dk_sb, dv_sb, dq_hg, df_hg, di_hg, d_gate], axis=1).astype(BF16)
    gw["int"] = weight_grad(dproj, hm, "in_dw")
    tie = sent("in")
    dhm = _mm(dproj, w["int"], name="in_dx", tm=512, tn=D_MODEL)
    dx1, gv["mix"] = _norm_bwd(dhm, x1, norms["mix"] if tie is None else norms["mix"] + tie[0, 0], dx2,
                               name="mix_norm_bwd")

    dx, gv["ffn1"], da1, db1, dob1 = _ffn_bwd(dx1, x, norms["ffn1"], a1, b1, w["g1t"], w["u1t"], w["d1"],
                                              name="ffn1_bwd")
    gw["g1t"] = weight_grad(da1, h1, "ffn1_dgate")
    gw["u1t"] = weight_grad(db1, h1, "ffn1_dup", tie=sent("g1t"))
    gw["d1"] = weight_grad(s1, dob1, "ffn1_ddown", tie=sent("u1t"))
    sent("d1")
    gv["lb"] = d_lb
    return loss_row, dx, gw, gv


HBM = pl.BlockSpec(memory_space=pl.ANY)


def _place():
    return lax.axis_index("x"), lax.axis_index("y"), lax.axis_index("c")


def _slot(px, py, pc):
    return 4 * px + 2 * py + pc


def _all_gather(blocks, *, name):
    n = len(blocks)

    def body(*refs):
        ins, outs = refs[:n], refs[n:2 * n]
        send_sems, recv_sems, local_sems = refs[2 * n:]
        x, y, c = _place()
        me, sibling = (x, y, c), (x, y, 1 - c)
        chips = [(1 - x, y), (x, 1 - y), (1 - x, 1 - y)]

        def copy(a, k, block, to, src=None):
            dst = outs[a].at[_slot(*block)]
            return pltpu.make_async_remote_copy(
                src_ref=dst if src is None else src, dst_ref=dst, send_sem=send_sems.at[7 * a + k],
                recv_sem=recv_sems.at[7 * a + k], device_id=to, device_id_type=MESH)

        mine = [pltpu.make_async_copy(ins[a], outs[a].at[_slot(*me)], local_sems.at[a]) for a in range(n)]
        for cp in mine:
            cp.start()
        first = []
        for a in range(n):
            first.append(copy(a, 0, me, sibling, src=ins[a]))
            first += [copy(a, 1 + j, me, (*chip, c), src=ins[a]) for j, chip in enumerate(chips)]
        for cp in first:
            cp.start()
        passed = []
        for j, chip in enumerate(chips):
            for a in range(n):
                copy(a, 1 + j, (*chip, c), me).wait_recv()
                fwd = copy(a, 4 + j, (*chip, c), sibling)
                fwd.start()
                passed.append(fwd)
        for a in range(n):
            copy(a, 0, sibling, me).wait_recv()
            for j, chip in enumerate(chips):
                copy(a, 4 + j, (*chip, 1 - c), me).wait_recv()
        for cp in first + passed:
            cp.wait_send()
        for cp in mine:
            cp.wait()

    return pl.pallas_call(
        body,
        name=name,
        in_specs=[HBM] * n,
        out_specs=[HBM] * n,
        out_shape=[jax.ShapeDtypeStruct((N_DEV,) + b.shape, b.dtype) for b in blocks],
        scratch_shapes=[pltpu.SemaphoreType.DMA((7 * n,)), pltpu.SemaphoreType.DMA((7 * n,)),
                        pltpu.SemaphoreType.DMA((n,))],
    )(*blocks)


def _flipped(place, d):
    return tuple(1 - p if (d >> (2 - axis)) & 1 else p for axis, p in enumerate(place))


SEM = pl.BlockSpec(memory_space=pltpu.SEMAPHORE)
EFFECT = pltpu.SideEffectType.DATAFLOW_SIDE_EFFECTING


def _split_copies(me, srcs, lands, send_sems, recv_sems, by_owner):
    copies = []
    for d in range(1, N_DEV):
        peer = _flipped(me, d)
        for a, (src, land) in enumerate(zip(srcs, lands)):
            copies.append(pltpu.make_async_remote_copy(
                src_ref=src.at[_slot(*peer)] if by_owner else src, dst_ref=land.at[_slot(*me)],
                send_sem=send_sems.at[7 * a + d - 1], recv_sem=recv_sems.at[7 * a + d - 1], device_id=peer,
                device_id_type=MESH))
    return copies


def _copies_start(srcs, *, name, by_owner, after=None):
    n = len(srcs)
    extra = [] if after is None else [after]
    land_shapes = [s.shape if by_owner else (N_DEV,) + s.shape for s in srcs]
    lands = [pltpu.with_memory_space_constraint(lax.empty(shape, s.dtype), pltpu.HBM) for shape, s in zip(land_shapes, srcs)]
    srcs = [pltpu.with_memory_space_constraint(s, pltpu.HBM) for s in srcs]

    def body(*refs):
        src_refs, land_refs = refs[:n], refs[n:2 * n]
        send_sems, recv_sems = refs[2 * n + len(extra)], refs[2 * n + len(extra) + 1]
        token = refs[-1]
        for cp in _split_copies(_place(), src_refs, land_refs, send_sems, recv_sems, by_owner):
            cp.start()
        token[...] = jnp.zeros_like(token)

    out = pl.pallas_call(
        body,
        name=name,
        in_specs=[HBM] * (2 * n + len(extra)),
        out_specs=[SEM, SEM] + [HBM] * (2 * n) + [pl.BlockSpec(memory_space=pltpu.VMEM)],
        out_shape=[pltpu.SemaphoreType.DMA((7 * n,)), pltpu.SemaphoreType.DMA((7 * n,))]
        + [pltpu.HBM(s.shape, s.dtype) for s in srcs] + [pltpu.HBM(shape, s.dtype) for shape, s in zip(land_shapes, srcs)]
        + [jax.ShapeDtypeStruct((8, LANES), F32)],
        input_output_aliases={i: 2 + i for i in range(2 * n)},
        compiler_params=pltpu.CompilerParams(has_side_effects=EFFECT),
    )(*srcs, *lands, *extra)
    return (out[0], out[1], out[2:2 + n], out[2 + n:2 + 2 * n]), out[-1]


def _copies_wait(started, after, *, name, by_owner):
    send_sems, recv_sems, srcs, lands = started
    n = len(srcs)

    def body(*refs):
        src_refs, land_refs = refs[:n], refs[n:2 * n]
        for cp in _split_copies(_place(), src_refs, land_refs, refs[2 * n], refs[2 * n + 1], by_owner):
            cp.wait_send()
            cp.wait_recv()

    out = pl.pallas_call(
        body,
        name=name,
        in_specs=[HBM] * (2 * n) + [SEM, SEM, HBM],
        out_specs=[HBM] * (2 * n),
        out_shape=[pltpu.HBM(s.shape, s.dtype) for s in srcs] + [pltpu.HBM(s.shape, s.dtype) for s in lands],
        input_output_aliases={i: i for i in range(2 * n)},
        compiler_params=pltpu.CompilerParams(has_side_effects=EFFECT),
    )(*srcs, *lands, send_sems, recv_sems, after)
    return out[n:]


def _with_own(lands, own, slot):
    zero = jnp.zeros((), jnp.int32)
    return [lax.dynamic_update_slice(land, o[None], (slot.astype(jnp.int32),) + (zero,) * o.ndim)
            for land, o in zip(lands, own)]


def _adamw(w, g, m, v):
    m = ADAM_B1 * m + (1.0 - ADAM_B1) * g
    v = ADAM_B2 * v + (1.0 - ADAM_B2) * (g * g)
    m_hat = m / (1.0 - ADAM_B1 ** ADAM_STEP)
    v_hat = v / (1.0 - ADAM_B2 ** ADAM_STEP)
    delta = -ADAM_LR * (m_hat / (jnp.sqrt(v_hat) + ADAM_EPS) + ADAM_WD * w)
    return delta, m, v


def _sum_and_update(parts, w, m, v, *, name, tie=None):
    _, rows, cols = w.shape
    tr = rows // 2

    def body(p_ref, w_ref, m_ref, v_ref, *rest):
        g_ref, d_ref, mo_ref, vo_ref = rest[-4:]
        g = p_ref[0].astype(F32)
        for s in range(1, N_DEV):
            g = g + p_ref[s].astype(F32)
        g_ref[0] = g
        d_ref[0], mo_ref[0], vo_ref[0] = _adamw(w_ref[0], g, m_ref[0], v_ref[0])

    flat = pl.BlockSpec((1, tr, cols), lambda i: (0, i, 0))
    return pl.pallas_call(
        body,
        name=name,
        grid=(rows // tr,),
        in_specs=[pl.BlockSpec((N_DEV, tr, cols), lambda i: (0, i, 0)), flat, flat, flat]
        + ([] if tie is None else [pl.BlockSpec(memory_space=pl.ANY)]),
        out_specs=[flat] * 4,
        out_shape=[jax.ShapeDtypeStruct((1, rows, cols), F32)] * 4,
        compiler_params=_params("parallel"),
    )(parts, w, m, v, *([] if tie is None else [tie]))


VEC_ROWS = 8
ROW_LOGITS, ROW_LOSS = 5, 7


def _vectors_update(part, w, m, v, *, name):
    def body(p_ref, w_ref, m_ref, v_ref, g_ref, d_ref, mo_ref, vo_ref, loss_ref, all_ref, send_sems, recv_sems):
        me = _place()
        all_ref[_slot(*me)] = p_ref[...]
        copies = []
        for d in range(1, N_DEV):
            peer = _flipped(me, d)
            copies.append(pltpu.make_async_remote_copy(
                src_ref=p_ref, dst_ref=all_ref.at[_slot(*me)], send_sem=send_sems.at[d - 1], recv_sem=recv_sems.at[d - 1],
                device_id=peer, device_id_type=MESH))
        for cp in copies:
            cp.start()
        for cp in copies:
            cp.wait()
        total = all_ref[0]
        for s in range(1, N_DEV):
            total = total + all_ref[s]
        wv = w_ref[...]
        half = D_MODEL // 2
        lb = _sigmoid(wv[ROW_LOGITS:ROW_LOGITS + 1, :half] - wv[ROW_LOGITS:ROW_LOGITS + 1, half:])
        d_first = total[ROW_LOGITS:ROW_LOGITS + 1, :half] * lb * (1.0 - lb)
        d_logits = jnp.concatenate([d_first, -d_first], axis=1)
        rowi = lax.broadcasted_iota(jnp.int32, (VEC_ROWS, D_MODEL), 0)
        g = jnp.where(rowi == ROW_LOGITS, d_logits, jnp.where(rowi < ROW_LOGITS, total, 0.0))
        g_ref[...] = g
        d_ref[...], mo_ref[...], vo_ref[...] = _adamw(wv, g, m_ref[...], v_ref[...])
        loss_ref[...] = total[ROW_LOSS:ROW_LOSS + 1, :]

    vmem = pl.BlockSpec(memory_space=pltpu.VMEM)
    return pl.pallas_call(
        body,
        name=name,
        in_specs=[vmem] * 4,
        out_specs=[vmem] * 5,
        out_shape=[jax.ShapeDtypeStruct((VEC_ROWS, D_MODEL), F32)] * 4 + [jax.ShapeDtypeStruct((1, D_MODEL), F32)],
        scratch_shapes=[pltpu.VMEM((N_DEV, VEC_ROWS, D_MODEL), F32), pltpu.SemaphoreType.DMA((7,)),
                        pltpu.SemaphoreType.DMA((7,))],
    )(part, w, m, v)


TRANSPOSED = ("g1t", "u1t", "g2t", "u2t", "int")


def _vector_rows(rows):
    rowi = lax.broadcasted_iota(jnp.int32, (VEC_ROWS, D_MODEL), 0)
    out = jnp.zeros((VEC_ROWS, D_MODEL), F32)
    for i, r in enumerate(rows):
        if r is not None:
            out = jnp.where(rowi == i, r, out)
    return out


def kernel(x, ffn1_norm, ffn1_w_gate, ffn1_w_up, ffn1_w_down, mix_norm, w_in, sb_out_norm, hg_lower_bound_logits, hg_out_norm, w_out, ffn2_norm, ffn2_w_gate, ffn2_w_up, ffn2_w_down, final_norm, loss_target, m_ffn1_norm, m_ffn1_w_gate, m_ffn1_w_up, m_ffn1_w_down, m_mix_norm, m_w_in, m_sb_out_norm, m_hg_lower_bound_logits, m_hg_out_norm, m_w_out, m_ffn2_norm, m_ffn2_w_gate, m_ffn2_w_up, m_ffn2_w_down, m_final_norm, v_ffn1_norm, v_ffn1_w_gate, v_ffn1_w_up, v_ffn1_w_down, v_mix_norm, v_w_in, v_sb_out_norm, v_hg_lower_bound_logits, v_hg_out_norm, v_w_out, v_ffn2_norm, v_ffn2_w_gate, v_ffn2_w_up, v_ffn2_w_down, v_final_norm):
    def matrices(g1, u1, d1, win, wout, g2, u2, d2):
        return {"g1t": g1, "u1t": u1, "d1": d1, "int": win, "out": wout, "g2t": g2, "u2t": u2, "d2": d2}

    def vectors(n1, nm, nsb, lg, nhg, n2, nf):
        return [n1, nm, n2, nf.reshape(1, D_MODEL), jnp.concatenate([nsb, nhg], axis=1), lg.reshape(1, D_MODEL), None, None]

    w_sh = matrices(ffn1_w_gate, ffn1_w_up, ffn1_w_down, w_in, w_out, ffn2_w_gate, ffn2_w_up, ffn2_w_down)
    m_sh = matrices(m_ffn1_w_gate, m_ffn1_w_up, m_ffn1_w_down, m_w_in, m_w_out, m_ffn2_w_gate, m_ffn2_w_up, m_ffn2_w_down)
    v_sh = matrices(v_ffn1_w_gate, v_ffn1_w_up, v_ffn1_w_down, v_w_in, v_w_out, v_ffn2_w_gate, v_ffn2_w_up, v_ffn2_w_down)
    keys = list(w_sh)

    slot = _slot(*_place())

    def full(key, stack):
        return stack.reshape(-1, D_MODEL)

    def by_owner(key, grad):
        return grad.reshape(N_DEV, -1, D_MODEL)

    def view(key, a):
        return jnp.swapaxes(a, 1, 2) if key in TRANSPOSED else a

    blocks = {k: view(k, w_sh[k])[0].astype(BF16) for k in keys}
    first, mid, last = ("g1t", "u1t", "d1"), ("int", "out"), ("g2t", "u2t", "d2")
    w_first = {k: full(k, s) for k, s in zip(first, _all_gather([blocks[k] for k in first], name="gather_ffn1"))}
    flights = {}
    flights["ffn1"], token_mid = _copies_start([blocks[k] for k in mid], name="gather_mid_start", by_owner=False,
                                               after=w_first["d1"])
    flights["mix"], token_last = _copies_start([blocks[k] for k in last], name="gather_ffn2_start", by_owner=False,
                                               after=token_mid)

    def weights_after(stage, result):
        group = mid if stage == "ffn1" else last
        lands = _copies_wait(flights[stage], result, name="gather_" + stage + "_wait", by_owner=False)
        return {k: full(k, s) for k, s in zip(group, _with_own(lands, [blocks[k] for k in group], slot))}

    groups = {"mix": ("g2t", "u2t", "d2", "out"), "in": ("int",), "g1t": ("g1t",), "u1t": ("u1t",), "d1": ("d1",)}
    sent, sent_tokens = {}, []

    def grads_ready(stage, gw):
        stacks = [by_owner(k, gw[k]) for k in groups[stage]]
        flight, token = _copies_start(stacks, name="grads_" + stage + "_start", by_owner=True)
        sent[stage] = (stacks, flight)
        sent_tokens.append(token)
        return token

    norms = {"ffn1": ffn1_norm + token_last[0, 0], "mix": mix_norm, "sb": sb_out_norm, "hg": hg_out_norm,
             "ffn2": ffn2_norm, "final": final_norm.reshape(1, D_MODEL)}
    loss_row, grad_x, gw, gv = _local_step(x[0], loss_target[0], norms, hg_lower_bound_logits, w_first, weights_after,
                                           grads_ready)

    updated, after = {}, sent_tokens[-1]
    for stage, (stacks, flight) in sent.items():
        lands = _copies_wait(flight, after, name="grads_" + stage + "_wait", by_owner=True)
        own = [lax.dynamic_index_in_dim(s, slot, keepdims=False) for s in stacks]
        for k, part in zip(groups[stage], _with_own(lands, own, slot)):
            updated[k] = _sum_and_update(part, view(k, w_sh[k]), view(k, m_sh[k]), view(k, v_sh[k]), name="adamw_" + k,
                                         tie=after)
            after = updated[k][0]
    mats = [{k: view(k, updated[k][i]) for k in keys} for i in range(4)]

    lb_row = jnp.concatenate([gv["lb"], jnp.zeros_like(gv["lb"])], axis=1)
    part = _vector_rows([gv["ffn1"], gv["mix"], gv["ffn2"], gv["final"], jnp.concatenate([gv["sb"], gv["hg"]], axis=1),
                         lb_row, None, loss_row])
    vec_w = _vector_rows(vectors(ffn1_norm, mix_norm, sb_out_norm, hg_lower_bound_logits, hg_out_norm, ffn2_norm, final_norm))
    vec_m = _vector_rows(vectors(m_ffn1_norm, m_mix_norm, m_sb_out_norm, m_hg_lower_bound_logits, m_hg_out_norm,
                                 m_ffn2_norm, m_final_norm))
    vec_v = _vector_rows(vectors(v_ffn1_norm, v_mix_norm, v_sb_out_norm, v_hg_lower_bound_logits, v_hg_out_norm,
                                 v_ffn2_norm, v_final_norm))
    *vecs, loss_out = _vectors_update(part, vec_w, vec_m, vec_v, name="vectors_update")

    def leaves(mat, vec):
        half = D_MODEL // 2
        return (
            vec[0:1], mat["g1t"], mat["u1t"], mat["d1"], vec[1:2], mat["int"], vec[4:5, :half],
            vec[ROW_LOGITS].reshape(2, half), vec[4:5, half:], mat["out"], vec[2:3], mat["g2t"], mat["u2t"],
            mat["d2"], vec[3],
        )

    out = [loss_out[0, 0], grad_x[None]]
    for mat, vec in zip(mats, vecs):
        out.extend(leaves(mat, vec))
    return tuple(out)
```

```python
import jax
import jax.numpy as jnp
from jax import lax
from jax.experimental import pallas as pl
from jax.experimental.pallas import tpu as pltpu

F32, BF16 = jnp.float32, jnp.bfloat16
D_MODEL = 1024
D_FF = 2816
SB_WIDTH = 512
HG_WIDTH = 512
SB_HEAD_DIM = 64
HG_HEAD_DIM = 128
IN_COLS = 3584
EPS = 1e-6
N_DEV = 8
LANES = 128
HG_CHUNK = 16
VMEM_LIMIT_BYTES = 48 * 1024 * 1024
FFN_BWD_VMEM_LIMIT_BYTES = 56 * 1024 * 1024
ADAM_LR, ADAM_B1, ADAM_B2, ADAM_EPS, ADAM_WD, ADAM_STEP = 0.001, 0.9, 0.999, 1e-08, 0.01, 10
MESH = pl.DeviceIdType.MESH


def _params(*semantics, vmem_limit_bytes=VMEM_LIMIT_BYTES):
    return pltpu.CompilerParams(dimension_semantics=semantics, vmem_limit_bytes=vmem_limit_bytes)


def _dot(a, b):
    return jnp.dot(a, b, preferred_element_type=F32)


def _dot_nt(a, b):
    return lax.dot_general(a, b, (((1,), (1,)), ((), ())), preferred_element_type=F32)


def _dot_tn(a, b):
    return lax.dot_general(a, b, (((0,), (0,)), ((), ())), preferred_element_type=F32)


def _split3(x):
    hi = x.astype(BF16)
    r1 = x - hi.astype(F32)
    mid = r1.astype(BF16)
    lo = (r1 - mid.astype(F32)).astype(BF16)
    return hi, mid, lo


def _rms(xv):
    rstd = lax.rsqrt(jnp.mean(xv * xv, axis=-1, keepdims=True) + EPS)
    return xv * rstd, rstd


def _sigmoid(x):
    return 1.0 / (1.0 + jnp.exp(-x))


def _mm(a, b, *, name, tm, tn, nt=False, ta=False, out_dtype=F32, tie=None):
    k, m = a.shape if ta else a.shape[::-1]
    n = b.shape[0] if nt else b.shape[1]
    assert m % tm == 0 and n % tn == 0 and not (nt and ta), (name, a.shape, b.shape, tm, tn)

    def body(a_ref, b_ref, *rest):
        av = a_ref[...].astype(BF16)
        bv = b_ref[...].astype(BF16)
        rest[-1][...] = (_dot_nt(av, bv) if nt else _dot_tn(av, bv) if ta else _dot(av, bv)).astype(out_dtype)

    in_specs = [
        pl.BlockSpec((k, tm), lambda i, j: (0, i)) if ta else pl.BlockSpec((tm, k), lambda i, j: (i, 0)),
        pl.BlockSpec((tn, k), lambda i, j: (j, 0)) if nt else pl.BlockSpec((k, tn), lambda i, j: (0, j)),
    ]
    operands = [a, b]
    if tie is not None:
        in_specs.append(pl.BlockSpec(memory_space=pl.ANY))
        operands.append(tie)
    return pl.pallas_call(
        body,
        name=name,
        grid=(m // tm, n // tn),
        in_specs=in_specs,
        out_specs=pl.BlockSpec((tm, tn), lambda i, j: (i, j)),
        out_shape=jax.ShapeDtypeStruct((m, n), out_dtype),
        compiler_params=_params("parallel", "parallel"),
    )(*operands)


def _ffn_fwd(x, gain, wgt, wut, wd, *, name, tm=1024, tf=256):
    t = x.shape[0]
    nj = D_FF // tf

    def body(x_ref, g_ref, wg_ref, wu_ref, wd_prev_ref, wd_last_ref, xo_ref, a_ref, b_ref, h_ref, st_ref, acc, s_prev):
        j = pl.program_id(1)

        @pl.when(j == 0)
        def _():
            xhat, _ = _rms(x_ref[...])
            h_ref[...] = (xhat * g_ref[...]).astype(BF16)
            acc[...] = jnp.zeros_like(acc)
            s_prev[...] = jnp.zeros_like(s_prev)

        acc[...] += _dot(s_prev[...], wd_prev_ref[...])
        h = h_ref[...]
        a = _dot_nt(h, wg_ref[...])
        b = _dot_nt(h, wu_ref[...])
        a_ref[...] = a.astype(BF16)
        b_ref[...] = b.astype(BF16)
        s = (a * _sigmoid(a) * b).astype(BF16)
        st_ref[...] = s
        s_prev[...] = s

        @pl.when(j == nj - 1)
        def _():
            xo_ref[...] = x_ref[...] + 0.5 * (acc[...] + _dot(s, wd_last_ref[...]))

    return pl.pallas_call(
        body,
        name=name,
        grid=(t // tm, nj),
        in_specs=[
            pl.BlockSpec((tm, D_MODEL), lambda i, j: (i, 0)),
            pl.BlockSpec((1, D_MODEL), lambda i, j: (0, 0)),
            pl.BlockSpec((tf, D_MODEL), lambda i, j: (j, 0)),
            pl.BlockSpec((tf, D_MODEL), lambda i, j: (j, 0)),
            pl.BlockSpec((tf, D_MODEL), lambda i, j: (jnp.maximum(j - 1, 0), 0)),
            pl.BlockSpec((tf, D_MODEL), lambda i, j: (nj - 1, 0)),
        ],
        out_specs=[
            pl.BlockSpec((tm, D_MODEL), lambda i, j: (i, 0)),
            pl.BlockSpec((tm, tf), lambda i, j: (i, j)),
            pl.BlockSpec((tm, tf), lambda i, j: (i, j)),
            pl.BlockSpec((tm, D_MODEL), lambda i, j: (i, 0)),
            pl.BlockSpec((tm, tf), lambda i, j: (i, j)),
        ],
        out_shape=[
            jax.ShapeDtypeStruct((t, D_MODEL), F32),
            jax.ShapeDtypeStruct((t, D_FF), BF16),
            jax.ShapeDtypeStruct((t, D_FF), BF16),
            jax.ShapeDtypeStruct((t, D_MODEL), BF16),
            jax.ShapeDtypeStruct((t, D_FF), BF16),
        ],
        scratch_shapes=[pltpu.VMEM((tm, D_MODEL), F32), pltpu.VMEM((tm, tf), BF16)],
        compiler_params=_params("parallel", "arbitrary"),
    )(x, gain, wgt, wut, wd, wd)


def _ffn_bwd(dout, x, gain, a, b, wgt, wut, wd, *, name, tm=1024, tf=256):
    t = x.shape[0]
    nj = D_FF // tf

    def body(do_ref, x_ref, g_ref, a_ref, b_ref, wg_prev_ref, wu_prev_ref, wg_last_ref, wu_last_ref, wd_ref,
             dx_ref, dg_ref, da_ref, db_ref, dob_ref, dob_scr, dh, da_prev, db_prev):
        i = pl.program_id(0)
        j = pl.program_id(1)

        @pl.when(j == 0)
        def _():
            d = (0.5 * do_ref[...]).astype(BF16)
            dob_scr[...] = d
            dob_ref[...] = d
            dh[...] = jnp.zeros_like(dh)
            da_prev[...] = jnp.zeros_like(da_prev)
            db_prev[...] = jnp.zeros_like(db_prev)

        dh[...] += _dot(da_prev[...], wg_prev_ref[...]) + _dot(db_prev[...], wu_prev_ref[...])
        ds = _dot_nt(dob_scr[...], wd_ref[...])
        av = a_ref[...].astype(F32)
        bv = b_ref[...].astype(F32)
        sig = _sigmoid(av)
        dbv = (ds * (av * sig)).astype(BF16)
        dav = (ds * bv * (sig * (1.0 + av * (1.0 - sig)))).astype(BF16)
        da_ref[...] = dav
        db_ref[...] = dbv
        da_prev[...] = dav
        db_prev[...] = dbv

        @pl.when(j == nj - 1)
        def _():
            xhat, rstd = _rms(x_ref[...])
            dhv = dh[...] + _dot(dav, wg_last_ref[...]) + _dot(dbv, wu_last_ref[...])
            part = jnp.sum(dhv * xhat, axis=0, keepdims=True)

            @pl.when(i == 0)
            def _():
                dg_ref[...] = part

            @pl.when(i > 0)
            def _():
                dg_ref[...] += part

            dxh = dhv * g_ref[...]
            dx_ref[...] = do_ref[...] + rstd * (dxh - xhat * jnp.mean(dxh * xhat, axis=-1, keepdims=True))

    return pl.pallas_call(
        body,
        name=name,
        grid=(t // tm, nj),
        in_specs=[
            pl.BlockSpec((tm, D_MODEL), lambda i, j: (i, 0)),
            pl.BlockSpec((tm, D_MODEL), lambda i, j: (i, 0)),
            pl.BlockSpec((1, D_MODEL), lambda i, j: (0, 0)),
            pl.BlockSpec((tm, tf), lambda i, j: (i, j)),
            pl.BlockSpec((tm, tf), lambda i, j: (i, j)),
            pl.BlockSpec((tf, D_MODEL), lambda i, j: (jnp.maximum(j - 1, 0), 0)),
            pl.BlockSpec((tf, D_MODEL), lambda i, j: (jnp.maximum(j - 1, 0), 0)),
            pl.BlockSpec((tf, D_MODEL), lambda i, j: (nj - 1, 0)),
            pl.BlockSpec((tf, D_MODEL), lambda i, j: (nj - 1, 0)),
            pl.BlockSpec((tf, D_MODEL), lambda i, j: (j, 0)),
        ],
        out_specs=[
            pl.BlockSpec((tm, D_MODEL), lambda i, j: (i, 0)),
            pl.BlockSpec((1, D_MODEL), lambda i, j: (0, 0)),
            pl.BlockSpec((tm, tf), lambda i, j: (i, j)),
            pl.BlockSpec((tm, tf), lambda i, j: (i, j)),
            pl.BlockSpec((tm, D_MODEL), lambda i, j: (i, 0)),
        ],
        out_shape=[
            jax.ShapeDtypeStruct((t, D_MODEL), F32),
            jax.ShapeDtypeStruct((1, D_MODEL), F32),
            jax.ShapeDtypeStruct((t, D_FF), BF16),
            jax.ShapeDtypeStruct((t, D_FF), BF16),
            jax.ShapeDtypeStruct((t, D_MODEL), BF16),
        ],
        scratch_shapes=[pltpu.VMEM((tm, D_MODEL), BF16), pltpu.VMEM((tm, D_MODEL), F32), pltpu.VMEM((tm, tf), BF16),
                        pltpu.VMEM((tm, tf), BF16)],
        compiler_params=_params("arbitrary", "arbitrary", vmem_limit_bytes=FFN_BWD_VMEM_LIMIT_BYTES),
    )(dout, x, gain, a, b, wgt, wut, wgt, wut, wd)


def _norm_fwd(x, gain, *, name, tm=512):
    t = x.shape[0]

    def body(x_ref, g_ref, h_ref):
        xhat, _ = _rms(x_ref[...])
        h_ref[...] = (xhat * g_ref[...]).astype(BF16)

    return pl.pallas_call(
        body,
        name=name,
        grid=(t // tm,),
        in_specs=[pl.BlockSpec((tm, D_MODEL), lambda i: (i, 0)), pl.BlockSpec((1, D_MODEL), lambda i: (0, 0))],
        out_specs=pl.BlockSpec((tm, D_MODEL), lambda i: (i, 0)),
        out_shape=jax.ShapeDtypeStruct((t, D_MODEL), BF16),
        compiler_params=_params("parallel"),
    )(x, gain)


def _norm_bwd(dh, x, gain, dres, *, name, tm=512):
    t = x.shape[0]

    def body(dh_ref, x_ref, g_ref, dr_ref, dx_ref, dg_ref):
        i = pl.program_id(0)
        xhat, rstd = _rms(x_ref[...])
        dhv = dh_ref[...]
        part = jnp.sum(dhv * xhat, axis=0, keepdims=True)

        @pl.when(i == 0)
        def _():
            dg_ref[...] = part

        @pl.when(i > 0)
        def _():
            dg_ref[...] += part

        dxh = dhv * g_ref[...]
        dx_ref[...] = dr_ref[...] + rstd * (dxh - xhat * jnp.mean(dxh * xhat, axis=-1, keepdims=True))

    row = pl.BlockSpec((tm, D_MODEL), lambda i: (i, 0))
    vec = pl.BlockSpec((1, D_MODEL), lambda i: (0, 0))
    return pl.pallas_call(
        body,
        name=name,
        grid=(t // tm,),
        in_specs=[row, row, vec, row],
        out_specs=[row, vec],
        out_shape=[jax.ShapeDtypeStruct((t, D_MODEL), F32), jax.ShapeDtypeStruct((1, D_MODEL), F32)],
        compiler_params=_params("arbitrary"),
    )(dh, x, gain, dres)


ATT_Q_TILE = 512
ATT_K_BLOCK = 256


def _first_head_lanes():
    return lax.broadcasted_iota(jnp.int32, (1, LANES), 1) < SB_HEAD_DIM


def _stack_heads(x):
    first = _first_head_lanes()
    return jnp.concatenate([jnp.where(first, x, 0.0), jnp.where(first, 0.0, x)], axis=0)


def _unstack_heads(x, rows):
    return jnp.where(_first_head_lanes(), x[:rows], x[rows:])


def _rows_from(x, first, rows):
    return x if first == 0 else jnp.concatenate([x[first:rows], x[rows + first:]], axis=0)


def _rows_into(full, part, first, rows):
    if first == 0:
        return part
    n = rows - first
    return jnp.concatenate([full[:first], part[:n], full[rows:rows + first], part[n:]], axis=0)


def _tri(n, relation):
    r = lax.broadcasted_iota(jnp.int32, (n, n), 0)
    c = lax.broadcasted_iota(jnp.int32, (n, n), 1)
    return relation(r, c).astype(BF16)


def _scan_dot(x, tri):
    hi = x.astype(BF16)
    lo = (x - hi.astype(F32)).astype(BF16)
    return _dot(jnp.concatenate([hi, lo], axis=1), jnp.concatenate([tri, tri], axis=0))


def _log_terms(z):
    lbeta = jnp.minimum(z, 0.0) - jnp.log(1.0 + jnp.exp(-jnp.abs(z)))
    return lbeta, lbeta - z


def _attn_fwd(proj, *, name):
    t = proj.shape[0]
    tq, tk = ATT_Q_TILE, ATT_K_BLOCK
    diag = tq // tk
    n_pairs = SB_WIDTH // LANES

    def body(q_ref, k_ref, v_ref, o_ref, kept_ref):
        qi = pl.program_id(1)
        q = q_ref[...] * (SB_HEAD_DIM ** -0.5)
        qs = _stack_heads(q).astype(BF16)
        tri = _tri(tk, lambda j, s: j > s)
        trow = lax.broadcasted_iota(jnp.int32, (tq, tk), 0)
        scol = lax.broadcasted_iota(jnp.int32, (tq, tk), 1)

        def block(kb, carry, causal, first=0):
            acc, c = carry
            off = pl.multiple_of(kb * tk, tk)
            lbeta, lrest = _log_terms(_dot_nt(_rows_from(qs, first, tq), k_ref[pl.ds(off, tk), :].astype(BF16)))
            if causal is not None:
                lrest = jnp.where(causal, lrest, 0.0)
            w = jnp.exp(lbeta + (_scan_dot(lrest, tri) + _rows_from(c, first, tq)))
            if causal is not None:
                w = jnp.where(causal, w, 0.0)
            wb = w.astype(BF16)
            kept_ref[0, 0, kb] = _rows_into(jnp.zeros((2 * tq, tk), BF16), wb, first, tq)
            acc = _rows_into(acc, _rows_from(acc, first, tq) + _dot(wb, v_ref[pl.ds(off, tk), :].astype(BF16)), first, tq)
            return acc, _rows_into(c, _rows_from(c, first, tq) + jnp.sum(lrest, axis=1, keepdims=True), first, tq)

        carry = (jnp.zeros((2 * tq, LANES), F32), jnp.zeros((2 * tq, 1), F32))
        n_full = qi * diag
        for j in reversed(range(diag)):
            mask = ((scol + j * tk) < trow)[j * tk:]
            carry = block(n_full + j, carry, jnp.concatenate([mask, mask], axis=0), first=j * tk)

        def step(it, carry):
            return block(n_full - 1 - it, carry, None)

        acc, _ = lax.fori_loop(0, n_full, step, carry)
        o_ref[...] = _unstack_heads(acc, tq)

    return pl.pallas_call(
        body,
        name=name,
        grid=(n_pairs, t // tq),
        in_specs=[
            pl.BlockSpec((tq, LANES), lambda p, i: (i, p)),
            pl.BlockSpec((t, LANES), lambda p, i: (0, n_pairs + p)),
            pl.BlockSpec((t, LANES), lambda p, i: (0, 2 * n_pairs + p)),
        ],
        out_specs=[pl.BlockSpec((tq, LANES), lambda p, i: (i, p)),
                   pl.BlockSpec((1, 1, t // tk, 2 * tq, tk), lambda p, i: (p, i, 0, 0, 0))],
        out_shape=[jax.ShapeDtypeStruct((t, SB_WIDTH), F32),
                   jax.ShapeDtypeStruct((n_pairs, t // tq, t // tk, 2 * tq, tk), BF16)],
        compiler_params=_params("parallel", "parallel"),
    )(proj, proj, proj)


def _attn_bwd(proj, kept, do, *, name, tie=None):
    t = proj.shape[0]
    tq, tk = ATT_Q_TILE, ATT_K_BLOCK
    diag = tq // tk
    n_pairs = SB_WIDTH // LANES
    scale = SB_HEAD_DIM ** -0.5

    def body(q_ref, k_ref, v_ref, kept_ref, do_ref, *rest):
        dq_ref, dk_ref, dv_ref = rest[-3:]
        qi = pl.program_id(1)

        @pl.when(qi == 0)
        def _():
            dk_ref[...] = jnp.zeros_like(dk_ref)
            dv_ref[...] = jnp.zeros_like(dv_ref)

        qs = _stack_heads(q_ref[...] * scale).astype(BF16)
        dos = _stack_heads(do_ref[...]).astype(BF16)
        before = _tri(tk, lambda s, j: s < j)
        trow = lax.broadcasted_iota(jnp.int32, (tq, tk), 0)
        scol = lax.broadcasted_iota(jnp.int32, (tq, tk), 1)

        def block(kb, carry, causal, first=0):
            dq, cg = carry
            off = pl.multiple_of(kb * tk, tk)
            q_rows, do_rows = _rows_from(qs, first, tq), _rows_from(dos, first, tq)
            wb = _rows_from(kept_ref[0, 0, kb], first, tq)
            kblk = k_ref[pl.ds(off, tk), :].astype(BF16)
            sig = 0.5 + 0.5 * jnp.tanh(0.5 * _dot_nt(q_rows, kblk))
            g = wb.astype(F32) * _dot_nt(do_rows, v_ref[pl.ds(off, tk), :].astype(BF16))
            prior = _scan_dot(g, before) + _rows_from(cg, first, tq)
            dz = g - sig * (g + prior)
            if causal is not None:
                dz = jnp.where(causal, dz, 0.0)
            dzb = dz.astype(BF16)
            dq = _rows_into(dq, _rows_from(dq, first, tq) + _dot(dzb, kblk), first, tq)
            dk_ref[pl.ds(off, tk), :] += _dot_tn(dzb, q_rows)
            dv_ref[pl.ds(off, tk), :] += _dot_tn(wb, do_rows)
            return dq, _rows_into(cg, _rows_from(cg, first, tq) + jnp.sum(g, axis=1, keepdims=True), first, tq)

        n_full = qi * diag
        carry = lax.fori_loop(0, n_full, lambda kb, carry: block(kb, carry, None),
                              (jnp.zeros((2 * tq, LANES), F32), jnp.zeros((2 * tq, 1), F32)))
        for j in range(diag):
            mask = ((scol + j * tk) < trow)[j * tk:]
            carry = block(n_full + j, carry, jnp.concatenate([mask, mask], axis=0), first=j * tk)
        dq_ref[...] = (_unstack_heads(carry[0], tq) * scale).astype(BF16)

    tile_spec = pl.BlockSpec((tq, LANES), lambda p, i: (i, p))
    full_spec = pl.BlockSpec((t, LANES), lambda p, i: (0, p))
    return pl.pallas_call(
        body,
        name=name,
        grid=(n_pairs, t // tq),
        in_specs=[
            tile_spec,
            pl.BlockSpec((t, LANES), lambda p, i: (0, n_pairs + p)),
            pl.BlockSpec((t, LANES), lambda p, i: (0, 2 * n_pairs + p)),
            pl.BlockSpec((1, 1, t // tk, 2 * tq, tk), lambda p, i: (p, i, 0, 0, 0)),
            tile_spec,
        ] + ([] if tie is None else [pl.BlockSpec(memory_space=pl.ANY)]),
        out_specs=[tile_spec, full_spec, full_spec],
        out_shape=[jax.ShapeDtypeStruct((t, SB_WIDTH), BF16)] + [jax.ShapeDtypeStruct((t, SB_WIDTH), F32)] * 2,
        compiler_params=_params("arbitrary", "arbitrary"),
    )(proj, proj, proj, kept, do, *([] if tie is None else [tie]))


HG_BLOCK = 128
HG_HEADS = HG_WIDTH // HG_HEAD_DIM


def _chunk_mats(n):
    r = lax.broadcasted_iota(jnp.int32, (n, n), 0)
    c = lax.broadcasted_iota(jnp.int32, (n, n), 1)
    same = (r // HG_CHUNK) == (c // HG_CHUNK)
    upto = (same & (c <= r)).astype(BF16)
    whole = same.astype(BF16)
    onward = (same & (c >= r)).astype(BF16)
    return upto, whole, onward


def _rows_dot(mat, x):
    return _dot(jnp.concatenate([mat, mat, mat], axis=1), jnp.concatenate(_split3(x), axis=0))


def _split_heads(x):
    return jnp.stack([x[:, h * HG_HEAD_DIM:(h + 1) * HG_HEAD_DIM] for h in range(HG_HEADS)], axis=0)


def _merge_heads(x):
    return jnp.concatenate([x[h] for h in range(HG_HEADS)], axis=1)


def _lower_bound(lg_ref):
    lg = lg_ref[...]
    return _sigmoid(lg[0:1, :] - lg[1:2, :])


def _hgrn_prepare(q_ref, f_ref, lb, h, upto, whole):
    cols = slice(h * HG_HEAD_DIM, (h + 1) * HG_HEAD_DIM)
    lbh = lb[:, cols]
    sg = _sigmoid(f_ref[:, cols])
    forget = lbh + (1.0 - lbh) * sg
    logf = jnp.log(forget)
    kk = (1.0 - lbh) * (1.0 - sg)
    qv = q_ref[:, cols]
    qsig = _sigmoid(qv)
    qh = qv * qsig
    b = _rows_dot(upto, logf)
    blast = _rows_dot(whole, logf)
    return dict(lbh=lbh, sg=sg, forget=forget, kk=kk, qv=qv, qsig=qsig, qh=qh, b=b, eb=jnp.exp(b),
                ekb=jnp.exp(blast - b), dl=jnp.exp(blast))


def _hgrn_fwd(proj, logits, *, name):
    t = proj.shape[0]
    tb = HG_BLOCK
    nc = tb // HG_CHUNK
    hd = HG_HEAD_DIM

    def body(q_ref, f_ref, i_ref, lg_ref, o_ref, st_ref, state, qh_s, kk_s, b_s, qe_s, ke_s, dl_s):
        @pl.when(pl.program_id(0) == 0)
        def _():
            state[...] = jnp.zeros_like(state)

        lb = _lower_bound(lg_ref)
        upto, whole, _ = _chunk_mats(tb)
        for h in range(HG_HEADS):
            p = _hgrn_prepare(q_ref, f_ref, lb, h, upto, whole)
            qh_s[h] = p["qh"]
            kk_s[h] = p["kk"]
            b_s[h] = p["b"]
            qe_s[h] = (p["qh"] * p["eb"]).astype(BF16)
            ke_s[h] = (p["kk"] * p["ekb"]).astype(BF16)
            dl_s[h] = p["dl"]
        rowi = lax.broadcasted_iota(jnp.int32, (HG_HEADS, HG_CHUNK, hd), 1)

        def chunk(c, _):
            r0 = pl.multiple_of(c * HG_CHUNK, HG_CHUNK)
            rows = pl.ds(r0, HG_CHUNK)
            bc = b_s[:, rows, :]
            qc = qh_s[:, rows, :]
            kc = kk_s[:, rows, :]
            vc = _split_heads(i_ref[rows, :])
            s_in = state[...]
            st_ref[c] = s_in
            s_in_b = s_in.astype(BF16)
            qe = qe_s[:, rows, :]
            o = jnp.stack([_dot_nt(qe[h], s_in_b[h]) for h in range(HG_HEADS)], axis=0)
            for s in range(HG_CHUNK):
                pair = jnp.where(rowi >= s, qc * jnp.exp(bc - bc[:, s:s + 1, :]) * kc[:, s:s + 1, :], 0.0)
                o = o + jnp.sum(pair, axis=2, keepdims=True) * vc[:, s:s + 1, :]
            o_ref[rows, :] = _merge_heads(o)
            vcb = vc.astype(BF16)
            ke = ke_s[:, rows, :]
            update = jnp.stack([_dot_tn(vcb[h], ke[h]) for h in range(HG_HEADS)], axis=0)
            state[...] = s_in * dl_s[:, pl.ds(r0, 1), :] + update
            return 0

        lax.fori_loop(0, nc, chunk, 0)

    blk = lambda col: pl.BlockSpec((tb, HG_WIDTH), lambda i: (i, col))
    head_f32 = pltpu.VMEM((HG_HEADS, tb, hd), F32)
    head_bf16 = pltpu.VMEM((HG_HEADS, tb, hd), BF16)
    return pl.pallas_call(
        body,
        name=name,
        grid=(t // tb,),
        in_specs=[blk(3), blk(4), blk(5), pl.BlockSpec((2, HG_WIDTH), lambda i: (0, 0))],
        out_specs=[
            pl.BlockSpec((tb, HG_WIDTH), lambda i: (i, 0)),
            pl.BlockSpec((nc, HG_HEADS, hd, hd), lambda i: (i, 0, 0, 0)),
        ],
        out_shape=[
            jax.ShapeDtypeStruct((t, HG_WIDTH), F32),
            jax.ShapeDtypeStruct((t // HG_CHUNK, HG_HEADS, hd, hd), F32),
        ],
        scratch_shapes=[pltpu.VMEM((HG_HEADS, hd, hd), F32), head_f32, head_f32, head_f32, head_bf16, head_bf16,
                        head_f32],
        compiler_params=_params("arbitrary"),
    )(proj, proj, proj, logits)


def _hgrn_bwd(proj, logits, states, do, *, name):
    t = proj.shape[0]
    tb = HG_BLOCK
    nb = t // tb
    nc = tb // HG_CHUNK
    hd = HG_HEAD_DIM

    def body(q_ref, f_ref, i_ref, lg_ref, st_ref, do_ref, dq_ref, df_ref, di_ref, dlb_ref,
             dstate, qh_s, kk_s, b_s, eb_s, ekb_s, qe_s, ke_s, dl_s, dqh_s, dkk_s, dlf_s):
        step = pl.program_id(0)

        @pl.when(step == 0)
        def _():
            dstate[...] = jnp.zeros_like(dstate)
            dlb_ref[...] = jnp.zeros_like(dlb_ref)

        lb = _lower_bound(lg_ref)
        upto, whole, _ = _chunk_mats(tb)
        prepared = []
        for h in range(HG_HEADS):
            p = _hgrn_prepare(q_ref, f_ref, lb, h, upto, whole)
            prepared.append(p)
            qh_s[h] = p["qh"]
            kk_s[h] = p["kk"]
            b_s[h] = p["b"]
            eb_s[h] = p["eb"]
            ekb_s[h] = p["ekb"]
            qe_s[h] = (p["qh"] * p["eb"]).astype(BF16)
            ke_s[h] = (p["kk"] * p["ekb"]).astype(BF16)
            dl_s[h] = p["dl"]
        rowi = lax.broadcasted_iota(jnp.int32, (HG_CHUNK, hd), 0)
        r16 = lax.broadcasted_iota(jnp.int32, (HG_CHUNK, HG_CHUNK), 0)
        c16 = lax.broadcasted_iota(jnp.int32, (HG_CHUNK, HG_CHUNK), 1)
        onward = (c16 >= r16).astype(BF16)

        def chunk(it, _):
            c = nc - 1 - it
            r0 = pl.multiple_of(c * HG_CHUNK, HG_CHUNK)
            rows = pl.ds(r0, HG_CHUNK)
            for h in range(HG_HEADS):
                cols = slice(h * hd, (h + 1) * hd)
                bc = b_s[h, rows, :]
                qc = qh_s[h, rows, :]
                kc = kk_s[h, rows, :]
                vc = i_ref[rows, cols]
                doc = do_ref[rows, cols]
                s_in = st_ref[c, h]
                ds_out = dstate[h]
                ds_out_b = ds_out.astype(BF16)
                docb = doc.astype(BF16)
                dl_row = dl_s[h, pl.ds(r0, 1), :]
                dqh = _dot(docb, s_in.astype(BF16)) * eb_s[h, rows, :]
                dkk = _dot(vc.astype(BF16), ds_out_b) * ekb_s[h, rows, :]
                dv = _dot_nt(ke_s[h, rows, :], ds_out_b)
                db = dqh * qc - dkk * kc
                dwhole = jnp.sum(dkk * kc, axis=0, keepdims=True) + jnp.sum(ds_out * s_in, axis=0, keepdims=True) * dl_row
                dk_rows, dv_rows = [], []
                for s in range(HG_CHUNK):
                    keep = rowi >= s
                    e = jnp.exp(bc - bc[s:s + 1, :])
                    k_row = kc[s:s + 1, :]
                    pcol = jnp.sum(jnp.where(keep, qc * e * k_row, 0.0), axis=1, keepdims=True)
                    dpcol = jnp.sum(doc * vc[s:s + 1, :], axis=1, keepdims=True)
                    m = jnp.where(keep, e * dpcol, 0.0)
                    y = m * qc
                    dqh = dqh + m * k_row
                    db = db + y * k_row
                    dk_rows.append(jnp.sum(y, axis=0, keepdims=True))
                    dv_rows.append(jnp.sum(pcol * doc, axis=0, keepdims=True))
                dkk_pairs = jnp.concatenate(dk_rows, axis=0)
                dkk = dkk + dkk_pairs
                db = db - dkk_pairs * kc
                dv = dv + jnp.concatenate(dv_rows, axis=0)
                dqh_s[h, rows, :] = dqh
                dkk_s[h, rows, :] = dkk
                dlf_s[h, rows, :] = _rows_dot(onward, db) + dwhole
                di_ref[rows, cols] = dv.astype(BF16)
                dstate[h] = ds_out * dl_row + _dot_tn(docb, qe_s[h, rows, :])
            return 0

        lax.fori_loop(0, nc, chunk, 0)
        for h in range(HG_HEADS):
            cols = slice(h * hd, (h + 1) * hd)
            p = prepared[h]
            dq_ref[:, cols] = (dqh_s[h] * (p["qsig"] * (1.0 + p["qv"] * (1.0 - p["qsig"])))).astype(BF16)
            dforget = dlf_s[h] / p["forget"] - dkk_s[h]
            df_ref[:, cols] = (dforget * (1.0 - p["lbh"]) * p["sg"] * (1.0 - p["sg"])).astype(BF16)
            dlb_ref[:, cols] += jnp.sum(dforget * (1.0 - p["sg"]), axis=0, keepdims=True)

    blk = lambda col: pl.BlockSpec((tb, HG_WIDTH), lambda i: (nb - 1 - i, col))
    vec = pl.BlockSpec((1, HG_WIDTH), lambda i: (0, 0))
    head_f32 = pltpu.VMEM((HG_HEADS, tb, hd), F32)
    head_bf16 = pltpu.VMEM((HG_HEADS, tb, hd), BF16)
    return pl.pallas_call(
        body,
        name=name,
        grid=(nb,),
        in_specs=[
            blk(3), blk(4), blk(5),
            pl.BlockSpec((2, HG_WIDTH), lambda i: (0, 0)),
            pl.BlockSpec((nc, HG_HEADS, hd, hd), lambda i: (nb - 1 - i, 0, 0, 0)),
            blk(0),
        ],
        out_specs=[blk(0), blk(0), blk(0), vec],
        out_shape=[jax.ShapeDtypeStruct((t, HG_WIDTH), BF16)] * 3 + [jax.ShapeDtypeStruct((1, HG_WIDTH), F32)],
        scratch_shapes=[
            pltpu.VMEM((HG_HEADS, hd, hd), F32),
            head_f32, head_f32, head_f32, head_f32, head_f32, head_bf16, head_bf16, head_f32,
            head_f32, head_f32, head_f32,
        ],
        compiler_params=_params("arbitrary"),
    )(proj, proj, proj, logits, states, do)


def _group_mat(width, head_dim):
    r = lax.broadcasted_iota(jnp.int32, (width, width), 0)
    c = lax.broadcasted_iota(jnp.int32, (width, width), 1)
    return ((r // head_dim) == (c // head_dim)).astype(BF16)


def _head_mean(x, mat, head_dim):
    hi = x.astype(BF16)
    lo = (x - hi.astype(F32)).astype(BF16)
    return (_dot(hi, mat) + _dot(lo, mat)) * (1.0 / head_dim)


def _mix_out_fwd(o_sb, o_hg, proj, g_sb, g_hg, w_out, x1, *, name, tm=512):
    t = x1.shape[0]

    def body(osb_ref, ohg_ref, gate_ref, gsb_ref, ghg_ref, w_ref, x_ref, xo_ref, mt_ref):
        msb = _group_mat(SB_WIDTH, SB_HEAD_DIM)
        mhg = _group_mat(HG_WIDTH, HG_HEAD_DIM)
        osb = osb_ref[...]
        ohg = ohg_ref[...]
        nsb = osb * lax.rsqrt(_head_mean(osb * osb, msb, SB_HEAD_DIM) + EPS) * gsb_ref[...]
        gate = gate_ref[...]
        nhg = ohg * lax.rsqrt(_head_mean(ohg * ohg, mhg, HG_HEAD_DIM) + EPS) * ghg_ref[...] * (gate * _sigmoid(gate))
        mixed = jnp.concatenate([nsb, nhg], axis=1).astype(BF16)
        mt_ref[...] = mixed
        xo_ref[...] = x_ref[...] + _dot(mixed, w_ref[...])

    half = pl.BlockSpec((tm, SB_WIDTH), lambda i: (i, 0))
    vec = pl.BlockSpec((1, SB_WIDTH), lambda i: (0, 0))
    row = pl.BlockSpec((tm, D_MODEL), lambda i: (i, 0))
    return pl.pallas_call(
        body,
        name=name,
        grid=(t // tm,),
        in_specs=[half, half, pl.BlockSpec((tm, HG_WIDTH), lambda i: (i, 6)), vec, vec,
                  pl.BlockSpec((D_MODEL, D_MODEL), lambda i: (0, 0)), row],
        out_specs=[row, row],
        out_shape=[jax.ShapeDtypeStruct((t, D_MODEL), F32), jax.ShapeDtypeStruct((t, D_MODEL), BF16)],
        compiler_params=_params("parallel"),
    )(o_sb, o_hg, proj, g_sb, g_hg, w_out, x1)


def _mix_out_bwd(dx2, o_sb, o_hg, proj, g_sb, g_hg, w_out, *, name, tm=512):
    t = dx2.shape[0]

    def body(dx_ref, osb_ref, ohg_ref, gate_ref, gsb_ref, ghg_ref, w_ref, dosb_ref, dohg_ref, dgate_ref, dgsb_ref,
             dghg_ref, dxb_ref):
        i = pl.program_id(0)
        msb = _group_mat(SB_WIDTH, SB_HEAD_DIM)
        mhg = _group_mat(HG_WIDTH, HG_HEAD_DIM)
        dxb = dx_ref[...].astype(BF16)
        dxb_ref[...] = dxb
        dmixed = _dot_nt(dxb, w_ref[...])
        dnsb = dmixed[:, :SB_WIDTH]
        dy = dmixed[:, SB_WIDTH:]

        osb = osb_ref[...]
        rstd = lax.rsqrt(_head_mean(osb * osb, msb, SB_HEAD_DIM) + EPS)
        ohat = osb * rstd
        part_sb = jnp.sum(dnsb * ohat, axis=0, keepdims=True)
        dohat = dnsb * gsb_ref[...]
        dosb_ref[...] = rstd * (dohat - ohat * _head_mean(dohat * ohat, msb, SB_HEAD_DIM))

        ohg = ohg_ref[...]
        rstd = lax.rsqrt(_head_mean(ohg * ohg, mhg, HG_HEAD_DIM) + EPS)
        ohat = ohg * rstd
        gate = gate_ref[...]
        sig = _sigmoid(gate)
        dn = dy * (gate * sig)
        dgate_ref[...] = (dy * (ohat * ghg_ref[...]) * (sig * (1.0 + gate * (1.0 - sig)))).astype(BF16)
        part_hg = jnp.sum(dn * ohat, axis=0, keepdims=True)
        dohat = dn * ghg_ref[...]
        dohg_ref[...] = rstd * (dohat - ohat * _head_mean(dohat * ohat, mhg, HG_HEAD_DIM))

        @pl.when(i == 0)
        def _():
            dgsb_ref[...] = part_sb
            dghg_ref[...] = part_hg

        @pl.when(i > 0)
        def _():
            dgsb_ref[...] += part_sb
            dghg_ref[...] += part_hg

    half = pl.BlockSpec((tm, SB_WIDTH), lambda i: (i, 0))
    vec = pl.BlockSpec((1, SB_WIDTH), lambda i: (0, 0))
    row = pl.BlockSpec((tm, D_MODEL), lambda i: (i, 0))
    return pl.pallas_call(
        body,
        name=name,
        grid=(t // tm,),
        in_specs=[row, half, half, pl.BlockSpec((tm, HG_WIDTH), lambda i: (i, 6)), vec, vec,
                  pl.BlockSpec((D_MODEL, D_MODEL), lambda i: (0, 0))],
        out_specs=[half, half, half, vec, vec, row],
        out_shape=[jax.ShapeDtypeStruct((t, SB_WIDTH), F32)] * 2 + [jax.ShapeDtypeStruct((t, SB_WIDTH), BF16)]
        + [jax.ShapeDtypeStruct((1, SB_WIDTH), F32)] * 2 + [jax.ShapeDtypeStruct((t, D_MODEL), BF16)],
        compiler_params=_params("arbitrary"),
    )(dx2, o_sb, o_hg, proj, g_sb, g_hg, w_out)


def _loss_head(x3, gain, target, *, name, tm=512):
    t = x3.shape[0]

    def body(x_ref, g_ref, y_ref, dx_ref, dg_ref, loss_ref):
        i = pl.program_id(0)
        xhat, rstd = _rms(x_ref[...])
        err = xhat * g_ref[...] - y_ref[...]
        part_loss = 0.5 * jnp.sum(jnp.mean(err * err, axis=-1, keepdims=True), axis=0, keepdims=True)
        dy = err * (1.0 / D_MODEL)
        part_g = jnp.sum(dy * xhat, axis=0, keepdims=True)

        @pl.when(i == 0)
        def _():
            dg_ref[...] = part_g
            loss_ref[...] = jnp.broadcast_to(part_loss, loss_ref.shape)

        @pl.when(i > 0)
        def _():
            dg_ref[...] += part_g
            loss_ref[...] += jnp.broadcast_to(part_loss, loss_ref.shape)

        dxh = dy * g_ref[...]
        dx_ref[...] = rstd * (dxh - xhat * jnp.mean(dxh * xhat, axis=-1, keepdims=True))

    row = pl.BlockSpec((tm, D_MODEL), lambda i: (i, 0))
    vec = pl.BlockSpec((1, D_MODEL), lambda i: (0, 0))
    return pl.pallas_call(
        body,
        name=name,
        grid=(t // tm,),
        in_specs=[row, vec, row],
        out_specs=[row, vec, vec],
        out_shape=[jax.ShapeDtypeStruct((t, D_MODEL), F32), jax.ShapeDtypeStruct((1, D_MODEL), F32),
                   jax.ShapeDtypeStruct((1, D_MODEL), F32)],
        compiler_params=_params("arbitrary"),
    )(x3, gain, target)


def _local_step(x, target, norms, logits, w, weights_after=None, grads_ready=None):
    w = dict(w)
    x1, a1, b1, h1, s1 = _ffn_fwd(x, norms["ffn1"], w["g1t"], w["u1t"], w["d1"], name="ffn1_fwd")
    if weights_after is not None:
        w.update(weights_after("ffn1", x1))
    hm = _norm_fwd(x1, norms["mix"], name="mix_norm_fwd")
    proj = _mm(hm, w["int"], name="in_proj", tm=512, tn=IN_COLS, nt=True)
    o_sb, sb_kept = _attn_fwd(proj, name="sb_attn_fwd")
    o_hg, states = _hgrn_fwd(proj, logits, name="hgrn2_fwd")
    x2, mixed = _mix_out_fwd(o_sb, o_hg, proj, norms["sb"], norms["hg"], w["out"], x1, name="mix_out_fwd")
    if weights_after is not None:
        w.update(weights_after("mix", x2))
    x3, a2, b2, h2, s2 = _ffn_fwd(x2, norms["ffn2"], w["g2t"], w["u2t"], w["d2"], name="ffn2_fwd")
    dx3, d_final, loss_row = _loss_head(x3, norms["final"], target, name="loss_head")

    def weight_grad(lhs, rhs, name, tie=None):
        return _mm(lhs, rhs, name=name, tm=256, tn=D_MODEL, ta=True, out_dtype=BF16, tie=tie)

    def sent(stage):
        return grads_ready(stage, gw) if grads_ready is not None else None

    gw, gv = {}, {"final": d_final}
    dx2, gv["ffn2"], da2, db2, dob2 = _ffn_bwd(dx3, x2, norms["ffn2"], a2, b2, w["g2t"], w["u2t"], w["d2"],
                                               name="ffn2_bwd")
    gw["g2t"] = weight_grad(da2, h2, "ffn2_dgate")
    gw["u2t"] = weight_grad(db2, h2, "ffn2_dup")
    gw["d2"] = weight_grad(s2, dob2, "ffn2_ddown")

    do_sb, do_hg, d_gate, gv["sb"], gv["hg"], dx2b = _mix_out_bwd(
        dx2, o_sb, o_hg, proj, norms["sb"], norms["hg"], w["out"], name="mix_out_bwd")
    gw["out"] = weight_grad(mixed, dx2b, "out_dw")
    tie = sent("mix")
    dq_sb, dk_sb, dv_sb = _attn_bwd(proj, sb_kept, do_sb, name="sb_attn_bwd", tie=tie)
    dq_hg, df_hg, di_hg, d_lb = _hgrn_bwd(proj, logits if tie is None else logits + tie[0, 0], states, do_hg,
                                          name="hgrn2_bwd")
    dproj = jnp.concatenate([dq_sb, dk_sb.astype(BF16), dv_sb.astype(BF16), dq_hg, df_hg, di_hg, d_gate], axis=1)
    gw["int"] = weight_grad(dproj, hm, "in_dw")
    tie = sent("in")
    dhm = _mm(dproj, w["int"], name="in_dx", tm=512, tn=D_MODEL)
    dx1, gv["mix"] = _norm_bwd(dhm, x1, norms["mix"] if tie is None else norms["mix"] + tie[0, 0], dx2,
                               name="mix_norm_bwd")

    dx, gv["ffn1"], da1, db1, dob1 = _ffn_bwd(dx1, x, norms["ffn1"], a1, b1, w["g1t"], w["u1t"], w["d1"],
                                              name="ffn1_bwd")
    gw["g1t"] = weight_grad(da1, h1, "ffn1_dgate")
    gw["u1t"] = weight_grad(db1, h1, "ffn1_dup", tie=sent("g1t"))
    gw["d1"] = weight_grad(s1, dob1, "ffn1_ddown", tie=sent("u1t"))
    sent("d1")
    gv["lb"] = d_lb
    return loss_row, dx, gw, gv


HBM = pl.BlockSpec(memory_space=pl.ANY)


def _place():
    return lax.axis_index("x"), lax.axis_index("y"), lax.axis_index("c")


def _slot(px, py, pc):
    return 4 * px + 2 * py + pc


def _all_gather(blocks, *, name):
    n = len(blocks)

    def body(*refs):
        ins, outs = refs[:n], refs[n:2 * n]
        send_sems, recv_sems, local_sems = refs[2 * n:]
        x, y, c = _place()
        me, sibling = (x, y, c), (x, y, 1 - c)
        chips = [(1 - x, y), (x, 1 - y), (1 - x, 1 - y)]

        def copy(a, k, block, to, src=None):
            dst = outs[a].at[_slot(*block)]
            return pltpu.make_async_remote_copy(
                src_ref=dst if src is None else src, dst_ref=dst, send_sem=send_sems.at[7 * a + k],
                recv_sem=recv_sems.at[7 * a + k], device_id=to, device_id_type=MESH)

        mine = [pltpu.make_async_copy(ins[a], outs[a].at[_slot(*me)], local_sems.at[a]) for a in range(n)]
        for cp in mine:
            cp.start()
        first = []
        for a in range(n):
            first.append(copy(a, 0, me, sibling, src=ins[a]))
            first += [copy(a, 1 + j, me, (*chip, c), src=ins[a]) for j, chip in enumerate(chips)]
        for cp in first:
            cp.start()
        passed = []
        for j, chip in enumerate(chips):
            for a in range(n):
                copy(a, 1 + j, (*chip, c), me).wait_recv()
                fwd = copy(a, 4 + j, (*chip, c), sibling)
                fwd.start()
                passed.append(fwd)
        for a in range(n):
            copy(a, 0, sibling, me).wait_recv()
            for j, chip in enumerate(chips):
                copy(a, 4 + j, (*chip, 1 - c), me).wait_recv()
        for cp in first + passed:
            cp.wait_send()
        for cp in mine:
            cp.wait()

    return pl.pallas_call(
        body,
        name=name,
        in_specs=[HBM] * n,
        out_specs=[HBM] * n,
        out_shape=[jax.ShapeDtypeStruct((N_DEV,) + b.shape, b.dtype) for b in blocks],
        scratch_shapes=[pltpu.SemaphoreType.DMA((7 * n,)), pltpu.SemaphoreType.DMA((7 * n,)),
                        pltpu.SemaphoreType.DMA((n,))],
    )(*blocks)


def _flipped(place, d):
    return tuple(1 - p if (d >> (2 - axis)) & 1 else p for axis, p in enumerate(place))


SEM = pl.BlockSpec(memory_space=pltpu.SEMAPHORE)
EFFECT = pltpu.SideEffectType.DATAFLOW_SIDE_EFFECTING


def _split_copies(me, srcs, lands, send_sems, recv_sems, by_owner):
    copies = []
    for d in range(1, N_DEV):
        peer = _flipped(me, d)
        for a, (src, land) in enumerate(zip(srcs, lands)):
            copies.append(pltpu.make_async_remote_copy(
                src_ref=src.at[_slot(*peer)] if by_owner else src, dst_ref=land.at[_slot(*me)],
                send_sem=send_sems.at[7 * a + d - 1], recv_sem=recv_sems.at[7 * a + d - 1], device_id=peer,
                device_id_type=MESH))
    return copies


def _copies_start(srcs, *, name, by_owner, after=None):
    n = len(srcs)
    extra = [] if after is None else [after]
    land_shapes = [s.shape if by_owner else (N_DEV,) + s.shape for s in srcs]
    lands = [pltpu.with_memory_space_constraint(lax.empty(shape, s.dtype), pltpu.HBM) for shape, s in zip(land_shapes, srcs)]
    srcs = [pltpu.with_memory_space_constraint(s, pltpu.HBM) for s in srcs]

    def body(*refs):
        src_refs, land_refs = refs[:n], refs[n:2 * n]
        send_sems, recv_sems = refs[2 * n + len(extra)], refs[2 * n + len(extra) + 1]
        token = refs[-1]
        for cp in _split_copies(_place(), src_refs, land_refs, send_sems, recv_sems, by_owner):
            cp.start()
        token[...] = jnp.zeros_like(token)

    out = pl.pallas_call(
        body,
        name=name,
        in_specs=[HBM] * (2 * n + len(extra)),
        out_specs=[SEM, SEM] + [HBM] * (2 * n) + [pl.BlockSpec(memory_space=pltpu.VMEM)],
        out_shape=[pltpu.SemaphoreType.DMA((7 * n,)), pltpu.SemaphoreType.DMA((7 * n,))]
        + [pltpu.HBM(s.shape, s.dtype) for s in srcs] + [pltpu.HBM(shape, s.dtype) for shape, s in zip(land_shapes, srcs)]
        + [jax.ShapeDtypeStruct((8, LANES), F32)],
        input_output_aliases={i: 2 + i for i in range(2 * n)},
        compiler_params=pltpu.CompilerParams(has_side_effects=EFFECT),
    )(*srcs, *lands, *extra)
    return (out[0], out[1], out[2:2 + n], out[2 + n:2 + 2 * n]), out[-1]


def _copies_wait(started, after, *, name, by_owner):
    send_sems, recv_sems, srcs, lands = started
    n = len(srcs)

    def body(*refs):
        src_refs, land_refs = refs[:n], refs[n:2 * n]
        for cp in _split_copies(_place(), src_refs, land_refs, refs[2 * n], refs[2 * n + 1], by_owner):
            cp.wait_send()
            cp.wait_recv()

    out = pl.pallas_call(
        body,
        name=name,
        in_specs=[HBM] * (2 * n) + [SEM, SEM, HBM],
        out_specs=[HBM] * (2 * n),
        out_shape=[pltpu.HBM(s.shape, s.dtype) for s in srcs] + [pltpu.HBM(s.shape, s.dtype) for s in lands],
        input_output_aliases={i: i for i in range(2 * n)},
        compiler_params=pltpu.CompilerParams(has_side_effects=EFFECT),
    )(*srcs, *lands, send_sems, recv_sems, after)
    return out[:n], out[n:]


def _with_own(lands, own, slot):
    zero = jnp.zeros((), jnp.int32)
    return [lax.dynamic_update_slice(land, o[None], (slot.astype(jnp.int32),) + (zero,) * o.ndim)
            for land, o in zip(lands, own)]


def _adamw(w, g, m, v):
    m = ADAM_B1 * m + (1.0 - ADAM_B1) * g
    v = ADAM_B2 * v + (1.0 - ADAM_B2) * (g * g)
    m_hat = m / (1.0 - ADAM_B1 ** ADAM_STEP)
    v_hat = v / (1.0 - ADAM_B2 ** ADAM_STEP)
    delta = -ADAM_LR * (m_hat / (jnp.sqrt(v_hat) + ADAM_EPS) + ADAM_WD * w)
    return delta, m, v


def _sum_and_update(parts, w, m, v, *, name, tie=None):
    _, rows, cols = w.shape
    tr = rows // 2

    def body(p_ref, w_ref, m_ref, v_ref, *rest):
        g_ref, d_ref, mo_ref, vo_ref = rest[-4:]
        g = p_ref[0].astype(F32)
        for s in range(1, N_DEV):
            g = g + p_ref[s].astype(F32)
        g_ref[0] = g
        d_ref[0], mo_ref[0], vo_ref[0] = _adamw(w_ref[0], g, m_ref[0], v_ref[0])

    flat = pl.BlockSpec((1, tr, cols), lambda i: (0, i, 0))
    return pl.pallas_call(
        body,
        name=name,
        grid=(rows // tr,),
        in_specs=[pl.BlockSpec((N_DEV, tr, cols), lambda i: (0, i, 0)), flat, flat, flat]
        + ([] if tie is None else [pl.BlockSpec(memory_space=pl.ANY)]),
        out_specs=[flat] * 4,
        out_shape=[jax.ShapeDtypeStruct((1, rows, cols), F32)] * 4,
        compiler_params=_params("parallel"),
    )(parts, w, m, v, *([] if tie is None else [tie]))


VEC_ROWS = 8
ROW_LOGITS, ROW_LOSS = 5, 7


def _vectors_update(part, w, m, v, *, name):
    def body(p_ref, w_ref, m_ref, v_ref, g_ref, d_ref, mo_ref, vo_ref, loss_ref, all_ref, send_sems, recv_sems):
        me = _place()
        all_ref[_slot(*me)] = p_ref[...]
        copies = []
        for d in range(1, N_DEV):
            peer = _flipped(me, d)
            copies.append(pltpu.make_async_remote_copy(
                src_ref=p_ref, dst_ref=all_ref.at[_slot(*me)], send_sem=send_sems.at[d - 1], recv_sem=recv_sems.at[d - 1],
                device_id=peer, device_id_type=MESH))
        for cp in copies:
            cp.start()
        for cp in copies:
            cp.wait()
        total = all_ref[0]
        for s in range(1, N_DEV):
            total = total + all_ref[s]
        wv = w_ref[...]
        half = D_MODEL // 2
        lb = _sigmoid(wv[ROW_LOGITS:ROW_LOGITS + 1, :half] - wv[ROW_LOGITS:ROW_LOGITS + 1, half:])
        d_first = total[ROW_LOGITS:ROW_LOGITS + 1, :half] * lb * (1.0 - lb)
        d_logits = jnp.concatenate([d_first, -d_first], axis=1)
        rowi = lax.broadcasted_iota(jnp.int32, (VEC_ROWS, D_MODEL), 0)
        g = jnp.where(rowi == ROW_LOGITS, d_logits, jnp.where(rowi < ROW_LOGITS, total, 0.0))
        g_ref[...] = g
        d_ref[...], mo_ref[...], vo_ref[...] = _adamw(wv, g, m_ref[...], v_ref[...])
        loss_ref[...] = total[ROW_LOSS:ROW_LOSS + 1, :]

    vmem = pl.BlockSpec(memory_space=pltpu.VMEM)
    return pl.pallas_call(
        body,
        name=name,
        in_specs=[vmem] * 4,
        out_specs=[vmem] * 5,
        out_shape=[jax.ShapeDtypeStruct((VEC_ROWS, D_MODEL), F32)] * 4 + [jax.ShapeDtypeStruct((1, D_MODEL), F32)],
        scratch_shapes=[pltpu.VMEM((N_DEV, VEC_ROWS, D_MODEL), F32), pltpu.SemaphoreType.DMA((7,)),
                        pltpu.SemaphoreType.DMA((7,))],
    )(part, w, m, v)


TRANSPOSED = ("g1t", "u1t", "g2t", "u2t", "int")


def _vector_rows(rows):
    rowi = lax.broadcasted_iota(jnp.int32, (VEC_ROWS, D_MODEL), 0)
    out = jnp.zeros((VEC_ROWS, D_MODEL), F32)
    for i, r in enumerate(rows):
        if r is not None:
            out = jnp.where(rowi == i, r, out)
    return out


def kernel(x, ffn1_norm, ffn1_w_gate, ffn1_w_up, ffn1_w_down, mix_norm, w_in, sb_out_norm, hg_lower_bound_logits, hg_out_norm, w_out, ffn2_norm, ffn2_w_gate, ffn2_w_up, ffn2_w_down, final_norm, loss_target, m_ffn1_norm, m_ffn1_w_gate, m_ffn1_w_up, m_ffn1_w_down, m_mix_norm, m_w_in, m_sb_out_norm, m_hg_lower_bound_logits, m_hg_out_norm, m_w_out, m_ffn2_norm, m_ffn2_w_gate, m_ffn2_w_up, m_ffn2_w_down, m_final_norm, v_ffn1_norm, v_ffn1_w_gate, v_ffn1_w_up, v_ffn1_w_down, v_mix_norm, v_w_in, v_sb_out_norm, v_hg_lower_bound_logits, v_hg_out_norm, v_w_out, v_ffn2_norm, v_ffn2_w_gate, v_ffn2_w_up, v_ffn2_w_down, v_final_norm):
    def matrices(g1, u1, d1, win, wout, g2, u2, d2):
        return {"g1t": g1, "u1t": u1, "d1": d1, "int": win, "out": wout, "g2t": g2, "u2t": u2, "d2": d2}

    def vectors(n1, nm, nsb, lg, nhg, n2, nf):
        return [n1, nm, n2, nf.reshape(1, D_MODEL), jnp.concatenate([nsb, nhg], axis=1), lg.reshape(1, D_MODEL), None, None]

    w_sh = matrices(ffn1_w_gate, ffn1_w_up, ffn1_w_down, w_in, w_out, ffn2_w_gate, ffn2_w_up, ffn2_w_down)
    m_sh = matrices(m_ffn1_w_gate, m_ffn1_w_up, m_ffn1_w_down, m_w_in, m_w_out, m_ffn2_w_gate, m_ffn2_w_up, m_ffn2_w_down)
    v_sh = matrices(v_ffn1_w_gate, v_ffn1_w_up, v_ffn1_w_down, v_w_in, v_w_out, v_ffn2_w_gate, v_ffn2_w_up, v_ffn2_w_down)
    keys = list(w_sh)

    slot = _slot(*_place())

    def full(key, stack):
        return stack.reshape(-1, D_MODEL)

    def by_owner(key, grad):
        return grad.reshape(N_DEV, -1, D_MODEL)

    def view(key, a):
        return jnp.swapaxes(a, 1, 2) if key in TRANSPOSED else a

    blocks = {k: view(k, w_sh[k])[0].astype(BF16) for k in keys}
    first, mid, last = ("g1t", "u1t", "d1"), ("int", "out"), ("g2t", "u2t", "d2")
    w_first = {k: full(k, s) for k, s in zip(first, _all_gather([blocks[k] for k in first], name="gather_ffn1"))}
    flights = {}
    flights["ffn1"], token_mid = _copies_start([blocks[k] for k in mid], name="gather_mid_start", by_owner=False,
                                               after=w_first["d1"])
    flights["mix"], token_last = _copies_start([blocks[k] for k in last], name="gather_ffn2_start", by_owner=False,
                                               after=token_mid)

    def weights_after(stage, result):
        group = mid if stage == "ffn1" else last
        own, lands = _copies_wait(flights[stage], result, name="gather_" + stage + "_wait", by_owner=False)
        return {k: full(k, s) for k, s in zip(group, _with_own(lands, own, slot))}

    groups = {"mix": ("g2t", "u2t", "d2", "out"), "in": ("int",), "g1t": ("g1t",), "u1t": ("u1t",), "d1": ("d1",)}
    sent, sent_tokens = {}, []

    def grads_ready(stage, gw):
        stacks = [by_owner(k, gw[k]) for k in groups[stage]]
        flight, token = _copies_start(stacks, name="grads_" + stage + "_start", by_owner=True)
        sent[stage] = flight
        sent_tokens.append(token)
        return token

    norms = {"ffn1": ffn1_norm + token_last[0, 0], "mix": mix_norm, "sb": sb_out_norm, "hg": hg_out_norm,
             "ffn2": ffn2_norm, "final": final_norm.reshape(1, D_MODEL)}
    loss_row, grad_x, gw, gv = _local_step(x[0], loss_target[0], norms, hg_lower_bound_logits, w_first, weights_after,
                                           grads_ready)

    updated, after = {}, sent_tokens[-1]
    for stage, flight in sent.items():
        stacks, lands = _copies_wait(flight, after, name="grads_" + stage + "_wait", by_owner=True)
        own = [lax.dynamic_index_in_dim(s, slot, keepdims=False) for s in stacks]
        for k, part in zip(groups[stage], _with_own(lands, own, slot)):
            updated[k] = _sum_and_update(part, view(k, w_sh[k]), view(k, m_sh[k]), view(k, v_sh[k]), name="adamw_" + k,
                                         tie=after)
            after = updated[k][0]
    mats = [{k: view(k, updated[k][i]) for k in keys} for i in range(4)]

    lb_row = jnp.concatenate([gv["lb"], jnp.zeros_like(gv["lb"])], axis=1)
    part = _vector_rows([gv["ffn1"], gv["mix"], gv["ffn2"], gv["final"], jnp.concatenate([gv["sb"], gv["hg"]], axis=1),
                         lb_row, None, loss_row])
    vec_w = _vector_rows(vectors(ffn1_norm, mix_norm, sb_out_norm, hg_lower_bound_logits, hg_out_norm, ffn2_norm, final_norm))
    vec_m = _vector_rows(vectors(m_ffn1_norm, m_mix_norm, m_sb_out_norm, m_hg_lower_bound_logits, m_hg_out_norm,
                                 m_ffn2_norm, m_final_norm))
    vec_v = _vector_rows(vectors(v_ffn1_norm, v_mix_norm, v_sb_out_norm, v_hg_lower_bound_logits, v_hg_out_norm,
                                 v_ffn2_norm, v_final_norm))
    *vecs, loss_out = _vectors_update(part, vec_w, vec_m, vec_v, name="vectors_update")

    def leaves(mat, vec):
        half = D_MODEL // 2
        return (
            vec[0:1], mat["g1t"], mat["u1t"], mat["d1"], vec[1:2], mat["int"], vec[4:5, :half],
            vec[ROW_LOGITS].reshape(2, half), vec[4:5, half:], mat["out"], vec[2:3], mat["g2t"], mat["u2t"],
            mat["d2"], vec[3],
        )

    out = [loss_out[0, 0], grad_x[None]]
    for mat, vec in zip(mats, vecs):
        out.extend(leaves(mat, vec))
    return tuple(out)
```

```python
import jax
import jax.numpy as jnp
from jax import lax
from jax.experimental import pallas as pl
from jax.experimental.pallas import tpu as pltpu

F32, BF16 = jnp.float32, jnp.bfloat16
D_MODEL = 1024
D_FF = 2816
SB_WIDTH = 512
HG_WIDTH = 512
SB_HEAD_DIM = 64
HG_HEAD_DIM = 128
IN_COLS = 3584
EPS = 1e-6
N_DEV = 8
LANES = 128
HG_CHUNK = 16
VMEM_LIMIT_BYTES = 48 * 1024 * 1024
FFN_BWD_VMEM_LIMIT_BYTES = 56 * 1024 * 1024
ADAM_LR, ADAM_B1, ADAM_B2, ADAM_EPS, ADAM_WD, ADAM_STEP = 0.001, 0.9, 0.999, 1e-08, 0.01, 10
MESH = pl.DeviceIdType.MESH


def _params(*semantics, vmem_limit_bytes=VMEM_LIMIT_BYTES):
    return pltpu.CompilerParams(dimension_semantics=semantics, vmem_limit_bytes=vmem_limit_bytes)


def _dot(a, b):
    return jnp.dot(a, b, preferred_element_type=F32)


def _dot_nt(a, b):
    return lax.dot_general(a, b, (((1,), (1,)), ((), ())), preferred_element_type=F32)


def _dot_tn(a, b):
    return lax.dot_general(a, b, (((0,), (0,)), ((), ())), preferred_element_type=F32)


def _split3(x):
    hi = x.astype(BF16)
    r1 = x - hi.astype(F32)
    mid = r1.astype(BF16)
    lo = (r1 - mid.astype(F32)).astype(BF16)
    return hi, mid, lo


def _rms(xv):
    rstd = lax.rsqrt(jnp.mean(xv * xv, axis=-1, keepdims=True) + EPS)
    return xv * rstd, rstd


def _sigmoid(x):
    return 1.0 / (1.0 + jnp.exp(-x))


def _loss_terms(xv, gain, target):
    xhat, rstd = _rms(xv)
    err = xhat * gain - target
    loss = 0.5 * jnp.sum(jnp.mean(err * err, axis=-1, keepdims=True), axis=0, keepdims=True)
    dy = err * (1.0 / xv.shape[-1])
    dxh = dy * gain
    dx = rstd * (dxh - xhat * jnp.mean(dxh * xhat, axis=-1, keepdims=True))
    return dx, jnp.sum(dy * xhat, axis=0, keepdims=True), loss


def _mm(a, b, *, name, tm, tn, nt=False, ta=False, out_dtype=F32, tie=None):
    k, m = a.shape if ta else a.shape[::-1]
    n = b.shape[0] if nt else b.shape[1]
    assert m % tm == 0 and n % tn == 0 and not (nt and ta), (name, a.shape, b.shape, tm, tn)

    def body(a_ref, b_ref, *rest):
        av = a_ref[...].astype(BF16)
        bv = b_ref[...].astype(BF16)
        rest[-1][...] = (_dot_nt(av, bv) if nt else _dot_tn(av, bv) if ta else _dot(av, bv)).astype(out_dtype)

    in_specs = [
        pl.BlockSpec((k, tm), lambda i, j: (0, i)) if ta else pl.BlockSpec((tm, k), lambda i, j: (i, 0)),
        pl.BlockSpec((tn, k), lambda i, j: (j, 0)) if nt else pl.BlockSpec((k, tn), lambda i, j: (0, j)),
    ]
    operands = [a, b]
    if tie is not None:
        in_specs.append(pl.BlockSpec(memory_space=pl.ANY))
        operands.append(tie)
    return pl.pallas_call(
        body,
        name=name,
        grid=(m // tm, n // tn),
        in_specs=in_specs,
        out_specs=pl.BlockSpec((tm, tn), lambda i, j: (i, j)),
        out_shape=jax.ShapeDtypeStruct((m, n), out_dtype),
        compiler_params=_params("parallel", "parallel"),
    )(*operands)


def _ffn_fwd(x, gain, wgt, wut, wd, *, name, next_gain=None, head=None, tm=1024, tf=256):
    t = x.shape[0]
    nj = D_FF // tf
    extra_in = [] if next_gain is None else [next_gain]
    extra_in += [] if head is None else list(head)

    def body(x_ref, g_ref, wg_ref, wu_ref, wd_prev_ref, wd_last_ref, *rest):
        extra, (xo_ref, a_ref, b_ref, h_ref, st_ref) = rest[:len(extra_in)], rest[len(extra_in):len(extra_in) + 5]
        tail_out, (acc, s_prev) = rest[len(extra_in) + 5:-2], rest[-2:]
        i = pl.program_id(0)
        j = pl.program_id(1)

        @pl.when(j == 0)
        def _():
            xhat, _ = _rms(x_ref[...])
            h_ref[...] = (xhat * g_ref[...]).astype(BF16)
            acc[...] = jnp.zeros_like(acc)
            s_prev[...] = jnp.zeros_like(s_prev)

        acc[...] += _dot(s_prev[...], wd_prev_ref[...])
        h = h_ref[...]
        a = _dot_nt(h, wg_ref[...])
        b = _dot_nt(h, wu_ref[...])
        a_ref[...] = a.astype(BF16)
        b_ref[...] = b.astype(BF16)
        s = (a * _sigmoid(a) * b).astype(BF16)
        st_ref[...] = s
        s_prev[...] = s

        @pl.when(j == nj - 1)
        def _():
            xo = x_ref[...] + 0.5 * (acc[...] + _dot(s, wd_last_ref[...]))
            if head is None:
                xo_ref[...] = xo
            if next_gain is not None:
                tail_out[0][...] = (_rms(xo)[0] * extra[0][...]).astype(BF16)
            if head is not None:
                gain_ref, target_ref = extra[-2:]
                dg_ref, loss_ref = tail_out[-2:]
                xo_ref[...], part_g, part_loss = _loss_terms(xo, gain_ref[...], target_ref[...])

                @pl.when(i == 0)
                def _():
                    dg_ref[...] = part_g
                    loss_ref[...] = jnp.broadcast_to(part_loss, loss_ref.shape)

                @pl.when(i > 0)
                def _():
                    dg_ref[...] += part_g
                    loss_ref[...] += jnp.broadcast_to(part_loss, loss_ref.shape)

    row = pl.BlockSpec((tm, D_MODEL), lambda i, j: (i, 0))
    vec = pl.BlockSpec((1, D_MODEL), lambda i, j: (0, 0))
    tile = pl.BlockSpec((tm, tf), lambda i, j: (i, j))
    weights = pl.BlockSpec((tf, D_MODEL), lambda i, j: (j, 0))
    tail_specs = ([] if next_gain is None else [row]) + ([] if head is None else [vec, vec])
    tail_shapes = ([] if next_gain is None else [jax.ShapeDtypeStruct((t, D_MODEL), BF16)]) + (
        [] if head is None else [jax.ShapeDtypeStruct((1, D_MODEL), F32)] * 2)
    return pl.pallas_call(
        body,
        name=name,
        grid=(t // tm, nj),
        in_specs=[
            row, vec, weights, weights,
            pl.BlockSpec((tf, D_MODEL), lambda i, j: (jnp.maximum(j - 1, 0), 0)),
            pl.BlockSpec((tf, D_MODEL), lambda i, j: (nj - 1, 0)),
        ] + ([] if next_gain is None else [vec]) + ([] if head is None else [vec, row]),
        out_specs=[row, tile, tile, row, tile] + tail_specs,
        out_shape=[
            jax.ShapeDtypeStruct((t, D_MODEL), F32),
            jax.ShapeDtypeStruct((t, D_FF), BF16),
            jax.ShapeDtypeStruct((t, D_FF), BF16),
            jax.ShapeDtypeStruct((t, D_MODEL), BF16),
            jax.ShapeDtypeStruct((t, D_FF), BF16),
        ] + tail_shapes,
        scratch_shapes=[pltpu.VMEM((tm, D_MODEL), F32), pltpu.VMEM((tm, tf), BF16)],
        compiler_params=_params("arbitrary", "arbitrary"),
    )(x, gain, wgt, wut, wd, wd, *extra_in)


def _ffn_bwd(dout, x, gain, a, b, wgt, wut, wd, *, name, tm=1024, tf=256):
    t = x.shape[0]
    nj = D_FF // tf

    def body(do_ref, x_ref, g_ref, a_ref, b_ref, wg_prev_ref, wu_prev_ref, wg_last_ref, wu_last_ref, wd_ref,
             dx_ref, dg_ref, da_ref, db_ref, dob_ref, dob_scr, dh, da_prev, db_prev):
        i = pl.program_id(0)
        j = pl.program_id(1)

        @pl.when(j == 0)
        def _():
            d = (0.5 * do_ref[...]).astype(BF16)
            dob_scr[...] = d
            dob_ref[...] = d
            dh[...] = jnp.zeros_like(dh)
            da_prev[...] = jnp.zeros_like(da_prev)
            db_prev[...] = jnp.zeros_like(db_prev)

        dh[...] += _dot(da_prev[...], wg_prev_ref[...]) + _dot(db_prev[...], wu_prev_ref[...])
        ds = _dot_nt(dob_scr[...], wd_ref[...])
        av = a_ref[...].astype(F32)
        bv = b_ref[...].astype(F32)
        sig = _sigmoid(av)
        dbv = (ds * (av * sig)).astype(BF16)
        dav = (ds * bv * (sig * (1.0 + av * (1.0 - sig)))).astype(BF16)
        da_ref[...] = dav
        db_ref[...] = dbv
        da_prev[...] = dav
        db_prev[...] = dbv

        @pl.when(j == nj - 1)
        def _():
            xhat, rstd = _rms(x_ref[...])
            dhv = dh[...] + _dot(dav, wg_last_ref[...]) + _dot(dbv, wu_last_ref[...])
            part = jnp.sum(dhv * xhat, axis=0, keepdims=True)

            @pl.when(i == 0)
            def _():
                dg_ref[...] = part

            @pl.when(i > 0)
            def _():
                dg_ref[...] += part

            dxh = dhv * g_ref[...]
            dx_ref[...] = do_ref[...] + rstd * (dxh - xhat * jnp.mean(dxh * xhat, axis=-1, keepdims=True))

    return pl.pallas_call(
        body,
        name=name,
        grid=(t // tm, nj),
        in_specs=[
            pl.BlockSpec((tm, D_MODEL), lambda i, j: (i, 0)),
            pl.BlockSpec((tm, D_MODEL), lambda i, j: (i, 0)),
            pl.BlockSpec((1, D_MODEL), lambda i, j: (0, 0)),
            pl.BlockSpec((tm, tf), lambda i, j: (i, j)),
            pl.BlockSpec((tm, tf), lambda i, j: (i, j)),
            pl.BlockSpec((tf, D_MODEL), lambda i, j: (jnp.maximum(j - 1, 0), 0)),
            pl.BlockSpec((tf, D_MODEL), lambda i, j: (jnp.maximum(j - 1, 0), 0)),
            pl.BlockSpec((tf, D_MODEL), lambda i, j: (nj - 1, 0)),
            pl.BlockSpec((tf, D_MODEL), lambda i, j: (nj - 1, 0)),
            pl.BlockSpec((tf, D_MODEL), lambda i, j: (j, 0)),
        ],
        out_specs=[
            pl.BlockSpec((tm, D_MODEL), lambda i, j: (i, 0)),
            pl.BlockSpec((1, D_MODEL), lambda i, j: (0, 0)),
            pl.BlockSpec((tm, tf), lambda i, j: (i, j)),
            pl.BlockSpec((tm, tf), lambda i, j: (i, j)),
            pl.BlockSpec((tm, D_MODEL), lambda i, j: (i, 0)),
        ],
        out_shape=[
            jax.ShapeDtypeStruct((t, D_MODEL), F32),
            jax.ShapeDtypeStruct((1, D_MODEL), F32),
            jax.ShapeDtypeStruct((t, D_FF), BF16),
            jax.ShapeDtypeStruct((t, D_FF), BF16),
            jax.ShapeDtypeStruct((t, D_MODEL), BF16),
        ],
        scratch_shapes=[pltpu.VMEM((tm, D_MODEL), BF16), pltpu.VMEM((tm, D_MODEL), F32), pltpu.VMEM((tm, tf), BF16),
                        pltpu.VMEM((tm, tf), BF16)],
        compiler_params=_params("arbitrary", "arbitrary", vmem_limit_bytes=FFN_BWD_VMEM_LIMIT_BYTES),
    )(dout, x, gain, a, b, wgt, wut, wgt, wut, wd)


def _in_proj_bwd(dproj, w_int, x, gain, dres, *, name, tm=512):
    t, k = dproj.shape

    def body(dp_ref, w_ref, x_ref, g_ref, dr_ref, dx_ref, dg_ref):
        i = pl.program_id(0)
        dhv = _dot(dp_ref[...], w_ref[...])
        xhat, rstd = _rms(x_ref[...])
        part = jnp.sum(dhv * xhat, axis=0, keepdims=True)

        @pl.when(i == 0)
        def _():
            dg_ref[...] = part

        @pl.when(i > 0)
        def _():
            dg_ref[...] += part

        dxh = dhv * g_ref[...]
        dx_ref[...] = dr_ref[...] + rstd * (dxh - xhat * jnp.mean(dxh * xhat, axis=-1, keepdims=True))

    row = pl.BlockSpec((tm, D_MODEL), lambda i: (i, 0))
    vec = pl.BlockSpec((1, D_MODEL), lambda i: (0, 0))
    return pl.pallas_call(
        body,
        name=name,
        grid=(t // tm,),
        in_specs=[pl.BlockSpec((tm, k), lambda i: (i, 0)), pl.BlockSpec((k, D_MODEL), lambda i: (0, 0)), row, vec, row],
        out_specs=[row, vec],
        out_shape=[jax.ShapeDtypeStruct((t, D_MODEL), F32), jax.ShapeDtypeStruct((1, D_MODEL), F32)],
        compiler_params=_params("arbitrary"),
    )(dproj, w_int, x, gain, dres)


ATT_Q_TILE = 512
ATT_K_BLOCK = 256


def _first_head_lanes():
    return lax.broadcasted_iota(jnp.int32, (1, LANES), 1) < SB_HEAD_DIM


def _stack_heads(x):
    first = _first_head_lanes()
    return jnp.concatenate([jnp.where(first, x, 0.0), jnp.where(first, 0.0, x)], axis=0)


def _unstack_heads(x, rows):
    return jnp.where(_first_head_lanes(), x[:rows], x[rows:])


def _rows_from(x, first, rows):
    return x if first == 0 else jnp.concatenate([x[first:rows], x[rows + first:]], axis=0)


def _rows_into(full, part, first, rows):
    if first == 0:
        return part
    n = rows - first
    return jnp.concatenate([full[:first], part[:n], full[rows:rows + first], part[n:]], axis=0)


def _tri(n, relation):
    r = lax.broadcasted_iota(jnp.int32, (n, n), 0)
    c = lax.broadcasted_iota(jnp.int32, (n, n), 1)
    return relation(r, c).astype(BF16)


def _scan_dot(x, tri):
    hi = x.astype(BF16)
    lo = (x - hi.astype(F32)).astype(BF16)
    return _dot(jnp.concatenate([hi, lo], axis=1), jnp.concatenate([tri, tri], axis=0))


def _log_terms(z):
    lbeta = jnp.minimum(z, 0.0) - jnp.log(1.0 + jnp.exp(-jnp.abs(z)))
    return lbeta, lbeta - z


def _attn_fwd(proj, *, name):
    t = proj.shape[0]
    tq, tk = ATT_Q_TILE, ATT_K_BLOCK
    diag = tq // tk
    n_pairs = SB_WIDTH // LANES

    def body(q_ref, k_ref, v_ref, o_ref, kept_ref):
        qi = pl.program_id(1)
        q = q_ref[...] * (SB_HEAD_DIM ** -0.5)
        qs = _stack_heads(q).astype(BF16)
        tri = _tri(tk, lambda j, s: j > s)
        trow = lax.broadcasted_iota(jnp.int32, (tq, tk), 0)
        scol = lax.broadcasted_iota(jnp.int32, (tq, tk), 1)

        def block(kb, carry, causal, first=0):
            acc, c = carry
            off = pl.multiple_of(kb * tk, tk)
            lbeta, lrest = _log_terms(_dot_nt(_rows_from(qs, first, tq), k_ref[pl.ds(off, tk), :].astype(BF16)))
            if causal is not None:
                lrest = jnp.where(causal, lrest, 0.0)
            w = jnp.exp(lbeta + (_scan_dot(lrest, tri) + _rows_from(c, first, tq)))
            if causal is not None:
                w = jnp.where(causal, w, 0.0)
            wb = w.astype(BF16)
            kept_ref[0, 0, kb] = _rows_into(jnp.zeros((2 * tq, tk), BF16), wb, first, tq)
            acc = _rows_into(acc, _rows_from(acc, first, tq) + _dot(wb, v_ref[pl.ds(off, tk), :].astype(BF16)), first, tq)
            return acc, _rows_into(c, _rows_from(c, first, tq) + jnp.sum(lrest, axis=1, keepdims=True), first, tq)

        carry = (jnp.zeros((2 * tq, LANES), F32), jnp.zeros((2 * tq, 1), F32))
        n_full = qi * diag
        for j in reversed(range(diag)):
            mask = ((scol + j * tk) < trow)[j * tk:]
            carry = block(n_full + j, carry, jnp.concatenate([mask, mask], axis=0), first=j * tk)

        def step(it, carry):
            return block(n_full - 1 - it, carry, None)

        acc, _ = lax.fori_loop(0, n_full, step, carry)
        o_ref[...] = _unstack_heads(acc, tq)

    return pl.pallas_call(
        body,
        name=name,
        grid=(n_pairs, t // tq),
        in_specs=[
            pl.BlockSpec((tq, LANES), lambda p, i: (i, p)),
            pl.BlockSpec((t, LANES), lambda p, i: (0, n_pairs + p)),
            pl.BlockSpec((t, LANES), lambda p, i: (0, 2 * n_pairs + p)),
        ],
        out_specs=[pl.BlockSpec((tq, LANES), lambda p, i: (i, p)),
                   pl.BlockSpec((1, 1, t // tk, 2 * tq, tk), lambda p, i: (p, i, 0, 0, 0))],
        out_shape=[jax.ShapeDtypeStruct((t, SB_WIDTH), F32),
                   jax.ShapeDtypeStruct((n_pairs, t // tq, t // tk, 2 * tq, tk), BF16)],
        compiler_params=_params("parallel", "parallel"),
    )(proj, proj, proj)


def _attn_bwd(proj, kept, do, *, name, tie=None):
    t = proj.shape[0]
    tq, tk = ATT_Q_TILE, ATT_K_BLOCK
    diag = tq // tk
    n_pairs = SB_WIDTH // LANES
    scale = SB_HEAD_DIM ** -0.5

    def body(q_ref, k_ref, v_ref, kept_ref, do_ref, *rest):
        dq_ref, dk_ref, dv_ref = rest[-3:]
        qi = pl.program_id(1)

        @pl.when(qi == 0)
        def _():
            dk_ref[...] = jnp.zeros_like(dk_ref)
            dv_ref[...] = jnp.zeros_like(dv_ref)

        qs = _stack_heads(q_ref[...] * scale).astype(BF16)
        dos = _stack_heads(do_ref[...]).astype(BF16)
        before = _tri(tk, lambda s, j: s < j)
        trow = lax.broadcasted_iota(jnp.int32, (tq, tk), 0)
        scol = lax.broadcasted_iota(jnp.int32, (tq, tk), 1)

        def block(kb, carry, causal, first=0):
            dq, cg = carry
            off = pl.multiple_of(kb * tk, tk)
            q_rows, do_rows = _rows_from(qs, first, tq), _rows_from(dos, first, tq)
            wb = _rows_from(kept_ref[0, 0, kb], first, tq)
            kblk = k_ref[pl.ds(off, tk), :].astype(BF16)
            sig = 0.5 + 0.5 * jnp.tanh(0.5 * _dot_nt(q_rows, kblk))
            g = wb.astype(F32) * _dot_nt(do_rows, v_ref[pl.ds(off, tk), :].astype(BF16))
            prior = _scan_dot(g, before) + _rows_from(cg, first, tq)
            dz = g - sig * (g + prior)
            if causal is not None:
                dz = jnp.where(causal, dz, 0.0)
            dzb = dz.astype(BF16)
            dq = _rows_into(dq, _rows_from(dq, first, tq) + _dot(dzb, kblk), first, tq)
            dk_ref[pl.ds(off, tk), :] += _dot_tn(dzb, q_rows)
            dv_ref[pl.ds(off, tk), :] += _dot_tn(wb, do_rows)
            return dq, _rows_into(cg, _rows_from(cg, first, tq) + jnp.sum(g, axis=1, keepdims=True), first, tq)

        n_full = qi * diag
        carry = lax.fori_loop(0, n_full, lambda kb, carry: block(kb, carry, None),
                              (jnp.zeros((2 * tq, LANES), F32), jnp.zeros((2 * tq, 1), F32)))
        for j in range(diag):
            mask = ((scol + j * tk) < trow)[j * tk:]
            carry = block(n_full + j, carry, jnp.concatenate([mask, mask], axis=0), first=j * tk)
        dq_ref[...] = (_unstack_heads(carry[0], tq) * scale).astype(BF16)

    tile_spec = pl.BlockSpec((tq, LANES), lambda p, i: (i, p))
    full_spec = pl.BlockSpec((t, LANES), lambda p, i: (0, p))
    return pl.pallas_call(
        body,
        name=name,
        grid=(n_pairs, t // tq),
        in_specs=[
            tile_spec,
            pl.BlockSpec((t, LANES), lambda p, i: (0, n_pairs + p)),
            pl.BlockSpec((t, LANES), lambda p, i: (0, 2 * n_pairs + p)),
            pl.BlockSpec((1, 1, t // tk, 2 * tq, tk), lambda p, i: (p, i, 0, 0, 0)),
            tile_spec,
        ] + ([] if tie is None else [pl.BlockSpec(memory_space=pl.ANY)]),
        out_specs=[tile_spec, full_spec, full_spec],
        out_shape=[jax.ShapeDtypeStruct((t, SB_WIDTH), BF16)] + [jax.ShapeDtypeStruct((t, SB_WIDTH), F32)] * 2,
        compiler_params=_params("arbitrary", "arbitrary"),
    )(proj, proj, proj, kept, do, *([] if tie is None else [tie]))


HG_BLOCK = 128
HG_HEADS = HG_WIDTH // HG_HEAD_DIM


def _chunk_mats(n):
    r = lax.broadcasted_iota(jnp.int32, (n, n), 0)
    c = lax.broadcasted_iota(jnp.int32, (n, n), 1)
    same = (r // HG_CHUNK) == (c // HG_CHUNK)
    upto = (same & (c <= r)).astype(BF16)
    whole = same.astype(BF16)
    onward = (same & (c >= r)).astype(BF16)
    return upto, whole, onward


def _rows_dot(mat, x):
    return _dot(jnp.concatenate([mat, mat, mat], axis=1), jnp.concatenate(_split3(x), axis=0))


def _split_heads(x):
    return jnp.stack([x[:, h * HG_HEAD_DIM:(h + 1) * HG_HEAD_DIM] for h in range(HG_HEADS)], axis=0)


def _merge_heads(x):
    return jnp.concatenate([x[h] for h in range(HG_HEADS)], axis=1)


def _lower_bound(lg_ref):
    lg = lg_ref[...]
    return _sigmoid(lg[0:1, :] - lg[1:2, :])


def _hgrn_prepare(q_ref, f_ref, lb, h, upto, whole):
    cols = slice(h * HG_HEAD_DIM, (h + 1) * HG_HEAD_DIM)
    lbh = lb[:, cols]
    sg = _sigmoid(f_ref[:, cols])
    forget = lbh + (1.0 - lbh) * sg
    logf = jnp.log(forget)
    kk = (1.0 - lbh) * (1.0 - sg)
    qv = q_ref[:, cols]
    qsig = _sigmoid(qv)
    qh = qv * qsig
    b = _rows_dot(upto, logf)
    blast = _rows_dot(whole, logf)
    return dict(lbh=lbh, sg=sg, forget=forget, kk=kk, qv=qv, qsig=qsig, qh=qh, b=b, eb=jnp.exp(b),
                ekb=jnp.exp(blast - b), dl=jnp.exp(blast))


def _hgrn_fwd(proj, logits, *, name):
    t = proj.shape[0]
    tb = HG_BLOCK
    nc = tb // HG_CHUNK
    hd = HG_HEAD_DIM

    def body(q_ref, f_ref, i_ref, lg_ref, o_ref, st_ref, state, qh_s, kk_s, b_s, qe_s, ke_s, dl_s):
        @pl.when(pl.program_id(0) == 0)
        def _():
            state[...] = jnp.zeros_like(state)

        lb = _lower_bound(lg_ref)
        upto, whole, _ = _chunk_mats(tb)
        for h in range(HG_HEADS):
            p = _hgrn_prepare(q_ref, f_ref, lb, h, upto, whole)
            qh_s[h] = p["qh"]
            kk_s[h] = p["kk"]
            b_s[h] = p["b"]
            qe_s[h] = (p["qh"] * p["eb"]).astype(BF16)
            ke_s[h] = (p["kk"] * p["ekb"]).astype(BF16)
            dl_s[h] = p["dl"]
        rowi = lax.broadcasted_iota(jnp.int32, (HG_HEADS, HG_CHUNK, hd), 1)

        def chunk(c, _):
            r0 = pl.multiple_of(c * HG_CHUNK, HG_CHUNK)
            rows = pl.ds(r0, HG_CHUNK)
            bc = b_s[:, rows, :]
            qc = qh_s[:, rows, :]
            kc = kk_s[:, rows, :]
            vc = _split_heads(i_ref[rows, :])
            s_in = state[...]
            st_ref[c] = s_in
            s_in_b = s_in.astype(BF16)
            qe = qe_s[:, rows, :]
            o = jnp.stack([_dot_nt(qe[h], s_in_b[h]) for h in range(HG_HEADS)], axis=0)
            for s in range(HG_CHUNK):
                pair = jnp.where(rowi >= s, qc * jnp.exp(bc - bc[:, s:s + 1, :]) * kc[:, s:s + 1, :], 0.0)
                o = o + jnp.sum(pair, axis=2, keepdims=True) * vc[:, s:s + 1, :]
            o_ref[rows, :] = _merge_heads(o)
            vcb = vc.astype(BF16)
            ke = ke_s[:, rows, :]
            update = jnp.stack([_dot_tn(vcb[h], ke[h]) for h in range(HG_HEADS)], axis=0)
            state[...] = s_in * dl_s[:, pl.ds(r0, 1), :] + update
            return 0

        lax.fori_loop(0, nc, chunk, 0)

    blk = lambda col: pl.BlockSpec((tb, HG_WIDTH), lambda i: (i, col))
    head_f32 = pltpu.VMEM((HG_HEADS, tb, hd), F32)
    head_bf16 = pltpu.VMEM((HG_HEADS, tb, hd), BF16)
    return pl.pallas_call(
        body,
        name=name,
        grid=(t // tb,),
        in_specs=[blk(3), blk(4), blk(5), pl.BlockSpec((2, HG_WIDTH), lambda i: (0, 0))],
        out_specs=[
            pl.BlockSpec((tb, HG_WIDTH), lambda i: (i, 0)),
            pl.BlockSpec((nc, HG_HEADS, hd, hd), lambda i: (i, 0, 0, 0)),
        ],
        out_shape=[
            jax.ShapeDtypeStruct((t, HG_WIDTH), F32),
            jax.ShapeDtypeStruct((t // HG_CHUNK, HG_HEADS, hd, hd), F32),
        ],
        scratch_shapes=[pltpu.VMEM((HG_HEADS, hd, hd), F32), head_f32, head_f32, head_f32, head_bf16, head_bf16,
                        head_f32],
        compiler_params=_params("arbitrary"),
    )(proj, proj, proj, logits)


def _hgrn_bwd(proj, logits, states, do, *, name):
    t = proj.shape[0]
    tb = HG_BLOCK
    nb = t // tb
    nc = tb // HG_CHUNK
    hd = HG_HEAD_DIM

    def body(q_ref, f_ref, i_ref, lg_ref, st_ref, do_ref, dq_ref, df_ref, di_ref, dlb_ref,
             dstate, qh_s, kk_s, b_s, eb_s, ekb_s, qe_s, ke_s, dl_s, dqh_s, dkk_s, dlf_s):
        step = pl.program_id(0)

        @pl.when(step == 0)
        def _():
            dstate[...] = jnp.zeros_like(dstate)
            dlb_ref[...] = jnp.zeros_like(dlb_ref)

        lb = _lower_bound(lg_ref)
        upto, whole, _ = _chunk_mats(tb)
        prepared = []
        for h in range(HG_HEADS):
            p = _hgrn_prepare(q_ref, f_ref, lb, h, upto, whole)
            prepared.append(p)
            qh_s[h] = p["qh"]
            kk_s[h] = p["kk"]
            b_s[h] = p["b"]
            eb_s[h] = p["eb"]
            ekb_s[h] = p["ekb"]
            qe_s[h] = (p["qh"] * p["eb"]).astype(BF16)
            ke_s[h] = (p["kk"] * p["ekb"]).astype(BF16)
            dl_s[h] = p["dl"]
        rowi = lax.broadcasted_iota(jnp.int32, (HG_CHUNK, hd), 0)
        r16 = lax.broadcasted_iota(jnp.int32, (HG_CHUNK, HG_CHUNK), 0)
        c16 = lax.broadcasted_iota(jnp.int32, (HG_CHUNK, HG_CHUNK), 1)
        onward = (c16 >= r16).astype(BF16)

        def chunk(it, _):
            c = nc - 1 - it
            r0 = pl.multiple_of(c * HG_CHUNK, HG_CHUNK)
            rows = pl.ds(r0, HG_CHUNK)
            for h in range(HG_HEADS):
                cols = slice(h * hd, (h + 1) * hd)
                bc = b_s[h, rows, :]
                qc = qh_s[h, rows, :]
                kc = kk_s[h, rows, :]
                vc = i_ref[rows, cols]
                doc = do_ref[rows, cols]
                s_in = st_ref[c, h]
                ds_out = dstate[h]
                ds_out_b = ds_out.astype(BF16)
                docb = doc.astype(BF16)
                dl_row = dl_s[h, pl.ds(r0, 1), :]
                dqh = _dot(docb, s_in.astype(BF16)) * eb_s[h, rows, :]
                dkk = _dot(vc.astype(BF16), ds_out_b) * ekb_s[h, rows, :]
                dv = _dot_nt(ke_s[h, rows, :], ds_out_b)
                db = dqh * qc - dkk * kc
                dwhole = jnp.sum(dkk * kc, axis=0, keepdims=True) + jnp.sum(ds_out * s_in, axis=0, keepdims=True) * dl_row
                dk_rows, dv_rows = [], []
                for s in range(HG_CHUNK):
                    keep = rowi >= s
                    e = jnp.exp(bc - bc[s:s + 1, :])
                    k_row = kc[s:s + 1, :]
                    pcol = jnp.sum(jnp.where(keep, qc * e * k_row, 0.0), axis=1, keepdims=True)
                    dpcol = jnp.sum(doc * vc[s:s + 1, :], axis=1, keepdims=True)
                    m = jnp.where(keep, e * dpcol, 0.0)
                    y = m * qc
                    dqh = dqh + m * k_row
                    db = db + y * k_row
                    dk_rows.append(jnp.sum(y, axis=0, keepdims=True))
                    dv_rows.append(jnp.sum(pcol * doc, axis=0, keepdims=True))
                dkk_pairs = jnp.concatenate(dk_rows, axis=0)
                dkk = dkk + dkk_pairs
                db = db - dkk_pairs * kc
                dv = dv + jnp.concatenate(dv_rows, axis=0)
                dqh_s[h, rows, :] = dqh
                dkk_s[h, rows, :] = dkk
                dlf_s[h, rows, :] = _rows_dot(onward, db) + dwhole
                di_ref[rows, cols] = dv.astype(BF16)
                dstate[h] = ds_out * dl_row + _dot_tn(docb, qe_s[h, rows, :])
            return 0

        lax.fori_loop(0, nc, chunk, 0)
        for h in range(HG_HEADS):
            cols = slice(h * hd, (h + 1) * hd)
            p = prepared[h]
            dq_ref[:, cols] = (dqh_s[h] * (p["qsig"] * (1.0 + p["qv"] * (1.0 - p["qsig"])))).astype(BF16)
            dforget = dlf_s[h] / p["forget"] - dkk_s[h]
            df_ref[:, cols] = (dforget * (1.0 - p["lbh"]) * p["sg"] * (1.0 - p["sg"])).astype(BF16)
            dlb_ref[:, cols] += jnp.sum(dforget * (1.0 - p["sg"]), axis=0, keepdims=True)

    blk = lambda col: pl.BlockSpec((tb, HG_WIDTH), lambda i: (nb - 1 - i, col))
    vec = pl.BlockSpec((1, HG_WIDTH), lambda i: (0, 0))
    head_f32 = pltpu.VMEM((HG_HEADS, tb, hd), F32)
    head_bf16 = pltpu.VMEM((HG_HEADS, tb, hd), BF16)
    return pl.pallas_call(
        body,
        name=name,
        grid=(nb,),
        in_specs=[
            blk(3), blk(4), blk(5),
            pl.BlockSpec((2, HG_WIDTH), lambda i: (0, 0)),
            pl.BlockSpec((nc, HG_HEADS, hd, hd), lambda i: (nb - 1 - i, 0, 0, 0)),
            blk(0),
        ],
        out_specs=[blk(0), blk(0), blk(0), vec],
        out_shape=[jax.ShapeDtypeStruct((t, HG_WIDTH), BF16)] * 3 + [jax.ShapeDtypeStruct((1, HG_WIDTH), F32)],
        scratch_shapes=[
            pltpu.VMEM((HG_HEADS, hd, hd), F32),
            head_f32, head_f32, head_f32, head_f32, head_f32, head_bf16, head_bf16, head_f32,
            head_f32, head_f32, head_f32,
        ],
        compiler_params=_params("arbitrary"),
    )(proj, proj, proj, logits, states, do)


def _group_mat(width, head_dim):
    r = lax.broadcasted_iota(jnp.int32, (width, width), 0)
    c = lax.broadcasted_iota(jnp.int32, (width, width), 1)
    return ((r // head_dim) == (c // head_dim)).astype(BF16)


def _head_mean(x, mat, head_dim):
    hi = x.astype(BF16)
    lo = (x - hi.astype(F32)).astype(BF16)
    return (_dot(hi, mat) + _dot(lo, mat)) * (1.0 / head_dim)


def _mix_out_fwd(o_sb, o_hg, proj, g_sb, g_hg, w_out, x1, *, name, tm=512):
    t = x1.shape[0]

    def body(osb_ref, ohg_ref, gate_ref, gsb_ref, ghg_ref, w_ref, x_ref, xo_ref, mt_ref):
        msb = _group_mat(SB_WIDTH, SB_HEAD_DIM)
        mhg = _group_mat(HG_WIDTH, HG_HEAD_DIM)
        osb = osb_ref[...]
        ohg = ohg_ref[...]
        nsb = osb * lax.rsqrt(_head_mean(osb * osb, msb, SB_HEAD_DIM) + EPS) * gsb_ref[...]
        gate = gate_ref[...]
        nhg = ohg * lax.rsqrt(_head_mean(ohg * ohg, mhg, HG_HEAD_DIM) + EPS) * ghg_ref[...] * (gate * _sigmoid(gate))
        mixed = jnp.concatenate([nsb, nhg], axis=1).astype(BF16)
        mt_ref[...] = mixed
        xo_ref[...] = x_ref[...] + _dot(mixed, w_ref[...])

    half = pl.BlockSpec((tm, SB_WIDTH), lambda i: (i, 0))
    vec = pl.BlockSpec((1, SB_WIDTH), lambda i: (0, 0))
    row = pl.BlockSpec((tm, D_MODEL), lambda i: (i, 0))
    return pl.pallas_call(
        body,
        name=name,
        grid=(t // tm,),
        in_specs=[half, half, pl.BlockSpec((tm, HG_WIDTH), lambda i: (i, 6)), vec, vec,
                  pl.BlockSpec((D_MODEL, D_MODEL), lambda i: (0, 0)), row],
        out_specs=[row, row],
        out_shape=[jax.ShapeDtypeStruct((t, D_MODEL), F32), jax.ShapeDtypeStruct((t, D_MODEL), BF16)],
        compiler_params=_params("parallel"),
    )(o_sb, o_hg, proj, g_sb, g_hg, w_out, x1)


def _mix_out_bwd(dx2, o_sb, o_hg, proj, g_sb, g_hg, w_out, *, name, tm=512):
    t = dx2.shape[0]

    def body(dx_ref, osb_ref, ohg_ref, gate_ref, gsb_ref, ghg_ref, w_ref, dosb_ref, dohg_ref, dgate_ref, dgsb_ref,
             dghg_ref, dxb_ref):
        i = pl.program_id(0)
        msb = _group_mat(SB_WIDTH, SB_HEAD_DIM)
        mhg = _group_mat(HG_WIDTH, HG_HEAD_DIM)
        dxb = dx_ref[...].astype(BF16)
        dxb_ref[...] = dxb
        dmixed = _dot_nt(dxb, w_ref[...])
        dnsb = dmixed[:, :SB_WIDTH]
        dy = dmixed[:, SB_WIDTH:]

        osb = osb_ref[...]
        rstd = lax.rsqrt(_head_mean(osb * osb, msb, SB_HEAD_DIM) + EPS)
        ohat = osb * rstd
        part_sb = jnp.sum(dnsb * ohat, axis=0, keepdims=True)
        dohat = dnsb * gsb_ref[...]
        dosb_ref[...] = rstd * (dohat - ohat * _head_mean(dohat * ohat, msb, SB_HEAD_DIM))

        ohg = ohg_ref[...]
        rstd = lax.rsqrt(_head_mean(ohg * ohg, mhg, HG_HEAD_DIM) + EPS)
        ohat = ohg * rstd
        gate = gate_ref[...]
        sig = _sigmoid(gate)
        dn = dy * (gate * sig)
        dgate_ref[...] = (dy * (ohat * ghg_ref[...]) * (sig * (1.0 + gate * (1.0 - sig)))).astype(BF16)
        part_hg = jnp.sum(dn * ohat, axis=0, keepdims=True)
        dohat = dn * ghg_ref[...]
        dohg_ref[...] = rstd * (dohat - ohat * _head_mean(dohat * ohat, mhg, HG_HEAD_DIM))

        @pl.when(i == 0)
        def _():
            dgsb_ref[...] = part_sb
            dghg_ref[...] = part_hg

        @pl.when(i > 0)
        def _():
            dgsb_ref[...] += part_sb
            dghg_ref[...] += part_hg

    half = pl.BlockSpec((tm, SB_WIDTH), lambda i: (i, 0))
    vec = pl.BlockSpec((1, SB_WIDTH), lambda i: (0, 0))
    row = pl.BlockSpec((tm, D_MODEL), lambda i: (i, 0))
    return pl.pallas_call(
        body,
        name=name,
        grid=(t // tm,),
        in_specs=[row, half, half, pl.BlockSpec((tm, HG_WIDTH), lambda i: (i, 6)), vec, vec,
                  pl.BlockSpec((D_MODEL, D_MODEL), lambda i: (0, 0))],
        out_specs=[half, half, half, vec, vec, row],
        out_shape=[jax.ShapeDtypeStruct((t, SB_WIDTH), F32)] * 2 + [jax.ShapeDtypeStruct((t, SB_WIDTH), BF16)]
        + [jax.ShapeDtypeStruct((1, SB_WIDTH), F32)] * 2 + [jax.ShapeDtypeStruct((t, D_MODEL), BF16)],
        compiler_params=_params("arbitrary"),
    )(dx2, o_sb, o_hg, proj, g_sb, g_hg, w_out)


def _local_step(x, target, norms, logits, w, weights_after=None, grads_ready=None):
    w = dict(w)
    x1, a1, b1, h1, s1, hm = _ffn_fwd(x, norms["ffn1"], w["g1t"], w["u1t"], w["d1"], name="ffn1_fwd",
                                      next_gain=norms["mix"])
    if weights_after is not None:
        w.update(weights_after("ffn1", x1))
    proj = _mm(hm, w["int"], name="in_proj", tm=512, tn=IN_COLS, nt=True)
    o_sb, sb_kept = _attn_fwd(proj, name="sb_attn_fwd")
    o_hg, states = _hgrn_fwd(proj, logits, name="hgrn2_fwd")
    x2, mixed = _mix_out_fwd(o_sb, o_hg, proj, norms["sb"], norms["hg"], w["out"], x1, name="mix_out_fwd")
    if weights_after is not None:
        w.update(weights_after("mix", x2))
    dx3, a2, b2, h2, s2, d_final, loss_row = _ffn_fwd(x2, norms["ffn2"], w["g2t"], w["u2t"], w["d2"], name="ffn2_fwd",
                                                      head=(norms["final"], target))

    def weight_grad(lhs, rhs, name, tie=None):
        return _mm(lhs, rhs, name=name, tm=256, tn=D_MODEL, ta=True, out_dtype=BF16, tie=tie)

    def sent(stage):
        return grads_ready(stage, gw) if grads_ready is not None else None

    gw, gv = {}, {"final": d_final}
    dx2, gv["ffn2"], da2, db2, dob2 = _ffn_bwd(dx3, x2, norms["ffn2"], a2, b2, w["g2t"], w["u2t"], w["d2"],
                                               name="ffn2_bwd")
    gw["g2t"] = weight_grad(da2, h2, "ffn2_dgate")
    gw["u2t"] = weight_grad(db2, h2, "ffn2_dup")
    gw["d2"] = weight_grad(s2, dob2, "ffn2_ddown")

    do_sb, do_hg, d_gate, gv["sb"], gv["hg"], dx2b = _mix_out_bwd(
        dx2, o_sb, o_hg, proj, norms["sb"], norms["hg"], w["out"], name="mix_out_bwd")
    gw["out"] = weight_grad(mixed, dx2b, "out_dw")
    tie = sent("mix")
    dq_sb, dk_sb, dv_sb = _attn_bwd(proj, sb_kept, do_sb, name="sb_attn_bwd", tie=tie)
    dq_hg, df_hg, di_hg, d_lb = _hgrn_bwd(proj, logits if tie is None else logits + tie[0, 0], states, do_hg,
                                          name="hgrn2_bwd")
    dproj = jnp.concatenate([dq_sb, dk_sb.astype(BF16), dv_sb.astype(BF16), dq_hg, df_hg, di_hg, d_gate], axis=1)
    gw["int"] = weight_grad(dproj, hm, "in_dw")
    tie = sent("in")
    dx1, gv["mix"] = _in_proj_bwd(dproj, w["int"], x1, norms["mix"] if tie is None else norms["mix"] + tie[0, 0], dx2,
                                  name="in_dx")

    dx, gv["ffn1"], da1, db1, dob1 = _ffn_bwd(dx1, x, norms["ffn1"], a1, b1, w["g1t"], w["u1t"], w["d1"],
                                              name="ffn1_bwd")
    gw["g1t"] = weight_grad(da1, h1, "ffn1_dgate")
    gw["u1t"] = weight_grad(db1, h1, "ffn1_dup", tie=sent("g1t"))
    gw["d1"] = weight_grad(s1, dob1, "ffn1_ddown", tie=sent("u1t"))
    sent("d1")
    gv["lb"] = d_lb
    return loss_row, dx, gw, gv


HBM = pl.BlockSpec(memory_space=pl.ANY)


def _place():
    return lax.axis_index("x"), lax.axis_index("y"), lax.axis_index("c")


def _slot(px, py, pc):
    return 4 * px + 2 * py + pc


def _all_gather(blocks, *, name):
    n = len(blocks)

    def body(*refs):
        ins, outs = refs[:n], refs[n:2 * n]
        send_sems, recv_sems, local_sems = refs[2 * n:]
        x, y, c = _place()
        me, sibling = (x, y, c), (x, y, 1 - c)
        chips = [(1 - x, y), (x, 1 - y), (1 - x, 1 - y)]

        def copy(a, k, block, to, src=None):
            dst = outs[a].at[_slot(*block)]
            return pltpu.make_async_remote_copy(
                src_ref=dst if src is None else src, dst_ref=dst, send_sem=send_sems.at[7 * a + k],
                recv_sem=recv_sems.at[7 * a + k], device_id=to, device_id_type=MESH)

        mine = [pltpu.make_async_copy(ins[a], outs[a].at[_slot(*me)], local_sems.at[a]) for a in range(n)]
        for cp in mine:
            cp.start()
        first = []
        for a in range(n):
            first.append(copy(a, 0, me, sibling, src=ins[a]))
            first += [copy(a, 1 + j, me, (*chip, c), src=ins[a]) for j, chip in enumerate(chips)]
        for cp in first:
            cp.start()
        passed = []
        for j, chip in enumerate(chips):
            for a in range(n):
                copy(a, 1 + j, (*chip, c), me).wait_recv()
                fwd = copy(a, 4 + j, (*chip, c), sibling)
                fwd.start()
                passed.append(fwd)
        for a in range(n):
            copy(a, 0, sibling, me).wait_recv()
            for j, chip in enumerate(chips):
                copy(a, 4 + j, (*chip, 1 - c), me).wait_recv()
        for cp in first + passed:
            cp.wait_send()
        for cp in mine:
            cp.wait()

    return pl.pallas_call(
        body,
        name=name,
        in_specs=[HBM] * n,
        out_specs=[HBM] * n,
        out_shape=[jax.ShapeDtypeStruct((N_DEV,) + b.shape, b.dtype) for b in blocks],
        scratch_shapes=[pltpu.SemaphoreType.DMA((7 * n,)), pltpu.SemaphoreType.DMA((7 * n,)),
                        pltpu.SemaphoreType.DMA((n,))],
    )(*blocks)


def _flipped(place, d):
    return tuple(1 - p if (d >> (2 - axis)) & 1 else p for axis, p in enumerate(place))


SEM = pl.BlockSpec(memory_space=pltpu.SEMAPHORE)
EFFECT = pltpu.SideEffectType.DATAFLOW_SIDE_EFFECTING


def _split_copies(me, srcs, lands, send_sems, recv_sems, by_owner):
    copies = []
    for d in range(1, N_DEV):
        peer = _flipped(me, d)
        for a, (src, land) in enumerate(zip(srcs, lands)):
            copies.append(pltpu.make_async_remote_copy(
                src_ref=src.at[_slot(*peer)] if by_owner else src, dst_ref=land.at[_slot(*me)],
                send_sem=send_sems.at[7 * a + d - 1], recv_sem=recv_sems.at[7 * a + d - 1], device_id=peer,
                device_id_type=MESH))
    return copies


def _copies_start(srcs, *, name, by_owner, after=None):
    n = len(srcs)
    extra = [] if after is None else [after]
    land_shapes = [s.shape if by_owner else (N_DEV,) + s.shape for s in srcs]
    lands = [pltpu.with_memory_space_constraint(lax.empty(shape, s.dtype), pltpu.HBM) for shape, s in zip(land_shapes, srcs)]
    srcs = [pltpu.with_memory_space_constraint(s, pltpu.HBM) for s in srcs]

    def body(*refs):
        src_refs, land_refs = refs[:n], refs[n:2 * n]
        send_sems, recv_sems = refs[2 * n + len(extra)], refs[2 * n + len(extra) + 1]
        token = refs[-1]
        for cp in _split_copies(_place(), src_refs, land_refs, send_sems, recv_sems, by_owner):
            cp.start()
        token[...] = jnp.zeros_like(token)

    out = pl.pallas_call(
        body,
        name=name,
        in_specs=[HBM] * (2 * n + len(extra)),
        out_specs=[SEM, SEM] + [HBM] * (2 * n) + [pl.BlockSpec(memory_space=pltpu.VMEM)],
        out_shape=[pltpu.SemaphoreType.DMA((7 * n,)), pltpu.SemaphoreType.DMA((7 * n,))]
        + [pltpu.HBM(s.shape, s.dtype) for s in srcs] + [pltpu.HBM(shape, s.dtype) for shape, s in zip(land_shapes, srcs)]
        + [jax.ShapeDtypeStruct((8, LANES), F32)],
        input_output_aliases={i: 2 + i for i in range(2 * n)},
        compiler_params=pltpu.CompilerParams(has_side_effects=EFFECT),
    )(*srcs, *lands, *extra)
    return (out[0], out[1], out[2:2 + n], out[2 + n:2 + 2 * n]), out[-1]


def _copies_wait(started, after, *, name, by_owner):
    send_sems, recv_sems, srcs, lands = started
    n = len(srcs)

    def body(*refs):
        src_refs, land_refs = refs[:n], refs[n:2 * n]
        for cp in _split_copies(_place(), src_refs, land_refs, refs[2 * n], refs[2 * n + 1], by_owner):
            cp.wait_send()
            cp.wait_recv()

    out = pl.pallas_call(
        body,
        name=name,
        in_specs=[HBM] * (2 * n) + [SEM, SEM, HBM],
        out_specs=[HBM] * (2 * n),
        out_shape=[pltpu.HBM(s.shape, s.dtype) for s in srcs] + [pltpu.HBM(s.shape, s.dtype) for s in lands],
        input_output_aliases={i: i for i in range(2 * n)},
        compiler_params=pltpu.CompilerParams(has_side_effects=EFFECT),
    )(*srcs, *lands, send_sems, recv_sems, after)
    return out[:n], out[n:]


def _with_own(lands, own, slot):
    zero = jnp.zeros((), jnp.int32)
    return [lax.dynamic_update_slice(land, o[None], (slot.astype(jnp.int32),) + (zero,) * o.ndim)
            for land, o in zip(lands, own)]


def _adamw(w, g, m, v):
    m = ADAM_B1 * m + (1.0 - ADAM_B1) * g
    v = ADAM_B2 * v + (1.0 - ADAM_B2) * (g * g)
    m_hat = m / (1.0 - ADAM_B1 ** ADAM_STEP)
    v_hat = v / (1.0 - ADAM_B2 ** ADAM_STEP)
    delta = -ADAM_LR * (m_hat / (jnp.sqrt(v_hat) + ADAM_EPS) + ADAM_WD * w)
    return delta, m, v


def _sum_and_update(parts, w, m, v, *, name, tie=None):
    _, rows, cols = w.shape
    tr = rows // 2

    def body(p_ref, w_ref, m_ref, v_ref, *rest):
        g_ref, d_ref, mo_ref, vo_ref = rest[-4:]
        g = p_ref[0].astype(F32)
        for s in range(1, N_DEV):
            g = g + p_ref[s].astype(F32)
        g_ref[0] = g
        d_ref[0], mo_ref[0], vo_ref[0] = _adamw(w_ref[0], g, m_ref[0], v_ref[0])

    flat = pl.BlockSpec((1, tr, cols), lambda i: (0, i, 0))
    return pl.pallas_call(
        body,
        name=name,
        grid=(rows // tr,),
        in_specs=[pl.BlockSpec((N_DEV, tr, cols), lambda i: (0, i, 0)), flat, flat, flat]
        + ([] if tie is None else [pl.BlockSpec(memory_space=pl.ANY)]),
        out_specs=[flat] * 4,
        out_shape=[jax.ShapeDtypeStruct((1, rows, cols), F32)] * 4,
        compiler_params=_params("parallel"),
    )(parts, w, m, v, *([] if tie is None else [tie]))


VEC_ROWS = 8
ROW_LOGITS, ROW_LOSS = 5, 7


def _vectors_update(part, w, m, v, *, name):
    def body(p_ref, w_ref, m_ref, v_ref, g_ref, d_ref, mo_ref, vo_ref, loss_ref, all_ref, send_sems, recv_sems):
        me = _place()
        all_ref[_slot(*me)] = p_ref[...]
        copies = []
        for d in range(1, N_DEV):
            peer = _flipped(me, d)
            copies.append(pltpu.make_async_remote_copy(
                src_ref=p_ref, dst_ref=all_ref.at[_slot(*me)], send_sem=send_sems.at[d - 1], recv_sem=recv_sems.at[d - 1],
                device_id=peer, device_id_type=MESH))
        for cp in copies:
            cp.start()
        for cp in copies:
            cp.wait()
        total = all_ref[0]
        for s in range(1, N_DEV):
            total = total + all_ref[s]
        wv = w_ref[...]
        half = D_MODEL // 2
        lb = _sigmoid(wv[ROW_LOGITS:ROW_LOGITS + 1, :half] - wv[ROW_LOGITS:ROW_LOGITS + 1, half:])
        d_first = total[ROW_LOGITS:ROW_LOGITS + 1, :half] * lb * (1.0 - lb)
        d_logits = jnp.concatenate([d_first, -d_first], axis=1)
        rowi = lax.broadcasted_iota(jnp.int32, (VEC_ROWS, D_MODEL), 0)
        g = jnp.where(rowi == ROW_LOGITS, d_logits, jnp.where(rowi < ROW_LOGITS, total, 0.0))
        g_ref[...] = g
        d_ref[...], mo_ref[...], vo_ref[...] = _adamw(wv, g, m_ref[...], v_ref[...])
        loss_ref[...] = total[ROW_LOSS:ROW_LOSS + 1, :]

    vmem = pl.BlockSpec(memory_space=pltpu.VMEM)
    return pl.pallas_call(
        body,
        name=name,
        in_specs=[vmem] * 4,
        out_specs=[vmem] * 5,
        out_shape=[jax.ShapeDtypeStruct((VEC_ROWS, D_MODEL), F32)] * 4 + [jax.ShapeDtypeStruct((1, D_MODEL), F32)],
        scratch_shapes=[pltpu.VMEM((N_DEV, VEC_ROWS, D_MODEL), F32), pltpu.SemaphoreType.DMA((7,)),
                        pltpu.SemaphoreType.DMA((7,))],
    )(part, w, m, v)


TRANSPOSED = ("g1t", "u1t", "g2t", "u2t", "int")


def _vector_rows(rows):
    rowi = lax.broadcasted_iota(jnp.int32, (VEC_ROWS, D_MODEL), 0)
    out = jnp.zeros((VEC_ROWS, D_MODEL), F32)
    for i, r in enumerate(rows):
        if r is not None:
            out = jnp.where(rowi == i, r, out)
    return out


def kernel(x, ffn1_norm, ffn1_w_gate, ffn1_w_up, ffn1_w_down, mix_norm, w_in, sb_out_norm, hg_lower_bound_logits, hg_out_norm, w_out, ffn2_norm, ffn2_w_gate, ffn2_w_up, ffn2_w_down, final_norm, loss_target, m_ffn1_norm, m_ffn1_w_gate, m_ffn1_w_up, m_ffn1_w_down, m_mix_norm, m_w_in, m_sb_out_norm, m_hg_lower_bound_logits, m_hg_out_norm, m_w_out, m_ffn2_norm, m_ffn2_w_gate, m_ffn2_w_up, m_ffn2_w_down, m_final_norm, v_ffn1_norm, v_ffn1_w_gate, v_ffn1_w_up, v_ffn1_w_down, v_mix_norm, v_w_in, v_sb_out_norm, v_hg_lower_bound_logits, v_hg_out_norm, v_w_out, v_ffn2_norm, v_ffn2_w_gate, v_ffn2_w_up, v_ffn2_w_down, v_final_norm):
    def matrices(g1, u1, d1, win, wout, g2, u2, d2):
        return {"g1t": g1, "u1t": u1, "d1": d1, "int": win, "out": wout, "g2t": g2, "u2t": u2, "d2": d2}

    def vectors(n1, nm, nsb, lg, nhg, n2, nf):
        return [n1, nm, n2, nf.reshape(1, D_MODEL), jnp.concatenate([nsb, nhg], axis=1), lg.reshape(1, D_MODEL), None, None]

    w_sh = matrices(ffn1_w_gate, ffn1_w_up, ffn1_w_down, w_in, w_out, ffn2_w_gate, ffn2_w_up, ffn2_w_down)
    m_sh = matrices(m_ffn1_w_gate, m_ffn1_w_up, m_ffn1_w_down, m_w_in, m_w_out, m_ffn2_w_gate, m_ffn2_w_up, m_ffn2_w_down)
    v_sh = matrices(v_ffn1_w_gate, v_ffn1_w_up, v_ffn1_w_down, v_w_in, v_w_out, v_ffn2_w_gate, v_ffn2_w_up, v_ffn2_w_down)
    keys = list(w_sh)

    slot = _slot(*_place())

    def full(key, stack):
        return stack.reshape(-1, D_MODEL)

    def by_owner(key, grad):
        return grad.reshape(N_DEV, -1, D_MODEL)

    def view(key, a):
        return jnp.swapaxes(a, 1, 2) if key in TRANSPOSED else a

    blocks = {k: view(k, w_sh[k])[0].astype(BF16) for k in keys}
    first, mid, last = ("g1t", "u1t", "d1"), ("int", "out"), ("g2t", "u2t", "d2")
    w_first = {k: full(k, s) for k, s in zip(first, _all_gather([blocks[k] for k in first], name="gather_ffn1"))}
    flights = {}
    flights["ffn1"], token_mid = _copies_start([blocks[k] for k in mid], name="gather_mid_start", by_owner=False,
                                               after=w_first["d1"])
    flights["mix"], token_last = _copies_start([blocks[k] for k in last], name="gather_ffn2_start", by_owner=False,
                                               after=token_mid)

    def weights_after(stage, result):
        group = mid if stage == "ffn1" else last
        own, lands = _copies_wait(flights[stage], result, name="gather_" + stage + "_wait", by_owner=False)
        return {k: full(k, s) for k, s in zip(group, _with_own(lands, own, slot))}

    groups = {"mix": ("g2t", "u2t", "d2", "out"), "in": ("int",), "g1t": ("g1t",), "u1t": ("u1t",), "d1": ("d1",)}
    sent, sent_tokens = {}, []

    def grads_ready(stage, gw):
        stacks = [by_owner(k, gw[k]) for k in groups[stage]]
        flight, token = _copies_start(stacks, name="grads_" + stage + "_start", by_owner=True)
        sent[stage] = flight
        sent_tokens.append(token)
        return token

    norms = {"ffn1": ffn1_norm + token_last[0, 0], "mix": mix_norm, "sb": sb_out_norm, "hg": hg_out_norm,
             "ffn2": ffn2_norm, "final": final_norm.reshape(1, D_MODEL)}
    loss_row, grad_x, gw, gv = _local_step(x[0], loss_target[0], norms, hg_lower_bound_logits, w_first, weights_after,
                                           grads_ready)

    updated, after = {}, sent_tokens[-1]
    for stage, flight in sent.items():
        stacks, lands = _copies_wait(flight, after, name="grads_" + stage + "_wait", by_owner=True)
        own = [lax.dynamic_index_in_dim(s, slot, keepdims=False) for s in stacks]
        for k, part in zip(groups[stage], _with_own(lands, own, slot)):
            updated[k] = _sum_and_update(part, view(k, w_sh[k]), view(k, m_sh[k]), view(k, v_sh[k]), name="adamw_" + k,
                                         tie=after)
            after = updated[k][0]
    mats = [{k: view(k, updated[k][i]) for k in keys} for i in range(4)]

    lb_row = jnp.concatenate([gv["lb"], jnp.zeros_like(gv["lb"])], axis=1)
    part = _vector_rows([gv["ffn1"], gv["mix"], gv["ffn2"], gv["final"], jnp.concatenate([gv["sb"], gv["hg"]], axis=1),
                         lb_row, None, loss_row])
    vec_w = _vector_rows(vectors(ffn1_norm, mix_norm, sb_out_norm, hg_lower_bound_logits, hg_out_norm, ffn2_norm, final_norm))
    vec_m = _vector_rows(vectors(m_ffn1_norm, m_mix_norm, m_sb_out_norm, m_hg_lower_bound_logits, m_hg_out_norm,
                                 m_ffn2_norm, m_final_norm))
    vec_v = _vector_rows(vectors(v_ffn1_norm, v_mix_norm, v_sb_out_norm, v_hg_lower_bound_logits, v_hg_out_norm,
                                 v_ffn2_norm, v_final_norm))
    *vecs, loss_out = _vectors_update(part, vec_w, vec_m, vec_v, name="vectors_update")

    def leaves(mat, vec):
        half = D_MODEL // 2
        return (
            vec[0:1], mat["g1t"], mat["u1t"], mat["d1"], vec[1:2], mat["int"], vec[4:5, :half],
            vec[ROW_LOGITS].reshape(2, half), vec[4:5, half:], mat["out"], vec[2:3], mat["g2t"], mat["u2t"],
            mat["d2"], vec[3],
        )

    out = [loss_out[0, 0], grad_x[None]]
    for mat, vec in zip(mats, vecs):
        out.extend(leaves(mat, vec))
    return tuple(out)
```

```python
import jax
import jax.numpy as jnp
from jax import lax
from jax.experimental import pallas as pl
from jax.experimental.pallas import tpu as pltpu

F32, BF16 = jnp.float32, jnp.bfloat16
D_MODEL = 1024
D_FF = 2816
SB_WIDTH = 512
HG_WIDTH = 512
SB_HEAD_DIM = 64
HG_HEAD_DIM = 128
IN_COLS = 3584
EPS = 1e-6
N_DEV = 8
LANES = 128
HG_CHUNK = 16
VMEM_LIMIT_BYTES = 48 * 1024 * 1024
FFN_BWD_VMEM_LIMIT_BYTES = 56 * 1024 * 1024
ATT_VMEM_LIMIT_BYTES = 56 * 1024 * 1024
ADAM_LR, ADAM_B1, ADAM_B2, ADAM_EPS, ADAM_WD, ADAM_STEP = 0.001, 0.9, 0.999, 1e-08, 0.01, 10
MESH = pl.DeviceIdType.MESH


def _params(*semantics, vmem_limit_bytes=VMEM_LIMIT_BYTES):
    return pltpu.CompilerParams(dimension_semantics=semantics, vmem_limit_bytes=vmem_limit_bytes)


def _dot(a, b):
    return jnp.dot(a, b, preferred_element_type=F32)


def _dot_nt(a, b):
    return lax.dot_general(a, b, (((1,), (1,)), ((), ())), preferred_element_type=F32)


def _dot_tn(a, b):
    return lax.dot_general(a, b, (((0,), (0,)), ((), ())), preferred_element_type=F32)


def _split3(x):
    hi = x.astype(BF16)
    r1 = x - hi.astype(F32)
    mid = r1.astype(BF16)
    lo = (r1 - mid.astype(F32)).astype(BF16)
    return hi, mid, lo


def _rms(xv):
    rstd = lax.rsqrt(jnp.mean(xv * xv, axis=-1, keepdims=True) + EPS)
    return xv * rstd, rstd


def _sigmoid(x):
    return 1.0 / (1.0 + jnp.exp(-x))


def _loss_terms(xv, gain, target):
    xhat, rstd = _rms(xv)
    err = xhat * gain - target
    loss = 0.5 * jnp.sum(jnp.mean(err * err, axis=-1, keepdims=True), axis=0, keepdims=True)
    dy = err * (1.0 / xv.shape[-1])
    dxh = dy * gain
    dx = rstd * (dxh - xhat * jnp.mean(dxh * xhat, axis=-1, keepdims=True))
    return dx, jnp.sum(dy * xhat, axis=0, keepdims=True), loss


def _mm(a, b, *, name, tm, tn, nt=False, ta=False, out_dtype=F32, tie=None):
    k, m = a.shape if ta else a.shape[::-1]
    n = b.shape[0] if nt else b.shape[1]
    assert m % tm == 0 and n % tn == 0 and not (nt and ta), (name, a.shape, b.shape, tm, tn)

    def body(a_ref, b_ref, *rest):
        av = a_ref[...].astype(BF16)
        bv = b_ref[...].astype(BF16)
        rest[-1][...] = (_dot_nt(av, bv) if nt else _dot_tn(av, bv) if ta else _dot(av, bv)).astype(out_dtype)

    in_specs = [
        pl.BlockSpec((k, tm), lambda i, j: (0, i)) if ta else pl.BlockSpec((tm, k), lambda i, j: (i, 0)),
        pl.BlockSpec((tn, k), lambda i, j: (j, 0)) if nt else pl.BlockSpec((k, tn), lambda i, j: (0, j)),
    ]
    operands = [a, b]
    if tie is not None:
        in_specs.append(pl.BlockSpec(memory_space=pl.ANY))
        operands.append(tie)
    return pl.pallas_call(
        body,
        name=name,
        grid=(m // tm, n // tn),
        in_specs=in_specs,
        out_specs=pl.BlockSpec((tm, tn), lambda i, j: (i, j)),
        out_shape=jax.ShapeDtypeStruct((m, n), out_dtype),
        compiler_params=_params("parallel", "parallel"),
    )(*operands)


def _ffn_fwd(x, gain, wgt, wut, wd, *, name, next_gain=None, head=None, tm=1024, tf=256):
    t = x.shape[0]
    nj = D_FF // tf
    extra_in = [] if next_gain is None else [next_gain]
    extra_in += [] if head is None else list(head)

    def body(x_ref, g_ref, wg_ref, wu_ref, wd_prev_ref, wd_last_ref, *rest):
        extra, (xo_ref, a_ref, b_ref, h_ref, st_ref) = rest[:len(extra_in)], rest[len(extra_in):len(extra_in) + 5]
        tail_out, (acc, s_prev) = rest[len(extra_in) + 5:-2], rest[-2:]
        i = pl.program_id(0)
        j = pl.program_id(1)

        @pl.when(j == 0)
        def _():
            xhat, _ = _rms(x_ref[...])
            h_ref[...] = (xhat * g_ref[...]).astype(BF16)
            acc[...] = jnp.zeros_like(acc)
            s_prev[...] = jnp.zeros_like(s_prev)

        acc[...] += _dot(s_prev[...], wd_prev_ref[...])
        h = h_ref[...]
        a = _dot_nt(h, wg_ref[...])
        b = _dot_nt(h, wu_ref[...])
        a_ref[...] = a.astype(BF16)
        b_ref[...] = b.astype(BF16)
        s = (a * _sigmoid(a) * b).astype(BF16)
        st_ref[...] = s
        s_prev[...] = s

        @pl.when(j == nj - 1)
        def _():
            xo = x_ref[...] + 0.5 * (acc[...] + _dot(s, wd_last_ref[...]))
            if head is None:
                xo_ref[...] = xo
            if next_gain is not None:
                tail_out[0][...] = (_rms(xo)[0] * extra[0][...]).astype(BF16)
            if head is not None:
                gain_ref, target_ref = extra[-2:]
                dg_ref, loss_ref = tail_out[-2:]
                xo_ref[...], part_g, part_loss = _loss_terms(xo, gain_ref[...], target_ref[...])

                @pl.when(i == 0)
                def _():
                    dg_ref[...] = part_g
                    loss_ref[...] = jnp.broadcast_to(part_loss, loss_ref.shape)

                @pl.when(i > 0)
                def _():
                    dg_ref[...] += part_g
                    loss_ref[...] += jnp.broadcast_to(part_loss, loss_ref.shape)

    row = pl.BlockSpec((tm, D_MODEL), lambda i, j: (i, 0))
    vec = pl.BlockSpec((1, D_MODEL), lambda i, j: (0, 0))
    tile = pl.BlockSpec((tm, tf), lambda i, j: (i, j))
    weights = pl.BlockSpec((tf, D_MODEL), lambda i, j: (j, 0))
    tail_specs = ([] if next_gain is None else [row]) + ([] if head is None else [vec, vec])
    tail_shapes = ([] if next_gain is None else [jax.ShapeDtypeStruct((t, D_MODEL), BF16)]) + (
        [] if head is None else [jax.ShapeDtypeStruct((1, D_MODEL), F32)] * 2)
    return pl.pallas_call(
        body,
        name=name,
        grid=(t // tm, nj),
        in_specs=[
            row, vec, weights, weights,
            pl.BlockSpec((tf, D_MODEL), lambda i, j: (jnp.maximum(j - 1, 0), 0)),
            pl.BlockSpec((tf, D_MODEL), lambda i, j: (nj - 1, 0)),
        ] + ([] if next_gain is None else [vec]) + ([] if head is None else [vec, row]),
        out_specs=[row, tile, tile, row, tile] + tail_specs,
        out_shape=[
            jax.ShapeDtypeStruct((t, D_MODEL), F32),
            jax.ShapeDtypeStruct((t, D_FF), BF16),
            jax.ShapeDtypeStruct((t, D_FF), BF16),
            jax.ShapeDtypeStruct((t, D_MODEL), BF16),
            jax.ShapeDtypeStruct((t, D_FF), BF16),
        ] + tail_shapes,
        scratch_shapes=[pltpu.VMEM((tm, D_MODEL), F32), pltpu.VMEM((tm, tf), BF16)],
        compiler_params=_params("arbitrary", "arbitrary"),
    )(x, gain, wgt, wut, wd, wd, *extra_in)


def _ffn_bwd(dout, x, gain, a, b, wgt, wut, wd, *, name, tm=1024, tf=256):
    t = x.shape[0]
    nj = D_FF // tf

    def body(do_ref, x_ref, g_ref, a_ref, b_ref, wg_prev_ref, wu_prev_ref, wg_last_ref, wu_last_ref, wd_ref,
             dx_ref, dg_ref, da_ref, db_ref, dob_ref, dob_scr, dh, da_prev, db_prev):
        i = pl.program_id(0)
        j = pl.program_id(1)

        @pl.when(j == 0)
        def _():
            d = (0.5 * do_ref[...]).astype(BF16)
            dob_scr[...] = d
            dob_ref[...] = d
            dh[...] = jnp.zeros_like(dh)
            da_prev[...] = jnp.zeros_like(da_prev)
            db_prev[...] = jnp.zeros_like(db_prev)

        dh[...] += _dot(da_prev[...], wg_prev_ref[...]) + _dot(db_prev[...], wu_prev_ref[...])
        ds = _dot_nt(dob_scr[...], wd_ref[...])
        av = a_ref[...].astype(F32)
        bv = b_ref[...].astype(F32)
        sig = _sigmoid(av)
        dbv = (ds * (av * sig)).astype(BF16)
        dav = (ds * bv * (sig * (1.0 + av * (1.0 - sig)))).astype(BF16)
        da_ref[...] = dav
        db_ref[...] = dbv
        da_prev[...] = dav
        db_prev[...] = dbv

        @pl.when(j == nj - 1)
        def _():
            xhat, rstd = _rms(x_ref[...])
            dhv = dh[...] + _dot(dav, wg_last_ref[...]) + _dot(dbv, wu_last_ref[...])
            part = jnp.sum(dhv * xhat, axis=0, keepdims=True)

            @pl.when(i == 0)
            def _():
                dg_ref[...] = part

            @pl.when(i > 0)
            def _():
                dg_ref[...] += part

            dxh = dhv * g_ref[...]
            dx_ref[...] = do_ref[...] + rstd * (dxh - xhat * jnp.mean(dxh * xhat, axis=-1, keepdims=True))

    return pl.pallas_call(
        body,
        name=name,
        grid=(t // tm, nj),
        in_specs=[
            pl.BlockSpec((tm, D_MODEL), lambda i, j: (i, 0)),
            pl.BlockSpec((tm, D_MODEL), lambda i, j: (i, 0)),
            pl.BlockSpec((1, D_MODEL), lambda i, j: (0, 0)),
            pl.BlockSpec((tm, tf), lambda i, j: (i, j)),
            pl.BlockSpec((tm, tf), lambda i, j: (i, j)),
            pl.BlockSpec((tf, D_MODEL), lambda i, j: (jnp.maximum(j - 1, 0), 0)),
            pl.BlockSpec((tf, D_MODEL), lambda i, j: (jnp.maximum(j - 1, 0), 0)),
            pl.BlockSpec((tf, D_MODEL), lambda i, j: (nj - 1, 0)),
            pl.BlockSpec((tf, D_MODEL), lambda i, j: (nj - 1, 0)),
            pl.BlockSpec((tf, D_MODEL), lambda i, j: (j, 0)),
        ],
        out_specs=[
            pl.BlockSpec((tm, D_MODEL), lambda i, j: (i, 0)),
            pl.BlockSpec((1, D_MODEL), lambda i, j: (0, 0)),
            pl.BlockSpec((tm, tf), lambda i, j: (i, j)),
            pl.BlockSpec((tm, tf), lambda i, j: (i, j)),
            pl.BlockSpec((tm, D_MODEL), lambda i, j: (i, 0)),
        ],
        out_shape=[
            jax.ShapeDtypeStruct((t, D_MODEL), F32),
            jax.ShapeDtypeStruct((1, D_MODEL), F32),
            jax.ShapeDtypeStruct((t, D_FF), BF16),
            jax.ShapeDtypeStruct((t, D_FF), BF16),
            jax.ShapeDtypeStruct((t, D_MODEL), BF16),
        ],
        scratch_shapes=[pltpu.VMEM((tm, D_MODEL), BF16), pltpu.VMEM((tm, D_MODEL), F32), pltpu.VMEM((tm, tf), BF16),
                        pltpu.VMEM((tm, tf), BF16)],
        compiler_params=_params("arbitrary", "arbitrary", vmem_limit_bytes=FFN_BWD_VMEM_LIMIT_BYTES),
    )(dout, x, gain, a, b, wgt, wut, wgt, wut, wd)


def _in_proj_bwd(dproj, w_int, x, gain, dres, *, name, tm=512):
    t, k = dproj.shape

    def body(dp_ref, w_ref, x_ref, g_ref, dr_ref, dx_ref, dg_ref):
        i = pl.program_id(0)
        dhv = _dot(dp_ref[...], w_ref[...])
        xhat, rstd = _rms(x_ref[...])
        part = jnp.sum(dhv * xhat, axis=0, keepdims=True)

        @pl.when(i == 0)
        def _():
            dg_ref[...] = part

        @pl.when(i > 0)
        def _():
            dg_ref[...] += part

        dxh = dhv * g_ref[...]
        dx_ref[...] = dr_ref[...] + rstd * (dxh - xhat * jnp.mean(dxh * xhat, axis=-1, keepdims=True))

    row = pl.BlockSpec((tm, D_MODEL), lambda i: (i, 0))
    vec = pl.BlockSpec((1, D_MODEL), lambda i: (0, 0))
    return pl.pallas_call(
        body,
        name=name,
        grid=(t // tm,),
        in_specs=[pl.BlockSpec((tm, k), lambda i: (i, 0)), pl.BlockSpec((k, D_MODEL), lambda i: (0, 0)), row, vec, row],
        out_specs=[row, vec],
        out_shape=[jax.ShapeDtypeStruct((t, D_MODEL), F32), jax.ShapeDtypeStruct((1, D_MODEL), F32)],
        compiler_params=_params("arbitrary"),
    )(dproj, w_int, x, gain, dres)


ATT_Q_TILE = 512
ATT_K_BLOCK = 256


def _first_head_lanes():
    return lax.broadcasted_iota(jnp.int32, (1, LANES), 1) < SB_HEAD_DIM


def _stack_heads(x):
    first = _first_head_lanes()
    return jnp.concatenate([jnp.where(first, x, 0.0), jnp.where(first, 0.0, x)], axis=0)


def _unstack_heads(x, rows):
    return jnp.where(_first_head_lanes(), x[:rows], x[rows:])


def _rows_from(x, first, rows):
    return x if first == 0 else jnp.concatenate([x[first:rows], x[rows + first:]], axis=0)


def _rows_into(full, part, first, rows):
    if first == 0:
        return part
    n = rows - first
    return jnp.concatenate([full[:first], part[:n], full[rows:rows + first], part[n:]], axis=0)


def _tri(n, relation):
    r = lax.broadcasted_iota(jnp.int32, (n, n), 0)
    c = lax.broadcasted_iota(jnp.int32, (n, n), 1)
    return relation(r, c).astype(BF16)


def _scan_dot(x, tri):
    hi = x.astype(BF16)
    lo = (x - hi.astype(F32)).astype(BF16)
    return _dot(jnp.concatenate([hi, lo], axis=1), jnp.concatenate([tri, tri], axis=0))


def _log_terms(z):
    lbeta = jnp.minimum(z, 0.0) - jnp.log(1.0 + jnp.exp(-jnp.abs(z)))
    return lbeta, lbeta - z


def _attn_fwd(proj, *, name):
    t = proj.shape[0]
    tq, tk = ATT_Q_TILE, ATT_K_BLOCK
    diag = tq // tk
    n_pairs = SB_WIDTH // LANES

    def body(q_ref, k_ref, v_ref, o_ref, kept_ref, sig_ref):
        qi = pl.program_id(1)
        q = q_ref[...] * (SB_HEAD_DIM ** -0.5)
        qs = _stack_heads(q).astype(BF16)
        tri = _tri(tk, lambda j, s: j > s)
        trow = lax.broadcasted_iota(jnp.int32, (tq, tk), 0)
        scol = lax.broadcasted_iota(jnp.int32, (tq, tk), 1)

        def block(kb, carry, causal, first=0):
            acc, c = carry
            off = pl.multiple_of(kb * tk, tk)
            lbeta, lrest = _log_terms(_dot_nt(_rows_from(qs, first, tq), k_ref[pl.ds(off, tk), :].astype(BF16)))
            if causal is not None:
                lrest = jnp.where(causal, lrest, 0.0)
            w = jnp.exp(lbeta + (_scan_dot(lrest, tri) + _rows_from(c, first, tq)))
            if causal is not None:
                w = jnp.where(causal, w, 0.0)
            wb = w.astype(BF16)
            kept_ref[0, 0, kb] = _rows_into(jnp.zeros((2 * tq, tk), BF16), wb, first, tq)
            sig_ref[0, 0, kb] = _rows_into(jnp.zeros((2 * tq, tk), BF16), jnp.exp(lbeta).astype(BF16), first, tq)
            acc = _rows_into(acc, _rows_from(acc, first, tq) + _dot(wb, v_ref[pl.ds(off, tk), :].astype(BF16)), first, tq)
            return acc, _rows_into(c, _rows_from(c, first, tq) + jnp.sum(lrest, axis=1, keepdims=True), first, tq)

        carry = (jnp.zeros((2 * tq, LANES), F32), jnp.zeros((2 * tq, 1), F32))
        n_full = qi * diag
        for j in reversed(range(diag)):
            mask = ((scol + j * tk) < trow)[j * tk:]
            carry = block(n_full + j, carry, jnp.concatenate([mask, mask], axis=0), first=j * tk)

        def step(it, carry):
            return block(n_full - 1 - it, carry, None)

        acc, _ = lax.fori_loop(0, n_full, step, carry)
        o_ref[...] = _unstack_heads(acc, tq)

    return pl.pallas_call(
        body,
        name=name,
        grid=(n_pairs, t // tq),
        in_specs=[
            pl.BlockSpec((tq, LANES), lambda p, i: (i, p)),
            pl.BlockSpec((t, LANES), lambda p, i: (0, n_pairs + p), pipeline_mode=pl.Buffered(1)),
            pl.BlockSpec((t, LANES), lambda p, i: (0, 2 * n_pairs + p), pipeline_mode=pl.Buffered(1)),
        ],
        out_specs=[pl.BlockSpec((tq, LANES), lambda p, i: (i, p)),
                   pl.BlockSpec((1, 1, t // tk, 2 * tq, tk), lambda p, i: (p, i, 0, 0, 0)),
                   pl.BlockSpec((1, 1, t // tk, 2 * tq, tk), lambda p, i: (p, i, 0, 0, 0))],
        out_shape=[jax.ShapeDtypeStruct((t, SB_WIDTH), F32)]
        + [jax.ShapeDtypeStruct((n_pairs, t // tq, t // tk, 2 * tq, tk), BF16)] * 2,
        compiler_params=_params("parallel", "parallel", vmem_limit_bytes=ATT_VMEM_LIMIT_BYTES),
    )(proj, proj, proj)


def _attn_bwd(proj, kept, sigmoids, do, *, name, tie=None):
    t = proj.shape[0]
    tq, tk = ATT_Q_TILE, ATT_K_BLOCK
    diag = tq // tk
    n_pairs = SB_WIDTH // LANES
    scale = SB_HEAD_DIM ** -0.5

    def body(q_ref, k_ref, v_ref, kept_ref, sig_ref, do_ref, *rest):
        dq_ref, dk_ref, dv_ref = rest[-3:]
        qi = pl.program_id(1)

        @pl.when(qi == 0)
        def _():
            dk_ref[...] = jnp.zeros_like(dk_ref)
            dv_ref[...] = jnp.zeros_like(dv_ref)

        qs = _stack_heads(q_ref[...] * scale).astype(BF16)
        dos = _stack_heads(do_ref[...]).astype(BF16)
        before = _tri(tk, lambda s, j: s < j)
        trow = lax.broadcasted_iota(jnp.int32, (tq, tk), 0)
        scol = lax.broadcasted_iota(jnp.int32, (tq, tk), 1)

        def block(kb, carry, causal, first=0):
            dq, cg = carry
            off = pl.multiple_of(kb * tk, tk)
            q_rows, do_rows = _rows_from(qs, first, tq), _rows_from(dos, first, tq)
            wb = _rows_from(kept_ref[0, 0, kb], first, tq)
            kblk = k_ref[pl.ds(off, tk), :].astype(BF16)
            sig = _rows_from(sig_ref[0, 0, kb], first, tq).astype(F32)
            g = wb.astype(F32) * _dot_nt(do_rows, v_ref[pl.ds(off, tk), :].astype(BF16))
            prior = _scan_dot(g, before) + _rows_from(cg, first, tq)
            dz = g - sig * (g + prior)
            if causal is not None:
                dz = jnp.where(causal, dz, 0.0)
            dzb = dz.astype(BF16)
            dq = _rows_into(dq, _rows_from(dq, first, tq) + _dot(dzb, kblk), first, tq)
            dk_ref[pl.ds(off, tk), :] += _dot_tn(dzb, q_rows)
            dv_ref[pl.ds(off, tk), :] += _dot_tn(wb, do_rows)
            return dq, _rows_into(cg, _rows_from(cg, first, tq) + jnp.sum(g, axis=1, keepdims=True), first, tq)

        n_full = qi * diag
        carry = lax.fori_loop(0, n_full, lambda kb, carry: block(kb, carry, None),
                              (jnp.zeros((2 * tq, LANES), F32), jnp.zeros((2 * tq, 1), F32)))
        for j in range(diag):
            mask = ((scol + j * tk) < trow)[j * tk:]
            carry = block(n_full + j, carry, jnp.concatenate([mask, mask], axis=0), first=j * tk)
        dq_ref[...] = (_unstack_heads(carry[0], tq) * scale).astype(BF16)

    tile_spec = pl.BlockSpec((tq, LANES), lambda p, i: (i, p))
    full_spec = pl.BlockSpec((t, LANES), lambda p, i: (0, p), pipeline_mode=pl.Buffered(1))
    kept_spec = pl.BlockSpec((1, 1, t // tk, 2 * tq, tk), lambda p, i: (p, i, 0, 0, 0))
    return pl.pallas_call(
        body,
        name=name,
        grid=(n_pairs, t // tq),
        in_specs=[
            tile_spec,
            pl.BlockSpec((t, LANES), lambda p, i: (0, n_pairs + p), pipeline_mode=pl.Buffered(1)),
            pl.BlockSpec((t, LANES), lambda p, i: (0, 2 * n_pairs + p), pipeline_mode=pl.Buffered(1)),
            kept_spec,
            kept_spec,
            tile_spec,
        ] + ([] if tie is None else [pl.BlockSpec(memory_space=pl.ANY)]),
        out_specs=[tile_spec, full_spec, full_spec],
        out_shape=[jax.ShapeDtypeStruct((t, SB_WIDTH), BF16)] + [jax.ShapeDtypeStruct((t, SB_WIDTH), F32)] * 2,
        compiler_params=_params("arbitrary", "arbitrary", vmem_limit_bytes=ATT_VMEM_LIMIT_BYTES),
    )(proj, proj, proj, kept, sigmoids, do, *([] if tie is None else [tie]))


HG_BLOCK = 128
HG_HEADS = HG_WIDTH // HG_HEAD_DIM


def _chunk_mats(n):
    r = lax.broadcasted_iota(jnp.int32, (n, n), 0)
    c = lax.broadcasted_iota(jnp.int32, (n, n), 1)
    same = (r // HG_CHUNK) == (c // HG_CHUNK)
    upto = (same & (c <= r)).astype(BF16)
    whole = same.astype(BF16)
    onward = (same & (c >= r)).astype(BF16)
    return upto, whole, onward


def _rows_dot(mat, x):
    return _dot(jnp.concatenate([mat, mat, mat], axis=1), jnp.concatenate(_split3(x), axis=0))


def _split_heads(x):
    return jnp.stack([x[:, h * HG_HEAD_DIM:(h + 1) * HG_HEAD_DIM] for h in range(HG_HEADS)], axis=0)


def _merge_heads(x):
    return jnp.concatenate([x[h] for h in range(HG_HEADS)], axis=1)


def _lower_bound(lg_ref):
    lg = lg_ref[...]
    return _sigmoid(lg[0:1, :] - lg[1:2, :])


def _hgrn_prepare(q_ref, f_ref, lb, h, upto, whole):
    cols = slice(h * HG_HEAD_DIM, (h + 1) * HG_HEAD_DIM)
    lbh = lb[:, cols]
    sg = _sigmoid(f_ref[:, cols])
    forget = lbh + (1.0 - lbh) * sg
    logf = jnp.log(forget)
    kk = (1.0 - lbh) * (1.0 - sg)
    qv = q_ref[:, cols]
    qsig = _sigmoid(qv)
    qh = qv * qsig
    b = _rows_dot(upto, logf)
    blast = _rows_dot(whole, logf)
    return dict(lbh=lbh, sg=sg, forget=forget, kk=kk, qv=qv, qsig=qsig, qh=qh, b=b, eb=jnp.exp(b),
                ekb=jnp.exp(blast - b), dl=jnp.exp(blast))


def _hgrn_fwd(proj, logits, *, name):
    t = proj.shape[0]
    tb = HG_BLOCK
    nc = tb // HG_CHUNK
    hd = HG_HEAD_DIM

    def body(q_ref, f_ref, i_ref, lg_ref, o_ref, st_ref, state, qh_s, kk_s, b_s, qe_s, ke_s, dl_s):
        @pl.when(pl.program_id(0) == 0)
        def _():
            state[...] = jnp.zeros_like(state)

        lb = _lower_bound(lg_ref)
        upto, whole, _ = _chunk_mats(tb)
        for h in range(HG_HEADS):
            p = _hgrn_prepare(q_ref, f_ref, lb, h, upto, whole)
            qh_s[h] = p["qh"]
            kk_s[h] = p["kk"]
            b_s[h] = p["b"]
            qe_s[h] = (p["qh"] * p["eb"]).astype(BF16)
            ke_s[h] = (p["kk"] * p["ekb"]).astype(BF16)
            dl_s[h] = p["dl"]
        rowi = lax.broadcasted_iota(jnp.int32, (HG_HEADS, HG_CHUNK, hd), 1)

        def chunk(c, _):
            r0 = pl.multiple_of(c * HG_CHUNK, HG_CHUNK)
            rows = pl.ds(r0, HG_CHUNK)
            bc = b_s[:, rows, :]
            qc = qh_s[:, rows, :]
            kc = kk_s[:, rows, :]
            vc = _split_heads(i_ref[rows, :])
            s_in = state[...]
            st_ref[c] = s_in
            s_in_b = s_in.astype(BF16)
            qe = qe_s[:, rows, :]
            o = jnp.stack([_dot_nt(qe[h], s_in_b[h]) for h in range(HG_HEADS)], axis=0)
            for s in range(HG_CHUNK):
                pair = jnp.where(rowi >= s, qc * jnp.exp(bc - bc[:, s:s + 1, :]) * kc[:, s:s + 1, :], 0.0)
                o = o + jnp.sum(pair, axis=2, keepdims=True) * vc[:, s:s + 1, :]
            o_ref[rows, :] = _merge_heads(o)
            vcb = vc.astype(BF16)
            ke = ke_s[:, rows, :]
            update = jnp.stack([_dot_tn(vcb[h], ke[h]) for h in range(HG_HEADS)], axis=0)
            state[...] = s_in * dl_s[:, pl.ds(r0, 1), :] + update
            return 0

        lax.fori_loop(0, nc, chunk, 0)

    blk = lambda col: pl.BlockSpec((tb, HG_WIDTH), lambda i: (i, col))
    head_f32 = pltpu.VMEM((HG_HEADS, tb, hd), F32)
    head_bf16 = pltpu.VMEM((HG_HEADS, tb, hd), BF16)
    return pl.pallas_call(
        body,
        name=name,
        grid=(t // tb,),
        in_specs=[blk(3), blk(4), blk(5), pl.BlockSpec((2, HG_WIDTH), lambda i: (0, 0))],
        out_specs=[
            pl.BlockSpec((tb, HG_WIDTH), lambda i: (i, 0)),
            pl.BlockSpec((nc, HG_HEADS, hd, hd), lambda i: (i, 0, 0, 0)),
        ],
        out_shape=[
            jax.ShapeDtypeStruct((t, HG_WIDTH), F32),
            jax.ShapeDtypeStruct((t // HG_CHUNK, HG_HEADS, hd, hd), F32),
        ],
        scratch_shapes=[pltpu.VMEM((HG_HEADS, hd, hd), F32), head_f32, head_f32, head_f32, head_bf16, head_bf16,
                        head_f32],
        compiler_params=_params("arbitrary"),
    )(proj, proj, proj, logits)


def _hgrn_bwd(proj, logits, states, do, *, name):
    t = proj.shape[0]
    tb = HG_BLOCK
    nb = t // tb
    nc = tb // HG_CHUNK
    hd = HG_HEAD_DIM

    def body(q_ref, f_ref, i_ref, lg_ref, st_ref, do_ref, dq_ref, df_ref, di_ref, dlb_ref,
             dstate, qh_s, kk_s, b_s, eb_s, ekb_s, qe_s, ke_s, dl_s, dqh_s, dkk_s, dlf_s):
        step = pl.program_id(0)

        @pl.when(step == 0)
        def _():
            dstate[...] = jnp.zeros_like(dstate)
            dlb_ref[...] = jnp.zeros_like(dlb_ref)

        lb = _lower_bound(lg_ref)
        upto, whole, _ = _chunk_mats(tb)
        prepared = []
        for h in range(HG_HEADS):
            p = _hgrn_prepare(q_ref, f_ref, lb, h, upto, whole)
            prepared.append(p)
            qh_s[h] = p["qh"]
            kk_s[h] = p["kk"]
            b_s[h] = p["b"]
            eb_s[h] = p["eb"]
            ekb_s[h] = p["ekb"]
            qe_s[h] = (p["qh"] * p["eb"]).astype(BF16)
            ke_s[h] = (p["kk"] * p["ekb"]).astype(BF16)
            dl_s[h] = p["dl"]
        rowi = lax.broadcasted_iota(jnp.int32, (HG_CHUNK, hd), 0)
        r16 = lax.broadcasted_iota(jnp.int32, (HG_CHUNK, HG_CHUNK), 0)
        c16 = lax.broadcasted_iota(jnp.int32, (HG_CHUNK, HG_CHUNK), 1)
        onward = (c16 >= r16).astype(BF16)

        def chunk(it, _):
            c = nc - 1 - it
            r0 = pl.multiple_of(c * HG_CHUNK, HG_CHUNK)
            rows = pl.ds(r0, HG_CHUNK)
            for h in range(HG_HEADS):
                cols = slice(h * hd, (h + 1) * hd)
                bc = b_s[h, rows, :]
                qc = qh_s[h, rows, :]
                kc = kk_s[h, rows, :]
                vc = i_ref[rows, cols]
                doc = do_ref[rows, cols]
                s_in = st_ref[c, h]
                ds_out = dstate[h]
                ds_out_b = ds_out.astype(BF16)
                docb = doc.astype(BF16)
                dl_row = dl_s[h, pl.ds(r0, 1), :]
                dqh = _dot(docb, s_in.astype(BF16)) * eb_s[h, rows, :]
                dkk = _dot(vc.astype(BF16), ds_out_b) * ekb_s[h, rows, :]
                dv = _dot_nt(ke_s[h, rows, :], ds_out_b)
                db = dqh * qc - dkk * kc
                dwhole = jnp.sum(dkk * kc, axis=0, keepdims=True) + jnp.sum(ds_out * s_in, axis=0, keepdims=True) * dl_row
                dk_rows, dv_rows = [], []
                for s in range(HG_CHUNK):
                    keep = rowi >= s
                    e = jnp.exp(bc - bc[s:s + 1, :])
                    k_row = kc[s:s + 1, :]
                    pcol = jnp.sum(jnp.where(keep, qc * e * k_row, 0.0), axis=1, keepdims=True)
                    dpcol = jnp.sum(doc * vc[s:s + 1, :], axis=1, keepdims=True)
                    m = jnp.where(keep, e * dpcol, 0.0)
                    y = m * qc
                    dqh = dqh + m * k_row
                    db = db + y * k_row
                    dk_rows.append(jnp.sum(y, axis=0, keepdims=True))
                    dv_rows.append(jnp.sum(pcol * doc, axis=0, keepdims=True))
                dkk_pairs = jnp.concatenate(dk_rows, axis=0)
                dkk = dkk + dkk_pairs
                db = db - dkk_pairs * kc
                dv = dv + jnp.concatenate(dv_rows, axis=0)
                dqh_s[h, rows, :] = dqh
                dkk_s[h, rows, :] = dkk
                dlf_s[h, rows, :] = _rows_dot(onward, db) + dwhole
                di_ref[rows, cols] = dv.astype(BF16)
                dstate[h] = ds_out * dl_row + _dot_tn(docb, qe_s[h, rows, :])
            return 0

        lax.fori_loop(0, nc, chunk, 0)
        for h in range(HG_HEADS):
            cols = slice(h * hd, (h + 1) * hd)
            p = prepared[h]
            dq_ref[:, cols] = (dqh_s[h] * (p["qsig"] * (1.0 + p["qv"] * (1.0 - p["qsig"])))).astype(BF16)
            dforget = dlf_s[h] / p["forget"] - dkk_s[h]
            df_ref[:, cols] = (dforget * (1.0 - p["lbh"]) * p["sg"] * (1.0 - p["sg"])).astype(BF16)
            dlb_ref[:, cols] += jnp.sum(dforget * (1.0 - p["sg"]), axis=0, keepdims=True)

    blk = lambda col: pl.BlockSpec((tb, HG_WIDTH), lambda i: (nb - 1 - i, col))
    vec = pl.BlockSpec((1, HG_WIDTH), lambda i: (0, 0))
    head_f32 = pltpu.VMEM((HG_HEADS, tb, hd), F32)
    head_bf16 = pltpu.VMEM((HG_HEADS, tb, hd), BF16)
    return pl.pallas_call(
        body,
        name=name,
        grid=(nb,),
        in_specs=[
            blk(3), blk(4), blk(5),
            pl.BlockSpec((2, HG_WIDTH), lambda i: (0, 0)),
            pl.BlockSpec((nc, HG_HEADS, hd, hd), lambda i: (nb - 1 - i, 0, 0, 0)),
            blk(0),
        ],
        out_specs=[blk(0), blk(0), blk(0), vec],
        out_shape=[jax.ShapeDtypeStruct((t, HG_WIDTH), BF16)] * 3 + [jax.ShapeDtypeStruct((1, HG_WIDTH), F32)],
        scratch_shapes=[
            pltpu.VMEM((HG_HEADS, hd, hd), F32),
            head_f32, head_f32, head_f32, head_f32, head_f32, head_bf16, head_bf16, head_f32,
            head_f32, head_f32, head_f32,
        ],
        compiler_params=_params("arbitrary"),
    )(proj, proj, proj, logits, states, do)


def _group_mat(width, head_dim):
    r = lax.broadcasted_iota(jnp.int32, (width, width), 0)
    c = lax.broadcasted_iota(jnp.int32, (width, width), 1)
    return ((r // head_dim) == (c // head_dim)).astype(BF16)


def _head_mean(x, mat, head_dim):
    hi = x.astype(BF16)
    lo = (x - hi.astype(F32)).astype(BF16)
    return (_dot(hi, mat) + _dot(lo, mat)) * (1.0 / head_dim)


def _mix_out_fwd(o_sb, o_hg, proj, g_sb, g_hg, w_out, x1, *, name, tm=512):
    t = x1.shape[0]

    def body(osb_ref, ohg_ref, gate_ref, gsb_ref, ghg_ref, w_ref, x_ref, xo_ref, mt_ref):
        msb = _group_mat(SB_WIDTH, SB_HEAD_DIM)
        mhg = _group_mat(HG_WIDTH, HG_HEAD_DIM)
        osb = osb_ref[...]
        ohg = ohg_ref[...]
        nsb = osb * lax.rsqrt(_head_mean(osb * osb, msb, SB_HEAD_DIM) + EPS) * gsb_ref[...]
        gate = gate_ref[...]
        nhg = ohg * lax.rsqrt(_head_mean(ohg * ohg, mhg, HG_HEAD_DIM) + EPS) * ghg_ref[...] * (gate * _sigmoid(gate))
        mixed = jnp.concatenate([nsb, nhg], axis=1).astype(BF16)
        mt_ref[...] = mixed
        xo_ref[...] = x_ref[...] + _dot(mixed, w_ref[...])

    half = pl.BlockSpec((tm, SB_WIDTH), lambda i: (i, 0))
    vec = pl.BlockSpec((1, SB_WIDTH), lambda i: (0, 0))
    row = pl.BlockSpec((tm, D_MODEL), lambda i: (i, 0))
    return pl.pallas_call(
        body,
        name=name,
        grid=(t // tm,),
        in_specs=[half, half, pl.BlockSpec((tm, HG_WIDTH), lambda i: (i, 6)), vec, vec,
                  pl.BlockSpec((D_MODEL, D_MODEL), lambda i: (0, 0)), row],
        out_specs=[row, row],
        out_shape=[jax.ShapeDtypeStruct((t, D_MODEL), F32), jax.ShapeDtypeStruct((t, D_MODEL), BF16)],
        compiler_params=_params("parallel"),
    )(o_sb, o_hg, proj, g_sb, g_hg, w_out, x1)


def _mix_out_bwd(dx2, o_sb, o_hg, proj, g_sb, g_hg, w_out, *, name, tm=512):
    t = dx2.shape[0]

    def body(dx_ref, osb_ref, ohg_ref, gate_ref, gsb_ref, ghg_ref, w_ref, dosb_ref, dohg_ref, dgate_ref, dgsb_ref,
             dghg_ref, dxb_ref):
        i = pl.program_id(0)
        msb = _group_mat(SB_WIDTH, SB_HEAD_DIM)
        mhg = _group_mat(HG_WIDTH, HG_HEAD_DIM)
        dxb = dx_ref[...].astype(BF16)
        dxb_ref[...] = dxb
        dmixed = _dot_nt(dxb, w_ref[...])
        dnsb = dmixed[:, :SB_WIDTH]
        dy = dmixed[:, SB_WIDTH:]

        osb = osb_ref[...]
        rstd = lax.rsqrt(_head_mean(osb * osb, msb, SB_HEAD_DIM) + EPS)
        ohat = osb * rstd
        part_sb = jnp.sum(dnsb * ohat, axis=0, keepdims=True)
        dohat = dnsb * gsb_ref[...]
        dosb_ref[...] = rstd * (dohat - ohat * _head_mean(dohat * ohat, msb, SB_HEAD_DIM))

        ohg = ohg_ref[...]
        rstd = lax.rsqrt(_head_mean(ohg * ohg, mhg, HG_HEAD_DIM) + EPS)
        ohat = ohg * rstd
        gate = gate_ref[...]
        sig = _sigmoid(gate)
        dn = dy * (gate * sig)
        dgate_ref[...] = (dy * (ohat * ghg_ref[...]) * (sig * (1.0 + gate * (1.0 - sig)))).astype(BF16)
        part_hg = jnp.sum(dn * ohat, axis=0, keepdims=True)
        dohat = dn * ghg_ref[...]
        dohg_ref[...] = rstd * (dohat - ohat * _head_mean(dohat * ohat, mhg, HG_HEAD_DIM))

        @pl.when(i == 0)
        def _():
            dgsb_ref[...] = part_sb
            dghg_ref[...] = part_hg

        @pl.when(i > 0)
        def _():
            dgsb_ref[...] += part_sb
            dghg_ref[...] += part_hg

    half = pl.BlockSpec((tm, SB_WIDTH), lambda i: (i, 0))
    vec = pl.BlockSpec((1, SB_WIDTH), lambda i: (0, 0))
    row = pl.BlockSpec((tm, D_MODEL), lambda i: (i, 0))
    return pl.pallas_call(
        body,
        name=name,
        grid=(t // tm,),
        in_specs=[row, half, half, pl.BlockSpec((tm, HG_WIDTH), lambda i: (i, 6)), vec, vec,
                  pl.BlockSpec((D_MODEL, D_MODEL), lambda i: (0, 0))],
        out_specs=[half, half, half, vec, vec, row],
        out_shape=[jax.ShapeDtypeStruct((t, SB_WIDTH), F32)] * 2 + [jax.ShapeDtypeStruct((t, SB_WIDTH), BF16)]
        + [jax.ShapeDtypeStruct((1, SB_WIDTH), F32)] * 2 + [jax.ShapeDtypeStruct((t, D_MODEL), BF16)],
        compiler_params=_params("arbitrary"),
    )(dx2, o_sb, o_hg, proj, g_sb, g_hg, w_out)


def _local_step(x, target, norms, logits, w, weights_after=None, grads_ready=None):
    w = dict(w)
    x1, a1, b1, h1, s1, hm = _ffn_fwd(x, norms["ffn1"], w["g1t"], w["u1t"], w["d1"], name="ffn1_fwd",
                                      next_gain=norms["mix"])
    if weights_after is not None:
        w.update(weights_after("ffn1", x1))
    proj = _mm(hm, w["int"], name="in_proj", tm=512, tn=IN_COLS, nt=True)
    o_sb, sb_kept, sb_sigmoids = _attn_fwd(proj, name="sb_attn_fwd")
    o_hg, states = _hgrn_fwd(proj, logits, name="hgrn2_fwd")
    x2, mixed = _mix_out_fwd(o_sb, o_hg, proj, norms["sb"], norms["hg"], w["out"], x1, name="mix_out_fwd")
    if weights_after is not None:
        w.update(weights_after("mix", x2))
    dx3, a2, b2, h2, s2, d_final, loss_row = _ffn_fwd(x2, norms["ffn2"], w["g2t"], w["u2t"], w["d2"], name="ffn2_fwd",
                                                      head=(norms["final"], target))

    def weight_grad(lhs, rhs, name, tie=None):
        return _mm(lhs, rhs, name=name, tm=256, tn=D_MODEL, ta=True, out_dtype=BF16, tie=tie)

    def sent(stage):
        return grads_ready(stage, gw) if grads_ready is not None else None

    gw, gv = {}, {"final": d_final}
    dx2, gv["ffn2"], da2, db2, dob2 = _ffn_bwd(dx3, x2, norms["ffn2"], a2, b2, w["g2t"], w["u2t"], w["d2"],
                                               name="ffn2_bwd")
    gw["g2t"] = weight_grad(da2, h2, "ffn2_dgate")
    gw["u2t"] = weight_grad(db2, h2, "ffn2_dup")
    gw["d2"] = weight_grad(s2, dob2, "ffn2_ddown")

    do_sb, do_hg, d_gate, gv["sb"], gv["hg"], dx2b = _mix_out_bwd(
        dx2, o_sb, o_hg, proj, norms["sb"], norms["hg"], w["out"], name="mix_out_bwd")
    gw["out"] = weight_grad(mixed, dx2b, "out_dw")
    tie = sent("mix")
    dq_sb, dk_sb, dv_sb = _attn_bwd(proj, sb_kept, sb_sigmoids, do_sb, name="sb_attn_bwd", tie=tie)
    dq_hg, df_hg, di_hg, d_lb = _hgrn_bwd(proj, logits if tie is None else logits + tie[0, 0], states, do_hg,
                                          name="hgrn2_bwd")
    dproj = jnp.concatenate([dq_sb, dk_sb.astype(BF16), dv_sb.astype(BF16), dq_hg, df_hg, di_hg, d_gate], axis=1)
    gw["int"] = weight_grad(dproj, hm, "in_dw")
    tie = sent("in")
    dx1, gv["mix"] = _in_proj_bwd(dproj, w["int"], x1, norms["mix"] if tie is None else norms["mix"] + tie[0, 0], dx2,
                                  name="in_dx")

    dx, gv["ffn1"], da1, db1, dob1 = _ffn_bwd(dx1, x, norms["ffn1"], a1, b1, w["g1t"], w["u1t"], w["d1"],
                                              name="ffn1_bwd")
    gw["g1t"] = weight_grad(da1, h1, "ffn1_dgate")
    gw["u1t"] = weight_grad(db1, h1, "ffn1_dup", tie=sent("g1t"))
    gw["d1"] = weight_grad(s1, dob1, "ffn1_ddown", tie=sent("u1t"))
    sent("d1")
    gv["lb"] = d_lb
    return loss_row, dx, gw, gv


HBM = pl.BlockSpec(memory_space=pl.ANY)


def _place():
    return lax.axis_index("x"), lax.axis_index("y"), lax.axis_index("c")


def _slot(px, py, pc):
    return 4 * px + 2 * py + pc


def _all_gather(blocks, *, name):
    n = len(blocks)

    def body(*refs):
        ins, outs = refs[:n], refs[n:2 * n]
        send_sems, recv_sems, local_sems = refs[2 * n:]
        x, y, c = _place()
        me, sibling = (x, y, c), (x, y, 1 - c)
        chips = [(1 - x, y), (x, 1 - y), (1 - x, 1 - y)]

        def copy(a, k, block, to, src=None):
            dst = outs[a].at[_slot(*block)]
            return pltpu.make_async_remote_copy(
                src_ref=dst if src is None else src, dst_ref=dst, send_sem=send_sems.at[7 * a + k],
                recv_sem=recv_sems.at[7 * a + k], device_id=to, device_id_type=MESH)

        mine = [pltpu.make_async_copy(ins[a], outs[a].at[_slot(*me)], local_sems.at[a]) for a in range(n)]
        for cp in mine:
            cp.start()
        first = []
        for a in range(n):
            first.append(copy(a, 0, me, sibling, src=ins[a]))
            first += [copy(a, 1 + j, me, (*chip, c), src=ins[a]) for j, chip in enumerate(chips)]
        for cp in first:
            cp.start()
        passed = []
        for j, chip in enumerate(chips):
            for a in range(n):
                copy(a, 1 + j, (*chip, c), me).wait_recv()
                fwd = copy(a, 4 + j, (*chip, c), sibling)
                fwd.start()
                passed.append(fwd)
        for a in range(n):
            copy(a, 0, sibling, me).wait_recv()
            for j, chip in enumerate(chips):
                copy(a, 4 + j, (*chip, 1 - c), me).wait_recv()
        for cp in first + passed:
            cp.wait_send()
        for cp in mine:
            cp.wait()

    return pl.pallas_call(
        body,
        name=name,
        in_specs=[HBM] * n,
        out_specs=[HBM] * n,
        out_shape=[jax.ShapeDtypeStruct((N_DEV,) + b.shape, b.dtype) for b in blocks],
        scratch_shapes=[pltpu.SemaphoreType.DMA((7 * n,)), pltpu.SemaphoreType.DMA((7 * n,)),
                        pltpu.SemaphoreType.DMA((n,))],
    )(*blocks)


def _flipped(place, d):
    return tuple(1 - p if (d >> (2 - axis)) & 1 else p for axis, p in enumerate(place))


SEM = pl.BlockSpec(memory_space=pltpu.SEMAPHORE)
EFFECT = pltpu.SideEffectType.DATAFLOW_SIDE_EFFECTING


def _split_copies(me, srcs, lands, send_sems, recv_sems, by_owner):
    copies = []
    for d in range(1, N_DEV):
        peer = _flipped(me, d)
        for a, (src, land) in enumerate(zip(srcs, lands)):
            copies.append(pltpu.make_async_remote_copy(
                src_ref=src.at[_slot(*peer)] if by_owner else src, dst_ref=land.at[_slot(*me)],
                send_sem=send_sems.at[7 * a + d - 1], recv_sem=recv_sems.at[7 * a + d - 1], device_id=peer,
                device_id_type=MESH))
    return copies


def _copies_start(srcs, *, name, by_owner, after=None):
    n = len(srcs)
    extra = [] if after is None else [after]
    land_shapes = [s.shape if by_owner else (N_DEV,) + s.shape for s in srcs]
    lands = [pltpu.with_memory_space_constraint(lax.empty(shape, s.dtype), pltpu.HBM) for shape, s in zip(land_shapes, srcs)]
    srcs = [pltpu.with_memory_space_constraint(s, pltpu.HBM) for s in srcs]

    def body(*refs):
        src_refs, land_refs = refs[:n], refs[n:2 * n]
        send_sems, recv_sems = refs[2 * n + len(extra)], refs[2 * n + len(extra) + 1]
        token = refs[-1]
        for cp in _split_copies(_place(), src_refs, land_refs, send_sems, recv_sems, by_owner):
            cp.start()
        token[...] = jnp.zeros_like(token)

    out = pl.pallas_call(
        body,
        name=name,
        in_specs=[HBM] * (2 * n + len(extra)),
        out_specs=[SEM, SEM] + [HBM] * (2 * n) + [pl.BlockSpec(memory_space=pltpu.VMEM)],
        out_shape=[pltpu.SemaphoreType.DMA((7 * n,)), pltpu.SemaphoreType.DMA((7 * n,))]
        + [pltpu.HBM(s.shape, s.dtype) for s in srcs] + [pltpu.HBM(shape, s.dtype) for shape, s in zip(land_shapes, srcs)]
        + [jax.ShapeDtypeStruct((8, LANES), F32)],
        input_output_aliases={i: 2 + i for i in range(2 * n)},
        compiler_params=pltpu.CompilerParams(has_side_effects=EFFECT),
    )(*srcs, *lands, *extra)
    return (out[0], out[1], out[2:2 + n], out[2 + n:2 + 2 * n]), out[-1]


def _copies_wait(started, after, *, name, by_owner):
    send_sems, recv_sems, srcs, lands = started
    n = len(srcs)

    def body(*refs):
        src_refs, land_refs = refs[:n], refs[n:2 * n]
        for cp in _split_copies(_place(), src_refs, land_refs, refs[2 * n], refs[2 * n + 1], by_owner):
            cp.wait_send()
            cp.wait_recv()

    out = pl.pallas_call(
        body,
        name=name,
        in_specs=[HBM] * (2 * n) + [SEM, SEM, HBM],
        out_specs=[HBM] * (2 * n),
        out_shape=[pltpu.HBM(s.shape, s.dtype) for s in srcs] + [pltpu.HBM(s.shape, s.dtype) for s in lands],
        input_output_aliases={i: i for i in range(2 * n)},
        compiler_params=pltpu.CompilerParams(has_side_effects=EFFECT),
    )(*srcs, *lands, send_sems, recv_sems, after)
    return out[:n], out[n:]


def _with_own(lands, own, slot):
    zero = jnp.zeros((), jnp.int32)
    return [lax.dynamic_update_slice(land, o[None], (slot.astype(jnp.int32),) + (zero,) * o.ndim)
            for land, o in zip(lands, own)]


def _adamw(w, g, m, v):
    m = ADAM_B1 * m + (1.0 - ADAM_B1) * g
    v = ADAM_B2 * v + (1.0 - ADAM_B2) * (g * g)
    m_hat = m / (1.0 - ADAM_B1 ** ADAM_STEP)
    v_hat = v / (1.0 - ADAM_B2 ** ADAM_STEP)
    delta = -ADAM_LR * (m_hat / (jnp.sqrt(v_hat) + ADAM_EPS) + ADAM_WD * w)
    return delta, m, v


def _sum_and_update(parts, w, m, v, *, name, tie=None):
    _, rows, cols = w.shape
    tr = rows // 2

    def body(p_ref, w_ref, m_ref, v_ref, *rest):
        g_ref, d_ref, mo_ref, vo_ref = rest[-4:]
        g = p_ref[0].astype(F32)
        for s in range(1, N_DEV):
            g = g + p_ref[s].astype(F32)
        g_ref[0] = g
        d_ref[0], mo_ref[0], vo_ref[0] = _adamw(w_ref[0], g, m_ref[0], v_ref[0])

    flat = pl.BlockSpec((1, tr, cols), lambda i: (0, i, 0))
    return pl.pallas_call(
        body,
        name=name,
        grid=(rows // tr,),
        in_specs=[pl.BlockSpec((N_DEV, tr, cols), lambda i: (0, i, 0)), flat, flat, flat]
        + ([] if tie is None else [pl.BlockSpec(memory_space=pl.ANY)]),
        out_specs=[flat] * 4,
        out_shape=[jax.ShapeDtypeStruct((1, rows, cols), F32)] * 4,
        compiler_params=_params("parallel"),
    )(parts, w, m, v, *([] if tie is None else [tie]))


VEC_ROWS = 8
ROW_LOGITS, ROW_LOSS = 5, 7


def _vectors_update(part, w, m, v, *, name):
    def body(p_ref, w_ref, m_ref, v_ref, g_ref, d_ref, mo_ref, vo_ref, loss_ref, all_ref, send_sems, recv_sems):
        me = _place()
        all_ref[_slot(*me)] = p_ref[...]
        copies = []
        for d in range(1, N_DEV):
            peer = _flipped(me, d)
            copies.append(pltpu.make_async_remote_copy(
                src_ref=p_ref, dst_ref=all_ref.at[_slot(*me)], send_sem=send_sems.at[d - 1], recv_sem=recv_sems.at[d - 1],
                device_id=peer, device_id_type=MESH))
        for cp in copies:
            cp.start()
        for cp in copies:
            cp.wait()
        total = all_ref[0]
        for s in range(1, N_DEV):
            total = total + all_ref[s]
        wv = w_ref[...]
        half = D_MODEL // 2
        lb = _sigmoid(wv[ROW_LOGITS:ROW_LOGITS + 1, :half] - wv[ROW_LOGITS:ROW_LOGITS + 1, half:])
        d_first = total[ROW_LOGITS:ROW_LOGITS + 1, :half] * lb * (1.0 - lb)
        d_logits = jnp.concatenate([d_first, -d_first], axis=1)
        rowi = lax.broadcasted_iota(jnp.int32, (VEC_ROWS, D_MODEL), 0)
        g = jnp.where(rowi == ROW_LOGITS, d_logits, jnp.where(rowi < ROW_LOGITS, total, 0.0))
        g_ref[...] = g
        d_ref[...], mo_ref[...], vo_ref[...] = _adamw(wv, g, m_ref[...], v_ref[...])
        loss_ref[...] = total[ROW_LOSS:ROW_LOSS + 1, :]

    vmem = pl.BlockSpec(memory_space=pltpu.VMEM)
    return pl.pallas_call(
        body,
        name=name,
        in_specs=[vmem] * 4,
        out_specs=[vmem] * 5,
        out_shape=[jax.ShapeDtypeStruct((VEC_ROWS, D_MODEL), F32)] * 4 + [jax.ShapeDtypeStruct((1, D_MODEL), F32)],
        scratch_shapes=[pltpu.VMEM((N_DEV, VEC_ROWS, D_MODEL), F32), pltpu.SemaphoreType.DMA((7,)),
                        pltpu.SemaphoreType.DMA((7,))],
    )(part, w, m, v)


TRANSPOSED = ("g1t", "u1t", "g2t", "u2t", "int")


def _vector_rows(rows):
    rowi = lax.broadcasted_iota(jnp.int32, (VEC_ROWS, D_MODEL), 0)
    out = jnp.zeros((VEC_ROWS, D_MODEL), F32)
    for i, r in enumerate(rows):
        if r is not None:
            out = jnp.where(rowi == i, r, out)
    return out


def kernel(x, ffn1_norm, ffn1_w_gate, ffn1_w_up, ffn1_w_down, mix_norm, w_in, sb_out_norm, hg_lower_bound_logits, hg_out_norm, w_out, ffn2_norm, ffn2_w_gate, ffn2_w_up, ffn2_w_down, final_norm, loss_target, m_ffn1_norm, m_ffn1_w_gate, m_ffn1_w_up, m_ffn1_w_down, m_mix_norm, m_w_in, m_sb_out_norm, m_hg_lower_bound_logits, m_hg_out_norm, m_w_out, m_ffn2_norm, m_ffn2_w_gate, m_ffn2_w_up, m_ffn2_w_down, m_final_norm, v_ffn1_norm, v_ffn1_w_gate, v_ffn1_w_up, v_ffn1_w_down, v_mix_norm, v_w_in, v_sb_out_norm, v_hg_lower_bound_logits, v_hg_out_norm, v_w_out, v_ffn2_norm, v_ffn2_w_gate, v_ffn2_w_up, v_ffn2_w_down, v_final_norm):
    def matrices(g1, u1, d1, win, wout, g2, u2, d2):
        return {"g1t": g1, "u1t": u1, "d1": d1, "int": win, "out": wout, "g2t": g2, "u2t": u2, "d2": d2}

    def vectors(n1, nm, nsb, lg, nhg, n2, nf):
        return [n1, nm, n2, nf.reshape(1, D_MODEL), jnp.concatenate([nsb, nhg], axis=1), lg.reshape(1, D_MODEL), None, None]

    w_sh = matrices(ffn1_w_gate, ffn1_w_up, ffn1_w_down, w_in, w_out, ffn2_w_gate, ffn2_w_up, ffn2_w_down)
    m_sh = matrices(m_ffn1_w_gate, m_ffn1_w_up, m_ffn1_w_down, m_w_in, m_w_out, m_ffn2_w_gate, m_ffn2_w_up, m_ffn2_w_down)
    v_sh = matrices(v_ffn1_w_gate, v_ffn1_w_up, v_ffn1_w_down, v_w_in, v_w_out, v_ffn2_w_gate, v_ffn2_w_up, v_ffn2_w_down)
    keys = list(w_sh)

    slot = _slot(*_place())

    def full(key, stack):
        return stack.reshape(-1, D_MODEL)

    def by_owner(key, grad):
        return grad.reshape(N_DEV, -1, D_MODEL)

    def view(key, a):
        return jnp.swapaxes(a, 1, 2) if key in TRANSPOSED else a

    blocks = {k: view(k, w_sh[k])[0].astype(BF16) for k in keys}
    first, mid, last = ("g1t", "u1t", "d1"), ("int", "out"), ("g2t", "u2t", "d2")
    w_first = {k: full(k, s) for k, s in zip(first, _all_gather([blocks[k] for k in first], name="gather_ffn1"))}
    flights = {}
    flights["ffn1"], token_mid = _copies_start([blocks[k] for k in mid], name="gather_mid_start", by_owner=False,
                                               after=w_first["d1"])
    flights["mix"], token_last = _copies_start([blocks[k] for k in last], name="gather_ffn2_start", by_owner=False,
                                               after=token_mid)

    def weights_after(stage, result):
        group = mid if stage == "ffn1" else last
        own, lands = _copies_wait(flights[stage], result, name="gather_" + stage + "_wait", by_owner=False)
        return {k: full(k, s) for k, s in zip(group, _with_own(lands, own, slot))}

    groups = {"mix": ("g2t", "u2t", "d2", "out"), "in": ("int",), "g1t": ("g1t",), "u1t": ("u1t",), "d1": ("d1",)}
    sent, sent_tokens = {}, []

    def grads_ready(stage, gw):
        stacks = [by_owner(k, gw[k]) for k in groups[stage]]
        flight, token = _copies_start(stacks, name="grads_" + stage + "_start", by_owner=True)
        sent[stage] = flight
        sent_tokens.append(token)
        return token

    norms = {"ffn1": ffn1_norm + token_last[0, 0], "mix": mix_norm, "sb": sb_out_norm, "hg": hg_out_norm,
             "ffn2": ffn2_norm, "final": final_norm.reshape(1, D_MODEL)}
    loss_row, grad_x, gw, gv = _local_step(x[0], loss_target[0], norms, hg_lower_bound_logits, w_first, weights_after,
                                           grads_ready)

    updated, after = {}, sent_tokens[-1]
    for stage, flight in sent.items():
        stacks, lands = _copies_wait(flight, after, name="grads_" + stage + "_wait", by_owner=True)
        own = [lax.dynamic_index_in_dim(s, slot, keepdims=False) for s in stacks]
        for k, part in zip(groups[stage], _with_own(lands, own, slot)):
            updated[k] = _sum_and_update(part, view(k, w_sh[k]), view(k, m_sh[k]), view(k, v_sh[k]), name="adamw_" + k,
                                         tie=after)
            after = updated[k][0]
    mats = [{k: view(k, updated[k][i]) for k in keys} for i in range(4)]

    lb_row = jnp.concatenate([gv["lb"], jnp.zeros_like(gv["lb"])], axis=1)
    part = _vector_rows([gv["ffn1"], gv["mix"], gv["ffn2"], gv["final"], jnp.concatenate([gv["sb"], gv["hg"]], axis=1),
                         lb_row, None, loss_row])
    vec_w = _vector_rows(vectors(ffn1_norm, mix_norm, sb_out_norm, hg_lower_bound_logits, hg_out_norm, ffn2_norm, final_norm))
    vec_m = _vector_rows(vectors(m_ffn1_norm, m_mix_norm, m_sb_out_norm, m_hg_lower_bound_logits, m_hg_out_norm,
                                 m_ffn2_norm, m_final_norm))
    vec_v = _vector_rows(vectors(v_ffn1_norm, v_mix_norm, v_sb_out_norm, v_hg_lower_bound_logits, v_hg_out_norm,
                                 v_ffn2_norm, v_final_norm))
    *vecs, loss_out = _vectors_update(part, vec_w, vec_m, vec_v, name="vectors_update")

    def leaves(mat, vec):
        half = D_MODEL // 2
        return (
            vec[0:1], mat["g1t"], mat["u1t"], mat["d1"], vec[1:2], mat["int"], vec[4:5, :half],
            vec[ROW_LOGITS].reshape(2, half), vec[4:5, half:], mat["out"], vec[2:3], mat["g2t"], mat["u2t"],
            mat["d2"], vec[3],
        )

    out = [loss_out[0, 0], grad_x[None]]
    for mat, vec in zip(mats, vecs):
        out.extend(leaves(mat, vec))
    return tuple(out)
```

```python
import jax
import jax.numpy as jnp
from jax import lax
from jax.experimental import pallas as pl
from jax.experimental.pallas import tpu as pltpu

F32, BF16 = jnp.float32, jnp.bfloat16
D_MODEL = 1024
D_FF = 2816
SB_WIDTH = 512
HG_WIDTH = 512
SB_HEAD_DIM = 64
HG_HEAD_DIM = 128
IN_COLS = 3584
EPS = 1e-6
N_DEV = 8
LANES = 128
HG_CHUNK = 16
VMEM_LIMIT_BYTES = 48 * 1024 * 1024
FFN_BWD_VMEM_LIMIT_BYTES = 56 * 1024 * 1024
ADAM_LR, ADAM_B1, ADAM_B2, ADAM_EPS, ADAM_WD, ADAM_STEP = 0.001, 0.9, 0.999, 1e-08, 0.01, 10
MESH = pl.DeviceIdType.MESH


def _params(*semantics, vmem_limit_bytes=VMEM_LIMIT_BYTES):
    return pltpu.CompilerParams(dimension_semantics=semantics, vmem_limit_bytes=vmem_limit_bytes)


def _dot(a, b):
    return jnp.dot(a, b, preferred_element_type=F32)


def _dot_nt(a, b):
    return lax.dot_general(a, b, (((1,), (1,)), ((), ())), preferred_element_type=F32)


def _dot_tn(a, b):
    return lax.dot_general(a, b, (((0,), (0,)), ((), ())), preferred_element_type=F32)


def _split3(x):
    hi = x.astype(BF16)
    r1 = x - hi.astype(F32)
    mid = r1.astype(BF16)
    lo = (r1 - mid.astype(F32)).astype(BF16)
    return hi, mid, lo


def _rms(xv):
    rstd = lax.rsqrt(jnp.mean(xv * xv, axis=-1, keepdims=True) + EPS)
    return xv * rstd, rstd


def _sigmoid(x):
    return 0.5 + 0.5 * jnp.tanh(0.5 * x)


def _loss_terms(xv, gain, target):
    xhat, rstd = _rms(xv)
    err = xhat * gain - target
    loss = 0.5 * jnp.sum(jnp.mean(err * err, axis=-1, keepdims=True), axis=0, keepdims=True)
    dy = err * (1.0 / xv.shape[-1])
    dxh = dy * gain
    dx = rstd * (dxh - xhat * jnp.mean(dxh * xhat, axis=-1, keepdims=True))
    return dx, jnp.sum(dy * xhat, axis=0, keepdims=True), loss


def _mm(a, b, *, name, tm, tn, nt=False, ta=False, out_dtype=F32, tie=None):
    k, m = a.shape if ta else a.shape[::-1]
    n = b.shape[0] if nt else b.shape[1]
    assert m % tm == 0 and n % tn == 0 and not (nt and ta), (name, a.shape, b.shape, tm, tn)

    def body(a_ref, b_ref, *rest):
        av = a_ref[...].astype(BF16)
        bv = b_ref[...].astype(BF16)
        rest[-1][...] = (_dot_nt(av, bv) if nt else _dot_tn(av, bv) if ta else _dot(av, bv)).astype(out_dtype)

    in_specs = [
        pl.BlockSpec((k, tm), lambda i, j: (0, i)) if ta else pl.BlockSpec((tm, k), lambda i, j: (i, 0)),
        pl.BlockSpec((tn, k), lambda i, j: (j, 0)) if nt else pl.BlockSpec((k, tn), lambda i, j: (0, j)),
    ]
    operands = [a, b]
    if tie is not None:
        in_specs.append(pl.BlockSpec(memory_space=pl.ANY))
        operands.append(tie)
    return pl.pallas_call(
        body,
        name=name,
        grid=(m // tm, n // tn),
        in_specs=in_specs,
        out_specs=pl.BlockSpec((tm, tn), lambda i, j: (i, j)),
        out_shape=jax.ShapeDtypeStruct((m, n), out_dtype),
        compiler_params=_params("parallel", "parallel"),
    )(*operands)


def _ffn_fwd(x, gain, wgt, wut, wd, *, name, next_gain=None, head=None, tm=1024, tf=256):
    t = x.shape[0]
    nj = D_FF // tf
    extra_in = [] if next_gain is None else [next_gain]
    extra_in += [] if head is None else list(head)

    def body(x_ref, g_ref, wg_ref, wu_ref, wd_prev_ref, wd_last_ref, *rest):
        extra, (xo_ref, a_ref, b_ref, h_ref, st_ref) = rest[:len(extra_in)], rest[len(extra_in):len(extra_in) + 5]
        tail_out, (acc, s_prev) = rest[len(extra_in) + 5:-2], rest[-2:]
        i = pl.program_id(0)
        j = pl.program_id(1)

        @pl.when(j == 0)
        def _():
            xhat, _ = _rms(x_ref[...])
            h_ref[...] = (xhat * g_ref[...]).astype(BF16)
            acc[...] = jnp.zeros_like(acc)
            s_prev[...] = jnp.zeros_like(s_prev)

        acc[...] += _dot(s_prev[...], wd_prev_ref[...])
        h = h_ref[...]
        a = _dot_nt(h, wg_ref[...])
        b = _dot_nt(h, wu_ref[...])
        a_ref[...] = a.astype(BF16)
        b_ref[...] = b.astype(BF16)
        s = (a * _sigmoid(a) * b).astype(BF16)
        st_ref[...] = s
        s_prev[...] = s

        @pl.when(j == nj - 1)
        def _():
            xo = x_ref[...] + 0.5 * (acc[...] + _dot(s, wd_last_ref[...]))
            if head is None:
                xo_ref[...] = xo
            if next_gain is not None:
                tail_out[0][...] = (_rms(xo)[0] * extra[0][...]).astype(BF16)
            if head is not None:
                gain_ref, target_ref = extra[-2:]
                dg_ref, loss_ref = tail_out[-2:]
                xo_ref[...], part_g, part_loss = _loss_terms(xo, gain_ref[...], target_ref[...])

                @pl.when(i == 0)
                def _():
                    dg_ref[...] = part_g
                    loss_ref[...] = jnp.broadcast_to(part_loss, loss_ref.shape)

                @pl.when(i > 0)
                def _():
                    dg_ref[...] += part_g
                    loss_ref[...] += jnp.broadcast_to(part_loss, loss_ref.shape)

    row = pl.BlockSpec((tm, D_MODEL), lambda i, j: (i, 0))
    vec = pl.BlockSpec((1, D_MODEL), lambda i, j: (0, 0))
    tile = pl.BlockSpec((tm, tf), lambda i, j: (i, j))
    weights = pl.BlockSpec((tf, D_MODEL), lambda i, j: (j, 0))
    tail_specs = ([] if next_gain is None else [row]) + ([] if head is None else [vec, vec])
    tail_shapes = ([] if next_gain is None else [jax.ShapeDtypeStruct((t, D_MODEL), BF16)]) + (
        [] if head is None else [jax.ShapeDtypeStruct((1, D_MODEL), F32)] * 2)
    return pl.pallas_call(
        body,
        name=name,
        grid=(t // tm, nj),
        in_specs=[
            row, vec, weights, weights,
            pl.BlockSpec((tf, D_MODEL), lambda i, j: (jnp.maximum(j - 1, 0), 0)),
            pl.BlockSpec((tf, D_MODEL), lambda i, j: (nj - 1, 0)),
        ] + ([] if next_gain is None else [vec]) + ([] if head is None else [vec, row]),
        out_specs=[row, tile, tile, row, tile] + tail_specs,
        out_shape=[
            jax.ShapeDtypeStruct((t, D_MODEL), F32),
            jax.ShapeDtypeStruct((t, D_FF), BF16),
            jax.ShapeDtypeStruct((t, D_FF), BF16),
            jax.ShapeDtypeStruct((t, D_MODEL), BF16),
            jax.ShapeDtypeStruct((t, D_FF), BF16),
        ] + tail_shapes,
        scratch_shapes=[pltpu.VMEM((tm, D_MODEL), F32), pltpu.VMEM((tm, tf), BF16)],
        compiler_params=_params("arbitrary", "arbitrary"),
    )(x, gain, wgt, wut, wd, wd, *extra_in)


def _ffn_bwd(dout, x, gain, a, b, wgt, wut, wd, *, name, tm=1024, tf=256):
    t = x.shape[0]
    nj = D_FF // tf

    def body(do_ref, x_ref, g_ref, a_ref, b_ref, wg_prev_ref, wu_prev_ref, wg_last_ref, wu_last_ref, wd_ref,
             dx_ref, dg_ref, da_ref, db_ref, dob_ref, dob_scr, dh, da_prev, db_prev):
        i = pl.program_id(0)
        j = pl.program_id(1)

        @pl.when(j == 0)
        def _():
            d = (0.5 * do_ref[...]).astype(BF16)
            dob_scr[...] = d
            dob_ref[...] = d
            dh[...] = jnp.zeros_like(dh)
            da_prev[...] = jnp.zeros_like(da_prev)
            db_prev[...] = jnp.zeros_like(db_prev)

        dh[...] += _dot(da_prev[...], wg_prev_ref[...]) + _dot(db_prev[...], wu_prev_ref[...])
        ds = _dot_nt(dob_scr[...], wd_ref[...])
        av = a_ref[...].astype(F32)
        bv = b_ref[...].astype(F32)
        sig = _sigmoid(av)
        dbv = (ds * (av * sig)).astype(BF16)
        dav = (ds * bv * (sig * (1.0 + av * (1.0 - sig)))).astype(BF16)
        da_ref[...] = dav
        db_ref[...] = dbv
        da_prev[...] = dav
        db_prev[...] = dbv

        @pl.when(j == nj - 1)
        def _():
            xhat, rstd = _rms(x_ref[...])
            dhv = dh[...] + _dot(dav, wg_last_ref[...]) + _dot(dbv, wu_last_ref[...])
            part = jnp.sum(dhv * xhat, axis=0, keepdims=True)

            @pl.when(i == 0)
            def _():
                dg_ref[...] = part

            @pl.when(i > 0)
            def _():
                dg_ref[...] += part

            dxh = dhv * g_ref[...]
            dx_ref[...] = do_ref[...] + rstd * (dxh - xhat * jnp.mean(dxh * xhat, axis=-1, keepdims=True))

    return pl.pallas_call(
        body,
        name=name,
        grid=(t // tm, nj),
        in_specs=[
            pl.BlockSpec((tm, D_MODEL), lambda i, j: (i, 0)),
            pl.BlockSpec((tm, D_MODEL), lambda i, j: (i, 0)),
            pl.BlockSpec((1, D_MODEL), lambda i, j: (0, 0)),
            pl.BlockSpec((tm, tf), lambda i, j: (i, j)),
            pl.BlockSpec((tm, tf), lambda i, j: (i, j)),
            pl.BlockSpec((tf, D_MODEL), lambda i, j: (jnp.maximum(j - 1, 0), 0)),
            pl.BlockSpec((tf, D_MODEL), lambda i, j: (jnp.maximum(j - 1, 0), 0)),
            pl.BlockSpec((tf, D_MODEL), lambda i, j: (nj - 1, 0)),
            pl.BlockSpec((tf, D_MODEL), lambda i, j: (nj - 1, 0)),
            pl.BlockSpec((tf, D_MODEL), lambda i, j: (j, 0)),
        ],
        out_specs=[
            pl.BlockSpec((tm, D_MODEL), lambda i, j: (i, 0)),
            pl.BlockSpec((1, D_MODEL), lambda i, j: (0, 0)),
            pl.BlockSpec((tm, tf), lambda i, j: (i, j)),
            pl.BlockSpec((tm, tf), lambda i, j: (i, j)),
            pl.BlockSpec((tm, D_MODEL), lambda i, j: (i, 0)),
        ],
        out_shape=[
            jax.ShapeDtypeStruct((t, D_MODEL), F32),
            jax.ShapeDtypeStruct((1, D_MODEL), F32),
            jax.ShapeDtypeStruct((t, D_FF), BF16),
            jax.ShapeDtypeStruct((t, D_FF), BF16),
            jax.ShapeDtypeStruct((t, D_MODEL), BF16),
        ],
        scratch_shapes=[pltpu.VMEM((tm, D_MODEL), BF16), pltpu.VMEM((tm, D_MODEL), F32), pltpu.VMEM((tm, tf), BF16),
                        pltpu.VMEM((tm, tf), BF16)],
        compiler_params=_params("arbitrary", "arbitrary", vmem_limit_bytes=FFN_BWD_VMEM_LIMIT_BYTES),
    )(dout, x, gain, a, b, wgt, wut, wgt, wut, wd)


def _in_proj_bwd(dproj, w_int, x, gain, dres, *, name, tm=512):
    t, k = dproj.shape

    def body(dp_ref, w_ref, x_ref, g_ref, dr_ref, dx_ref, dg_ref):
        i = pl.program_id(0)
        dhv = _dot(dp_ref[...], w_ref[...])
        xhat, rstd = _rms(x_ref[...])
        part = jnp.sum(dhv * xhat, axis=0, keepdims=True)

        @pl.when(i == 0)
        def _():
            dg_ref[...] = part

        @pl.when(i > 0)
        def _():
            dg_ref[...] += part

        dxh = dhv * g_ref[...]
        dx_ref[...] = dr_ref[...] + rstd * (dxh - xhat * jnp.mean(dxh * xhat, axis=-1, keepdims=True))

    row = pl.BlockSpec((tm, D_MODEL), lambda i: (i, 0))
    vec = pl.BlockSpec((1, D_MODEL), lambda i: (0, 0))
    return pl.pallas_call(
        body,
        name=name,
        grid=(t // tm,),
        in_specs=[pl.BlockSpec((tm, k), lambda i: (i, 0)), pl.BlockSpec((k, D_MODEL), lambda i: (0, 0)), row, vec, row],
        out_specs=[row, vec],
        out_shape=[jax.ShapeDtypeStruct((t, D_MODEL), F32), jax.ShapeDtypeStruct((1, D_MODEL), F32)],
        compiler_params=_params("arbitrary"),
    )(dproj, w_int, x, gain, dres)


ATT_Q_TILE = 512
ATT_K_BLOCK = 256


def _first_head_lanes():
    return lax.broadcasted_iota(jnp.int32, (1, LANES), 1) < SB_HEAD_DIM


def _stack_heads(x):
    first = _first_head_lanes()
    return jnp.concatenate([jnp.where(first, x, 0.0), jnp.where(first, 0.0, x)], axis=0)


def _unstack_heads(x, rows):
    return jnp.where(_first_head_lanes(), x[:rows], x[rows:])


def _rows_from(x, first, rows):
    return x if first == 0 else jnp.concatenate([x[first:rows], x[rows + first:]], axis=0)


def _rows_into(full, part, first, rows):
    if first == 0:
        return part
    n = rows - first
    return jnp.concatenate([full[:first], part[:n], full[rows:rows + first], part[n:]], axis=0)


def _tri(n, relation):
    r = lax.broadcasted_iota(jnp.int32, (n, n), 0)
    c = lax.broadcasted_iota(jnp.int32, (n, n), 1)
    return relation(r, c).astype(BF16)


def _scan_dot(x, tri):
    hi = x.astype(BF16)
    lo = (x - hi.astype(F32)).astype(BF16)
    return _dot(jnp.concatenate([hi, lo], axis=1), jnp.concatenate([tri, tri], axis=0))


def _log_terms(z):
    lbeta = jnp.minimum(z, 0.0) - jnp.log(1.0 + jnp.exp(-jnp.abs(z)))
    return lbeta, lbeta - z


def _attn_fwd(proj, *, name):
    t = proj.shape[0]
    tq, tk = ATT_Q_TILE, ATT_K_BLOCK
    diag = tq // tk
    n_pairs = SB_WIDTH // LANES

    def body(q_ref, k_ref, v_ref, o_ref, kept_ref):
        qi = pl.program_id(1)
        q = q_ref[...] * (SB_HEAD_DIM ** -0.5)
        qs = _stack_heads(q).astype(BF16)
        tri = _tri(tk, lambda j, s: j > s)
        trow = lax.broadcasted_iota(jnp.int32, (tq, tk), 0)
        scol = lax.broadcasted_iota(jnp.int32, (tq, tk), 1)

        def block(kb, carry, causal, first=0):
            acc, c = carry
            off = pl.multiple_of(kb * tk, tk)
            lbeta, lrest = _log_terms(_dot_nt(_rows_from(qs, first, tq), k_ref[pl.ds(off, tk), :].astype(BF16)))
            if causal is not None:
                lrest = jnp.where(causal, lrest, 0.0)
            w = jnp.exp(lbeta + (_scan_dot(lrest, tri) + _rows_from(c, first, tq)))
            if causal is not None:
                w = jnp.where(causal, w, 0.0)
            wb = w.astype(BF16)
            kept_ref[0, 0, kb] = _rows_into(jnp.zeros((2 * tq, tk), BF16), wb, first, tq)
            acc = _rows_into(acc, _rows_from(acc, first, tq) + _dot(wb, v_ref[pl.ds(off, tk), :].astype(BF16)), first, tq)
            return acc, _rows_into(c, _rows_from(c, first, tq) + jnp.sum(lrest, axis=1, keepdims=True), first, tq)

        carry = (jnp.zeros((2 * tq, LANES), F32), jnp.zeros((2 * tq, 1), F32))
        n_full = qi * diag
        for j in reversed(range(diag)):
            mask = ((scol + j * tk) < trow)[j * tk:]
            carry = block(n_full + j, carry, jnp.concatenate([mask, mask], axis=0), first=j * tk)

        def step(it, carry):
            return block(n_full - 1 - it, carry, None)

        acc, _ = lax.fori_loop(0, n_full, step, carry)
        o_ref[...] = _unstack_heads(acc, tq)

    return pl.pallas_call(
        body,
        name=name,
        grid=(n_pairs, t // tq),
        in_specs=[
            pl.BlockSpec((tq, LANES), lambda p, i: (i, p)),
            pl.BlockSpec((t, LANES), lambda p, i: (0, n_pairs + p)),
            pl.BlockSpec((t, LANES), lambda p, i: (0, 2 * n_pairs + p)),
        ],
        out_specs=[pl.BlockSpec((tq, LANES), lambda p, i: (i, p)),
                   pl.BlockSpec((1, 1, t // tk, 2 * tq, tk), lambda p, i: (p, i, 0, 0, 0))],
        out_shape=[jax.ShapeDtypeStruct((t, SB_WIDTH), F32),
                   jax.ShapeDtypeStruct((n_pairs, t // tq, t // tk, 2 * tq, tk), BF16)],
        compiler_params=_params("parallel", "parallel"),
    )(proj, proj, proj)


def _attn_bwd(proj, kept, do, *, name, tie=None):
    t = proj.shape[0]
    tq, tk = ATT_Q_TILE, ATT_K_BLOCK
    diag = tq // tk
    n_pairs = SB_WIDTH // LANES
    scale = SB_HEAD_DIM ** -0.5

    def body(q_ref, k_ref, v_ref, kept_ref, do_ref, *rest):
        dq_ref, dk_ref, dv_ref = rest[-3:]
        qi = pl.program_id(1)

        @pl.when(qi == 0)
        def _():
            dk_ref[...] = jnp.zeros_like(dk_ref)
            dv_ref[...] = jnp.zeros_like(dv_ref)

        qs = _stack_heads(q_ref[...] * scale).astype(BF16)
        dos = _stack_heads(do_ref[...]).astype(BF16)
        before = _tri(tk, lambda s, j: s < j)
        trow = lax.broadcasted_iota(jnp.int32, (tq, tk), 0)
        scol = lax.broadcasted_iota(jnp.int32, (tq, tk), 1)

        def block(kb, carry, causal, first=0):
            dq, cg = carry
            off = pl.multiple_of(kb * tk, tk)
            q_rows, do_rows = _rows_from(qs, first, tq), _rows_from(dos, first, tq)
            wb = _rows_from(kept_ref[0, 0, kb], first, tq)
            kblk = k_ref[pl.ds(off, tk), :].astype(BF16)
            sig = _sigmoid(_dot_nt(q_rows, kblk))
            g = wb.astype(F32) * _dot_nt(do_rows, v_ref[pl.ds(off, tk), :].astype(BF16))
            prior = _scan_dot(g, before) + _rows_from(cg, first, tq)
            dz = g - sig * (g + prior)
            if causal is not None:
                dz = jnp.where(causal, dz, 0.0)
            dzb = dz.astype(BF16)
            dq = _rows_into(dq, _rows_from(dq, first, tq) + _dot(dzb, kblk), first, tq)
            dk_ref[pl.ds(off, tk), :] += _dot_tn(dzb, q_rows)
            dv_ref[pl.ds(off, tk), :] += _dot_tn(wb, do_rows)
            return dq, _rows_into(cg, _rows_from(cg, first, tq) + jnp.sum(g, axis=1, keepdims=True), first, tq)

        n_full = qi * diag
        carry = lax.fori_loop(0, n_full, lambda kb, carry: block(kb, carry, None),
                              (jnp.zeros((2 * tq, LANES), F32), jnp.zeros((2 * tq, 1), F32)))
        for j in range(diag):
            mask = ((scol + j * tk) < trow)[j * tk:]
            carry = block(n_full + j, carry, jnp.concatenate([mask, mask], axis=0), first=j * tk)
        dq_ref[...] = (_unstack_heads(carry[0], tq) * scale).astype(BF16)

    tile_spec = pl.BlockSpec((tq, LANES), lambda p, i: (i, p))
    full_spec = pl.BlockSpec((t, LANES), lambda p, i: (0, p))
    return pl.pallas_call(
        body,
        name=name,
        grid=(n_pairs, t // tq),
        in_specs=[
            tile_spec,
            pl.BlockSpec((t, LANES), lambda p, i: (0, n_pairs + p)),
            pl.BlockSpec((t, LANES), lambda p, i: (0, 2 * n_pairs + p)),
            pl.BlockSpec((1, 1, t // tk, 2 * tq, tk), lambda p, i: (p, i, 0, 0, 0)),
            tile_spec,
        ] + ([] if tie is None else [pl.BlockSpec(memory_space=pl.ANY)]),
        out_specs=[tile_spec, full_spec, full_spec],
        out_shape=[jax.ShapeDtypeStruct((t, SB_WIDTH), BF16)] + [jax.ShapeDtypeStruct((t, SB_WIDTH), F32)] * 2,
        compiler_params=_params("arbitrary", "arbitrary"),
    )(proj, proj, proj, kept, do, *([] if tie is None else [tie]))


HG_BLOCK = 128
HG_HEADS = HG_WIDTH // HG_HEAD_DIM


def _chunk_mats(n):
    r = lax.broadcasted_iota(jnp.int32, (n, n), 0)
    c = lax.broadcasted_iota(jnp.int32, (n, n), 1)
    same = (r // HG_CHUNK) == (c // HG_CHUNK)
    upto = (same & (c <= r)).astype(BF16)
    whole = same.astype(BF16)
    onward = (same & (c >= r)).astype(BF16)
    return upto, whole, onward


def _rows_dot(mat, x):
    return _dot(jnp.concatenate([mat, mat, mat], axis=1), jnp.concatenate(_split3(x), axis=0))


def _split_heads(x):
    return jnp.stack([x[:, h * HG_HEAD_DIM:(h + 1) * HG_HEAD_DIM] for h in range(HG_HEADS)], axis=0)


def _merge_heads(x):
    return jnp.concatenate([x[h] for h in range(HG_HEADS)], axis=1)


def _lower_bound(lg_ref):
    lg = lg_ref[...]
    return _sigmoid(lg[0:1, :] - lg[1:2, :])


def _hgrn_prepare(q_ref, f_ref, lb, h, upto, whole):
    cols = slice(h * HG_HEAD_DIM, (h + 1) * HG_HEAD_DIM)
    lbh = lb[:, cols]
    sg = _sigmoid(f_ref[:, cols])
    forget = lbh + (1.0 - lbh) * sg
    logf = jnp.log(forget)
    kk = (1.0 - lbh) * (1.0 - sg)
    qv = q_ref[:, cols]
    qsig = _sigmoid(qv)
    qh = qv * qsig
    b = _rows_dot(upto, logf)
    blast = _rows_dot(whole, logf)
    return dict(lbh=lbh, sg=sg, forget=forget, kk=kk, qv=qv, qsig=qsig, qh=qh, b=b, eb=jnp.exp(b),
                ekb=jnp.exp(blast - b), dl=jnp.exp(blast))


def _hgrn_fwd(proj, logits, *, name):
    t = proj.shape[0]
    tb = HG_BLOCK
    nc = tb // HG_CHUNK
    hd = HG_HEAD_DIM

    def body(q_ref, f_ref, i_ref, lg_ref, o_ref, st_ref, state, qh_s, kk_s, b_s, qe_s, ke_s, dl_s):
        @pl.when(pl.program_id(0) == 0)
        def _():
            state[...] = jnp.zeros_like(state)

        lb = _lower_bound(lg_ref)
        upto, whole, _ = _chunk_mats(tb)
        for h in range(HG_HEADS):
            p = _hgrn_prepare(q_ref, f_ref, lb, h, upto, whole)
            qh_s[h] = p["qh"]
            kk_s[h] = p["kk"]
            b_s[h] = p["b"]
            qe_s[h] = (p["qh"] * p["eb"]).astype(BF16)
            ke_s[h] = (p["kk"] * p["ekb"]).astype(BF16)
            dl_s[h] = p["dl"]
        rowi = lax.broadcasted_iota(jnp.int32, (HG_HEADS, HG_CHUNK, hd), 1)

        def chunk(c, _):
            r0 = pl.multiple_of(c * HG_CHUNK, HG_CHUNK)
            rows = pl.ds(r0, HG_CHUNK)
            bc = b_s[:, rows, :]
            qc = qh_s[:, rows, :]
            kc = kk_s[:, rows, :]
            vc = _split_heads(i_ref[rows, :])
            s_in = state[...]
            st_ref[c] = s_in
            s_in_b = s_in.astype(BF16)
            qe = qe_s[:, rows, :]
            o = jnp.stack([_dot_nt(qe[h], s_in_b[h]) for h in range(HG_HEADS)], axis=0)
            for s in range(HG_CHUNK):
                pair = jnp.where(rowi >= s, qc * jnp.exp(bc - bc[:, s:s + 1, :]) * kc[:, s:s + 1, :], 0.0)
                o = o + jnp.sum(pair, axis=2, keepdims=True) * vc[:, s:s + 1, :]
            o_ref[rows, :] = _merge_heads(o)
            vcb = vc.astype(BF16)
            ke = ke_s[:, rows, :]
            update = jnp.stack([_dot_tn(vcb[h], ke[h]) for h in range(HG_HEADS)], axis=0)
            state[...] = s_in * dl_s[:, pl.ds(r0, 1), :] + update
            return 0

        lax.fori_loop(0, nc, chunk, 0)

    blk = lambda col: pl.BlockSpec((tb, HG_WIDTH), lambda i: (i, col))
    head_f32 = pltpu.VMEM((HG_HEADS, tb, hd), F32)
    head_bf16 = pltpu.VMEM((HG_HEADS, tb, hd), BF16)
    return pl.pallas_call(
        body,
        name=name,
        grid=(t // tb,),
        in_specs=[blk(3), blk(4), blk(5), pl.BlockSpec((2, HG_WIDTH), lambda i: (0, 0))],
        out_specs=[
            pl.BlockSpec((tb, HG_WIDTH), lambda i: (i, 0)),
            pl.BlockSpec((nc, HG_HEADS, hd, hd), lambda i: (i, 0, 0, 0)),
        ],
        out_shape=[
            jax.ShapeDtypeStruct((t, HG_WIDTH), F32),
            jax.ShapeDtypeStruct((t // HG_CHUNK, HG_HEADS, hd, hd), F32),
        ],
        scratch_shapes=[pltpu.VMEM((HG_HEADS, hd, hd), F32), head_f32, head_f32, head_f32, head_bf16, head_bf16,
                        head_f32],
        compiler_params=_params("arbitrary"),
    )(proj, proj, proj, logits)


def _hgrn_bwd(proj, logits, states, do, *, name):
    t = proj.shape[0]
    tb = HG_BLOCK
    nb = t // tb
    nc = tb // HG_CHUNK
    hd = HG_HEAD_DIM

    def body(q_ref, f_ref, i_ref, lg_ref, st_ref, do_ref, dq_ref, df_ref, di_ref, dlb_ref,
             dstate, qh_s, kk_s, b_s, eb_s, ekb_s, qe_s, ke_s, dl_s, dqh_s, dkk_s, dlf_s):
        step = pl.program_id(0)

        @pl.when(step == 0)
        def _():
            dstate[...] = jnp.zeros_like(dstate)
            dlb_ref[...] = jnp.zeros_like(dlb_ref)

        lb = _lower_bound(lg_ref)
        upto, whole, _ = _chunk_mats(tb)
        prepared = []
        for h in range(HG_HEADS):
            p = _hgrn_prepare(q_ref, f_ref, lb, h, upto, whole)
            prepared.append(p)
            qh_s[h] = p["qh"]
            kk_s[h] = p["kk"]
            b_s[h] = p["b"]
            eb_s[h] = p["eb"]
            ekb_s[h] = p["ekb"]
            qe_s[h] = (p["qh"] * p["eb"]).astype(BF16)
            ke_s[h] = (p["kk"] * p["ekb"]).astype(BF16)
            dl_s[h] = p["dl"]
        rowi = lax.broadcasted_iota(jnp.int32, (HG_CHUNK, hd), 0)
        r16 = lax.broadcasted_iota(jnp.int32, (HG_CHUNK, HG_CHUNK), 0)
        c16 = lax.broadcasted_iota(jnp.int32, (HG_CHUNK, HG_CHUNK), 1)
        onward = (c16 >= r16).astype(BF16)

        def chunk(it, _):
            c = nc - 1 - it
            r0 = pl.multiple_of(c * HG_CHUNK, HG_CHUNK)
            rows = pl.ds(r0, HG_CHUNK)
            for h in range(HG_HEADS):
                cols = slice(h * hd, (h + 1) * hd)
                bc = b_s[h, rows, :]
                qc = qh_s[h, rows, :]
                kc = kk_s[h, rows, :]
                vc = i_ref[rows, cols]
                doc = do_ref[rows, cols]
                s_in = st_ref[c, h]
                ds_out = dstate[h]
                ds_out_b = ds_out.astype(BF16)
                docb = doc.astype(BF16)
                dl_row = dl_s[h, pl.ds(r0, 1), :]
                dqh = _dot(docb, s_in.astype(BF16)) * eb_s[h, rows, :]
                dkk = _dot(vc.astype(BF16), ds_out_b) * ekb_s[h, rows, :]
                dv = _dot_nt(ke_s[h, rows, :], ds_out_b)
                db = dqh * qc - dkk * kc
                dwhole = jnp.sum(dkk * kc, axis=0, keepdims=True) + jnp.sum(ds_out * s_in, axis=0, keepdims=True) * dl_row
                dk_rows, dv_rows = [], []
                for s in range(HG_CHUNK):
                    keep = rowi >= s
                    e = jnp.exp(bc - bc[s:s + 1, :])
                    k_row = kc[s:s + 1, :]
                    pcol = jnp.sum(jnp.where(keep, qc * e * k_row, 0.0), axis=1, keepdims=True)
                    dpcol = jnp.sum(doc * vc[s:s + 1, :], axis=1, keepdims=True)
                    m = jnp.where(keep, e * dpcol, 0.0)
                    y = m * qc
                    dqh = dqh + m * k_row
                    db = db + y * k_row
                    dk_rows.append(jnp.sum(y, axis=0, keepdims=True))
                    dv_rows.append(jnp.sum(pcol * doc, axis=0, keepdims=True))
                dkk_pairs = jnp.concatenate(dk_rows, axis=0)
                dkk = dkk + dkk_pairs
                db = db - dkk_pairs * kc
                dv = dv + jnp.concatenate(dv_rows, axis=0)
                dqh_s[h, rows, :] = dqh
                dkk_s[h, rows, :] = dkk
                dlf_s[h, rows, :] = _rows_dot(onward, db) + dwhole
                di_ref[rows, cols] = dv.astype(BF16)
                dstate[h] = ds_out * dl_row + _dot_tn(docb, qe_s[h, rows, :])
            return 0

        lax.fori_loop(0, nc, chunk, 0)
        for h in range(HG_HEADS):
            cols = slice(h * hd, (h + 1) * hd)
            p = prepared[h]
            dq_ref[:, cols] = (dqh_s[h] * (p["qsig"] * (1.0 + p["qv"] * (1.0 - p["qsig"])))).astype(BF16)
            dforget = dlf_s[h] / p["forget"] - dkk_s[h]
            df_ref[:, cols] = (dforget * (1.0 - p["lbh"]) * p["sg"] * (1.0 - p["sg"])).astype(BF16)
            dlb_ref[:, cols] += jnp.sum(dforget * (1.0 - p["sg"]), axis=0, keepdims=True)

    blk = lambda col: pl.BlockSpec((tb, HG_WIDTH), lambda i: (nb - 1 - i, col))
    vec = pl.BlockSpec((1, HG_WIDTH), lambda i: (0, 0))
    head_f32 = pltpu.VMEM((HG_HEADS, tb, hd), F32)
    head_bf16 = pltpu.VMEM((HG_HEADS, tb, hd), BF16)
    return pl.pallas_call(
        body,
        name=name,
        grid=(nb,),
        in_specs=[
            blk(3), blk(4), blk(5),
            pl.BlockSpec((2, HG_WIDTH), lambda i: (0, 0)),
            pl.BlockSpec((nc, HG_HEADS, hd, hd), lambda i: (nb - 1 - i, 0, 0, 0)),
            blk(0),
        ],
        out_specs=[blk(0), blk(0), blk(0), vec],
        out_shape=[jax.ShapeDtypeStruct((t, HG_WIDTH), BF16)] * 3 + [jax.ShapeDtypeStruct((1, HG_WIDTH), F32)],
        scratch_shapes=[
            pltpu.VMEM((HG_HEADS, hd, hd), F32),
            head_f32, head_f32, head_f32, head_f32, head_f32, head_bf16, head_bf16, head_f32,
            head_f32, head_f32, head_f32,
        ],
        compiler_params=_params("arbitrary"),
    )(proj, proj, proj, logits, states, do)


def _group_mat(width, head_dim):
    r = lax.broadcasted_iota(jnp.int32, (width, width), 0)
    c = lax.broadcasted_iota(jnp.int32, (width, width), 1)
    return ((r // head_dim) == (c // head_dim)).astype(BF16)


def _head_mean(x, mat, head_dim):
    hi = x.astype(BF16)
    lo = (x - hi.astype(F32)).astype(BF16)
    return (_dot(hi, mat) + _dot(lo, mat)) * (1.0 / head_dim)


def _mix_out_fwd(o_sb, o_hg, proj, g_sb, g_hg, w_out, x1, *, name, tm=512):
    t = x1.shape[0]

    def body(osb_ref, ohg_ref, gate_ref, gsb_ref, ghg_ref, w_ref, x_ref, xo_ref, mt_ref):
        msb = _group_mat(SB_WIDTH, SB_HEAD_DIM)
        mhg = _group_mat(HG_WIDTH, HG_HEAD_DIM)
        osb = osb_ref[...]
        ohg = ohg_ref[...]
        nsb = osb * lax.rsqrt(_head_mean(osb * osb, msb, SB_HEAD_DIM) + EPS) * gsb_ref[...]
        gate = gate_ref[...]
        nhg = ohg * lax.rsqrt(_head_mean(ohg * ohg, mhg, HG_HEAD_DIM) + EPS) * ghg_ref[...] * (gate * _sigmoid(gate))
        mixed = jnp.concatenate([nsb, nhg], axis=1).astype(BF16)
        mt_ref[...] = mixed
        xo_ref[...] = x_ref[...] + _dot(mixed, w_ref[...])

    half = pl.BlockSpec((tm, SB_WIDTH), lambda i: (i, 0))
    vec = pl.BlockSpec((1, SB_WIDTH), lambda i: (0, 0))
    row = pl.BlockSpec((tm, D_MODEL), lambda i: (i, 0))
    return pl.pallas_call(
        body,
        name=name,
        grid=(t // tm,),
        in_specs=[half, half, pl.BlockSpec((tm, HG_WIDTH), lambda i: (i, 6)), vec, vec,
                  pl.BlockSpec((D_MODEL, D_MODEL), lambda i: (0, 0)), row],
        out_specs=[row, row],
        out_shape=[jax.ShapeDtypeStruct((t, D_MODEL), F32), jax.ShapeDtypeStruct((t, D_MODEL), BF16)],
        compiler_params=_params("parallel"),
    )(o_sb, o_hg, proj, g_sb, g_hg, w_out, x1)


def _mix_out_bwd(dx2, o_sb, o_hg, proj, g_sb, g_hg, w_out, *, name, tm=512):
    t = dx2.shape[0]

    def body(dx_ref, osb_ref, ohg_ref, gate_ref, gsb_ref, ghg_ref, w_ref, dosb_ref, dohg_ref, dgate_ref, dgsb_ref,
             dghg_ref, dxb_ref):
        i = pl.program_id(0)
        msb = _group_mat(SB_WIDTH, SB_HEAD_DIM)
        mhg = _group_mat(HG_WIDTH, HG_HEAD_DIM)
        dxb = dx_ref[...].astype(BF16)
        dxb_ref[...] = dxb
        dmixed = _dot_nt(dxb, w_ref[...])
        dnsb = dmixed[:, :SB_WIDTH]
        dy = dmixed[:, SB_WIDTH:]

        osb = osb_ref[...]
        rstd = lax.rsqrt(_head_mean(osb * osb, msb, SB_HEAD_DIM) + EPS)
        ohat = osb * rstd
        part_sb = jnp.sum(dnsb * ohat, axis=0, keepdims=True)
        dohat = dnsb * gsb_ref[...]
        dosb_ref[...] = rstd * (dohat - ohat * _head_mean(dohat * ohat, msb, SB_HEAD_DIM))

        ohg = ohg_ref[...]
        rstd = lax.rsqrt(_head_mean(ohg * ohg, mhg, HG_HEAD_DIM) + EPS)
        ohat = ohg * rstd
        gate = gate_ref[...]
        sig = _sigmoid(gate)
        dn = dy * (gate * sig)
        dgate_ref[...] = (dy * (ohat * ghg_ref[...]) * (sig * (1.0 + gate * (1.0 - sig)))).astype(BF16)
        part_hg = jnp.sum(dn * ohat, axis=0, keepdims=True)
        dohat = dn * ghg_ref[...]
        dohg_ref[...] = rstd * (dohat - ohat * _head_mean(dohat * ohat, mhg, HG_HEAD_DIM))

        @pl.when(i == 0)
        def _():
            dgsb_ref[...] = part_sb
            dghg_ref[...] = part_hg

        @pl.when(i > 0)
        def _():
            dgsb_ref[...] += part_sb
            dghg_ref[...] += part_hg

    half = pl.BlockSpec((tm, SB_WIDTH), lambda i: (i, 0))
    vec = pl.BlockSpec((1, SB_WIDTH), lambda i: (0, 0))
    row = pl.BlockSpec((tm, D_MODEL), lambda i: (i, 0))
    return pl.pallas_call(
        body,
        name=name,
        grid=(t // tm,),
        in_specs=[row, half, half, pl.BlockSpec((tm, HG_WIDTH), lambda i: (i, 6)), vec, vec,
                  pl.BlockSpec((D_MODEL, D_MODEL), lambda i: (0, 0))],
        out_specs=[half, half, half, vec, vec, row],
        out_shape=[jax.ShapeDtypeStruct((t, SB_WIDTH), F32)] * 2 + [jax.ShapeDtypeStruct((t, SB_WIDTH), BF16)]
        + [jax.ShapeDtypeStruct((1, SB_WIDTH), F32)] * 2 + [jax.ShapeDtypeStruct((t, D_MODEL), BF16)],
        compiler_params=_params("arbitrary"),
    )(dx2, o_sb, o_hg, proj, g_sb, g_hg, w_out)


def _local_step(x, target, norms, logits, w, weights_after=None, grads_ready=None):
    w = dict(w)
    x1, a1, b1, h1, s1, hm = _ffn_fwd(x, norms["ffn1"], w["g1t"], w["u1t"], w["d1"], name="ffn1_fwd",
                                      next_gain=norms["mix"])
    if weights_after is not None:
        w.update(weights_after("ffn1", x1))
    proj = _mm(hm, w["int"], name="in_proj", tm=512, tn=IN_COLS, nt=True)
    o_sb, sb_kept = _attn_fwd(proj, name="sb_attn_fwd")
    o_hg, states = _hgrn_fwd(proj, logits, name="hgrn2_fwd")
    x2, mixed = _mix_out_fwd(o_sb, o_hg, proj, norms["sb"], norms["hg"], w["out"], x1, name="mix_out_fwd")
    if weights_after is not None:
        w.update(weights_after("mix", x2))
    dx3, a2, b2, h2, s2, d_final, loss_row = _ffn_fwd(x2, norms["ffn2"], w["g2t"], w["u2t"], w["d2"], name="ffn2_fwd",
                                                      head=(norms["final"], target))

    def weight_grad(lhs, rhs, name, tie=None):
        return _mm(lhs, rhs, name=name, tm=256, tn=D_MODEL, ta=True, out_dtype=BF16, tie=tie)

    def sent(stage):
        return grads_ready(stage, gw) if grads_ready is not None else None

    gw, gv = {}, {"final": d_final}
    dx2, gv["ffn2"], da2, db2, dob2 = _ffn_bwd(dx3, x2, norms["ffn2"], a2, b2, w["g2t"], w["u2t"], w["d2"],
                                               name="ffn2_bwd")
    gw["g2t"] = weight_grad(da2, h2, "ffn2_dgate")
    gw["u2t"] = weight_grad(db2, h2, "ffn2_dup")
    gw["d2"] = weight_grad(s2, dob2, "ffn2_ddown")

    do_sb, do_hg, d_gate, gv["sb"], gv["hg"], dx2b = _mix_out_bwd(
        dx2, o_sb, o_hg, proj, norms["sb"], norms["hg"], w["out"], name="mix_out_bwd")
    gw["out"] = weight_grad(mixed, dx2b, "out_dw")
    tie = sent("mix")
    dq_sb, dk_sb, dv_sb = _attn_bwd(proj, sb_kept, do_sb, name="sb_attn_bwd", tie=tie)
    dq_hg, df_hg, di_hg, d_lb = _hgrn_bwd(proj, logits if tie is None else logits + tie[0, 0], states, do_hg,
                                          name="hgrn2_bwd")
    dproj = jnp.concatenate([dq_sb, dk_sb.astype(BF16), dv_sb.astype(BF16), dq_hg, df_hg, di_hg, d_gate], axis=1)
    gw["int"] = weight_grad(dproj, hm, "in_dw")
    tie = sent("in")
    dx1, gv["mix"] = _in_proj_bwd(dproj, w["int"], x1, norms["mix"] if tie is None else norms["mix"] + tie[0, 0], dx2,
                                  name="in_dx")

    dx, gv["ffn1"], da1, db1, dob1 = _ffn_bwd(dx1, x, norms["ffn1"], a1, b1, w["g1t"], w["u1t"], w["d1"],
                                              name="ffn1_bwd")
    gw["g1t"] = weight_grad(da1, h1, "ffn1_dgate")
    gw["u1t"] = weight_grad(db1, h1, "ffn1_dup", tie=sent("g1t"))
    gw["d1"] = weight_grad(s1, dob1, "ffn1_ddown", tie=sent("u1t"))
    sent("d1")
    gv["lb"] = d_lb
    return loss_row, dx, gw, gv


HBM = pl.BlockSpec(memory_space=pl.ANY)


def _place():
    return lax.axis_index("x"), lax.axis_index("y"), lax.axis_index("c")


def _slot(px, py, pc):
    return 4 * px + 2 * py + pc


def _all_gather(blocks, *, name):
    n = len(blocks)

    def body(*refs):
        ins, outs = refs[:n], refs[n:2 * n]
        send_sems, recv_sems, local_sems = refs[2 * n:]
        x, y, c = _place()
        me, sibling = (x, y, c), (x, y, 1 - c)
        chips = [(1 - x, y), (x, 1 - y), (1 - x, 1 - y)]

        def copy(a, k, block, to, src=None):
            dst = outs[a].at[_slot(*block)]
            return pltpu.make_async_remote_copy(
                src_ref=dst if src is None else src, dst_ref=dst, send_sem=send_sems.at[7 * a + k],
                recv_sem=recv_sems.at[7 * a + k], device_id=to, device_id_type=MESH)

        mine = [pltpu.make_async_copy(ins[a], outs[a].at[_slot(*me)], local_sems.at[a]) for a in range(n)]
        for cp in mine:
            cp.start()
        first = []
        for a in range(n):
            first.append(copy(a, 0, me, sibling, src=ins[a]))
            first += [copy(a, 1 + j, me, (*chip, c), src=ins[a]) for j, chip in enumerate(chips)]
        for cp in first:
            cp.start()
        passed = []
        for j, chip in enumerate(chips):
            for a in range(n):
                copy(a, 1 + j, (*chip, c), me).wait_recv()
                fwd = copy(a, 4 + j, (*chip, c), sibling)
                fwd.start()
                passed.append(fwd)
        for a in range(n):
            copy(a, 0, sibling, me).wait_recv()
            for j, chip in enumerate(chips):
                copy(a, 4 + j, (*chip, 1 - c), me).wait_recv()
        for cp in first + passed:
            cp.wait_send()
        for cp in mine:
            cp.wait()

    return pl.pallas_call(
        body,
        name=name,
        in_specs=[HBM] * n,
        out_specs=[HBM] * n,
        out_shape=[jax.ShapeDtypeStruct((N_DEV,) + b.shape, b.dtype) for b in blocks],
        scratch_shapes=[pltpu.SemaphoreType.DMA((7 * n,)), pltpu.SemaphoreType.DMA((7 * n,)),
                        pltpu.SemaphoreType.DMA((n,))],
    )(*blocks)


def _flipped(place, d):
    return tuple(1 - p if (d >> (2 - axis)) & 1 else p for axis, p in enumerate(place))


SEM = pl.BlockSpec(memory_space=pltpu.SEMAPHORE)
EFFECT = pltpu.SideEffectType.DATAFLOW_SIDE_EFFECTING


def _split_copies(me, srcs, lands, send_sems, recv_sems, by_owner):
    copies = []
    for d in range(1, N_DEV):
        peer = _flipped(me, d)
        for a, (src, land) in enumerate(zip(srcs, lands)):
            copies.append(pltpu.make_async_remote_copy(
                src_ref=src.at[_slot(*peer)] if by_owner else src, dst_ref=land.at[_slot(*me)],
                send_sem=send_sems.at[7 * a + d - 1], recv_sem=recv_sems.at[7 * a + d - 1], device_id=peer,
                device_id_type=MESH))
    return copies


def _copies_start(srcs, *, name, by_owner, after=None):
    n = len(srcs)
    extra = [] if after is None else [after]
    land_shapes = [s.shape if by_owner else (N_DEV,) + s.shape for s in srcs]
    lands = [pltpu.with_memory_space_constraint(lax.empty(shape, s.dtype), pltpu.HBM) for shape, s in zip(land_shapes, srcs)]
    srcs = [pltpu.with_memory_space_constraint(s, pltpu.HBM) for s in srcs]

    def body(*refs):
        src_refs, land_refs = refs[:n], refs[n:2 * n]
        send_sems, recv_sems = refs[2 * n + len(extra)], refs[2 * n + len(extra) + 1]
        token = refs[-1]
        for cp in _split_copies(_place(), src_refs, land_refs, send_sems, recv_sems, by_owner):
            cp.start()
        token[...] = jnp.zeros_like(token)

    out = pl.pallas_call(
        body,
        name=name,
        in_specs=[HBM] * (2 * n + len(extra)),
        out_specs=[SEM, SEM] + [HBM] * (2 * n) + [pl.BlockSpec(memory_space=pltpu.VMEM)],
        out_shape=[pltpu.SemaphoreType.DMA((7 * n,)), pltpu.SemaphoreType.DMA((7 * n,))]
        + [pltpu.HBM(s.shape, s.dtype) for s in srcs] + [pltpu.HBM(shape, s.dtype) for shape, s in zip(land_shapes, srcs)]
        + [jax.ShapeDtypeStruct((8, LANES), F32)],
        input_output_aliases={i: 2 + i for i in range(2 * n)},
        compiler_params=pltpu.CompilerParams(has_side_effects=EFFECT),
    )(*srcs, *lands, *extra)
    return (out[0], out[1], out[2:2 + n], out[2 + n:2 + 2 * n]), out[-1]


def _copies_wait(started, after, *, name, by_owner):
    send_sems, recv_sems, srcs, lands = started
    n = len(srcs)

    def body(*refs):
        src_refs, land_refs = refs[:n], refs[n:2 * n]
        for cp in _split_copies(_place(), src_refs, land_refs, refs[2 * n], refs[2 * n + 1], by_owner):
            cp.wait_send()
            cp.wait_recv()

    out = pl.pallas_call(
        body,
        name=name,
        in_specs=[HBM] * (2 * n) + [SEM, SEM, HBM],
        out_specs=[HBM] * (2 * n),
        out_shape=[pltpu.HBM(s.shape, s.dtype) for s in srcs] + [pltpu.HBM(s.shape, s.dtype) for s in lands],
        input_output_aliases={i: i for i in range(2 * n)},
        compiler_params=pltpu.CompilerParams(has_side_effects=EFFECT),
    )(*srcs, *lands, send_sems, recv_sems, after)
    return out[:n], out[n:]


def _with_own(lands, own, slot):
    zero = jnp.zeros((), jnp.int32)
    return [lax.dynamic_update_slice(land, o[None], (slot.astype(jnp.int32),) + (zero,) * o.ndim)
            for land, o in zip(lands, own)]


def _adamw(w, g, m, v):
    m = ADAM_B1 * m + (1.0 - ADAM_B1) * g
    v = ADAM_B2 * v + (1.0 - ADAM_B2) * (g * g)
    m_hat = m / (1.0 - ADAM_B1 ** ADAM_STEP)
    v_hat = v / (1.0 - ADAM_B2 ** ADAM_STEP)
    delta = -ADAM_LR * (m_hat / (jnp.sqrt(v_hat) + ADAM_EPS) + ADAM_WD * w)
    return delta, m, v


def _sum_and_update(parts, w, m, v, *, name, tie=None):
    _, rows, cols = w.shape
    tr = rows // 2

    def body(p_ref, w_ref, m_ref, v_ref, *rest):
        g_ref, d_ref, mo_ref, vo_ref = rest[-4:]
        g = p_ref[0].astype(F32)
        for s in range(1, N_DEV):
            g = g + p_ref[s].astype(F32)
        g_ref[0] = g
        d_ref[0], mo_ref[0], vo_ref[0] = _adamw(w_ref[0], g, m_ref[0], v_ref[0])

    flat = pl.BlockSpec((1, tr, cols), lambda i: (0, i, 0))
    return pl.pallas_call(
        body,
        name=name,
        grid=(rows // tr,),
        in_specs=[pl.BlockSpec((N_DEV, tr, cols), lambda i: (0, i, 0)), flat, flat, flat]
        + ([] if tie is None else [pl.BlockSpec(memory_space=pl.ANY)]),
        out_specs=[flat] * 4,
        out_shape=[jax.ShapeDtypeStruct((1, rows, cols), F32)] * 4,
        compiler_params=_params("parallel"),
    )(parts, w, m, v, *([] if tie is None else [tie]))


VEC_ROWS = 8
ROW_LOGITS, ROW_LOSS = 5, 7


def _vectors_update(part, w, m, v, *, name):
    def body(p_ref, w_ref, m_ref, v_ref, g_ref, d_ref, mo_ref, vo_ref, loss_ref, all_ref, send_sems, recv_sems):
        me = _place()
        all_ref[_slot(*me)] = p_ref[...]
        copies = []
        for d in range(1, N_DEV):
            peer = _flipped(me, d)
            copies.append(pltpu.make_async_remote_copy(
                src_ref=p_ref, dst_ref=all_ref.at[_slot(*me)], send_sem=send_sems.at[d - 1], recv_sem=recv_sems.at[d - 1],
                device_id=peer, device_id_type=MESH))
        for cp in copies:
            cp.start()
        for cp in copies:
            cp.wait()
        total = all_ref[0]
        for s in range(1, N_DEV):
            total = total + all_ref[s]
        wv = w_ref[...]
        half = D_MODEL // 2
        lb = _sigmoid(wv[ROW_LOGITS:ROW_LOGITS + 1, :half] - wv[ROW_LOGITS:ROW_LOGITS + 1, half:])
        d_first = total[ROW_LOGITS:ROW_LOGITS + 1, :half] * lb * (1.0 - lb)
        d_logits = jnp.concatenate([d_first, -d_first], axis=1)
        rowi = lax.broadcasted_iota(jnp.int32, (VEC_ROWS, D_MODEL), 0)
        g = jnp.where(rowi == ROW_LOGITS, d_logits, jnp.where(rowi < ROW_LOGITS, total, 0.0))
        g_ref[...] = g
        d_ref[...], mo_ref[...], vo_ref[...] = _adamw(wv, g, m_ref[...], v_ref[...])
        loss_ref[...] = total[ROW_LOSS:ROW_LOSS + 1, :]

    vmem = pl.BlockSpec(memory_space=pltpu.VMEM)
    return pl.pallas_call(
        body,
        name=name,
        in_specs=[vmem] * 4,
        out_specs=[vmem] * 5,
        out_shape=[jax.ShapeDtypeStruct((VEC_ROWS, D_MODEL), F32)] * 4 + [jax.ShapeDtypeStruct((1, D_MODEL), F32)],
        scratch_shapes=[pltpu.VMEM((N_DEV, VEC_ROWS, D_MODEL), F32), pltpu.SemaphoreType.DMA((7,)),
                        pltpu.SemaphoreType.DMA((7,))],
    )(part, w, m, v)


TRANSPOSED = ("g1t", "u1t", "g2t", "u2t", "int")


def _vector_rows(rows):
    rowi = lax.broadcasted_iota(jnp.int32, (VEC_ROWS, D_MODEL), 0)
    out = jnp.zeros((VEC_ROWS, D_MODEL), F32)
    for i, r in enumerate(rows):
        if r is not None:
            out = jnp.where(rowi == i, r, out)
    return out


def kernel(x, ffn1_norm, ffn1_w_gate, ffn1_w_up, ffn1_w_down, mix_norm, w_in, sb_out_norm, hg_lower_bound_logits, hg_out_norm, w_out, ffn2_norm, ffn2_w_gate, ffn2_w_up, ffn2_w_down, final_norm, loss_target, m_ffn1_norm, m_ffn1_w_gate, m_ffn1_w_up, m_ffn1_w_down, m_mix_norm, m_w_in, m_sb_out_norm, m_hg_lower_bound_logits, m_hg_out_norm, m_w_out, m_ffn2_norm, m_ffn2_w_gate, m_ffn2_w_up, m_ffn2_w_down, m_final_norm, v_ffn1_norm, v_ffn1_w_gate, v_ffn1_w_up, v_ffn1_w_down, v_mix_norm, v_w_in, v_sb_out_norm, v_hg_lower_bound_logits, v_hg_out_norm, v_w_out, v_ffn2_norm, v_ffn2_w_gate, v_ffn2_w_up, v_ffn2_w_down, v_final_norm):
    def matrices(g1, u1, d1, win, wout, g2, u2, d2):
        return {"g1t": g1, "u1t": u1, "d1": d1, "int": win, "out": wout, "g2t": g2, "u2t": u2, "d2": d2}

    def vectors(n1, nm, nsb, lg, nhg, n2, nf):
        return [n1, nm, n2, nf.reshape(1, D_MODEL), jnp.concatenate([nsb, nhg], axis=1), lg.reshape(1, D_MODEL), None, None]

    w_sh = matrices(ffn1_w_gate, ffn1_w_up, ffn1_w_down, w_in, w_out, ffn2_w_gate, ffn2_w_up, ffn2_w_down)
    m_sh = matrices(m_ffn1_w_gate, m_ffn1_w_up, m_ffn1_w_down, m_w_in, m_w_out, m_ffn2_w_gate, m_ffn2_w_up, m_ffn2_w_down)
    v_sh = matrices(v_ffn1_w_gate, v_ffn1_w_up, v_ffn1_w_down, v_w_in, v_w_out, v_ffn2_w_gate, v_ffn2_w_up, v_ffn2_w_down)
    keys = list(w_sh)

    slot = _slot(*_place())

    def full(key, stack):
        return stack.reshape(-1, D_MODEL)

    def by_owner(key, grad):
        return grad.reshape(N_DEV, -1, D_MODEL)

    def view(key, a):
        return jnp.swapaxes(a, 1, 2) if key in TRANSPOSED else a

    blocks = {k: view(k, w_sh[k])[0].astype(BF16) for k in keys}
    first, mid, last = ("g1t", "u1t", "d1"), ("int", "out"), ("g2t", "u2t", "d2")
    w_first = {k: full(k, s) for k, s in zip(first, _all_gather([blocks[k] for k in first], name="gather_ffn1"))}
    flights = {}
    flights["ffn1"], token_mid = _copies_start([blocks[k] for k in mid], name="gather_mid_start", by_owner=False,
                                               after=w_first["d1"])
    flights["mix"], token_last = _copies_start([blocks[k] for k in last], name="gather_ffn2_start", by_owner=False,
                                               after=token_mid)

    def weights_after(stage, result):
        group = mid if stage == "ffn1" else last
        own, lands = _copies_wait(flights[stage], result, name="gather_" + stage + "_wait", by_owner=False)
        return {k: full(k, s) for k, s in zip(group, _with_own(lands, own, slot))}

    groups = {"mix": ("g2t", "u2t", "d2", "out"), "in": ("int",), "g1t": ("g1t",), "u1t": ("u1t",), "d1": ("d1",)}
    sent, sent_tokens = {}, []

    def grads_ready(stage, gw):
        stacks = [by_owner(k, gw[k]) for k in groups[stage]]
        flight, token = _copies_start(stacks, name="grads_" + stage + "_start", by_owner=True)
        sent[stage] = flight
        sent_tokens.append(token)
        return token

    norms = {"ffn1": ffn1_norm + token_last[0, 0], "mix": mix_norm, "sb": sb_out_norm, "hg": hg_out_norm,
             "ffn2": ffn2_norm, "final": final_norm.reshape(1, D_MODEL)}
    loss_row, grad_x, gw, gv = _local_step(x[0], loss_target[0], norms, hg_lower_bound_logits, w_first, weights_after,
                                           grads_ready)

    updated, after = {}, sent_tokens[-1]
    for stage, flight in sent.items():
        stacks, lands = _copies_wait(flight, after, name="grads_" + stage + "_wait", by_owner=True)
        own = [lax.dynamic_index_in_dim(s, slot, keepdims=False) for s in stacks]
        for k, part in zip(groups[stage], _with_own(lands, own, slot)):
            updated[k] = _sum_and_update(part, view(k, w_sh[k]), view(k, m_sh[k]), view(k, v_sh[k]), name="adamw_" + k,
                                         tie=after)
            after = updated[k][0]
    mats = [{k: view(k, updated[k][i]) for k in keys} for i in range(4)]

    lb_row = jnp.concatenate([gv["lb"], jnp.zeros_like(gv["lb"])], axis=1)
    part = _vector_rows([gv["ffn1"], gv["mix"], gv["ffn2"], gv["final"], jnp.concatenate([gv["sb"], gv["hg"]], axis=1),
                         lb_row, None, loss_row])
    vec_w = _vector_rows(vectors(ffn1_norm, mix_norm, sb_out_norm, hg_lower_bound_logits, hg_out_norm, ffn2_norm, final_norm))
    vec_m = _vector_rows(vectors(m_ffn1_norm, m_mix_norm, m_sb_out_norm, m_hg_lower_bound_logits, m_hg_out_norm,
                                 m_ffn2_norm, m_final_norm))
    vec_v = _vector_rows(vectors(v_ffn1_norm, v_mix_norm, v_sb_out_norm, v_hg_lower_bound_logits, v_hg_out_norm,
                                 v_ffn2_norm, v_final_norm))
    *vecs, loss_out = _vectors_update(part, vec_w, vec_m, vec_v, name="vectors_update")

    def leaves(mat, vec):
        half = D_MODEL // 2
        return (
            vec[0:1], mat["g1t"], mat["u1t"], mat["d1"], vec[1:2], mat["int"], vec[4:5, :half],
            vec[ROW_LOGITS].reshape(2, half), vec[4:5, half:], mat["out"], vec[2:3], mat["g2t"], mat["u2t"],
            mat["d2"], vec[3],
        )

    out = [loss_out[0, 0], grad_x[None]]
    for mat, vec in zip(mats, vecs):
        out.extend(leaves(mat, vec))
    return tuple(out)
```

```python
import jax
import jax.numpy as jnp
from jax import lax
from jax.experimental import pallas as pl
from jax.experimental.pallas import tpu as pltpu

F32, BF16 = jnp.float32, jnp.bfloat16
D_MODEL = 1024
D_FF = 2816
SB_WIDTH = 512
HG_WIDTH = 512
SB_HEAD_DIM = 64
HG_HEAD_DIM = 128
IN_COLS = 3584
EPS = 1e-6
N_DEV = 8
LANES = 128
HG_CHUNK = 16
VMEM_LIMIT_BYTES = 48 * 1024 * 1024
FFN_BWD_VMEM_LIMIT_BYTES = 56 * 1024 * 1024
ADAM_LR, ADAM_B1, ADAM_B2, ADAM_EPS, ADAM_WD, ADAM_STEP = 0.001, 0.9, 0.999, 1e-08, 0.01, 10
MESH = pl.DeviceIdType.MESH


def _params(*semantics, vmem_limit_bytes=VMEM_LIMIT_BYTES):
    return pltpu.CompilerParams(dimension_semantics=semantics, vmem_limit_bytes=vmem_limit_bytes)


def _dot(a, b):
    return jnp.dot(a, b, preferred_element_type=F32)


def _dot_nt(a, b):
    return lax.dot_general(a, b, (((1,), (1,)), ((), ())), preferred_element_type=F32)


def _dot_tn(a, b):
    return lax.dot_general(a, b, (((0,), (0,)), ((), ())), preferred_element_type=F32)


def _split3(x):
    hi = x.astype(BF16)
    r1 = x - hi.astype(F32)
    mid = r1.astype(BF16)
    lo = (r1 - mid.astype(F32)).astype(BF16)
    return hi, mid, lo


def _rms(xv):
    rstd = lax.rsqrt(jnp.mean(xv * xv, axis=-1, keepdims=True) + EPS)
    return xv * rstd, rstd


def _sigmoid(x):
    return 0.5 + 0.5 * jnp.tanh(0.5 * x)


def _loss_terms(xv, gain, target):
    xhat, rstd = _rms(xv)
    err = xhat * gain - target
    loss = 0.5 * jnp.sum(jnp.mean(err * err, axis=-1, keepdims=True), axis=0, keepdims=True)
    dy = err * (1.0 / xv.shape[-1])
    dxh = dy * gain
    dx = rstd * (dxh - xhat * jnp.mean(dxh * xhat, axis=-1, keepdims=True))
    return dx, jnp.sum(dy * xhat, axis=0, keepdims=True), loss


def _mm(a, b, *, name, tm, tn, nt=False, ta=False, out_dtype=F32, tie=None):
    k, m = a.shape if ta else a.shape[::-1]
    n = b.shape[0] if nt else b.shape[1]
    assert m % tm == 0 and n % tn == 0 and not (nt and ta), (name, a.shape, b.shape, tm, tn)

    def body(a_ref, b_ref, *rest):
        av = a_ref[...].astype(BF16)
        bv = b_ref[...].astype(BF16)
        rest[-1][...] = (_dot_nt(av, bv) if nt else _dot_tn(av, bv) if ta else _dot(av, bv)).astype(out_dtype)

    in_specs = [
        pl.BlockSpec((k, tm), lambda i, j: (0, i)) if ta else pl.BlockSpec((tm, k), lambda i, j: (i, 0)),
        pl.BlockSpec((tn, k), lambda i, j: (j, 0)) if nt else pl.BlockSpec((k, tn), lambda i, j: (0, j)),
    ]
    operands = [a, b]
    if tie is not None:
        in_specs.append(pl.BlockSpec(memory_space=pl.ANY))
        operands.append(tie)
    return pl.pallas_call(
        body,
        name=name,
        grid=(m // tm, n // tn),
        in_specs=in_specs,
        out_specs=pl.BlockSpec((tm, tn), lambda i, j: (i, j)),
        out_shape=jax.ShapeDtypeStruct((m, n), out_dtype),
        compiler_params=_params("parallel", "parallel"),
    )(*operands)


def _ffn_fwd(x, gain, wgt, wut, wd, *, name, next_gain=None, head=None, tm=1024, tf=256):
    t = x.shape[0]
    nj = D_FF // tf
    extra_in = [] if next_gain is None else [next_gain]
    extra_in += [] if head is None else list(head)

    def body(x_ref, g_ref, wg_ref, wu_ref, wd_prev_ref, wd_last_ref, *rest):
        extra, (xo_ref, a_ref, b_ref, h_ref, st_ref) = rest[:len(extra_in)], rest[len(extra_in):len(extra_in) + 5]
        tail_out, (acc, s_prev) = rest[len(extra_in) + 5:-2], rest[-2:]
        i = pl.program_id(0)
        j = pl.program_id(1)

        @pl.when(j == 0)
        def _():
            xhat, _ = _rms(x_ref[...])
            h_ref[...] = (xhat * g_ref[...]).astype(BF16)
            acc[...] = jnp.zeros_like(acc)
            s_prev[...] = jnp.zeros_like(s_prev)

        acc[...] += _dot(s_prev[...], wd_prev_ref[...])
        h = h_ref[...]
        a = _dot_nt(h, wg_ref[...])
        b = _dot_nt(h, wu_ref[...])
        a_ref[...] = a.astype(BF16)
        b_ref[...] = b.astype(BF16)
        s = (a * _sigmoid(a) * b).astype(BF16)
        st_ref[...] = s
        s_prev[...] = s

        @pl.when(j == nj - 1)
        def _():
            xo = x_ref[...] + 0.5 * (acc[...] + _dot(s, wd_last_ref[...]))
            if head is None:
                xo_ref[...] = xo
            if next_gain is not None:
                tail_out[0][...] = (_rms(xo)[0] * extra[0][...]).astype(BF16)
            if head is not None:
                gain_ref, target_ref = extra[-2:]
                dg_ref, loss_ref = tail_out[-2:]
                xo_ref[...], part_g, part_loss = _loss_terms(xo, gain_ref[...], target_ref[...])

                @pl.when(i == 0)
                def _():
                    dg_ref[...] = part_g
                    loss_ref[...] = jnp.broadcast_to(part_loss, loss_ref.shape)

                @pl.when(i > 0)
                def _():
                    dg_ref[...] += part_g
                    loss_ref[...] += jnp.broadcast_to(part_loss, loss_ref.shape)

    row = pl.BlockSpec((tm, D_MODEL), lambda i, j: (i, 0))
    vec = pl.BlockSpec((1, D_MODEL), lambda i, j: (0, 0))
    tile = pl.BlockSpec((tm, tf), lambda i, j: (i, j))
    weights = pl.BlockSpec((tf, D_MODEL), lambda i, j: (j, 0))
    tail_specs = ([] if next_gain is None else [row]) + ([] if head is None else [vec, vec])
    tail_shapes = ([] if next_gain is None else [jax.ShapeDtypeStruct((t, D_MODEL), BF16)]) + (
        [] if head is None else [jax.ShapeDtypeStruct((1, D_MODEL), F32)] * 2)
    return pl.pallas_call(
        body,
        name=name,
        grid=(t // tm, nj),
        in_specs=[
            row, vec, weights, weights,
            pl.BlockSpec((tf, D_MODEL), lambda i, j: (jnp.maximum(j - 1, 0), 0)),
            pl.BlockSpec((tf, D_MODEL), lambda i, j: (nj - 1, 0)),
        ] + ([] if next_gain is None else [vec]) + ([] if head is None else [vec, row]),
        out_specs=[row, tile, tile, row, tile] + tail_specs,
        out_shape=[
            jax.ShapeDtypeStruct((t, D_MODEL), F32),
            jax.ShapeDtypeStruct((t, D_FF), BF16),
            jax.ShapeDtypeStruct((t, D_FF), BF16),
            jax.ShapeDtypeStruct((t, D_MODEL), BF16),
            jax.ShapeDtypeStruct((t, D_FF), BF16),
        ] + tail_shapes,
        scratch_shapes=[pltpu.VMEM((tm, D_MODEL), F32), pltpu.VMEM((tm, tf), BF16)],
        compiler_params=_params("arbitrary", "arbitrary"),
    )(x, gain, wgt, wut, wd, wd, *extra_in)


def _ffn_bwd(dout, x, gain, a, b, wgt, wut, wd, *, name, tm=1024, tf=256):
    t = x.shape[0]
    nj = D_FF // tf

    def body(do_ref, x_ref, g_ref, a_ref, b_ref, wg_prev_ref, wu_prev_ref, wg_last_ref, wu_last_ref, wd_ref,
             dx_ref, dg_ref, da_ref, db_ref, dob_ref, dob_scr, dh, da_prev, db_prev):
        i = pl.program_id(0)
        j = pl.program_id(1)

        @pl.when(j == 0)
        def _():
            d = (0.5 * do_ref[...]).astype(BF16)
            dob_scr[...] = d
            dob_ref[...] = d
            dh[...] = jnp.zeros_like(dh)
            da_prev[...] = jnp.zeros_like(da_prev)
            db_prev[...] = jnp.zeros_like(db_prev)

        dh[...] += _dot(da_prev[...], wg_prev_ref[...]) + _dot(db_prev[...], wu_prev_ref[...])
        ds = _dot_nt(dob_scr[...], wd_ref[...])
        av = a_ref[...].astype(F32)
        bv = b_ref[...].astype(F32)
        sig = _sigmoid(av)
        dbv = (ds * (av * sig)).astype(BF16)
        dav = (ds * bv * (sig * (1.0 + av * (1.0 - sig)))).astype(BF16)
        da_ref[...] = dav
        db_ref[...] = dbv
        da_prev[...] = dav
        db_prev[...] = dbv

        @pl.when(j == nj - 1)
        def _():
            xhat, rstd = _rms(x_ref[...])
            dhv = dh[...] + _dot(dav, wg_last_ref[...]) + _dot(dbv, wu_last_ref[...])
            part = jnp.sum(dhv * xhat, axis=0, keepdims=True)

            @pl.when(i == 0)
            def _():
                dg_ref[...] = part

            @pl.when(i > 0)
            def _():
                dg_ref[...] += part

            dxh = dhv * g_ref[...]
            dx_ref[...] = do_ref[...] + rstd * (dxh - xhat * jnp.mean(dxh * xhat, axis=-1, keepdims=True))

    return pl.pallas_call(
        body,
        name=name,
        grid=(t // tm, nj),
        in_specs=[
            pl.BlockSpec((tm, D_MODEL), lambda i, j: (i, 0)),
            pl.BlockSpec((tm, D_MODEL), lambda i, j: (i, 0)),
            pl.BlockSpec((1, D_MODEL), lambda i, j: (0, 0)),
            pl.BlockSpec((tm, tf), lambda i, j: (i, j)),
            pl.BlockSpec((tm, tf), lambda i, j: (i, j)),
            pl.BlockSpec((tf, D_MODEL), lambda i, j: (jnp.maximum(j - 1, 0), 0)),
            pl.BlockSpec((tf, D_MODEL), lambda i, j: (jnp.maximum(j - 1, 0), 0)),
            pl.BlockSpec((tf, D_MODEL), lambda i, j: (nj - 1, 0)),
            pl.BlockSpec((tf, D_MODEL), lambda i, j: (nj - 1, 0)),
            pl.BlockSpec((tf, D_MODEL), lambda i, j: (j, 0)),
        ],
        out_specs=[
            pl.BlockSpec((tm, D_MODEL), lambda i, j: (i, 0)),
            pl.BlockSpec((1, D_MODEL), lambda i, j: (0, 0)),
            pl.BlockSpec((tm, tf), lambda i, j: (i, j)),
            pl.BlockSpec((tm, tf), lambda i, j: (i, j)),
            pl.BlockSpec((tm, D_MODEL), lambda i, j: (i, 0)),
        ],
        out_shape=[
            jax.ShapeDtypeStruct((t, D_MODEL), F32),
            jax.ShapeDtypeStruct((1, D_MODEL), F32),
            jax.ShapeDtypeStruct((t, D_FF), BF16),
            jax.ShapeDtypeStruct((t, D_FF), BF16),
            jax.ShapeDtypeStruct((t, D_MODEL), BF16),
        ],
        scratch_shapes=[pltpu.VMEM((tm, D_MODEL), BF16), pltpu.VMEM((tm, D_MODEL), F32), pltpu.VMEM((tm, tf), BF16),
                        pltpu.VMEM((tm, tf), BF16)],
        compiler_params=_params("arbitrary", "arbitrary", vmem_limit_bytes=FFN_BWD_VMEM_LIMIT_BYTES),
    )(dout, x, gain, a, b, wgt, wut, wgt, wut, wd)


def _in_proj_bwd(dproj, w_int, x, gain, dres, *, name, tm=512):
    t, k = dproj.shape

    def body(dp_ref, w_ref, x_ref, g_ref, dr_ref, dx_ref, dg_ref):
        i = pl.program_id(0)
        dhv = _dot(dp_ref[...], w_ref[...])
        xhat, rstd = _rms(x_ref[...])
        part = jnp.sum(dhv * xhat, axis=0, keepdims=True)

        @pl.when(i == 0)
        def _():
            dg_ref[...] = part

        @pl.when(i > 0)
        def _():
            dg_ref[...] += part

        dxh = dhv * g_ref[...]
        dx_ref[...] = dr_ref[...] + rstd * (dxh - xhat * jnp.mean(dxh * xhat, axis=-1, keepdims=True))

    row = pl.BlockSpec((tm, D_MODEL), lambda i: (i, 0))
    vec = pl.BlockSpec((1, D_MODEL), lambda i: (0, 0))
    return pl.pallas_call(
        body,
        name=name,
        grid=(t // tm,),
        in_specs=[pl.BlockSpec((tm, k), lambda i: (i, 0)), pl.BlockSpec((k, D_MODEL), lambda i: (0, 0)), row, vec, row],
        out_specs=[row, vec],
        out_shape=[jax.ShapeDtypeStruct((t, D_MODEL), F32), jax.ShapeDtypeStruct((1, D_MODEL), F32)],
        compiler_params=_params("arbitrary"),
    )(dproj, w_int, x, gain, dres)


ATT_Q_TILE = 512
ATT_K_BLOCK = 256


def _first_head_lanes():
    return lax.broadcasted_iota(jnp.int32, (1, LANES), 1) < SB_HEAD_DIM


def _stack_heads(x):
    first = _first_head_lanes()
    return jnp.concatenate([jnp.where(first, x, 0.0), jnp.where(first, 0.0, x)], axis=0)


def _unstack_heads(x, rows):
    return jnp.where(_first_head_lanes(), x[:rows], x[rows:])


def _rows_from(x, first, rows):
    return x if first == 0 else jnp.concatenate([x[first:rows], x[rows + first:]], axis=0)


def _rows_into(full, part, first, rows):
    if first == 0:
        return part
    n = rows - first
    return jnp.concatenate([full[:first], part[:n], full[rows:rows + first], part[n:]], axis=0)


def _tri(n, relation):
    r = lax.broadcasted_iota(jnp.int32, (n, n), 0)
    c = lax.broadcasted_iota(jnp.int32, (n, n), 1)
    return relation(r, c).astype(BF16)


def _scan_dot(x, tri):
    hi = x.astype(BF16)
    lo = (x - hi.astype(F32)).astype(BF16)
    return _dot(jnp.concatenate([hi, lo], axis=1), jnp.concatenate([tri, tri], axis=0))


def _log_terms(z):
    lbeta = jnp.minimum(z, 0.0) - jnp.log(1.0 + jnp.exp(-jnp.abs(z)))
    return lbeta, lbeta - z


def _attn_fwd(proj, *, name):
    t = proj.shape[0]
    tq, tk = ATT_Q_TILE, ATT_K_BLOCK
    diag = tq // tk
    n_pairs = SB_WIDTH // LANES

    def body(q_ref, k_ref, v_ref, o_ref, kept_ref):
        qi = pl.program_id(1)
        q = q_ref[...] * (SB_HEAD_DIM ** -0.5)
        qs = _stack_heads(q).astype(BF16)
        tri = _tri(tk, lambda j, s: j > s)
        trow = lax.broadcasted_iota(jnp.int32, (tq, tk), 0)
        scol = lax.broadcasted_iota(jnp.int32, (tq, tk), 1)

        def block(kb, carry, causal, first=0):
            acc, c = carry
            off = pl.multiple_of(kb * tk, tk)
            lbeta, lrest = _log_terms(_dot_nt(_rows_from(qs, first, tq), k_ref[pl.ds(off, tk), :].astype(BF16)))
            if causal is not None:
                lrest = jnp.where(causal, lrest, 0.0)
            w = jnp.exp(lbeta + (_scan_dot(lrest, tri) + _rows_from(c, first, tq)))
            if causal is not None:
                w = jnp.where(causal, w, 0.0)
            wb = w.astype(BF16)
            kept_ref[0, 0, kb] = _rows_into(jnp.zeros((2 * tq, tk), BF16), wb, first, tq)
            acc = _rows_into(acc, _rows_from(acc, first, tq) + _dot(wb, v_ref[pl.ds(off, tk), :].astype(BF16)), first, tq)
            return acc, _rows_into(c, _rows_from(c, first, tq) + jnp.sum(lrest, axis=1, keepdims=True), first, tq)

        carry = (jnp.zeros((2 * tq, LANES), F32), jnp.zeros((2 * tq, 1), F32))
        n_full = qi * diag
        for j in reversed(range(diag)):
            mask = ((scol + j * tk) < trow)[j * tk:]
            carry = block(n_full + j, carry, jnp.concatenate([mask, mask], axis=0), first=j * tk)

        def step(it, carry):
            return block(n_full - 1 - it, carry, None)

        acc, _ = lax.fori_loop(0, n_full, step, carry)
        o_ref[...] = _unstack_heads(acc, tq)

    return pl.pallas_call(
        body,
        name=name,
        grid=(n_pairs, t // tq),
        in_specs=[
            pl.BlockSpec((tq, LANES), lambda p, i: (i, p)),
            pl.BlockSpec((t, LANES), lambda p, i: (0, n_pairs + p)),
            pl.BlockSpec((t, LANES), lambda p, i: (0, 2 * n_pairs + p)),
        ],
        out_specs=[pl.BlockSpec((tq, LANES), lambda p, i: (i, p)),
                   pl.BlockSpec((1, 1, t // tk, 2 * tq, tk), lambda p, i: (p, i, 0, 0, 0))],
        out_shape=[jax.ShapeDtypeStruct((t, SB_WIDTH), F32),
                   jax.ShapeDtypeStruct((n_pairs, t // tq, t // tk, 2 * tq, tk), BF16)],
        compiler_params=_params("parallel", "parallel"),
    )(proj, proj, proj)


def _attn_bwd(proj, kept, do, *, name, tie=None):
    t = proj.shape[0]
    tq, tk = ATT_Q_TILE, ATT_K_BLOCK
    diag = tq // tk
    n_pairs = SB_WIDTH // LANES
    scale = SB_HEAD_DIM ** -0.5

    def body(q_ref, k_ref, v_ref, kept_ref, do_ref, *rest):
        dq_ref, dk_ref, dv_ref = rest[-3:]
        qi = pl.program_id(1)

        @pl.when(qi == 0)
        def _():
            dk_ref[...] = jnp.zeros_like(dk_ref)
            dv_ref[...] = jnp.zeros_like(dv_ref)

        qs = _stack_heads(q_ref[...] * scale).astype(BF16)
        dos = _stack_heads(do_ref[...]).astype(BF16)
        before = _tri(tk, lambda s, j: s < j)
        trow = lax.broadcasted_iota(jnp.int32, (tq, tk), 0)
        scol = lax.broadcasted_iota(jnp.int32, (tq, tk), 1)

        def block(kb, carry, causal, first=0):
            dq, cg = carry
            off = pl.multiple_of(kb * tk, tk)
            q_rows, do_rows = _rows_from(qs, first, tq), _rows_from(dos, first, tq)
            wb = _rows_from(kept_ref[0, 0, kb], first, tq)
            kblk = k_ref[pl.ds(off, tk), :].astype(BF16)
            sig = _sigmoid(_dot_nt(q_rows, kblk))
            g = wb.astype(F32) * _dot_nt(do_rows, v_ref[pl.ds(off, tk), :].astype(BF16))
            prior = _scan_dot(g, before) + _rows_from(cg, first, tq)
            dz = g - sig * (g + prior)
            if causal is not None:
                dz = jnp.where(causal, dz, 0.0)
            dzb = dz.astype(BF16)
            dq = _rows_into(dq, _rows_from(dq, first, tq) + _dot(dzb, kblk), first, tq)
            dk_ref[pl.ds(off, tk), :] += _dot_tn(dzb, q_rows)
            dv_ref[pl.ds(off, tk), :] += _dot_tn(wb, do_rows)
            return dq, _rows_into(cg, _rows_from(cg, first, tq) + jnp.sum(g, axis=1, keepdims=True), first, tq)

        n_full = qi * diag
        carry = lax.fori_loop(0, n_full, lambda kb, carry: block(kb, carry, None),
                              (jnp.zeros((2 * tq, LANES), F32), jnp.zeros((2 * tq, 1), F32)))
        for j in range(diag):
            mask = ((scol + j * tk) < trow)[j * tk:]
            carry = block(n_full + j, carry, jnp.concatenate([mask, mask], axis=0), first=j * tk)
        dq_ref[...] = (_unstack_heads(carry[0], tq) * scale).astype(BF16)

    tile_spec = pl.BlockSpec((tq, LANES), lambda p, i: (i, p))
    full_spec = pl.BlockSpec((t, LANES), lambda p, i: (0, p))
    return pl.pallas_call(
        body,
        name=name,
        grid=(n_pairs, t // tq),
        in_specs=[
            tile_spec,
            pl.BlockSpec((t, LANES), lambda p, i: (0, n_pairs + p)),
            pl.BlockSpec((t, LANES), lambda p, i: (0, 2 * n_pairs + p)),
            pl.BlockSpec((1, 1, t // tk, 2 * tq, tk), lambda p, i: (p, i, 0, 0, 0)),
            tile_spec,
        ] + ([] if tie is None else [pl.BlockSpec(memory_space=pl.ANY)]),
        out_specs=[tile_spec, full_spec, full_spec],
        out_shape=[jax.ShapeDtypeStruct((t, SB_WIDTH), BF16)] + [jax.ShapeDtypeStruct((t, SB_WIDTH), F32)] * 2,
        compiler_params=_params("arbitrary", "arbitrary"),
    )(proj, proj, proj, kept, do, *([] if tie is None else [tie]))


HG_BLOCK = 128
HG_HEADS = HG_WIDTH // HG_HEAD_DIM


def _chunk_mats(n):
    r = lax.broadcasted_iota(jnp.int32, (n, n), 0)
    c = lax.broadcasted_iota(jnp.int32, (n, n), 1)
    same = (r // HG_CHUNK) == (c // HG_CHUNK)
    upto = (same & (c <= r)).astype(BF16)
    whole = same.astype(BF16)
    onward = (same & (c >= r)).astype(BF16)
    return upto, whole, onward


def _rows_dot(mat, x):
    return _dot(jnp.concatenate([mat, mat, mat], axis=1), jnp.concatenate(_split3(x), axis=0))


def _split_heads(x):
    return jnp.stack([x[:, h * HG_HEAD_DIM:(h + 1) * HG_HEAD_DIM] for h in range(HG_HEADS)], axis=0)


def _merge_heads(x):
    return jnp.concatenate([x[h] for h in range(HG_HEADS)], axis=1)


def _lower_bound(lg_ref):
    lg = lg_ref[...]
    return _sigmoid(lg[0:1, :] - lg[1:2, :])


def _hgrn_prepare(q_ref, f_ref, lb, h, upto, whole):
    cols = slice(h * HG_HEAD_DIM, (h + 1) * HG_HEAD_DIM)
    lbh = lb[:, cols]
    sg = _sigmoid(f_ref[:, cols])
    forget = lbh + (1.0 - lbh) * sg
    logf = jnp.log(forget)
    kk = (1.0 - lbh) * (1.0 - sg)
    qv = q_ref[:, cols]
    qsig = _sigmoid(qv)
    qh = qv * qsig
    b = _rows_dot(upto, logf)
    blast = _rows_dot(whole, logf)
    return dict(lbh=lbh, sg=sg, forget=forget, kk=kk, qv=qv, qsig=qsig, qh=qh, b=b, eb=jnp.exp(b),
                ekb=jnp.exp(blast - b), dl=jnp.exp(blast))


def _hgrn_fwd(proj, logits, *, name):
    t = proj.shape[0]
    tb = HG_BLOCK
    nc = tb // HG_CHUNK
    hd = HG_HEAD_DIM

    def body(q_ref, f_ref, i_ref, lg_ref, o_ref, st_ref, state, qh_s, kk_s, b_s, qe_s, ke_s, dl_s):
        @pl.when(pl.program_id(0) == 0)
        def _():
            state[...] = jnp.zeros_like(state)

        lb = _lower_bound(lg_ref)
        upto, whole, _ = _chunk_mats(tb)
        for h in range(HG_HEADS):
            p = _hgrn_prepare(q_ref, f_ref, lb, h, upto, whole)
            qh_s[h] = p["qh"]
            kk_s[h] = p["kk"]
            b_s[h] = p["b"]
            qe_s[h] = (p["qh"] * p["eb"]).astype(BF16)
            ke_s[h] = (p["kk"] * p["ekb"]).astype(BF16)
            dl_s[h] = p["dl"]
        rowi = lax.broadcasted_iota(jnp.int32, (HG_HEADS, HG_CHUNK, hd), 1)

        def chunk(c, _):
            r0 = pl.multiple_of(c * HG_CHUNK, HG_CHUNK)
            rows = pl.ds(r0, HG_CHUNK)
            bc = b_s[:, rows, :]
            qc = qh_s[:, rows, :]
            kc = kk_s[:, rows, :]
            vc = _split_heads(i_ref[rows, :])
            s_in = state[...]
            st_ref[c] = s_in
            s_in_b = s_in.astype(BF16)
            qe = qe_s[:, rows, :]
            o = jnp.stack([_dot_nt(qe[h], s_in_b[h]) for h in range(HG_HEADS)], axis=0)
            for s in range(HG_CHUNK):
                pair = jnp.where(rowi >= s, qc * jnp.exp(bc - bc[:, s:s + 1, :]) * kc[:, s:s + 1, :], 0.0)
                o = o + jnp.sum(pair, axis=2, keepdims=True) * vc[:, s:s + 1, :]
            o_ref[rows, :] = _merge_heads(o)
            vcb = vc.astype(BF16)
            ke = ke_s[:, rows, :]
            update = jnp.stack([_dot_tn(vcb[h], ke[h]) for h in range(HG_HEADS)], axis=0)
            state[...] = s_in * dl_s[:, pl.ds(r0, 1), :] + update
            return 0

        lax.fori_loop(0, nc, chunk, 0)

    blk = lambda col: pl.BlockSpec((tb, HG_WIDTH), lambda i: (i, col))
    head_f32 = pltpu.VMEM((HG_HEADS, tb, hd), F32)
    head_bf16 = pltpu.VMEM((HG_HEADS, tb, hd), BF16)
    return pl.pallas_call(
        body,
        name=name,
        grid=(t // tb,),
        in_specs=[blk(3), blk(4), blk(5), pl.BlockSpec((2, HG_WIDTH), lambda i: (0, 0))],
        out_specs=[
            pl.BlockSpec((tb, HG_WIDTH), lambda i: (i, 0)),
            pl.BlockSpec((nc, HG_HEADS, hd, hd), lambda i: (i, 0, 0, 0)),
        ],
        out_shape=[
            jax.ShapeDtypeStruct((t, HG_WIDTH), F32),
            jax.ShapeDtypeStruct((t // HG_CHUNK, HG_HEADS, hd, hd), F32),
        ],
        scratch_shapes=[pltpu.VMEM((HG_HEADS, hd, hd), F32), head_f32, head_f32, head_f32, head_bf16, head_bf16,
                        head_f32],
        compiler_params=_params("arbitrary"),
    )(proj, proj, proj, logits)


def _hgrn_bwd(proj, logits, states, do, *, name):
    t = proj.shape[0]
    tb = HG_BLOCK
    nb = t // tb
    nc = tb // HG_CHUNK
    hd = HG_HEAD_DIM

    def body(q_ref, f_ref, i_ref, lg_ref, st_ref, do_ref, dq_ref, df_ref, di_ref, dlb_ref,
             dstate, qh_s, kk_s, b_s, eb_s, ekb_s, qe_s, ke_s, dl_s, dqh_s, dkk_s, dlf_s):
        step = pl.program_id(0)

        @pl.when(step == 0)
        def _():
            dstate[...] = jnp.zeros_like(dstate)
            dlb_ref[...] = jnp.zeros_like(dlb_ref)

        lb = _lower_bound(lg_ref)
        upto, whole, _ = _chunk_mats(tb)
        prepared = []
        for h in range(HG_HEADS):
            p = _hgrn_prepare(q_ref, f_ref, lb, h, upto, whole)
            prepared.append(p)
            qh_s[h] = p["qh"]
            kk_s[h] = p["kk"]
            b_s[h] = p["b"]
            eb_s[h] = p["eb"]
            ekb_s[h] = p["ekb"]
            qe_s[h] = (p["qh"] * p["eb"]).astype(BF16)
            ke_s[h] = (p["kk"] * p["ekb"]).astype(BF16)
            dl_s[h] = p["dl"]
        rowi = lax.broadcasted_iota(jnp.int32, (HG_CHUNK, hd), 0)
        r16 = lax.broadcasted_iota(jnp.int32, (HG_CHUNK, HG_CHUNK), 0)
        c16 = lax.broadcasted_iota(jnp.int32, (HG_CHUNK, HG_CHUNK), 1)
        onward = (c16 >= r16).astype(BF16)

        def chunk(it, _):
            c = nc - 1 - it
            r0 = pl.multiple_of(c * HG_CHUNK, HG_CHUNK)
            rows = pl.ds(r0, HG_CHUNK)
            for h in range(HG_HEADS):
                cols = slice(h * hd, (h + 1) * hd)
                bc = b_s[h, rows, :]
                qc = qh_s[h, rows, :]
                kc = kk_s[h, rows, :]
                vc = i_ref[rows, cols]
                doc = do_ref[rows, cols]
                s_in = st_ref[c, h]
                ds_out = dstate[h]
                ds_out_b = ds_out.astype(BF16)
                docb = doc.astype(BF16)
                dl_row = dl_s[h, pl.ds(r0, 1), :]
                dqh = _dot(docb, s_in.astype(BF16)) * eb_s[h, rows, :]
                dkk = _dot(vc.astype(BF16), ds_out_b) * ekb_s[h, rows, :]
                dv = _dot_nt(ke_s[h, rows, :], ds_out_b)
                db = dqh * qc - dkk * kc
                dwhole = jnp.sum(dkk * kc, axis=0, keepdims=True) + jnp.sum(ds_out * s_in, axis=0, keepdims=True) * dl_row
                dk_rows, dv_rows = [], []
                for s in range(HG_CHUNK):
                    keep = rowi >= s
                    e = jnp.exp(bc - bc[s:s + 1, :])
                    k_row = kc[s:s + 1, :]
                    pcol = jnp.sum(jnp.where(keep, qc * e * k_row, 0.0), axis=1, keepdims=True)
                    dpcol = jnp.sum(doc * vc[s:s + 1, :], axis=1, keepdims=True)
                    m = jnp.where(keep, e * dpcol, 0.0)
                    y = m * qc
                    dqh = dqh + m * k_row
                    db = db + y * k_row
                    dk_rows.append(jnp.sum(y, axis=0, keepdims=True))
                    dv_rows.append(jnp.sum(pcol * doc, axis=0, keepdims=True))
                dkk_pairs = jnp.concatenate(dk_rows, axis=0)
                dkk = dkk + dkk_pairs
                db = db - dkk_pairs * kc
                dv = dv + jnp.concatenate(dv_rows, axis=0)
                dqh_s[h, rows, :] = dqh
                dkk_s[h, rows, :] = dkk
                dlf_s[h, rows, :] = _rows_dot(onward, db) + dwhole
                di_ref[rows, cols] = dv.astype(BF16)
                dstate[h] = ds_out * dl_row + _dot_tn(docb, qe_s[h, rows, :])
            return 0

        lax.fori_loop(0, nc, chunk, 0)
        for h in range(HG_HEADS):
            cols = slice(h * hd, (h + 1) * hd)
            p = prepared[h]
            dq_ref[:, cols] = (dqh_s[h] * (p["qsig"] * (1.0 + p["qv"] * (1.0 - p["qsig"])))).astype(BF16)
            dforget = dlf_s[h] / p["forget"] - dkk_s[h]
            df_ref[:, cols] = (dforget * (1.0 - p["lbh"]) * p["sg"] * (1.0 - p["sg"])).astype(BF16)
            dlb_ref[:, cols] += jnp.sum(dforget * (1.0 - p["sg"]), axis=0, keepdims=True)

    blk = lambda col: pl.BlockSpec((tb, HG_WIDTH), lambda i: (nb - 1 - i, col))
    vec = pl.BlockSpec((1, HG_WIDTH), lambda i: (0, 0))
    head_f32 = pltpu.VMEM((HG_HEADS, tb, hd), F32)
    head_bf16 = pltpu.VMEM((HG_HEADS, tb, hd), BF16)
    return pl.pallas_call(
        body,
        name=name,
        grid=(nb,),
        in_specs=[
            blk(3), blk(4), blk(5),
            pl.BlockSpec((2, HG_WIDTH), lambda i: (0, 0)),
            pl.BlockSpec((nc, HG_HEADS, hd, hd), lambda i: (nb - 1 - i, 0, 0, 0)),
            blk(0),
        ],
        out_specs=[blk(0), blk(0), blk(0), vec],
        out_shape=[jax.ShapeDtypeStruct((t, HG_WIDTH), BF16)] * 3 + [jax.ShapeDtypeStruct((1, HG_WIDTH), F32)],
        scratch_shapes=[
            pltpu.VMEM((HG_HEADS, hd, hd), F32),
            head_f32, head_f32, head_f32, head_f32, head_f32, head_bf16, head_bf16, head_f32,
            head_f32, head_f32, head_f32,
        ],
        compiler_params=_params("arbitrary"),
    )(proj, proj, proj, logits, states, do)


def _group_mat(width, head_dim):
    r = lax.broadcasted_iota(jnp.int32, (width, width), 0)
    c = lax.broadcasted_iota(jnp.int32, (width, width), 1)
    return ((r // head_dim) == (c // head_dim)).astype(BF16)


def _head_mean(x, mat, head_dim):
    hi = x.astype(BF16)
    lo = (x - hi.astype(F32)).astype(BF16)
    return (_dot(hi, mat) + _dot(lo, mat)) * (1.0 / head_dim)


def _mix_out_fwd(o_sb, o_hg, proj, g_sb, g_hg, w_out, x1, *, name, tm=512):
    t = x1.shape[0]

    def body(osb_ref, ohg_ref, gate_ref, gsb_ref, ghg_ref, w_ref, x_ref, xo_ref, mt_ref):
        msb = _group_mat(SB_WIDTH, SB_HEAD_DIM)
        mhg = _group_mat(HG_WIDTH, HG_HEAD_DIM)
        osb = osb_ref[...]
        ohg = ohg_ref[...]
        nsb = osb * lax.rsqrt(_head_mean(osb * osb, msb, SB_HEAD_DIM) + EPS) * gsb_ref[...]
        gate = gate_ref[...]
        nhg = ohg * lax.rsqrt(_head_mean(ohg * ohg, mhg, HG_HEAD_DIM) + EPS) * ghg_ref[...] * (gate * _sigmoid(gate))
        mixed = jnp.concatenate([nsb, nhg], axis=1).astype(BF16)
        mt_ref[...] = mixed
        xo_ref[...] = x_ref[...] + _dot(mixed, w_ref[...])

    half = pl.BlockSpec((tm, SB_WIDTH), lambda i: (i, 0))
    vec = pl.BlockSpec((1, SB_WIDTH), lambda i: (0, 0))
    row = pl.BlockSpec((tm, D_MODEL), lambda i: (i, 0))
    return pl.pallas_call(
        body,
        name=name,
        grid=(t // tm,),
        in_specs=[half, half, pl.BlockSpec((tm, HG_WIDTH), lambda i: (i, 6)), vec, vec,
                  pl.BlockSpec((D_MODEL, D_MODEL), lambda i: (0, 0)), row],
        out_specs=[row, row],
        out_shape=[jax.ShapeDtypeStruct((t, D_MODEL), F32), jax.ShapeDtypeStruct((t, D_MODEL), BF16)],
        compiler_params=_params("parallel"),
    )(o_sb, o_hg, proj, g_sb, g_hg, w_out, x1)


def _mix_out_bwd(dx2, o_sb, o_hg, proj, g_sb, g_hg, w_out, *, name, tm=512):
    t = dx2.shape[0]

    def body(dx_ref, osb_ref, ohg_ref, gate_ref, gsb_ref, ghg_ref, w_ref, dosb_ref, dohg_ref, dgate_ref, dgsb_ref,
             dghg_ref, dxb_ref):
        i = pl.program_id(0)
        msb = _group_mat(SB_WIDTH, SB_HEAD_DIM)
        mhg = _group_mat(HG_WIDTH, HG_HEAD_DIM)
        dxb = dx_ref[...].astype(BF16)
        dxb_ref[...] = dxb
        dmixed = _dot_nt(dxb, w_ref[...])
        dnsb = dmixed[:, :SB_WIDTH]
        dy = dmixed[:, SB_WIDTH:]

        osb = osb_ref[...]
        rstd = lax.rsqrt(_head_mean(osb * osb, msb, SB_HEAD_DIM) + EPS)
        ohat = osb * rstd
        part_sb = jnp.sum(dnsb * ohat, axis=0, keepdims=True)
        dohat = dnsb * gsb_ref[...]
        dosb_ref[...] = rstd * (dohat - ohat * _head_mean(dohat * ohat, msb, SB_HEAD_DIM))

        ohg = ohg_ref[...]
        rstd = lax.rsqrt(_head_mean(ohg * ohg, mhg, HG_HEAD_DIM) + EPS)
        ohat = ohg * rstd
        gate = gate_ref[...]
        sig = _sigmoid(gate)
        dn = dy * (gate * sig)
        dgate_ref[...] = (dy * (ohat * ghg_ref[...]) * (sig * (1.0 + gate * (1.0 - sig)))).astype(BF16)
        part_hg = jnp.sum(dn * ohat, axis=0, keepdims=True)
        dohat = dn * ghg_ref[...]
        dohg_ref[...] = rstd * (dohat - ohat * _head_mean(dohat * ohat, mhg, HG_HEAD_DIM))

        @pl.when(i == 0)
        def _():
            dgsb_ref[...] = part_sb
            dghg_ref[...] = part_hg

        @pl.when(i > 0)
        def _():
            dgsb_ref[...] += part_sb
            dghg_ref[...] += part_hg

    half = pl.BlockSpec((tm, SB_WIDTH), lambda i: (i, 0))
    vec = pl.BlockSpec((1, SB_WIDTH), lambda i: (0, 0))
    row = pl.BlockSpec((tm, D_MODEL), lambda i: (i, 0))
    return pl.pallas_call(
        body,
        name=name,
        grid=(t // tm,),
        in_specs=[row, half, half, pl.BlockSpec((tm, HG_WIDTH), lambda i: (i, 6)), vec, vec,
                  pl.BlockSpec((D_MODEL, D_MODEL), lambda i: (0, 0))],
        out_specs=[half, half, half, vec, vec, row],
        out_shape=[jax.ShapeDtypeStruct((t, SB_WIDTH), F32)] * 2 + [jax.ShapeDtypeStruct((t, SB_WIDTH), BF16)]
        + [jax.ShapeDtypeStruct((1, SB_WIDTH), F32)] * 2 + [jax.ShapeDtypeStruct((t, D_MODEL), BF16)],
        compiler_params=_params("arbitrary"),
    )(dx2, o_sb, o_hg, proj, g_sb, g_hg, w_out)


def _local_step(x, target, norms, logits, w, weights_after=None, grads_ready=None):
    w = dict(w)
    x1, a1, b1, h1, s1, hm = _ffn_fwd(x, norms["ffn1"], w["g1t"], w["u1t"], w["d1"], name="ffn1_fwd",
                                      next_gain=norms["mix"])
    if weights_after is not None:
        w.update(weights_after("ffn1", x1))
    proj = _mm(hm, w["int"], name="in_proj", tm=512, tn=IN_COLS, nt=True)
    o_sb, sb_kept = _attn_fwd(proj, name="sb_attn_fwd")
    o_hg, states = _hgrn_fwd(proj, logits, name="hgrn2_fwd")
    x2, mixed = _mix_out_fwd(o_sb, o_hg, proj, norms["sb"], norms["hg"], w["out"], x1, name="mix_out_fwd")
    if weights_after is not None:
        w.update(weights_after("mix", x2))
    dx3, a2, b2, h2, s2, d_final, loss_row = _ffn_fwd(x2, norms["ffn2"], w["g2t"], w["u2t"], w["d2"], name="ffn2_fwd",
                                                      head=(norms["final"], target))

    def weight_grad(lhs, rhs, name, tie=None):
        return _mm(lhs, rhs, name=name, tm=256, tn=D_MODEL, ta=True, out_dtype=BF16, tie=tie)

    def sent(stage):
        return grads_ready(stage, gw) if grads_ready is not None else None

    gw, gv = {}, {"final": d_final}
    dx2, gv["ffn2"], da2, db2, dob2 = _ffn_bwd(dx3, x2, norms["ffn2"], a2, b2, w["g2t"], w["u2t"], w["d2"],
                                               name="ffn2_bwd")
    gw["g2t"] = weight_grad(da2, h2, "ffn2_dgate")
    gw["u2t"] = weight_grad(db2, h2, "ffn2_dup")
    gw["d2"] = weight_grad(s2, dob2, "ffn2_ddown")

    do_sb, do_hg, d_gate, gv["sb"], gv["hg"], dx2b = _mix_out_bwd(
        dx2, o_sb, o_hg, proj, norms["sb"], norms["hg"], w["out"], name="mix_out_bwd")
    gw["out"] = weight_grad(mixed, dx2b, "out_dw")
    tie = sent("mix")
    dq_sb, dk_sb, dv_sb = _attn_bwd(proj, sb_kept, do_sb, name="sb_attn_bwd", tie=tie)
    dq_hg, df_hg, di_hg, d_lb = _hgrn_bwd(proj, logits if tie is None else logits + tie[0, 0], states, do_hg,
                                          name="hgrn2_bwd")
    dproj = jnp.concatenate([dq_sb, dk_sb.astype(BF16), dv_sb.astype(BF16), dq_hg, df_hg, di_hg, d_gate], axis=1)
    gw["int"] = weight_grad(dproj, hm, "in_dw")
    tie = sent("in")
    dx1, gv["mix"] = _in_proj_bwd(dproj, w["int"], x1, norms["mix"] if tie is None else norms["mix"] + tie[0, 0], dx2,
                                  name="in_dx")

    dx, gv["ffn1"], da1, db1, dob1 = _ffn_bwd(dx1, x, norms["ffn1"], a1, b1, w["g1t"], w["u1t"], w["d1"],
                                              name="ffn1_bwd")
    gw["g1t"] = weight_grad(da1, h1, "ffn1_dgate")
    gw["u1t"] = weight_grad(db1, h1, "ffn1_dup", tie=sent("g1t"))
    gw["d1"] = weight_grad(s1, dob1, "ffn1_ddown", tie=sent("u1t"))
    sent("d1")
    gv["lb"] = d_lb
    return loss_row, dx, gw, gv


HBM = pl.BlockSpec(memory_space=pl.ANY)


def _place():
    return lax.axis_index("x"), lax.axis_index("y"), lax.axis_index("c")


def _slot(px, py, pc):
    return 4 * px + 2 * py + pc


def _all_gather(blocks, *, name):
    n = len(blocks)

    def body(*refs):
        ins, outs = refs[:n], refs[n:2 * n]
        send_sems, recv_sems, local_sems = refs[2 * n:]
        x, y, c = _place()
        me, sibling = (x, y, c), (x, y, 1 - c)
        chips = [(1 - x, y), (x, 1 - y), (1 - x, 1 - y)]

        def copy(a, k, block, to, src=None):
            dst = outs[a].at[_slot(*block)]
            return pltpu.make_async_remote_copy(
                src_ref=dst if src is None else src, dst_ref=dst, send_sem=send_sems.at[7 * a + k],
                recv_sem=recv_sems.at[7 * a + k], device_id=to, device_id_type=MESH)

        mine = [pltpu.make_async_copy(ins[a], outs[a].at[_slot(*me)], local_sems.at[a]) for a in range(n)]
        for cp in mine:
            cp.start()
        first = []
        for a in range(n):
            first.append(copy(a, 0, me, sibling, src=ins[a]))
            first += [copy(a, 1 + j, me, (*chip, c), src=ins[a]) for j, chip in enumerate(chips)]
        for cp in first:
            cp.start()
        passed = []
        for j, chip in enumerate(chips):
            for a in range(n):
                copy(a, 1 + j, (*chip, c), me).wait_recv()
                fwd = copy(a, 4 + j, (*chip, c), sibling)
                fwd.start()
                passed.append(fwd)
        for a in range(n):
            copy(a, 0, sibling, me).wait_recv()
            for j, chip in enumerate(chips):
                copy(a, 4 + j, (*chip, 1 - c), me).wait_recv()
        for cp in first + passed:
            cp.wait_send()
        for cp in mine:
            cp.wait()

    return pl.pallas_call(
        body,
        name=name,
        in_specs=[HBM] * n,
        out_specs=[HBM] * n,
        out_shape=[jax.ShapeDtypeStruct((N_DEV,) + b.shape, b.dtype) for b in blocks],
        scratch_shapes=[pltpu.SemaphoreType.DMA((7 * n,)), pltpu.SemaphoreType.DMA((7 * n,)),
                        pltpu.SemaphoreType.DMA((n,))],
    )(*blocks)


def _flipped(place, d):
    return tuple(1 - p if (d >> (2 - axis)) & 1 else p for axis, p in enumerate(place))


SEM = pl.BlockSpec(memory_space=pltpu.SEMAPHORE)
EFFECT = pltpu.SideEffectType.DATAFLOW_SIDE_EFFECTING


def _split_copies(me, srcs, lands, send_sems, recv_sems, by_owner):
    copies = []
    for d in range(1, N_DEV):
        peer = _flipped(me, d)
        for a, (src, land) in enumerate(zip(srcs, lands)):
            copies.append(pltpu.make_async_remote_copy(
                src_ref=src.at[_slot(*peer)] if by_owner else src, dst_ref=land.at[_slot(*me)],
                send_sem=send_sems.at[7 * a + d - 1], recv_sem=recv_sems.at[7 * a + d - 1], device_id=peer,
                device_id_type=MESH))
    return copies


def _copies_start(srcs, *, name, by_owner, after=None):
    n = len(srcs)
    extra = [] if after is None else [after]
    land_shapes = [s.shape if by_owner else (N_DEV,) + s.shape for s in srcs]
    lands = [pltpu.with_memory_space_constraint(lax.empty(shape, s.dtype), pltpu.HBM) for shape, s in zip(land_shapes, srcs)]
    srcs = [pltpu.with_memory_space_constraint(s, pltpu.HBM) for s in srcs]

    def body(*refs):
        src_refs, land_refs = refs[:n], refs[n:2 * n]
        send_sems, recv_sems = refs[2 * n + len(extra)], refs[2 * n + len(extra) + 1]
        token = refs[-1]
        for cp in _split_copies(_place(), src_refs, land_refs, send_sems, recv_sems, by_owner):
            cp.start()
        token[...] = jnp.zeros_like(token)

    out = pl.pallas_call(
        body,
        name=name,
        in_specs=[HBM] * (2 * n + len(extra)),
        out_specs=[SEM, SEM] + [HBM] * (2 * n) + [pl.BlockSpec(memory_space=pltpu.VMEM)],
        out_shape=[pltpu.SemaphoreType.DMA((7 * n,)), pltpu.SemaphoreType.DMA((7 * n,))]
        + [pltpu.HBM(s.shape, s.dtype) for s in srcs] + [pltpu.HBM(shape, s.dtype) for shape, s in zip(land_shapes, srcs)]
        + [jax.ShapeDtypeStruct((8, LANES), F32)],
        input_output_aliases={i: 2 + i for i in range(2 * n)},
        compiler_params=pltpu.CompilerParams(has_side_effects=EFFECT),
    )(*srcs, *lands, *extra)
    return (out[0], out[1], out[2:2 + n], out[2 + n:2 + 2 * n]), out[-1]


def _copies_wait(started, after, *, name, by_owner):
    send_sems, recv_sems, srcs, lands = started
    n = len(srcs)

    def body(*refs):
        src_refs, land_refs = refs[:n], refs[n:2 * n]
        for cp in _split_copies(_place(), src_refs, land_refs, refs[2 * n], refs[2 * n + 1], by_owner):
            cp.wait_send()
            cp.wait_recv()

    out = pl.pallas_call(
        body,
        name=name,
        in_specs=[HBM] * (2 * n) + [SEM, SEM, HBM],
        out_specs=[HBM] * (2 * n),
        out_shape=[pltpu.HBM(s.shape, s.dtype) for s in srcs] + [pltpu.HBM(s.shape, s.dtype) for s in lands],
        input_output_aliases={i: i for i in range(2 * n)},
        compiler_params=pltpu.CompilerParams(has_side_effects=EFFECT),
    )(*srcs, *lands, send_sems, recv_sems, after)
    return out[:n], out[n:]


def _with_own(lands, own, slot):
    zero = jnp.zeros((), jnp.int32)
    return [lax.dynamic_update_slice(land, o[None], (slot.astype(jnp.int32),) + (zero,) * o.ndim)
            for land, o in zip(lands, own)]


def _adamw(w, g, m, v):
    m = ADAM_B1 * m + (1.0 - ADAM_B1) * g
    v = ADAM_B2 * v + (1.0 - ADAM_B2) * (g * g)
    m_hat = m / (1.0 - ADAM_B1 ** ADAM_STEP)
    v_hat = v / (1.0 - ADAM_B2 ** ADAM_STEP)
    delta = -ADAM_LR * (m_hat / (jnp.sqrt(v_hat) + ADAM_EPS) + ADAM_WD * w)
    return delta, m, v


def _sum_and_update(parts, w, m, v, *, name, tie=None):
    _, rows, cols = w.shape
    tr = rows // 2

    def body(p_ref, w_ref, m_ref, v_ref, *rest):
        g_ref, d_ref, mo_ref, vo_ref = rest[-4:]
        g = p_ref[0].astype(F32)
        for s in range(1, N_DEV):
            g = g + p_ref[s].astype(F32)
        g_ref[0] = g
        d_ref[0], mo_ref[0], vo_ref[0] = _adamw(w_ref[0], g, m_ref[0], v_ref[0])

    flat = pl.BlockSpec((1, tr, cols), lambda i: (0, i, 0))
    return pl.pallas_call(
        body,
        name=name,
        grid=(rows // tr,),
        in_specs=[pl.BlockSpec((N_DEV, tr, cols), lambda i: (0, i, 0)), flat, flat, flat]
        + ([] if tie is None else [pl.BlockSpec(memory_space=pl.ANY)]),
        out_specs=[flat] * 4,
        out_shape=[jax.ShapeDtypeStruct((1, rows, cols), F32)] * 4,
        compiler_params=_params("parallel"),
    )(parts, w, m, v, *([] if tie is None else [tie]))


VEC_ROWS = 8
ROW_LOGITS, ROW_LOSS = 5, 7


def _vectors_update(part, w, m, v, *, name, tie):
    def body(p_ref, w_ref, m_ref, v_ref, tie_ref, g_ref, d_ref, mo_ref, vo_ref, loss_ref, all_ref, send_sems, recv_sems):
        me = _place()
        all_ref[_slot(*me)] = p_ref[...]
        copies = []
        for d in range(1, N_DEV):
            peer = _flipped(me, d)
            copies.append(pltpu.make_async_remote_copy(
                src_ref=p_ref, dst_ref=all_ref.at[_slot(*me)], send_sem=send_sems.at[d - 1], recv_sem=recv_sems.at[d - 1],
                device_id=peer, device_id_type=MESH))
        for cp in copies:
            cp.start()
        for cp in copies:
            cp.wait()
        total = all_ref[0]
        for s in range(1, N_DEV):
            total = total + all_ref[s]
        wv = w_ref[...]
        half = D_MODEL // 2
        lb = _sigmoid(wv[ROW_LOGITS:ROW_LOGITS + 1, :half] - wv[ROW_LOGITS:ROW_LOGITS + 1, half:])
        d_first = total[ROW_LOGITS:ROW_LOGITS + 1, :half] * lb * (1.0 - lb)
        d_logits = jnp.concatenate([d_first, -d_first], axis=1)
        rowi = lax.broadcasted_iota(jnp.int32, (VEC_ROWS, D_MODEL), 0)
        g = jnp.where(rowi == ROW_LOGITS, d_logits, jnp.where(rowi < ROW_LOGITS, total, 0.0))
        g_ref[...] = g
        d_ref[...], mo_ref[...], vo_ref[...] = _adamw(wv, g, m_ref[...], v_ref[...])
        loss_ref[...] = total[ROW_LOSS:ROW_LOSS + 1, :]

    vmem = pl.BlockSpec(memory_space=pltpu.VMEM)
    return pl.pallas_call(
        body,
        name=name,
        in_specs=[vmem] * 4 + [HBM],
        out_specs=[vmem] * 5,
        out_shape=[jax.ShapeDtypeStruct((VEC_ROWS, D_MODEL), F32)] * 4 + [jax.ShapeDtypeStruct((1, D_MODEL), F32)],
        scratch_shapes=[pltpu.VMEM((N_DEV, VEC_ROWS, D_MODEL), F32), pltpu.SemaphoreType.DMA((7,)),
                        pltpu.SemaphoreType.DMA((7,))],
    )(part, w, m, v, tie)


TRANSPOSED = ("g1t", "u1t", "g2t", "u2t", "int")


def _vector_rows(rows):
    rowi = lax.broadcasted_iota(jnp.int32, (VEC_ROWS, D_MODEL), 0)
    out = jnp.zeros((VEC_ROWS, D_MODEL), F32)
    for i, r in enumerate(rows):
        if r is not None:
            out = jnp.where(rowi == i, r, out)
    return out


def kernel(x, ffn1_norm, ffn1_w_gate, ffn1_w_up, ffn1_w_down, mix_norm, w_in, sb_out_norm, hg_lower_bound_logits, hg_out_norm, w_out, ffn2_norm, ffn2_w_gate, ffn2_w_up, ffn2_w_down, final_norm, loss_target, m_ffn1_norm, m_ffn1_w_gate, m_ffn1_w_up, m_ffn1_w_down, m_mix_norm, m_w_in, m_sb_out_norm, m_hg_lower_bound_logits, m_hg_out_norm, m_w_out, m_ffn2_norm, m_ffn2_w_gate, m_ffn2_w_up, m_ffn2_w_down, m_final_norm, v_ffn1_norm, v_ffn1_w_gate, v_ffn1_w_up, v_ffn1_w_down, v_mix_norm, v_w_in, v_sb_out_norm, v_hg_lower_bound_logits, v_hg_out_norm, v_w_out, v_ffn2_norm, v_ffn2_w_gate, v_ffn2_w_up, v_ffn2_w_down, v_final_norm):
    def matrices(g1, u1, d1, win, wout, g2, u2, d2):
        return {"g1t": g1, "u1t": u1, "d1": d1, "int": win, "out": wout, "g2t": g2, "u2t": u2, "d2": d2}

    def vectors(n1, nm, nsb, lg, nhg, n2, nf):
        return [n1, nm, n2, nf.reshape(1, D_MODEL), jnp.concatenate([nsb, nhg], axis=1), lg.reshape(1, D_MODEL), None, None]

    w_sh = matrices(ffn1_w_gate, ffn1_w_up, ffn1_w_down, w_in, w_out, ffn2_w_gate, ffn2_w_up, ffn2_w_down)
    m_sh = matrices(m_ffn1_w_gate, m_ffn1_w_up, m_ffn1_w_down, m_w_in, m_w_out, m_ffn2_w_gate, m_ffn2_w_up, m_ffn2_w_down)
    v_sh = matrices(v_ffn1_w_gate, v_ffn1_w_up, v_ffn1_w_down, v_w_in, v_w_out, v_ffn2_w_gate, v_ffn2_w_up, v_ffn2_w_down)
    keys = list(w_sh)

    slot = _slot(*_place())

    def full(key, stack):
        return stack.reshape(-1, D_MODEL)

    def by_owner(key, grad):
        return grad.reshape(N_DEV, -1, D_MODEL)

    def view(key, a):
        return jnp.swapaxes(a, 1, 2) if key in TRANSPOSED else a

    blocks = {k: view(k, w_sh[k])[0].astype(BF16) for k in keys}
    first, mid, last = ("g1t", "u1t", "d1"), ("int", "out"), ("g2t", "u2t", "d2")
    w_first = {k: full(k, s) for k, s in zip(first, _all_gather([blocks[k] for k in first], name="gather_ffn1"))}
    flights = {}
    flights["ffn1"], token_mid = _copies_start([blocks[k] for k in mid], name="gather_mid_start", by_owner=False,
                                               after=w_first["d1"])
    flights["mix"], token_last = _copies_start([blocks[k] for k in last], name="gather_ffn2_start", by_owner=False,
                                               after=token_mid)

    def weights_after(stage, result):
        group = mid if stage == "ffn1" else last
        own, lands = _copies_wait(flights[stage], result, name="gather_" + stage + "_wait", by_owner=False)
        return {k: full(k, s) for k, s in zip(group, _with_own(lands, own, slot))}

    groups = {"mix": ("g2t", "u2t", "d2", "out"), "in": ("int",), "g1t": ("g1t",), "u1t": ("u1t",), "d1": ("d1",)}
    sent, sent_tokens = {}, []

    def grads_ready(stage, gw):
        stacks = [by_owner(k, gw[k]) for k in groups[stage]]
        flight, token = _copies_start(stacks, name="grads_" + stage + "_start", by_owner=True)
        sent[stage] = flight
        sent_tokens.append(token)
        return token

    norms = {"ffn1": ffn1_norm + token_last[0, 0], "mix": mix_norm, "sb": sb_out_norm, "hg": hg_out_norm,
             "ffn2": ffn2_norm, "final": final_norm.reshape(1, D_MODEL)}
    loss_row, grad_x, gw, gv = _local_step(x[0], loss_target[0], norms, hg_lower_bound_logits, w_first, weights_after,
                                           grads_ready)

    lb_row = jnp.concatenate([gv["lb"], jnp.zeros_like(gv["lb"])], axis=1)
    part = _vector_rows([gv["ffn1"], gv["mix"], gv["ffn2"], gv["final"], jnp.concatenate([gv["sb"], gv["hg"]], axis=1),
                         lb_row, None, loss_row])
    vec_w = _vector_rows(vectors(ffn1_norm, mix_norm, sb_out_norm, hg_lower_bound_logits, hg_out_norm, ffn2_norm, final_norm))
    vec_m = _vector_rows(vectors(m_ffn1_norm, m_mix_norm, m_sb_out_norm, m_hg_lower_bound_logits, m_hg_out_norm,
                                 m_ffn2_norm, m_final_norm))
    vec_v = _vector_rows(vectors(v_ffn1_norm, v_mix_norm, v_sb_out_norm, v_hg_lower_bound_logits, v_hg_out_norm,
                                 v_ffn2_norm, v_final_norm))
    updated, after = {}, sent_tokens[-1]
    for stage, flight in sent.items():
        if stage == list(sent)[-1]:
            *vecs, loss_out = _vectors_update(part, vec_w, vec_m, vec_v, name="vectors_update", tie=after)
            after = loss_out
        stacks, lands = _copies_wait(flight, after, name="grads_" + stage + "_wait", by_owner=True)
        own = [lax.dynamic_index_in_dim(s, slot, keepdims=False) for s in stacks]
        for k, part_k in zip(groups[stage], _with_own(lands, own, slot)):
            updated[k] = _sum_and_update(part_k, view(k, w_sh[k]), view(k, m_sh[k]), view(k, v_sh[k]), name="adamw_" + k,
                                         tie=after)
            after = updated[k][0]
    mats = [{k: view(k, updated[k][i]) for k in keys} for i in range(4)]

    def leaves(mat, vec):
        half = D_MODEL // 2
        return (
            vec[0:1], mat["g1t"], mat["u1t"], mat["d1"], vec[1:2], mat["int"], vec[4:5, :half],
            vec[ROW_LOGITS].reshape(2, half), vec[4:5, half:], mat["out"], vec[2:3], mat["g2t"], mat["u2t"],
            mat["d2"], vec[3],
        )

    out = [loss_out[0, 0], grad_x[None]]
    for mat, vec in zip(mats, vecs):
        out.extend(leaves(mat, vec))
    return tuple(out)
```

```python
import jax
import jax.numpy as jnp
from jax import lax
from jax.experimental import pallas as pl
from jax.experimental.pallas import tpu as pltpu

F32, BF16 = jnp.float32, jnp.bfloat16
D_MODEL = 1024
D_FF = 2816
SB_WIDTH = 512
HG_WIDTH = 512
SB_HEAD_DIM = 64
HG_HEAD_DIM = 128
IN_COLS = 3584
EPS = 1e-6
N_DEV = 8
LANES = 128
HG_CHUNK = 16
VMEM_LIMIT_BYTES = 48 * 1024 * 1024
FFN_BWD_VMEM_LIMIT_BYTES = 56 * 1024 * 1024
ADAM_LR, ADAM_B1, ADAM_B2, ADAM_EPS, ADAM_WD, ADAM_STEP = 0.001, 0.9, 0.999, 1e-08, 0.01, 10
MESH = pl.DeviceIdType.MESH


def _params(*semantics, vmem_limit_bytes=VMEM_LIMIT_BYTES):
    return pltpu.CompilerParams(dimension_semantics=semantics, vmem_limit_bytes=vmem_limit_bytes)


def _dot(a, b):
    return jnp.dot(a, b, preferred_element_type=F32)


def _dot_nt(a, b):
    return lax.dot_general(a, b, (((1,), (1,)), ((), ())), preferred_element_type=F32)


def _dot_tn(a, b):
    return lax.dot_general(a, b, (((0,), (0,)), ((), ())), preferred_element_type=F32)


def _split3(x):
    hi = x.astype(BF16)
    r1 = x - hi.astype(F32)
    mid = r1.astype(BF16)
    lo = (r1 - mid.astype(F32)).astype(BF16)
    return hi, mid, lo


def _rms(xv):
    rstd = lax.rsqrt(jnp.mean(xv * xv, axis=-1, keepdims=True) + EPS)
    return xv * rstd, rstd


def _sigmoid(x):
    return 0.5 + 0.5 * jnp.tanh(0.5 * x)


def _loss_terms(xv, gain, target):
    xhat, rstd = _rms(xv)
    err = xhat * gain - target
    loss = 0.5 * jnp.sum(jnp.mean(err * err, axis=-1, keepdims=True), axis=0, keepdims=True)
    dy = err * (1.0 / xv.shape[-1])
    dxh = dy * gain
    dx = rstd * (dxh - xhat * jnp.mean(dxh * xhat, axis=-1, keepdims=True))
    return dx, jnp.sum(dy * xhat, axis=0, keepdims=True), loss


def _mm(a, b, *, name, tm, tn, nt=False, ta=False, out_dtype=F32, tie=None):
    k, m = a.shape if ta else a.shape[::-1]
    n = b.shape[0] if nt else b.shape[1]
    assert m % tm == 0 and n % tn == 0 and not (nt and ta), (name, a.shape, b.shape, tm, tn)

    def body(a_ref, b_ref, *rest):
        av = a_ref[...].astype(BF16)
        bv = b_ref[...].astype(BF16)
        rest[-1][...] = (_dot_nt(av, bv) if nt else _dot_tn(av, bv) if ta else _dot(av, bv)).astype(out_dtype)

    in_specs = [
        pl.BlockSpec((k, tm), lambda i, j: (0, i)) if ta else pl.BlockSpec((tm, k), lambda i, j: (i, 0)),
        pl.BlockSpec((tn, k), lambda i, j: (j, 0)) if nt else pl.BlockSpec((k, tn), lambda i, j: (0, j)),
    ]
    operands = [a, b]
    if tie is not None:
        in_specs.append(pl.BlockSpec(memory_space=pl.ANY))
        operands.append(tie)
    return pl.pallas_call(
        body,
        name=name,
        grid=(m // tm, n // tn),
        in_specs=in_specs,
        out_specs=pl.BlockSpec((tm, tn), lambda i, j: (i, j)),
        out_shape=jax.ShapeDtypeStruct((m, n), out_dtype),
        compiler_params=_params("parallel", "parallel"),
    )(*operands)


def _ffn_fwd(x, gain, wgt, wut, wd, *, name, next_gain=None, head=None, tm=1024, tf=256):
    t = x.shape[0]
    nj = D_FF // tf
    extra_in = [] if next_gain is None else [next_gain]
    extra_in += [] if head is None else list(head)

    def body(x_ref, g_ref, wg_ref, wu_ref, wd_prev_ref, wd_last_ref, *rest):
        extra, (xo_ref, a_ref, b_ref, h_ref, st_ref) = rest[:len(extra_in)], rest[len(extra_in):len(extra_in) + 5]
        tail_out, (acc, s_prev) = rest[len(extra_in) + 5:-2], rest[-2:]
        i = pl.program_id(0)
        j = pl.program_id(1)

        @pl.when(j == 0)
        def _():
            xhat, _ = _rms(x_ref[...])
            h_ref[...] = (xhat * g_ref[...]).astype(BF16)
            acc[...] = jnp.zeros_like(acc)
            s_prev[...] = jnp.zeros_like(s_prev)

        acc[...] += _dot(s_prev[...], wd_prev_ref[...])
        h = h_ref[...]
        a = _dot_nt(h, wg_ref[...])
        b = _dot_nt(h, wu_ref[...])
        a_ref[...] = a.astype(BF16)
        b_ref[...] = b.astype(BF16)
        s = (a * _sigmoid(a) * b).astype(BF16)
        st_ref[...] = s
        s_prev[...] = s

        @pl.when(j == nj - 1)
        def _():
            xo = x_ref[...] + 0.5 * (acc[...] + _dot(s, wd_last_ref[...]))
            if head is None:
                xo_ref[...] = xo
            if next_gain is not None:
                tail_out[0][...] = (_rms(xo)[0] * extra[0][...]).astype(BF16)
            if head is not None:
                gain_ref, target_ref = extra[-2:]
                dg_ref, loss_ref = tail_out[-2:]
                xo_ref[...], part_g, part_loss = _loss_terms(xo, gain_ref[...], target_ref[...])

                @pl.when(i == 0)
                def _():
                    dg_ref[...] = part_g
                    loss_ref[...] = jnp.broadcast_to(part_loss, loss_ref.shape)

                @pl.when(i > 0)
                def _():
                    dg_ref[...] += part_g
                    loss_ref[...] += jnp.broadcast_to(part_loss, loss_ref.shape)

    row = pl.BlockSpec((tm, D_MODEL), lambda i, j: (i, 0))
    vec = pl.BlockSpec((1, D_MODEL), lambda i, j: (0, 0))
    tile = pl.BlockSpec((tm, tf), lambda i, j: (i, j))
    weights = pl.BlockSpec((tf, D_MODEL), lambda i, j: (j, 0))
    tail_specs = ([] if next_gain is None else [row]) + ([] if head is None else [vec, vec])
    tail_shapes = ([] if next_gain is None else [jax.ShapeDtypeStruct((t, D_MODEL), BF16)]) + (
        [] if head is None else [jax.ShapeDtypeStruct((1, D_MODEL), F32)] * 2)
    return pl.pallas_call(
        body,
        name=name,
        grid=(t // tm, nj),
        in_specs=[
            row, vec, weights, weights,
            pl.BlockSpec((tf, D_MODEL), lambda i, j: (jnp.maximum(j - 1, 0), 0)),
            pl.BlockSpec((tf, D_MODEL), lambda i, j: (nj - 1, 0)),
        ] + ([] if next_gain is None else [vec]) + ([] if head is None else [vec, row]),
        out_specs=[row, tile, tile, row, tile] + tail_specs,
        out_shape=[
            jax.ShapeDtypeStruct((t, D_MODEL), F32),
            jax.ShapeDtypeStruct((t, D_FF), BF16),
            jax.ShapeDtypeStruct((t, D_FF), BF16),
            jax.ShapeDtypeStruct((t, D_MODEL), BF16),
            jax.ShapeDtypeStruct((t, D_FF), BF16),
        ] + tail_shapes,
        scratch_shapes=[pltpu.VMEM((tm, D_MODEL), F32), pltpu.VMEM((tm, tf), BF16)],
        compiler_params=_params("arbitrary", "arbitrary"),
    )(x, gain, wgt, wut, wd, wd, *extra_in)


def _ffn_bwd(dout, x, gain, a, b, wgt, wut, wd, *, name, tm=1024, tf=256):
    t = x.shape[0]
    nj = D_FF // tf

    def body(do_ref, x_ref, g_ref, a_ref, b_ref, wg_prev_ref, wu_prev_ref, wg_last_ref, wu_last_ref, wd_ref,
             dx_ref, dg_ref, da_ref, db_ref, dob_ref, dob_scr, dh, da_prev, db_prev):
        i = pl.program_id(0)
        j = pl.program_id(1)

        @pl.when(j == 0)
        def _():
            d = (0.5 * do_ref[...]).astype(BF16)
            dob_scr[...] = d
            dob_ref[...] = d
            dh[...] = jnp.zeros_like(dh)
            da_prev[...] = jnp.zeros_like(da_prev)
            db_prev[...] = jnp.zeros_like(db_prev)

        dh[...] += _dot(da_prev[...], wg_prev_ref[...]) + _dot(db_prev[...], wu_prev_ref[...])
        ds = _dot_nt(dob_scr[...], wd_ref[...])
        av = a_ref[...].astype(F32)
        bv = b_ref[...].astype(F32)
        sig = _sigmoid(av)
        dbv = (ds * (av * sig)).astype(BF16)
        dav = (ds * bv * (sig * (1.0 + av * (1.0 - sig)))).astype(BF16)
        da_ref[...] = dav
        db_ref[...] = dbv
        da_prev[...] = dav
        db_prev[...] = dbv

        @pl.when(j == nj - 1)
        def _():
            xhat, rstd = _rms(x_ref[...])
            dhv = dh[...] + _dot(dav, wg_last_ref[...]) + _dot(dbv, wu_last_ref[...])
            part = jnp.sum(dhv * xhat, axis=0, keepdims=True)

            @pl.when(i == 0)
            def _():
                dg_ref[...] = part

            @pl.when(i > 0)
            def _():
                dg_ref[...] += part

            dxh = dhv * g_ref[...]
            dx_ref[...] = do_ref[...] + rstd * (dxh - xhat * jnp.mean(dxh * xhat, axis=-1, keepdims=True))

    return pl.pallas_call(
        body,
        name=name,
        grid=(t // tm, nj),
        in_specs=[
            pl.BlockSpec((tm, D_MODEL), lambda i, j: (i, 0)),
            pl.BlockSpec((tm, D_MODEL), lambda i, j: (i, 0)),
            pl.BlockSpec((1, D_MODEL), lambda i, j: (0, 0)),
            pl.BlockSpec((tm, tf), lambda i, j: (i, j)),
            pl.BlockSpec((tm, tf), lambda i, j: (i, j)),
            pl.BlockSpec((tf, D_MODEL), lambda i, j: (jnp.maximum(j - 1, 0), 0)),
            pl.BlockSpec((tf, D_MODEL), lambda i, j: (jnp.maximum(j - 1, 0), 0)),
            pl.BlockSpec((tf, D_MODEL), lambda i, j: (nj - 1, 0)),
            pl.BlockSpec((tf, D_MODEL), lambda i, j: (nj - 1, 0)),
            pl.BlockSpec((tf, D_MODEL), lambda i, j: (j, 0)),
        ],
        out_specs=[
            pl.BlockSpec((tm, D_MODEL), lambda i, j: (i, 0)),
            pl.BlockSpec((1, D_MODEL), lambda i, j: (0, 0)),
            pl.BlockSpec((tm, tf), lambda i, j: (i, j)),
            pl.BlockSpec((tm, tf), lambda i, j: (i, j)),
            pl.BlockSpec((tm, D_MODEL), lambda i, j: (i, 0)),
        ],
        out_shape=[
            jax.ShapeDtypeStruct((t, D_MODEL), F32),
            jax.ShapeDtypeStruct((1, D_MODEL), F32),
            jax.ShapeDtypeStruct((t, D_FF), BF16),
            jax.ShapeDtypeStruct((t, D_FF), BF16),
            jax.ShapeDtypeStruct((t, D_MODEL), BF16),
        ],
        scratch_shapes=[pltpu.VMEM((tm, D_MODEL), BF16), pltpu.VMEM((tm, D_MODEL), F32), pltpu.VMEM((tm, tf), BF16),
                        pltpu.VMEM((tm, tf), BF16)],
        compiler_params=_params("arbitrary", "arbitrary", vmem_limit_bytes=FFN_BWD_VMEM_LIMIT_BYTES),
    )(dout, x, gain, a, b, wgt, wut, wgt, wut, wd)


def _in_proj_bwd(dproj, w_int, x, gain, dres, *, name, tm=512):
    t, k = dproj.shape

    def body(dp_ref, w_ref, x_ref, g_ref, dr_ref, dx_ref, dg_ref):
        i = pl.program_id(0)
        dhv = _dot(dp_ref[...], w_ref[...])
        xhat, rstd = _rms(x_ref[...])
        part = jnp.sum(dhv * xhat, axis=0, keepdims=True)

        @pl.when(i == 0)
        def _():
            dg_ref[...] = part

        @pl.when(i > 0)
        def _():
            dg_ref[...] += part

        dxh = dhv * g_ref[...]
        dx_ref[...] = dr_ref[...] + rstd * (dxh - xhat * jnp.mean(dxh * xhat, axis=-1, keepdims=True))

    row = pl.BlockSpec((tm, D_MODEL), lambda i: (i, 0))
    vec = pl.BlockSpec((1, D_MODEL), lambda i: (0, 0))
    return pl.pallas_call(
        body,
        name=name,
        grid=(t // tm,),
        in_specs=[pl.BlockSpec((tm, k), lambda i: (i, 0)), pl.BlockSpec((k, D_MODEL), lambda i: (0, 0)), row, vec, row],
        out_specs=[row, vec],
        out_shape=[jax.ShapeDtypeStruct((t, D_MODEL), F32), jax.ShapeDtypeStruct((1, D_MODEL), F32)],
        compiler_params=_params("arbitrary"),
    )(dproj, w_int, x, gain, dres)


ATT_Q_TILE = 512
ATT_K_BLOCK = 256


def _first_head_lanes():
    return lax.broadcasted_iota(jnp.int32, (1, LANES), 1) < SB_HEAD_DIM


def _stack_heads(x):
    first = _first_head_lanes()
    return jnp.concatenate([jnp.where(first, x, 0.0), jnp.where(first, 0.0, x)], axis=0)


def _unstack_heads(x, rows):
    return jnp.where(_first_head_lanes(), x[:rows], x[rows:])


def _rows_from(x, first, rows):
    return x if first == 0 else jnp.concatenate([x[first:rows], x[rows + first:]], axis=0)


def _rows_into(full, part, first, rows):
    if first == 0:
        return part
    n = rows - first
    return jnp.concatenate([full[:first], part[:n], full[rows:rows + first], part[n:]], axis=0)


def _tri(n, relation):
    r = lax.broadcasted_iota(jnp.int32, (n, n), 0)
    c = lax.broadcasted_iota(jnp.int32, (n, n), 1)
    return relation(r, c).astype(BF16)


def _scan_dot(x, tri):
    hi = x.astype(BF16)
    lo = (x - hi.astype(F32)).astype(BF16)
    return _dot(jnp.concatenate([hi, lo], axis=1), jnp.concatenate([tri, tri], axis=0))


def _log_terms(z):
    lbeta = jnp.minimum(z, 0.0) - jnp.log(1.0 + jnp.exp(-jnp.abs(z)))
    return lbeta, lbeta - z


def _attn_fwd(proj, *, name):
    t = proj.shape[0]
    tq, tk = ATT_Q_TILE, ATT_K_BLOCK
    diag = tq // tk
    n_pairs = SB_WIDTH // LANES

    def body(q_ref, k_ref, v_ref, o_ref, kept_ref):
        qi = pl.program_id(1)
        q = q_ref[...] * (SB_HEAD_DIM ** -0.5)
        qs = _stack_heads(q).astype(BF16)
        tri = _tri(tk, lambda j, s: j > s)
        trow = lax.broadcasted_iota(jnp.int32, (tq, tk), 0)
        scol = lax.broadcasted_iota(jnp.int32, (tq, tk), 1)

        def block(kb, carry, causal, first=0):
            acc, c = carry
            off = pl.multiple_of(kb * tk, tk)
            lbeta, lrest = _log_terms(_dot_nt(_rows_from(qs, first, tq), k_ref[pl.ds(off, tk), :].astype(BF16)))
            if causal is not None:
                lrest = jnp.where(causal, lrest, 0.0)
            w = jnp.exp(lbeta + (_scan_dot(lrest, tri) + _rows_from(c, first, tq)))
            if causal is not None:
                w = jnp.where(causal, w, 0.0)
            wb = w.astype(BF16)
            kept_ref[0, 0, kb] = _rows_into(jnp.zeros((2 * tq, tk), BF16), wb, first, tq)
            acc = _rows_into(acc, _rows_from(acc, first, tq) + _dot(wb, v_ref[pl.ds(off, tk), :].astype(BF16)), first, tq)
            return acc, _rows_into(c, _rows_from(c, first, tq) + jnp.sum(lrest, axis=1, keepdims=True), first, tq)

        carry = (jnp.zeros((2 * tq, LANES), F32), jnp.zeros((2 * tq, 1), F32))
        n_full = qi * diag
        for j in reversed(range(diag)):
            mask = ((scol + j * tk) < trow)[j * tk:]
            carry = block(n_full + j, carry, jnp.concatenate([mask, mask], axis=0), first=j * tk)

        def step(it, carry):
            for j in range(diag):
                carry = block(n_full - 1 - (diag * it + j), carry, None)
            return carry

        acc, _ = lax.fori_loop(0, qi, step, carry)
        o_ref[...] = _unstack_heads(acc, tq)

    return pl.pallas_call(
        body,
        name=name,
        grid=(n_pairs, t // tq),
        in_specs=[
            pl.BlockSpec((tq, LANES), lambda p, i: (i, p)),
            pl.BlockSpec((t, LANES), lambda p, i: (0, n_pairs + p)),
            pl.BlockSpec((t, LANES), lambda p, i: (0, 2 * n_pairs + p)),
        ],
        out_specs=[pl.BlockSpec((tq, LANES), lambda p, i: (i, p)),
                   pl.BlockSpec((1, 1, t // tk, 2 * tq, tk), lambda p, i: (p, i, 0, 0, 0))],
        out_shape=[jax.ShapeDtypeStruct((t, SB_WIDTH), F32),
                   jax.ShapeDtypeStruct((n_pairs, t // tq, t // tk, 2 * tq, tk), BF16)],
        compiler_params=_params("parallel", "parallel"),
    )(proj, proj, proj)


def _attn_bwd(proj, kept, do, *, name, tie=None):
    t = proj.shape[0]
    tq, tk = ATT_Q_TILE, ATT_K_BLOCK
    diag = tq // tk
    n_pairs = SB_WIDTH // LANES
    scale = SB_HEAD_DIM ** -0.5

    def body(q_ref, k_ref, v_ref, kept_ref, do_ref, *rest):
        dq_ref, dk_ref, dv_ref = rest[-3:]
        qi = pl.program_id(1)

        @pl.when(qi == 0)
        def _():
            dk_ref[...] = jnp.zeros_like(dk_ref)
            dv_ref[...] = jnp.zeros_like(dv_ref)

        qs = _stack_heads(q_ref[...] * scale).astype(BF16)
        dos = _stack_heads(do_ref[...]).astype(BF16)
        before = _tri(tk, lambda s, j: s < j)
        trow = lax.broadcasted_iota(jnp.int32, (tq, tk), 0)
        scol = lax.broadcasted_iota(jnp.int32, (tq, tk), 1)

        def block(kb, carry, causal, first=0):
            dq, cg = carry
            off = pl.multiple_of(kb * tk, tk)
            q_rows, do_rows = _rows_from(qs, first, tq), _rows_from(dos, first, tq)
            wb = _rows_from(kept_ref[0, 0, kb], first, tq)
            kblk = k_ref[pl.ds(off, tk), :].astype(BF16)
            sig = _sigmoid(_dot_nt(q_rows, kblk))
            g = wb.astype(F32) * _dot_nt(do_rows, v_ref[pl.ds(off, tk), :].astype(BF16))
            prior = _scan_dot(g, before) + _rows_from(cg, first, tq)
            dz = g - sig * (g + prior)
            if causal is not None:
                dz = jnp.where(causal, dz, 0.0)
            dzb = dz.astype(BF16)
            dq = _rows_into(dq, _rows_from(dq, first, tq) + _dot(dzb, kblk), first, tq)
            dk_ref[pl.ds(off, tk), :] += _dot_tn(dzb, q_rows)
            dv_ref[pl.ds(off, tk), :] += _dot_tn(wb, do_rows)
            return dq, _rows_into(cg, _rows_from(cg, first, tq) + jnp.sum(g, axis=1, keepdims=True), first, tq)

        n_full = qi * diag

        def step(it, carry):
            for j in range(diag):
                carry = block(diag * it + j, carry, None)
            return carry

        carry = lax.fori_loop(0, qi, step, (jnp.zeros((2 * tq, LANES), F32), jnp.zeros((2 * tq, 1), F32)))
        for j in range(diag):
            mask = ((scol + j * tk) < trow)[j * tk:]
            carry = block(n_full + j, carry, jnp.concatenate([mask, mask], axis=0), first=j * tk)
        dq_ref[...] = (_unstack_heads(carry[0], tq) * scale).astype(BF16)

    tile_spec = pl.BlockSpec((tq, LANES), lambda p, i: (i, p))
    full_spec = pl.BlockSpec((t, LANES), lambda p, i: (0, p))
    return pl.pallas_call(
        body,
        name=name,
        grid=(n_pairs, t // tq),
        in_specs=[
            tile_spec,
            pl.BlockSpec((t, LANES), lambda p, i: (0, n_pairs + p)),
            pl.BlockSpec((t, LANES), lambda p, i: (0, 2 * n_pairs + p)),
            pl.BlockSpec((1, 1, t // tk, 2 * tq, tk), lambda p, i: (p, i, 0, 0, 0)),
            tile_spec,
        ] + ([] if tie is None else [pl.BlockSpec(memory_space=pl.ANY)]),
        out_specs=[tile_spec, full_spec, full_spec],
        out_shape=[jax.ShapeDtypeStruct((t, SB_WIDTH), BF16)] + [jax.ShapeDtypeStruct((t, SB_WIDTH), F32)] * 2,
        compiler_params=_params("arbitrary", "arbitrary"),
    )(proj, proj, proj, kept, do, *([] if tie is None else [tie]))


HG_BLOCK = 128
HG_HEADS = HG_WIDTH // HG_HEAD_DIM


def _chunk_mats(n):
    r = lax.broadcasted_iota(jnp.int32, (n, n), 0)
    c = lax.broadcasted_iota(jnp.int32, (n, n), 1)
    same = (r // HG_CHUNK) == (c // HG_CHUNK)
    upto = (same & (c <= r)).astype(BF16)
    whole = same.astype(BF16)
    onward = (same & (c >= r)).astype(BF16)
    return upto, whole, onward


def _rows_dot(mat, x):
    return _dot(jnp.concatenate([mat, mat, mat], axis=1), jnp.concatenate(_split3(x), axis=0))


def _split_heads(x):
    return jnp.stack([x[:, h * HG_HEAD_DIM:(h + 1) * HG_HEAD_DIM] for h in range(HG_HEADS)], axis=0)


def _merge_heads(x):
    return jnp.concatenate([x[h] for h in range(HG_HEADS)], axis=1)


def _lower_bound(lg_ref):
    lg = lg_ref[...]
    return _sigmoid(lg[0:1, :] - lg[1:2, :])


def _hgrn_prepare(q_ref, f_ref, lb, h, upto, whole):
    cols = slice(h * HG_HEAD_DIM, (h + 1) * HG_HEAD_DIM)
    lbh = lb[:, cols]
    sg = _sigmoid(f_ref[:, cols])
    forget = lbh + (1.0 - lbh) * sg
    logf = jnp.log(forget)
    kk = (1.0 - lbh) * (1.0 - sg)
    qv = q_ref[:, cols]
    qsig = _sigmoid(qv)
    qh = qv * qsig
    b = _rows_dot(upto, logf)
    blast = _rows_dot(whole, logf)
    return dict(lbh=lbh, sg=sg, forget=forget, kk=kk, qv=qv, qsig=qsig, qh=qh, b=b, eb=jnp.exp(b),
                ekb=jnp.exp(blast - b), dl=jnp.exp(blast))


def _hgrn_fwd(proj, logits, *, name):
    t = proj.shape[0]
    tb = HG_BLOCK
    nc = tb // HG_CHUNK
    hd = HG_HEAD_DIM

    def body(q_ref, f_ref, i_ref, lg_ref, o_ref, st_ref, state, qh_s, kk_s, b_s, qe_s, ke_s, dl_s):
        @pl.when(pl.program_id(0) == 0)
        def _():
            state[...] = jnp.zeros_like(state)

        lb = _lower_bound(lg_ref)
        upto, whole, _ = _chunk_mats(tb)
        for h in range(HG_HEADS):
            p = _hgrn_prepare(q_ref, f_ref, lb, h, upto, whole)
            qh_s[h] = p["qh"]
            kk_s[h] = p["kk"]
            b_s[h] = p["b"]
            qe_s[h] = (p["qh"] * p["eb"]).astype(BF16)
            ke_s[h] = (p["kk"] * p["ekb"]).astype(BF16)
            dl_s[h] = p["dl"]
        rowi = lax.broadcasted_iota(jnp.int32, (HG_HEADS, HG_CHUNK, hd), 1)

        def chunk(c, _):
            r0 = pl.multiple_of(c * HG_CHUNK, HG_CHUNK)
            rows = pl.ds(r0, HG_CHUNK)
            bc = b_s[:, rows, :]
            qc = qh_s[:, rows, :]
            kc = kk_s[:, rows, :]
            vc = _split_heads(i_ref[rows, :])
            s_in = state[...]
            st_ref[c] = s_in
            s_in_b = s_in.astype(BF16)
            qe = qe_s[:, rows, :]
            o = jnp.stack([_dot_nt(qe[h], s_in_b[h]) for h in range(HG_HEADS)], axis=0)
            for s in range(HG_CHUNK):
                pair = jnp.where(rowi >= s, qc * jnp.exp(bc - bc[:, s:s + 1, :]) * kc[:, s:s + 1, :], 0.0)
                o = o + jnp.sum(pair, axis=2, keepdims=True) * vc[:, s:s + 1, :]
            o_ref[rows, :] = _merge_heads(o)
            vcb = vc.astype(BF16)
            ke = ke_s[:, rows, :]
            update = jnp.stack([_dot_tn(vcb[h], ke[h]) for h in range(HG_HEADS)], axis=0)
            state[...] = s_in * dl_s[:, pl.ds(r0, 1), :] + update
            return 0

        lax.fori_loop(0, nc, chunk, 0, unroll=4)

    blk =lambda col: pl.BlockSpec((tb, HG_WIDTH), lambda i: (i, col))
    head_f32 = pltpu.VMEM((HG_HEADS, tb, hd), F32)
    head_bf16 = pltpu.VMEM((HG_HEADS, tb, hd), BF16)
    return pl.pallas_call(
        body,
        name=name,
        grid=(t // tb,),
        in_specs=[blk(3), blk(4), blk(5), pl.BlockSpec((2, HG_WIDTH), lambda i: (0, 0))],
        out_specs=[
            pl.BlockSpec((tb, HG_WIDTH), lambda i: (i, 0)),
            pl.BlockSpec((nc, HG_HEADS, hd, hd), lambda i: (i, 0, 0, 0)),
        ],
        out_shape=[
            jax.ShapeDtypeStruct((t, HG_WIDTH), F32),
            jax.ShapeDtypeStruct((t // HG_CHUNK, HG_HEADS, hd, hd), F32),
        ],
        scratch_shapes=[pltpu.VMEM((HG_HEADS, hd, hd), F32), head_f32, head_f32, head_f32, head_bf16, head_bf16,
                        head_f32],
        compiler_params=_params("arbitrary"),
    )(proj, proj, proj, logits)


def _hgrn_bwd(proj, logits, states, do, *, name):
    t = proj.shape[0]
    tb = HG_BLOCK
    nb = t // tb
    nc = tb // HG_CHUNK
    hd = HG_HEAD_DIM

    def body(q_ref, f_ref, i_ref, lg_ref, st_ref, do_ref, dq_ref, df_ref, di_ref, dlb_ref,
             dstate, qh_s, kk_s, b_s, eb_s, ekb_s, qe_s, ke_s, dl_s, dqh_s, dkk_s, dlf_s):
        step = pl.program_id(0)

        @pl.when(step == 0)
        def _():
            dstate[...] = jnp.zeros_like(dstate)
            dlb_ref[...] = jnp.zeros_like(dlb_ref)

        lb = _lower_bound(lg_ref)
        upto, whole, _ = _chunk_mats(tb)
        prepared = []
        for h in range(HG_HEADS):
            p = _hgrn_prepare(q_ref, f_ref, lb, h, upto, whole)
            prepared.append(p)
            qh_s[h] = p["qh"]
            kk_s[h] = p["kk"]
            b_s[h] = p["b"]
            eb_s[h] = p["eb"]
            ekb_s[h] = p["ekb"]
            qe_s[h] = (p["qh"] * p["eb"]).astype(BF16)
            ke_s[h] = (p["kk"] * p["ekb"]).astype(BF16)
            dl_s[h] = p["dl"]
        rowi = lax.broadcasted_iota(jnp.int32, (HG_CHUNK, hd), 0)
        r16 = lax.broadcasted_iota(jnp.int32, (HG_CHUNK, HG_CHUNK), 0)
        c16 = lax.broadcasted_iota(jnp.int32, (HG_CHUNK, HG_CHUNK), 1)
        onward = (c16 >= r16).astype(BF16)

        def chunk(it, _):
            c = nc - 1 - it
            r0 = pl.multiple_of(c * HG_CHUNK, HG_CHUNK)
            rows = pl.ds(r0, HG_CHUNK)
            for h in range(HG_HEADS):
                cols = slice(h * hd, (h + 1) * hd)
                bc = b_s[h, rows, :]
                qc = qh_s[h, rows, :]
                kc = kk_s[h, rows, :]
                vc = i_ref[rows, cols]
                doc = do_ref[rows, cols]
                s_in = st_ref[c, h]
                ds_out = dstate[h]
                ds_out_b = ds_out.astype(BF16)
                docb = doc.astype(BF16)
                dl_row = dl_s[h, pl.ds(r0, 1), :]
                dqh = _dot(docb, s_in.astype(BF16)) * eb_s[h, rows, :]
                dkk = _dot(vc.astype(BF16), ds_out_b) * ekb_s[h, rows, :]
                dv = _dot_nt(ke_s[h, rows, :], ds_out_b)
                db = dqh * qc - dkk * kc
                dwhole = jnp.sum(dkk * kc, axis=0, keepdims=True) + jnp.sum(ds_out * s_in, axis=0, keepdims=True) * dl_row
                dk_rows, dv_rows = [], []
                for s in range(HG_CHUNK):
                    keep = rowi >= s
                    e = jnp.exp(bc - bc[s:s + 1, :])
                    k_row = kc[s:s + 1, :]
                    pcol = jnp.sum(jnp.where(keep, qc * e * k_row, 0.0), axis=1, keepdims=True)
                    dpcol = jnp.sum(doc * vc[s:s + 1, :], axis=1, keepdims=True)
                    m = jnp.where(keep, e * dpcol, 0.0)
                    y = m * qc
                    dqh = dqh + m * k_row
                    db = db + y * k_row
                    dk_rows.append(jnp.sum(y, axis=0, keepdims=True))
                    dv_rows.append(jnp.sum(pcol * doc, axis=0, keepdims=True))
                dkk_pairs = jnp.concatenate(dk_rows, axis=0)
                dkk = dkk + dkk_pairs
                db = db - dkk_pairs * kc
                dv = dv + jnp.concatenate(dv_rows, axis=0)
                dqh_s[h, rows, :] = dqh
                dkk_s[h, rows, :] = dkk
                dlf_s[h, rows, :] = _rows_dot(onward, db) + dwhole
                di_ref[rows, cols] = dv.astype(BF16)
                dstate[h] = ds_out * dl_row + _dot_tn(docb, qe_s[h, rows, :])
            return 0

        lax.fori_loop(0, nc, chunk, 0, unroll=4)
        for h in range(HG_HEADS):
            cols = slice(h * hd, (h + 1) * hd)
            p = prepared[h]
            dq_ref[:, cols] = (dqh_s[h] * (p["qsig"] * (1.0 + p["qv"] * (1.0 - p["qsig"])))).astype(BF16)
            dforget = dlf_s[h] / p["forget"] - dkk_s[h]
            df_ref[:, cols] = (dforget * (1.0 - p["lbh"]) * p["sg"] * (1.0 - p["sg"])).astype(BF16)
            dlb_ref[:, cols] += jnp.sum(dforget * (1.0 - p["sg"]), axis=0, keepdims=True)

    blk = lambda col: pl.BlockSpec((tb, HG_WIDTH), lambda i: (nb - 1 - i, col))
    vec = pl.BlockSpec((1, HG_WIDTH), lambda i: (0, 0))
    head_f32 = pltpu.VMEM((HG_HEADS, tb, hd), F32)
    head_bf16 = pltpu.VMEM((HG_HEADS, tb, hd), BF16)
    return pl.pallas_call(
        body,
        name=name,
        grid=(nb,),
        in_specs=[
            blk(3), blk(4), blk(5),
            pl.BlockSpec((2, HG_WIDTH), lambda i: (0, 0)),
            pl.BlockSpec((nc, HG_HEADS, hd, hd), lambda i: (nb - 1 - i, 0, 0, 0)),
            blk(0),
        ],
        out_specs=[blk(0), blk(0), blk(0), vec],
        out_shape=[jax.ShapeDtypeStruct((t, HG_WIDTH), BF16)] * 3 + [jax.ShapeDtypeStruct((1, HG_WIDTH), F32)],
        scratch_shapes=[
            pltpu.VMEM((HG_HEADS, hd, hd), F32),
            head_f32, head_f32, head_f32, head_f32, head_f32, head_bf16, head_bf16, head_f32,
            head_f32, head_f32, head_f32,
        ],
        compiler_params=_params("arbitrary"),
    )(proj, proj, proj, logits, states, do)


def _group_mat(width, head_dim):
    r = lax.broadcasted_iota(jnp.int32, (width, width), 0)
    c = lax.broadcasted_iota(jnp.int32, (width, width), 1)
    return ((r // head_dim) == (c // head_dim)).astype(BF16)


def _head_mean(x, mat, head_dim):
    hi = x.astype(BF16)
    lo = (x - hi.astype(F32)).astype(BF16)
    return (_dot(hi, mat) + _dot(lo, mat)) * (1.0 / head_dim)


def _mix_out_fwd(o_sb, o_hg, proj, g_sb, g_hg, w_out, x1, *, name, tm=512):
    t = x1.shape[0]

    def body(osb_ref, ohg_ref, gate_ref, gsb_ref, ghg_ref, w_ref, x_ref, xo_ref, mt_ref):
        msb = _group_mat(SB_WIDTH, SB_HEAD_DIM)
        mhg = _group_mat(HG_WIDTH, HG_HEAD_DIM)
        osb = osb_ref[...]
        ohg = ohg_ref[...]
        nsb = osb * lax.rsqrt(_head_mean(osb * osb, msb, SB_HEAD_DIM) + EPS) * gsb_ref[...]
        gate = gate_ref[...]
        nhg = ohg * lax.rsqrt(_head_mean(ohg * ohg, mhg, HG_HEAD_DIM) + EPS) * ghg_ref[...] * (gate * _sigmoid(gate))
        mixed = jnp.concatenate([nsb, nhg], axis=1).astype(BF16)
        mt_ref[...] = mixed
        xo_ref[...] = x_ref[...] + _dot(mixed, w_ref[...])

    half = pl.BlockSpec((tm, SB_WIDTH), lambda i: (i, 0))
    vec = pl.BlockSpec((1, SB_WIDTH), lambda i: (0, 0))
    row = pl.BlockSpec((tm, D_MODEL), lambda i: (i, 0))
    return pl.pallas_call(
        body,
        name=name,
        grid=(t // tm,),
        in_specs=[half, half, pl.BlockSpec((tm, HG_WIDTH), lambda i: (i, 6)), vec, vec,
                  pl.BlockSpec((D_MODEL, D_MODEL), lambda i: (0, 0)), row],
        out_specs=[row, row],
        out_shape=[jax.ShapeDtypeStruct((t, D_MODEL), F32), jax.ShapeDtypeStruct((t, D_MODEL), BF16)],
        compiler_params=_params("parallel"),
    )(o_sb, o_hg, proj, g_sb, g_hg, w_out, x1)


def _mix_out_bwd(dx2, o_sb, o_hg, proj, g_sb, g_hg, w_out, *, name, tm=512):
    t = dx2.shape[0]

    def body(dx_ref, osb_ref, ohg_ref, gate_ref, gsb_ref, ghg_ref, w_ref, dosb_ref, dohg_ref, dgate_ref, dgsb_ref,
             dghg_ref, dxb_ref):
        i = pl.program_id(0)
        msb = _group_mat(SB_WIDTH, SB_HEAD_DIM)
        mhg = _group_mat(HG_WIDTH, HG_HEAD_DIM)
        dxb = dx_ref[...].astype(BF16)
        dxb_ref[...] = dxb
        dmixed = _dot_nt(dxb, w_ref[...])
        dnsb = dmixed[:, :SB_WIDTH]
        dy = dmixed[:, SB_WIDTH:]

        osb = osb_ref[...]
        rstd = lax.rsqrt(_head_mean(osb * osb, msb, SB_HEAD_DIM) + EPS)
        ohat = osb * rstd
        part_sb = jnp.sum(dnsb * ohat, axis=0, keepdims=True)
        dohat = dnsb * gsb_ref[...]
        dosb_ref[...] = rstd * (dohat - ohat * _head_mean(dohat * ohat, msb, SB_HEAD_DIM))

        ohg = ohg_ref[...]
        rstd = lax.rsqrt(_head_mean(ohg * ohg, mhg, HG_HEAD_DIM) + EPS)
        ohat = ohg * rstd
        gate = gate_ref[...]
        sig = _sigmoid(gate)
        dn = dy * (gate * sig)
        dgate_ref[...] = (dy * (ohat * ghg_ref[...]) * (sig * (1.0 + gate * (1.0 - sig)))).astype(BF16)
        part_hg = jnp.sum(dn * ohat, axis=0, keepdims=True)
        dohat = dn * ghg_ref[...]
        dohg_ref[...] = rstd * (dohat - ohat * _head_mean(dohat * ohat, mhg, HG_HEAD_DIM))

        @pl.when(i == 0)
        def _():
            dgsb_ref[...] = part_sb
            dghg_ref[...] = part_hg

        @pl.when(i > 0)
        def _():
            dgsb_ref[...] += part_sb
            dghg_ref[...] += part_hg

    half = pl.BlockSpec((tm, SB_WIDTH), lambda i: (i, 0))
    vec = pl.BlockSpec((1, SB_WIDTH), lambda i: (0, 0))
    row = pl.BlockSpec((tm, D_MODEL), lambda i: (i, 0))
    return pl.pallas_call(
        body,
        name=name,
        grid=(t // tm,),
        in_specs=[row, half, half, pl.BlockSpec((tm, HG_WIDTH), lambda i: (i, 6)), vec, vec,
                  pl.BlockSpec((D_MODEL, D_MODEL), lambda i: (0, 0))],
        out_specs=[half, half, half, vec, vec, row],
        out_shape=[jax.ShapeDtypeStruct((t, SB_WIDTH), F32)] * 2 + [jax.ShapeDtypeStruct((t, SB_WIDTH), BF16)]
        + [jax.ShapeDtypeStruct((1, SB_WIDTH), F32)] * 2 + [jax.ShapeDtypeStruct((t, D_MODEL), BF16)],
        compiler_params=_params("arbitrary"),
    )(dx2, o_sb, o_hg, proj, g_sb, g_hg, w_out)


def _local_step(x, target, norms, logits, w, weights_after=None, grads_ready=None):
    w = dict(w)
    x1, a1, b1, h1, s1, hm = _ffn_fwd(x, norms["ffn1"], w["g1t"], w["u1t"], w["d1"], name="ffn1_fwd",
                                      next_gain=norms["mix"])
    if weights_after is not None:
        w.update(weights_after("ffn1", x1))
    proj = _mm(hm, w["int"], name="in_proj", tm=512, tn=IN_COLS, nt=True)
    o_sb, sb_kept = _attn_fwd(proj, name="sb_attn_fwd")
    o_hg, states = _hgrn_fwd(proj, logits, name="hgrn2_fwd")
    x2, mixed = _mix_out_fwd(o_sb, o_hg, proj, norms["sb"], norms["hg"], w["out"], x1, name="mix_out_fwd")
    if weights_after is not None:
        w.update(weights_after("mix", x2))
    dx3, a2, b2, h2, s2, d_final, loss_row = _ffn_fwd(x2, norms["ffn2"], w["g2t"], w["u2t"], w["d2"], name="ffn2_fwd",
                                                      head=(norms["final"], target))

    def weight_grad(lhs, rhs, name, tie=None):
        return _mm(lhs, rhs, name=name, tm=256, tn=D_MODEL, ta=True, out_dtype=BF16, tie=tie)

    def sent(stage):
        return grads_ready(stage, gw) if grads_ready is not None else None

    gw, gv = {}, {"final": d_final}
    dx2, gv["ffn2"], da2, db2, dob2 = _ffn_bwd(dx3, x2, norms["ffn2"], a2, b2, w["g2t"], w["u2t"], w["d2"],
                                               name="ffn2_bwd")
    gw["g2t"] = weight_grad(da2, h2, "ffn2_dgate")
    gw["u2t"] = weight_grad(db2, h2, "ffn2_dup")
    gw["d2"] = weight_grad(s2, dob2, "ffn2_ddown")

    do_sb, do_hg, d_gate, gv["sb"], gv["hg"], dx2b = _mix_out_bwd(
        dx2, o_sb, o_hg, proj, norms["sb"], norms["hg"], w["out"], name="mix_out_bwd")
    gw["out"] = weight_grad(mixed, dx2b, "out_dw")
    tie = sent("mix")
    dq_sb, dk_sb, dv_sb = _attn_bwd(proj, sb_kept, do_sb, name="sb_attn_bwd", tie=tie)
    dq_hg, df_hg, di_hg, d_lb = _hgrn_bwd(proj, logits if tie is None else logits + tie[0, 0], states, do_hg,
                                          name="hgrn2_bwd")
    dproj = jnp.concatenate([dq_sb, dk_sb.astype(BF16), dv_sb.astype(BF16), dq_hg, df_hg, di_hg, d_gate], axis=1)
    gw["int"] = weight_grad(dproj, hm, "in_dw")
    tie = sent("in")
    dx1, gv["mix"] = _in_proj_bwd(dproj, w["int"], x1, norms["mix"] if tie is None else norms["mix"] + tie[0, 0], dx2,
                                  name="in_dx")

    dx, gv["ffn1"], da1, db1, dob1 = _ffn_bwd(dx1, x, norms["ffn1"], a1, b1, w["g1t"], w["u1t"], w["d1"],
                                              name="ffn1_bwd")
    gw["g1t"] = weight_grad(da1, h1, "ffn1_dgate")
    gw["u1t"] = weight_grad(db1, h1, "ffn1_dup", tie=sent("g1t"))
    gw["d1"] = weight_grad(s1, dob1, "ffn1_ddown", tie=sent("u1t"))
    sent("d1")
    gv["lb"] = d_lb
    return loss_row, dx, gw, gv


HBM = pl.BlockSpec(memory_space=pl.ANY)


def _place():
    return lax.axis_index("x"), lax.axis_index("y"), lax.axis_index("c")


def _slot(px, py, pc):
    return 4 * px + 2 * py + pc


def _all_gather(blocks, *, name):
    n = len(blocks)

    def body(*refs):
        ins, outs = refs[:n], refs[n:2 * n]
        send_sems, recv_sems, local_sems = refs[2 * n:]
        x, y, c = _place()
        me, sibling = (x, y, c), (x, y, 1 - c)
        chips = [(1 - x, y), (x, 1 - y), (1 - x, 1 - y)]

        def copy(a, k, block, to, src=None):
            dst = outs[a].at[_slot(*block)]
            return pltpu.make_async_remote_copy(
                src_ref=dst if src is None else src, dst_ref=dst, send_sem=send_sems.at[7 * a + k],
                recv_sem=recv_sems.at[7 * a + k], device_id=to, device_id_type=MESH)

        mine = [pltpu.make_async_copy(ins[a], outs[a].at[_slot(*me)], local_sems.at[a]) for a in range(n)]
        for cp in mine:
            cp.start()
        first = []
        for a in range(n):
            first.append(copy(a, 0, me, sibling, src=ins[a]))
            first += [copy(a, 1 + j, me, (*chip, c), src=ins[a]) for j, chip in enumerate(chips)]
        for cp in first:
            cp.start()
        passed = []
        for j, chip in enumerate(chips):
            for a in range(n):
                copy(a, 1 + j, (*chip, c), me).wait_recv()
                fwd = copy(a, 4 + j, (*chip, c), sibling)
                fwd.start()
                passed.append(fwd)
        for a in range(n):
            copy(a, 0, sibling, me).wait_recv()
            for j, chip in enumerate(chips):
                copy(a, 4 + j, (*chip, 1 - c), me).wait_recv()
        for cp in first + passed:
            cp.wait_send()
        for cp in mine:
            cp.wait()

    return pl.pallas_call(
        body,
        name=name,
        in_specs=[HBM] * n,
        out_specs=[HBM] * n,
        out_shape=[jax.ShapeDtypeStruct((N_DEV,) + b.shape, b.dtype) for b in blocks],
        scratch_shapes=[pltpu.SemaphoreType.DMA((7 * n,)), pltpu.SemaphoreType.DMA((7 * n,)),
                        pltpu.SemaphoreType.DMA((n,))],
    )(*blocks)


def _flipped(place, d):
    return tuple(1 - p if (d >> (2 - axis)) & 1 else p for axis, p in enumerate(place))


SEM = pl.BlockSpec(memory_space=pltpu.SEMAPHORE)
EFFECT = pltpu.SideEffectType.DATAFLOW_SIDE_EFFECTING


def _split_copies(me, srcs, lands, send_sems, recv_sems, by_owner):
    copies = []
    for d in range(1, N_DEV):
        peer = _flipped(me, d)
        for a, (src, land) in enumerate(zip(srcs, lands)):
            copies.append(pltpu.make_async_remote_copy(
                src_ref=src.at[_slot(*peer)] if by_owner else src, dst_ref=land.at[_slot(*me)],
                send_sem=send_sems.at[7 * a + d - 1], recv_sem=recv_sems.at[7 * a + d - 1], device_id=peer,
                device_id_type=MESH))
    return copies


def _copies_start(srcs, *, name, by_owner, after=None):
    n = len(srcs)
    extra = [] if after is None else [after]
    land_shapes = [s.shape if by_owner else (N_DEV,) + s.shape for s in srcs]
    lands = [pltpu.with_memory_space_constraint(lax.empty(shape, s.dtype), pltpu.HBM) for shape, s in zip(land_shapes, srcs)]
    srcs = [pltpu.with_memory_space_constraint(s, pltpu.HBM) for s in srcs]

    def body(*refs):
        src_refs, land_refs = refs[:n], refs[n:2 * n]
        send_sems, recv_sems = refs[2 * n + len(extra)], refs[2 * n + len(extra) + 1]
        token = refs[-1]
        for cp in _split_copies(_place(), src_refs, land_refs, send_sems, recv_sems, by_owner):
            cp.start()
        token[...] = jnp.zeros_like(token)

    out = pl.pallas_call(
        body,
        name=name,
        in_specs=[HBM] * (2 * n + len(extra)),
        out_specs=[SEM, SEM] + [HBM] * (2 * n) + [pl.BlockSpec(memory_space=pltpu.VMEM)],
        out_shape=[pltpu.SemaphoreType.DMA((7 * n,)), pltpu.SemaphoreType.DMA((7 * n,))]
        + [pltpu.HBM(s.shape, s.dtype) for s in srcs] + [pltpu.HBM(shape, s.dtype) for shape, s in zip(land_shapes, srcs)]
        + [jax.ShapeDtypeStruct((8, LANES), F32)],
        input_output_aliases={i: 2 + i for i in range(2 * n)},
        compiler_params=pltpu.CompilerParams(has_side_effects=EFFECT),
    )(*srcs, *lands, *extra)
    return (out[0], out[1], out[2:2 + n], out[2 + n:2 + 2 * n]), out[-1]


def _copies_wait(started, after, *, name, by_owner):
    send_sems, recv_sems, srcs, lands = started
    n = len(srcs)

    def body(*refs):
        src_refs, land_refs = refs[:n], refs[n:2 * n]
        for cp in _split_copies(_place(), src_refs, land_refs, refs[2 * n], refs[2 * n + 1], by_owner):
            cp.wait_send()
            cp.wait_recv()

    out = pl.pallas_call(
        body,
        name=name,
        in_specs=[HBM] * (2 * n) + [SEM, SEM, HBM],
        out_specs=[HBM] * (2 * n),
        out_shape=[pltpu.HBM(s.shape, s.dtype) for s in srcs] + [pltpu.HBM(s.shape, s.dtype) for s in lands],
        input_output_aliases={i: i for i in range(2 * n)},
        compiler_params=pltpu.CompilerParams(has_side_effects=EFFECT),
    )(*srcs, *lands, send_sems, recv_sems, after)
    return out[:n], out[n:]


def _with_own(lands, own, slot):
    zero = jnp.zeros((), jnp.int32)
    return [lax.dynamic_update_slice(land, o[None], (slot.astype(jnp.int32),) + (zero,) * o.ndim)
            for land, o in zip(lands, own)]


def _adamw(w, g, m, v):
    m = ADAM_B1 * m + (1.0 - ADAM_B1) * g
    v = ADAM_B2 * v + (1.0 - ADAM_B2) * (g * g)
    m_hat = m / (1.0 - ADAM_B1 ** ADAM_STEP)
    v_hat = v / (1.0 - ADAM_B2 ** ADAM_STEP)
    delta = -ADAM_LR * (m_hat / (jnp.sqrt(v_hat) + ADAM_EPS) + ADAM_WD * w)
    return delta, m, v


def _sum_and_update(parts, w, m, v, *, name, tie=None):
    _, rows, cols = w.shape
    tr = rows // 2

    def body(p_ref, w_ref, m_ref, v_ref, *rest):
        g_ref, d_ref, mo_ref, vo_ref = rest[-4:]
        g = p_ref[0].astype(F32)
        for s in range(1, N_DEV):
            g = g + p_ref[s].astype(F32)
        g_ref[0] = g
        d_ref[0], mo_ref[0], vo_ref[0] = _adamw(w_ref[0], g, m_ref[0], v_ref[0])

    flat = pl.BlockSpec((1, tr, cols), lambda i: (0, i, 0))
    return pl.pallas_call(
        body,
        name=name,
        grid=(rows // tr,),
        in_specs=[pl.BlockSpec((N_DEV, tr, cols), lambda i: (0, i, 0)), flat, flat, flat]
        + ([] if tie is None else [pl.BlockSpec(memory_space=pl.ANY)]),
        out_specs=[flat] * 4,
        out_shape=[jax.ShapeDtypeStruct((1, rows, cols), F32)] * 4,
        compiler_params=_params("parallel"),
    )(parts, w, m, v, *([] if tie is None else [tie]))


VEC_ROWS = 8
ROW_LOGITS, ROW_LOSS = 5, 7


def _vectors_update(part, w, m, v, *, name, tie):
    def body(p_ref, w_ref, m_ref, v_ref, tie_ref, g_ref, d_ref, mo_ref, vo_ref, loss_ref, all_ref, send_sems, recv_sems):
        me = _place()
        all_ref[_slot(*me)] = p_ref[...]
        copies = []
        for d in range(1, N_DEV):
            peer = _flipped(me, d)
            copies.append(pltpu.make_async_remote_copy(
                src_ref=p_ref, dst_ref=all_ref.at[_slot(*me)], send_sem=send_sems.at[d - 1], recv_sem=recv_sems.at[d - 1],
                device_id=peer, device_id_type=MESH))
        for cp in copies:
            cp.start()
        for cp in copies:
            cp.wait()
        total = all_ref[0]
        for s in range(1, N_DEV):
            total = total + all_ref[s]
        wv = w_ref[...]
        half = D_MODEL // 2
        lb = _sigmoid(wv[ROW_LOGITS:ROW_LOGITS + 1, :half] - wv[ROW_LOGITS:ROW_LOGITS + 1, half:])
        d_first = total[ROW_LOGITS:ROW_LOGITS + 1, :half] * lb * (1.0 - lb)
        d_logits = jnp.concatenate([d_first, -d_first], axis=1)
        rowi = lax.broadcasted_iota(jnp.int32, (VEC_ROWS, D_MODEL), 0)
        g = jnp.where(rowi == ROW_LOGITS, d_logits, jnp.where(rowi < ROW_LOGITS, total, 0.0))
        g_ref[...] = g
        d_ref[...], mo_ref[...], vo_ref[...] = _adamw(wv, g, m_ref[...], v_ref[...])
        loss_ref[...] = total[ROW_LOSS:ROW_LOSS + 1, :]

    vmem = pl.BlockSpec(memory_space=pltpu.VMEM)
    return pl.pallas_call(
        body,
        name=name,
        in_specs=[vmem] * 4 + [HBM],
        out_specs=[vmem] * 5,
        out_shape=[jax.ShapeDtypeStruct((VEC_ROWS, D_MODEL), F32)] * 4 + [jax.ShapeDtypeStruct((1, D_MODEL), F32)],
        scratch_shapes=[pltpu.VMEM((N_DEV, VEC_ROWS, D_MODEL), F32), pltpu.SemaphoreType.DMA((7,)),
                        pltpu.SemaphoreType.DMA((7,))],
    )(part, w, m, v, tie)


TRANSPOSED = ("g1t", "u1t", "g2t", "u2t", "int")


def _vector_rows(rows):
    rowi = lax.broadcasted_iota(jnp.int32, (VEC_ROWS, D_MODEL), 0)
    out = jnp.zeros((VEC_ROWS, D_MODEL), F32)
    for i, r in enumerate(rows):
        if r is not None:
            out = jnp.where(rowi == i, r, out)
    return out


def kernel(x, ffn1_norm, ffn1_w_gate, ffn1_w_up, ffn1_w_down, mix_norm, w_in, sb_out_norm, hg_lower_bound_logits, hg_out_norm, w_out, ffn2_norm, ffn2_w_gate, ffn2_w_up, ffn2_w_down, final_norm, loss_target, m_ffn1_norm, m_ffn1_w_gate, m_ffn1_w_up, m_ffn1_w_down, m_mix_norm, m_w_in, m_sb_out_norm, m_hg_lower_bound_logits, m_hg_out_norm, m_w_out, m_ffn2_norm, m_ffn2_w_gate, m_ffn2_w_up, m_ffn2_w_down, m_final_norm, v_ffn1_norm, v_ffn1_w_gate, v_ffn1_w_up, v_ffn1_w_down, v_mix_norm, v_w_in, v_sb_out_norm, v_hg_lower_bound_logits, v_hg_out_norm, v_w_out, v_ffn2_norm, v_ffn2_w_gate, v_ffn2_w_up, v_ffn2_w_down, v_final_norm):
    def matrices(g1, u1, d1, win, wout, g2, u2, d2):
        return {"g1t": g1, "u1t": u1, "d1": d1, "int": win, "out": wout, "g2t": g2, "u2t": u2, "d2": d2}

    def vectors(n1, nm, nsb, lg, nhg, n2, nf):
        return [n1, nm, n2, nf.reshape(1, D_MODEL), jnp.concatenate([nsb, nhg], axis=1), lg.reshape(1, D_MODEL), None, None]

    w_sh = matrices(ffn1_w_gate, ffn1_w_up, ffn1_w_down, w_in, w_out, ffn2_w_gate, ffn2_w_up, ffn2_w_down)
    m_sh = matrices(m_ffn1_w_gate, m_ffn1_w_up, m_ffn1_w_down, m_w_in, m_w_out, m_ffn2_w_gate, m_ffn2_w_up, m_ffn2_w_down)
    v_sh = matrices(v_ffn1_w_gate, v_ffn1_w_up, v_ffn1_w_down, v_w_in, v_w_out, v_ffn2_w_gate, v_ffn2_w_up, v_ffn2_w_down)
    keys = list(w_sh)

    slot = _slot(*_place())

    def full(key, stack):
        return stack.reshape(-1, D_MODEL)

    def by_owner(key, grad):
        return grad.reshape(N_DEV, -1, D_MODEL)

    def view(key, a):
        return jnp.swapaxes(a, 1, 2) if key in TRANSPOSED else a

    blocks = {k: view(k, w_sh[k])[0].astype(BF16) for k in keys}
    first, mid, last = ("g1t", "u1t", "d1"), ("int", "out"), ("g2t", "u2t", "d2")
    w_first = {k: full(k, s) for k, s in zip(first, _all_gather([blocks[k] for k in first], name="gather_ffn1"))}
    flights = {}
    flights["ffn1"], token_mid = _copies_start([blocks[k] for k in mid], name="gather_mid_start", by_owner=False,
                                               after=w_first["d1"])
    flights["mix"], token_last = _copies_start([blocks[k] for k in last], name="gather_ffn2_start", by_owner=False,
                                               after=token_mid)

    def weights_after(stage, result):
        group = mid if stage == "ffn1" else last
        own, lands = _copies_wait(flights[stage], result, name="gather_" + stage + "_wait", by_owner=False)
        return {k: full(k, s) for k, s in zip(group, _with_own(lands, own, slot))}

    groups = {"mix": ("g2t", "u2t", "d2", "out"), "in": ("int",), "g1t": ("g1t",), "u1t": ("u1t",), "d1": ("d1",)}
    sent, sent_tokens = {}, []

    def grads_ready(stage, gw):
        stacks = [by_owner(k, gw[k]) for k in groups[stage]]
        flight, token = _copies_start(stacks, name="grads_" + stage + "_start", by_owner=True)
        sent[stage] = flight
        sent_tokens.append(token)
        return token

    norms = {"ffn1": ffn1_norm + token_last[0, 0], "mix": mix_norm, "sb": sb_out_norm, "hg": hg_out_norm,
             "ffn2": ffn2_norm, "final": final_norm.reshape(1, D_MODEL)}
    loss_row, grad_x, gw, gv = _local_step(x[0], loss_target[0], norms, hg_lower_bound_logits, w_first, weights_after,
                                           grads_ready)

    lb_row = jnp.concatenate([gv["lb"], jnp.zeros_like(gv["lb"])], axis=1)
    part = _vector_rows([gv["ffn1"], gv["mix"], gv["ffn2"], gv["final"], jnp.concatenate([gv["sb"], gv["hg"]], axis=1),
                         lb_row, None, loss_row])
    vec_w = _vector_rows(vectors(ffn1_norm, mix_norm, sb_out_norm, hg_lower_bound_logits, hg_out_norm, ffn2_norm, final_norm))
    vec_m = _vector_rows(vectors(m_ffn1_norm, m_mix_norm, m_sb_out_norm, m_hg_lower_bound_logits, m_hg_out_norm,
                                 m_ffn2_norm, m_final_norm))
    vec_v = _vector_rows(vectors(v_ffn1_norm, v_mix_norm, v_sb_out_norm, v_hg_lower_bound_logits, v_hg_out_norm,
                                 v_ffn2_norm, v_final_norm))
    updated, after = {}, sent_tokens[-1]
    for stage, flight in sent.items():
        if stage == list(sent)[-1]:
            *vecs, loss_out = _vectors_update(part, vec_w, vec_m, vec_v, name="vectors_update", tie=after)
            after = loss_out
        stacks, lands = _copies_wait(flight, after, name="grads_" + stage + "_wait", by_owner=True)
        own = [lax.dynamic_index_in_dim(s, slot, keepdims=False) for s in stacks]
        for k, part_k in zip(groups[stage], _with_own(lands, own, slot)):
            updated[k] = _sum_and_update(part_k, view(k, w_sh[k]), view(k, m_sh[k]), view(k, v_sh[k]), name="adamw_" + k,
                                         tie=after)
            after = updated[k][0]
    mats = [{k: view(k, updated[k][i]) for k in keys} for i in range(4)]

    def leaves(mat, vec):
        half = D_MODEL // 2
        return (
            vec[0:1], mat["g1t"], mat["u1t"], mat["d1"], vec[1:2], mat["int"], vec[4:5, :half],
            vec[ROW_LOGITS].reshape(2, half), vec[4:5, half:], mat["out"], vec[2:3], mat["g2t"], mat["u2t"],
            mat["d2"], vec[3],
        )

    out = [loss_out[0, 0], grad_x[None]]
    for mat, vec in zip(mats, vecs):
        out.extend(leaves(mat, vec))
    return tuple(out)
```

```python
import jax
import jax.numpy as jnp
from jax import lax
from jax.experimental import pallas as pl
from jax.experimental.pallas import tpu as pltpu

F32, BF16 = jnp.float32, jnp.bfloat16
D_MODEL = 1024
D_FF = 2816
SB_WIDTH = 512
HG_WIDTH = 512
SB_HEAD_DIM = 64
HG_HEAD_DIM = 128
IN_COLS = 3584
EPS = 1e-6
N_DEV = 8
LANES = 128
HG_CHUNK = 16
VMEM_LIMIT_BYTES = 48 * 1024 * 1024
FFN_BWD_VMEM_LIMIT_BYTES = 56 * 1024 * 1024
ADAM_LR, ADAM_B1, ADAM_B2, ADAM_EPS, ADAM_WD, ADAM_STEP = 0.001, 0.9, 0.999, 1e-08, 0.01, 10
MESH = pl.DeviceIdType.MESH


def _params(*semantics, vmem_limit_bytes=VMEM_LIMIT_BYTES):
    return pltpu.CompilerParams(dimension_semantics=semantics, vmem_limit_bytes=vmem_limit_bytes)


def _dot(a, b):
    return jnp.dot(a, b, preferred_element_type=F32)


def _dot_nt(a, b):
    return lax.dot_general(a, b, (((1,), (1,)), ((), ())), preferred_element_type=F32)


def _dot_tn(a, b):
    return lax.dot_general(a, b, (((0,), (0,)), ((), ())), preferred_element_type=F32)


def _split3(x):
    hi = x.astype(BF16)
    r1 = x - hi.astype(F32)
    mid = r1.astype(BF16)
    lo = (r1 - mid.astype(F32)).astype(BF16)
    return hi, mid, lo


def _rms(xv):
    rstd = lax.rsqrt(jnp.mean(xv * xv, axis=-1, keepdims=True) + EPS)
    return xv * rstd, rstd


def _sigmoid(x):
    return 0.5 + 0.5 * jnp.tanh(0.5 * x)


def _loss_terms(xv, gain, target):
    xhat, rstd = _rms(xv)
    err = xhat * gain - target
    loss = 0.5 * jnp.sum(jnp.mean(err * err, axis=-1, keepdims=True), axis=0, keepdims=True)
    dy = err * (1.0 / xv.shape[-1])
    dxh = dy * gain
    dx = rstd * (dxh - xhat * jnp.mean(dxh * xhat, axis=-1, keepdims=True))
    return dx, jnp.sum(dy * xhat, axis=0, keepdims=True), loss


def _mm(a, b, *, name, tm, tn, nt=False, ta=False, out_dtype=F32, tie=None):
    k, m = a.shape if ta else a.shape[::-1]
    n = b.shape[0] if nt else b.shape[1]
    assert m % tm == 0 and n % tn == 0 and not (nt and ta), (name, a.shape, b.shape, tm, tn)

    def body(a_ref, b_ref, *rest):
        av = a_ref[...].astype(BF16)
        bv = b_ref[...].astype(BF16)
        rest[-1][...] = (_dot_nt(av, bv) if nt else _dot_tn(av, bv) if ta else _dot(av, bv)).astype(out_dtype)

    in_specs = [
        pl.BlockSpec((k, tm), lambda i, j: (0, i)) if ta else pl.BlockSpec((tm, k), lambda i, j: (i, 0)),
        pl.BlockSpec((tn, k), lambda i, j: (j, 0)) if nt else pl.BlockSpec((k, tn), lambda i, j: (0, j)),
    ]
    operands = [a, b]
    if tie is not None:
        in_specs.append(pl.BlockSpec(memory_space=pl.ANY))
        operands.append(tie)
    return pl.pallas_call(
        body,
        name=name,
        grid=(m // tm, n // tn),
        in_specs=in_specs,
        out_specs=pl.BlockSpec((tm, tn), lambda i, j: (i, j)),
        out_shape=jax.ShapeDtypeStruct((m, n), out_dtype),
        compiler_params=_params("parallel", "parallel"),
    )(*operands)


def _ffn_fwd(x, gain, wgt, wut, wd, *, name, next_gain=None, head=None, tm=1024, tf=256):
    t = x.shape[0]
    nj = D_FF // tf
    extra_in = [] if next_gain is None else [next_gain]
    extra_in += [] if head is None else list(head)

    def body(x_ref, g_ref, wg_ref, wu_ref, wd_prev_ref, wd_last_ref, *rest):
        extra, (xo_ref, a_ref, b_ref, h_ref, st_ref) = rest[:len(extra_in)], rest[len(extra_in):len(extra_in) + 5]
        tail_out, (acc, s_prev) = rest[len(extra_in) + 5:-2], rest[-2:]
        i = pl.program_id(0)
        j = pl.program_id(1)

        @pl.when(j == 0)
        def _():
            xhat, _ = _rms(x_ref[...])
            h_ref[...] = (xhat * g_ref[...]).astype(BF16)
            acc[...] = jnp.zeros_like(acc)
            s_prev[...] = jnp.zeros_like(s_prev)

        acc[...] += _dot(s_prev[...], wd_prev_ref[...])
        h = h_ref[...]
        a = _dot_nt(h, wg_ref[...])
        b = _dot_nt(h, wu_ref[...])
        a_ref[...] = a.astype(BF16)
        b_ref[...] = b.astype(BF16)
        s = (a * _sigmoid(a) * b).astype(BF16)
        st_ref[...] = s
        s_prev[...] = s

        @pl.when(j == nj - 1)
        def _():
            xo = x_ref[...] + 0.5 * (acc[...] + _dot(s, wd_last_ref[...]))
            if head is None:
                xo_ref[...] = xo
            if next_gain is not None:
                tail_out[0][...] = (_rms(xo)[0] * extra[0][...]).astype(BF16)
            if head is not None:
                gain_ref, target_ref = extra[-2:]
                dg_ref, loss_ref = tail_out[-2:]
                xo_ref[...], part_g, part_loss = _loss_terms(xo, gain_ref[...], target_ref[...])

                @pl.when(i == 0)
                def _():
                    dg_ref[...] = part_g
                    loss_ref[...] = jnp.broadcast_to(part_loss, loss_ref.shape)

                @pl.when(i > 0)
                def _():
                    dg_ref[...] += part_g
                    loss_ref[...] += jnp.broadcast_to(part_loss, loss_ref.shape)

    row = pl.BlockSpec((tm, D_MODEL), lambda i, j: (i, 0))
    vec = pl.BlockSpec((1, D_MODEL), lambda i, j: (0, 0))
    tile = pl.BlockSpec((tm, tf), lambda i, j: (i, j))
    weights = pl.BlockSpec((tf, D_MODEL), lambda i, j: (j, 0))
    tail_specs = ([] if next_gain is None else [row]) + ([] if head is None else [vec, vec])
    tail_shapes = ([] if next_gain is None else [jax.ShapeDtypeStruct((t, D_MODEL), BF16)]) + (
        [] if head is None else [jax.ShapeDtypeStruct((1, D_MODEL), F32)] * 2)
    return pl.pallas_call(
        body,
        name=name,
        grid=(t // tm, nj),
        in_specs=[
            row, vec, weights, weights,
            pl.BlockSpec((tf, D_MODEL), lambda i, j: (jnp.maximum(j - 1, 0), 0)),
            pl.BlockSpec((tf, D_MODEL), lambda i, j: (nj - 1, 0)),
        ] + ([] if next_gain is None else [vec]) + ([] if head is None else [vec, row]),
        out_specs=[row, tile, tile, row, tile] + tail_specs,
        out_shape=[
            jax.ShapeDtypeStruct((t, D_MODEL), F32),
            jax.ShapeDtypeStruct((t, D_FF), BF16),
            jax.ShapeDtypeStruct((t, D_FF), BF16),
            jax.ShapeDtypeStruct((t, D_MODEL), BF16),
            jax.ShapeDtypeStruct((t, D_FF), BF16),
        ] + tail_shapes,
        scratch_shapes=[pltpu.VMEM((tm, D_MODEL), F32), pltpu.VMEM((tm, tf), BF16)],
        compiler_params=_params("arbitrary", "arbitrary"),
    )(x, gain, wgt, wut, wd, wd, *extra_in)


def _ffn_bwd(dout, x, gain, a, b, wgt, wut, wd, *, name, tm=1024, tf=256):
    t = x.shape[0]
    nj = D_FF // tf

    def body(do_ref, x_ref, g_ref, a_ref, b_ref, wg_prev_ref, wu_prev_ref, wg_last_ref, wu_last_ref, wd_ref,
             dx_ref, dg_ref, da_ref, db_ref, dob_ref, dob_scr, dh, da_prev, db_prev):
        i = pl.program_id(0)
        j = pl.program_id(1)

        @pl.when(j == 0)
        def _():
            d = (0.5 * do_ref[...]).astype(BF16)
            dob_scr[...] = d
            dob_ref[...] = d
            dh[...] = jnp.zeros_like(dh)
            da_prev[...] = jnp.zeros_like(da_prev)
            db_prev[...] = jnp.zeros_like(db_prev)

        dh[...] += _dot(da_prev[...], wg_prev_ref[...]) + _dot(db_prev[...], wu_prev_ref[...])
        ds = _dot_nt(dob_scr[...], wd_ref[...])
        av = a_ref[...].astype(F32)
        bv = b_ref[...].astype(F32)
        sig = _sigmoid(av)
        dbv = (ds * (av * sig)).astype(BF16)
        dav = (ds * bv * (sig * (1.0 + av * (1.0 - sig)))).astype(BF16)
        da_ref[...] = dav
        db_ref[...] = dbv
        da_prev[...] = dav
        db_prev[...] = dbv

        @pl.when(j == nj - 1)
        def _():
            xhat, rstd = _rms(x_ref[...])
            dhv = dh[...] + _dot(dav, wg_last_ref[...]) + _dot(dbv, wu_last_ref[...])
            part = jnp.sum(dhv * xhat, axis=0, keepdims=True)

            @pl.when(i == 0)
            def _():
                dg_ref[...] = part

            @pl.when(i > 0)
            def _():
                dg_ref[...] += part

            dxh = dhv * g_ref[...]
            dx_ref[...] = do_ref[...] + rstd * (dxh - xhat * jnp.mean(dxh * xhat, axis=-1, keepdims=True))

    return pl.pallas_call(
        body,
        name=name,
        grid=(t // tm, nj),
        in_specs=[
            pl.BlockSpec((tm, D_MODEL), lambda i, j: (i, 0)),
            pl.BlockSpec((tm, D_MODEL), lambda i, j: (i, 0)),
            pl.BlockSpec((1, D_MODEL), lambda i, j: (0, 0)),
            pl.BlockSpec((tm, tf), lambda i, j: (i, j)),
            pl.BlockSpec((tm, tf), lambda i, j: (i, j)),
            pl.BlockSpec((tf, D_MODEL), lambda i, j: (jnp.maximum(j - 1, 0), 0)),
            pl.BlockSpec((tf, D_MODEL), lambda i, j: (jnp.maximum(j - 1, 0), 0)),
            pl.BlockSpec((tf, D_MODEL), lambda i, j: (nj - 1, 0)),
            pl.BlockSpec((tf, D_MODEL), lambda i, j: (nj - 1, 0)),
            pl.BlockSpec((tf, D_MODEL), lambda i, j: (j, 0)),
        ],
        out_specs=[
            pl.BlockSpec((tm, D_MODEL), lambda i, j: (i, 0)),
            pl.BlockSpec((1, D_MODEL), lambda i, j: (0, 0)),
            pl.BlockSpec((tm, tf), lambda i, j: (i, j)),
            pl.BlockSpec((tm, tf), lambda i, j: (i, j)),
            pl.BlockSpec((tm, D_MODEL), lambda i, j: (i, 0)),
        ],
        out_shape=[
            jax.ShapeDtypeStruct((t, D_MODEL), F32),
            jax.ShapeDtypeStruct((1, D_MODEL), F32),
            jax.ShapeDtypeStruct((t, D_FF), BF16),
            jax.ShapeDtypeStruct((t, D_FF), BF16),
            jax.ShapeDtypeStruct((t, D_MODEL), BF16),
        ],
        scratch_shapes=[pltpu.VMEM((tm, D_MODEL), BF16), pltpu.VMEM((tm, D_MODEL), F32), pltpu.VMEM((tm, tf), BF16),
                        pltpu.VMEM((tm, tf), BF16)],
        compiler_params=_params("arbitrary", "arbitrary", vmem_limit_bytes=FFN_BWD_VMEM_LIMIT_BYTES),
    )(dout, x, gain, a, b, wgt, wut, wgt, wut, wd)


def _in_proj_bwd(dproj, w_int, x, gain, dres, *, name, tm=512):
    t, k = dproj.shape

    def body(dp_ref, w_ref, x_ref, g_ref, dr_ref, dx_ref, dg_ref):
        i = pl.program_id(0)
        dhv = _dot(dp_ref[...], w_ref[...])
        xhat, rstd = _rms(x_ref[...])
        part = jnp.sum(dhv * xhat, axis=0, keepdims=True)

        @pl.when(i == 0)
        def _():
            dg_ref[...] = part

        @pl.when(i > 0)
        def _():
            dg_ref[...] += part

        dxh = dhv * g_ref[...]
        dx_ref[...] = dr_ref[...] + rstd * (dxh - xhat * jnp.mean(dxh * xhat, axis=-1, keepdims=True))

    row = pl.BlockSpec((tm, D_MODEL), lambda i: (i, 0))
    vec = pl.BlockSpec((1, D_MODEL), lambda i: (0, 0))
    return pl.pallas_call(
        body,
        name=name,
        grid=(t // tm,),
        in_specs=[pl.BlockSpec((tm, k), lambda i: (i, 0)), pl.BlockSpec((k, D_MODEL), lambda i: (0, 0)), row, vec, row],
        out_specs=[row, vec],
        out_shape=[jax.ShapeDtypeStruct((t, D_MODEL), F32), jax.ShapeDtypeStruct((1, D_MODEL), F32)],
        compiler_params=_params("arbitrary"),
    )(dproj, w_int, x, gain, dres)


ATT_Q_TILE = 512
ATT_K_BLOCK = 256


def _first_head_lanes():
    return lax.broadcasted_iota(jnp.int32, (1, LANES), 1) < SB_HEAD_DIM


def _stack_heads(x):
    first = _first_head_lanes()
    return jnp.concatenate([jnp.where(first, x, 0.0), jnp.where(first, 0.0, x)], axis=0)


def _unstack_heads(x, rows):
    return jnp.where(_first_head_lanes(), x[:rows], x[rows:])


def _rows_from(x, first, rows):
    return x if first == 0 else jnp.concatenate([x[first:rows], x[rows + first:]], axis=0)


def _rows_into(full, part, first, rows):
    if first == 0:
        return part
    n = rows - first
    return jnp.concatenate([full[:first], part[:n], full[rows:rows + first], part[n:]], axis=0)


def _tri(n, relation):
    r = lax.broadcasted_iota(jnp.int32, (n, n), 0)
    c = lax.broadcasted_iota(jnp.int32, (n, n), 1)
    return relation(r, c).astype(BF16)


def _scan_dot(x, tri):
    hi = x.astype(BF16)
    lo = (x - hi.astype(F32)).astype(BF16)
    return _dot(jnp.concatenate([hi, lo], axis=1), jnp.concatenate([tri, tri], axis=0))


def _log_terms(z):
    lbeta = jnp.minimum(z, 0.0) - jnp.log(1.0 + jnp.exp(-jnp.abs(z)))
    return lbeta, lbeta - z


def _attn_fwd(proj, *, name):
    t = proj.shape[0]
    tq, tk = ATT_Q_TILE, ATT_K_BLOCK
    diag = tq // tk
    n_pairs = SB_WIDTH // LANES

    def body(q_ref, k_ref, v_ref, o_ref, kept_ref):
        qi = pl.program_id(1)
        q = q_ref[...] * (SB_HEAD_DIM ** -0.5)
        qs = _stack_heads(q).astype(BF16)
        tri = _tri(tk, lambda j, s: j > s)
        trow = lax.broadcasted_iota(jnp.int32, (tq, tk), 0)
        scol = lax.broadcasted_iota(jnp.int32, (tq, tk), 1)

        def block(kb, carry, causal, first=0):
            acc, c = carry
            off = pl.multiple_of(kb * tk, tk)
            lbeta, lrest = _log_terms(_dot_nt(_rows_from(qs, first, tq), k_ref[pl.ds(off, tk), :].astype(BF16)))
            if causal is not None:
                lrest = jnp.where(causal, lrest, 0.0)
            w = jnp.exp(lbeta + (_scan_dot(lrest, tri) + _rows_from(c, first, tq)))
            if causal is not None:
                w = jnp.where(causal, w, 0.0)
            wb = w.astype(BF16)
            kept_ref[0, 0, kb] = _rows_into(jnp.zeros((2 * tq, tk), BF16), wb, first, tq)
            acc = _rows_into(acc, _rows_from(acc, first, tq) + _dot(wb, v_ref[pl.ds(off, tk), :].astype(BF16)), first, tq)
            return acc, _rows_into(c, _rows_from(c, first, tq) + jnp.sum(lrest, axis=1, keepdims=True), first, tq)

        carry = (jnp.zeros((2 * tq, LANES), F32), jnp.zeros((2 * tq, 1), F32))
        n_full = qi * diag
        for j in reversed(range(diag)):
            mask = ((scol + j * tk) < trow)[j * tk:]
            carry = block(n_full + j, carry, jnp.concatenate([mask, mask], axis=0), first=j * tk)

        def odd_tile(carry):
            for j in range(diag):
                carry = block(n_full - 1 - j, carry, None)
            return carry

        carry = lax.cond(qi % 2 == 1, odd_tile, lambda c: c, carry)
        last = n_full - 1 - (qi % 2) * diag

        def step(it, carry):
            for j in range(2 * diag):
                carry = block(last - (2 * diag * it + j), carry, None)
            return carry

        acc, _ = lax.fori_loop(0, qi // 2, step, carry)
        o_ref[...] = _unstack_heads(acc, tq)

    return pl.pallas_call(
        body,
        name=name,
        grid=(n_pairs, t // tq),
        in_specs=[
            pl.BlockSpec((tq, LANES), lambda p, i: (i, p)),
            pl.BlockSpec((t, LANES), lambda p, i: (0, n_pairs + p)),
            pl.BlockSpec((t, LANES), lambda p, i: (0, 2 * n_pairs + p)),
        ],
        out_specs=[pl.BlockSpec((tq, LANES), lambda p, i: (i, p)),
                   pl.BlockSpec((1, 1, t // tk, 2 * tq, tk), lambda p, i: (p, i, 0, 0, 0))],
        out_shape=[jax.ShapeDtypeStruct((t, SB_WIDTH), F32),
                   jax.ShapeDtypeStruct((n_pairs, t // tq, t // tk, 2 * tq, tk), BF16)],
        compiler_params=_params("parallel", "parallel"),
    )(proj, proj, proj)


def _attn_bwd(proj, kept, do, *, name, tie=None):
    t = proj.shape[0]
    tq, tk = ATT_Q_TILE, ATT_K_BLOCK
    diag = tq // tk
    n_pairs = SB_WIDTH // LANES
    scale = SB_HEAD_DIM ** -0.5

    def body(q_ref, k_ref, v_ref, kept_ref, do_ref, *rest):
        dq_ref, dk_ref, dv_ref = rest[-3:]
        qi = pl.program_id(1)

        @pl.when(qi == 0)
        def _():
            dk_ref[...] = jnp.zeros_like(dk_ref)
            dv_ref[...] = jnp.zeros_like(dv_ref)

        qs = _stack_heads(q_ref[...] * scale).astype(BF16)
        dos = _stack_heads(do_ref[...]).astype(BF16)
        before = _tri(tk, lambda s, j: s < j)
        trow = lax.broadcasted_iota(jnp.int32, (tq, tk), 0)
        scol = lax.broadcasted_iota(jnp.int32, (tq, tk), 1)

        def block(kb, carry, causal, first=0):
            dq, cg = carry
            off = pl.multiple_of(kb * tk, tk)
            q_rows, do_rows = _rows_from(qs, first, tq), _rows_from(dos, first, tq)
            wb = _rows_from(kept_ref[0, 0, kb], first, tq)
            kblk = k_ref[pl.ds(off, tk), :].astype(BF16)
            sig = _sigmoid(_dot_nt(q_rows, kblk))
            g = wb.astype(F32) * _dot_nt(do_rows, v_ref[pl.ds(off, tk), :].astype(BF16))
            prior = _scan_dot(g, before) + _rows_from(cg, first, tq)
            dz = g - sig * (g + prior)
            if causal is not None:
                dz = jnp.where(causal, dz, 0.0)
            dzb = dz.astype(BF16)
            dq = _rows_into(dq, _rows_from(dq, first, tq) + _dot(dzb, kblk), first, tq)
            dk_ref[pl.ds(off, tk), :] += _dot_tn(dzb, q_rows)
            dv_ref[pl.ds(off, tk), :] += _dot_tn(wb, do_rows)
            return dq, _rows_into(cg, _rows_from(cg, first, tq) + jnp.sum(g, axis=1, keepdims=True), first, tq)

        n_full = qi * diag

        def step(it, carry):
            for j in range(2 * diag):
                carry = block(2 * diag * it + j, carry, None)
            return carry

        def odd_tile(carry):
            for j in range(diag):
                carry = block(n_full - diag + j, carry, None)
            return carry

        carry = lax.fori_loop(0, qi // 2, step, (jnp.zeros((2 * tq, LANES), F32), jnp.zeros((2 * tq, 1), F32)))
        carry = lax.cond(qi % 2 == 1, odd_tile, lambda c: c, carry)
        for j in range(diag):
            mask = ((scol + j * tk) < trow)[j * tk:]
            carry = block(n_full + j, carry, jnp.concatenate([mask, mask], axis=0), first=j * tk)
        dq_ref[...] = (_unstack_heads(carry[0], tq) * scale).astype(BF16)

    tile_spec = pl.BlockSpec((tq, LANES), lambda p, i: (i, p))
    full_spec = pl.BlockSpec((t, LANES), lambda p, i: (0, p))
    return pl.pallas_call(
        body,
        name=name,
        grid=(n_pairs, t // tq),
        in_specs=[
            tile_spec,
            pl.BlockSpec((t, LANES), lambda p, i: (0, n_pairs + p)),
            pl.BlockSpec((t, LANES), lambda p, i: (0, 2 * n_pairs + p)),
            pl.BlockSpec((1, 1, t // tk, 2 * tq, tk), lambda p, i: (p, i, 0, 0, 0)),
            tile_spec,
        ] + ([] if tie is None else [pl.BlockSpec(memory_space=pl.ANY)]),
        out_specs=[tile_spec, full_spec, full_spec],
        out_shape=[jax.ShapeDtypeStruct((t, SB_WIDTH), BF16)] + [jax.ShapeDtypeStruct((t, SB_WIDTH), F32)] * 2,
        compiler_params=_params("arbitrary", "arbitrary"),
    )(proj, proj, proj, kept, do, *([] if tie is None else [tie]))


HG_BLOCK = 128
HG_HEADS = HG_WIDTH // HG_HEAD_DIM


def _chunk_mats(n):
    r = lax.broadcasted_iota(jnp.int32, (n, n), 0)
    c = lax.broadcasted_iota(jnp.int32, (n, n), 1)
    same = (r // HG_CHUNK) == (c // HG_CHUNK)
    upto = (same & (c <= r)).astype(BF16)
    whole = same.astype(BF16)
    onward = (same & (c >= r)).astype(BF16)
    return upto, whole, onward


def _rows_dot(mat, x):
    return _dot(jnp.concatenate([mat, mat, mat], axis=1), jnp.concatenate(_split3(x), axis=0))


def _split_heads(x):
    return jnp.stack([x[:, h * HG_HEAD_DIM:(h + 1) * HG_HEAD_DIM] for h in range(HG_HEADS)], axis=0)


def _merge_heads(x):
    return jnp.concatenate([x[h] for h in range(HG_HEADS)], axis=1)


def _lower_bound(lg_ref):
    lg = lg_ref[...]
    return _sigmoid(lg[0:1, :] - lg[1:2, :])


def _hgrn_prepare(q_ref, f_ref, lb, h, upto, whole):
    cols = slice(h * HG_HEAD_DIM, (h + 1) * HG_HEAD_DIM)
    lbh = lb[:, cols]
    sg = _sigmoid(f_ref[:, cols])
    forget = lbh + (1.0 - lbh) * sg
    logf = jnp.log(forget)
    kk = (1.0 - lbh) * (1.0 - sg)
    qv = q_ref[:, cols]
    qsig = _sigmoid(qv)
    qh = qv * qsig
    b = _rows_dot(upto, logf)
    blast = _rows_dot(whole, logf)
    return dict(lbh=lbh, sg=sg, forget=forget, kk=kk, qv=qv, qsig=qsig, qh=qh, b=b, eb=jnp.exp(b),
                ekb=jnp.exp(blast - b), dl=jnp.exp(blast))


def _hgrn_fwd(proj, logits, *, name):
    t = proj.shape[0]
    tb = HG_BLOCK
    nc = tb // HG_CHUNK
    hd = HG_HEAD_DIM

    def body(q_ref, f_ref, i_ref, lg_ref, o_ref, st_ref, state, qh_s, kk_s, b_s, qe_s, ke_s, dl_s):
        @pl.when(pl.program_id(0) == 0)
        def _():
            state[...] = jnp.zeros_like(state)

        lb = _lower_bound(lg_ref)
        upto, whole, _ = _chunk_mats(tb)
        for h in range(HG_HEADS):
            p = _hgrn_prepare(q_ref, f_ref, lb, h, upto, whole)
            qh_s[h] = p["qh"]
            kk_s[h] = p["kk"]
            b_s[h] = p["b"]
            qe_s[h] = (p["qh"] * p["eb"]).astype(BF16)
            ke_s[h] = (p["kk"] * p["ekb"]).astype(BF16)
            dl_s[h] = p["dl"]
        rowi = lax.broadcasted_iota(jnp.int32, (HG_HEADS, HG_CHUNK, hd), 1)

        def chunk(c, _):
            r0 = pl.multiple_of(c * HG_CHUNK, HG_CHUNK)
            rows = pl.ds(r0, HG_CHUNK)
            bc = b_s[:, rows, :]
            qc = qh_s[:, rows, :]
            kc = kk_s[:, rows, :]
            vc = _split_heads(i_ref[rows, :])
            s_in = state[...]
            st_ref[c] = s_in
            s_in_b = s_in.astype(BF16)
            qe = qe_s[:, rows, :]
            o = jnp.stack([_dot_nt(qe[h], s_in_b[h]) for h in range(HG_HEADS)], axis=0)
            for s in range(HG_CHUNK):
                pair = jnp.where(rowi >= s, qc * jnp.exp(bc - bc[:, s:s + 1, :]) * kc[:, s:s + 1, :], 0.0)
                o = o + jnp.sum(pair, axis=2, keepdims=True) * vc[:, s:s + 1, :]
            o_ref[rows, :] = _merge_heads(o)
            vcb = vc.astype(BF16)
            ke = ke_s[:, rows, :]
            update = jnp.stack([_dot_tn(vcb[h], ke[h]) for h in range(HG_HEADS)], axis=0)
            state[...] = s_in * dl_s[:, pl.ds(r0, 1), :] + update
            return 0

        lax.fori_loop(0, nc, chunk, 0, unroll=4)

    blk =lambda col: pl.BlockSpec((tb, HG_WIDTH), lambda i: (i, col))
    head_f32 = pltpu.VMEM((HG_HEADS, tb, hd), F32)
    head_bf16 = pltpu.VMEM((HG_HEADS, tb, hd), BF16)
    return pl.pallas_call(
        body,
        name=name,
        grid=(t // tb,),
        in_specs=[blk(3), blk(4), blk(5), pl.BlockSpec((2, HG_WIDTH), lambda i: (0, 0))],
        out_specs=[
            pl.BlockSpec((tb, HG_WIDTH), lambda i: (i, 0)),
            pl.BlockSpec((nc, HG_HEADS, hd, hd), lambda i: (i, 0, 0, 0)),
        ],
        out_shape=[
            jax.ShapeDtypeStruct((t, HG_WIDTH), F32),
            jax.ShapeDtypeStruct((t // HG_CHUNK, HG_HEADS, hd, hd), F32),
        ],
        scratch_shapes=[pltpu.VMEM((HG_HEADS, hd, hd), F32), head_f32, head_f32, head_f32, head_bf16, head_bf16,
                        head_f32],
        compiler_params=_params("arbitrary"),
    )(proj, proj, proj, logits)


def _hgrn_bwd(proj, logits, states, do, *, name):
    t = proj.shape[0]
    tb = HG_BLOCK
    nb = t // tb
    nc = tb // HG_CHUNK
    hd = HG_HEAD_DIM

    def body(q_ref, f_ref, i_ref, lg_ref, st_ref, do_ref, dq_ref, df_ref, di_ref, dlb_ref,
             dstate, qh_s, kk_s, b_s, eb_s, ekb_s, qe_s, ke_s, dl_s, dqh_s, dkk_s, dlf_s):
        step = pl.program_id(0)

        @pl.when(step == 0)
        def _():
            dstate[...] = jnp.zeros_like(dstate)
            dlb_ref[...] = jnp.zeros_like(dlb_ref)

        lb = _lower_bound(lg_ref)
        upto, whole, _ = _chunk_mats(tb)
        prepared = []
        for h in range(HG_HEADS):
            p = _hgrn_prepare(q_ref, f_ref, lb, h, upto, whole)
            prepared.append(p)
            qh_s[h] = p["qh"]
            kk_s[h] = p["kk"]
            b_s[h] = p["b"]
            eb_s[h] = p["eb"]
            ekb_s[h] = p["ekb"]
            qe_s[h] = (p["qh"] * p["eb"]).astype(BF16)
            ke_s[h] = (p["kk"] * p["ekb"]).astype(BF16)
            dl_s[h] = p["dl"]
        rowi = lax.broadcasted_iota(jnp.int32, (HG_CHUNK, hd), 0)
        r16 = lax.broadcasted_iota(jnp.int32, (HG_CHUNK, HG_CHUNK), 0)
        c16 = lax.broadcasted_iota(jnp.int32, (HG_CHUNK, HG_CHUNK), 1)
        onward = (c16 >= r16).astype(BF16)

        def chunk(it, _):
            c = nc - 1 - it
            r0 = pl.multiple_of(c * HG_CHUNK, HG_CHUNK)
            rows = pl.ds(r0, HG_CHUNK)
            for h in range(HG_HEADS):
                cols = slice(h * hd, (h + 1) * hd)
                bc = b_s[h, rows, :]
                qc = qh_s[h, rows, :]
                kc = kk_s[h, rows, :]
                vc = i_ref[rows, cols]
                doc = do_ref[rows, cols]
                s_in = st_ref[c, h]
                ds_out = dstate[h]
                ds_out_b = ds_out.astype(BF16)
                docb = doc.astype(BF16)
                dl_row = dl_s[h, pl.ds(r0, 1), :]
                dqh = _dot(docb, s_in.astype(BF16)) * eb_s[h, rows, :]
                dkk = _dot(vc.astype(BF16), ds_out_b) * ekb_s[h, rows, :]
                dv = _dot_nt(ke_s[h, rows, :], ds_out_b)
                db = dqh * qc - dkk * kc
                dwhole = jnp.sum(dkk * kc, axis=0, keepdims=True) + jnp.sum(ds_out * s_in, axis=0, keepdims=True) * dl_row
                dk_rows, dv_rows = [], []
                for s in range(HG_CHUNK):
                    keep = rowi >= s
                    e = jnp.exp(bc - bc[s:s + 1, :])
                    k_row = kc[s:s + 1, :]
                    pcol = jnp.sum(jnp.where(keep, qc * e * k_row, 0.0), axis=1, keepdims=True)
                    dpcol = jnp.sum(doc * vc[s:s + 1, :], axis=1, keepdims=True)
                    m = jnp.where(keep, e * dpcol, 0.0)
                    y = m * qc
                    dqh = dqh + m * k_row
                    db = db + y * k_row
                    dk_rows.append(jnp.sum(y, axis=0, keepdims=True))
                    dv_rows.append(jnp.sum(pcol * doc, axis=0, keepdims=True))
                dkk_pairs = jnp.concatenate(dk_rows, axis=0)
                dkk = dkk + dkk_pairs
                db = db - dkk_pairs * kc
                dv = dv + jnp.concatenate(dv_rows, axis=0)
                dqh_s[h, rows, :] = dqh
                dkk_s[h, rows, :] = dkk
                dlf_s[h, rows, :] = _rows_dot(onward, db) + dwhole
                di_ref[rows, cols] = dv.astype(BF16)
                dstate[h] = ds_out * dl_row + _dot_tn(docb, qe_s[h, rows, :])
            return 0

        lax.fori_loop(0, nc, chunk, 0, unroll=4)
        for h in range(HG_HEADS):
            cols = slice(h * hd, (h + 1) * hd)
            p = prepared[h]
            dq_ref[:, cols] = (dqh_s[h] * (p["qsig"] * (1.0 + p["qv"] * (1.0 - p["qsig"])))).astype(BF16)
            dforget = dlf_s[h] / p["forget"] - dkk_s[h]
            df_ref[:, cols] = (dforget * (1.0 - p["lbh"]) * p["sg"] * (1.0 - p["sg"])).astype(BF16)
            dlb_ref[:, cols] += jnp.sum(dforget * (1.0 - p["sg"]), axis=0, keepdims=True)

    blk = lambda col: pl.BlockSpec((tb, HG_WIDTH), lambda i: (nb - 1 - i, col))
    vec = pl.BlockSpec((1, HG_WIDTH), lambda i: (0, 0))
    head_f32 = pltpu.VMEM((HG_HEADS, tb, hd), F32)
    head_bf16 = pltpu.VMEM((HG_HEADS, tb, hd), BF16)
    return pl.pallas_call(
        body,
        name=name,
        grid=(nb,),
        in_specs=[
            blk(3), blk(4), blk(5),
            pl.BlockSpec((2, HG_WIDTH), lambda i: (0, 0)),
            pl.BlockSpec((nc, HG_HEADS, hd, hd), lambda i: (nb - 1 - i, 0, 0, 0)),
            blk(0),
        ],
        out_specs=[blk(0), blk(0), blk(0), vec],
        out_shape=[jax.ShapeDtypeStruct((t, HG_WIDTH), BF16)] * 3 + [jax.ShapeDtypeStruct((1, HG_WIDTH), F32)],
        scratch_shapes=[
            pltpu.VMEM((HG_HEADS, hd, hd), F32),
            head_f32, head_f32, head_f32, head_f32, head_f32, head_bf16, head_bf16, head_f32,
            head_f32, head_f32, head_f32,
        ],
        compiler_params=_params("arbitrary"),
    )(proj, proj, proj, logits, states, do)


def _group_mat(width, head_dim):
    r = lax.broadcasted_iota(jnp.int32, (width, width), 0)
    c = lax.broadcasted_iota(jnp.int32, (width, width), 1)
    return ((r // head_dim) == (c // head_dim)).astype(BF16)


def _head_mean(x, mat, head_dim):
    hi = x.astype(BF16)
    lo = (x - hi.astype(F32)).astype(BF16)
    return (_dot(hi, mat) + _dot(lo, mat)) * (1.0 / head_dim)


def _mix_out_fwd(o_sb, o_hg, proj, g_sb, g_hg, w_out, x1, *, name, tm=512):
    t = x1.shape[0]

    def body(osb_ref, ohg_ref, gate_ref, gsb_ref, ghg_ref, w_ref, x_ref, xo_ref, mt_ref):
        msb = _group_mat(SB_WIDTH, SB_HEAD_DIM)
        mhg = _group_mat(HG_WIDTH, HG_HEAD_DIM)
        osb = osb_ref[...]
        ohg = ohg_ref[...]
        nsb = osb * lax.rsqrt(_head_mean(osb * osb, msb, SB_HEAD_DIM) + EPS) * gsb_ref[...]
        gate = gate_ref[...]
        nhg = ohg * lax.rsqrt(_head_mean(ohg * ohg, mhg, HG_HEAD_DIM) + EPS) * ghg_ref[...] * (gate * _sigmoid(gate))
        mixed = jnp.concatenate([nsb, nhg], axis=1).astype(BF16)
        mt_ref[...] = mixed
        xo_ref[...] = x_ref[...] + _dot(mixed, w_ref[...])

    half = pl.BlockSpec((tm, SB_WIDTH), lambda i: (i, 0))
    vec = pl.BlockSpec((1, SB_WIDTH), lambda i: (0, 0))
    row = pl.BlockSpec((tm, D_MODEL), lambda i: (i, 0))
    return pl.pallas_call(
        body,
        name=name,
        grid=(t // tm,),
        in_specs=[half, half, pl.BlockSpec((tm, HG_WIDTH), lambda i: (i, 6)), vec, vec,
                  pl.BlockSpec((D_MODEL, D_MODEL), lambda i: (0, 0)), row],
        out_specs=[row, row],
        out_shape=[jax.ShapeDtypeStruct((t, D_MODEL), F32), jax.ShapeDtypeStruct((t, D_MODEL), BF16)],
        compiler_params=_params("parallel"),
    )(o_sb, o_hg, proj, g_sb, g_hg, w_out, x1)


def _mix_out_bwd(dx2, o_sb, o_hg, proj, g_sb, g_hg, w_out, *, name, tm=512):
    t = dx2.shape[0]

    def body(dx_ref, osb_ref, ohg_ref, gate_ref, gsb_ref, ghg_ref, w_ref, dosb_ref, dohg_ref, dgate_ref, dgsb_ref,
             dghg_ref, dxb_ref):
        i = pl.program_id(0)
        msb = _group_mat(SB_WIDTH, SB_HEAD_DIM)
        mhg = _group_mat(HG_WIDTH, HG_HEAD_DIM)
        dxb = dx_ref[...].astype(BF16)
        dxb_ref[...] = dxb
        dmixed = _dot_nt(dxb, w_ref[...])
        dnsb = dmixed[:, :SB_WIDTH]
        dy = dmixed[:, SB_WIDTH:]

        osb = osb_ref[...]
        rstd = lax.rsqrt(_head_mean(osb * osb, msb, SB_HEAD_DIM) + EPS)
        ohat = osb * rstd
        part_sb = jnp.sum(dnsb * ohat, axis=0, keepdims=True)
        dohat = dnsb * gsb_ref[...]
        dosb_ref[...] = rstd * (dohat - ohat * _head_mean(dohat * ohat, msb, SB_HEAD_DIM))

        ohg = ohg_ref[...]
        rstd = lax.rsqrt(_head_mean(ohg * ohg, mhg, HG_HEAD_DIM) + EPS)
        ohat = ohg * rstd
        gate = gate_ref[...]
        sig = _sigmoid(gate)
        dn = dy * (gate * sig)
        dgate_ref[...] = (dy * (ohat * ghg_ref[...]) * (sig * (1.0 + gate * (1.0 - sig)))).astype(BF16)
        part_hg = jnp.sum(dn * ohat, axis=0, keepdims=True)
        dohat = dn * ghg_ref[...]
        dohg_ref[...] = rstd * (dohat - ohat * _head_mean(dohat * ohat, mhg, HG_HEAD_DIM))

        @pl.when(i == 0)
        def _():
            dgsb_ref[...] = part_sb
            dghg_ref[...] = part_hg

        @pl.when(i > 0)
        def _():
            dgsb_ref[...] += part_sb
            dghg_ref[...] += part_hg

    half = pl.BlockSpec((tm, SB_WIDTH), lambda i: (i, 0))
    vec = pl.BlockSpec((1, SB_WIDTH), lambda i: (0, 0))
    row = pl.BlockSpec((tm, D_MODEL), lambda i: (i, 0))
    return pl.pallas_call(
        body,
        name=name,
        grid=(t // tm,),
        in_specs=[row, half, half, pl.BlockSpec((tm, HG_WIDTH), lambda i: (i, 6)), vec, vec,
                  pl.BlockSpec((D_MODEL, D_MODEL), lambda i: (0, 0))],
        out_specs=[half, half, half, vec, vec, row],
        out_shape=[jax.ShapeDtypeStruct((t, SB_WIDTH), F32)] * 2 + [jax.ShapeDtypeStruct((t, SB_WIDTH), BF16)]
        + [jax.ShapeDtypeStruct((1, SB_WIDTH), F32)] * 2 + [jax.ShapeDtypeStruct((t, D_MODEL), BF16)],
        compiler_params=_params("arbitrary"),
    )(dx2, o_sb, o_hg, proj, g_sb, g_hg, w_out)


def _local_step(x, target, norms, logits, w, weights_after=None, grads_ready=None):
    w = dict(w)
    x1, a1, b1, h1, s1, hm = _ffn_fwd(x, norms["ffn1"], w["g1t"], w["u1t"], w["d1"], name="ffn1_fwd",
                                      next_gain=norms["mix"])
    if weights_after is not None:
        w.update(weights_after("ffn1", x1))
    proj = _mm(hm, w["int"], name="in_proj", tm=512, tn=IN_COLS, nt=True)
    o_sb, sb_kept = _attn_fwd(proj, name="sb_attn_fwd")
    o_hg, states = _hgrn_fwd(proj, logits, name="hgrn2_fwd")
    x2, mixed = _mix_out_fwd(o_sb, o_hg, proj, norms["sb"], norms["hg"], w["out"], x1, name="mix_out_fwd")
    if weights_after is not None:
        w.update(weights_after("mix", x2))
    dx3, a2, b2, h2, s2, d_final, loss_row = _ffn_fwd(x2, norms["ffn2"], w["g2t"], w["u2t"], w["d2"], name="ffn2_fwd",
                                                      head=(norms["final"], target))

    def weight_grad(lhs, rhs, name, tie=None):
        return _mm(lhs, rhs, name=name, tm=256, tn=D_MODEL, ta=True, out_dtype=BF16, tie=tie)

    def sent(stage):
        return grads_ready(stage, gw) if grads_ready is not None else None

    gw, gv = {}, {"final": d_final}
    dx2, gv["ffn2"], da2, db2, dob2 = _ffn_bwd(dx3, x2, norms["ffn2"], a2, b2, w["g2t"], w["u2t"], w["d2"],
                                               name="ffn2_bwd")
    gw["g2t"] = weight_grad(da2, h2, "ffn2_dgate")
    gw["u2t"] = weight_grad(db2, h2, "ffn2_dup")
    gw["d2"] = weight_grad(s2, dob2, "ffn2_ddown")

    do_sb, do_hg, d_gate, gv["sb"], gv["hg"], dx2b = _mix_out_bwd(
        dx2, o_sb, o_hg, proj, norms["sb"], norms["hg"], w["out"], name="mix_out_bwd")
    gw["out"] = weight_grad(mixed, dx2b, "out_dw")
    tie = sent("mix")
    dq_sb, dk_sb, dv_sb = _attn_bwd(proj, sb_kept, do_sb, name="sb_attn_bwd", tie=tie)
    dq_hg, df_hg, di_hg, d_lb = _hgrn_bwd(proj, logits if tie is None else logits + tie[0, 0], states, do_hg,
                                          name="hgrn2_bwd")
    dproj = jnp.concatenate([dq_sb, dk_sb.astype(BF16), dv_sb.astype(BF16), dq_hg, df_hg, di_hg, d_gate], axis=1)
    gw["int"] = weight_grad(dproj, hm, "in_dw")
    tie = sent("in")
    dx1, gv["mix"] = _in_proj_bwd(dproj, w["int"], x1, norms["mix"] if tie is None else norms["mix"] + tie[0, 0], dx2,
                                  name="in_dx")

    dx, gv["ffn1"], da1, db1, dob1 = _ffn_bwd(dx1, x, norms["ffn1"], a1, b1, w["g1t"], w["u1t"], w["d1"],
                                              name="ffn1_bwd")
    gw["g1t"] = weight_grad(da1, h1, "ffn1_dgate")
    gw["u1t"] = weight_grad(db1, h1, "ffn1_dup", tie=sent("g1t"))
    gw["d1"] = weight_grad(s1, dob1, "ffn1_ddown", tie=sent("u1t"))
    sent("d1")
    gv["lb"] = d_lb
    return loss_row, dx, gw, gv


HBM = pl.BlockSpec(memory_space=pl.ANY)


def _place():
    return lax.axis_index("x"), lax.axis_index("y"), lax.axis_index("c")


def _slot(px, py, pc):
    return 4 * px + 2 * py + pc


def _all_gather(blocks, *, name):
    n = len(blocks)

    def body(*refs):
        ins, outs = refs[:n], refs[n:2 * n]
        send_sems, recv_sems, local_sems = refs[2 * n:]
        x, y, c = _place()
        me, sibling = (x, y, c), (x, y, 1 - c)
        chips = [(1 - x, y), (x, 1 - y), (1 - x, 1 - y)]

        def copy(a, k, block, to, src=None):
            dst = outs[a].at[_slot(*block)]
            return pltpu.make_async_remote_copy(
                src_ref=dst if src is None else src, dst_ref=dst, send_sem=send_sems.at[7 * a + k],
                recv_sem=recv_sems.at[7 * a + k], device_id=to, device_id_type=MESH)

        mine = [pltpu.make_async_copy(ins[a], outs[a].at[_slot(*me)], local_sems.at[a]) for a in range(n)]
        for cp in mine:
            cp.start()
        first = []
        for a in range(n):
            first.append(copy(a, 0, me, sibling, src=ins[a]))
            first += [copy(a, 1 + j, me, (*chip, c), src=ins[a]) for j, chip in enumerate(chips)]
        for cp in first:
            cp.start()
        passed = []
        for j, chip in enumerate(chips):
            for a in range(n):
                copy(a, 1 + j, (*chip, c), me).wait_recv()
                fwd = copy(a, 4 + j, (*chip, c), sibling)
                fwd.start()
                passed.append(fwd)
        for a in range(n):
            copy(a, 0, sibling, me).wait_recv()
            for j, chip in enumerate(chips):
                copy(a, 4 + j, (*chip, 1 - c), me).wait_recv()
        for cp in first + passed:
            cp.wait_send()
        for cp in mine:
            cp.wait()

    return pl.pallas_call(
        body,
        name=name,
        in_specs=[HBM] * n,
        out_specs=[HBM] * n,
        out_shape=[jax.ShapeDtypeStruct((N_DEV,) + b.shape, b.dtype) for b in blocks],
        scratch_shapes=[pltpu.SemaphoreType.DMA((7 * n,)), pltpu.SemaphoreType.DMA((7 * n,)),
                        pltpu.SemaphoreType.DMA((n,))],
    )(*blocks)


def _flipped(place, d):
    return tuple(1 - p if (d >> (2 - axis)) & 1 else p for axis, p in enumerate(place))


SEM = pl.BlockSpec(memory_space=pltpu.SEMAPHORE)
EFFECT = pltpu.SideEffectType.DATAFLOW_SIDE_EFFECTING


def _split_copies(me, srcs, lands, send_sems, recv_sems, by_owner):
    copies = []
    for d in range(1, N_DEV):
        peer = _flipped(me, d)
        for a, (src, land) in enumerate(zip(srcs, lands)):
            copies.append(pltpu.make_async_remote_copy(
                src_ref=src.at[_slot(*peer)] if by_owner else src, dst_ref=land.at[_slot(*me)],
                send_sem=send_sems.at[7 * a + d - 1], recv_sem=recv_sems.at[7 * a + d - 1], device_id=peer,
                device_id_type=MESH))
    return copies


def _copies_start(srcs, *, name, by_owner, after=None):
    n = len(srcs)
    extra = [] if after is None else [after]
    land_shapes = [s.shape if by_owner else (N_DEV,) + s.shape for s in srcs]
    lands = [pltpu.with_memory_space_constraint(lax.empty(shape, s.dtype), pltpu.HBM) for shape, s in zip(land_shapes, srcs)]
    srcs = [pltpu.with_memory_space_constraint(s, pltpu.HBM) for s in srcs]

    def body(*refs):
        src_refs, land_refs = refs[:n], refs[n:2 * n]
        send_sems, recv_sems = refs[2 * n + len(extra)], refs[2 * n + len(extra) + 1]
        token = refs[-1]
        for cp in _split_copies(_place(), src_refs, land_refs, send_sems, recv_sems, by_owner):
            cp.start()
        token[...] = jnp.zeros_like(token)

    out = pl.pallas_call(
        body,
        name=name,
        in_specs=[HBM] * (2 * n + len(extra)),
        out_specs=[SEM, SEM] + [HBM] * (2 * n) + [pl.BlockSpec(memory_space=pltpu.VMEM)],
        out_shape=[pltpu.SemaphoreType.DMA((7 * n,)), pltpu.SemaphoreType.DMA((7 * n,))]
        + [pltpu.HBM(s.shape, s.dtype) for s in srcs] + [pltpu.HBM(shape, s.dtype) for shape, s in zip(land_shapes, srcs)]
        + [jax.ShapeDtypeStruct((8, LANES), F32)],
        input_output_aliases={i: 2 + i for i in range(2 * n)},
        compiler_params=pltpu.CompilerParams(has_side_effects=EFFECT),
    )(*srcs, *lands, *extra)
    return (out[0], out[1], out[2:2 + n], out[2 + n:2 + 2 * n]), out[-1]


def _copies_wait(started, after, *, name, by_owner):
    send_sems, recv_sems, srcs, lands = started
    n = len(srcs)

    def body(*refs):
        src_refs, land_refs = refs[:n], refs[n:2 * n]
        for cp in _split_copies(_place(), src_refs, land_refs, refs[2 * n], refs[2 * n + 1], by_owner):
            cp.wait_send()
            cp.wait_recv()

    out = pl.pallas_call(
        body,
        name=name,
        in_specs=[HBM] * (2 * n) + [SEM, SEM, HBM],
        out_specs=[HBM] * (2 * n),
        out_shape=[pltpu.HBM(s.shape, s.dtype) for s in srcs] + [pltpu.HBM(s.shape, s.dtype) for s in lands],
        input_output_aliases={i: i for i in range(2 * n)},
        compiler_params=pltpu.CompilerParams(has_side_effects=EFFECT),
    )(*srcs, *lands, send_sems, recv_sems, after)
    return out[:n], out[n:]


def _with_own(lands, own, slot):
    zero = jnp.zeros((), jnp.int32)
    return [lax.dynamic_update_slice(land, o[None], (slot.astype(jnp.int32),) + (zero,) * o.ndim)
            for land, o in zip(lands, own)]


def _adamw(w, g, m, v):
    m = ADAM_B1 * m + (1.0 - ADAM_B1) * g
    v = ADAM_B2 * v + (1.0 - ADAM_B2) * (g * g)
    m_hat = m / (1.0 - ADAM_B1 ** ADAM_STEP)
    v_hat = v / (1.0 - ADAM_B2 ** ADAM_STEP)
    delta = -ADAM_LR * (m_hat / (jnp.sqrt(v_hat) + ADAM_EPS) + ADAM_WD * w)
    return delta, m, v


def _sum_and_update(parts, w, m, v, *, name, tie=None):
    _, rows, cols = w.shape
    tr = rows // 2

    def body(p_ref, w_ref, m_ref, v_ref, *rest):
        g_ref, d_ref, mo_ref, vo_ref = rest[-4:]
        g = p_ref[0].astype(F32)
        for s in range(1, N_DEV):
            g = g + p_ref[s].astype(F32)
        g_ref[0] = g
        d_ref[0], mo_ref[0], vo_ref[0] = _adamw(w_ref[0], g, m_ref[0], v_ref[0])

    flat = pl.BlockSpec((1, tr, cols), lambda i: (0, i, 0))
    return pl.pallas_call(
        body,
        name=name,
        grid=(rows // tr,),
        in_specs=[pl.BlockSpec((N_DEV, tr, cols), lambda i: (0, i, 0)), flat, flat, flat]
        + ([] if tie is None else [pl.BlockSpec(memory_space=pl.ANY)]),
        out_specs=[flat] * 4,
        out_shape=[jax.ShapeDtypeStruct((1, rows, cols), F32)] * 4,
        compiler_params=_params("parallel"),
    )(parts, w, m, v, *([] if tie is None else [tie]))


VEC_ROWS = 8
ROW_LOGITS, ROW_LOSS = 5, 7


def _vectors_update(part, w, m, v, *, name, tie):
    def body(p_ref, w_ref, m_ref, v_ref, tie_ref, g_ref, d_ref, mo_ref, vo_ref, loss_ref, all_ref, send_sems, recv_sems):
        me = _place()
        all_ref[_slot(*me)] = p_ref[...]
        copies = []
        for d in range(1, N_DEV):
            peer = _flipped(me, d)
            copies.append(pltpu.make_async_remote_copy(
                src_ref=p_ref, dst_ref=all_ref.at[_slot(*me)], send_sem=send_sems.at[d - 1], recv_sem=recv_sems.at[d - 1],
                device_id=peer, device_id_type=MESH))
        for cp in copies:
            cp.start()
        for cp in copies:
            cp.wait()
        total = all_ref[0]
        for s in range(1, N_DEV):
            total = total + all_ref[s]
        wv = w_ref[...]
        half = D_MODEL // 2
        lb = _sigmoid(wv[ROW_LOGITS:ROW_LOGITS + 1, :half] - wv[ROW_LOGITS:ROW_LOGITS + 1, half:])
        d_first = total[ROW_LOGITS:ROW_LOGITS + 1, :half] * lb * (1.0 - lb)
        d_logits = jnp.concatenate([d_first, -d_first], axis=1)
        rowi = lax.broadcasted_iota(jnp.int32, (VEC_ROWS, D_MODEL), 0)
        g = jnp.where(rowi == ROW_LOGITS, d_logits, jnp.where(rowi < ROW_LOGITS, total, 0.0))
        g_ref[...] = g
        d_ref[...], mo_ref[...], vo_ref[...] = _adamw(wv, g, m_ref[...], v_ref[...])
        loss_ref[...] = total[ROW_LOSS:ROW_LOSS + 1, :]

    vmem = pl.BlockSpec(memory_space=pltpu.VMEM)
    return pl.pallas_call(
        body,
        name=name,
        in_specs=[vmem] * 4 + [HBM],
        out_specs=[vmem] * 5,
        out_shape=[jax.ShapeDtypeStruct((VEC_ROWS, D_MODEL), F32)] * 4 + [jax.ShapeDtypeStruct((1, D_MODEL), F32)],
        scratch_shapes=[pltpu.VMEM((N_DEV, VEC_ROWS, D_MODEL), F32), pltpu.SemaphoreType.DMA((7,)),
                        pltpu.SemaphoreType.DMA((7,))],
    )(part, w, m, v, tie)


TRANSPOSED = ("g1t", "u1t", "g2t", "u2t", "int")


def _vector_rows(rows):
    rowi = lax.broadcasted_iota(jnp.int32, (VEC_ROWS, D_MODEL), 0)
    out = jnp.zeros((VEC_ROWS, D_MODEL), F32)
    for i, r in enumerate(rows):
        if r is not None:
            out = jnp.where(rowi == i, r, out)
    return out


def kernel(x, ffn1_norm, ffn1_w_gate, ffn1_w_up, ffn1_w_down, mix_norm, w_in, sb_out_norm, hg_lower_bound_logits, hg_out_norm, w_out, ffn2_norm, ffn2_w_gate, ffn2_w_up, ffn2_w_down, final_norm, loss_target, m_ffn1_norm, m_ffn1_w_gate, m_ffn1_w_up, m_ffn1_w_down, m_mix_norm, m_w_in, m_sb_out_norm, m_hg_lower_bound_logits, m_hg_out_norm, m_w_out, m_ffn2_norm, m_ffn2_w_gate, m_ffn2_w_up, m_ffn2_w_down, m_final_norm, v_ffn1_norm, v_ffn1_w_gate, v_ffn1_w_up, v_ffn1_w_down, v_mix_norm, v_w_in, v_sb_out_norm, v_hg_lower_bound_logits, v_hg_out_norm, v_w_out, v_ffn2_norm, v_ffn2_w_gate, v_ffn2_w_up, v_ffn2_w_down, v_final_norm):
    def matrices(g1, u1, d1, win, wout, g2, u2, d2):
        return {"g1t": g1, "u1t": u1, "d1": d1, "int": win, "out": wout, "g2t": g2, "u2t": u2, "d2": d2}

    def vectors(n1, nm, nsb, lg, nhg, n2, nf):
        return [n1, nm, n2, nf.reshape(1, D_MODEL), jnp.concatenate([nsb, nhg], axis=1), lg.reshape(1, D_MODEL), None, None]

    w_sh = matrices(ffn1_w_gate, ffn1_w_up, ffn1_w_down, w_in, w_out, ffn2_w_gate, ffn2_w_up, ffn2_w_down)
    m_sh = matrices(m_ffn1_w_gate, m_ffn1_w_up, m_ffn1_w_down, m_w_in, m_w_out, m_ffn2_w_gate, m_ffn2_w_up, m_ffn2_w_down)
    v_sh = matrices(v_ffn1_w_gate, v_ffn1_w_up, v_ffn1_w_down, v_w_in, v_w_out, v_ffn2_w_gate, v_ffn2_w_up, v_ffn2_w_down)
    keys = list(w_sh)

    slot = _slot(*_place())

    def full(key, stack):
        return stack.reshape(-1, D_MODEL)

    def by_owner(key, grad):
        return grad.reshape(N_DEV, -1, D_MODEL)

    def view(key, a):
        return jnp.swapaxes(a, 1, 2) if key in TRANSPOSED else a

    blocks = {k: view(k, w_sh[k])[0].astype(BF16) for k in keys}
    first, mid, last = ("g1t", "u1t", "d1"), ("int", "out"), ("g2t", "u2t", "d2")
    w_first = {k: full(k, s) for k, s in zip(first, _all_gather([blocks[k] for k in first], name="gather_ffn1"))}
    flights = {}
    flights["ffn1"], token_mid = _copies_start([blocks[k] for k in mid], name="gather_mid_start", by_owner=False,
                                               after=w_first["d1"])
    flights["mix"], token_last = _copies_start([blocks[k] for k in last], name="gather_ffn2_start", by_owner=False,
                                               after=token_mid)

    def weights_after(stage, result):
        group = mid if stage == "ffn1" else last
        own, lands = _copies_wait(flights[stage], result, name="gather_" + stage + "_wait", by_owner=False)
        return {k: full(k, s) for k, s in zip(group, _with_own(lands, own, slot))}

    groups = {"mix": ("g2t", "u2t", "d2", "out"), "in": ("int",), "g1t": ("g1t",), "u1t": ("u1t",), "d1": ("d1",)}
    sent, sent_tokens = {}, []

    def grads_ready(stage, gw):
        stacks = [by_owner(k, gw[k]) for k in groups[stage]]
        flight, token = _copies_start(stacks, name="grads_" + stage + "_start", by_owner=True)
        sent[stage] = flight
        sent_tokens.append(token)
        return token

    norms = {"ffn1": ffn1_norm + token_last[0, 0], "mix": mix_norm, "sb": sb_out_norm, "hg": hg_out_norm,
             "ffn2": ffn2_norm, "final": final_norm.reshape(1, D_MODEL)}
    loss_row, grad_x, gw, gv = _local_step(x[0], loss_target[0], norms, hg_lower_bound_logits, w_first, weights_after,
                                           grads_ready)

    lb_row = jnp.concatenate([gv["lb"], jnp.zeros_like(gv["lb"])], axis=1)
    part = _vector_rows([gv["ffn1"], gv["mix"], gv["ffn2"], gv["final"], jnp.concatenate([gv["sb"], gv["hg"]], axis=1),
                         lb_row, None, loss_row])
    vec_w = _vector_rows(vectors(ffn1_norm, mix_norm, sb_out_norm, hg_lower_bound_logits, hg_out_norm, ffn2_norm, final_norm))
    vec_m = _vector_rows(vectors(m_ffn1_norm, m_mix_norm, m_sb_out_norm, m_hg_lower_bound_logits, m_hg_out_norm,
                                 m_ffn2_norm, m_final_norm))
    vec_v = _vector_rows(vectors(v_ffn1_norm, v_mix_norm, v_sb_out_norm, v_hg_lower_bound_logits, v_hg_out_norm,
                                 v_ffn2_norm, v_final_norm))
    updated, after = {}, sent_tokens[-1]
    for stage, flight in sent.items():
        if stage == list(sent)[-1]:
            *vecs, loss_out = _vectors_update(part, vec_w, vec_m, vec_v, name="vectors_update", tie=after)
            after = loss_out
        stacks, lands = _copies_wait(flight, after, name="grads_" + stage + "_wait", by_owner=True)
        own = [lax.dynamic_index_in_dim(s, slot, keepdims=False) for s in stacks]
        for k, part_k in zip(groups[stage], _with_own(lands, own, slot)):
            updated[k] = _sum_and_update(part_k, view(k, w_sh[k]), view(k, m_sh[k]), view(k, v_sh[k]), name="adamw_" + k,
                                         tie=after)
            after = updated[k][0]
    mats = [{k: view(k, updated[k][i]) for k in keys} for i in range(4)]

    def leaves(mat, vec):
        half = D_MODEL // 2
        return (
            vec[0:1], mat["g1t"], mat["u1t"], mat["d1"], vec[1:2], mat["int"], vec[4:5, :half],
            vec[ROW_LOGITS].reshape(2, half), vec[4:5, half:], mat["out"], vec[2:3], mat["g2t"], mat["u2t"],
            mat["d2"], vec[3],
        )

    out = [loss_out[0, 0], grad_x[None]]
    for mat, vec in zip(mats, vecs):
        out.extend(leaves(mat, vec))
    return tuple(out)
```

```python
import jax
import jax.numpy as jnp
from jax import lax
from jax.experimental import pallas as pl
from jax.experimental.pallas import tpu as pltpu

F32, BF16 = jnp.float32, jnp.bfloat16
D_MODEL = 1024
D_FF = 2816
SB_WIDTH = 512
HG_WIDTH = 512
SB_HEAD_DIM = 64
HG_HEAD_DIM = 128
IN_COLS = 3584
EPS = 1e-6
N_DEV = 8
LANES = 128
HG_CHUNK = 16
VMEM_LIMIT_BYTES = 48 * 1024 * 1024
FFN_BWD_VMEM_LIMIT_BYTES = 56 * 1024 * 1024
ADAM_LR, ADAM_B1, ADAM_B2, ADAM_EPS, ADAM_WD, ADAM_STEP = 0.001, 0.9, 0.999, 1e-08, 0.01, 10
MESH = pl.DeviceIdType.MESH


def _params(*semantics, vmem_limit_bytes=VMEM_LIMIT_BYTES):
    return pltpu.CompilerParams(dimension_semantics=semantics, vmem_limit_bytes=vmem_limit_bytes)


def _dot(a, b):
    return jnp.dot(a, b, preferred_element_type=F32)


def _dot_nt(a, b):
    return lax.dot_general(a, b, (((1,), (1,)), ((), ())), preferred_element_type=F32)


def _dot_tn(a, b):
    return lax.dot_general(a, b, (((0,), (0,)), ((), ())), preferred_element_type=F32)


def _split3(x):
    hi = x.astype(BF16)
    r1 = x - hi.astype(F32)
    mid = r1.astype(BF16)
    lo = (r1 - mid.astype(F32)).astype(BF16)
    return hi, mid, lo


def _rms(xv):
    rstd = lax.rsqrt(jnp.mean(xv * xv, axis=-1, keepdims=True) + EPS)
    return xv * rstd, rstd


def _sigmoid(x):
    return 0.5 + 0.5 * jnp.tanh(0.5 * x)


def _loss_terms(xv, gain, target):
    xhat, rstd = _rms(xv)
    err = xhat * gain - target
    loss = 0.5 * jnp.sum(jnp.mean(err * err, axis=-1, keepdims=True), axis=0, keepdims=True)
    dy = err * (1.0 / xv.shape[-1])
    dxh = dy * gain
    dx = rstd * (dxh - xhat * jnp.mean(dxh * xhat, axis=-1, keepdims=True))
    return dx, jnp.sum(dy * xhat, axis=0, keepdims=True), loss


def _mm(a, b, *, name, tm, tn, nt=False, ta=False, out_dtype=F32, tie=None):
    k, m = a.shape if ta else a.shape[::-1]
    n = b.shape[0] if nt else b.shape[1]
    assert m % tm == 0 and n % tn == 0 and not (nt and ta), (name, a.shape, b.shape, tm, tn)

    def body(a_ref, b_ref, *rest):
        av = a_ref[...].astype(BF16)
        bv = b_ref[...].astype(BF16)
        rest[-1][...] = (_dot_nt(av, bv) if nt else _dot_tn(av, bv) if ta else _dot(av, bv)).astype(out_dtype)

    in_specs = [
        pl.BlockSpec((k, tm), lambda i, j: (0, i)) if ta else pl.BlockSpec((tm, k), lambda i, j: (i, 0)),
        pl.BlockSpec((tn, k), lambda i, j: (j, 0)) if nt else pl.BlockSpec((k, tn), lambda i, j: (0, j)),
    ]
    operands = [a, b]
    if tie is not None:
        in_specs.append(pl.BlockSpec(memory_space=pl.ANY))
        operands.append(tie)
    return pl.pallas_call(
        body,
        name=name,
        grid=(m // tm, n // tn),
        in_specs=in_specs,
        out_specs=pl.BlockSpec((tm, tn), lambda i, j: (i, j)),
        out_shape=jax.ShapeDtypeStruct((m, n), out_dtype),
        compiler_params=_params("parallel", "parallel"),
    )(*operands)


def _ffn_fwd(x, gain, wgt, wut, wd, *, name, next_gain=None, head=None, tm=1024, tf=256):
    t = x.shape[0]
    nj = D_FF // tf
    extra_in = [] if next_gain is None else [next_gain]
    extra_in += [] if head is None else list(head)

    def body(x_ref, g_ref, wg_ref, wu_ref, wd_prev_ref, wd_last_ref, *rest):
        extra, (xo_ref, a_ref, b_ref, h_ref, st_ref) = rest[:len(extra_in)], rest[len(extra_in):len(extra_in) + 5]
        tail_out, (acc, s_prev) = rest[len(extra_in) + 5:-2], rest[-2:]
        i = pl.program_id(0)
        j = pl.program_id(1)

        @pl.when(j == 0)
        def _():
            xhat, _ = _rms(x_ref[...])
            h_ref[...] = (xhat * g_ref[...]).astype(BF16)
            acc[...] = jnp.zeros_like(acc)
            s_prev[...] = jnp.zeros_like(s_prev)

        acc[...] += _dot(s_prev[...], wd_prev_ref[...])
        h = h_ref[...]
        a = _dot_nt(h, wg_ref[...])
        b = _dot_nt(h, wu_ref[...])
        a_ref[...] = a.astype(BF16)
        b_ref[...] = b.astype(BF16)
        s = (a * _sigmoid(a) * b).astype(BF16)
        st_ref[...] = s
        s_prev[...] = s

        @pl.when(j == nj - 1)
        def _():
            xo = x_ref[...] + 0.5 * (acc[...] + _dot(s, wd_last_ref[...]))
            if head is None:
                xo_ref[...] = xo
            if next_gain is not None:
                tail_out[0][...] = (_rms(xo)[0] * extra[0][...]).astype(BF16)
            if head is not None:
                gain_ref, target_ref = extra[-2:]
                dg_ref, loss_ref = tail_out[-2:]
                xo_ref[...], part_g, part_loss = _loss_terms(xo, gain_ref[...], target_ref[...])

                @pl.when(i == 0)
                def _():
                    dg_ref[...] = part_g
                    loss_ref[...] = jnp.broadcast_to(part_loss, loss_ref.shape)

                @pl.when(i > 0)
                def _():
                    dg_ref[...] += part_g
                    loss_ref[...] += jnp.broadcast_to(part_loss, loss_ref.shape)

    row = pl.BlockSpec((tm, D_MODEL), lambda i, j: (i, 0))
    vec = pl.BlockSpec((1, D_MODEL), lambda i, j: (0, 0))
    tile = pl.BlockSpec((tm, tf), lambda i, j: (i, j))
    weights = pl.BlockSpec((tf, D_MODEL), lambda i, j: (j, 0))
    tail_specs = ([] if next_gain is None else [row]) + ([] if head is None else [vec, vec])
    tail_shapes = ([] if next_gain is None else [jax.ShapeDtypeStruct((t, D_MODEL), BF16)]) + (
        [] if head is None else [jax.ShapeDtypeStruct((1, D_MODEL), F32)] * 2)
    return pl.pallas_call(
        body,
        name=name,
        grid=(t // tm, nj),
        in_specs=[
            row, vec, weights, weights,
            pl.BlockSpec((tf, D_MODEL), lambda i, j: (jnp.maximum(j - 1, 0), 0)),
            pl.BlockSpec((tf, D_MODEL), lambda i, j: (nj - 1, 0)),
        ] + ([] if next_gain is None else [vec]) + ([] if head is None else [vec, row]),
        out_specs=[row, tile, tile, row, tile] + tail_specs,
        out_shape=[
            jax.ShapeDtypeStruct((t, D_MODEL), F32),
            jax.ShapeDtypeStruct((t, D_FF), BF16),
            jax.ShapeDtypeStruct((t, D_FF), BF16),
            jax.ShapeDtypeStruct((t, D_MODEL), BF16),
            jax.ShapeDtypeStruct((t, D_FF), BF16),
        ] + tail_shapes,
        scratch_shapes=[pltpu.VMEM((tm, D_MODEL), F32), pltpu.VMEM((tm, tf), BF16)],
        compiler_params=_params("arbitrary", "arbitrary"),
    )(x, gain, wgt, wut, wd, wd, *extra_in)


def _ffn_bwd(dout, x, gain, a, b, wgt, wut, wd, *, name, tm=1024, tf=256):
    t = x.shape[0]
    nj = D_FF // tf

    def body(do_ref, x_ref, g_ref, a_ref, b_ref, wg_prev_ref, wu_prev_ref, wg_last_ref, wu_last_ref, wd_ref,
             dx_ref, dg_ref, da_ref, db_ref, dob_ref, dob_scr, dh, da_prev, db_prev):
        i = pl.program_id(0)
        j = pl.program_id(1)

        @pl.when(j == 0)
        def _():
            d = (0.5 * do_ref[...]).astype(BF16)
            dob_scr[...] = d
            dob_ref[...] = d
            dh[...] = jnp.zeros_like(dh)
            da_prev[...] = jnp.zeros_like(da_prev)
            db_prev[...] = jnp.zeros_like(db_prev)

        dh[...] += _dot(da_prev[...], wg_prev_ref[...]) + _dot(db_prev[...], wu_prev_ref[...])
        ds = _dot_nt(dob_scr[...], wd_ref[...])
        av = a_ref[...].astype(F32)
        bv = b_ref[...].astype(F32)
        sig = _sigmoid(av)
        dbv = (ds * (av * sig)).astype(BF16)
        dav = (ds * bv * (sig * (1.0 + av * (1.0 - sig)))).astype(BF16)
        da_ref[...] = dav
        db_ref[...] = dbv
        da_prev[...] = dav
        db_prev[...] = dbv

        @pl.when(j == nj - 1)
        def _():
            xhat, rstd = _rms(x_ref[...])
            dhv = dh[...] + _dot(dav, wg_last_ref[...]) + _dot(dbv, wu_last_ref[...])
            part = jnp.sum(dhv * xhat, axis=0, keepdims=True)

            @pl.when(i == 0)
            def _():
                dg_ref[...] = part

            @pl.when(i > 0)
            def _():
                dg_ref[...] += part

            dxh = dhv * g_ref[...]
            dx_ref[...] = do_ref[...] + rstd * (dxh - xhat * jnp.mean(dxh * xhat, axis=-1, keepdims=True))

    return pl.pallas_call(
        body,
        name=name,
        grid=(t // tm, nj),
        in_specs=[
            pl.BlockSpec((tm, D_MODEL), lambda i, j: (i, 0)),
            pl.BlockSpec((tm, D_MODEL), lambda i, j: (i, 0)),
            pl.BlockSpec((1, D_MODEL), lambda i, j: (0, 0)),
            pl.BlockSpec((tm, tf), lambda i, j: (i, j)),
            pl.BlockSpec((tm, tf), lambda i, j: (i, j)),
            pl.BlockSpec((tf, D_MODEL), lambda i, j: (jnp.maximum(j - 1, 0), 0)),
            pl.BlockSpec((tf, D_MODEL), lambda i, j: (jnp.maximum(j - 1, 0), 0)),
            pl.BlockSpec((tf, D_MODEL), lambda i, j: (nj - 1, 0)),
            pl.BlockSpec((tf, D_MODEL), lambda i, j: (nj - 1, 0)),
            pl.BlockSpec((tf, D_MODEL), lambda i, j: (j, 0)),
        ],
        out_specs=[
            pl.BlockSpec((tm, D_MODEL), lambda i, j: (i, 0)),
            pl.BlockSpec((1, D_MODEL), lambda i, j: (0, 0)),
            pl.BlockSpec((tm, tf), lambda i, j: (i, j)),
            pl.BlockSpec((tm, tf), lambda i, j: (i, j)),
            pl.BlockSpec((tm, D_MODEL), lambda i, j: (i, 0)),
        ],
        out_shape=[
            jax.ShapeDtypeStruct((t, D_MODEL), F32),
            jax.ShapeDtypeStruct((1, D_MODEL), F32),
            jax.ShapeDtypeStruct((t, D_FF), BF16),
            jax.ShapeDtypeStruct((t, D_FF), BF16),
            jax.ShapeDtypeStruct((t, D_MODEL), BF16),
        ],
        scratch_shapes=[pltpu.VMEM((tm, D_MODEL), BF16), pltpu.VMEM((tm, D_MODEL), F32), pltpu.VMEM((tm, tf), BF16),
                        pltpu.VMEM((tm, tf), BF16)],
        compiler_params=_params("arbitrary", "arbitrary", vmem_limit_bytes=FFN_BWD_VMEM_LIMIT_BYTES),
    )(dout, x, gain, a, b, wgt, wut, wgt, wut, wd)


def _in_proj_bwd(dproj, w_int, x, gain, dres, *, name, tm=512):
    t, k = dproj.shape

    def body(dp_ref, w_ref, x_ref, g_ref, dr_ref, dx_ref, dg_ref):
        i = pl.program_id(0)
        dhv = _dot(dp_ref[...], w_ref[...])
        xhat, rstd = _rms(x_ref[...])
        part = jnp.sum(dhv * xhat, axis=0, keepdims=True)

        @pl.when(i == 0)
        def _():
            dg_ref[...] = part

        @pl.when(i > 0)
        def _():
            dg_ref[...] += part

        dxh = dhv * g_ref[...]
        dx_ref[...] = dr_ref[...] + rstd * (dxh - xhat * jnp.mean(dxh * xhat, axis=-1, keepdims=True))

    row = pl.BlockSpec((tm, D_MODEL), lambda i: (i, 0))
    vec = pl.BlockSpec((1, D_MODEL), lambda i: (0, 0))
    return pl.pallas_call(
        body,
        name=name,
        grid=(t // tm,),
        in_specs=[pl.BlockSpec((tm, k), lambda i: (i, 0)), pl.BlockSpec((k, D_MODEL), lambda i: (0, 0)), row, vec, row],
        out_specs=[row, vec],
        out_shape=[jax.ShapeDtypeStruct((t, D_MODEL), F32), jax.ShapeDtypeStruct((1, D_MODEL), F32)],
        compiler_params=_params("arbitrary"),
    )(dproj, w_int, x, gain, dres)


ATT_Q_TILE = 512
ATT_K_BLOCK = 256


def _first_head_lanes():
    return lax.broadcasted_iota(jnp.int32, (1, LANES), 1) < SB_HEAD_DIM


def _stack_heads(x):
    first = _first_head_lanes()
    return jnp.concatenate([jnp.where(first, x, 0.0), jnp.where(first, 0.0, x)], axis=0)


def _unstack_heads(x, rows):
    return jnp.where(_first_head_lanes(), x[:rows], x[rows:])


def _rows_from(x, first, rows):
    return x if first == 0 else jnp.concatenate([x[first:rows], x[rows + first:]], axis=0)


def _rows_into(full, part, first, rows):
    if first == 0:
        return part
    n = rows - first
    return jnp.concatenate([full[:first], part[:n], full[rows:rows + first], part[n:]], axis=0)


def _tri(n, relation):
    r = lax.broadcasted_iota(jnp.int32, (n, n), 0)
    c = lax.broadcasted_iota(jnp.int32, (n, n), 1)
    return relation(r, c).astype(BF16)


def _scan_dot(x, tri):
    hi = x.astype(BF16)
    lo = (x - hi.astype(F32)).astype(BF16)
    return _dot(jnp.concatenate([hi, lo], axis=1), jnp.concatenate([tri, tri], axis=0))


def _log_terms(z):
    lbeta = jnp.minimum(z, 0.0) - jnp.log(1.0 + jnp.exp(-jnp.abs(z)))
    return lbeta, lbeta - z


def _attn_fwd(proj, *, name):
    t = proj.shape[0]
    tq, tk = ATT_Q_TILE, ATT_K_BLOCK
    diag = tq // tk
    n_pairs = SB_WIDTH // LANES

    def body(q_ref, k_ref, v_ref, o_ref, kept_ref):
        qi = pl.program_id(1)
        q = q_ref[...] * (SB_HEAD_DIM ** -0.5)
        qs = _stack_heads(q).astype(BF16)
        tri = _tri(tk, lambda j, s: j > s)
        trow = lax.broadcasted_iota(jnp.int32, (tq, tk), 0)
        scol = lax.broadcasted_iota(jnp.int32, (tq, tk), 1)

        def block(kb, carry, causal, first=0):
            acc, c = carry
            off = pl.multiple_of(kb * tk, tk)
            lbeta, lrest = _log_terms(_dot_nt(_rows_from(qs, first, tq), k_ref[pl.ds(off, tk), :].astype(BF16)))
            if causal is not None:
                lrest = jnp.where(causal, lrest, 0.0)
            w = jnp.exp(lbeta + (_scan_dot(lrest, tri) + _rows_from(c, first, tq)))
            if causal is not None:
                w = jnp.where(causal, w, 0.0)
            wb = w.astype(BF16)
            kept_ref[0, 0, kb] = _rows_into(jnp.zeros((2 * tq, tk), BF16), wb, first, tq)
            acc = _rows_into(acc, _rows_from(acc, first, tq) + _dot(wb, v_ref[pl.ds(off, tk), :].astype(BF16)), first, tq)
            return acc, _rows_into(c, _rows_from(c, first, tq) + jnp.sum(lrest, axis=1, keepdims=True), first, tq)

        carry = (jnp.zeros((2 * tq, LANES), F32), jnp.zeros((2 * tq, 1), F32))
        n_full = qi * diag
        for j in reversed(range(diag)):
            mask = ((scol + j * tk) < trow)[j * tk:]
            carry = block(n_full + j, carry, jnp.concatenate([mask, mask], axis=0), first=j * tk)

        def odd_tile(carry):
            for j in range(diag):
                carry = block(n_full - 1 - j, carry, None)
            return carry

        carry = lax.cond(qi % 2 == 1, odd_tile, lambda c: c, carry)
        last = n_full - 1 - (qi % 2) * diag

        def step(it, carry):
            for j in range(2 * diag):
                carry = block(last - (2 * diag * it + j), carry, None)
            return carry

        acc, _ = lax.fori_loop(0, qi // 2, step, carry)
        o_ref[...] = _unstack_heads(acc, tq)

    return pl.pallas_call(
        body,
        name=name,
        grid=(n_pairs, t // tq),
        in_specs=[
            pl.BlockSpec((tq, LANES), lambda p, i: (i, p)),
            pl.BlockSpec((t, LANES), lambda p, i: (0, n_pairs + p)),
            pl.BlockSpec((t, LANES), lambda p, i: (0, 2 * n_pairs + p)),
        ],
        out_specs=[pl.BlockSpec((tq, LANES), lambda p, i: (i, p)),
                   pl.BlockSpec((1, 1, t // tk, 2 * tq, tk), lambda p, i: (p, i, 0, 0, 0))],
        out_shape=[jax.ShapeDtypeStruct((t, SB_WIDTH), F32),
                   jax.ShapeDtypeStruct((n_pairs, t // tq, t // tk, 2 * tq, tk), BF16)],
        compiler_params=_params("parallel", "parallel"),
    )(proj, proj, proj)


def _attn_bwd(proj, kept, do, *, name, tie=None):
    t = proj.shape[0]
    tq, tk = ATT_Q_TILE, ATT_K_BLOCK
    diag = tq // tk
    n_pairs = SB_WIDTH // LANES
    scale = SB_HEAD_DIM ** -0.5

    def body(q_ref, k_ref, v_ref, kept_ref, do_ref, *rest):
        dq_ref, dk_ref, dv_ref = rest[-3:]
        qi = pl.program_id(1)

        @pl.when(qi == 0)
        def _():
            dk_ref[...] = jnp.zeros_like(dk_ref)
            dv_ref[...] = jnp.zeros_like(dv_ref)

        qs = _stack_heads(q_ref[...] * scale).astype(BF16)
        dos = _stack_heads(do_ref[...]).astype(BF16)
        before = _tri(tk, lambda s, j: s < j)
        trow = lax.broadcasted_iota(jnp.int32, (tq, tk), 0)
        scol = lax.broadcasted_iota(jnp.int32, (tq, tk), 1)

        def block(kb, carry, causal, first=0):
            dq, cg = carry
            off = pl.multiple_of(kb * tk, tk)
            q_rows, do_rows = _rows_from(qs, first, tq), _rows_from(dos, first, tq)
            wb = _rows_from(kept_ref[0, 0, kb], first, tq)
            kblk = k_ref[pl.ds(off, tk), :].astype(BF16)
            sig = _sigmoid(_dot_nt(q_rows, kblk))
            g = wb.astype(F32) * _dot_nt(do_rows, v_ref[pl.ds(off, tk), :].astype(BF16))
            prior = _scan_dot(g, before) + _rows_from(cg, first, tq)
            dz = g - sig * (g + prior)
            if causal is not None:
                dz = jnp.where(causal, dz, 0.0)
            dzb = dz.astype(BF16)
            dq = _rows_into(dq, _rows_from(dq, first, tq) + _dot(dzb, kblk), first, tq)
            dk_ref[pl.ds(off, tk), :] += _dot_tn(dzb, q_rows)
            dv_ref[pl.ds(off, tk), :] += _dot_tn(wb, do_rows)
            return dq, _rows_into(cg, _rows_from(cg, first, tq) + jnp.sum(g, axis=1, keepdims=True), first, tq)

        n_full = qi * diag

        def step(it, carry):
            for j in range(2 * diag):
                carry = block(2 * diag * it + j, carry, None)
            return carry

        def odd_tile(carry):
            for j in range(diag):
                carry = block(n_full - diag + j, carry, None)
            return carry

        carry = lax.fori_loop(0, qi // 2, step, (jnp.zeros((2 * tq, LANES), F32), jnp.zeros((2 * tq, 1), F32)))
        carry = lax.cond(qi % 2 == 1, odd_tile, lambda c: c, carry)
        for j in range(diag):
            mask = ((scol + j * tk) < trow)[j * tk:]
            carry = block(n_full + j, carry, jnp.concatenate([mask, mask], axis=0), first=j * tk)
        dq_ref[...] = (_unstack_heads(carry[0], tq) * scale).astype(BF16)

    tile_spec = pl.BlockSpec((tq, LANES), lambda p, i: (i, p))
    full_spec = pl.BlockSpec((t, LANES), lambda p, i: (0, p))
    return pl.pallas_call(
        body,
        name=name,
        grid=(n_pairs, t // tq),
        in_specs=[
            tile_spec,
            pl.BlockSpec((t, LANES), lambda p, i: (0, n_pairs + p)),
            pl.BlockSpec((t, LANES), lambda p, i: (0, 2 * n_pairs + p)),
            pl.BlockSpec((1, 1, t // tk, 2 * tq, tk), lambda p, i: (p, i, 0, 0, 0)),
            tile_spec,
        ] + ([] if tie is None else [pl.BlockSpec(memory_space=pl.ANY)]),
        out_specs=[tile_spec, full_spec, full_spec],
        out_shape=[jax.ShapeDtypeStruct((t, SB_WIDTH), BF16)] + [jax.ShapeDtypeStruct((t, SB_WIDTH), F32)] * 2,
        compiler_params=_params("arbitrary", "arbitrary"),
    )(proj, proj, proj, kept, do, *([] if tie is None else [tie]))


HG_BLOCK = 128
HG_HEADS = HG_WIDTH // HG_HEAD_DIM


def _chunk_mats(n):
    r = lax.broadcasted_iota(jnp.int32, (n, n), 0)
    c = lax.broadcasted_iota(jnp.int32, (n, n), 1)
    same = (r // HG_CHUNK) == (c // HG_CHUNK)
    upto = (same & (c <= r)).astype(BF16)
    whole = same.astype(BF16)
    onward = (same & (c >= r)).astype(BF16)
    return upto, whole, onward


def _rows_dot(mat, x):
    return _dot(jnp.concatenate([mat, mat, mat], axis=1), jnp.concatenate(_split3(x), axis=0))


def _split_heads(x):
    return jnp.stack([x[:, h * HG_HEAD_DIM:(h + 1) * HG_HEAD_DIM] for h in range(HG_HEADS)], axis=0)


def _merge_heads(x):
    return jnp.concatenate([x[h] for h in range(HG_HEADS)], axis=1)


def _lower_bound(lg_ref):
    lg = lg_ref[...]
    return _sigmoid(lg[0:1, :] - lg[1:2, :])


def _hgrn_prepare(q_ref, f_ref, lb, h, upto, whole):
    cols = slice(h * HG_HEAD_DIM, (h + 1) * HG_HEAD_DIM)
    lbh = lb[:, cols]
    sg = _sigmoid(f_ref[:, cols])
    forget = lbh + (1.0 - lbh) * sg
    logf = jnp.log(forget)
    kk = (1.0 - lbh) * (1.0 - sg)
    qv = q_ref[:, cols]
    qsig = _sigmoid(qv)
    qh = qv * qsig
    b = _rows_dot(upto, logf)
    blast = _rows_dot(whole, logf)
    return dict(lbh=lbh, sg=sg, forget=forget, kk=kk, qv=qv, qsig=qsig, qh=qh, b=b, eb=jnp.exp(b),
                ekb=jnp.exp(blast - b), dl=jnp.exp(blast))


def _hgrn_fwd(proj, logits, *, name):
    t = proj.shape[0]
    tb = HG_BLOCK
    nc = tb // HG_CHUNK
    hd = HG_HEAD_DIM

    def body(q_ref, f_ref, i_ref, lg_ref, o_ref, st_ref, state, qh_s, kk_s, b_s, qe_s, ke_s, dl_s):
        @pl.when(pl.program_id(0) == 0)
        def _():
            state[...] = jnp.zeros_like(state)

        lb = _lower_bound(lg_ref)
        upto, whole, _ = _chunk_mats(tb)
        for h in range(HG_HEADS):
            p = _hgrn_prepare(q_ref, f_ref, lb, h, upto, whole)
            qh_s[h] = p["qh"]
            kk_s[h] = p["kk"]
            b_s[h] = p["b"]
            qe_s[h] = (p["qh"] * p["eb"]).astype(BF16)
            ke_s[h] = (p["kk"] * p["ekb"]).astype(BF16)
            dl_s[h] = p["dl"]
        rowi = lax.broadcasted_iota(jnp.int32, (HG_HEADS, HG_CHUNK, hd), 1)

        def chunk(c, _):
            r0 = pl.multiple_of(c * HG_CHUNK, HG_CHUNK)
            rows = pl.ds(r0, HG_CHUNK)
            bc = b_s[:, rows, :]
            qc = qh_s[:, rows, :]
            kc = kk_s[:, rows, :]
            vc = _split_heads(i_ref[rows, :])
            s_in = state[...]
            st_ref[c] = s_in
            s_in_b = s_in.astype(BF16)
            qe = qe_s[:, rows, :]
            o = jnp.stack([_dot_nt(qe[h], s_in_b[h]) for h in range(HG_HEADS)], axis=0)
            for s in range(HG_CHUNK):
                pair = jnp.where(rowi >= s, qc * jnp.exp(bc - bc[:, s:s + 1, :]) * kc[:, s:s + 1, :], 0.0)
                o = o + jnp.sum(pair, axis=2, keepdims=True) * vc[:, s:s + 1, :]
            o_ref[rows, :] = _merge_heads(o)
            vcb = vc.astype(BF16)
            ke = ke_s[:, rows, :]
            update = jnp.stack([_dot_tn(vcb[h], ke[h]) for h in range(HG_HEADS)], axis=0)
            state[...] = s_in * dl_s[:, pl.ds(r0, 1), :] + update
            return 0

        lax.fori_loop(0, nc, chunk, 0, unroll=4)

    blk =lambda col: pl.BlockSpec((tb, HG_WIDTH), lambda i: (i, col))
    head_f32 = pltpu.VMEM((HG_HEADS, tb, hd), F32)
    head_bf16 = pltpu.VMEM((HG_HEADS, tb, hd), BF16)
    return pl.pallas_call(
        body,
        name=name,
        grid=(t // tb,),
        in_specs=[blk(3), blk(4), blk(5), pl.BlockSpec((2, HG_WIDTH), lambda i: (0, 0))],
        out_specs=[
            pl.BlockSpec((tb, HG_WIDTH), lambda i: (i, 0)),
            pl.BlockSpec((nc, HG_HEADS, hd, hd), lambda i: (i, 0, 0, 0)),
        ],
        out_shape=[
            jax.ShapeDtypeStruct((t, HG_WIDTH), F32),
            jax.ShapeDtypeStruct((t // HG_CHUNK, HG_HEADS, hd, hd), F32),
        ],
        scratch_shapes=[pltpu.VMEM((HG_HEADS, hd, hd), F32), head_f32, head_f32, head_f32, head_bf16, head_bf16,
                        head_f32],
        compiler_params=_params("arbitrary"),
    )(proj, proj, proj, logits)


def _hgrn_bwd(proj, logits, states, do, *, name):
    t = proj.shape[0]
    tb = HG_BLOCK
    nb = t // tb
    nc = tb // HG_CHUNK
    hd = HG_HEAD_DIM

    def body(q_ref, f_ref, i_ref, lg_ref, st_ref, do_ref, dq_ref, df_ref, di_ref, dlb_ref,
             dstate, qh_s, kk_s, b_s, eb_s, ekb_s, qe_s, ke_s, dl_s, dqh_s, dkk_s, dlf_s):
        step = pl.program_id(0)

        @pl.when(step == 0)
        def _():
            dstate[...] = jnp.zeros_like(dstate)
            dlb_ref[...] = jnp.zeros_like(dlb_ref)

        lb = _lower_bound(lg_ref)
        upto, whole, _ = _chunk_mats(tb)
        prepared = []
        for h in range(HG_HEADS):
            p = _hgrn_prepare(q_ref, f_ref, lb, h, upto, whole)
            prepared.append(p)
            qh_s[h] = p["qh"]
            kk_s[h] = p["kk"]
            b_s[h] = p["b"]
            eb_s[h] = p["eb"]
            ekb_s[h] = p["ekb"]
            qe_s[h] = (p["qh"] * p["eb"]).astype(BF16)
            ke_s[h] = (p["kk"] * p["ekb"]).astype(BF16)
            dl_s[h] = p["dl"]
        rowi = lax.broadcasted_iota(jnp.int32, (HG_CHUNK, hd), 0)
        r16 = lax.broadcasted_iota(jnp.int32, (HG_CHUNK, HG_CHUNK), 0)
        c16 = lax.broadcasted_iota(jnp.int32, (HG_CHUNK, HG_CHUNK), 1)
        onward = (c16 >= r16).astype(BF16)

        def chunk(it, _):
            c = nc - 1 - it
            r0 = pl.multiple_of(c * HG_CHUNK, HG_CHUNK)
            rows = pl.ds(r0, HG_CHUNK)
            for h in range(HG_HEADS):
                cols = slice(h * hd, (h + 1) * hd)
                bc = b_s[h, rows, :]
                qc = qh_s[h, rows, :]
                kc = kk_s[h, rows, :]
                vc = i_ref[rows, cols]
                doc = do_ref[rows, cols]
                s_in = st_ref[c, h]
                ds_out = dstate[h]
                ds_out_b = ds_out.astype(BF16)
                docb = doc.astype(BF16)
                dl_row = dl_s[h, pl.ds(r0, 1), :]
                dqh = _dot(docb, s_in.astype(BF16)) * eb_s[h, rows, :]
                dkk = _dot(vc.astype(BF16), ds_out_b) * ekb_s[h, rows, :]
                dv = _dot_nt(ke_s[h, rows, :], ds_out_b)
                db = dqh * qc - dkk * kc
                dwhole = jnp.sum(dkk * kc, axis=0, keepdims=True) + jnp.sum(ds_out * s_in, axis=0, keepdims=True) * dl_row
                dk_rows, dv_rows = [], []
                for s in range(HG_CHUNK):
                    keep = rowi >= s
                    e = jnp.exp(bc - bc[s:s + 1, :])
                    k_row = kc[s:s + 1, :]
                    pcol = jnp.sum(jnp.where(keep, qc * e * k_row, 0.0), axis=1, keepdims=True)
                    dpcol = jnp.sum(doc * vc[s:s + 1, :], axis=1, keepdims=True)
                    m = jnp.where(keep, e * dpcol, 0.0)
                    y = m * qc
                    dqh = dqh + m * k_row
                    db = db + y * k_row
                    dk_rows.append(jnp.sum(y, axis=0, keepdims=True))
                    dv_rows.append(jnp.sum(pcol * doc, axis=0, keepdims=True))
                dkk_pairs = jnp.concatenate(dk_rows, axis=0)
                dkk = dkk + dkk_pairs
                db = db - dkk_pairs * kc
                dv = dv + jnp.concatenate(dv_rows, axis=0)
                dqh_s[h, rows, :] = dqh
                dkk_s[h, rows, :] = dkk
                dlf_s[h, rows, :] = _rows_dot(onward, db) + dwhole
                di_ref[rows, cols] = dv.astype(BF16)
                dstate[h] = ds_out * dl_row + _dot_tn(docb, qe_s[h, rows, :])
            return 0

        lax.fori_loop(0, nc, chunk, 0, unroll=4)
        for h in range(HG_HEADS):
            cols = slice(h * hd, (h + 1) * hd)
            p = prepared[h]
            dq_ref[:, cols] = (dqh_s[h] * (p["qsig"] * (1.0 + p["qv"] * (1.0 - p["qsig"])))).astype(BF16)
            dforget = dlf_s[h] / p["forget"] - dkk_s[h]
            df_ref[:, cols] = (dforget * (1.0 - p["lbh"]) * p["sg"] * (1.0 - p["sg"])).astype(BF16)
            dlb_ref[:, cols] += jnp.sum(dforget * (1.0 - p["sg"]), axis=0, keepdims=True)

    blk = lambda col: pl.BlockSpec((tb, HG_WIDTH), lambda i: (nb - 1 - i, col))
    vec = pl.BlockSpec((1, HG_WIDTH), lambda i: (0, 0))
    head_f32 = pltpu.VMEM((HG_HEADS, tb, hd), F32)
    head_bf16 = pltpu.VMEM((HG_HEADS, tb, hd), BF16)
    return pl.pallas_call(
        body,
        name=name,
        grid=(nb,),
        in_specs=[
            blk(3), blk(4), blk(5),
            pl.BlockSpec((2, HG_WIDTH), lambda i: (0, 0)),
            pl.BlockSpec((nc, HG_HEADS, hd, hd), lambda i: (nb - 1 - i, 0, 0, 0)),
            blk(0),
        ],
        out_specs=[blk(0), blk(0), blk(0), vec],
        out_shape=[jax.ShapeDtypeStruct((t, HG_WIDTH), BF16)] * 3 + [jax.ShapeDtypeStruct((1, HG_WIDTH), F32)],
        scratch_shapes=[
            pltpu.VMEM((HG_HEADS, hd, hd), F32),
            head_f32, head_f32, head_f32, head_f32, head_f32, head_bf16, head_bf16, head_f32,
            head_f32, head_f32, head_f32,
        ],
        compiler_params=_params("arbitrary"),
    )(proj, proj, proj, logits, states, do)


def _group_mat(width, head_dim):
    r = lax.broadcasted_iota(jnp.int32, (width, width), 0)
    c = lax.broadcasted_iota(jnp.int32, (width, width), 1)
    return ((r // head_dim) == (c // head_dim)).astype(BF16)


def _head_mean(x, mat, head_dim):
    hi = x.astype(BF16)
    lo = (x - hi.astype(F32)).astype(BF16)
    return (_dot(hi, mat) + _dot(lo, mat)) * (1.0 / head_dim)


def _mix_out_fwd(o_sb, o_hg, proj, g_sb, g_hg, w_out, x1, *, name, tm=512):
    t = x1.shape[0]

    def body(osb_ref, ohg_ref, gate_ref, gsb_ref, ghg_ref, w_ref, x_ref, xo_ref, mt_ref):
        msb = _group_mat(SB_WIDTH, SB_HEAD_DIM)
        mhg = _group_mat(HG_WIDTH, HG_HEAD_DIM)
        osb = osb_ref[...]
        ohg = ohg_ref[...]
        nsb = osb * lax.rsqrt(_head_mean(osb * osb, msb, SB_HEAD_DIM) + EPS) * gsb_ref[...]
        gate = gate_ref[...]
        nhg = ohg * lax.rsqrt(_head_mean(ohg * ohg, mhg, HG_HEAD_DIM) + EPS) * ghg_ref[...] * (gate * _sigmoid(gate))
        mixed = jnp.concatenate([nsb, nhg], axis=1).astype(BF16)
        mt_ref[...] = mixed
        xo_ref[...] = x_ref[...] + _dot(mixed, w_ref[...])

    half = pl.BlockSpec((tm, SB_WIDTH), lambda i: (i, 0))
    vec = pl.BlockSpec((1, SB_WIDTH), lambda i: (0, 0))
    row = pl.BlockSpec((tm, D_MODEL), lambda i: (i, 0))
    return pl.pallas_call(
        body,
        name=name,
        grid=(t // tm,),
        in_specs=[half, half, pl.BlockSpec((tm, HG_WIDTH), lambda i: (i, 6)), vec, vec,
                  pl.BlockSpec((D_MODEL, D_MODEL), lambda i: (0, 0)), row],
        out_specs=[row, row],
        out_shape=[jax.ShapeDtypeStruct((t, D_MODEL), F32), jax.ShapeDtypeStruct((t, D_MODEL), BF16)],
        compiler_params=_params("parallel"),
    )(o_sb, o_hg, proj, g_sb, g_hg, w_out, x1)


def _mix_out_bwd(dx2, o_sb, o_hg, proj, g_sb, g_hg, w_out, *, name, tm=512):
    t = dx2.shape[0]

    def body(dx_ref, osb_ref, ohg_ref, gate_ref, gsb_ref, ghg_ref, w_ref, dosb_ref, dohg_ref, dgate_ref, dgsb_ref,
             dghg_ref, dxb_ref):
        i = pl.program_id(0)
        msb = _group_mat(SB_WIDTH, SB_HEAD_DIM)
        mhg = _group_mat(HG_WIDTH, HG_HEAD_DIM)
        dxb = dx_ref[...].astype(BF16)
        dxb_ref[...] = dxb
        dmixed = _dot_nt(dxb, w_ref[...])
        dnsb = dmixed[:, :SB_WIDTH]
        dy = dmixed[:, SB_WIDTH:]

        osb = osb_ref[...]
        rstd = lax.rsqrt(_head_mean(osb * osb, msb, SB_HEAD_DIM) + EPS)
        ohat = osb * rstd
        part_sb = jnp.sum(dnsb * ohat, axis=0, keepdims=True)
        dohat = dnsb * gsb_ref[...]
        dosb_ref[...] = rstd * (dohat - ohat * _head_mean(dohat * ohat, msb, SB_HEAD_DIM))

        ohg = ohg_ref[...]
        rstd = lax.rsqrt(_head_mean(ohg * ohg, mhg, HG_HEAD_DIM) + EPS)
        ohat = ohg * rstd
        gate = gate_ref[...]
        sig = _sigmoid(gate)
        dn = dy * (gate * sig)
        dgate_ref[...] = (dy * (ohat * ghg_ref[...]) * (sig * (1.0 + gate * (1.0 - sig)))).astype(BF16)
        part_hg = jnp.sum(dn * ohat, axis=0, keepdims=True)
        dohat = dn * ghg_ref[...]
        dohg_ref[...] = rstd * (dohat - ohat * _head_mean(dohat * ohat, mhg, HG_HEAD_DIM))

        @pl.when(i == 0)
        def _():
            dgsb_ref[...] = part_sb
            dghg_ref[...] = part_hg

        @pl.when(i > 0)
        def _():
            dgsb_ref[...] += part_sb
            dghg_ref[...] += part_hg

    half = pl.BlockSpec((tm, SB_WIDTH), lambda i: (i, 0))
    vec = pl.BlockSpec((1, SB_WIDTH), lambda i: (0, 0))
    row = pl.BlockSpec((tm, D_MODEL), lambda i: (i, 0))
    return pl.pallas_call(
        body,
        name=name,
        grid=(t // tm,),
        in_specs=[row, half, half, pl.BlockSpec((tm, HG_WIDTH), lambda i: (i, 6)), vec, vec,
                  pl.BlockSpec((D_MODEL, D_MODEL), lambda i: (0, 0))],
        out_specs=[half, half, half, vec, vec, row],
        out_shape=[jax.ShapeDtypeStruct((t, SB_WIDTH), F32)] * 2 + [jax.ShapeDtypeStruct((t, SB_WIDTH), BF16)]
        + [jax.ShapeDtypeStruct((1, SB_WIDTH), F32)] * 2 + [jax.ShapeDtypeStruct((t, D_MODEL), BF16)],
        compiler_params=_params("arbitrary"),
    )(dx2, o_sb, o_hg, proj, g_sb, g_hg, w_out)


def _local_step(x, target, norms, logits, w, weights_after=None, grads_ready=None):
    w = dict(w)
    x1, a1, b1, h1, s1, hm = _ffn_fwd(x, norms["ffn1"], w["g1t"], w["u1t"], w["d1"], name="ffn1_fwd",
                                      next_gain=norms["mix"])
    if weights_after is not None:
        w.update(weights_after("ffn1", x1))
    proj = _mm(hm, w["int"], name="in_proj", tm=512, tn=IN_COLS, nt=True)
    o_sb, sb_kept = _attn_fwd(proj, name="sb_attn_fwd")
    o_hg, states = _hgrn_fwd(proj, logits, name="hgrn2_fwd")
    x2, mixed = _mix_out_fwd(o_sb, o_hg, proj, norms["sb"], norms["hg"], w["out"], x1, name="mix_out_fwd")
    if weights_after is not None:
        w.update(weights_after("mix", x2))
    dx3, a2, b2, h2, s2, d_final, loss_row = _ffn_fwd(x2, norms["ffn2"], w["g2t"], w["u2t"], w["d2"], name="ffn2_fwd",
                                                      head=(norms["final"], target))

    def weight_grad(lhs, rhs, name, tie=None):
        return _mm(lhs, rhs, name=name, tm=256, tn=D_MODEL, ta=True, out_dtype=BF16, tie=tie)

    def sent(stage):
        return grads_ready(stage, gw) if grads_ready is not None else None

    gw, gv = {}, {"final": d_final}
    dx2, gv["ffn2"], da2, db2, dob2 = _ffn_bwd(dx3, x2, norms["ffn2"], a2, b2, w["g2t"], w["u2t"], w["d2"],
                                               name="ffn2_bwd")
    gw["g2t"] = weight_grad(da2, h2, "ffn2_dgate")
    gw["u2t"] = weight_grad(db2, h2, "ffn2_dup")
    gw["d2"] = weight_grad(s2, dob2, "ffn2_ddown")

    do_sb, do_hg, d_gate, gv["sb"], gv["hg"], dx2b = _mix_out_bwd(
        dx2, o_sb, o_hg, proj, norms["sb"], norms["hg"], w["out"], name="mix_out_bwd")
    gw["out"] = weight_grad(mixed, dx2b, "out_dw")
    tie = sent("mix")
    dq_sb, dk_sb, dv_sb = _attn_bwd(proj, sb_kept, do_sb, name="sb_attn_bwd", tie=tie)
    dq_hg, df_hg, di_hg, d_lb = _hgrn_bwd(proj, logits if tie is None else logits + tie[0, 0], states, do_hg,
                                          name="hgrn2_bwd")
    dproj = jnp.concatenate([dq_sb, dk_sb.astype(BF16), dv_sb.astype(BF16), dq_hg, df_hg, di_hg, d_gate], axis=1)
    gw["int"] = weight_grad(dproj, hm, "in_dw")
    tie = sent("in")
    dx1, gv["mix"] = _in_proj_bwd(dproj, w["int"], x1, norms["mix"] if tie is None else norms["mix"] + tie[0, 0], dx2,
                                  name="in_dx")

    dx, gv["ffn1"], da1, db1, dob1 = _ffn_bwd(dx1, x, norms["ffn1"], a1, b1, w["g1t"], w["u1t"], w["d1"],
                                              name="ffn1_bwd")
    gw["g1t"] = weight_grad(da1, h1, "ffn1_dgate")
    gw["u1t"] = weight_grad(db1, h1, "ffn1_dup", tie=sent("g1t"))
    gw["d1"] = weight_grad(s1, dob1, "ffn1_ddown", tie=sent("u1t"))
    sent("d1")
    gv["lb"] = d_lb
    return loss_row, dx, gw, gv


HBM = pl.BlockSpec(memory_space=pl.ANY)


def _place():
    return lax.axis_index("x"), lax.axis_index("y"), lax.axis_index("c")


def _slot(px, py, pc):
    return 4 * px + 2 * py + pc


GATHER_COPIES = 8


def _all_gather(blocks, *, name):
    n = len(blocks)

    def body(*refs):
        ins, outs = refs[:n], refs[n:2 * n]
        send_sems, recv_sems, local_sems = refs[2 * n:]
        x, y, c = _place()
        me, sibling = (x, y, c), (x, y, 1 - c)
        beside, across, diagonal = (1 - x, y, c), (x, 1 - y, c), (1 - x, 1 - y, c)

        def copy(a, k, block, to, src=None, half=None):
            dst = outs[a].at[_slot(*block)]
            if half is not None:
                rows = blocks[a].shape[0] // 2
                dst = dst.at[pl.ds(half * rows, rows)]
            return pltpu.make_async_remote_copy(
                src_ref=dst if src is None else src, dst_ref=dst, send_sem=send_sems.at[GATHER_COPIES * a + k],
                recv_sem=recv_sems.at[GATHER_COPIES * a + k], device_id=to, device_id_type=MESH)

        mine = [pltpu.make_async_copy(ins[a], outs[a].at[_slot(*me)], local_sems.at[a]) for a in range(n)]
        for cp in mine:
            cp.start()
        sent = []
        for a in range(n):
            sent += [copy(a, 0, me, sibling, src=ins[a]), copy(a, 1, me, beside, src=ins[a]),
                     copy(a, 2, me, across, src=ins[a])]
        for cp in sent:
            cp.start()
        for a in range(n):
            copy(a, 1, beside, me).wait_recv()
            sent += [copy(a, 3, beside, across, half=0), copy(a, 5, beside, sibling)]
            sent[-2].start()
            sent[-1].start()
        for a in range(n):
            copy(a, 2, across, me).wait_recv()
            sent += [copy(a, 4, across, beside, half=1), copy(a, 6, across, sibling)]
            sent[-2].start()
            sent[-1].start()
        for a in range(n):
            copy(a, 3, diagonal, me, half=0).wait_recv()
            copy(a, 4, diagonal, me, half=1).wait_recv()
            sent.append(copy(a, 7, diagonal, sibling))
            sent[-1].start()
        for a in range(n):
            for k, origin in ((0, sibling), (5, (1 - x, y, 1 - c)), (6, (x, 1 - y, 1 - c)), (7, (1 - x, 1 - y, 1 - c))):
                copy(a, k, origin, me).wait_recv()
        for cp in sent:
            cp.wait_send()
        for cp in mine:
            cp.wait()

    return pl.pallas_call(
        body,
        name=name,
        in_specs=[HBM] * n,
        out_specs=[HBM] * n,
        out_shape=[jax.ShapeDtypeStruct((N_DEV,) + b.shape, b.dtype) for b in blocks],
        scratch_shapes=[pltpu.SemaphoreType.DMA((GATHER_COPIES * n,)), pltpu.SemaphoreType.DMA((GATHER_COPIES * n,)),
                        pltpu.SemaphoreType.DMA((n,))],
    )(*blocks)


def _flipped(place, d):
    return tuple(1 - p if (d >> (2 - axis)) & 1 else p for axis, p in enumerate(place))


SEM = pl.BlockSpec(memory_space=pltpu.SEMAPHORE)
EFFECT = pltpu.SideEffectType.DATAFLOW_SIDE_EFFECTING


def _split_copies(me, srcs, lands, send_sems, recv_sems, by_owner):
    copies = []
    for d in range(1, N_DEV):
        peer = _flipped(me, d)
        for a, (src, land) in enumerate(zip(srcs, lands)):
            copies.append(pltpu.make_async_remote_copy(
                src_ref=src.at[_slot(*peer)] if by_owner else src, dst_ref=land.at[_slot(*me)],
                send_sem=send_sems.at[7 * a + d - 1], recv_sem=recv_sems.at[7 * a + d - 1], device_id=peer,
                device_id_type=MESH))
    return copies


def _copies_start(srcs, *, name, by_owner, after=None):
    n = len(srcs)
    extra = [] if after is None else [after]
    land_shapes = [s.shape if by_owner else (N_DEV,) + s.shape for s in srcs]
    lands = [pltpu.with_memory_space_constraint(lax.empty(shape, s.dtype), pltpu.HBM) for shape, s in zip(land_shapes, srcs)]
    srcs = [pltpu.with_memory_space_constraint(s, pltpu.HBM) for s in srcs]

    def body(*refs):
        src_refs, land_refs = refs[:n], refs[n:2 * n]
        send_sems, recv_sems = refs[2 * n + len(extra)], refs[2 * n + len(extra) + 1]
        token = refs[-1]
        for cp in _split_copies(_place(), src_refs, land_refs, send_sems, recv_sems, by_owner):
            cp.start()
        token[...] = jnp.zeros_like(token)

    out = pl.pallas_call(
        body,
        name=name,
        in_specs=[HBM] * (2 * n + len(extra)),
        out_specs=[SEM, SEM] + [HBM] * (2 * n) + [pl.BlockSpec(memory_space=pltpu.VMEM)],
        out_shape=[pltpu.SemaphoreType.DMA((7 * n,)), pltpu.SemaphoreType.DMA((7 * n,))]
        + [pltpu.HBM(s.shape, s.dtype) for s in srcs] + [pltpu.HBM(shape, s.dtype) for shape, s in zip(land_shapes, srcs)]
        + [jax.ShapeDtypeStruct((8, LANES), F32)],
        input_output_aliases={i: 2 + i for i in range(2 * n)},
        compiler_params=pltpu.CompilerParams(has_side_effects=EFFECT),
    )(*srcs, *lands, *extra)
    return (out[0], out[1], out[2:2 + n], out[2 + n:2 + 2 * n]), out[-1]


def _copies_wait(started, after, *, name, by_owner):
    send_sems, recv_sems, srcs, lands = started
    n = len(srcs)

    def body(*refs):
        src_refs, land_refs = refs[:n], refs[n:2 * n]
        for cp in _split_copies(_place(), src_refs, land_refs, refs[2 * n], refs[2 * n + 1], by_owner):
            cp.wait_send()
            cp.wait_recv()

    out = pl.pallas_call(
        body,
        name=name,
        in_specs=[HBM] * (2 * n) + [SEM, SEM, HBM],
        out_specs=[HBM] * (2 * n),
        out_shape=[pltpu.HBM(s.shape, s.dtype) for s in srcs] + [pltpu.HBM(s.shape, s.dtype) for s in lands],
        input_output_aliases={i: i for i in range(2 * n)},
        compiler_params=pltpu.CompilerParams(has_side_effects=EFFECT),
    )(*srcs, *lands, send_sems, recv_sems, after)
    return out[:n], out[n:]


def _with_own(lands, own, slot):
    zero = jnp.zeros((), jnp.int32)
    return [lax.dynamic_update_slice(land, o[None], (slot.astype(jnp.int32),) + (zero,) * o.ndim)
            for land, o in zip(lands, own)]


def _adamw(w, g, m, v):
    m = ADAM_B1 * m + (1.0 - ADAM_B1) * g
    v = ADAM_B2 * v + (1.0 - ADAM_B2) * (g * g)
    m_hat = m / (1.0 - ADAM_B1 ** ADAM_STEP)
    v_hat = v / (1.0 - ADAM_B2 ** ADAM_STEP)
    delta = -ADAM_LR * (m_hat / (jnp.sqrt(v_hat) + ADAM_EPS) + ADAM_WD * w)
    return delta, m, v


def _sum_and_update(parts, w, m, v, *, name, tie=None):
    _, rows, cols = w.shape
    tr = rows // 2

    def body(p_ref, w_ref, m_ref, v_ref, *rest):
        g_ref, d_ref, mo_ref, vo_ref = rest[-4:]
        g = p_ref[0].astype(F32)
        for s in range(1, N_DEV):
            g = g + p_ref[s].astype(F32)
        g_ref[0] = g
        d_ref[0], mo_ref[0], vo_ref[0] = _adamw(w_ref[0], g, m_ref[0], v_ref[0])

    flat = pl.BlockSpec((1, tr, cols), lambda i: (0, i, 0))
    return pl.pallas_call(
        body,
        name=name,
        grid=(rows // tr,),
        in_specs=[pl.BlockSpec((N_DEV, tr, cols), lambda i: (0, i, 0)), flat, flat, flat]
        + ([] if tie is None else [pl.BlockSpec(memory_space=pl.ANY)]),
        out_specs=[flat] * 4,
        out_shape=[jax.ShapeDtypeStruct((1, rows, cols), F32)] * 4,
        compiler_params=_params("parallel"),
    )(parts, w, m, v, *([] if tie is None else [tie]))


VEC_ROWS = 8
ROW_LOGITS, ROW_LOSS = 5, 7


def _vectors_update(part, w, m, v, *, name, tie):
    def body(p_ref, w_ref, m_ref, v_ref, tie_ref, g_ref, d_ref, mo_ref, vo_ref, loss_ref, all_ref, send_sems, recv_sems):
        me = _place()
        all_ref[_slot(*me)] = p_ref[...]
        copies = []
        for d in range(1, N_DEV):
            peer = _flipped(me, d)
            copies.append(pltpu.make_async_remote_copy(
                src_ref=p_ref, dst_ref=all_ref.at[_slot(*me)], send_sem=send_sems.at[d - 1], recv_sem=recv_sems.at[d - 1],
                device_id=peer, device_id_type=MESH))
        for cp in copies:
            cp.start()
        for cp in copies:
            cp.wait()
        total = all_ref[0]
        for s in range(1, N_DEV):
            total = total + all_ref[s]
        wv = w_ref[...]
        half = D_MODEL // 2
        lb = _sigmoid(wv[ROW_LOGITS:ROW_LOGITS + 1, :half] - wv[ROW_LOGITS:ROW_LOGITS + 1, half:])
        d_first = total[ROW_LOGITS:ROW_LOGITS + 1, :half] * lb * (1.0 - lb)
        d_logits = jnp.concatenate([d_first, -d_first], axis=1)
        rowi = lax.broadcasted_iota(jnp.int32, (VEC_ROWS, D_MODEL), 0)
        g = jnp.where(rowi == ROW_LOGITS, d_logits, jnp.where(rowi < ROW_LOGITS, total, 0.0))
        g_ref[...] = g
        d_ref[...], mo_ref[...], vo_ref[...] = _adamw(wv, g, m_ref[...], v_ref[...])
        loss_ref[...] = total[ROW_LOSS:ROW_LOSS + 1, :]

    vmem = pl.BlockSpec(memory_space=pltpu.VMEM)
    return pl.pallas_call(
        body,
        name=name,
        in_specs=[vmem] * 4 + [HBM],
        out_specs=[vmem] * 5,
        out_shape=[jax.ShapeDtypeStruct((VEC_ROWS, D_MODEL), F32)] * 4 + [jax.ShapeDtypeStruct((1, D_MODEL), F32)],
        scratch_shapes=[pltpu.VMEM((N_DEV, VEC_ROWS, D_MODEL), F32), pltpu.SemaphoreType.DMA((7,)),
                        pltpu.SemaphoreType.DMA((7,))],
    )(part, w, m, v, tie)


TRANSPOSED = ("g1t", "u1t", "g2t", "u2t", "int")


def _vector_rows(rows):
    rowi = lax.broadcasted_iota(jnp.int32, (VEC_ROWS, D_MODEL), 0)
    out = jnp.zeros((VEC_ROWS, D_MODEL), F32)
    for i, r in enumerate(rows):
        if r is not None:
            out = jnp.where(rowi == i, r, out)
    return out


def kernel(x, ffn1_norm, ffn1_w_gate, ffn1_w_up, ffn1_w_down, mix_norm, w_in, sb_out_norm, hg_lower_bound_logits, hg_out_norm, w_out, ffn2_norm, ffn2_w_gate, ffn2_w_up, ffn2_w_down, final_norm, loss_target, m_ffn1_norm, m_ffn1_w_gate, m_ffn1_w_up, m_ffn1_w_down, m_mix_norm, m_w_in, m_sb_out_norm, m_hg_lower_bound_logits, m_hg_out_norm, m_w_out, m_ffn2_norm, m_ffn2_w_gate, m_ffn2_w_up, m_ffn2_w_down, m_final_norm, v_ffn1_norm, v_ffn1_w_gate, v_ffn1_w_up, v_ffn1_w_down, v_mix_norm, v_w_in, v_sb_out_norm, v_hg_lower_bound_logits, v_hg_out_norm, v_w_out, v_ffn2_norm, v_ffn2_w_gate, v_ffn2_w_up, v_ffn2_w_down, v_final_norm):
    def matrices(g1, u1, d1, win, wout, g2, u2, d2):
        return {"g1t": g1, "u1t": u1, "d1": d1, "int": win, "out": wout, "g2t": g2, "u2t": u2, "d2": d2}

    def vectors(n1, nm, nsb, lg, nhg, n2, nf):
        return [n1, nm, n2, nf.reshape(1, D_MODEL), jnp.concatenate([nsb, nhg], axis=1), lg.reshape(1, D_MODEL), None, None]

    w_sh = matrices(ffn1_w_gate, ffn1_w_up, ffn1_w_down, w_in, w_out, ffn2_w_gate, ffn2_w_up, ffn2_w_down)
    m_sh = matrices(m_ffn1_w_gate, m_ffn1_w_up, m_ffn1_w_down, m_w_in, m_w_out, m_ffn2_w_gate, m_ffn2_w_up, m_ffn2_w_down)
    v_sh = matrices(v_ffn1_w_gate, v_ffn1_w_up, v_ffn1_w_down, v_w_in, v_w_out, v_ffn2_w_gate, v_ffn2_w_up, v_ffn2_w_down)
    keys = list(w_sh)

    slot = _slot(*_place())

    def full(key, stack):
        return stack.reshape(-1, D_MODEL)

    def by_owner(key, grad):
        return grad.reshape(N_DEV, -1, D_MODEL)

    def view(key, a):
        return jnp.swapaxes(a, 1, 2) if key in TRANSPOSED else a

    blocks = {k: view(k, w_sh[k])[0].astype(BF16) for k in keys}
    first, mid, last = ("g1t", "u1t", "d1"), ("int", "out"), ("g2t", "u2t", "d2")
    w_first = {k: full(k, s) for k, s in zip(first, _all_gather([blocks[k] for k in first], name="gather_ffn1"))}
    flights = {}
    flights["ffn1"], token_mid = _copies_start([blocks[k] for k in mid], name="gather_mid_start", by_owner=False,
                                               after=w_first["d1"])
    flights["mix"], token_last = _copies_start([blocks[k] for k in last], name="gather_ffn2_start", by_owner=False,
                                               after=token_mid)

    def weights_after(stage, result):
        group = mid if stage == "ffn1" else last
        own, lands = _copies_wait(flights[stage], result, name="gather_" + stage + "_wait", by_owner=False)
        return {k: full(k, s) for k, s in zip(group, _with_own(lands, own, slot))}

    groups = {"mix": ("g2t", "u2t", "d2", "out"), "in": ("int",), "g1t": ("g1t",), "u1t": ("u1t",), "d1": ("d1",)}
    sent, sent_tokens = {}, []

    def grads_ready(stage, gw):
        stacks = [by_owner(k, gw[k]) for k in groups[stage]]
        flight, token = _copies_start(stacks, name="grads_" + stage + "_start", by_owner=True)
        sent[stage] = flight
        sent_tokens.append(token)
        return token

    norms = {"ffn1": ffn1_norm + token_last[0, 0], "mix": mix_norm, "sb": sb_out_norm, "hg": hg_out_norm,
             "ffn2": ffn2_norm, "final": final_norm.reshape(1, D_MODEL)}
    loss_row, grad_x, gw, gv = _local_step(x[0], loss_target[0], norms, hg_lower_bound_logits, w_first, weights_after,
                                           grads_ready)

    lb_row = jnp.concatenate([gv["lb"], jnp.zeros_like(gv["lb"])], axis=1)
    part = _vector_rows([gv["ffn1"], gv["mix"], gv["ffn2"], gv["final"], jnp.concatenate([gv["sb"], gv["hg"]], axis=1),
                         lb_row, None, loss_row])
    vec_w = _vector_rows(vectors(ffn1_norm, mix_norm, sb_out_norm, hg_lower_bound_logits, hg_out_norm, ffn2_norm, final_norm))
    vec_m = _vector_rows(vectors(m_ffn1_norm, m_mix_norm, m_sb_out_norm, m_hg_lower_bound_logits, m_hg_out_norm,
                                 m_ffn2_norm, m_final_norm))
    vec_v = _vector_rows(vectors(v_ffn1_norm, v_mix_norm, v_sb_out_norm, v_hg_lower_bound_logits, v_hg_out_norm,
                                 v_ffn2_norm, v_final_norm))
    updated, after = {}, sent_tokens[-1]
    for stage, flight in sent.items():
        if stage == list(sent)[-1]:
            *vecs, loss_out = _vectors_update(part, vec_w, vec_m, vec_v, name="vectors_update", tie=after)
            after = loss_out
        stacks, lands = _copies_wait(flight, after, name="grads_" + stage + "_wait", by_owner=True)
        own = [lax.dynamic_index_in_dim(s, slot, keepdims=False) for s in stacks]
        for k, part_k in zip(groups[stage], _with_own(lands, own, slot)):
            updated[k] = _sum_and_update(part_k, view(k, w_sh[k]), view(k, m_sh[k]), view(k, v_sh[k]), name="adamw_" + k,
                                         tie=after)
            after = updated[k][0]
    mats = [{k: view(k, updated[k][i]) for k in keys} for i in range(4)]

    def leaves(mat, vec):
        half = D_MODEL // 2
        return (
            vec[0:1], mat["g1t"], mat["u1t"], mat["d1"], vec[1:2], mat["int"], vec[4:5, :half],
            vec[ROW_LOGITS].reshape(2, half), vec[4:5, half:], mat["out"], vec[2:3], mat["g2t"], mat["u2t"],
            mat["d2"], vec[3],
        )

    out = [loss_out[0, 0], grad_x[None]]
    for mat, vec in zip(mats, vecs):
        out.extend(leaves(mat, vec))
    return tuple(out)
```

```python
import jax
import jax.numpy as jnp
from jax import lax
from jax.experimental import pallas as pl
from jax.experimental.pallas import tpu as pltpu

F32, BF16 = jnp.float32, jnp.bfloat16
D_MODEL = 1024
D_FF = 2816
SB_WIDTH = 512
HG_WIDTH = 512
SB_HEAD_DIM = 64
HG_HEAD_DIM = 128
IN_COLS = 3584
EPS = 1e-6
N_DEV = 8
LANES = 128
HG_CHUNK = 16
VMEM_LIMIT_BYTES = 48 * 1024 * 1024
FFN_BWD_VMEM_LIMIT_BYTES = 56 * 1024 * 1024
ADAM_LR, ADAM_B1, ADAM_B2, ADAM_EPS, ADAM_WD, ADAM_STEP = 0.001, 0.9, 0.999, 1e-08, 0.01, 10
MESH = pl.DeviceIdType.MESH


def _params(*semantics, vmem_limit_bytes=VMEM_LIMIT_BYTES):
    return pltpu.CompilerParams(dimension_semantics=semantics, vmem_limit_bytes=vmem_limit_bytes)


def _dot(a, b):
    return jnp.dot(a, b, preferred_element_type=F32)


def _dot_nt(a, b):
    return lax.dot_general(a, b, (((1,), (1,)), ((), ())), preferred_element_type=F32)


def _dot_tn(a, b):
    return lax.dot_general(a, b, (((0,), (0,)), ((), ())), preferred_element_type=F32)


def _split3(x):
    hi = x.astype(BF16)
    r1 = x - hi.astype(F32)
    mid = r1.astype(BF16)
    lo = (r1 - mid.astype(F32)).astype(BF16)
    return hi, mid, lo


def _rms(xv):
    rstd = lax.rsqrt(jnp.mean(xv * xv, axis=-1, keepdims=True) + EPS)
    return xv * rstd, rstd


def _sigmoid(x):
    return 0.5 + 0.5 * jnp.tanh(0.5 * x)


def _loss_terms(xv, gain, target):
    xhat, rstd = _rms(xv)
    err = xhat * gain - target
    loss = 0.5 * jnp.sum(jnp.mean(err * err, axis=-1, keepdims=True), axis=0, keepdims=True)
    dy = err * (1.0 / xv.shape[-1])
    dxh = dy * gain
    dx = rstd * (dxh - xhat * jnp.mean(dxh * xhat, axis=-1, keepdims=True))
    return dx, jnp.sum(dy * xhat, axis=0, keepdims=True), loss


def _mm(a, b, *, name, tm, tn, nt=False, ta=False, out_dtype=F32, tie=None):
    k, m = a.shape if ta else a.shape[::-1]
    n = b.shape[0] if nt else b.shape[1]
    assert m % tm == 0 and n % tn == 0 and not (nt and ta), (name, a.shape, b.shape, tm, tn)

    def body(a_ref, b_ref, *rest):
        av = a_ref[...].astype(BF16)
        bv = b_ref[...].astype(BF16)
        rest[-1][...] = (_dot_nt(av, bv) if nt else _dot_tn(av, bv) if ta else _dot(av, bv)).astype(out_dtype)

    in_specs = [
        pl.BlockSpec((k, tm), lambda i, j: (0, i)) if ta else pl.BlockSpec((tm, k), lambda i, j: (i, 0)),
        pl.BlockSpec((tn, k), lambda i, j: (j, 0)) if nt else pl.BlockSpec((k, tn), lambda i, j: (0, j)),
    ]
    operands = [a, b]
    if tie is not None:
        in_specs.append(pl.BlockSpec(memory_space=pl.ANY))
        operands.append(tie)
    return pl.pallas_call(
        body,
        name=name,
        grid=(m // tm, n // tn),
        in_specs=in_specs,
        out_specs=pl.BlockSpec((tm, tn), lambda i, j: (i, j)),
        out_shape=jax.ShapeDtypeStruct((m, n), out_dtype),
        compiler_params=_params("parallel", "parallel"),
    )(*operands)


def _ffn_fwd(x, gain, wgt, wut, wd, *, name, next_gain=None, head=None, tm=1024, tf=256):
    t = x.shape[0]
    nj = D_FF // tf
    extra_in = [] if next_gain is None else [next_gain]
    extra_in += [] if head is None else list(head)

    def body(x_ref, g_ref, wg_ref, wu_ref, wd_prev_ref, wd_last_ref, *rest):
        extra, (xo_ref, a_ref, b_ref, h_ref, st_ref) = rest[:len(extra_in)], rest[len(extra_in):len(extra_in) + 5]
        tail_out, (acc, s_prev) = rest[len(extra_in) + 5:-2], rest[-2:]
        i = pl.program_id(0)
        j = pl.program_id(1)

        @pl.when(j == 0)
        def _():
            xhat, _ = _rms(x_ref[...])
            h_ref[...] = (xhat * g_ref[...]).astype(BF16)
            acc[...] = jnp.zeros_like(acc)
            s_prev[...] = jnp.zeros_like(s_prev)

        acc[...] += _dot(s_prev[...], wd_prev_ref[...])
        h = h_ref[...]
        a = _dot_nt(h, wg_ref[...])
        b = _dot_nt(h, wu_ref[...])
        a_ref[...] = a.astype(BF16)
        b_ref[...] = b.astype(BF16)
        s = (a * _sigmoid(a) * b).astype(BF16)
        st_ref[...] = s
        s_prev[...] = s

        @pl.when(j == nj - 1)
        def _():
            xo = x_ref[...] + 0.5 * (acc[...] + _dot(s, wd_last_ref[...]))
            if head is None:
                xo_ref[...] = xo
            if next_gain is not None:
                tail_out[0][...] = (_rms(xo)[0] * extra[0][...]).astype(BF16)
            if head is not None:
                gain_ref, target_ref = extra[-2:]
                dg_ref, loss_ref = tail_out[-2:]
                xo_ref[...], part_g, part_loss = _loss_terms(xo, gain_ref[...], target_ref[...])

                @pl.when(i == 0)
                def _():
                    dg_ref[...] = part_g
                    loss_ref[...] = jnp.broadcast_to(part_loss, loss_ref.shape)

                @pl.when(i > 0)
                def _():
                    dg_ref[...] += part_g
                    loss_ref[...] += jnp.broadcast_to(part_loss, loss_ref.shape)

    row = pl.BlockSpec((tm, D_MODEL), lambda i, j: (i, 0))
    vec = pl.BlockSpec((1, D_MODEL), lambda i, j: (0, 0))
    tile = pl.BlockSpec((tm, tf), lambda i, j: (i, j))
    weights = pl.BlockSpec((tf, D_MODEL), lambda i, j: (j, 0))
    tail_specs = ([] if next_gain is None else [row]) + ([] if head is None else [vec, vec])
    tail_shapes = ([] if next_gain is None else [jax.ShapeDtypeStruct((t, D_MODEL), BF16)]) + (
        [] if head is None else [jax.ShapeDtypeStruct((1, D_MODEL), F32)] * 2)
    return pl.pallas_call(
        body,
        name=name,
        grid=(t // tm, nj),
        in_specs=[
            row, vec, weights, weights,
            pl.BlockSpec((tf, D_MODEL), lambda i, j: (jnp.maximum(j - 1, 0), 0)),
            pl.BlockSpec((tf, D_MODEL), lambda i, j: (nj - 1, 0)),
        ] + ([] if next_gain is None else [vec]) + ([] if head is None else [vec, row]),
        out_specs=[row, tile, tile, row, tile] + tail_specs,
        out_shape=[
            jax.ShapeDtypeStruct((t, D_MODEL), F32),
            jax.ShapeDtypeStruct((t, D_FF), BF16),
            jax.ShapeDtypeStruct((t, D_FF), BF16),
            jax.ShapeDtypeStruct((t, D_MODEL), BF16),
            jax.ShapeDtypeStruct((t, D_FF), BF16),
        ] + tail_shapes,
        scratch_shapes=[pltpu.VMEM((tm, D_MODEL), F32), pltpu.VMEM((tm, tf), BF16)],
        compiler_params=_params("arbitrary", "arbitrary"),
    )(x, gain, wgt, wut, wd, wd, *extra_in)


def _ffn_bwd(dout, x, gain, a, b, wgt, wut, wd, *, name, tm=1024, tf=256):
    t = x.shape[0]
    nj = D_FF // tf

    def body(do_ref, x_ref, g_ref, a_ref, b_ref, wg_prev_ref, wu_prev_ref, wg_last_ref, wu_last_ref, wd_ref,
             dx_ref, dg_ref, da_ref, db_ref, dob_ref, dob_scr, dh, da_prev, db_prev):
        i = pl.program_id(0)
        j = pl.program_id(1)

        @pl.when(j == 0)
        def _():
            d = (0.5 * do_ref[...]).astype(BF16)
            dob_scr[...] = d
            dob_ref[...] = d
            dh[...] = jnp.zeros_like(dh)
            da_prev[...] = jnp.zeros_like(da_prev)
            db_prev[...] = jnp.zeros_like(db_prev)

        dh[...] += _dot(da_prev[...], wg_prev_ref[...]) + _dot(db_prev[...], wu_prev_ref[...])
        ds = _dot_nt(dob_scr[...], wd_ref[...])
        av = a_ref[...].astype(F32)
        bv = b_ref[...].astype(F32)
        sig = _sigmoid(av)
        dbv = (ds * (av * sig)).astype(BF16)
        dav = (ds * bv * (sig * (1.0 + av * (1.0 - sig)))).astype(BF16)
        da_ref[...] = dav
        db_ref[...] = dbv
        da_prev[...] = dav
        db_prev[...] = dbv

        @pl.when(j == nj - 1)
        def _():
            xhat, rstd = _rms(x_ref[...])
            dhv = dh[...] + _dot(dav, wg_last_ref[...]) + _dot(dbv, wu_last_ref[...])
            part = jnp.sum(dhv * xhat, axis=0, keepdims=True)

            @pl.when(i == 0)
            def _():
                dg_ref[...] = part

            @pl.when(i > 0)
            def _():
                dg_ref[...] += part

            dxh = dhv * g_ref[...]
            dx_ref[...] = do_ref[...] + rstd * (dxh - xhat * jnp.mean(dxh * xhat, axis=-1, keepdims=True))

    return pl.pallas_call(
        body,
        name=name,
        grid=(t // tm, nj),
        in_specs=[
            pl.BlockSpec((tm, D_MODEL), lambda i, j: (i, 0)),
            pl.BlockSpec((tm, D_MODEL), lambda i, j: (i, 0)),
            pl.BlockSpec((1, D_MODEL), lambda i, j: (0, 0)),
            pl.BlockSpec((tm, tf), lambda i, j: (i, j)),
            pl.BlockSpec((tm, tf), lambda i, j: (i, j)),
            pl.BlockSpec((tf, D_MODEL), lambda i, j: (jnp.maximum(j - 1, 0), 0)),
            pl.BlockSpec((tf, D_MODEL), lambda i, j: (jnp.maximum(j - 1, 0), 0)),
            pl.BlockSpec((tf, D_MODEL), lambda i, j: (nj - 1, 0)),
            pl.BlockSpec((tf, D_MODEL), lambda i, j: (nj - 1, 0)),
            pl.BlockSpec((tf, D_MODEL), lambda i, j: (j, 0)),
        ],
        out_specs=[
            pl.BlockSpec((tm, D_MODEL), lambda i, j: (i, 0)),
            pl.BlockSpec((1, D_MODEL), lambda i, j: (0, 0)),
            pl.BlockSpec((tm, tf), lambda i, j: (i, j)),
            pl.BlockSpec((tm, tf), lambda i, j: (i, j)),
            pl.BlockSpec((tm, D_MODEL), lambda i, j: (i, 0)),
        ],
        out_shape=[
            jax.ShapeDtypeStruct((t, D_MODEL), F32),
            jax.ShapeDtypeStruct((1, D_MODEL), F32),
            jax.ShapeDtypeStruct((t, D_FF), BF16),
            jax.ShapeDtypeStruct((t, D_FF), BF16),
            jax.ShapeDtypeStruct((t, D_MODEL), BF16),
        ],
        scratch_shapes=[pltpu.VMEM((tm, D_MODEL), BF16), pltpu.VMEM((tm, D_MODEL), F32), pltpu.VMEM((tm, tf), BF16),
                        pltpu.VMEM((tm, tf), BF16)],
        compiler_params=_params("arbitrary", "arbitrary", vmem_limit_bytes=FFN_BWD_VMEM_LIMIT_BYTES),
    )(dout, x, gain, a, b, wgt, wut, wgt, wut, wd)


def _in_proj_bwd(dproj, w_int, x, gain, dres, *, name, tm=512):
    t, k = dproj.shape

    def body(dp_ref, w_ref, x_ref, g_ref, dr_ref, dx_ref, dg_ref, dxb_ref):
        i = pl.program_id(0)
        dhv = _dot(dp_ref[...], w_ref[...])
        xhat, rstd = _rms(x_ref[...])
        part = jnp.sum(dhv * xhat, axis=0, keepdims=True)

        @pl.when(i == 0)
        def _():
            dg_ref[...] = part

        @pl.when(i > 0)
        def _():
            dg_ref[...] += part

        dxh = dhv * g_ref[...]
        dx = dr_ref[...] + rstd * (dxh - xhat * jnp.mean(dxh * xhat, axis=-1, keepdims=True))
        dx_ref[...] = dx
        dxb_ref[...] = (0.5 * dx).astype(BF16)

    row = pl.BlockSpec((tm, D_MODEL), lambda i: (i, 0))
    vec = pl.BlockSpec((1, D_MODEL), lambda i: (0, 0))
    return pl.pallas_call(
        body,
        name=name,
        grid=(t // tm,),
        in_specs=[pl.BlockSpec((tm, k), lambda i: (i, 0)), pl.BlockSpec((k, D_MODEL), lambda i: (0, 0)), row, vec, row],
        out_specs=[row, vec, row],
        out_shape=[jax.ShapeDtypeStruct((t, D_MODEL), F32), jax.ShapeDtypeStruct((1, D_MODEL), F32),
                   jax.ShapeDtypeStruct((t, D_MODEL), BF16)],
        compiler_params=_params("arbitrary"),
    )(dproj, w_int, x, gain, dres)


ATT_Q_TILE = 512
ATT_K_BLOCK = 256


def _first_head_lanes():
    return lax.broadcasted_iota(jnp.int32, (1, LANES), 1) < SB_HEAD_DIM


def _stack_heads(x):
    first = _first_head_lanes()
    return jnp.concatenate([jnp.where(first, x, 0.0), jnp.where(first, 0.0, x)], axis=0)


def _unstack_heads(x, rows):
    return jnp.where(_first_head_lanes(), x[:rows], x[rows:])


def _rows_from(x, first, rows):
    return x if first == 0 else jnp.concatenate([x[first:rows], x[rows + first:]], axis=0)


def _rows_into(full, part, first, rows):
    if first == 0:
        return part
    n = rows - first
    return jnp.concatenate([full[:first], part[:n], full[rows:rows + first], part[n:]], axis=0)


def _tri(n, relation):
    r = lax.broadcasted_iota(jnp.int32, (n, n), 0)
    c = lax.broadcasted_iota(jnp.int32, (n, n), 1)
    return relation(r, c).astype(BF16)


def _scan_dot(x, tri):
    hi = x.astype(BF16)
    lo = (x - hi.astype(F32)).astype(BF16)
    return _dot(jnp.concatenate([hi, lo], axis=1), jnp.concatenate([tri, tri], axis=0))


def _log_terms(z):
    lbeta = jnp.minimum(z, 0.0) - jnp.log(1.0 + jnp.exp(-jnp.abs(z)))
    return lbeta, lbeta - z


def _attn_fwd(proj, *, name):
    t = proj.shape[0]
    tq, tk = ATT_Q_TILE, ATT_K_BLOCK
    diag = tq // tk
    n_pairs = SB_WIDTH // LANES

    def body(q_ref, k_ref, v_ref, o_ref, kept_ref):
        qi = pl.program_id(1)
        q = q_ref[...] * (SB_HEAD_DIM ** -0.5)
        qs = _stack_heads(q).astype(BF16)
        tri = _tri(tk, lambda j, s: j > s)
        trow = lax.broadcasted_iota(jnp.int32, (tq, tk), 0)
        scol = lax.broadcasted_iota(jnp.int32, (tq, tk), 1)

        def block(kb, carry, causal, first=0):
            acc, c = carry
            off = pl.multiple_of(kb * tk, tk)
            lbeta, lrest = _log_terms(_dot_nt(_rows_from(qs, first, tq), k_ref[pl.ds(off, tk), :].astype(BF16)))
            if causal is not None:
                lrest = jnp.where(causal, lrest, 0.0)
            w = jnp.exp(lbeta + (_scan_dot(lrest, tri) + _rows_from(c, first, tq)))
            if causal is not None:
                w = jnp.where(causal, w, 0.0)
            wb = w.astype(BF16)
            kept_ref[0, 0, kb] = _rows_into(jnp.zeros((2 * tq, tk), BF16), wb, first, tq)
            acc = _rows_into(acc, _rows_from(acc, first, tq) + _dot(wb, v_ref[pl.ds(off, tk), :].astype(BF16)), first, tq)
            return acc, _rows_into(c, _rows_from(c, first, tq) + jnp.sum(lrest, axis=1, keepdims=True), first, tq)

        carry = (jnp.zeros((2 * tq, LANES), F32), jnp.zeros((2 * tq, 1), F32))
        n_full = qi * diag
        for j in reversed(range(diag)):
            mask = ((scol + j * tk) < trow)[j * tk:]
            carry = block(n_full + j, carry, jnp.concatenate([mask, mask], axis=0), first=j * tk)

        def odd_tile(carry):
            for j in range(diag):
                carry = block(n_full - 1 - j, carry, None)
            return carry

        carry = lax.cond(qi % 2 == 1, odd_tile, lambda c: c, carry)
        last = n_full - 1 - (qi % 2) * diag

        def step(it, carry):
            for j in range(2 * diag):
                carry = block(last - (2 * diag * it + j), carry, None)
            return carry

        acc, _ = lax.fori_loop(0, qi // 2, step, carry)
        o_ref[...] = _unstack_heads(acc, tq)

    return pl.pallas_call(
        body,
        name=name,
        grid=(n_pairs, t // tq),
        in_specs=[
            pl.BlockSpec((tq, LANES), lambda p, i: (i, p)),
            pl.BlockSpec((t, LANES), lambda p, i: (0, n_pairs + p)),
            pl.BlockSpec((t, LANES), lambda p, i: (0, 2 * n_pairs + p)),
        ],
        out_specs=[pl.BlockSpec((tq, LANES), lambda p, i: (i, p)),
                   pl.BlockSpec((1, 1, t // tk, 2 * tq, tk), lambda p, i: (p, i, 0, 0, 0))],
        out_shape=[jax.ShapeDtypeStruct((t, SB_WIDTH), F32),
                   jax.ShapeDtypeStruct((n_pairs, t // tq, t // tk, 2 * tq, tk), BF16)],
        compiler_params=_params("parallel", "parallel"),
    )(proj, proj, proj)


def _attn_bwd(proj, kept, do, *, name, tie=None):
    t = proj.shape[0]
    tq, tk = ATT_Q_TILE, ATT_K_BLOCK
    diag = tq // tk
    n_pairs = SB_WIDTH // LANES
    scale = SB_HEAD_DIM ** -0.5

    def body(q_ref, k_ref, v_ref, kept_ref, do_ref, *rest):
        dq_ref, dk_ref, dv_ref = rest[-3:]
        qi = pl.program_id(1)

        @pl.when(qi == 0)
        def _():
            dk_ref[...] = jnp.zeros_like(dk_ref)
            dv_ref[...] = jnp.zeros_like(dv_ref)

        qs = _stack_heads(q_ref[...] * scale).astype(BF16)
        dos = _stack_heads(do_ref[...]).astype(BF16)
        before = _tri(tk, lambda s, j: s < j)
        trow = lax.broadcasted_iota(jnp.int32, (tq, tk), 0)
        scol = lax.broadcasted_iota(jnp.int32, (tq, tk), 1)

        def block(kb, carry, causal, first=0):
            dq, cg = carry
            off = pl.multiple_of(kb * tk, tk)
            q_rows, do_rows = _rows_from(qs, first, tq), _rows_from(dos, first, tq)
            wb = _rows_from(kept_ref[0, 0, kb], first, tq)
            kblk = k_ref[pl.ds(off, tk), :].astype(BF16)
            sig = _sigmoid(_dot_nt(q_rows, kblk))
            g = wb.astype(F32) * _dot_nt(do_rows, v_ref[pl.ds(off, tk), :].astype(BF16))
            prior = _scan_dot(g, before) + _rows_from(cg, first, tq)
            dz = g - sig * (g + prior)
            if causal is not None:
                dz = jnp.where(causal, dz, 0.0)
            dzb = dz.astype(BF16)
            dq = _rows_into(dq, _rows_from(dq, first, tq) + _dot(dzb, kblk), first, tq)
            dk_ref[pl.ds(off, tk), :] += _dot_tn(dzb, q_rows)
            dv_ref[pl.ds(off, tk), :] += _dot_tn(wb, do_rows)
            return dq, _rows_into(cg, _rows_from(cg, first, tq) + jnp.sum(g, axis=1, keepdims=True), first, tq)

        n_full = qi * diag

        def step(it, carry):
            for j in range(2 * diag):
                carry = block(2 * diag * it + j, carry, None)
            return carry

        def odd_tile(carry):
            for j in range(diag):
                carry = block(n_full - diag + j, carry, None)
            return carry

        carry = lax.fori_loop(0, qi // 2, step, (jnp.zeros((2 * tq, LANES), F32), jnp.zeros((2 * tq, 1), F32)))
        carry = lax.cond(qi % 2 == 1, odd_tile, lambda c: c, carry)
        for j in range(diag):
            mask = ((scol + j * tk) < trow)[j * tk:]
            carry = block(n_full + j, carry, jnp.concatenate([mask, mask], axis=0), first=j * tk)
        dq_ref[...] = (_unstack_heads(carry[0], tq) * scale).astype(BF16)

    tile_spec = pl.BlockSpec((tq, LANES), lambda p, i: (i, p))
    full_spec = pl.BlockSpec((t, LANES), lambda p, i: (0, p))
    return pl.pallas_call(
        body,
        name=name,
        grid=(n_pairs, t // tq),
        in_specs=[
            tile_spec,
            pl.BlockSpec((t, LANES), lambda p, i: (0, n_pairs + p)),
            pl.BlockSpec((t, LANES), lambda p, i: (0, 2 * n_pairs + p)),
            pl.BlockSpec((1, 1, t // tk, 2 * tq, tk), lambda p, i: (p, i, 0, 0, 0)),
            tile_spec,
        ] + ([] if tie is None else [pl.BlockSpec(memory_space=pl.ANY)]),
        out_specs=[tile_spec, full_spec, full_spec],
        out_shape=[jax.ShapeDtypeStruct((t, SB_WIDTH), BF16)] + [jax.ShapeDtypeStruct((t, SB_WIDTH), F32)] * 2,
        compiler_params=_params("arbitrary", "arbitrary"),
    )(proj, proj, proj, kept, do, *([] if tie is None else [tie]))


HG_BLOCK = 128
HG_HEADS = HG_WIDTH // HG_HEAD_DIM


def _chunk_mats(n):
    r = lax.broadcasted_iota(jnp.int32, (n, n), 0)
    c = lax.broadcasted_iota(jnp.int32, (n, n), 1)
    same = (r // HG_CHUNK) == (c // HG_CHUNK)
    upto = (same & (c <= r)).astype(BF16)
    whole = same.astype(BF16)
    onward = (same & (c >= r)).astype(BF16)
    return upto, whole, onward


def _rows_dot(mat, x):
    return _dot(jnp.concatenate([mat, mat, mat], axis=1), jnp.concatenate(_split3(x), axis=0))


def _split_heads(x):
    return jnp.stack([x[:, h * HG_HEAD_DIM:(h + 1) * HG_HEAD_DIM] for h in range(HG_HEADS)], axis=0)


def _merge_heads(x):
    return jnp.concatenate([x[h] for h in range(HG_HEADS)], axis=1)


def _lower_bound(lg_ref):
    lg = lg_ref[...]
    return _sigmoid(lg[0:1, :] - lg[1:2, :])


def _hgrn_prepare(q_ref, f_ref, lb, h, upto, whole):
    cols = slice(h * HG_HEAD_DIM, (h + 1) * HG_HEAD_DIM)
    lbh = lb[:, cols]
    sg = _sigmoid(f_ref[:, cols])
    forget = lbh + (1.0 - lbh) * sg
    logf = jnp.log(forget)
    kk = (1.0 - lbh) * (1.0 - sg)
    qv = q_ref[:, cols]
    qsig = _sigmoid(qv)
    qh = qv * qsig
    b = _rows_dot(upto, logf)
    blast = _rows_dot(whole, logf)
    return dict(lbh=lbh, sg=sg, forget=forget, kk=kk, qv=qv, qsig=qsig, qh=qh, b=b, eb=jnp.exp(b),
                ekb=jnp.exp(blast - b), dl=jnp.exp(blast))


def _hgrn_fwd(proj, logits, *, name):
    t = proj.shape[0]
    tb = HG_BLOCK
    nc = tb // HG_CHUNK
    hd = HG_HEAD_DIM

    def body(q_ref, f_ref, i_ref, lg_ref, o_ref, st_ref, state, qh_s, kk_s, b_s, qe_s, ke_s, dl_s):
        @pl.when(pl.program_id(0) == 0)
        def _():
            state[...] = jnp.zeros_like(state)

        lb = _lower_bound(lg_ref)
        upto, whole, _ = _chunk_mats(tb)
        for h in range(HG_HEADS):
            p = _hgrn_prepare(q_ref, f_ref, lb, h, upto, whole)
            qh_s[h] = p["qh"]
            kk_s[h] = p["kk"]
            b_s[h] = p["b"]
            qe_s[h] = (p["qh"] * p["eb"]).astype(BF16)
            ke_s[h] = (p["kk"] * p["ekb"]).astype(BF16)
            dl_s[h] = p["dl"]
        rowi = lax.broadcasted_iota(jnp.int32, (HG_HEADS, HG_CHUNK, hd), 1)

        def chunk(c, _):
            r0 = pl.multiple_of(c * HG_CHUNK, HG_CHUNK)
            rows = pl.ds(r0, HG_CHUNK)
            bc = b_s[:, rows, :]
            qc = qh_s[:, rows, :]
            kc = kk_s[:, rows, :]
            vc = _split_heads(i_ref[rows, :])
            s_in = state[...]
            st_ref[c] = s_in
            s_in_b = s_in.astype(BF16)
            qe = qe_s[:, rows, :]
            o = jnp.stack([_dot_nt(qe[h], s_in_b[h]) for h in range(HG_HEADS)], axis=0)
            for s in range(HG_CHUNK):
                pair = jnp.where(rowi >= s, qc * jnp.exp(bc - bc[:, s:s + 1, :]) * kc[:, s:s + 1, :], 0.0)
                o = o + jnp.sum(pair, axis=2, keepdims=True) * vc[:, s:s + 1, :]
            o_ref[rows, :] = _merge_heads(o)
            vcb = vc.astype(BF16)
            ke = ke_s[:, rows, :]
            update = jnp.stack([_dot_tn(vcb[h], ke[h]) for h in range(HG_HEADS)], axis=0)
            state[...] = s_in * dl_s[:, pl.ds(r0, 1), :] + update
            return 0

        lax.fori_loop(0, nc, chunk, 0, unroll=4)

    blk =lambda col: pl.BlockSpec((tb, HG_WIDTH), lambda i: (i, col))
    head_f32 = pltpu.VMEM((HG_HEADS, tb, hd), F32)
    head_bf16 = pltpu.VMEM((HG_HEADS, tb, hd), BF16)
    return pl.pallas_call(
        body,
        name=name,
        grid=(t // tb,),
        in_specs=[blk(3), blk(4), blk(5), pl.BlockSpec((2, HG_WIDTH), lambda i: (0, 0))],
        out_specs=[
            pl.BlockSpec((tb, HG_WIDTH), lambda i: (i, 0)),
            pl.BlockSpec((nc, HG_HEADS, hd, hd), lambda i: (i, 0, 0, 0)),
        ],
        out_shape=[
            jax.ShapeDtypeStruct((t, HG_WIDTH), F32),
            jax.ShapeDtypeStruct((t // HG_CHUNK, HG_HEADS, hd, hd), F32),
        ],
        scratch_shapes=[pltpu.VMEM((HG_HEADS, hd, hd), F32), head_f32, head_f32, head_f32, head_bf16, head_bf16,
                        head_f32],
        compiler_params=_params("arbitrary"),
    )(proj, proj, proj, logits)


def _hgrn_bwd(proj, logits, states, do, *, name):
    t = proj.shape[0]
    tb = HG_BLOCK
    nb = t // tb
    nc = tb // HG_CHUNK
    hd = HG_HEAD_DIM

    def body(q_ref, f_ref, i_ref, lg_ref, st_ref, do_ref, dq_ref, df_ref, di_ref, dlb_ref,
             dstate, qh_s, kk_s, b_s, eb_s, ekb_s, qe_s, ke_s, dl_s, dqh_s, dkk_s, dlf_s):
        step = pl.program_id(0)

        @pl.when(step == 0)
        def _():
            dstate[...] = jnp.zeros_like(dstate)
            dlb_ref[...] = jnp.zeros_like(dlb_ref)

        lb = _lower_bound(lg_ref)
        upto, whole, _ = _chunk_mats(tb)
        prepared = []
        for h in range(HG_HEADS):
            p = _hgrn_prepare(q_ref, f_ref, lb, h, upto, whole)
            prepared.append(p)
            qh_s[h] = p["qh"]
            kk_s[h] = p["kk"]
            b_s[h] = p["b"]
            eb_s[h] = p["eb"]
            ekb_s[h] = p["ekb"]
            qe_s[h] = (p["qh"] * p["eb"]).astype(BF16)
            ke_s[h] = (p["kk"] * p["ekb"]).astype(BF16)
            dl_s[h] = p["dl"]
        rowi = lax.broadcasted_iota(jnp.int32, (HG_CHUNK, hd), 0)
        r16 = lax.broadcasted_iota(jnp.int32, (HG_CHUNK, HG_CHUNK), 0)
        c16 = lax.broadcasted_iota(jnp.int32, (HG_CHUNK, HG_CHUNK), 1)
        onward = (c16 >= r16).astype(BF16)

        def chunk(it, _):
            c = nc - 1 - it
            r0 = pl.multiple_of(c * HG_CHUNK, HG_CHUNK)
            rows = pl.ds(r0, HG_CHUNK)
            for h in range(HG_HEADS):
                cols = slice(h * hd, (h + 1) * hd)
                bc = b_s[h, rows, :]
                qc = qh_s[h, rows, :]
                kc = kk_s[h, rows, :]
                vc = i_ref[rows, cols]
                doc = do_ref[rows, cols]
                s_in = st_ref[c, h]
                ds_out = dstate[h]
                ds_out_b = ds_out.astype(BF16)
                docb = doc.astype(BF16)
                dl_row = dl_s[h, pl.ds(r0, 1), :]
                dqh = _dot(docb, s_in.astype(BF16)) * eb_s[h, rows, :]
                dkk = _dot(vc.astype(BF16), ds_out_b) * ekb_s[h, rows, :]
                dv = _dot_nt(ke_s[h, rows, :], ds_out_b)
                db = dqh * qc - dkk * kc
                dwhole = jnp.sum(dkk * kc, axis=0, keepdims=True) + jnp.sum(ds_out * s_in, axis=0, keepdims=True) * dl_row
                dk_rows, dv_rows = [], []
                for s in range(HG_CHUNK):
                    keep = rowi >= s
                    e = jnp.exp(bc - bc[s:s + 1, :])
                    k_row = kc[s:s + 1, :]
                    pcol = jnp.sum(jnp.where(keep, qc * e * k_row, 0.0), axis=1, keepdims=True)
                    dpcol = jnp.sum(doc * vc[s:s + 1, :], axis=1, keepdims=True)
                    m = jnp.where(keep, e * dpcol, 0.0)
                    y = m * qc
                    dqh = dqh + m * k_row
                    db = db + y * k_row
                    dk_rows.append(jnp.sum(y, axis=0, keepdims=True))
                    dv_rows.append(jnp.sum(pcol * doc, axis=0, keepdims=True))
                dkk_pairs = jnp.concatenate(dk_rows, axis=0)
                dkk = dkk + dkk_pairs
                db = db - dkk_pairs * kc
                dv = dv + jnp.concatenate(dv_rows, axis=0)
                dqh_s[h, rows, :] = dqh
                dkk_s[h, rows, :] = dkk
                dlf_s[h, rows, :] = _rows_dot(onward, db) + dwhole
                di_ref[rows, cols] = dv.astype(BF16)
                dstate[h] = ds_out * dl_row + _dot_tn(docb, qe_s[h, rows, :])
            return 0

        lax.fori_loop(0, nc, chunk, 0, unroll=4)
        for h in range(HG_HEADS):
            cols = slice(h * hd, (h + 1) * hd)
            p = prepared[h]
            dq_ref[:, cols] = (dqh_s[h] * (p["qsig"] * (1.0 + p["qv"] * (1.0 - p["qsig"])))).astype(BF16)
            dforget = dlf_s[h] / p["forget"] - dkk_s[h]
            df_ref[:, cols] = (dforget * (1.0 - p["lbh"]) * p["sg"] * (1.0 - p["sg"])).astype(BF16)
            dlb_ref[:, cols] += jnp.sum(dforget * (1.0 - p["sg"]), axis=0, keepdims=True)

    blk = lambda col: pl.BlockSpec((tb, HG_WIDTH), lambda i: (nb - 1 - i, col))
    vec = pl.BlockSpec((1, HG_WIDTH), lambda i: (0, 0))
    head_f32 = pltpu.VMEM((HG_HEADS, tb, hd), F32)
    head_bf16 = pltpu.VMEM((HG_HEADS, tb, hd), BF16)
    return pl.pallas_call(
        body,
        name=name,
        grid=(nb,),
        in_specs=[
            blk(3), blk(4), blk(5),
            pl.BlockSpec((2, HG_WIDTH), lambda i: (0, 0)),
            pl.BlockSpec((nc, HG_HEADS, hd, hd), lambda i: (nb - 1 - i, 0, 0, 0)),
            blk(0),
        ],
        out_specs=[blk(0), blk(0), blk(0), vec],
        out_shape=[jax.ShapeDtypeStruct((t, HG_WIDTH), BF16)] * 3 + [jax.ShapeDtypeStruct((1, HG_WIDTH), F32)],
        scratch_shapes=[
            pltpu.VMEM((HG_HEADS, hd, hd), F32),
            head_f32, head_f32, head_f32, head_f32, head_f32, head_bf16, head_bf16, head_f32,
            head_f32, head_f32, head_f32,
        ],
        compiler_params=_params("arbitrary"),
    )(proj, proj, proj, logits, states, do)


def _group_mat(width, head_dim):
    r = lax.broadcasted_iota(jnp.int32, (width, width), 0)
    c = lax.broadcasted_iota(jnp.int32, (width, width), 1)
    return ((r // head_dim) == (c // head_dim)).astype(BF16)


def _head_mean(x, mat, head_dim):
    hi = x.astype(BF16)
    lo = (x - hi.astype(F32)).astype(BF16)
    return (_dot(hi, mat) + _dot(lo, mat)) * (1.0 / head_dim)


def _mix_out_fwd(o_sb, o_hg, proj, g_sb, g_hg, w_out, x1, *, name, tm=512):
    t = x1.shape[0]

    def body(osb_ref, ohg_ref, gate_ref, gsb_ref, ghg_ref, w_ref, x_ref, xo_ref, mt_ref):
        msb = _group_mat(SB_WIDTH, SB_HEAD_DIM)
        mhg = _group_mat(HG_WIDTH, HG_HEAD_DIM)
        osb = osb_ref[...]
        ohg = ohg_ref[...]
        nsb = osb * lax.rsqrt(_head_mean(osb * osb, msb, SB_HEAD_DIM) + EPS) * gsb_ref[...]
        gate = gate_ref[...]
        nhg = ohg * lax.rsqrt(_head_mean(ohg * ohg, mhg, HG_HEAD_DIM) + EPS) * ghg_ref[...] * (gate * _sigmoid(gate))
        mixed = jnp.concatenate([nsb, nhg], axis=1).astype(BF16)
        mt_ref[...] = mixed
        xo_ref[...] = x_ref[...] + _dot(mixed, w_ref[...])

    half = pl.BlockSpec((tm, SB_WIDTH), lambda i: (i, 0))
    vec = pl.BlockSpec((1, SB_WIDTH), lambda i: (0, 0))
    row = pl.BlockSpec((tm, D_MODEL), lambda i: (i, 0))
    return pl.pallas_call(
        body,
        name=name,
        grid=(t // tm,),
        in_specs=[half, half, pl.BlockSpec((tm, HG_WIDTH), lambda i: (i, 6)), vec, vec,
                  pl.BlockSpec((D_MODEL, D_MODEL), lambda i: (0, 0)), row],
        out_specs=[row, row],
        out_shape=[jax.ShapeDtypeStruct((t, D_MODEL), F32), jax.ShapeDtypeStruct((t, D_MODEL), BF16)],
        compiler_params=_params("parallel"),
    )(o_sb, o_hg, proj, g_sb, g_hg, w_out, x1)


def _mix_out_bwd(dx2, o_sb, o_hg, proj, g_sb, g_hg, w_out, *, name, tm=512):
    t = dx2.shape[0]

    def body(dx_ref, osb_ref, ohg_ref, gate_ref, gsb_ref, ghg_ref, w_ref, dosb_ref, dohg_ref, dgate_ref, dgsb_ref,
             dghg_ref, dxb_ref):
        i = pl.program_id(0)
        msb = _group_mat(SB_WIDTH, SB_HEAD_DIM)
        mhg = _group_mat(HG_WIDTH, HG_HEAD_DIM)
        dxb = dx_ref[...].astype(BF16)
        dxb_ref[...] = dxb
        dmixed = _dot_nt(dxb, w_ref[...])
        dnsb = dmixed[:, :SB_WIDTH]
        dy = dmixed[:, SB_WIDTH:]

        osb = osb_ref[...]
        rstd = lax.rsqrt(_head_mean(osb * osb, msb, SB_HEAD_DIM) + EPS)
        ohat = osb * rstd
        part_sb = jnp.sum(dnsb * ohat, axis=0, keepdims=True)
        dohat = dnsb * gsb_ref[...]
        dosb_ref[...] = rstd * (dohat - ohat * _head_mean(dohat * ohat, msb, SB_HEAD_DIM))

        ohg = ohg_ref[...]
        rstd = lax.rsqrt(_head_mean(ohg * ohg, mhg, HG_HEAD_DIM) + EPS)
        ohat = ohg * rstd
        gate = gate_ref[...]
        sig = _sigmoid(gate)
        dn = dy * (gate * sig)
        dgate_ref[...] = (dy * (ohat * ghg_ref[...]) * (sig * (1.0 + gate * (1.0 - sig)))).astype(BF16)
        part_hg = jnp.sum(dn * ohat, axis=0, keepdims=True)
        dohat = dn * ghg_ref[...]
        dohg_ref[...] = rstd * (dohat - ohat * _head_mean(dohat * ohat, mhg, HG_HEAD_DIM))

        @pl.when(i == 0)
        def _():
            dgsb_ref[...] = part_sb
            dghg_ref[...] = part_hg

        @pl.when(i > 0)
        def _():
            dgsb_ref[...] += part_sb
            dghg_ref[...] += part_hg

    half = pl.BlockSpec((tm, SB_WIDTH), lambda i: (i, 0))
    vec = pl.BlockSpec((1, SB_WIDTH), lambda i: (0, 0))
    row = pl.BlockSpec((tm, D_MODEL), lambda i: (i, 0))
    return pl.pallas_call(
        body,
        name=name,
        grid=(t // tm,),
        in_specs=[row, half, half, pl.BlockSpec((tm, HG_WIDTH), lambda i: (i, 6)), vec, vec,
                  pl.BlockSpec((D_MODEL, D_MODEL), lambda i: (0, 0))],
        out_specs=[half, half, half, vec, vec, row],
        out_shape=[jax.ShapeDtypeStruct((t, SB_WIDTH), F32)] * 2 + [jax.ShapeDtypeStruct((t, SB_WIDTH), BF16)]
        + [jax.ShapeDtypeStruct((1, SB_WIDTH), F32)] * 2 + [jax.ShapeDtypeStruct((t, D_MODEL), BF16)],
        compiler_params=_params("arbitrary"),
    )(dx2, o_sb, o_hg, proj, g_sb, g_hg, w_out)


def _local_step(x, target, norms, logits, w, weights_after=None, grads_ready=None):
    w = dict(w)
    x1, a1, b1, h1, s1, hm = _ffn_fwd(x, norms["ffn1"], w["g1t"], w["u1t"], w["d1"], name="ffn1_fwd",
                                      next_gain=norms["mix"])
    if weights_after is not None:
        w.update(weights_after("ffn1", x1))
    proj = _mm(hm, w["int"], name="in_proj", tm=512, tn=IN_COLS, nt=True)
    o_sb, sb_kept = _attn_fwd(proj, name="sb_attn_fwd")
    o_hg, states = _hgrn_fwd(proj, logits, name="hgrn2_fwd")
    x2, mixed = _mix_out_fwd(o_sb, o_hg, proj, norms["sb"], norms["hg"], w["out"], x1, name="mix_out_fwd")
    if weights_after is not None:
        w.update(weights_after("mix", x2))
    dx3, a2, b2, h2, s2, d_final, loss_row = _ffn_fwd(x2, norms["ffn2"], w["g2t"], w["u2t"], w["d2"], name="ffn2_fwd",
                                                      head=(norms["final"], target))

    def weight_grad(lhs, rhs, name, tie=None):
        return _mm(lhs, rhs, name=name, tm=256, tn=D_MODEL, ta=True, out_dtype=BF16, tie=tie)

    def sent(stage):
        return grads_ready(stage, gw) if grads_ready is not None else None

    gw, gv = {}, {"final": d_final}
    dx2, gv["ffn2"], da2, db2, dob2 = _ffn_bwd(dx3, x2, norms["ffn2"], a2, b2, w["g2t"], w["u2t"], w["d2"],
                                               name="ffn2_bwd")
    gw["g2t"] = weight_grad(da2, h2, "ffn2_dgate")
    gw["u2t"] = weight_grad(db2, h2, "ffn2_dup")
    gw["d2"] = weight_grad(s2, dob2, "ffn2_ddown")

    do_sb, do_hg, d_gate, gv["sb"], gv["hg"], dx2b = _mix_out_bwd(
        dx2, o_sb, o_hg, proj, norms["sb"], norms["hg"], w["out"], name="mix_out_bwd")
    gw["out"] = weight_grad(mixed, dx2b, "out_dw")
    tie = sent("mix")
    dq_sb, dk_sb, dv_sb = _attn_bwd(proj, sb_kept, do_sb, name="sb_attn_bwd", tie=tie)
    dq_hg, df_hg, di_hg, d_lb = _hgrn_bwd(proj, logits if tie is None else logits + tie[0, 0], states, do_hg,
                                          name="hgrn2_bwd")
    dproj = jnp.concatenate([dq_sb, dk_sb.astype(BF16), dv_sb.astype(BF16), dq_hg, df_hg, di_hg, d_gate], axis=1)
    gw["int"] = weight_grad(dproj, hm, "in_dw")
    tie = sent("in")
    dx1, gv["mix"], dob1 = _in_proj_bwd(dproj, w["int"], x1, norms["mix"] if tie is None else norms["mix"] + tie[0, 0],
                                        dx2, name="in_dx")

    gw["d1"] = weight_grad(s1, dob1, "ffn1_ddown")
    tie = sent("d1")
    dx, gv["ffn1"], da1, db1, _ = _ffn_bwd(dx1, x, norms["ffn1"] if tie is None else norms["ffn1"] + tie[0, 0], a1, b1,
                                           w["g1t"], w["u1t"], w["d1"], name="ffn1_bwd")
    gw["g1t"] = weight_grad(da1, h1, "ffn1_dgate")
    gw["u1t"] = weight_grad(db1, h1, "ffn1_dup", tie=sent("g1t"))
    sent("u1t")
    gv["lb"] = d_lb
    return loss_row, dx, gw, gv


HBM = pl.BlockSpec(memory_space=pl.ANY)


def _place():
    return lax.axis_index("x"), lax.axis_index("y"), lax.axis_index("c")


def _slot(px, py, pc):
    return 4 * px + 2 * py + pc


GATHER_COPIES = 8


def _all_gather(blocks, *, name):
    n = len(blocks)

    def body(*refs):
        ins, outs = refs[:n], refs[n:2 * n]
        send_sems, recv_sems, local_sems = refs[2 * n:]
        x, y, c = _place()
        me, sibling = (x, y, c), (x, y, 1 - c)
        beside, across, diagonal = (1 - x, y, c), (x, 1 - y, c), (1 - x, 1 - y, c)

        def copy(a, k, block, to, src=None, half=None):
            dst = outs[a].at[_slot(*block)]
            if half is not None:
                rows = blocks[a].shape[0] // 2
                dst = dst.at[pl.ds(half * rows, rows)]
            return pltpu.make_async_remote_copy(
                src_ref=dst if src is None else src, dst_ref=dst, send_sem=send_sems.at[GATHER_COPIES * a + k],
                recv_sem=recv_sems.at[GATHER_COPIES * a + k], device_id=to, device_id_type=MESH)

        mine = [pltpu.make_async_copy(ins[a], outs[a].at[_slot(*me)], local_sems.at[a]) for a in range(n)]
        for cp in mine:
            cp.start()
        sent = []
        for a in range(n):
            sent += [copy(a, 0, me, sibling, src=ins[a]), copy(a, 1, me, beside, src=ins[a]),
                     copy(a, 2, me, across, src=ins[a])]
        for cp in sent:
            cp.start()
        for a in range(n):
            copy(a, 1, beside, me).wait_recv()
            sent += [copy(a, 3, beside, across, half=0), copy(a, 5, beside, sibling)]
            sent[-2].start()
            sent[-1].start()
        for a in range(n):
            copy(a, 2, across, me).wait_recv()
            sent += [copy(a, 4, across, beside, half=1), copy(a, 6, across, sibling)]
            sent[-2].start()
            sent[-1].start()
        for a in range(n):
            copy(a, 3, diagonal, me, half=0).wait_recv()
            copy(a, 4, diagonal, me, half=1).wait_recv()
            sent.append(copy(a, 7, diagonal, sibling))
            sent[-1].start()
        for a in range(n):
            for k, origin in ((0, sibling), (5, (1 - x, y, 1 - c)), (6, (x, 1 - y, 1 - c)), (7, (1 - x, 1 - y, 1 - c))):
                copy(a, k, origin, me).wait_recv()
        for cp in sent:
            cp.wait_send()
        for cp in mine:
            cp.wait()

    return pl.pallas_call(
        body,
        name=name,
        in_specs=[HBM] * n,
        out_specs=[HBM] * n,
        out_shape=[jax.ShapeDtypeStruct((N_DEV,) + b.shape, b.dtype) for b in blocks],
        scratch_shapes=[pltpu.SemaphoreType.DMA((GATHER_COPIES * n,)), pltpu.SemaphoreType.DMA((GATHER_COPIES * n,)),
                        pltpu.SemaphoreType.DMA((n,))],
    )(*blocks)


def _flipped(place, d):
    return tuple(1 - p if (d >> (2 - axis)) & 1 else p for axis, p in enumerate(place))


SEM = pl.BlockSpec(memory_space=pltpu.SEMAPHORE)
EFFECT = pltpu.SideEffectType.DATAFLOW_SIDE_EFFECTING


def _split_copies(me, srcs, lands, send_sems, recv_sems, by_owner):
    copies = []
    for d in range(1, N_DEV):
        peer = _flipped(me, d)
        for a, (src, land) in enumerate(zip(srcs, lands)):
            copies.append(pltpu.make_async_remote_copy(
                src_ref=src.at[_slot(*peer)] if by_owner else src, dst_ref=land.at[_slot(*me)],
                send_sem=send_sems.at[7 * a + d - 1], recv_sem=recv_sems.at[7 * a + d - 1], device_id=peer,
                device_id_type=MESH))
    return copies


def _copies_start(srcs, *, name, by_owner, after=None):
    n = len(srcs)
    extra = [] if after is None else [after]
    land_shapes = [s.shape if by_owner else (N_DEV,) + s.shape for s in srcs]
    lands = [pltpu.with_memory_space_constraint(lax.empty(shape, s.dtype), pltpu.HBM) for shape, s in zip(land_shapes, srcs)]
    srcs = [pltpu.with_memory_space_constraint(s, pltpu.HBM) for s in srcs]

    def body(*refs):
        src_refs, land_refs = refs[:n], refs[n:2 * n]
        send_sems, recv_sems = refs[2 * n + len(extra)], refs[2 * n + len(extra) + 1]
        token = refs[-1]
        for cp in _split_copies(_place(), src_refs, land_refs, send_sems, recv_sems, by_owner):
            cp.start()
        token[...] = jnp.zeros_like(token)

    out = pl.pallas_call(
        body,
        name=name,
        in_specs=[HBM] * (2 * n + len(extra)),
        out_specs=[SEM, SEM] + [HBM] * (2 * n) + [pl.BlockSpec(memory_space=pltpu.VMEM)],
        out_shape=[pltpu.SemaphoreType.DMA((7 * n,)), pltpu.SemaphoreType.DMA((7 * n,))]
        + [pltpu.HBM(s.shape, s.dtype) for s in srcs] + [pltpu.HBM(shape, s.dtype) for shape, s in zip(land_shapes, srcs)]
        + [jax.ShapeDtypeStruct((8, LANES), F32)],
        input_output_aliases={i: 2 + i for i in range(2 * n)},
        compiler_params=pltpu.CompilerParams(has_side_effects=EFFECT),
    )(*srcs, *lands, *extra)
    return (out[0], out[1], out[2:2 + n], out[2 + n:2 + 2 * n]), out[-1]


def _copies_wait(started, after, *, name, by_owner):
    send_sems, recv_sems, srcs, lands = started
    n = len(srcs)

    def body(*refs):
        src_refs, land_refs = refs[:n], refs[n:2 * n]
        for cp in _split_copies(_place(), src_refs, land_refs, refs[2 * n], refs[2 * n + 1], by_owner):
            cp.wait_send()
            cp.wait_recv()

    out = pl.pallas_call(
        body,
        name=name,
        in_specs=[HBM] * (2 * n) + [SEM, SEM, HBM],
        out_specs=[HBM] * (2 * n),
        out_shape=[pltpu.HBM(s.shape, s.dtype) for s in srcs] + [pltpu.HBM(s.shape, s.dtype) for s in lands],
        input_output_aliases={i: i for i in range(2 * n)},
        compiler_params=pltpu.CompilerParams(has_side_effects=EFFECT),
    )(*srcs, *lands, send_sems, recv_sems, after)
    return out[:n], out[n:]


def _with_own(lands, own, slot):
    zero = jnp.zeros((), jnp.int32)
    return [lax.dynamic_update_slice(land, o[None], (slot.astype(jnp.int32),) + (zero,) * o.ndim)
            for land, o in zip(lands, own)]


def _adamw(w, g, m, v):
    m = ADAM_B1 * m + (1.0 - ADAM_B1) * g
    v = ADAM_B2 * v + (1.0 - ADAM_B2) * (g * g)
    m_hat = m / (1.0 - ADAM_B1 ** ADAM_STEP)
    v_hat = v / (1.0 - ADAM_B2 ** ADAM_STEP)
    delta = -ADAM_LR * (m_hat / (jnp.sqrt(v_hat) + ADAM_EPS) + ADAM_WD * w)
    return delta, m, v


def _sum_and_update(parts, w, m, v, *, name, tie=None):
    _, rows, cols = w.shape
    tr = rows // 2

    def body(p_ref, w_ref, m_ref, v_ref, *rest):
        g_ref, d_ref, mo_ref, vo_ref = rest[-4:]
        g = p_ref[0].astype(F32)
        for s in range(1, N_DEV):
            g = g + p_ref[s].astype(F32)
        g_ref[0] = g
        d_ref[0], mo_ref[0], vo_ref[0] = _adamw(w_ref[0], g, m_ref[0], v_ref[0])

    flat = pl.BlockSpec((1, tr, cols), lambda i: (0, i, 0))
    return pl.pallas_call(
        body,
        name=name,
        grid=(rows // tr,),
        in_specs=[pl.BlockSpec((N_DEV, tr, cols), lambda i: (0, i, 0)), flat, flat, flat]
        + ([] if tie is None else [pl.BlockSpec(memory_space=pl.ANY)]),
        out_specs=[flat] * 4,
        out_shape=[jax.ShapeDtypeStruct((1, rows, cols), F32)] * 4,
        compiler_params=_params("parallel"),
    )(parts, w, m, v, *([] if tie is None else [tie]))


VEC_ROWS = 8
ROW_LOGITS, ROW_LOSS = 5, 7


def _vectors_update(part, w, m, v, *, name, tie):
    def body(p_ref, w_ref, m_ref, v_ref, tie_ref, g_ref, d_ref, mo_ref, vo_ref, loss_ref, all_ref, send_sems, recv_sems):
        me = _place()
        all_ref[_slot(*me)] = p_ref[...]
        copies = []
        for d in range(1, N_DEV):
            peer = _flipped(me, d)
            copies.append(pltpu.make_async_remote_copy(
                src_ref=p_ref, dst_ref=all_ref.at[_slot(*me)], send_sem=send_sems.at[d - 1], recv_sem=recv_sems.at[d - 1],
                device_id=peer, device_id_type=MESH))
        for cp in copies:
            cp.start()
        for cp in copies:
            cp.wait()
        total = all_ref[0]
        for s in range(1, N_DEV):
            total = total + all_ref[s]
        wv = w_ref[...]
        half = D_MODEL // 2
        lb = _sigmoid(wv[ROW_LOGITS:ROW_LOGITS + 1, :half] - wv[ROW_LOGITS:ROW_LOGITS + 1, half:])
        d_first = total[ROW_LOGITS:ROW_LOGITS + 1, :half] * lb * (1.0 - lb)
        d_logits = jnp.concatenate([d_first, -d_first], axis=1)
        rowi = lax.broadcasted_iota(jnp.int32, (VEC_ROWS, D_MODEL), 0)
        g = jnp.where(rowi == ROW_LOGITS, d_logits, jnp.where(rowi < ROW_LOGITS, total, 0.0))
        g_ref[...] = g
        d_ref[...], mo_ref[...], vo_ref[...] = _adamw(wv, g, m_ref[...], v_ref[...])
        loss_ref[...] = total[ROW_LOSS:ROW_LOSS + 1, :]

    vmem = pl.BlockSpec(memory_space=pltpu.VMEM)
    return pl.pallas_call(
        body,
        name=name,
        in_specs=[vmem] * 4 + [HBM],
        out_specs=[vmem] * 5,
        out_shape=[jax.ShapeDtypeStruct((VEC_ROWS, D_MODEL), F32)] * 4 + [jax.ShapeDtypeStruct((1, D_MODEL), F32)],
        scratch_shapes=[pltpu.VMEM((N_DEV, VEC_ROWS, D_MODEL), F32), pltpu.SemaphoreType.DMA((7,)),
                        pltpu.SemaphoreType.DMA((7,))],
    )(part, w, m, v, tie)


TRANSPOSED = ("g1t", "u1t", "g2t", "u2t", "int")


def _vector_rows(rows):
    rowi = lax.broadcasted_iota(jnp.int32, (VEC_ROWS, D_MODEL), 0)
    out = jnp.zeros((VEC_ROWS, D_MODEL), F32)
    for i, r in enumerate(rows):
        if r is not None:
            out = jnp.where(rowi == i, r, out)
    return out


def kernel(x, ffn1_norm, ffn1_w_gate, ffn1_w_up, ffn1_w_down, mix_norm, w_in, sb_out_norm, hg_lower_bound_logits, hg_out_norm, w_out, ffn2_norm, ffn2_w_gate, ffn2_w_up, ffn2_w_down, final_norm, loss_target, m_ffn1_norm, m_ffn1_w_gate, m_ffn1_w_up, m_ffn1_w_down, m_mix_norm, m_w_in, m_sb_out_norm, m_hg_lower_bound_logits, m_hg_out_norm, m_w_out, m_ffn2_norm, m_ffn2_w_gate, m_ffn2_w_up, m_ffn2_w_down, m_final_norm, v_ffn1_norm, v_ffn1_w_gate, v_ffn1_w_up, v_ffn1_w_down, v_mix_norm, v_w_in, v_sb_out_norm, v_hg_lower_bound_logits, v_hg_out_norm, v_w_out, v_ffn2_norm, v_ffn2_w_gate, v_ffn2_w_up, v_ffn2_w_down, v_final_norm):
    def matrices(g1, u1, d1, win, wout, g2, u2, d2):
        return {"g1t": g1, "u1t": u1, "d1": d1, "int": win, "out": wout, "g2t": g2, "u2t": u2, "d2": d2}

    def vectors(n1, nm, nsb, lg, nhg, n2, nf):
        return [n1, nm, n2, nf.reshape(1, D_MODEL), jnp.concatenate([nsb, nhg], axis=1), lg.reshape(1, D_MODEL), None, None]

    w_sh = matrices(ffn1_w_gate, ffn1_w_up, ffn1_w_down, w_in, w_out, ffn2_w_gate, ffn2_w_up, ffn2_w_down)
    m_sh = matrices(m_ffn1_w_gate, m_ffn1_w_up, m_ffn1_w_down, m_w_in, m_w_out, m_ffn2_w_gate, m_ffn2_w_up, m_ffn2_w_down)
    v_sh = matrices(v_ffn1_w_gate, v_ffn1_w_up, v_ffn1_w_down, v_w_in, v_w_out, v_ffn2_w_gate, v_ffn2_w_up, v_ffn2_w_down)
    keys = list(w_sh)

    slot = _slot(*_place())

    def full(key, stack):
        return stack.reshape(-1, D_MODEL)

    def by_owner(key, grad):
        return grad.reshape(N_DEV, -1, D_MODEL)

    def view(key, a):
        return jnp.swapaxes(a, 1, 2) if key in TRANSPOSED else a

    blocks = {k: view(k, w_sh[k])[0].astype(BF16) for k in keys}
    first, mid, last = ("g1t", "u1t", "d1"), ("int", "out"), ("g2t", "u2t", "d2")
    w_first = {k: full(k, s) for k, s in zip(first, _all_gather([blocks[k] for k in first], name="gather_ffn1"))}
    flights = {}
    flights["ffn1"], token_mid = _copies_start([blocks[k] for k in mid], name="gather_mid_start", by_owner=False,
                                               after=w_first["d1"])
    flights["mix"], token_last = _copies_start([blocks[k] for k in last], name="gather_ffn2_start", by_owner=False,
                                               after=token_mid)

    def weights_after(stage, result):
        group = mid if stage == "ffn1" else last
        own, lands = _copies_wait(flights[stage], result, name="gather_" + stage + "_wait", by_owner=False)
        return {k: full(k, s) for k, s in zip(group, _with_own(lands, own, slot))}

    groups = {"mix": ("g2t", "u2t", "d2", "out"), "in": ("int",), "g1t": ("g1t",), "u1t": ("u1t",), "d1": ("d1",)}
    sent, sent_tokens = {}, []

    def grads_ready(stage, gw):
        stacks = [by_owner(k, gw[k]) for k in groups[stage]]
        flight, token = _copies_start(stacks, name="grads_" + stage + "_start", by_owner=True)
        sent[stage] = flight
        sent_tokens.append(token)
        return token

    norms = {"ffn1": ffn1_norm + token_last[0, 0], "mix": mix_norm, "sb": sb_out_norm, "hg": hg_out_norm,
             "ffn2": ffn2_norm, "final": final_norm.reshape(1, D_MODEL)}
    loss_row, grad_x, gw, gv = _local_step(x[0], loss_target[0], norms, hg_lower_bound_logits, w_first, weights_after,
                                           grads_ready)

    lb_row = jnp.concatenate([gv["lb"], jnp.zeros_like(gv["lb"])], axis=1)
    part = _vector_rows([gv["ffn1"], gv["mix"], gv["ffn2"], gv["final"], jnp.concatenate([gv["sb"], gv["hg"]], axis=1),
                         lb_row, None, loss_row])
    vec_w = _vector_rows(vectors(ffn1_norm, mix_norm, sb_out_norm, hg_lower_bound_logits, hg_out_norm, ffn2_norm, final_norm))
    vec_m = _vector_rows(vectors(m_ffn1_norm, m_mix_norm, m_sb_out_norm, m_hg_lower_bound_logits, m_hg_out_norm,
                                 m_ffn2_norm, m_final_norm))
    vec_v = _vector_rows(vectors(v_ffn1_norm, v_mix_norm, v_sb_out_norm, v_hg_lower_bound_logits, v_hg_out_norm,
                                 v_ffn2_norm, v_final_norm))
    updated, after = {}, sent_tokens[-1]
    for stage, flight in sent.items():
        if stage == list(sent)[-1]:
            *vecs, loss_out = _vectors_update(part, vec_w, vec_m, vec_v, name="vectors_update", tie=after)
            after = loss_out
        stacks, lands = _copies_wait(flight, after, name="grads_" + stage + "_wait", by_owner=True)
        own = [lax.dynamic_index_in_dim(s, slot, keepdims=False) for s in stacks]
        for k, part_k in zip(groups[stage], _with_own(lands, own, slot)):
            updated[k] = _sum_and_update(part_k, view(k, w_sh[k]), view(k, m_sh[k]), view(k, v_sh[k]), name="adamw_" + k,
                                         tie=after)
            after = updated[k][0]
    mats = [{k: view(k, updated[k][i]) for k in keys} for i in range(4)]

    def leaves(mat, vec):
        half = D_MODEL // 2
        return (
            vec[0:1], mat["g1t"], mat["u1t"], mat["d1"], vec[1:2], mat["int"], vec[4:5, :half],
            vec[ROW_LOGITS].reshape(2, half), vec[4:5, half:], mat["out"], vec[2:3], mat["g2t"], mat["u2t"],
            mat["d2"], vec[3],
        )

    out = [loss_out[0, 0], grad_x[None]]
    for mat, vec in zip(mats, vecs):
        out.extend(leaves(mat, vec))
    return tuple(out)
```

```python
import jax
import jax.numpy as jnp
from jax import lax
from jax.experimental import pallas as pl
from jax.experimental.pallas import tpu as pltpu

F32, BF16 = jnp.float32, jnp.bfloat16
D_MODEL = 1024
D_FF = 2816
SB_WIDTH = 512
HG_WIDTH = 512
SB_HEAD_DIM = 64
HG_HEAD_DIM = 128
IN_COLS = 3584
EPS = 1e-6
N_DEV = 8
LANES = 128
HG_CHUNK = 16
VMEM_LIMIT_BYTES = 48 * 1024 * 1024
FFN_BWD_VMEM_LIMIT_BYTES = 56 * 1024 * 1024
ADAM_LR, ADAM_B1, ADAM_B2, ADAM_EPS, ADAM_WD, ADAM_STEP = 0.001, 0.9, 0.999, 1e-08, 0.01, 10
MESH = pl.DeviceIdType.MESH


def _params(*semantics, vmem_limit_bytes=VMEM_LIMIT_BYTES):
    return pltpu.CompilerParams(dimension_semantics=semantics, vmem_limit_bytes=vmem_limit_bytes)


def _dot(a, b):
    return jnp.dot(a, b, preferred_element_type=F32)


def _dot_nt(a, b):
    return lax.dot_general(a, b, (((1,), (1,)), ((), ())), preferred_element_type=F32)


def _dot_tn(a, b):
    return lax.dot_general(a, b, (((0,), (0,)), ((), ())), preferred_element_type=F32)


def _split3(x):
    hi = x.astype(BF16)
    r1 = x - hi.astype(F32)
    mid = r1.astype(BF16)
    lo = (r1 - mid.astype(F32)).astype(BF16)
    return hi, mid, lo


def _rms(xv):
    rstd = lax.rsqrt(jnp.mean(xv * xv, axis=-1, keepdims=True) + EPS)
    return xv * rstd, rstd


def _sigmoid(x):
    return 0.5 + 0.5 * jnp.tanh(0.5 * x)


def _loss_terms(xv, gain, target):
    xhat, rstd = _rms(xv)
    err = xhat * gain - target
    loss = 0.5 * jnp.sum(jnp.mean(err * err, axis=-1, keepdims=True), axis=0, keepdims=True)
    dy = err * (1.0 / xv.shape[-1])
    dxh = dy * gain
    dx = rstd * (dxh - xhat * jnp.mean(dxh * xhat, axis=-1, keepdims=True))
    return dx, jnp.sum(dy * xhat, axis=0, keepdims=True), loss


def _mm(a, b, *, name, tm, tn, nt=False, ta=False, out_dtype=F32, tie=None):
    k, m = a.shape if ta else a.shape[::-1]
    n = b.shape[0] if nt else b.shape[1]
    assert m % tm == 0 and n % tn == 0 and not (nt and ta), (name, a.shape, b.shape, tm, tn)

    def body(a_ref, b_ref, *rest):
        av = a_ref[...].astype(BF16)
        bv = b_ref[...].astype(BF16)
        rest[-1][...] = (_dot_nt(av, bv) if nt else _dot_tn(av, bv) if ta else _dot(av, bv)).astype(out_dtype)

    in_specs = [
        pl.BlockSpec((k, tm), lambda i, j: (0, i)) if ta else pl.BlockSpec((tm, k), lambda i, j: (i, 0)),
        pl.BlockSpec((tn, k), lambda i, j: (j, 0)) if nt else pl.BlockSpec((k, tn), lambda i, j: (0, j)),
    ]
    operands = [a, b]
    if tie is not None:
        in_specs.append(pl.BlockSpec(memory_space=pl.ANY))
        operands.append(tie)
    return pl.pallas_call(
        body,
        name=name,
        grid=(m // tm, n // tn),
        in_specs=in_specs,
        out_specs=pl.BlockSpec((tm, tn), lambda i, j: (i, j)),
        out_shape=jax.ShapeDtypeStruct((m, n), out_dtype),
        compiler_params=_params("parallel", "parallel"),
    )(*operands)


def _ffn_fwd(x, gain, wgt, wut, wd, *, name, next_gain=None, head=None, tm=1024, tf=256):
    t = x.shape[0]
    nj = D_FF // tf
    extra_in = [] if next_gain is None else [next_gain]
    extra_in += [] if head is None else list(head)

    def body(x_ref, g_ref, wg_ref, wu_ref, wd_prev_ref, wd_last_ref, *rest):
        extra, (xo_ref, a_ref, b_ref, h_ref, st_ref) = rest[:len(extra_in)], rest[len(extra_in):len(extra_in) + 5]
        tail_out, (acc, s_prev) = rest[len(extra_in) + 5:-2], rest[-2:]
        i = pl.program_id(0)
        j = pl.program_id(1)

        @pl.when(j == 0)
        def _():
            xhat, _ = _rms(x_ref[...])
            h_ref[...] = (xhat * g_ref[...]).astype(BF16)
            acc[...] = jnp.zeros_like(acc)
            s_prev[...] = jnp.zeros_like(s_prev)

        acc[...] += _dot(s_prev[...], wd_prev_ref[...])
        h = h_ref[...]
        a = _dot_nt(h, wg_ref[...])
        b = _dot_nt(h, wu_ref[...])
        a_ref[...] = a.astype(BF16)
        b_ref[...] = b.astype(BF16)
        s = (a * _sigmoid(a) * b).astype(BF16)
        st_ref[...] = s
        s_prev[...] = s

        @pl.when(j == nj - 1)
        def _():
            xo = x_ref[...] + 0.5 * (acc[...] + _dot(s, wd_last_ref[...]))
            if head is None:
                xo_ref[...] = xo
            if next_gain is not None:
                tail_out[0][...] = (_rms(xo)[0] * extra[0][...]).astype(BF16)
            if head is not None:
                gain_ref, target_ref = extra[-2:]
                half_ref, dg_ref, loss_ref = tail_out[-3:]
                dxo, part_g, part_loss = _loss_terms(xo, gain_ref[...], target_ref[...])
                xo_ref[...] = dxo
                half_ref[...] = (0.5 * dxo).astype(BF16)

                @pl.when(i == 0)
                def _():
                    dg_ref[...] = part_g
                    loss_ref[...] = jnp.broadcast_to(part_loss, loss_ref.shape)

                @pl.when(i > 0)
                def _():
                    dg_ref[...] += part_g
                    loss_ref[...] += jnp.broadcast_to(part_loss, loss_ref.shape)

    row = pl.BlockSpec((tm, D_MODEL), lambda i, j: (i, 0))
    vec = pl.BlockSpec((1, D_MODEL), lambda i, j: (0, 0))
    tile = pl.BlockSpec((tm, tf), lambda i, j: (i, j))
    weights = pl.BlockSpec((tf, D_MODEL), lambda i, j: (j, 0))
    tail_specs = ([] if next_gain is None else [row]) + ([] if head is None else [row, vec, vec])
    tail_shapes = ([] if next_gain is None else [jax.ShapeDtypeStruct((t, D_MODEL), BF16)]) + (
        [] if head is None else
        [jax.ShapeDtypeStruct((t, D_MODEL), BF16)] + [jax.ShapeDtypeStruct((1, D_MODEL), F32)] * 2)
    return pl.pallas_call(
        body,
        name=name,
        grid=(t // tm, nj),
        in_specs=[
            row, vec, weights, weights,
            pl.BlockSpec((tf, D_MODEL), lambda i, j: (jnp.maximum(j - 1, 0), 0)),
            pl.BlockSpec((tf, D_MODEL), lambda i, j: (nj - 1, 0)),
        ] + ([] if next_gain is None else [vec]) + ([] if head is None else [vec, row]),
        out_specs=[row, tile, tile, row, tile] + tail_specs,
        out_shape=[
            jax.ShapeDtypeStruct((t, D_MODEL), F32),
            jax.ShapeDtypeStruct((t, D_FF), BF16),
            jax.ShapeDtypeStruct((t, D_FF), BF16),
            jax.ShapeDtypeStruct((t, D_MODEL), BF16),
            jax.ShapeDtypeStruct((t, D_FF), BF16),
        ] + tail_shapes,
        scratch_shapes=[pltpu.VMEM((tm, D_MODEL), F32), pltpu.VMEM((tm, tf), BF16)],
        compiler_params=_params("arbitrary", "arbitrary"),
    )(x, gain, wgt, wut, wd, wd, *extra_in)


def _ffn_bwd(dout, dout_half, x, gain, a, b, wgt, wut, wd, *, name, tm=1024, tf=256):
    t = x.shape[0]
    nj = D_FF // tf

    def body(do_ref, dob_ref, x_ref, g_ref, a_ref, b_ref, wg_prev_ref, wu_prev_ref, wg_last_ref, wu_last_ref, wd_ref,
             dx_ref, dg_ref, da_ref, db_ref, dh, da_prev, db_prev):
        i = pl.program_id(0)
        j = pl.program_id(1)

        @pl.when(j == 0)
        def _():
            dh[...] = jnp.zeros_like(dh)
            da_prev[...] = jnp.zeros_like(da_prev)
            db_prev[...] = jnp.zeros_like(db_prev)

        dh[...] += _dot(da_prev[...], wg_prev_ref[...]) + _dot(db_prev[...], wu_prev_ref[...])
        ds = _dot_nt(dob_ref[...], wd_ref[...])
        av = a_ref[...].astype(F32)
        bv = b_ref[...].astype(F32)
        sig = _sigmoid(av)
        dbv = (ds * (av * sig)).astype(BF16)
        dav = (ds * bv * (sig * (1.0 + av * (1.0 - sig)))).astype(BF16)
        da_ref[...] = dav
        db_ref[...] = dbv
        da_prev[...] = dav
        db_prev[...] = dbv

        @pl.when(j == nj - 1)
        def _():
            xhat, rstd = _rms(x_ref[...])
            dhv = dh[...] + _dot(dav, wg_last_ref[...]) + _dot(dbv, wu_last_ref[...])
            part = jnp.sum(dhv * xhat, axis=0, keepdims=True)

            @pl.when(i == 0)
            def _():
                dg_ref[...] = part

            @pl.when(i > 0)
            def _():
                dg_ref[...] += part

            dxh = dhv * g_ref[...]
            dx_ref[...] = do_ref[...] + rstd * (dxh - xhat * jnp.mean(dxh * xhat, axis=-1, keepdims=True))

    return pl.pallas_call(
        body,
        name=name,
        grid=(t // tm, nj),
        in_specs=[
            pl.BlockSpec((tm, D_MODEL), lambda i, j: (i, 0)),
            pl.BlockSpec((tm, D_MODEL), lambda i, j: (i, 0)),
            pl.BlockSpec((tm, D_MODEL), lambda i, j: (i, 0)),
            pl.BlockSpec((1, D_MODEL), lambda i, j: (0, 0)),
            pl.BlockSpec((tm, tf), lambda i, j: (i, j)),
            pl.BlockSpec((tm, tf), lambda i, j: (i, j)),
            pl.BlockSpec((tf, D_MODEL), lambda i, j: (jnp.maximum(j - 1, 0), 0)),
            pl.BlockSpec((tf, D_MODEL), lambda i, j: (jnp.maximum(j - 1, 0), 0)),
            pl.BlockSpec((tf, D_MODEL), lambda i, j: (nj - 1, 0)),
            pl.BlockSpec((tf, D_MODEL), lambda i, j: (nj - 1, 0)),
            pl.BlockSpec((tf, D_MODEL), lambda i, j: (j, 0)),
        ],
        out_specs=[
            pl.BlockSpec((tm, D_MODEL), lambda i, j: (i, 0)),
            pl.BlockSpec((1, D_MODEL), lambda i, j: (0, 0)),
            pl.BlockSpec((tm, tf), lambda i, j: (i, j)),
            pl.BlockSpec((tm, tf), lambda i, j: (i, j)),
        ],
        out_shape=[
            jax.ShapeDtypeStruct((t, D_MODEL), F32),
            jax.ShapeDtypeStruct((1, D_MODEL), F32),
            jax.ShapeDtypeStruct((t, D_FF), BF16),
            jax.ShapeDtypeStruct((t, D_FF), BF16),
        ],
        scratch_shapes=[pltpu.VMEM((tm, D_MODEL), F32), pltpu.VMEM((tm, tf), BF16), pltpu.VMEM((tm, tf), BF16)],
        compiler_params=_params("arbitrary", "arbitrary", vmem_limit_bytes=FFN_BWD_VMEM_LIMIT_BYTES),
    )(dout, dout_half, x, gain, a, b, wgt, wut, wgt, wut, wd)


def _in_proj_bwd(dproj, w_int, x, gain, dres, *, name, tm=512):
    t, k = dproj.shape

    def body(dp_ref, w_ref, x_ref, g_ref, dr_ref, dx_ref, dg_ref, dxb_ref):
        i = pl.program_id(0)
        dhv = _dot(dp_ref[...], w_ref[...])
        xhat, rstd = _rms(x_ref[...])
        part = jnp.sum(dhv * xhat, axis=0, keepdims=True)

        @pl.when(i == 0)
        def _():
            dg_ref[...] = part

        @pl.when(i > 0)
        def _():
            dg_ref[...] += part

        dxh = dhv * g_ref[...]
        dx = dr_ref[...] + rstd * (dxh - xhat * jnp.mean(dxh * xhat, axis=-1, keepdims=True))
        dx_ref[...] = dx
        dxb_ref[...] = (0.5 * dx).astype(BF16)

    row = pl.BlockSpec((tm, D_MODEL), lambda i: (i, 0))
    vec = pl.BlockSpec((1, D_MODEL), lambda i: (0, 0))
    return pl.pallas_call(
        body,
        name=name,
        grid=(t // tm,),
        in_specs=[pl.BlockSpec((tm, k), lambda i: (i, 0)), pl.BlockSpec((k, D_MODEL), lambda i: (0, 0)), row, vec, row],
        out_specs=[row, vec, row],
        out_shape=[jax.ShapeDtypeStruct((t, D_MODEL), F32), jax.ShapeDtypeStruct((1, D_MODEL), F32),
                   jax.ShapeDtypeStruct((t, D_MODEL), BF16)],
        compiler_params=_params("arbitrary"),
    )(dproj, w_int, x, gain, dres)


ATT_Q_TILE = 512
ATT_K_BLOCK = 256


def _first_head_lanes():
    return lax.broadcasted_iota(jnp.int32, (1, LANES), 1) < SB_HEAD_DIM


def _stack_heads(x):
    first = _first_head_lanes()
    return jnp.concatenate([jnp.where(first, x, 0.0), jnp.where(first, 0.0, x)], axis=0)


def _unstack_heads(x, rows):
    return jnp.where(_first_head_lanes(), x[:rows], x[rows:])


def _rows_from(x, first, rows):
    return x if first == 0 else jnp.concatenate([x[first:rows], x[rows + first:]], axis=0)


def _rows_into(full, part, first, rows):
    if first == 0:
        return part
    n = rows - first
    return jnp.concatenate([full[:first], part[:n], full[rows:rows + first], part[n:]], axis=0)


def _tri(n, relation):
    r = lax.broadcasted_iota(jnp.int32, (n, n), 0)
    c = lax.broadcasted_iota(jnp.int32, (n, n), 1)
    return relation(r, c).astype(BF16)


def _scan_dot(x, tri):
    hi = x.astype(BF16)
    lo = (x - hi.astype(F32)).astype(BF16)
    return _dot(jnp.concatenate([hi, lo], axis=1), jnp.concatenate([tri, tri], axis=0))


def _log_terms(z):
    lbeta = jnp.minimum(z, 0.0) - jnp.log(1.0 + jnp.exp(-jnp.abs(z)))
    return lbeta, lbeta - z


def _attn_fwd(proj, *, name):
    t = proj.shape[0]
    tq, tk = ATT_Q_TILE, ATT_K_BLOCK
    diag = tq // tk
    n_pairs = SB_WIDTH // LANES

    def body(q_ref, k_ref, v_ref, o_ref, kept_ref):
        qi = pl.program_id(1)
        q = q_ref[...] * (SB_HEAD_DIM ** -0.5)
        qs = _stack_heads(q).astype(BF16)
        tri = _tri(tk, lambda j, s: j > s)
        trow = lax.broadcasted_iota(jnp.int32, (tq, tk), 0)
        scol = lax.broadcasted_iota(jnp.int32, (tq, tk), 1)

        def block(kb, carry, causal, first=0):
            acc, c = carry
            off = pl.multiple_of(kb * tk, tk)
            lbeta, lrest = _log_terms(_dot_nt(_rows_from(qs, first, tq), k_ref[pl.ds(off, tk), :].astype(BF16)))
            if causal is not None:
                lrest = jnp.where(causal, lrest, 0.0)
            w = jnp.exp(lbeta + (_scan_dot(lrest, tri) + _rows_from(c, first, tq)))
            if causal is not None:
                w = jnp.where(causal, w, 0.0)
            wb = w.astype(BF16)
            kept_ref[0, 0, kb] = _rows_into(jnp.zeros((2 * tq, tk), BF16), wb, first, tq)
            acc = _rows_into(acc, _rows_from(acc, first, tq) + _dot(wb, v_ref[pl.ds(off, tk), :].astype(BF16)), first, tq)
            return acc, _rows_into(c, _rows_from(c, first, tq) + jnp.sum(lrest, axis=1, keepdims=True), first, tq)

        carry = (jnp.zeros((2 * tq, LANES), F32), jnp.zeros((2 * tq, 1), F32))
        n_full = qi * diag
        for j in reversed(range(diag)):
            mask = ((scol + j * tk) < trow)[j * tk:]
            carry = block(n_full + j, carry, jnp.concatenate([mask, mask], axis=0), first=j * tk)

        def odd_tile(carry):
            for j in range(diag):
                carry = block(n_full - 1 - j, carry, None)
            return carry

        carry = lax.cond(qi % 2 == 1, odd_tile, lambda c: c, carry)
        last = n_full - 1 - (qi % 2) * diag

        def step(it, carry):
            for j in range(2 * diag):
                carry = block(last - (2 * diag * it + j), carry, None)
            return carry

        acc, _ = lax.fori_loop(0, qi // 2, step, carry)
        o_ref[...] = _unstack_heads(acc, tq)

    return pl.pallas_call(
        body,
        name=name,
        grid=(n_pairs, t // tq),
        in_specs=[
            pl.BlockSpec((tq, LANES), lambda p, i: (i, p)),
            pl.BlockSpec((t, LANES), lambda p, i: (0, n_pairs + p)),
            pl.BlockSpec((t, LANES), lambda p, i: (0, 2 * n_pairs + p)),
        ],
        out_specs=[pl.BlockSpec((tq, LANES), lambda p, i: (i, p)),
                   pl.BlockSpec((1, 1, t // tk, 2 * tq, tk), lambda p, i: (p, i, 0, 0, 0))],
        out_shape=[jax.ShapeDtypeStruct((t, SB_WIDTH), F32),
                   jax.ShapeDtypeStruct((n_pairs, t // tq, t // tk, 2 * tq, tk), BF16)],
        compiler_params=_params("parallel", "parallel"),
    )(proj, proj, proj)


def _attn_bwd(proj, kept, do, *, name, tie=None):
    t = proj.shape[0]
    tq, tk = ATT_Q_TILE, ATT_K_BLOCK
    diag = tq // tk
    n_pairs = SB_WIDTH // LANES
    scale = SB_HEAD_DIM ** -0.5

    def body(q_ref, k_ref, v_ref, kept_ref, do_ref, *rest):
        dq_ref, dk_ref, dv_ref = rest[-3:]
        qi = pl.program_id(1)

        @pl.when(qi == 0)
        def _():
            dk_ref[...] = jnp.zeros_like(dk_ref)
            dv_ref[...] = jnp.zeros_like(dv_ref)

        qs = _stack_heads(q_ref[...] * scale).astype(BF16)
        dos = _stack_heads(do_ref[...]).astype(BF16)
        before = _tri(tk, lambda s, j: s < j)
        trow = lax.broadcasted_iota(jnp.int32, (tq, tk), 0)
        scol = lax.broadcasted_iota(jnp.int32, (tq, tk), 1)

        def block(kb, carry, causal, first=0):
            dq, cg = carry
            off = pl.multiple_of(kb * tk, tk)
            q_rows, do_rows = _rows_from(qs, first, tq), _rows_from(dos, first, tq)
            wb = _rows_from(kept_ref[0, 0, kb], first, tq)
            kblk = k_ref[pl.ds(off, tk), :].astype(BF16)
            sig = _sigmoid(_dot_nt(q_rows, kblk))
            g = wb.astype(F32) * _dot_nt(do_rows, v_ref[pl.ds(off, tk), :].astype(BF16))
            prior = _scan_dot(g, before) + _rows_from(cg, first, tq)
            dz = g - sig * (g + prior)
            if causal is not None:
                dz = jnp.where(causal, dz, 0.0)
            dzb = dz.astype(BF16)
            dq = _rows_into(dq, _rows_from(dq, first, tq) + _dot(dzb, kblk), first, tq)
            dk_ref[pl.ds(off, tk), :] += _dot_tn(dzb, q_rows)
            dv_ref[pl.ds(off, tk), :] += _dot_tn(wb, do_rows)
            return dq, _rows_into(cg, _rows_from(cg, first, tq) + jnp.sum(g, axis=1, keepdims=True), first, tq)

        n_full = qi * diag

        def step(it, carry):
            for j in range(2 * diag):
                carry = block(2 * diag * it + j, carry, None)
            return carry

        def odd_tile(carry):
            for j in range(diag):
                carry = block(n_full - diag + j, carry, None)
            return carry

        carry = lax.fori_loop(0, qi // 2, step, (jnp.zeros((2 * tq, LANES), F32), jnp.zeros((2 * tq, 1), F32)))
        carry = lax.cond(qi % 2 == 1, odd_tile, lambda c: c, carry)
        for j in range(diag):
            mask = ((scol + j * tk) < trow)[j * tk:]
            carry = block(n_full + j, carry, jnp.concatenate([mask, mask], axis=0), first=j * tk)
        dq_ref[...] = (_unstack_heads(carry[0], tq) * scale).astype(BF16)

    tile_spec = pl.BlockSpec((tq, LANES), lambda p, i: (i, p))
    full_spec = pl.BlockSpec((t, LANES), lambda p, i: (0, p))
    return pl.pallas_call(
        body,
        name=name,
        grid=(n_pairs, t // tq),
        in_specs=[
            tile_spec,
            pl.BlockSpec((t, LANES), lambda p, i: (0, n_pairs + p)),
            pl.BlockSpec((t, LANES), lambda p, i: (0, 2 * n_pairs + p)),
            pl.BlockSpec((1, 1, t // tk, 2 * tq, tk), lambda p, i: (p, i, 0, 0, 0)),
            tile_spec,
        ] + ([] if tie is None else [pl.BlockSpec(memory_space=pl.ANY)]),
        out_specs=[tile_spec, full_spec, full_spec],
        out_shape=[jax.ShapeDtypeStruct((t, SB_WIDTH), BF16)] + [jax.ShapeDtypeStruct((t, SB_WIDTH), F32)] * 2,
        compiler_params=_params("arbitrary", "arbitrary"),
    )(proj, proj, proj, kept, do, *([] if tie is None else [tie]))


HG_BLOCK = 128
HG_HEADS = HG_WIDTH // HG_HEAD_DIM


def _chunk_mats(n):
    r = lax.broadcasted_iota(jnp.int32, (n, n), 0)
    c = lax.broadcasted_iota(jnp.int32, (n, n), 1)
    same = (r // HG_CHUNK) == (c // HG_CHUNK)
    upto = (same & (c <= r)).astype(BF16)
    whole = same.astype(BF16)
    onward = (same & (c >= r)).astype(BF16)
    return upto, whole, onward


def _rows_dot(mat, x):
    return _dot(jnp.concatenate([mat, mat, mat], axis=1), jnp.concatenate(_split3(x), axis=0))


def _split_heads(x):
    return jnp.stack([x[:, h * HG_HEAD_DIM:(h + 1) * HG_HEAD_DIM] for h in range(HG_HEADS)], axis=0)


def _merge_heads(x):
    return jnp.concatenate([x[h] for h in range(HG_HEADS)], axis=1)


def _lower_bound(lg_ref):
    lg = lg_ref[...]
    return _sigmoid(lg[0:1, :] - lg[1:2, :])


def _hgrn_prepare(q_ref, f_ref, lb, h, upto, whole):
    cols = slice(h * HG_HEAD_DIM, (h + 1) * HG_HEAD_DIM)
    lbh = lb[:, cols]
    sg = _sigmoid(f_ref[:, cols])
    forget = lbh + (1.0 - lbh) * sg
    logf = jnp.log(forget)
    kk = (1.0 - lbh) * (1.0 - sg)
    qv = q_ref[:, cols]
    qsig = _sigmoid(qv)
    qh = qv * qsig
    b = _rows_dot(upto, logf)
    blast = _rows_dot(whole, logf)
    return dict(lbh=lbh, sg=sg, forget=forget, kk=kk, qv=qv, qsig=qsig, qh=qh, b=b, eb=jnp.exp(b),
                ekb=jnp.exp(blast - b), dl=jnp.exp(blast))


def _hgrn_fwd(proj, logits, *, name):
    t = proj.shape[0]
    tb = HG_BLOCK
    nc = tb // HG_CHUNK
    hd = HG_HEAD_DIM

    def body(q_ref, f_ref, i_ref, lg_ref, o_ref, st_ref, state, qh_s, kk_s, b_s, qe_s, ke_s, dl_s):
        @pl.when(pl.program_id(0) == 0)
        def _():
            state[...] = jnp.zeros_like(state)

        lb = _lower_bound(lg_ref)
        upto, whole, _ = _chunk_mats(tb)
        for h in range(HG_HEADS):
            p = _hgrn_prepare(q_ref, f_ref, lb, h, upto, whole)
            qh_s[h] = p["qh"]
            kk_s[h] = p["kk"]
            b_s[h] = p["b"]
            qe_s[h] = (p["qh"] * p["eb"]).astype(BF16)
            ke_s[h] = (p["kk"] * p["ekb"]).astype(BF16)
            dl_s[h] = p["dl"]
        rowi = lax.broadcasted_iota(jnp.int32, (HG_HEADS, HG_CHUNK, hd), 1)

        def chunk(c, _):
            r0 = pl.multiple_of(c * HG_CHUNK, HG_CHUNK)
            rows = pl.ds(r0, HG_CHUNK)
            bc = b_s[:, rows, :]
            qc = qh_s[:, rows, :]
            kc = kk_s[:, rows, :]
            vc = _split_heads(i_ref[rows, :])
            s_in = state[...]
            st_ref[c] = s_in
            s_in_b = s_in.astype(BF16)
            qe = qe_s[:, rows, :]
            o = jnp.stack([_dot_nt(qe[h], s_in_b[h]) for h in range(HG_HEADS)], axis=0)
            for s in range(HG_CHUNK):
                pair = jnp.where(rowi >= s, qc * jnp.exp(bc - bc[:, s:s + 1, :]) * kc[:, s:s + 1, :], 0.0)
                o = o + jnp.sum(pair, axis=2, keepdims=True) * vc[:, s:s + 1, :]
            o_ref[rows, :] = _merge_heads(o)
            vcb = vc.astype(BF16)
            ke = ke_s[:, rows, :]
            update = jnp.stack([_dot_tn(vcb[h], ke[h]) for h in range(HG_HEADS)], axis=0)
            state[...] = s_in * dl_s[:, pl.ds(r0, 1), :] + update
            return 0

        lax.fori_loop(0, nc, chunk, 0, unroll=4)

    blk =lambda col: pl.BlockSpec((tb, HG_WIDTH), lambda i: (i, col))
    head_f32 = pltpu.VMEM((HG_HEADS, tb, hd), F32)
    head_bf16 = pltpu.VMEM((HG_HEADS, tb, hd), BF16)
    return pl.pallas_call(
        body,
        name=name,
        grid=(t // tb,),
        in_specs=[blk(3), blk(4), blk(5), pl.BlockSpec((2, HG_WIDTH), lambda i: (0, 0))],
        out_specs=[
            pl.BlockSpec((tb, HG_WIDTH), lambda i: (i, 0)),
            pl.BlockSpec((nc, HG_HEADS, hd, hd), lambda i: (i, 0, 0, 0)),
        ],
        out_shape=[
            jax.ShapeDtypeStruct((t, HG_WIDTH), F32),
            jax.ShapeDtypeStruct((t // HG_CHUNK, HG_HEADS, hd, hd), F32),
        ],
        scratch_shapes=[pltpu.VMEM((HG_HEADS, hd, hd), F32), head_f32, head_f32, head_f32, head_bf16, head_bf16,
                        head_f32],
        compiler_params=_params("arbitrary"),
    )(proj, proj, proj, logits)


def _hgrn_bwd(proj, logits, states, do, *, name):
    t = proj.shape[0]
    tb = HG_BLOCK
    nb = t // tb
    nc = tb // HG_CHUNK
    hd = HG_HEAD_DIM

    def body(q_ref, f_ref, i_ref, lg_ref, st_ref, do_ref, dq_ref, df_ref, di_ref, dlb_ref,
             dstate, qh_s, kk_s, b_s, eb_s, ekb_s, qe_s, ke_s, dl_s, dqh_s, dkk_s, dlf_s):
        step = pl.program_id(0)

        @pl.when(step == 0)
        def _():
            dstate[...] = jnp.zeros_like(dstate)
            dlb_ref[...] = jnp.zeros_like(dlb_ref)

        lb = _lower_bound(lg_ref)
        upto, whole, _ = _chunk_mats(tb)
        prepared = []
        for h in range(HG_HEADS):
            p = _hgrn_prepare(q_ref, f_ref, lb, h, upto, whole)
            prepared.append(p)
            qh_s[h] = p["qh"]
            kk_s[h] = p["kk"]
            b_s[h] = p["b"]
            eb_s[h] = p["eb"]
            ekb_s[h] = p["ekb"]
            qe_s[h] = (p["qh"] * p["eb"]).astype(BF16)
            ke_s[h] = (p["kk"] * p["ekb"]).astype(BF16)
            dl_s[h] = p["dl"]
        rowi = lax.broadcasted_iota(jnp.int32, (HG_CHUNK, hd), 0)
        r16 = lax.broadcasted_iota(jnp.int32, (HG_CHUNK, HG_CHUNK), 0)
        c16 = lax.broadcasted_iota(jnp.int32, (HG_CHUNK, HG_CHUNK), 1)
        onward = (c16 >= r16).astype(BF16)

        def chunk(it, _):
            c = nc - 1 - it
            r0 = pl.multiple_of(c * HG_CHUNK, HG_CHUNK)
            rows = pl.ds(r0, HG_CHUNK)
            for h in range(HG_HEADS):
                cols = slice(h * hd, (h + 1) * hd)
                bc = b_s[h, rows, :]
                qc = qh_s[h, rows, :]
                kc = kk_s[h, rows, :]
                vc = i_ref[rows, cols]
                doc = do_ref[rows, cols]
                s_in = st_ref[c, h]
                ds_out = dstate[h]
                ds_out_b = ds_out.astype(BF16)
                docb = doc.astype(BF16)
                dl_row = dl_s[h, pl.ds(r0, 1), :]
                dqh = _dot(docb, s_in.astype(BF16)) * eb_s[h, rows, :]
                dkk = _dot(vc.astype(BF16), ds_out_b) * ekb_s[h, rows, :]
                dv = _dot_nt(ke_s[h, rows, :], ds_out_b)
                db = dqh * qc - dkk * kc
                dwhole = jnp.sum(dkk * kc, axis=0, keepdims=True) + jnp.sum(ds_out * s_in, axis=0, keepdims=True) * dl_row
                dk_rows, dv_rows = [], []
                for s in range(HG_CHUNK):
                    keep = rowi >= s
                    e = jnp.exp(bc - bc[s:s + 1, :])
                    k_row = kc[s:s + 1, :]
                    pcol = jnp.sum(jnp.where(keep, qc * e * k_row, 0.0), axis=1, keepdims=True)
                    dpcol = jnp.sum(doc * vc[s:s + 1, :], axis=1, keepdims=True)
                    m = jnp.where(keep, e * dpcol, 0.0)
                    y = m * qc
                    dqh = dqh + m * k_row
                    db = db + y * k_row
                    dk_rows.append(jnp.sum(y, axis=0, keepdims=True))
                    dv_rows.append(jnp.sum(pcol * doc, axis=0, keepdims=True))
                dkk_pairs = jnp.concatenate(dk_rows, axis=0)
                dkk = dkk + dkk_pairs
                db = db - dkk_pairs * kc
                dv = dv + jnp.concatenate(dv_rows, axis=0)
                dqh_s[h, rows, :] = dqh
                dkk_s[h, rows, :] = dkk
                dlf_s[h, rows, :] = _rows_dot(onward, db) + dwhole
                di_ref[rows, cols] = dv.astype(BF16)
                dstate[h] = ds_out * dl_row + _dot_tn(docb, qe_s[h, rows, :])
            return 0

        lax.fori_loop(0, nc, chunk, 0, unroll=4)
        for h in range(HG_HEADS):
            cols = slice(h * hd, (h + 1) * hd)
            p = prepared[h]
            dq_ref[:, cols] = (dqh_s[h] * (p["qsig"] * (1.0 + p["qv"] * (1.0 - p["qsig"])))).astype(BF16)
            dforget = dlf_s[h] / p["forget"] - dkk_s[h]
            df_ref[:, cols] = (dforget * (1.0 - p["lbh"]) * p["sg"] * (1.0 - p["sg"])).astype(BF16)
            dlb_ref[:, cols] += jnp.sum(dforget * (1.0 - p["sg"]), axis=0, keepdims=True)

    blk = lambda col: pl.BlockSpec((tb, HG_WIDTH), lambda i: (nb - 1 - i, col))
    vec = pl.BlockSpec((1, HG_WIDTH), lambda i: (0, 0))
    head_f32 = pltpu.VMEM((HG_HEADS, tb, hd), F32)
    head_bf16 = pltpu.VMEM((HG_HEADS, tb, hd), BF16)
    return pl.pallas_call(
        body,
        name=name,
        grid=(nb,),
        in_specs=[
            blk(3), blk(4), blk(5),
            pl.BlockSpec((2, HG_WIDTH), lambda i: (0, 0)),
            pl.BlockSpec((nc, HG_HEADS, hd, hd), lambda i: (nb - 1 - i, 0, 0, 0)),
            blk(0),
        ],
        out_specs=[blk(0), blk(0), blk(0), vec],
        out_shape=[jax.ShapeDtypeStruct((t, HG_WIDTH), BF16)] * 3 + [jax.ShapeDtypeStruct((1, HG_WIDTH), F32)],
        scratch_shapes=[
            pltpu.VMEM((HG_HEADS, hd, hd), F32),
            head_f32, head_f32, head_f32, head_f32, head_f32, head_bf16, head_bf16, head_f32,
            head_f32, head_f32, head_f32,
        ],
        compiler_params=_params("arbitrary"),
    )(proj, proj, proj, logits, states, do)


def _group_mat(width, head_dim):
    r = lax.broadcasted_iota(jnp.int32, (width, width), 0)
    c = lax.broadcasted_iota(jnp.int32, (width, width), 1)
    return ((r // head_dim) == (c // head_dim)).astype(BF16)


def _head_mean(x, mat, head_dim):
    hi = x.astype(BF16)
    lo = (x - hi.astype(F32)).astype(BF16)
    return (_dot(hi, mat) + _dot(lo, mat)) * (1.0 / head_dim)


def _mix_out_fwd(o_sb, o_hg, proj, g_sb, g_hg, w_out, x1, *, name, tm=512):
    t = x1.shape[0]

    def body(osb_ref, ohg_ref, gate_ref, gsb_ref, ghg_ref, w_ref, x_ref, xo_ref, mt_ref):
        msb = _group_mat(SB_WIDTH, SB_HEAD_DIM)
        mhg = _group_mat(HG_WIDTH, HG_HEAD_DIM)
        osb = osb_ref[...]
        ohg = ohg_ref[...]
        nsb = osb * lax.rsqrt(_head_mean(osb * osb, msb, SB_HEAD_DIM) + EPS) * gsb_ref[...]
        gate = gate_ref[...]
        nhg = ohg * lax.rsqrt(_head_mean(ohg * ohg, mhg, HG_HEAD_DIM) + EPS) * ghg_ref[...] * (gate * _sigmoid(gate))
        mixed = jnp.concatenate([nsb, nhg], axis=1).astype(BF16)
        mt_ref[...] = mixed
        xo_ref[...] = x_ref[...] + _dot(mixed, w_ref[...])

    half = pl.BlockSpec((tm, SB_WIDTH), lambda i: (i, 0))
    vec = pl.BlockSpec((1, SB_WIDTH), lambda i: (0, 0))
    row = pl.BlockSpec((tm, D_MODEL), lambda i: (i, 0))
    return pl.pallas_call(
        body,
        name=name,
        grid=(t // tm,),
        in_specs=[half, half, pl.BlockSpec((tm, HG_WIDTH), lambda i: (i, 6)), vec, vec,
                  pl.BlockSpec((D_MODEL, D_MODEL), lambda i: (0, 0)), row],
        out_specs=[row, row],
        out_shape=[jax.ShapeDtypeStruct((t, D_MODEL), F32), jax.ShapeDtypeStruct((t, D_MODEL), BF16)],
        compiler_params=_params("parallel"),
    )(o_sb, o_hg, proj, g_sb, g_hg, w_out, x1)


def _mix_out_bwd(dx2, o_sb, o_hg, proj, g_sb, g_hg, w_out, *, name, tm=512):
    t = dx2.shape[0]

    def body(dx_ref, osb_ref, ohg_ref, gate_ref, gsb_ref, ghg_ref, w_ref, dosb_ref, dohg_ref, dgate_ref, dgsb_ref,
             dghg_ref, dxb_ref):
        i = pl.program_id(0)
        msb = _group_mat(SB_WIDTH, SB_HEAD_DIM)
        mhg = _group_mat(HG_WIDTH, HG_HEAD_DIM)
        dxb = dx_ref[...].astype(BF16)
        dxb_ref[...] = dxb
        dmixed = _dot_nt(dxb, w_ref[...])
        dnsb = dmixed[:, :SB_WIDTH]
        dy = dmixed[:, SB_WIDTH:]

        osb = osb_ref[...]
        rstd = lax.rsqrt(_head_mean(osb * osb, msb, SB_HEAD_DIM) + EPS)
        ohat = osb * rstd
        part_sb = jnp.sum(dnsb * ohat, axis=0, keepdims=True)
        dohat = dnsb * gsb_ref[...]
        dosb_ref[...] = rstd * (dohat - ohat * _head_mean(dohat * ohat, msb, SB_HEAD_DIM))

        ohg = ohg_ref[...]
        rstd = lax.rsqrt(_head_mean(ohg * ohg, mhg, HG_HEAD_DIM) + EPS)
        ohat = ohg * rstd
        gate = gate_ref[...]
        sig = _sigmoid(gate)
        dn = dy * (gate * sig)
        dgate_ref[...] = (dy * (ohat * ghg_ref[...]) * (sig * (1.0 + gate * (1.0 - sig)))).astype(BF16)
        part_hg = jnp.sum(dn * ohat, axis=0, keepdims=True)
        dohat = dn * ghg_ref[...]
        dohg_ref[...] = rstd * (dohat - ohat * _head_mean(dohat * ohat, mhg, HG_HEAD_DIM))

        @pl.when(i == 0)
        def _():
            dgsb_ref[...] = part_sb
            dghg_ref[...] = part_hg

        @pl.when(i > 0)
        def _():
            dgsb_ref[...] += part_sb
            dghg_ref[...] += part_hg

    half = pl.BlockSpec((tm, SB_WIDTH), lambda i: (i, 0))
    vec = pl.BlockSpec((1, SB_WIDTH), lambda i: (0, 0))
    row = pl.BlockSpec((tm, D_MODEL), lambda i: (i, 0))
    return pl.pallas_call(
        body,
        name=name,
        grid=(t // tm,),
        in_specs=[row, half, half, pl.BlockSpec((tm, HG_WIDTH), lambda i: (i, 6)), vec, vec,
                  pl.BlockSpec((D_MODEL, D_MODEL), lambda i: (0, 0))],
        out_specs=[half, half, half, vec, vec, row],
        out_shape=[jax.ShapeDtypeStruct((t, SB_WIDTH), F32)] * 2 + [jax.ShapeDtypeStruct((t, SB_WIDTH), BF16)]
        + [jax.ShapeDtypeStruct((1, SB_WIDTH), F32)] * 2 + [jax.ShapeDtypeStruct((t, D_MODEL), BF16)],
        compiler_params=_params("arbitrary"),
    )(dx2, o_sb, o_hg, proj, g_sb, g_hg, w_out)


def _local_step(x, target, norms, logits, w, weights_after=None, grads_ready=None):
    w = dict(w)
    x1, a1, b1, h1, s1, hm = _ffn_fwd(x, norms["ffn1"], w["g1t"], w["u1t"], w["d1"], name="ffn1_fwd",
                                      next_gain=norms["mix"])
    if weights_after is not None:
        w.update(weights_after("ffn1", x1))
    proj = _mm(hm, w["int"], name="in_proj", tm=512, tn=IN_COLS, nt=True)
    o_sb, sb_kept = _attn_fwd(proj, name="sb_attn_fwd")
    o_hg, states = _hgrn_fwd(proj, logits, name="hgrn2_fwd")
    x2, mixed = _mix_out_fwd(o_sb, o_hg, proj, norms["sb"], norms["hg"], w["out"], x1, name="mix_out_fwd")
    if weights_after is not None:
        w.update(weights_after("mix", x2))
    dx3, a2, b2, h2, s2, dob2, d_final, loss_row = _ffn_fwd(x2, norms["ffn2"], w["g2t"], w["u2t"], w["d2"],
                                                            name="ffn2_fwd", head=(norms["final"], target))

    def weight_grad(lhs, rhs, name, tie=None):
        return _mm(lhs, rhs, name=name, tm=256, tn=D_MODEL, ta=True, out_dtype=BF16, tie=tie)

    def sent(stage):
        return grads_ready(stage, gw) if grads_ready is not None else None

    gw, gv = {}, {"final": d_final}
    dx2, gv["ffn2"], da2, db2 = _ffn_bwd(dx3, dob2, x2, norms["ffn2"], a2, b2, w["g2t"], w["u2t"], w["d2"],
                                         name="ffn2_bwd")
    gw["g2t"] = weight_grad(da2, h2, "ffn2_dgate")
    gw["u2t"] = weight_grad(db2, h2, "ffn2_dup")
    gw["d2"] = weight_grad(s2, dob2, "ffn2_ddown")

    do_sb, do_hg, d_gate, gv["sb"], gv["hg"], dx2b = _mix_out_bwd(
        dx2, o_sb, o_hg, proj, norms["sb"], norms["hg"], w["out"], name="mix_out_bwd")
    gw["out"] = weight_grad(mixed, dx2b, "out_dw")
    tie = sent("mix")
    dq_sb, dk_sb, dv_sb = _attn_bwd(proj, sb_kept, do_sb, name="sb_attn_bwd", tie=tie)
    dq_hg, df_hg, di_hg, d_lb = _hgrn_bwd(proj, logits if tie is None else logits + tie[0, 0], states, do_hg,
                                          name="hgrn2_bwd")
    dproj = jnp.concatenate([dq_sb, dk_sb.astype(BF16), dv_sb.astype(BF16), dq_hg, df_hg, di_hg, d_gate], axis=1)
    gw["int"] = weight_grad(dproj, hm, "in_dw")
    tie = sent("in")
    dx1, gv["mix"], dob1 = _in_proj_bwd(dproj, w["int"], x1, norms["mix"] if tie is None else norms["mix"] + tie[0, 0],
                                        dx2, name="in_dx")

    gw["d1"] = weight_grad(s1, dob1, "ffn1_ddown")
    tie = sent("d1")
    dx, gv["ffn1"], da1, db1 = _ffn_bwd(dx1, dob1, x, norms["ffn1"] if tie is None else norms["ffn1"] + tie[0, 0],
                                        a1, b1, w["g1t"], w["u1t"], w["d1"], name="ffn1_bwd")
    gw["g1t"] = weight_grad(da1, h1, "ffn1_dgate")
    gw["u1t"] = weight_grad(db1, h1, "ffn1_dup", tie=sent("g1t"))
    sent("u1t")
    gv["lb"] = d_lb
    return loss_row, dx, gw, gv


HBM = pl.BlockSpec(memory_space=pl.ANY)


def _place():
    return lax.axis_index("x"), lax.axis_index("y"), lax.axis_index("c")


def _slot(px, py, pc):
    return 4 * px + 2 * py + pc


GATHER_COPIES = 8


def _all_gather(blocks, *, name):
    n = len(blocks)

    def body(*refs):
        ins, outs = refs[:n], refs[n:2 * n]
        send_sems, recv_sems, local_sems = refs[2 * n:]
        x, y, c = _place()
        me, sibling = (x, y, c), (x, y, 1 - c)
        beside, across, diagonal = (1 - x, y, c), (x, 1 - y, c), (1 - x, 1 - y, c)

        def copy(a, k, block, to, src=None, half=None):
            dst = outs[a].at[_slot(*block)]
            if half is not None:
                rows = blocks[a].shape[0] // 2
                dst = dst.at[pl.ds(half * rows, rows)]
            return pltpu.make_async_remote_copy(
                src_ref=dst if src is None else src, dst_ref=dst, send_sem=send_sems.at[GATHER_COPIES * a + k],
                recv_sem=recv_sems.at[GATHER_COPIES * a + k], device_id=to, device_id_type=MESH)

        mine = [pltpu.make_async_copy(ins[a], outs[a].at[_slot(*me)], local_sems.at[a]) for a in range(n)]
        for cp in mine:
            cp.start()
        sent = []
        for a in range(n):
            sent += [copy(a, 0, me, sibling, src=ins[a]), copy(a, 1, me, beside, src=ins[a]),
                     copy(a, 2, me, across, src=ins[a])]
        for cp in sent:
            cp.start()
        for a in range(n):
            copy(a, 1, beside, me).wait_recv()
            sent += [copy(a, 3, beside, across, half=0), copy(a, 5, beside, sibling)]
            sent[-2].start()
            sent[-1].start()
        for a in range(n):
            copy(a, 2, across, me).wait_recv()
            sent += [copy(a, 4, across, beside, half=1), copy(a, 6, across, sibling)]
            sent[-2].start()
            sent[-1].start()
        for a in range(n):
            copy(a, 3, diagonal, me, half=0).wait_recv()
            copy(a, 4, diagonal, me, half=1).wait_recv()
            sent.append(copy(a, 7, diagonal, sibling))
            sent[-1].start()
        for a in range(n):
            for k, origin in ((0, sibling), (5, (1 - x, y, 1 - c)), (6, (x, 1 - y, 1 - c)), (7, (1 - x, 1 - y, 1 - c))):
                copy(a, k, origin, me).wait_recv()
        for cp in sent:
            cp.wait_send()
        for cp in mine:
            cp.wait()

    return pl.pallas_call(
        body,
        name=name,
        in_specs=[HBM] * n,
        out_specs=[HBM] * n,
        out_shape=[jax.ShapeDtypeStruct((N_DEV,) + b.shape, b.dtype) for b in blocks],
        scratch_shapes=[pltpu.SemaphoreType.DMA((GATHER_COPIES * n,)), pltpu.SemaphoreType.DMA((GATHER_COPIES * n,)),
                        pltpu.SemaphoreType.DMA((n,))],
    )(*blocks)


def _flipped(place, d):
    return tuple(1 - p if (d >> (2 - axis)) & 1 else p for axis, p in enumerate(place))


SEM = pl.BlockSpec(memory_space=pltpu.SEMAPHORE)
EFFECT = pltpu.SideEffectType.DATAFLOW_SIDE_EFFECTING


def _split_copies(me, srcs, lands, send_sems, recv_sems, by_owner):
    copies = []
    for d in range(1, N_DEV):
        peer = _flipped(me, d)
        for a, (src, land) in enumerate(zip(srcs, lands)):
            copies.append(pltpu.make_async_remote_copy(
                src_ref=src.at[_slot(*peer)] if by_owner else src, dst_ref=land.at[_slot(*me)],
                send_sem=send_sems.at[7 * a + d - 1], recv_sem=recv_sems.at[7 * a + d - 1], device_id=peer,
                device_id_type=MESH))
    return copies


def _copies_start(srcs, *, name, by_owner, after=None):
    n = len(srcs)
    extra = [] if after is None else [after]
    land_shapes = [s.shape if by_owner else (N_DEV,) + s.shape for s in srcs]
    lands = [pltpu.with_memory_space_constraint(lax.empty(shape, s.dtype), pltpu.HBM) for shape, s in zip(land_shapes, srcs)]
    srcs = [pltpu.with_memory_space_constraint(s, pltpu.HBM) for s in srcs]

    def body(*refs):
        src_refs, land_refs = refs[:n], refs[n:2 * n]
        send_sems, recv_sems = refs[2 * n + len(extra)], refs[2 * n + len(extra) + 1]
        token = refs[-1]
        for cp in _split_copies(_place(), src_refs, land_refs, send_sems, recv_sems, by_owner):
            cp.start()
        token[...] = jnp.zeros_like(token)

    out = pl.pallas_call(
        body,
        name=name,
        in_specs=[HBM] * (2 * n + len(extra)),
        out_specs=[SEM, SEM] + [HBM] * (2 * n) + [pl.BlockSpec(memory_space=pltpu.VMEM)],
        out_shape=[pltpu.SemaphoreType.DMA((7 * n,)), pltpu.SemaphoreType.DMA((7 * n,))]
        + [pltpu.HBM(s.shape, s.dtype) for s in srcs] + [pltpu.HBM(shape, s.dtype) for shape, s in zip(land_shapes, srcs)]
        + [jax.ShapeDtypeStruct((8, LANES), F32)],
        input_output_aliases={i: 2 + i for i in range(2 * n)},
        compiler_params=pltpu.CompilerParams(has_side_effects=EFFECT),
    )(*srcs, *lands, *extra)
    return (out[0], out[1], out[2:2 + n], out[2 + n:2 + 2 * n]), out[-1]


def _copies_wait(started, after, *, name, by_owner):
    send_sems, recv_sems, srcs, lands = started
    n = len(srcs)

    def body(*refs):
        src_refs, land_refs = refs[:n], refs[n:2 * n]
        for cp in _split_copies(_place(), src_refs, land_refs, refs[2 * n], refs[2 * n + 1], by_owner):
            cp.wait_send()
            cp.wait_recv()

    out = pl.pallas_call(
        body,
        name=name,
        in_specs=[HBM] * (2 * n) + [SEM, SEM, HBM],
        out_specs=[HBM] * (2 * n),
        out_shape=[pltpu.HBM(s.shape, s.dtype) for s in srcs] + [pltpu.HBM(s.shape, s.dtype) for s in lands],
        input_output_aliases={i: i for i in range(2 * n)},
        compiler_params=pltpu.CompilerParams(has_side_effects=EFFECT),
    )(*srcs, *lands, send_sems, recv_sems, after)
    return out[:n], out[n:]


def _with_own(lands, own, slot):
    zero = jnp.zeros((), jnp.int32)
    return [lax.dynamic_update_slice(land, o[None], (slot.astype(jnp.int32),) + (zero,) * o.ndim)
            for land, o in zip(lands, own)]


def _adamw(w, g, m, v):
    m = ADAM_B1 * m + (1.0 - ADAM_B1) * g
    v = ADAM_B2 * v + (1.0 - ADAM_B2) * (g * g)
    m_hat = m / (1.0 - ADAM_B1 ** ADAM_STEP)
    v_hat = v / (1.0 - ADAM_B2 ** ADAM_STEP)
    delta = -ADAM_LR * (m_hat / (jnp.sqrt(v_hat) + ADAM_EPS) + ADAM_WD * w)
    return delta, m, v


def _sum_and_update(parts, w, m, v, *, name, tie=None):
    _, rows, cols = w.shape
    tr = rows // 2

    def body(p_ref, w_ref, m_ref, v_ref, *rest):
        g_ref, d_ref, mo_ref, vo_ref = rest[-4:]
        g = p_ref[0].astype(F32)
        for s in range(1, N_DEV):
            g = g + p_ref[s].astype(F32)
        g_ref[0] = g
        d_ref[0], mo_ref[0], vo_ref[0] = _adamw(w_ref[0], g, m_ref[0], v_ref[0])

    flat = pl.BlockSpec((1, tr, cols), lambda i: (0, i, 0))
    return pl.pallas_call(
        body,
        name=name,
        grid=(rows // tr,),
        in_specs=[pl.BlockSpec((N_DEV, tr, cols), lambda i: (0, i, 0)), flat, flat, flat]
        + ([] if tie is None else [pl.BlockSpec(memory_space=pl.ANY)]),
        out_specs=[flat] * 4,
        out_shape=[jax.ShapeDtypeStruct((1, rows, cols), F32)] * 4,
        compiler_params=_params("parallel"),
    )(parts, w, m, v, *([] if tie is None else [tie]))


VEC_ROWS = 8
ROW_LOGITS, ROW_LOSS = 5, 7


def _vectors_update(part, w, m, v, *, name, tie):
    def body(p_ref, w_ref, m_ref, v_ref, tie_ref, g_ref, d_ref, mo_ref, vo_ref, loss_ref, all_ref, send_sems, recv_sems):
        me = _place()
        all_ref[_slot(*me)] = p_ref[...]
        copies = []
        for d in range(1, N_DEV):
            peer = _flipped(me, d)
            copies.append(pltpu.make_async_remote_copy(
                src_ref=p_ref, dst_ref=all_ref.at[_slot(*me)], send_sem=send_sems.at[d - 1], recv_sem=recv_sems.at[d - 1],
                device_id=peer, device_id_type=MESH))
        for cp in copies:
            cp.start()
        for cp in copies:
            cp.wait()
        total = all_ref[0]
        for s in range(1, N_DEV):
            total = total + all_ref[s]
        wv = w_ref[...]
        half = D_MODEL // 2
        lb = _sigmoid(wv[ROW_LOGITS:ROW_LOGITS + 1, :half] - wv[ROW_LOGITS:ROW_LOGITS + 1, half:])
        d_first = total[ROW_LOGITS:ROW_LOGITS + 1, :half] * lb * (1.0 - lb)
        d_logits = jnp.concatenate([d_first, -d_first], axis=1)
        rowi = lax.broadcasted_iota(jnp.int32, (VEC_ROWS, D_MODEL), 0)
        g = jnp.where(rowi == ROW_LOGITS, d_logits, jnp.where(rowi < ROW_LOGITS, total, 0.0))
        g_ref[...] = g
        d_ref[...], mo_ref[...], vo_ref[...] = _adamw(wv, g, m_ref[...], v_ref[...])
        loss_ref[...] = total[ROW_LOSS:ROW_LOSS + 1, :]

    vmem = pl.BlockSpec(memory_space=pltpu.VMEM)
    return pl.pallas_call(
        body,
        name=name,
        in_specs=[vmem] * 4 + [HBM],
        out_specs=[vmem] * 5,
        out_shape=[jax.ShapeDtypeStruct((VEC_ROWS, D_MODEL), F32)] * 4 + [jax.ShapeDtypeStruct((1, D_MODEL), F32)],
        scratch_shapes=[pltpu.VMEM((N_DEV, VEC_ROWS, D_MODEL), F32), pltpu.SemaphoreType.DMA((7,)),
                        pltpu.SemaphoreType.DMA((7,))],
    )(part, w, m, v, tie)


TRANSPOSED = ("g1t", "u1t", "g2t", "u2t", "int")


def _vector_rows(rows):
    rowi = lax.broadcasted_iota(jnp.int32, (VEC_ROWS, D_MODEL), 0)
    out = jnp.zeros((VEC_ROWS, D_MODEL), F32)
    for i, r in enumerate(rows):
        if r is not None:
            out = jnp.where(rowi == i, r, out)
    return out


def kernel(x, ffn1_norm, ffn1_w_gate, ffn1_w_up, ffn1_w_down, mix_norm, w_in, sb_out_norm, hg_lower_bound_logits, hg_out_norm, w_out, ffn2_norm, ffn2_w_gate, ffn2_w_up, ffn2_w_down, final_norm, loss_target, m_ffn1_norm, m_ffn1_w_gate, m_ffn1_w_up, m_ffn1_w_down, m_mix_norm, m_w_in, m_sb_out_norm, m_hg_lower_bound_logits, m_hg_out_norm, m_w_out, m_ffn2_norm, m_ffn2_w_gate, m_ffn2_w_up, m_ffn2_w_down, m_final_norm, v_ffn1_norm, v_ffn1_w_gate, v_ffn1_w_up, v_ffn1_w_down, v_mix_norm, v_w_in, v_sb_out_norm, v_hg_lower_bound_logits, v_hg_out_norm, v_w_out, v_ffn2_norm, v_ffn2_w_gate, v_ffn2_w_up, v_ffn2_w_down, v_final_norm):
    def matrices(g1, u1, d1, win, wout, g2, u2, d2):
        return {"g1t": g1, "u1t": u1, "d1": d1, "int": win, "out": wout, "g2t": g2, "u2t": u2, "d2": d2}

    def vectors(n1, nm, nsb, lg, nhg, n2, nf):
        return [n1, nm, n2, nf.reshape(1, D_MODEL), jnp.concatenate([nsb, nhg], axis=1), lg.reshape(1, D_MODEL), None, None]

    w_sh = matrices(ffn1_w_gate, ffn1_w_up, ffn1_w_down, w_in, w_out, ffn2_w_gate, ffn2_w_up, ffn2_w_down)
    m_sh = matrices(m_ffn1_w_gate, m_ffn1_w_up, m_ffn1_w_down, m_w_in, m_w_out, m_ffn2_w_gate, m_ffn2_w_up, m_ffn2_w_down)
    v_sh = matrices(v_ffn1_w_gate, v_ffn1_w_up, v_ffn1_w_down, v_w_in, v_w_out, v_ffn2_w_gate, v_ffn2_w_up, v_ffn2_w_down)
    keys = list(w_sh)

    slot = _slot(*_place())

    def full(key, stack):
        return stack.reshape(-1, D_MODEL)

    def by_owner(key, grad):
        return grad.reshape(N_DEV, -1, D_MODEL)

    def view(key, a):
        return jnp.swapaxes(a, 1, 2) if key in TRANSPOSED else a

    blocks = {k: view(k, w_sh[k])[0].astype(BF16) for k in keys}
    first, mid, last = ("g1t", "u1t", "d1"), ("int", "out"), ("g2t", "u2t", "d2")
    w_first = {k: full(k, s) for k, s in zip(first, _all_gather([blocks[k] for k in first], name="gather_ffn1"))}
    flights = {}
    flights["ffn1"], token_mid = _copies_start([blocks[k] for k in mid], name="gather_mid_start", by_owner=False,
                                               after=w_first["d1"])
    flights["mix"], token_last = _copies_start([blocks[k] for k in last], name="gather_ffn2_start", by_owner=False,
                                               after=token_mid)

    def weights_after(stage, result):
        group = mid if stage == "ffn1" else last
        own, lands = _copies_wait(flights[stage], result, name="gather_" + stage + "_wait", by_owner=False)
        return {k: full(k, s) for k, s in zip(group, _with_own(lands, own, slot))}

    groups = {"mix": ("g2t", "u2t", "d2", "out"), "in": ("int",), "g1t": ("g1t",), "u1t": ("u1t",), "d1": ("d1",)}
    sent, sent_tokens = {}, []

    def grads_ready(stage, gw):
        stacks = [by_owner(k, gw[k]) for k in groups[stage]]
        flight, token = _copies_start(stacks, name="grads_" + stage + "_start", by_owner=True)
        sent[stage] = flight
        sent_tokens.append(token)
        return token

    norms = {"ffn1": ffn1_norm + token_last[0, 0], "mix": mix_norm, "sb": sb_out_norm, "hg": hg_out_norm,
             "ffn2": ffn2_norm, "final": final_norm.reshape(1, D_MODEL)}
    loss_row, grad_x, gw, gv = _local_step(x[0], loss_target[0], norms, hg_lower_bound_logits, w_first, weights_after,
                                           grads_ready)

    lb_row = jnp.concatenate([gv["lb"], jnp.zeros_like(gv["lb"])], axis=1)
    part = _vector_rows([gv["ffn1"], gv["mix"], gv["ffn2"], gv["final"], jnp.concatenate([gv["sb"], gv["hg"]], axis=1),
                         lb_row, None, loss_row])
    vec_w = _vector_rows(vectors(ffn1_norm, mix_norm, sb_out_norm, hg_lower_bound_logits, hg_out_norm, ffn2_norm, final_norm))
    vec_m = _vector_rows(vectors(m_ffn1_norm, m_mix_norm, m_sb_out_norm, m_hg_lower_bound_logits, m_hg_out_norm,
                                 m_ffn2_norm, m_final_norm))
    vec_v = _vector_rows(vectors(v_ffn1_norm, v_mix_norm, v_sb_out_norm, v_hg_lower_bound_logits, v_hg_out_norm,
                                 v_ffn2_norm, v_final_norm))
    updated, after = {}, sent_tokens[-1]
    for stage, flight in sent.items():
        if stage == list(sent)[-1]:
            *vecs, loss_out = _vectors_update(part, vec_w, vec_m, vec_v, name="vectors_update", tie=after)
            after = loss_out
        stacks, lands = _copies_wait(flight, after, name="grads_" + stage + "_wait", by_owner=True)
        own = [lax.dynamic_index_in_dim(s, slot, keepdims=False) for s in stacks]
        for k, part_k in zip(groups[stage], _with_own(lands, own, slot)):
            updated[k] = _sum_and_update(part_k, view(k, w_sh[k]), view(k, m_sh[k]), view(k, v_sh[k]), name="adamw_" + k,
                                         tie=after)
            after = updated[k][0]
    mats = [{k: view(k, updated[k][i]) for k in keys} for i in range(4)]

    def leaves(mat, vec):
        half = D_MODEL // 2
        return (
            vec[0:1], mat["g1t"], mat["u1t"], mat["d1"], vec[1:2], mat["int"], vec[4:5, :half],
            vec[ROW_LOGITS].reshape(2, half), vec[4:5, half:], mat["out"], vec[2:3], mat["g2t"], mat["u2t"],
            mat["d2"], vec[3],
        )

    out = [loss_out[0, 0], grad_x[None]]
    for mat, vec in zip(mats, vecs):
        out.extend(leaves(mat, vec))
    return tuple(out)
```

```python
import jax
import jax.numpy as jnp
from jax import lax
from jax.experimental import pallas as pl
from jax.experimental.pallas import tpu as pltpu

F32, BF16 = jnp.float32, jnp.bfloat16
D_MODEL = 1024
D_FF = 2816
SB_WIDTH = 512
HG_WIDTH = 512
SB_HEAD_DIM = 64
HG_HEAD_DIM = 128
IN_COLS = 3584
EPS = 1e-6
N_DEV = 8
LANES = 128
HG_CHUNK = 16
VMEM_LIMIT_BYTES = 48 * 1024 * 1024
FFN_BWD_VMEM_LIMIT_BYTES = 56 * 1024 * 1024
ADAM_LR, ADAM_B1, ADAM_B2, ADAM_EPS, ADAM_WD, ADAM_STEP = 0.001, 0.9, 0.999, 1e-08, 0.01, 10
MESH = pl.DeviceIdType.MESH


def _params(*semantics, vmem_limit_bytes=VMEM_LIMIT_BYTES):
    return pltpu.CompilerParams(dimension_semantics=semantics, vmem_limit_bytes=vmem_limit_bytes)


def _dot(a, b):
    return jnp.dot(a, b, preferred_element_type=F32)


def _dot_nt(a, b):
    return lax.dot_general(a, b, (((1,), (1,)), ((), ())), preferred_element_type=F32)


def _dot_tn(a, b):
    return lax.dot_general(a, b, (((0,), (0,)), ((), ())), preferred_element_type=F32)


def _split3(x):
    hi = x.astype(BF16)
    r1 = x - hi.astype(F32)
    mid = r1.astype(BF16)
    lo = (r1 - mid.astype(F32)).astype(BF16)
    return hi, mid, lo


def _rms(xv):
    rstd = lax.rsqrt(jnp.mean(xv * xv, axis=-1, keepdims=True) + EPS)
    return xv * rstd, rstd


def _sigmoid(x):
    return 0.5 + 0.5 * jnp.tanh(0.5 * x)


def _loss_terms(xv, gain, target):
    xhat, rstd = _rms(xv)
    err = xhat * gain - target
    loss = 0.5 * jnp.sum(jnp.mean(err * err, axis=-1, keepdims=True), axis=0, keepdims=True)
    dy = err * (1.0 / xv.shape[-1])
    dxh = dy * gain
    dx = rstd * (dxh - xhat * jnp.mean(dxh * xhat, axis=-1, keepdims=True))
    return dx, jnp.sum(dy * xhat, axis=0, keepdims=True), loss


def _mm(a, b, *, name, tm, tn, nt=False, ta=False, out_dtype=F32, tie=None):
    k, m = a.shape if ta else a.shape[::-1]
    n = b.shape[0] if nt else b.shape[1]
    assert m % tm == 0 and n % tn == 0 and not (nt and ta), (name, a.shape, b.shape, tm, tn)

    def body(a_ref, b_ref, *rest):
        av = a_ref[...].astype(BF16)
        bv = b_ref[...].astype(BF16)
        rest[-1][...] = (_dot_nt(av, bv) if nt else _dot_tn(av, bv) if ta else _dot(av, bv)).astype(out_dtype)

    in_specs = [
        pl.BlockSpec((k, tm), lambda i, j: (0, i)) if ta else pl.BlockSpec((tm, k), lambda i, j: (i, 0)),
        pl.BlockSpec((tn, k), lambda i, j: (j, 0)) if nt else pl.BlockSpec((k, tn), lambda i, j: (0, j)),
    ]
    operands = [a, b]
    if tie is not None:
        in_specs.append(pl.BlockSpec(memory_space=pl.ANY))
        operands.append(tie)
    return pl.pallas_call(
        body,
        name=name,
        grid=(m // tm, n // tn),
        in_specs=in_specs,
        out_specs=pl.BlockSpec((tm, tn), lambda i, j: (i, j)),
        out_shape=jax.ShapeDtypeStruct((m, n), out_dtype),
        compiler_params=_params("parallel", "parallel"),
    )(*operands)


def _ffn_fwd(x, gain, wgt, wut, wd, *, name, next_gain=None, head=None, tm=1024, tf=256):
    t = x.shape[0]
    nj = D_FF // tf
    extra_in = [] if next_gain is None else [next_gain]
    extra_in += [] if head is None else list(head)

    def body(x_ref, g_ref, wg_ref, wu_ref, wd_prev_ref, wd_last_ref, *rest):
        extra, (xo_ref, a_ref, b_ref, h_ref, st_ref) = rest[:len(extra_in)], rest[len(extra_in):len(extra_in) + 5]
        tail_out, (acc, s_prev) = rest[len(extra_in) + 5:-2], rest[-2:]
        i = pl.program_id(0)
        j = pl.program_id(1)

        @pl.when(j == 0)
        def _():
            xhat, _ = _rms(x_ref[...])
            h_ref[...] = (xhat * g_ref[...]).astype(BF16)
            acc[...] = jnp.zeros_like(acc)
            s_prev[...] = jnp.zeros_like(s_prev)

        acc[...] += _dot(s_prev[...], wd_prev_ref[...])
        h = h_ref[...]
        a = _dot_nt(h, wg_ref[...])
        b = _dot_nt(h, wu_ref[...])
        a_ref[...] = a.astype(BF16)
        b_ref[...] = b.astype(BF16)
        s = (a * _sigmoid(a) * b).astype(BF16)
        st_ref[...] = s
        s_prev[...] = s

        @pl.when(j == nj - 1)
        def _():
            xo = x_ref[...] + 0.5 * (acc[...] + _dot(s, wd_last_ref[...]))
            if head is None:
                xo_ref[...] = xo
            if next_gain is not None:
                tail_out[0][...] = (_rms(xo)[0] * extra[0][...]).astype(BF16)
            if head is not None:
                gain_ref, target_ref = extra[-2:]
                dg_ref, loss_ref = tail_out[-2:]
                xo_ref[...], part_g, part_loss = _loss_terms(xo, gain_ref[...], target_ref[...])

                @pl.when(i == 0)
                def _():
                    dg_ref[...] = part_g
                    loss_ref[...] = jnp.broadcast_to(part_loss, loss_ref.shape)

                @pl.when(i > 0)
                def _():
                    dg_ref[...] += part_g
                    loss_ref[...] += jnp.broadcast_to(part_loss, loss_ref.shape)

    row = pl.BlockSpec((tm, D_MODEL), lambda i, j: (i, 0))
    vec = pl.BlockSpec((1, D_MODEL), lambda i, j: (0, 0))
    tile = pl.BlockSpec((tm, tf), lambda i, j: (i, j))
    weights = pl.BlockSpec((tf, D_MODEL), lambda i, j: (j, 0))
    tail_specs = ([] if next_gain is None else [row]) + ([] if head is None else [vec, vec])
    tail_shapes = ([] if next_gain is None else [jax.ShapeDtypeStruct((t, D_MODEL), BF16)]) + (
        [] if head is None else [jax.ShapeDtypeStruct((1, D_MODEL), F32)] * 2)
    return pl.pallas_call(
        body,
        name=name,
        grid=(t // tm, nj),
        in_specs=[
            row, vec, weights, weights,
            pl.BlockSpec((tf, D_MODEL), lambda i, j: (jnp.maximum(j - 1, 0), 0)),
            pl.BlockSpec((tf, D_MODEL), lambda i, j: (nj - 1, 0)),
        ] + ([] if next_gain is None else [vec]) + ([] if head is None else [vec, row]),
        out_specs=[row, tile, tile, row, tile] + tail_specs,
        out_shape=[
            jax.ShapeDtypeStruct((t, D_MODEL), F32),
            jax.ShapeDtypeStruct((t, D_FF), BF16),
            jax.ShapeDtypeStruct((t, D_FF), BF16),
            jax.ShapeDtypeStruct((t, D_MODEL), BF16),
            jax.ShapeDtypeStruct((t, D_FF), BF16),
        ] + tail_shapes,
        scratch_shapes=[pltpu.VMEM((tm, D_MODEL), F32), pltpu.VMEM((tm, tf), BF16)],
        compiler_params=_params("arbitrary", "arbitrary"),
    )(x, gain, wgt, wut, wd, wd, *extra_in)


def _ffn_bwd(dout, x, gain, a, b, wgt, wut, wd, *, name, tm=1024, tf=256):
    t = x.shape[0]
    nj = D_FF // tf

    def body(do_ref, x_ref, g_ref, a_ref, b_ref, wg_prev_ref, wu_prev_ref, wg_last_ref, wu_last_ref, wd_ref,
             dx_ref, dg_ref, da_ref, db_ref, dob_ref, dob_scr, dh, da_prev, db_prev):
        i = pl.program_id(0)
        j = pl.program_id(1)

        @pl.when(j == 0)
        def _():
            d = (0.5 * do_ref[...]).astype(BF16)
            dob_scr[...] = d
            dob_ref[...] = d
            dh[...] = jnp.zeros_like(dh)
            da_prev[...] = jnp.zeros_like(da_prev)
            db_prev[...] = jnp.zeros_like(db_prev)

        dh[...] += _dot(da_prev[...], wg_prev_ref[...]) + _dot(db_prev[...], wu_prev_ref[...])
        ds = _dot_nt(dob_scr[...], wd_ref[...])
        av = a_ref[...].astype(F32)
        bv = b_ref[...].astype(F32)
        sig = _sigmoid(av)
        dbv = (ds * (av * sig)).astype(BF16)
        dav = (ds * bv * (sig * (1.0 + av * (1.0 - sig)))).astype(BF16)
        da_ref[...] = dav
        db_ref[...] = dbv
        da_prev[...] = dav
        db_prev[...] = dbv

        @pl.when(j == nj - 1)
        def _():
            xhat, rstd = _rms(x_ref[...])
            dhv = dh[...] + _dot(dav, wg_last_ref[...]) + _dot(dbv, wu_last_ref[...])
            part = jnp.sum(dhv * xhat, axis=0, keepdims=True)

            @pl.when(i == 0)
            def _():
                dg_ref[...] = part

            @pl.when(i > 0)
            def _():
                dg_ref[...] += part

            dxh = dhv * g_ref[...]
            dx_ref[...] = do_ref[...] + rstd * (dxh - xhat * jnp.mean(dxh * xhat, axis=-1, keepdims=True))

    return pl.pallas_call(
        body,
        name=name,
        grid=(t // tm, nj),
        in_specs=[
            pl.BlockSpec((tm, D_MODEL), lambda i, j: (i, 0)),
            pl.BlockSpec((tm, D_MODEL), lambda i, j: (i, 0)),
            pl.BlockSpec((1, D_MODEL), lambda i, j: (0, 0)),
            pl.BlockSpec((tm, tf), lambda i, j: (i, j)),
            pl.BlockSpec((tm, tf), lambda i, j: (i, j)),
            pl.BlockSpec((tf, D_MODEL), lambda i, j: (jnp.maximum(j - 1, 0), 0)),
            pl.BlockSpec((tf, D_MODEL), lambda i, j: (jnp.maximum(j - 1, 0), 0)),
            pl.BlockSpec((tf, D_MODEL), lambda i, j: (nj - 1, 0)),
            pl.BlockSpec((tf, D_MODEL), lambda i, j: (nj - 1, 0)),
            pl.BlockSpec((tf, D_MODEL), lambda i, j: (j, 0)),
        ],
        out_specs=[
            pl.BlockSpec((tm, D_MODEL), lambda i, j: (i, 0)),
            pl.BlockSpec((1, D_MODEL), lambda i, j: (0, 0)),
            pl.BlockSpec((tm, tf), lambda i, j: (i, j)),
            pl.BlockSpec((tm, tf), lambda i, j: (i, j)),
            pl.BlockSpec((tm, D_MODEL), lambda i, j: (i, 0)),
        ],
        out_shape=[
            jax.ShapeDtypeStruct((t, D_MODEL), F32),
            jax.ShapeDtypeStruct((1, D_MODEL), F32),
            jax.ShapeDtypeStruct((t, D_FF), BF16),
            jax.ShapeDtypeStruct((t, D_FF), BF16),
            jax.ShapeDtypeStruct((t, D_MODEL), BF16),
        ],
        scratch_shapes=[pltpu.VMEM((tm, D_MODEL), BF16), pltpu.VMEM((tm, D_MODEL), F32), pltpu.VMEM((tm, tf), BF16),
                        pltpu.VMEM((tm, tf), BF16)],
        compiler_params=_params("arbitrary", "arbitrary", vmem_limit_bytes=FFN_BWD_VMEM_LIMIT_BYTES),
    )(dout, x, gain, a, b, wgt, wut, wgt, wut, wd)


def _in_proj_bwd(dproj, w_int, x, gain, dres, *, name, tm=512):
    t, k = dproj.shape

    def body(dp_ref, w_ref, x_ref, g_ref, dr_ref, dx_ref, dg_ref, dxb_ref):
        i = pl.program_id(0)
        dhv = _dot(dp_ref[...], w_ref[...])
        xhat, rstd = _rms(x_ref[...])
        part = jnp.sum(dhv * xhat, axis=0, keepdims=True)

        @pl.when(i == 0)
        def _():
            dg_ref[...] = part

        @pl.when(i > 0)
        def _():
            dg_ref[...] += part

        dxh = dhv * g_ref[...]
        dx = dr_ref[...] + rstd * (dxh - xhat * jnp.mean(dxh * xhat, axis=-1, keepdims=True))
        dx_ref[...] = dx
        dxb_ref[...] = (0.5 * dx).astype(BF16)

    row = pl.BlockSpec((tm, D_MODEL), lambda i: (i, 0))
    vec = pl.BlockSpec((1, D_MODEL), lambda i: (0, 0))
    return pl.pallas_call(
        body,
        name=name,
        grid=(t // tm,),
        in_specs=[pl.BlockSpec((tm, k), lambda i: (i, 0)), pl.BlockSpec((k, D_MODEL), lambda i: (0, 0)), row, vec, row],
        out_specs=[row, vec, row],
        out_shape=[jax.ShapeDtypeStruct((t, D_MODEL), F32), jax.ShapeDtypeStruct((1, D_MODEL), F32),
                   jax.ShapeDtypeStruct((t, D_MODEL), BF16)],
        compiler_params=_params("arbitrary"),
    )(dproj, w_int, x, gain, dres)


ATT_Q_TILE = 512
ATT_K_BLOCK = 256


def _first_head_lanes():
    return lax.broadcasted_iota(jnp.int32, (1, LANES), 1) < SB_HEAD_DIM


def _stack_heads(x):
    first = _first_head_lanes()
    return jnp.concatenate([jnp.where(first, x, 0.0), jnp.where(first, 0.0, x)], axis=0)


def _unstack_heads(x, rows):
    return jnp.where(_first_head_lanes(), x[:rows], x[rows:])


def _rows_from(x, first, rows):
    return x if first == 0 else jnp.concatenate([x[first:rows], x[rows + first:]], axis=0)


def _rows_into(full, part, first, rows):
    if first == 0:
        return part
    n = rows - first
    return jnp.concatenate([full[:first], part[:n], full[rows:rows + first], part[n:]], axis=0)


def _tri(n, relation):
    r = lax.broadcasted_iota(jnp.int32, (n, n), 0)
    c = lax.broadcasted_iota(jnp.int32, (n, n), 1)
    return relation(r, c).astype(BF16)


def _scan_dot(x, tri):
    hi = x.astype(BF16)
    lo = (x - hi.astype(F32)).astype(BF16)
    return _dot(jnp.concatenate([hi, lo], axis=1), jnp.concatenate([tri, tri], axis=0))


def _log_terms(z):
    lbeta = jnp.minimum(z, 0.0) - jnp.log(1.0 + jnp.exp(-jnp.abs(z)))
    return lbeta, lbeta - z


def _attn_fwd(proj, *, name):
    t = proj.shape[0]
    tq, tk = ATT_Q_TILE, ATT_K_BLOCK
    diag = tq // tk
    n_pairs = SB_WIDTH // LANES

    def body(q_ref, k_ref, v_ref, o_ref, kept_ref):
        qi = pl.program_id(1)
        q = q_ref[...] * (SB_HEAD_DIM ** -0.5)
        qs = _stack_heads(q).astype(BF16)
        tri = _tri(tk, lambda j, s: j > s)
        trow = lax.broadcasted_iota(jnp.int32, (tq, tk), 0)
        scol = lax.broadcasted_iota(jnp.int32, (tq, tk), 1)

        def block(kb, carry, causal, first=0):
            acc, c = carry
            off = pl.multiple_of(kb * tk, tk)
            lbeta, lrest = _log_terms(_dot_nt(_rows_from(qs, first, tq), k_ref[pl.ds(off, tk), :].astype(BF16)))
            if causal is not None:
                lrest = jnp.where(causal, lrest, 0.0)
            w = jnp.exp(lbeta + (_scan_dot(lrest, tri) + _rows_from(c, first, tq)))
            if causal is not None:
                w = jnp.where(causal, w, 0.0)
            wb = w.astype(BF16)
            kept_ref[0, 0, kb] = _rows_into(jnp.zeros((2 * tq, tk), BF16), wb, first, tq)
            acc = _rows_into(acc, _rows_from(acc, first, tq) + _dot(wb, v_ref[pl.ds(off, tk), :].astype(BF16)), first, tq)
            return acc, _rows_into(c, _rows_from(c, first, tq) + jnp.sum(lrest, axis=1, keepdims=True), first, tq)

        carry = (jnp.zeros((2 * tq, LANES), F32), jnp.zeros((2 * tq, 1), F32))
        n_full = qi * diag
        for j in reversed(range(diag)):
            mask = ((scol + j * tk) < trow)[j * tk:]
            carry = block(n_full + j, carry, jnp.concatenate([mask, mask], axis=0), first=j * tk)

        def odd_tile(carry):
            for j in range(diag):
                carry = block(n_full - 1 - j, carry, None)
            return carry

        carry = lax.cond(qi % 2 == 1, odd_tile, lambda c: c, carry)
        last = n_full - 1 - (qi % 2) * diag

        def step(it, carry):
            for j in range(2 * diag):
                carry = block(last - (2 * diag * it + j), carry, None)
            return carry

        acc, _ = lax.fori_loop(0, qi // 2, step, carry)
        o_ref[...] = _unstack_heads(acc, tq)

    return pl.pallas_call(
        body,
        name=name,
        grid=(n_pairs, t // tq),
        in_specs=[
            pl.BlockSpec((tq, LANES), lambda p, i: (i, p)),
            pl.BlockSpec((t, LANES), lambda p, i: (0, n_pairs + p)),
            pl.BlockSpec((t, LANES), lambda p, i: (0, 2 * n_pairs + p)),
        ],
        out_specs=[pl.BlockSpec((tq, LANES), lambda p, i: (i, p)),
                   pl.BlockSpec((1, 1, t // tk, 2 * tq, tk), lambda p, i: (p, i, 0, 0, 0))],
        out_shape=[jax.ShapeDtypeStruct((t, SB_WIDTH), F32),
                   jax.ShapeDtypeStruct((n_pairs, t // tq, t // tk, 2 * tq, tk), BF16)],
        compiler_params=_params("parallel", "parallel"),
    )(proj, proj, proj)


def _attn_bwd(proj, kept, do, *, name, tie=None):
    t = proj.shape[0]
    tq, tk = ATT_Q_TILE, ATT_K_BLOCK
    diag = tq // tk
    n_pairs = SB_WIDTH // LANES
    scale = SB_HEAD_DIM ** -0.5

    def body(q_ref, k_ref, v_ref, kept_ref, do_ref, *rest):
        dq_ref, dk_ref, dv_ref = rest[-3:]
        qi = pl.program_id(1)

        @pl.when(qi == 0)
        def _():
            dk_ref[...] = jnp.zeros_like(dk_ref)
            dv_ref[...] = jnp.zeros_like(dv_ref)

        qs = _stack_heads(q_ref[...] * scale).astype(BF16)
        dos = _stack_heads(do_ref[...]).astype(BF16)
        before = _tri(tk, lambda s, j: s < j)
        trow = lax.broadcasted_iota(jnp.int32, (tq, tk), 0)
        scol = lax.broadcasted_iota(jnp.int32, (tq, tk), 1)

        def block(kb, carry, causal, first=0):
            dq, cg = carry
            off = pl.multiple_of(kb * tk, tk)
            q_rows, do_rows = _rows_from(qs, first, tq), _rows_from(dos, first, tq)
            wb = _rows_from(kept_ref[0, 0, kb], first, tq)
            kblk = k_ref[pl.ds(off, tk), :].astype(BF16)
            sig = _sigmoid(_dot_nt(q_rows, kblk))
            g = wb.astype(F32) * _dot_nt(do_rows, v_ref[pl.ds(off, tk), :].astype(BF16))
            prior = _scan_dot(g, before) + _rows_from(cg, first, tq)
            dz = g - sig * (g + prior)
            if causal is not None:
                dz = jnp.where(causal, dz, 0.0)
            dzb = dz.astype(BF16)
            dq = _rows_into(dq, _rows_from(dq, first, tq) + _dot(dzb, kblk), first, tq)
            dk_ref[pl.ds(off, tk), :] += _dot_tn(dzb, q_rows)
            dv_ref[pl.ds(off, tk), :] += _dot_tn(wb, do_rows)
            return dq, _rows_into(cg, _rows_from(cg, first, tq) + jnp.sum(g, axis=1, keepdims=True), first, tq)

        n_full = qi * diag

        def step(it, carry):
            for j in range(2 * diag):
                carry = block(2 * diag * it + j, carry, None)
            return carry

        def odd_tile(carry):
            for j in range(diag):
                carry = block(n_full - diag + j, carry, None)
            return carry

        carry = lax.fori_loop(0, qi // 2, step, (jnp.zeros((2 * tq, LANES), F32), jnp.zeros((2 * tq, 1), F32)))
        carry = lax.cond(qi % 2 == 1, odd_tile, lambda c: c, carry)
        for j in range(diag):
            mask = ((scol + j * tk) < trow)[j * tk:]
            carry = block(n_full + j, carry, jnp.concatenate([mask, mask], axis=0), first=j * tk)
        dq_ref[...] = (_unstack_heads(carry[0], tq) * scale).astype(BF16)

    tile_spec = pl.BlockSpec((tq, LANES), lambda p, i: (i, p))
    full_spec = pl.BlockSpec((t, LANES), lambda p, i: (0, p))
    return pl.pallas_call(
        body,
        name=name,
        grid=(n_pairs, t // tq),
        in_specs=[
            tile_spec,
            pl.BlockSpec((t, LANES), lambda p, i: (0, n_pairs + p)),
            pl.BlockSpec((t, LANES), lambda p, i: (0, 2 * n_pairs + p)),
            pl.BlockSpec((1, 1, t // tk, 2 * tq, tk), lambda p, i: (p, i, 0, 0, 0)),
            tile_spec,
        ] + ([] if tie is None else [pl.BlockSpec(memory_space=pl.ANY)]),
        out_specs=[tile_spec, full_spec, full_spec],
        out_shape=[jax.ShapeDtypeStruct((t, SB_WIDTH), BF16)] + [jax.ShapeDtypeStruct((t, SB_WIDTH), F32)] * 2,
        compiler_params=_params("arbitrary", "arbitrary"),
    )(proj, proj, proj, kept, do, *([] if tie is None else [tie]))


HG_BLOCK = 128
HG_HEADS = HG_WIDTH // HG_HEAD_DIM


def _chunk_mats(n):
    r = lax.broadcasted_iota(jnp.int32, (n, n), 0)
    c = lax.broadcasted_iota(jnp.int32, (n, n), 1)
    same = (r // HG_CHUNK) == (c // HG_CHUNK)
    upto = (same & (c <= r)).astype(BF16)
    whole = same.astype(BF16)
    onward = (same & (c >= r)).astype(BF16)
    return upto, whole, onward


def _rows_dot(mat, x):
    return _dot(jnp.concatenate([mat, mat, mat], axis=1), jnp.concatenate(_split3(x), axis=0))


def _split_heads(x):
    return jnp.stack([x[:, h * HG_HEAD_DIM:(h + 1) * HG_HEAD_DIM] for h in range(HG_HEADS)], axis=0)


def _merge_heads(x):
    return jnp.concatenate([x[h] for h in range(HG_HEADS)], axis=1)


def _lower_bound(lg_ref):
    lg = lg_ref[...]
    return _sigmoid(lg[0:1, :] - lg[1:2, :])


def _hgrn_prepare(q_ref, f_ref, lb, h, upto, whole):
    cols = slice(h * HG_HEAD_DIM, (h + 1) * HG_HEAD_DIM)
    lbh = lb[:, cols]
    sg = _sigmoid(f_ref[:, cols])
    forget = lbh + (1.0 - lbh) * sg
    logf = jnp.log(forget)
    kk = (1.0 - lbh) * (1.0 - sg)
    qv = q_ref[:, cols]
    qsig = _sigmoid(qv)
    qh = qv * qsig
    b = _rows_dot(upto, logf)
    blast = _rows_dot(whole, logf)
    return dict(lbh=lbh, sg=sg, forget=forget, kk=kk, qv=qv, qsig=qsig, qh=qh, b=b, eb=jnp.exp(b),
                ekb=jnp.exp(blast - b), dl=jnp.exp(blast))


def _hgrn_fwd(proj, logits, *, name):
    t = proj.shape[0]
    tb = HG_BLOCK
    nc = tb // HG_CHUNK
    hd = HG_HEAD_DIM

    def body(q_ref, f_ref, i_ref, lg_ref, o_ref, st_ref, state, qh_s, kk_s, b_s, qe_s, ke_s, dl_s):
        @pl.when(pl.program_id(0) == 0)
        def _():
            state[...] = jnp.zeros_like(state)

        lb = _lower_bound(lg_ref)
        upto, whole, _ = _chunk_mats(tb)
        for h in range(HG_HEADS):
            p = _hgrn_prepare(q_ref, f_ref, lb, h, upto, whole)
            qh_s[h] = p["qh"]
            kk_s[h] = p["kk"]
            b_s[h] = p["b"]
            qe_s[h] = (p["qh"] * p["eb"]).astype(BF16)
            ke_s[h] = (p["kk"] * p["ekb"]).astype(BF16)
            dl_s[h] = p["dl"]
        rowi = lax.broadcasted_iota(jnp.int32, (HG_HEADS, HG_CHUNK, hd), 1)

        def chunk(c, _):
            r0 = pl.multiple_of(c * HG_CHUNK, HG_CHUNK)
            rows = pl.ds(r0, HG_CHUNK)
            bc = b_s[:, rows, :]
            qc = qh_s[:, rows, :]
            kc = kk_s[:, rows, :]
            vc = _split_heads(i_ref[rows, :])
            s_in = state[...]
            st_ref[c] = s_in
            s_in_b = s_in.astype(BF16)
            qe = qe_s[:, rows, :]
            o = jnp.stack([_dot_nt(qe[h], s_in_b[h]) for h in range(HG_HEADS)], axis=0)
            for s in range(HG_CHUNK):
                pair = jnp.where(rowi >= s, qc * jnp.exp(bc - bc[:, s:s + 1, :]) * kc[:, s:s + 1, :], 0.0)
                o = o + jnp.sum(pair, axis=2, keepdims=True) * vc[:, s:s + 1, :]
            o_ref[rows, :] = _merge_heads(o)
            vcb = vc.astype(BF16)
            ke = ke_s[:, rows, :]
            update = jnp.stack([_dot_tn(vcb[h], ke[h]) for h in range(HG_HEADS)], axis=0)
            state[...] = s_in * dl_s[:, pl.ds(r0, 1), :] + update
            return 0

        lax.fori_loop(0, nc, chunk, 0, unroll=4)

    blk =lambda col: pl.BlockSpec((tb, HG_WIDTH), lambda i: (i, col))
    head_f32 = pltpu.VMEM((HG_HEADS, tb, hd), F32)
    head_bf16 = pltpu.VMEM((HG_HEADS, tb, hd), BF16)
    return pl.pallas_call(
        body,
        name=name,
        grid=(t // tb,),
        in_specs=[blk(3), blk(4), blk(5), pl.BlockSpec((2, HG_WIDTH), lambda i: (0, 0))],
        out_specs=[
            pl.BlockSpec((tb, HG_WIDTH), lambda i: (i, 0)),
            pl.BlockSpec((nc, HG_HEADS, hd, hd), lambda i: (i, 0, 0, 0)),
        ],
        out_shape=[
            jax.ShapeDtypeStruct((t, HG_WIDTH), F32),
            jax.ShapeDtypeStruct((t // HG_CHUNK, HG_HEADS, hd, hd), F32),
        ],
        scratch_shapes=[pltpu.VMEM((HG_HEADS, hd, hd), F32), head_f32, head_f32, head_f32, head_bf16, head_bf16,
                        head_f32],
        compiler_params=_params("arbitrary"),
    )(proj, proj, proj, logits)


def _hgrn_bwd(proj, logits, states, do, *, name):
    t = proj.shape[0]
    tb = HG_BLOCK
    nb = t // tb
    nc = tb // HG_CHUNK
    hd = HG_HEAD_DIM

    def body(q_ref, f_ref, i_ref, lg_ref, st_ref, do_ref, dq_ref, df_ref, di_ref, dlb_ref,
             dstate, qh_s, kk_s, b_s, eb_s, ekb_s, qe_s, ke_s, dl_s, dqh_s, dkk_s, dlf_s):
        step = pl.program_id(0)

        @pl.when(step == 0)
        def _():
            dstate[...] = jnp.zeros_like(dstate)
            dlb_ref[...] = jnp.zeros_like(dlb_ref)

        lb = _lower_bound(lg_ref)
        upto, whole, _ = _chunk_mats(tb)
        prepared = []
        for h in range(HG_HEADS):
            p = _hgrn_prepare(q_ref, f_ref, lb, h, upto, whole)
            prepared.append(p)
            qh_s[h] = p["qh"]
            kk_s[h] = p["kk"]
            b_s[h] = p["b"]
            eb_s[h] = p["eb"]
            ekb_s[h] = p["ekb"]
            qe_s[h] = (p["qh"] * p["eb"]).astype(BF16)
            ke_s[h] = (p["kk"] * p["ekb"]).astype(BF16)
            dl_s[h] = p["dl"]
        rowi = lax.broadcasted_iota(jnp.int32, (HG_CHUNK, hd), 0)
        r16 = lax.broadcasted_iota(jnp.int32, (HG_CHUNK, HG_CHUNK), 0)
        c16 = lax.broadcasted_iota(jnp.int32, (HG_CHUNK, HG_CHUNK), 1)
        onward = (c16 >= r16).astype(BF16)

        def chunk(it, _):
            c = nc - 1 - it
            r0 = pl.multiple_of(c * HG_CHUNK, HG_CHUNK)
            rows = pl.ds(r0, HG_CHUNK)
            for h in range(HG_HEADS):
                cols = slice(h * hd, (h + 1) * hd)
                bc = b_s[h, rows, :]
                qc = qh_s[h, rows, :]
                kc = kk_s[h, rows, :]
                vc = i_ref[rows, cols]
                doc = do_ref[rows, cols]
                s_in = st_ref[c, h]
                ds_out = dstate[h]
                ds_out_b = ds_out.astype(BF16)
                docb = doc.astype(BF16)
                dl_row = dl_s[h, pl.ds(r0, 1), :]
                dqh = _dot(docb, s_in.astype(BF16)) * eb_s[h, rows, :]
                dkk = _dot(vc.astype(BF16), ds_out_b) * ekb_s[h, rows, :]
                dv = _dot_nt(ke_s[h, rows, :], ds_out_b)
                db = dqh * qc - dkk * kc
                dwhole = jnp.sum(dkk * kc, axis=0, keepdims=True) + jnp.sum(ds_out * s_in, axis=0, keepdims=True) * dl_row
                dk_rows, dv_rows = [], []
                for s in range(HG_CHUNK):
                    keep = rowi >= s
                    e = jnp.exp(bc - bc[s:s + 1, :])
                    k_row = kc[s:s + 1, :]
                    pcol = jnp.sum(jnp.where(keep, qc * e * k_row, 0.0), axis=1, keepdims=True)
                    dpcol = jnp.sum(doc * vc[s:s + 1, :], axis=1, keepdims=True)
                    m = jnp.where(keep, e * dpcol, 0.0)
                    y = m * qc
                    dqh = dqh + m * k_row
                    db = db + y * k_row
                    dk_rows.append(jnp.sum(y, axis=0, keepdims=True))
                    dv_rows.append(jnp.sum(pcol * doc, axis=0, keepdims=True))
                dkk_pairs = jnp.concatenate(dk_rows, axis=0)
                dkk = dkk + dkk_pairs
                db = db - dkk_pairs * kc
                dv = dv + jnp.concatenate(dv_rows, axis=0)
                dqh_s[h, rows, :] = dqh
                dkk_s[h, rows, :] = dkk
                dlf_s[h, rows, :] = _rows_dot(onward, db) + dwhole
                di_ref[rows, cols] = dv.astype(BF16)
                dstate[h] = ds_out * dl_row + _dot_tn(docb, qe_s[h, rows, :])
            return 0

        lax.fori_loop(0, nc, chunk, 0, unroll=4)
        for h in range(HG_HEADS):
            cols = slice(h * hd, (h + 1) * hd)
            p = prepared[h]
            dq_ref[:, cols] = (dqh_s[h] * (p["qsig"] * (1.0 + p["qv"] * (1.0 - p["qsig"])))).astype(BF16)
            dforget = dlf_s[h] / p["forget"] - dkk_s[h]
            df_ref[:, cols] = (dforget * (1.0 - p["lbh"]) * p["sg"] * (1.0 - p["sg"])).astype(BF16)
            dlb_ref[:, cols] += jnp.sum(dforget * (1.0 - p["sg"]), axis=0, keepdims=True)

    blk = lambda col: pl.BlockSpec((tb, HG_WIDTH), lambda i: (nb - 1 - i, col))
    vec = pl.BlockSpec((1, HG_WIDTH), lambda i: (0, 0))
    head_f32 = pltpu.VMEM((HG_HEADS, tb, hd), F32)
    head_bf16 = pltpu.VMEM((HG_HEADS, tb, hd), BF16)
    return pl.pallas_call(
        body,
        name=name,
        grid=(nb,),
        in_specs=[
            blk(3), blk(4), blk(5),
            pl.BlockSpec((2, HG_WIDTH), lambda i: (0, 0)),
            pl.BlockSpec((nc, HG_HEADS, hd, hd), lambda i: (nb - 1 - i, 0, 0, 0)),
            blk(0),
        ],
        out_specs=[blk(0), blk(0), blk(0), vec],
        out_shape=[jax.ShapeDtypeStruct((t, HG_WIDTH), BF16)] * 3 + [jax.ShapeDtypeStruct((1, HG_WIDTH), F32)],
        scratch_shapes=[
            pltpu.VMEM((HG_HEADS, hd, hd), F32),
            head_f32, head_f32, head_f32, head_f32, head_f32, head_bf16, head_bf16, head_f32,
            head_f32, head_f32, head_f32,
        ],
        compiler_params=_params("arbitrary"),
    )(proj, proj, proj, logits, states, do)


def _group_mat(width, head_dim):
    r = lax.broadcasted_iota(jnp.int32, (width, width), 0)
    c = lax.broadcasted_iota(jnp.int32, (width, width), 1)
    return ((r // head_dim) == (c // head_dim)).astype(BF16)


def _head_mean(x, mat, head_dim):
    hi = x.astype(BF16)
    lo = (x - hi.astype(F32)).astype(BF16)
    return (_dot(hi, mat) + _dot(lo, mat)) * (1.0 / head_dim)


def _mix_out_fwd(o_sb, o_hg, proj, g_sb, g_hg, w_out, x1, *, name, tm=512):
    t = x1.shape[0]

    def body(osb_ref, ohg_ref, gate_ref, gsb_ref, ghg_ref, w_ref, x_ref, xo_ref, mt_ref):
        msb = _group_mat(SB_WIDTH, SB_HEAD_DIM)
        mhg = _group_mat(HG_WIDTH, HG_HEAD_DIM)
        osb = osb_ref[...]
        ohg = ohg_ref[...]
        nsb = osb * lax.rsqrt(_head_mean(osb * osb, msb, SB_HEAD_DIM) + EPS) * gsb_ref[...]
        gate = gate_ref[...]
        nhg = ohg * lax.rsqrt(_head_mean(ohg * ohg, mhg, HG_HEAD_DIM) + EPS) * ghg_ref[...] * (gate * _sigmoid(gate))
        mixed = jnp.concatenate([nsb, nhg], axis=1).astype(BF16)
        mt_ref[...] = mixed
        xo_ref[...] = x_ref[...] + _dot(mixed, w_ref[...])

    half = pl.BlockSpec((tm, SB_WIDTH), lambda i: (i, 0))
    vec = pl.BlockSpec((1, SB_WIDTH), lambda i: (0, 0))
    row = pl.BlockSpec((tm, D_MODEL), lambda i: (i, 0))
    return pl.pallas_call(
        body,
        name=name,
        grid=(t // tm,),
        in_specs=[half, half, pl.BlockSpec((tm, HG_WIDTH), lambda i: (i, 6)), vec, vec,
                  pl.BlockSpec((D_MODEL, D_MODEL), lambda i: (0, 0)), row],
        out_specs=[row, row],
        out_shape=[jax.ShapeDtypeStruct((t, D_MODEL), F32), jax.ShapeDtypeStruct((t, D_MODEL), BF16)],
        compiler_params=_params("parallel"),
    )(o_sb, o_hg, proj, g_sb, g_hg, w_out, x1)


def _mix_out_bwd(dx2, o_sb, o_hg, proj, g_sb, g_hg, w_out, *, name, tm=512):
    t = dx2.shape[0]

    def body(dx_ref, osb_ref, ohg_ref, gate_ref, gsb_ref, ghg_ref, w_ref, dosb_ref, dohg_ref, dgate_ref, dgsb_ref,
             dghg_ref, dxb_ref):
        i = pl.program_id(0)
        msb = _group_mat(SB_WIDTH, SB_HEAD_DIM)
        mhg = _group_mat(HG_WIDTH, HG_HEAD_DIM)
        dxb = dx_ref[...].astype(BF16)
        dxb_ref[...] = dxb
        dmixed = _dot_nt(dxb, w_ref[...])
        dnsb = dmixed[:, :SB_WIDTH]
        dy = dmixed[:, SB_WIDTH:]

        osb = osb_ref[...]
        rstd = lax.rsqrt(_head_mean(osb * osb, msb, SB_HEAD_DIM) + EPS)
        ohat = osb * rstd
        part_sb = jnp.sum(dnsb * ohat, axis=0, keepdims=True)
        dohat = dnsb * gsb_ref[...]
        dosb_ref[...] = rstd * (dohat - ohat * _head_mean(dohat * ohat, msb, SB_HEAD_DIM))

        ohg = ohg_ref[...]
        rstd = lax.rsqrt(_head_mean(ohg * ohg, mhg, HG_HEAD_DIM) + EPS)
        ohat = ohg * rstd
        gate = gate_ref[...]
        sig = _sigmoid(gate)
        dn = dy * (gate * sig)
        dgate_ref[...] = (dy * (ohat * ghg_ref[...]) * (sig * (1.0 + gate * (1.0 - sig)))).astype(BF16)
        part_hg = jnp.sum(dn * ohat, axis=0, keepdims=True)
        dohat = dn * ghg_ref[...]
        dohg_ref[...] = rstd * (dohat - ohat * _head_mean(dohat * ohat, mhg, HG_HEAD_DIM))

        @pl.when(i == 0)
        def _():
            dgsb_ref[...] = part_sb
            dghg_ref[...] = part_hg

        @pl.when(i > 0)
        def _():
            dgsb_ref[...] += part_sb
            dghg_ref[...] += part_hg

    half = pl.BlockSpec((tm, SB_WIDTH), lambda i: (i, 0))
    vec = pl.BlockSpec((1, SB_WIDTH), lambda i: (0, 0))
    row = pl.BlockSpec((tm, D_MODEL), lambda i: (i, 0))
    return pl.pallas_call(
        body,
        name=name,
        grid=(t // tm,),
        in_specs=[row, half, half, pl.BlockSpec((tm, HG_WIDTH), lambda i: (i, 6)), vec, vec,
                  pl.BlockSpec((D_MODEL, D_MODEL), lambda i: (0, 0))],
        out_specs=[half, half, half, vec, vec, row],
        out_shape=[jax.ShapeDtypeStruct((t, SB_WIDTH), F32)] * 2 + [jax.ShapeDtypeStruct((t, SB_WIDTH), BF16)]
        + [jax.ShapeDtypeStruct((1, SB_WIDTH), F32)] * 2 + [jax.ShapeDtypeStruct((t, D_MODEL), BF16)],
        compiler_params=_params("arbitrary"),
    )(dx2, o_sb, o_hg, proj, g_sb, g_hg, w_out)


def _local_step(x, target, norms, logits, w, weights_after=None, grads_ready=None):
    w = dict(w)
    x1, a1, b1, h1, s1, hm = _ffn_fwd(x, norms["ffn1"], w["g1t"], w["u1t"], w["d1"], name="ffn1_fwd",
                                      next_gain=norms["mix"])
    if weights_after is not None:
        w.update(weights_after("ffn1", x1))
    proj = _mm(hm, w["int"], name="in_proj", tm=512, tn=IN_COLS, nt=True)
    o_sb, sb_kept = _attn_fwd(proj, name="sb_attn_fwd")
    o_hg, states = _hgrn_fwd(proj, logits, name="hgrn2_fwd")
    x2, mixed = _mix_out_fwd(o_sb, o_hg, proj, norms["sb"], norms["hg"], w["out"], x1, name="mix_out_fwd")
    if weights_after is not None:
        w.update(weights_after("mix", x2))
    dx3, a2, b2, h2, s2, d_final, loss_row = _ffn_fwd(x2, norms["ffn2"], w["g2t"], w["u2t"], w["d2"], name="ffn2_fwd",
                                                      head=(norms["final"], target))

    def weight_grad(lhs, rhs, name, tie=None):
        return _mm(lhs, rhs, name=name, tm=256, tn=D_MODEL, ta=True, out_dtype=BF16, tie=tie)

    def sent(stage):
        return grads_ready(stage, gw) if grads_ready is not None else None

    gw, gv = {}, {"final": d_final}
    dx2, gv["ffn2"], da2, db2, dob2 = _ffn_bwd(dx3, x2, norms["ffn2"], a2, b2, w["g2t"], w["u2t"], w["d2"],
                                               name="ffn2_bwd")
    gw["g2t"] = weight_grad(da2, h2, "ffn2_dgate")
    gw["u2t"] = weight_grad(db2, h2, "ffn2_dup")
    gw["d2"] = weight_grad(s2, dob2, "ffn2_ddown")

    do_sb, do_hg, d_gate, gv["sb"], gv["hg"], dx2b = _mix_out_bwd(
        dx2, o_sb, o_hg, proj, norms["sb"], norms["hg"], w["out"], name="mix_out_bwd")
    gw["out"] = weight_grad(mixed, dx2b, "out_dw")
    tie = sent("mix")
    dq_sb, dk_sb, dv_sb = _attn_bwd(proj, sb_kept, do_sb, name="sb_attn_bwd", tie=tie)
    dq_hg, df_hg, di_hg, d_lb = _hgrn_bwd(proj, logits if tie is None else logits + tie[0, 0], states, do_hg,
                                          name="hgrn2_bwd")
    dproj = jnp.concatenate([dq_sb, dk_sb.astype(BF16), dv_sb.astype(BF16), dq_hg, df_hg, di_hg, d_gate], axis=1)
    gw["int"] = weight_grad(dproj, hm, "in_dw")
    tie = sent("in")
    dx1, gv["mix"], dob1 = _in_proj_bwd(dproj, w["int"], x1, norms["mix"] if tie is None else norms["mix"] + tie[0, 0],
                                        dx2, name="in_dx")

    gw["d1"] = weight_grad(s1, dob1, "ffn1_ddown")
    tie = sent("d1")
    dx, gv["ffn1"], da1, db1, _ = _ffn_bwd(dx1, x, norms["ffn1"] if tie is None else norms["ffn1"] + tie[0, 0], a1, b1,
                                           w["g1t"], w["u1t"], w["d1"], name="ffn1_bwd")
    gw["g1t"] = weight_grad(da1, h1, "ffn1_dgate")
    gw["u1t"] = weight_grad(db1, h1, "ffn1_dup", tie=sent("g1t"))
    sent("u1t")
    gv["lb"] = d_lb
    return loss_row, dx, gw, gv


HBM = pl.BlockSpec(memory_space=pl.ANY)


def _place():
    return lax.axis_index("x"), lax.axis_index("y"), lax.axis_index("c")


def _slot(px, py, pc):
    return 4 * px + 2 * py + pc


GATHER_COPIES = 8


def _all_gather(blocks, *, name):
    n = len(blocks)

    def body(*refs):
        ins, outs = refs[:n], refs[n:2 * n]
        send_sems, recv_sems, local_sems = refs[2 * n:]
        x, y, c = _place()
        me, sibling = (x, y, c), (x, y, 1 - c)
        beside, across, diagonal = (1 - x, y, c), (x, 1 - y, c), (1 - x, 1 - y, c)

        def copy(a, k, block, to, src=None, half=None):
            dst = outs[a].at[_slot(*block)]
            if half is not None:
                rows = blocks[a].shape[0] // 2
                dst = dst.at[pl.ds(half * rows, rows)]
            return pltpu.make_async_remote_copy(
                src_ref=dst if src is None else src, dst_ref=dst, send_sem=send_sems.at[GATHER_COPIES * a + k],
                recv_sem=recv_sems.at[GATHER_COPIES * a + k], device_id=to, device_id_type=MESH)

        mine = [pltpu.make_async_copy(ins[a], outs[a].at[_slot(*me)], local_sems.at[a]) for a in range(n)]
        for cp in mine:
            cp.start()
        sent = []
        for a in range(n):
            sent += [copy(a, 0, me, sibling, src=ins[a]), copy(a, 1, me, beside, src=ins[a]),
                     copy(a, 2, me, across, src=ins[a])]
        for cp in sent:
            cp.start()
        for a in range(n):
            copy(a, 1, beside, me).wait_recv()
            sent += [copy(a, 3, beside, across, half=0), copy(a, 5, beside, sibling)]
            sent[-2].start()
            sent[-1].start()
        for a in range(n):
            copy(a, 2, across, me).wait_recv()
            sent += [copy(a, 4, across, beside, half=1), copy(a, 6, across, sibling)]
            sent[-2].start()
            sent[-1].start()
        for a in range(n):
            copy(a, 3, diagonal, me, half=0).wait_recv()
            copy(a, 4, diagonal, me, half=1).wait_recv()
            sent.append(copy(a, 7, diagonal, sibling))
            sent[-1].start()
        for a in range(n):
            for k, origin in ((0, sibling), (5, (1 - x, y, 1 - c)), (6, (x, 1 - y, 1 - c)), (7, (1 - x, 1 - y, 1 - c))):
                copy(a, k, origin, me).wait_recv()
        for cp in sent:
            cp.wait_send()
        for cp in mine:
            cp.wait()

    return pl.pallas_call(
        body,
        name=name,
        in_specs=[HBM] * n,
        out_specs=[HBM] * n,
        out_shape=[jax.ShapeDtypeStruct((N_DEV,) + b.shape, b.dtype) for b in blocks],
        scratch_shapes=[pltpu.SemaphoreType.DMA((GATHER_COPIES * n,)), pltpu.SemaphoreType.DMA((GATHER_COPIES * n,)),
                        pltpu.SemaphoreType.DMA((n,))],
    )(*blocks)


def _flipped(place, d):
    return tuple(1 - p if (d >> (2 - axis)) & 1 else p for axis, p in enumerate(place))


SEM = pl.BlockSpec(memory_space=pltpu.SEMAPHORE)
EFFECT = pltpu.SideEffectType.DATAFLOW_SIDE_EFFECTING


def _split_copies(me, srcs, lands, send_sems, recv_sems, by_owner):
    copies = []
    for d in range(1, N_DEV):
        peer = _flipped(me, d)
        for a, (src, land) in enumerate(zip(srcs, lands)):
            copies.append(pltpu.make_async_remote_copy(
                src_ref=src.at[_slot(*peer)] if by_owner else src, dst_ref=land.at[_slot(*me)],
                send_sem=send_sems.at[7 * a + d - 1], recv_sem=recv_sems.at[7 * a + d - 1], device_id=peer,
                device_id_type=MESH))
    own = [pltpu.make_async_copy(src.at[_slot(*me)] if by_owner else src, land.at[_slot(*me)],
                                 recv_sems.at[7 * len(srcs) + a]) for a, (src, land) in enumerate(zip(srcs, lands))]
    return copies, own


def _copies_start(srcs, *, name, by_owner, after=None):
    n = len(srcs)
    extra = [] if after is None else [after]
    land_shapes = [s.shape if by_owner else (N_DEV,) + s.shape for s in srcs]
    lands = [pltpu.with_memory_space_constraint(lax.empty(shape, s.dtype), pltpu.HBM) for shape, s in zip(land_shapes, srcs)]
    srcs = [pltpu.with_memory_space_constraint(s, pltpu.HBM) for s in srcs]

    def body(*refs):
        src_refs, land_refs = refs[:n], refs[n:2 * n]
        send_sems, recv_sems = refs[2 * n + len(extra)], refs[2 * n + len(extra) + 1]
        token = refs[-1]
        copies, own = _split_copies(_place(), src_refs, land_refs, send_sems, recv_sems, by_owner)
        for cp in copies + own:
            cp.start()
        token[...] = jnp.zeros_like(token)

    out = pl.pallas_call(
        body,
        name=name,
        in_specs=[HBM] * (2 * n + len(extra)),
        out_specs=[SEM, SEM] + [HBM] * (2 * n) + [pl.BlockSpec(memory_space=pltpu.VMEM)],
        out_shape=[pltpu.SemaphoreType.DMA((7 * n,)), pltpu.SemaphoreType.DMA((8 * n,))]
        + [pltpu.HBM(s.shape, s.dtype) for s in srcs] + [pltpu.HBM(shape, s.dtype) for shape, s in zip(land_shapes, srcs)]
        + [jax.ShapeDtypeStruct((8, LANES), F32)],
        input_output_aliases={i: 2 + i for i in range(2 * n)},
        compiler_params=pltpu.CompilerParams(has_side_effects=EFFECT),
    )(*srcs, *lands, *extra)
    return (out[0], out[1], out[2:2 + n], out[2 + n:2 + 2 * n]), out[-1]


def _copies_wait(started, after, *, name, by_owner):
    send_sems, recv_sems, srcs, lands = started
    n = len(srcs)

    def body(*refs):
        src_refs, land_refs = refs[:n], refs[n:2 * n]
        copies, own = _split_copies(_place(), src_refs, land_refs, refs[2 * n], refs[2 * n + 1], by_owner)
        for cp in copies:
            cp.wait_send()
            cp.wait_recv()
        for cp in own:
            cp.wait()

    out = pl.pallas_call(
        body,
        name=name,
        in_specs=[HBM] * (2 * n) + [SEM, SEM, HBM],
        out_specs=[HBM] * (2 * n),
        out_shape=[pltpu.HBM(s.shape, s.dtype) for s in srcs] + [pltpu.HBM(s.shape, s.dtype) for s in lands],
        input_output_aliases={i: i for i in range(2 * n)},
        compiler_params=pltpu.CompilerParams(has_side_effects=EFFECT),
    )(*srcs, *lands, send_sems, recv_sems, after)
    return out[:n], out[n:]


def _adamw(w, g, m, v):
    m = ADAM_B1 * m + (1.0 - ADAM_B1) * g
    v = ADAM_B2 * v + (1.0 - ADAM_B2) * (g * g)
    m_hat = m / (1.0 - ADAM_B1 ** ADAM_STEP)
    v_hat = v / (1.0 - ADAM_B2 ** ADAM_STEP)
    delta = -ADAM_LR * (m_hat / (jnp.sqrt(v_hat) + ADAM_EPS) + ADAM_WD * w)
    return delta, m, v


def _sum_and_update(parts, w, m, v, *, name, tie=None):
    _, rows, cols = w.shape
    tr = rows // 2

    def body(p_ref, w_ref, m_ref, v_ref, *rest):
        g_ref, d_ref, mo_ref, vo_ref = rest[-4:]
        g = p_ref[0].astype(F32)
        for s in range(1, N_DEV):
            g = g + p_ref[s].astype(F32)
        g_ref[0] = g
        d_ref[0], mo_ref[0], vo_ref[0] = _adamw(w_ref[0], g, m_ref[0], v_ref[0])

    flat = pl.BlockSpec((1, tr, cols), lambda i: (0, i, 0))
    return pl.pallas_call(
        body,
        name=name,
        grid=(rows // tr,),
        in_specs=[pl.BlockSpec((N_DEV, tr, cols), lambda i: (0, i, 0)), flat, flat, flat]
        + ([] if tie is None else [pl.BlockSpec(memory_space=pl.ANY)]),
        out_specs=[flat] * 4,
        out_shape=[jax.ShapeDtypeStruct((1, rows, cols), F32)] * 4,
        compiler_params=_params("parallel"),
    )(parts, w, m, v, *([] if tie is None else [tie]))


VEC_ROWS = 8
ROW_LOGITS, ROW_LOSS = 5, 7


def _vectors_update(part, w, m, v, *, name, tie):
    def body(p_ref, w_ref, m_ref, v_ref, tie_ref, g_ref, d_ref, mo_ref, vo_ref, loss_ref, all_ref, send_sems, recv_sems):
        me = _place()
        all_ref[_slot(*me)] = p_ref[...]
        copies = []
        for d in range(1, N_DEV):
            peer = _flipped(me, d)
            copies.append(pltpu.make_async_remote_copy(
                src_ref=p_ref, dst_ref=all_ref.at[_slot(*me)], send_sem=send_sems.at[d - 1], recv_sem=recv_sems.at[d - 1],
                device_id=peer, device_id_type=MESH))
        for cp in copies:
            cp.start()
        for cp in copies:
            cp.wait()
        total = all_ref[0]
        for s in range(1, N_DEV):
            total = total + all_ref[s]
        wv = w_ref[...]
        half = D_MODEL // 2
        lb = _sigmoid(wv[ROW_LOGITS:ROW_LOGITS + 1, :half] - wv[ROW_LOGITS:ROW_LOGITS + 1, half:])
        d_first = total[ROW_LOGITS:ROW_LOGITS + 1, :half] * lb * (1.0 - lb)
        d_logits = jnp.concatenate([d_first, -d_first], axis=1)
        rowi = lax.broadcasted_iota(jnp.int32, (VEC_ROWS, D_MODEL), 0)
        g = jnp.where(rowi == ROW_LOGITS, d_logits, jnp.where(rowi < ROW_LOGITS, total, 0.0))
        g_ref[...] = g
        d_ref[...], mo_ref[...], vo_ref[...] = _adamw(wv, g, m_ref[...], v_ref[...])
        loss_ref[...] = total[ROW_LOSS:ROW_LOSS + 1, :]

    vmem = pl.BlockSpec(memory_space=pltpu.VMEM)
    return pl.pallas_call(
        body,
        name=name,
        in_specs=[vmem] * 4 + [HBM],
        out_specs=[vmem] * 5,
        out_shape=[jax.ShapeDtypeStruct((VEC_ROWS, D_MODEL), F32)] * 4 + [jax.ShapeDtypeStruct((1, D_MODEL), F32)],
        scratch_shapes=[pltpu.VMEM((N_DEV, VEC_ROWS, D_MODEL), F32), pltpu.SemaphoreType.DMA((7,)),
                        pltpu.SemaphoreType.DMA((7,))],
    )(part, w, m, v, tie)


TRANSPOSED = ("g1t", "u1t", "g2t", "u2t", "int")


def _vector_rows(rows):
    rowi = lax.broadcasted_iota(jnp.int32, (VEC_ROWS, D_MODEL), 0)
    out = jnp.zeros((VEC_ROWS, D_MODEL), F32)
    for i, r in enumerate(rows):
        if r is not None:
            out = jnp.where(rowi == i, r, out)
    return out


def kernel(x, ffn1_norm, ffn1_w_gate, ffn1_w_up, ffn1_w_down, mix_norm, w_in, sb_out_norm, hg_lower_bound_logits, hg_out_norm, w_out, ffn2_norm, ffn2_w_gate, ffn2_w_up, ffn2_w_down, final_norm, loss_target, m_ffn1_norm, m_ffn1_w_gate, m_ffn1_w_up, m_ffn1_w_down, m_mix_norm, m_w_in, m_sb_out_norm, m_hg_lower_bound_logits, m_hg_out_norm, m_w_out, m_ffn2_norm, m_ffn2_w_gate, m_ffn2_w_up, m_ffn2_w_down, m_final_norm, v_ffn1_norm, v_ffn1_w_gate, v_ffn1_w_up, v_ffn1_w_down, v_mix_norm, v_w_in, v_sb_out_norm, v_hg_lower_bound_logits, v_hg_out_norm, v_w_out, v_ffn2_norm, v_ffn2_w_gate, v_ffn2_w_up, v_ffn2_w_down, v_final_norm):
    def matrices(g1, u1, d1, win, wout, g2, u2, d2):
        return {"g1t": g1, "u1t": u1, "d1": d1, "int": win, "out": wout, "g2t": g2, "u2t": u2, "d2": d2}

    def vectors(n1, nm, nsb, lg, nhg, n2, nf):
        return [n1, nm, n2, nf.reshape(1, D_MODEL), jnp.concatenate([nsb, nhg], axis=1), lg.reshape(1, D_MODEL), None, None]

    w_sh = matrices(ffn1_w_gate, ffn1_w_up, ffn1_w_down, w_in, w_out, ffn2_w_gate, ffn2_w_up, ffn2_w_down)
    m_sh = matrices(m_ffn1_w_gate, m_ffn1_w_up, m_ffn1_w_down, m_w_in, m_w_out, m_ffn2_w_gate, m_ffn2_w_up, m_ffn2_w_down)
    v_sh = matrices(v_ffn1_w_gate, v_ffn1_w_up, v_ffn1_w_down, v_w_in, v_w_out, v_ffn2_w_gate, v_ffn2_w_up, v_ffn2_w_down)
    keys = list(w_sh)

    def full(key, stack):
        return stack.reshape(-1, D_MODEL)

    def by_owner(key, grad):
        return grad.reshape(N_DEV, -1, D_MODEL)

    def view(key, a):
        return jnp.swapaxes(a, 1, 2) if key in TRANSPOSED else a

    blocks = {k: view(k, w_sh[k])[0].astype(BF16) for k in keys}
    first, mid, last = ("g1t", "u1t", "d1"), ("int", "out"), ("g2t", "u2t", "d2")
    w_first = {k: full(k, s) for k, s in zip(first, _all_gather([blocks[k] for k in first], name="gather_ffn1"))}
    flights = {}
    flights["ffn1"], token_mid = _copies_start([blocks[k] for k in mid], name="gather_mid_start", by_owner=False,
                                               after=w_first["d1"])
    flights["mix"], token_last = _copies_start([blocks[k] for k in last], name="gather_ffn2_start", by_owner=False,
                                               after=token_mid)

    def weights_after(stage, result):
        group = mid if stage == "ffn1" else last
        _, lands = _copies_wait(flights[stage], result, name="gather_" + stage + "_wait", by_owner=False)
        return {k: full(k, s) for k, s in zip(group, lands)}

    groups = {"mix": ("g2t", "u2t", "d2", "out"), "in": ("int",), "g1t": ("g1t",), "u1t": ("u1t",), "d1": ("d1",)}
    sent, sent_tokens = {}, []

    def grads_ready(stage, gw):
        stacks = [by_owner(k, gw[k]) for k in groups[stage]]
        flight, token = _copies_start(stacks, name="grads_" + stage + "_start", by_owner=True)
        sent[stage] = flight
        sent_tokens.append(token)
        return token

    norms = {"ffn1": ffn1_norm + token_last[0, 0], "mix": mix_norm, "sb": sb_out_norm, "hg": hg_out_norm,
             "ffn2": ffn2_norm, "final": final_norm.reshape(1, D_MODEL)}
    loss_row, grad_x, gw, gv = _local_step(x[0], loss_target[0], norms, hg_lower_bound_logits, w_first, weights_after,
                                           grads_ready)

    lb_row = jnp.concatenate([gv["lb"], jnp.zeros_like(gv["lb"])], axis=1)
    part = _vector_rows([gv["ffn1"], gv["mix"], gv["ffn2"], gv["final"], jnp.concatenate([gv["sb"], gv["hg"]], axis=1),
                         lb_row, None, loss_row])
    vec_w = _vector_rows(vectors(ffn1_norm, mix_norm, sb_out_norm, hg_lower_bound_logits, hg_out_norm, ffn2_norm, final_norm))
    vec_m = _vector_rows(vectors(m_ffn1_norm, m_mix_norm, m_sb_out_norm, m_hg_lower_bound_logits, m_hg_out_norm,
                                 m_ffn2_norm, m_final_norm))
    vec_v = _vector_rows(vectors(v_ffn1_norm, v_mix_norm, v_sb_out_norm, v_hg_lower_bound_logits, v_hg_out_norm,
                                 v_ffn2_norm, v_final_norm))
    updated, after = {}, sent_tokens[-1]
    for stage, flight in sent.items():
        if stage == list(sent)[-1]:
            *vecs, loss_out = _vectors_update(part, vec_w, vec_m, vec_v, name="vectors_update", tie=after)
            after = loss_out
        _, lands = _copies_wait(flight, after, name="grads_" + stage + "_wait", by_owner=True)
        for k, part_k in zip(groups[stage], lands):
            updated[k] = _sum_and_update(part_k, view(k, w_sh[k]), view(k, m_sh[k]), view(k, v_sh[k]), name="adamw_" + k,
                                         tie=after)
            after = updated[k][0]
    mats = [{k: view(k, updated[k][i]) for k in keys} for i in range(4)]

    def leaves(mat, vec):
        half = D_MODEL // 2
        return (
            vec[0:1], mat["g1t"], mat["u1t"], mat["d1"], vec[1:2], mat["int"], vec[4:5, :half],
            vec[ROW_LOGITS].reshape(2, half), vec[4:5, half:], mat["out"], vec[2:3], mat["g2t"], mat["u2t"],
            mat["d2"], vec[3],
        )

    out = [loss_out[0, 0], grad_x[None]]
    for mat, vec in zip(mats, vecs):
        out.extend(leaves(mat, vec))
    return tuple(out)
```

```python
import jax
import jax.numpy as jnp
from jax import lax
from jax.experimental import pallas as pl
from jax.experimental.pallas import tpu as pltpu

F32, BF16 = jnp.float32, jnp.bfloat16
D_MODEL = 1024
D_FF = 2816
SB_WIDTH = 512
HG_WIDTH = 512
SB_HEAD_DIM = 64
HG_HEAD_DIM = 128
IN_COLS = 3584
EPS = 1e-6
N_DEV = 8
LANES = 128
HG_CHUNK = 16
VMEM_LIMIT_BYTES = 48 * 1024 * 1024
FFN_BWD_VMEM_LIMIT_BYTES = 56 * 1024 * 1024
ADAM_LR, ADAM_B1, ADAM_B2, ADAM_EPS, ADAM_WD, ADAM_STEP = 0.001, 0.9, 0.999, 1e-08, 0.01, 10
MESH = pl.DeviceIdType.MESH


def _params(*semantics, vmem_limit_bytes=VMEM_LIMIT_BYTES):
    return pltpu.CompilerParams(dimension_semantics=semantics, vmem_limit_bytes=vmem_limit_bytes)


def _dot(a, b):
    return jnp.dot(a, b, preferred_element_type=F32)


def _dot_nt(a, b):
    return lax.dot_general(a, b, (((1,), (1,)), ((), ())), preferred_element_type=F32)


def _dot_tn(a, b):
    return lax.dot_general(a, b, (((0,), (0,)), ((), ())), preferred_element_type=F32)


def _split3(x):
    hi = x.astype(BF16)
    r1 = x - hi.astype(F32)
    mid = r1.astype(BF16)
    lo = (r1 - mid.astype(F32)).astype(BF16)
    return hi, mid, lo


def _rms(xv):
    rstd = lax.rsqrt(jnp.mean(xv * xv, axis=-1, keepdims=True) + EPS)
    return xv * rstd, rstd


def _sigmoid(x):
    return 0.5 + 0.5 * jnp.tanh(0.5 * x)


def _loss_terms(xv, gain, target):
    xhat, rstd = _rms(xv)
    err = xhat * gain - target
    loss = 0.5 * jnp.sum(jnp.mean(err * err, axis=-1, keepdims=True), axis=0, keepdims=True)
    dy = err * (1.0 / xv.shape[-1])
    dxh = dy * gain
    dx = rstd * (dxh - xhat * jnp.mean(dxh * xhat, axis=-1, keepdims=True))
    return dx, jnp.sum(dy * xhat, axis=0, keepdims=True), loss


def _mm(a, b, *, name, tm, tn, nt=False, ta=False, out_dtype=F32, tie=None):
    k, m = a.shape if ta else a.shape[::-1]
    n = b.shape[0] if nt else b.shape[1]
    assert m % tm == 0 and n % tn == 0 and not (nt and ta), (name, a.shape, b.shape, tm, tn)

    def body(a_ref, b_ref, *rest):
        av = a_ref[...].astype(BF16)
        bv = b_ref[...].astype(BF16)
        rest[-1][...] = (_dot_nt(av, bv) if nt else _dot_tn(av, bv) if ta else _dot(av, bv)).astype(out_dtype)

    in_specs = [
        pl.BlockSpec((k, tm), lambda i, j: (0, i)) if ta else pl.BlockSpec((tm, k), lambda i, j: (i, 0)),
        pl.BlockSpec((tn, k), lambda i, j: (j, 0)) if nt else pl.BlockSpec((k, tn), lambda i, j: (0, j)),
    ]
    operands = [a, b]
    if tie is not None:
        in_specs.append(pl.BlockSpec(memory_space=pl.ANY))
        operands.append(tie)
    return pl.pallas_call(
        body,
        name=name,
        grid=(m // tm, n // tn),
        in_specs=in_specs,
        out_specs=pl.BlockSpec((tm, tn), lambda i, j: (i, j)),
        out_shape=jax.ShapeDtypeStruct((m, n), out_dtype),
        compiler_params=_params("parallel", "parallel"),
    )(*operands)


def _ffn_fwd(x, gain, wgt, wut, wd, *, name, next_gain=None, head=None, tm=1024, tf=256):
    t = x.shape[0]
    nj = D_FF // tf
    extra_in = [] if next_gain is None else [next_gain]
    extra_in += [] if head is None else list(head)

    def body(x_ref, g_ref, wg_ref, wu_ref, wd_prev_ref, wd_last_ref, *rest):
        extra, (xo_ref, a_ref, b_ref, h_ref, st_ref) = rest[:len(extra_in)], rest[len(extra_in):len(extra_in) + 5]
        tail_out, (acc, s_prev) = rest[len(extra_in) + 5:-2], rest[-2:]
        i = pl.program_id(0)
        j = pl.program_id(1)

        @pl.when(j == 0)
        def _():
            xhat, _ = _rms(x_ref[...])
            h_ref[...] = (xhat * g_ref[...]).astype(BF16)
            acc[...] = jnp.zeros_like(acc)
            s_prev[...] = jnp.zeros_like(s_prev)

        acc[...] += _dot(s_prev[...], wd_prev_ref[...])
        h = h_ref[...]
        a = _dot_nt(h, wg_ref[...])
        b = _dot_nt(h, wu_ref[...])
        a_ref[...] = a.astype(BF16)
        b_ref[...] = b.astype(BF16)
        s = (a * _sigmoid(a) * b).astype(BF16)
        st_ref[...] = s
        s_prev[...] = s

        @pl.when(j == nj - 1)
        def _():
            xo = x_ref[...] + 0.5 * (acc[...] + _dot(s, wd_last_ref[...]))
            if head is None:
                xo_ref[...] = xo
            if next_gain is not None:
                tail_out[0][...] = (_rms(xo)[0] * extra[0][...]).astype(BF16)
            if head is not None:
                gain_ref, target_ref = extra[-2:]
                dg_ref, loss_ref = tail_out[-2:]
                xo_ref[...], part_g, part_loss = _loss_terms(xo, gain_ref[...], target_ref[...])

                @pl.when(i == 0)
                def _():
                    dg_ref[...] = part_g
                    loss_ref[...] = jnp.broadcast_to(part_loss, loss_ref.shape)

                @pl.when(i > 0)
                def _():
                    dg_ref[...] += part_g
                    loss_ref[...] += jnp.broadcast_to(part_loss, loss_ref.shape)

    row = pl.BlockSpec((tm, D_MODEL), lambda i, j: (i, 0))
    vec = pl.BlockSpec((1, D_MODEL), lambda i, j: (0, 0))
    tile = pl.BlockSpec((tm, tf), lambda i, j: (i, j))
    weights = pl.BlockSpec((tf, D_MODEL), lambda i, j: (j, 0))
    tail_specs = ([] if next_gain is None else [row]) + ([] if head is None else [vec, vec])
    tail_shapes = ([] if next_gain is None else [jax.ShapeDtypeStruct((t, D_MODEL), BF16)]) + (
        [] if head is None else [jax.ShapeDtypeStruct((1, D_MODEL), F32)] * 2)
    return pl.pallas_call(
        body,
        name=name,
        grid=(t // tm, nj),
        in_specs=[
            row, vec, weights, weights,
            pl.BlockSpec((tf, D_MODEL), lambda i, j: (jnp.maximum(j - 1, 0), 0)),
            pl.BlockSpec((tf, D_MODEL), lambda i, j: (nj - 1, 0)),
        ] + ([] if next_gain is None else [vec]) + ([] if head is None else [vec, row]),
        out_specs=[row, tile, tile, row, tile] + tail_specs,
        out_shape=[
            jax.ShapeDtypeStruct((t, D_MODEL), F32),
            jax.ShapeDtypeStruct((t, D_FF), BF16),
            jax.ShapeDtypeStruct((t, D_FF), BF16),
            jax.ShapeDtypeStruct((t, D_MODEL), BF16),
            jax.ShapeDtypeStruct((t, D_FF), BF16),
        ] + tail_shapes,
        scratch_shapes=[pltpu.VMEM((tm, D_MODEL), F32), pltpu.VMEM((tm, tf), BF16)],
        compiler_params=_params("arbitrary", "arbitrary"),
    )(x, gain, wgt, wut, wd, wd, *extra_in)


def _ffn_bwd(dout, x, gain, a, b, wgt, wut, wd, *, name, tm=1024, tf=256):
    t = x.shape[0]
    nj = D_FF // tf

    def body(do_ref, x_ref, g_ref, a_ref, b_ref, wg_prev_ref, wu_prev_ref, wg_last_ref, wu_last_ref, wd_ref,
             dx_ref, dg_ref, da_ref, db_ref, dob_ref, dob_scr, dh, da_prev, db_prev):
        i = pl.program_id(0)
        j = pl.program_id(1)

        @pl.when(j == 0)
        def _():
            d = (0.5 * do_ref[...]).astype(BF16)
            dob_scr[...] = d
            dob_ref[...] = d
            dh[...] = jnp.zeros_like(dh)
            da_prev[...] = jnp.zeros_like(da_prev)
            db_prev[...] = jnp.zeros_like(db_prev)

        dh[...] += _dot(da_prev[...], wg_prev_ref[...]) + _dot(db_prev[...], wu_prev_ref[...])
        ds = _dot_nt(dob_scr[...], wd_ref[...])
        av = a_ref[...].astype(F32)
        bv = b_ref[...].astype(F32)
        sig = _sigmoid(av)
        dbv = (ds * (av * sig)).astype(BF16)
        dav = (ds * bv * (sig * (1.0 + av * (1.0 - sig)))).astype(BF16)
        da_ref[...] = dav
        db_ref[...] = dbv
        da_prev[...] = dav
        db_prev[...] = dbv

        @pl.when(j == nj - 1)
        def _():
            xhat, rstd = _rms(x_ref[...])
            dhv = dh[...] + _dot(dav, wg_last_ref[...]) + _dot(dbv, wu_last_ref[...])
            part = jnp.sum(dhv * xhat, axis=0, keepdims=True)

            @pl.when(i == 0)
            def _():
                dg_ref[...] = part

            @pl.when(i > 0)
            def _():
                dg_ref[...] += part

            dxh = dhv * g_ref[...]
            dx_ref[...] = do_ref[...] + rstd * (dxh - xhat * jnp.mean(dxh * xhat, axis=-1, keepdims=True))

    return pl.pallas_call(
        body,
        name=name,
        grid=(t // tm, nj),
        in_specs=[
            pl.BlockSpec((tm, D_MODEL), lambda i, j: (i, 0)),
            pl.BlockSpec((tm, D_MODEL), lambda i, j: (i, 0)),
            pl.BlockSpec((1, D_MODEL), lambda i, j: (0, 0)),
            pl.BlockSpec((tm, tf), lambda i, j: (i, j)),
            pl.BlockSpec((tm, tf), lambda i, j: (i, j)),
            pl.BlockSpec((tf, D_MODEL), lambda i, j: (jnp.maximum(j - 1, 0), 0)),
            pl.BlockSpec((tf, D_MODEL), lambda i, j: (jnp.maximum(j - 1, 0), 0)),
            pl.BlockSpec((tf, D_MODEL), lambda i, j: (nj - 1, 0)),
            pl.BlockSpec((tf, D_MODEL), lambda i, j: (nj - 1, 0)),
            pl.BlockSpec((tf, D_MODEL), lambda i, j: (j, 0)),
        ],
        out_specs=[
            pl.BlockSpec((tm, D_MODEL), lambda i, j: (i, 0)),
            pl.BlockSpec((1, D_MODEL), lambda i, j: (0, 0)),
            pl.BlockSpec((tm, tf), lambda i, j: (i, j)),
            pl.BlockSpec((tm, tf), lambda i, j: (i, j)),
            pl.BlockSpec((tm, D_MODEL), lambda i, j: (i, 0)),
        ],
        out_shape=[
            jax.ShapeDtypeStruct((t, D_MODEL), F32),
            jax.ShapeDtypeStruct((1, D_MODEL), F32),
            jax.ShapeDtypeStruct((t, D_FF), BF16),
            jax.ShapeDtypeStruct((t, D_FF), BF16),
            jax.ShapeDtypeStruct((t, D_MODEL), BF16),
        ],
        scratch_shapes=[pltpu.VMEM((tm, D_MODEL), BF16), pltpu.VMEM((tm, D_MODEL), F32), pltpu.VMEM((tm, tf), BF16),
                        pltpu.VMEM((tm, tf), BF16)],
        compiler_params=_params("arbitrary", "arbitrary", vmem_limit_bytes=FFN_BWD_VMEM_LIMIT_BYTES),
    )(dout, x, gain, a, b, wgt, wut, wgt, wut, wd)


def _ffn_bwd_act(dout_half, a, b, wd, *, name, tie, tm=1024, tf=256):
    t = a.shape[0]

    def body(dob_ref, a_ref, b_ref, wd_ref, tie_ref, da_ref, db_ref):
        ds = _dot_nt(dob_ref[...], wd_ref[...])
        av = a_ref[...].astype(F32)
        bv = b_ref[...].astype(F32)
        sig = _sigmoid(av)
        db_ref[...] = (ds * (av * sig)).astype(BF16)
        da_ref[...] = (ds * bv * (sig * (1.0 + av * (1.0 - sig)))).astype(BF16)

    tile = pl.BlockSpec((tm, tf), lambda i, j: (i, j))
    return pl.pallas_call(
        body,
        name=name,
        grid=(t // tm, D_FF // tf),
        in_specs=[pl.BlockSpec((tm, D_MODEL), lambda i, j: (i, 0)), tile, tile,
                  pl.BlockSpec((tf, D_MODEL), lambda i, j: (j, 0)), pl.BlockSpec(memory_space=pl.ANY)],
        out_specs=[tile, tile],
        out_shape=[jax.ShapeDtypeStruct((t, D_FF), BF16)] * 2,
        compiler_params=_params("parallel", "parallel"),
    )(dout_half, a, b, wd, tie)


def _ffn_bwd_in(dout, x, gain, da, db, wgt, wut, *, name, tm=1024, tf=256):
    t = x.shape[0]
    nj = D_FF // tf

    def body(do_ref, x_ref, g_ref, da_ref, db_ref, wg_ref, wu_ref, dx_ref, dg_ref, dh):
        i = pl.program_id(0)
        j = pl.program_id(1)
        part_h = _dot(da_ref[...], wg_ref[...]) + _dot(db_ref[...], wu_ref[...])

        @pl.when(j == 0)
        def _():
            dh[...] = part_h

        @pl.when(j > 0)
        def _():
            dh[...] += part_h

        @pl.when(j == nj - 1)
        def _():
            xhat, rstd = _rms(x_ref[...])
            dhv = dh[...]
            part = jnp.sum(dhv * xhat, axis=0, keepdims=True)

            @pl.when(i == 0)
            def _():
                dg_ref[...] = part

            @pl.when(i > 0)
            def _():
                dg_ref[...] += part

            dxh = dhv * g_ref[...]
            dx_ref[...] = do_ref[...] + rstd * (dxh - xhat * jnp.mean(dxh * xhat, axis=-1, keepdims=True))

    row = pl.BlockSpec((tm, D_MODEL), lambda i, j: (i, 0))
    vec = pl.BlockSpec((1, D_MODEL), lambda i, j: (0, 0))
    tile = pl.BlockSpec((tm, tf), lambda i, j: (i, j))
    weights = pl.BlockSpec((tf, D_MODEL), lambda i, j: (j, 0))
    return pl.pallas_call(
        body,
        name=name,
        grid=(t // tm, nj),
        in_specs=[row, row, vec, tile, tile, weights, weights],
        out_specs=[row, vec],
        out_shape=[jax.ShapeDtypeStruct((t, D_MODEL), F32), jax.ShapeDtypeStruct((1, D_MODEL), F32)],
        scratch_shapes=[pltpu.VMEM((tm, D_MODEL), F32)],
        compiler_params=_params("arbitrary", "arbitrary", vmem_limit_bytes=FFN_BWD_VMEM_LIMIT_BYTES),
    )(dout, x, gain, da, db, wgt, wut)


def _in_proj_bwd(dproj, w_int, x, gain, dres, *, name, tm=512):
    t, k = dproj.shape

    def body(dp_ref, w_ref, x_ref, g_ref, dr_ref, dx_ref, dg_ref, dxb_ref):
        i = pl.program_id(0)
        dhv = _dot(dp_ref[...], w_ref[...])
        xhat, rstd = _rms(x_ref[...])
        part = jnp.sum(dhv * xhat, axis=0, keepdims=True)

        @pl.when(i == 0)
        def _():
            dg_ref[...] = part

        @pl.when(i > 0)
        def _():
            dg_ref[...] += part

        dxh = dhv * g_ref[...]
        dx = dr_ref[...] + rstd * (dxh - xhat * jnp.mean(dxh * xhat, axis=-1, keepdims=True))
        dx_ref[...] = dx
        dxb_ref[...] = (0.5 * dx).astype(BF16)

    row = pl.BlockSpec((tm, D_MODEL), lambda i: (i, 0))
    vec = pl.BlockSpec((1, D_MODEL), lambda i: (0, 0))
    return pl.pallas_call(
        body,
        name=name,
        grid=(t // tm,),
        in_specs=[pl.BlockSpec((tm, k), lambda i: (i, 0)), pl.BlockSpec((k, D_MODEL), lambda i: (0, 0)), row, vec, row],
        out_specs=[row, vec, row],
        out_shape=[jax.ShapeDtypeStruct((t, D_MODEL), F32), jax.ShapeDtypeStruct((1, D_MODEL), F32),
                   jax.ShapeDtypeStruct((t, D_MODEL), BF16)],
        compiler_params=_params("arbitrary"),
    )(dproj, w_int, x, gain, dres)


ATT_Q_TILE = 512
ATT_K_BLOCK = 256


def _first_head_lanes():
    return lax.broadcasted_iota(jnp.int32, (1, LANES), 1) < SB_HEAD_DIM


def _stack_heads(x):
    first = _first_head_lanes()
    return jnp.concatenate([jnp.where(first, x, 0.0), jnp.where(first, 0.0, x)], axis=0)


def _unstack_heads(x, rows):
    return jnp.where(_first_head_lanes(), x[:rows], x[rows:])


def _rows_from(x, first, rows):
    return x if first == 0 else jnp.concatenate([x[first:rows], x[rows + first:]], axis=0)


def _rows_into(full, part, first, rows):
    if first == 0:
        return part
    n = rows - first
    return jnp.concatenate([full[:first], part[:n], full[rows:rows + first], part[n:]], axis=0)


def _tri(n, relation):
    r = lax.broadcasted_iota(jnp.int32, (n, n), 0)
    c = lax.broadcasted_iota(jnp.int32, (n, n), 1)
    return relation(r, c).astype(BF16)


def _scan_dot(x, tri):
    hi = x.astype(BF16)
    lo = (x - hi.astype(F32)).astype(BF16)
    return _dot(jnp.concatenate([hi, lo], axis=1), jnp.concatenate([tri, tri], axis=0))


def _log_terms(z):
    lbeta = jnp.minimum(z, 0.0) - jnp.log(1.0 + jnp.exp(-jnp.abs(z)))
    return lbeta, lbeta - z


def _attn_fwd(proj, *, name):
    t = proj.shape[0]
    tq, tk = ATT_Q_TILE, ATT_K_BLOCK
    diag = tq // tk
    n_pairs = SB_WIDTH // LANES

    def body(q_ref, k_ref, v_ref, o_ref, kept_ref):
        qi = pl.program_id(1)
        q = q_ref[...] * (SB_HEAD_DIM ** -0.5)
        qs = _stack_heads(q).astype(BF16)
        tri = _tri(tk, lambda j, s: j > s)
        trow = lax.broadcasted_iota(jnp.int32, (tq, tk), 0)
        scol = lax.broadcasted_iota(jnp.int32, (tq, tk), 1)

        def block(kb, carry, causal, first=0):
            acc, c = carry
            off = pl.multiple_of(kb * tk, tk)
            lbeta, lrest = _log_terms(_dot_nt(_rows_from(qs, first, tq), k_ref[pl.ds(off, tk), :].astype(BF16)))
            if causal is not None:
                lrest = jnp.where(causal, lrest, 0.0)
            w = jnp.exp(lbeta + (_scan_dot(lrest, tri) + _rows_from(c, first, tq)))
            if causal is not None:
                w = jnp.where(causal, w, 0.0)
            wb = w.astype(BF16)
            kept_ref[0, 0, kb] = _rows_into(jnp.zeros((2 * tq, tk), BF16), wb, first, tq)
            acc = _rows_into(acc, _rows_from(acc, first, tq) + _dot(wb, v_ref[pl.ds(off, tk), :].astype(BF16)), first, tq)
            return acc, _rows_into(c, _rows_from(c, first, tq) + jnp.sum(lrest, axis=1, keepdims=True), first, tq)

        carry = (jnp.zeros((2 * tq, LANES), F32), jnp.zeros((2 * tq, 1), F32))
        n_full = qi * diag
        for j in reversed(range(diag)):
            mask = ((scol + j * tk) < trow)[j * tk:]
            carry = block(n_full + j, carry, jnp.concatenate([mask, mask], axis=0), first=j * tk)

        def odd_tile(carry):
            for j in range(diag):
                carry = block(n_full - 1 - j, carry, None)
            return carry

        carry = lax.cond(qi % 2 == 1, odd_tile, lambda c: c, carry)
        last = n_full - 1 - (qi % 2) * diag

        def step(it, carry):
            for j in range(2 * diag):
                carry = block(last - (2 * diag * it + j), carry, None)
            return carry

        acc, _ = lax.fori_loop(0, qi // 2, step, carry)
        o_ref[...] = _unstack_heads(acc, tq)

    return pl.pallas_call(
        body,
        name=name,
        grid=(n_pairs, t // tq),
        in_specs=[
            pl.BlockSpec((tq, LANES), lambda p, i: (i, p)),
            pl.BlockSpec((t, LANES), lambda p, i: (0, n_pairs + p)),
            pl.BlockSpec((t, LANES), lambda p, i: (0, 2 * n_pairs + p)),
        ],
        out_specs=[pl.BlockSpec((tq, LANES), lambda p, i: (i, p)),
                   pl.BlockSpec((1, 1, t // tk, 2 * tq, tk), lambda p, i: (p, i, 0, 0, 0))],
        out_shape=[jax.ShapeDtypeStruct((t, SB_WIDTH), F32),
                   jax.ShapeDtypeStruct((n_pairs, t // tq, t // tk, 2 * tq, tk), BF16)],
        compiler_params=_params("parallel", "parallel"),
    )(proj, proj, proj)


def _attn_bwd(proj, kept, do, *, name, tie=None):
    t = proj.shape[0]
    tq, tk = ATT_Q_TILE, ATT_K_BLOCK
    diag = tq // tk
    n_pairs = SB_WIDTH // LANES
    scale = SB_HEAD_DIM ** -0.5

    def body(q_ref, k_ref, v_ref, kept_ref, do_ref, *rest):
        dq_ref, dk_ref, dv_ref = rest[-3:]
        qi = pl.program_id(1)

        @pl.when(qi == 0)
        def _():
            dk_ref[...] = jnp.zeros_like(dk_ref)
            dv_ref[...] = jnp.zeros_like(dv_ref)

        qs = _stack_heads(q_ref[...] * scale).astype(BF16)
        dos = _stack_heads(do_ref[...]).astype(BF16)
        before = _tri(tk, lambda s, j: s < j)
        trow = lax.broadcasted_iota(jnp.int32, (tq, tk), 0)
        scol = lax.broadcasted_iota(jnp.int32, (tq, tk), 1)

        def block(kb, carry, causal, first=0):
            dq, cg = carry
            off = pl.multiple_of(kb * tk, tk)
            q_rows, do_rows = _rows_from(qs, first, tq), _rows_from(dos, first, tq)
            wb = _rows_from(kept_ref[0, 0, kb], first, tq)
            kblk = k_ref[pl.ds(off, tk), :].astype(BF16)
            sig = _sigmoid(_dot_nt(q_rows, kblk))
            g = wb.astype(F32) * _dot_nt(do_rows, v_ref[pl.ds(off, tk), :].astype(BF16))
            prior = _scan_dot(g, before) + _rows_from(cg, first, tq)
            dz = g - sig * (g + prior)
            if causal is not None:
                dz = jnp.where(causal, dz, 0.0)
            dzb = dz.astype(BF16)
            dq = _rows_into(dq, _rows_from(dq, first, tq) + _dot(dzb, kblk), first, tq)
            dk_ref[pl.ds(off, tk), :] += _dot_tn(dzb, q_rows)
            dv_ref[pl.ds(off, tk), :] += _dot_tn(wb, do_rows)
            return dq, _rows_into(cg, _rows_from(cg, first, tq) + jnp.sum(g, axis=1, keepdims=True), first, tq)

        n_full = qi * diag

        def step(it, carry):
            for j in range(2 * diag):
                carry = block(2 * diag * it + j, carry, None)
            return carry

        def odd_tile(carry):
            for j in range(diag):
                carry = block(n_full - diag + j, carry, None)
            return carry

        carry = lax.fori_loop(0, qi // 2, step, (jnp.zeros((2 * tq, LANES), F32), jnp.zeros((2 * tq, 1), F32)))
        carry = lax.cond(qi % 2 == 1, odd_tile, lambda c: c, carry)
        for j in range(diag):
            mask = ((scol + j * tk) < trow)[j * tk:]
            carry = block(n_full + j, carry, jnp.concatenate([mask, mask], axis=0), first=j * tk)
        dq_ref[...] = (_unstack_heads(carry[0], tq) * scale).astype(BF16)

    tile_spec = pl.BlockSpec((tq, LANES), lambda p, i: (i, p))
    full_spec = pl.BlockSpec((t, LANES), lambda p, i: (0, p))
    return pl.pallas_call(
        body,
        name=name,
        grid=(n_pairs, t // tq),
        in_specs=[
            tile_spec,
            pl.BlockSpec((t, LANES), lambda p, i: (0, n_pairs + p)),
            pl.BlockSpec((t, LANES), lambda p, i: (0, 2 * n_pairs + p)),
            pl.BlockSpec((1, 1, t // tk, 2 * tq, tk), lambda p, i: (p, i, 0, 0, 0)),
            tile_spec,
        ] + ([] if tie is None else [pl.BlockSpec(memory_space=pl.ANY)]),
        out_specs=[tile_spec, full_spec, full_spec],
        out_shape=[jax.ShapeDtypeStruct((t, SB_WIDTH), BF16)] + [jax.ShapeDtypeStruct((t, SB_WIDTH), F32)] * 2,
        compiler_params=_params("arbitrary", "arbitrary"),
    )(proj, proj, proj, kept, do, *([] if tie is None else [tie]))


HG_BLOCK = 128
HG_HEADS = HG_WIDTH // HG_HEAD_DIM


def _chunk_mats(n):
    r = lax.broadcasted_iota(jnp.int32, (n, n), 0)
    c = lax.broadcasted_iota(jnp.int32, (n, n), 1)
    same = (r // HG_CHUNK) == (c // HG_CHUNK)
    upto = (same & (c <= r)).astype(BF16)
    whole = same.astype(BF16)
    onward = (same & (c >= r)).astype(BF16)
    return upto, whole, onward


def _rows_dot(mat, x):
    return _dot(jnp.concatenate([mat, mat, mat], axis=1), jnp.concatenate(_split3(x), axis=0))


def _split_heads(x):
    return jnp.stack([x[:, h * HG_HEAD_DIM:(h + 1) * HG_HEAD_DIM] for h in range(HG_HEADS)], axis=0)


def _merge_heads(x):
    return jnp.concatenate([x[h] for h in range(HG_HEADS)], axis=1)


def _lower_bound(lg_ref):
    lg = lg_ref[...]
    return _sigmoid(lg[0:1, :] - lg[1:2, :])


def _hgrn_prepare(q_ref, f_ref, lb, h, upto, whole):
    cols = slice(h * HG_HEAD_DIM, (h + 1) * HG_HEAD_DIM)
    lbh = lb[:, cols]
    sg = _sigmoid(f_ref[:, cols])
    forget = lbh + (1.0 - lbh) * sg
    logf = jnp.log(forget)
    kk = (1.0 - lbh) * (1.0 - sg)
    qv = q_ref[:, cols]
    qsig = _sigmoid(qv)
    qh = qv * qsig
    b = _rows_dot(upto, logf)
    blast = _rows_dot(whole, logf)
    return dict(lbh=lbh, sg=sg, forget=forget, kk=kk, qv=qv, qsig=qsig, qh=qh, b=b, eb=jnp.exp(b),
                ekb=jnp.exp(blast - b), dl=jnp.exp(blast))


def _hgrn_fwd(proj, logits, *, name):
    t = proj.shape[0]
    tb = HG_BLOCK
    nc = tb // HG_CHUNK
    hd = HG_HEAD_DIM

    def body(q_ref, f_ref, i_ref, lg_ref, o_ref, st_ref, state, qh_s, kk_s, b_s, qe_s, ke_s, dl_s):
        @pl.when(pl.program_id(0) == 0)
        def _():
            state[...] = jnp.zeros_like(state)

        lb = _lower_bound(lg_ref)
        upto, whole, _ = _chunk_mats(tb)
        for h in range(HG_HEADS):
            p = _hgrn_prepare(q_ref, f_ref, lb, h, upto, whole)
            qh_s[h] = p["qh"]
            kk_s[h] = p["kk"]
            b_s[h] = p["b"]
            qe_s[h] = (p["qh"] * p["eb"]).astype(BF16)
            ke_s[h] = (p["kk"] * p["ekb"]).astype(BF16)
            dl_s[h] = p["dl"]
        rowi = lax.broadcasted_iota(jnp.int32, (HG_HEADS, HG_CHUNK, hd), 1)

        def chunk(c, _):
            r0 = pl.multiple_of(c * HG_CHUNK, HG_CHUNK)
            rows = pl.ds(r0, HG_CHUNK)
            bc = b_s[:, rows, :]
            qc = qh_s[:, rows, :]
            kc = kk_s[:, rows, :]
            vc = _split_heads(i_ref[rows, :])
            s_in = state[...]
            st_ref[c] = s_in
            s_in_b = s_in.astype(BF16)
            qe = qe_s[:, rows, :]
            o = jnp.stack([_dot_nt(qe[h], s_in_b[h]) for h in range(HG_HEADS)], axis=0)
            for s in range(HG_CHUNK):
                pair = jnp.where(rowi >= s, qc * jnp.exp(bc - bc[:, s:s + 1, :]) * kc[:, s:s + 1, :], 0.0)
                o = o + jnp.sum(pair, axis=2, keepdims=True) * vc[:, s:s + 1, :]
            o_ref[rows, :] = _merge_heads(o)
            vcb = vc.astype(BF16)
            ke = ke_s[:, rows, :]
            update = jnp.stack([_dot_tn(vcb[h], ke[h]) for h in range(HG_HEADS)], axis=0)
            state[...] = s_in * dl_s[:, pl.ds(r0, 1), :] + update
            return 0

        lax.fori_loop(0, nc, chunk, 0, unroll=4)

    blk =lambda col: pl.BlockSpec((tb, HG_WIDTH), lambda i: (i, col))
    head_f32 = pltpu.VMEM((HG_HEADS, tb, hd), F32)
    head_bf16 = pltpu.VMEM((HG_HEADS, tb, hd), BF16)
    return pl.pallas_call(
        body,
        name=name,
        grid=(t // tb,),
        in_specs=[blk(3), blk(4), blk(5), pl.BlockSpec((2, HG_WIDTH), lambda i: (0, 0))],
        out_specs=[
            pl.BlockSpec((tb, HG_WIDTH), lambda i: (i, 0)),
            pl.BlockSpec((nc, HG_HEADS, hd, hd), lambda i: (i, 0, 0, 0)),
        ],
        out_shape=[
            jax.ShapeDtypeStruct((t, HG_WIDTH), F32),
            jax.ShapeDtypeStruct((t // HG_CHUNK, HG_HEADS, hd, hd), F32),
        ],
        scratch_shapes=[pltpu.VMEM((HG_HEADS, hd, hd), F32), head_f32, head_f32, head_f32, head_bf16, head_bf16,
                        head_f32],
        compiler_params=_params("arbitrary"),
    )(proj, proj, proj, logits)


def _hgrn_bwd(proj, logits, states, do, *, name):
    t = proj.shape[0]
    tb = HG_BLOCK
    nb = t // tb
    nc = tb // HG_CHUNK
    hd = HG_HEAD_DIM

    def body(q_ref, f_ref, i_ref, lg_ref, st_ref, do_ref, dq_ref, df_ref, di_ref, dlb_ref,
             dstate, qh_s, kk_s, b_s, eb_s, ekb_s, qe_s, ke_s, dl_s, dqh_s, dkk_s, dlf_s):
        step = pl.program_id(0)

        @pl.when(step == 0)
        def _():
            dstate[...] = jnp.zeros_like(dstate)
            dlb_ref[...] = jnp.zeros_like(dlb_ref)

        lb = _lower_bound(lg_ref)
        upto, whole, _ = _chunk_mats(tb)
        prepared = []
        for h in range(HG_HEADS):
            p = _hgrn_prepare(q_ref, f_ref, lb, h, upto, whole)
            prepared.append(p)
            qh_s[h] = p["qh"]
            kk_s[h] = p["kk"]
            b_s[h] = p["b"]
            eb_s[h] = p["eb"]
            ekb_s[h] = p["ekb"]
            qe_s[h] = (p["qh"] * p["eb"]).astype(BF16)
            ke_s[h] = (p["kk"] * p["ekb"]).astype(BF16)
            dl_s[h] = p["dl"]
        rowi = lax.broadcasted_iota(jnp.int32, (HG_CHUNK, hd), 0)
        r16 = lax.broadcasted_iota(jnp.int32, (HG_CHUNK, HG_CHUNK), 0)
        c16 = lax.broadcasted_iota(jnp.int32, (HG_CHUNK, HG_CHUNK), 1)
        onward = (c16 >= r16).astype(BF16)

        def chunk(it, _):
            c = nc - 1 - it
            r0 = pl.multiple_of(c * HG_CHUNK, HG_CHUNK)
            rows = pl.ds(r0, HG_CHUNK)
            for h in range(HG_HEADS):
                cols = slice(h * hd, (h + 1) * hd)
                bc = b_s[h, rows, :]
                qc = qh_s[h, rows, :]
                kc = kk_s[h, rows, :]
                vc = i_ref[rows, cols]
                doc = do_ref[rows, cols]
                s_in = st_ref[c, h]
                ds_out = dstate[h]
                ds_out_b = ds_out.astype(BF16)
                docb = doc.astype(BF16)
                dl_row = dl_s[h, pl.ds(r0, 1), :]
                dqh = _dot(docb, s_in.astype(BF16)) * eb_s[h, rows, :]
                dkk = _dot(vc.astype(BF16), ds_out_b) * ekb_s[h, rows, :]
                dv = _dot_nt(ke_s[h, rows, :], ds_out_b)
                db = dqh * qc - dkk * kc
                dwhole = jnp.sum(dkk * kc, axis=0, keepdims=True) + jnp.sum(ds_out * s_in, axis=0, keepdims=True) * dl_row
                dk_rows, dv_rows = [], []
                for s in range(HG_CHUNK):
                    keep = rowi >= s
                    e = jnp.exp(bc - bc[s:s + 1, :])
                    k_row = kc[s:s + 1, :]
                    pcol = jnp.sum(jnp.where(keep, qc * e * k_row, 0.0), axis=1, keepdims=True)
                    dpcol = jnp.sum(doc * vc[s:s + 1, :], axis=1, keepdims=True)
                    m = jnp.where(keep, e * dpcol, 0.0)
                    y = m * qc
                    dqh = dqh + m * k_row
                    db = db + y * k_row
                    dk_rows.append(jnp.sum(y, axis=0, keepdims=True))
                    dv_rows.append(jnp.sum(pcol * doc, axis=0, keepdims=True))
                dkk_pairs = jnp.concatenate(dk_rows, axis=0)
                dkk = dkk + dkk_pairs
                db = db - dkk_pairs * kc
                dv = dv + jnp.concatenate(dv_rows, axis=0)
                dqh_s[h, rows, :] = dqh
                dkk_s[h, rows, :] = dkk
                dlf_s[h, rows, :] = _rows_dot(onward, db) + dwhole
                di_ref[rows, cols] = dv.astype(BF16)
                dstate[h] = ds_out * dl_row + _dot_tn(docb, qe_s[h, rows, :])
            return 0

        lax.fori_loop(0, nc, chunk, 0, unroll=4)
        for h in range(HG_HEADS):
            cols = slice(h * hd, (h + 1) * hd)
            p = prepared[h]
            dq_ref[:, cols] = (dqh_s[h] * (p["qsig"] * (1.0 + p["qv"] * (1.0 - p["qsig"])))).astype(BF16)
            dforget = dlf_s[h] / p["forget"] - dkk_s[h]
            df_ref[:, cols] = (dforget * (1.0 - p["lbh"]) * p["sg"] * (1.0 - p["sg"])).astype(BF16)
            dlb_ref[:, cols] += jnp.sum(dforget * (1.0 - p["sg"]), axis=0, keepdims=True)

    blk = lambda col: pl.BlockSpec((tb, HG_WIDTH), lambda i: (nb - 1 - i, col))
    vec = pl.BlockSpec((1, HG_WIDTH), lambda i: (0, 0))
    head_f32 = pltpu.VMEM((HG_HEADS, tb, hd), F32)
    head_bf16 = pltpu.VMEM((HG_HEADS, tb, hd), BF16)
    return pl.pallas_call(
        body,
        name=name,
        grid=(nb,),
        in_specs=[
            blk(3), blk(4), blk(5),
            pl.BlockSpec((2, HG_WIDTH), lambda i: (0, 0)),
            pl.BlockSpec((nc, HG_HEADS, hd, hd), lambda i: (nb - 1 - i, 0, 0, 0)),
            blk(0),
        ],
        out_specs=[blk(0), blk(0), blk(0), vec],
        out_shape=[jax.ShapeDtypeStruct((t, HG_WIDTH), BF16)] * 3 + [jax.ShapeDtypeStruct((1, HG_WIDTH), F32)],
        scratch_shapes=[
            pltpu.VMEM((HG_HEADS, hd, hd), F32),
            head_f32, head_f32, head_f32, head_f32, head_f32, head_bf16, head_bf16, head_f32,
            head_f32, head_f32, head_f32,
        ],
        compiler_params=_params("arbitrary"),
    )(proj, proj, proj, logits, states, do)


def _group_mat(width, head_dim):
    r = lax.broadcasted_iota(jnp.int32, (width, width), 0)
    c = lax.broadcasted_iota(jnp.int32, (width, width), 1)
    return ((r // head_dim) == (c // head_dim)).astype(BF16)


def _head_mean(x, mat, head_dim):
    hi = x.astype(BF16)
    lo = (x - hi.astype(F32)).astype(BF16)
    return (_dot(hi, mat) + _dot(lo, mat)) * (1.0 / head_dim)


def _mix_out_fwd(o_sb, o_hg, proj, g_sb, g_hg, w_out, x1, *, name, tm=512):
    t = x1.shape[0]

    def body(osb_ref, ohg_ref, gate_ref, gsb_ref, ghg_ref, w_ref, x_ref, xo_ref, mt_ref):
        msb = _group_mat(SB_WIDTH, SB_HEAD_DIM)
        mhg = _group_mat(HG_WIDTH, HG_HEAD_DIM)
        osb = osb_ref[...]
        ohg = ohg_ref[...]
        nsb = osb * lax.rsqrt(_head_mean(osb * osb, msb, SB_HEAD_DIM) + EPS) * gsb_ref[...]
        gate = gate_ref[...]
        nhg = ohg * lax.rsqrt(_head_mean(ohg * ohg, mhg, HG_HEAD_DIM) + EPS) * ghg_ref[...] * (gate * _sigmoid(gate))
        mixed = jnp.concatenate([nsb, nhg], axis=1).astype(BF16)
        mt_ref[...] = mixed
        xo_ref[...] = x_ref[...] + _dot(mixed, w_ref[...])

    half = pl.BlockSpec((tm, SB_WIDTH), lambda i: (i, 0))
    vec = pl.BlockSpec((1, SB_WIDTH), lambda i: (0, 0))
    row = pl.BlockSpec((tm, D_MODEL), lambda i: (i, 0))
    return pl.pallas_call(
        body,
        name=name,
        grid=(t // tm,),
        in_specs=[half, half, pl.BlockSpec((tm, HG_WIDTH), lambda i: (i, 6)), vec, vec,
                  pl.BlockSpec((D_MODEL, D_MODEL), lambda i: (0, 0)), row],
        out_specs=[row, row],
        out_shape=[jax.ShapeDtypeStruct((t, D_MODEL), F32), jax.ShapeDtypeStruct((t, D_MODEL), BF16)],
        compiler_params=_params("parallel"),
    )(o_sb, o_hg, proj, g_sb, g_hg, w_out, x1)


def _mix_out_bwd(dx2, o_sb, o_hg, proj, g_sb, g_hg, w_out, *, name, tm=512):
    t = dx2.shape[0]

    def body(dx_ref, osb_ref, ohg_ref, gate_ref, gsb_ref, ghg_ref, w_ref, dosb_ref, dohg_ref, dgate_ref, dgsb_ref,
             dghg_ref, dxb_ref):
        i = pl.program_id(0)
        msb = _group_mat(SB_WIDTH, SB_HEAD_DIM)
        mhg = _group_mat(HG_WIDTH, HG_HEAD_DIM)
        dxb = dx_ref[...].astype(BF16)
        dxb_ref[...] = dxb
        dmixed = _dot_nt(dxb, w_ref[...])
        dnsb = dmixed[:, :SB_WIDTH]
        dy = dmixed[:, SB_WIDTH:]

        osb = osb_ref[...]
        rstd = lax.rsqrt(_head_mean(osb * osb, msb, SB_HEAD_DIM) + EPS)
        ohat = osb * rstd
        part_sb = jnp.sum(dnsb * ohat, axis=0, keepdims=True)
        dohat = dnsb * gsb_ref[...]
        dosb_ref[...] = rstd * (dohat - ohat * _head_mean(dohat * ohat, msb, SB_HEAD_DIM))

        ohg = ohg_ref[...]
        rstd = lax.rsqrt(_head_mean(ohg * ohg, mhg, HG_HEAD_DIM) + EPS)
        ohat = ohg * rstd
        gate = gate_ref[...]
        sig = _sigmoid(gate)
        dn = dy * (gate * sig)
        dgate_ref[...] = (dy * (ohat * ghg_ref[...]) * (sig * (1.0 + gate * (1.0 - sig)))).astype(BF16)
        part_hg = jnp.sum(dn * ohat, axis=0, keepdims=True)
        dohat = dn * ghg_ref[...]
        dohg_ref[...] = rstd * (dohat - ohat * _head_mean(dohat * ohat, mhg, HG_HEAD_DIM))

        @pl.when(i == 0)
        def _():
            dgsb_ref[...] = part_sb
            dghg_ref[...] = part_hg

        @pl.when(i > 0)
        def _():
            dgsb_ref[...] += part_sb
            dghg_ref[...] += part_hg

    half = pl.BlockSpec((tm, SB_WIDTH), lambda i: (i, 0))
    vec = pl.BlockSpec((1, SB_WIDTH), lambda i: (0, 0))
    row = pl.BlockSpec((tm, D_MODEL), lambda i: (i, 0))
    return pl.pallas_call(
        body,
        name=name,
        grid=(t // tm,),
        in_specs=[row, half, half, pl.BlockSpec((tm, HG_WIDTH), lambda i: (i, 6)), vec, vec,
                  pl.BlockSpec((D_MODEL, D_MODEL), lambda i: (0, 0))],
        out_specs=[half, half, half, vec, vec, row],
        out_shape=[jax.ShapeDtypeStruct((t, SB_WIDTH), F32)] * 2 + [jax.ShapeDtypeStruct((t, SB_WIDTH), BF16)]
        + [jax.ShapeDtypeStruct((1, SB_WIDTH), F32)] * 2 + [jax.ShapeDtypeStruct((t, D_MODEL), BF16)],
        compiler_params=_params("arbitrary"),
    )(dx2, o_sb, o_hg, proj, g_sb, g_hg, w_out)


def _local_step(x, target, norms, logits, w, weights_after=None, grads_ready=None):
    w = dict(w)
    x1, a1, b1, h1, s1, hm = _ffn_fwd(x, norms["ffn1"], w["g1t"], w["u1t"], w["d1"], name="ffn1_fwd",
                                      next_gain=norms["mix"])
    if weights_after is not None:
        w.update(weights_after("ffn1", x1))
    proj = _mm(hm, w["int"], name="in_proj", tm=512, tn=IN_COLS, nt=True)
    o_sb, sb_kept = _attn_fwd(proj, name="sb_attn_fwd")
    o_hg, states = _hgrn_fwd(proj, logits, name="hgrn2_fwd")
    x2, mixed = _mix_out_fwd(o_sb, o_hg, proj, norms["sb"], norms["hg"], w["out"], x1, name="mix_out_fwd")
    if weights_after is not None:
        w.update(weights_after("mix", x2))
    dx3, a2, b2, h2, s2, d_final, loss_row = _ffn_fwd(x2, norms["ffn2"], w["g2t"], w["u2t"], w["d2"], name="ffn2_fwd",
                                                      head=(norms["final"], target))

    def weight_grad(lhs, rhs, name, tie=None):
        return _mm(lhs, rhs, name=name, tm=256, tn=D_MODEL, ta=True, out_dtype=BF16, tie=tie)

    def sent(stage):
        return grads_ready(stage, gw) if grads_ready is not None else None

    gw, gv = {}, {"final": d_final}
    dx2, gv["ffn2"], da2, db2, dob2 = _ffn_bwd(dx3, x2, norms["ffn2"], a2, b2, w["g2t"], w["u2t"], w["d2"],
                                               name="ffn2_bwd")
    gw["g2t"] = weight_grad(da2, h2, "ffn2_dgate")
    gw["u2t"] = weight_grad(db2, h2, "ffn2_dup")
    gw["d2"] = weight_grad(s2, dob2, "ffn2_ddown")

    do_sb, do_hg, d_gate, gv["sb"], gv["hg"], dx2b = _mix_out_bwd(
        dx2, o_sb, o_hg, proj, norms["sb"], norms["hg"], w["out"], name="mix_out_bwd")
    gw["out"] = weight_grad(mixed, dx2b, "out_dw")
    tie = sent("mix")
    dq_sb, dk_sb, dv_sb = _attn_bwd(proj, sb_kept, do_sb, name="sb_attn_bwd", tie=tie)
    dq_hg, df_hg, di_hg, d_lb = _hgrn_bwd(proj, logits if tie is None else logits + tie[0, 0], states, do_hg,
                                          name="hgrn2_bwd")
    dproj = jnp.concatenate([dq_sb, dk_sb.astype(BF16), dv_sb.astype(BF16), dq_hg, df_hg, di_hg, d_gate], axis=1)
    gw["int"] = weight_grad(dproj, hm, "in_dw")
    tie = sent("in")
    dx1, gv["mix"], dob1 = _in_proj_bwd(dproj, w["int"], x1, norms["mix"] if tie is None else norms["mix"] + tie[0, 0],
                                        dx2, name="in_dx")

    gw["d1"] = weight_grad(s1, dob1, "ffn1_ddown")
    tie = sent("d1")
    da1, db1 = _ffn_bwd_act(dob1, a1, b1, w["d1"], name="ffn1_bwd_act",
                            tie=jnp.zeros((8, LANES), F32) if tie is None else tie)
    gw["g1t"] = weight_grad(da1, h1, "ffn1_dgate")
    gw["u1t"] = weight_grad(db1, h1, "ffn1_dup", tie=sent("g1t"))
    tie = sent("u1t")
    dx, gv["ffn1"] = _ffn_bwd_in(dx1, x, norms["ffn1"] if tie is None else norms["ffn1"] + tie[0, 0], da1, db1,
                                 w["g1t"], w["u1t"], name="ffn1_bwd_in")
    gv["lb"] = d_lb
    return loss_row, dx, gw, gv


HBM = pl.BlockSpec(memory_space=pl.ANY)


def _place():
    return lax.axis_index("x"), lax.axis_index("y"), lax.axis_index("c")


def _slot(px, py, pc):
    return 4 * px + 2 * py + pc


GATHER_COPIES = 8


def _all_gather(blocks, *, name):
    n = len(blocks)

    def body(*refs):
        ins, outs = refs[:n], refs[n:2 * n]
        send_sems, recv_sems, local_sems = refs[2 * n:]
        x, y, c = _place()
        me, sibling = (x, y, c), (x, y, 1 - c)
        beside, across, diagonal = (1 - x, y, c), (x, 1 - y, c), (1 - x, 1 - y, c)

        def copy(a, k, block, to, src=None, half=None):
            dst = outs[a].at[_slot(*block)]
            if half is not None:
                rows = blocks[a].shape[0] // 2
                dst = dst.at[pl.ds(half * rows, rows)]
            return pltpu.make_async_remote_copy(
                src_ref=dst if src is None else src, dst_ref=dst, send_sem=send_sems.at[GATHER_COPIES * a + k],
                recv_sem=recv_sems.at[GATHER_COPIES * a + k], device_id=to, device_id_type=MESH)

        mine = [pltpu.make_async_copy(ins[a], outs[a].at[_slot(*me)], local_sems.at[a]) for a in range(n)]
        for cp in mine:
            cp.start()
        sent = []
        for a in range(n):
            sent += [copy(a, 0, me, sibling, src=ins[a]), copy(a, 1, me, beside, src=ins[a]),
                     copy(a, 2, me, across, src=ins[a])]
        for cp in sent:
            cp.start()
        for a in range(n):
            copy(a, 1, beside, me).wait_recv()
            sent += [copy(a, 3, beside, across, half=0), copy(a, 5, beside, sibling)]
            sent[-2].start()
            sent[-1].start()
        for a in range(n):
            copy(a, 2, across, me).wait_recv()
            sent += [copy(a, 4, across, beside, half=1), copy(a, 6, across, sibling)]
            sent[-2].start()
            sent[-1].start()
        for a in range(n):
            copy(a, 3, diagonal, me, half=0).wait_recv()
            copy(a, 4, diagonal, me, half=1).wait_recv()
            sent.append(copy(a, 7, diagonal, sibling))
            sent[-1].start()
        for a in range(n):
            for k, origin in ((0, sibling), (5, (1 - x, y, 1 - c)), (6, (x, 1 - y, 1 - c)), (7, (1 - x, 1 - y, 1 - c))):
                copy(a, k, origin, me).wait_recv()
        for cp in sent:
            cp.wait_send()
        for cp in mine:
            cp.wait()

    return pl.pallas_call(
        body,
        name=name,
        in_specs=[HBM] * n,
        out_specs=[HBM] * n,
        out_shape=[jax.ShapeDtypeStruct((N_DEV,) + b.shape, b.dtype) for b in blocks],
        scratch_shapes=[pltpu.SemaphoreType.DMA((GATHER_COPIES * n,)), pltpu.SemaphoreType.DMA((GATHER_COPIES * n,)),
                        pltpu.SemaphoreType.DMA((n,))],
    )(*blocks)


def _flipped(place, d):
    return tuple(1 - p if (d >> (2 - axis)) & 1 else p for axis, p in enumerate(place))


SEM = pl.BlockSpec(memory_space=pltpu.SEMAPHORE)
EFFECT = pltpu.SideEffectType.DATAFLOW_SIDE_EFFECTING


def _split_copies(me, srcs, lands, send_sems, recv_sems, by_owner):
    copies = []
    for d in range(1, N_DEV):
        peer = _flipped(me, d)
        for a, (src, land) in enumerate(zip(srcs, lands)):
            copies.append(pltpu.make_async_remote_copy(
                src_ref=src.at[_slot(*peer)] if by_owner else src, dst_ref=land.at[_slot(*me)],
                send_sem=send_sems.at[7 * a + d - 1], recv_sem=recv_sems.at[7 * a + d - 1], device_id=peer,
                device_id_type=MESH))
    own = [pltpu.make_async_copy(src.at[_slot(*me)] if by_owner else src, land.at[_slot(*me)],
                                 recv_sems.at[7 * len(srcs) + a]) for a, (src, land) in enumerate(zip(srcs, lands))]
    return copies, own


def _copies_start(srcs, *, name, by_owner, after=None):
    n = len(srcs)
    extra = [] if after is None else [after]
    land_shapes = [s.shape if by_owner else (N_DEV,) + s.shape for s in srcs]
    lands = [pltpu.with_memory_space_constraint(lax.empty(shape, s.dtype), pltpu.HBM) for shape, s in zip(land_shapes, srcs)]
    srcs = [pltpu.with_memory_space_constraint(s, pltpu.HBM) for s in srcs]

    def body(*refs):
        src_refs, land_refs = refs[:n], refs[n:2 * n]
        send_sems, recv_sems = refs[2 * n + len(extra)], refs[2 * n + len(extra) + 1]
        token = refs[-1]
        copies, own = _split_copies(_place(), src_refs, land_refs, send_sems, recv_sems, by_owner)
        for cp in copies + own:
            cp.start()
        token[...] = jnp.zeros_like(token)

    out = pl.pallas_call(
        body,
        name=name,
        in_specs=[HBM] * (2 * n + len(extra)),
        out_specs=[SEM, SEM] + [HBM] * (2 * n) + [pl.BlockSpec(memory_space=pltpu.VMEM)],
        out_shape=[pltpu.SemaphoreType.DMA((7 * n,)), pltpu.SemaphoreType.DMA((8 * n,))]
        + [pltpu.HBM(s.shape, s.dtype) for s in srcs] + [pltpu.HBM(shape, s.dtype) for shape, s in zip(land_shapes, srcs)]
        + [jax.ShapeDtypeStruct((8, LANES), F32)],
        input_output_aliases={i: 2 + i for i in range(2 * n)},
        compiler_params=pltpu.CompilerParams(has_side_effects=EFFECT),
    )(*srcs, *lands, *extra)
    return (out[0], out[1], out[2:2 + n], out[2 + n:2 + 2 * n]), out[-1]


def _copies_wait(started, after, *, name, by_owner):
    send_sems, recv_sems, srcs, lands = started
    n = len(srcs)

    def body(*refs):
        src_refs, land_refs = refs[:n], refs[n:2 * n]
        copies, own = _split_copies(_place(), src_refs, land_refs, refs[2 * n], refs[2 * n + 1], by_owner)
        for cp in copies:
            cp.wait_send()
            cp.wait_recv()
        for cp in own:
            cp.wait()

    out = pl.pallas_call(
        body,
        name=name,
        in_specs=[HBM] * (2 * n) + [SEM, SEM, HBM],
        out_specs=[HBM] * (2 * n),
        out_shape=[pltpu.HBM(s.shape, s.dtype) for s in srcs] + [pltpu.HBM(s.shape, s.dtype) for s in lands],
        input_output_aliases={i: i for i in range(2 * n)},
        compiler_params=pltpu.CompilerParams(has_side_effects=EFFECT),
    )(*srcs, *lands, send_sems, recv_sems, after)
    return out[:n], out[n:]


def _adamw(w, g, m, v):
    m = ADAM_B1 * m + (1.0 - ADAM_B1) * g
    v = ADAM_B2 * v + (1.0 - ADAM_B2) * (g * g)
    m_hat = m / (1.0 - ADAM_B1 ** ADAM_STEP)
    v_hat = v / (1.0 - ADAM_B2 ** ADAM_STEP)
    delta = -ADAM_LR * (m_hat / (jnp.sqrt(v_hat) + ADAM_EPS) + ADAM_WD * w)
    return delta, m, v


def _sum_and_update(parts, w, m, v, *, name, tie=None):
    _, rows, cols = w.shape
    tr = rows // 2

    def body(p_ref, w_ref, m_ref, v_ref, *rest):
        g_ref, d_ref, mo_ref, vo_ref = rest[-4:]
        g = p_ref[0].astype(F32)
        for s in range(1, N_DEV):
            g = g + p_ref[s].astype(F32)
        g_ref[0] = g
        d_ref[0], mo_ref[0], vo_ref[0] = _adamw(w_ref[0], g, m_ref[0], v_ref[0])

    flat = pl.BlockSpec((1, tr, cols), lambda i: (0, i, 0))
    return pl.pallas_call(
        body,
        name=name,
        grid=(rows // tr,),
        in_specs=[pl.BlockSpec((N_DEV, tr, cols), lambda i: (0, i, 0)), flat, flat, flat]
        + ([] if tie is None else [pl.BlockSpec(memory_space=pl.ANY)]),
        out_specs=[flat] * 4,
        out_shape=[jax.ShapeDtypeStruct((1, rows, cols), F32)] * 4,
        compiler_params=_params("parallel"),
    )(parts, w, m, v, *([] if tie is None else [tie]))


VEC_ROWS = 8
ROW_LOGITS, ROW_LOSS = 5, 7


def _vectors_update(part, w, m, v, *, name, tie):
    def body(p_ref, w_ref, m_ref, v_ref, tie_ref, g_ref, d_ref, mo_ref, vo_ref, loss_ref, all_ref, send_sems, recv_sems):
        me = _place()
        all_ref[_slot(*me)] = p_ref[...]
        copies = []
        for d in range(1, N_DEV):
            peer = _flipped(me, d)
            copies.append(pltpu.make_async_remote_copy(
                src_ref=p_ref, dst_ref=all_ref.at[_slot(*me)], send_sem=send_sems.at[d - 1], recv_sem=recv_sems.at[d - 1],
                device_id=peer, device_id_type=MESH))
        for cp in copies:
            cp.start()
        for cp in copies:
            cp.wait()
        total = all_ref[0]
        for s in range(1, N_DEV):
            total = total + all_ref[s]
        wv = w_ref[...]
        half = D_MODEL // 2
        lb = _sigmoid(wv[ROW_LOGITS:ROW_LOGITS + 1, :half] - wv[ROW_LOGITS:ROW_LOGITS + 1, half:])
        d_first = total[ROW_LOGITS:ROW_LOGITS + 1, :half] * lb * (1.0 - lb)
        d_logits = jnp.concatenate([d_first, -d_first], axis=1)
        rowi = lax.broadcasted_iota(jnp.int32, (VEC_ROWS, D_MODEL), 0)
        g = jnp.where(rowi == ROW_LOGITS, d_logits, jnp.where(rowi < ROW_LOGITS, total, 0.0))
        g_ref[...] = g
        d_ref[...], mo_ref[...], vo_ref[...] = _adamw(wv, g, m_ref[...], v_ref[...])
        loss_ref[...] = total[ROW_LOSS:ROW_LOSS + 1, :]

    vmem = pl.BlockSpec(memory_space=pltpu.VMEM)
    return pl.pallas_call(
        body,
        name=name,
        in_specs=[vmem] * 4 + [HBM],
        out_specs=[vmem] * 5,
        out_shape=[jax.ShapeDtypeStruct((VEC_ROWS, D_MODEL), F32)] * 4 + [jax.ShapeDtypeStruct((1, D_MODEL), F32)],
        scratch_shapes=[pltpu.VMEM((N_DEV, VEC_ROWS, D_MODEL), F32), pltpu.SemaphoreType.DMA((7,)),
                        pltpu.SemaphoreType.DMA((7,))],
    )(part, w, m, v, tie)


TRANSPOSED = ("g1t", "u1t", "g2t", "u2t", "int")


def _vector_rows(rows):
    rowi = lax.broadcasted_iota(jnp.int32, (VEC_ROWS, D_MODEL), 0)
    out = jnp.zeros((VEC_ROWS, D_MODEL), F32)
    for i, r in enumerate(rows):
        if r is not None:
            out = jnp.where(rowi == i, r, out)
    return out


def kernel(x, ffn1_norm, ffn1_w_gate, ffn1_w_up, ffn1_w_down, mix_norm, w_in, sb_out_norm, hg_lower_bound_logits, hg_out_norm, w_out, ffn2_norm, ffn2_w_gate, ffn2_w_up, ffn2_w_down, final_norm, loss_target, m_ffn1_norm, m_ffn1_w_gate, m_ffn1_w_up, m_ffn1_w_down, m_mix_norm, m_w_in, m_sb_out_norm, m_hg_lower_bound_logits, m_hg_out_norm, m_w_out, m_ffn2_norm, m_ffn2_w_gate, m_ffn2_w_up, m_ffn2_w_down, m_final_norm, v_ffn1_norm, v_ffn1_w_gate, v_ffn1_w_up, v_ffn1_w_down, v_mix_norm, v_w_in, v_sb_out_norm, v_hg_lower_bound_logits, v_hg_out_norm, v_w_out, v_ffn2_norm, v_ffn2_w_gate, v_ffn2_w_up, v_ffn2_w_down, v_final_norm):
    def matrices(g1, u1, d1, win, wout, g2, u2, d2):
        return {"g1t": g1, "u1t": u1, "d1": d1, "int": win, "out": wout, "g2t": g2, "u2t": u2, "d2": d2}

    def vectors(n1, nm, nsb, lg, nhg, n2, nf):
        return [n1, nm, n2, nf.reshape(1, D_MODEL), jnp.concatenate([nsb, nhg], axis=1), lg.reshape(1, D_MODEL), None, None]

    w_sh = matrices(ffn1_w_gate, ffn1_w_up, ffn1_w_down, w_in, w_out, ffn2_w_gate, ffn2_w_up, ffn2_w_down)
    m_sh = matrices(m_ffn1_w_gate, m_ffn1_w_up, m_ffn1_w_down, m_w_in, m_w_out, m_ffn2_w_gate, m_ffn2_w_up, m_ffn2_w_down)
    v_sh = matrices(v_ffn1_w_gate, v_ffn1_w_up, v_ffn1_w_down, v_w_in, v_w_out, v_ffn2_w_gate, v_ffn2_w_up, v_ffn2_w_down)
    keys = list(w_sh)

    def full(key, stack):
        return stack.reshape(-1, D_MODEL)

    def by_owner(key, grad):
        return grad.reshape(N_DEV, -1, D_MODEL)

    def view(key, a):
        return jnp.swapaxes(a, 1, 2) if key in TRANSPOSED else a

    blocks = {k: view(k, w_sh[k])[0].astype(BF16) for k in keys}
    first, mid, last = ("g1t", "u1t", "d1"), ("int", "out"), ("g2t", "u2t", "d2")
    w_first = {k: full(k, s) for k, s in zip(first, _all_gather([blocks[k] for k in first], name="gather_ffn1"))}
    flights = {}
    flights["ffn1"], token_mid = _copies_start([blocks[k] for k in mid], name="gather_mid_start", by_owner=False,
                                               after=w_first["d1"])
    flights["mix"], token_last = _copies_start([blocks[k] for k in last], name="gather_ffn2_start", by_owner=False,
                                               after=token_mid)

    def weights_after(stage, result):
        group = mid if stage == "ffn1" else last
        _, lands = _copies_wait(flights[stage], result, name="gather_" + stage + "_wait", by_owner=False)
        return {k: full(k, s) for k, s in zip(group, lands)}

    groups = {"mix": ("g2t", "u2t", "d2", "out"), "in": ("int",), "g1t": ("g1t",), "u1t": ("u1t",), "d1": ("d1",)}
    sent = {}

    def grads_ready(stage, gw):
        stacks = [by_owner(k, gw[k]) for k in groups[stage]]
        sent[stage], token = _copies_start(stacks, name="grads_" + stage + "_start", by_owner=True)
        return token

    norms = {"ffn1": ffn1_norm + token_last[0, 0], "mix": mix_norm, "sb": sb_out_norm, "hg": hg_out_norm,
             "ffn2": ffn2_norm, "final": final_norm.reshape(1, D_MODEL)}
    loss_row, grad_x, gw, gv = _local_step(x[0], loss_target[0], norms, hg_lower_bound_logits, w_first, weights_after,
                                           grads_ready)

    lb_row = jnp.concatenate([gv["lb"], jnp.zeros_like(gv["lb"])], axis=1)
    part = _vector_rows([gv["ffn1"], gv["mix"], gv["ffn2"], gv["final"], jnp.concatenate([gv["sb"], gv["hg"]], axis=1),
                         lb_row, None, loss_row])
    vec_w = _vector_rows(vectors(ffn1_norm, mix_norm, sb_out_norm, hg_lower_bound_logits, hg_out_norm, ffn2_norm, final_norm))
    vec_m = _vector_rows(vectors(m_ffn1_norm, m_mix_norm, m_sb_out_norm, m_hg_lower_bound_logits, m_hg_out_norm,
                                 m_ffn2_norm, m_final_norm))
    vec_v = _vector_rows(vectors(v_ffn1_norm, v_mix_norm, v_sb_out_norm, v_hg_lower_bound_logits, v_hg_out_norm,
                                 v_ffn2_norm, v_final_norm))
    updated, after = {}, grad_x
    for stage, flight in sent.items():
        if stage == list(sent)[-1]:
            *vecs, loss_out = _vectors_update(part, vec_w, vec_m, vec_v, name="vectors_update", tie=after)
            after = loss_out
        _, lands = _copies_wait(flight, after, name="grads_" + stage + "_wait", by_owner=True)
        for k, part_k in zip(groups[stage], lands):
            updated[k] = _sum_and_update(part_k, view(k, w_sh[k]), view(k, m_sh[k]), view(k, v_sh[k]), name="adamw_" + k,
                                         tie=after)
            after = updated[k][0]
    mats = [{k: view(k, updated[k][i]) for k in keys} for i in range(4)]

    def leaves(mat, vec):
        half = D_MODEL // 2
        return (
            vec[0:1], mat["g1t"], mat["u1t"], mat["d1"], vec[1:2], mat["int"], vec[4:5, :half],
            vec[ROW_LOGITS].reshape(2, half), vec[4:5, half:], mat["out"], vec[2:3], mat["g2t"], mat["u2t"],
            mat["d2"], vec[3],
        )

    out = [loss_out[0, 0], grad_x[None]]
    for mat, vec in zip(mats, vecs):
        out.extend(leaves(mat, vec))
    return tuple(out)
```

```python
import jax
import jax.numpy as jnp
from jax import lax
from jax.experimental import pallas as pl
from jax.experimental.pallas import tpu as pltpu

F32, BF16 = jnp.float32, jnp.bfloat16
D_MODEL = 1024
D_FF = 2816
SB_WIDTH = 512
HG_WIDTH = 512
SB_HEAD_DIM = 64
HG_HEAD_DIM = 128
IN_COLS = 3584
EPS = 1e-6
N_DEV = 8
LANES = 128
HG_CHUNK = 16
VMEM_LIMIT_BYTES = 48 * 1024 * 1024
FFN_BWD_VMEM_LIMIT_BYTES = 56 * 1024 * 1024
ADAM_LR, ADAM_B1, ADAM_B2, ADAM_EPS, ADAM_WD, ADAM_STEP = 0.001, 0.9, 0.999, 1e-08, 0.01, 10
MESH = pl.DeviceIdType.MESH


def _params(*semantics, vmem_limit_bytes=VMEM_LIMIT_BYTES):
    return pltpu.CompilerParams(dimension_semantics=semantics, vmem_limit_bytes=vmem_limit_bytes)


def _dot(a, b):
    return jnp.dot(a, b, preferred_element_type=F32)


def _dot_nt(a, b):
    return lax.dot_general(a, b, (((1,), (1,)), ((), ())), preferred_element_type=F32)


def _dot_tn(a, b):
    return lax.dot_general(a, b, (((0,), (0,)), ((), ())), preferred_element_type=F32)


def _split3(x):
    hi = x.astype(BF16)
    r1 = x - hi.astype(F32)
    mid = r1.astype(BF16)
    lo = (r1 - mid.astype(F32)).astype(BF16)
    return hi, mid, lo


def _rms(xv):
    rstd = lax.rsqrt(jnp.mean(xv * xv, axis=-1, keepdims=True) + EPS)
    return xv * rstd, rstd


def _sigmoid(x):
    return 0.5 + 0.5 * jnp.tanh(0.5 * x)


def _loss_terms(xv, gain, target):
    xhat, rstd = _rms(xv)
    err = xhat * gain - target
    loss = 0.5 * jnp.sum(jnp.mean(err * err, axis=-1, keepdims=True), axis=0, keepdims=True)
    dy = err * (1.0 / xv.shape[-1])
    dxh = dy * gain
    dx = rstd * (dxh - xhat * jnp.mean(dxh * xhat, axis=-1, keepdims=True))
    return dx, jnp.sum(dy * xhat, axis=0, keepdims=True), loss


def _mm(a, b, *, name, tm, tn, nt=False, ta=False, out_dtype=F32, tie=None):
    k, m = a.shape if ta else a.shape[::-1]
    n = b.shape[0] if nt else b.shape[1]
    assert m % tm == 0 and n % tn == 0 and not (nt and ta), (name, a.shape, b.shape, tm, tn)

    def body(a_ref, b_ref, *rest):
        av = a_ref[...].astype(BF16)
        bv = b_ref[...].astype(BF16)
        rest[-1][...] = (_dot_nt(av, bv) if nt else _dot_tn(av, bv) if ta else _dot(av, bv)).astype(out_dtype)

    in_specs = [
        pl.BlockSpec((k, tm), lambda i, j: (0, i)) if ta else pl.BlockSpec((tm, k), lambda i, j: (i, 0)),
        pl.BlockSpec((tn, k), lambda i, j: (j, 0)) if nt else pl.BlockSpec((k, tn), lambda i, j: (0, j)),
    ]
    operands = [a, b]
    if tie is not None:
        in_specs.append(pl.BlockSpec(memory_space=pl.ANY))
        operands.append(tie)
    return pl.pallas_call(
        body,
        name=name,
        grid=(m // tm, n // tn),
        in_specs=in_specs,
        out_specs=pl.BlockSpec((tm, tn), lambda i, j: (i, j)),
        out_shape=jax.ShapeDtypeStruct((m, n), out_dtype),
        compiler_params=_params("parallel", "parallel"),
    )(*operands)


def _ffn_fwd(x, gain, wgt, wut, wd, *, name, next_gain=None, head=None, tm=1024, tf=256):
    t = x.shape[0]
    nj = D_FF // tf
    extra_in = [] if next_gain is None else [next_gain]
    extra_in += [] if head is None else list(head)

    def body(x_ref, g_ref, wg_ref, wu_ref, wd_prev_ref, wd_last_ref, *rest):
        extra, (xo_ref, a_ref, b_ref, h_ref, st_ref) = rest[:len(extra_in)], rest[len(extra_in):len(extra_in) + 5]
        tail_out, (acc, s_prev) = rest[len(extra_in) + 5:-2], rest[-2:]
        i = pl.program_id(0)
        j = pl.program_id(1)

        @pl.when(j == 0)
        def _():
            xhat, _ = _rms(x_ref[...])
            h_ref[...] = (xhat * g_ref[...]).astype(BF16)
            acc[...] = jnp.zeros_like(acc)
            s_prev[...] = jnp.zeros_like(s_prev)

        acc[...] += _dot(s_prev[...], wd_prev_ref[...])
        h = h_ref[...]
        a = _dot_nt(h, wg_ref[...])
        b = _dot_nt(h, wu_ref[...])
        a_ref[...] = a.astype(BF16)
        b_ref[...] = b.astype(BF16)
        s = (a * _sigmoid(a) * b).astype(BF16)
        st_ref[...] = s
        s_prev[...] = s

        @pl.when(j == nj - 1)
        def _():
            xo = x_ref[...] + 0.5 * (acc[...] + _dot(s, wd_last_ref[...]))
            if head is None:
                xo_ref[...] = xo
            if next_gain is not None:
                tail_out[0][...] = (_rms(xo)[0] * extra[0][...]).astype(BF16)
            if head is not None:
                gain_ref, target_ref = extra[-2:]
                dg_ref, loss_ref = tail_out[-2:]
                xo_ref[...], part_g, part_loss = _loss_terms(xo, gain_ref[...], target_ref[...])

                @pl.when(i == 0)
                def _():
                    dg_ref[...] = part_g
                    loss_ref[...] = jnp.broadcast_to(part_loss, loss_ref.shape)

                @pl.when(i > 0)
                def _():
                    dg_ref[...] += part_g
                    loss_ref[...] += jnp.broadcast_to(part_loss, loss_ref.shape)

    row = pl.BlockSpec((tm, D_MODEL), lambda i, j: (i, 0))
    vec = pl.BlockSpec((1, D_MODEL), lambda i, j: (0, 0))
    tile = pl.BlockSpec((tm, tf), lambda i, j: (i, j))
    weights = pl.BlockSpec((tf, D_MODEL), lambda i, j: (j, 0))
    tail_specs = ([] if next_gain is None else [row]) + ([] if head is None else [vec, vec])
    tail_shapes = ([] if next_gain is None else [jax.ShapeDtypeStruct((t, D_MODEL), BF16)]) + (
        [] if head is None else [jax.ShapeDtypeStruct((1, D_MODEL), F32)] * 2)
    return pl.pallas_call(
        body,
        name=name,
        grid=(t // tm, nj),
        in_specs=[
            row, vec, weights, weights,
            pl.BlockSpec((tf, D_MODEL), lambda i, j: (jnp.maximum(j - 1, 0), 0)),
            pl.BlockSpec((tf, D_MODEL), lambda i, j: (nj - 1, 0)),
        ] + ([] if next_gain is None else [vec]) + ([] if head is None else [vec, row]),
        out_specs=[row, tile, tile, row, tile] + tail_specs,
        out_shape=[
            jax.ShapeDtypeStruct((t, D_MODEL), F32),
            jax.ShapeDtypeStruct((t, D_FF), BF16),
            jax.ShapeDtypeStruct((t, D_FF), BF16),
            jax.ShapeDtypeStruct((t, D_MODEL), BF16),
            jax.ShapeDtypeStruct((t, D_FF), BF16),
        ] + tail_shapes,
        scratch_shapes=[pltpu.VMEM((tm, D_MODEL), F32), pltpu.VMEM((tm, tf), BF16)],
        compiler_params=_params("arbitrary", "arbitrary"),
    )(x, gain, wgt, wut, wd, wd, *extra_in)


def _ffn_bwd(dout, x, gain, a, b, wgt, wut, wd, *, name, tm=1024, tf=256):
    t = x.shape[0]
    nj = D_FF // tf

    def body(do_ref, x_ref, g_ref, a_ref, b_ref, wg_prev_ref, wu_prev_ref, wg_last_ref, wu_last_ref, wd_ref,
             dx_ref, dg_ref, da_ref, db_ref, dob_ref, dob_scr, dh, da_prev, db_prev):
        i = pl.program_id(0)
        j = pl.program_id(1)

        @pl.when(j == 0)
        def _():
            d = (0.5 * do_ref[...]).astype(BF16)
            dob_scr[...] = d
            dob_ref[...] = d
            dh[...] = jnp.zeros_like(dh)
            da_prev[...] = jnp.zeros_like(da_prev)
            db_prev[...] = jnp.zeros_like(db_prev)

        dh[...] += _dot(da_prev[...], wg_prev_ref[...]) + _dot(db_prev[...], wu_prev_ref[...])
        ds = _dot_nt(dob_scr[...], wd_ref[...])
        av = a_ref[...].astype(F32)
        bv = b_ref[...].astype(F32)
        sig = _sigmoid(av)
        dbv = (ds * (av * sig)).astype(BF16)
        dav = (ds * bv * (sig * (1.0 + av * (1.0 - sig)))).astype(BF16)
        da_ref[...] = dav
        db_ref[...] = dbv
        da_prev[...] = dav
        db_prev[...] = dbv

        @pl.when(j == nj - 1)
        def _():
            xhat, rstd = _rms(x_ref[...])
            dhv = dh[...] + _dot(dav, wg_last_ref[...]) + _dot(dbv, wu_last_ref[...])
            part = jnp.sum(dhv * xhat, axis=0, keepdims=True)

            @pl.when(i == 0)
            def _():
                dg_ref[...] = part

            @pl.when(i > 0)
            def _():
                dg_ref[...] += part

            dxh = dhv * g_ref[...]
            dx_ref[...] = do_ref[...] + rstd * (dxh - xhat * jnp.mean(dxh * xhat, axis=-1, keepdims=True))

    return pl.pallas_call(
        body,
        name=name,
        grid=(t // tm, nj),
        in_specs=[
            pl.BlockSpec((tm, D_MODEL), lambda i, j: (i, 0)),
            pl.BlockSpec((tm, D_MODEL), lambda i, j: (i, 0)),
            pl.BlockSpec((1, D_MODEL), lambda i, j: (0, 0)),
            pl.BlockSpec((tm, tf), lambda i, j: (i, j)),
            pl.BlockSpec((tm, tf), lambda i, j: (i, j)),
            pl.BlockSpec((tf, D_MODEL), lambda i, j: (jnp.maximum(j - 1, 0), 0)),
            pl.BlockSpec((tf, D_MODEL), lambda i, j: (jnp.maximum(j - 1, 0), 0)),
            pl.BlockSpec((tf, D_MODEL), lambda i, j: (nj - 1, 0)),
            pl.BlockSpec((tf, D_MODEL), lambda i, j: (nj - 1, 0)),
            pl.BlockSpec((tf, D_MODEL), lambda i, j: (j, 0)),
        ],
        out_specs=[
            pl.BlockSpec((tm, D_MODEL), lambda i, j: (i, 0)),
            pl.BlockSpec((1, D_MODEL), lambda i, j: (0, 0)),
            pl.BlockSpec((tm, tf), lambda i, j: (i, j)),
            pl.BlockSpec((tm, tf), lambda i, j: (i, j)),
            pl.BlockSpec((tm, D_MODEL), lambda i, j: (i, 0)),
        ],
        out_shape=[
            jax.ShapeDtypeStruct((t, D_MODEL), F32),
            jax.ShapeDtypeStruct((1, D_MODEL), F32),
            jax.ShapeDtypeStruct((t, D_FF), BF16),
            jax.ShapeDtypeStruct((t, D_FF), BF16),
            jax.ShapeDtypeStruct((t, D_MODEL), BF16),
        ],
        scratch_shapes=[pltpu.VMEM((tm, D_MODEL), BF16), pltpu.VMEM((tm, D_MODEL), F32), pltpu.VMEM((tm, tf), BF16),
                        pltpu.VMEM((tm, tf), BF16)],
        compiler_params=_params("arbitrary", "arbitrary", vmem_limit_bytes=FFN_BWD_VMEM_LIMIT_BYTES),
    )(dout, x, gain, a, b, wgt, wut, wgt, wut, wd)


def _ffn_bwd_act(dout_half, a, b, wd, *, name, tie, tm=1024, tf=256):
    t = a.shape[0]

    def body(dob_ref, a_ref, b_ref, wd_ref, tie_ref, da_ref, db_ref):
        ds = _dot_nt(dob_ref[...], wd_ref[...])
        av = a_ref[...].astype(F32)
        bv = b_ref[...].astype(F32)
        sig = _sigmoid(av)
        db_ref[...] = (ds * (av * sig)).astype(BF16)
        da_ref[...] = (ds * bv * (sig * (1.0 + av * (1.0 - sig)))).astype(BF16)

    tile = pl.BlockSpec((tm, tf), lambda i, j: (i, j))
    return pl.pallas_call(
        body,
        name=name,
        grid=(t // tm, D_FF // tf),
        in_specs=[pl.BlockSpec((tm, D_MODEL), lambda i, j: (i, 0)), tile, tile,
                  pl.BlockSpec((tf, D_MODEL), lambda i, j: (j, 0)), pl.BlockSpec(memory_space=pl.ANY)],
        out_specs=[tile, tile],
        out_shape=[jax.ShapeDtypeStruct((t, D_FF), BF16)] * 2,
        compiler_params=_params("parallel", "parallel"),
    )(dout_half, a, b, wd, tie)


def _ffn_bwd_in(dout, x, gain, da, db, wgt, wut, *, name, tm=512):
    t = x.shape[0]

    def body(do_ref, x_ref, g_ref, da_ref, db_ref, wg_ref, wu_ref, dx_ref, dg_ref):
        i = pl.program_id(0)
        dhv = _dot(da_ref[...], wg_ref[...]) + _dot(db_ref[...], wu_ref[...])
        xhat, rstd = _rms(x_ref[...])
        part = jnp.sum(dhv * xhat, axis=0, keepdims=True)

        @pl.when(i == 0)
        def _():
            dg_ref[...] = part

        @pl.when(i > 0)
        def _():
            dg_ref[...] += part

        dxh = dhv * g_ref[...]
        dx_ref[...] = do_ref[...] + rstd * (dxh - xhat * jnp.mean(dxh * xhat, axis=-1, keepdims=True))

    row = pl.BlockSpec((tm, D_MODEL), lambda i: (i, 0))
    vec = pl.BlockSpec((1, D_MODEL), lambda i: (0, 0))
    tile = pl.BlockSpec((tm, D_FF), lambda i: (i, 0))
    weights = pl.BlockSpec((D_FF, D_MODEL), lambda i: (0, 0), pipeline_mode=pl.Buffered(1))
    return pl.pallas_call(
        body,
        name=name,
        grid=(t // tm,),
        in_specs=[row, row, vec, tile, tile, weights, weights],
        out_specs=[row, vec],
        out_shape=[jax.ShapeDtypeStruct((t, D_MODEL), F32), jax.ShapeDtypeStruct((1, D_MODEL), F32)],
        compiler_params=_params("arbitrary", vmem_limit_bytes=FFN_BWD_VMEM_LIMIT_BYTES),
    )(dout, x, gain, da, db, wgt, wut)


def _in_proj_bwd(dproj, w_int, x, gain, dres, *, name, tm=512):
    t, k = dproj.shape

    def body(dp_ref, w_ref, x_ref, g_ref, dr_ref, dx_ref, dg_ref, dxb_ref):
        i = pl.program_id(0)
        dhv = _dot(dp_ref[...], w_ref[...])
        xhat, rstd = _rms(x_ref[...])
        part = jnp.sum(dhv * xhat, axis=0, keepdims=True)

        @pl.when(i == 0)
        def _():
            dg_ref[...] = part

        @pl.when(i > 0)
        def _():
            dg_ref[...] += part

        dxh = dhv * g_ref[...]
        dx = dr_ref[...] + rstd * (dxh - xhat * jnp.mean(dxh * xhat, axis=-1, keepdims=True))
        dx_ref[...] = dx
        dxb_ref[...] = (0.5 * dx).astype(BF16)

    row = pl.BlockSpec((tm, D_MODEL), lambda i: (i, 0))
    vec = pl.BlockSpec((1, D_MODEL), lambda i: (0, 0))
    return pl.pallas_call(
        body,
        name=name,
        grid=(t // tm,),
        in_specs=[pl.BlockSpec((tm, k), lambda i: (i, 0)), pl.BlockSpec((k, D_MODEL), lambda i: (0, 0)), row, vec, row],
        out_specs=[row, vec, row],
        out_shape=[jax.ShapeDtypeStruct((t, D_MODEL), F32), jax.ShapeDtypeStruct((1, D_MODEL), F32),
                   jax.ShapeDtypeStruct((t, D_MODEL), BF16)],
        compiler_params=_params("arbitrary"),
    )(dproj, w_int, x, gain, dres)


ATT_Q_TILE = 512
ATT_K_BLOCK = 256


def _first_head_lanes():
    return lax.broadcasted_iota(jnp.int32, (1, LANES), 1) < SB_HEAD_DIM


def _stack_heads(x):
    first = _first_head_lanes()
    return jnp.concatenate([jnp.where(first, x, 0.0), jnp.where(first, 0.0, x)], axis=0)


def _unstack_heads(x, rows):
    return jnp.where(_first_head_lanes(), x[:rows], x[rows:])


def _rows_from(x, first, rows):
    return x if first == 0 else jnp.concatenate([x[first:rows], x[rows + first:]], axis=0)


def _rows_into(full, part, first, rows):
    if first == 0:
        return part
    n = rows - first
    return jnp.concatenate([full[:first], part[:n], full[rows:rows + first], part[n:]], axis=0)


def _tri(n, relation):
    r = lax.broadcasted_iota(jnp.int32, (n, n), 0)
    c = lax.broadcasted_iota(jnp.int32, (n, n), 1)
    return relation(r, c).astype(BF16)


def _scan_dot(x, tri):
    hi = x.astype(BF16)
    lo = (x - hi.astype(F32)).astype(BF16)
    return _dot(jnp.concatenate([hi, lo], axis=1), jnp.concatenate([tri, tri], axis=0))


def _log_terms(z):
    lbeta = jnp.minimum(z, 0.0) - jnp.log(1.0 + jnp.exp(-jnp.abs(z)))
    return lbeta, lbeta - z


def _attn_fwd(proj, *, name):
    t = proj.shape[0]
    tq, tk = ATT_Q_TILE, ATT_K_BLOCK
    diag = tq // tk
    n_pairs = SB_WIDTH // LANES

    def body(q_ref, k_ref, v_ref, o_ref, kept_ref):
        qi = pl.program_id(1)
        q = q_ref[...] * (SB_HEAD_DIM ** -0.5)
        qs = _stack_heads(q).astype(BF16)
        tri = _tri(tk, lambda j, s: j > s)
        trow = lax.broadcasted_iota(jnp.int32, (tq, tk), 0)
        scol = lax.broadcasted_iota(jnp.int32, (tq, tk), 1)

        def block(kb, carry, causal, first=0):
            acc, c = carry
            off = pl.multiple_of(kb * tk, tk)
            lbeta, lrest = _log_terms(_dot_nt(_rows_from(qs, first, tq), k_ref[pl.ds(off, tk), :].astype(BF16)))
            if causal is not None:
                lrest = jnp.where(causal, lrest, 0.0)
            w = jnp.exp(lbeta + (_scan_dot(lrest, tri) + _rows_from(c, first, tq)))
            if causal is not None:
                w = jnp.where(causal, w, 0.0)
            wb = w.astype(BF16)
            kept_ref[0, 0, kb] = _rows_into(jnp.zeros((2 * tq, tk), BF16), wb, first, tq)
            acc = _rows_into(acc, _rows_from(acc, first, tq) + _dot(wb, v_ref[pl.ds(off, tk), :].astype(BF16)), first, tq)
            return acc, _rows_into(c, _rows_from(c, first, tq) + jnp.sum(lrest, axis=1, keepdims=True), first, tq)

        carry = (jnp.zeros((2 * tq, LANES), F32), jnp.zeros((2 * tq, 1), F32))
        n_full = qi * diag
        for j in reversed(range(diag)):
            mask = ((scol + j * tk) < trow)[j * tk:]
            carry = block(n_full + j, carry, jnp.concatenate([mask, mask], axis=0), first=j * tk)

        def odd_tile(carry):
            for j in range(diag):
                carry = block(n_full - 1 - j, carry, None)
            return carry

        carry = lax.cond(qi % 2 == 1, odd_tile, lambda c: c, carry)
        last = n_full - 1 - (qi % 2) * diag

        def step(it, carry):
            for j in range(2 * diag):
                carry = block(last - (2 * diag * it + j), carry, None)
            return carry

        acc, _ = lax.fori_loop(0, qi // 2, step, carry)
        o_ref[...] = _unstack_heads(acc, tq)

    return pl.pallas_call(
        body,
        name=name,
        grid=(n_pairs, t // tq),
        in_specs=[
            pl.BlockSpec((tq, LANES), lambda p, i: (i, p)),
            pl.BlockSpec((t, LANES), lambda p, i: (0, n_pairs + p)),
            pl.BlockSpec((t, LANES), lambda p, i: (0, 2 * n_pairs + p)),
        ],
        out_specs=[pl.BlockSpec((tq, LANES), lambda p, i: (i, p)),
                   pl.BlockSpec((1, 1, t // tk, 2 * tq, tk), lambda p, i: (p, i, 0, 0, 0))],
        out_shape=[jax.ShapeDtypeStruct((t, SB_WIDTH), F32),
                   jax.ShapeDtypeStruct((n_pairs, t // tq, t // tk, 2 * tq, tk), BF16)],
        compiler_params=_params("parallel", "parallel"),
    )(proj, proj, proj)


def _attn_bwd(proj, kept, do, *, name, tie=None):
    t = proj.shape[0]
    tq, tk = ATT_Q_TILE, ATT_K_BLOCK
    diag = tq // tk
    n_pairs = SB_WIDTH // LANES
    scale = SB_HEAD_DIM ** -0.5

    def body(q_ref, k_ref, v_ref, kept_ref, do_ref, *rest):
        dq_ref, dk_ref, dv_ref = rest[-3:]
        qi = pl.program_id(1)

        @pl.when(qi == 0)
        def _():
            dk_ref[...] = jnp.zeros_like(dk_ref)
            dv_ref[...] = jnp.zeros_like(dv_ref)

        qs = _stack_heads(q_ref[...] * scale).astype(BF16)
        dos = _stack_heads(do_ref[...]).astype(BF16)
        before = _tri(tk, lambda s, j: s < j)
        trow = lax.broadcasted_iota(jnp.int32, (tq, tk), 0)
        scol = lax.broadcasted_iota(jnp.int32, (tq, tk), 1)

        def block(kb, carry, causal, first=0):
            dq, cg = carry
            off = pl.multiple_of(kb * tk, tk)
            q_rows, do_rows = _rows_from(qs, first, tq), _rows_from(dos, first, tq)
            wb = _rows_from(kept_ref[0, 0, kb], first, tq)
            kblk = k_ref[pl.ds(off, tk), :].astype(BF16)
            sig = _sigmoid(_dot_nt(q_rows, kblk))
            g = wb.astype(F32) * _dot_nt(do_rows, v_ref[pl.ds(off, tk), :].astype(BF16))
            prior = _scan_dot(g, before) + _rows_from(cg, first, tq)
            dz = g - sig * (g + prior)
            if causal is not None:
                dz = jnp.where(causal, dz, 0.0)
            dzb = dz.astype(BF16)
            dq = _rows_into(dq, _rows_from(dq, first, tq) + _dot(dzb, kblk), first, tq)
            dk_ref[pl.ds(off, tk), :] += _dot_tn(dzb, q_rows)
            dv_ref[pl.ds(off, tk), :] += _dot_tn(wb, do_rows)
            return dq, _rows_into(cg, _rows_from(cg, first, tq) + jnp.sum(g, axis=1, keepdims=True), first, tq)

        n_full = qi * diag

        def step(it, carry):
            for j in range(2 * diag):
                carry = block(2 * diag * it + j, carry, None)
            return carry

        def odd_tile(carry):
            for j in range(diag):
                carry = block(n_full - diag + j, carry, None)
            return carry

        carry = lax.fori_loop(0, qi // 2, step, (jnp.zeros((2 * tq, LANES), F32), jnp.zeros((2 * tq, 1), F32)))
        carry = lax.cond(qi % 2 == 1, odd_tile, lambda c: c, carry)
        for j in range(diag):
            mask = ((scol + j * tk) < trow)[j * tk:]
            carry = block(n_full + j, carry, jnp.concatenate([mask, mask], axis=0), first=j * tk)
        dq_ref[...] = (_unstack_heads(carry[0], tq) * scale).astype(BF16)

    tile_spec = pl.BlockSpec((tq, LANES), lambda p, i: (i, p))
    full_spec = pl.BlockSpec((t, LANES), lambda p, i: (0, p))
    return pl.pallas_call(
        body,
        name=name,
        grid=(n_pairs, t // tq),
        in_specs=[
            tile_spec,
            pl.BlockSpec((t, LANES), lambda p, i: (0, n_pairs + p)),
            pl.BlockSpec((t, LANES), lambda p, i: (0, 2 * n_pairs + p)),
            pl.BlockSpec((1, 1, t // tk, 2 * tq, tk), lambda p, i: (p, i, 0, 0, 0)),
            tile_spec,
        ] + ([] if tie is None else [pl.BlockSpec(memory_space=pl.ANY)]),
        out_specs=[tile_spec, full_spec, full_spec],
        out_shape=[jax.ShapeDtypeStruct((t, SB_WIDTH), BF16)] + [jax.ShapeDtypeStruct((t, SB_WIDTH), F32)] * 2,
        compiler_params=_params("arbitrary", "arbitrary"),
    )(proj, proj, proj, kept, do, *([] if tie is None else [tie]))


HG_BLOCK = 128
HG_HEADS = HG_WIDTH // HG_HEAD_DIM


def _chunk_mats(n):
    r = lax.broadcasted_iota(jnp.int32, (n, n), 0)
    c = lax.broadcasted_iota(jnp.int32, (n, n), 1)
    same = (r // HG_CHUNK) == (c // HG_CHUNK)
    upto = (same & (c <= r)).astype(BF16)
    whole = same.astype(BF16)
    onward = (same & (c >= r)).astype(BF16)
    return upto, whole, onward


def _rows_dot(mat, x):
    return _dot(jnp.concatenate([mat, mat, mat], axis=1), jnp.concatenate(_split3(x), axis=0))


def _split_heads(x):
    return jnp.stack([x[:, h * HG_HEAD_DIM:(h + 1) * HG_HEAD_DIM] for h in range(HG_HEADS)], axis=0)


def _merge_heads(x):
    return jnp.concatenate([x[h] for h in range(HG_HEADS)], axis=1)


def _lower_bound(lg_ref):
    lg = lg_ref[...]
    return _sigmoid(lg[0:1, :] - lg[1:2, :])


def _hgrn_prepare(q_ref, f_ref, lb, h, upto, whole):
    cols = slice(h * HG_HEAD_DIM, (h + 1) * HG_HEAD_DIM)
    lbh = lb[:, cols]
    sg = _sigmoid(f_ref[:, cols])
    forget = lbh + (1.0 - lbh) * sg
    logf = jnp.log(forget)
    kk = (1.0 - lbh) * (1.0 - sg)
    qv = q_ref[:, cols]
    qsig = _sigmoid(qv)
    qh = qv * qsig
    b = _rows_dot(upto, logf)
    blast = _rows_dot(whole, logf)
    return dict(lbh=lbh, sg=sg, forget=forget, kk=kk, qv=qv, qsig=qsig, qh=qh, b=b, eb=jnp.exp(b),
                ekb=jnp.exp(blast - b), dl=jnp.exp(blast))


def _hgrn_fwd(proj, logits, *, name):
    t = proj.shape[0]
    tb = HG_BLOCK
    nc = tb // HG_CHUNK
    hd = HG_HEAD_DIM

    def body(q_ref, f_ref, i_ref, lg_ref, o_ref, st_ref, state, qh_s, kk_s, b_s, qe_s, ke_s, dl_s):
        @pl.when(pl.program_id(0) == 0)
        def _():
            state[...] = jnp.zeros_like(state)

        lb = _lower_bound(lg_ref)
        upto, whole, _ = _chunk_mats(tb)
        for h in range(HG_HEADS):
            p = _hgrn_prepare(q_ref, f_ref, lb, h, upto, whole)
            qh_s[h] = p["qh"]
            kk_s[h] = p["kk"]
            b_s[h] = p["b"]
            qe_s[h] = (p["qh"] * p["eb"]).astype(BF16)
            ke_s[h] = (p["kk"] * p["ekb"]).astype(BF16)
            dl_s[h] = p["dl"]
        rowi = lax.broadcasted_iota(jnp.int32, (HG_HEADS, HG_CHUNK, hd), 1)

        def chunk(c, _):
            r0 = pl.multiple_of(c * HG_CHUNK, HG_CHUNK)
            rows = pl.ds(r0, HG_CHUNK)
            bc = b_s[:, rows, :]
            qc = qh_s[:, rows, :]
            kc = kk_s[:, rows, :]
            vc = _split_heads(i_ref[rows, :])
            s_in = state[...]
            st_ref[c] = s_in
            s_in_b = s_in.astype(BF16)
            qe = qe_s[:, rows, :]
            o = jnp.stack([_dot_nt(qe[h], s_in_b[h]) for h in range(HG_HEADS)], axis=0)
            for s in range(HG_CHUNK):
                pair = jnp.where(rowi >= s, qc * jnp.exp(bc - bc[:, s:s + 1, :]) * kc[:, s:s + 1, :], 0.0)
                o = o + jnp.sum(pair, axis=2, keepdims=True) * vc[:, s:s + 1, :]
            o_ref[rows, :] = _merge_heads(o)
            vcb = vc.astype(BF16)
            ke = ke_s[:, rows, :]
            update = jnp.stack([_dot_tn(vcb[h], ke[h]) for h in range(HG_HEADS)], axis=0)
            state[...] = s_in * dl_s[:, pl.ds(r0, 1), :] + update
            return 0

        lax.fori_loop(0, nc, chunk, 0, unroll=4)

    blk =lambda col: pl.BlockSpec((tb, HG_WIDTH), lambda i: (i, col))
    head_f32 = pltpu.VMEM((HG_HEADS, tb, hd), F32)
    head_bf16 = pltpu.VMEM((HG_HEADS, tb, hd), BF16)
    return pl.pallas_call(
        body,
        name=name,
        grid=(t // tb,),
        in_specs=[blk(3), blk(4), blk(5), pl.BlockSpec((2, HG_WIDTH), lambda i: (0, 0))],
        out_specs=[
            pl.BlockSpec((tb, HG_WIDTH), lambda i: (i, 0)),
            pl.BlockSpec((nc, HG_HEADS, hd, hd), lambda i: (i, 0, 0, 0)),
        ],
        out_shape=[
            jax.ShapeDtypeStruct((t, HG_WIDTH), F32),
            jax.ShapeDtypeStruct((t // HG_CHUNK, HG_HEADS, hd, hd), F32),
        ],
        scratch_shapes=[pltpu.VMEM((HG_HEADS, hd, hd), F32), head_f32, head_f32, head_f32, head_bf16, head_bf16,
                        head_f32],
        compiler_params=_params("arbitrary"),
    )(proj, proj, proj, logits)


def _hgrn_bwd(proj, logits, states, do, *, name):
    t = proj.shape[0]
    tb = HG_BLOCK
    nb = t // tb
    nc = tb // HG_CHUNK
    hd = HG_HEAD_DIM

    def body(q_ref, f_ref, i_ref, lg_ref, st_ref, do_ref, dq_ref, df_ref, di_ref, dlb_ref,
             dstate, qh_s, kk_s, b_s, eb_s, ekb_s, qe_s, ke_s, dl_s, dqh_s, dkk_s, dlf_s):
        step = pl.program_id(0)

        @pl.when(step == 0)
        def _():
            dstate[...] = jnp.zeros_like(dstate)
            dlb_ref[...] = jnp.zeros_like(dlb_ref)

        lb = _lower_bound(lg_ref)
        upto, whole, _ = _chunk_mats(tb)
        prepared = []
        for h in range(HG_HEADS):
            p = _hgrn_prepare(q_ref, f_ref, lb, h, upto, whole)
            prepared.append(p)
            qh_s[h] = p["qh"]
            kk_s[h] = p["kk"]
            b_s[h] = p["b"]
            eb_s[h] = p["eb"]
            ekb_s[h] = p["ekb"]
            qe_s[h] = (p["qh"] * p["eb"]).astype(BF16)
            ke_s[h] = (p["kk"] * p["ekb"]).astype(BF16)
            dl_s[h] = p["dl"]
        rowi = lax.broadcasted_iota(jnp.int32, (HG_CHUNK, hd), 0)
        r16 = lax.broadcasted_iota(jnp.int32, (HG_CHUNK, HG_CHUNK), 0)
        c16 = lax.broadcasted_iota(jnp.int32, (HG_CHUNK, HG_CHUNK), 1)
        onward = (c16 >= r16).astype(BF16)

        def chunk(it, _):
            c = nc - 1 - it
            r0 = pl.multiple_of(c * HG_CHUNK, HG_CHUNK)
            rows = pl.ds(r0, HG_CHUNK)
            for h in range(HG_HEADS):
                cols = slice(h * hd, (h + 1) * hd)
                bc = b_s[h, rows, :]
                qc = qh_s[h, rows, :]
                kc = kk_s[h, rows, :]
                vc = i_ref[rows, cols]
                doc = do_ref[rows, cols]
                s_in = st_ref[c, h]
                ds_out = dstate[h]
                ds_out_b = ds_out.astype(BF16)
                docb = doc.astype(BF16)
                dl_row = dl_s[h, pl.ds(r0, 1), :]
                dqh = _dot(docb, s_in.astype(BF16)) * eb_s[h, rows, :]
                dkk = _dot(vc.astype(BF16), ds_out_b) * ekb_s[h, rows, :]
                dv = _dot_nt(ke_s[h, rows, :], ds_out_b)
                db = dqh * qc - dkk * kc
                dwhole = jnp.sum(dkk * kc, axis=0, keepdims=True) + jnp.sum(ds_out * s_in, axis=0, keepdims=True) * dl_row
                dk_rows, dv_rows = [], []
                for s in range(HG_CHUNK):
                    keep = rowi >= s
                    e = jnp.exp(bc - bc[s:s + 1, :])
                    k_row = kc[s:s + 1, :]
                    pcol = jnp.sum(jnp.where(keep, qc * e * k_row, 0.0), axis=1, keepdims=True)
                    dpcol = jnp.sum(doc * vc[s:s + 1, :], axis=1, keepdims=True)
                    m = jnp.where(keep, e * dpcol, 0.0)
                    y = m * qc
                    dqh = dqh + m * k_row
                    db = db + y * k_row
                    dk_rows.append(jnp.sum(y, axis=0, keepdims=True))
                    dv_rows.append(jnp.sum(pcol * doc, axis=0, keepdims=True))
                dkk_pairs = jnp.concatenate(dk_rows, axis=0)
                dkk = dkk + dkk_pairs
                db = db - dkk_pairs * kc
                dv = dv + jnp.concatenate(dv_rows, axis=0)
                dqh_s[h, rows, :] = dqh
                dkk_s[h, rows, :] = dkk
                dlf_s[h, rows, :] = _rows_dot(onward, db) + dwhole
                di_ref[rows, cols] = dv.astype(BF16)
                dstate[h] = ds_out * dl_row + _dot_tn(docb, qe_s[h, rows, :])
            return 0

        lax.fori_loop(0, nc, chunk, 0, unroll=4)
        for h in range(HG_HEADS):
            cols = slice(h * hd, (h + 1) * hd)
            p = prepared[h]
            dq_ref[:, cols] = (dqh_s[h] * (p["qsig"] * (1.0 + p["qv"] * (1.0 - p["qsig"])))).astype(BF16)
            dforget = dlf_s[h] / p["forget"] - dkk_s[h]
            df_ref[:, cols] = (dforget * (1.0 - p["lbh"]) * p["sg"] * (1.0 - p["sg"])).astype(BF16)
            dlb_ref[:, cols] += jnp.sum(dforget * (1.0 - p["sg"]), axis=0, keepdims=True)

    blk = lambda col: pl.BlockSpec((tb, HG_WIDTH), lambda i: (nb - 1 - i, col))
    vec = pl.BlockSpec((1, HG_WIDTH), lambda i: (0, 0))
    head_f32 = pltpu.VMEM((HG_HEADS, tb, hd), F32)
    head_bf16 = pltpu.VMEM((HG_HEADS, tb, hd), BF16)
    return pl.pallas_call(
        body,
        name=name,
        grid=(nb,),
        in_specs=[
            blk(3), blk(4), blk(5),
            pl.BlockSpec((2, HG_WIDTH), lambda i: (0, 0)),
            pl.BlockSpec((nc, HG_HEADS, hd, hd), lambda i: (nb - 1 - i, 0, 0, 0)),
            blk(0),
        ],
        out_specs=[blk(0), blk(0), blk(0), vec],
        out_shape=[jax.ShapeDtypeStruct((t, HG_WIDTH), BF16)] * 3 + [jax.ShapeDtypeStruct((1, HG_WIDTH), F32)],
        scratch_shapes=[
            pltpu.VMEM((HG_HEADS, hd, hd), F32),
            head_f32, head_f32, head_f32, head_f32, head_f32, head_bf16, head_bf16, head_f32,
            head_f32, head_f32, head_f32,
        ],
        compiler_params=_params("arbitrary"),
    )(proj, proj, proj, logits, states, do)


def _group_mat(width, head_dim):
    r = lax.broadcasted_iota(jnp.int32, (width, width), 0)
    c = lax.broadcasted_iota(jnp.int32, (width, width), 1)
    return ((r // head_dim) == (c // head_dim)).astype(BF16)


def _head_mean(x, mat, head_dim):
    hi = x.astype(BF16)
    lo = (x - hi.astype(F32)).astype(BF16)
    return (_dot(hi, mat) + _dot(lo, mat)) * (1.0 / head_dim)


def _mix_out_fwd(o_sb, o_hg, proj, g_sb, g_hg, w_out, x1, *, name, tm=512):
    t = x1.shape[0]

    def body(osb_ref, ohg_ref, gate_ref, gsb_ref, ghg_ref, w_ref, x_ref, xo_ref, mt_ref):
        msb = _group_mat(SB_WIDTH, SB_HEAD_DIM)
        mhg = _group_mat(HG_WIDTH, HG_HEAD_DIM)
        osb = osb_ref[...]
        ohg = ohg_ref[...]
        nsb = osb * lax.rsqrt(_head_mean(osb * osb, msb, SB_HEAD_DIM) + EPS) * gsb_ref[...]
        gate = gate_ref[...]
        nhg = ohg * lax.rsqrt(_head_mean(ohg * ohg, mhg, HG_HEAD_DIM) + EPS) * ghg_ref[...] * (gate * _sigmoid(gate))
        mixed = jnp.concatenate([nsb, nhg], axis=1).astype(BF16)
        mt_ref[...] = mixed
        xo_ref[...] = x_ref[...] + _dot(mixed, w_ref[...])

    half = pl.BlockSpec((tm, SB_WIDTH), lambda i: (i, 0))
    vec = pl.BlockSpec((1, SB_WIDTH), lambda i: (0, 0))
    row = pl.BlockSpec((tm, D_MODEL), lambda i: (i, 0))
    return pl.pallas_call(
        body,
        name=name,
        grid=(t // tm,),
        in_specs=[half, half, pl.BlockSpec((tm, HG_WIDTH), lambda i: (i, 6)), vec, vec,
                  pl.BlockSpec((D_MODEL, D_MODEL), lambda i: (0, 0)), row],
        out_specs=[row, row],
        out_shape=[jax.ShapeDtypeStruct((t, D_MODEL), F32), jax.ShapeDtypeStruct((t, D_MODEL), BF16)],
        compiler_params=_params("parallel"),
    )(o_sb, o_hg, proj, g_sb, g_hg, w_out, x1)


def _mix_out_bwd(dx2, o_sb, o_hg, proj, g_sb, g_hg, w_out, *, name, tm=512):
    t = dx2.shape[0]

    def body(dx_ref, osb_ref, ohg_ref, gate_ref, gsb_ref, ghg_ref, w_ref, dosb_ref, dohg_ref, dgate_ref, dgsb_ref,
             dghg_ref, dxb_ref):
        i = pl.program_id(0)
        msb = _group_mat(SB_WIDTH, SB_HEAD_DIM)
        mhg = _group_mat(HG_WIDTH, HG_HEAD_DIM)
        dxb = dx_ref[...].astype(BF16)
        dxb_ref[...] = dxb
        dmixed = _dot_nt(dxb, w_ref[...])
        dnsb = dmixed[:, :SB_WIDTH]
        dy = dmixed[:, SB_WIDTH:]

        osb = osb_ref[...]
        rstd = lax.rsqrt(_head_mean(osb * osb, msb, SB_HEAD_DIM) + EPS)
        ohat = osb * rstd
        part_sb = jnp.sum(dnsb * ohat, axis=0, keepdims=True)
        dohat = dnsb * gsb_ref[...]
        dosb_ref[...] = rstd * (dohat - ohat * _head_mean(dohat * ohat, msb, SB_HEAD_DIM))

        ohg = ohg_ref[...]
        rstd = lax.rsqrt(_head_mean(ohg * ohg, mhg, HG_HEAD_DIM) + EPS)
        ohat = ohg * rstd
        gate = gate_ref[...]
        sig = _sigmoid(gate)
        dn = dy * (gate * sig)
        dgate_ref[...] = (dy * (ohat * ghg_ref[...]) * (sig * (1.0 + gate * (1.0 - sig)))).astype(BF16)
        part_hg = jnp.sum(dn * ohat, axis=0, keepdims=True)
        dohat = dn * ghg_ref[...]
        dohg_ref[...] = rstd * (dohat - ohat * _head_mean(dohat * ohat, mhg, HG_HEAD_DIM))

        @pl.when(i == 0)
        def _():
            dgsb_ref[...] = part_sb
            dghg_ref[...] = part_hg

        @pl.when(i > 0)
        def _():
            dgsb_ref[...] += part_sb
            dghg_ref[...] += part_hg

    half = pl.BlockSpec((tm, SB_WIDTH), lambda i: (i, 0))
    vec = pl.BlockSpec((1, SB_WIDTH), lambda i: (0, 0))
    row = pl.BlockSpec((tm, D_MODEL), lambda i: (i, 0))
    return pl.pallas_call(
        body,
        name=name,
        grid=(t // tm,),
        in_specs=[row, half, half, pl.BlockSpec((tm, HG_WIDTH), lambda i: (i, 6)), vec, vec,
                  pl.BlockSpec((D_MODEL, D_MODEL), lambda i: (0, 0))],
        out_specs=[half, half, half, vec, vec, row],
        out_shape=[jax.ShapeDtypeStruct((t, SB_WIDTH), F32)] * 2 + [jax.ShapeDtypeStruct((t, SB_WIDTH), BF16)]
        + [jax.ShapeDtypeStruct((1, SB_WIDTH), F32)] * 2 + [jax.ShapeDtypeStruct((t, D_MODEL), BF16)],
        compiler_params=_params("arbitrary"),
    )(dx2, o_sb, o_hg, proj, g_sb, g_hg, w_out)


def _local_step(x, target, norms, logits, w, weights_after=None, grads_ready=None):
    w = dict(w)
    x1, a1, b1, h1, s1, hm = _ffn_fwd(x, norms["ffn1"], w["g1t"], w["u1t"], w["d1"], name="ffn1_fwd",
                                      next_gain=norms["mix"])
    if weights_after is not None:
        w.update(weights_after("ffn1", x1))
    proj = _mm(hm, w["int"], name="in_proj", tm=512, tn=IN_COLS, nt=True)
    o_sb, sb_kept = _attn_fwd(proj, name="sb_attn_fwd")
    o_hg, states = _hgrn_fwd(proj, logits, name="hgrn2_fwd")
    x2, mixed = _mix_out_fwd(o_sb, o_hg, proj, norms["sb"], norms["hg"], w["out"], x1, name="mix_out_fwd")
    if weights_after is not None:
        w.update(weights_after("mix", x2))
    dx3, a2, b2, h2, s2, d_final, loss_row = _ffn_fwd(x2, norms["ffn2"], w["g2t"], w["u2t"], w["d2"], name="ffn2_fwd",
                                                      head=(norms["final"], target))

    def weight_grad(lhs, rhs, name, tie=None):
        return _mm(lhs, rhs, name=name, tm=256, tn=D_MODEL, ta=True, out_dtype=BF16, tie=tie)

    def sent(stage):
        return grads_ready(stage, gw) if grads_ready is not None else None

    gw, gv = {}, {"final": d_final}
    dx2, gv["ffn2"], da2, db2, dob2 = _ffn_bwd(dx3, x2, norms["ffn2"], a2, b2, w["g2t"], w["u2t"], w["d2"],
                                               name="ffn2_bwd")
    gw["g2t"] = weight_grad(da2, h2, "ffn2_dgate")
    gw["u2t"] = weight_grad(db2, h2, "ffn2_dup")
    gw["d2"] = weight_grad(s2, dob2, "ffn2_ddown")

    do_sb, do_hg, d_gate, gv["sb"], gv["hg"], dx2b = _mix_out_bwd(
        dx2, o_sb, o_hg, proj, norms["sb"], norms["hg"], w["out"], name="mix_out_bwd")
    gw["out"] = weight_grad(mixed, dx2b, "out_dw")
    tie = sent("mix")
    dq_sb, dk_sb, dv_sb = _attn_bwd(proj, sb_kept, do_sb, name="sb_attn_bwd", tie=tie)
    dq_hg, df_hg, di_hg, d_lb = _hgrn_bwd(proj, logits if tie is None else logits + tie[0, 0], states, do_hg,
                                          name="hgrn2_bwd")
    dproj = jnp.concatenate([dq_sb, dk_sb.astype(BF16), dv_sb.astype(BF16), dq_hg, df_hg, di_hg, d_gate], axis=1)
    gw["int"] = weight_grad(dproj, hm, "in_dw")
    tie = sent("in")
    dx1, gv["mix"], dob1 = _in_proj_bwd(dproj, w["int"], x1, norms["mix"] if tie is None else norms["mix"] + tie[0, 0],
                                        dx2, name="in_dx")

    gw["d1"] = weight_grad(s1, dob1, "ffn1_ddown")
    tie = sent("d1")
    da1, db1 = _ffn_bwd_act(dob1, a1, b1, w["d1"], name="ffn1_bwd_act",
                            tie=jnp.zeros((8, LANES), F32) if tie is None else tie)
    gw["g1t"] = weight_grad(da1, h1, "ffn1_dgate")
    gw["u1t"] = weight_grad(db1, h1, "ffn1_dup", tie=sent("g1t"))
    tie = sent("u1t")
    dx, gv["ffn1"] = _ffn_bwd_in(dx1, x, norms["ffn1"] if tie is None else norms["ffn1"] + tie[0, 0], da1, db1,
                                 w["g1t"], w["u1t"], name="ffn1_bwd_in")
    gv["lb"] = d_lb
    return loss_row, dx, gw, gv


HBM = pl.BlockSpec(memory_space=pl.ANY)


def _place():
    return lax.axis_index("x"), lax.axis_index("y"), lax.axis_index("c")


def _slot(px, py, pc):
    return 4 * px + 2 * py + pc


GATHER_COPIES = 8


def _all_gather(blocks, *, name):
    n = len(blocks)

    def body(*refs):
        ins, outs = refs[:n], refs[n:2 * n]
        send_sems, recv_sems, local_sems = refs[2 * n:]
        x, y, c = _place()
        me, sibling = (x, y, c), (x, y, 1 - c)
        beside, across, diagonal = (1 - x, y, c), (x, 1 - y, c), (1 - x, 1 - y, c)

        def copy(a, k, block, to, src=None, half=None):
            dst = outs[a].at[_slot(*block)]
            if half is not None:
                rows = blocks[a].shape[0] // 2
                dst = dst.at[pl.ds(half * rows, rows)]
            return pltpu.make_async_remote_copy(
                src_ref=dst if src is None else src, dst_ref=dst, send_sem=send_sems.at[GATHER_COPIES * a + k],
                recv_sem=recv_sems.at[GATHER_COPIES * a + k], device_id=to, device_id_type=MESH)

        mine = [pltpu.make_async_copy(ins[a], outs[a].at[_slot(*me)], local_sems.at[a]) for a in range(n)]
        for cp in mine:
            cp.start()
        sent = []
        for a in range(n):
            sent += [copy(a, 0, me, sibling, src=ins[a]), copy(a, 1, me, beside, src=ins[a]),
                     copy(a, 2, me, across, src=ins[a])]
        for cp in sent:
            cp.start()
        for a in range(n):
            copy(a, 1, beside, me).wait_recv()
            sent += [copy(a, 3, beside, across, half=0), copy(a, 5, beside, sibling)]
            sent[-2].start()
            sent[-1].start()
        for a in range(n):
            copy(a, 2, across, me).wait_recv()
            sent += [copy(a, 4, across, beside, half=1), copy(a, 6, across, sibling)]
            sent[-2].start()
            sent[-1].start()
        for a in range(n):
            copy(a, 3, diagonal, me, half=0).wait_recv()
            copy(a, 4, diagonal, me, half=1).wait_recv()
            sent.append(copy(a, 7, diagonal, sibling))
            sent[-1].start()
        for a in range(n):
            for k, origin in ((0, sibling), (5, (1 - x, y, 1 - c)), (6, (x, 1 - y, 1 - c)), (7, (1 - x, 1 - y, 1 - c))):
                copy(a, k, origin, me).wait_recv()
        for cp in sent:
            cp.wait_send()
        for cp in mine:
            cp.wait()

    return pl.pallas_call(
        body,
        name=name,
        in_specs=[HBM] * n,
        out_specs=[HBM] * n,
        out_shape=[jax.ShapeDtypeStruct((N_DEV,) + b.shape, b.dtype) for b in blocks],
        scratch_shapes=[pltpu.SemaphoreType.DMA((GATHER_COPIES * n,)), pltpu.SemaphoreType.DMA((GATHER_COPIES * n,)),
                        pltpu.SemaphoreType.DMA((n,))],
    )(*blocks)


def _flipped(place, d):
    return tuple(1 - p if (d >> (2 - axis)) & 1 else p for axis, p in enumerate(place))


SEM = pl.BlockSpec(memory_space=pltpu.SEMAPHORE)
EFFECT = pltpu.SideEffectType.DATAFLOW_SIDE_EFFECTING


def _split_copies(me, srcs, lands, send_sems, recv_sems, by_owner):
    copies = []
    for d in range(1, N_DEV):
        peer = _flipped(me, d)
        for a, (src, land) in enumerate(zip(srcs, lands)):
            copies.append(pltpu.make_async_remote_copy(
                src_ref=src.at[_slot(*peer)] if by_owner else src, dst_ref=land.at[_slot(*me)],
                send_sem=send_sems.at[7 * a + d - 1], recv_sem=recv_sems.at[7 * a + d - 1], device_id=peer,
                device_id_type=MESH))
    own = [pltpu.make_async_copy(src.at[_slot(*me)] if by_owner else src, land.at[_slot(*me)],
                                 recv_sems.at[7 * len(srcs) + a]) for a, (src, land) in enumerate(zip(srcs, lands))]
    return copies, own


def _copies_start(srcs, *, name, by_owner, after=None):
    n = len(srcs)
    extra = [] if after is None else [after]
    land_shapes = [s.shape if by_owner else (N_DEV,) + s.shape for s in srcs]
    lands = [pltpu.with_memory_space_constraint(lax.empty(shape, s.dtype), pltpu.HBM) for shape, s in zip(land_shapes, srcs)]
    srcs = [pltpu.with_memory_space_constraint(s, pltpu.HBM) for s in srcs]

    def body(*refs):
        src_refs, land_refs = refs[:n], refs[n:2 * n]
        send_sems, recv_sems = refs[2 * n + len(extra)], refs[2 * n + len(extra) + 1]
        token = refs[-1]
        copies, own = _split_copies(_place(), src_refs, land_refs, send_sems, recv_sems, by_owner)
        for cp in copies + own:
            cp.start()
        token[...] = jnp.zeros_like(token)

    out = pl.pallas_call(
        body,
        name=name,
        in_specs=[HBM] * (2 * n + len(extra)),
        out_specs=[SEM, SEM] + [HBM] * (2 * n) + [pl.BlockSpec(memory_space=pltpu.VMEM)],
        out_shape=[pltpu.SemaphoreType.DMA((7 * n,)), pltpu.SemaphoreType.DMA((8 * n,))]
        + [pltpu.HBM(s.shape, s.dtype) for s in srcs] + [pltpu.HBM(shape, s.dtype) for shape, s in zip(land_shapes, srcs)]
        + [jax.ShapeDtypeStruct((8, LANES), F32)],
        input_output_aliases={i: 2 + i for i in range(2 * n)},
        compiler_params=pltpu.CompilerParams(has_side_effects=EFFECT),
    )(*srcs, *lands, *extra)
    return (out[0], out[1], out[2:2 + n], out[2 + n:2 + 2 * n]), out[-1]


def _copies_wait(started, after, *, name, by_owner):
    send_sems, recv_sems, srcs, lands = started
    n = len(srcs)

    def body(*refs):
        src_refs, land_refs = refs[:n], refs[n:2 * n]
        copies, own = _split_copies(_place(), src_refs, land_refs, refs[2 * n], refs[2 * n + 1], by_owner)
        for cp in copies:
            cp.wait_send()
            cp.wait_recv()
        for cp in own:
            cp.wait()

    out = pl.pallas_call(
        body,
        name=name,
        in_specs=[HBM] * (2 * n) + [SEM, SEM, HBM],
        out_specs=[HBM] * (2 * n),
        out_shape=[pltpu.HBM(s.shape, s.dtype) for s in srcs] + [pltpu.HBM(s.shape, s.dtype) for s in lands],
        input_output_aliases={i: i for i in range(2 * n)},
        compiler_params=pltpu.CompilerParams(has_side_effects=EFFECT),
    )(*srcs, *lands, send_sems, recv_sems, after)
    return out[:n], out[n:]


def _adamw(w, g, m, v):
    m = ADAM_B1 * m + (1.0 - ADAM_B1) * g
    v = ADAM_B2 * v + (1.0 - ADAM_B2) * (g * g)
    m_hat = m / (1.0 - ADAM_B1 ** ADAM_STEP)
    v_hat = v / (1.0 - ADAM_B2 ** ADAM_STEP)
    delta = -ADAM_LR * (m_hat / (jnp.sqrt(v_hat) + ADAM_EPS) + ADAM_WD * w)
    return delta, m, v


def _sum_and_update(parts, w, m, v, *, name, tie=None):
    _, rows, cols = w.shape
    tr = rows // 2

    def body(p_ref, w_ref, m_ref, v_ref, *rest):
        g_ref, d_ref, mo_ref, vo_ref = rest[-4:]
        g = p_ref[0].astype(F32)
        for s in range(1, N_DEV):
            g = g + p_ref[s].astype(F32)
        g_ref[0] = g
        d_ref[0], mo_ref[0], vo_ref[0] = _adamw(w_ref[0], g, m_ref[0], v_ref[0])

    flat = pl.BlockSpec((1, tr, cols), lambda i: (0, i, 0))
    return pl.pallas_call(
        body,
        name=name,
        grid=(rows // tr,),
        in_specs=[pl.BlockSpec((N_DEV, tr, cols), lambda i: (0, i, 0)), flat, flat, flat]
        + ([] if tie is None else [pl.BlockSpec(memory_space=pl.ANY)]),
        out_specs=[flat] * 4,
        out_shape=[jax.ShapeDtypeStruct((1, rows, cols), F32)] * 4,
        compiler_params=_params("parallel"),
    )(parts, w, m, v, *([] if tie is None else [tie]))


VEC_ROWS = 8
ROW_LOGITS, ROW_LOSS = 5, 7


def _vectors_update(part, w, m, v, *, name, tie):
    def body(p_ref, w_ref, m_ref, v_ref, tie_ref, g_ref, d_ref, mo_ref, vo_ref, loss_ref, all_ref, send_sems, recv_sems):
        me = _place()
        all_ref[_slot(*me)] = p_ref[...]
        copies = []
        for d in range(1, N_DEV):
            peer = _flipped(me, d)
            copies.append(pltpu.make_async_remote_copy(
                src_ref=p_ref, dst_ref=all_ref.at[_slot(*me)], send_sem=send_sems.at[d - 1], recv_sem=recv_sems.at[d - 1],
                device_id=peer, device_id_type=MESH))
        for cp in copies:
            cp.start()
        for cp in copies:
            cp.wait()
        total = all_ref[0]
        for s in range(1, N_DEV):
            total = total + all_ref[s]
        wv = w_ref[...]
        half = D_MODEL // 2
        lb = _sigmoid(wv[ROW_LOGITS:ROW_LOGITS + 1, :half] - wv[ROW_LOGITS:ROW_LOGITS + 1, half:])
        d_first = total[ROW_LOGITS:ROW_LOGITS + 1, :half] * lb * (1.0 - lb)
        d_logits = jnp.concatenate([d_first, -d_first], axis=1)
        rowi = lax.broadcasted_iota(jnp.int32, (VEC_ROWS, D_MODEL), 0)
        g = jnp.where(rowi == ROW_LOGITS, d_logits, jnp.where(rowi < ROW_LOGITS, total, 0.0))
        g_ref[...] = g
        d_ref[...], mo_ref[...], vo_ref[...] = _adamw(wv, g, m_ref[...], v_ref[...])
        loss_ref[...] = total[ROW_LOSS:ROW_LOSS + 1, :]

    vmem = pl.BlockSpec(memory_space=pltpu.VMEM)
    return pl.pallas_call(
        body,
        name=name,
        in_specs=[vmem] * 4 + [HBM],
        out_specs=[vmem] * 5,
        out_shape=[jax.ShapeDtypeStruct((VEC_ROWS, D_MODEL), F32)] * 4 + [jax.ShapeDtypeStruct((1, D_MODEL), F32)],
        scratch_shapes=[pltpu.VMEM((N_DEV, VEC_ROWS, D_MODEL), F32), pltpu.SemaphoreType.DMA((7,)),
                        pltpu.SemaphoreType.DMA((7,))],
    )(part, w, m, v, tie)


TRANSPOSED = ("g1t", "u1t", "g2t", "u2t", "int")


def _vector_rows(rows):
    rowi = lax.broadcasted_iota(jnp.int32, (VEC_ROWS, D_MODEL), 0)
    out = jnp.zeros((VEC_ROWS, D_MODEL), F32)
    for i, r in enumerate(rows):
        if r is not None:
            out = jnp.where(rowi == i, r, out)
    return out


def kernel(x, ffn1_norm, ffn1_w_gate, ffn1_w_up, ffn1_w_down, mix_norm, w_in, sb_out_norm, hg_lower_bound_logits, hg_out_norm, w_out, ffn2_norm, ffn2_w_gate, ffn2_w_up, ffn2_w_down, final_norm, loss_target, m_ffn1_norm, m_ffn1_w_gate, m_ffn1_w_up, m_ffn1_w_down, m_mix_norm, m_w_in, m_sb_out_norm, m_hg_lower_bound_logits, m_hg_out_norm, m_w_out, m_ffn2_norm, m_ffn2_w_gate, m_ffn2_w_up, m_ffn2_w_down, m_final_norm, v_ffn1_norm, v_ffn1_w_gate, v_ffn1_w_up, v_ffn1_w_down, v_mix_norm, v_w_in, v_sb_out_norm, v_hg_lower_bound_logits, v_hg_out_norm, v_w_out, v_ffn2_norm, v_ffn2_w_gate, v_ffn2_w_up, v_ffn2_w_down, v_final_norm):
    def matrices(g1, u1, d1, win, wout, g2, u2, d2):
        return {"g1t": g1, "u1t": u1, "d1": d1, "int": win, "out": wout, "g2t": g2, "u2t": u2, "d2": d2}

    def vectors(n1, nm, nsb, lg, nhg, n2, nf):
        return [n1, nm, n2, nf.reshape(1, D_MODEL), jnp.concatenate([nsb, nhg], axis=1), lg.reshape(1, D_MODEL), None, None]

    w_sh = matrices(ffn1_w_gate, ffn1_w_up, ffn1_w_down, w_in, w_out, ffn2_w_gate, ffn2_w_up, ffn2_w_down)
    m_sh = matrices(m_ffn1_w_gate, m_ffn1_w_up, m_ffn1_w_down, m_w_in, m_w_out, m_ffn2_w_gate, m_ffn2_w_up, m_ffn2_w_down)
    v_sh = matrices(v_ffn1_w_gate, v_ffn1_w_up, v_ffn1_w_down, v_w_in, v_w_out, v_ffn2_w_gate, v_ffn2_w_up, v_ffn2_w_down)
    keys = list(w_sh)

    def full(key, stack):
        return stack.reshape(-1, D_MODEL)

    def by_owner(key, grad):
        return grad.reshape(N_DEV, -1, D_MODEL)

    def view(key, a):
        return jnp.swapaxes(a, 1, 2) if key in TRANSPOSED else a

    blocks = {k: view(k, w_sh[k])[0].astype(BF16) for k in keys}
    first, mid, last = ("g1t", "u1t", "d1"), ("int", "out"), ("g2t", "u2t", "d2")
    w_first = {k: full(k, s) for k, s in zip(first, _all_gather([blocks[k] for k in first], name="gather_ffn1"))}
    flights = {}
    flights["ffn1"], token_mid = _copies_start([blocks[k] for k in mid], name="gather_mid_start", by_owner=False,
                                               after=w_first["d1"])
    flights["mix"], token_last = _copies_start([blocks[k] for k in last], name="gather_ffn2_start", by_owner=False,
                                               after=token_mid)

    def weights_after(stage, result):
        group = mid if stage == "ffn1" else last
        _, lands = _copies_wait(flights[stage], result, name="gather_" + stage + "_wait", by_owner=False)
        return {k: full(k, s) for k, s in zip(group, lands)}

    groups = {"mix": ("g2t", "u2t", "d2", "out"), "in": ("int",), "g1t": ("g1t",), "u1t": ("u1t",), "d1": ("d1",)}
    sent = {}

    def grads_ready(stage, gw):
        stacks = [by_owner(k, gw[k]) for k in groups[stage]]
        sent[stage], token = _copies_start(stacks, name="grads_" + stage + "_start", by_owner=True)
        return token

    norms = {"ffn1": ffn1_norm + token_last[0, 0], "mix": mix_norm, "sb": sb_out_norm, "hg": hg_out_norm,
             "ffn2": ffn2_norm, "final": final_norm.reshape(1, D_MODEL)}
    loss_row, grad_x, gw, gv = _local_step(x[0], loss_target[0], norms, hg_lower_bound_logits, w_first, weights_after,
                                           grads_ready)

    lb_row = jnp.concatenate([gv["lb"], jnp.zeros_like(gv["lb"])], axis=1)
    part = _vector_rows([gv["ffn1"], gv["mix"], gv["ffn2"], gv["final"], jnp.concatenate([gv["sb"], gv["hg"]], axis=1),
                         lb_row, None, loss_row])
    vec_w = _vector_rows(vectors(ffn1_norm, mix_norm, sb_out_norm, hg_lower_bound_logits, hg_out_norm, ffn2_norm, final_norm))
    vec_m = _vector_rows(vectors(m_ffn1_norm, m_mix_norm, m_sb_out_norm, m_hg_lower_bound_logits, m_hg_out_norm,
                                 m_ffn2_norm, m_final_norm))
    vec_v = _vector_rows(vectors(v_ffn1_norm, v_mix_norm, v_sb_out_norm, v_hg_lower_bound_logits, v_hg_out_norm,
                                 v_ffn2_norm, v_final_norm))
    updated, after = {}, grad_x
    for stage, flight in sent.items():
        if stage == list(sent)[-1]:
            *vecs, loss_out = _vectors_update(part, vec_w, vec_m, vec_v, name="vectors_update", tie=after)
            after = loss_out
        _, lands = _copies_wait(flight, after, name="grads_" + stage + "_wait", by_owner=True)
        for k, part_k in zip(groups[stage], lands):
            updated[k] = _sum_and_update(part_k, view(k, w_sh[k]), view(k, m_sh[k]), view(k, v_sh[k]), name="adamw_" + k,
                                         tie=after)
            after = updated[k][0]
    mats = [{k: view(k, updated[k][i]) for k in keys} for i in range(4)]

    def leaves(mat, vec):
        half = D_MODEL // 2
        return (
            vec[0:1], mat["g1t"], mat["u1t"], mat["d1"], vec[1:2], mat["int"], vec[4:5, :half],
            vec[ROW_LOGITS].reshape(2, half), vec[4:5, half:], mat["out"], vec[2:3], mat["g2t"], mat["u2t"],
            mat["d2"], vec[3],
        )

    out = [loss_out[0, 0], grad_x[None]]
    for mat, vec in zip(mats, vecs):
        out.extend(leaves(mat, vec))
    return tuple(out)
```

```python
import jax
import jax.numpy as jnp
from jax import lax
from jax.experimental import pallas as pl
from jax.experimental.pallas import tpu as pltpu

F32, BF16 = jnp.float32, jnp.bfloat16
D_MODEL = 1024
D_FF = 2816
SB_WIDTH = 512
HG_WIDTH = 512
SB_HEAD_DIM = 64
HG_HEAD_DIM = 128
IN_COLS = 3584
EPS = 1e-6
N_DEV = 8
LANES = 128
HG_CHUNK = 16
VMEM_LIMIT_BYTES = 48 * 1024 * 1024
FFN_BWD_VMEM_LIMIT_BYTES = 56 * 1024 * 1024
ADAM_LR, ADAM_B1, ADAM_B2, ADAM_EPS, ADAM_WD, ADAM_STEP = 0.001, 0.9, 0.999, 1e-08, 0.01, 10
MESH = pl.DeviceIdType.MESH


def _params(*semantics, vmem_limit_bytes=VMEM_LIMIT_BYTES):
    return pltpu.CompilerParams(dimension_semantics=semantics, vmem_limit_bytes=vmem_limit_bytes)


def _dot(a, b):
    return jnp.dot(a, b, preferred_element_type=F32)


def _dot_nt(a, b):
    return lax.dot_general(a, b, (((1,), (1,)), ((), ())), preferred_element_type=F32)


def _dot_tn(a, b):
    return lax.dot_general(a, b, (((0,), (0,)), ((), ())), preferred_element_type=F32)


def _split3(x):
    hi = x.astype(BF16)
    r1 = x - hi.astype(F32)
    mid = r1.astype(BF16)
    lo = (r1 - mid.astype(F32)).astype(BF16)
    return hi, mid, lo


def _rms(xv):
    rstd = lax.rsqrt(jnp.mean(xv * xv, axis=-1, keepdims=True) + EPS)
    return xv * rstd, rstd


def _sigmoid(x):
    return 0.5 + 0.5 * jnp.tanh(0.5 * x)


def _loss_terms(xv, gain, target):
    xhat, rstd = _rms(xv)
    err = xhat * gain - target
    loss = 0.5 * jnp.sum(jnp.mean(err * err, axis=-1, keepdims=True), axis=0, keepdims=True)
    dy = err * (1.0 / xv.shape[-1])
    dxh = dy * gain
    dx = rstd * (dxh - xhat * jnp.mean(dxh * xhat, axis=-1, keepdims=True))
    return dx, jnp.sum(dy * xhat, axis=0, keepdims=True), loss


def _mm(a, b, *, name, tm, tn, nt=False, ta=False, out_dtype=F32, tie=None):
    k, m = a.shape if ta else a.shape[::-1]
    n = b.shape[0] if nt else b.shape[1]
    assert m % tm == 0 and n % tn == 0 and not (nt and ta), (name, a.shape, b.shape, tm, tn)

    def body(a_ref, b_ref, *rest):
        av = a_ref[...].astype(BF16)
        bv = b_ref[...].astype(BF16)
        rest[-1][...] = (_dot_nt(av, bv) if nt else _dot_tn(av, bv) if ta else _dot(av, bv)).astype(out_dtype)

    in_specs = [
        pl.BlockSpec((k, tm), lambda i, j: (0, i)) if ta else pl.BlockSpec((tm, k), lambda i, j: (i, 0)),
        pl.BlockSpec((tn, k), lambda i, j: (j, 0)) if nt else pl.BlockSpec((k, tn), lambda i, j: (0, j)),
    ]
    operands = [a, b]
    if tie is not None:
        in_specs.append(pl.BlockSpec(memory_space=pl.ANY))
        operands.append(tie)
    return pl.pallas_call(
        body,
        name=name,
        grid=(m // tm, n // tn),
        in_specs=in_specs,
        out_specs=pl.BlockSpec((tm, tn), lambda i, j: (i, j)),
        out_shape=jax.ShapeDtypeStruct((m, n), out_dtype),
        compiler_params=_params("parallel", "parallel"),
    )(*operands)


def _ffn_fwd(x, gain, wgt, wut, wd, *, name, next_gain=None, head=None, tm=1024, tf=256):
    t = x.shape[0]
    nj = D_FF // tf
    extra_in = [] if next_gain is None else [next_gain]
    extra_in += [] if head is None else list(head)

    def body(x_ref, g_ref, wg_ref, wu_ref, wd_prev_ref, wd_last_ref, *rest):
        extra, (xo_ref, a_ref, b_ref, h_ref, st_ref) = rest[:len(extra_in)], rest[len(extra_in):len(extra_in) + 5]
        tail_out, (acc, s_prev) = rest[len(extra_in) + 5:-2], rest[-2:]
        i = pl.program_id(0)
        j = pl.program_id(1)

        @pl.when(j == 0)
        def _():
            xhat, _ = _rms(x_ref[...])
            h_ref[...] = (xhat * g_ref[...]).astype(BF16)
            acc[...] = jnp.zeros_like(acc)
            s_prev[...] = jnp.zeros_like(s_prev)

        acc[...] += _dot(s_prev[...], wd_prev_ref[...])
        h = h_ref[...]
        a = _dot_nt(h, wg_ref[...])
        b = _dot_nt(h, wu_ref[...])
        a_ref[...] = a.astype(BF16)
        b_ref[...] = b.astype(BF16)
        s = (a * _sigmoid(a) * b).astype(BF16)
        st_ref[...] = s
        s_prev[...] = s

        @pl.when(j == nj - 1)
        def _():
            xo = x_ref[...] + 0.5 * (acc[...] + _dot(s, wd_last_ref[...]))
            if head is None:
                xo_ref[...] = xo
            if next_gain is not None:
                tail_out[0][...] = (_rms(xo)[0] * extra[0][...]).astype(BF16)
            if head is not None:
                gain_ref, target_ref = extra[-2:]
                dg_ref, loss_ref = tail_out[-2:]
                xo_ref[...], part_g, part_loss = _loss_terms(xo, gain_ref[...], target_ref[...])

                @pl.when(i == 0)
                def _():
                    dg_ref[...] = part_g
                    loss_ref[...] = jnp.broadcast_to(part_loss, loss_ref.shape)

                @pl.when(i > 0)
                def _():
                    dg_ref[...] += part_g
                    loss_ref[...] += jnp.broadcast_to(part_loss, loss_ref.shape)

    row = pl.BlockSpec((tm, D_MODEL), lambda i, j: (i, 0))
    vec = pl.BlockSpec((1, D_MODEL), lambda i, j: (0, 0))
    tile = pl.BlockSpec((tm, tf), lambda i, j: (i, j))
    weights = pl.BlockSpec((tf, D_MODEL), lambda i, j: (j, 0))
    tail_specs = ([] if next_gain is None else [row]) + ([] if head is None else [vec, vec])
    tail_shapes = ([] if next_gain is None else [jax.ShapeDtypeStruct((t, D_MODEL), BF16)]) + (
        [] if head is None else [jax.ShapeDtypeStruct((1, D_MODEL), F32)] * 2)
    return pl.pallas_call(
        body,
        name=name,
        grid=(t // tm, nj),
        in_specs=[
            row, vec, weights, weights,
            pl.BlockSpec((tf, D_MODEL), lambda i, j: (jnp.maximum(j - 1, 0), 0)),
            pl.BlockSpec((tf, D_MODEL), lambda i, j: (nj - 1, 0)),
        ] + ([] if next_gain is None else [vec]) + ([] if head is None else [vec, row]),
        out_specs=[row, tile, tile, row, tile] + tail_specs,
        out_shape=[
            jax.ShapeDtypeStruct((t, D_MODEL), F32),
            jax.ShapeDtypeStruct((t, D_FF), BF16),
            jax.ShapeDtypeStruct((t, D_FF), BF16),
            jax.ShapeDtypeStruct((t, D_MODEL), BF16),
            jax.ShapeDtypeStruct((t, D_FF), BF16),
        ] + tail_shapes,
        scratch_shapes=[pltpu.VMEM((tm, D_MODEL), F32), pltpu.VMEM((tm, tf), BF16)],
        compiler_params=_params("arbitrary", "arbitrary"),
    )(x, gain, wgt, wut, wd, wd, *extra_in)


def _ffn_bwd(dout, x, gain, a, b, wgt, wut, wd, *, name, tm=1024, tf=256):
    t = x.shape[0]
    nj = D_FF // tf

    def body(do_ref, x_ref, g_ref, a_ref, b_ref, wg_prev_ref, wu_prev_ref, wg_last_ref, wu_last_ref, wd_ref,
             dx_ref, dg_ref, da_ref, db_ref, dob_ref, dob_scr, dh, da_prev, db_prev):
        i = pl.program_id(0)
        j = pl.program_id(1)

        @pl.when(j == 0)
        def _():
            d = (0.5 * do_ref[...]).astype(BF16)
            dob_scr[...] = d
            dob_ref[...] = d
            dh[...] = jnp.zeros_like(dh)
            da_prev[...] = jnp.zeros_like(da_prev)
            db_prev[...] = jnp.zeros_like(db_prev)

        dh[...] += _dot(da_prev[...], wg_prev_ref[...]) + _dot(db_prev[...], wu_prev_ref[...])
        ds = _dot_nt(dob_scr[...], wd_ref[...])
        av = a_ref[...].astype(F32)
        bv = b_ref[...].astype(F32)
        sig = _sigmoid(av)
        dbv = (ds * (av * sig)).astype(BF16)
        dav = (ds * bv * (sig * (1.0 + av * (1.0 - sig)))).astype(BF16)
        da_ref[...] = dav
        db_ref[...] = dbv
        da_prev[...] = dav
        db_prev[...] = dbv

        @pl.when(j == nj - 1)
        def _():
            xhat, rstd = _rms(x_ref[...])
            dhv = dh[...] + _dot(dav, wg_last_ref[...]) + _dot(dbv, wu_last_ref[...])
            part = jnp.sum(dhv * xhat, axis=0, keepdims=True)

            @pl.when(i == 0)
            def _():
                dg_ref[...] = part

            @pl.when(i > 0)
            def _():
                dg_ref[...] += part

            dxh = dhv * g_ref[...]
            dx_ref[...] = do_ref[...] + rstd * (dxh - xhat * jnp.mean(dxh * xhat, axis=-1, keepdims=True))

    return pl.pallas_call(
        body,
        name=name,
        grid=(t // tm, nj),
        in_specs=[
            pl.BlockSpec((tm, D_MODEL), lambda i, j: (i, 0)),
            pl.BlockSpec((tm, D_MODEL), lambda i, j: (i, 0)),
            pl.BlockSpec((1, D_MODEL), lambda i, j: (0, 0)),
            pl.BlockSpec((tm, tf), lambda i, j: (i, j)),
            pl.BlockSpec((tm, tf), lambda i, j: (i, j)),
            pl.BlockSpec((tf, D_MODEL), lambda i, j: (jnp.maximum(j - 1, 0), 0)),
            pl.BlockSpec((tf, D_MODEL), lambda i, j: (jnp.maximum(j - 1, 0), 0)),
            pl.BlockSpec((tf, D_MODEL), lambda i, j: (nj - 1, 0)),
            pl.BlockSpec((tf, D_MODEL), lambda i, j: (nj - 1, 0)),
            pl.BlockSpec((tf, D_MODEL), lambda i, j: (j, 0)),
        ],
        out_specs=[
            pl.BlockSpec((tm, D_MODEL), lambda i, j: (i, 0)),
            pl.BlockSpec((1, D_MODEL), lambda i, j: (0, 0)),
            pl.BlockSpec((tm, tf), lambda i, j: (i, j)),
            pl.BlockSpec((tm, tf), lambda i, j: (i, j)),
            pl.BlockSpec((tm, D_MODEL), lambda i, j: (i, 0)),
        ],
        out_shape=[
            jax.ShapeDtypeStruct((t, D_MODEL), F32),
            jax.ShapeDtypeStruct((1, D_MODEL), F32),
            jax.ShapeDtypeStruct((t, D_FF), BF16),
            jax.ShapeDtypeStruct((t, D_FF), BF16),
            jax.ShapeDtypeStruct((t, D_MODEL), BF16),
        ],
        scratch_shapes=[pltpu.VMEM((tm, D_MODEL), BF16), pltpu.VMEM((tm, D_MODEL), F32), pltpu.VMEM((tm, tf), BF16),
                        pltpu.VMEM((tm, tf), BF16)],
        compiler_params=_params("arbitrary", "arbitrary", vmem_limit_bytes=FFN_BWD_VMEM_LIMIT_BYTES),
    )(dout, x, gain, a, b, wgt, wut, wgt, wut, wd)


def _ffn_bwd_act(dout_half, a, b, wd, *, name, tie, tm=512, tf=D_FF // 2):
    t = a.shape[0]

    def body(dob_ref, a_ref, b_ref, wd_ref, tie_ref, da_ref, db_ref):
        ds = _dot_nt(dob_ref[...], wd_ref[...])
        av = a_ref[...].astype(F32)
        bv = b_ref[...].astype(F32)
        sig = _sigmoid(av)
        db_ref[...] = (ds * (av * sig)).astype(BF16)
        da_ref[...] = (ds * bv * (sig * (1.0 + av * (1.0 - sig)))).astype(BF16)

    tile = pl.BlockSpec((tm, tf), lambda i, j: (i, j))
    return pl.pallas_call(
        body,
        name=name,
        grid=(t // tm, D_FF // tf),
        in_specs=[pl.BlockSpec((tm, D_MODEL), lambda i, j: (i, 0)), tile, tile,
                  pl.BlockSpec((tf, D_MODEL), lambda i, j: (j, 0)), pl.BlockSpec(memory_space=pl.ANY)],
        out_specs=[tile, tile],
        out_shape=[jax.ShapeDtypeStruct((t, D_FF), BF16)] * 2,
        compiler_params=_params("parallel", "parallel"),
    )(dout_half, a, b, wd, tie)


def _ffn_bwd_in(dout, x, gain, da, db, wgt, wut, *, name, tm=512):
    t = x.shape[0]

    def body(do_ref, x_ref, g_ref, da_ref, db_ref, wg_ref, wu_ref, dx_ref, dg_ref):
        i = pl.program_id(0)
        dhv = _dot(da_ref[...], wg_ref[...]) + _dot(db_ref[...], wu_ref[...])
        xhat, rstd = _rms(x_ref[...])
        part = jnp.sum(dhv * xhat, axis=0, keepdims=True)

        @pl.when(i == 0)
        def _():
            dg_ref[...] = part

        @pl.when(i > 0)
        def _():
            dg_ref[...] += part

        dxh = dhv * g_ref[...]
        dx_ref[...] = do_ref[...] + rstd * (dxh - xhat * jnp.mean(dxh * xhat, axis=-1, keepdims=True))

    row = pl.BlockSpec((tm, D_MODEL), lambda i: (i, 0))
    vec = pl.BlockSpec((1, D_MODEL), lambda i: (0, 0))
    tile = pl.BlockSpec((tm, D_FF), lambda i: (i, 0))
    weights = pl.BlockSpec((D_FF, D_MODEL), lambda i: (0, 0), pipeline_mode=pl.Buffered(1))
    return pl.pallas_call(
        body,
        name=name,
        grid=(t // tm,),
        in_specs=[row, row, vec, tile, tile, weights, weights],
        out_specs=[row, vec],
        out_shape=[jax.ShapeDtypeStruct((t, D_MODEL), F32), jax.ShapeDtypeStruct((1, D_MODEL), F32)],
        compiler_params=_params("arbitrary", vmem_limit_bytes=FFN_BWD_VMEM_LIMIT_BYTES),
    )(dout, x, gain, da, db, wgt, wut)


def _in_proj_bwd(dproj, w_int, x, gain, dres, *, name, tm=512):
    t, k = dproj.shape

    def body(dp_ref, w_ref, x_ref, g_ref, dr_ref, dx_ref, dg_ref, dxb_ref):
        i = pl.program_id(0)
        dhv = _dot(dp_ref[...], w_ref[...])
        xhat, rstd = _rms(x_ref[...])
        part = jnp.sum(dhv * xhat, axis=0, keepdims=True)

        @pl.when(i == 0)
        def _():
            dg_ref[...] = part

        @pl.when(i > 0)
        def _():
            dg_ref[...] += part

        dxh = dhv * g_ref[...]
        dx = dr_ref[...] + rstd * (dxh - xhat * jnp.mean(dxh * xhat, axis=-1, keepdims=True))
        dx_ref[...] = dx
        dxb_ref[...] = (0.5 * dx).astype(BF16)

    row = pl.BlockSpec((tm, D_MODEL), lambda i: (i, 0))
    vec = pl.BlockSpec((1, D_MODEL), lambda i: (0, 0))
    return pl.pallas_call(
        body,
        name=name,
        grid=(t // tm,),
        in_specs=[pl.BlockSpec((tm, k), lambda i: (i, 0)), pl.BlockSpec((k, D_MODEL), lambda i: (0, 0)), row, vec, row],
        out_specs=[row, vec, row],
        out_shape=[jax.ShapeDtypeStruct((t, D_MODEL), F32), jax.ShapeDtypeStruct((1, D_MODEL), F32),
                   jax.ShapeDtypeStruct((t, D_MODEL), BF16)],
        compiler_params=_params("arbitrary"),
    )(dproj, w_int, x, gain, dres)


ATT_Q_TILE = 512
ATT_K_BLOCK = 256


def _first_head_lanes():
    return lax.broadcasted_iota(jnp.int32, (1, LANES), 1) < SB_HEAD_DIM


def _stack_heads(x):
    first = _first_head_lanes()
    return jnp.concatenate([jnp.where(first, x, 0.0), jnp.where(first, 0.0, x)], axis=0)


def _unstack_heads(x, rows):
    return jnp.where(_first_head_lanes(), x[:rows], x[rows:])


def _rows_from(x, first, rows):
    return x if first == 0 else jnp.concatenate([x[first:rows], x[rows + first:]], axis=0)


def _rows_into(full, part, first, rows):
    if first == 0:
        return part
    n = rows - first
    return jnp.concatenate([full[:first], part[:n], full[rows:rows + first], part[n:]], axis=0)


def _tri(n, relation):
    r = lax.broadcasted_iota(jnp.int32, (n, n), 0)
    c = lax.broadcasted_iota(jnp.int32, (n, n), 1)
    return relation(r, c).astype(BF16)


def _scan_dot(x, tri):
    hi = x.astype(BF16)
    lo = (x - hi.astype(F32)).astype(BF16)
    return _dot(jnp.concatenate([hi, lo], axis=1), jnp.concatenate([tri, tri], axis=0))


def _log_terms(z):
    lbeta = jnp.minimum(z, 0.0) - jnp.log(1.0 + jnp.exp(-jnp.abs(z)))
    return lbeta, lbeta - z


def _attn_fwd(proj, *, name):
    t = proj.shape[0]
    tq, tk = ATT_Q_TILE, ATT_K_BLOCK
    diag = tq // tk
    n_pairs = SB_WIDTH // LANES

    def body(q_ref, k_ref, v_ref, o_ref, kept_ref):
        qi = pl.program_id(1)
        q = q_ref[...] * (SB_HEAD_DIM ** -0.5)
        qs = _stack_heads(q).astype(BF16)
        tri = _tri(tk, lambda j, s: j > s)
        trow = lax.broadcasted_iota(jnp.int32, (tq, tk), 0)
        scol = lax.broadcasted_iota(jnp.int32, (tq, tk), 1)

        def block(kb, carry, causal, first=0):
            acc, c = carry
            off = pl.multiple_of(kb * tk, tk)
            lbeta, lrest = _log_terms(_dot_nt(_rows_from(qs, first, tq), k_ref[pl.ds(off, tk), :].astype(BF16)))
            if causal is not None:
                lrest = jnp.where(causal, lrest, 0.0)
            w = jnp.exp(lbeta + (_scan_dot(lrest, tri) + _rows_from(c, first, tq)))
            if causal is not None:
                w = jnp.where(causal, w, 0.0)
            wb = w.astype(BF16)
            kept_ref[0, 0, kb] = _rows_into(jnp.zeros((2 * tq, tk), BF16), wb, first, tq)
            acc = _rows_into(acc, _rows_from(acc, first, tq) + _dot(wb, v_ref[pl.ds(off, tk), :].astype(BF16)), first, tq)
            return acc, _rows_into(c, _rows_from(c, first, tq) + jnp.sum(lrest, axis=1, keepdims=True), first, tq)

        carry = (jnp.zeros((2 * tq, LANES), F32), jnp.zeros((2 * tq, 1), F32))
        n_full = qi * diag
        for j in reversed(range(diag)):
            mask = ((scol + j * tk) < trow)[j * tk:]
            carry = block(n_full + j, carry, jnp.concatenate([mask, mask], axis=0), first=j * tk)

        def odd_tile(carry):
            for j in range(diag):
                carry = block(n_full - 1 - j, carry, None)
            return carry

        carry = lax.cond(qi % 2 == 1, odd_tile, lambda c: c, carry)
        last = n_full - 1 - (qi % 2) * diag

        def step(it, carry):
            for j in range(2 * diag):
                carry = block(last - (2 * diag * it + j), carry, None)
            return carry

        acc, _ = lax.fori_loop(0, qi // 2, step, carry)
        o_ref[...] = _unstack_heads(acc, tq)

    return pl.pallas_call(
        body,
        name=name,
        grid=(n_pairs, t // tq),
        in_specs=[
            pl.BlockSpec((tq, LANES), lambda p, i: (i, p)),
            pl.BlockSpec((t, LANES), lambda p, i: (0, n_pairs + p)),
            pl.BlockSpec((t, LANES), lambda p, i: (0, 2 * n_pairs + p)),
        ],
        out_specs=[pl.BlockSpec((tq, LANES), lambda p, i: (i, p)),
                   pl.BlockSpec((1, 1, t // tk, 2 * tq, tk), lambda p, i: (p, i, 0, 0, 0))],
        out_shape=[jax.ShapeDtypeStruct((t, SB_WIDTH), F32),
                   jax.ShapeDtypeStruct((n_pairs, t // tq, t // tk, 2 * tq, tk), BF16)],
        compiler_params=_params("parallel", "parallel"),
    )(proj, proj, proj)


def _attn_bwd(proj, kept, do, *, name, tie=None):
    t = proj.shape[0]
    tq, tk = ATT_Q_TILE, ATT_K_BLOCK
    diag = tq // tk
    n_pairs = SB_WIDTH // LANES
    scale = SB_HEAD_DIM ** -0.5

    def body(q_ref, k_ref, v_ref, kept_ref, do_ref, *rest):
        dq_ref, dk_ref, dv_ref = rest[-3:]
        qi = pl.program_id(1)

        @pl.when(qi == 0)
        def _():
            dk_ref[...] = jnp.zeros_like(dk_ref)
            dv_ref[...] = jnp.zeros_like(dv_ref)

        qs = _stack_heads(q_ref[...] * scale).astype(BF16)
        dos = _stack_heads(do_ref[...]).astype(BF16)
        before = _tri(tk, lambda s, j: s < j)
        trow = lax.broadcasted_iota(jnp.int32, (tq, tk), 0)
        scol = lax.broadcasted_iota(jnp.int32, (tq, tk), 1)

        def block(kb, carry, causal, first=0):
            dq, cg = carry
            off = pl.multiple_of(kb * tk, tk)
            q_rows, do_rows = _rows_from(qs, first, tq), _rows_from(dos, first, tq)
            wb = _rows_from(kept_ref[0, 0, kb], first, tq)
            kblk = k_ref[pl.ds(off, tk), :].astype(BF16)
            sig = _sigmoid(_dot_nt(q_rows, kblk))
            g = wb.astype(F32) * _dot_nt(do_rows, v_ref[pl.ds(off, tk), :].astype(BF16))
            prior = _scan_dot(g, before) + _rows_from(cg, first, tq)
            dz = g - sig * (g + prior)
            if causal is not None:
                dz = jnp.where(causal, dz, 0.0)
            dzb = dz.astype(BF16)
            dq = _rows_into(dq, _rows_from(dq, first, tq) + _dot(dzb, kblk), first, tq)
            dk_ref[pl.ds(off, tk), :] += _dot_tn(dzb, q_rows)
            dv_ref[pl.ds(off, tk), :] += _dot_tn(wb, do_rows)
            return dq, _rows_into(cg, _rows_from(cg, first, tq) + jnp.sum(g, axis=1, keepdims=True), first, tq)

        n_full = qi * diag

        def step(it, carry):
            for j in range(2 * diag):
                carry = block(2 * diag * it + j, carry, None)
            return carry

        def odd_tile(carry):
            for j in range(diag):
                carry = block(n_full - diag + j, carry, None)
            return carry

        carry = lax.fori_loop(0, qi // 2, step, (jnp.zeros((2 * tq, LANES), F32), jnp.zeros((2 * tq, 1), F32)))
        carry = lax.cond(qi % 2 == 1, odd_tile, lambda c: c, carry)
        for j in range(diag):
            mask = ((scol + j * tk) < trow)[j * tk:]
            carry = block(n_full + j, carry, jnp.concatenate([mask, mask], axis=0), first=j * tk)
        dq_ref[...] = (_unstack_heads(carry[0], tq) * scale).astype(BF16)

    tile_spec = pl.BlockSpec((tq, LANES), lambda p, i: (i, p))
    full_spec = pl.BlockSpec((t, LANES), lambda p, i: (0, p))
    return pl.pallas_call(
        body,
        name=name,
        grid=(n_pairs, t // tq),
        in_specs=[
            tile_spec,
            pl.BlockSpec((t, LANES), lambda p, i: (0, n_pairs + p)),
            pl.BlockSpec((t, LANES), lambda p, i: (0, 2 * n_pairs + p)),
            pl.BlockSpec((1, 1, t // tk, 2 * tq, tk), lambda p, i: (p, i, 0, 0, 0)),
            tile_spec,
        ] + ([] if tie is None else [pl.BlockSpec(memory_space=pl.ANY)]),
        out_specs=[tile_spec, full_spec, full_spec],
        out_shape=[jax.ShapeDtypeStruct((t, SB_WIDTH), BF16)] + [jax.ShapeDtypeStruct((t, SB_WIDTH), F32)] * 2,
        compiler_params=_params("arbitrary", "arbitrary"),
    )(proj, proj, proj, kept, do, *([] if tie is None else [tie]))


HG_BLOCK = 128
HG_HEADS = HG_WIDTH // HG_HEAD_DIM


def _chunk_mats(n):
    r = lax.broadcasted_iota(jnp.int32, (n, n), 0)
    c = lax.broadcasted_iota(jnp.int32, (n, n), 1)
    same = (r // HG_CHUNK) == (c // HG_CHUNK)
    upto = (same & (c <= r)).astype(BF16)
    whole = same.astype(BF16)
    onward = (same & (c >= r)).astype(BF16)
    return upto, whole, onward


def _rows_dot(mat, x):
    return _dot(jnp.concatenate([mat, mat, mat], axis=1), jnp.concatenate(_split3(x), axis=0))


def _split_heads(x):
    return jnp.stack([x[:, h * HG_HEAD_DIM:(h + 1) * HG_HEAD_DIM] for h in range(HG_HEADS)], axis=0)


def _merge_heads(x):
    return jnp.concatenate([x[h] for h in range(HG_HEADS)], axis=1)


def _lower_bound(lg_ref):
    lg = lg_ref[...]
    return _sigmoid(lg[0:1, :] - lg[1:2, :])


def _hgrn_prepare(q_ref, f_ref, lb, h, upto, whole):
    cols = slice(h * HG_HEAD_DIM, (h + 1) * HG_HEAD_DIM)
    lbh = lb[:, cols]
    sg = _sigmoid(f_ref[:, cols])
    forget = lbh + (1.0 - lbh) * sg
    logf = jnp.log(forget)
    kk = (1.0 - lbh) * (1.0 - sg)
    qv = q_ref[:, cols]
    qsig = _sigmoid(qv)
    qh = qv * qsig
    b = _rows_dot(upto, logf)
    blast = _rows_dot(whole, logf)
    return dict(lbh=lbh, sg=sg, forget=forget, kk=kk, qv=qv, qsig=qsig, qh=qh, b=b, eb=jnp.exp(b),
                ekb=jnp.exp(blast - b), dl=jnp.exp(blast))


def _hgrn_fwd(proj, logits, *, name):
    t = proj.shape[0]
    tb = HG_BLOCK
    nc = tb // HG_CHUNK
    hd = HG_HEAD_DIM

    def body(q_ref, f_ref, i_ref, lg_ref, o_ref, st_ref, state, qh_s, kk_s, b_s, qe_s, ke_s, dl_s):
        @pl.when(pl.program_id(0) == 0)
        def _():
            state[...] = jnp.zeros_like(state)

        lb = _lower_bound(lg_ref)
        upto, whole, _ = _chunk_mats(tb)
        for h in range(HG_HEADS):
            p = _hgrn_prepare(q_ref, f_ref, lb, h, upto, whole)
            qh_s[h] = p["qh"]
            kk_s[h] = p["kk"]
            b_s[h] = p["b"]
            qe_s[h] = (p["qh"] * p["eb"]).astype(BF16)
            ke_s[h] = (p["kk"] * p["ekb"]).astype(BF16)
            dl_s[h] = p["dl"]
        rowi = lax.broadcasted_iota(jnp.int32, (HG_HEADS, HG_CHUNK, hd), 1)

        def chunk(c, _):
            r0 = pl.multiple_of(c * HG_CHUNK, HG_CHUNK)
            rows = pl.ds(r0, HG_CHUNK)
            bc = b_s[:, rows, :]
            qc = qh_s[:, rows, :]
            kc = kk_s[:, rows, :]
            vc = _split_heads(i_ref[rows, :])
            s_in = state[...]
            st_ref[c] = s_in
            s_in_b = s_in.astype(BF16)
            qe = qe_s[:, rows, :]
            o = jnp.stack([_dot_nt(qe[h], s_in_b[h]) for h in range(HG_HEADS)], axis=0)
            for s in range(HG_CHUNK):
                pair = jnp.where(rowi >= s, qc * jnp.exp(bc - bc[:, s:s + 1, :]) * kc[:, s:s + 1, :], 0.0)
                o = o + jnp.sum(pair, axis=2, keepdims=True) * vc[:, s:s + 1, :]
            o_ref[rows, :] = _merge_heads(o)
            vcb = vc.astype(BF16)
            ke = ke_s[:, rows, :]
            update = jnp.stack([_dot_tn(vcb[h], ke[h]) for h in range(HG_HEADS)], axis=0)
            state[...] = s_in * dl_s[:, pl.ds(r0, 1), :] + update
            return 0

        lax.fori_loop(0, nc, chunk, 0, unroll=4)

    blk =lambda col: pl.BlockSpec((tb, HG_WIDTH), lambda i: (i, col))
    head_f32 = pltpu.VMEM((HG_HEADS, tb, hd), F32)
    head_bf16 = pltpu.VMEM((HG_HEADS, tb, hd), BF16)
    return pl.pallas_call(
        body,
        name=name,
        grid=(t // tb,),
        in_specs=[blk(3), blk(4), blk(5), pl.BlockSpec((2, HG_WIDTH), lambda i: (0, 0))],
        out_specs=[
            pl.BlockSpec((tb, HG_WIDTH), lambda i: (i, 0)),
            pl.BlockSpec((nc, HG_HEADS, hd, hd), lambda i: (i, 0, 0, 0)),
        ],
        out_shape=[
            jax.ShapeDtypeStruct((t, HG_WIDTH), F32),
            jax.ShapeDtypeStruct((t // HG_CHUNK, HG_HEADS, hd, hd), F32),
        ],
        scratch_shapes=[pltpu.VMEM((HG_HEADS, hd, hd), F32), head_f32, head_f32, head_f32, head_bf16, head_bf16,
                        head_f32],
        compiler_params=_params("arbitrary"),
    )(proj, proj, proj, logits)


def _hgrn_bwd(proj, logits, states, do, *, name):
    t = proj.shape[0]
    tb = HG_BLOCK
    nb = t // tb
    nc = tb // HG_CHUNK
    hd = HG_HEAD_DIM

    def body(q_ref, f_ref, i_ref, lg_ref, st_ref, do_ref, dq_ref, df_ref, di_ref, dlb_ref,
             dstate, qh_s, kk_s, b_s, eb_s, ekb_s, qe_s, ke_s, dl_s, dqh_s, dkk_s, dlf_s):
        step = pl.program_id(0)

        @pl.when(step == 0)
        def _():
            dstate[...] = jnp.zeros_like(dstate)
            dlb_ref[...] = jnp.zeros_like(dlb_ref)

        lb = _lower_bound(lg_ref)
        upto, whole, _ = _chunk_mats(tb)
        prepared = []
        for h in range(HG_HEADS):
            p = _hgrn_prepare(q_ref, f_ref, lb, h, upto, whole)
            prepared.append(p)
            qh_s[h] = p["qh"]
            kk_s[h] = p["kk"]
            b_s[h] = p["b"]
            eb_s[h] = p["eb"]
            ekb_s[h] = p["ekb"]
            qe_s[h] = (p["qh"] * p["eb"]).astype(BF16)
            ke_s[h] = (p["kk"] * p["ekb"]).astype(BF16)
            dl_s[h] = p["dl"]
        rowi = lax.broadcasted_iota(jnp.int32, (HG_CHUNK, hd), 0)
        r16 = lax.broadcasted_iota(jnp.int32, (HG_CHUNK, HG_CHUNK), 0)
        c16 = lax.broadcasted_iota(jnp.int32, (HG_CHUNK, HG_CHUNK), 1)
        onward = (c16 >= r16).astype(BF16)

        def chunk(it, _):
            c = nc - 1 - it
            r0 = pl.multiple_of(c * HG_CHUNK, HG_CHUNK)
            rows = pl.ds(r0, HG_CHUNK)
            for h in range(HG_HEADS):
                cols = slice(h * hd, (h + 1) * hd)
                bc = b_s[h, rows, :]
                qc = qh_s[h, rows, :]
                kc = kk_s[h, rows, :]
                vc = i_ref[rows, cols]
                doc = do_ref[rows, cols]
                s_in = st_ref[c, h]
                ds_out = dstate[h]
                ds_out_b = ds_out.astype(BF16)
                docb = doc.astype(BF16)
                dl_row = dl_s[h, pl.ds(r0, 1), :]
                dqh = _dot(docb, s_in.astype(BF16)) * eb_s[h, rows, :]
                dkk = _dot(vc.astype(BF16), ds_out_b) * ekb_s[h, rows, :]
                dv = _dot_nt(ke_s[h, rows, :], ds_out_b)
                db = dqh * qc - dkk * kc
                dwhole = jnp.sum(dkk * kc, axis=0, keepdims=True) + jnp.sum(ds_out * s_in, axis=0, keepdims=True) * dl_row
                dk_rows, dv_rows = [], []
                for s in range(HG_CHUNK):
                    keep = rowi >= s
                    e = jnp.exp(bc - bc[s:s + 1, :])
                    k_row = kc[s:s + 1, :]
                    pcol = jnp.sum(jnp.where(keep, qc * e * k_row, 0.0), axis=1, keepdims=True)
                    dpcol = jnp.sum(doc * vc[s:s + 1, :], axis=1, keepdims=True)
                    m = jnp.where(keep, e * dpcol, 0.0)
                    y = m * qc
                    dqh = dqh + m * k_row
                    db = db + y * k_row
                    dk_rows.append(jnp.sum(y, axis=0, keepdims=True))
                    dv_rows.append(jnp.sum(pcol * doc, axis=0, keepdims=True))
                dkk_pairs = jnp.concatenate(dk_rows, axis=0)
                dkk = dkk + dkk_pairs
                db = db - dkk_pairs * kc
                dv = dv + jnp.concatenate(dv_rows, axis=0)
                dqh_s[h, rows, :] = dqh
                dkk_s[h, rows, :] = dkk
                dlf_s[h, rows, :] = _rows_dot(onward, db) + dwhole
                di_ref[rows, cols] = dv.astype(BF16)
                dstate[h] = ds_out * dl_row + _dot_tn(docb, qe_s[h, rows, :])
            return 0

        lax.fori_loop(0, nc, chunk, 0, unroll=4)
        for h in range(HG_HEADS):
            cols = slice(h * hd, (h + 1) * hd)
            p = prepared[h]
            dq_ref[:, cols] = (dqh_s[h] * (p["qsig"] * (1.0 + p["qv"] * (1.0 - p["qsig"])))).astype(BF16)
            dforget = dlf_s[h] / p["forget"] - dkk_s[h]
            df_ref[:, cols] = (dforget * (1.0 - p["lbh"]) * p["sg"] * (1.0 - p["sg"])).astype(BF16)
            dlb_ref[:, cols] += jnp.sum(dforget * (1.0 - p["sg"]), axis=0, keepdims=True)

    blk = lambda col: pl.BlockSpec((tb, HG_WIDTH), lambda i: (nb - 1 - i, col))
    vec = pl.BlockSpec((1, HG_WIDTH), lambda i: (0, 0))
    head_f32 = pltpu.VMEM((HG_HEADS, tb, hd), F32)
    head_bf16 = pltpu.VMEM((HG_HEADS, tb, hd), BF16)
    return pl.pallas_call(
        body,
        name=name,
        grid=(nb,),
        in_specs=[
            blk(3), blk(4), blk(5),
            pl.BlockSpec((2, HG_WIDTH), lambda i: (0, 0)),
            pl.BlockSpec((nc, HG_HEADS, hd, hd), lambda i: (nb - 1 - i, 0, 0, 0)),
            blk(0),
        ],
        out_specs=[blk(0), blk(0), blk(0), vec],
        out_shape=[jax.ShapeDtypeStruct((t, HG_WIDTH), BF16)] * 3 + [jax.ShapeDtypeStruct((1, HG_WIDTH), F32)],
        scratch_shapes=[
            pltpu.VMEM((HG_HEADS, hd, hd), F32),
            head_f32, head_f32, head_f32, head_f32, head_f32, head_bf16, head_bf16, head_f32,
            head_f32, head_f32, head_f32,
        ],
        compiler_params=_params("arbitrary"),
    )(proj, proj, proj, logits, states, do)


def _group_mat(width, head_dim):
    r = lax.broadcasted_iota(jnp.int32, (width, width), 0)
    c = lax.broadcasted_iota(jnp.int32, (width, width), 1)
    return ((r // head_dim) == (c // head_dim)).astype(BF16)


def _head_mean(x, mat, head_dim):
    hi = x.astype(BF16)
    lo = (x - hi.astype(F32)).astype(BF16)
    return (_dot(hi, mat) + _dot(lo, mat)) * (1.0 / head_dim)


def _mix_out_fwd(o_sb, o_hg, proj, g_sb, g_hg, w_out, x1, *, name, tm=512):
    t = x1.shape[0]

    def body(osb_ref, ohg_ref, gate_ref, gsb_ref, ghg_ref, w_ref, x_ref, xo_ref, mt_ref):
        msb = _group_mat(SB_WIDTH, SB_HEAD_DIM)
        mhg = _group_mat(HG_WIDTH, HG_HEAD_DIM)
        osb = osb_ref[...]
        ohg = ohg_ref[...]
        nsb = osb * lax.rsqrt(_head_mean(osb * osb, msb, SB_HEAD_DIM) + EPS) * gsb_ref[...]
        gate = gate_ref[...]
        nhg = ohg * lax.rsqrt(_head_mean(ohg * ohg, mhg, HG_HEAD_DIM) + EPS) * ghg_ref[...] * (gate * _sigmoid(gate))
        mixed = jnp.concatenate([nsb, nhg], axis=1).astype(BF16)
        mt_ref[...] = mixed
        xo_ref[...] = x_ref[...] + _dot(mixed, w_ref[...])

    half = pl.BlockSpec((tm, SB_WIDTH), lambda i: (i, 0))
    vec = pl.BlockSpec((1, SB_WIDTH), lambda i: (0, 0))
    row = pl.BlockSpec((tm, D_MODEL), lambda i: (i, 0))
    return pl.pallas_call(
        body,
        name=name,
        grid=(t // tm,),
        in_specs=[half, half, pl.BlockSpec((tm, HG_WIDTH), lambda i: (i, 6)), vec, vec,
                  pl.BlockSpec((D_MODEL, D_MODEL), lambda i: (0, 0)), row],
        out_specs=[row, row],
        out_shape=[jax.ShapeDtypeStruct((t, D_MODEL), F32), jax.ShapeDtypeStruct((t, D_MODEL), BF16)],
        compiler_params=_params("parallel"),
    )(o_sb, o_hg, proj, g_sb, g_hg, w_out, x1)


def _mix_out_bwd(dx2, o_sb, o_hg, proj, g_sb, g_hg, w_out, *, name, tm=512):
    t = dx2.shape[0]

    def body(dx_ref, osb_ref, ohg_ref, gate_ref, gsb_ref, ghg_ref, w_ref, dosb_ref, dohg_ref, dgate_ref, dgsb_ref,
             dghg_ref, dxb_ref):
        i = pl.program_id(0)
        msb = _group_mat(SB_WIDTH, SB_HEAD_DIM)
        mhg = _group_mat(HG_WIDTH, HG_HEAD_DIM)
        dxb = dx_ref[...].astype(BF16)
        dxb_ref[...] = dxb
        dmixed = _dot_nt(dxb, w_ref[...])
        dnsb = dmixed[:, :SB_WIDTH]
        dy = dmixed[:, SB_WIDTH:]

        osb = osb_ref[...]
        rstd = lax.rsqrt(_head_mean(osb * osb, msb, SB_HEAD_DIM) + EPS)
        ohat = osb * rstd
        part_sb = jnp.sum(dnsb * ohat, axis=0, keepdims=True)
        dohat = dnsb * gsb_ref[...]
        dosb_ref[...] = rstd * (dohat - ohat * _head_mean(dohat * ohat, msb, SB_HEAD_DIM))

        ohg = ohg_ref[...]
        rstd = lax.rsqrt(_head_mean(ohg * ohg, mhg, HG_HEAD_DIM) + EPS)
        ohat = ohg * rstd
        gate = gate_ref[...]
        sig = _sigmoid(gate)
        dn = dy * (gate * sig)
        dgate_ref[...] = (dy * (ohat * ghg_ref[...]) * (sig * (1.0 + gate * (1.0 - sig)))).astype(BF16)
        part_hg = jnp.sum(dn * ohat, axis=0, keepdims=True)
        dohat = dn * ghg_ref[...]
        dohg_ref[...] = rstd * (dohat - ohat * _head_mean(dohat * ohat, mhg, HG_HEAD_DIM))

        @pl.when(i == 0)
        def _():
            dgsb_ref[...] = part_sb
            dghg_ref[...] = part_hg

        @pl.when(i > 0)
        def _():
            dgsb_ref[...] += part_sb
            dghg_ref[...] += part_hg

    half = pl.BlockSpec((tm, SB_WIDTH), lambda i: (i, 0))
    vec = pl.BlockSpec((1, SB_WIDTH), lambda i: (0, 0))
    row = pl.BlockSpec((tm, D_MODEL), lambda i: (i, 0))
    return pl.pallas_call(
        body,
        name=name,
        grid=(t // tm,),
        in_specs=[row, half, half, pl.BlockSpec((tm, HG_WIDTH), lambda i: (i, 6)), vec, vec,
                  pl.BlockSpec((D_MODEL, D_MODEL), lambda i: (0, 0))],
        out_specs=[half, half, half, vec, vec, row],
        out_shape=[jax.ShapeDtypeStruct((t, SB_WIDTH), F32)] * 2 + [jax.ShapeDtypeStruct((t, SB_WIDTH), BF16)]
        + [jax.ShapeDtypeStruct((1, SB_WIDTH), F32)] * 2 + [jax.ShapeDtypeStruct((t, D_MODEL), BF16)],
        compiler_params=_params("arbitrary"),
    )(dx2, o_sb, o_hg, proj, g_sb, g_hg, w_out)


def _local_step(x, target, norms, logits, w, weights_after=None, grads_ready=None):
    w = dict(w)
    x1, a1, b1, h1, s1, hm = _ffn_fwd(x, norms["ffn1"], w["g1t"], w["u1t"], w["d1"], name="ffn1_fwd",
                                      next_gain=norms["mix"])
    if weights_after is not None:
        w.update(weights_after("ffn1", x1))
    proj = _mm(hm, w["int"], name="in_proj", tm=512, tn=IN_COLS, nt=True)
    o_sb, sb_kept = _attn_fwd(proj, name="sb_attn_fwd")
    o_hg, states = _hgrn_fwd(proj, logits, name="hgrn2_fwd")
    x2, mixed = _mix_out_fwd(o_sb, o_hg, proj, norms["sb"], norms["hg"], w["out"], x1, name="mix_out_fwd")
    if weights_after is not None:
        w.update(weights_after("mix", x2))
    dx3, a2, b2, h2, s2, d_final, loss_row = _ffn_fwd(x2, norms["ffn2"], w["g2t"], w["u2t"], w["d2"], name="ffn2_fwd",
                                                      head=(norms["final"], target))

    def weight_grad(lhs, rhs, name, tie=None):
        return _mm(lhs, rhs, name=name, tm=256, tn=D_MODEL, ta=True, out_dtype=BF16, tie=tie)

    def sent(stage):
        return grads_ready(stage, gw) if grads_ready is not None else None

    gw, gv = {}, {"final": d_final}
    dx2, gv["ffn2"], da2, db2, dob2 = _ffn_bwd(dx3, x2, norms["ffn2"], a2, b2, w["g2t"], w["u2t"], w["d2"],
                                               name="ffn2_bwd")
    gw["g2t"] = weight_grad(da2, h2, "ffn2_dgate")
    gw["u2t"] = weight_grad(db2, h2, "ffn2_dup")
    gw["d2"] = weight_grad(s2, dob2, "ffn2_ddown")

    do_sb, do_hg, d_gate, gv["sb"], gv["hg"], dx2b = _mix_out_bwd(
        dx2, o_sb, o_hg, proj, norms["sb"], norms["hg"], w["out"], name="mix_out_bwd")
    gw["out"] = weight_grad(mixed, dx2b, "out_dw")
    tie = sent("mix")
    dq_sb, dk_sb, dv_sb = _attn_bwd(proj, sb_kept, do_sb, name="sb_attn_bwd", tie=tie)
    dq_hg, df_hg, di_hg, d_lb = _hgrn_bwd(proj, logits if tie is None else logits + tie[0, 0], states, do_hg,
                                          name="hgrn2_bwd")
    dproj = jnp.concatenate([dq_sb, dk_sb.astype(BF16), dv_sb.astype(BF16), dq_hg, df_hg, di_hg, d_gate], axis=1)
    gw["int"] = weight_grad(dproj, hm, "in_dw")
    tie = sent("in")
    dx1, gv["mix"], dob1 = _in_proj_bwd(dproj, w["int"], x1, norms["mix"] if tie is None else norms["mix"] + tie[0, 0],
                                        dx2, name="in_dx")

    gw["d1"] = weight_grad(s1, dob1, "ffn1_ddown")
    tie = sent("d1")
    da1, db1 = _ffn_bwd_act(dob1, a1, b1, w["d1"], name="ffn1_bwd_act",
                            tie=jnp.zeros((8, LANES), F32) if tie is None else tie)
    gw["g1t"] = weight_grad(da1, h1, "ffn1_dgate")
    gw["u1t"] = weight_grad(db1, h1, "ffn1_dup", tie=sent("g1t"))
    tie = sent("u1t")
    dx, gv["ffn1"] = _ffn_bwd_in(dx1, x, norms["ffn1"] if tie is None else norms["ffn1"] + tie[0, 0], da1, db1,
                                 w["g1t"], w["u1t"], name="ffn1_bwd_in")
    gv["lb"] = d_lb
    return loss_row, dx, gw, gv


HBM = pl.BlockSpec(memory_space=pl.ANY)


def _place():
    return lax.axis_index("x"), lax.axis_index("y"), lax.axis_index("c")


def _slot(px, py, pc):
    return 4 * px + 2 * py + pc


GATHER_COPIES = 8


def _all_gather(blocks, *, name):
    n = len(blocks)

    def body(*refs):
        ins, outs = refs[:n], refs[n:2 * n]
        send_sems, recv_sems, local_sems = refs[2 * n:]
        x, y, c = _place()
        me, sibling = (x, y, c), (x, y, 1 - c)
        beside, across, diagonal = (1 - x, y, c), (x, 1 - y, c), (1 - x, 1 - y, c)

        def copy(a, k, block, to, src=None, half=None):
            dst = outs[a].at[_slot(*block)]
            if half is not None:
                rows = blocks[a].shape[0] // 2
                dst = dst.at[pl.ds(half * rows, rows)]
            return pltpu.make_async_remote_copy(
                src_ref=dst if src is None else src, dst_ref=dst, send_sem=send_sems.at[GATHER_COPIES * a + k],
                recv_sem=recv_sems.at[GATHER_COPIES * a + k], device_id=to, device_id_type=MESH)

        mine = [pltpu.make_async_copy(ins[a], outs[a].at[_slot(*me)], local_sems.at[a]) for a in range(n)]
        for cp in mine:
            cp.start()
        sent = []
        for a in range(n):
            sent += [copy(a, 0, me, sibling, src=ins[a]), copy(a, 1, me, beside, src=ins[a]),
                     copy(a, 2, me, across, src=ins[a])]
        for cp in sent:
            cp.start()
        for a in range(n):
            copy(a, 1, beside, me).wait_recv()
            sent += [copy(a, 3, beside, across, half=0), copy(a, 5, beside, sibling)]
            sent[-2].start()
            sent[-1].start()
        for a in range(n):
            copy(a, 2, across, me).wait_recv()
            sent += [copy(a, 4, across, beside, half=1), copy(a, 6, across, sibling)]
            sent[-2].start()
            sent[-1].start()
        for a in range(n):
            copy(a, 3, diagonal, me, half=0).wait_recv()
            copy(a, 4, diagonal, me, half=1).wait_recv()
            sent.append(copy(a, 7, diagonal, sibling))
            sent[-1].start()
        for a in range(n):
            for k, origin in ((0, sibling), (5, (1 - x, y, 1 - c)), (6, (x, 1 - y, 1 - c)), (7, (1 - x, 1 - y, 1 - c))):
                copy(a, k, origin, me).wait_recv()
        for cp in sent:
            cp.wait_send()
        for cp in mine:
            cp.wait()

    return pl.pallas_call(
        body,
        name=name,
        in_specs=[HBM] * n,
        out_specs=[HBM] * n,
        out_shape=[jax.ShapeDtypeStruct((N_DEV,) + b.shape, b.dtype) for b in blocks],
        scratch_shapes=[pltpu.SemaphoreType.DMA((GATHER_COPIES * n,)), pltpu.SemaphoreType.DMA((GATHER_COPIES * n,)),
                        pltpu.SemaphoreType.DMA((n,))],
    )(*blocks)


def _flipped(place, d):
    return tuple(1 - p if (d >> (2 - axis)) & 1 else p for axis, p in enumerate(place))


SEM = pl.BlockSpec(memory_space=pltpu.SEMAPHORE)
EFFECT = pltpu.SideEffectType.DATAFLOW_SIDE_EFFECTING


def _split_copies(me, srcs, lands, send_sems, recv_sems, by_owner):
    copies = []
    for d in range(1, N_DEV):
        peer = _flipped(me, d)
        for a, (src, land) in enumerate(zip(srcs, lands)):
            copies.append(pltpu.make_async_remote_copy(
                src_ref=src.at[_slot(*peer)] if by_owner else src, dst_ref=land.at[_slot(*me)],
                send_sem=send_sems.at[7 * a + d - 1], recv_sem=recv_sems.at[7 * a + d - 1], device_id=peer,
                device_id_type=MESH))
    own = [pltpu.make_async_copy(src.at[_slot(*me)] if by_owner else src, land.at[_slot(*me)],
                                 recv_sems.at[7 * len(srcs) + a]) for a, (src, land) in enumerate(zip(srcs, lands))]
    return copies, own


def _copies_start(srcs, *, name, by_owner, after=None):
    n = len(srcs)
    extra = [] if after is None else [after]
    land_shapes = [s.shape if by_owner else (N_DEV,) + s.shape for s in srcs]
    lands = [pltpu.with_memory_space_constraint(lax.empty(shape, s.dtype), pltpu.HBM) for shape, s in zip(land_shapes, srcs)]
    srcs = [pltpu.with_memory_space_constraint(s, pltpu.HBM) for s in srcs]

    def body(*refs):
        src_refs, land_refs = refs[:n], refs[n:2 * n]
        send_sems, recv_sems = refs[2 * n + len(extra)], refs[2 * n + len(extra) + 1]
        token = refs[-1]
        copies, own = _split_copies(_place(), src_refs, land_refs, send_sems, recv_sems, by_owner)
        for cp in copies + own:
            cp.start()
        token[...] = jnp.zeros_like(token)

    out = pl.pallas_call(
        body,
        name=name,
        in_specs=[HBM] * (2 * n + len(extra)),
        out_specs=[SEM, SEM] + [HBM] * (2 * n) + [pl.BlockSpec(memory_space=pltpu.VMEM)],
        out_shape=[pltpu.SemaphoreType.DMA((7 * n,)), pltpu.SemaphoreType.DMA((8 * n,))]
        + [pltpu.HBM(s.shape, s.dtype) for s in srcs] + [pltpu.HBM(shape, s.dtype) for shape, s in zip(land_shapes, srcs)]
        + [jax.ShapeDtypeStruct((8, LANES), F32)],
        input_output_aliases={i: 2 + i for i in range(2 * n)},
        compiler_params=pltpu.CompilerParams(has_side_effects=EFFECT),
    )(*srcs, *lands, *extra)
    return (out[0], out[1], out[2:2 + n], out[2 + n:2 + 2 * n]), out[-1]


def _copies_wait(started, after, *, name, by_owner):
    send_sems, recv_sems, srcs, lands = started
    n = len(srcs)

    def body(*refs):
        src_refs, land_refs = refs[:n], refs[n:2 * n]
        copies, own = _split_copies(_place(), src_refs, land_refs, refs[2 * n], refs[2 * n + 1], by_owner)
        for cp in copies:
            cp.wait_send()
            cp.wait_recv()
        for cp in own:
            cp.wait()

    out = pl.pallas_call(
        body,
        name=name,
        in_specs=[HBM] * (2 * n) + [SEM, SEM, HBM],
        out_specs=[HBM] * (2 * n),
        out_shape=[pltpu.HBM(s.shape, s.dtype) for s in srcs] + [pltpu.HBM(s.shape, s.dtype) for s in lands],
        input_output_aliases={i: i for i in range(2 * n)},
        compiler_params=pltpu.CompilerParams(has_side_effects=EFFECT),
    )(*srcs, *lands, send_sems, recv_sems, after)
    return out[:n], out[n:]


def _adamw(w, g, m, v):
    m = ADAM_B1 * m + (1.0 - ADAM_B1) * g
    v = ADAM_B2 * v + (1.0 - ADAM_B2) * (g * g)
    m_hat = m / (1.0 - ADAM_B1 ** ADAM_STEP)
    v_hat = v / (1.0 - ADAM_B2 ** ADAM_STEP)
    delta = -ADAM_LR * (m_hat / (jnp.sqrt(v_hat) + ADAM_EPS) + ADAM_WD * w)
    return delta, m, v


def _sum_and_update(parts, w, m, v, *, name, tie=None):
    _, rows, cols = w.shape
    tr = rows // 2

    def body(p_ref, w_ref, m_ref, v_ref, *rest):
        g_ref, d_ref, mo_ref, vo_ref = rest[-4:]
        g = p_ref[0].astype(F32)
        for s in range(1, N_DEV):
            g = g + p_ref[s].astype(F32)
        g_ref[0] = g
        d_ref[0], mo_ref[0], vo_ref[0] = _adamw(w_ref[0], g, m_ref[0], v_ref[0])

    flat = pl.BlockSpec((1, tr, cols), lambda i: (0, i, 0))
    return pl.pallas_call(
        body,
        name=name,
        grid=(rows // tr,),
        in_specs=[pl.BlockSpec((N_DEV, tr, cols), lambda i: (0, i, 0)), flat, flat, flat]
        + ([] if tie is None else [pl.BlockSpec(memory_space=pl.ANY)]),
        out_specs=[flat] * 4,
        out_shape=[jax.ShapeDtypeStruct((1, rows, cols), F32)] * 4,
        compiler_params=_params("parallel"),
    )(parts, w, m, v, *([] if tie is None else [tie]))


VEC_ROWS = 8
ROW_LOGITS, ROW_LOSS = 5, 7


def _vectors_update(part, w, m, v, *, name, tie):
    def body(p_ref, w_ref, m_ref, v_ref, tie_ref, g_ref, d_ref, mo_ref, vo_ref, loss_ref, all_ref, send_sems, recv_sems):
        me = _place()
        all_ref[_slot(*me)] = p_ref[...]
        copies = []
        for d in range(1, N_DEV):
            peer = _flipped(me, d)
            copies.append(pltpu.make_async_remote_copy(
                src_ref=p_ref, dst_ref=all_ref.at[_slot(*me)], send_sem=send_sems.at[d - 1], recv_sem=recv_sems.at[d - 1],
                device_id=peer, device_id_type=MESH))
        for cp in copies:
            cp.start()
        for cp in copies:
            cp.wait()
        total = all_ref[0]
        for s in range(1, N_DEV):
            total = total + all_ref[s]
        wv = w_ref[...]
        half = D_MODEL // 2
        lb = _sigmoid(wv[ROW_LOGITS:ROW_LOGITS + 1, :half] - wv[ROW_LOGITS:ROW_LOGITS + 1, half:])
        d_first = total[ROW_LOGITS:ROW_LOGITS + 1, :half] * lb * (1.0 - lb)
        d_logits = jnp.concatenate([d_first, -d_first], axis=1)
        rowi = lax.broadcasted_iota(jnp.int32, (VEC_ROWS, D_MODEL), 0)
        g = jnp.where(rowi == ROW_LOGITS, d_logits, jnp.where(rowi < ROW_LOGITS, total, 0.0))
        g_ref[...] = g
        d_ref[...], mo_ref[...], vo_ref[...] = _adamw(wv, g, m_ref[...], v_ref[...])
        loss_ref[...] = total[ROW_LOSS:ROW_LOSS + 1, :]

    vmem = pl.BlockSpec(memory_space=pltpu.VMEM)
    return pl.pallas_call(
        body,
        name=name,
        in_specs=[vmem] * 4 + [HBM],
        out_specs=[vmem] * 5,
        out_shape=[jax.ShapeDtypeStruct((VEC_ROWS, D_MODEL), F32)] * 4 + [jax.ShapeDtypeStruct((1, D_MODEL), F32)],
        scratch_shapes=[pltpu.VMEM((N_DEV, VEC_ROWS, D_MODEL), F32), pltpu.SemaphoreType.DMA((7,)),
                        pltpu.SemaphoreType.DMA((7,))],
    )(part, w, m, v, tie)


TRANSPOSED = ("g1t", "u1t", "g2t", "u2t", "int")


def _vector_rows(rows):
    rowi = lax.broadcasted_iota(jnp.int32, (VEC_ROWS, D_MODEL), 0)
    out = jnp.zeros((VEC_ROWS, D_MODEL), F32)
    for i, r in enumerate(rows):
        if r is not None:
            out = jnp.where(rowi == i, r, out)
    return out


def kernel(x, ffn1_norm, ffn1_w_gate, ffn1_w_up, ffn1_w_down, mix_norm, w_in, sb_out_norm, hg_lower_bound_logits, hg_out_norm, w_out, ffn2_norm, ffn2_w_gate, ffn2_w_up, ffn2_w_down, final_norm, loss_target, m_ffn1_norm, m_ffn1_w_gate, m_ffn1_w_up, m_ffn1_w_down, m_mix_norm, m_w_in, m_sb_out_norm, m_hg_lower_bound_logits, m_hg_out_norm, m_w_out, m_ffn2_norm, m_ffn2_w_gate, m_ffn2_w_up, m_ffn2_w_down, m_final_norm, v_ffn1_norm, v_ffn1_w_gate, v_ffn1_w_up, v_ffn1_w_down, v_mix_norm, v_w_in, v_sb_out_norm, v_hg_lower_bound_logits, v_hg_out_norm, v_w_out, v_ffn2_norm, v_ffn2_w_gate, v_ffn2_w_up, v_ffn2_w_down, v_final_norm):
    def matrices(g1, u1, d1, win, wout, g2, u2, d2):
        return {"g1t": g1, "u1t": u1, "d1": d1, "int": win, "out": wout, "g2t": g2, "u2t": u2, "d2": d2}

    def vectors(n1, nm, nsb, lg, nhg, n2, nf):
        return [n1, nm, n2, nf.reshape(1, D_MODEL), jnp.concatenate([nsb, nhg], axis=1), lg.reshape(1, D_MODEL), None, None]

    w_sh = matrices(ffn1_w_gate, ffn1_w_up, ffn1_w_down, w_in, w_out, ffn2_w_gate, ffn2_w_up, ffn2_w_down)
    m_sh = matrices(m_ffn1_w_gate, m_ffn1_w_up, m_ffn1_w_down, m_w_in, m_w_out, m_ffn2_w_gate, m_ffn2_w_up, m_ffn2_w_down)
    v_sh = matrices(v_ffn1_w_gate, v_ffn1_w_up, v_ffn1_w_down, v_w_in, v_w_out, v_ffn2_w_gate, v_ffn2_w_up, v_ffn2_w_down)
    keys = list(w_sh)

    def full(key, stack):
        return stack.reshape(-1, D_MODEL)

    def by_owner(key, grad):
        return grad.reshape(N_DEV, -1, D_MODEL)

    def view(key, a):
        return jnp.swapaxes(a, 1, 2) if key in TRANSPOSED else a

    blocks = {k: view(k, w_sh[k])[0].astype(BF16) for k in keys}
    first, mid, last = ("g1t", "u1t", "d1"), ("int", "out"), ("g2t", "u2t", "d2")
    w_first = {k: full(k, s) for k, s in zip(first, _all_gather([blocks[k] for k in first], name="gather_ffn1"))}
    flights = {}
    flights["ffn1"], token_mid = _copies_start([blocks[k] for k in mid], name="gather_mid_start", by_owner=False,
                                               after=w_first["d1"])
    flights["mix"], token_last = _copies_start([blocks[k] for k in last], name="gather_ffn2_start", by_owner=False,
                                               after=token_mid)

    def weights_after(stage, result):
        group = mid if stage == "ffn1" else last
        _, lands = _copies_wait(flights[stage], result, name="gather_" + stage + "_wait", by_owner=False)
        return {k: full(k, s) for k, s in zip(group, lands)}

    groups = {"mix": ("g2t", "u2t", "d2", "out"), "in": ("int",), "g1t": ("g1t",), "u1t": ("u1t",), "d1": ("d1",)}
    sent = {}

    def grads_ready(stage, gw):
        stacks = [by_owner(k, gw[k]) for k in groups[stage]]
        sent[stage], token = _copies_start(stacks, name="grads_" + stage + "_start", by_owner=True)
        return token

    norms = {"ffn1": ffn1_norm + token_last[0, 0], "mix": mix_norm, "sb": sb_out_norm, "hg": hg_out_norm,
             "ffn2": ffn2_norm, "final": final_norm.reshape(1, D_MODEL)}
    loss_row, grad_x, gw, gv = _local_step(x[0], loss_target[0], norms, hg_lower_bound_logits, w_first, weights_after,
                                           grads_ready)

    lb_row = jnp.concatenate([gv["lb"], jnp.zeros_like(gv["lb"])], axis=1)
    part = _vector_rows([gv["ffn1"], gv["mix"], gv["ffn2"], gv["final"], jnp.concatenate([gv["sb"], gv["hg"]], axis=1),
                         lb_row, None, loss_row])
    vec_w = _vector_rows(vectors(ffn1_norm, mix_norm, sb_out_norm, hg_lower_bound_logits, hg_out_norm, ffn2_norm, final_norm))
    vec_m = _vector_rows(vectors(m_ffn1_norm, m_mix_norm, m_sb_out_norm, m_hg_lower_bound_logits, m_hg_out_norm,
                                 m_ffn2_norm, m_final_norm))
    vec_v = _vector_rows(vectors(v_ffn1_norm, v_mix_norm, v_sb_out_norm, v_hg_lower_bound_logits, v_hg_out_norm,
                                 v_ffn2_norm, v_final_norm))
    updated, after = {}, grad_x
    for stage, flight in sent.items():
        if stage == list(sent)[-1]:
            *vecs, loss_out = _vectors_update(part, vec_w, vec_m, vec_v, name="vectors_update", tie=after)
            after = loss_out
        _, lands = _copies_wait(flight, after, name="grads_" + stage + "_wait", by_owner=True)
        for k, part_k in zip(groups[stage], lands):
            updated[k] = _sum_and_update(part_k, view(k, w_sh[k]), view(k, m_sh[k]), view(k, v_sh[k]), name="adamw_" + k,
                                         tie=after)
            after = updated[k][0]
    mats = [{k: view(k, updated[k][i]) for k in keys} for i in range(4)]

    def leaves(mat, vec):
        half = D_MODEL // 2
        return (
            vec[0:1], mat["g1t"], mat["u1t"], mat["d1"], vec[1:2], mat["int"], vec[4:5, :half],
            vec[ROW_LOGITS].reshape(2, half), vec[4:5, half:], mat["out"], vec[2:3], mat["g2t"], mat["u2t"],
            mat["d2"], vec[3],
        )

    out = [loss_out[0, 0], grad_x[None]]
    for mat, vec in zip(mats, vecs):
        out.extend(leaves(mat, vec))
    return tuple(out)
```

```python
import jax
import jax.numpy as jnp
from jax import lax
from jax.experimental import pallas as pl
from jax.experimental.pallas import tpu as pltpu

F32, BF16 = jnp.float32, jnp.bfloat16
D_MODEL = 1024
D_FF = 2816
SB_WIDTH = 512
HG_WIDTH = 512
SB_HEAD_DIM = 64
HG_HEAD_DIM = 128
IN_COLS = 3584
EPS = 1e-6
N_DEV = 8
LANES = 128
HG_CHUNK = 16
VMEM_LIMIT_BYTES = 48 * 1024 * 1024
FFN_BWD_VMEM_LIMIT_BYTES = 56 * 1024 * 1024
ADAM_LR, ADAM_B1, ADAM_B2, ADAM_EPS, ADAM_WD, ADAM_STEP = 0.001, 0.9, 0.999, 1e-08, 0.01, 10
MESH = pl.DeviceIdType.MESH


def _params(*semantics, vmem_limit_bytes=VMEM_LIMIT_BYTES):
    return pltpu.CompilerParams(dimension_semantics=semantics, vmem_limit_bytes=vmem_limit_bytes)


def _dot(a, b):
    return jnp.dot(a, b, preferred_element_type=F32)


def _dot_nt(a, b):
    return lax.dot_general(a, b, (((1,), (1,)), ((), ())), preferred_element_type=F32)


def _dot_tn(a, b):
    return lax.dot_general(a, b, (((0,), (0,)), ((), ())), preferred_element_type=F32)


def _split3(x):
    hi = x.astype(BF16)
    r1 = x - hi.astype(F32)
    mid = r1.astype(BF16)
    lo = (r1 - mid.astype(F32)).astype(BF16)
    return hi, mid, lo


def _rms(xv):
    rstd = lax.rsqrt(jnp.mean(xv * xv, axis=-1, keepdims=True) + EPS)
    return xv * rstd, rstd


def _sigmoid(x):
    return 0.5 + 0.5 * jnp.tanh(0.5 * x)


def _loss_terms(xv, gain, target):
    xhat, rstd = _rms(xv)
    err = xhat * gain - target
    loss = 0.5 * jnp.sum(jnp.mean(err * err, axis=-1, keepdims=True), axis=0, keepdims=True)
    dy = err * (1.0 / xv.shape[-1])
    dxh = dy * gain
    dx = rstd * (dxh - xhat * jnp.mean(dxh * xhat, axis=-1, keepdims=True))
    return dx, jnp.sum(dy * xhat, axis=0, keepdims=True), loss


def _mm(a, b, *, name, tm, tn, nt=False, ta=False, out_dtype=F32, tie=None):
    k, m = a.shape if ta else a.shape[::-1]
    n = b.shape[0] if nt else b.shape[1]
    assert m % tm == 0 and n % tn == 0 and not (nt and ta), (name, a.shape, b.shape, tm, tn)

    def body(a_ref, b_ref, *rest):
        av = a_ref[...].astype(BF16)
        bv = b_ref[...].astype(BF16)
        rest[-1][...] = (_dot_nt(av, bv) if nt else _dot_tn(av, bv) if ta else _dot(av, bv)).astype(out_dtype)

    in_specs = [
        pl.BlockSpec((k, tm), lambda i, j: (0, i)) if ta else pl.BlockSpec((tm, k), lambda i, j: (i, 0)),
        pl.BlockSpec((tn, k), lambda i, j: (j, 0)) if nt else pl.BlockSpec((k, tn), lambda i, j: (0, j)),
    ]
    operands = [a, b]
    if tie is not None:
        in_specs.append(pl.BlockSpec(memory_space=pl.ANY))
        operands.append(tie)
    return pl.pallas_call(
        body,
        name=name,
        grid=(m // tm, n // tn),
        in_specs=in_specs,
        out_specs=pl.BlockSpec((tm, tn), lambda i, j: (i, j)),
        out_shape=jax.ShapeDtypeStruct((m, n), out_dtype),
        compiler_params=_params("parallel", "parallel"),
    )(*operands)


def _ffn_fwd(x, gain, wgt, wut, wd, *, name, next_gain=None, head=None, tm=512, tf=D_FF // 2):
    t = x.shape[0]
    nj = D_FF // tf
    extra_in = [] if next_gain is None else [next_gain]
    extra_in += [] if head is None else list(head)

    def body(x_ref, g_ref, wg_ref, wu_ref, wd_ref, *rest):
        extra, (xo_ref, a_ref, b_ref, h_ref, st_ref) = rest[:len(extra_in)], rest[len(extra_in):len(extra_in) + 5]
        tail_out, acc = rest[len(extra_in) + 5:-1], rest[-1]
        i = pl.program_id(0)
        j = pl.program_id(1)

        @pl.when(j == 0)
        def _():
            xhat, _ = _rms(x_ref[...])
            h_ref[...] = (xhat * g_ref[...]).astype(BF16)

        h = h_ref[...]
        a = _dot_nt(h, wg_ref[...])
        b = _dot_nt(h, wu_ref[...])
        a_ref[...] = a.astype(BF16)
        b_ref[...] = b.astype(BF16)
        s = (a * _sigmoid(a) * b).astype(BF16)
        st_ref[...] = s
        down = _dot(s, wd_ref[...])

        @pl.when(j == 0)
        def _():
            acc[...] = down

        @pl.when((j > 0) & (j < nj - 1))
        def _():
            acc[...] += down

        @pl.when(j == nj - 1)
        def _():
            xo = x_ref[...] + 0.5 * (acc[...] + down)
            if head is None:
                xo_ref[...] = xo
            if next_gain is not None:
                tail_out[0][...] = (_rms(xo)[0] * extra[0][...]).astype(BF16)
            if head is not None:
                gain_ref, target_ref = extra[-2:]
                dg_ref, loss_ref = tail_out[-2:]
                xo_ref[...], part_g, part_loss = _loss_terms(xo, gain_ref[...], target_ref[...])

                @pl.when(i == 0)
                def _():
                    dg_ref[...] = part_g
                    loss_ref[...] = jnp.broadcast_to(part_loss, loss_ref.shape)

                @pl.when(i > 0)
                def _():
                    dg_ref[...] += part_g
                    loss_ref[...] += jnp.broadcast_to(part_loss, loss_ref.shape)

    row = pl.BlockSpec((tm, D_MODEL), lambda i, j: (i, 0))
    vec = pl.BlockSpec((1, D_MODEL), lambda i, j: (0, 0))
    tile = pl.BlockSpec((tm, tf), lambda i, j: (i, j))
    weights = pl.BlockSpec((tf, D_MODEL), lambda i, j: (j, 0))
    tail_specs = ([] if next_gain is None else [row]) + ([] if head is None else [vec, vec])
    tail_shapes = ([] if next_gain is None else [jax.ShapeDtypeStruct((t, D_MODEL), BF16)]) + (
        [] if head is None else [jax.ShapeDtypeStruct((1, D_MODEL), F32)] * 2)
    return pl.pallas_call(
        body,
        name=name,
        grid=(t // tm, nj),
        in_specs=[row, vec, weights, weights, weights]
        + ([] if next_gain is None else [vec]) + ([] if head is None else [vec, row]),
        out_specs=[row, tile, tile, row, tile] + tail_specs,
        out_shape=[
            jax.ShapeDtypeStruct((t, D_MODEL), F32),
            jax.ShapeDtypeStruct((t, D_FF), BF16),
            jax.ShapeDtypeStruct((t, D_FF), BF16),
            jax.ShapeDtypeStruct((t, D_MODEL), BF16),
            jax.ShapeDtypeStruct((t, D_FF), BF16),
        ] + tail_shapes,
        scratch_shapes=[pltpu.VMEM((tm, D_MODEL), F32)],
        compiler_params=_params("arbitrary", "arbitrary", vmem_limit_bytes=FFN_BWD_VMEM_LIMIT_BYTES),
    )(x, gain, wgt, wut, wd, *extra_in)


def _ffn_bwd(dout, x, gain, a, b, wgt, wut, wd, *, name, tm=1024, tf=256):
    t = x.shape[0]
    nj = D_FF // tf

    def body(do_ref, x_ref, g_ref, a_ref, b_ref, wg_prev_ref, wu_prev_ref, wg_last_ref, wu_last_ref, wd_ref,
             dx_ref, dg_ref, da_ref, db_ref, dob_ref, dob_scr, dh, da_prev, db_prev):
        i = pl.program_id(0)
        j = pl.program_id(1)

        @pl.when(j == 0)
        def _():
            d = (0.5 * do_ref[...]).astype(BF16)
            dob_scr[...] = d
            dob_ref[...] = d
            dh[...] = jnp.zeros_like(dh)
            da_prev[...] = jnp.zeros_like(da_prev)
            db_prev[...] = jnp.zeros_like(db_prev)

        dh[...] += _dot(da_prev[...], wg_prev_ref[...]) + _dot(db_prev[...], wu_prev_ref[...])
        ds = _dot_nt(dob_scr[...], wd_ref[...])
        av = a_ref[...].astype(F32)
        bv = b_ref[...].astype(F32)
        sig = _sigmoid(av)
        dbv = (ds * (av * sig)).astype(BF16)
        dav = (ds * bv * (sig * (1.0 + av * (1.0 - sig)))).astype(BF16)
        da_ref[...] = dav
        db_ref[...] = dbv
        da_prev[...] = dav
        db_prev[...] = dbv

        @pl.when(j == nj - 1)
        def _():
            xhat, rstd = _rms(x_ref[...])
            dhv = dh[...] + _dot(dav, wg_last_ref[...]) + _dot(dbv, wu_last_ref[...])
            part = jnp.sum(dhv * xhat, axis=0, keepdims=True)

            @pl.when(i == 0)
            def _():
                dg_ref[...] = part

            @pl.when(i > 0)
            def _():
                dg_ref[...] += part

            dxh = dhv * g_ref[...]
            dx_ref[...] = do_ref[...] + rstd * (dxh - xhat * jnp.mean(dxh * xhat, axis=-1, keepdims=True))

    return pl.pallas_call(
        body,
        name=name,
        grid=(t // tm, nj),
        in_specs=[
            pl.BlockSpec((tm, D_MODEL), lambda i, j: (i, 0)),
            pl.BlockSpec((tm, D_MODEL), lambda i, j: (i, 0)),
            pl.BlockSpec((1, D_MODEL), lambda i, j: (0, 0)),
            pl.BlockSpec((tm, tf), lambda i, j: (i, j)),
            pl.BlockSpec((tm, tf), lambda i, j: (i, j)),
            pl.BlockSpec((tf, D_MODEL), lambda i, j: (jnp.maximum(j - 1, 0), 0)),
            pl.BlockSpec((tf, D_MODEL), lambda i, j: (jnp.maximum(j - 1, 0), 0)),
            pl.BlockSpec((tf, D_MODEL), lambda i, j: (nj - 1, 0)),
            pl.BlockSpec((tf, D_MODEL), lambda i, j: (nj - 1, 0)),
            pl.BlockSpec((tf, D_MODEL), lambda i, j: (j, 0)),
        ],
        out_specs=[
            pl.BlockSpec((tm, D_MODEL), lambda i, j: (i, 0)),
            pl.BlockSpec((1, D_MODEL), lambda i, j: (0, 0)),
            pl.BlockSpec((tm, tf), lambda i, j: (i, j)),
            pl.BlockSpec((tm, tf), lambda i, j: (i, j)),
            pl.BlockSpec((tm, D_MODEL), lambda i, j: (i, 0)),
        ],
        out_shape=[
            jax.ShapeDtypeStruct((t, D_MODEL), F32),
            jax.ShapeDtypeStruct((1, D_MODEL), F32),
            jax.ShapeDtypeStruct((t, D_FF), BF16),
            jax.ShapeDtypeStruct((t, D_FF), BF16),
            jax.ShapeDtypeStruct((t, D_MODEL), BF16),
        ],
        scratch_shapes=[pltpu.VMEM((tm, D_MODEL), BF16), pltpu.VMEM((tm, D_MODEL), F32), pltpu.VMEM((tm, tf), BF16),
                        pltpu.VMEM((tm, tf), BF16)],
        compiler_params=_params("arbitrary", "arbitrary", vmem_limit_bytes=FFN_BWD_VMEM_LIMIT_BYTES),
    )(dout, x, gain, a, b, wgt, wut, wgt, wut, wd)


def _ffn_bwd_act(dout_half, a, b, wd, *, name, tie, tm=512, tf=D_FF // 2):
    t = a.shape[0]

    def body(dob_ref, a_ref, b_ref, wd_ref, tie_ref, da_ref, db_ref):
        ds = _dot_nt(dob_ref[...], wd_ref[...])
        av = a_ref[...].astype(F32)
        bv = b_ref[...].astype(F32)
        sig = _sigmoid(av)
        db_ref[...] = (ds * (av * sig)).astype(BF16)
        da_ref[...] = (ds * bv * (sig * (1.0 + av * (1.0 - sig)))).astype(BF16)

    tile = pl.BlockSpec((tm, tf), lambda i, j: (i, j))
    return pl.pallas_call(
        body,
        name=name,
        grid=(t // tm, D_FF // tf),
        in_specs=[pl.BlockSpec((tm, D_MODEL), lambda i, j: (i, 0)), tile, tile,
                  pl.BlockSpec((tf, D_MODEL), lambda i, j: (j, 0)), pl.BlockSpec(memory_space=pl.ANY)],
        out_specs=[tile, tile],
        out_shape=[jax.ShapeDtypeStruct((t, D_FF), BF16)] * 2,
        compiler_params=_params("parallel", "parallel"),
    )(dout_half, a, b, wd, tie)


def _ffn_bwd_in(dout, x, gain, da, db, wgt, wut, *, name, tm=512):
    t = x.shape[0]

    def body(do_ref, x_ref, g_ref, da_ref, db_ref, wg_ref, wu_ref, dx_ref, dg_ref):
        i = pl.program_id(0)
        dhv = _dot(da_ref[...], wg_ref[...]) + _dot(db_ref[...], wu_ref[...])
        xhat, rstd = _rms(x_ref[...])
        part = jnp.sum(dhv * xhat, axis=0, keepdims=True)

        @pl.when(i == 0)
        def _():
            dg_ref[...] = part

        @pl.when(i > 0)
        def _():
            dg_ref[...] += part

        dxh = dhv * g_ref[...]
        dx_ref[...] = do_ref[...] + rstd * (dxh - xhat * jnp.mean(dxh * xhat, axis=-1, keepdims=True))

    row = pl.BlockSpec((tm, D_MODEL), lambda i: (i, 0))
    vec = pl.BlockSpec((1, D_MODEL), lambda i: (0, 0))
    tile = pl.BlockSpec((tm, D_FF), lambda i: (i, 0))
    weights = pl.BlockSpec((D_FF, D_MODEL), lambda i: (0, 0), pipeline_mode=pl.Buffered(1))
    return pl.pallas_call(
        body,
        name=name,
        grid=(t // tm,),
        in_specs=[row, row, vec, tile, tile, weights, weights],
        out_specs=[row, vec],
        out_shape=[jax.ShapeDtypeStruct((t, D_MODEL), F32), jax.ShapeDtypeStruct((1, D_MODEL), F32)],
        compiler_params=_params("arbitrary", vmem_limit_bytes=FFN_BWD_VMEM_LIMIT_BYTES),
    )(dout, x, gain, da, db, wgt, wut)


def _in_proj_bwd(dproj, w_int, x, gain, dres, *, name, tm=512):
    t, k = dproj.shape

    def body(dp_ref, w_ref, x_ref, g_ref, dr_ref, dx_ref, dg_ref, dxb_ref):
        i = pl.program_id(0)
        dhv = _dot(dp_ref[...], w_ref[...])
        xhat, rstd = _rms(x_ref[...])
        part = jnp.sum(dhv * xhat, axis=0, keepdims=True)

        @pl.when(i == 0)
        def _():
            dg_ref[...] = part

        @pl.when(i > 0)
        def _():
            dg_ref[...] += part

        dxh = dhv * g_ref[...]
        dx = dr_ref[...] + rstd * (dxh - xhat * jnp.mean(dxh * xhat, axis=-1, keepdims=True))
        dx_ref[...] = dx
        dxb_ref[...] = (0.5 * dx).astype(BF16)

    row = pl.BlockSpec((tm, D_MODEL), lambda i: (i, 0))
    vec = pl.BlockSpec((1, D_MODEL), lambda i: (0, 0))
    return pl.pallas_call(
        body,
        name=name,
        grid=(t // tm,),
        in_specs=[pl.BlockSpec((tm, k), lambda i: (i, 0)), pl.BlockSpec((k, D_MODEL), lambda i: (0, 0)), row, vec, row],
        out_specs=[row, vec, row],
        out_shape=[jax.ShapeDtypeStruct((t, D_MODEL), F32), jax.ShapeDtypeStruct((1, D_MODEL), F32),
                   jax.ShapeDtypeStruct((t, D_MODEL), BF16)],
        compiler_params=_params("arbitrary"),
    )(dproj, w_int, x, gain, dres)


ATT_Q_TILE = 512
ATT_K_BLOCK = 256


def _first_head_lanes():
    return lax.broadcasted_iota(jnp.int32, (1, LANES), 1) < SB_HEAD_DIM


def _stack_heads(x):
    first = _first_head_lanes()
    return jnp.concatenate([jnp.where(first, x, 0.0), jnp.where(first, 0.0, x)], axis=0)


def _unstack_heads(x, rows):
    return jnp.where(_first_head_lanes(), x[:rows], x[rows:])


def _rows_from(x, first, rows):
    return x if first == 0 else jnp.concatenate([x[first:rows], x[rows + first:]], axis=0)


def _rows_into(full, part, first, rows):
    if first == 0:
        return part
    n = rows - first
    return jnp.concatenate([full[:first], part[:n], full[rows:rows + first], part[n:]], axis=0)


def _tri(n, relation):
    r = lax.broadcasted_iota(jnp.int32, (n, n), 0)
    c = lax.broadcasted_iota(jnp.int32, (n, n), 1)
    return relation(r, c).astype(BF16)


def _scan_dot(x, tri):
    hi = x.astype(BF16)
    lo = (x - hi.astype(F32)).astype(BF16)
    return _dot(jnp.concatenate([hi, lo], axis=1), jnp.concatenate([tri, tri], axis=0))


def _log_terms(z):
    lbeta = jnp.minimum(z, 0.0) - jnp.log(1.0 + jnp.exp(-jnp.abs(z)))
    return lbeta, lbeta - z


def _attn_fwd(proj, *, name):
    t = proj.shape[0]
    tq, tk = ATT_Q_TILE, ATT_K_BLOCK
    diag = tq // tk
    n_pairs = SB_WIDTH // LANES

    def body(q_ref, k_ref, v_ref, o_ref, kept_ref):
        qi = pl.program_id(1)
        q = q_ref[...] * (SB_HEAD_DIM ** -0.5)
        qs = _stack_heads(q).astype(BF16)
        tri = _tri(tk, lambda j, s: j > s)
        trow = lax.broadcasted_iota(jnp.int32, (tq, tk), 0)
        scol = lax.broadcasted_iota(jnp.int32, (tq, tk), 1)

        def block(kb, carry, causal, first=0):
            acc, c = carry
            off = pl.multiple_of(kb * tk, tk)
            lbeta, lrest = _log_terms(_dot_nt(_rows_from(qs, first, tq), k_ref[pl.ds(off, tk), :].astype(BF16)))
            if causal is not None:
                lrest = jnp.where(causal, lrest, 0.0)
            w = jnp.exp(lbeta + (_scan_dot(lrest, tri) + _rows_from(c, first, tq)))
            if causal is not None:
                w = jnp.where(causal, w, 0.0)
            wb = w.astype(BF16)
            kept_ref[0, 0, kb] = _rows_into(jnp.zeros((2 * tq, tk), BF16), wb, first, tq)
            acc = _rows_into(acc, _rows_from(acc, first, tq) + _dot(wb, v_ref[pl.ds(off, tk), :].astype(BF16)), first, tq)
            return acc, _rows_into(c, _rows_from(c, first, tq) + jnp.sum(lrest, axis=1, keepdims=True), first, tq)

        carry = (jnp.zeros((2 * tq, LANES), F32), jnp.zeros((2 * tq, 1), F32))
        n_full = qi * diag
        for j in reversed(range(diag)):
            mask = ((scol + j * tk) < trow)[j * tk:]
            carry = block(n_full + j, carry, jnp.concatenate([mask, mask], axis=0), first=j * tk)

        def odd_tile(carry):
            for j in range(diag):
                carry = block(n_full - 1 - j, carry, None)
            return carry

        carry = lax.cond(qi % 2 == 1, odd_tile, lambda c: c, carry)
        last = n_full - 1 - (qi % 2) * diag

        def step(it, carry):
            for j in range(2 * diag):
                carry = block(last - (2 * diag * it + j), carry, None)
            return carry

        acc, _ = lax.fori_loop(0, qi // 2, step, carry)
        o_ref[...] = _unstack_heads(acc, tq)

    return pl.pallas_call(
        body,
        name=name,
        grid=(n_pairs, t // tq),
        in_specs=[
            pl.BlockSpec((tq, LANES), lambda p, i: (i, p)),
            pl.BlockSpec((t, LANES), lambda p, i: (0, n_pairs + p)),
            pl.BlockSpec((t, LANES), lambda p, i: (0, 2 * n_pairs + p)),
        ],
        out_specs=[pl.BlockSpec((tq, LANES), lambda p, i: (i, p)),
                   pl.BlockSpec((1, 1, t // tk, 2 * tq, tk), lambda p, i: (p, i, 0, 0, 0))],
        out_shape=[jax.ShapeDtypeStruct((t, SB_WIDTH), F32),
                   jax.ShapeDtypeStruct((n_pairs, t // tq, t // tk, 2 * tq, tk), BF16)],
        compiler_params=_params("parallel", "parallel"),
    )(proj, proj, proj)


def _attn_bwd(proj, kept, do, *, name, tie=None):
    t = proj.shape[0]
    tq, tk = ATT_Q_TILE, ATT_K_BLOCK
    diag = tq // tk
    n_pairs = SB_WIDTH // LANES
    scale = SB_HEAD_DIM ** -0.5

    def body(q_ref, k_ref, v_ref, kept_ref, do_ref, *rest):
        dq_ref, dk_ref, dv_ref = rest[-3:]
        qi = pl.program_id(1)

        @pl.when(qi == 0)
        def _():
            dk_ref[...] = jnp.zeros_like(dk_ref)
            dv_ref[...] = jnp.zeros_like(dv_ref)

        qs = _stack_heads(q_ref[...] * scale).astype(BF16)
        dos = _stack_heads(do_ref[...]).astype(BF16)
        before = _tri(tk, lambda s, j: s < j)
        trow = lax.broadcasted_iota(jnp.int32, (tq, tk), 0)
        scol = lax.broadcasted_iota(jnp.int32, (tq, tk), 1)

        def block(kb, carry, causal, first=0):
            dq, cg = carry
            off = pl.multiple_of(kb * tk, tk)
            q_rows, do_rows = _rows_from(qs, first, tq), _rows_from(dos, first, tq)
            wb = _rows_from(kept_ref[0, 0, kb], first, tq)
            kblk = k_ref[pl.ds(off, tk), :].astype(BF16)
            sig = _sigmoid(_dot_nt(q_rows, kblk))
            g = wb.astype(F32) * _dot_nt(do_rows, v_ref[pl.ds(off, tk), :].astype(BF16))
            prior = _scan_dot(g, before) + _rows_from(cg, first, tq)
            dz = g - sig * (g + prior)
            if causal is not None:
                dz = jnp.where(causal, dz, 0.0)
            dzb = dz.astype(BF16)
            dq = _rows_into(dq, _rows_from(dq, first, tq) + _dot(dzb, kblk), first, tq)
            dk_ref[pl.ds(off, tk), :] += _dot_tn(dzb, q_rows)
            dv_ref[pl.ds(off, tk), :] += _dot_tn(wb, do_rows)
            return dq, _rows_into(cg, _rows_from(cg, first, tq) + jnp.sum(g, axis=1, keepdims=True), first, tq)

        n_full = qi * diag

        def step(it, carry):
            for j in range(2 * diag):
                carry = block(2 * diag * it + j, carry, None)
            return carry

        def odd_tile(carry):
            for j in range(diag):
                carry = block(n_full - diag + j, carry, None)
            return carry

        carry = lax.fori_loop(0, qi // 2, step, (jnp.zeros((2 * tq, LANES), F32), jnp.zeros((2 * tq, 1), F32)))
        carry = lax.cond(qi % 2 == 1, odd_tile, lambda c: c, carry)
        for j in range(diag):
            mask = ((scol + j * tk) < trow)[j * tk:]
            carry = block(n_full + j, carry, jnp.concatenate([mask, mask], axis=0), first=j * tk)
        dq_ref[...] = (_unstack_heads(carry[0], tq) * scale).astype(BF16)

    tile_spec = pl.BlockSpec((tq, LANES), lambda p, i: (i, p))
    full_spec = pl.BlockSpec((t, LANES), lambda p, i: (0, p))
    return pl.pallas_call(
        body,
        name=name,
        grid=(n_pairs, t // tq),
        in_specs=[
            tile_spec,
            pl.BlockSpec((t, LANES), lambda p, i: (0, n_pairs + p)),
            pl.BlockSpec((t, LANES), lambda p, i: (0, 2 * n_pairs + p)),
            pl.BlockSpec((1, 1, t // tk, 2 * tq, tk), lambda p, i: (p, i, 0, 0, 0)),
            tile_spec,
        ] + ([] if tie is None else [pl.BlockSpec(memory_space=pl.ANY)]),
        out_specs=[tile_spec, full_spec, full_spec],
        out_shape=[jax.ShapeDtypeStruct((t, SB_WIDTH), BF16)] + [jax.ShapeDtypeStruct((t, SB_WIDTH), F32)] * 2,
        compiler_params=_params("arbitrary", "arbitrary"),
    )(proj, proj, proj, kept, do, *([] if tie is None else [tie]))


HG_BLOCK = 128
HG_HEADS = HG_WIDTH // HG_HEAD_DIM


def _chunk_mats(n):
    r = lax.broadcasted_iota(jnp.int32, (n, n), 0)
    c = lax.broadcasted_iota(jnp.int32, (n, n), 1)
    same = (r // HG_CHUNK) == (c // HG_CHUNK)
    upto = (same & (c <= r)).astype(BF16)
    whole = same.astype(BF16)
    onward = (same & (c >= r)).astype(BF16)
    return upto, whole, onward


def _rows_dot(mat, x):
    return _dot(jnp.concatenate([mat, mat, mat], axis=1), jnp.concatenate(_split3(x), axis=0))


def _split_heads(x):
    return jnp.stack([x[:, h * HG_HEAD_DIM:(h + 1) * HG_HEAD_DIM] for h in range(HG_HEADS)], axis=0)


def _merge_heads(x):
    return jnp.concatenate([x[h] for h in range(HG_HEADS)], axis=1)


def _lower_bound(lg_ref):
    lg = lg_ref[...]
    return _sigmoid(lg[0:1, :] - lg[1:2, :])


def _hgrn_prepare(q_ref, f_ref, lb, h, upto, whole):
    cols = slice(h * HG_HEAD_DIM, (h + 1) * HG_HEAD_DIM)
    lbh = lb[:, cols]
    sg = _sigmoid(f_ref[:, cols])
    forget = lbh + (1.0 - lbh) * sg
    logf = jnp.log(forget)
    kk = (1.0 - lbh) * (1.0 - sg)
    qv = q_ref[:, cols]
    qsig = _sigmoid(qv)
    qh = qv * qsig
    b = _rows_dot(upto, logf)
    blast = _rows_dot(whole, logf)
    return dict(lbh=lbh, sg=sg, forget=forget, kk=kk, qv=qv, qsig=qsig, qh=qh, b=b, eb=jnp.exp(b),
                ekb=jnp.exp(blast - b), dl=jnp.exp(blast))


def _hgrn_fwd(proj, logits, *, name):
    t = proj.shape[0]
    tb = HG_BLOCK
    nc = tb // HG_CHUNK
    hd = HG_HEAD_DIM

    def body(q_ref, f_ref, i_ref, lg_ref, o_ref, st_ref, state, qh_s, kk_s, b_s, qe_s, ke_s, dl_s):
        @pl.when(pl.program_id(0) == 0)
        def _():
            state[...] = jnp.zeros_like(state)

        lb = _lower_bound(lg_ref)
        upto, whole, _ = _chunk_mats(tb)
        for h in range(HG_HEADS):
            p = _hgrn_prepare(q_ref, f_ref, lb, h, upto, whole)
            qh_s[h] = p["qh"]
            kk_s[h] = p["kk"]
            b_s[h] = p["b"]
            qe_s[h] = (p["qh"] * p["eb"]).astype(BF16)
            ke_s[h] = (p["kk"] * p["ekb"]).astype(BF16)
            dl_s[h] = p["dl"]
        rowi = lax.broadcasted_iota(jnp.int32, (HG_HEADS, HG_CHUNK, hd), 1)

        def chunk(c, _):
            r0 = pl.multiple_of(c * HG_CHUNK, HG_CHUNK)
            rows = pl.ds(r0, HG_CHUNK)
            bc = b_s[:, rows, :]
            qc = qh_s[:, rows, :]
            kc = kk_s[:, rows, :]
            vc = _split_heads(i_ref[rows, :])
            s_in = state[...]
            st_ref[c] = s_in
            s_in_b = s_in.astype(BF16)
            qe = qe_s[:, rows, :]
            o = jnp.stack([_dot_nt(qe[h], s_in_b[h]) for h in range(HG_HEADS)], axis=0)
            for s in range(HG_CHUNK):
                pair = jnp.where(rowi >= s, qc * jnp.exp(bc - bc[:, s:s + 1, :]) * kc[:, s:s + 1, :], 0.0)
                o = o + jnp.sum(pair, axis=2, keepdims=True) * vc[:, s:s + 1, :]
            o_ref[rows, :] = _merge_heads(o)
            vcb = vc.astype(BF16)
            ke = ke_s[:, rows, :]
            update = jnp.stack([_dot_tn(vcb[h], ke[h]) for h in range(HG_HEADS)], axis=0)
            state[...] = s_in * dl_s[:, pl.ds(r0, 1), :] + update
            return 0

        lax.fori_loop(0, nc, chunk, 0, unroll=4)

    blk =lambda col: pl.BlockSpec((tb, HG_WIDTH), lambda i: (i, col))
    head_f32 = pltpu.VMEM((HG_HEADS, tb, hd), F32)
    head_bf16 = pltpu.VMEM((HG_HEADS, tb, hd), BF16)
    return pl.pallas_call(
        body,
        name=name,
        grid=(t // tb,),
        in_specs=[blk(3), blk(4), blk(5), pl.BlockSpec((2, HG_WIDTH), lambda i: (0, 0))],
        out_specs=[
            pl.BlockSpec((tb, HG_WIDTH), lambda i: (i, 0)),
            pl.BlockSpec((nc, HG_HEADS, hd, hd), lambda i: (i, 0, 0, 0)),
        ],
        out_shape=[
            jax.ShapeDtypeStruct((t, HG_WIDTH), F32),
            jax.ShapeDtypeStruct((t // HG_CHUNK, HG_HEADS, hd, hd), F32),
        ],
        scratch_shapes=[pltpu.VMEM((HG_HEADS, hd, hd), F32), head_f32, head_f32, head_f32, head_bf16, head_bf16,
                        head_f32],
        compiler_params=_params("arbitrary"),
    )(proj, proj, proj, logits)


def _hgrn_bwd(proj, logits, states, do, *, name):
    t = proj.shape[0]
    tb = HG_BLOCK
    nb = t // tb
    nc = tb // HG_CHUNK
    hd = HG_HEAD_DIM

    def body(q_ref, f_ref, i_ref, lg_ref, st_ref, do_ref, dq_ref, df_ref, di_ref, dlb_ref,
             dstate, qh_s, kk_s, b_s, eb_s, ekb_s, qe_s, ke_s, dl_s, dqh_s, dkk_s, dlf_s):
        step = pl.program_id(0)

        @pl.when(step == 0)
        def _():
            dstate[...] = jnp.zeros_like(dstate)
            dlb_ref[...] = jnp.zeros_like(dlb_ref)

        lb = _lower_bound(lg_ref)
        upto, whole, _ = _chunk_mats(tb)
        prepared = []
        for h in range(HG_HEADS):
            p = _hgrn_prepare(q_ref, f_ref, lb, h, upto, whole)
            prepared.append(p)
            qh_s[h] = p["qh"]
            kk_s[h] = p["kk"]
            b_s[h] = p["b"]
            eb_s[h] = p["eb"]
            ekb_s[h] = p["ekb"]
            qe_s[h] = (p["qh"] * p["eb"]).astype(BF16)
            ke_s[h] = (p["kk"] * p["ekb"]).astype(BF16)
            dl_s[h] = p["dl"]
        rowi = lax.broadcasted_iota(jnp.int32, (HG_CHUNK, hd), 0)
        r16 = lax.broadcasted_iota(jnp.int32, (HG_CHUNK, HG_CHUNK), 0)
        c16 = lax.broadcasted_iota(jnp.int32, (HG_CHUNK, HG_CHUNK), 1)
        onward = (c16 >= r16).astype(BF16)

        def chunk(it, _):
            c = nc - 1 - it
            r0 = pl.multiple_of(c * HG_CHUNK, HG_CHUNK)
            rows = pl.ds(r0, HG_CHUNK)
            for h in range(HG_HEADS):
                cols = slice(h * hd, (h + 1) * hd)
                bc = b_s[h, rows, :]
                qc = qh_s[h, rows, :]
                kc = kk_s[h, rows, :]
                vc = i_ref[rows, cols]
                doc = do_ref[rows, cols]
                s_in = st_ref[c, h]
                ds_out = dstate[h]
                ds_out_b = ds_out.astype(BF16)
                docb = doc.astype(BF16)
                dl_row = dl_s[h, pl.ds(r0, 1), :]
                dqh = _dot(docb, s_in.astype(BF16)) * eb_s[h, rows, :]
                dkk = _dot(vc.astype(BF16), ds_out_b) * ekb_s[h, rows, :]
                dv = _dot_nt(ke_s[h, rows, :], ds_out_b)
                db = dqh * qc - dkk * kc
                dwhole = jnp.sum(dkk * kc, axis=0, keepdims=True) + jnp.sum(ds_out * s_in, axis=0, keepdims=True) * dl_row
                dk_rows, dv_rows = [], []
                for s in range(HG_CHUNK):
                    keep = rowi >= s
                    e = jnp.exp(bc - bc[s:s + 1, :])
                    k_row = kc[s:s + 1, :]
                    pcol = jnp.sum(jnp.where(keep, qc * e * k_row, 0.0), axis=1, keepdims=True)
                    dpcol = jnp.sum(doc * vc[s:s + 1, :], axis=1, keepdims=True)
                    m = jnp.where(keep, e * dpcol, 0.0)
                    y = m * qc
                    dqh = dqh + m * k_row
                    db = db + y * k_row
                    dk_rows.append(jnp.sum(y, axis=0, keepdims=True))
                    dv_rows.append(jnp.sum(pcol * doc, axis=0, keepdims=True))
                dkk_pairs = jnp.concatenate(dk_rows, axis=0)
                dkk = dkk + dkk_pairs
                db = db - dkk_pairs * kc
                dv = dv + jnp.concatenate(dv_rows, axis=0)
                dqh_s[h, rows, :] = dqh
                dkk_s[h, rows, :] = dkk
                dlf_s[h, rows, :] = _rows_dot(onward, db) + dwhole
                di_ref[rows, cols] = dv.astype(BF16)
                dstate[h] = ds_out * dl_row + _dot_tn(docb, qe_s[h, rows, :])
            return 0

        lax.fori_loop(0, nc, chunk, 0, unroll=4)
        for h in range(HG_HEADS):
            cols = slice(h * hd, (h + 1) * hd)
            p = prepared[h]
            dq_ref[:, cols] = (dqh_s[h] * (p["qsig"] * (1.0 + p["qv"] * (1.0 - p["qsig"])))).astype(BF16)
            dforget = dlf_s[h] / p["forget"] - dkk_s[h]
            df_ref[:, cols] = (dforget * (1.0 - p["lbh"]) * p["sg"] * (1.0 - p["sg"])).astype(BF16)
            dlb_ref[:, cols] += jnp.sum(dforget * (1.0 - p["sg"]), axis=0, keepdims=True)

    blk = lambda col: pl.BlockSpec((tb, HG_WIDTH), lambda i: (nb - 1 - i, col))
    vec = pl.BlockSpec((1, HG_WIDTH), lambda i: (0, 0))
    head_f32 = pltpu.VMEM((HG_HEADS, tb, hd), F32)
    head_bf16 = pltpu.VMEM((HG_HEADS, tb, hd), BF16)
    return pl.pallas_call(
        body,
        name=name,
        grid=(nb,),
        in_specs=[
            blk(3), blk(4), blk(5),
            pl.BlockSpec((2, HG_WIDTH), lambda i: (0, 0)),
            pl.BlockSpec((nc, HG_HEADS, hd, hd), lambda i: (nb - 1 - i, 0, 0, 0)),
            blk(0),
        ],
        out_specs=[blk(0), blk(0), blk(0), vec],
        out_shape=[jax.ShapeDtypeStruct((t, HG_WIDTH), BF16)] * 3 + [jax.ShapeDtypeStruct((1, HG_WIDTH), F32)],
        scratch_shapes=[
            pltpu.VMEM((HG_HEADS, hd, hd), F32),
            head_f32, head_f32, head_f32, head_f32, head_f32, head_bf16, head_bf16, head_f32,
            head_f32, head_f32, head_f32,
        ],
        compiler_params=_params("arbitrary"),
    )(proj, proj, proj, logits, states, do)


def _group_mat(width, head_dim):
    r = lax.broadcasted_iota(jnp.int32, (width, width), 0)
    c = lax.broadcasted_iota(jnp.int32, (width, width), 1)
    return ((r // head_dim) == (c // head_dim)).astype(BF16)


def _head_mean(x, mat, head_dim):
    hi = x.astype(BF16)
    lo = (x - hi.astype(F32)).astype(BF16)
    return (_dot(hi, mat) + _dot(lo, mat)) * (1.0 / head_dim)


def _mix_out_fwd(o_sb, o_hg, proj, g_sb, g_hg, w_out, x1, *, name, tm=512):
    t = x1.shape[0]

    def body(osb_ref, ohg_ref, gate_ref, gsb_ref, ghg_ref, w_ref, x_ref, xo_ref, mt_ref):
        msb = _group_mat(SB_WIDTH, SB_HEAD_DIM)
        mhg = _group_mat(HG_WIDTH, HG_HEAD_DIM)
        osb = osb_ref[...]
        ohg = ohg_ref[...]
        nsb = osb * lax.rsqrt(_head_mean(osb * osb, msb, SB_HEAD_DIM) + EPS) * gsb_ref[...]
        gate = gate_ref[...]
        nhg = ohg * lax.rsqrt(_head_mean(ohg * ohg, mhg, HG_HEAD_DIM) + EPS) * ghg_ref[...] * (gate * _sigmoid(gate))
        mixed = jnp.concatenate([nsb, nhg], axis=1).astype(BF16)
        mt_ref[...] = mixed
        xo_ref[...] = x_ref[...] + _dot(mixed, w_ref[...])

    half = pl.BlockSpec((tm, SB_WIDTH), lambda i: (i, 0))
    vec = pl.BlockSpec((1, SB_WIDTH), lambda i: (0, 0))
    row = pl.BlockSpec((tm, D_MODEL), lambda i: (i, 0))
    return pl.pallas_call(
        body,
        name=name,
        grid=(t // tm,),
        in_specs=[half, half, pl.BlockSpec((tm, HG_WIDTH), lambda i: (i, 6)), vec, vec,
                  pl.BlockSpec((D_MODEL, D_MODEL), lambda i: (0, 0)), row],
        out_specs=[row, row],
        out_shape=[jax.ShapeDtypeStruct((t, D_MODEL), F32), jax.ShapeDtypeStruct((t, D_MODEL), BF16)],
        compiler_params=_params("parallel"),
    )(o_sb, o_hg, proj, g_sb, g_hg, w_out, x1)


def _mix_out_bwd(dx2, o_sb, o_hg, proj, g_sb, g_hg, w_out, *, name, tm=512):
    t = dx2.shape[0]

    def body(dx_ref, osb_ref, ohg_ref, gate_ref, gsb_ref, ghg_ref, w_ref, dosb_ref, dohg_ref, dgate_ref, dgsb_ref,
             dghg_ref, dxb_ref):
        i = pl.program_id(0)
        msb = _group_mat(SB_WIDTH, SB_HEAD_DIM)
        mhg = _group_mat(HG_WIDTH, HG_HEAD_DIM)
        dxb = dx_ref[...].astype(BF16)
        dxb_ref[...] = dxb
        dmixed = _dot_nt(dxb, w_ref[...])
        dnsb = dmixed[:, :SB_WIDTH]
        dy = dmixed[:, SB_WIDTH:]

        osb = osb_ref[...]
        rstd = lax.rsqrt(_head_mean(osb * osb, msb, SB_HEAD_DIM) + EPS)
        ohat = osb * rstd
        part_sb = jnp.sum(dnsb * ohat, axis=0, keepdims=True)
        dohat = dnsb * gsb_ref[...]
        dosb_ref[...] = rstd * (dohat - ohat * _head_mean(dohat * ohat, msb, SB_HEAD_DIM))

        ohg = ohg_ref[...]
        rstd = lax.rsqrt(_head_mean(ohg * ohg, mhg, HG_HEAD_DIM) + EPS)
        ohat = ohg * rstd
        gate = gate_ref[...]
        sig = _sigmoid(gate)
        dn = dy * (gate * sig)
        dgate_ref[...] = (dy * (ohat * ghg_ref[...]) * (sig * (1.0 + gate * (1.0 - sig)))).astype(BF16)
        part_hg = jnp.sum(dn * ohat, axis=0, keepdims=True)
        dohat = dn * ghg_ref[...]
        dohg_ref[...] = rstd * (dohat - ohat * _head_mean(dohat * ohat, mhg, HG_HEAD_DIM))

        @pl.when(i == 0)
        def _():
            dgsb_ref[...] = part_sb
            dghg_ref[...] = part_hg

        @pl.when(i > 0)
        def _():
            dgsb_ref[...] += part_sb
            dghg_ref[...] += part_hg

    half = pl.BlockSpec((tm, SB_WIDTH), lambda i: (i, 0))
    vec = pl.BlockSpec((1, SB_WIDTH), lambda i: (0, 0))
    row = pl.BlockSpec((tm, D_MODEL), lambda i: (i, 0))
    return pl.pallas_call(
        body,
        name=name,
        grid=(t // tm,),
        in_specs=[row, half, half, pl.BlockSpec((tm, HG_WIDTH), lambda i: (i, 6)), vec, vec,
                  pl.BlockSpec((D_MODEL, D_MODEL), lambda i: (0, 0))],
        out_specs=[half, half, half, vec, vec, row],
        out_shape=[jax.ShapeDtypeStruct((t, SB_WIDTH), F32)] * 2 + [jax.ShapeDtypeStruct((t, SB_WIDTH), BF16)]
        + [jax.ShapeDtypeStruct((1, SB_WIDTH), F32)] * 2 + [jax.ShapeDtypeStruct((t, D_MODEL), BF16)],
        compiler_params=_params("arbitrary"),
    )(dx2, o_sb, o_hg, proj, g_sb, g_hg, w_out)


def _local_step(x, target, norms, logits, w, weights_after=None, grads_ready=None):
    w = dict(w)
    x1, a1, b1, h1, s1, hm = _ffn_fwd(x, norms["ffn1"], w["g1t"], w["u1t"], w["d1"], name="ffn1_fwd",
                                      next_gain=norms["mix"])
    if weights_after is not None:
        w.update(weights_after("ffn1", x1))
    proj = _mm(hm, w["int"], name="in_proj", tm=512, tn=IN_COLS, nt=True)
    o_sb, sb_kept = _attn_fwd(proj, name="sb_attn_fwd")
    o_hg, states = _hgrn_fwd(proj, logits, name="hgrn2_fwd")
    x2, mixed = _mix_out_fwd(o_sb, o_hg, proj, norms["sb"], norms["hg"], w["out"], x1, name="mix_out_fwd")
    if weights_after is not None:
        w.update(weights_after("mix", x2))
    dx3, a2, b2, h2, s2, d_final, loss_row = _ffn_fwd(x2, norms["ffn2"], w["g2t"], w["u2t"], w["d2"], name="ffn2_fwd",
                                                      head=(norms["final"], target))

    def weight_grad(lhs, rhs, name, tie=None):
        return _mm(lhs, rhs, name=name, tm=256, tn=D_MODEL, ta=True, out_dtype=BF16, tie=tie)

    def sent(stage):
        return grads_ready(stage, gw) if grads_ready is not None else None

    gw, gv = {}, {"final": d_final}
    dx2, gv["ffn2"], da2, db2, dob2 = _ffn_bwd(dx3, x2, norms["ffn2"], a2, b2, w["g2t"], w["u2t"], w["d2"],
                                               name="ffn2_bwd")
    gw["g2t"] = weight_grad(da2, h2, "ffn2_dgate")
    gw["u2t"] = weight_grad(db2, h2, "ffn2_dup")
    gw["d2"] = weight_grad(s2, dob2, "ffn2_ddown")

    do_sb, do_hg, d_gate, gv["sb"], gv["hg"], dx2b = _mix_out_bwd(
        dx2, o_sb, o_hg, proj, norms["sb"], norms["hg"], w["out"], name="mix_out_bwd")
    gw["out"] = weight_grad(mixed, dx2b, "out_dw")
    tie = sent("mix")
    dq_sb, dk_sb, dv_sb = _attn_bwd(proj, sb_kept, do_sb, name="sb_attn_bwd", tie=tie)
    dq_hg, df_hg, di_hg, d_lb = _hgrn_bwd(proj, logits if tie is None else logits + tie[0, 0], states, do_hg,
                                          name="hgrn2_bwd")
    dproj = jnp.concatenate([dq_sb, dk_sb.astype(BF16), dv_sb.astype(BF16), dq_hg, df_hg, di_hg, d_gate], axis=1)
    gw["int"] = weight_grad(dproj, hm, "in_dw")
    tie = sent("in")
    dx1, gv["mix"], dob1 = _in_proj_bwd(dproj, w["int"], x1, norms["mix"] if tie is None else norms["mix"] + tie[0, 0],
                                        dx2, name="in_dx")

    gw["d1"] = weight_grad(s1, dob1, "ffn1_ddown")
    tie = sent("d1")
    da1, db1 = _ffn_bwd_act(dob1, a1, b1, w["d1"], name="ffn1_bwd_act",
                            tie=jnp.zeros((8, LANES), F32) if tie is None else tie)
    gw["g1t"] = weight_grad(da1, h1, "ffn1_dgate")
    gw["u1t"] = weight_grad(db1, h1, "ffn1_dup", tie=sent("g1t"))
    tie = sent("u1t")
    dx, gv["ffn1"] = _ffn_bwd_in(dx1, x, norms["ffn1"] if tie is None else norms["ffn1"] + tie[0, 0], da1, db1,
                                 w["g1t"], w["u1t"], name="ffn1_bwd_in")
    gv["lb"] = d_lb
    return loss_row, dx, gw, gv


HBM = pl.BlockSpec(memory_space=pl.ANY)


def _place():
    return lax.axis_index("x"), lax.axis_index("y"), lax.axis_index("c")


def _slot(px, py, pc):
    return 4 * px + 2 * py + pc


GATHER_COPIES = 8


def _all_gather(blocks, *, name):
    n = len(blocks)

    def body(*refs):
        ins, outs = refs[:n], refs[n:2 * n]
        send_sems, recv_sems, local_sems = refs[2 * n:]
        x, y, c = _place()
        me, sibling = (x, y, c), (x, y, 1 - c)
        beside, across, diagonal = (1 - x, y, c), (x, 1 - y, c), (1 - x, 1 - y, c)

        def copy(a, k, block, to, src=None, half=None):
            dst = outs[a].at[_slot(*block)]
            if half is not None:
                rows = blocks[a].shape[0] // 2
                dst = dst.at[pl.ds(half * rows, rows)]
            return pltpu.make_async_remote_copy(
                src_ref=dst if src is None else src, dst_ref=dst, send_sem=send_sems.at[GATHER_COPIES * a + k],
                recv_sem=recv_sems.at[GATHER_COPIES * a + k], device_id=to, device_id_type=MESH)

        mine = [pltpu.make_async_copy(ins[a], outs[a].at[_slot(*me)], local_sems.at[a]) for a in range(n)]
        for cp in mine:
            cp.start()
        sent = []
        for a in range(n):
            sent += [copy(a, 0, me, sibling, src=ins[a]), copy(a, 1, me, beside, src=ins[a]),
                     copy(a, 2, me, across, src=ins[a])]
        for cp in sent:
            cp.start()
        for a in range(n):
            copy(a, 1, beside, me).wait_recv()
            sent += [copy(a, 3, beside, across, half=0), copy(a, 5, beside, sibling)]
            sent[-2].start()
            sent[-1].start()
        for a in range(n):
            copy(a, 2, across, me).wait_recv()
            sent += [copy(a, 4, across, beside, half=1), copy(a, 6, across, sibling)]
            sent[-2].start()
            sent[-1].start()
        for a in range(n):
            copy(a, 3, diagonal, me, half=0).wait_recv()
            copy(a, 4, diagonal, me, half=1).wait_recv()
            sent.append(copy(a, 7, diagonal, sibling))
            sent[-1].start()
        for a in range(n):
            for k, origin in ((0, sibling), (5, (1 - x, y, 1 - c)), (6, (x, 1 - y, 1 - c)), (7, (1 - x, 1 - y, 1 - c))):
                copy(a, k, origin, me).wait_recv()
        for cp in sent:
            cp.wait_send()
        for cp in mine:
            cp.wait()

    return pl.pallas_call(
        body,
        name=name,
        in_specs=[HBM] * n,
        out_specs=[HBM] * n,
        out_shape=[jax.ShapeDtypeStruct((N_DEV,) + b.shape, b.dtype) for b in blocks],
        scratch_shapes=[pltpu.SemaphoreType.DMA((GATHER_COPIES * n,)), pltpu.SemaphoreType.DMA((GATHER_COPIES * n,)),
                        pltpu.SemaphoreType.DMA((n,))],
    )(*blocks)


def _flipped(place, d):
    return tuple(1 - p if (d >> (2 - axis)) & 1 else p for axis, p in enumerate(place))


SEM = pl.BlockSpec(memory_space=pltpu.SEMAPHORE)
EFFECT = pltpu.SideEffectType.DATAFLOW_SIDE_EFFECTING


def _split_copies(me, srcs, lands, send_sems, recv_sems, by_owner):
    copies = []
    for d in range(1, N_DEV):
        peer = _flipped(me, d)
        for a, (src, land) in enumerate(zip(srcs, lands)):
            copies.append(pltpu.make_async_remote_copy(
                src_ref=src.at[_slot(*peer)] if by_owner else src, dst_ref=land.at[_slot(*me)],
                send_sem=send_sems.at[7 * a + d - 1], recv_sem=recv_sems.at[7 * a + d - 1], device_id=peer,
                device_id_type=MESH))
    own = [pltpu.make_async_copy(src.at[_slot(*me)] if by_owner else src, land.at[_slot(*me)],
                                 recv_sems.at[7 * len(srcs) + a]) for a, (src, land) in enumerate(zip(srcs, lands))]
    return copies, own


def _copies_start(srcs, *, name, by_owner, after=None):
    n = len(srcs)
    extra = [] if after is None else [after]
    land_shapes = [s.shape if by_owner else (N_DEV,) + s.shape for s in srcs]
    lands = [pltpu.with_memory_space_constraint(lax.empty(shape, s.dtype), pltpu.HBM) for shape, s in zip(land_shapes, srcs)]
    srcs = [pltpu.with_memory_space_constraint(s, pltpu.HBM) for s in srcs]

    def body(*refs):
        src_refs, land_refs = refs[:n], refs[n:2 * n]
        send_sems, recv_sems = refs[2 * n + len(extra)], refs[2 * n + len(extra) + 1]
        token = refs[-1]
        copies, own = _split_copies(_place(), src_refs, land_refs, send_sems, recv_sems, by_owner)
        for cp in copies + own:
            cp.start()
        token[...] = jnp.zeros_like(token)

    out = pl.pallas_call(
        body,
        name=name,
        in_specs=[HBM] * (2 * n + len(extra)),
        out_specs=[SEM, SEM] + [HBM] * (2 * n) + [pl.BlockSpec(memory_space=pltpu.VMEM)],
        out_shape=[pltpu.SemaphoreType.DMA((7 * n,)), pltpu.SemaphoreType.DMA((8 * n,))]
        + [pltpu.HBM(s.shape, s.dtype) for s in srcs] + [pltpu.HBM(shape, s.dtype) for shape, s in zip(land_shapes, srcs)]
        + [jax.ShapeDtypeStruct((8, LANES), F32)],
        input_output_aliases={i: 2 + i for i in range(2 * n)},
        compiler_params=pltpu.CompilerParams(has_side_effects=EFFECT),
    )(*srcs, *lands, *extra)
    return (out[0], out[1], out[2:2 + n], out[2 + n:2 + 2 * n]), out[-1]


def _copies_wait(started, after, *, name, by_owner):
    send_sems, recv_sems, srcs, lands = started
    n = len(srcs)

    def body(*refs):
        src_refs, land_refs = refs[:n], refs[n:2 * n]
        copies, own = _split_copies(_place(), src_refs, land_refs, refs[2 * n], refs[2 * n + 1], by_owner)
        for cp in copies:
            cp.wait_send()
            cp.wait_recv()
        for cp in own:
            cp.wait()

    out = pl.pallas_call(
        body,
        name=name,
        in_specs=[HBM] * (2 * n) + [SEM, SEM, HBM],
        out_specs=[HBM] * (2 * n),
        out_shape=[pltpu.HBM(s.shape, s.dtype) for s in srcs] + [pltpu.HBM(s.shape, s.dtype) for s in lands],
        input_output_aliases={i: i for i in range(2 * n)},
        compiler_params=pltpu.CompilerParams(has_side_effects=EFFECT),
    )(*srcs, *lands, send_sems, recv_sems, after)
    return out[:n], out[n:]


def _adamw(w, g, m, v):
    m = ADAM_B1 * m + (1.0 - ADAM_B1) * g
    v = ADAM_B2 * v + (1.0 - ADAM_B2) * (g * g)
    m_hat = m / (1.0 - ADAM_B1 ** ADAM_STEP)
    v_hat = v / (1.0 - ADAM_B2 ** ADAM_STEP)
    delta = -ADAM_LR * (m_hat / (jnp.sqrt(v_hat) + ADAM_EPS) + ADAM_WD * w)
    return delta, m, v


def _sum_and_update(parts, w, m, v, *, name, tie=None):
    _, rows, cols = w.shape
    tr = rows // 2

    def body(p_ref, w_ref, m_ref, v_ref, *rest):
        g_ref, d_ref, mo_ref, vo_ref = rest[-4:]
        g = p_ref[0].astype(F32)
        for s in range(1, N_DEV):
            g = g + p_ref[s].astype(F32)
        g_ref[0] = g
        d_ref[0], mo_ref[0], vo_ref[0] = _adamw(w_ref[0], g, m_ref[0], v_ref[0])

    flat = pl.BlockSpec((1, tr, cols), lambda i: (0, i, 0))
    return pl.pallas_call(
        body,
        name=name,
        grid=(rows // tr,),
        in_specs=[pl.BlockSpec((N_DEV, tr, cols), lambda i: (0, i, 0)), flat, flat, flat]
        + ([] if tie is None else [pl.BlockSpec(memory_space=pl.ANY)]),
        out_specs=[flat] * 4,
        out_shape=[jax.ShapeDtypeStruct((1, rows, cols), F32)] * 4,
        compiler_params=_params("parallel"),
    )(parts, w, m, v, *([] if tie is None else [tie]))


VEC_ROWS = 8
ROW_LOGITS, ROW_LOSS = 5, 7


def _vectors_update(part, w, m, v, *, name, tie):
    def body(p_ref, w_ref, m_ref, v_ref, tie_ref, g_ref, d_ref, mo_ref, vo_ref, loss_ref, all_ref, send_sems, recv_sems):
        me = _place()
        all_ref[_slot(*me)] = p_ref[...]
        copies = []
        for d in range(1, N_DEV):
            peer = _flipped(me, d)
            copies.append(pltpu.make_async_remote_copy(
                src_ref=p_ref, dst_ref=all_ref.at[_slot(*me)], send_sem=send_sems.at[d - 1], recv_sem=recv_sems.at[d - 1],
                device_id=peer, device_id_type=MESH))
        for cp in copies:
            cp.start()
        for cp in copies:
            cp.wait()
        total = all_ref[0]
        for s in range(1, N_DEV):
            total = total + all_ref[s]
        wv = w_ref[...]
        half = D_MODEL // 2
        lb = _sigmoid(wv[ROW_LOGITS:ROW_LOGITS + 1, :half] - wv[ROW_LOGITS:ROW_LOGITS + 1, half:])
        d_first = total[ROW_LOGITS:ROW_LOGITS + 1, :half] * lb * (1.0 - lb)
        d_logits = jnp.concatenate([d_first, -d_first], axis=1)
        rowi = lax.broadcasted_iota(jnp.int32, (VEC_ROWS, D_MODEL), 0)
        g = jnp.where(rowi == ROW_LOGITS, d_logits, jnp.where(rowi < ROW_LOGITS, total, 0.0))
        g_ref[...] = g
        d_ref[...], mo_ref[...], vo_ref[...] = _adamw(wv, g, m_ref[...], v_ref[...])
        loss_ref[...] = total[ROW_LOSS:ROW_LOSS + 1, :]

    vmem = pl.BlockSpec(memory_space=pltpu.VMEM)
    return pl.pallas_call(
        body,
        name=name,
        in_specs=[vmem] * 4 + [HBM],
        out_specs=[vmem] * 5,
        out_shape=[jax.ShapeDtypeStruct((VEC_ROWS, D_MODEL), F32)] * 4 + [jax.ShapeDtypeStruct((1, D_MODEL), F32)],
        scratch_shapes=[pltpu.VMEM((N_DEV, VEC_ROWS, D_MODEL), F32), pltpu.SemaphoreType.DMA((7,)),
                        pltpu.SemaphoreType.DMA((7,))],
    )(part, w, m, v, tie)


TRANSPOSED = ("g1t", "u1t", "g2t", "u2t", "int")


def _vector_rows(rows):
    rowi = lax.broadcasted_iota(jnp.int32, (VEC_ROWS, D_MODEL), 0)
    out = jnp.zeros((VEC_ROWS, D_MODEL), F32)
    for i, r in enumerate(rows):
        if r is not None:
            out = jnp.where(rowi == i, r, out)
    return out


def kernel(x, ffn1_norm, ffn1_w_gate, ffn1_w_up, ffn1_w_down, mix_norm, w_in, sb_out_norm, hg_lower_bound_logits, hg_out_norm, w_out, ffn2_norm, ffn2_w_gate, ffn2_w_up, ffn2_w_down, final_norm, loss_target, m_ffn1_norm, m_ffn1_w_gate, m_ffn1_w_up, m_ffn1_w_down, m_mix_norm, m_w_in, m_sb_out_norm, m_hg_lower_bound_logits, m_hg_out_norm, m_w_out, m_ffn2_norm, m_ffn2_w_gate, m_ffn2_w_up, m_ffn2_w_down, m_final_norm, v_ffn1_norm, v_ffn1_w_gate, v_ffn1_w_up, v_ffn1_w_down, v_mix_norm, v_w_in, v_sb_out_norm, v_hg_lower_bound_logits, v_hg_out_norm, v_w_out, v_ffn2_norm, v_ffn2_w_gate, v_ffn2_w_up, v_ffn2_w_down, v_final_norm):
    def matrices(g1, u1, d1, win, wout, g2, u2, d2):
        return {"g1t": g1, "u1t": u1, "d1": d1, "int": win, "out": wout, "g2t": g2, "u2t": u2, "d2": d2}

    def vectors(n1, nm, nsb, lg, nhg, n2, nf):
        return [n1, nm, n2, nf.reshape(1, D_MODEL), jnp.concatenate([nsb, nhg], axis=1), lg.reshape(1, D_MODEL), None, None]

    w_sh = matrices(ffn1_w_gate, ffn1_w_up, ffn1_w_down, w_in, w_out, ffn2_w_gate, ffn2_w_up, ffn2_w_down)
    m_sh = matrices(m_ffn1_w_gate, m_ffn1_w_up, m_ffn1_w_down, m_w_in, m_w_out, m_ffn2_w_gate, m_ffn2_w_up, m_ffn2_w_down)
    v_sh = matrices(v_ffn1_w_gate, v_ffn1_w_up, v_ffn1_w_down, v_w_in, v_w_out, v_ffn2_w_gate, v_ffn2_w_up, v_ffn2_w_down)
    keys = list(w_sh)

    def full(key, stack):
        return stack.reshape(-1, D_MODEL)

    def by_owner(key, grad):
        return grad.reshape(N_DEV, -1, D_MODEL)

    def view(key, a):
        return jnp.swapaxes(a, 1, 2) if key in TRANSPOSED else a

    blocks = {k: view(k, w_sh[k])[0].astype(BF16) for k in keys}
    first, mid, last = ("g1t", "u1t", "d1"), ("int", "out"), ("g2t", "u2t", "d2")
    w_first = {k: full(k, s) for k, s in zip(first, _all_gather([blocks[k] for k in first], name="gather_ffn1"))}
    flights = {}
    flights["ffn1"], token_mid = _copies_start([blocks[k] for k in mid], name="gather_mid_start", by_owner=False,
                                               after=w_first["d1"])
    flights["mix"], token_last = _copies_start([blocks[k] for k in last], name="gather_ffn2_start", by_owner=False,
                                               after=token_mid)

    def weights_after(stage, result):
        group = mid if stage == "ffn1" else last
        _, lands = _copies_wait(flights[stage], result, name="gather_" + stage + "_wait", by_owner=False)
        return {k: full(k, s) for k, s in zip(group, lands)}

    groups = {"mix": ("g2t", "u2t", "d2", "out"), "in": ("int",), "g1t": ("g1t",), "u1t": ("u1t",), "d1": ("d1",)}
    sent = {}

    def grads_ready(stage, gw):
        stacks = [by_owner(k, gw[k]) for k in groups[stage]]
        sent[stage], token = _copies_start(stacks, name="grads_" + stage + "_start", by_owner=True)
        return token

    norms = {"ffn1": ffn1_norm + token_last[0, 0], "mix": mix_norm, "sb": sb_out_norm, "hg": hg_out_norm,
             "ffn2": ffn2_norm, "final": final_norm.reshape(1, D_MODEL)}
    loss_row, grad_x, gw, gv = _local_step(x[0], loss_target[0], norms, hg_lower_bound_logits, w_first, weights_after,
                                           grads_ready)

    lb_row = jnp.concatenate([gv["lb"], jnp.zeros_like(gv["lb"])], axis=1)
    part = _vector_rows([gv["ffn1"], gv["mix"], gv["ffn2"], gv["final"], jnp.concatenate([gv["sb"], gv["hg"]], axis=1),
                         lb_row, None, loss_row])
    vec_w = _vector_rows(vectors(ffn1_norm, mix_norm, sb_out_norm, hg_lower_bound_logits, hg_out_norm, ffn2_norm, final_norm))
    vec_m = _vector_rows(vectors(m_ffn1_norm, m_mix_norm, m_sb_out_norm, m_hg_lower_bound_logits, m_hg_out_norm,
                                 m_ffn2_norm, m_final_norm))
    vec_v = _vector_rows(vectors(v_ffn1_norm, v_mix_norm, v_sb_out_norm, v_hg_lower_bound_logits, v_hg_out_norm,
                                 v_ffn2_norm, v_final_norm))
    updated, after = {}, grad_x
    for stage, flight in sent.items():
        if stage == list(sent)[-1]:
            *vecs, loss_out = _vectors_update(part, vec_w, vec_m, vec_v, name="vectors_update", tie=after)
            after = loss_out
        _, lands = _copies_wait(flight, after, name="grads_" + stage + "_wait", by_owner=True)
        for k, part_k in zip(groups[stage], lands):
            updated[k] = _sum_and_update(part_k, view(k, w_sh[k]), view(k, m_sh[k]), view(k, v_sh[k]), name="adamw_" + k,
                                         tie=after)
            after = updated[k][0]
    mats = [{k: view(k, updated[k][i]) for k in keys} for i in range(4)]

    def leaves(mat, vec):
        half = D_MODEL // 2
        return (
            vec[0:1], mat["g1t"], mat["u1t"], mat["d1"], vec[1:2], mat["int"], vec[4:5, :half],
            vec[ROW_LOGITS].reshape(2, half), vec[4:5, half:], mat["out"], vec[2:3], mat["g2t"], mat["u2t"],
            mat["d2"], vec[3],
        )

    out = [loss_out[0, 0], grad_x[None]]
    for mat, vec in zip(mats, vecs):
        out.extend(leaves(mat, vec))
    return tuple(out)
```

```python
import jax
import jax.numpy as jnp
from jax import lax
from jax.experimental import pallas as pl
from jax.experimental.pallas import tpu as pltpu

F32, BF16 = jnp.float32, jnp.bfloat16
D_MODEL = 1024
D_FF = 2816
SB_WIDTH = 512
HG_WIDTH = 512
SB_HEAD_DIM = 64
HG_HEAD_DIM = 128
IN_COLS = 3584
EPS = 1e-6
N_DEV = 8
LANES = 128
HG_CHUNK = 16
VMEM_LIMIT_BYTES = 48 * 1024 * 1024
FFN_BWD_VMEM_LIMIT_BYTES = 56 * 1024 * 1024
ADAM_LR, ADAM_B1, ADAM_B2, ADAM_EPS, ADAM_WD, ADAM_STEP = 0.001, 0.9, 0.999, 1e-08, 0.01, 10
MESH = pl.DeviceIdType.MESH


def _params(*semantics, vmem_limit_bytes=VMEM_LIMIT_BYTES):
    return pltpu.CompilerParams(dimension_semantics=semantics, vmem_limit_bytes=vmem_limit_bytes)


def _dot(a, b):
    return jnp.dot(a, b, preferred_element_type=F32)


def _dot_nt(a, b):
    return lax.dot_general(a, b, (((1,), (1,)), ((), ())), preferred_element_type=F32)


def _dot_tn(a, b):
    return lax.dot_general(a, b, (((0,), (0,)), ((), ())), preferred_element_type=F32)


def _split3(x):
    hi = x.astype(BF16)
    r1 = x - hi.astype(F32)
    mid = r1.astype(BF16)
    lo = (r1 - mid.astype(F32)).astype(BF16)
    return hi, mid, lo


def _rms(xv):
    rstd = lax.rsqrt(jnp.mean(xv * xv, axis=-1, keepdims=True) + EPS)
    return xv * rstd, rstd


def _sigmoid(x):
    return 0.5 + 0.5 * jnp.tanh(0.5 * x)


def _loss_terms(xv, gain, target):
    xhat, rstd = _rms(xv)
    err = xhat * gain - target
    loss = 0.5 * jnp.sum(jnp.mean(err * err, axis=-1, keepdims=True), axis=0, keepdims=True)
    dy = err * (1.0 / xv.shape[-1])
    dxh = dy * gain
    dx = rstd * (dxh - xhat * jnp.mean(dxh * xhat, axis=-1, keepdims=True))
    return dx, jnp.sum(dy * xhat, axis=0, keepdims=True), loss


def _mm(a, b, *, name, tm, tn, nt=False, ta=False, out_dtype=F32, tie=None):
    k, m = a.shape if ta else a.shape[::-1]
    n = b.shape[0] if nt else b.shape[1]
    assert m % tm == 0 and n % tn == 0 and not (nt and ta), (name, a.shape, b.shape, tm, tn)

    def body(a_ref, b_ref, *rest):
        av = a_ref[...].astype(BF16)
        bv = b_ref[...].astype(BF16)
        rest[-1][...] = (_dot_nt(av, bv) if nt else _dot_tn(av, bv) if ta else _dot(av, bv)).astype(out_dtype)

    in_specs = [
        pl.BlockSpec((k, tm), lambda i, j: (0, i)) if ta else pl.BlockSpec((tm, k), lambda i, j: (i, 0)),
        pl.BlockSpec((tn, k), lambda i, j: (j, 0)) if nt else pl.BlockSpec((k, tn), lambda i, j: (0, j)),
    ]
    operands = [a, b]
    if tie is not None:
        in_specs.append(pl.BlockSpec(memory_space=pl.ANY))
        operands.append(tie)
    return pl.pallas_call(
        body,
        name=name,
        grid=(m // tm, n // tn),
        in_specs=in_specs,
        out_specs=pl.BlockSpec((tm, tn), lambda i, j: (i, j)),
        out_shape=jax.ShapeDtypeStruct((m, n), out_dtype),
        compiler_params=_params("parallel", "parallel"),
    )(*operands)


def _ffn_fwd(x, gain, wgt, wut, wd, *, name, next_gain=None, head=None, tm=1024, tf=256):
    t = x.shape[0]
    nj = D_FF // tf
    extra_in = [] if next_gain is None else [next_gain]
    extra_in += [] if head is None else list(head)

    def body(x_ref, g_ref, wg_ref, wu_ref, wd_prev_ref, wd_last_ref, *rest):
        extra, (xo_ref, a_ref, b_ref, h_ref, st_ref) = rest[:len(extra_in)], rest[len(extra_in):len(extra_in) + 5]
        tail_out, (acc, s_prev) = rest[len(extra_in) + 5:-2], rest[-2:]
        i = pl.program_id(0)
        j = pl.program_id(1)

        @pl.when(j == 0)
        def _():
            xhat, _ = _rms(x_ref[...])
            h_ref[...] = (xhat * g_ref[...]).astype(BF16)
            acc[...] = jnp.zeros_like(acc)
            s_prev[...] = jnp.zeros_like(s_prev)

        acc[...] += _dot(s_prev[...], wd_prev_ref[...])
        h = h_ref[...]
        a = _dot_nt(h, wg_ref[...])
        b = _dot_nt(h, wu_ref[...])
        a_ref[...] = a.astype(BF16)
        b_ref[...] = b.astype(BF16)
        s = (a * _sigmoid(a) * b).astype(BF16)
        st_ref[...] = s
        s_prev[...] = s

        @pl.when(j == nj - 1)
        def _():
            xo = x_ref[...] + 0.5 * (acc[...] + _dot(s, wd_last_ref[...]))
            if head is None:
                xo_ref[...] = xo
            if next_gain is not None:
                tail_out[0][...] = (_rms(xo)[0] * extra[0][...]).astype(BF16)
            if head is not None:
                gain_ref, target_ref = extra[-2:]
                dg_ref, loss_ref = tail_out[-2:]
                xo_ref[...], part_g, part_loss = _loss_terms(xo, gain_ref[...], target_ref[...])

                @pl.when(i == 0)
                def _():
                    dg_ref[...] = part_g
                    loss_ref[...] = jnp.broadcast_to(part_loss, loss_ref.shape)

                @pl.when(i > 0)
                def _():
                    dg_ref[...] += part_g
                    loss_ref[...] += jnp.broadcast_to(part_loss, loss_ref.shape)

    row = pl.BlockSpec((tm, D_MODEL), lambda i, j: (i, 0))
    vec = pl.BlockSpec((1, D_MODEL), lambda i, j: (0, 0))
    tile = pl.BlockSpec((tm, tf), lambda i, j: (i, j))
    weights = pl.BlockSpec((tf, D_MODEL), lambda i, j: (j, 0))
    tail_specs = ([] if next_gain is None else [row]) + ([] if head is None else [vec, vec])
    tail_shapes = ([] if next_gain is None else [jax.ShapeDtypeStruct((t, D_MODEL), BF16)]) + (
        [] if head is None else [jax.ShapeDtypeStruct((1, D_MODEL), F32)] * 2)
    return pl.pallas_call(
        body,
        name=name,
        grid=(t // tm, nj),
        in_specs=[
            row, vec, weights, weights,
            pl.BlockSpec((tf, D_MODEL), lambda i, j: (jnp.maximum(j - 1, 0), 0)),
            pl.BlockSpec((tf, D_MODEL), lambda i, j: (nj - 1, 0)),
        ] + ([] if next_gain is None else [vec]) + ([] if head is None else [vec, row]),
        out_specs=[row, tile, tile, row, tile] + tail_specs,
        out_shape=[
            jax.ShapeDtypeStruct((t, D_MODEL), F32),
            jax.ShapeDtypeStruct((t, D_FF), BF16),
            jax.ShapeDtypeStruct((t, D_FF), BF16),
            jax.ShapeDtypeStruct((t, D_MODEL), BF16),
            jax.ShapeDtypeStruct((t, D_FF), BF16),
        ] + tail_shapes,
        scratch_shapes=[pltpu.VMEM((tm, D_MODEL), F32), pltpu.VMEM((tm, tf), BF16)],
        compiler_params=_params("arbitrary", "arbitrary"),
    )(x, gain, wgt, wut, wd, wd, *extra_in)


def _ffn_fwd_act(x, gain, wgt, wut, *, name, tm=512, tf=D_FF // 2):
    t = x.shape[0]

    def body(x_ref, g_ref, wg_ref, wu_ref, a_ref, b_ref, st_ref, h_ref):
        @pl.when(pl.program_id(1) == 0)
        def _():
            xhat, _ = _rms(x_ref[...])
            h_ref[...] = (xhat * g_ref[...]).astype(BF16)

        h = h_ref[...]
        a = _dot_nt(h, wg_ref[...])
        b = _dot_nt(h, wu_ref[...])
        a_ref[...] = a.astype(BF16)
        b_ref[...] = b.astype(BF16)
        st_ref[...] = (a * _sigmoid(a) * b).astype(BF16)

    row = pl.BlockSpec((tm, D_MODEL), lambda i, j: (i, 0))
    tile = pl.BlockSpec((tm, tf), lambda i, j: (i, j))
    weights = pl.BlockSpec((tf, D_MODEL), lambda i, j: (j, 0))
    return pl.pallas_call(
        body,
        name=name,
        grid=(t // tm, D_FF // tf),
        in_specs=[row, pl.BlockSpec((1, D_MODEL), lambda i, j: (0, 0)), weights, weights],
        out_specs=[tile, tile, tile, row],
        out_shape=[jax.ShapeDtypeStruct((t, D_FF), BF16)] * 3 + [jax.ShapeDtypeStruct((t, D_MODEL), BF16)],
        compiler_params=_params("arbitrary", "arbitrary"),
    )(x, gain, wgt, wut)


def _ffn_fwd_down(x, s, wd, next_gain, *, name, tm=512):
    t = x.shape[0]

    def body(x_ref, s_ref, wd_ref, g_ref, xo_ref, hn_ref):
        xo = x_ref[...] + 0.5 * _dot(s_ref[...], wd_ref[...])
        xo_ref[...] = xo
        hn_ref[...] = (_rms(xo)[0] * g_ref[...]).astype(BF16)

    row = pl.BlockSpec((tm, D_MODEL), lambda i: (i, 0))
    return pl.pallas_call(
        body,
        name=name,
        grid=(t // tm,),
        in_specs=[row, pl.BlockSpec((tm, D_FF), lambda i: (i, 0)),
                  pl.BlockSpec((D_FF, D_MODEL), lambda i: (0, 0), pipeline_mode=pl.Buffered(1)),
                  pl.BlockSpec((1, D_MODEL), lambda i: (0, 0))],
        out_specs=[row, row],
        out_shape=[jax.ShapeDtypeStruct((t, D_MODEL), F32), jax.ShapeDtypeStruct((t, D_MODEL), BF16)],
        compiler_params=_params("parallel"),
    )(x, s, wd, next_gain)


def _ffn_bwd(dout, x, gain, a, b, wgt, wut, wd, *, name, tm=1024, tf=256):
    t = x.shape[0]
    nj = D_FF // tf

    def body(do_ref, x_ref, g_ref, a_ref, b_ref, wg_prev_ref, wu_prev_ref, wg_last_ref, wu_last_ref, wd_ref,
             dx_ref, dg_ref, da_ref, db_ref, dob_ref, dob_scr, dh, da_prev, db_prev):
        i = pl.program_id(0)
        j = pl.program_id(1)

        @pl.when(j == 0)
        def _():
            d = (0.5 * do_ref[...]).astype(BF16)
            dob_scr[...] = d
            dob_ref[...] = d
            dh[...] = jnp.zeros_like(dh)
            da_prev[...] = jnp.zeros_like(da_prev)
            db_prev[...] = jnp.zeros_like(db_prev)

        dh[...] += _dot(da_prev[...], wg_prev_ref[...]) + _dot(db_prev[...], wu_prev_ref[...])
        ds = _dot_nt(dob_scr[...], wd_ref[...])
        av = a_ref[...].astype(F32)
        bv = b_ref[...].astype(F32)
        sig = _sigmoid(av)
        dbv = (ds * (av * sig)).astype(BF16)
        dav = (ds * bv * (sig * (1.0 + av * (1.0 - sig)))).astype(BF16)
        da_ref[...] = dav
        db_ref[...] = dbv
        da_prev[...] = dav
        db_prev[...] = dbv

        @pl.when(j == nj - 1)
        def _():
            xhat, rstd = _rms(x_ref[...])
            dhv = dh[...] + _dot(dav, wg_last_ref[...]) + _dot(dbv, wu_last_ref[...])
            part = jnp.sum(dhv * xhat, axis=0, keepdims=True)

            @pl.when(i == 0)
            def _():
                dg_ref[...] = part

            @pl.when(i > 0)
            def _():
                dg_ref[...] += part

            dxh = dhv * g_ref[...]
            dx_ref[...] = do_ref[...] + rstd * (dxh - xhat * jnp.mean(dxh * xhat, axis=-1, keepdims=True))

    return pl.pallas_call(
        body,
        name=name,
        grid=(t // tm, nj),
        in_specs=[
            pl.BlockSpec((tm, D_MODEL), lambda i, j: (i, 0)),
            pl.BlockSpec((tm, D_MODEL), lambda i, j: (i, 0)),
            pl.BlockSpec((1, D_MODEL), lambda i, j: (0, 0)),
            pl.BlockSpec((tm, tf), lambda i, j: (i, j)),
            pl.BlockSpec((tm, tf), lambda i, j: (i, j)),
            pl.BlockSpec((tf, D_MODEL), lambda i, j: (jnp.maximum(j - 1, 0), 0)),
            pl.BlockSpec((tf, D_MODEL), lambda i, j: (jnp.maximum(j - 1, 0), 0)),
            pl.BlockSpec((tf, D_MODEL), lambda i, j: (nj - 1, 0)),
            pl.BlockSpec((tf, D_MODEL), lambda i, j: (nj - 1, 0)),
            pl.BlockSpec((tf, D_MODEL), lambda i, j: (j, 0)),
        ],
        out_specs=[
            pl.BlockSpec((tm, D_MODEL), lambda i, j: (i, 0)),
            pl.BlockSpec((1, D_MODEL), lambda i, j: (0, 0)),
            pl.BlockSpec((tm, tf), lambda i, j: (i, j)),
            pl.BlockSpec((tm, tf), lambda i, j: (i, j)),
            pl.BlockSpec((tm, D_MODEL), lambda i, j: (i, 0)),
        ],
        out_shape=[
            jax.ShapeDtypeStruct((t, D_MODEL), F32),
            jax.ShapeDtypeStruct((1, D_MODEL), F32),
            jax.ShapeDtypeStruct((t, D_FF), BF16),
            jax.ShapeDtypeStruct((t, D_FF), BF16),
            jax.ShapeDtypeStruct((t, D_MODEL), BF16),
        ],
        scratch_shapes=[pltpu.VMEM((tm, D_MODEL), BF16), pltpu.VMEM((tm, D_MODEL), F32), pltpu.VMEM((tm, tf), BF16),
                        pltpu.VMEM((tm, tf), BF16)],
        compiler_params=_params("arbitrary", "arbitrary", vmem_limit_bytes=FFN_BWD_VMEM_LIMIT_BYTES),
    )(dout, x, gain, a, b, wgt, wut, wgt, wut, wd)


def _ffn_bwd_act(dout_half, a, b, wd, *, name, tie, tm=512, tf=D_FF // 2):
    t = a.shape[0]

    def body(dob_ref, a_ref, b_ref, wd_ref, tie_ref, da_ref, db_ref):
        ds = _dot_nt(dob_ref[...], wd_ref[...])
        av = a_ref[...].astype(F32)
        bv = b_ref[...].astype(F32)
        sig = _sigmoid(av)
        db_ref[...] = (ds * (av * sig)).astype(BF16)
        da_ref[...] = (ds * bv * (sig * (1.0 + av * (1.0 - sig)))).astype(BF16)

    tile = pl.BlockSpec((tm, tf), lambda i, j: (i, j))
    return pl.pallas_call(
        body,
        name=name,
        grid=(t // tm, D_FF // tf),
        in_specs=[pl.BlockSpec((tm, D_MODEL), lambda i, j: (i, 0)), tile, tile,
                  pl.BlockSpec((tf, D_MODEL), lambda i, j: (j, 0)), pl.BlockSpec(memory_space=pl.ANY)],
        out_specs=[tile, tile],
        out_shape=[jax.ShapeDtypeStruct((t, D_FF), BF16)] * 2,
        compiler_params=_params("parallel", "parallel"),
    )(dout_half, a, b, wd, tie)


def _ffn_bwd_in(dout, x, gain, da, db, wgt, wut, *, name, tm=512):
    t = x.shape[0]

    def body(do_ref, x_ref, g_ref, da_ref, db_ref, wg_ref, wu_ref, dx_ref, dg_ref):
        i = pl.program_id(0)
        dhv = _dot(da_ref[...], wg_ref[...]) + _dot(db_ref[...], wu_ref[...])
        xhat, rstd = _rms(x_ref[...])
        part = jnp.sum(dhv * xhat, axis=0, keepdims=True)

        @pl.when(i == 0)
        def _():
            dg_ref[...] = part

        @pl.when(i > 0)
        def _():
            dg_ref[...] += part

        dxh = dhv * g_ref[...]
        dx_ref[...] = do_ref[...] + rstd * (dxh - xhat * jnp.mean(dxh * xhat, axis=-1, keepdims=True))

    row = pl.BlockSpec((tm, D_MODEL), lambda i: (i, 0))
    vec = pl.BlockSpec((1, D_MODEL), lambda i: (0, 0))
    tile = pl.BlockSpec((tm, D_FF), lambda i: (i, 0))
    weights = pl.BlockSpec((D_FF, D_MODEL), lambda i: (0, 0), pipeline_mode=pl.Buffered(1))
    return pl.pallas_call(
        body,
        name=name,
        grid=(t // tm,),
        in_specs=[row, row, vec, tile, tile, weights, weights],
        out_specs=[row, vec],
        out_shape=[jax.ShapeDtypeStruct((t, D_MODEL), F32), jax.ShapeDtypeStruct((1, D_MODEL), F32)],
        compiler_params=_params("arbitrary", vmem_limit_bytes=FFN_BWD_VMEM_LIMIT_BYTES),
    )(dout, x, gain, da, db, wgt, wut)


def _in_proj_bwd(dproj, w_int, x, gain, dres, *, name, tm=512):
    t, k = dproj.shape

    def body(dp_ref, w_ref, x_ref, g_ref, dr_ref, dx_ref, dg_ref, dxb_ref):
        i = pl.program_id(0)
        dhv = _dot(dp_ref[...], w_ref[...])
        xhat, rstd = _rms(x_ref[...])
        part = jnp.sum(dhv * xhat, axis=0, keepdims=True)

        @pl.when(i == 0)
        def _():
            dg_ref[...] = part

        @pl.when(i > 0)
        def _():
            dg_ref[...] += part

        dxh = dhv * g_ref[...]
        dx = dr_ref[...] + rstd * (dxh - xhat * jnp.mean(dxh * xhat, axis=-1, keepdims=True))
        dx_ref[...] = dx
        dxb_ref[...] = (0.5 * dx).astype(BF16)

    row = pl.BlockSpec((tm, D_MODEL), lambda i: (i, 0))
    vec = pl.BlockSpec((1, D_MODEL), lambda i: (0, 0))
    return pl.pallas_call(
        body,
        name=name,
        grid=(t // tm,),
        in_specs=[pl.BlockSpec((tm, k), lambda i: (i, 0)), pl.BlockSpec((k, D_MODEL), lambda i: (0, 0)), row, vec, row],
        out_specs=[row, vec, row],
        out_shape=[jax.ShapeDtypeStruct((t, D_MODEL), F32), jax.ShapeDtypeStruct((1, D_MODEL), F32),
                   jax.ShapeDtypeStruct((t, D_MODEL), BF16)],
        compiler_params=_params("arbitrary"),
    )(dproj, w_int, x, gain, dres)


ATT_Q_TILE = 512
ATT_K_BLOCK = 256


def _first_head_lanes():
    return lax.broadcasted_iota(jnp.int32, (1, LANES), 1) < SB_HEAD_DIM


def _stack_heads(x):
    first = _first_head_lanes()
    return jnp.concatenate([jnp.where(first, x, 0.0), jnp.where(first, 0.0, x)], axis=0)


def _unstack_heads(x, rows):
    return jnp.where(_first_head_lanes(), x[:rows], x[rows:])


def _rows_from(x, first, rows):
    return x if first == 0 else jnp.concatenate([x[first:rows], x[rows + first:]], axis=0)


def _rows_into(full, part, first, rows):
    if first == 0:
        return part
    n = rows - first
    return jnp.concatenate([full[:first], part[:n], full[rows:rows + first], part[n:]], axis=0)


def _tri(n, relation):
    r = lax.broadcasted_iota(jnp.int32, (n, n), 0)
    c = lax.broadcasted_iota(jnp.int32, (n, n), 1)
    return relation(r, c).astype(BF16)


def _scan_dot(x, tri):
    hi = x.astype(BF16)
    lo = (x - hi.astype(F32)).astype(BF16)
    return _dot(jnp.concatenate([hi, lo], axis=1), jnp.concatenate([tri, tri], axis=0))


def _log_terms(z):
    lbeta = jnp.minimum(z, 0.0) - jnp.log(1.0 + jnp.exp(-jnp.abs(z)))
    return lbeta, lbeta - z


def _attn_fwd(proj, *, name):
    t = proj.shape[0]
    tq, tk = ATT_Q_TILE, ATT_K_BLOCK
    diag = tq // tk
    n_pairs = SB_WIDTH // LANES

    def body(q_ref, k_ref, v_ref, o_ref, kept_ref):
        qi = pl.program_id(1)
        q = q_ref[...] * (SB_HEAD_DIM ** -0.5)
        qs = _stack_heads(q).astype(BF16)
        tri = _tri(tk, lambda j, s: j > s)
        trow = lax.broadcasted_iota(jnp.int32, (tq, tk), 0)
        scol = lax.broadcasted_iota(jnp.int32, (tq, tk), 1)

        def block(kb, carry, causal, first=0):
            acc, c = carry
            off = pl.multiple_of(kb * tk, tk)
            lbeta, lrest = _log_terms(_dot_nt(_rows_from(qs, first, tq), k_ref[pl.ds(off, tk), :].astype(BF16)))
            if causal is not None:
                lrest = jnp.where(causal, lrest, 0.0)
            w = jnp.exp(lbeta + (_scan_dot(lrest, tri) + _rows_from(c, first, tq)))
            if causal is not None:
                w = jnp.where(causal, w, 0.0)
            wb = w.astype(BF16)
            kept_ref[0, 0, kb] = _rows_into(jnp.zeros((2 * tq, tk), BF16), wb, first, tq)
            acc = _rows_into(acc, _rows_from(acc, first, tq) + _dot(wb, v_ref[pl.ds(off, tk), :].astype(BF16)), first, tq)
            return acc, _rows_into(c, _rows_from(c, first, tq) + jnp.sum(lrest, axis=1, keepdims=True), first, tq)

        carry = (jnp.zeros((2 * tq, LANES), F32), jnp.zeros((2 * tq, 1), F32))
        n_full = qi * diag
        for j in reversed(range(diag)):
            mask = ((scol + j * tk) < trow)[j * tk:]
            carry = block(n_full + j, carry, jnp.concatenate([mask, mask], axis=0), first=j * tk)

        def odd_tile(carry):
            for j in range(diag):
                carry = block(n_full - 1 - j, carry, None)
            return carry

        carry = lax.cond(qi % 2 == 1, odd_tile, lambda c: c, carry)
        last = n_full - 1 - (qi % 2) * diag

        def step(it, carry):
            for j in range(2 * diag):
                carry = block(last - (2 * diag * it + j), carry, None)
            return carry

        acc, _ = lax.fori_loop(0, qi // 2, step, carry)
        o_ref[...] = _unstack_heads(acc, tq)

    return pl.pallas_call(
        body,
        name=name,
        grid=(n_pairs, t // tq),
        in_specs=[
            pl.BlockSpec((tq, LANES), lambda p, i: (i, p)),
            pl.BlockSpec((t, LANES), lambda p, i: (0, n_pairs + p)),
            pl.BlockSpec((t, LANES), lambda p, i: (0, 2 * n_pairs + p)),
        ],
        out_specs=[pl.BlockSpec((tq, LANES), lambda p, i: (i, p)),
                   pl.BlockSpec((1, 1, t // tk, 2 * tq, tk), lambda p, i: (p, i, 0, 0, 0))],
        out_shape=[jax.ShapeDtypeStruct((t, SB_WIDTH), F32),
                   jax.ShapeDtypeStruct((n_pairs, t // tq, t // tk, 2 * tq, tk), BF16)],
        compiler_params=_params("parallel", "parallel"),
    )(proj, proj, proj)


def _attn_bwd(proj, kept, do, *, name, tie=None):
    t = proj.shape[0]
    tq, tk = ATT_Q_TILE, ATT_K_BLOCK
    diag = tq // tk
    n_pairs = SB_WIDTH // LANES
    scale = SB_HEAD_DIM ** -0.5

    def body(q_ref, k_ref, v_ref, kept_ref, do_ref, *rest):
        dq_ref, dk_ref, dv_ref = rest[-3:]
        qi = pl.program_id(1)

        @pl.when(qi == 0)
        def _():
            dk_ref[...] = jnp.zeros_like(dk_ref)
            dv_ref[...] = jnp.zeros_like(dv_ref)

        qs = _stack_heads(q_ref[...] * scale).astype(BF16)
        dos = _stack_heads(do_ref[...]).astype(BF16)
        before = _tri(tk, lambda s, j: s < j)
        trow = lax.broadcasted_iota(jnp.int32, (tq, tk), 0)
        scol = lax.broadcasted_iota(jnp.int32, (tq, tk), 1)

        def block(kb, carry, causal, first=0):
            dq, cg = carry
            off = pl.multiple_of(kb * tk, tk)
            q_rows, do_rows = _rows_from(qs, first, tq), _rows_from(dos, first, tq)
            wb = _rows_from(kept_ref[0, 0, kb], first, tq)
            kblk = k_ref[pl.ds(off, tk), :].astype(BF16)
            sig = _sigmoid(_dot_nt(q_rows, kblk))
            g = wb.astype(F32) * _dot_nt(do_rows, v_ref[pl.ds(off, tk), :].astype(BF16))
            prior = _scan_dot(g, before) + _rows_from(cg, first, tq)
            dz = g - sig * (g + prior)
            if causal is not None:
                dz = jnp.where(causal, dz, 0.0)
            dzb = dz.astype(BF16)
            dq = _rows_into(dq, _rows_from(dq, first, tq) + _dot(dzb, kblk), first, tq)
            dk_ref[pl.ds(off, tk), :] += _dot_tn(dzb, q_rows)
            dv_ref[pl.ds(off, tk), :] += _dot_tn(wb, do_rows)
            return dq, _rows_into(cg, _rows_from(cg, first, tq) + jnp.sum(g, axis=1, keepdims=True), first, tq)

        n_full = qi * diag

        def step(it, carry):
            for j in range(2 * diag):
                carry = block(2 * diag * it + j, carry, None)
            return carry

        def odd_tile(carry):
            for j in range(diag):
                carry = block(n_full - diag + j, carry, None)
            return carry

        carry = lax.fori_loop(0, qi // 2, step, (jnp.zeros((2 * tq, LANES), F32), jnp.zeros((2 * tq, 1), F32)))
        carry = lax.cond(qi % 2 == 1, odd_tile, lambda c: c, carry)
        for j in range(diag):
            mask = ((scol + j * tk) < trow)[j * tk:]
            carry = block(n_full + j, carry, jnp.concatenate([mask, mask], axis=0), first=j * tk)
        dq_ref[...] = (_unstack_heads(carry[0], tq) * scale).astype(BF16)

    tile_spec = pl.BlockSpec((tq, LANES), lambda p, i: (i, p))
    full_spec = pl.BlockSpec((t, LANES), lambda p, i: (0, p))
    return pl.pallas_call(
        body,
        name=name,
        grid=(n_pairs, t // tq),
        in_specs=[
            tile_spec,
            pl.BlockSpec((t, LANES), lambda p, i: (0, n_pairs + p)),
            pl.BlockSpec((t, LANES), lambda p, i: (0, 2 * n_pairs + p)),
            pl.BlockSpec((1, 1, t // tk, 2 * tq, tk), lambda p, i: (p, i, 0, 0, 0)),
            tile_spec,
        ] + ([] if tie is None else [pl.BlockSpec(memory_space=pl.ANY)]),
        out_specs=[tile_spec, full_spec, full_spec],
        out_shape=[jax.ShapeDtypeStruct((t, SB_WIDTH), BF16)] + [jax.ShapeDtypeStruct((t, SB_WIDTH), F32)] * 2,
        compiler_params=_params("arbitrary", "arbitrary"),
    )(proj, proj, proj, kept, do, *([] if tie is None else [tie]))


HG_BLOCK = 128
HG_HEADS = HG_WIDTH // HG_HEAD_DIM


def _chunk_mats(n):
    r = lax.broadcasted_iota(jnp.int32, (n, n), 0)
    c = lax.broadcasted_iota(jnp.int32, (n, n), 1)
    same = (r // HG_CHUNK) == (c // HG_CHUNK)
    upto = (same & (c <= r)).astype(BF16)
    whole = same.astype(BF16)
    onward = (same & (c >= r)).astype(BF16)
    return upto, whole, onward


def _rows_dot(mat, x):
    return _dot(jnp.concatenate([mat, mat, mat], axis=1), jnp.concatenate(_split3(x), axis=0))


def _split_heads(x):
    return jnp.stack([x[:, h * HG_HEAD_DIM:(h + 1) * HG_HEAD_DIM] for h in range(HG_HEADS)], axis=0)


def _merge_heads(x):
    return jnp.concatenate([x[h] for h in range(HG_HEADS)], axis=1)


def _lower_bound(lg_ref):
    lg = lg_ref[...]
    return _sigmoid(lg[0:1, :] - lg[1:2, :])


def _hgrn_prepare(q_ref, f_ref, lb, h, upto, whole):
    cols = slice(h * HG_HEAD_DIM, (h + 1) * HG_HEAD_DIM)
    lbh = lb[:, cols]
    sg = _sigmoid(f_ref[:, cols])
    forget = lbh + (1.0 - lbh) * sg
    logf = jnp.log(forget)
    kk = (1.0 - lbh) * (1.0 - sg)
    qv = q_ref[:, cols]
    qsig = _sigmoid(qv)
    qh = qv * qsig
    b = _rows_dot(upto, logf)
    blast = _rows_dot(whole, logf)
    return dict(lbh=lbh, sg=sg, forget=forget, kk=kk, qv=qv, qsig=qsig, qh=qh, b=b, eb=jnp.exp(b),
                ekb=jnp.exp(blast - b), dl=jnp.exp(blast))


def _hgrn_fwd(proj, logits, *, name):
    t = proj.shape[0]
    tb = HG_BLOCK
    nc = tb // HG_CHUNK
    hd = HG_HEAD_DIM

    def body(q_ref, f_ref, i_ref, lg_ref, o_ref, st_ref, state, qh_s, kk_s, b_s, qe_s, ke_s, dl_s):
        @pl.when(pl.program_id(0) == 0)
        def _():
            state[...] = jnp.zeros_like(state)

        lb = _lower_bound(lg_ref)
        upto, whole, _ = _chunk_mats(tb)
        for h in range(HG_HEADS):
            p = _hgrn_prepare(q_ref, f_ref, lb, h, upto, whole)
            qh_s[h] = p["qh"]
            kk_s[h] = p["kk"]
            b_s[h] = p["b"]
            qe_s[h] = (p["qh"] * p["eb"]).astype(BF16)
            ke_s[h] = (p["kk"] * p["ekb"]).astype(BF16)
            dl_s[h] = p["dl"]
        rowi = lax.broadcasted_iota(jnp.int32, (HG_HEADS, HG_CHUNK, hd), 1)

        def chunk(c, _):
            r0 = pl.multiple_of(c * HG_CHUNK, HG_CHUNK)
            rows = pl.ds(r0, HG_CHUNK)
            bc = b_s[:, rows, :]
            qc = qh_s[:, rows, :]
            kc = kk_s[:, rows, :]
            vc = _split_heads(i_ref[rows, :])
            s_in = state[...]
            st_ref[c] = s_in
            s_in_b = s_in.astype(BF16)
            qe = qe_s[:, rows, :]
            o = jnp.stack([_dot_nt(qe[h], s_in_b[h]) for h in range(HG_HEADS)], axis=0)
            for s in range(HG_CHUNK):
                pair = jnp.where(rowi >= s, qc * jnp.exp(bc - bc[:, s:s + 1, :]) * kc[:, s:s + 1, :], 0.0)
                o = o + jnp.sum(pair, axis=2, keepdims=True) * vc[:, s:s + 1, :]
            o_ref[rows, :] = _merge_heads(o)
            vcb = vc.astype(BF16)
            ke = ke_s[:, rows, :]
            update = jnp.stack([_dot_tn(vcb[h], ke[h]) for h in range(HG_HEADS)], axis=0)
            state[...] = s_in * dl_s[:, pl.ds(r0, 1), :] + update
            return 0

        lax.fori_loop(0, nc, chunk, 0, unroll=4)

    blk =lambda col: pl.BlockSpec((tb, HG_WIDTH), lambda i: (i, col))
    head_f32 = pltpu.VMEM((HG_HEADS, tb, hd), F32)
    head_bf16 = pltpu.VMEM((HG_HEADS, tb, hd), BF16)
    return pl.pallas_call(
        body,
        name=name,
        grid=(t // tb,),
        in_specs=[blk(3), blk(4), blk(5), pl.BlockSpec((2, HG_WIDTH), lambda i: (0, 0))],
        out_specs=[
            pl.BlockSpec((tb, HG_WIDTH), lambda i: (i, 0)),
            pl.BlockSpec((nc, HG_HEADS, hd, hd), lambda i: (i, 0, 0, 0)),
        ],
        out_shape=[
            jax.ShapeDtypeStruct((t, HG_WIDTH), F32),
            jax.ShapeDtypeStruct((t // HG_CHUNK, HG_HEADS, hd, hd), F32),
        ],
        scratch_shapes=[pltpu.VMEM((HG_HEADS, hd, hd), F32), head_f32, head_f32, head_f32, head_bf16, head_bf16,
                        head_f32],
        compiler_params=_params("arbitrary"),
    )(proj, proj, proj, logits)


def _hgrn_bwd(proj, logits, states, do, *, name):
    t = proj.shape[0]
    tb = HG_BLOCK
    nb = t // tb
    nc = tb // HG_CHUNK
    hd = HG_HEAD_DIM

    def body(q_ref, f_ref, i_ref, lg_ref, st_ref, do_ref, dq_ref, df_ref, di_ref, dlb_ref,
             dstate, qh_s, kk_s, b_s, eb_s, ekb_s, qe_s, ke_s, dl_s, dqh_s, dkk_s, dlf_s):
        step = pl.program_id(0)

        @pl.when(step == 0)
        def _():
            dstate[...] = jnp.zeros_like(dstate)
            dlb_ref[...] = jnp.zeros_like(dlb_ref)

        lb = _lower_bound(lg_ref)
        upto, whole, _ = _chunk_mats(tb)
        prepared = []
        for h in range(HG_HEADS):
            p = _hgrn_prepare(q_ref, f_ref, lb, h, upto, whole)
            prepared.append(p)
            qh_s[h] = p["qh"]
            kk_s[h] = p["kk"]
            b_s[h] = p["b"]
            eb_s[h] = p["eb"]
            ekb_s[h] = p["ekb"]
            qe_s[h] = (p["qh"] * p["eb"]).astype(BF16)
            ke_s[h] = (p["kk"] * p["ekb"]).astype(BF16)
            dl_s[h] = p["dl"]
        rowi = lax.broadcasted_iota(jnp.int32, (HG_CHUNK, hd), 0)
        r16 = lax.broadcasted_iota(jnp.int32, (HG_CHUNK, HG_CHUNK), 0)
        c16 = lax.broadcasted_iota(jnp.int32, (HG_CHUNK, HG_CHUNK), 1)
        onward = (c16 >= r16).astype(BF16)

        def chunk(it, _):
            c = nc - 1 - it
            r0 = pl.multiple_of(c * HG_CHUNK, HG_CHUNK)
            rows = pl.ds(r0, HG_CHUNK)
            for h in range(HG_HEADS):
                cols = slice(h * hd, (h + 1) * hd)
                bc = b_s[h, rows, :]
                qc = qh_s[h, rows, :]
                kc = kk_s[h, rows, :]
                vc = i_ref[rows, cols]
                doc = do_ref[rows, cols]
                s_in = st_ref[c, h]
                ds_out = dstate[h]
                ds_out_b = ds_out.astype(BF16)
                docb = doc.astype(BF16)
                dl_row = dl_s[h, pl.ds(r0, 1), :]
                dqh = _dot(docb, s_in.astype(BF16)) * eb_s[h, rows, :]
                dkk = _dot(vc.astype(BF16), ds_out_b) * ekb_s[h, rows, :]
                dv = _dot_nt(ke_s[h, rows, :], ds_out_b)
                db = dqh * qc - dkk * kc
                dwhole = jnp.sum(dkk * kc, axis=0, keepdims=True) + jnp.sum(ds_out * s_in, axis=0, keepdims=True) * dl_row
                dk_rows, dv_rows = [], []
                for s in range(HG_CHUNK):
                    keep = rowi >= s
                    e = jnp.exp(bc - bc[s:s + 1, :])
                    k_row = kc[s:s + 1, :]
                    pcol = jnp.sum(jnp.where(keep, qc * e * k_row, 0.0), axis=1, keepdims=True)
                    dpcol = jnp.sum(doc * vc[s:s + 1, :], axis=1, keepdims=True)
                    m = jnp.where(keep, e * dpcol, 0.0)
                    y = m * qc
                    dqh = dqh + m * k_row
                    db = db + y * k_row
                    dk_rows.append(jnp.sum(y, axis=0, keepdims=True))
                    dv_rows.append(jnp.sum(pcol * doc, axis=0, keepdims=True))
                dkk_pairs = jnp.concatenate(dk_rows, axis=0)
                dkk = dkk + dkk_pairs
                db = db - dkk_pairs * kc
                dv = dv + jnp.concatenate(dv_rows, axis=0)
                dqh_s[h, rows, :] = dqh
                dkk_s[h, rows, :] = dkk
                dlf_s[h, rows, :] = _rows_dot(onward, db) + dwhole
                di_ref[rows, cols] = dv.astype(BF16)
                dstate[h] = ds_out * dl_row + _dot_tn(docb, qe_s[h, rows, :])
            return 0

        lax.fori_loop(0, nc, chunk, 0, unroll=4)
        for h in range(HG_HEADS):
            cols = slice(h * hd, (h + 1) * hd)
            p = prepared[h]
            dq_ref[:, cols] = (dqh_s[h] * (p["qsig"] * (1.0 + p["qv"] * (1.0 - p["qsig"])))).astype(BF16)
            dforget = dlf_s[h] / p["forget"] - dkk_s[h]
            df_ref[:, cols] = (dforget * (1.0 - p["lbh"]) * p["sg"] * (1.0 - p["sg"])).astype(BF16)
            dlb_ref[:, cols] += jnp.sum(dforget * (1.0 - p["sg"]), axis=0, keepdims=True)

    blk = lambda col: pl.BlockSpec((tb, HG_WIDTH), lambda i: (nb - 1 - i, col))
    vec = pl.BlockSpec((1, HG_WIDTH), lambda i: (0, 0))
    head_f32 = pltpu.VMEM((HG_HEADS, tb, hd), F32)
    head_bf16 = pltpu.VMEM((HG_HEADS, tb, hd), BF16)
    return pl.pallas_call(
        body,
        name=name,
        grid=(nb,),
        in_specs=[
            blk(3), blk(4), blk(5),
            pl.BlockSpec((2, HG_WIDTH), lambda i: (0, 0)),
            pl.BlockSpec((nc, HG_HEADS, hd, hd), lambda i: (nb - 1 - i, 0, 0, 0)),
            blk(0),
        ],
        out_specs=[blk(0), blk(0), blk(0), vec],
        out_shape=[jax.ShapeDtypeStruct((t, HG_WIDTH), BF16)] * 3 + [jax.ShapeDtypeStruct((1, HG_WIDTH), F32)],
        scratch_shapes=[
            pltpu.VMEM((HG_HEADS, hd, hd), F32),
            head_f32, head_f32, head_f32, head_f32, head_f32, head_bf16, head_bf16, head_f32,
            head_f32, head_f32, head_f32,
        ],
        compiler_params=_params("arbitrary"),
    )(proj, proj, proj, logits, states, do)


def _group_mat(width, head_dim):
    r = lax.broadcasted_iota(jnp.int32, (width, width), 0)
    c = lax.broadcasted_iota(jnp.int32, (width, width), 1)
    return ((r // head_dim) == (c // head_dim)).astype(BF16)


def _head_mean(x, mat, head_dim):
    hi = x.astype(BF16)
    lo = (x - hi.astype(F32)).astype(BF16)
    return (_dot(hi, mat) + _dot(lo, mat)) * (1.0 / head_dim)


def _mix_out_fwd(o_sb, o_hg, proj, g_sb, g_hg, w_out, x1, *, name, tm=512):
    t = x1.shape[0]

    def body(osb_ref, ohg_ref, gate_ref, gsb_ref, ghg_ref, w_ref, x_ref, xo_ref, mt_ref):
        msb = _group_mat(SB_WIDTH, SB_HEAD_DIM)
        mhg = _group_mat(HG_WIDTH, HG_HEAD_DIM)
        osb = osb_ref[...]
        ohg = ohg_ref[...]
        nsb = osb * lax.rsqrt(_head_mean(osb * osb, msb, SB_HEAD_DIM) + EPS) * gsb_ref[...]
        gate = gate_ref[...]
        nhg = ohg * lax.rsqrt(_head_mean(ohg * ohg, mhg, HG_HEAD_DIM) + EPS) * ghg_ref[...] * (gate * _sigmoid(gate))
        mixed = jnp.concatenate([nsb, nhg], axis=1).astype(BF16)
        mt_ref[...] = mixed
        xo_ref[...] = x_ref[...] + _dot(mixed, w_ref[...])

    half = pl.BlockSpec((tm, SB_WIDTH), lambda i: (i, 0))
    vec = pl.BlockSpec((1, SB_WIDTH), lambda i: (0, 0))
    row = pl.BlockSpec((tm, D_MODEL), lambda i: (i, 0))
    return pl.pallas_call(
        body,
        name=name,
        grid=(t // tm,),
        in_specs=[half, half, pl.BlockSpec((tm, HG_WIDTH), lambda i: (i, 6)), vec, vec,
                  pl.BlockSpec((D_MODEL, D_MODEL), lambda i: (0, 0)), row],
        out_specs=[row, row],
        out_shape=[jax.ShapeDtypeStruct((t, D_MODEL), F32), jax.ShapeDtypeStruct((t, D_MODEL), BF16)],
        compiler_params=_params("parallel"),
    )(o_sb, o_hg, proj, g_sb, g_hg, w_out, x1)


def _mix_out_bwd(dx2, o_sb, o_hg, proj, g_sb, g_hg, w_out, *, name, tm=512):
    t = dx2.shape[0]

    def body(dx_ref, osb_ref, ohg_ref, gate_ref, gsb_ref, ghg_ref, w_ref, dosb_ref, dohg_ref, dgate_ref, dgsb_ref,
             dghg_ref, dxb_ref):
        i = pl.program_id(0)
        msb = _group_mat(SB_WIDTH, SB_HEAD_DIM)
        mhg = _group_mat(HG_WIDTH, HG_HEAD_DIM)
        dxb = dx_ref[...].astype(BF16)
        dxb_ref[...] = dxb
        dmixed = _dot_nt(dxb, w_ref[...])
        dnsb = dmixed[:, :SB_WIDTH]
        dy = dmixed[:, SB_WIDTH:]

        osb = osb_ref[...]
        rstd = lax.rsqrt(_head_mean(osb * osb, msb, SB_HEAD_DIM) + EPS)
        ohat = osb * rstd
        part_sb = jnp.sum(dnsb * ohat, axis=0, keepdims=True)
        dohat = dnsb * gsb_ref[...]
        dosb_ref[...] = rstd * (dohat - ohat * _head_mean(dohat * ohat, msb, SB_HEAD_DIM))

        ohg = ohg_ref[...]
        rstd = lax.rsqrt(_head_mean(ohg * ohg, mhg, HG_HEAD_DIM) + EPS)
        ohat = ohg * rstd
        gate = gate_ref[...]
        sig = _sigmoid(gate)
        dn = dy * (gate * sig)
        dgate_ref[...] = (dy * (ohat * ghg_ref[...]) * (sig * (1.0 + gate * (1.0 - sig)))).astype(BF16)
        part_hg = jnp.sum(dn * ohat, axis=0, keepdims=True)
        dohat = dn * ghg_ref[...]
        dohg_ref[...] = rstd * (dohat - ohat * _head_mean(dohat * ohat, mhg, HG_HEAD_DIM))

        @pl.when(i == 0)
        def _():
            dgsb_ref[...] = part_sb
            dghg_ref[...] = part_hg

        @pl.when(i > 0)
        def _():
            dgsb_ref[...] += part_sb
            dghg_ref[...] += part_hg

    half = pl.BlockSpec((tm, SB_WIDTH), lambda i: (i, 0))
    vec = pl.BlockSpec((1, SB_WIDTH), lambda i: (0, 0))
    row = pl.BlockSpec((tm, D_MODEL), lambda i: (i, 0))
    return pl.pallas_call(
        body,
        name=name,
        grid=(t // tm,),
        in_specs=[row, half, half, pl.BlockSpec((tm, HG_WIDTH), lambda i: (i, 6)), vec, vec,
                  pl.BlockSpec((D_MODEL, D_MODEL), lambda i: (0, 0))],
        out_specs=[half, half, half, vec, vec, row],
        out_shape=[jax.ShapeDtypeStruct((t, SB_WIDTH), F32)] * 2 + [jax.ShapeDtypeStruct((t, SB_WIDTH), BF16)]
        + [jax.ShapeDtypeStruct((1, SB_WIDTH), F32)] * 2 + [jax.ShapeDtypeStruct((t, D_MODEL), BF16)],
        compiler_params=_params("arbitrary"),
    )(dx2, o_sb, o_hg, proj, g_sb, g_hg, w_out)


def _local_step(x, target, norms, logits, w, weights_after=None, grads_ready=None):
    w = dict(w)
    a1, b1, s1, h1 = _ffn_fwd_act(x, norms["ffn1"], w["g1t"], w["u1t"], name="ffn1_fwd_act")
    if weights_after is not None:
        w.update(weights_after("act", s1))
    x1, hm = _ffn_fwd_down(x, s1, w["d1"], norms["mix"], name="ffn1_fwd_down")
    if weights_after is not None:
        w.update(weights_after("ffn1", x1))
    proj = _mm(hm, w["int"], name="in_proj", tm=512, tn=IN_COLS, nt=True)
    o_sb, sb_kept = _attn_fwd(proj, name="sb_attn_fwd")
    o_hg, states = _hgrn_fwd(proj, logits, name="hgrn2_fwd")
    x2, mixed = _mix_out_fwd(o_sb, o_hg, proj, norms["sb"], norms["hg"], w["out"], x1, name="mix_out_fwd")
    if weights_after is not None:
        w.update(weights_after("mix", x2))
    dx3, a2, b2, h2, s2, d_final, loss_row = _ffn_fwd(x2, norms["ffn2"], w["g2t"], w["u2t"], w["d2"], name="ffn2_fwd",
                                                      head=(norms["final"], target))

    def weight_grad(lhs, rhs, name, tie=None):
        return _mm(lhs, rhs, name=name, tm=256, tn=D_MODEL, ta=True, out_dtype=BF16, tie=tie)

    def sent(stage):
        return grads_ready(stage, gw) if grads_ready is not None else None

    gw, gv = {}, {"final": d_final}
    dx2, gv["ffn2"], da2, db2, dob2 = _ffn_bwd(dx3, x2, norms["ffn2"], a2, b2, w["g2t"], w["u2t"], w["d2"],
                                               name="ffn2_bwd")
    gw["g2t"] = weight_grad(da2, h2, "ffn2_dgate")
    gw["u2t"] = weight_grad(db2, h2, "ffn2_dup")
    gw["d2"] = weight_grad(s2, dob2, "ffn2_ddown")

    do_sb, do_hg, d_gate, gv["sb"], gv["hg"], dx2b = _mix_out_bwd(
        dx2, o_sb, o_hg, proj, norms["sb"], norms["hg"], w["out"], name="mix_out_bwd")
    gw["out"] = weight_grad(mixed, dx2b, "out_dw")
    tie = sent("mix")
    dq_sb, dk_sb, dv_sb = _attn_bwd(proj, sb_kept, do_sb, name="sb_attn_bwd", tie=tie)
    dq_hg, df_hg, di_hg, d_lb = _hgrn_bwd(proj, logits if tie is None else logits + tie[0, 0], states, do_hg,
                                          name="hgrn2_bwd")
    dproj = jnp.concatenate([dq_sb, dk_sb.astype(BF16), dv_sb.astype(BF16), dq_hg, df_hg, di_hg, d_gate], axis=1)
    gw["int"] = weight_grad(dproj, hm, "in_dw")
    tie = sent("in")
    dx1, gv["mix"], dob1 = _in_proj_bwd(dproj, w["int"], x1, norms["mix"] if tie is None else norms["mix"] + tie[0, 0],
                                        dx2, name="in_dx")

    gw["d1"] = weight_grad(s1, dob1, "ffn1_ddown")
    tie = sent("d1")
    da1, db1 = _ffn_bwd_act(dob1, a1, b1, w["d1"], name="ffn1_bwd_act",
                            tie=jnp.zeros((8, LANES), F32) if tie is None else tie)
    gw["g1t"] = weight_grad(da1, h1, "ffn1_dgate")
    gw["u1t"] = weight_grad(db1, h1, "ffn1_dup", tie=sent("g1t"))
    tie = sent("u1t")
    dx, gv["ffn1"] = _ffn_bwd_in(dx1, x, norms["ffn1"] if tie is None else norms["ffn1"] + tie[0, 0], da1, db1,
                                 w["g1t"], w["u1t"], name="ffn1_bwd_in")
    gv["lb"] = d_lb
    return loss_row, dx, gw, gv


HBM = pl.BlockSpec(memory_space=pl.ANY)


def _place():
    return lax.axis_index("x"), lax.axis_index("y"), lax.axis_index("c")


def _slot(px, py, pc):
    return 4 * px + 2 * py + pc


GATHER_COPIES = 8


def _all_gather(blocks, *, name):
    n = len(blocks)

    def body(*refs):
        ins, outs = refs[:n], refs[n:2 * n]
        send_sems, recv_sems, local_sems = refs[2 * n:]
        x, y, c = _place()
        me, sibling = (x, y, c), (x, y, 1 - c)
        beside, across, diagonal = (1 - x, y, c), (x, 1 - y, c), (1 - x, 1 - y, c)

        def copy(a, k, block, to, src=None, half=None):
            dst = outs[a].at[_slot(*block)]
            if half is not None:
                rows = blocks[a].shape[0] // 2
                dst = dst.at[pl.ds(half * rows, rows)]
            return pltpu.make_async_remote_copy(
                src_ref=dst if src is None else src, dst_ref=dst, send_sem=send_sems.at[GATHER_COPIES * a + k],
                recv_sem=recv_sems.at[GATHER_COPIES * a + k], device_id=to, device_id_type=MESH)

        mine = [pltpu.make_async_copy(ins[a], outs[a].at[_slot(*me)], local_sems.at[a]) for a in range(n)]
        for cp in mine:
            cp.start()
        sent = []
        for a in range(n):
            sent += [copy(a, 0, me, sibling, src=ins[a]), copy(a, 1, me, beside, src=ins[a]),
                     copy(a, 2, me, across, src=ins[a])]
        for cp in sent:
            cp.start()
        for a in range(n):
            copy(a, 1, beside, me).wait_recv()
            sent += [copy(a, 3, beside, across, half=0), copy(a, 5, beside, sibling)]
            sent[-2].start()
            sent[-1].start()
        for a in range(n):
            copy(a, 2, across, me).wait_recv()
            sent += [copy(a, 4, across, beside, half=1), copy(a, 6, across, sibling)]
            sent[-2].start()
            sent[-1].start()
        for a in range(n):
            copy(a, 3, diagonal, me, half=0).wait_recv()
            copy(a, 4, diagonal, me, half=1).wait_recv()
            sent.append(copy(a, 7, diagonal, sibling))
            sent[-1].start()
        for a in range(n):
            for k, origin in ((0, sibling), (5, (1 - x, y, 1 - c)), (6, (x, 1 - y, 1 - c)), (7, (1 - x, 1 - y, 1 - c))):
                copy(a, k, origin, me).wait_recv()
        for cp in sent:
            cp.wait_send()
        for cp in mine:
            cp.wait()

    return pl.pallas_call(
        body,
        name=name,
        in_specs=[HBM] * n,
        out_specs=[HBM] * n,
        out_shape=[jax.ShapeDtypeStruct((N_DEV,) + b.shape, b.dtype) for b in blocks],
        scratch_shapes=[pltpu.SemaphoreType.DMA((GATHER_COPIES * n,)), pltpu.SemaphoreType.DMA((GATHER_COPIES * n,)),
                        pltpu.SemaphoreType.DMA((n,))],
    )(*blocks)


def _flipped(place, d):
    return tuple(1 - p if (d >> (2 - axis)) & 1 else p for axis, p in enumerate(place))


SEM = pl.BlockSpec(memory_space=pltpu.SEMAPHORE)
EFFECT = pltpu.SideEffectType.DATAFLOW_SIDE_EFFECTING


def _split_copies(me, srcs, lands, send_sems, recv_sems, by_owner):
    copies = []
    for d in range(1, N_DEV):
        peer = _flipped(me, d)
        for a, (src, land) in enumerate(zip(srcs, lands)):
            copies.append(pltpu.make_async_remote_copy(
                src_ref=src.at[_slot(*peer)] if by_owner else src, dst_ref=land.at[_slot(*me)],
                send_sem=send_sems.at[7 * a + d - 1], recv_sem=recv_sems.at[7 * a + d - 1], device_id=peer,
                device_id_type=MESH))
    own = [pltpu.make_async_copy(src.at[_slot(*me)] if by_owner else src, land.at[_slot(*me)],
                                 recv_sems.at[7 * len(srcs) + a]) for a, (src, land) in enumerate(zip(srcs, lands))]
    return copies, own


def _copies_start(srcs, *, name, by_owner, after=None):
    n = len(srcs)
    extra = [] if after is None else [after]
    land_shapes = [s.shape if by_owner else (N_DEV,) + s.shape for s in srcs]
    lands = [pltpu.with_memory_space_constraint(lax.empty(shape, s.dtype), pltpu.HBM) for shape, s in zip(land_shapes, srcs)]
    srcs = [pltpu.with_memory_space_constraint(s, pltpu.HBM) for s in srcs]

    def body(*refs):
        src_refs, land_refs = refs[:n], refs[n:2 * n]
        send_sems, recv_sems = refs[2 * n + len(extra)], refs[2 * n + len(extra) + 1]
        token = refs[-1]
        copies, own = _split_copies(_place(), src_refs, land_refs, send_sems, recv_sems, by_owner)
        for cp in copies + own:
            cp.start()
        token[...] = jnp.zeros_like(token)

    out = pl.pallas_call(
        body,
        name=name,
        in_specs=[HBM] * (2 * n + len(extra)),
        out_specs=[SEM, SEM] + [HBM] * (2 * n) + [pl.BlockSpec(memory_space=pltpu.VMEM)],
        out_shape=[pltpu.SemaphoreType.DMA((7 * n,)), pltpu.SemaphoreType.DMA((8 * n,))]
        + [pltpu.HBM(s.shape, s.dtype) for s in srcs] + [pltpu.HBM(shape, s.dtype) for shape, s in zip(land_shapes, srcs)]
        + [jax.ShapeDtypeStruct((8, LANES), F32)],
        input_output_aliases={i: 2 + i for i in range(2 * n)},
        compiler_params=pltpu.CompilerParams(has_side_effects=EFFECT),
    )(*srcs, *lands, *extra)
    return (out[0], out[1], out[2:2 + n], out[2 + n:2 + 2 * n]), out[-1]


def _copies_wait(started, after, *, name, by_owner):
    send_sems, recv_sems, srcs, lands = started
    n = len(srcs)

    def body(*refs):
        src_refs, land_refs = refs[:n], refs[n:2 * n]
        copies, own = _split_copies(_place(), src_refs, land_refs, refs[2 * n], refs[2 * n + 1], by_owner)
        for cp in copies:
            cp.wait_send()
            cp.wait_recv()
        for cp in own:
            cp.wait()

    out = pl.pallas_call(
        body,
        name=name,
        in_specs=[HBM] * (2 * n) + [SEM, SEM, HBM],
        out_specs=[HBM] * (2 * n),
        out_shape=[pltpu.HBM(s.shape, s.dtype) for s in srcs] + [pltpu.HBM(s.shape, s.dtype) for s in lands],
        input_output_aliases={i: i for i in range(2 * n)},
        compiler_params=pltpu.CompilerParams(has_side_effects=EFFECT),
    )(*srcs, *lands, send_sems, recv_sems, after)
    return out[:n], out[n:]


def _adamw(w, g, m, v):
    m = ADAM_B1 * m + (1.0 - ADAM_B1) * g
    v = ADAM_B2 * v + (1.0 - ADAM_B2) * (g * g)
    m_hat = m / (1.0 - ADAM_B1 ** ADAM_STEP)
    v_hat = v / (1.0 - ADAM_B2 ** ADAM_STEP)
    delta = -ADAM_LR * (m_hat / (jnp.sqrt(v_hat) + ADAM_EPS) + ADAM_WD * w)
    return delta, m, v


def _sum_and_update(parts, w, m, v, *, name, tie=None):
    _, rows, cols = w.shape
    tr = rows // 2

    def body(p_ref, w_ref, m_ref, v_ref, *rest):
        g_ref, d_ref, mo_ref, vo_ref = rest[-4:]
        g = p_ref[0].astype(F32)
        for s in range(1, N_DEV):
            g = g + p_ref[s].astype(F32)
        g_ref[0] = g
        d_ref[0], mo_ref[0], vo_ref[0] = _adamw(w_ref[0], g, m_ref[0], v_ref[0])

    flat = pl.BlockSpec((1, tr, cols), lambda i: (0, i, 0))
    return pl.pallas_call(
        body,
        name=name,
        grid=(rows // tr,),
        in_specs=[pl.BlockSpec((N_DEV, tr, cols), lambda i: (0, i, 0)), flat, flat, flat]
        + ([] if tie is None else [pl.BlockSpec(memory_space=pl.ANY)]),
        out_specs=[flat] * 4,
        out_shape=[jax.ShapeDtypeStruct((1, rows, cols), F32)] * 4,
        compiler_params=_params("parallel"),
    )(parts, w, m, v, *([] if tie is None else [tie]))


VEC_ROWS = 8
ROW_LOGITS, ROW_LOSS = 5, 7


def _vectors_update(part, w, m, v, *, name, tie):
    def body(p_ref, w_ref, m_ref, v_ref, tie_ref, g_ref, d_ref, mo_ref, vo_ref, loss_ref, all_ref, send_sems, recv_sems):
        me = _place()
        all_ref[_slot(*me)] = p_ref[...]
        copies = []
        for d in range(1, N_DEV):
            peer = _flipped(me, d)
            copies.append(pltpu.make_async_remote_copy(
                src_ref=p_ref, dst_ref=all_ref.at[_slot(*me)], send_sem=send_sems.at[d - 1], recv_sem=recv_sems.at[d - 1],
                device_id=peer, device_id_type=MESH))
        for cp in copies:
            cp.start()
        for cp in copies:
            cp.wait()
        total = all_ref[0]
        for s in range(1, N_DEV):
            total = total + all_ref[s]
        wv = w_ref[...]
        half = D_MODEL // 2
        lb = _sigmoid(wv[ROW_LOGITS:ROW_LOGITS + 1, :half] - wv[ROW_LOGITS:ROW_LOGITS + 1, half:])
        d_first = total[ROW_LOGITS:ROW_LOGITS + 1, :half] * lb * (1.0 - lb)
        d_logits = jnp.concatenate([d_first, -d_first], axis=1)
        rowi = lax.broadcasted_iota(jnp.int32, (VEC_ROWS, D_MODEL), 0)
        g = jnp.where(rowi == ROW_LOGITS, d_logits, jnp.where(rowi < ROW_LOGITS, total, 0.0))
        g_ref[...] = g
        d_ref[...], mo_ref[...], vo_ref[...] = _adamw(wv, g, m_ref[...], v_ref[...])
        loss_ref[...] = total[ROW_LOSS:ROW_LOSS + 1, :]

    vmem = pl.BlockSpec(memory_space=pltpu.VMEM)
    return pl.pallas_call(
        body,
        name=name,
        in_specs=[vmem] * 4 + [HBM],
        out_specs=[vmem] * 5,
        out_shape=[jax.ShapeDtypeStruct((VEC_ROWS, D_MODEL), F32)] * 4 + [jax.ShapeDtypeStruct((1, D_MODEL), F32)],
        scratch_shapes=[pltpu.VMEM((N_DEV, VEC_ROWS, D_MODEL), F32), pltpu.SemaphoreType.DMA((7,)),
                        pltpu.SemaphoreType.DMA((7,))],
    )(part, w, m, v, tie)


TRANSPOSED = ("g1t", "u1t", "g2t", "u2t", "int")


def _vector_rows(rows):
    rowi = lax.broadcasted_iota(jnp.int32, (VEC_ROWS, D_MODEL), 0)
    out = jnp.zeros((VEC_ROWS, D_MODEL), F32)
    for i, r in enumerate(rows):
        if r is not None:
            out = jnp.where(rowi == i, r, out)
    return out


def kernel(x, ffn1_norm, ffn1_w_gate, ffn1_w_up, ffn1_w_down, mix_norm, w_in, sb_out_norm, hg_lower_bound_logits, hg_out_norm, w_out, ffn2_norm, ffn2_w_gate, ffn2_w_up, ffn2_w_down, final_norm, loss_target, m_ffn1_norm, m_ffn1_w_gate, m_ffn1_w_up, m_ffn1_w_down, m_mix_norm, m_w_in, m_sb_out_norm, m_hg_lower_bound_logits, m_hg_out_norm, m_w_out, m_ffn2_norm, m_ffn2_w_gate, m_ffn2_w_up, m_ffn2_w_down, m_final_norm, v_ffn1_norm, v_ffn1_w_gate, v_ffn1_w_up, v_ffn1_w_down, v_mix_norm, v_w_in, v_sb_out_norm, v_hg_lower_bound_logits, v_hg_out_norm, v_w_out, v_ffn2_norm, v_ffn2_w_gate, v_ffn2_w_up, v_ffn2_w_down, v_final_norm):
    def matrices(g1, u1, d1, win, wout, g2, u2, d2):
        return {"g1t": g1, "u1t": u1, "d1": d1, "int": win, "out": wout, "g2t": g2, "u2t": u2, "d2": d2}

    def vectors(n1, nm, nsb, lg, nhg, n2, nf):
        return [n1, nm, n2, nf.reshape(1, D_MODEL), jnp.concatenate([nsb, nhg], axis=1), lg.reshape(1, D_MODEL), None, None]

    w_sh = matrices(ffn1_w_gate, ffn1_w_up, ffn1_w_down, w_in, w_out, ffn2_w_gate, ffn2_w_up, ffn2_w_down)
    m_sh = matrices(m_ffn1_w_gate, m_ffn1_w_up, m_ffn1_w_down, m_w_in, m_w_out, m_ffn2_w_gate, m_ffn2_w_up, m_ffn2_w_down)
    v_sh = matrices(v_ffn1_w_gate, v_ffn1_w_up, v_ffn1_w_down, v_w_in, v_w_out, v_ffn2_w_gate, v_ffn2_w_up, v_ffn2_w_down)
    keys = list(w_sh)

    def full(key, stack):
        return stack.reshape(-1, D_MODEL)

    def by_owner(key, grad):
        return grad.reshape(N_DEV, -1, D_MODEL)

    def view(key, a):
        return jnp.swapaxes(a, 1, 2) if key in TRANSPOSED else a

    blocks = {k: view(k, w_sh[k])[0].astype(BF16) for k in keys}
    later = {"act": ("d1",), "ffn1": ("int", "out"), "mix": ("g2t", "u2t", "d2")}
    first = ("g1t", "u1t")
    w_first = {k: full(k, s) for k, s in zip(first, _all_gather([blocks[k] for k in first], name="gather_ffn1"))}
    flights, token = {}, w_first["u1t"]
    for stage, group in later.items():
        flights[stage], token = _copies_start([blocks[k] for k in group], name="gather_" + stage + "_start",
                                              by_owner=False, after=token)
    token_last = token

    def weights_after(stage, result):
        _, lands = _copies_wait(flights[stage], result, name="gather_" + stage + "_wait", by_owner=False)
        return {k: full(k, s) for k, s in zip(later[stage], lands)}

    groups = {"mix": ("g2t", "u2t", "d2", "out"), "in": ("int",), "g1t": ("g1t",), "u1t": ("u1t",), "d1": ("d1",)}
    sent = {}

    def grads_ready(stage, gw):
        stacks = [by_owner(k, gw[k]) for k in groups[stage]]
        sent[stage], token = _copies_start(stacks, name="grads_" + stage + "_start", by_owner=True)
        return token

    norms = {"ffn1": ffn1_norm + token_last[0, 0], "mix": mix_norm, "sb": sb_out_norm, "hg": hg_out_norm,
             "ffn2": ffn2_norm, "final": final_norm.reshape(1, D_MODEL)}
    loss_row, grad_x, gw, gv = _local_step(x[0], loss_target[0], norms, hg_lower_bound_logits, w_first, weights_after,
                                           grads_ready)

    lb_row = jnp.concatenate([gv["lb"], jnp.zeros_like(gv["lb"])], axis=1)
    part = _vector_rows([gv["ffn1"], gv["mix"], gv["ffn2"], gv["final"], jnp.concatenate([gv["sb"], gv["hg"]], axis=1),
                         lb_row, None, loss_row])
    vec_w = _vector_rows(vectors(ffn1_norm, mix_norm, sb_out_norm, hg_lower_bound_logits, hg_out_norm, ffn2_norm, final_norm))
    vec_m = _vector_rows(vectors(m_ffn1_norm, m_mix_norm, m_sb_out_norm, m_hg_lower_bound_logits, m_hg_out_norm,
                                 m_ffn2_norm, m_final_norm))
    vec_v = _vector_rows(vectors(v_ffn1_norm, v_mix_norm, v_sb_out_norm, v_hg_lower_bound_logits, v_hg_out_norm,
                                 v_ffn2_norm, v_final_norm))
    updated, after = {}, grad_x
    for stage, flight in sent.items():
        if stage == list(sent)[-1]:
            *vecs, loss_out = _vectors_update(part, vec_w, vec_m, vec_v, name="vectors_update", tie=after)
            after = loss_out
        _, lands = _copies_wait(flight, after, name="grads_" + stage + "_wait", by_owner=True)
        for k, part_k in zip(groups[stage], lands):
            updated[k] = _sum_and_update(part_k, view(k, w_sh[k]), view(k, m_sh[k]), view(k, v_sh[k]), name="adamw_" + k,
                                         tie=after)
            after = updated[k][0]
    mats = [{k: view(k, updated[k][i]) for k in keys} for i in range(4)]

    def leaves(mat, vec):
        half = D_MODEL // 2
        return (
            vec[0:1], mat["g1t"], mat["u1t"], mat["d1"], vec[1:2], mat["int"], vec[4:5, :half],
            vec[ROW_LOGITS].reshape(2, half), vec[4:5, half:], mat["out"], vec[2:3], mat["g2t"], mat["u2t"],
            mat["d2"], vec[3],
        )

    out = [loss_out[0, 0], grad_x[None]]
    for mat, vec in zip(mats, vecs):
        out.extend(leaves(mat, vec))
    return tuple(out)
```

```python
import jax
import jax.numpy as jnp
from jax import lax
from jax.experimental import pallas as pl
from jax.experimental.pallas import tpu as pltpu

F32, BF16 = jnp.float32, jnp.bfloat16
D_MODEL = 1024
D_FF = 2816
SB_WIDTH = 512
HG_WIDTH = 512
SB_HEAD_DIM = 64
HG_HEAD_DIM = 128
IN_COLS = 3584
EPS = 1e-6
N_DEV = 8
LANES = 128
HG_CHUNK = 16
VMEM_LIMIT_BYTES = 48 * 1024 * 1024
FFN_BWD_VMEM_LIMIT_BYTES = 56 * 1024 * 1024
ADAM_LR, ADAM_B1, ADAM_B2, ADAM_EPS, ADAM_WD, ADAM_STEP = 0.001, 0.9, 0.999, 1e-08, 0.01, 10
MESH = pl.DeviceIdType.MESH


def _params(*semantics, vmem_limit_bytes=VMEM_LIMIT_BYTES):
    return pltpu.CompilerParams(dimension_semantics=semantics, vmem_limit_bytes=vmem_limit_bytes)


def _dot(a, b):
    return jnp.dot(a, b, preferred_element_type=F32)


def _dot_nt(a, b):
    return lax.dot_general(a, b, (((1,), (1,)), ((), ())), preferred_element_type=F32)


def _dot_tn(a, b):
    return lax.dot_general(a, b, (((0,), (0,)), ((), ())), preferred_element_type=F32)


def _split3(x):
    hi = x.astype(BF16)
    r1 = x - hi.astype(F32)
    mid = r1.astype(BF16)
    lo = (r1 - mid.astype(F32)).astype(BF16)
    return hi, mid, lo


def _rms(xv):
    rstd = lax.rsqrt(jnp.mean(xv * xv, axis=-1, keepdims=True) + EPS)
    return xv * rstd, rstd


def _sigmoid(x):
    return 0.5 + 0.5 * jnp.tanh(0.5 * x)


def _loss_terms(xv, gain, target):
    xhat, rstd = _rms(xv)
    err = xhat * gain - target
    loss = 0.5 * jnp.sum(jnp.mean(err * err, axis=-1, keepdims=True), axis=0, keepdims=True)
    dy = err * (1.0 / xv.shape[-1])
    dxh = dy * gain
    dx = rstd * (dxh - xhat * jnp.mean(dxh * xhat, axis=-1, keepdims=True))
    return dx, jnp.sum(dy * xhat, axis=0, keepdims=True), loss


def _mm(a, b, *, name, tm, tn, nt=False, ta=False, out_dtype=F32, tie=None):
    k, m = a.shape if ta else a.shape[::-1]
    n = b.shape[0] if nt else b.shape[1]
    assert m % tm == 0 and n % tn == 0 and not (nt and ta), (name, a.shape, b.shape, tm, tn)

    def body(a_ref, b_ref, *rest):
        av = a_ref[...].astype(BF16)
        bv = b_ref[...].astype(BF16)
        rest[-1][...] = (_dot_nt(av, bv) if nt else _dot_tn(av, bv) if ta else _dot(av, bv)).astype(out_dtype)

    in_specs = [
        pl.BlockSpec((k, tm), lambda i, j: (0, i)) if ta else pl.BlockSpec((tm, k), lambda i, j: (i, 0)),
        pl.BlockSpec((tn, k), lambda i, j: (j, 0)) if nt else pl.BlockSpec((k, tn), lambda i, j: (0, j)),
    ]
    operands = [a, b]
    if tie is not None:
        in_specs.append(pl.BlockSpec(memory_space=pl.ANY))
        operands.append(tie)
    return pl.pallas_call(
        body,
        name=name,
        grid=(m // tm, n // tn),
        in_specs=in_specs,
        out_specs=pl.BlockSpec((tm, tn), lambda i, j: (i, j)),
        out_shape=jax.ShapeDtypeStruct((m, n), out_dtype),
        compiler_params=_params("parallel", "parallel"),
    )(*operands)


def _ffn_fwd(x, gain, wgt, wut, wd, *, name, next_gain=None, head=None, tm=1024, tf=256):
    t = x.shape[0]
    nj = D_FF // tf
    extra_in = [] if next_gain is None else [next_gain]
    extra_in += [] if head is None else list(head)

    def body(x_ref, g_ref, wg_ref, wu_ref, wd_prev_ref, wd_last_ref, *rest):
        extra, (xo_ref, a_ref, b_ref, h_ref, st_ref) = rest[:len(extra_in)], rest[len(extra_in):len(extra_in) + 5]
        tail_out, (acc, s_prev) = rest[len(extra_in) + 5:-2], rest[-2:]
        i = pl.program_id(0)
        j = pl.program_id(1)

        @pl.when(j == 0)
        def _():
            xhat, _ = _rms(x_ref[...])
            h_ref[...] = (xhat * g_ref[...]).astype(BF16)
            acc[...] = jnp.zeros_like(acc)
            s_prev[...] = jnp.zeros_like(s_prev)

        acc[...] += _dot(s_prev[...], wd_prev_ref[...])
        h = h_ref[...]
        a = _dot_nt(h, wg_ref[...])
        b = _dot_nt(h, wu_ref[...])
        a_ref[...] = a.astype(BF16)
        b_ref[...] = b.astype(BF16)
        s = (a * _sigmoid(a) * b).astype(BF16)
        st_ref[...] = s
        s_prev[...] = s

        @pl.when(j == nj - 1)
        def _():
            xo = x_ref[...] + 0.5 * (acc[...] + _dot(s, wd_last_ref[...]))
            if head is None:
                xo_ref[...] = xo
            if next_gain is not None:
                tail_out[0][...] = (_rms(xo)[0] * extra[0][...]).astype(BF16)
            if head is not None:
                gain_ref, target_ref = extra[-2:]
                dg_ref, loss_ref = tail_out[-2:]
                xo_ref[...], part_g, part_loss = _loss_terms(xo, gain_ref[...], target_ref[...])

                @pl.when(i == 0)
                def _():
                    dg_ref[...] = part_g
                    loss_ref[...] = jnp.broadcast_to(part_loss, loss_ref.shape)

                @pl.when(i > 0)
                def _():
                    dg_ref[...] += part_g
                    loss_ref[...] += jnp.broadcast_to(part_loss, loss_ref.shape)

    row = pl.BlockSpec((tm, D_MODEL), lambda i, j: (i, 0))
    vec = pl.BlockSpec((1, D_MODEL), lambda i, j: (0, 0))
    tile = pl.BlockSpec((tm, tf), lambda i, j: (i, j))
    weights = pl.BlockSpec((tf, D_MODEL), lambda i, j: (j, 0))
    tail_specs = ([] if next_gain is None else [row]) + ([] if head is None else [vec, vec])
    tail_shapes = ([] if next_gain is None else [jax.ShapeDtypeStruct((t, D_MODEL), BF16)]) + (
        [] if head is None else [jax.ShapeDtypeStruct((1, D_MODEL), F32)] * 2)
    return pl.pallas_call(
        body,
        name=name,
        grid=(t // tm, nj),
        in_specs=[
            row, vec, weights, weights,
            pl.BlockSpec((tf, D_MODEL), lambda i, j: (jnp.maximum(j - 1, 0), 0)),
            pl.BlockSpec((tf, D_MODEL), lambda i, j: (nj - 1, 0)),
        ] + ([] if next_gain is None else [vec]) + ([] if head is None else [vec, row]),
        out_specs=[row, tile, tile, row, tile] + tail_specs,
        out_shape=[
            jax.ShapeDtypeStruct((t, D_MODEL), F32),
            jax.ShapeDtypeStruct((t, D_FF), BF16),
            jax.ShapeDtypeStruct((t, D_FF), BF16),
            jax.ShapeDtypeStruct((t, D_MODEL), BF16),
            jax.ShapeDtypeStruct((t, D_FF), BF16),
        ] + tail_shapes,
        scratch_shapes=[pltpu.VMEM((tm, D_MODEL), F32), pltpu.VMEM((tm, tf), BF16)],
        compiler_params=_params("arbitrary", "arbitrary"),
    )(x, gain, wgt, wut, wd, wd, *extra_in)


def _ffn_fwd_act(x, gain, wgt, wut, *, name, tm=512, tf=D_FF // 2):
    t = x.shape[0]

    def body(x_ref, g_ref, wg_ref, wu_ref, a_ref, b_ref, st_ref, h_ref):
        @pl.when(pl.program_id(1) == 0)
        def _():
            xhat, _ = _rms(x_ref[...])
            h_ref[...] = (xhat * g_ref[...]).astype(BF16)

        h = h_ref[...]
        a = _dot_nt(h, wg_ref[...])
        b = _dot_nt(h, wu_ref[...])
        a_ref[...] = a.astype(BF16)
        b_ref[...] = b.astype(BF16)
        st_ref[...] = (a * _sigmoid(a) * b).astype(BF16)

    row = pl.BlockSpec((tm, D_MODEL), lambda i, j: (i, 0))
    tile = pl.BlockSpec((tm, tf), lambda i, j: (i, j))
    weights = pl.BlockSpec((tf, D_MODEL), lambda i, j: (j, 0))
    return pl.pallas_call(
        body,
        name=name,
        grid=(t // tm, D_FF // tf),
        in_specs=[row, pl.BlockSpec((1, D_MODEL), lambda i, j: (0, 0)), weights, weights],
        out_specs=[tile, tile, tile, row],
        out_shape=[jax.ShapeDtypeStruct((t, D_FF), BF16)] * 3 + [jax.ShapeDtypeStruct((t, D_MODEL), BF16)],
        compiler_params=_params("arbitrary", "arbitrary"),
    )(x, gain, wgt, wut)


def _ffn_fwd_down(x, s, wd, next_gain, *, name, tm=512):
    t = x.shape[0]

    def body(x_ref, s_ref, wd_ref, g_ref, xo_ref, hn_ref):
        xo = x_ref[...] + 0.5 * _dot(s_ref[...], wd_ref[...])
        xo_ref[...] = xo
        hn_ref[...] = (_rms(xo)[0] * g_ref[...]).astype(BF16)

    row = pl.BlockSpec((tm, D_MODEL), lambda i: (i, 0))
    return pl.pallas_call(
        body,
        name=name,
        grid=(t // tm,),
        in_specs=[row, pl.BlockSpec((tm, D_FF), lambda i: (i, 0)),
                  pl.BlockSpec((D_FF, D_MODEL), lambda i: (0, 0), pipeline_mode=pl.Buffered(1)),
                  pl.BlockSpec((1, D_MODEL), lambda i: (0, 0))],
        out_specs=[row, row],
        out_shape=[jax.ShapeDtypeStruct((t, D_MODEL), F32), jax.ShapeDtypeStruct((t, D_MODEL), BF16)],
        compiler_params=_params("parallel"),
    )(x, s, wd, next_gain)


def _ffn_bwd(dout, x, gain, a, b, wgt, wut, wd, *, name, tm=1024, tf=256):
    t = x.shape[0]
    nj = D_FF // tf

    def body(do_ref, x_ref, g_ref, a_ref, b_ref, wg_prev_ref, wu_prev_ref, wg_last_ref, wu_last_ref, wd_ref,
             dx_ref, dg_ref, da_ref, db_ref, dob_ref, dob_scr, dh, da_prev, db_prev):
        i = pl.program_id(0)
        j = pl.program_id(1)

        @pl.when(j == 0)
        def _():
            d = (0.5 * do_ref[...]).astype(BF16)
            dob_scr[...] = d
            dob_ref[...] = d
            dh[...] = jnp.zeros_like(dh)
            da_prev[...] = jnp.zeros_like(da_prev)
            db_prev[...] = jnp.zeros_like(db_prev)

        dh[...] += _dot(da_prev[...], wg_prev_ref[...]) + _dot(db_prev[...], wu_prev_ref[...])
        ds = _dot_nt(dob_scr[...], wd_ref[...])
        av = a_ref[...].astype(F32)
        bv = b_ref[...].astype(F32)
        sig = _sigmoid(av)
        dbv = (ds * (av * sig)).astype(BF16)
        dav = (ds * bv * (sig * (1.0 + av * (1.0 - sig)))).astype(BF16)
        da_ref[...] = dav
        db_ref[...] = dbv
        da_prev[...] = dav
        db_prev[...] = dbv

        @pl.when(j == nj - 1)
        def _():
            xhat, rstd = _rms(x_ref[...])
            dhv = dh[...] + _dot(dav, wg_last_ref[...]) + _dot(dbv, wu_last_ref[...])
            part = jnp.sum(dhv * xhat, axis=0, keepdims=True)

            @pl.when(i == 0)
            def _():
                dg_ref[...] = part

            @pl.when(i > 0)
            def _():
                dg_ref[...] += part

            dxh = dhv * g_ref[...]
            dx_ref[...] = do_ref[...] + rstd * (dxh - xhat * jnp.mean(dxh * xhat, axis=-1, keepdims=True))

    return pl.pallas_call(
        body,
        name=name,
        grid=(t // tm, nj),
        in_specs=[
            pl.BlockSpec((tm, D_MODEL), lambda i, j: (i, 0)),
            pl.BlockSpec((tm, D_MODEL), lambda i, j: (i, 0)),
            pl.BlockSpec((1, D_MODEL), lambda i, j: (0, 0)),
            pl.BlockSpec((tm, tf), lambda i, j: (i, j)),
            pl.BlockSpec((tm, tf), lambda i, j: (i, j)),
            pl.BlockSpec((tf, D_MODEL), lambda i, j: (jnp.maximum(j - 1, 0), 0)),
            pl.BlockSpec((tf, D_MODEL), lambda i, j: (jnp.maximum(j - 1, 0), 0)),
            pl.BlockSpec((tf, D_MODEL), lambda i, j: (nj - 1, 0)),
            pl.BlockSpec((tf, D_MODEL), lambda i, j: (nj - 1, 0)),
            pl.BlockSpec((tf, D_MODEL), lambda i, j: (j, 0)),
        ],
        out_specs=[
            pl.BlockSpec((tm, D_MODEL), lambda i, j: (i, 0)),
            pl.BlockSpec((1, D_MODEL), lambda i, j: (0, 0)),
            pl.BlockSpec((tm, tf), lambda i, j: (i, j)),
            pl.BlockSpec((tm, tf), lambda i, j: (i, j)),
            pl.BlockSpec((tm, D_MODEL), lambda i, j: (i, 0)),
        ],
        out_shape=[
            jax.ShapeDtypeStruct((t, D_MODEL), F32),
            jax.ShapeDtypeStruct((1, D_MODEL), F32),
            jax.ShapeDtypeStruct((t, D_FF), BF16),
            jax.ShapeDtypeStruct((t, D_FF), BF16),
            jax.ShapeDtypeStruct((t, D_MODEL), BF16),
        ],
        scratch_shapes=[pltpu.VMEM((tm, D_MODEL), BF16), pltpu.VMEM((tm, D_MODEL), F32), pltpu.VMEM((tm, tf), BF16),
                        pltpu.VMEM((tm, tf), BF16)],
        compiler_params=_params("arbitrary", "arbitrary", vmem_limit_bytes=FFN_BWD_VMEM_LIMIT_BYTES),
    )(dout, x, gain, a, b, wgt, wut, wgt, wut, wd)


def _ffn_bwd_act(dout_half, a, b, wd, *, name, tie, tm=512, tf=D_FF // 2):
    t = a.shape[0]

    def body(dob_ref, a_ref, b_ref, wd_ref, tie_ref, da_ref, db_ref):
        ds = _dot_nt(dob_ref[...], wd_ref[...])
        av = a_ref[...].astype(F32)
        bv = b_ref[...].astype(F32)
        sig = _sigmoid(av)
        db_ref[...] = (ds * (av * sig)).astype(BF16)
        da_ref[...] = (ds * bv * (sig * (1.0 + av * (1.0 - sig)))).astype(BF16)

    tile = pl.BlockSpec((tm, tf), lambda i, j: (i, j))
    return pl.pallas_call(
        body,
        name=name,
        grid=(t // tm, D_FF // tf),
        in_specs=[pl.BlockSpec((tm, D_MODEL), lambda i, j: (i, 0)), tile, tile,
                  pl.BlockSpec((tf, D_MODEL), lambda i, j: (j, 0)), pl.BlockSpec(memory_space=pl.ANY)],
        out_specs=[tile, tile],
        out_shape=[jax.ShapeDtypeStruct((t, D_FF), BF16)] * 2,
        compiler_params=_params("parallel", "parallel"),
    )(dout_half, a, b, wd, tie)


def _ffn_bwd_in(dout, x, gain, da, db, wgt, wut, *, name, tm=512):
    t = x.shape[0]

    def body(do_ref, x_ref, g_ref, da_ref, db_ref, wg_ref, wu_ref, dx_ref, dg_ref):
        i = pl.program_id(0)
        dhv = _dot(da_ref[...], wg_ref[...]) + _dot(db_ref[...], wu_ref[...])
        xhat, rstd = _rms(x_ref[...])
        part = jnp.sum(dhv * xhat, axis=0, keepdims=True)

        @pl.when(i == 0)
        def _():
            dg_ref[...] = part

        @pl.when(i > 0)
        def _():
            dg_ref[...] += part

        dxh = dhv * g_ref[...]
        dx_ref[...] = do_ref[...] + rstd * (dxh - xhat * jnp.mean(dxh * xhat, axis=-1, keepdims=True))

    row = pl.BlockSpec((tm, D_MODEL), lambda i: (i, 0))
    vec = pl.BlockSpec((1, D_MODEL), lambda i: (0, 0))
    tile = pl.BlockSpec((tm, D_FF), lambda i: (i, 0))
    weights = pl.BlockSpec((D_FF, D_MODEL), lambda i: (0, 0), pipeline_mode=pl.Buffered(1))
    return pl.pallas_call(
        body,
        name=name,
        grid=(t // tm,),
        in_specs=[row, row, vec, tile, tile, weights, weights],
        out_specs=[row, vec],
        out_shape=[jax.ShapeDtypeStruct((t, D_MODEL), F32), jax.ShapeDtypeStruct((1, D_MODEL), F32)],
        compiler_params=_params("arbitrary", vmem_limit_bytes=FFN_BWD_VMEM_LIMIT_BYTES),
    )(dout, x, gain, da, db, wgt, wut)


def _in_proj_bwd(dproj, w_int, x, gain, dres, *, name, tm=512):
    t, k = dproj.shape

    def body(dp_ref, w_ref, x_ref, g_ref, dr_ref, dx_ref, dg_ref, dxb_ref):
        i = pl.program_id(0)
        dhv = _dot(dp_ref[...], w_ref[...])
        xhat, rstd = _rms(x_ref[...])
        part = jnp.sum(dhv * xhat, axis=0, keepdims=True)

        @pl.when(i == 0)
        def _():
            dg_ref[...] = part

        @pl.when(i > 0)
        def _():
            dg_ref[...] += part

        dxh = dhv * g_ref[...]
        dx = dr_ref[...] + rstd * (dxh - xhat * jnp.mean(dxh * xhat, axis=-1, keepdims=True))
        dx_ref[...] = dx
        dxb_ref[...] = (0.5 * dx).astype(BF16)

    row = pl.BlockSpec((tm, D_MODEL), lambda i: (i, 0))
    vec = pl.BlockSpec((1, D_MODEL), lambda i: (0, 0))
    return pl.pallas_call(
        body,
        name=name,
        grid=(t // tm,),
        in_specs=[pl.BlockSpec((tm, k), lambda i: (i, 0)), pl.BlockSpec((k, D_MODEL), lambda i: (0, 0)), row, vec, row],
        out_specs=[row, vec, row],
        out_shape=[jax.ShapeDtypeStruct((t, D_MODEL), F32), jax.ShapeDtypeStruct((1, D_MODEL), F32),
                   jax.ShapeDtypeStruct((t, D_MODEL), BF16)],
        compiler_params=_params("arbitrary"),
    )(dproj, w_int, x, gain, dres)


ATT_Q_TILE = 512
ATT_K_BLOCK = 256


def _first_head_lanes():
    return lax.broadcasted_iota(jnp.int32, (1, LANES), 1) < SB_HEAD_DIM


def _stack_heads(x):
    first = _first_head_lanes()
    return jnp.concatenate([jnp.where(first, x, 0.0), jnp.where(first, 0.0, x)], axis=0)


def _unstack_heads(x, rows):
    return jnp.where(_first_head_lanes(), x[:rows], x[rows:])


def _rows_from(x, first, rows):
    return x if first == 0 else jnp.concatenate([x[first:rows], x[rows + first:]], axis=0)


def _rows_into(full, part, first, rows):
    if first == 0:
        return part
    n = rows - first
    return jnp.concatenate([full[:first], part[:n], full[rows:rows + first], part[n:]], axis=0)


def _tri(n, relation):
    r = lax.broadcasted_iota(jnp.int32, (n, n), 0)
    c = lax.broadcasted_iota(jnp.int32, (n, n), 1)
    return relation(r, c).astype(BF16)


def _scan_dot(x, tri):
    hi = x.astype(BF16)
    lo = (x - hi.astype(F32)).astype(BF16)
    return _dot(jnp.concatenate([hi, lo], axis=1), jnp.concatenate([tri, tri], axis=0))


def _log_terms(z):
    lbeta = jnp.minimum(z, 0.0) - jnp.log(1.0 + jnp.exp(-jnp.abs(z)))
    return lbeta, lbeta - z


def _attn_fwd(proj, *, name):
    t = proj.shape[0]
    tq, tk = ATT_Q_TILE, ATT_K_BLOCK
    diag = tq // tk
    n_pairs = SB_WIDTH // LANES

    def body(q_ref, k_ref, v_ref, o_ref, kept_ref):
        qi = pl.program_id(1)
        q = q_ref[...] * (SB_HEAD_DIM ** -0.5)
        qs = _stack_heads(q).astype(BF16)
        tri = _tri(tk, lambda j, s: j > s)
        trow = lax.broadcasted_iota(jnp.int32, (tq, tk), 0)
        scol = lax.broadcasted_iota(jnp.int32, (tq, tk), 1)

        def block(kb, carry, causal, first=0):
            acc, c = carry
            off = pl.multiple_of(kb * tk, tk)
            lbeta, lrest = _log_terms(_dot_nt(_rows_from(qs, first, tq), k_ref[pl.ds(off, tk), :].astype(BF16)))
            if causal is not None:
                lrest = jnp.where(causal, lrest, 0.0)
            w = jnp.exp(lbeta + (_scan_dot(lrest, tri) + _rows_from(c, first, tq)))
            if causal is not None:
                w = jnp.where(causal, w, 0.0)
            wb = w.astype(BF16)
            kept_ref[0, 0, kb] = _rows_into(jnp.zeros((2 * tq, tk), BF16), wb, first, tq)
            acc = _rows_into(acc, _rows_from(acc, first, tq) + _dot(wb, v_ref[pl.ds(off, tk), :].astype(BF16)), first, tq)
            return acc, _rows_into(c, _rows_from(c, first, tq) + jnp.sum(lrest, axis=1, keepdims=True), first, tq)

        carry = (jnp.zeros((2 * tq, LANES), F32), jnp.zeros((2 * tq, 1), F32))
        n_full = qi * diag
        for j in reversed(range(diag)):
            mask = ((scol + j * tk) < trow)[j * tk:]
            carry = block(n_full + j, carry, jnp.concatenate([mask, mask], axis=0), first=j * tk)

        def odd_tile(carry):
            for j in range(diag):
                carry = block(n_full - 1 - j, carry, None)
            return carry

        carry = lax.cond(qi % 2 == 1, odd_tile, lambda c: c, carry)
        last = n_full - 1 - (qi % 2) * diag

        def step(it, carry):
            for j in range(2 * diag):
                carry = block(last - (2 * diag * it + j), carry, None)
            return carry

        acc, _ = lax.fori_loop(0, qi // 2, step, carry)
        o_ref[...] = _unstack_heads(acc, tq)

    return pl.pallas_call(
        body,
        name=name,
        grid=(n_pairs, t // tq),
        in_specs=[
            pl.BlockSpec((tq, LANES), lambda p, i: (i, p)),
            pl.BlockSpec((t, LANES), lambda p, i: (0, n_pairs + p)),
            pl.BlockSpec((t, LANES), lambda p, i: (0, 2 * n_pairs + p)),
        ],
        out_specs=[pl.BlockSpec((tq, LANES), lambda p, i: (i, p)),
                   pl.BlockSpec((1, 1, t // tk, 2 * tq, tk), lambda p, i: (p, i, 0, 0, 0))],
        out_shape=[jax.ShapeDtypeStruct((t, SB_WIDTH), F32),
                   jax.ShapeDtypeStruct((n_pairs, t // tq, t // tk, 2 * tq, tk), BF16)],
        compiler_params=_params("parallel", "parallel"),
    )(proj, proj, proj)


def _attn_bwd(proj, kept, do, *, name, tie=None):
    t = proj.shape[0]
    tq, tk = ATT_Q_TILE, ATT_K_BLOCK
    diag = tq // tk
    n_pairs = SB_WIDTH // LANES
    scale = SB_HEAD_DIM ** -0.5

    def body(q_ref, k_ref, v_ref, kept_ref, do_ref, *rest):
        dq_ref, dk_ref, dv_ref = rest[-3:]
        qi = pl.program_id(1)

        @pl.when(qi == 0)
        def _():
            dk_ref[...] = jnp.zeros_like(dk_ref)
            dv_ref[...] = jnp.zeros_like(dv_ref)

        qs = _stack_heads(q_ref[...] * scale).astype(BF16)
        dos = _stack_heads(do_ref[...]).astype(BF16)
        before = _tri(tk, lambda s, j: s < j)
        trow = lax.broadcasted_iota(jnp.int32, (tq, tk), 0)
        scol = lax.broadcasted_iota(jnp.int32, (tq, tk), 1)

        def block(kb, carry, causal, first=0):
            dq, cg = carry
            off = pl.multiple_of(kb * tk, tk)
            q_rows, do_rows = _rows_from(qs, first, tq), _rows_from(dos, first, tq)
            wb = _rows_from(kept_ref[0, 0, kb], first, tq)
            kblk = k_ref[pl.ds(off, tk), :].astype(BF16)
            sig = _sigmoid(_dot_nt(q_rows, kblk))
            g = wb.astype(F32) * _dot_nt(do_rows, v_ref[pl.ds(off, tk), :].astype(BF16))
            prior = _scan_dot(g, before) + _rows_from(cg, first, tq)
            dz = g - sig * (g + prior)
            if causal is not None:
                dz = jnp.where(causal, dz, 0.0)
            dzb = dz.astype(BF16)
            dq = _rows_into(dq, _rows_from(dq, first, tq) + _dot(dzb, kblk), first, tq)
            dk_ref[pl.ds(off, tk), :] += _dot_tn(dzb, q_rows)
            dv_ref[pl.ds(off, tk), :] += _dot_tn(wb, do_rows)
            return dq, _rows_into(cg, _rows_from(cg, first, tq) + jnp.sum(g, axis=1, keepdims=True), first, tq)

        n_full = qi * diag

        def step(it, carry):
            for j in range(2 * diag):
                carry = block(2 * diag * it + j, carry, None)
            return carry

        def odd_tile(carry):
            for j in range(diag):
                carry = block(n_full - diag + j, carry, None)
            return carry

        carry = lax.fori_loop(0, qi // 2, step, (jnp.zeros((2 * tq, LANES), F32), jnp.zeros((2 * tq, 1), F32)))
        carry = lax.cond(qi % 2 == 1, odd_tile, lambda c: c, carry)
        for j in range(diag):
            mask = ((scol + j * tk) < trow)[j * tk:]
            carry = block(n_full + j, carry, jnp.concatenate([mask, mask], axis=0), first=j * tk)
        dq_ref[...] = (_unstack_heads(carry[0], tq) * scale).astype(BF16)

    tile_spec = pl.BlockSpec((tq, LANES), lambda p, i: (i, p))
    full_spec = pl.BlockSpec((t, LANES), lambda p, i: (0, p))
    return pl.pallas_call(
        body,
        name=name,
        grid=(n_pairs, t // tq),
        in_specs=[
            tile_spec,
            pl.BlockSpec((t, LANES), lambda p, i: (0, n_pairs + p)),
            pl.BlockSpec((t, LANES), lambda p, i: (0, 2 * n_pairs + p)),
            pl.BlockSpec((1, 1, t // tk, 2 * tq, tk), lambda p, i: (p, i, 0, 0, 0)),
            tile_spec,
        ] + ([] if tie is None else [pl.BlockSpec(memory_space=pl.ANY)]),
        out_specs=[tile_spec, full_spec, full_spec],
        out_shape=[jax.ShapeDtypeStruct((t, SB_WIDTH), BF16)] + [jax.ShapeDtypeStruct((t, SB_WIDTH), F32)] * 2,
        compiler_params=_params("arbitrary", "arbitrary"),
    )(proj, proj, proj, kept, do, *([] if tie is None else [tie]))


HG_BLOCK = 128
HG_HEADS = HG_WIDTH // HG_HEAD_DIM


def _chunk_mats(n):
    r = lax.broadcasted_iota(jnp.int32, (n, n), 0)
    c = lax.broadcasted_iota(jnp.int32, (n, n), 1)
    same = (r // HG_CHUNK) == (c // HG_CHUNK)
    upto = (same & (c <= r)).astype(BF16)
    whole = same.astype(BF16)
    onward = (same & (c >= r)).astype(BF16)
    return upto, whole, onward


def _rows_dot(mat, x):
    return _dot(jnp.concatenate([mat, mat, mat], axis=1), jnp.concatenate(_split3(x), axis=0))


def _split_heads(x):
    return jnp.stack([x[:, h * HG_HEAD_DIM:(h + 1) * HG_HEAD_DIM] for h in range(HG_HEADS)], axis=0)


def _merge_heads(x):
    return jnp.concatenate([x[h] for h in range(HG_HEADS)], axis=1)


def _lower_bound(lg_ref):
    lg = lg_ref[...]
    return _sigmoid(lg[0:1, :] - lg[1:2, :])


def _hgrn_prepare(q_ref, f_ref, lb, h, upto, whole):
    cols = slice(h * HG_HEAD_DIM, (h + 1) * HG_HEAD_DIM)
    lbh = lb[:, cols]
    sg = _sigmoid(f_ref[:, cols])
    forget = lbh + (1.0 - lbh) * sg
    logf = jnp.log(forget)
    kk = (1.0 - lbh) * (1.0 - sg)
    qv = q_ref[:, cols]
    qsig = _sigmoid(qv)
    qh = qv * qsig
    b = _rows_dot(upto, logf)
    blast = _rows_dot(whole, logf)
    return dict(lbh=lbh, sg=sg, forget=forget, kk=kk, qv=qv, qsig=qsig, qh=qh, b=b, eb=jnp.exp(b),
                ekb=jnp.exp(blast - b), dl=jnp.exp(blast))


def _hgrn_fwd(proj, logits, *, name):
    t = proj.shape[0]
    tb = HG_BLOCK
    nc = tb // HG_CHUNK
    hd = HG_HEAD_DIM

    def body(q_ref, f_ref, i_ref, lg_ref, o_ref, st_ref, state, qh_s, kk_s, b_s, qe_s, ke_s, dl_s):
        @pl.when(pl.program_id(0) == 0)
        def _():
            state[...] = jnp.zeros_like(state)

        lb = _lower_bound(lg_ref)
        upto, whole, _ = _chunk_mats(tb)
        for h in range(HG_HEADS):
            p = _hgrn_prepare(q_ref, f_ref, lb, h, upto, whole)
            qh_s[h] = p["qh"]
            kk_s[h] = p["kk"]
            b_s[h] = p["b"]
            qe_s[h] = (p["qh"] * p["eb"]).astype(BF16)
            ke_s[h] = (p["kk"] * p["ekb"]).astype(BF16)
            dl_s[h] = p["dl"]
        rowi = lax.broadcasted_iota(jnp.int32, (HG_HEADS, HG_CHUNK, hd), 1)

        def chunk(c, _):
            r0 = pl.multiple_of(c * HG_CHUNK, HG_CHUNK)
            rows = pl.ds(r0, HG_CHUNK)
            bc = b_s[:, rows, :]
            qc = qh_s[:, rows, :]
            kc = kk_s[:, rows, :]
            vc = _split_heads(i_ref[rows, :])
            s_in = state[...]
            st_ref[c] = s_in
            s_in_b = s_in.astype(BF16)
            qe = qe_s[:, rows, :]
            o = jnp.stack([_dot_nt(qe[h], s_in_b[h]) for h in range(HG_HEADS)], axis=0)
            for s in range(HG_CHUNK):
                pair = jnp.where(rowi >= s, qc * jnp.exp(bc - bc[:, s:s + 1, :]) * kc[:, s:s + 1, :], 0.0)
                o = o + jnp.sum(pair, axis=2, keepdims=True) * vc[:, s:s + 1, :]
            o_ref[rows, :] = _merge_heads(o)
            vcb = vc.astype(BF16)
            ke = ke_s[:, rows, :]
            update = jnp.stack([_dot_tn(vcb[h], ke[h]) for h in range(HG_HEADS)], axis=0)
            state[...] = s_in * dl_s[:, pl.ds(r0, 1), :] + update
            return 0

        lax.fori_loop(0, nc, chunk, 0, unroll=4)

    blk =lambda col: pl.BlockSpec((tb, HG_WIDTH), lambda i: (i, col))
    head_f32 = pltpu.VMEM((HG_HEADS, tb, hd), F32)
    head_bf16 = pltpu.VMEM((HG_HEADS, tb, hd), BF16)
    return pl.pallas_call(
        body,
        name=name,
        grid=(t // tb,),
        in_specs=[blk(3), blk(4), blk(5), pl.BlockSpec((2, HG_WIDTH), lambda i: (0, 0))],
        out_specs=[
            pl.BlockSpec((tb, HG_WIDTH), lambda i: (i, 0)),
            pl.BlockSpec((nc, HG_HEADS, hd, hd), lambda i: (i, 0, 0, 0)),
        ],
        out_shape=[
            jax.ShapeDtypeStruct((t, HG_WIDTH), F32),
            jax.ShapeDtypeStruct((t // HG_CHUNK, HG_HEADS, hd, hd), F32),
        ],
        scratch_shapes=[pltpu.VMEM((HG_HEADS, hd, hd), F32), head_f32, head_f32, head_f32, head_bf16, head_bf16,
                        head_f32],
        compiler_params=_params("arbitrary"),
    )(proj, proj, proj, logits)


def _hgrn_bwd(proj, logits, states, do, *, name):
    t = proj.shape[0]
    tb = HG_BLOCK
    nb = t // tb
    nc = tb // HG_CHUNK
    hd = HG_HEAD_DIM

    def body(q_ref, f_ref, i_ref, lg_ref, st_ref, do_ref, dq_ref, df_ref, di_ref, dlb_ref,
             dstate, qh_s, kk_s, b_s, eb_s, ekb_s, qe_s, ke_s, dl_s, dqh_s, dkk_s, dlf_s):
        step = pl.program_id(0)

        @pl.when(step == 0)
        def _():
            dstate[...] = jnp.zeros_like(dstate)
            dlb_ref[...] = jnp.zeros_like(dlb_ref)

        lb = _lower_bound(lg_ref)
        upto, whole, _ = _chunk_mats(tb)
        prepared = []
        for h in range(HG_HEADS):
            p = _hgrn_prepare(q_ref, f_ref, lb, h, upto, whole)
            prepared.append(p)
            qh_s[h] = p["qh"]
            kk_s[h] = p["kk"]
            b_s[h] = p["b"]
            eb_s[h] = p["eb"]
            ekb_s[h] = p["ekb"]
            qe_s[h] = (p["qh"] * p["eb"]).astype(BF16)
            ke_s[h] = (p["kk"] * p["ekb"]).astype(BF16)
            dl_s[h] = p["dl"]
        rowi = lax.broadcasted_iota(jnp.int32, (HG_CHUNK, hd), 0)
        r16 = lax.broadcasted_iota(jnp.int32, (HG_CHUNK, HG_CHUNK), 0)
        c16 = lax.broadcasted_iota(jnp.int32, (HG_CHUNK, HG_CHUNK), 1)
        onward = (c16 >= r16).astype(BF16)

        def chunk(it, _):
            c = nc - 1 - it
            r0 = pl.multiple_of(c * HG_CHUNK, HG_CHUNK)
            rows = pl.ds(r0, HG_CHUNK)
            for h in range(HG_HEADS):
                cols = slice(h * hd, (h + 1) * hd)
                bc = b_s[h, rows, :]
                qc = qh_s[h, rows, :]
                kc = kk_s[h, rows, :]
                vc = i_ref[rows, cols]
                doc = do_ref[rows, cols]
                s_in = st_ref[c, h]
                ds_out = dstate[h]
                ds_out_b = ds_out.astype(BF16)
                docb = doc.astype(BF16)
                dl_row = dl_s[h, pl.ds(r0, 1), :]
                dqh = _dot(docb, s_in.astype(BF16)) * eb_s[h, rows, :]
                dkk = _dot(vc.astype(BF16), ds_out_b) * ekb_s[h, rows, :]
                dv = _dot_nt(ke_s[h, rows, :], ds_out_b)
                db = dqh * qc - dkk * kc
                dwhole = jnp.sum(dkk * kc, axis=0, keepdims=True) + jnp.sum(ds_out * s_in, axis=0, keepdims=True) * dl_row
                dk_rows, dv_rows = [], []
                for s in range(HG_CHUNK):
                    keep = rowi >= s
                    e = jnp.exp(bc - bc[s:s + 1, :])
                    k_row = kc[s:s + 1, :]
                    pcol = jnp.sum(jnp.where(keep, qc * e * k_row, 0.0), axis=1, keepdims=True)
                    dpcol = jnp.sum(doc * vc[s:s + 1, :], axis=1, keepdims=True)
                    m = jnp.where(keep, e * dpcol, 0.0)
                    y = m * qc
                    dqh = dqh + m * k_row
                    db = db + y * k_row
                    dk_rows.append(jnp.sum(y, axis=0, keepdims=True))
                    dv_rows.append(jnp.sum(pcol * doc, axis=0, keepdims=True))
                dkk_pairs = jnp.concatenate(dk_rows, axis=0)
                dkk = dkk + dkk_pairs
                db = db - dkk_pairs * kc
                dv = dv + jnp.concatenate(dv_rows, axis=0)
                dqh_s[h, rows, :] = dqh
                dkk_s[h, rows, :] = dkk
                dlf_s[h, rows, :] = _rows_dot(onward, db) + dwhole
                di_ref[rows, cols] = dv.astype(BF16)
                dstate[h] = ds_out * dl_row + _dot_tn(docb, qe_s[h, rows, :])
            return 0

        lax.fori_loop(0, nc, chunk, 0, unroll=4)
        for h in range(HG_HEADS):
            cols = slice(h * hd, (h + 1) * hd)
            p = prepared[h]
            dq_ref[:, cols] = (dqh_s[h] * (p["qsig"] * (1.0 + p["qv"] * (1.0 - p["qsig"])))).astype(BF16)
            dforget = dlf_s[h] / p["forget"] - dkk_s[h]
            df_ref[:, cols] = (dforget * (1.0 - p["lbh"]) * p["sg"] * (1.0 - p["sg"])).astype(BF16)
            dlb_ref[:, cols] += jnp.sum(dforget * (1.0 - p["sg"]), axis=0, keepdims=True)

    blk = lambda col: pl.BlockSpec((tb, HG_WIDTH), lambda i: (nb - 1 - i, col))
    vec = pl.BlockSpec((1, HG_WIDTH), lambda i: (0, 0))
    head_f32 = pltpu.VMEM((HG_HEADS, tb, hd), F32)
    head_bf16 = pltpu.VMEM((HG_HEADS, tb, hd), BF16)
    return pl.pallas_call(
        body,
        name=name,
        grid=(nb,),
        in_specs=[
            blk(3), blk(4), blk(5),
            pl.BlockSpec((2, HG_WIDTH), lambda i: (0, 0)),
            pl.BlockSpec((nc, HG_HEADS, hd, hd), lambda i: (nb - 1 - i, 0, 0, 0)),
            blk(0),
        ],
        out_specs=[blk(0), blk(0), blk(0), vec],
        out_shape=[jax.ShapeDtypeStruct((t, HG_WIDTH), BF16)] * 3 + [jax.ShapeDtypeStruct((1, HG_WIDTH), F32)],
        scratch_shapes=[
            pltpu.VMEM((HG_HEADS, hd, hd), F32),
            head_f32, head_f32, head_f32, head_f32, head_f32, head_bf16, head_bf16, head_f32,
            head_f32, head_f32, head_f32,
        ],
        compiler_params=_params("arbitrary"),
    )(proj, proj, proj, logits, states, do)


def _group_mat(width, head_dim):
    r = lax.broadcasted_iota(jnp.int32, (width, width), 0)
    c = lax.broadcasted_iota(jnp.int32, (width, width), 1)
    return ((r // head_dim) == (c // head_dim)).astype(BF16)


def _head_mean(x, mat, head_dim):
    hi = x.astype(BF16)
    lo = (x - hi.astype(F32)).astype(BF16)
    return (_dot(hi, mat) + _dot(lo, mat)) * (1.0 / head_dim)


def _mix_out_fwd(o_sb, o_hg, proj, g_sb, g_hg, w_out, x1, *, name, tm=512):
    t = x1.shape[0]

    def body(osb_ref, ohg_ref, gate_ref, gsb_ref, ghg_ref, w_ref, x_ref, xo_ref, mt_ref):
        msb = _group_mat(SB_WIDTH, SB_HEAD_DIM)
        mhg = _group_mat(HG_WIDTH, HG_HEAD_DIM)
        osb = osb_ref[...]
        ohg = ohg_ref[...]
        nsb = osb * lax.rsqrt(_head_mean(osb * osb, msb, SB_HEAD_DIM) + EPS) * gsb_ref[...]
        gate = gate_ref[...]
        nhg = ohg * lax.rsqrt(_head_mean(ohg * ohg, mhg, HG_HEAD_DIM) + EPS) * ghg_ref[...] * (gate * _sigmoid(gate))
        mixed = jnp.concatenate([nsb, nhg], axis=1).astype(BF16)
        mt_ref[...] = mixed
        xo_ref[...] = x_ref[...] + _dot(mixed, w_ref[...])

    half = pl.BlockSpec((tm, SB_WIDTH), lambda i: (i, 0))
    vec = pl.BlockSpec((1, SB_WIDTH), lambda i: (0, 0))
    row = pl.BlockSpec((tm, D_MODEL), lambda i: (i, 0))
    return pl.pallas_call(
        body,
        name=name,
        grid=(t // tm,),
        in_specs=[half, half, pl.BlockSpec((tm, HG_WIDTH), lambda i: (i, 6)), vec, vec,
                  pl.BlockSpec((D_MODEL, D_MODEL), lambda i: (0, 0)), row],
        out_specs=[row, row],
        out_shape=[jax.ShapeDtypeStruct((t, D_MODEL), F32), jax.ShapeDtypeStruct((t, D_MODEL), BF16)],
        compiler_params=_params("parallel"),
    )(o_sb, o_hg, proj, g_sb, g_hg, w_out, x1)


def _mix_out_bwd(dx2, o_sb, o_hg, proj, g_sb, g_hg, w_out, *, name, tm=512):
    t = dx2.shape[0]

    def body(dx_ref, osb_ref, ohg_ref, gate_ref, gsb_ref, ghg_ref, w_ref, dosb_ref, dohg_ref, dgate_ref, dgsb_ref,
             dghg_ref, dxb_ref):
        i = pl.program_id(0)
        msb = _group_mat(SB_WIDTH, SB_HEAD_DIM)
        mhg = _group_mat(HG_WIDTH, HG_HEAD_DIM)
        dxb = dx_ref[...].astype(BF16)
        dxb_ref[...] = dxb
        dmixed = _dot_nt(dxb, w_ref[...])
        dnsb = dmixed[:, :SB_WIDTH]
        dy = dmixed[:, SB_WIDTH:]

        osb = osb_ref[...]
        rstd = lax.rsqrt(_head_mean(osb * osb, msb, SB_HEAD_DIM) + EPS)
        ohat = osb * rstd
        part_sb = jnp.sum(dnsb * ohat, axis=0, keepdims=True)
        dohat = dnsb * gsb_ref[...]
        dosb_ref[...] = rstd * (dohat - ohat * _head_mean(dohat * ohat, msb, SB_HEAD_DIM))

        ohg = ohg_ref[...]
        rstd = lax.rsqrt(_head_mean(ohg * ohg, mhg, HG_HEAD_DIM) + EPS)
        ohat = ohg * rstd
        gate = gate_ref[...]
        sig = _sigmoid(gate)
        dn = dy * (gate * sig)
        dgate_ref[...] = (dy * (ohat * ghg_ref[...]) * (sig * (1.0 + gate * (1.0 - sig)))).astype(BF16)
        part_hg = jnp.sum(dn * ohat, axis=0, keepdims=True)
        dohat = dn * ghg_ref[...]
        dohg_ref[...] = rstd * (dohat - ohat * _head_mean(dohat * ohat, mhg, HG_HEAD_DIM))

        @pl.when(i == 0)
        def _():
            dgsb_ref[...] = part_sb
            dghg_ref[...] = part_hg

        @pl.when(i > 0)
        def _():
            dgsb_ref[...] += part_sb
            dghg_ref[...] += part_hg

    half = pl.BlockSpec((tm, SB_WIDTH), lambda i: (i, 0))
    vec = pl.BlockSpec((1, SB_WIDTH), lambda i: (0, 0))
    row = pl.BlockSpec((tm, D_MODEL), lambda i: (i, 0))
    return pl.pallas_call(
        body,
        name=name,
        grid=(t // tm,),
        in_specs=[row, half, half, pl.BlockSpec((tm, HG_WIDTH), lambda i: (i, 6)), vec, vec,
                  pl.BlockSpec((D_MODEL, D_MODEL), lambda i: (0, 0))],
        out_specs=[half, half, half, vec, vec, row],
        out_shape=[jax.ShapeDtypeStruct((t, SB_WIDTH), F32)] * 2 + [jax.ShapeDtypeStruct((t, SB_WIDTH), BF16)]
        + [jax.ShapeDtypeStruct((1, SB_WIDTH), F32)] * 2 + [jax.ShapeDtypeStruct((t, D_MODEL), BF16)],
        compiler_params=_params("arbitrary"),
    )(dx2, o_sb, o_hg, proj, g_sb, g_hg, w_out)


def _local_step(x, target, norms, logits, w, weights_after=None, grads_ready=None):
    w = dict(w)
    a1, b1, s1, h1 = _ffn_fwd_act(x, norms["ffn1"], w["g1t"], w["u1t"], name="ffn1_fwd_act")
    x1, hm = _ffn_fwd_down(x, s1, w["d1"], norms["mix"], name="ffn1_fwd_down")
    if weights_after is not None:
        w.update(weights_after("ffn1", x1))
    proj = _mm(hm, w["int"], name="in_proj", tm=512, tn=IN_COLS, nt=True)
    o_sb, sb_kept = _attn_fwd(proj, name="sb_attn_fwd")
    o_hg, states = _hgrn_fwd(proj, logits, name="hgrn2_fwd")
    x2, mixed = _mix_out_fwd(o_sb, o_hg, proj, norms["sb"], norms["hg"], w["out"], x1, name="mix_out_fwd")
    if weights_after is not None:
        w.update(weights_after("mix", x2))
    dx3, a2, b2, h2, s2, d_final, loss_row = _ffn_fwd(x2, norms["ffn2"], w["g2t"], w["u2t"], w["d2"], name="ffn2_fwd",
                                                      head=(norms["final"], target))

    def weight_grad(lhs, rhs, name, tie=None):
        return _mm(lhs, rhs, name=name, tm=256, tn=D_MODEL, ta=True, out_dtype=BF16, tie=tie)

    def sent(stage):
        return grads_ready(stage, gw) if grads_ready is not None else None

    gw, gv = {}, {"final": d_final}
    dx2, gv["ffn2"], da2, db2, dob2 = _ffn_bwd(dx3, x2, norms["ffn2"], a2, b2, w["g2t"], w["u2t"], w["d2"],
                                               name="ffn2_bwd")
    gw["g2t"] = weight_grad(da2, h2, "ffn2_dgate")
    gw["u2t"] = weight_grad(db2, h2, "ffn2_dup")
    gw["d2"] = weight_grad(s2, dob2, "ffn2_ddown")

    do_sb, do_hg, d_gate, gv["sb"], gv["hg"], dx2b = _mix_out_bwd(
        dx2, o_sb, o_hg, proj, norms["sb"], norms["hg"], w["out"], name="mix_out_bwd")
    gw["out"] = weight_grad(mixed, dx2b, "out_dw")
    tie = sent("mix")
    dq_sb, dk_sb, dv_sb = _attn_bwd(proj, sb_kept, do_sb, name="sb_attn_bwd", tie=tie)
    dq_hg, df_hg, di_hg, d_lb = _hgrn_bwd(proj, logits if tie is None else logits + tie[0, 0], states, do_hg,
                                          name="hgrn2_bwd")
    dproj = jnp.concatenate([dq_sb, dk_sb.astype(BF16), dv_sb.astype(BF16), dq_hg, df_hg, di_hg, d_gate], axis=1)
    gw["int"] = weight_grad(dproj, hm, "in_dw")
    tie = sent("in")
    dx1, gv["mix"], dob1 = _in_proj_bwd(dproj, w["int"], x1, norms["mix"] if tie is None else norms["mix"] + tie[0, 0],
                                        dx2, name="in_dx")

    gw["d1"] = weight_grad(s1, dob1, "ffn1_ddown")
    tie = sent("d1")
    da1, db1 = _ffn_bwd_act(dob1, a1, b1, w["d1"], name="ffn1_bwd_act",
                            tie=jnp.zeros((8, LANES), F32) if tie is None else tie)
    gw["g1t"] = weight_grad(da1, h1, "ffn1_dgate")
    gw["u1t"] = weight_grad(db1, h1, "ffn1_dup", tie=sent("g1t"))
    tie = sent("u1t")
    dx, gv["ffn1"] = _ffn_bwd_in(dx1, x, norms["ffn1"] if tie is None else norms["ffn1"] + tie[0, 0], da1, db1,
                                 w["g1t"], w["u1t"], name="ffn1_bwd_in")
    gv["lb"] = d_lb
    return loss_row, dx, gw, gv


HBM = pl.BlockSpec(memory_space=pl.ANY)


def _place():
    return lax.axis_index("x"), lax.axis_index("y"), lax.axis_index("c")


def _slot(px, py, pc):
    return 4 * px + 2 * py + pc


GATHER_COPIES = 8


def _all_gather(blocks, *, name):
    n = len(blocks)

    def body(*refs):
        ins, outs = refs[:n], refs[n:2 * n]
        send_sems, recv_sems, local_sems = refs[2 * n:]
        x, y, c = _place()
        me, sibling = (x, y, c), (x, y, 1 - c)
        beside, across, diagonal = (1 - x, y, c), (x, 1 - y, c), (1 - x, 1 - y, c)

        def copy(a, k, block, to, src=None, half=None):
            dst = outs[a].at[_slot(*block)]
            if half is not None:
                rows = blocks[a].shape[0] // 2
                dst = dst.at[pl.ds(half * rows, rows)]
            return pltpu.make_async_remote_copy(
                src_ref=dst if src is None else src, dst_ref=dst, send_sem=send_sems.at[GATHER_COPIES * a + k],
                recv_sem=recv_sems.at[GATHER_COPIES * a + k], device_id=to, device_id_type=MESH)

        mine = [pltpu.make_async_copy(ins[a], outs[a].at[_slot(*me)], local_sems.at[a]) for a in range(n)]
        for cp in mine:
            cp.start()
        sent = []
        for a in range(n):
            sent += [copy(a, 0, me, sibling, src=ins[a]), copy(a, 1, me, beside, src=ins[a]),
                     copy(a, 2, me, across, src=ins[a])]
        for cp in sent:
            cp.start()
        for a in range(n):
            copy(a, 1, beside, me).wait_recv()
            sent += [copy(a, 3, beside, across, half=0), copy(a, 5, beside, sibling)]
            sent[-2].start()
            sent[-1].start()
        for a in range(n):
            copy(a, 2, across, me).wait_recv()
            sent += [copy(a, 4, across, beside, half=1), copy(a, 6, across, sibling)]
            sent[-2].start()
            sent[-1].start()
        for a in range(n):
            copy(a, 3, diagonal, me, half=0).wait_recv()
            copy(a, 4, diagonal, me, half=1).wait_recv()
            sent.append(copy(a, 7, diagonal, sibling))
            sent[-1].start()
        for a in range(n):
            for k, origin in ((0, sibling), (5, (1 - x, y, 1 - c)), (6, (x, 1 - y, 1 - c)), (7, (1 - x, 1 - y, 1 - c))):
                copy(a, k, origin, me).wait_recv()
        for cp in sent:
            cp.wait_send()
        for cp in mine:
            cp.wait()

    return pl.pallas_call(
        body,
        name=name,
        in_specs=[HBM] * n,
        out_specs=[HBM] * n,
        out_shape=[jax.ShapeDtypeStruct((N_DEV,) + b.shape, b.dtype) for b in blocks],
        scratch_shapes=[pltpu.SemaphoreType.DMA((GATHER_COPIES * n,)), pltpu.SemaphoreType.DMA((GATHER_COPIES * n,)),
                        pltpu.SemaphoreType.DMA((n,))],
    )(*blocks)


def _flipped(place, d):
    return tuple(1 - p if (d >> (2 - axis)) & 1 else p for axis, p in enumerate(place))


SEM = pl.BlockSpec(memory_space=pltpu.SEMAPHORE)
EFFECT = pltpu.SideEffectType.DATAFLOW_SIDE_EFFECTING


def _split_copies(me, srcs, lands, send_sems, recv_sems, by_owner):
    copies = []
    for d in range(1, N_DEV):
        peer = _flipped(me, d)
        for a, (src, land) in enumerate(zip(srcs, lands)):
            copies.append(pltpu.make_async_remote_copy(
                src_ref=src.at[_slot(*peer)] if by_owner else src, dst_ref=land.at[_slot(*me)],
                send_sem=send_sems.at[7 * a + d - 1], recv_sem=recv_sems.at[7 * a + d - 1], device_id=peer,
                device_id_type=MESH))
    own = [pltpu.make_async_copy(src.at[_slot(*me)] if by_owner else src, land.at[_slot(*me)],
                                 recv_sems.at[7 * len(srcs) + a]) for a, (src, land) in enumerate(zip(srcs, lands))]
    return copies, own


def _copies_start(srcs, *, name, by_owner, after=None):
    n = len(srcs)
    extra = [] if after is None else [after]
    land_shapes = [s.shape if by_owner else (N_DEV,) + s.shape for s in srcs]
    lands = [pltpu.with_memory_space_constraint(lax.empty(shape, s.dtype), pltpu.HBM) for shape, s in zip(land_shapes, srcs)]
    srcs = [pltpu.with_memory_space_constraint(s, pltpu.HBM) for s in srcs]

    def body(*refs):
        src_refs, land_refs = refs[:n], refs[n:2 * n]
        send_sems, recv_sems = refs[2 * n + len(extra)], refs[2 * n + len(extra) + 1]
        token = refs[-1]
        copies, own = _split_copies(_place(), src_refs, land_refs, send_sems, recv_sems, by_owner)
        for cp in copies + own:
            cp.start()
        token[...] = jnp.zeros_like(token)

    out = pl.pallas_call(
        body,
        name=name,
        in_specs=[HBM] * (2 * n + len(extra)),
        out_specs=[SEM, SEM] + [HBM] * (2 * n) + [pl.BlockSpec(memory_space=pltpu.VMEM)],
        out_shape=[pltpu.SemaphoreType.DMA((7 * n,)), pltpu.SemaphoreType.DMA((8 * n,))]
        + [pltpu.HBM(s.shape, s.dtype) for s in srcs] + [pltpu.HBM(shape, s.dtype) for shape, s in zip(land_shapes, srcs)]
        + [jax.ShapeDtypeStruct((8, LANES), F32)],
        input_output_aliases={i: 2 + i for i in range(2 * n)},
        compiler_params=pltpu.CompilerParams(has_side_effects=EFFECT),
    )(*srcs, *lands, *extra)
    return (out[0], out[1], out[2:2 + n], out[2 + n:2 + 2 * n]), out[-1]


def _copies_wait(started, after, *, name, by_owner):
    send_sems, recv_sems, srcs, lands = started
    n = len(srcs)

    def body(*refs):
        src_refs, land_refs = refs[:n], refs[n:2 * n]
        copies, own = _split_copies(_place(), src_refs, land_refs, refs[2 * n], refs[2 * n + 1], by_owner)
        for cp in copies:
            cp.wait_send()
            cp.wait_recv()
        for cp in own:
            cp.wait()

    out = pl.pallas_call(
        body,
        name=name,
        in_specs=[HBM] * (2 * n) + [SEM, SEM, HBM],
        out_specs=[HBM] * (2 * n),
        out_shape=[pltpu.HBM(s.shape, s.dtype) for s in srcs] + [pltpu.HBM(s.shape, s.dtype) for s in lands],
        input_output_aliases={i: i for i in range(2 * n)},
        compiler_params=pltpu.CompilerParams(has_side_effects=EFFECT),
    )(*srcs, *lands, send_sems, recv_sems, after)
    return out[:n], out[n:]


def _adamw(w, g, m, v):
    m = ADAM_B1 * m + (1.0 - ADAM_B1) * g
    v = ADAM_B2 * v + (1.0 - ADAM_B2) * (g * g)
    m_hat = m / (1.0 - ADAM_B1 ** ADAM_STEP)
    v_hat = v / (1.0 - ADAM_B2 ** ADAM_STEP)
    delta = -ADAM_LR * (m_hat / (jnp.sqrt(v_hat) + ADAM_EPS) + ADAM_WD * w)
    return delta, m, v


def _sum_and_update(parts, w, m, v, *, name, tie=None):
    _, rows, cols = w.shape
    tr = rows // 2

    def body(p_ref, w_ref, m_ref, v_ref, *rest):
        g_ref, d_ref, mo_ref, vo_ref = rest[-4:]
        g = p_ref[0].astype(F32)
        for s in range(1, N_DEV):
            g = g + p_ref[s].astype(F32)
        g_ref[0] = g
        d_ref[0], mo_ref[0], vo_ref[0] = _adamw(w_ref[0], g, m_ref[0], v_ref[0])

    flat = pl.BlockSpec((1, tr, cols), lambda i: (0, i, 0))
    return pl.pallas_call(
        body,
        name=name,
        grid=(rows // tr,),
        in_specs=[pl.BlockSpec((N_DEV, tr, cols), lambda i: (0, i, 0)), flat, flat, flat]
        + ([] if tie is None else [pl.BlockSpec(memory_space=pl.ANY)]),
        out_specs=[flat] * 4,
        out_shape=[jax.ShapeDtypeStruct((1, rows, cols), F32)] * 4,
        compiler_params=_params("parallel"),
    )(parts, w, m, v, *([] if tie is None else [tie]))


VEC_ROWS = 8
ROW_LOGITS, ROW_LOSS = 5, 7


def _vectors_update(part, w, m, v, *, name, tie):
    def body(p_ref, w_ref, m_ref, v_ref, tie_ref, g_ref, d_ref, mo_ref, vo_ref, loss_ref, all_ref, send_sems, recv_sems):
        me = _place()
        all_ref[_slot(*me)] = p_ref[...]
        copies = []
        for d in range(1, N_DEV):
            peer = _flipped(me, d)
            copies.append(pltpu.make_async_remote_copy(
                src_ref=p_ref, dst_ref=all_ref.at[_slot(*me)], send_sem=send_sems.at[d - 1], recv_sem=recv_sems.at[d - 1],
                device_id=peer, device_id_type=MESH))
        for cp in copies:
            cp.start()
        for cp in copies:
            cp.wait()
        total = all_ref[0]
        for s in range(1, N_DEV):
            total = total + all_ref[s]
        wv = w_ref[...]
        half = D_MODEL // 2
        lb = _sigmoid(wv[ROW_LOGITS:ROW_LOGITS + 1, :half] - wv[ROW_LOGITS:ROW_LOGITS + 1, half:])
        d_first = total[ROW_LOGITS:ROW_LOGITS + 1, :half] * lb * (1.0 - lb)
        d_logits = jnp.concatenate([d_first, -d_first], axis=1)
        rowi = lax.broadcasted_iota(jnp.int32, (VEC_ROWS, D_MODEL), 0)
        g = jnp.where(rowi == ROW_LOGITS, d_logits, jnp.where(rowi < ROW_LOGITS, total, 0.0))
        g_ref[...] = g
        d_ref[...], mo_ref[...], vo_ref[...] = _adamw(wv, g, m_ref[...], v_ref[...])
        loss_ref[...] = total[ROW_LOSS:ROW_LOSS + 1, :]

    vmem = pl.BlockSpec(memory_space=pltpu.VMEM)
    return pl.pallas_call(
        body,
        name=name,
        in_specs=[vmem] * 4 + [HBM],
        out_specs=[vmem] * 5,
        out_shape=[jax.ShapeDtypeStruct((VEC_ROWS, D_MODEL), F32)] * 4 + [jax.ShapeDtypeStruct((1, D_MODEL), F32)],
        scratch_shapes=[pltpu.VMEM((N_DEV, VEC_ROWS, D_MODEL), F32), pltpu.SemaphoreType.DMA((7,)),
                        pltpu.SemaphoreType.DMA((7,))],
    )(part, w, m, v, tie)


TRANSPOSED = ("g1t", "u1t", "g2t", "u2t", "int")


def _vector_rows(rows):
    rowi = lax.broadcasted_iota(jnp.int32, (VEC_ROWS, D_MODEL), 0)
    out = jnp.zeros((VEC_ROWS, D_MODEL), F32)
    for i, r in enumerate(rows):
        if r is not None:
            out = jnp.where(rowi == i, r, out)
    return out


def kernel(x, ffn1_norm, ffn1_w_gate, ffn1_w_up, ffn1_w_down, mix_norm, w_in, sb_out_norm, hg_lower_bound_logits, hg_out_norm, w_out, ffn2_norm, ffn2_w_gate, ffn2_w_up, ffn2_w_down, final_norm, loss_target, m_ffn1_norm, m_ffn1_w_gate, m_ffn1_w_up, m_ffn1_w_down, m_mix_norm, m_w_in, m_sb_out_norm, m_hg_lower_bound_logits, m_hg_out_norm, m_w_out, m_ffn2_norm, m_ffn2_w_gate, m_ffn2_w_up, m_ffn2_w_down, m_final_norm, v_ffn1_norm, v_ffn1_w_gate, v_ffn1_w_up, v_ffn1_w_down, v_mix_norm, v_w_in, v_sb_out_norm, v_hg_lower_bound_logits, v_hg_out_norm, v_w_out, v_ffn2_norm, v_ffn2_w_gate, v_ffn2_w_up, v_ffn2_w_down, v_final_norm):
    def matrices(g1, u1, d1, win, wout, g2, u2, d2):
        return {"g1t": g1, "u1t": u1, "d1": d1, "int": win, "out": wout, "g2t": g2, "u2t": u2, "d2": d2}

    def vectors(n1, nm, nsb, lg, nhg, n2, nf):
        return [n1, nm, n2, nf.reshape(1, D_MODEL), jnp.concatenate([nsb, nhg], axis=1), lg.reshape(1, D_MODEL), None, None]

    w_sh = matrices(ffn1_w_gate, ffn1_w_up, ffn1_w_down, w_in, w_out, ffn2_w_gate, ffn2_w_up, ffn2_w_down)
    m_sh = matrices(m_ffn1_w_gate, m_ffn1_w_up, m_ffn1_w_down, m_w_in, m_w_out, m_ffn2_w_gate, m_ffn2_w_up, m_ffn2_w_down)
    v_sh = matrices(v_ffn1_w_gate, v_ffn1_w_up, v_ffn1_w_down, v_w_in, v_w_out, v_ffn2_w_gate, v_ffn2_w_up, v_ffn2_w_down)
    keys = list(w_sh)

    def full(key, stack):
        return stack.reshape(-1, D_MODEL)

    def by_owner(key, grad):
        return grad.reshape(N_DEV, -1, D_MODEL)

    def view(key, a):
        return jnp.swapaxes(a, 1, 2) if key in TRANSPOSED else a

    blocks = {k: view(k, w_sh[k])[0].astype(BF16) for k in keys}
    first, mid, last = ("g1t", "u1t", "d1"), ("int", "out"), ("g2t", "u2t", "d2")
    w_first = {k: full(k, s) for k, s in zip(first, _all_gather([blocks[k] for k in first], name="gather_ffn1"))}
    flights = {}
    flights["ffn1"], token_mid = _copies_start([blocks[k] for k in mid], name="gather_mid_start", by_owner=False,
                                               after=w_first["d1"])
    flights["mix"], token_last = _copies_start([blocks[k] for k in last], name="gather_ffn2_start", by_owner=False,
                                               after=token_mid)

    def weights_after(stage, result):
        group = mid if stage == "ffn1" else last
        _, lands = _copies_wait(flights[stage], result, name="gather_" + stage + "_wait", by_owner=False)
        return {k: full(k, s) for k, s in zip(group, lands)}

    groups = {"mix": ("g2t", "u2t", "d2", "out"), "in": ("int",), "g1t": ("g1t",), "u1t": ("u1t",), "d1": ("d1",)}
    sent = {}

    def grads_ready(stage, gw):
        stacks = [by_owner(k, gw[k]) for k in groups[stage]]
        sent[stage], token = _copies_start(stacks, name="grads_" + stage + "_start", by_owner=True)
        return token

    norms = {"ffn1": ffn1_norm + token_last[0, 0], "mix": mix_norm, "sb": sb_out_norm, "hg": hg_out_norm,
             "ffn2": ffn2_norm, "final": final_norm.reshape(1, D_MODEL)}
    loss_row, grad_x, gw, gv = _local_step(x[0], loss_target[0], norms, hg_lower_bound_logits, w_first, weights_after,
                                           grads_ready)

    lb_row = jnp.concatenate([gv["lb"], jnp.zeros_like(gv["lb"])], axis=1)
    part = _vector_rows([gv["ffn1"], gv["mix"], gv["ffn2"], gv["final"], jnp.concatenate([gv["sb"], gv["hg"]], axis=1),
                         lb_row, None, loss_row])
    vec_w = _vector_rows(vectors(ffn1_norm, mix_norm, sb_out_norm, hg_lower_bound_logits, hg_out_norm, ffn2_norm, final_norm))
    vec_m = _vector_rows(vectors(m_ffn1_norm, m_mix_norm, m_sb_out_norm, m_hg_lower_bound_logits, m_hg_out_norm,
                                 m_ffn2_norm, m_final_norm))
    vec_v = _vector_rows(vectors(v_ffn1_norm, v_mix_norm, v_sb_out_norm, v_hg_lower_bound_logits, v_hg_out_norm,
                                 v_ffn2_norm, v_final_norm))
    updated, after = {}, grad_x
    for stage, flight in sent.items():
        if stage == list(sent)[-1]:
            *vecs, loss_out = _vectors_update(part, vec_w, vec_m, vec_v, name="vectors_update", tie=after)
            after = loss_out
        _, lands = _copies_wait(flight, after, name="grads_" + stage + "_wait", by_owner=True)
        for k, part_k in zip(groups[stage], lands):
            updated[k] = _sum_and_update(part_k, view(k, w_sh[k]), view(k, m_sh[k]), view(k, v_sh[k]), name="adamw_" + k,
                                         tie=after)
            after = updated[k][0]
    mats = [{k: view(k, updated[k][i]) for k in keys} for i in range(4)]

    def leaves(mat, vec):
        half = D_MODEL // 2
        return (
            vec[0:1], mat["g1t"], mat["u1t"], mat["d1"], vec[1:2], mat["int"], vec[4:5, :half],
            vec[ROW_LOGITS].reshape(2, half), vec[4:5, half:], mat["out"], vec[2:3], mat["g2t"], mat["u2t"],
            mat["d2"], vec[3],
        )

    out = [loss_out[0, 0], grad_x[None]]
    for mat, vec in zip(mats, vecs):
        out.extend(leaves(mat, vec))
    return tuple(out)
```
